```python
import jax, jax.numpy as jnp
from jax import lax
import numpy as np

D_MODEL = 1024
BATCH = 8
SEQ = 4096
DEPTH = 2

PLE_DIM = 256
D_FF = 2816
LN_EPS = 1e-5
RMS_EPS = 1e-6
DEEPNORM_ALPHA = (2 * DEPTH) ** 0.25
DEEPNORM_BETA = (8 * DEPTH) ** -0.25

GDN_HEADS = 8
GDN_DK = 64
GDN_DV = 64
GDN_CONV = 4
GDN_CHUNK = 64

SB_HEADS = 4
SB_DIM = 64

MLA_HEADS = 4
MLA_NOPE = 64
MLA_ROPE = 32
MLA_V = 64
MLA_Q_RANK = 256
MLA_KV_RANK = 128
ROPE_BASE = 10000.0

Q_BLOCK = 128

MIX_WIDTH = GDN_HEADS * GDN_DV + SB_HEADS * SB_DIM + MLA_HEADS * MLA_V
IN_WIDTHS = (
    GDN_HEADS * GDN_DK, GDN_HEADS * GDN_DK, GDN_HEADS * GDN_DV,
    GDN_HEADS * GDN_DV,
    GDN_HEADS, GDN_HEADS,
    SB_HEADS * SB_DIM, SB_HEADS * SB_DIM, SB_HEADS * SB_DIM,
    MLA_Q_RANK,
    MLA_KV_RANK + MLA_ROPE,
)
IN_TOTAL = int(sum(IN_WIDTHS))
IN_SPLITS = tuple(int(s) for s in np.cumsum(IN_WIDTHS)[:-1])
GDN_CONV_CH = 2 * GDN_HEADS * GDN_DK + GDN_HEADS * GDN_DV

kernel_name = "hybrid_gdn_stickbreak_mla_macaron_deepnorm"


def layer_norm(x, g, b):
    xf = x.astype(jnp.float32)
    mu = jnp.mean(xf, axis=-1, keepdims=True)
    var = jnp.mean(jnp.square(xf - mu), axis=-1, keepdims=True)
    y = (xf - mu) * lax.rsqrt(var + LN_EPS)
    return (y * g.astype(jnp.float32) + b.astype(jnp.float32)).astype(x.dtype)


def rms_norm(x, w):
    xf = x.astype(jnp.float32)
    y = xf * lax.rsqrt(jnp.mean(jnp.square(xf), axis=-1, keepdims=True) + RMS_EPS)
    return (y * w.astype(jnp.float32)).astype(x.dtype)


def l2_normalize(x):
    xf = x.astype(jnp.float32)
    return xf * lax.rsqrt(jnp.sum(jnp.square(xf), axis=-1, keepdims=True) + RMS_EPS)


def swiglu(h, w_in, w_out):
    gate, up = jnp.split(h @ w_in, 2, axis=-1)
    return (jax.nn.silu(gate) * up) @ w_out


def causal_depthwise_conv(x, w):
    k_width, ch = w.shape
    return lax.conv_general_dilated(
        x, w[:, None, :].astype(x.dtype), window_strides=(1,), padding=[(k_width - 1, 0)],
        dimension_numbers=("NWC", "WIO", "NWC"), feature_group_count=ch)


def rope_tables(positions):
    inv = 1.0 / (ROPE_BASE ** (jnp.arange(0, MLA_ROPE, 2, dtype=jnp.float32) / MLA_ROPE))
    ang = positions.astype(jnp.float32)[..., None] * inv
    return jnp.cos(ang), jnp.sin(ang)


def apply_rope(x, cos, sin):
    x1, x2 = jnp.split(x.astype(jnp.float32), 2, axis=-1)
    return jnp.concatenate([x1 * cos - x2 * sin, x2 * cos + x1 * sin], axis=-1).astype(x.dtype)


def chunk_gated_delta_rule(q, k, v, g, beta):
    B, S, H, DK = q.shape
    DV = v.shape[-1]
    C = GDN_CHUNK
    N = S // C
    f32 = jnp.float32

    def chunks(t):
        t = t.astype(f32).reshape((B, N, C) + t.shape[2:])
        return jnp.moveaxis(jnp.moveaxis(t, 3, 2), 1, 0)

    qc, kc, vc, gc, bc = (chunks(t) for t in (q, k, v, g, beta))
    gc = jnp.cumsum(gc, axis=-1)
    incl = jnp.tril(jnp.ones((C, C), dtype=bool))
    strict = jnp.tril(jnp.ones((C, C), dtype=bool), -1)
    diff = gc[..., :, None] - gc[..., None, :]
    decay = jnp.where(incl, jnp.exp(jnp.where(incl, diff, 0.0)), 0.0)
    kb = kc * bc[..., None]
    lhs = jnp.where(strict, jnp.einsum('nbhid,nbhjd->nbhij', kb, kc) * decay, 0.0) + jnp.eye(C, dtype=f32)
    u = lax.linalg.triangular_solve(lhs, vc * bc[..., None], left_side=True, lower=True, unit_diagonal=True)
    w = lax.linalg.triangular_solve(lhs, kb * jnp.exp(gc)[..., None], left_side=True, lower=True, unit_diagonal=True)
    qk = jnp.where(incl, jnp.einsum('nbhid,nbhjd->nbhij', qc, kc) * decay, 0.0)
    q_dec = qc * jnp.exp(gc)[..., None]
    k_dec = kc * jnp.exp(gc[..., -1:] - gc)[..., None]
    g_last = jnp.exp(gc[..., -1])

    def step(state, xs):
        u_n, w_n, qk_n, qd_n, kd_n, gl_n = xs
        v_new = u_n - jnp.einsum('bhck,bhkv->bhcv', w_n, state)
        o_n = jnp.einsum('bhck,bhkv->bhcv', qd_n, state) + jnp.einsum('bhij,bhjv->bhiv', qk_n, v_new)
        state = state * gl_n[..., None, None] + jnp.einsum('bhck,bhcv->bhkv', kd_n, v_new)
        return state, o_n

    s0 = jnp.zeros((B, H, DK, DV), f32)
    _, o = lax.scan(step, s0, (u, w, qk, q_dec, k_dec, g_last))
    return jnp.transpose(o, (1, 0, 3, 2, 4)).reshape(B, S, H, DV)


def gated_deltanet(gq, gk, gv, gz, ga, gb, conv_w, a_log, dt_bias, norm_w):
    B, S, _ = gq.shape
    f32 = jnp.float32
    qkv = jax.nn.silu(causal_depthwise_conv(jnp.concatenate([gq, gk, gv], axis=-1), conv_w))
    q, k, v = jnp.split(qkv, [GDN_HEADS * GDN_DK, 2 * GDN_HEADS * GDN_DK], axis=-1)
    q = l2_normalize(q.reshape(B, S, GDN_HEADS, GDN_DK)) * (GDN_DK ** -0.5)
    k = l2_normalize(k.reshape(B, S, GDN_HEADS, GDN_DK))
    v = v.reshape(B, S, GDN_HEADS, GDN_DV)
    beta = jax.nn.sigmoid(gb.astype(f32))
    g = -jnp.exp(a_log.astype(f32)) * jax.nn.softplus(ga.astype(f32) + dt_bias.astype(f32))
    o = chunk_gated_delta_rule(q, k, v, g, beta)
    o = rms_norm(o, norm_w) * jax.nn.silu(gz.reshape(B, S, GDN_HEADS, GDN_DV).astype(f32))
    return o.reshape(B, S, GDN_HEADS * GDN_DV).astype(gq.dtype)


def to_query_blocks(t):
    B, S = t.shape[:2]
    return jnp.swapaxes(t.reshape((B, S // Q_BLOCK, Q_BLOCK) + t.shape[2:]), 0, 1)


def from_query_blocks(t):
    t = jnp.swapaxes(t, 0, 1)
    return t.reshape((t.shape[0], t.shape[1] * t.shape[2]) + t.shape[3:])


def stick_breaking_attention(q, k, v):
    S = q.shape[1]
    scale = SB_DIM ** -0.5
    kpos = jnp.arange(S)

    def block(args):
        qb, i = args
        z = jnp.einsum('bqhd,bkhd->bhqk', qb, k).astype(jnp.float32) * scale
        qpos = i * Q_BLOCK + jnp.arange(Q_BLOCK)
        mask = kpos[None, :] < qpos[:, None]
        log_1m = jnp.where(mask, jax.nn.log_sigmoid(-z), 0.0)
        rest = lax.cumsum(log_1m, axis=3, reverse=True) - log_1m
        wts = jnp.where(mask, jnp.exp(jax.nn.log_sigmoid(z) + rest), 0.0)
        return jnp.einsum('bhqk,bkhd->bqhd', wts.astype(v.dtype), v)

    nb = S // Q_BLOCK
    out = lax.map(block, (to_query_blocks(q), jnp.arange(nb)))
    return from_query_blocks(out)


def mla_attention(q_nope, q_rope, k_nope, k_rope, v):
    S = q_nope.shape[1]
    scale = (MLA_NOPE + MLA_ROPE) ** -0.5
    kpos = jnp.arange(S)

    def block(args):
        qn, qr, i = args
        s = (jnp.einsum('bqhd,bkhd->bhqk', qn, k_nope) + jnp.einsum('bqhd,bkd->bhqk', qr, k_rope)).astype(jnp.float32) * scale
        qpos = i * Q_BLOCK + jnp.arange(Q_BLOCK)
        s = jnp.where(kpos[None, :] <= qpos[:, None], s, -jnp.inf)
        pr = jax.nn.softmax(s, axis=-1)
        return jnp.einsum('bhqk,bkhd->bqhd', pr.astype(v.dtype), v)

    nb = S // Q_BLOCK
    out = lax.map(block, (to_query_blocks(q_nope), to_query_blocks(q_rope), jnp.arange(nb)))
    return from_query_blocks(out)


def hybrid_mixer(h, cos, sin, w_in, conv_w, a_log, dt_bias, gdn_norm_w, q_norm_w, kv_norm_w, w_uq, w_ukv, w_o):
    B, S, _ = h.shape
    proj = h @ w_in
    gq, gk, gv, gz, ga, gb, sq, sk, sv, mq, mkv = jnp.split(proj, IN_SPLITS, axis=-1)
    o_gdn = gated_deltanet(gq, gk, gv, gz, ga, gb, conv_w, a_log, dt_bias, gdn_norm_w)
    shp = (B, S, SB_HEADS, SB_DIM)
    o_sb = stick_breaking_attention(sq.reshape(shp), sk.reshape(shp), sv.reshape(shp)).reshape(B, S, SB_HEADS * SB_DIM)
    qf = (rms_norm(mq, q_norm_w) @ w_uq).reshape(B, S, MLA_HEADS, MLA_NOPE + MLA_ROPE)
    q_nope, q_rope = jnp.split(qf, [MLA_NOPE], axis=-1)
    ckv, k_rope = jnp.split(mkv, [MLA_KV_RANK], axis=-1)
    kvf = (rms_norm(ckv, kv_norm_w) @ w_ukv).reshape(B, S, MLA_HEADS, MLA_NOPE + MLA_V)
    k_nope, v_mla = jnp.split(kvf, [MLA_NOPE], axis=-1)
    q_rope = apply_rope(q_rope, cos[:, :, None, :], sin[:, :, None, :])
    k_rope = apply_rope(k_rope, cos, sin)
    o_mla = mla_attention(q_nope, q_rope, k_nope, k_rope, v_mla).reshape(B, S, MLA_HEADS * MLA_V)
    return jnp.concatenate([o_gdn, o_sb, o_mla], axis=-1) @ w_o


def _fwd_setup_inputs(seed: int = 0) -> dict:
    key = jax.random.key(seed)
    ks = jax.random.split(key, 24)
    f32 = jnp.float32
    L = DEPTH

    def nrm(k, shape, scale):
        return jax.random.normal(k, shape, f32) * scale

    x = nrm(ks[0], (BATCH, SEQ, D_MODEL), 1.0)
    p = nrm(ks[1], (DEPTH, BATCH, SEQ, PLE_DIM), 1.0)
    positions = (jnp.arange(SEQ, dtype=jnp.int32)[None, :]
                 + jax.random.randint(ks[2], (BATCH, 1), 0, 1024, dtype=jnp.int32))
    a_init = jax.random.uniform(ks[7], (L, GDN_HEADS), f32, 1.0, 16.0)
    dt = jnp.exp(jax.random.uniform(ks[8], (L, GDN_HEADS), f32, np.log(1e-3), np.log(1e-1)))
    return {
        "x": x,
        "p": p,
        "positions": positions,
        "ffa_w_in": nrm(ks[3], (L, D_MODEL, 2 * D_FF), D_MODEL ** -0.5),
        "ffa_w_out": nrm(ks[4], (L, D_FF, D_MODEL), D_FF ** -0.5 * DEEPNORM_BETA),
        "mix_w_in": nrm(ks[5], (L, D_MODEL, IN_TOTAL), D_MODEL ** -0.5),
        "gdn_conv_w": nrm(ks[6], (L, GDN_CONV, GDN_CONV_CH), GDN_CONV ** -0.5),
        "gdn_a_log": jnp.log(a_init),
        "gdn_dt_bias": dt + jnp.log(-jnp.expm1(-dt)),
        "gdn_norm_w": 1.0 + nrm(ks[9], (L, GDN_DV), 0.02),
        "mla_q_norm_w": 1.0 + nrm(ks[10], (L, MLA_Q_RANK), 0.02),
        "mla_kv_norm_w": 1.0 + nrm(ks[11], (L, MLA_KV_RANK), 0.02),
        "mla_w_uq": nrm(ks[12], (L, MLA_Q_RANK, MLA_HEADS * (MLA_NOPE + MLA_ROPE)), MLA_Q_RANK ** -0.5),
        "mla_w_ukv": nrm(ks[13], (L, MLA_KV_RANK, MLA_HEADS * (MLA_NOPE + MLA_V)), MLA_KV_RANK ** -0.5),
        "mix_w_o": nrm(ks[14], (L, MIX_WIDTH, D_MODEL), MIX_WIDTH ** -0.5 * DEEPNORM_BETA),
        "ffb_w_in": nrm(ks[15], (L, D_MODEL, 2 * D_FF), D_MODEL ** -0.5),
        "ffb_w_out": nrm(ks[16], (L, D_FF, D_MODEL), D_FF ** -0.5 * DEEPNORM_BETA),
        "ln_g": 1.0 + nrm(ks[17], (L, 3, D_MODEL), 0.02),
        "ln_b": nrm(ks[18], (L, 3, D_MODEL), 0.02),
        "ple_w_gate": nrm(ks[19], (L, D_MODEL, D_MODEL), D_MODEL ** -0.5),
        "ple_w_proj": nrm(ks[20], (L, PLE_DIM, D_MODEL), PLE_DIM ** -0.5 * DEEPNORM_BETA),
    }


def _fwd_reference(x, p, positions, ffa_w_in, ffa_w_out, mix_w_in, gdn_conv_w, gdn_a_log, gdn_dt_bias,
              gdn_norm_w, mla_q_norm_w, mla_kv_norm_w, mla_w_uq, mla_w_ukv, mix_w_o,
              ffb_w_in, ffb_w_out, ln_g, ln_b, ple_w_gate, ple_w_proj):
    cos, sin = rope_tables(positions)
    h = x
    for i in range(DEPTH):
        h = layer_norm(DEEPNORM_ALPHA * h + 0.5 * swiglu(h, ffa_w_in[i], ffa_w_out[i]), ln_g[i, 0], ln_b[i, 0])
        mix = hybrid_mixer(h, cos, sin, mix_w_in[i], gdn_conv_w[i], gdn_a_log[i], gdn_dt_bias[i], gdn_norm_w[i],
                           mla_q_norm_w[i], mla_kv_norm_w[i], mla_w_uq[i], mla_w_ukv[i], mix_w_o[i])
        h = layer_norm(DEEPNORM_ALPHA * h + mix, ln_g[i, 1], ln_b[i, 1])
        h = layer_norm(DEEPNORM_ALPHA * h + 0.5 * swiglu(h, ffb_w_in[i], ffb_w_out[i]), ln_g[i, 2], ln_b[i, 2])
        h = h + jax.nn.sigmoid(h @ ple_w_gate[i]) * (p[i] @ ple_w_proj[i])
    return h


import jax as _jax
import jax.numpy as _jnp

TWIN_FORMAT = 'train_step'
FWD_PARAMS = ['x', 'p', 'positions', 'ffa_w_in', 'ffa_w_out', 'mix_w_in', 'gdn_conv_w', 'gdn_a_log', 'gdn_dt_bias', 'gdn_norm_w', 'mla_q_norm_w', 'mla_kv_norm_w', 'mla_w_uq', 'mla_w_ukv', 'mix_w_o', 'ffb_w_in', 'ffb_w_out', 'ln_g', 'ln_b', 'ple_w_gate', 'ple_w_proj']
TWIN_WEIGHTS = ['ffa_w_in', 'ffa_w_out', 'mix_w_in', 'gdn_conv_w', 'gdn_a_log', 'gdn_dt_bias', 'gdn_norm_w', 'mla_q_norm_w', 'mla_kv_norm_w', 'mla_w_uq', 'mla_w_ukv', 'mix_w_o', 'ffb_w_in', 'ffb_w_out', 'ln_g', 'ln_b', 'ple_w_gate', 'ple_w_proj']
TWIN_DIFF_INPUT = 'x'
TWIN_INPUTS = ['x', 'p', 'positions', 'ffa_w_in', 'ffa_w_out', 'mix_w_in', 'gdn_conv_w', 'gdn_a_log', 'gdn_dt_bias', 'gdn_norm_w', 'mla_q_norm_w', 'mla_kv_norm_w', 'mla_w_uq', 'mla_w_ukv', 'mix_w_o', 'ffb_w_in', 'ffb_w_out', 'ln_g', 'ln_b', 'ple_w_gate', 'ple_w_proj', 'loss_target', 'm_ffa_w_in', 'm_ffa_w_out', 'm_mix_w_in', 'm_gdn_conv_w', 'm_gdn_a_log', 'm_gdn_dt_bias', 'm_gdn_norm_w', 'm_mla_q_norm_w', 'm_mla_kv_norm_w', 'm_mla_w_uq', 'm_mla_w_ukv', 'm_mix_w_o', 'm_ffb_w_in', 'm_ffb_w_out', 'm_ln_g', 'm_ln_b', 'm_ple_w_gate', 'm_ple_w_proj', 'v_ffa_w_in', 'v_ffa_w_out', 'v_mix_w_in', 'v_gdn_conv_w', 'v_gdn_a_log', 'v_gdn_dt_bias', 'v_gdn_norm_w', 'v_mla_q_norm_w', 'v_mla_kv_norm_w', 'v_mla_w_uq', 'v_mla_w_ukv', 'v_mix_w_o', 'v_ffb_w_in', 'v_ffb_w_out', 'v_ln_g', 'v_ln_b', 'v_ple_w_gate', 'v_ple_w_proj']
TWIN_OUTPUTS = ['loss', 'grad_x', 'grad_ffa_w_in', 'grad_ffa_w_out', 'grad_mix_w_in', 'grad_gdn_conv_w', 'grad_gdn_a_log', 'grad_gdn_dt_bias', 'grad_gdn_norm_w', 'grad_mla_q_norm_w', 'grad_mla_kv_norm_w', 'grad_mla_w_uq', 'grad_mla_w_ukv', 'grad_mix_w_o', 'grad_ffb_w_in', 'grad_ffb_w_out', 'grad_ln_g', 'grad_ln_b', 'grad_ple_w_gate', 'grad_ple_w_proj', 'delta_ffa_w_in', 'delta_ffa_w_out', 'delta_mix_w_in', 'delta_gdn_conv_w', 'delta_gdn_a_log', 'delta_gdn_dt_bias', 'delta_gdn_norm_w', 'delta_mla_q_norm_w', 'delta_mla_kv_norm_w', 'delta_mla_w_uq', 'delta_mla_w_ukv', 'delta_mix_w_o', 'delta_ffb_w_in', 'delta_ffb_w_out', 'delta_ln_g', 'delta_ln_b', 'delta_ple_w_gate', 'delta_ple_w_proj', 'new_m_ffa_w_in', 'new_m_ffa_w_out', 'new_m_mix_w_in', 'new_m_gdn_conv_w', 'new_m_gdn_a_log', 'new_m_gdn_dt_bias', 'new_m_gdn_norm_w', 'new_m_mla_q_norm_w', 'new_m_mla_kv_norm_w', 'new_m_mla_w_uq', 'new_m_mla_w_ukv', 'new_m_mix_w_o', 'new_m_ffb_w_in', 'new_m_ffb_w_out', 'new_m_ln_g', 'new_m_ln_b', 'new_m_ple_w_gate', 'new_m_ple_w_proj', 'new_v_ffa_w_in', 'new_v_ffa_w_out', 'new_v_mix_w_in', 'new_v_gdn_conv_w', 'new_v_gdn_a_log', 'new_v_gdn_dt_bias', 'new_v_gdn_norm_w', 'new_v_mla_q_norm_w', 'new_v_mla_kv_norm_w', 'new_v_mla_w_uq', 'new_v_mla_w_ukv', 'new_v_mix_w_o', 'new_v_ffb_w_in', 'new_v_ffb_w_out', 'new_v_ln_g', 'new_v_ln_b', 'new_v_ple_w_gate', 'new_v_ple_w_proj']
TWIN_LEAF_KINDS = {'loss': 'loss', 'grad_x': 'grad_x', 'grad_ffa_w_in': 'grad_w', 'grad_ffa_w_out': 'grad_w', 'grad_mix_w_in': 'grad_w', 'grad_gdn_conv_w': 'grad_w', 'grad_gdn_a_log': 'grad_w', 'grad_gdn_dt_bias': 'grad_w', 'grad_gdn_norm_w': 'grad_w', 'grad_mla_q_norm_w': 'grad_w', 'grad_mla_kv_norm_w': 'grad_w', 'grad_mla_w_uq': 'grad_w', 'grad_mla_w_ukv': 'grad_w', 'grad_mix_w_o': 'grad_w', 'grad_ffb_w_in': 'grad_w', 'grad_ffb_w_out': 'grad_w', 'grad_ln_g': 'grad_w', 'grad_ln_b': 'grad_w', 'grad_ple_w_gate': 'grad_w', 'grad_ple_w_proj': 'grad_w', 'delta_ffa_w_in': 'delta_w', 'delta_ffa_w_out': 'delta_w', 'delta_mix_w_in': 'delta_w', 'delta_gdn_conv_w': 'delta_w', 'delta_gdn_a_log': 'delta_w', 'delta_gdn_dt_bias': 'delta_w', 'delta_gdn_norm_w': 'delta_w', 'delta_mla_q_norm_w': 'delta_w', 'delta_mla_kv_norm_w': 'delta_w', 'delta_mla_w_uq': 'delta_w', 'delta_mla_w_ukv': 'delta_w', 'delta_mix_w_o': 'delta_w', 'delta_ffb_w_in': 'delta_w', 'delta_ffb_w_out': 'delta_w', 'delta_ln_g': 'delta_w', 'delta_ln_b': 'delta_w', 'delta_ple_w_gate': 'delta_w', 'delta_ple_w_proj': 'delta_w', 'new_m_ffa_w_in': 'new_m', 'new_m_ffa_w_out': 'new_m', 'new_m_mix_w_in': 'new_m', 'new_m_gdn_conv_w': 'new_m', 'new_m_gdn_a_log': 'new_m', 'new_m_gdn_dt_bias': 'new_m', 'new_m_gdn_norm_w': 'new_m', 'new_m_mla_q_norm_w': 'new_m', 'new_m_mla_kv_norm_w': 'new_m', 'new_m_mla_w_uq': 'new_m', 'new_m_mla_w_ukv': 'new_m', 'new_m_mix_w_o': 'new_m', 'new_m_ffb_w_in': 'new_m', 'new_m_ffb_w_out': 'new_m', 'new_m_ln_g': 'new_m', 'new_m_ln_b': 'new_m', 'new_m_ple_w_gate': 'new_m', 'new_m_ple_w_proj': 'new_m', 'new_v_ffa_w_in': 'new_v', 'new_v_ffa_w_out': 'new_v', 'new_v_mix_w_in': 'new_v', 'new_v_gdn_conv_w': 'new_v', 'new_v_gdn_a_log': 'new_v', 'new_v_gdn_dt_bias': 'new_v', 'new_v_gdn_norm_w': 'new_v', 'new_v_mla_q_norm_w': 'new_v', 'new_v_mla_kv_norm_w': 'new_v', 'new_v_mla_w_uq': 'new_v', 'new_v_mla_w_ukv': 'new_v', 'new_v_mix_w_o': 'new_v', 'new_v_ffb_w_in': 'new_v', 'new_v_ffb_w_out': 'new_v', 'new_v_ln_g': 'new_v', 'new_v_ln_b': 'new_v', 'new_v_ple_w_gate': 'new_v', 'new_v_ple_w_proj': 'new_v'}


def _forward(args):
    return _fwd_reference(*[args[k] for k in FWD_PARAMS])


def _output_shape():
    def fwd():
        inp = _fwd_setup_inputs(0)
        return _fwd_reference(*[inp[k] for k in FWD_PARAMS])
    out = _jax.eval_shape(fwd)
    return out.shape, out.dtype

N_MICROBATCH = 1
ADAM_LR = 0.001
ADAM_B1 = 0.9
ADAM_B2 = 0.999
ADAM_EPS = 1e-08
ADAM_WD = 0.01
ADAM_STEP = 10
PER_EXAMPLE_BATCH_AXIS = {'x': 0, 'p': 1, 'positions': 0, 'loss_target': 0}
SHARED_INPUTS = []
_WEIGHT_DTYPES = {'ffa_w_in': _jnp.float32, 'ffa_w_out': _jnp.float32, 'mix_w_in': _jnp.float32, 'gdn_conv_w': _jnp.float32, 'gdn_a_log': _jnp.float32, 'gdn_dt_bias': _jnp.float32, 'gdn_norm_w': _jnp.float32, 'mla_q_norm_w': _jnp.float32, 'mla_kv_norm_w': _jnp.float32, 'mla_w_uq': _jnp.float32, 'mla_w_ukv': _jnp.float32, 'mix_w_o': _jnp.float32, 'ffb_w_in': _jnp.float32, 'ffb_w_out': _jnp.float32, 'ln_g': _jnp.float32, 'ln_b': _jnp.float32, 'ple_w_gate': _jnp.float32, 'ple_w_proj': _jnp.float32}
MOMENT_SCALE = {'ffa_w_in': 1.270523e-02, 'ffa_w_out': 4.153283e-02, 'mix_w_in': 3.281296e-02, 'gdn_conv_w': 3.722952e-02, 'gdn_a_log': 1.740276e-01, 'gdn_dt_bias': 1.671553e-01, 'gdn_norm_w': 1.478279e-01, 'mla_q_norm_w': 1.219137e-02, 'mla_kv_norm_w': 2.965943e-02, 'mla_w_uq': 1.006276e-02, 'mla_w_ukv': 1.323272e-02, 'mix_w_o': 8.079819e-02, 'ffb_w_in': 1.206792e-02, 'ffb_w_out': 3.942356e-02, 'ln_g': 1.322875e+01, 'ln_b': 1.201398e+00, 'ple_w_gate': 3.595361e-02, 'ple_w_proj': 1.937347e-01}


def _to_microbatches(a, axis):
    t = _jnp.moveaxis(a, axis, 0)
    t = t.reshape((N_MICROBATCH, t.shape[0] // N_MICROBATCH) + t.shape[1:])
    return _jnp.moveaxis(t, 1, axis + 1)


def setup_inputs(seed: int = 0) -> dict:
    inp = _fwd_setup_inputs(seed)
    key = _jax.random.fold_in(_jax.random.key(seed), 7919)
    shape, _ = _output_shape()
    out = dict(inp)
    out["loss_target"] = _jax.random.normal(_jax.random.fold_in(key, 0), shape, _jnp.float32)
    for i, name in enumerate(TWIN_WEIGHTS):
        w = inp[name].astype(_jnp.float32)
        if MOMENT_SCALE is None:
            s = _jnp.sqrt(_jnp.mean(_jnp.square(w)) + 1e-30)
        else:
            s = MOMENT_SCALE[name]
        km, kv = _jax.random.split(_jax.random.fold_in(key, i + 1))
        out[name] = w
        out["m_" + name] = s * _jax.random.normal(km, w.shape, _jnp.float32)
        out["v_" + name] = (s * s) * _jax.random.uniform(kv, w.shape, _jnp.float32, 0.5, 1.5)
    if N_MICROBATCH > 1:
        for name, axis in PER_EXAMPLE_BATCH_AXIS.items():
            out[name] = _to_microbatches(out[name], axis)
    return {'x': out['x'], 'p': out['p'], 'positions': out['positions'], 'ffa_w_in': out['ffa_w_in'], 'ffa_w_out': out['ffa_w_out'], 'mix_w_in': out['mix_w_in'], 'gdn_conv_w': out['gdn_conv_w'], 'gdn_a_log': out['gdn_a_log'], 'gdn_dt_bias': out['gdn_dt_bias'], 'gdn_norm_w': out['gdn_norm_w'], 'mla_q_norm_w': out['mla_q_norm_w'], 'mla_kv_norm_w': out['mla_kv_norm_w'], 'mla_w_uq': out['mla_w_uq'], 'mla_w_ukv': out['mla_w_ukv'], 'mix_w_o': out['mix_w_o'], 'ffb_w_in': out['ffb_w_in'], 'ffb_w_out': out['ffb_w_out'], 'ln_g': out['ln_g'], 'ln_b': out['ln_b'], 'ple_w_gate': out['ple_w_gate'], 'ple_w_proj': out['ple_w_proj'], 'loss_target': out['loss_target'], 'm_ffa_w_in': out['m_ffa_w_in'], 'm_ffa_w_out': out['m_ffa_w_out'], 'm_mix_w_in': out['m_mix_w_in'], 'm_gdn_conv_w': out['m_gdn_conv_w'], 'm_gdn_a_log': out['m_gdn_a_log'], 'm_gdn_dt_bias': out['m_gdn_dt_bias'], 'm_gdn_norm_w': out['m_gdn_norm_w'], 'm_mla_q_norm_w': out['m_mla_q_norm_w'], 'm_mla_kv_norm_w': out['m_mla_kv_norm_w'], 'm_mla_w_uq': out['m_mla_w_uq'], 'm_mla_w_ukv': out['m_mla_w_ukv'], 'm_mix_w_o': out['m_mix_w_o'], 'm_ffb_w_in': out['m_ffb_w_in'], 'm_ffb_w_out': out['m_ffb_w_out'], 'm_ln_g': out['m_ln_g'], 'm_ln_b': out['m_ln_b'], 'm_ple_w_gate': out['m_ple_w_gate'], 'm_ple_w_proj': out['m_ple_w_proj'], 'v_ffa_w_in': out['v_ffa_w_in'], 'v_ffa_w_out': out['v_ffa_w_out'], 'v_mix_w_in': out['v_mix_w_in'], 'v_gdn_conv_w': out['v_gdn_conv_w'], 'v_gdn_a_log': out['v_gdn_a_log'], 'v_gdn_dt_bias': out['v_gdn_dt_bias'], 'v_gdn_norm_w': out['v_gdn_norm_w'], 'v_mla_q_norm_w': out['v_mla_q_norm_w'], 'v_mla_kv_norm_w': out['v_mla_kv_norm_w'], 'v_mla_w_uq': out['v_mla_w_uq'], 'v_mla_w_ukv': out['v_mla_w_ukv'], 'v_mix_w_o': out['v_mix_w_o'], 'v_ffb_w_in': out['v_ffb_w_in'], 'v_ffb_w_out': out['v_ffb_w_out'], 'v_ln_g': out['v_ln_g'], 'v_ln_b': out['v_ln_b'], 'v_ple_w_gate': out['v_ple_w_gate'], 'v_ple_w_proj': out['v_ple_w_proj']}


def _loss(weights, diff, rest, loss_target):
    with _jax.named_scope("forward"):
        args = {**rest, TWIN_DIFF_INPUT: diff, **{k: w.astype(_WEIGHT_DTYPES[k]) for k, w in weights.items()}}
        y = _forward(args)
    with _jax.named_scope("loss_head"):
        err = _jnp.square(y.astype(_jnp.float32) - loss_target)
        return 0.5 * _jnp.sum(_jnp.mean(err, axis=-1)) if err.ndim else 0.5 * err


def _adamw(w, g, m, v):
    m = ADAM_B1 * m + (1.0 - ADAM_B1) * g
    v = ADAM_B2 * v + (1.0 - ADAM_B2) * _jnp.square(g)
    m_hat = m / (1.0 - ADAM_B1 ** ADAM_STEP)
    v_hat = v / (1.0 - ADAM_B2 ** ADAM_STEP)
    delta = -ADAM_LR * (m_hat / (_jnp.sqrt(v_hat) + ADAM_EPS) + ADAM_WD * w)
    return delta, m, v


def reference(x, p, positions, ffa_w_in, ffa_w_out, mix_w_in, gdn_conv_w, gdn_a_log, gdn_dt_bias, gdn_norm_w, mla_q_norm_w, mla_kv_norm_w, mla_w_uq, mla_w_ukv, mix_w_o, ffb_w_in, ffb_w_out, ln_g, ln_b, ple_w_gate, ple_w_proj, loss_target, m_ffa_w_in, m_ffa_w_out, m_mix_w_in, m_gdn_conv_w, m_gdn_a_log, m_gdn_dt_bias, m_gdn_norm_w, m_mla_q_norm_w, m_mla_kv_norm_w, m_mla_w_uq, m_mla_w_ukv, m_mix_w_o, m_ffb_w_in, m_ffb_w_out, m_ln_g, m_ln_b, m_ple_w_gate, m_ple_w_proj, v_ffa_w_in, v_ffa_w_out, v_mix_w_in, v_gdn_conv_w, v_gdn_a_log, v_gdn_dt_bias, v_gdn_norm_w, v_mla_q_norm_w, v_mla_kv_norm_w, v_mla_w_uq, v_mla_w_ukv, v_mix_w_o, v_ffb_w_in, v_ffb_w_out, v_ln_g, v_ln_b, v_ple_w_gate, v_ple_w_proj):
    given = dict(x=x, p=p, positions=positions, ffa_w_in=ffa_w_in, ffa_w_out=ffa_w_out, mix_w_in=mix_w_in, gdn_conv_w=gdn_conv_w, gdn_a_log=gdn_a_log, gdn_dt_bias=gdn_dt_bias, gdn_norm_w=gdn_norm_w, mla_q_norm_w=mla_q_norm_w, mla_kv_norm_w=mla_kv_norm_w, mla_w_uq=mla_w_uq, mla_w_ukv=mla_w_ukv, mix_w_o=mix_w_o, ffb_w_in=ffb_w_in, ffb_w_out=ffb_w_out, ln_g=ln_g, ln_b=ln_b, ple_w_gate=ple_w_gate, ple_w_proj=ple_w_proj, loss_target=loss_target, m_ffa_w_in=m_ffa_w_in, m_ffa_w_out=m_ffa_w_out, m_mix_w_in=m_mix_w_in, m_gdn_conv_w=m_gdn_conv_w, m_gdn_a_log=m_gdn_a_log, m_gdn_dt_bias=m_gdn_dt_bias, m_gdn_norm_w=m_gdn_norm_w, m_mla_q_norm_w=m_mla_q_norm_w, m_mla_kv_norm_w=m_mla_kv_norm_w, m_mla_w_uq=m_mla_w_uq, m_mla_w_ukv=m_mla_w_ukv, m_mix_w_o=m_mix_w_o, m_ffb_w_in=m_ffb_w_in, m_ffb_w_out=m_ffb_w_out, m_ln_g=m_ln_g, m_ln_b=m_ln_b, m_ple_w_gate=m_ple_w_gate, m_ple_w_proj=m_ple_w_proj, v_ffa_w_in=v_ffa_w_in, v_ffa_w_out=v_ffa_w_out, v_mix_w_in=v_mix_w_in, v_gdn_conv_w=v_gdn_conv_w, v_gdn_a_log=v_gdn_a_log, v_gdn_dt_bias=v_gdn_dt_bias, v_gdn_norm_w=v_gdn_norm_w, v_mla_q_norm_w=v_mla_q_norm_w, v_mla_kv_norm_w=v_mla_kv_norm_w, v_mla_w_uq=v_mla_w_uq, v_mla_w_ukv=v_mla_w_ukv, v_mix_w_o=v_mix_w_o, v_ffb_w_in=v_ffb_w_in, v_ffb_w_out=v_ffb_w_out, v_ln_g=v_ln_g, v_ln_b=v_ln_b, v_ple_w_gate=v_ple_w_gate, v_ple_w_proj=v_ple_w_proj)
    weights = {n: given[n] for n in TWIN_WEIGHTS}
    shared = {n: given[n] for n in SHARED_INPUTS}
    per_example = {n: given[n] for n in ['x', 'p', 'positions']}
    grad_fn = _jax.value_and_grad(_loss, argnums=(0, 1))

    def one_microbatch(ex, loss_target):
        ex = dict(ex)
        diff = ex.pop(TWIN_DIFF_INPUT)
        return grad_fn(weights, diff, {**shared, **ex}, loss_target)

    if N_MICROBATCH == 1:
        loss, (grad_w, grad_x) = one_microbatch(per_example, given["loss_target"])
    else:
        def body(carry, xs):
            loss_sum, grad_sum = carry
            l_k, (gw_k, gx_k) = one_microbatch(xs[0], xs[1])
            with _jax.named_scope("update"):
                return (loss_sum + l_k, _jax.tree.map(_jnp.add, grad_sum, gw_k)), gx_k

        init = (_jnp.zeros((), _jnp.float32), _jax.tree.map(_jnp.zeros_like, weights))
        (loss, grad_w), grad_x = _jax.lax.scan(body, init, (per_example, given["loss_target"]))
    with _jax.named_scope("update"):
        delta_w, new_m, new_v = {}, {}, {}
        for n in TWIN_WEIGHTS:
            delta_w[n], new_m[n], new_v[n] = _adamw(weights[n], grad_w[n], given["m_" + n], given["v_" + n])
    return (loss, grad_x, *[grad_w[n] for n in TWIN_WEIGHTS], *[delta_w[n] for n in TWIN_WEIGHTS],
            *[new_m[n] for n in TWIN_WEIGHTS], *[new_v[n] for n in TWIN_WEIGHTS])
```

```python
import functools
import numpy as np
import jax
import jax.numpy as jnp
from jax import lax
from jax.experimental import pallas as pl
from jax.experimental.pallas import tpu as pltpu

F32 = jnp.float32
BF16 = jnp.bfloat16

DEPTH = 2
LN_EPS = 1e-5
RMS_EPS = 1e-6
ALPHA = (2 * DEPTH) ** 0.25
GDN_HEADS, GDN_D, GDN_CONV, GDN_CHUNK = 8, 64, 4, 64
SB_HEADS, SB_DIM = 4, 64
MLA_HEADS, MLA_NOPE, MLA_ROPE, MLA_V, MLA_Q_RANK, MLA_KV_RANK = 4, 64, 32, 64, 256, 128
ROPE_BASE = 10000.0
HALF_ROPE = MLA_ROPE // 2
LANES = 128
N_DEV = 8
ADAM_LR, ADAM_B1, ADAM_B2, ADAM_EPS, ADAM_WD, ADAM_STEP = 0.001, 0.9, 0.999, 1e-08, 0.01, 10

CB_GQ, CB_GK, CB_GV, CB_GZ = 0, 8, 16, 24
CB_SQ, CB_SK, CB_SV = 32, 36, 40
CB_MQ, CB_MKV, CB_GAB = 44, 46, 48
PROJ_W = 49 * LANES
VMEM_LIMIT = 56 * 1024 * 1024

NT_DIMS = (((1,), (1,)), ((), ()))
TN_DIMS = (((0,), (0,)), ((), ()))


def _cp(sem):
    return pltpu.CompilerParams(dimension_semantics=sem, vmem_limit_bytes=VMEM_LIMIT)


def _bdot(a, b):
    return jnp.dot(a.astype(BF16), b.astype(BF16), preferred_element_type=F32)


def _bdot_nt(a, b):
    return lax.dot_general(a.astype(BF16), b.astype(BF16), NT_DIMS, preferred_element_type=F32)


def _bdot_tn(a, b):
    return lax.dot_general(a.astype(BF16), b.astype(BF16), TN_DIMS, preferred_element_type=F32)


def _split2(a):
    hi = a.astype(BF16)
    lo = (a - hi.astype(F32)).astype(BF16)
    return hi, lo


def _hdot(a, b):
    m = a.shape[0]
    a_hi, a_lo = _split2(a)
    b_hi, b_lo = _split2(b)
    r = jnp.dot(jnp.concatenate([a_hi, a_lo], axis=0), b_hi, preferred_element_type=F32)
    return r[:m] + r[m:] + jnp.dot(a_hi, b_lo, preferred_element_type=F32)


def _hdot_tn(a, b):
    a_hi, a_lo = _split2(a)
    b_hi, b_lo = _split2(b)
    d = functools.partial(lax.dot_general, dimension_numbers=TN_DIMS, preferred_element_type=F32)
    return d(a_hi, b_hi) + d(a_lo, b_hi) + d(a_hi, b_lo)


def _ones_dot(x, ones_bf16):
    hi = x.astype(BF16)
    r1 = x - hi.astype(F32)
    mid = r1.astype(BF16)
    lo = (r1 - mid.astype(F32)).astype(BF16)
    d = functools.partial(jnp.dot, preferred_element_type=F32)
    return d(hi, ones_bf16) + d(mid, ones_bf16) + d(lo, ones_bf16)


def _ones_dot_left(ones_bf16, x):
    hi = x.astype(BF16)
    r1 = x - hi.astype(F32)
    mid = r1.astype(BF16)
    lo = (r1 - mid.astype(F32)).astype(BF16)
    d = functools.partial(jnp.dot, preferred_element_type=F32)
    return d(ones_bf16, hi) + d(ones_bf16, mid) + d(ones_bf16, lo)


def _iota2(shape, dim):
    return lax.broadcasted_iota(jnp.int32, shape, dim)


def _sigmoid(x):
    return 1.0 / (1.0 + jnp.exp(-x))


def _softplus(x):
    return jnp.maximum(x, 0.0) + jnp.log(1.0 + jnp.exp(-jnp.abs(x)))


def _pick(n, limit, mult):
    if n <= limit:
        return n
    best = None
    for t in range(mult, limit + 1, mult):
        if n % t == 0:
            best = t
    assert best is not None, (n, limit, mult)
    return best


def mm_nn(a, b, name, out_dtype=F32, res=None, res_scale=1.0):
    M, K = a.shape
    _, N = b.shape
    tm, tn, tk = _pick(M, 512, 16), _pick(N, 1024, LANES), _pick(K, 1024, LANES)
    nk = K // tk
    has_res = res is not None

    def body(*refs):
        if has_res:
            a_ref, b_ref, r_ref, o_ref, acc_ref = refs
        else:
            a_ref, b_ref, o_ref, acc_ref = refs
        k = pl.program_id(2)

        @pl.when(k == 0)
        def _():
            acc_ref[...] = jnp.zeros_like(acc_ref)

        acc_ref[...] += _bdot(a_ref[...], b_ref[...])

        @pl.when(k == nk - 1)
        def _():
            out = acc_ref[...]
            if has_res:
                out = out + res_scale * r_ref[...]
            o_ref[...] = out.astype(o_ref.dtype)

    in_specs = [pl.BlockSpec((tm, tk), lambda i, j, k: (i, k)), pl.BlockSpec((tk, tn), lambda i, j, k: (k, j))]
    args = [a, b]
    if has_res:
        in_specs.append(pl.BlockSpec((tm, tn), lambda i, j, k: (i, j)))
        args.append(res)
    return pl.pallas_call(
        body, grid=(M // tm, N // tn, nk), in_specs=in_specs,
        out_specs=pl.BlockSpec((tm, tn), lambda i, j, k: (i, j)),
        out_shape=jax.ShapeDtypeStruct((M, N), out_dtype),
        scratch_shapes=[pltpu.VMEM((tm, tn), F32)],
        compiler_params=_cp(("parallel", "parallel", "arbitrary")), name=name)(*args)


def mm_tn(a, b, name, out_dtype=F32):
    T, K = a.shape
    _, N = b.shape
    tk, tn, tt = _pick(K, 512, LANES), _pick(N, 1024, LANES), _pick(T, 512, 16)
    nt = T // tt

    def body(a_ref, b_ref, o_ref, acc_ref):
        t = pl.program_id(2)

        @pl.when(t == 0)
        def _():
            acc_ref[...] = jnp.zeros_like(acc_ref)

        acc_ref[...] += _bdot_tn(a_ref[...], b_ref[...])

        @pl.when(t == nt - 1)
        def _():
            o_ref[...] = acc_ref[...].astype(o_ref.dtype)

    return pl.pallas_call(
        body, grid=(K // tk, N // tn, nt),
        in_specs=[pl.BlockSpec((tt, tk), lambda i, j, t: (t, i)), pl.BlockSpec((tt, tn), lambda i, j, t: (t, j))],
        out_specs=pl.BlockSpec((tk, tn), lambda i, j, t: (i, j)),
        out_shape=jax.ShapeDtypeStruct((K, N), out_dtype),
        scratch_shapes=[pltpu.VMEM((tk, tn), F32)],
        compiler_params=_cp(("parallel", "parallel", "arbitrary")), name=name)(a, b)


def _ln_apply(z, g, b):
    mu = jnp.mean(z, axis=-1, keepdims=True)
    zc = z - mu
    var = jnp.mean(zc * zc, axis=-1, keepdims=True)
    rstd = lax.rsqrt(var + LN_EPS)
    xhat = zc * rstd
    return xhat * g + b, xhat, rstd


def ln_bwd(dout, xhat, rstd, g, name):
    T, D = dout.shape
    tm = _pick(T, 512, 8)

    def body(do_ref, xh_ref, rs_ref, g_ref, dz_ref, dg_ref, db_ref):
        i = pl.program_id(0)

        @pl.when(i == 0)
        def _():
            dg_ref[...] = jnp.zeros_like(dg_ref)
            db_ref[...] = jnp.zeros_like(db_ref)

        do = do_ref[...]
        xh = xh_ref[...]
        dxh = do * g_ref[...]
        m1 = jnp.mean(dxh, axis=-1, keepdims=True)
        m2 = jnp.mean(dxh * xh, axis=-1, keepdims=True)
        dz_ref[...] = rs_ref[...] * (dxh - m1 - xh * m2)
        dg_ref[...] += jnp.sum(do * xh, axis=0, keepdims=True)
        db_ref[...] += jnp.sum(do, axis=0, keepdims=True)

    row = pl.BlockSpec((tm, D), lambda i: (i, 0))
    vec = pl.BlockSpec((1, D), lambda i: (0, 0))
    return pl.pallas_call(
        body, grid=(T // tm,),
        in_specs=[row, row, pl.BlockSpec((tm, 1), lambda i: (i, 0)), vec],
        out_specs=[row, vec, vec],
        out_shape=[jax.ShapeDtypeStruct((T, D), F32), jax.ShapeDtypeStruct((1, D), F32), jax.ShapeDtypeStruct((1, D), F32)],
        compiler_params=_cp(("arbitrary",)), name=name)(dout, xhat, rstd, g)


FFN_TF = 256


def ffn_interleave(w_in):
    D, F2 = w_in.shape
    nf = F2 // 2 // FFN_TF
    return w_in.reshape(D, 2, nf, FFN_TF).transpose(0, 2, 1, 3).reshape(D, F2)


def ffn_deinterleave(w_in_r):
    D, F2 = w_in_r.shape
    nf = F2 // 2 // FFN_TF
    return w_in_r.reshape(D, nf, 2, FFN_TF).transpose(0, 2, 1, 3).reshape(D, F2)


def ffn_fwd(h, w_in_r, w_out, g, b, name):
    T, D = h.shape
    F = w_out.shape[0]
    tm, tf = _pick(T, 512, 8), FFN_TF
    nf = F // tf

    def body(h_ref, wi_ref, wo_ref, g_ref, b_ref, out_ref, xh_ref, rs_ref, GU_ref, acc_ref):
        j = pl.program_id(1)

        @pl.when(j == 0)
        def _():
            acc_ref[...] = jnp.zeros_like(acc_ref)

        GU = jnp.dot(h_ref[...].astype(BF16), wi_ref[...], preferred_element_type=F32)
        GU_ref[...] = GU
        G = GU[:, :tf]
        U = GU[:, tf:]
        act = G * _sigmoid(G) * U
        acc_ref[...] += _bdot(act, wo_ref[...])

        @pl.when(j == nf - 1)
        def _():
            z = ALPHA * h_ref[...] + 0.5 * acc_ref[...]
            out, xh, rs = _ln_apply(z, g_ref[...], b_ref[...])
            out_ref[...] = out
            xh_ref[...] = xh
            rs_ref[...] = rs

    row = pl.BlockSpec((tm, D), lambda i, j: (i, 0))
    vec = pl.BlockSpec((1, D), lambda i, j: (0, 0))
    return pl.pallas_call(
        body, grid=(T // tm, nf),
        in_specs=[row, pl.BlockSpec((D, 2 * tf), lambda i, j: (0, j)), pl.BlockSpec((tf, D), lambda i, j: (j, 0)), vec, vec],
        out_specs=[row, row, pl.BlockSpec((tm, 1), lambda i, j: (i, 0)), pl.BlockSpec((tm, 2 * tf), lambda i, j: (i, j))],
        out_shape=[jax.ShapeDtypeStruct((T, D), F32), jax.ShapeDtypeStruct((T, D), F32), jax.ShapeDtypeStruct((T, 1), F32),
                   jax.ShapeDtypeStruct((T, 2 * F), F32)],
        scratch_shapes=[pltpu.VMEM((tm, D), F32)],
        compiler_params=_cp(("parallel", "arbitrary")), name=name)(h, w_in_r, w_out, g, b)


def ffn_bwd(dz, GU, w_in_rt, w_out_t, name):
    T, D = dz.shape
    F = GU.shape[1] // 2
    tm, tf = _pick(T, 512, 16), FFN_TF
    nf = F // tf

    def body(dz_ref, GU_ref, wot_ref, wit_ref, dh_ref, dGU_ref, act_ref, acc_ref):
        j = pl.program_id(1)

        @pl.when(j == 0)
        def _():
            acc_ref[...] = jnp.zeros_like(acc_ref)

        dy = (0.5 * dz_ref[...]).astype(BF16)
        dact = jnp.dot(dy, wot_ref[...], preferred_element_type=F32)
        GU = GU_ref[...]
        G = GU[:, :tf]
        U = GU[:, tf:]
        s = _sigmoid(G)
        silu = G * s
        dG = dact * U * (s * (1.0 + G * (1.0 - s)))
        dU = dact * silu
        dGU = jnp.concatenate([dG, dU], axis=1).astype(BF16)
        dGU_ref[...] = dGU
        act_ref[...] = (silu * U).astype(BF16)
        acc_ref[...] += jnp.dot(dGU, wit_ref[...], preferred_element_type=F32)

        @pl.when(j == nf - 1)
        def _():
            dh_ref[...] = ALPHA * dz_ref[...] + acc_ref[...]

    row = pl.BlockSpec((tm, D), lambda i, j: (i, 0))
    gublk = pl.BlockSpec((tm, 2 * tf), lambda i, j: (i, j))
    return pl.pallas_call(
        body, grid=(T // tm, nf),
        in_specs=[row, gublk, pl.BlockSpec((D, tf), lambda i, j: (0, j)), pl.BlockSpec((2 * tf, D), lambda i, j: (j, 0))],
        out_specs=[row, gublk, pl.BlockSpec((tm, tf), lambda i, j: (i, j))],
        out_shape=[jax.ShapeDtypeStruct((T, D), F32), jax.ShapeDtypeStruct((T, 2 * F), BF16), jax.ShapeDtypeStruct((T, F), BF16)],
        scratch_shapes=[pltpu.VMEM((tm, D), F32)],
        compiler_params=_cp(("parallel", "arbitrary")), name=name)(dz, GU, w_out_t, w_in_rt)


def proj_res_ln(parts, w, res, g, b, name):
    T, D = res.shape
    tm = _pick(T, 512, 8)
    widths = [p.shape[1] for p in parts]
    offs = [int(sum(widths[:i])) for i in range(len(parts))]
    n = len(parts)

    def body(*refs):
        p_refs = refs[:n]
        w_ref, r_ref, g_ref, b_ref, out_ref, xh_ref, rs_ref = refs[n:]
        acc = ALPHA * r_ref[...]
        for p_ref, o, wd in zip(p_refs, offs, widths):
            acc = acc + _bdot(p_ref[...], w_ref[o:o + wd, :])
        out, xh, rs = _ln_apply(acc, g_ref[...], b_ref[...])
        out_ref[...] = out
        xh_ref[...] = xh
        rs_ref[...] = rs

    row = pl.BlockSpec((tm, D), lambda i: (i, 0))
    vec = pl.BlockSpec((1, D), lambda i: (0, 0))
    return pl.pallas_call(
        body, grid=(T // tm,),
        in_specs=[pl.BlockSpec((tm, wd), lambda i: (i, 0)) for wd in widths]
        + [pl.BlockSpec(w.shape, lambda i: (0, 0)), row, vec, vec],
        out_specs=[row, row, pl.BlockSpec((tm, 1), lambda i: (i, 0))],
        out_shape=[jax.ShapeDtypeStruct((T, D), F32), jax.ShapeDtypeStruct((T, D), F32), jax.ShapeDtypeStruct((T, 1), F32)],
        compiler_params=_cp(("parallel",)), name=name)(*parts, w, res, g, b)


def ple_fwd(h, p, wg, wp, name):
    T, D = h.shape
    P = p.shape[1]
    tm, tn = _pick(T, 512, 8), _pick(D, 512, LANES)

    def body(h_ref, hn_ref, p_ref, wg_ref, wp_ref, out_ref, a_ref, e_ref):
        a = _bdot(h_ref[...], wg_ref[...])
        e = _bdot(p_ref[...], wp_ref[...])
        a_ref[...] = a
        e_ref[...] = e
        out_ref[...] = hn_ref[...] + _sigmoid(a) * e

    blk = pl.BlockSpec((tm, tn), lambda i, j: (i, j))
    sds = jax.ShapeDtypeStruct((T, D), F32)
    return pl.pallas_call(
        body, grid=(T // tm, D // tn),
        in_specs=[pl.BlockSpec((tm, D), lambda i, j: (i, 0)), blk, pl.BlockSpec((tm, P), lambda i, j: (i, 0)),
                  pl.BlockSpec((D, tn), lambda i, j: (0, j)), pl.BlockSpec((P, tn), lambda i, j: (0, j))],
        out_specs=[blk, blk, blk], out_shape=[sds, sds, sds],
        compiler_params=_cp(("parallel", "parallel")), name=name)(h, h, p, wg, wp)


def ple_bwd(dout, a, e, wg_t, name):
    T, D = dout.shape
    tm = _pick(T, 512, 16)

    def body(do_ref, a_ref, e_ref, wgt_ref, dh_ref, da_ref, de_ref):
        do = do_ref[...]
        s = _sigmoid(a_ref[...])
        da = (do * e_ref[...] * s * (1.0 - s)).astype(BF16)
        da_ref[...] = da
        de_ref[...] = (do * s).astype(BF16)
        dh_ref[...] = do + jnp.dot(da, wgt_ref[...], preferred_element_type=F32)

    row = pl.BlockSpec((tm, D), lambda i: (i, 0))
    return pl.pallas_call(
        body, grid=(T // tm,),
        in_specs=[row, row, row, pl.BlockSpec((D, D), lambda i: (0, 0))],
        out_specs=[row, row, row],
        out_shape=[jax.ShapeDtypeStruct((T, D), F32), jax.ShapeDtypeStruct((T, D), BF16), jax.ShapeDtypeStruct((T, D), BF16)],
        compiler_params=_cp(("parallel",)), name=name)(dout, a, e, wg_t)


def loss_head(y, target, name):
    T, D = y.shape
    tm = _pick(T, 512, 8)

    def body(y_ref, t_ref, loss_ref, dy_ref):
        i = pl.program_id(0)

        @pl.when(i == 0)
        def _():
            loss_ref[...] = jnp.zeros_like(loss_ref)

        err = y_ref[...] - t_ref[...]
        dy_ref[...] = err * (1.0 / D)
        per_tok = jnp.sum(err * err, axis=-1, keepdims=True) * (1.0 / D)
        loss_ref[...] += 0.5 * jnp.sum(per_tok, axis=0, keepdims=True)

    row = pl.BlockSpec((tm, D), lambda i: (i, 0))
    return pl.pallas_call(
        body, grid=(T // tm,), in_specs=[row, row],
        out_specs=[pl.BlockSpec((1, 1), lambda i: (0, 0)), row],
        out_shape=[jax.ShapeDtypeStruct((1, 1), F32), jax.ShapeDtypeStruct((T, D), F32)],
        compiler_params=_cp(("arbitrary",)), name=name)(y, target)


GDN_QKV_BLOCKS = 3 * GDN_HEADS
HALO = 8


def _conv_taps(pad_ref, w_ref, tm, base):
    acc = w_ref[0:1, :] * pad_ref[pl.ds(base, tm), :]
    for k in range(1, GDN_CONV):
        acc = acc + w_ref[k:k + 1, :] * pad_ref[pl.ds(base + k, tm), :]
    return acc


def _gdn_pre_common(x_ref, halo_ref, w_ref, pad_ref, tm):
    i = pl.program_id(1)
    hb = pl.program_id(0)
    pad_ref[0:HALO, :] = jnp.where(i == 0, 0.0, halo_ref[...])
    pad_ref[HALO:HALO + tm, :] = x_ref[...]
    c = _conv_taps(pad_ref, w_ref, tm, HALO - (GDN_CONV - 1))
    s = _sigmoid(c)
    y = c * s
    r = lax.rsqrt(jnp.sum(y * y, axis=-1, keepdims=True) + RMS_EPS)
    scale = jnp.where(hb < GDN_HEADS, GDN_D ** -0.5, 1.0)
    return hb, c, s, y, r, scale


def gdn_pre_fwd(proj, conv_w_p, name):
    T = proj.shape[0]
    tm = _pick(T, 512, 8)

    def body(x_ref, halo_ref, w_ref, o_ref, pad_ref):
        hb, c, s, y, r, scale = _gdn_pre_common(x_ref, halo_ref, w_ref, pad_ref, tm)
        o_ref[...] = jnp.where(hb < 2 * GDN_HEADS, y * r * scale, y)

    return pl.pallas_call(
        body, grid=(GDN_QKV_BLOCKS, T // tm),
        in_specs=[pl.BlockSpec((tm, LANES), lambda hb, i: (i, hb)),
                  pl.BlockSpec((HALO, LANES), lambda hb, i: (jnp.maximum(i * (tm // HALO) - 1, 0), hb)),
                  pl.BlockSpec((GDN_CONV, LANES), lambda hb, i: (0, hb))],
        out_specs=pl.BlockSpec((tm, LANES), lambda hb, i: (i, hb)),
        out_shape=jax.ShapeDtypeStruct((T, GDN_QKV_BLOCKS * LANES), F32),
        scratch_shapes=[pltpu.VMEM((tm + HALO, LANES), F32)],
        compiler_params=_cp(("parallel", "parallel")), name=name)(proj, proj, conv_w_p)


def gdn_pre_bwd_pointwise(proj, conv_w_p, dqkv, name):
    T = proj.shape[0]
    tm = _pick(T, 512, 8)

    def body(x_ref, halo_ref, w_ref, d_ref, dc_ref, dw_ref, pad_ref):
        i = pl.program_id(1)
        hb, c, s, y, r, scale = _gdn_pre_common(x_ref, halo_ref, w_ref, pad_ref, tm)

        @pl.when(i == 0)
        def _():
            dw_ref[...] = jnp.zeros_like(dw_ref)

        d = d_ref[...]
        n = y * r
        dn = d * scale
        dy = jnp.where(hb < 2 * GDN_HEADS, r * (dn - n * jnp.sum(dn * n, axis=-1, keepdims=True)), d)
        dc = dy * (s * (1.0 + c * (1.0 - s)))
        dc_ref[...] = dc
        for k in range(GDN_CONV):
            xs = pad_ref[pl.ds(HALO - (GDN_CONV - 1) + k, tm), :]
            dw_ref[k:k + 1, :] += jnp.sum(dc * xs, axis=0, keepdims=True)

    blk = pl.BlockSpec((tm, LANES), lambda hb, i: (i, hb))
    wblk = pl.BlockSpec((GDN_CONV, LANES), lambda hb, i: (0, hb))
    return pl.pallas_call(
        body, grid=(GDN_QKV_BLOCKS, T // tm),
        in_specs=[blk, pl.BlockSpec((HALO, LANES), lambda hb, i: (jnp.maximum(i * (tm // HALO) - 1, 0), hb)), wblk, blk],
        out_specs=[blk, wblk],
        out_shape=[jax.ShapeDtypeStruct((T, GDN_QKV_BLOCKS * LANES), F32),
                   jax.ShapeDtypeStruct((GDN_CONV, GDN_QKV_BLOCKS * LANES), F32)],
        scratch_shapes=[pltpu.VMEM((tm + HALO, LANES), F32)],
        compiler_params=_cp(("parallel", "arbitrary")), name=name)(proj, proj, conv_w_p, dqkv)


def gdn_pre_bwd_conv(dc, conv_w_p, name):
    T = dc.shape[0]
    tm = _pick(T, 512, 8)
    nt = T // tm

    def body(dc_ref, halo_ref, w_ref, dx_ref, pad_ref):
        i = pl.program_id(1)
        pad_ref[0:tm, :] = dc_ref[...]
        pad_ref[tm:tm + HALO, :] = jnp.where(i == nt - 1, 0.0, halo_ref[...])
        acc = w_ref[GDN_CONV - 1:GDN_CONV, :] * pad_ref[pl.ds(0, tm), :]
        for k in range(GDN_CONV - 1):
            acc = acc + w_ref[k:k + 1, :] * pad_ref[pl.ds(GDN_CONV - 1 - k, tm), :]
        dx_ref[...] = acc

    blk = pl.BlockSpec((tm, LANES), lambda hb, i: (i, hb))
    return pl.pallas_call(
        body, grid=(GDN_QKV_BLOCKS, nt),
        in_specs=[blk, pl.BlockSpec((HALO, LANES), lambda hb, i: (jnp.minimum((i + 1) * (tm // HALO), T // HALO - 1), hb)),
                  pl.BlockSpec((GDN_CONV, LANES), lambda hb, i: (0, hb))],
        out_specs=blk,
        out_shape=jax.ShapeDtypeStruct((T, GDN_QKV_BLOCKS * LANES), F32),
        scratch_shapes=[pltpu.VMEM((tm + HALO, LANES), F32)],
        compiler_params=_cp(("parallel", "parallel")), name=name)(dc, dc, conv_w_p)


def _chunk_masks(C):
    row = _iota2((C, C), 0)
    col = _iota2((C, C), 1)
    return row >= col, row > col, row == col


def _col_to_row(colv, eye):
    return jnp.sum(jnp.where(eye, colv, 0.0), axis=0, keepdims=True)


def _row_to_col(rowv, eye):
    return jnp.sum(jnp.where(eye, rowv, 0.0), axis=1, keepdims=True)


def _unit_lower_inverse(A, eye):
    C = A.shape[0]
    P = jnp.where(eye, 1.0, 0.0) - A
    Bp = _hdot(A, A)
    for _ in range(4):
        R = _hdot(jnp.concatenate([Bp, P], axis=0), Bp)
        Bp = R[:C]
        P = P + R[C:]
    return P + _hdot(P, Bp)


def _gdn_gates(gab, a_row, dt_row, incl):
    g_all = -jnp.exp(a_row) * _softplus(gab + dt_row)
    beta_all = _sigmoid(gab)
    gc_all = _ones_dot_left(incl.astype(BF16), g_all)
    return g_all, beta_all, gc_all


def _gdn_head_common(qkv_ref, h, gc_all, beta_all, incl, strict, eye):
    C = GDN_CHUNK
    hq = pl.ds(h * LANES, GDN_D)
    hk = pl.ds((GDN_HEADS + h) * LANES, GDN_D)
    hv = pl.ds((2 * GDN_HEADS + h) * LANES, GDN_D)
    q, k, v = qkv_ref[:, hq], qkv_ref[:, hk], qkv_ref[:, hv]
    gc = gc_all[:, h:h + 1]
    beta = beta_all[:, GDN_HEADS + h:GDN_HEADS + h + 1]
    gc_row = _col_to_row(gc, eye)
    decay = jnp.where(incl, jnp.exp(jnp.where(incl, gc - gc_row, 0.0)), 0.0)
    e_gc = jnp.exp(gc)
    gl = gc[C - 1:C, :]
    e_gl = jnp.exp(gl)
    ekd = jnp.exp(gl - gc)
    kb = k * beta
    A = jnp.where(strict, _bdot_nt(kb, k) * decay, 0.0)
    Pm = jnp.where(incl, _bdot_nt(q, k) * decay, 0.0)
    return (hq, hk, hv), q, k, v, gc, beta, decay, e_gc, e_gl, ekd, kb, A, Pm


def gdn_chunk_fwd(qkv, proj, a_row, dt_row, norm_w, name):
    T = qkv.shape[0]
    C, H, Dh = GDN_CHUNK, GDN_HEADS, GDN_D
    N = T // C

    def body(qkv_ref, gz_ref, gab_ref, a_ref, dt_ref, nw_ref, o_ref, opre_ref, Tm_ref, Sin_ref, S_ref):
        n = pl.program_id(0)

        @pl.when(n == 0)
        def _():
            S_ref[...] = jnp.zeros_like(S_ref)

        incl, strict, eye = _chunk_masks(C)
        _, beta_all, gc_all = _gdn_gates(gab_ref[...], a_ref[...], dt_ref[...], incl)
        o_ref[...] = jnp.zeros_like(o_ref)
        opre_ref[...] = jnp.zeros_like(opre_ref)
        for h in range(H):
            (hq, _, _), q, k, v, gc, beta, decay, e_gc, e_gl, ekd, kb, A, Pm = _gdn_head_common(
                qkv_ref, h, gc_all, beta_all, incl, strict, eye)
            Tm = _unit_lower_inverse(A, eye)
            u = _hdot(Tm, v * beta)
            w = _hdot(Tm, kb * e_gc)
            S = S_ref[h]
            v_new = u - _bdot(w, S)
            o = _bdot(q * e_gc, S) + _bdot(Pm, v_new)
            S_ref[h] = S * e_gl + _bdot_tn(k * ekd, v_new)
            Sin_ref[0, h] = S
            Tm_ref[0, h] = Tm
            r = lax.rsqrt(jnp.mean(o * o, axis=-1, keepdims=True) + RMS_EPS)
            gz = gz_ref[:, hq]
            opre_ref[:, hq] = o
            o_ref[:, hq] = o * r * nw_ref[...] * (gz * _sigmoid(gz))

    vec = pl.BlockSpec((1, LANES), lambda n: (0, 0))
    hblk = pl.BlockSpec((C, H * LANES), lambda n: (n, 0))
    sblk = pl.BlockSpec((1, H, Dh, Dh), lambda n: (n, 0, 0, 0))
    return pl.pallas_call(
        body, grid=(N,),
        in_specs=[pl.BlockSpec((C, GDN_QKV_BLOCKS * LANES), lambda n: (n, 0)),
                  pl.BlockSpec((C, H * LANES), lambda n: (n, CB_GZ // H)),
                  pl.BlockSpec((C, LANES), lambda n: (n, CB_GAB)), vec, vec, pl.BlockSpec((1, Dh), lambda n: (0, 0))],
        out_specs=[hblk, hblk, sblk, sblk],
        out_shape=[jax.ShapeDtypeStruct((T, H * LANES), F32), jax.ShapeDtypeStruct((T, H * LANES), F32),
                   jax.ShapeDtypeStruct((N, H, Dh, Dh), F32), jax.ShapeDtypeStruct((N, H, Dh, Dh), F32)],
        scratch_shapes=[pltpu.VMEM((H, Dh, Dh), F32)],
        compiler_params=_cp(("arbitrary",)), name=name)(qkv, proj, proj, a_row, dt_row, norm_w)


def gdn_chunk_bwd(qkv, proj, a_row, dt_row, norm_w, opre, Tm_all, Sin_all, docat, name):
    T = qkv.shape[0]
    C, H, Dh = GDN_CHUNK, GDN_HEADS, GDN_D
    N = T // C

    def body(qkv_ref, gz_ref, gab_ref, a_ref, dt_ref, nw_ref, opre_ref, Tm_ref, Sin_ref, do_ref,
             dqkv_ref, dgz_ref, dgab_ref, da_ref, ddt_ref, dnw_ref, dS_ref):
        n = pl.program_id(0)

        @pl.when(n == 0)
        def _():
            dS_ref[...] = jnp.zeros_like(dS_ref)
            da_ref[...] = jnp.zeros_like(da_ref)
            ddt_ref[...] = jnp.zeros_like(ddt_ref)
            dnw_ref[...] = jnp.zeros_like(dnw_ref)

        incl, strict, eye = _chunk_masks(C)
        gab = gab_ref[...]
        g_all, beta_all, gc_all = _gdn_gates(gab, a_ref[...], dt_ref[...], incl)
        lane = _iota2((C, LANES), 1)
        rowi = _iota2((C, 1), 0)
        dqkv_ref[...] = jnp.zeros_like(dqkv_ref)
        dgz_ref[...] = jnp.zeros_like(dgz_ref)
        dgc_all = jnp.zeros((C, LANES), F32)
        dbeta_all = jnp.zeros((C, LANES), F32)
        nw = nw_ref[...]
        dnw = jnp.zeros_like(nw)
        for h in range(H):
            (hq, hk, hv), q, k, v, gc, beta, decay, e_gc, e_gl, ekd, kb, A, Pm = _gdn_head_common(
                qkv_ref, h, gc_all, beta_all, incl, strict, eye)
            Tm = Tm_ref[0, h]
            S = Sin_ref[0, h]
            dS = dS_ref[h]
            kbe = kb * e_gc
            u = _hdot(Tm, v * beta)
            w = _hdot(Tm, kbe)
            qd = q * e_gc
            kd = k * ekd
            v_new = u - _bdot(w, S)
            o = opre_ref[:, hq]
            gz = gz_ref[:, hq]
            don = do_ref[:, hq]
            r = lax.rsqrt(jnp.mean(o * o, axis=-1, keepdims=True) + RMS_EPS)
            nn = o * r
            sgz = _sigmoid(gz)
            silu = gz * sgz
            dgz_ref[:, hq] = don * nn * nw * (sgz * (1.0 + gz * (1.0 - sgz)))
            dnn = don * nw * silu
            dnw = dnw + jnp.sum(don * nn * silu, axis=0, keepdims=True)
            do = r * (dnn - nn * jnp.mean(dnn * nn, axis=-1, keepdims=True))
            dv_new = _bdot_tn(Pm, do) + _bdot(kd, dS)
            dPm = jnp.where(incl, _bdot_nt(do, v_new), 0.0)
            dqd = _bdot_nt(do, S)
            dkd = _bdot_nt(v_new, dS)
            dS_ref[h] = _bdot_tn(qd, do) + e_gl * dS - _bdot_tn(w, dv_new)
            dgl = jnp.sum(jnp.sum(dS * S, axis=1, keepdims=True), axis=0, keepdims=True) * e_gl
            dw = -_bdot_nt(dv_new, S)
            dvb = _hdot_tn(Tm, dv_new)
            dkbe = _hdot_tn(Tm, dw)
            dA = -jnp.where(strict, _bdot_nt(dvb, u) + _bdot_nt(dkbe, w), 0.0)
            dAD = dA * decay
            dPD = dPm * decay
            Gm = dA * A + dPm * Pm
            dgc = jnp.sum(Gm, axis=1, keepdims=True) - _row_to_col(jnp.sum(Gm, axis=0, keepdims=True), eye)
            dkb = _bdot(dAD, k) + dkbe * e_gc
            dk = _bdot_tn(dAD, kb) + _bdot_tn(dPD, q) + dkd * ekd + dkb * beta
            dq = _bdot(dPD, k) + dqd * e_gc
            tkd = jnp.sum(dkd * kd, axis=-1, keepdims=True)
            dgc = dgc + jnp.sum(dqd * qd, axis=-1, keepdims=True) - tkd + jnp.sum(dkbe * kbe, axis=-1, keepdims=True)
            dgl = dgl + jnp.sum(tkd, axis=0, keepdims=True)
            dgc = dgc + jnp.where(rowi == C - 1, dgl, 0.0)
            dbeta = jnp.sum(dvb * v, axis=-1, keepdims=True) + jnp.sum(dkb * k, axis=-1, keepdims=True)
            dqkv_ref[:, hq] = dq
            dqkv_ref[:, hk] = dk
            dqkv_ref[:, hv] = dvb * beta
            dgc_all = dgc_all + jnp.where(lane == h, dgc, 0.0)
            dbeta_all = dbeta_all + jnp.where(lane == H + h, dbeta, 0.0)
        dnw_ref[...] += dnw
        upper = (_iota2((C, C), 0) <= _iota2((C, C), 1)).astype(BF16)
        dg_all = _ones_dot_left(upper, dgc_all)
        dga = dg_all * (-jnp.exp(a_ref[...])) * _sigmoid(gab + dt_ref[...])
        dgb = dbeta_all * beta_all * (1.0 - beta_all)
        dgab_ref[...] = jnp.where(lane < H, dga, jnp.where(lane < 2 * H, dgb, 0.0))
        da_ref[...] += jnp.sum(jnp.where(lane < H, dg_all * g_all, 0.0), axis=0, keepdims=True)
        ddt_ref[...] += jnp.sum(jnp.where(lane < H, dga, 0.0), axis=0, keepdims=True)

    rev = lambda n: N - 1 - n
    vec = pl.BlockSpec((1, LANES), lambda n: (0, 0))
    nwv = pl.BlockSpec((1, Dh), lambda n: (0, 0))
    hblk = pl.BlockSpec((C, H * LANES), lambda n: (rev(n), 0))
    sblk = pl.BlockSpec((1, H, Dh, Dh), lambda n: (rev(n), 0, 0, 0))
    qblk = pl.BlockSpec((C, GDN_QKV_BLOCKS * LANES), lambda n: (rev(n), 0))
    return pl.pallas_call(
        body, grid=(N,),
        in_specs=[qblk, pl.BlockSpec((C, H * LANES), lambda n: (rev(n), CB_GZ // H)),
                  pl.BlockSpec((C, LANES), lambda n: (rev(n), CB_GAB)), vec, vec, nwv, hblk, sblk, sblk, hblk],
        out_specs=[qblk, hblk, pl.BlockSpec((C, LANES), lambda n: (rev(n), 0)), vec, vec, nwv],
        out_shape=[jax.ShapeDtypeStruct((T, GDN_QKV_BLOCKS * LANES), F32), jax.ShapeDtypeStruct((T, H * LANES), F32),
                   jax.ShapeDtypeStruct((T, LANES), F32), jax.ShapeDtypeStruct((1, LANES), F32),
                   jax.ShapeDtypeStruct((1, LANES), F32), jax.ShapeDtypeStruct((1, Dh), F32)],
        scratch_shapes=[pltpu.VMEM((H, Dh, Dh), F32)],
        compiler_params=_cp(("arbitrary",)), name=name)(qkv, proj, proj, a_row, dt_row, norm_w, opre, Tm_all, Sin_all, docat)


ATT_BLOCK = 128
NEG_BIG = -1e30


def _att_specs(T, bq, cbs):
    qspec = lambda cb: pl.BlockSpec((bq, LANES), lambda h, i: (i, cb + h))
    kspec = lambda cb: pl.BlockSpec((T, LANES), lambda h, i: (0, cb + h))
    return qspec, kspec


def _kblock(ref, kb, bk):
    return ref[pl.ds(pl.multiple_of(kb * bk, bk), bk), :]


def _att_pos(i, kb, bq, bk):
    qpos = i * bq + _iota2((bq, bk), 0)
    kpos = kb * bk + _iota2((bq, bk), 1)
    return qpos, kpos


def sb_fwd(proj, name):
    T = proj.shape[0]
    H = SB_HEADS
    bq = bk = min(ATT_BLOCK, T)
    scale = SB_DIM ** -0.5

    def body(q_ref, k_ref, v_ref, o_ref, tot_ref):
        i = pl.program_id(1)
        qb = q_ref[...].astype(BF16)
        later = (_iota2((bk, bk), 0) > _iota2((bk, bk), 1)).astype(BF16)

        def step(j, carry):
            acc, R = carry
            kb = i - j
            z = _bdot_nt(qb, _kblock(k_ref, kb, bk)) * scale
            qpos, kpos = _att_pos(i, kb, bq, bk)
            mask = kpos < qpos
            sp = _softplus(z)
            l1m = jnp.where(mask, -sp, 0.0)
            lw = (z - sp) + _ones_dot(l1m, later) + R
            W = jnp.where(mask, jnp.exp(lw), 0.0)
            acc = acc + _bdot(W, _kblock(v_ref, kb, bk))
            return acc, R + jnp.sum(l1m, axis=-1, keepdims=True)

        acc, R = lax.fori_loop(0, i + 1, step, (jnp.zeros((bq, LANES), F32), jnp.zeros((bq, 1), F32)))
        o_ref[...] = acc
        tot_ref[...] = jnp.broadcast_to(R, (bq, LANES))

    qspec, kspec = _att_specs(T, bq, None)
    sds = jax.ShapeDtypeStruct((T, H * LANES), F32)
    oblk = pl.BlockSpec((bq, LANES), lambda h, i: (i, h))
    return pl.pallas_call(
        body, grid=(H, T // bq), in_specs=[qspec(CB_SQ), kspec(CB_SK), kspec(CB_SV)],
        out_specs=[oblk, oblk], out_shape=[sds, sds],
        compiler_params=_cp(("parallel", "parallel")), name=name)(proj, proj, proj)


def sb_bwd(proj, tot, docat, do_cb, name):
    T = proj.shape[0]
    H = SB_HEADS
    bq = bk = min(ATT_BLOCK, T)
    scale = SB_DIM ** -0.5

    def body(q_ref, k_ref, v_ref, tot_ref, do_ref, dq_ref, dk_ref, dv_ref):
        i = pl.program_id(1)

        @pl.when(i == 0)
        def _():
            dk_ref[...] = jnp.zeros_like(dk_ref)
            dv_ref[...] = jnp.zeros_like(dv_ref)

        qb = q_ref[...].astype(BF16)
        dob = do_ref[...].astype(BF16)
        total = tot_ref[:, 0:1]
        upto = (_iota2((bk, bk), 0) <= _iota2((bk, bk), 1)).astype(BF16)
        before = (_iota2((bk, bk), 0) < _iota2((bk, bk), 1)).astype(BF16)

        def step(kb, carry):
            dq, Ppre, Epre = carry
            kblk = _kblock(k_ref, kb, bk)
            z = _bdot_nt(qb, kblk) * scale
            qpos, kpos = _att_pos(i, kb, bq, bk)
            mask = kpos < qpos
            sp = _softplus(z)
            l1m = jnp.where(mask, -sp, 0.0)
            rest = total - (_ones_dot(l1m, upto) + Ppre)
            W = jnp.where(mask, jnp.exp((z - sp) + rest), 0.0)
            E = _bdot_nt(dob, _kblock(v_ref, kb, bk)) * W
            cexcl = _ones_dot(E, before) + Epre
            sg = _sigmoid(z)
            dz = (jnp.where(mask, E * (1.0 - sg) - cexcl * sg, 0.0) * scale).astype(BF16)
            rows = pl.ds(pl.multiple_of(kb * bk, bk), bk)
            dk_ref[rows, :] += lax.dot_general(dz, qb, TN_DIMS, preferred_element_type=F32)
            dv_ref[rows, :] += lax.dot_general(W.astype(BF16), dob, TN_DIMS, preferred_element_type=F32)
            dq = dq + jnp.dot(dz, kblk.astype(BF16), preferred_element_type=F32)
            return dq, Ppre + jnp.sum(l1m, axis=-1, keepdims=True), Epre + jnp.sum(E, axis=-1, keepdims=True)

        zc = jnp.zeros((bq, 1), F32)
        dq, _, _ = lax.fori_loop(0, i + 1, step, (jnp.zeros((bq, LANES), F32), zc, zc))
        dq_ref[...] = dq

    qspec, kspec = _att_specs(T, bq, None)
    sds = jax.ShapeDtypeStruct((T, H * LANES), F32)
    oblk = pl.BlockSpec((bq, LANES), lambda h, i: (i, h))
    kout = pl.BlockSpec((T, LANES), lambda h, i: (0, h))
    return pl.pallas_call(
        body, grid=(H, T // bq),
        in_specs=[qspec(CB_SQ), kspec(CB_SK), kspec(CB_SV), oblk, qspec(do_cb)],
        out_specs=[oblk, kout, kout], out_shape=[sds, sds, sds],
        compiler_params=_cp(("arbitrary", "arbitrary")), name=name)(proj, proj, proj, tot, docat)


def mla_fwd(Q, K, V, name):
    T = Q.shape[0]
    H = MLA_HEADS
    bq = bk = min(ATT_BLOCK, T)
    scale = (MLA_NOPE + MLA_ROPE) ** -0.5

    def body(q_ref, k_ref, v_ref, o_ref, lse_ref):
        i = pl.program_id(1)
        qb = q_ref[...]

        def step(kb, carry):
            acc, m, l = carry
            s = _bdot_nt(qb, _kblock(k_ref, kb, bk)) * scale
            qpos, kpos = _att_pos(i, kb, bq, bk)
            s = jnp.where(kpos <= qpos, s, NEG_BIG)
            m_new = jnp.maximum(m, jnp.max(s, axis=-1, keepdims=True))
            p = jnp.exp(s - m_new)
            corr = jnp.exp(m - m_new)
            acc = corr * acc + _bdot(p, _kblock(v_ref, kb, bk))
            return acc, m_new, corr * l + jnp.sum(p, axis=-1, keepdims=True)

        init = (jnp.zeros((bq, LANES), F32), jnp.full((bq, 1), NEG_BIG, F32), jnp.zeros((bq, 1), F32))
        acc, m, l = lax.fori_loop(0, i + 1, step, init)
        o_ref[...] = acc / l
        lse_ref[...] = jnp.broadcast_to(m + jnp.log(l), (bq, LANES))

    qspec, kspec = _att_specs(T, bq, None)
    sds = jax.ShapeDtypeStruct((T, H * LANES), F32)
    oblk = pl.BlockSpec((bq, LANES), lambda h, i: (i, h))
    return pl.pallas_call(
        body, grid=(H, T // bq), in_specs=[qspec(0), kspec(0), kspec(0)],
        out_specs=[oblk, oblk], out_shape=[sds, sds],
        compiler_params=_cp(("parallel", "parallel")), name=name)(Q, K, V)


def mla_bwd(Q, K, V, o, lse, docat, do_cb, name):
    T = Q.shape[0]
    H = MLA_HEADS
    bq = bk = min(ATT_BLOCK, T)
    scale = (MLA_NOPE + MLA_ROPE) ** -0.5

    def body(q_ref, k_ref, v_ref, o_ref, lse_ref, do_ref, dq_ref, dk_ref, dv_ref):
        i = pl.program_id(1)

        @pl.when(i == 0)
        def _():
            dk_ref[...] = jnp.zeros_like(dk_ref)
            dv_ref[...] = jnp.zeros_like(dv_ref)

        qb = q_ref[...]
        do = do_ref[...]
        dob = do.astype(BF16)
        delta = jnp.sum(do * o_ref[...], axis=-1, keepdims=True)
        lse = lse_ref[:, 0:1]

        def step(kb, dq):
            kblk = _kblock(k_ref, kb, bk)
            s = _bdot_nt(qb, kblk) * scale
            qpos, kpos = _att_pos(i, kb, bq, bk)
            p = jnp.where(kpos <= qpos, jnp.exp(s - lse), 0.0)
            dp = _bdot_nt(dob, _kblock(v_ref, kb, bk))
            ds = (p * (dp - delta) * scale).astype(BF16)
            rows = pl.ds(pl.multiple_of(kb * bk, bk), bk)
            dk_ref[rows, :] += lax.dot_general(ds, qb, TN_DIMS, preferred_element_type=F32)
            dv_ref[rows, :] += lax.dot_general(p.astype(BF16), dob, TN_DIMS, preferred_element_type=F32)
            return dq + jnp.dot(ds, kblk, preferred_element_type=F32)

        dq_ref[...] = lax.fori_loop(0, i + 1, step, jnp.zeros((bq, LANES), F32))

    qspec, kspec = _att_specs(T, bq, None)
    sds = jax.ShapeDtypeStruct((T, H * LANES), F32)
    oblk = pl.BlockSpec((bq, LANES), lambda h, i: (i, h))
    kout = pl.BlockSpec((T, LANES), lambda h, i: (0, h))
    return pl.pallas_call(
        body, grid=(H, T // bq),
        in_specs=[qspec(0), kspec(0), kspec(0), oblk, oblk, qspec(do_cb)],
        out_specs=[oblk, kout, kout], out_shape=[sds, sds, sds],
        compiler_params=_cp(("arbitrary", "arbitrary")), name=name)(Q, K, V, o, lse, docat)


def _tile_heads(t, n):
    return jnp.concatenate([t] * n, axis=1)


def _rope(X, C, Sn, Sp):
    n = X.shape[1]
    return X * C + pltpu.roll(X, n - HALF_ROPE, 1) * Sn + pltpu.roll(X, HALF_ROPE, 1) * Sp


def _rope_t(dO, C, Sn, Sp):
    n = dO.shape[1]
    return dO * C + pltpu.roll(dO * Sn, HALF_ROPE, 1) + pltpu.roll(dO * Sp, n - HALF_ROPE, 1)


def _rms(x, w):
    r = lax.rsqrt(jnp.mean(x * x, axis=-1, keepdims=True) + RMS_EPS)
    xh = x * r
    return r, xh, xh * w


def _rms_bwd(dn, w, r, xh):
    dxh = dn * w
    return r * (dxh - xh * jnp.mean(dxh * xh, axis=-1, keepdims=True)), jnp.sum(dn * xh, axis=0, keepdims=True)


def _mla_pre_specs(T, tm):
    KV = MLA_KV_RANK
    QR = MLA_Q_RANK
    W = MLA_HEADS * LANES
    full = lambda shape: pl.BlockSpec(shape, lambda i: (0, 0))
    specs = [pl.BlockSpec((tm, QR), lambda i: (i, CB_MQ * LANES // QR)),
             pl.BlockSpec((tm, 2 * LANES), lambda i: (i, CB_MKV // 2)),
             full((1, QR)), full((1, KV))]
    rope = [pl.BlockSpec((tm, LANES), lambda i: (i, 0))] * 3
    return specs, rope, full, W


def mla_pre_fwd(proj, wq, wkv, wuq, wuk, wuv, ropeC, ropeSn, ropeSp, name):
    T = proj.shape[0]
    tm = _pick(T, 512, 16)
    KV = MLA_KV_RANK
    H = MLA_HEADS

    def body(mq_ref, mkv_ref, wq_ref, wkv_ref, wuq_ref, wuk_ref, wuv_ref, c_ref, sn_ref, sp_ref, Q_ref, K_ref, V_ref):
        C, Sn, Sp = (_tile_heads(t[...], H) for t in (c_ref, sn_ref, sp_ref))
        _, _, qn = _rms(mq_ref[...], wq_ref[...])
        Q_ref[...] = _rope(_bdot(qn, wuq_ref[...]), C, Sn, Sp).astype(BF16)
        mkv = mkv_ref[...]
        _, _, kvn = _rms(mkv[:, :KV], wkv_ref[...])
        kr = pltpu.roll(mkv[:, KV:], MLA_NOPE, 1)
        K_ref[...] = _rope(_bdot(kvn, wuk_ref[...]) + _tile_heads(kr, H), C, Sn, Sp).astype(BF16)
        V_ref[...] = _bdot(kvn, wuv_ref[...]).astype(BF16)

    specs, rope, full, W = _mla_pre_specs(T, tm)
    oblk = pl.BlockSpec((tm, W), lambda i: (i, 0))
    sds = jax.ShapeDtypeStruct((T, W), BF16)
    return pl.pallas_call(
        body, grid=(T // tm,),
        in_specs=specs + [full(wuq.shape), full(wuk.shape), full(wuv.shape)] + rope,
        out_specs=[oblk, oblk, oblk], out_shape=[sds, sds, sds],
        compiler_params=_cp(("parallel",)), name=name)(proj, proj, wq, wkv, wuq, wuk, wuv, ropeC, ropeSn, ropeSp)


def mla_pre_bwd(proj, wq, wkv, wuq, wuk, wuv, wuq_t, wuk_t, wuv_t, ropeC, ropeSn, ropeSp, dQ, dK, dV, name):
    T = proj.shape[0]
    tm = _pick(T, 512, 16)
    KV = MLA_KV_RANK
    H = MLA_HEADS

    def body(mq_ref, mkv_ref, wq_ref, wkv_ref, wuq_ref, wuk_ref, wuv_ref, wuqt_ref, wukt_ref, wuvt_ref,
             c_ref, sn_ref, sp_ref, dQ_ref, dK_ref, dV_ref,
             dmq_ref, dmkv_ref, dwuq_ref, dwuk_ref, dwuv_ref, dwq_ref, dwkv_ref):
        i = pl.program_id(0)

        @pl.when(i == 0)
        def _():
            for ref in (dwuq_ref, dwuk_ref, dwuv_ref, dwq_ref, dwkv_ref):
                ref[...] = jnp.zeros_like(ref)

        C, Sn, Sp = (_tile_heads(t[...], H) for t in (c_ref, sn_ref, sp_ref))
        rq, xq, qn = _rms(mq_ref[...], wq_ref[...])
        mkv = mkv_ref[...]
        rkv, xkv, kvn = _rms(mkv[:, :KV], wkv_ref[...])
        dqf = _rope_t(dQ_ref[...], C, Sn, Sp)
        dkf = _rope_t(dK_ref[...], C, Sn, Sp)
        dv = dV_ref[...]
        dwuq_ref[...] += _bdot_tn(qn, dqf)
        dwuk_ref[...] += _bdot_tn(kvn, dkf)
        dwuv_ref[...] += _bdot_tn(kvn, dv)
        dmq, dwq = _rms_bwd(_bdot(dqf, wuqt_ref[...]), wq_ref[...], rq, xq)
        dckv, dwkv = _rms_bwd(_bdot(dkf, wukt_ref[...]) + _bdot(dv, wuvt_ref[...]), wkv_ref[...], rkv, xkv)
        dwq_ref[...] += dwq
        dwkv_ref[...] += dwkv
        dmq_ref[...] = dmq
        dkr = dkf[:, 0:LANES]
        for h in range(1, H):
            dkr = dkr + dkf[:, h * LANES:(h + 1) * LANES]
        dkr = pltpu.roll(dkr, LANES - MLA_NOPE, 1)
        dkr = jnp.where(_iota2(dkr.shape, 1) < MLA_ROPE, dkr, 0.0)
        dmkv_ref[...] = jnp.concatenate([dckv, dkr], axis=1)

    specs, rope, full, W = _mla_pre_specs(T, tm)
    wide = pl.BlockSpec((tm, W), lambda i: (i, 0))
    return pl.pallas_call(
        body, grid=(T // tm,),
        in_specs=specs + [full(w.shape) for w in (wuq, wuk, wuv, wuq_t, wuk_t, wuv_t)] + rope + [wide, wide, wide],
        out_specs=[pl.BlockSpec((tm, MLA_Q_RANK), lambda i: (i, 0)), pl.BlockSpec((tm, 2 * LANES), lambda i: (i, 0)),
                   full(wuq.shape), full(wuk.shape), full(wuv.shape), full((1, MLA_Q_RANK)), full((1, KV))],
        out_shape=[jax.ShapeDtypeStruct((T, MLA_Q_RANK), F32), jax.ShapeDtypeStruct((T, 2 * LANES), F32),
                   jax.ShapeDtypeStruct(wuq.shape, F32), jax.ShapeDtypeStruct(wuk.shape, F32),
                   jax.ShapeDtypeStruct(wuv.shape, F32), jax.ShapeDtypeStruct((1, MLA_Q_RANK), F32),
                   jax.ShapeDtypeStruct((1, KV), F32)],
        compiler_params=_cp(("arbitrary",)), name=name)(
            proj, proj, wq, wkv, wuq, wuk, wuv, wuq_t, wuk_t, wuv_t, ropeC, ropeSn, ropeSp, dQ, dK, dV)


MESH = pl.DeviceIdType.MESH
ANY = pl.BlockSpec(memory_space=pl.ANY)


def _place():
    return lax.axis_index("x"), lax.axis_index("y"), lax.axis_index("c")


def all_gather(shard, name):
    R, C = shard.shape

    def body(x_ref, out_ref, send_sems, recv_sems, local_sem):
        x, y, c = _place()
        me, sibling = (x, y, c), (x, y, 1 - c)
        chips = [(1 - x, y), (x, 1 - y), (1 - x, 1 - y)]

        def slot(px, py, pc):
            return out_ref.at[4 * px + 2 * py + pc]

        def copy(k, block, to, src=None):
            return pltpu.make_async_remote_copy(
                src_ref=slot(*block) if src is None else src, dst_ref=slot(*block),
                send_sem=send_sems.at[k], recv_sem=recv_sems.at[k], device_id=to, device_id_type=MESH)

        mine = pltpu.make_async_copy(x_ref, slot(*me), local_sem)
        mine.start()
        first = [copy(0, me, sibling, src=x_ref)]
        first += [copy(1 + j, me, (*chip, c), src=x_ref) for j, chip in enumerate(chips)]
        for cp in first:
            cp.start()
        passed = [copy(4 + j, (*chip, c), sibling) for j, chip in enumerate(chips)]
        for j, chip in enumerate(chips):
            copy(1 + j, (*chip, c), me).wait_recv()
            passed[j].start()
        copy(0, sibling, me).wait_recv()
        for j, chip in enumerate(chips):
            copy(4 + j, (*chip, 1 - c), me).wait_recv()
        for cp in first + passed:
            cp.wait_send()
        mine.wait()

    return pl.pallas_call(
        body, out_shape=jax.ShapeDtypeStruct((N_DEV, R, C), shard.dtype),
        in_specs=[ANY], out_specs=ANY,
        scratch_shapes=[pltpu.SemaphoreType.DMA((7,)), pltpu.SemaphoreType.DMA((7,)), pltpu.SemaphoreType.DMA],
        name=name)(shard)


def exchange_partials(parts, name):
    _, R, C = parts.shape

    def body(src_ref, dst_ref, send_sems, recv_sems, local_sem):
        x, y, c = _place()
        me = 4 * x + 2 * y + c
        copies = []
        for k in range(1, N_DEV):
            px = 1 - x if k & 4 else x
            py = 1 - y if k & 2 else y
            pc = 1 - c if k & 1 else c
            copies.append(pltpu.make_async_remote_copy(
                src_ref=src_ref.at[4 * px + 2 * py + pc], dst_ref=dst_ref.at[me],
                send_sem=send_sems.at[k - 1], recv_sem=recv_sems.at[k - 1],
                device_id=(px, py, pc), device_id_type=MESH))
        mine = pltpu.make_async_copy(src_ref.at[me], dst_ref.at[me], local_sem)
        mine.start()
        for cp in copies:
            cp.start()
        for cp in copies:
            cp.wait_recv()
        for cp in copies:
            cp.wait_send()
        mine.wait()

    return pl.pallas_call(
        body, out_shape=jax.ShapeDtypeStruct(parts.shape, parts.dtype),
        in_specs=[ANY], out_specs=ANY,
        scratch_shapes=[pltpu.SemaphoreType.DMA((7,)), pltpu.SemaphoreType.DMA((7,)), pltpu.SemaphoreType.DMA],
        name=name)(parts)


def sum_slots(parts, name):
    n, R, C = parts.shape
    tr = _pick(R, 512, 16)

    def body(p_ref, o_ref):
        acc = p_ref[0].astype(F32)
        for s in range(1, n):
            acc = acc + p_ref[s].astype(F32)
        o_ref[...] = acc

    return pl.pallas_call(
        body, grid=(R // tr,),
        in_specs=[pl.BlockSpec((n, tr, C), lambda i: (0, i, 0))],
        out_specs=pl.BlockSpec((tr, C), lambda i: (i, 0)),
        out_shape=jax.ShapeDtypeStruct((R, C), F32),
        compiler_params=_cp(("parallel",)), name=name)(parts)


def adamw(w, g, m, v, name):
    R, C = w.shape
    tr = _pick(R, 256, 8) if R * C > 512 * 1024 else R

    def body(w_ref, g_ref, m_ref, v_ref, d_ref, nm_ref, nv_ref):
        g_ = g_ref[...]
        m_ = ADAM_B1 * m_ref[...] + (1.0 - ADAM_B1) * g_
        v_ = ADAM_B2 * v_ref[...] + (1.0 - ADAM_B2) * (g_ * g_)
        m_hat = m_ / (1.0 - ADAM_B1 ** ADAM_STEP)
        v_hat = v_ / (1.0 - ADAM_B2 ** ADAM_STEP)
        d_ref[...] = -ADAM_LR * (m_hat / (jnp.sqrt(v_hat) + ADAM_EPS) + ADAM_WD * w_ref[...])
        nm_ref[...] = m_
        nv_ref[...] = v_

    blk = pl.BlockSpec((tr, C), lambda i: (i, 0))
    sds = jax.ShapeDtypeStruct((R, C), F32)
    return pl.pallas_call(
        body, grid=(R // tr,), in_specs=[blk] * 4, out_specs=[blk] * 3, out_shape=[sds] * 3,
        compiler_params=_cp(("parallel",)), name=name)(w, g, m, v)


PACK_COLS = 1024
BIG = (("ffa_w_in", 2), ("ffa_w_out", 1), ("mix_w_in", 2), ("mla_w_uq", 2), ("mla_w_ukv", 2), ("mix_w_o", 1),
       ("ffb_w_in", 2), ("ffb_w_out", 1), ("ple_w_gate", 1), ("ple_w_proj", 2))
SMALL_SHARDED = (("gdn_conv_w", 2), ("ln_g", 2), ("ln_b", 2))
REPLICATED = ("gdn_a_log", "gdn_dt_bias", "gdn_norm_w", "mla_q_norm_w", "mla_kv_norm_w")
WEIGHTS = ("ffa_w_in", "ffa_w_out", "mix_w_in", "gdn_conv_w", "gdn_a_log", "gdn_dt_bias", "gdn_norm_w", "mla_q_norm_w",
           "mla_kv_norm_w", "mla_w_uq", "mla_w_ukv", "mix_w_o", "ffb_w_in", "ffb_w_out", "ln_g", "ln_b", "ple_w_gate",
           "ple_w_proj")


def _pack_rows(flat, rows_mult):
    n = flat.shape[-1]
    unit = rows_mult * PACK_COLS
    total = -(-n // unit) * unit
    flat = jnp.pad(flat, [(0, 0)] * (flat.ndim - 1) + [(0, total - n)])
    return flat.reshape(flat.shape[:-1] + (total // PACK_COLS, PACK_COLS))


def _to_slots(full, axis):
    L, a, b = full.shape
    if axis == 2:
        t = full.reshape(L, a, N_DEV, b // N_DEV).transpose(2, 0, 1, 3)
    else:
        t = full.reshape(L, N_DEV, a // N_DEV, b).transpose(1, 0, 2, 3)
    return t.reshape(N_DEV, -1)


def _from_slots(slots, shard_shape, axis):
    L, a, b = shard_shape
    t = slots.reshape((N_DEV,) + tuple(shard_shape))
    if axis == 2:
        return t.transpose(1, 2, 0, 3).reshape(L, a, N_DEV * b)
    return t.transpose(1, 0, 2, 3).reshape(L, N_DEV * a, b)


def _gather_group(shards, spec, dtype, rows_mult, name):
    flat = jnp.concatenate([shards[n].astype(dtype).reshape(-1) for n, _ in spec])
    got = all_gather(_pack_rows(flat, rows_mult), name).reshape(N_DEV, -1)
    out, off = {}, 0
    for n, axis in spec:
        size = int(np.prod(shards[n].shape))
        out[n] = _from_slots(got[:, off:off + size], shards[n].shape, axis)
        off += size
    return out


def _pad_heads(w, nh):
    K = w.shape[0]
    return jnp.pad(w.reshape(K, nh, GDN_D), ((0, 0), (0, 0), (0, LANES - GDN_D))).reshape(K, nh * LANES)


def _unpad_heads(w, nh):
    K = w.shape[0]
    return w.reshape(K, nh, LANES)[:, :, :GDN_D].reshape(K, nh * GDN_D)


IN_WIDTHS = (512, 512, 512, 512, 8, 8, 256, 256, 256, 256, 160)


def _split_in(w):
    offs = np.cumsum((0,) + IN_WIDTHS)
    return [w[:, int(offs[i]):int(offs[i + 1])] for i in range(len(IN_WIDTHS))]


def _pad_in_proj(w):
    gq, gk, gv, gz, ga, gb, sq, sk, sv, mq, mkv = _split_in(w)
    K = w.shape[0]
    gab = jnp.pad(jnp.concatenate([ga, gb], axis=1), ((0, 0), (0, LANES - 2 * GDN_HEADS)))
    return jnp.concatenate(
        [_pad_heads(t, GDN_HEADS) for t in (gq, gk, gv, gz)] + [_pad_heads(t, SB_HEADS) for t in (sq, sk, sv)]
        + [mq, jnp.pad(mkv, ((0, 0), (0, 2 * LANES - mkv.shape[1]))), gab], axis=1)


def _unpad_in_proj(wp):
    c = lambda cb, n: wp[:, cb * LANES:(cb + n) * LANES]
    gab = c(CB_GAB, 1)
    parts = [_unpad_heads(c(cb, GDN_HEADS), GDN_HEADS) for cb in (CB_GQ, CB_GK, CB_GV, CB_GZ)]
    parts += [gab[:, :GDN_HEADS], gab[:, GDN_HEADS:2 * GDN_HEADS]]
    parts += [_unpad_heads(c(cb, SB_HEADS), SB_HEADS) for cb in (CB_SQ, CB_SK, CB_SV)]
    parts += [c(CB_MQ, 2), c(CB_MKV, 2)[:, :MLA_KV_RANK + MLA_ROPE]]
    return jnp.concatenate(parts, axis=1)


def _pad_lanes(w, width):
    return jnp.pad(w, ((0, 0), (0, width - w.shape[1])))


def _mla_up_pad(w_uq, w_ukv):
    H = MLA_HEADS
    dq = MLA_NOPE + MLA_ROPE
    wuq = jnp.pad(w_uq.reshape(-1, H, dq), ((0, 0), (0, 0), (0, LANES - dq))).reshape(-1, H * LANES)
    kv = w_ukv.reshape(-1, H, MLA_NOPE + MLA_V)
    wuk = jnp.pad(kv[:, :, :MLA_NOPE], ((0, 0), (0, 0), (0, LANES - MLA_NOPE))).reshape(-1, H * LANES)
    wuv = jnp.pad(kv[:, :, MLA_NOPE:], ((0, 0), (0, 0), (0, LANES - MLA_V))).reshape(-1, H * LANES)
    return wuq, wuk, wuv


def _mla_up_unpad(dwuq, dwuk, dwuv):
    H = MLA_HEADS
    dq = MLA_NOPE + MLA_ROPE
    g_uq = dwuq.reshape(-1, H, LANES)[:, :, :dq].reshape(-1, H * dq)
    g_ukv = jnp.concatenate([dwuk.reshape(-1, H, LANES)[:, :, :MLA_NOPE], dwuv.reshape(-1, H, LANES)[:, :, :MLA_V]],
                            axis=2).reshape(-1, H * (MLA_NOPE + MLA_V))
    return g_uq, g_ukv


def _rope_tables(positions):
    inv = 1.0 / (ROPE_BASE ** (jnp.arange(0, MLA_ROPE, 2, dtype=F32) / MLA_ROPE))
    ang = positions.astype(F32)[:, None] * inv
    cos, sin = jnp.cos(ang), jnp.sin(ang)
    T = positions.shape[0]
    one = lambda n: jnp.ones((T, n), F32)
    zero = lambda n: jnp.zeros((T, n), F32)
    tail = LANES - MLA_NOPE - MLA_ROPE
    C = jnp.concatenate([one(MLA_NOPE), cos, cos, one(tail)], axis=1)
    Sn = jnp.concatenate([zero(MLA_NOPE), -sin, zero(HALF_ROPE + tail)], axis=1)
    Sp = jnp.concatenate([zero(MLA_NOPE + HALF_ROPE), sin, zero(tail)], axis=1)
    return C, Sn, Sp


def _layer_weights(full, i):
    W = {}
    for tag in ("ffa", "ffb"):
        w_in_r = ffn_interleave(full[tag + "_w_in"][i])
        W[tag + "_in"], W[tag + "_in_t"] = w_in_r, w_in_r.T
        W[tag + "_out"], W[tag + "_out_t"] = full[tag + "_w_out"][i], full[tag + "_w_out"][i].T
    W["win"] = _pad_in_proj(full["mix_w_in"][i])
    W["win_t"] = W["win"].T
    wo = full["mix_w_o"][i]
    W["wo"] = jnp.pad(wo.reshape(-1, GDN_D, wo.shape[1]), ((0, 0), (0, LANES - GDN_D), (0, 0))).reshape(-1, wo.shape[1])
    W["wo_t"] = W["wo"].T
    W["wuq"], W["wuk"], W["wuv"] = _mla_up_pad(full["mla_w_uq"][i], full["mla_w_ukv"][i])
    W["wuq_t"], W["wuk_t"], W["wuv_t"] = W["wuq"].T, W["wuk"].T, W["wuv"].T
    W["wg"], W["wg_t"], W["wp"] = full["ple_w_gate"][i], full["ple_w_gate"][i].T, full["ple_w_proj"][i]
    W["conv"] = _pad_heads(full["gdn_conv_w"][i], GDN_QKV_BLOCKS)
    W["ln_g"] = [full["ln_g"][i, j][None, :] for j in range(3)]
    W["ln_b"] = [full["ln_b"][i, j][None, :] for j in range(3)]
    W["a_row"] = _pad_lanes(full["gdn_a_log"][i][None, :], LANES)
    W["dt_row"] = _pad_lanes(full["gdn_dt_bias"][i][None, :], LANES)
    W["nw"] = full["gdn_norm_w"][i][None, :]
    W["wq"] = full["mla_q_norm_w"][i][None, :]
    W["wkv"] = full["mla_kv_norm_w"][i][None, :]
    return W


def _layer_fwd(h0, p_i, W, rope, i):
    L = "L%d_" % i
    S = {"h0": h0, "p": p_i}
    S["h1"], S["xh1"], S["rs1"], S["GUa"] = ffn_fwd(h0, W["ffa_in"], W["ffa_out"], W["ln_g"][0], W["ln_b"][0], L + "ffa_fwd")
    S["proj"] = mm_nn(S["h1"], W["win"], L + "in_proj")
    S["qkv"] = gdn_pre_fwd(S["proj"], W["conv"], L + "gdn_pre_fwd")
    S["o_gdn"], S["opre"], S["Tm"], S["Sin"] = gdn_chunk_fwd(S["qkv"], S["proj"], W["a_row"], W["dt_row"], W["nw"],
                                                            L + "gdn_chunk_fwd")
    S["o_sb"], S["tot"] = sb_fwd(S["proj"], L + "sb_fwd")
    S["Q"], S["K"], S["V"] = mla_pre_fwd(S["proj"], W["wq"], W["wkv"], W["wuq"], W["wuk"], W["wuv"], *rope, L + "mla_pre_fwd")
    S["o_mla"], S["lse"] = mla_fwd(S["Q"], S["K"], S["V"], L + "mla_fwd")
    S["h2"], S["xh2"], S["rs2"] = proj_res_ln([S["o_gdn"], S["o_sb"], S["o_mla"]], W["wo"], S["h1"],
                                              W["ln_g"][1], W["ln_b"][1], L + "out_proj")
    S["h3"], S["xh3"], S["rs3"], S["GUb"] = ffn_fwd(S["h2"], W["ffb_in"], W["ffb_out"], W["ln_g"][2], W["ln_b"][2], L + "ffb_fwd")
    h4, S["a"], S["e"] = ple_fwd(S["h3"], p_i, W["wg"], W["wp"], L + "ple_fwd")
    return h4, S


def _layer_bwd(dh4, S, W, rope, i):
    L = "L%d_" % i
    G = {}
    dh3, da, de = ple_bwd(dh4, S["a"], S["e"], W["wg_t"], L + "ple_bwd")
    G["ple_w_gate"] = mm_tn(S["h3"], da, L + "d_ple_gate")
    G["ple_w_proj"] = mm_tn(S["p"], de, L + "d_ple_proj")
    dz3, dg2, db2 = ln_bwd(dh3, S["xh3"], S["rs3"], W["ln_g"][2], L + "ln3_bwd")
    dh2, dGUb, actb = ffn_bwd(dz3, S["GUb"], W["ffb_in_t"], W["ffb_out_t"], L + "ffb_bwd")
    G["ffb_w_in"] = ffn_deinterleave(mm_tn(S["h2"], dGUb, L + "d_ffb_in"))
    G["ffb_w_out"] = 0.5 * mm_tn(actb, dz3, L + "d_ffb_out")
    dz2, dg1, db1 = ln_bwd(dh2, S["xh2"], S["rs2"], W["ln_g"][1], L + "ln2_bwd")
    docat = mm_nn(dz2, W["wo_t"], L + "d_ocat")
    dwo = jnp.concatenate([mm_tn(S["o_gdn"], dz2, L + "d_wo_gdn"), mm_tn(S["o_sb"], dz2, L + "d_wo_sb"),
                           mm_tn(S["o_mla"], dz2, L + "d_wo_mla")], axis=0)
    G["mix_w_o"] = dwo.reshape(-1, LANES, dwo.shape[1])[:, :GDN_D, :].reshape(-1, dwo.shape[1])
    dqkv, dgz, dgab, d_alog, d_dt, d_nw = gdn_chunk_bwd(S["qkv"], S["proj"], W["a_row"], W["dt_row"], W["nw"],
                                                        S["opre"], S["Tm"], S["Sin"], docat, L + "gdn_chunk_bwd")
    dc, dconv = gdn_pre_bwd_pointwise(S["proj"], W["conv"], dqkv, L + "gdn_pre_bwd")
    dxqkv = gdn_pre_bwd_conv(dc, W["conv"], L + "gdn_conv_bwd")
    G["gdn_conv_w"] = _unpad_heads(dconv, GDN_QKV_BLOCKS)
    G["gdn_a_log"], G["gdn_dt_bias"], G["gdn_norm_w"] = d_alog[0, :GDN_HEADS], d_dt[0, :GDN_HEADS], d_nw[0]
    dsq, dsk, dsv = sb_bwd(S["proj"], S["tot"], docat, GDN_HEADS, L + "sb_bwd")
    dQ, dK, dV = mla_bwd(S["Q"], S["K"], S["V"], S["o_mla"], S["lse"], docat, GDN_HEADS + SB_HEADS, L + "mla_bwd")
    dmq, dmkv, dwuq, dwuk, dwuv, dwq, dwkv = mla_pre_bwd(
        S["proj"], W["wq"], W["wkv"], W["wuq"], W["wuk"], W["wuv"], W["wuq_t"], W["wuk_t"], W["wuv_t"], *rope,
        dQ, dK, dV, L + "mla_pre_bwd")
    G["mla_w_uq"], G["mla_w_ukv"] = _mla_up_unpad(dwuq, dwuk, dwuv)
    G["mla_q_norm_w"], G["mla_kv_norm_w"] = dwq[0], dwkv[0]
    dproj = jnp.concatenate([dxqkv, dgz, dsq, dsk, dsv, dmq, dmkv, dgab], axis=1).astype(BF16)
    G["mix_w_in"] = _unpad_in_proj(mm_tn(S["h1"], dproj, L + "d_in_proj"))
    dh1 = mm_nn(dproj, W["win_t"], L + "d_h1", res=dz2, res_scale=ALPHA)
    dz1, dg0, db0 = ln_bwd(dh1, S["xh1"], S["rs1"], W["ln_g"][0], L + "ln1_bwd")
    dh0, dGUa, acta = ffn_bwd(dz1, S["GUa"], W["ffa_in_t"], W["ffa_out_t"], L + "ffa_bwd")
    G["ffa_w_in"] = ffn_deinterleave(mm_tn(S["h0"], dGUa, L + "d_ffa_in"))
    G["ffa_w_out"] = 0.5 * mm_tn(acta, dz1, L + "d_ffa_out")
    G["ln_g"] = jnp.concatenate([dg0, dg1, dg2], axis=0)
    G["ln_b"] = jnp.concatenate([db0, db1, db2], axis=0)
    return dh0, G


def _local_step(x, p, positions, target, full):
    rope = _rope_tables(positions)
    Ws = [_layer_weights(full, i) for i in range(DEPTH)]
    h, saved = x, []
    for i in range(DEPTH):
        h, S = _layer_fwd(h, p[i], Ws[i], rope, i)
        saved.append(S)
    loss, dh = loss_head(h, target, "loss_head")
    grads = [None] * DEPTH
    for i in reversed(range(DEPTH)):
        dh, grads[i] = _layer_bwd(dh, saved[i], Ws[i], rope, i)
    return loss, dh, {n: jnp.stack([grads[i][n] for i in range(DEPTH)]) for n in WEIGHTS}


def kernel(x, p, positions, ffa_w_in, ffa_w_out, mix_w_in, gdn_conv_w, gdn_a_log, gdn_dt_bias, gdn_norm_w, mla_q_norm_w, mla_kv_norm_w, mla_w_uq, mla_w_ukv, mix_w_o, ffb_w_in, ffb_w_out, ln_g, ln_b, ple_w_gate, ple_w_proj, loss_target, m_ffa_w_in, m_ffa_w_out, m_mix_w_in, m_gdn_conv_w, m_gdn_a_log, m_gdn_dt_bias, m_gdn_norm_w, m_mla_q_norm_w, m_mla_kv_norm_w, m_mla_w_uq, m_mla_w_ukv, m_mix_w_o, m_ffb_w_in, m_ffb_w_out, m_ln_g, m_ln_b, m_ple_w_gate, m_ple_w_proj, v_ffa_w_in, v_ffa_w_out, v_mix_w_in, v_gdn_conv_w, v_gdn_a_log, v_gdn_dt_bias, v_gdn_norm_w, v_mla_q_norm_w, v_mla_kv_norm_w, v_mla_w_uq, v_mla_w_ukv, v_mix_w_o, v_ffb_w_in, v_ffb_w_out, v_ln_g, v_ln_b, v_ple_w_gate, v_ple_w_proj):
    given = dict(locals())
    shards = {n: given[n] for n in WEIGHTS}
    full = _gather_group(shards, BIG, BF16, 16, "gather_big")
    full.update(_gather_group(shards, SMALL_SHARDED, F32, 8, "gather_small"))
    full.update({n: shards[n] for n in REPLICATED})
    loss, grad_x, G = _local_step(x[0], p[:, 0], positions[0], loss_target[0], full)
    loss = lax.psum(loss[0, 0], ("x", "y", "c"))
    parts = jnp.concatenate([_to_slots(G[n], axis) for n, axis in BIG], axis=1)
    big_sum = sum_slots(exchange_partials(_pack_rows(parts, 16).astype(BF16), "scatter_big"), "sum_big").reshape(-1)
    small_names = [n for n, _ in SMALL_SHARDED] + list(REPLICATED)
    small_flat = jnp.concatenate([G[n].reshape(-1) for n in small_names])
    small_sum = sum_slots(all_gather(_pack_rows(small_flat, 8), "gather_small_grads"), "sum_small").reshape(-1)
    me = 4 * lax.axis_index("x") + 2 * lax.axis_index("y") + lax.axis_index("c")
    grad = {}
    off = 0
    for n, _ in BIG:
        size = int(np.prod(shards[n].shape))
        grad[n] = big_sum[off:off + size].reshape(shards[n].shape)
        off += size
    off = 0
    for n in small_names:
        size = int(np.prod(G[n].shape))
        g_full = small_sum[off:off + size].reshape(G[n].shape)
        off += size
        if n in REPLICATED:
            grad[n] = g_full
        else:
            width = shards[n].shape[2]
            grad[n] = lax.dynamic_slice_in_dim(g_full, me * width, width, axis=2)
    delta, new_m, new_v = {}, {}, {}
    for n in WEIGHTS:
        shape = shards[n].shape
        view = (lambda t: t.reshape(-1, shape[-1]))
        d, nm, nv = adamw(view(shards[n]), view(grad[n]), view(given["m_" + n]), view(given["v_" + n]), "adamw_" + n)
        delta[n], new_m[n], new_v[n] = d.reshape(shape), nm.reshape(shape), nv.reshape(shape)
    return (loss, grad_x[None], *[grad[n] for n in WEIGHTS], *[delta[n] for n in WEIGHTS],
            *[new_m[n] for n in WEIGHTS], *[new_v[n] for n in WEIGHTS])
```

```python
import functools
import numpy as np
import jax
import jax.numpy as jnp
from jax import lax
from jax.experimental import pallas as pl
from jax.experimental.pallas import tpu as pltpu

F32 = jnp.float32
BF16 = jnp.bfloat16

DEPTH = 2
LN_EPS = 1e-5
RMS_EPS = 1e-6
ALPHA = (2 * DEPTH) ** 0.25
GDN_HEADS, GDN_D, GDN_CONV, GDN_CHUNK = 8, 64, 4, 64
SB_HEADS, SB_DIM = 4, 64
MLA_HEADS, MLA_NOPE, MLA_ROPE, MLA_V, MLA_Q_RANK, MLA_KV_RANK = 4, 64, 32, 64, 256, 128
ROPE_BASE = 10000.0
HALF_ROPE = MLA_ROPE // 2
LANES = 128
N_DEV = 8
ADAM_LR, ADAM_B1, ADAM_B2, ADAM_EPS, ADAM_WD, ADAM_STEP = 0.001, 0.9, 0.999, 1e-08, 0.01, 10

CB_GQ, CB_GK, CB_GV, CB_GZ = 0, 8, 16, 24
CB_SQ, CB_SK, CB_SV = 32, 36, 40
CB_MQ, CB_MKV, CB_GAB = 44, 46, 48
PROJ_W = 49 * LANES
VMEM_LIMIT = 56 * 1024 * 1024

NT_DIMS = (((1,), (1,)), ((), ()))
TN_DIMS = (((0,), (0,)), ((), ()))


def _cp(sem):
    return pltpu.CompilerParams(dimension_semantics=sem, vmem_limit_bytes=VMEM_LIMIT)


def _bdot(a, b):
    return jnp.dot(a.astype(BF16), b.astype(BF16), preferred_element_type=F32)


def _bdot_nt(a, b):
    return lax.dot_general(a.astype(BF16), b.astype(BF16), NT_DIMS, preferred_element_type=F32)


def _bdot_tn(a, b):
    return lax.dot_general(a.astype(BF16), b.astype(BF16), TN_DIMS, preferred_element_type=F32)


def _split2(a):
    hi = a.astype(BF16)
    lo = (a - hi.astype(F32)).astype(BF16)
    return hi, lo


def _hdot(a, b):
    m = a.shape[0]
    a_hi, a_lo = _split2(a)
    b_hi, b_lo = _split2(b)
    r = jnp.dot(jnp.concatenate([a_hi, a_lo], axis=0), b_hi, preferred_element_type=F32)
    return r[:m] + r[m:] + jnp.dot(a_hi, b_lo, preferred_element_type=F32)


def _hdot_tn(a, b):
    a_hi, a_lo = _split2(a)
    b_hi, b_lo = _split2(b)
    d = functools.partial(lax.dot_general, dimension_numbers=TN_DIMS, preferred_element_type=F32)
    return d(a_hi, b_hi) + d(a_lo, b_hi) + d(a_hi, b_lo)


def _ones_dot(x, ones_bf16):
    hi = x.astype(BF16)
    r1 = x - hi.astype(F32)
    mid = r1.astype(BF16)
    lo = (r1 - mid.astype(F32)).astype(BF16)
    d = functools.partial(jnp.dot, preferred_element_type=F32)
    return d(hi, ones_bf16) + d(mid, ones_bf16) + d(lo, ones_bf16)


def _ones_dot_left(ones_bf16, x):
    hi = x.astype(BF16)
    r1 = x - hi.astype(F32)
    mid = r1.astype(BF16)
    lo = (r1 - mid.astype(F32)).astype(BF16)
    d = functools.partial(jnp.dot, preferred_element_type=F32)
    return d(ones_bf16, hi) + d(ones_bf16, mid) + d(ones_bf16, lo)


def _iota2(shape, dim):
    return lax.broadcasted_iota(jnp.int32, shape, dim)


def _sigmoid(x):
    return 1.0 / (1.0 + jnp.exp(-x))


def _softplus(x):
    return jnp.maximum(x, 0.0) + jnp.log(1.0 + jnp.exp(-jnp.abs(x)))


def _pick(n, limit, mult):
    if n <= limit:
        return n
    best = None
    for t in range(mult, limit + 1, mult):
        if n % t == 0:
            best = t
    assert best is not None, (n, limit, mult)
    return best


def mm_nn(a, b, name, out_dtype=F32, res=None, res_scale=1.0, b_transposed=False):
    M, K = a.shape
    N = b.shape[0] if b_transposed else b.shape[1]
    tm, tn, tk = _pick(M, 512, 16), _pick(N, 1024, LANES), _pick(K, 1024, LANES)
    nk = K // tk
    has_res = res is not None
    dot = _bdot_nt if b_transposed else _bdot

    def body(*refs):
        if has_res:
            a_ref, b_ref, r_ref, o_ref, acc_ref = refs
        else:
            a_ref, b_ref, o_ref, acc_ref = refs
        k = pl.program_id(2)

        @pl.when(k == 0)
        def _():
            acc_ref[...] = jnp.zeros_like(acc_ref)

        acc_ref[...] += dot(a_ref[...], b_ref[...])

        @pl.when(k == nk - 1)
        def _():
            out = acc_ref[...]
            if has_res:
                out = out + res_scale * r_ref[...]
            o_ref[...] = out.astype(o_ref.dtype)

    b_spec = pl.BlockSpec((tn, tk), lambda i, j, k: (j, k)) if b_transposed else pl.BlockSpec((tk, tn), lambda i, j, k: (k, j))
    in_specs = [pl.BlockSpec((tm, tk), lambda i, j, k: (i, k)), b_spec]
    args = [a, b]
    if has_res:
        in_specs.append(pl.BlockSpec((tm, tn), lambda i, j, k: (i, j)))
        args.append(res)
    return pl.pallas_call(
        body, grid=(M // tm, N // tn, nk), in_specs=in_specs,
        out_specs=pl.BlockSpec((tm, tn), lambda i, j, k: (i, j)),
        out_shape=jax.ShapeDtypeStruct((M, N), out_dtype),
        scratch_shapes=[pltpu.VMEM((tm, tn), F32)],
        compiler_params=_cp(("parallel", "parallel", "arbitrary")), name=name)(*args)


def mm_tn(a, b, name, out_dtype=F32):
    T, K = a.shape
    _, N = b.shape
    tk, tn, tt = _pick(K, 512, LANES), _pick(N, 1024, LANES), _pick(T, 512, 16)
    nt = T // tt

    def body(a_ref, b_ref, o_ref, acc_ref):
        t = pl.program_id(2)

        @pl.when(t == 0)
        def _():
            acc_ref[...] = jnp.zeros_like(acc_ref)

        acc_ref[...] += _bdot_tn(a_ref[...], b_ref[...])

        @pl.when(t == nt - 1)
        def _():
            o_ref[...] = acc_ref[...].astype(o_ref.dtype)

    return pl.pallas_call(
        body, grid=(K // tk, N // tn, nt),
        in_specs=[pl.BlockSpec((tt, tk), lambda i, j, t: (t, i)), pl.BlockSpec((tt, tn), lambda i, j, t: (t, j))],
        out_specs=pl.BlockSpec((tk, tn), lambda i, j, t: (i, j)),
        out_shape=jax.ShapeDtypeStruct((K, N), out_dtype),
        scratch_shapes=[pltpu.VMEM((tk, tn), F32)],
        compiler_params=_cp(("parallel", "parallel", "arbitrary")), name=name)(a, b)


def _ln_apply(z, g, b):
    mu = jnp.mean(z, axis=-1, keepdims=True)
    zc = z - mu
    var = jnp.mean(zc * zc, axis=-1, keepdims=True)
    rstd = lax.rsqrt(var + LN_EPS)
    xhat = zc * rstd
    return xhat * g + b, xhat, rstd


def ln_bwd(dout, xhat, rstd, g, name):
    T, D = dout.shape
    tm = _pick(T, 512, 8)

    def body(do_ref, xh_ref, rs_ref, g_ref, dz_ref, dg_ref, db_ref):
        i = pl.program_id(0)

        @pl.when(i == 0)
        def _():
            dg_ref[...] = jnp.zeros_like(dg_ref)
            db_ref[...] = jnp.zeros_like(db_ref)

        do = do_ref[...]
        xh = xh_ref[...]
        dxh = do * g_ref[...]
        m1 = jnp.mean(dxh, axis=-1, keepdims=True)
        m2 = jnp.mean(dxh * xh, axis=-1, keepdims=True)
        dz_ref[...] = rs_ref[...] * (dxh - m1 - xh * m2)
        dg_ref[...] += jnp.sum(do * xh, axis=0, keepdims=True)
        db_ref[...] += jnp.sum(do, axis=0, keepdims=True)

    row = pl.BlockSpec((tm, D), lambda i: (i, 0))
    vec = pl.BlockSpec((1, D), lambda i: (0, 0))
    return pl.pallas_call(
        body, grid=(T // tm,),
        in_specs=[row, row, pl.BlockSpec((tm, 1), lambda i: (i, 0)), vec],
        out_specs=[row, vec, vec],
        out_shape=[jax.ShapeDtypeStruct((T, D), F32), jax.ShapeDtypeStruct((1, D), F32), jax.ShapeDtypeStruct((1, D), F32)],
        compiler_params=_cp(("arbitrary",)), name=name)(dout, xhat, rstd, g)


FFN_CHUNKS = N_DEV // 2


def ffn_fwd(h, w_in, w_out, layer, g, b, name):
    T, D = h.shape
    fc = w_in.shape[3]
    half = w_out.shape[2]
    tm = _pick(T, 512, 8)
    nc = FFN_CHUNKS

    def body(h_ref, wg_ref, wu_ref, wo_ref, g_ref, b_ref, out_ref, xh_ref, rs_ref, G_ref, U_ref, acc_ref):
        c = pl.program_id(1)

        @pl.when(c == 0)
        def _():
            acc_ref[...] = jnp.zeros_like(acc_ref)

        hb = h_ref[...].astype(BF16)
        G = jnp.dot(hb, wg_ref[0, 0], preferred_element_type=F32)
        U = jnp.dot(hb, wu_ref[0, 0], preferred_element_type=F32)
        G_ref[0] = G
        U_ref[0] = U
        act = G * _sigmoid(G) * U
        acc_ref[...] += _bdot(act, wo_ref[:, 0].reshape(2 * half, D))

        @pl.when(c == nc - 1)
        def _():
            z = ALPHA * h_ref[...] + 0.5 * acc_ref[...]
            out, xh, rs = _ln_apply(z, g_ref[...], b_ref[...])
            out_ref[...] = out
            xh_ref[...] = xh
            rs_ref[...] = rs

    row = pl.BlockSpec((tm, D), lambda i, c: (i, 0))
    vec = pl.BlockSpec((1, D), lambda i, c: (0, 0))
    cblk = pl.BlockSpec((1, tm, fc), lambda i, c: (c, i, 0))
    csds = jax.ShapeDtypeStruct((nc, T, fc), F32)
    return pl.pallas_call(
        body, grid=(T // tm, nc),
        in_specs=[row, pl.BlockSpec((1, 1, D, fc), lambda i, c: (c, layer, 0, 0)),
                  pl.BlockSpec((1, 1, D, fc), lambda i, c: (c + nc, layer, 0, 0)),
                  pl.BlockSpec((2, 1, half, D), lambda i, c: (c, layer, 0, 0)), vec, vec],
        out_specs=[row, row, pl.BlockSpec((tm, 1), lambda i, c: (i, 0)), cblk, cblk],
        out_shape=[jax.ShapeDtypeStruct((T, D), F32), jax.ShapeDtypeStruct((T, D), F32), jax.ShapeDtypeStruct((T, 1), F32),
                   csds, csds],
        scratch_shapes=[pltpu.VMEM((tm, D), F32)],
        compiler_params=_cp(("parallel", "arbitrary")), name=name)(h, w_in, w_in, w_out, g, b)


def ffn_bwd(dz, G, U, w_in, w_out, layer, name):
    T, D = dz.shape
    nc, _, fc = G.shape
    half = w_out.shape[2]
    tm = _pick(T, 512, 16)

    def body(dz_ref, G_ref, U_ref, wg_ref, wu_ref, wo_ref, dh_ref, dG_ref, dU_ref, act_ref, acc_ref):
        c = pl.program_id(1)

        @pl.when(c == 0)
        def _():
            acc_ref[...] = jnp.zeros_like(acc_ref)

        dy = (0.5 * dz_ref[...]).astype(BF16)
        dact = _bdot_nt(dy, wo_ref[:, 0].reshape(2 * half, D))
        G = G_ref[0]
        U = U_ref[0]
        s = _sigmoid(G)
        silu = G * s
        dG = (dact * U * (s * (1.0 + G * (1.0 - s)))).astype(BF16)
        dU = (dact * silu).astype(BF16)
        dG_ref[0] = dG
        dU_ref[0] = dU
        act_ref[0] = (silu * U).astype(BF16)
        acc_ref[...] += _bdot_nt(dG, wg_ref[0, 0]) + _bdot_nt(dU, wu_ref[0, 0])

        @pl.when(c == nc - 1)
        def _():
            dh_ref[...] = ALPHA * dz_ref[...] + acc_ref[...]

    row = pl.BlockSpec((tm, D), lambda i, c: (i, 0))
    cblk = pl.BlockSpec((1, tm, fc), lambda i, c: (c, i, 0))
    csds = jax.ShapeDtypeStruct((nc, T, fc), BF16)
    return pl.pallas_call(
        body, grid=(T // tm, nc),
        in_specs=[row, cblk, cblk, pl.BlockSpec((1, 1, D, fc), lambda i, c: (c, layer, 0, 0)),
                  pl.BlockSpec((1, 1, D, fc), lambda i, c: (c + nc, layer, 0, 0)),
                  pl.BlockSpec((2, 1, half, D), lambda i, c: (c, layer, 0, 0))],
        out_specs=[row, cblk, cblk, cblk],
        out_shape=[jax.ShapeDtypeStruct((T, D), F32), csds, csds, csds],
        scratch_shapes=[pltpu.VMEM((tm, D), F32)],
        compiler_params=_cp(("parallel", "arbitrary")), name=name)(dz, G, U, w_in, w_in, w_out)


def _with_prev(prev, n_in):
    if prev is None:
        return [], [], {}
    return [prev], [pl.BlockSpec(memory_space=pl.ANY)], {n_in: 0}


def ffn_dw_in(h, dG, dU, layer, n_layers, prev, name):
    T, D = h.shape
    nc, _, fc = dG.shape
    tk, tt = _pick(D, 512, LANES), _pick(T, 512, 16)
    nt = T // tt
    extra, extra_specs, alias = _with_prev(prev, 3)

    def body(h_ref, dG_ref, dU_ref, *rest):
        o_ref, acc_ref = rest[-2:]
        s = pl.program_id(0)
        t = pl.program_id(2)

        @pl.when(t == 0)
        def _():
            acc_ref[...] = jnp.zeros_like(acc_ref)

        hb = h_ref[...].astype(BF16)

        @pl.when(s < nc)
        def _():
            acc_ref[...] += lax.dot_general(hb, dG_ref[0], TN_DIMS, preferred_element_type=F32)

        @pl.when(s >= nc)
        def _():
            acc_ref[...] += lax.dot_general(hb, dU_ref[0], TN_DIMS, preferred_element_type=F32)

        @pl.when(t == nt - 1)
        def _():
            o_ref[0, 0] = acc_ref[...].astype(o_ref.dtype)

    return pl.pallas_call(
        body, grid=(2 * nc, D // tk, nt),
        in_specs=[pl.BlockSpec((tt, tk), lambda s, k, t: (t, k)),
                  pl.BlockSpec((1, tt, fc), lambda s, k, t: (jnp.minimum(s, nc - 1), t, 0)),
                  pl.BlockSpec((1, tt, fc), lambda s, k, t: (jnp.maximum(s - nc, 0), t, 0))] + extra_specs,
        out_specs=pl.BlockSpec((1, 1, tk, fc), lambda s, k, t: (s, layer, k, 0)),
        out_shape=jax.ShapeDtypeStruct((2 * nc, n_layers, D, fc), BF16),
        scratch_shapes=[pltpu.VMEM((tk, fc), F32)],
        input_output_aliases=alias,
        compiler_params=_cp(("parallel", "parallel", "arbitrary")), name=name)(h, dG, dU, *extra)


def ffn_dw_out(act, dz, layer, n_layers, prev, name):
    nc, T, fc = act.shape
    D = dz.shape[1]
    half = fc // 2
    tt = _pick(T, 512, 16)
    nt = T // tt
    extra, extra_specs, alias = _with_prev(prev, 2)

    def body(a_ref, dz_ref, *rest):
        o_ref, acc_ref = rest[-2:]
        t = pl.program_id(1)

        @pl.when(t == 0)
        def _():
            acc_ref[...] = jnp.zeros_like(acc_ref)

        acc_ref[...] += _bdot_tn(a_ref[0], dz_ref[...])

        @pl.when(t == nt - 1)
        def _():
            o_ref[:, 0] = (0.5 * acc_ref[...]).reshape(2, half, D).astype(o_ref.dtype)

    return pl.pallas_call(
        body, grid=(nc, nt),
        in_specs=[pl.BlockSpec((1, tt, fc), lambda c, t: (c, t, 0)), pl.BlockSpec((tt, D), lambda c, t: (t, 0))] + extra_specs,
        out_specs=pl.BlockSpec((2, 1, half, D), lambda c, t: (c, layer, 0, 0)),
        out_shape=jax.ShapeDtypeStruct((2 * nc, n_layers, half, D), BF16),
        scratch_shapes=[pltpu.VMEM((fc, D), F32)],
        input_output_aliases=alias,
        compiler_params=_cp(("parallel", "arbitrary")), name=name)(act, dz, *extra)


def proj_res_ln(parts, w, res, g, b, name):
    T, D = res.shape
    tm = _pick(T, 512, 8)
    widths = [p.shape[1] for p in parts]
    offs = [int(sum(widths[:i])) for i in range(len(parts))]
    n = len(parts)

    def body(*refs):
        p_refs = refs[:n]
        w_ref, r_ref, g_ref, b_ref, out_ref, xh_ref, rs_ref = refs[n:]
        acc = ALPHA * r_ref[...]
        for p_ref, o, wd in zip(p_refs, offs, widths):
            acc = acc + _bdot(p_ref[...], w_ref[o:o + wd, :])
        out, xh, rs = _ln_apply(acc, g_ref[...], b_ref[...])
        out_ref[...] = out
        xh_ref[...] = xh
        rs_ref[...] = rs

    row = pl.BlockSpec((tm, D), lambda i: (i, 0))
    vec = pl.BlockSpec((1, D), lambda i: (0, 0))
    return pl.pallas_call(
        body, grid=(T // tm,),
        in_specs=[pl.BlockSpec((tm, wd), lambda i: (i, 0)) for wd in widths]
        + [pl.BlockSpec(w.shape, lambda i: (0, 0)), row, vec, vec],
        out_specs=[row, row, pl.BlockSpec((tm, 1), lambda i: (i, 0))],
        out_shape=[jax.ShapeDtypeStruct((T, D), F32), jax.ShapeDtypeStruct((T, D), F32), jax.ShapeDtypeStruct((T, 1), F32)],
        compiler_params=_cp(("parallel",)), name=name)(*parts, w, res, g, b)


def ple_fwd(h, p, wg, wp, name):
    T, D = h.shape
    P = p.shape[1]
    tm, tn = _pick(T, 512, 8), _pick(D, 512, LANES)

    def body(h_ref, hn_ref, p_ref, wg_ref, wp_ref, out_ref, a_ref, e_ref):
        a = _bdot(h_ref[...], wg_ref[...])
        e = _bdot(p_ref[...], wp_ref[...])
        a_ref[...] = a
        e_ref[...] = e
        out_ref[...] = hn_ref[...] + _sigmoid(a) * e

    blk = pl.BlockSpec((tm, tn), lambda i, j: (i, j))
    sds = jax.ShapeDtypeStruct((T, D), F32)
    return pl.pallas_call(
        body, grid=(T // tm, D // tn),
        in_specs=[pl.BlockSpec((tm, D), lambda i, j: (i, 0)), blk, pl.BlockSpec((tm, P), lambda i, j: (i, 0)),
                  pl.BlockSpec((D, tn), lambda i, j: (0, j)), pl.BlockSpec((P, tn), lambda i, j: (0, j))],
        out_specs=[blk, blk, blk], out_shape=[sds, sds, sds],
        compiler_params=_cp(("parallel", "parallel")), name=name)(h, h, p, wg, wp)


def ple_bwd(dout, a, e, wg, name):
    T, D = dout.shape
    tm = _pick(T, 512, 16)

    def body(do_ref, a_ref, e_ref, wg_ref, dh_ref, da_ref, de_ref):
        do = do_ref[...]
        s = _sigmoid(a_ref[...])
        da = (do * e_ref[...] * s * (1.0 - s)).astype(BF16)
        da_ref[...] = da
        de_ref[...] = (do * s).astype(BF16)
        dh_ref[...] = do + _bdot_nt(da, wg_ref[...])

    row = pl.BlockSpec((tm, D), lambda i: (i, 0))
    return pl.pallas_call(
        body, grid=(T // tm,),
        in_specs=[row, row, row, pl.BlockSpec((D, D), lambda i: (0, 0))],
        out_specs=[row, row, row],
        out_shape=[jax.ShapeDtypeStruct((T, D), F32), jax.ShapeDtypeStruct((T, D), BF16), jax.ShapeDtypeStruct((T, D), BF16)],
        compiler_params=_cp(("parallel",)), name=name)(dout, a, e, wg)


def loss_head(y, target, name):
    T, D = y.shape
    tm = _pick(T, 512, 8)

    def body(y_ref, t_ref, loss_ref, dy_ref):
        i = pl.program_id(0)

        @pl.when(i == 0)
        def _():
            loss_ref[...] = jnp.zeros_like(loss_ref)

        err = y_ref[...] - t_ref[...]
        dy_ref[...] = err * (1.0 / D)
        per_tok = jnp.sum(err * err, axis=-1, keepdims=True) * (1.0 / D)
        loss_ref[...] += 0.5 * jnp.sum(per_tok, axis=0, keepdims=True)

    row = pl.BlockSpec((tm, D), lambda i: (i, 0))
    return pl.pallas_call(
        body, grid=(T // tm,), in_specs=[row, row],
        out_specs=[pl.BlockSpec((1, 1), lambda i: (0, 0)), row],
        out_shape=[jax.ShapeDtypeStruct((1, 1), F32), jax.ShapeDtypeStruct((T, D), F32)],
        compiler_params=_cp(("arbitrary",)), name=name)(y, target)


GDN_QKV_BLOCKS = 3 * GDN_HEADS
HALO = 8


def _conv_taps(pad_ref, w_ref, tm, base):
    acc = w_ref[0:1, :] * pad_ref[pl.ds(base, tm), :]
    for k in range(1, GDN_CONV):
        acc = acc + w_ref[k:k + 1, :] * pad_ref[pl.ds(base + k, tm), :]
    return acc


def _gdn_pre_common(x_ref, halo_ref, w_ref, pad_ref, tm):
    i = pl.program_id(1)
    hb = pl.program_id(0)
    pad_ref[0:HALO, :] = jnp.where(i == 0, 0.0, halo_ref[...])
    pad_ref[HALO:HALO + tm, :] = x_ref[...]
    c = _conv_taps(pad_ref, w_ref, tm, HALO - (GDN_CONV - 1))
    s = _sigmoid(c)
    y = c * s
    r = lax.rsqrt(jnp.sum(y * y, axis=-1, keepdims=True) + RMS_EPS)
    scale = jnp.where(hb < GDN_HEADS, GDN_D ** -0.5, 1.0)
    return hb, c, s, y, r, scale


def gdn_pre_fwd(proj, conv_w_p, name):
    T = proj.shape[0]
    tm = _pick(T, 512, 8)

    def body(x_ref, halo_ref, w_ref, o_ref, pad_ref):
        hb, c, s, y, r, scale = _gdn_pre_common(x_ref, halo_ref, w_ref, pad_ref, tm)
        o_ref[...] = jnp.where(hb < 2 * GDN_HEADS, y * r * scale, y)

    return pl.pallas_call(
        body, grid=(GDN_QKV_BLOCKS, T // tm),
        in_specs=[pl.BlockSpec((tm, LANES), lambda hb, i: (i, hb)),
                  pl.BlockSpec((HALO, LANES), lambda hb, i: (jnp.maximum(i * (tm // HALO) - 1, 0), hb)),
                  pl.BlockSpec((GDN_CONV, LANES), lambda hb, i: (0, hb))],
        out_specs=pl.BlockSpec((tm, LANES), lambda hb, i: (i, hb)),
        out_shape=jax.ShapeDtypeStruct((T, GDN_QKV_BLOCKS * LANES), F32),
        scratch_shapes=[pltpu.VMEM((tm + HALO, LANES), F32)],
        compiler_params=_cp(("parallel", "parallel")), name=name)(proj, proj, conv_w_p)


def gdn_pre_bwd_pointwise(proj, conv_w_p, dqkv, name):
    T = proj.shape[0]
    tm = _pick(T, 512, 8)

    def body(x_ref, halo_ref, w_ref, d_ref, dc_ref, dw_ref, pad_ref):
        i = pl.program_id(1)
        hb, c, s, y, r, scale = _gdn_pre_common(x_ref, halo_ref, w_ref, pad_ref, tm)

        @pl.when(i == 0)
        def _():
            dw_ref[...] = jnp.zeros_like(dw_ref)

        d = d_ref[...]
        n = y * r
        dn = d * scale
        dy = jnp.where(hb < 2 * GDN_HEADS, r * (dn - n * jnp.sum(dn * n, axis=-1, keepdims=True)), d)
        dc = dy * (s * (1.0 + c * (1.0 - s)))
        dc_ref[...] = dc
        for k in range(GDN_CONV):
            xs = pad_ref[pl.ds(HALO - (GDN_CONV - 1) + k, tm), :]
            dw_ref[k:k + 1, :] += jnp.sum(dc * xs, axis=0, keepdims=True)

    blk = pl.BlockSpec((tm, LANES), lambda hb, i: (i, hb))
    wblk = pl.BlockSpec((GDN_CONV, LANES), lambda hb, i: (0, hb))
    return pl.pallas_call(
        body, grid=(GDN_QKV_BLOCKS, T // tm),
        in_specs=[blk, pl.BlockSpec((HALO, LANES), lambda hb, i: (jnp.maximum(i * (tm // HALO) - 1, 0), hb)), wblk, blk],
        out_specs=[blk, wblk],
        out_shape=[jax.ShapeDtypeStruct((T, GDN_QKV_BLOCKS * LANES), F32),
                   jax.ShapeDtypeStruct((GDN_CONV, GDN_QKV_BLOCKS * LANES), F32)],
        scratch_shapes=[pltpu.VMEM((tm + HALO, LANES), F32)],
        compiler_params=_cp(("parallel", "arbitrary")), name=name)(proj, proj, conv_w_p, dqkv)


def gdn_pre_bwd_conv(dc, conv_w_p, name):
    T = dc.shape[0]
    tm = _pick(T, 512, 8)
    nt = T // tm

    def body(dc_ref, halo_ref, w_ref, dx_ref, pad_ref):
        i = pl.program_id(1)
        pad_ref[0:tm, :] = dc_ref[...]
        pad_ref[tm:tm + HALO, :] = jnp.where(i == nt - 1, 0.0, halo_ref[...])
        acc = w_ref[GDN_CONV - 1:GDN_CONV, :] * pad_ref[pl.ds(0, tm), :]
        for k in range(GDN_CONV - 1):
            acc = acc + w_ref[k:k + 1, :] * pad_ref[pl.ds(GDN_CONV - 1 - k, tm), :]
        dx_ref[...] = acc

    blk = pl.BlockSpec((tm, LANES), lambda hb, i: (i, hb))
    return pl.pallas_call(
        body, grid=(GDN_QKV_BLOCKS, nt),
        in_specs=[blk, pl.BlockSpec((HALO, LANES), lambda hb, i: (jnp.minimum((i + 1) * (tm // HALO), T // HALO - 1), hb)),
                  pl.BlockSpec((GDN_CONV, LANES), lambda hb, i: (0, hb))],
        out_specs=blk,
        out_shape=jax.ShapeDtypeStruct((T, GDN_QKV_BLOCKS * LANES), F32),
        scratch_shapes=[pltpu.VMEM((tm + HALO, LANES), F32)],
        compiler_params=_cp(("parallel", "parallel")), name=name)(dc, dc, conv_w_p)


def _chunk_masks(C):
    row = _iota2((C, C), 0)
    col = _iota2((C, C), 1)
    return row >= col, row > col, row == col


def _col_to_row(colv, eye):
    return jnp.sum(jnp.where(eye, colv, 0.0), axis=0, keepdims=True)


def _row_to_col(rowv, eye):
    return jnp.sum(jnp.where(eye, rowv, 0.0), axis=1, keepdims=True)


def _unit_lower_inverse(A, eye):
    C = A.shape[0]
    P = jnp.where(eye, 1.0, 0.0) - A
    Bp = _hdot(A, A)
    for _ in range(4):
        R = _hdot(jnp.concatenate([Bp, P], axis=0), Bp)
        Bp = R[:C]
        P = P + R[C:]
    return P + _hdot(P, Bp)


def _gdn_gates(gab, a_row, dt_row, incl):
    g_all = -jnp.exp(a_row) * _softplus(gab + dt_row)
    beta_all = _sigmoid(gab)
    gc_all = _ones_dot_left(incl.astype(BF16), g_all)
    return g_all, beta_all, gc_all


def _gdn_head_common(qkv_ref, h, gc_all, beta_all, incl, strict, eye):
    C = GDN_CHUNK
    hq = pl.ds(h * LANES, GDN_D)
    hk = pl.ds((GDN_HEADS + h) * LANES, GDN_D)
    hv = pl.ds((2 * GDN_HEADS + h) * LANES, GDN_D)
    q, k, v = qkv_ref[:, hq], qkv_ref[:, hk], qkv_ref[:, hv]
    gc = gc_all[:, h:h + 1]
    beta = beta_all[:, GDN_HEADS + h:GDN_HEADS + h + 1]
    gc_row = _col_to_row(gc, eye)
    decay = jnp.where(incl, jnp.exp(jnp.where(incl, gc - gc_row, 0.0)), 0.0)
    e_gc = jnp.exp(gc)
    gl = gc[C - 1:C, :]
    e_gl = jnp.exp(gl)
    ekd = jnp.exp(gl - gc)
    kb = k * beta
    A = jnp.where(strict, _bdot_nt(kb, k) * decay, 0.0)
    Pm = jnp.where(incl, _bdot_nt(q, k) * decay, 0.0)
    return (hq, hk, hv), q, k, v, gc, beta, decay, e_gc, e_gl, ekd, kb, A, Pm


def gdn_chunk_fwd(qkv, proj, a_row, dt_row, norm_w, name):
    T = qkv.shape[0]
    C, H, Dh = GDN_CHUNK, GDN_HEADS, GDN_D
    N = T // C

    def body(qkv_ref, gz_ref, gab_ref, a_ref, dt_ref, nw_ref, o_ref, opre_ref, Tm_ref, Sin_ref, S_ref):
        n = pl.program_id(0)

        @pl.when(n == 0)
        def _():
            S_ref[...] = jnp.zeros_like(S_ref)

        incl, strict, eye = _chunk_masks(C)
        _, beta_all, gc_all = _gdn_gates(gab_ref[...], a_ref[...], dt_ref[...], incl)
        o_ref[...] = jnp.zeros_like(o_ref)
        opre_ref[...] = jnp.zeros_like(opre_ref)
        for h in range(H):
            (hq, _, _), q, k, v, gc, beta, decay, e_gc, e_gl, ekd, kb, A, Pm = _gdn_head_common(
                qkv_ref, h, gc_all, beta_all, incl, strict, eye)
            Tm = _unit_lower_inverse(A, eye)
            u = _hdot(Tm, v * beta)
            w = _hdot(Tm, kb * e_gc)
            S = S_ref[h]
            v_new = u - _bdot(w, S)
            o = _bdot(q * e_gc, S) + _bdot(Pm, v_new)
            S_ref[h] = S * e_gl + _bdot_tn(k * ekd, v_new)
            Sin_ref[0, h] = S
            Tm_ref[0, h] = Tm
            r = lax.rsqrt(jnp.mean(o * o, axis=-1, keepdims=True) + RMS_EPS)
            gz = gz_ref[:, hq]
            opre_ref[:, hq] = o
            o_ref[:, hq] = o * r * nw_ref[...] * (gz * _sigmoid(gz))

    vec = pl.BlockSpec((1, LANES), lambda n: (0, 0))
    hblk = pl.BlockSpec((C, H * LANES), lambda n: (n, 0))
    sblk = pl.BlockSpec((1, H, Dh, Dh), lambda n: (n, 0, 0, 0))
    return pl.pallas_call(
        body, grid=(N,),
        in_specs=[pl.BlockSpec((C, GDN_QKV_BLOCKS * LANES), lambda n: (n, 0)),
                  pl.BlockSpec((C, H * LANES), lambda n: (n, CB_GZ // H)),
                  pl.BlockSpec((C, LANES), lambda n: (n, CB_GAB)), vec, vec, pl.BlockSpec((1, Dh), lambda n: (0, 0))],
        out_specs=[hblk, hblk, sblk, sblk],
        out_shape=[jax.ShapeDtypeStruct((T, H * LANES), F32), jax.ShapeDtypeStruct((T, H * LANES), F32),
                   jax.ShapeDtypeStruct((N, H, Dh, Dh), F32), jax.ShapeDtypeStruct((N, H, Dh, Dh), F32)],
        scratch_shapes=[pltpu.VMEM((H, Dh, Dh), F32)],
        compiler_params=_cp(("arbitrary",)), name=name)(qkv, proj, proj, a_row, dt_row, norm_w)


def gdn_chunk_bwd(qkv, proj, a_row, dt_row, norm_w, opre, Tm_all, Sin_all, docat, name):
    T = qkv.shape[0]
    C, H, Dh = GDN_CHUNK, GDN_HEADS, GDN_D
    N = T // C

    def body(qkv_ref, gz_ref, gab_ref, a_ref, dt_ref, nw_ref, opre_ref, Tm_ref, Sin_ref, do_ref,
             dqkv_ref, dgz_ref, dgab_ref, da_ref, ddt_ref, dnw_ref, dS_ref):
        n = pl.program_id(0)

        @pl.when(n == 0)
        def _():
            dS_ref[...] = jnp.zeros_like(dS_ref)
            da_ref[...] = jnp.zeros_like(da_ref)
            ddt_ref[...] = jnp.zeros_like(ddt_ref)
            dnw_ref[...] = jnp.zeros_like(dnw_ref)

        incl, strict, eye = _chunk_masks(C)
        gab = gab_ref[...]
        g_all, beta_all, gc_all = _gdn_gates(gab, a_ref[...], dt_ref[...], incl)
        lane = _iota2((C, LANES), 1)
        rowi = _iota2((C, 1), 0)
        dqkv_ref[...] = jnp.zeros_like(dqkv_ref)
        dgz_ref[...] = jnp.zeros_like(dgz_ref)
        dgc_all = jnp.zeros((C, LANES), F32)
        dbeta_all = jnp.zeros((C, LANES), F32)
        nw = nw_ref[...]
        dnw = jnp.zeros_like(nw)
        for h in range(H):
            (hq, hk, hv), q, k, v, gc, beta, decay, e_gc, e_gl, ekd, kb, A, Pm = _gdn_head_common(
                qkv_ref, h, gc_all, beta_all, incl, strict, eye)
            Tm = Tm_ref[0, h]
            S = Sin_ref[0, h]
            dS = dS_ref[h]
            kbe = kb * e_gc
            u = _hdot(Tm, v * beta)
            w = _hdot(Tm, kbe)
            qd = q * e_gc
            kd = k * ekd
            v_new = u - _bdot(w, S)
            o = opre_ref[:, hq]
            gz = gz_ref[:, hq]
            don = do_ref[:, hq]
            r = lax.rsqrt(jnp.mean(o * o, axis=-1, keepdims=True) + RMS_EPS)
            nn = o * r
            sgz = _sigmoid(gz)
            silu = gz * sgz
            dgz_ref[:, hq] = don * nn * nw * (sgz * (1.0 + gz * (1.0 - sgz)))
            dnn = don * nw * silu
            dnw = dnw + jnp.sum(don * nn * silu, axis=0, keepdims=True)
            do = r * (dnn - nn * jnp.mean(dnn * nn, axis=-1, keepdims=True))
            dv_new = _bdot_tn(Pm, do) + _bdot(kd, dS)
            dPm = jnp.where(incl, _bdot_nt(do, v_new), 0.0)
            dqd = _bdot_nt(do, S)
            dkd = _bdot_nt(v_new, dS)
            dS_ref[h] = _bdot_tn(qd, do) + e_gl * dS - _bdot_tn(w, dv_new)
            dgl = jnp.sum(jnp.sum(dS * S, axis=1, keepdims=True), axis=0, keepdims=True) * e_gl
            dw = -_bdot_nt(dv_new, S)
            dvb = _hdot_tn(Tm, dv_new)
            dkbe = _hdot_tn(Tm, dw)
            dA = -jnp.where(strict, _bdot_nt(dvb, u) + _bdot_nt(dkbe, w), 0.0)
            dAD = dA * decay
            dPD = dPm * decay
            Gm = dA * A + dPm * Pm
            dgc = jnp.sum(Gm, axis=1, keepdims=True) - _row_to_col(jnp.sum(Gm, axis=0, keepdims=True), eye)
            dkb = _bdot(dAD, k) + dkbe * e_gc
            dk = _bdot_tn(dAD, kb) + _bdot_tn(dPD, q) + dkd * ekd + dkb * beta
            dq = _bdot(dPD, k) + dqd * e_gc
            tkd = jnp.sum(dkd * kd, axis=-1, keepdims=True)
            dgc = dgc + jnp.sum(dqd * qd, axis=-1, keepdims=True) - tkd + jnp.sum(dkbe * kbe, axis=-1, keepdims=True)
            dgl = dgl + jnp.sum(tkd, axis=0, keepdims=True)
            dgc = dgc + jnp.where(rowi == C - 1, dgl, 0.0)
            dbeta = jnp.sum(dvb * v, axis=-1, keepdims=True) + jnp.sum(dkb * k, axis=-1, keepdims=True)
            dqkv_ref[:, hq] = dq
            dqkv_ref[:, hk] = dk
            dqkv_ref[:, hv] = dvb * beta
            dgc_all = dgc_all + jnp.where(lane == h, dgc, 0.0)
            dbeta_all = dbeta_all + jnp.where(lane == H + h, dbeta, 0.0)
        dnw_ref[...] += dnw
        upper = (_iota2((C, C), 0) <= _iota2((C, C), 1)).astype(BF16)
        dg_all = _ones_dot_left(upper, dgc_all)
        dga = dg_all * (-jnp.exp(a_ref[...])) * _sigmoid(gab + dt_ref[...])
        dgb = dbeta_all * beta_all * (1.0 - beta_all)
        dgab_ref[...] = jnp.where(lane < H, dga, jnp.where(lane < 2 * H, dgb, 0.0))
        da_ref[...] += jnp.sum(jnp.where(lane < H, dg_all * g_all, 0.0), axis=0, keepdims=True)
        ddt_ref[...] += jnp.sum(jnp.where(lane < H, dga, 0.0), axis=0, keepdims=True)

    rev = lambda n: N - 1 - n
    vec = pl.BlockSpec((1, LANES), lambda n: (0, 0))
    nwv = pl.BlockSpec((1, Dh), lambda n: (0, 0))
    hblk = pl.BlockSpec((C, H * LANES), lambda n: (rev(n), 0))
    sblk = pl.BlockSpec((1, H, Dh, Dh), lambda n: (rev(n), 0, 0, 0))
    qblk = pl.BlockSpec((C, GDN_QKV_BLOCKS * LANES), lambda n: (rev(n), 0))
    return pl.pallas_call(
        body, grid=(N,),
        in_specs=[qblk, pl.BlockSpec((C, H * LANES), lambda n: (rev(n), CB_GZ // H)),
                  pl.BlockSpec((C, LANES), lambda n: (rev(n), CB_GAB)), vec, vec, nwv, hblk, sblk, sblk, hblk],
        out_specs=[qblk, hblk, pl.BlockSpec((C, LANES), lambda n: (rev(n), 0)), vec, vec, nwv],
        out_shape=[jax.ShapeDtypeStruct((T, GDN_QKV_BLOCKS * LANES), F32), jax.ShapeDtypeStruct((T, H * LANES), F32),
                   jax.ShapeDtypeStruct((T, LANES), F32), jax.ShapeDtypeStruct((1, LANES), F32),
                   jax.ShapeDtypeStruct((1, LANES), F32), jax.ShapeDtypeStruct((1, Dh), F32)],
        scratch_shapes=[pltpu.VMEM((H, Dh, Dh), F32)],
        compiler_params=_cp(("arbitrary",)), name=name)(qkv, proj, proj, a_row, dt_row, norm_w, opre, Tm_all, Sin_all, docat)


ATT_BQ, ATT_BK = 256, 512
NEG_BIG = -1e30


def _att_blocks(T):
    bq, bk = min(ATT_BQ, T), min(ATT_BK, T)
    assert bk % bq == 0 and T % bk == 0
    return bq, bk


def _att_specs(T, bq, cbs):
    qspec = lambda cb: pl.BlockSpec((bq, LANES), lambda h, i: (i, cb + h))
    kspec = lambda cb: pl.BlockSpec((T, LANES), lambda h, i: (0, cb + h))
    return qspec, kspec


def _kblock(ref, kb, bk):
    return ref[pl.ds(pl.multiple_of(kb * bk, bk), bk), :]


def _att_pos(i, kb, bq, bk):
    qpos = i * bq + _iota2((bq, bk), 0)
    kpos = kb * bk + _iota2((bq, bk), 1)
    return qpos, kpos


def _suffix_sum(x):
    n = x.shape[1]
    lane = _iota2(x.shape, 1)
    d = 1
    while d < n:
        x = x + jnp.where(lane < n - d, pltpu.roll(x, n - d, 1), 0.0)
        d *= 2
    return x


def _prefix_sum(x):
    n = x.shape[1]
    lane = _iota2(x.shape, 1)
    d = 1
    while d < n:
        x = x + jnp.where(lane >= d, pltpu.roll(x, d, 1), 0.0)
        d *= 2
    return x


def sb_fwd(proj, name):
    T = proj.shape[0]
    H = SB_HEADS
    bq, bk = _att_blocks(T)
    scale = SB_DIM ** -0.5

    def body(q_ref, k_ref, v_ref, o_ref, tot_ref):
        i = pl.program_id(1)
        qb = q_ref[...].astype(BF16)
        diag = (i * bq) // bk

        def block(kb, acc, R, masked):
            z = _bdot_nt(qb, _kblock(k_ref, kb, bk)) * scale
            sp = _softplus(z)
            if masked:
                qpos, kpos = _att_pos(i, kb, bq, bk)
                mask = kpos < qpos
                l1m = jnp.where(mask, -sp, 0.0)
            else:
                l1m = -sp
            W = jnp.exp((z - sp) + (_suffix_sum(l1m) - l1m) + R)
            if masked:
                W = jnp.where(mask, W, 0.0)
            acc = acc + _bdot(W, _kblock(v_ref, kb, bk))
            return acc, R + jnp.sum(l1m, axis=-1, keepdims=True)

        carry = block(diag, jnp.zeros((bq, LANES), F32), jnp.zeros((bq, 1), F32), True)
        acc, R = lax.fori_loop(0, diag, lambda j, c: block(diag - 1 - j, c[0], c[1], False), carry)
        o_ref[...] = acc
        tot_ref[...] = jnp.broadcast_to(R, (bq, LANES))

    qspec, kspec = _att_specs(T, bq, None)
    sds = jax.ShapeDtypeStruct((T, H * LANES), F32)
    oblk = pl.BlockSpec((bq, LANES), lambda h, i: (i, h))
    return pl.pallas_call(
        body, grid=(H, T // bq), in_specs=[qspec(CB_SQ), kspec(CB_SK), kspec(CB_SV)],
        out_specs=[oblk, oblk], out_shape=[sds, sds],
        compiler_params=_cp(("parallel", "parallel")), name=name)(proj, proj, proj)


def sb_bwd(proj, tot, docat, do_cb, name):
    T = proj.shape[0]
    H = SB_HEADS
    bq, bk = _att_blocks(T)
    scale = SB_DIM ** -0.5

    def body(q_ref, k_ref, v_ref, tot_ref, do_ref, dq_ref, dk_ref, dv_ref):
        i = pl.program_id(1)

        @pl.when(i == 0)
        def _():
            dk_ref[...] = jnp.zeros_like(dk_ref)
            dv_ref[...] = jnp.zeros_like(dv_ref)

        qb = q_ref[...].astype(BF16)
        dob = do_ref[...].astype(BF16)
        total = tot_ref[:, 0:1]
        diag = (i * bq) // bk

        def block(kb, carry, masked):
            dq, Ppre, Epre = carry
            kblk = _kblock(k_ref, kb, bk).astype(BF16)
            z = _bdot_nt(qb, kblk) * scale
            sp = _softplus(z)
            if masked:
                qpos, kpos = _att_pos(i, kb, bq, bk)
                mask = kpos < qpos
                l1m = jnp.where(mask, -sp, 0.0)
            else:
                l1m = -sp
            W = jnp.exp((z - sp) + (total - (_prefix_sum(l1m) + Ppre)))
            if masked:
                W = jnp.where(mask, W, 0.0)
            E = _bdot_nt(dob, _kblock(v_ref, kb, bk)) * W
            cexcl = (_prefix_sum(E) - E) + Epre
            neg = jnp.exp(-sp)
            dz = E * neg - cexcl * (1.0 - neg)
            if masked:
                dz = jnp.where(mask, dz, 0.0)
            dz = (dz * scale).astype(BF16)
            rows = pl.ds(pl.multiple_of(kb * bk, bk), bk)
            dk_ref[rows, :] += lax.dot_general(dz, qb, TN_DIMS, preferred_element_type=F32)
            dv_ref[rows, :] += lax.dot_general(W.astype(BF16), dob, TN_DIMS, preferred_element_type=F32)
            dq = dq + jnp.dot(dz, kblk, preferred_element_type=F32)
            return dq, Ppre + jnp.sum(l1m, axis=-1, keepdims=True), Epre + jnp.sum(E, axis=-1, keepdims=True)

        zc = jnp.zeros((bq, 1), F32)
        carry = lax.fori_loop(0, diag, lambda kb, c: block(kb, c, False), (jnp.zeros((bq, LANES), F32), zc, zc))
        dq, _, _ = block(diag, carry, True)
        dq_ref[...] = dq

    qspec, kspec = _att_specs(T, bq, None)
    sds = jax.ShapeDtypeStruct((T, H * LANES), F32)
    oblk = pl.BlockSpec((bq, LANES), lambda h, i: (i, h))
    kout = pl.BlockSpec((T, LANES), lambda h, i: (0, h))
    return pl.pallas_call(
        body, grid=(H, T // bq),
        in_specs=[qspec(CB_SQ), kspec(CB_SK), kspec(CB_SV), oblk, qspec(do_cb)],
        out_specs=[oblk, kout, kout], out_shape=[sds, sds, sds],
        compiler_params=_cp(("arbitrary", "arbitrary")), name=name)(proj, proj, proj, tot, docat)


def mla_fwd(Q, K, V, name):
    T = Q.shape[0]
    H = MLA_HEADS
    bq, bk = _att_blocks(T)
    scale = (MLA_NOPE + MLA_ROPE) ** -0.5

    def body(q_ref, k_ref, v_ref, o_ref, lse_ref):
        i = pl.program_id(1)
        qb = q_ref[...]
        diag = (i * bq) // bk

        def block(kb, carry, masked):
            acc, m, l = carry
            s = _bdot_nt(qb, _kblock(k_ref, kb, bk)) * scale
            if masked:
                qpos, kpos = _att_pos(i, kb, bq, bk)
                s = jnp.where(kpos <= qpos, s, NEG_BIG)
            m_new = jnp.maximum(m, jnp.max(s, axis=-1, keepdims=True))
            p = jnp.exp(s - m_new)
            corr = jnp.exp(m - m_new)
            acc = corr * acc + _bdot(p, _kblock(v_ref, kb, bk))
            return acc, m_new, corr * l + jnp.sum(p, axis=-1, keepdims=True)

        init = (jnp.zeros((bq, LANES), F32), jnp.full((bq, 1), NEG_BIG, F32), jnp.zeros((bq, 1), F32))
        carry = lax.fori_loop(0, diag, lambda kb, c: block(kb, c, False), init)
        acc, m, l = block(diag, carry, True)
        o_ref[...] = acc / l
        lse_ref[...] = jnp.broadcast_to(m + jnp.log(l), (bq, LANES))

    qspec, kspec = _att_specs(T, bq, None)
    sds = jax.ShapeDtypeStruct((T, H * LANES), F32)
    oblk = pl.BlockSpec((bq, LANES), lambda h, i: (i, h))
    return pl.pallas_call(
        body, grid=(H, T // bq), in_specs=[qspec(0), kspec(0), kspec(0)],
        out_specs=[oblk, oblk], out_shape=[sds, sds],
        compiler_params=_cp(("parallel", "parallel")), name=name)(Q, K, V)


def mla_bwd(Q, K, V, o, lse, docat, do_cb, name):
    T = Q.shape[0]
    H = MLA_HEADS
    bq, bk = _att_blocks(T)
    scale = (MLA_NOPE + MLA_ROPE) ** -0.5

    def body(q_ref, k_ref, v_ref, o_ref, lse_ref, do_ref, dq_ref, dk_ref, dv_ref):
        i = pl.program_id(1)

        @pl.when(i == 0)
        def _():
            dk_ref[...] = jnp.zeros_like(dk_ref)
            dv_ref[...] = jnp.zeros_like(dv_ref)

        qb = q_ref[...]
        do = do_ref[...]
        dob = do.astype(BF16)
        delta = jnp.sum(do * o_ref[...], axis=-1, keepdims=True)
        lse = lse_ref[:, 0:1]

        diag = (i * bq) // bk

        def block(kb, dq, masked):
            kblk = _kblock(k_ref, kb, bk)
            s = _bdot_nt(qb, kblk) * scale
            if masked:
                qpos, kpos = _att_pos(i, kb, bq, bk)
                s = jnp.where(kpos <= qpos, s, NEG_BIG)
            p = jnp.exp(s - lse)
            dp = _bdot_nt(dob, _kblock(v_ref, kb, bk))
            ds = (p * (dp - delta) * scale).astype(BF16)
            rows = pl.ds(pl.multiple_of(kb * bk, bk), bk)
            dk_ref[rows, :] += lax.dot_general(ds, qb, TN_DIMS, preferred_element_type=F32)
            dv_ref[rows, :] += lax.dot_general(p.astype(BF16), dob, TN_DIMS, preferred_element_type=F32)
            return dq + jnp.dot(ds, kblk, preferred_element_type=F32)

        dq = lax.fori_loop(0, diag, lambda kb, c: block(kb, c, False), jnp.zeros((bq, LANES), F32))
        dq_ref[...] = block(diag, dq, True)

    qspec, kspec = _att_specs(T, bq, None)
    sds = jax.ShapeDtypeStruct((T, H * LANES), F32)
    oblk = pl.BlockSpec((bq, LANES), lambda h, i: (i, h))
    kout = pl.BlockSpec((T, LANES), lambda h, i: (0, h))
    return pl.pallas_call(
        body, grid=(H, T // bq),
        in_specs=[qspec(0), kspec(0), kspec(0), oblk, oblk, qspec(do_cb)],
        out_specs=[oblk, kout, kout], out_shape=[sds, sds, sds],
        compiler_params=_cp(("arbitrary", "arbitrary")), name=name)(Q, K, V, o, lse, docat)


def _tile_heads(t, n):
    return jnp.concatenate([t] * n, axis=1)


def _rope(X, C, Sn, Sp):
    n = X.shape[1]
    return X * C + pltpu.roll(X, n - HALF_ROPE, 1) * Sn + pltpu.roll(X, HALF_ROPE, 1) * Sp


def _rope_t(dO, C, Sn, Sp):
    n = dO.shape[1]
    return dO * C + pltpu.roll(dO * Sn, HALF_ROPE, 1) + pltpu.roll(dO * Sp, n - HALF_ROPE, 1)


def _rms(x, w):
    r = lax.rsqrt(jnp.mean(x * x, axis=-1, keepdims=True) + RMS_EPS)
    xh = x * r
    return r, xh, xh * w


def _rms_bwd(dn, w, r, xh):
    dxh = dn * w
    return r * (dxh - xh * jnp.mean(dxh * xh, axis=-1, keepdims=True)), jnp.sum(dn * xh, axis=0, keepdims=True)


def _mla_pre_specs(T, tm):
    KV = MLA_KV_RANK
    QR = MLA_Q_RANK
    W = MLA_HEADS * LANES
    full = lambda shape: pl.BlockSpec(shape, lambda i: (0, 0))
    specs = [pl.BlockSpec((tm, QR), lambda i: (i, CB_MQ * LANES // QR)),
             pl.BlockSpec((tm, 2 * LANES), lambda i: (i, CB_MKV // 2)),
             full((1, QR)), full((1, KV))]
    rope = [pl.BlockSpec((tm, LANES), lambda i: (i, 0))] * 3
    return specs, rope, full, W


def mla_pre_fwd(proj, wq, wkv, wuq, wuk, wuv, ropeC, ropeSn, ropeSp, name):
    T = proj.shape[0]
    tm = _pick(T, 512, 16)
    KV = MLA_KV_RANK
    H = MLA_HEADS

    def body(mq_ref, mkv_ref, wq_ref, wkv_ref, wuq_ref, wuk_ref, wuv_ref, c_ref, sn_ref, sp_ref, Q_ref, K_ref, V_ref):
        C, Sn, Sp = (_tile_heads(t[...], H) for t in (c_ref, sn_ref, sp_ref))
        _, _, qn = _rms(mq_ref[...], wq_ref[...])
        Q_ref[...] = _rope(_bdot(qn, wuq_ref[...]), C, Sn, Sp).astype(BF16)
        mkv = mkv_ref[...]
        _, _, kvn = _rms(mkv[:, :KV], wkv_ref[...])
        kr = pltpu.roll(mkv[:, KV:], MLA_NOPE, 1)
        K_ref[...] = _rope(_bdot(kvn, wuk_ref[...]) + _tile_heads(kr, H), C, Sn, Sp).astype(BF16)
        V_ref[...] = _bdot(kvn, wuv_ref[...]).astype(BF16)

    specs, rope, full, W = _mla_pre_specs(T, tm)
    oblk = pl.BlockSpec((tm, W), lambda i: (i, 0))
    sds = jax.ShapeDtypeStruct((T, W), BF16)
    return pl.pallas_call(
        body, grid=(T // tm,),
        in_specs=specs + [full(wuq.shape), full(wuk.shape), full(wuv.shape)] + rope,
        out_specs=[oblk, oblk, oblk], out_shape=[sds, sds, sds],
        compiler_params=_cp(("parallel",)), name=name)(proj, proj, wq, wkv, wuq, wuk, wuv, ropeC, ropeSn, ropeSp)


def mla_pre_bwd(proj, wq, wkv, wuq, wuk, wuv, ropeC, ropeSn, ropeSp, dQ, dK, dV, name):
    T = proj.shape[0]
    tm = _pick(T, 512, 16)
    KV = MLA_KV_RANK
    H = MLA_HEADS

    def body(mq_ref, mkv_ref, wq_ref, wkv_ref, wuq_ref, wuk_ref, wuv_ref,
             c_ref, sn_ref, sp_ref, dQ_ref, dK_ref, dV_ref,
             dmq_ref, dmkv_ref, dwuq_ref, dwuk_ref, dwuv_ref, dwq_ref, dwkv_ref):
        i = pl.program_id(0)

        @pl.when(i == 0)
        def _():
            for ref in (dwuq_ref, dwuk_ref, dwuv_ref, dwq_ref, dwkv_ref):
                ref[...] = jnp.zeros_like(ref)

        C, Sn, Sp = (_tile_heads(t[...], H) for t in (c_ref, sn_ref, sp_ref))
        rq, xq, qn = _rms(mq_ref[...], wq_ref[...])
        mkv = mkv_ref[...]
        rkv, xkv, kvn = _rms(mkv[:, :KV], wkv_ref[...])
        dqf = _rope_t(dQ_ref[...], C, Sn, Sp)
        dkf = _rope_t(dK_ref[...], C, Sn, Sp)
        dv = dV_ref[...]
        dwuq_ref[...] += _bdot_tn(qn, dqf)
        dwuk_ref[...] += _bdot_tn(kvn, dkf)
        dwuv_ref[...] += _bdot_tn(kvn, dv)
        dmq, dwq = _rms_bwd(_bdot_nt(dqf, wuq_ref[...]), wq_ref[...], rq, xq)
        dckv, dwkv = _rms_bwd(_bdot_nt(dkf, wuk_ref[...]) + _bdot_nt(dv, wuv_ref[...]), wkv_ref[...], rkv, xkv)
        dwq_ref[...] += dwq
        dwkv_ref[...] += dwkv
        dmq_ref[...] = dmq
        dkr = dkf[:, 0:LANES]
        for h in range(1, H):
            dkr = dkr + dkf[:, h * LANES:(h + 1) * LANES]
        dkr = pltpu.roll(dkr, LANES - MLA_NOPE, 1)
        dkr = jnp.where(_iota2(dkr.shape, 1) < MLA_ROPE, dkr, 0.0)
        dmkv_ref[...] = jnp.concatenate([dckv, dkr], axis=1)

    specs, rope, full, W = _mla_pre_specs(T, tm)
    wide = pl.BlockSpec((tm, W), lambda i: (i, 0))
    return pl.pallas_call(
        body, grid=(T // tm,),
        in_specs=specs + [full(w.shape) for w in (wuq, wuk, wuv)] + rope + [wide, wide, wide],
        out_specs=[pl.BlockSpec((tm, MLA_Q_RANK), lambda i: (i, 0)), pl.BlockSpec((tm, 2 * LANES), lambda i: (i, 0)),
                   full(wuq.shape), full(wuk.shape), full(wuv.shape), full((1, MLA_Q_RANK)), full((1, KV))],
        out_shape=[jax.ShapeDtypeStruct((T, MLA_Q_RANK), F32), jax.ShapeDtypeStruct((T, 2 * LANES), F32),
                   jax.ShapeDtypeStruct(wuq.shape, F32), jax.ShapeDtypeStruct(wuk.shape, F32),
                   jax.ShapeDtypeStruct(wuv.shape, F32), jax.ShapeDtypeStruct((1, MLA_Q_RANK), F32),
                   jax.ShapeDtypeStruct((1, KV), F32)],
        compiler_params=_cp(("arbitrary",)), name=name)(
            proj, proj, wq, wkv, wuq, wuk, wuv, ropeC, ropeSn, ropeSp, dQ, dK, dV)


MESH = pl.DeviceIdType.MESH
ANY = pl.BlockSpec(memory_space=pl.ANY)


def _place():
    return lax.axis_index("x"), lax.axis_index("y"), lax.axis_index("c")


def all_gather(shards, name):
    n = len(shards)

    def body(*refs):
        x_refs, out_refs = refs[:n], refs[n:2 * n]
        send_sems, recv_sems, local_sems = refs[2 * n:]
        x, y, c = _place()
        me, sibling = (x, y, c), (x, y, 1 - c)
        chips = [(1 - x, y), (x, 1 - y), (1 - x, 1 - y)]

        def slot(a, px, py, pc):
            return out_refs[a].at[4 * px + 2 * py + pc]

        def copy(a, k, block, to, src=None):
            return pltpu.make_async_remote_copy(
                src_ref=slot(a, *block) if src is None else src, dst_ref=slot(a, *block),
                send_sem=send_sems.at[a, k], recv_sem=recv_sems.at[a, k], device_id=to, device_id_type=MESH)

        mine = [pltpu.make_async_copy(x_refs[a], slot(a, *me), local_sems.at[a]) for a in range(n)]
        first = []
        for a in range(n):
            mine[a].start()
            first.append(copy(a, 0, me, sibling, src=x_refs[a]))
            first += [copy(a, 1 + j, me, (*chip, c), src=x_refs[a]) for j, chip in enumerate(chips)]
        for cp in first:
            cp.start()
        passed = []
        for j, chip in enumerate(chips):
            for a in range(n):
                copy(a, 1 + j, (*chip, c), me).wait_recv()
                passed.append(copy(a, 4 + j, (*chip, c), sibling))
                passed[-1].start()
        for a in range(n):
            copy(a, 0, sibling, me).wait_recv()
            for j, chip in enumerate(chips):
                copy(a, 4 + j, (*chip, 1 - c), me).wait_recv()
        for cp in first + passed:
            cp.wait_send()
        for cp in mine:
            cp.wait()

    return pl.pallas_call(
        body, out_shape=[jax.ShapeDtypeStruct((N_DEV,) + s.shape, s.dtype) for s in shards],
        in_specs=[ANY] * n, out_specs=[ANY] * n,
        scratch_shapes=[pltpu.SemaphoreType.DMA((n, 7)), pltpu.SemaphoreType.DMA((n, 7)), pltpu.SemaphoreType.DMA((n,))],
        name=name)(*shards)


def exchange_partials(parts, name):
    n = len(parts)

    def body(*refs):
        src_refs, dst_refs = refs[:n], refs[n:2 * n]
        send_sems, recv_sems, local_sems = refs[2 * n:]
        x, y, c = _place()
        me = 4 * x + 2 * y + c
        copies = []
        mine = []
        for a in range(n):
            mine.append(pltpu.make_async_copy(src_refs[a].at[me], dst_refs[a].at[me], local_sems.at[a]))
            for k in range(1, N_DEV):
                px = 1 - x if k & 4 else x
                py = 1 - y if k & 2 else y
                pc = 1 - c if k & 1 else c
                copies.append(pltpu.make_async_remote_copy(
                    src_ref=src_refs[a].at[4 * px + 2 * py + pc], dst_ref=dst_refs[a].at[me],
                    send_sem=send_sems.at[a, k - 1], recv_sem=recv_sems.at[a, k - 1],
                    device_id=(px, py, pc), device_id_type=MESH))
        for cp in mine + copies:
            cp.start()
        for cp in copies:
            cp.wait_recv()
        for cp in copies:
            cp.wait_send()
        for cp in mine:
            cp.wait()

    return pl.pallas_call(
        body, out_shape=[jax.ShapeDtypeStruct(p.shape, p.dtype) for p in parts],
        in_specs=[ANY] * n, out_specs=[ANY] * n,
        scratch_shapes=[pltpu.SemaphoreType.DMA((n, 7)), pltpu.SemaphoreType.DMA((n, 7)), pltpu.SemaphoreType.DMA((n,))],
        name=name)(*parts)


def reduce_adamw(parts, w, m, v, name):
    n, R, C = parts.shape
    tr = R if R * C <= 256 * 1024 else _pick(R, 256, 16)

    def body(p_ref, w_ref, m_ref, v_ref, g_ref, d_ref, nm_ref, nv_ref):
        g_ = p_ref[0].astype(F32)
        for s in range(1, n):
            g_ = g_ + p_ref[s].astype(F32)
        m_ = ADAM_B1 * m_ref[...] + (1.0 - ADAM_B1) * g_
        v_ = ADAM_B2 * v_ref[...] + (1.0 - ADAM_B2) * (g_ * g_)
        m_hat = m_ / (1.0 - ADAM_B1 ** ADAM_STEP)
        v_hat = v_ / (1.0 - ADAM_B2 ** ADAM_STEP)
        g_ref[...] = g_
        d_ref[...] = -ADAM_LR * (m_hat / (jnp.sqrt(v_hat) + ADAM_EPS) + ADAM_WD * w_ref[...])
        nm_ref[...] = m_
        nv_ref[...] = v_

    blk = pl.BlockSpec((tr, C), lambda i: (i, 0))
    sds = jax.ShapeDtypeStruct((R, C), F32)
    return pl.pallas_call(
        body, grid=(R // tr,), in_specs=[pl.BlockSpec((n, tr, C), lambda i: (0, i, 0))] + [blk] * 3,
        out_specs=[blk] * 4, out_shape=[sds] * 4,
        compiler_params=_cp(("parallel",)), name=name)(parts, w, m, v)


SHARDED = {"ffa_w_in": (2, BF16), "ffa_w_out": (1, BF16), "mix_w_in": (2, BF16), "mla_w_uq": (2, BF16),
           "mla_w_ukv": (2, BF16), "mix_w_o": (1, BF16), "ffb_w_in": (2, BF16), "ffb_w_out": (1, BF16),
           "ple_w_gate": (1, BF16), "ple_w_proj": (2, BF16), "gdn_conv_w": (2, F32), "ln_g": (2, F32), "ln_b": (2, F32)}
FFN_SLOT = ("ffa_w_in", "ffa_w_out", "ffb_w_in", "ffb_w_out")
REPLICATED = ("gdn_a_log", "gdn_dt_bias", "gdn_norm_w", "mla_q_norm_w", "mla_kv_norm_w")
WEIGHTS = ("ffa_w_in", "ffa_w_out", "mix_w_in", "gdn_conv_w", "gdn_a_log", "gdn_dt_bias", "gdn_norm_w", "mla_q_norm_w",
           "mla_kv_norm_w", "mla_w_uq", "mla_w_ukv", "mix_w_o", "ffb_w_in", "ffb_w_out", "ln_g", "ln_b", "ple_w_gate",
           "ple_w_proj")


def _to_slots(full, axis):
    L, a, b = full.shape
    if axis == 2:
        return full.reshape(L, a, N_DEV, b // N_DEV).transpose(2, 0, 1, 3).reshape(N_DEV, L * a, b // N_DEV)
    return full.reshape(L, N_DEV, a // N_DEV, b).transpose(1, 0, 2, 3).reshape(N_DEV, L * a // N_DEV, b)


def _from_slots(slots, shard_shape, axis):
    L, a, b = shard_shape
    t = slots.reshape((N_DEV,) + tuple(shard_shape))
    if axis == 2:
        return t.transpose(1, 2, 0, 3).reshape(L, a, N_DEV * b)
    return t.transpose(1, 0, 2, 3).reshape(L, N_DEV * a, b)


def _view2d(t):
    return t.reshape(-1, t.shape[-1])


def _pad_heads(w, nh):
    K = w.shape[0]
    return jnp.pad(w.reshape(K, nh, GDN_D), ((0, 0), (0, 0), (0, LANES - GDN_D))).reshape(K, nh * LANES)


def _unpad_heads(w, nh):
    K = w.shape[0]
    return w.reshape(K, nh, LANES)[:, :, :GDN_D].reshape(K, nh * GDN_D)


IN_WIDTHS = (512, 512, 512, 512, 8, 8, 256, 256, 256, 256, 160)


def _split_in(w):
    offs = np.cumsum((0,) + IN_WIDTHS)
    return [w[:, int(offs[i]):int(offs[i + 1])] for i in range(len(IN_WIDTHS))]


def _pad_in_proj(w):
    gq, gk, gv, gz, ga, gb, sq, sk, sv, mq, mkv = _split_in(w)
    K = w.shape[0]
    gab = jnp.pad(jnp.concatenate([ga, gb], axis=1), ((0, 0), (0, LANES - 2 * GDN_HEADS)))
    return jnp.concatenate(
        [_pad_heads(t, GDN_HEADS) for t in (gq, gk, gv, gz)] + [_pad_heads(t, SB_HEADS) for t in (sq, sk, sv)]
        + [mq, jnp.pad(mkv, ((0, 0), (0, 2 * LANES - mkv.shape[1]))), gab], axis=1)


def _unpad_in_proj(wp):
    c = lambda cb, n: wp[:, cb * LANES:(cb + n) * LANES]
    gab = c(CB_GAB, 1)
    parts = [_unpad_heads(c(cb, GDN_HEADS), GDN_HEADS) for cb in (CB_GQ, CB_GK, CB_GV, CB_GZ)]
    parts += [gab[:, :GDN_HEADS], gab[:, GDN_HEADS:2 * GDN_HEADS]]
    parts += [_unpad_heads(c(cb, SB_HEADS), SB_HEADS) for cb in (CB_SQ, CB_SK, CB_SV)]
    parts += [c(CB_MQ, 2), c(CB_MKV, 2)[:, :MLA_KV_RANK + MLA_ROPE]]
    return jnp.concatenate(parts, axis=1)


def _pad_lanes(w, width):
    return jnp.pad(w, ((0, 0), (0, width - w.shape[1])))


def _mla_up_pad(w_uq, w_ukv):
    H = MLA_HEADS
    dq = MLA_NOPE + MLA_ROPE
    wuq = jnp.pad(w_uq.reshape(-1, H, dq), ((0, 0), (0, 0), (0, LANES - dq))).reshape(-1, H * LANES)
    kv = w_ukv.reshape(-1, H, MLA_NOPE + MLA_V)
    wuk = jnp.pad(kv[:, :, :MLA_NOPE], ((0, 0), (0, 0), (0, LANES - MLA_NOPE))).reshape(-1, H * LANES)
    wuv = jnp.pad(kv[:, :, MLA_NOPE:], ((0, 0), (0, 0), (0, LANES - MLA_V))).reshape(-1, H * LANES)
    return wuq, wuk, wuv


def _mla_up_unpad(dwuq, dwuk, dwuv):
    H = MLA_HEADS
    dq = MLA_NOPE + MLA_ROPE
    g_uq = dwuq.reshape(-1, H, LANES)[:, :, :dq].reshape(-1, H * dq)
    g_ukv = jnp.concatenate([dwuk.reshape(-1, H, LANES)[:, :, :MLA_NOPE], dwuv.reshape(-1, H, LANES)[:, :, :MLA_V]],
                            axis=2).reshape(-1, H * (MLA_NOPE + MLA_V))
    return g_uq, g_ukv


def _rope_tables(positions):
    inv = 1.0 / (ROPE_BASE ** (jnp.arange(0, MLA_ROPE, 2, dtype=F32) / MLA_ROPE))
    ang = positions.astype(F32)[:, None] * inv
    cos, sin = jnp.cos(ang), jnp.sin(ang)
    T = positions.shape[0]
    one = lambda n: jnp.ones((T, n), F32)
    zero = lambda n: jnp.zeros((T, n), F32)
    tail = LANES - MLA_NOPE - MLA_ROPE
    C = jnp.concatenate([one(MLA_NOPE), cos, cos, one(tail)], axis=1)
    Sn = jnp.concatenate([zero(MLA_NOPE), -sin, zero(HALF_ROPE + tail)], axis=1)
    Sp = jnp.concatenate([zero(MLA_NOPE + HALF_ROPE), sin, zero(tail)], axis=1)
    return C, Sn, Sp


def _layer_weights(full, i):
    W = {"layer": i}
    for tag in ("ffa", "ffb"):
        W[tag + "_in"], W[tag + "_out"] = full[tag + "_w_in"], full[tag + "_w_out"]
    W["win"] = _pad_in_proj(full["mix_w_in"][i])
    wo = full["mix_w_o"][i]
    W["wo"] = jnp.pad(wo.reshape(-1, GDN_D, wo.shape[1]), ((0, 0), (0, LANES - GDN_D), (0, 0))).reshape(-1, wo.shape[1])
    W["wuq"], W["wuk"], W["wuv"] = _mla_up_pad(full["mla_w_uq"][i], full["mla_w_ukv"][i])
    W["wg"], W["wp"] = full["ple_w_gate"][i], full["ple_w_proj"][i]
    W["conv"] = _pad_heads(full["gdn_conv_w"][i], GDN_QKV_BLOCKS)
    W["ln_g"] = [full["ln_g"][i, j][None, :] for j in range(3)]
    W["ln_b"] = [full["ln_b"][i, j][None, :] for j in range(3)]
    W["a_row"] = _pad_lanes(full["gdn_a_log"][i][None, :], LANES)
    W["dt_row"] = _pad_lanes(full["gdn_dt_bias"][i][None, :], LANES)
    W["nw"] = full["gdn_norm_w"][i][None, :]
    W["wq"] = full["mla_q_norm_w"][i][None, :]
    W["wkv"] = full["mla_kv_norm_w"][i][None, :]
    return W


def _layer_fwd(h0, p_i, W, rope, i):
    L = "L%d_" % i
    S = {"h0": h0, "p": p_i}
    S["h1"], S["xh1"], S["rs1"], S["Ga"], S["Ua"] = ffn_fwd(h0, W["ffa_in"], W["ffa_out"], i, W["ln_g"][0], W["ln_b"][0],
                                                            L + "ffa_fwd")
    S["proj"] = mm_nn(S["h1"], W["win"], L + "in_proj")
    S["qkv"] = gdn_pre_fwd(S["proj"], W["conv"], L + "gdn_pre_fwd")
    S["o_gdn"], S["opre"], S["Tm"], S["Sin"] = gdn_chunk_fwd(S["qkv"], S["proj"], W["a_row"], W["dt_row"], W["nw"],
                                                            L + "gdn_chunk_fwd")
    S["o_sb"], S["tot"] = sb_fwd(S["proj"], L + "sb_fwd")
    S["Q"], S["K"], S["V"] = mla_pre_fwd(S["proj"], W["wq"], W["wkv"], W["wuq"], W["wuk"], W["wuv"], *rope, L + "mla_pre_fwd")
    S["o_mla"], S["lse"] = mla_fwd(S["Q"], S["K"], S["V"], L + "mla_fwd")
    S["h2"], S["xh2"], S["rs2"] = proj_res_ln([S["o_gdn"], S["o_sb"], S["o_mla"]], W["wo"], S["h1"],
                                              W["ln_g"][1], W["ln_b"][1], L + "out_proj")
    S["h3"], S["xh3"], S["rs3"], S["Gb"], S["Ub"] = ffn_fwd(S["h2"], W["ffb_in"], W["ffb_out"], i, W["ln_g"][2], W["ln_b"][2],
                                                            L + "ffb_fwd")
    h4, S["a"], S["e"] = ple_fwd(S["h3"], p_i, W["wg"], W["wp"], L + "ple_fwd")
    return h4, S


def _layer_bwd(dh4, S, W, rope, i, bufs):
    L = "L%d_" % i
    G = {}
    bufs = dict(bufs)
    dh3, da, de = ple_bwd(dh4, S["a"], S["e"], W["wg"], L + "ple_bwd")
    G["ple_w_gate"] = mm_tn(S["h3"], da, L + "d_ple_gate")
    G["ple_w_proj"] = mm_tn(S["p"], de, L + "d_ple_proj")
    dz3, dg2, db2 = ln_bwd(dh3, S["xh3"], S["rs3"], W["ln_g"][2], L + "ln3_bwd")
    dh2, dGb, dUb, actb = ffn_bwd(dz3, S["Gb"], S["Ub"], W["ffb_in"], W["ffb_out"], i, L + "ffb_bwd")
    bufs["ffb_w_in"] = ffn_dw_in(S["h2"], dGb, dUb, i, DEPTH, bufs.get("ffb_w_in"), L + "d_ffb_in")
    bufs["ffb_w_out"] = ffn_dw_out(actb, dz3, i, DEPTH, bufs.get("ffb_w_out"), L + "d_ffb_out")
    dz2, dg1, db1 = ln_bwd(dh2, S["xh2"], S["rs2"], W["ln_g"][1], L + "ln2_bwd")
    docat = mm_nn(dz2, W["wo"], L + "d_ocat", b_transposed=True)
    dwo = jnp.concatenate([mm_tn(S["o_gdn"], dz2, L + "d_wo_gdn"), mm_tn(S["o_sb"], dz2, L + "d_wo_sb"),
                           mm_tn(S["o_mla"], dz2, L + "d_wo_mla")], axis=0)
    G["mix_w_o"] = dwo.reshape(-1, LANES, dwo.shape[1])[:, :GDN_D, :].reshape(-1, dwo.shape[1])
    dqkv, dgz, dgab, d_alog, d_dt, d_nw = gdn_chunk_bwd(S["qkv"], S["proj"], W["a_row"], W["dt_row"], W["nw"],
                                                        S["opre"], S["Tm"], S["Sin"], docat, L + "gdn_chunk_bwd")
    dc, dconv = gdn_pre_bwd_pointwise(S["proj"], W["conv"], dqkv, L + "gdn_pre_bwd")
    dxqkv = gdn_pre_bwd_conv(dc, W["conv"], L + "gdn_conv_bwd")
    G["gdn_conv_w"] = _unpad_heads(dconv, GDN_QKV_BLOCKS)
    G["gdn_a_log"], G["gdn_dt_bias"], G["gdn_norm_w"] = d_alog[0, :GDN_HEADS], d_dt[0, :GDN_HEADS], d_nw[0]
    dsq, dsk, dsv = sb_bwd(S["proj"], S["tot"], docat, GDN_HEADS, L + "sb_bwd")
    dQ, dK, dV = mla_bwd(S["Q"], S["K"], S["V"], S["o_mla"], S["lse"], docat, GDN_HEADS + SB_HEADS, L + "mla_bwd")
    dmq, dmkv, dwuq, dwuk, dwuv, dwq, dwkv = mla_pre_bwd(
        S["proj"], W["wq"], W["wkv"], W["wuq"], W["wuk"], W["wuv"], *rope, dQ, dK, dV, L + "mla_pre_bwd")
    G["mla_w_uq"], G["mla_w_ukv"] = _mla_up_unpad(dwuq, dwuk, dwuv)
    G["mla_q_norm_w"], G["mla_kv_norm_w"] = dwq[0], dwkv[0]
    dproj = jnp.concatenate([dxqkv, dgz, dsq, dsk, dsv, dmq, dmkv, dgab], axis=1).astype(BF16)
    G["mix_w_in"] = _unpad_in_proj(mm_tn(S["h1"], dproj, L + "d_in_proj"))
    dh1 = mm_nn(dproj, W["win"], L + "d_h1", res=dz2, res_scale=ALPHA, b_transposed=True)
    dz1, dg0, db0 = ln_bwd(dh1, S["xh1"], S["rs1"], W["ln_g"][0], L + "ln1_bwd")
    dh0, dGa, dUa, acta = ffn_bwd(dz1, S["Ga"], S["Ua"], W["ffa_in"], W["ffa_out"], i, L + "ffa_bwd")
    bufs["ffa_w_in"] = ffn_dw_in(S["h0"], dGa, dUa, i, DEPTH, bufs.get("ffa_w_in"), L + "d_ffa_in")
    bufs["ffa_w_out"] = ffn_dw_out(acta, dz1, i, DEPTH, bufs.get("ffa_w_out"), L + "d_ffa_out")
    G["ln_g"] = jnp.concatenate([dg0, dg1, dg2], axis=0)
    G["ln_b"] = jnp.concatenate([db0, db1, db2], axis=0)
    return dh0, G, bufs


def _local_step(x, p, positions, target, full):
    rope = _rope_tables(positions)
    Ws = [_layer_weights(full, i) for i in range(DEPTH)]
    h, saved = x, []
    for i in range(DEPTH):
        h, S = _layer_fwd(h, p[i], Ws[i], rope, i)
        saved.append(S)
    loss, dh = loss_head(h, target, "loss_head")
    grads, bufs = [None] * DEPTH, {}
    for i in reversed(range(DEPTH)):
        dh, grads[i], bufs = _layer_bwd(dh, saved[i], Ws[i], rope, i, bufs)
    return loss, dh, {n: jnp.stack([grads[i][n] for i in range(DEPTH)]) for n in WEIGHTS if n not in FFN_SLOT}, bufs


def kernel(x, p, positions, ffa_w_in, ffa_w_out, mix_w_in, gdn_conv_w, gdn_a_log, gdn_dt_bias, gdn_norm_w, mla_q_norm_w, mla_kv_norm_w, mla_w_uq, mla_w_ukv, mix_w_o, ffb_w_in, ffb_w_out, ln_g, ln_b, ple_w_gate, ple_w_proj, loss_target, m_ffa_w_in, m_ffa_w_out, m_mix_w_in, m_gdn_conv_w, m_gdn_a_log, m_gdn_dt_bias, m_gdn_norm_w, m_mla_q_norm_w, m_mla_kv_norm_w, m_mla_w_uq, m_mla_w_ukv, m_mix_w_o, m_ffb_w_in, m_ffb_w_out, m_ln_g, m_ln_b, m_ple_w_gate, m_ple_w_proj, v_ffa_w_in, v_ffa_w_out, v_mix_w_in, v_gdn_conv_w, v_gdn_a_log, v_gdn_dt_bias, v_gdn_norm_w, v_mla_q_norm_w, v_mla_kv_norm_w, v_mla_w_uq, v_mla_w_ukv, v_mix_w_o, v_ffb_w_in, v_ffb_w_out, v_ln_g, v_ln_b, v_ple_w_gate, v_ple_w_proj):
    given = dict(locals())
    shards = {n: given[n] for n in WEIGHTS}
    names = list(SHARDED)
    got = all_gather([_view2d(shards[n].astype(SHARDED[n][1])) for n in names], "gather_weights")
    full = {n: shards[n] for n in REPLICATED}
    for n, g in zip(names, got):
        if n in FFN_SLOT:
            full[n] = g.reshape((N_DEV,) + shards[n].shape)
        else:
            full[n] = _from_slots(g, shards[n].shape, SHARDED[n][0])
    loss, grad_x, G, bufs = _local_step(x[0], p[:, 0], positions[0], loss_target[0], full)
    loss = lax.psum(loss[0, 0], ("x", "y", "c"))
    parts = []
    for n in names:
        if n in FFN_SLOT:
            parts.append(bufs[n].reshape(N_DEV, -1, bufs[n].shape[-1]))
        else:
            parts.append(_to_slots(G[n], SHARDED[n][0]).astype(SHARDED[n][1]))
    received = dict(zip(names, exchange_partials(parts, "scatter_grads")))
    received.update(zip(REPLICATED, all_gather([G[n] for n in REPLICATED], "gather_replicated_grads")))
    grad, delta, new_m, new_v = {}, {}, {}, {}
    for n in WEIGHTS:
        shape = shards[n].shape
        outs = reduce_adamw(received[n], _view2d(shards[n]), _view2d(given["m_" + n]), _view2d(given["v_" + n]),
                            "adamw_" + n)
        grad[n], delta[n], new_m[n], new_v[n] = (t.reshape(shape) for t in outs)
    return (loss, grad_x[None], *[grad[n] for n in WEIGHTS], *[delta[n] for n in WEIGHTS],
            *[new_m[n] for n in WEIGHTS], *[new_v[n] for n in WEIGHTS])
```

```python
import functools
import numpy as np
import jax
import jax.numpy as jnp
from jax import lax
from jax.experimental import pallas as pl
from jax.experimental.pallas import tpu as pltpu

F32 = jnp.float32
BF16 = jnp.bfloat16

DEPTH = 2
LN_EPS = 1e-5
RMS_EPS = 1e-6
ALPHA = (2 * DEPTH) ** 0.25
GDN_HEADS, GDN_D, GDN_CONV, GDN_CHUNK = 8, 64, 4, 64
SB_HEADS, SB_DIM = 4, 64
MLA_HEADS, MLA_NOPE, MLA_ROPE, MLA_V, MLA_Q_RANK, MLA_KV_RANK = 4, 64, 32, 64, 256, 128
ROPE_BASE = 10000.0
HALF_ROPE = MLA_ROPE // 2
LANES = 128
N_DEV = 8
ADAM_LR, ADAM_B1, ADAM_B2, ADAM_EPS, ADAM_WD, ADAM_STEP = 0.001, 0.9, 0.999, 1e-08, 0.01, 10

CB_GQ, CB_GK, CB_GV, CB_GZ = 0, 8, 16, 24
CB_SQ, CB_SK, CB_SV = 32, 36, 40
CB_MQ, CB_MKV, CB_GAB = 44, 46, 48
PROJ_W = 49 * LANES
VMEM_LIMIT = 56 * 1024 * 1024

NT_DIMS = (((1,), (1,)), ((), ()))
TN_DIMS = (((0,), (0,)), ((), ()))


def _cp(sem):
    return pltpu.CompilerParams(dimension_semantics=sem, vmem_limit_bytes=VMEM_LIMIT)


def _bdot(a, b):
    return jnp.dot(a.astype(BF16), b.astype(BF16), preferred_element_type=F32)


def _bdot_nt(a, b):
    return lax.dot_general(a.astype(BF16), b.astype(BF16), NT_DIMS, preferred_element_type=F32)


def _bdot_tn(a, b):
    return lax.dot_general(a.astype(BF16), b.astype(BF16), TN_DIMS, preferred_element_type=F32)


def _split2(a):
    hi = a.astype(BF16)
    lo = (a - hi.astype(F32)).astype(BF16)
    return hi, lo


def _hdot(a, b):
    m = a.shape[0]
    a_hi, a_lo = _split2(a)
    b_hi, b_lo = _split2(b)
    r = jnp.dot(jnp.concatenate([a_hi, a_lo], axis=0), b_hi, preferred_element_type=F32)
    return r[:m] + r[m:] + jnp.dot(a_hi, b_lo, preferred_element_type=F32)


def _hdot_tn(a, b):
    a_hi, a_lo = _split2(a)
    b_hi, b_lo = _split2(b)
    d = functools.partial(lax.dot_general, dimension_numbers=TN_DIMS, preferred_element_type=F32)
    return d(a_hi, b_hi) + d(a_lo, b_hi) + d(a_hi, b_lo)


def _ones_dot(x, ones_bf16):
    hi = x.astype(BF16)
    r1 = x - hi.astype(F32)
    mid = r1.astype(BF16)
    lo = (r1 - mid.astype(F32)).astype(BF16)
    d = functools.partial(jnp.dot, preferred_element_type=F32)
    return d(hi, ones_bf16) + d(mid, ones_bf16) + d(lo, ones_bf16)


def _ones_dot_left(ones_bf16, x):
    hi = x.astype(BF16)
    r1 = x - hi.astype(F32)
    mid = r1.astype(BF16)
    lo = (r1 - mid.astype(F32)).astype(BF16)
    d = functools.partial(jnp.dot, preferred_element_type=F32)
    return d(ones_bf16, hi) + d(ones_bf16, mid) + d(ones_bf16, lo)


def _iota2(shape, dim):
    return lax.broadcasted_iota(jnp.int32, shape, dim)


def _sigmoid(x):
    return 1.0 / (1.0 + jnp.exp(-x))


def _softplus(x):
    return jnp.maximum(x, 0.0) + jnp.log(1.0 + jnp.exp(-jnp.abs(x)))


def _pick(n, limit, mult):
    if n <= limit:
        return n
    best = None
    for t in range(mult, limit + 1, mult):
        if n % t == 0:
            best = t
    assert best is not None, (n, limit, mult)
    return best


def mm_nn(a, b, name, out_dtype=F32, res=None, res_scale=1.0, b_transposed=False):
    M, K = a.shape
    N = b.shape[0] if b_transposed else b.shape[1]
    tm, tn, tk = _pick(M, 512, 16), _pick(N, 1024, LANES), _pick(K, 1024, LANES)
    nk = K // tk
    has_res = res is not None
    dot = _bdot_nt if b_transposed else _bdot

    def body(*refs):
        if has_res:
            a_ref, b_ref, r_ref, o_ref, acc_ref = refs
        else:
            a_ref, b_ref, o_ref, acc_ref = refs
        k = pl.program_id(2)

        @pl.when(k == 0)
        def _():
            acc_ref[...] = jnp.zeros_like(acc_ref)

        acc_ref[...] += dot(a_ref[...], b_ref[...])

        @pl.when(k == nk - 1)
        def _():
            out = acc_ref[...]
            if has_res:
                out = out + res_scale * r_ref[...]
            o_ref[...] = out.astype(o_ref.dtype)

    b_spec = pl.BlockSpec((tn, tk), lambda i, j, k: (j, k)) if b_transposed else pl.BlockSpec((tk, tn), lambda i, j, k: (k, j))
    in_specs = [pl.BlockSpec((tm, tk), lambda i, j, k: (i, k)), b_spec]
    args = [a, b]
    if has_res:
        in_specs.append(pl.BlockSpec((tm, tn), lambda i, j, k: (i, j)))
        args.append(res)
    return pl.pallas_call(
        body, grid=(M // tm, N // tn, nk), in_specs=in_specs,
        out_specs=pl.BlockSpec((tm, tn), lambda i, j, k: (i, j)),
        out_shape=jax.ShapeDtypeStruct((M, N), out_dtype),
        scratch_shapes=[pltpu.VMEM((tm, tn), F32)],
        compiler_params=_cp(("parallel", "parallel", "arbitrary")), name=name)(*args)


def mm_tn(a, b, name, out_dtype=F32, a_transposed=False):
    K, T = a.shape if a_transposed else a.shape[::-1]
    _, N = b.shape
    tk, tn, tt = _pick(K, 512, LANES), _pick(N, 1024, LANES), _pick(T, 512, LANES)
    nt = T // tt
    dot = _bdot if a_transposed else _bdot_tn

    def body(a_ref, b_ref, o_ref, acc_ref):
        t = pl.program_id(2)

        @pl.when(t == 0)
        def _():
            acc_ref[...] = jnp.zeros_like(acc_ref)

        acc_ref[...] += dot(a_ref[...], b_ref[...])

        @pl.when(t == nt - 1)
        def _():
            o_ref[...] = acc_ref[...].astype(o_ref.dtype)

    a_spec = pl.BlockSpec((tk, tt), lambda i, j, t: (i, t)) if a_transposed else pl.BlockSpec((tt, tk), lambda i, j, t: (t, i))
    return pl.pallas_call(
        body, grid=(K // tk, N // tn, nt),
        in_specs=[a_spec, pl.BlockSpec((tt, tn), lambda i, j, t: (t, j))],
        out_specs=pl.BlockSpec((tk, tn), lambda i, j, t: (i, j)),
        out_shape=jax.ShapeDtypeStruct((K, N), out_dtype),
        scratch_shapes=[pltpu.VMEM((tk, tn), F32)],
        compiler_params=_cp(("parallel", "parallel", "arbitrary")), name=name)(a, b)


def _ln_apply(z, g, b):
    mu = jnp.mean(z, axis=-1, keepdims=True)
    zc = z - mu
    var = jnp.mean(zc * zc, axis=-1, keepdims=True)
    rstd = lax.rsqrt(var + LN_EPS)
    xhat = zc * rstd
    return xhat * g + b, xhat, rstd


def ln_bwd(dout, xhat, rstd, g, name):
    T, D = dout.shape
    tm = _pick(T, 512, 8)

    def body(do_ref, xh_ref, rs_ref, g_ref, dz_ref, dg_ref, db_ref):
        i = pl.program_id(0)

        @pl.when(i == 0)
        def _():
            dg_ref[...] = jnp.zeros_like(dg_ref)
            db_ref[...] = jnp.zeros_like(db_ref)

        do = do_ref[...]
        xh = xh_ref[...]
        dxh = do * g_ref[...]
        m1 = jnp.mean(dxh, axis=-1, keepdims=True)
        m2 = jnp.mean(dxh * xh, axis=-1, keepdims=True)
        dz_ref[...] = rs_ref[...] * (dxh - m1 - xh * m2)
        dg_ref[...] += jnp.sum(do * xh, axis=0, keepdims=True)
        db_ref[...] += jnp.sum(do, axis=0, keepdims=True)

    row = pl.BlockSpec((tm, D), lambda i: (i, 0))
    vec = pl.BlockSpec((1, D), lambda i: (0, 0))
    return pl.pallas_call(
        body, grid=(T // tm,),
        in_specs=[row, row, pl.BlockSpec((tm, 1), lambda i: (i, 0)), vec],
        out_specs=[row, vec, vec],
        out_shape=[jax.ShapeDtypeStruct((T, D), F32), jax.ShapeDtypeStruct((1, D), F32), jax.ShapeDtypeStruct((1, D), F32)],
        compiler_params=_cp(("arbitrary",)), name=name)(dout, xhat, rstd, g)


FFN_CHUNKS = N_DEV // 2


def ffn_fwd(h, w_in, w_out, layer, g, b, name):
    T, D = h.shape
    fc = w_in.shape[3]
    half = w_out.shape[2]
    tm = _pick(T, 512, 8)
    nc = FFN_CHUNKS

    def body(h_ref, wg_ref, wu_ref, wo_ref, g_ref, b_ref, out_ref, xh_ref, rs_ref, G_ref, U_ref, acc_ref):
        c = pl.program_id(1)

        @pl.when(c == 0)
        def _():
            acc_ref[...] = jnp.zeros_like(acc_ref)

        hb = h_ref[...].astype(BF16)
        G = jnp.dot(hb, wg_ref[0, 0], preferred_element_type=F32)
        U = jnp.dot(hb, wu_ref[0, 0], preferred_element_type=F32)
        G_ref[0] = G
        U_ref[0] = U
        act = G * _sigmoid(G) * U
        acc_ref[...] += _bdot(act, wo_ref[:, 0].reshape(2 * half, D))

        @pl.when(c == nc - 1)
        def _():
            z = ALPHA * h_ref[...] + 0.5 * acc_ref[...]
            out, xh, rs = _ln_apply(z, g_ref[...], b_ref[...])
            out_ref[...] = out
            xh_ref[...] = xh
            rs_ref[...] = rs

    row = pl.BlockSpec((tm, D), lambda i, c: (i, 0))
    vec = pl.BlockSpec((1, D), lambda i, c: (0, 0))
    cblk = pl.BlockSpec((1, tm, fc), lambda i, c: (c, i, 0))
    csds = jax.ShapeDtypeStruct((nc, T, fc), F32)
    return pl.pallas_call(
        body, grid=(T // tm, nc),
        in_specs=[row, pl.BlockSpec((1, 1, D, fc), lambda i, c: (c, layer, 0, 0)),
                  pl.BlockSpec((1, 1, D, fc), lambda i, c: (c + nc, layer, 0, 0)),
                  pl.BlockSpec((2, 1, half, D), lambda i, c: (c, layer, 0, 0)), vec, vec],
        out_specs=[row, row, pl.BlockSpec((tm, 1), lambda i, c: (i, 0)), cblk, cblk],
        out_shape=[jax.ShapeDtypeStruct((T, D), F32), jax.ShapeDtypeStruct((T, D), F32), jax.ShapeDtypeStruct((T, 1), F32),
                   csds, csds],
        scratch_shapes=[pltpu.VMEM((tm, D), F32)],
        compiler_params=_cp(("parallel", "arbitrary")), name=name)(h, w_in, w_in, w_out, g, b)


def ffn_bwd(dz, G, U, w_in, w_out, layer, name):
    T, D = dz.shape
    nc, _, fc = G.shape
    half = w_out.shape[2]
    tm = _pick(T, 512, 16)

    def body(dz_ref, G_ref, U_ref, wg_ref, wu_ref, wo_ref, dh_ref, dG_ref, dU_ref, act_ref, acc_ref):
        c = pl.program_id(1)

        @pl.when(c == 0)
        def _():
            acc_ref[...] = jnp.zeros_like(acc_ref)

        dy = (0.5 * dz_ref[...]).astype(BF16)
        dact = _bdot_nt(dy, wo_ref[:, 0].reshape(2 * half, D))
        G = G_ref[0]
        U = U_ref[0]
        s = _sigmoid(G)
        silu = G * s
        dG = (dact * U * (s * (1.0 + G * (1.0 - s)))).astype(BF16)
        dU = (dact * silu).astype(BF16)
        dG_ref[0] = dG
        dU_ref[0] = dU
        act_ref[0] = (silu * U).astype(BF16)
        acc_ref[...] += _bdot_nt(dG, wg_ref[0, 0]) + _bdot_nt(dU, wu_ref[0, 0])

        @pl.when(c == nc - 1)
        def _():
            dh_ref[...] = ALPHA * dz_ref[...] + acc_ref[...]

    row = pl.BlockSpec((tm, D), lambda i, c: (i, 0))
    cblk = pl.BlockSpec((1, tm, fc), lambda i, c: (c, i, 0))
    csds = jax.ShapeDtypeStruct((nc, T, fc), BF16)
    return pl.pallas_call(
        body, grid=(T // tm, nc),
        in_specs=[row, cblk, cblk, pl.BlockSpec((1, 1, D, fc), lambda i, c: (c, layer, 0, 0)),
                  pl.BlockSpec((1, 1, D, fc), lambda i, c: (c + nc, layer, 0, 0)),
                  pl.BlockSpec((2, 1, half, D), lambda i, c: (c, layer, 0, 0))],
        out_specs=[row, cblk, cblk, cblk],
        out_shape=[jax.ShapeDtypeStruct((T, D), F32), csds, csds, csds],
        scratch_shapes=[pltpu.VMEM((tm, D), F32)],
        compiler_params=_cp(("parallel", "arbitrary")), name=name)(dz, G, U, w_in, w_in, w_out)


def _with_prev(prev, n_in):
    if prev is None:
        return [], [], {}
    return [prev], [pl.BlockSpec(memory_space=pl.ANY)], {n_in: 0}


def ffn_dw_in(h_t, dG, dU, layer, n_layers, prev, name):
    D, T = h_t.shape
    nc, _, fc = dG.shape
    tk, tt = _pick(D, 512, LANES), _pick(T, 512, LANES)
    nt = T // tt
    extra, extra_specs, alias = _with_prev(prev, 3)

    def body(h_ref, dG_ref, dU_ref, *rest):
        o_ref, acc_ref = rest[-2:]
        s = pl.program_id(0)
        t = pl.program_id(2)

        @pl.when(t == 0)
        def _():
            acc_ref[...] = jnp.zeros_like(acc_ref)

        @pl.when(s < nc)
        def _():
            acc_ref[...] += jnp.dot(h_ref[...], dG_ref[0], preferred_element_type=F32)

        @pl.when(s >= nc)
        def _():
            acc_ref[...] += jnp.dot(h_ref[...], dU_ref[0], preferred_element_type=F32)

        @pl.when(t == nt - 1)
        def _():
            o_ref[0, 0] = acc_ref[...].astype(o_ref.dtype)

    return pl.pallas_call(
        body, grid=(2 * nc, D // tk, nt),
        in_specs=[pl.BlockSpec((tk, tt), lambda s, k, t: (k, t)),
                  pl.BlockSpec((1, tt, fc), lambda s, k, t: (jnp.minimum(s, nc - 1), t, 0)),
                  pl.BlockSpec((1, tt, fc), lambda s, k, t: (jnp.maximum(s - nc, 0), t, 0))] + extra_specs,
        out_specs=pl.BlockSpec((1, 1, tk, fc), lambda s, k, t: (s, layer, k, 0)),
        out_shape=jax.ShapeDtypeStruct((2 * nc, n_layers, D, fc), BF16),
        scratch_shapes=[pltpu.VMEM((tk, fc), F32)],
        input_output_aliases=alias,
        compiler_params=_cp(("parallel", "parallel", "arbitrary")), name=name)(h_t, dG, dU, *extra)


def ffn_dw_out(act, dz, layer, n_layers, prev, name):
    nc, T, fc = act.shape
    D = dz.shape[1]
    half = fc // 2
    tt = _pick(T, 512, 16)
    nt = T // tt
    extra, extra_specs, alias = _with_prev(prev, 2)

    def body(a_ref, dz_ref, *rest):
        o_ref, acc_ref = rest[-2:]
        t = pl.program_id(1)

        @pl.when(t == 0)
        def _():
            acc_ref[...] = jnp.zeros_like(acc_ref)

        acc_ref[...] += _bdot_tn(a_ref[0], dz_ref[...])

        @pl.when(t == nt - 1)
        def _():
            o_ref[:, 0] = (0.5 * acc_ref[...]).reshape(2, half, D).astype(o_ref.dtype)

    return pl.pallas_call(
        body, grid=(nc, nt),
        in_specs=[pl.BlockSpec((1, tt, fc), lambda c, t: (c, t, 0)), pl.BlockSpec((tt, D), lambda c, t: (t, 0))] + extra_specs,
        out_specs=pl.BlockSpec((2, 1, half, D), lambda c, t: (c, layer, 0, 0)),
        out_shape=jax.ShapeDtypeStruct((2 * nc, n_layers, half, D), BF16),
        scratch_shapes=[pltpu.VMEM((fc, D), F32)],
        input_output_aliases=alias,
        compiler_params=_cp(("parallel", "arbitrary")), name=name)(act, dz, *extra)


def proj_res_ln(parts, w, res, g, b, name):
    T, D = res.shape
    tm = _pick(T, 512, 8)
    widths = [p.shape[1] for p in parts]
    offs = [int(sum(widths[:i])) for i in range(len(parts))]
    n = len(parts)

    def body(*refs):
        p_refs = refs[:n]
        w_ref, r_ref, g_ref, b_ref, out_ref, xh_ref, rs_ref = refs[n:]
        acc = ALPHA * r_ref[...]
        for p_ref, o, wd in zip(p_refs, offs, widths):
            acc = acc + _bdot(p_ref[...], w_ref[o:o + wd, :])
        out, xh, rs = _ln_apply(acc, g_ref[...], b_ref[...])
        out_ref[...] = out
        xh_ref[...] = xh
        rs_ref[...] = rs

    row = pl.BlockSpec((tm, D), lambda i: (i, 0))
    vec = pl.BlockSpec((1, D), lambda i: (0, 0))
    return pl.pallas_call(
        body, grid=(T // tm,),
        in_specs=[pl.BlockSpec((tm, wd), lambda i: (i, 0)) for wd in widths]
        + [pl.BlockSpec(w.shape, lambda i: (0, 0)), row, vec, vec],
        out_specs=[row, row, pl.BlockSpec((tm, 1), lambda i: (i, 0))],
        out_shape=[jax.ShapeDtypeStruct((T, D), F32), jax.ShapeDtypeStruct((T, D), F32), jax.ShapeDtypeStruct((T, 1), F32)],
        compiler_params=_cp(("parallel",)), name=name)(*parts, w, res, g, b)


def ple_fwd(h, p, wg, wp, name):
    T, D = h.shape
    P = p.shape[1]
    tm, tn = _pick(T, 512, 8), _pick(D, 512, LANES)

    def body(h_ref, hn_ref, p_ref, wg_ref, wp_ref, out_ref, a_ref, e_ref):
        a = _bdot(h_ref[...], wg_ref[...])
        e = _bdot(p_ref[...], wp_ref[...])
        a_ref[...] = a
        e_ref[...] = e
        out_ref[...] = hn_ref[...] + _sigmoid(a) * e

    blk = pl.BlockSpec((tm, tn), lambda i, j: (i, j))
    sds = jax.ShapeDtypeStruct((T, D), F32)
    return pl.pallas_call(
        body, grid=(T // tm, D // tn),
        in_specs=[pl.BlockSpec((tm, D), lambda i, j: (i, 0)), blk, pl.BlockSpec((tm, P), lambda i, j: (i, 0)),
                  pl.BlockSpec((D, tn), lambda i, j: (0, j)), pl.BlockSpec((P, tn), lambda i, j: (0, j))],
        out_specs=[blk, blk, blk], out_shape=[sds, sds, sds],
        compiler_params=_cp(("parallel", "parallel")), name=name)(h, h, p, wg, wp)


def ple_bwd(dout, a, e, wg, name):
    T, D = dout.shape
    tm = _pick(T, 512, 16)

    def body(do_ref, a_ref, e_ref, wg_ref, dh_ref, da_ref, de_ref):
        do = do_ref[...]
        s = _sigmoid(a_ref[...])
        da = (do * e_ref[...] * s * (1.0 - s)).astype(BF16)
        da_ref[...] = da
        de_ref[...] = (do * s).astype(BF16)
        dh_ref[...] = do + _bdot_nt(da, wg_ref[...])

    row = pl.BlockSpec((tm, D), lambda i: (i, 0))
    return pl.pallas_call(
        body, grid=(T // tm,),
        in_specs=[row, row, row, pl.BlockSpec((D, D), lambda i: (0, 0))],
        out_specs=[row, row, row],
        out_shape=[jax.ShapeDtypeStruct((T, D), F32), jax.ShapeDtypeStruct((T, D), BF16), jax.ShapeDtypeStruct((T, D), BF16)],
        compiler_params=_cp(("parallel",)), name=name)(dout, a, e, wg)


def loss_head(y, target, name):
    T, D = y.shape
    tm = _pick(T, 512, 8)

    def body(y_ref, t_ref, loss_ref, dy_ref):
        i = pl.program_id(0)

        @pl.when(i == 0)
        def _():
            loss_ref[...] = jnp.zeros_like(loss_ref)

        err = y_ref[...] - t_ref[...]
        dy_ref[...] = err * (1.0 / D)
        per_tok = jnp.sum(err * err, axis=-1, keepdims=True) * (1.0 / D)
        loss_ref[...] += 0.5 * jnp.sum(per_tok, axis=0, keepdims=True)

    row = pl.BlockSpec((tm, D), lambda i: (i, 0))
    return pl.pallas_call(
        body, grid=(T // tm,), in_specs=[row, row],
        out_specs=[pl.BlockSpec((1, 1), lambda i: (0, 0)), row],
        out_shape=[jax.ShapeDtypeStruct((1, 1), F32), jax.ShapeDtypeStruct((T, D), F32)],
        compiler_params=_cp(("arbitrary",)), name=name)(y, target)


GDN_QKV_BLOCKS = 3 * GDN_HEADS
HALO = 8


def _conv_taps(pad_ref, w_ref, tm, base):
    acc = w_ref[0:1, :] * pad_ref[pl.ds(base, tm), :]
    for k in range(1, GDN_CONV):
        acc = acc + w_ref[k:k + 1, :] * pad_ref[pl.ds(base + k, tm), :]
    return acc


GDN_GROUP_W = GDN_HEADS * LANES
GDN_PRE_ROWS = 256


def _head_sums(x):
    rows = x.shape[0]
    parts = [jnp.broadcast_to(jnp.sum(x[:, h * LANES:(h + 1) * LANES], axis=-1, keepdims=True), (rows, LANES))
             for h in range(x.shape[1] // LANES)]
    return jnp.concatenate(parts, axis=1)


def _gdn_pre_common(x_ref, halo_ref, w_ref, pad_ref, tm):
    i = pl.program_id(1)
    grp = pl.program_id(0)
    pad_ref[0:HALO, :] = jnp.where(i == 0, 0.0, halo_ref[...])
    pad_ref[HALO:HALO + tm, :] = x_ref[...]
    c = _conv_taps(pad_ref, w_ref, tm, HALO - (GDN_CONV - 1))
    s = _sigmoid(c)
    y = c * s
    r = lax.rsqrt(_head_sums(y * y) + RMS_EPS)
    scale = jnp.where(grp < 1, GDN_D ** -0.5, 1.0)
    return grp < 2, c, s, y, r, scale


def gdn_pre_fwd(proj, conv_w_p, name):
    T = proj.shape[0]
    tm = _pick(T, GDN_PRE_ROWS, 8)
    GW = GDN_GROUP_W

    def body(x_ref, halo_ref, w_ref, o_ref, pad_ref):
        normed, c, s, y, r, scale = _gdn_pre_common(x_ref, halo_ref, w_ref, pad_ref, tm)
        o_ref[...] = jnp.where(normed, y * r * scale, y)

    return pl.pallas_call(
        body, grid=(3, T // tm),
        in_specs=[pl.BlockSpec((tm, GW), lambda hb, i: (i, hb)),
                  pl.BlockSpec((HALO, GW), lambda hb, i: (jnp.maximum(i * (tm // HALO) - 1, 0), hb)),
                  pl.BlockSpec((GDN_CONV, GW), lambda hb, i: (0, hb))],
        out_specs=pl.BlockSpec((tm, GW), lambda hb, i: (i, hb)),
        out_shape=jax.ShapeDtypeStruct((T, 3 * GW), F32),
        scratch_shapes=[pltpu.VMEM((tm + HALO, GW), F32)],
        compiler_params=_cp(("parallel", "parallel")), name=name)(proj, proj, conv_w_p)


def gdn_pre_bwd_pointwise(proj, conv_w_p, dqkv, name):
    T = proj.shape[0]
    tm = _pick(T, GDN_PRE_ROWS, 8)
    GW = GDN_GROUP_W

    def body(x_ref, halo_ref, w_ref, d_ref, dc_ref, dw_ref, pad_ref):
        i = pl.program_id(1)
        normed, c, s, y, r, scale = _gdn_pre_common(x_ref, halo_ref, w_ref, pad_ref, tm)

        @pl.when(i == 0)
        def _():
            dw_ref[...] = jnp.zeros_like(dw_ref)

        d = d_ref[...]
        n = y * r
        dn = d * scale
        dy = jnp.where(normed, r * (dn - n * _head_sums(dn * n)), d)
        dc = dy * (s * (1.0 + c * (1.0 - s)))
        dc_ref[...] = dc
        for k in range(GDN_CONV):
            xs = pad_ref[pl.ds(HALO - (GDN_CONV - 1) + k, tm), :]
            dw_ref[k:k + 1, :] += jnp.sum(dc * xs, axis=0, keepdims=True)

    blk = pl.BlockSpec((tm, GW), lambda hb, i: (i, hb))
    wblk = pl.BlockSpec((GDN_CONV, GW), lambda hb, i: (0, hb))
    return pl.pallas_call(
        body, grid=(3, T // tm),
        in_specs=[blk, pl.BlockSpec((HALO, GW), lambda hb, i: (jnp.maximum(i * (tm // HALO) - 1, 0), hb)), wblk, blk],
        out_specs=[blk, wblk],
        out_shape=[jax.ShapeDtypeStruct((T, 3 * GW), F32), jax.ShapeDtypeStruct((GDN_CONV, 3 * GW), F32)],
        scratch_shapes=[pltpu.VMEM((tm + HALO, GW), F32)],
        compiler_params=_cp(("parallel", "arbitrary")), name=name)(proj, proj, conv_w_p, dqkv)


def gdn_pre_bwd_conv(dc, conv_w_p, name):
    T = dc.shape[0]
    tm = _pick(T, GDN_PRE_ROWS, 8)
    nt = T // tm
    GW = GDN_GROUP_W

    def body(dc_ref, halo_ref, w_ref, dx_ref, pad_ref):
        i = pl.program_id(1)
        pad_ref[0:tm, :] = dc_ref[...]
        pad_ref[tm:tm + HALO, :] = jnp.where(i == nt - 1, 0.0, halo_ref[...])
        acc = w_ref[GDN_CONV - 1:GDN_CONV, :] * pad_ref[pl.ds(0, tm), :]
        for k in range(GDN_CONV - 1):
            acc = acc + w_ref[k:k + 1, :] * pad_ref[pl.ds(GDN_CONV - 1 - k, tm), :]
        dx_ref[...] = acc

    blk = pl.BlockSpec((tm, GW), lambda hb, i: (i, hb))
    return pl.pallas_call(
        body, grid=(3, nt),
        in_specs=[blk, pl.BlockSpec((HALO, GW), lambda hb, i: (jnp.minimum((i + 1) * (tm // HALO), T // HALO - 1), hb)),
                  pl.BlockSpec((GDN_CONV, GW), lambda hb, i: (0, hb))],
        out_specs=blk,
        out_shape=jax.ShapeDtypeStruct((T, 3 * GW), F32),
        scratch_shapes=[pltpu.VMEM((tm + HALO, GW), F32)],
        compiler_params=_cp(("parallel", "parallel")), name=name)(dc, dc, conv_w_p)


def _chunk_masks(C):
    row = _iota2((C, C), 0)
    col = _iota2((C, C), 1)
    return row >= col, row > col, row == col


BNN = (((2,), (1,)), ((0,), (0,)))
BNT = (((2,), (2,)), ((0,), (0,)))
BTN = (((1,), (1,)), ((0,), (0,)))


def _bmm(a, b, dims=BNN):
    return lax.dot_general(a.astype(BF16), b.astype(BF16), dims, preferred_element_type=F32)


def _hbmm(a, b):
    m = a.shape[1]
    a_hi, a_lo = _split2(a)
    b_hi, b_lo = _split2(b)
    r = lax.dot_general(jnp.concatenate([a_hi, a_lo], axis=1), b_hi, BNN, preferred_element_type=F32)
    return r[:, :m] + r[:, m:] + lax.dot_general(a_hi, b_lo, BNN, preferred_element_type=F32)


def _hbmm_tn(a, b):
    a_hi, a_lo = _split2(a)
    b_hi, b_lo = _split2(b)
    d = functools.partial(lax.dot_general, dimension_numbers=BTN, preferred_element_type=F32)
    return d(a_hi, b_hi) + d(a_lo, b_hi) + d(a_hi, b_lo)


def _col_to_row(colv, eye):
    return jnp.sum(jnp.where(eye, colv, 0.0), axis=1, keepdims=True)


def _row_to_col(rowv, eye):
    return jnp.sum(jnp.where(eye, rowv, 0.0), axis=2, keepdims=True)


def _unit_lower_inverse(A, eye):
    C = A.shape[1]
    P = jnp.where(eye, 1.0, 0.0) - A
    Bp = _hbmm(A, A)
    for _ in range(4):
        R = _hbmm(jnp.concatenate([Bp, P], axis=1), Bp)
        Bp = R[:, :C]
        P = P + R[:, C:]
    return P + _hbmm(P, Bp)


def _stack_heads(ref, first_block, n, width=GDN_D):
    return jnp.stack([ref[:, pl.ds((first_block + h) * LANES, width)] for h in range(n)])


def _unstack_heads(ref, first_block, val):
    for h in range(val.shape[0]):
        ref[:, pl.ds((first_block + h) * LANES, val.shape[2])] = val[h]


def _gdn_gates(gab, a_row, dt_row, incl):
    g_all = -jnp.exp(a_row) * _softplus(gab + dt_row)
    beta_all = _sigmoid(gab)
    gc_all = _ones_dot_left(incl.astype(BF16), g_all)
    return g_all, beta_all, gc_all


def _gdn_common(qkv_ref, gc_all, beta_all, incl, strict, eye):
    C, H = GDN_CHUNK, GDN_HEADS
    q, k, v = (_stack_heads(qkv_ref, j * H, H) for j in range(3))
    gc = jnp.stack([gc_all[:, h:h + 1] for h in range(H)])
    beta = jnp.stack([beta_all[:, H + h:H + h + 1] for h in range(H)])
    gc_row = _col_to_row(gc, eye)
    decay = jnp.where(incl, jnp.exp(jnp.where(incl, gc - gc_row, 0.0)), 0.0)
    e_gc = jnp.exp(gc)
    gl = gc[:, C - 1:C, :]
    e_gl = jnp.exp(gl)
    ekd = jnp.exp(gl - gc)
    kb = k * beta
    A = jnp.where(strict, _bmm(kb, k, BNT) * decay, 0.0)
    Pm = jnp.where(incl, _bmm(q, k, BNT) * decay, 0.0)
    return q, k, v, gc, beta, decay, e_gc, e_gl, ekd, kb, A, Pm


def gdn_chunk_fwd(qkv, proj, a_row, dt_row, norm_w, name):
    T = qkv.shape[0]
    C, H, Dh = GDN_CHUNK, GDN_HEADS, GDN_D
    N = T // C

    def body(qkv_ref, gz_ref, gab_ref, a_ref, dt_ref, nw_ref, o_ref, opre_ref, Tm_ref, Sin_ref, S_ref):
        n = pl.program_id(0)

        @pl.when(n == 0)
        def _():
            S_ref[...] = jnp.zeros_like(S_ref)

        incl, strict, eye = _chunk_masks(C)
        _, beta_all, gc_all = _gdn_gates(gab_ref[...], a_ref[...], dt_ref[...], incl)
        o_ref[...] = jnp.zeros_like(o_ref)
        opre_ref[...] = jnp.zeros_like(opre_ref)
        q, k, v, gc, beta, decay, e_gc, e_gl, ekd, kb, A, Pm = _gdn_common(qkv_ref, gc_all, beta_all, incl, strict, eye)
        Tm = _unit_lower_inverse(A, eye)
        u = _hbmm(Tm, v * beta)
        w = _hbmm(Tm, kb * e_gc)
        S = S_ref[...]
        v_new = u - _bmm(w, S)
        o = _bmm(q * e_gc, S) + _bmm(Pm, v_new)
        S_ref[...] = S * e_gl + _bmm(k * ekd, v_new, BTN)
        Sin_ref[0] = S
        Tm_ref[0] = Tm
        r = lax.rsqrt(jnp.mean(o * o, axis=-1, keepdims=True) + RMS_EPS)
        gz = _stack_heads(gz_ref, 0, H)
        _unstack_heads(opre_ref, 0, o)
        _unstack_heads(o_ref, 0, o * r * nw_ref[...] * (gz * _sigmoid(gz)))

    vec = pl.BlockSpec((1, LANES), lambda n: (0, 0))
    hblk = pl.BlockSpec((C, H * LANES), lambda n: (n, 0))
    sblk = pl.BlockSpec((1, H, Dh, Dh), lambda n: (n, 0, 0, 0))
    return pl.pallas_call(
        body, grid=(N,),
        in_specs=[pl.BlockSpec((C, GDN_QKV_BLOCKS * LANES), lambda n: (n, 0)),
                  pl.BlockSpec((C, H * LANES), lambda n: (n, CB_GZ // H)),
                  pl.BlockSpec((C, LANES), lambda n: (n, CB_GAB)), vec, vec, pl.BlockSpec((1, Dh), lambda n: (0, 0))],
        out_specs=[hblk, hblk, sblk, sblk],
        out_shape=[jax.ShapeDtypeStruct((T, H * LANES), F32), jax.ShapeDtypeStruct((T, H * LANES), F32),
                   jax.ShapeDtypeStruct((N, H, Dh, Dh), F32), jax.ShapeDtypeStruct((N, H, Dh, Dh), F32)],
        scratch_shapes=[pltpu.VMEM((H, Dh, Dh), F32)],
        compiler_params=_cp(("arbitrary",)), name=name)(qkv, proj, proj, a_row, dt_row, norm_w)


def gdn_chunk_bwd(qkv, proj, a_row, dt_row, norm_w, opre, Tm_all, Sin_all, docat, name):
    T = qkv.shape[0]
    C, H, Dh = GDN_CHUNK, GDN_HEADS, GDN_D
    N = T // C

    def body(qkv_ref, gz_ref, gab_ref, a_ref, dt_ref, nw_ref, opre_ref, Tm_ref, Sin_ref, do_ref,
             dqkv_ref, dgz_ref, dgab_ref, da_ref, ddt_ref, dnw_ref, dS_ref):
        n = pl.program_id(0)

        @pl.when(n == 0)
        def _():
            dS_ref[...] = jnp.zeros_like(dS_ref)
            da_ref[...] = jnp.zeros_like(da_ref)
            ddt_ref[...] = jnp.zeros_like(ddt_ref)
            dnw_ref[...] = jnp.zeros_like(dnw_ref)

        incl, strict, eye = _chunk_masks(C)
        gab = gab_ref[...]
        g_all, beta_all, gc_all = _gdn_gates(gab, a_ref[...], dt_ref[...], incl)
        lane = _iota2((C, LANES), 1)
        rowi = _iota2((C, 1), 0)
        dqkv_ref[...] = jnp.zeros_like(dqkv_ref)
        dgz_ref[...] = jnp.zeros_like(dgz_ref)
        nw = nw_ref[...]
        q, k, v, gc, beta, decay, e_gc, e_gl, ekd, kb, A, Pm = _gdn_common(qkv_ref, gc_all, beta_all, incl, strict, eye)
        Tm = Tm_ref[0]
        S = Sin_ref[0]
        dS = dS_ref[...]
        kbe = kb * e_gc
        u = _hbmm(Tm, v * beta)
        w = _hbmm(Tm, kbe)
        qd = q * e_gc
        kd = k * ekd
        v_new = u - _bmm(w, S)
        o = _stack_heads(opre_ref, 0, H)
        gz = _stack_heads(gz_ref, 0, H)
        don = _stack_heads(do_ref, 0, H)
        r = lax.rsqrt(jnp.mean(o * o, axis=-1, keepdims=True) + RMS_EPS)
        nn = o * r
        sgz = _sigmoid(gz)
        silu = gz * sgz
        _unstack_heads(dgz_ref, 0, don * nn * nw * (sgz * (1.0 + gz * (1.0 - sgz))))
        dnn = don * nw * silu
        dnw_ref[...] += jnp.sum(jnp.sum(don * nn * silu, axis=0), axis=0, keepdims=True)
        do = r * (dnn - nn * jnp.mean(dnn * nn, axis=-1, keepdims=True))
        dv_new = _bmm(Pm, do, BTN) + _bmm(kd, dS)
        dPm = jnp.where(incl, _bmm(do, v_new, BNT), 0.0)
        dqd = _bmm(do, S, BNT)
        dkd = _bmm(v_new, dS, BNT)
        dS_ref[...] = _bmm(qd, do, BTN) + e_gl * dS - _bmm(w, dv_new, BTN)
        dgl = jnp.sum(jnp.sum(dS * S, axis=2, keepdims=True), axis=1, keepdims=True) * e_gl
        dw = -_bmm(dv_new, S, BNT)
        dvb = _hbmm_tn(Tm, dv_new)
        dkbe = _hbmm_tn(Tm, dw)
        dA = -jnp.where(strict, _bmm(dvb, u, BNT) + _bmm(dkbe, w, BNT), 0.0)
        dAD = dA * decay
        dPD = dPm * decay
        Gm = dA * A + dPm * Pm
        dgc = jnp.sum(Gm, axis=2, keepdims=True) - _row_to_col(jnp.sum(Gm, axis=1, keepdims=True), eye)
        dkb = _bmm(dAD, k) + dkbe * e_gc
        dk = _bmm(dAD, kb, BTN) + _bmm(dPD, q, BTN) + dkd * ekd + dkb * beta
        dq = _bmm(dPD, k) + dqd * e_gc
        tkd = jnp.sum(dkd * kd, axis=-1, keepdims=True)
        dgc = dgc + jnp.sum(dqd * qd, axis=-1, keepdims=True) - tkd + jnp.sum(dkbe * kbe, axis=-1, keepdims=True)
        dgl = dgl + jnp.sum(tkd, axis=1, keepdims=True)
        dgc = dgc + jnp.where(rowi == C - 1, dgl, 0.0)
        dbeta = jnp.sum(dvb * v, axis=-1, keepdims=True) + jnp.sum(dkb * k, axis=-1, keepdims=True)
        _unstack_heads(dqkv_ref, 0, dq)
        _unstack_heads(dqkv_ref, H, dk)
        _unstack_heads(dqkv_ref, 2 * H, dvb * beta)
        dgc_all = jnp.zeros((C, LANES), F32)
        dbeta_all = jnp.zeros((C, LANES), F32)
        for h in range(H):
            dgc_all = dgc_all + jnp.where(lane == h, dgc[h], 0.0)
            dbeta_all = dbeta_all + jnp.where(lane == H + h, dbeta[h], 0.0)
        upper = (_iota2((C, C), 0) <= _iota2((C, C), 1)).astype(BF16)
        dg_all = _ones_dot_left(upper, dgc_all)
        dga = dg_all * (-jnp.exp(a_ref[...])) * _sigmoid(gab + dt_ref[...])
        dgb = dbeta_all * beta_all * (1.0 - beta_all)
        dgab_ref[...] = jnp.where(lane < H, dga, jnp.where(lane < 2 * H, dgb, 0.0))
        da_ref[...] += jnp.sum(jnp.where(lane < H, dg_all * g_all, 0.0), axis=0, keepdims=True)
        ddt_ref[...] += jnp.sum(jnp.where(lane < H, dga, 0.0), axis=0, keepdims=True)

    rev = lambda n: N - 1 - n
    vec = pl.BlockSpec((1, LANES), lambda n: (0, 0))
    nwv = pl.BlockSpec((1, Dh), lambda n: (0, 0))
    hblk = pl.BlockSpec((C, H * LANES), lambda n: (rev(n), 0))
    sblk = pl.BlockSpec((1, H, Dh, Dh), lambda n: (rev(n), 0, 0, 0))
    qblk = pl.BlockSpec((C, GDN_QKV_BLOCKS * LANES), lambda n: (rev(n), 0))
    return pl.pallas_call(
        body, grid=(N,),
        in_specs=[qblk, pl.BlockSpec((C, H * LANES), lambda n: (rev(n), CB_GZ // H)),
                  pl.BlockSpec((C, LANES), lambda n: (rev(n), CB_GAB)), vec, vec, nwv, hblk, sblk, sblk, hblk],
        out_specs=[qblk, hblk, pl.BlockSpec((C, LANES), lambda n: (rev(n), 0)), vec, vec, nwv],
        out_shape=[jax.ShapeDtypeStruct((T, GDN_QKV_BLOCKS * LANES), F32), jax.ShapeDtypeStruct((T, H * LANES), F32),
                   jax.ShapeDtypeStruct((T, LANES), F32), jax.ShapeDtypeStruct((1, LANES), F32),
                   jax.ShapeDtypeStruct((1, LANES), F32), jax.ShapeDtypeStruct((1, Dh), F32)],
        scratch_shapes=[pltpu.VMEM((H, Dh, Dh), F32)],
        compiler_params=_cp(("arbitrary",)), name=name)(qkv, proj, proj, a_row, dt_row, norm_w, opre, Tm_all, Sin_all, docat)


ATT_BQ, ATT_BK = 256, 512
NEG_BIG = -1e30


def _att_blocks(T):
    bq, bk = min(ATT_BQ, T), min(ATT_BK, T)
    assert bk % bq == 0 and T % bk == 0
    return bq, bk


def _att_specs(T, bq, cbs):
    qspec = lambda cb: pl.BlockSpec((bq, LANES), lambda h, i: (i, cb + h))
    kspec = lambda cb: pl.BlockSpec((T, LANES), lambda h, i: (0, cb + h))
    return qspec, kspec


def _kblock(ref, kb, bk):
    return ref[pl.ds(pl.multiple_of(kb * bk, bk), bk), :]


def _att_pos(i, kb, bq, bk):
    qpos = i * bq + _iota2((bq, bk), 0)
    kpos = kb * bk + _iota2((bq, bk), 1)
    return qpos, kpos


def _suffix_sum(x):
    n = x.shape[1]
    lane = _iota2(x.shape, 1)
    d = 1
    while d < n:
        x = x + jnp.where(lane < n - d, pltpu.roll(x, n - d, 1), 0.0)
        d *= 2
    return x


def _prefix_sum(x):
    n = x.shape[1]
    lane = _iota2(x.shape, 1)
    d = 1
    while d < n:
        x = x + jnp.where(lane >= d, pltpu.roll(x, d, 1), 0.0)
        d *= 2
    return x


SB_BLOCK = 256
SB_DEAD = -104.0


def _sb_blocks(T):
    b = min(SB_BLOCK, T)
    assert T % b == 0 and T // b <= LANES
    return b, b


def sb_fwd(proj, name):
    T = proj.shape[0]
    H = SB_HEADS
    bq, bk = _sb_blocks(T)
    scale = SB_DIM ** -0.5

    def body(q_ref, k_ref, v_ref, o_ref, tot_ref):
        i = pl.program_id(1)
        qb = q_ref[...].astype(BF16)
        diag = (i * bq) // bk
        lane = _iota2((bq, LANES), 1)

        def block(kb, acc, R, masked):
            z = _bdot_nt(qb, _kblock(k_ref, kb, bk)) * scale
            sp = _softplus(z)
            if masked:
                qpos, kpos = _att_pos(i, kb, bq, bk)
                mask = kpos < qpos
                l1m = jnp.where(mask, -sp, 0.0)
            else:
                l1m = -sp
            W = jnp.exp((z - sp) + (_suffix_sum(l1m) - l1m) + R)
            if masked:
                W = jnp.where(mask, W, 0.0)
            acc = acc + _bdot(W, _kblock(v_ref, kb, bk))
            return acc, R + jnp.sum(l1m, axis=-1, keepdims=True)

        acc, R = block(diag, jnp.zeros((bq, LANES), F32), jnp.zeros((bq, 1), F32), True)

        def live(c):
            return jnp.logical_and(c[0] >= 0, jnp.max(c[2]) > SB_DEAD)

        def step(c):
            kb, acc, R, Rb = c
            acc, R_next = block(kb, acc, R, False)
            return kb - 1, acc, R_next, jnp.where(lane == kb, R, Rb)

        _, acc, _, Rb = lax.while_loop(live, step, (diag - 1, acc, R, jnp.where(lane == diag, 0.0, NEG_BIG)))
        o_ref[...] = acc
        tot_ref[...] = Rb

    qspec, kspec = _att_specs(T, bq, None)
    sds = jax.ShapeDtypeStruct((T, H * LANES), F32)
    oblk = pl.BlockSpec((bq, LANES), lambda h, i: (i, h))
    return pl.pallas_call(
        body, grid=(H, T // bq), in_specs=[qspec(CB_SQ), kspec(CB_SK), kspec(CB_SV)],
        out_specs=[oblk, oblk], out_shape=[sds, sds],
        compiler_params=_cp(("parallel", "parallel")), name=name)(proj, proj, proj)


def sb_bwd(proj, tot, docat, do_cb, name):
    T = proj.shape[0]
    H = SB_HEADS
    bq, bk = _sb_blocks(T)
    scale = SB_DIM ** -0.5

    def body(q_ref, k_ref, v_ref, tot_ref, do_ref, dq_ref, dk_ref, dv_ref):
        i = pl.program_id(1)

        @pl.when(i == 0)
        def _():
            dk_ref[...] = jnp.zeros_like(dk_ref)
            dv_ref[...] = jnp.zeros_like(dv_ref)

        qb = q_ref[...].astype(BF16)
        dob = do_ref[...].astype(BF16)
        Rb = tot_ref[...]
        diag = (i * bq) // bk
        lane = _iota2((bq, LANES), 1)
        first = lax.while_loop(
            lambda kb: jnp.logical_and(kb < diag, jnp.max(jnp.where(lane == kb, Rb, NEG_BIG)) <= SB_DEAD),
            lambda kb: kb + 1, jnp.int32(0))

        def block(kb, carry, masked):
            dq, Epre = carry
            R = jnp.sum(jnp.where(lane == kb, Rb, 0.0), axis=1, keepdims=True)
            kblk = _kblock(k_ref, kb, bk).astype(BF16)
            z = _bdot_nt(qb, kblk) * scale
            sp = _softplus(z)
            if masked:
                qpos, kpos = _att_pos(i, kb, bq, bk)
                mask = kpos < qpos
                l1m = jnp.where(mask, -sp, 0.0)
            else:
                l1m = -sp
            W = jnp.exp((z - sp) + (_suffix_sum(l1m) - l1m) + R)
            if masked:
                W = jnp.where(mask, W, 0.0)
            E = _bdot_nt(dob, _kblock(v_ref, kb, bk)) * W
            cexcl = (_prefix_sum(E) - E) + Epre
            neg = jnp.exp(-sp)
            dz = E * neg - cexcl * (1.0 - neg)
            if masked:
                dz = jnp.where(mask, dz, 0.0)
            dz = (dz * scale).astype(BF16)
            rows = pl.ds(pl.multiple_of(kb * bk, bk), bk)
            dk_ref[rows, :] += lax.dot_general(dz, qb, TN_DIMS, preferred_element_type=F32)
            dv_ref[rows, :] += lax.dot_general(W.astype(BF16), dob, TN_DIMS, preferred_element_type=F32)
            dq = dq + jnp.dot(dz, kblk, preferred_element_type=F32)
            return dq, Epre + jnp.sum(E, axis=-1, keepdims=True)

        init = (jnp.zeros((bq, LANES), F32), jnp.zeros((bq, 1), F32))
        carry = lax.fori_loop(first, diag, lambda kb, c: block(kb, c, False), init)
        dq, _ = block(diag, carry, True)
        dq_ref[...] = dq

    qspec, kspec = _att_specs(T, bq, None)
    sds = jax.ShapeDtypeStruct((T, H * LANES), F32)
    oblk = pl.BlockSpec((bq, LANES), lambda h, i: (i, h))
    kout = pl.BlockSpec((T, LANES), lambda h, i: (0, h))
    return pl.pallas_call(
        body, grid=(H, T // bq),
        in_specs=[qspec(CB_SQ), kspec(CB_SK), kspec(CB_SV), oblk, qspec(do_cb)],
        out_specs=[oblk, kout, kout], out_shape=[sds, sds, sds],
        compiler_params=_cp(("arbitrary", "arbitrary")), name=name)(proj, proj, proj, tot, docat)


def mla_fwd(Q, K, V, name):
    T = Q.shape[0]
    H = MLA_HEADS
    bq, bk = _att_blocks(T)
    scale = (MLA_NOPE + MLA_ROPE) ** -0.5

    def body(q_ref, k_ref, v_ref, o_ref, lse_ref):
        i = pl.program_id(1)
        qb = q_ref[...]
        diag = (i * bq) // bk

        def block(kb, carry, masked):
            acc, m, l = carry
            s = _bdot_nt(qb, _kblock(k_ref, kb, bk)) * scale
            if masked:
                qpos, kpos = _att_pos(i, kb, bq, bk)
                s = jnp.where(kpos <= qpos, s, NEG_BIG)
            m_new = jnp.maximum(m, jnp.max(s, axis=-1, keepdims=True))
            p = jnp.exp(s - m_new)
            corr = jnp.exp(m - m_new)
            acc = corr * acc + _bdot(p, _kblock(v_ref, kb, bk))
            return acc, m_new, corr * l + jnp.sum(p, axis=-1, keepdims=True)

        init = (jnp.zeros((bq, LANES), F32), jnp.full((bq, 1), NEG_BIG, F32), jnp.zeros((bq, 1), F32))
        carry = lax.fori_loop(0, diag, lambda kb, c: block(kb, c, False), init)
        acc, m, l = block(diag, carry, True)
        o_ref[...] = acc / l
        lse_ref[...] = jnp.broadcast_to(m + jnp.log(l), (bq, LANES))

    qspec, kspec = _att_specs(T, bq, None)
    sds = jax.ShapeDtypeStruct((T, H * LANES), F32)
    oblk = pl.BlockSpec((bq, LANES), lambda h, i: (i, h))
    return pl.pallas_call(
        body, grid=(H, T // bq), in_specs=[qspec(0), kspec(0), kspec(0)],
        out_specs=[oblk, oblk], out_shape=[sds, sds],
        compiler_params=_cp(("parallel", "parallel")), name=name)(Q, K, V)


def mla_bwd(Q, K, V, o, lse, docat, do_cb, name):
    T = Q.shape[0]
    H = MLA_HEADS
    bq, bk = _att_blocks(T)
    scale = (MLA_NOPE + MLA_ROPE) ** -0.5

    def body(q_ref, k_ref, v_ref, o_ref, lse_ref, do_ref, dq_ref, dk_ref, dv_ref):
        i = pl.program_id(1)

        @pl.when(i == 0)
        def _():
            dk_ref[...] = jnp.zeros_like(dk_ref)
            dv_ref[...] = jnp.zeros_like(dv_ref)

        qb = q_ref[...]
        do = do_ref[...]
        dob = do.astype(BF16)
        delta = jnp.sum(do * o_ref[...], axis=-1, keepdims=True)
        lse = lse_ref[:, 0:1]

        diag = (i * bq) // bk

        def block(kb, dq, masked):
            kblk = _kblock(k_ref, kb, bk)
            s = _bdot_nt(qb, kblk) * scale
            if masked:
                qpos, kpos = _att_pos(i, kb, bq, bk)
                s = jnp.where(kpos <= qpos, s, NEG_BIG)
            p = jnp.exp(s - lse)
            dp = _bdot_nt(dob, _kblock(v_ref, kb, bk))
            ds = (p * (dp - delta) * scale).astype(BF16)
            rows = pl.ds(pl.multiple_of(kb * bk, bk), bk)
            dk_ref[rows, :] += lax.dot_general(ds, qb, TN_DIMS, preferred_element_type=F32)
            dv_ref[rows, :] += lax.dot_general(p.astype(BF16), dob, TN_DIMS, preferred_element_type=F32)
            return dq + jnp.dot(ds, kblk, preferred_element_type=F32)

        dq = lax.fori_loop(0, diag, lambda kb, c: block(kb, c, False), jnp.zeros((bq, LANES), F32))
        dq_ref[...] = block(diag, dq, True)

    qspec, kspec = _att_specs(T, bq, None)
    sds = jax.ShapeDtypeStruct((T, H * LANES), F32)
    oblk = pl.BlockSpec((bq, LANES), lambda h, i: (i, h))
    kout = pl.BlockSpec((T, LANES), lambda h, i: (0, h))
    return pl.pallas_call(
        body, grid=(H, T // bq),
        in_specs=[qspec(0), kspec(0), kspec(0), oblk, oblk, qspec(do_cb)],
        out_specs=[oblk, kout, kout], out_shape=[sds, sds, sds],
        compiler_params=_cp(("arbitrary", "arbitrary")), name=name)(Q, K, V, o, lse, docat)


def _tile_heads(t, n):
    return jnp.concatenate([t] * n, axis=1)


def _rope(X, C, Sn, Sp):
    n = X.shape[1]
    return X * C + pltpu.roll(X, n - HALF_ROPE, 1) * Sn + pltpu.roll(X, HALF_ROPE, 1) * Sp


def _rope_t(dO, C, Sn, Sp):
    n = dO.shape[1]
    return dO * C + pltpu.roll(dO * Sn, HALF_ROPE, 1) + pltpu.roll(dO * Sp, n - HALF_ROPE, 1)


def _rms(x, w):
    r = lax.rsqrt(jnp.mean(x * x, axis=-1, keepdims=True) + RMS_EPS)
    xh = x * r
    return r, xh, xh * w


def _rms_bwd(dn, w, r, xh):
    dxh = dn * w
    return r * (dxh - xh * jnp.mean(dxh * xh, axis=-1, keepdims=True)), jnp.sum(dn * xh, axis=0, keepdims=True)


def _mla_pre_specs(T, tm):
    KV = MLA_KV_RANK
    QR = MLA_Q_RANK
    W = MLA_HEADS * LANES
    full = lambda shape: pl.BlockSpec(shape, lambda i: (0, 0))
    specs = [pl.BlockSpec((tm, QR), lambda i: (i, CB_MQ * LANES // QR)),
             pl.BlockSpec((tm, 2 * LANES), lambda i: (i, CB_MKV // 2)),
             full((1, QR)), full((1, KV))]
    rope = [pl.BlockSpec((tm, LANES), lambda i: (i, 0))] * 3
    return specs, rope, full, W


def mla_pre_fwd(proj, wq, wkv, wuq, wuk, wuv, ropeC, ropeSn, ropeSp, name):
    T = proj.shape[0]
    tm = _pick(T, 512, 16)
    KV = MLA_KV_RANK
    H = MLA_HEADS

    def body(mq_ref, mkv_ref, wq_ref, wkv_ref, wuq_ref, wuk_ref, wuv_ref, c_ref, sn_ref, sp_ref, Q_ref, K_ref, V_ref):
        C, Sn, Sp = (_tile_heads(t[...], H) for t in (c_ref, sn_ref, sp_ref))
        _, _, qn = _rms(mq_ref[...], wq_ref[...])
        Q_ref[...] = _rope(_bdot(qn, wuq_ref[...]), C, Sn, Sp).astype(BF16)
        mkv = mkv_ref[...]
        _, _, kvn = _rms(mkv[:, :KV], wkv_ref[...])
        kr = pltpu.roll(mkv[:, KV:], MLA_NOPE, 1)
        K_ref[...] = _rope(_bdot(kvn, wuk_ref[...]) + _tile_heads(kr, H), C, Sn, Sp).astype(BF16)
        V_ref[...] = _bdot(kvn, wuv_ref[...]).astype(BF16)

    specs, rope, full, W = _mla_pre_specs(T, tm)
    oblk = pl.BlockSpec((tm, W), lambda i: (i, 0))
    sds = jax.ShapeDtypeStruct((T, W), BF16)
    return pl.pallas_call(
        body, grid=(T // tm,),
        in_specs=specs + [full(wuq.shape), full(wuk.shape), full(wuv.shape)] + rope,
        out_specs=[oblk, oblk, oblk], out_shape=[sds, sds, sds],
        compiler_params=_cp(("parallel",)), name=name)(proj, proj, wq, wkv, wuq, wuk, wuv, ropeC, ropeSn, ropeSp)


def mla_pre_bwd(proj, wq, wkv, wuq, wuk, wuv, ropeC, ropeSn, ropeSp, dQ, dK, dV, name):
    T = proj.shape[0]
    tm = _pick(T, 512, 16)
    KV = MLA_KV_RANK
    H = MLA_HEADS

    def body(mq_ref, mkv_ref, wq_ref, wkv_ref, wuq_ref, wuk_ref, wuv_ref,
             c_ref, sn_ref, sp_ref, dQ_ref, dK_ref, dV_ref,
             dmq_ref, dmkv_ref, dwuq_ref, dwuk_ref, dwuv_ref, dwq_ref, dwkv_ref):
        i = pl.program_id(0)

        @pl.when(i == 0)
        def _():
            for ref in (dwuq_ref, dwuk_ref, dwuv_ref, dwq_ref, dwkv_ref):
                ref[...] = jnp.zeros_like(ref)

        C, Sn, Sp = (_tile_heads(t[...], H) for t in (c_ref, sn_ref, sp_ref))
        rq, xq, qn = _rms(mq_ref[...], wq_ref[...])
        mkv = mkv_ref[...]
        rkv, xkv, kvn = _rms(mkv[:, :KV], wkv_ref[...])
        dqf = _rope_t(dQ_ref[...], C, Sn, Sp)
        dkf = _rope_t(dK_ref[...], C, Sn, Sp)
        dv = dV_ref[...]
        dwuq_ref[...] += _bdot_tn(qn, dqf)
        dwuk_ref[...] += _bdot_tn(kvn, dkf)
        dwuv_ref[...] += _bdot_tn(kvn, dv)
        dmq, dwq = _rms_bwd(_bdot_nt(dqf, wuq_ref[...]), wq_ref[...], rq, xq)
        dckv, dwkv = _rms_bwd(_bdot_nt(dkf, wuk_ref[...]) + _bdot_nt(dv, wuv_ref[...]), wkv_ref[...], rkv, xkv)
        dwq_ref[...] += dwq
        dwkv_ref[...] += dwkv
        dmq_ref[...] = dmq
        dkr = dkf[:, 0:LANES]
        for h in range(1, H):
            dkr = dkr + dkf[:, h * LANES:(h + 1) * LANES]
        dkr = pltpu.roll(dkr, LANES - MLA_NOPE, 1)
        dkr = jnp.where(_iota2(dkr.shape, 1) < MLA_ROPE, dkr, 0.0)
        dmkv_ref[...] = jnp.concatenate([dckv, dkr], axis=1)

    specs, rope, full, W = _mla_pre_specs(T, tm)
    wide = pl.BlockSpec((tm, W), lambda i: (i, 0))
    return pl.pallas_call(
        body, grid=(T // tm,),
        in_specs=specs + [full(w.shape) for w in (wuq, wuk, wuv)] + rope + [wide, wide, wide],
        out_specs=[pl.BlockSpec((tm, MLA_Q_RANK), lambda i: (i, 0)), pl.BlockSpec((tm, 2 * LANES), lambda i: (i, 0)),
                   full(wuq.shape), full(wuk.shape), full(wuv.shape), full((1, MLA_Q_RANK)), full((1, KV))],
        out_shape=[jax.ShapeDtypeStruct((T, MLA_Q_RANK), F32), jax.ShapeDtypeStruct((T, 2 * LANES), F32),
                   jax.ShapeDtypeStruct(wuq.shape, F32), jax.ShapeDtypeStruct(wuk.shape, F32),
                   jax.ShapeDtypeStruct(wuv.shape, F32), jax.ShapeDtypeStruct((1, MLA_Q_RANK), F32),
                   jax.ShapeDtypeStruct((1, KV), F32)],
        compiler_params=_cp(("arbitrary",)), name=name)(
            proj, proj, wq, wkv, wuq, wuk, wuv, ropeC, ropeSn, ropeSp, dQ, dK, dV)


MESH = pl.DeviceIdType.MESH
ANY = pl.BlockSpec(memory_space=pl.ANY)


def _place():
    return lax.axis_index("x"), lax.axis_index("y"), lax.axis_index("c")


def all_gather(shards, name):
    n = len(shards)

    def body(*refs):
        x_refs, out_refs = refs[:n], refs[n:2 * n]
        send_sems, recv_sems, local_sems = refs[2 * n:]
        x, y, c = _place()
        me, sibling = (x, y, c), (x, y, 1 - c)
        chips = [(1 - x, y), (x, 1 - y), (1 - x, 1 - y)]

        def slot(a, px, py, pc):
            return out_refs[a].at[4 * px + 2 * py + pc]

        def copy(a, k, block, to, src=None):
            return pltpu.make_async_remote_copy(
                src_ref=slot(a, *block) if src is None else src, dst_ref=slot(a, *block),
                send_sem=send_sems.at[a, k], recv_sem=recv_sems.at[a, k], device_id=to, device_id_type=MESH)

        mine = [pltpu.make_async_copy(x_refs[a], slot(a, *me), local_sems.at[a]) for a in range(n)]
        first = []
        for a in range(n):
            mine[a].start()
            first.append(copy(a, 0, me, sibling, src=x_refs[a]))
            first += [copy(a, 1 + j, me, (*chip, c), src=x_refs[a]) for j, chip in enumerate(chips)]
        for cp in first:
            cp.start()
        passed = []
        for j, chip in enumerate(chips):
            for a in range(n):
                copy(a, 1 + j, (*chip, c), me).wait_recv()
                passed.append(copy(a, 4 + j, (*chip, c), sibling))
                passed[-1].start()
        for a in range(n):
            copy(a, 0, sibling, me).wait_recv()
            for j, chip in enumerate(chips):
                copy(a, 4 + j, (*chip, 1 - c), me).wait_recv()
        for cp in first + passed:
            cp.wait_send()
        for cp in mine:
            cp.wait()

    return pl.pallas_call(
        body, out_shape=[jax.ShapeDtypeStruct((N_DEV,) + s.shape, s.dtype) for s in shards],
        in_specs=[ANY] * n, out_specs=[ANY] * n,
        scratch_shapes=[pltpu.SemaphoreType.DMA((n, 7)), pltpu.SemaphoreType.DMA((n, 7)), pltpu.SemaphoreType.DMA((n,))],
        name=name)(*shards)


def exchange_partials(parts, name):
    n = len(parts)

    def body(*refs):
        src_refs, dst_refs = refs[:n], refs[n:2 * n]
        send_sems, recv_sems, local_sems = refs[2 * n:]
        x, y, c = _place()
        me = 4 * x + 2 * y + c
        copies = []
        mine = []
        for a in range(n):
            mine.append(pltpu.make_async_copy(src_refs[a].at[me], dst_refs[a].at[me], local_sems.at[a]))
            for k in range(1, N_DEV):
                px = 1 - x if k & 4 else x
                py = 1 - y if k & 2 else y
                pc = 1 - c if k & 1 else c
                copies.append(pltpu.make_async_remote_copy(
                    src_ref=src_refs[a].at[4 * px + 2 * py + pc], dst_ref=dst_refs[a].at[me],
                    send_sem=send_sems.at[a, k - 1], recv_sem=recv_sems.at[a, k - 1],
                    device_id=(px, py, pc), device_id_type=MESH))
        for cp in mine + copies:
            cp.start()
        for cp in copies:
            cp.wait_recv()
        for cp in copies:
            cp.wait_send()
        for cp in mine:
            cp.wait()

    return pl.pallas_call(
        body, out_shape=[jax.ShapeDtypeStruct(p.shape, p.dtype) for p in parts],
        in_specs=[ANY] * n, out_specs=[ANY] * n,
        scratch_shapes=[pltpu.SemaphoreType.DMA((n, 7)), pltpu.SemaphoreType.DMA((n, 7)), pltpu.SemaphoreType.DMA((n,))],
        name=name)(*parts)


def reduce_adamw(parts, w, m, v, name):
    n, R, C = parts.shape
    tr = R if R * C <= 256 * 1024 else _pick(R, 256, 16)

    def body(p_ref, w_ref, m_ref, v_ref, g_ref, d_ref, nm_ref, nv_ref):
        g_ = p_ref[0].astype(F32)
        for s in range(1, n):
            g_ = g_ + p_ref[s].astype(F32)
        m_ = ADAM_B1 * m_ref[...] + (1.0 - ADAM_B1) * g_
        v_ = ADAM_B2 * v_ref[...] + (1.0 - ADAM_B2) * (g_ * g_)
        m_hat = m_ / (1.0 - ADAM_B1 ** ADAM_STEP)
        v_hat = v_ / (1.0 - ADAM_B2 ** ADAM_STEP)
        g_ref[...] = g_
        d_ref[...] = -ADAM_LR * (m_hat / (jnp.sqrt(v_hat) + ADAM_EPS) + ADAM_WD * w_ref[...])
        nm_ref[...] = m_
        nv_ref[...] = v_

    blk = pl.BlockSpec((tr, C), lambda i: (i, 0))
    sds = jax.ShapeDtypeStruct((R, C), F32)
    return pl.pallas_call(
        body, grid=(R // tr,), in_specs=[pl.BlockSpec((n, tr, C), lambda i: (0, i, 0))] + [blk] * 3,
        out_specs=[blk] * 4, out_shape=[sds] * 4,
        compiler_params=_cp(("parallel",)), name=name)(parts, w, m, v)


SHARDED = {"ffa_w_in": (2, BF16), "ffa_w_out": (1, BF16), "mix_w_in": (2, BF16), "mla_w_uq": (2, BF16),
           "mla_w_ukv": (2, BF16), "mix_w_o": (1, BF16), "ffb_w_in": (2, BF16), "ffb_w_out": (1, BF16),
           "ple_w_gate": (1, BF16), "ple_w_proj": (2, BF16), "gdn_conv_w": (2, F32), "ln_g": (2, F32), "ln_b": (2, F32)}
FFN_SLOT = ("ffa_w_in", "ffa_w_out", "ffb_w_in", "ffb_w_out")
REPLICATED = ("gdn_a_log", "gdn_dt_bias", "gdn_norm_w", "mla_q_norm_w", "mla_kv_norm_w")
WEIGHTS = ("ffa_w_in", "ffa_w_out", "mix_w_in", "gdn_conv_w", "gdn_a_log", "gdn_dt_bias", "gdn_norm_w", "mla_q_norm_w",
           "mla_kv_norm_w", "mla_w_uq", "mla_w_ukv", "mix_w_o", "ffb_w_in", "ffb_w_out", "ln_g", "ln_b", "ple_w_gate",
           "ple_w_proj")


def _to_slots(full, axis):
    L, a, b = full.shape
    if axis == 2:
        return full.reshape(L, a, N_DEV, b // N_DEV).transpose(2, 0, 1, 3).reshape(N_DEV, L * a, b // N_DEV)
    return full.reshape(L, N_DEV, a // N_DEV, b).transpose(1, 0, 2, 3).reshape(N_DEV, L * a // N_DEV, b)


def _from_slots(slots, shard_shape, axis):
    L, a, b = shard_shape
    t = slots.reshape((N_DEV,) + tuple(shard_shape))
    if axis == 2:
        return t.transpose(1, 2, 0, 3).reshape(L, a, N_DEV * b)
    return t.transpose(1, 0, 2, 3).reshape(L, N_DEV * a, b)


def _view2d(t):
    return t.reshape(-1, t.shape[-1])


def _pad_heads(w, nh):
    K = w.shape[0]
    return jnp.pad(w.reshape(K, nh, GDN_D), ((0, 0), (0, 0), (0, LANES - GDN_D))).reshape(K, nh * LANES)


def _unpad_heads(w, nh):
    K = w.shape[0]
    return w.reshape(K, nh, LANES)[:, :, :GDN_D].reshape(K, nh * GDN_D)


IN_WIDTHS = (512, 512, 512, 512, 8, 8, 256, 256, 256, 256, 160)


def _split_in(w):
    offs = np.cumsum((0,) + IN_WIDTHS)
    return [w[:, int(offs[i]):int(offs[i + 1])] for i in range(len(IN_WIDTHS))]


def _pad_in_proj(w):
    gq, gk, gv, gz, ga, gb, sq, sk, sv, mq, mkv = _split_in(w)
    K = w.shape[0]
    gab = jnp.pad(jnp.concatenate([ga, gb], axis=1), ((0, 0), (0, LANES - 2 * GDN_HEADS)))
    return jnp.concatenate(
        [_pad_heads(t, GDN_HEADS) for t in (gq, gk, gv, gz)] + [_pad_heads(t, SB_HEADS) for t in (sq, sk, sv)]
        + [mq, jnp.pad(mkv, ((0, 0), (0, 2 * LANES - mkv.shape[1]))), gab], axis=1)


def _unpad_in_proj(wp):
    c = lambda cb, n: wp[:, cb * LANES:(cb + n) * LANES]
    gab = c(CB_GAB, 1)
    parts = [_unpad_heads(c(cb, GDN_HEADS), GDN_HEADS) for cb in (CB_GQ, CB_GK, CB_GV, CB_GZ)]
    parts += [gab[:, :GDN_HEADS], gab[:, GDN_HEADS:2 * GDN_HEADS]]
    parts += [_unpad_heads(c(cb, SB_HEADS), SB_HEADS) for cb in (CB_SQ, CB_SK, CB_SV)]
    parts += [c(CB_MQ, 2), c(CB_MKV, 2)[:, :MLA_KV_RANK + MLA_ROPE]]
    return jnp.concatenate(parts, axis=1)


def _pad_lanes(w, width):
    return jnp.pad(w, ((0, 0), (0, width - w.shape[1])))


def _mla_up_pad(w_uq, w_ukv):
    H = MLA_HEADS
    dq = MLA_NOPE + MLA_ROPE
    wuq = jnp.pad(w_uq.reshape(-1, H, dq), ((0, 0), (0, 0), (0, LANES - dq))).reshape(-1, H * LANES)
    kv = w_ukv.reshape(-1, H, MLA_NOPE + MLA_V)
    wuk = jnp.pad(kv[:, :, :MLA_NOPE], ((0, 0), (0, 0), (0, LANES - MLA_NOPE))).reshape(-1, H * LANES)
    wuv = jnp.pad(kv[:, :, MLA_NOPE:], ((0, 0), (0, 0), (0, LANES - MLA_V))).reshape(-1, H * LANES)
    return wuq, wuk, wuv


def _mla_up_unpad(dwuq, dwuk, dwuv):
    H = MLA_HEADS
    dq = MLA_NOPE + MLA_ROPE
    g_uq = dwuq.reshape(-1, H, LANES)[:, :, :dq].reshape(-1, H * dq)
    g_ukv = jnp.concatenate([dwuk.reshape(-1, H, LANES)[:, :, :MLA_NOPE], dwuv.reshape(-1, H, LANES)[:, :, :MLA_V]],
                            axis=2).reshape(-1, H * (MLA_NOPE + MLA_V))
    return g_uq, g_ukv


def _rope_tables(positions):
    inv = 1.0 / (ROPE_BASE ** (jnp.arange(0, MLA_ROPE, 2, dtype=F32) / MLA_ROPE))
    ang = positions.astype(F32)[:, None] * inv
    cos, sin = jnp.cos(ang), jnp.sin(ang)
    T = positions.shape[0]
    one = lambda n: jnp.ones((T, n), F32)
    zero = lambda n: jnp.zeros((T, n), F32)
    tail = LANES - MLA_NOPE - MLA_ROPE
    C = jnp.concatenate([one(MLA_NOPE), cos, cos, one(tail)], axis=1)
    Sn = jnp.concatenate([zero(MLA_NOPE), -sin, zero(HALF_ROPE + tail)], axis=1)
    Sp = jnp.concatenate([zero(MLA_NOPE + HALF_ROPE), sin, zero(tail)], axis=1)
    return C, Sn, Sp


def _layer_weights(full, i):
    W = {"layer": i}
    for tag in ("ffa", "ffb"):
        W[tag + "_in"], W[tag + "_out"] = full[tag + "_w_in"], full[tag + "_w_out"]
    W["win"] = _pad_in_proj(full["mix_w_in"][i])
    wo = full["mix_w_o"][i]
    W["wo"] = jnp.pad(wo.reshape(-1, GDN_D, wo.shape[1]), ((0, 0), (0, LANES - GDN_D), (0, 0))).reshape(-1, wo.shape[1])
    W["wuq"], W["wuk"], W["wuv"] = _mla_up_pad(full["mla_w_uq"][i], full["mla_w_ukv"][i])
    W["wg"], W["wp"] = full["ple_w_gate"][i], full["ple_w_proj"][i]
    W["conv"] = _pad_heads(full["gdn_conv_w"][i], GDN_QKV_BLOCKS)
    W["ln_g"] = [full["ln_g"][i, j][None, :] for j in range(3)]
    W["ln_b"] = [full["ln_b"][i, j][None, :] for j in range(3)]
    W["a_row"] = _pad_lanes(full["gdn_a_log"][i][None, :], LANES)
    W["dt_row"] = _pad_lanes(full["gdn_dt_bias"][i][None, :], LANES)
    W["nw"] = full["gdn_norm_w"][i][None, :]
    W["wq"] = full["mla_q_norm_w"][i][None, :]
    W["wkv"] = full["mla_kv_norm_w"][i][None, :]
    return W


def _layer_fwd(h0, p_i, W, rope, i):
    L = "L%d_" % i
    S = {"h0": h0, "p": p_i}
    S["h1"], S["xh1"], S["rs1"], S["Ga"], S["Ua"] = ffn_fwd(h0, W["ffa_in"], W["ffa_out"], i, W["ln_g"][0], W["ln_b"][0],
                                                            L + "ffa_fwd")
    S["proj"] = mm_nn(S["h1"], W["win"], L + "in_proj")
    S["qkv"] = gdn_pre_fwd(S["proj"], W["conv"], L + "gdn_pre_fwd")
    S["o_gdn"], S["opre"], S["Tm"], S["Sin"] = gdn_chunk_fwd(S["qkv"], S["proj"], W["a_row"], W["dt_row"], W["nw"],
                                                            L + "gdn_chunk_fwd")
    S["o_sb"], S["tot"] = sb_fwd(S["proj"], L + "sb_fwd")
    S["Q"], S["K"], S["V"] = mla_pre_fwd(S["proj"], W["wq"], W["wkv"], W["wuq"], W["wuk"], W["wuv"], *rope, L + "mla_pre_fwd")
    S["o_mla"], S["lse"] = mla_fwd(S["Q"], S["K"], S["V"], L + "mla_fwd")
    S["h2"], S["xh2"], S["rs2"] = proj_res_ln([S["o_gdn"], S["o_sb"], S["o_mla"]], W["wo"], S["h1"],
                                              W["ln_g"][1], W["ln_b"][1], L + "out_proj")
    S["h3"], S["xh3"], S["rs3"], S["Gb"], S["Ub"] = ffn_fwd(S["h2"], W["ffb_in"], W["ffb_out"], i, W["ln_g"][2], W["ln_b"][2],
                                                            L + "ffb_fwd")
    h4, S["a"], S["e"] = ple_fwd(S["h3"], p_i, W["wg"], W["wp"], L + "ple_fwd")
    return h4, S


def _layer_bwd(dh4, S, W, rope, i, bufs):
    L = "L%d_" % i
    G = {}
    bufs = dict(bufs)
    dh3, da, de = ple_bwd(dh4, S["a"], S["e"], W["wg"], L + "ple_bwd")
    G["ple_w_gate"] = mm_tn(S["h3"], da, L + "d_ple_gate")
    G["ple_w_proj"] = mm_tn(S["p"], de, L + "d_ple_proj")
    dz3, dg2, db2 = ln_bwd(dh3, S["xh3"], S["rs3"], W["ln_g"][2], L + "ln3_bwd")
    dh2, dGb, dUb, actb = ffn_bwd(dz3, S["Gb"], S["Ub"], W["ffb_in"], W["ffb_out"], i, L + "ffb_bwd")
    bufs["ffb_w_in"] = ffn_dw_in(S["h2"].T.astype(BF16), dGb, dUb, i, DEPTH, bufs.get("ffb_w_in"), L + "d_ffb_in")
    bufs["ffb_w_out"] = ffn_dw_out(actb, dz3, i, DEPTH, bufs.get("ffb_w_out"), L + "d_ffb_out")
    dz2, dg1, db1 = ln_bwd(dh2, S["xh2"], S["rs2"], W["ln_g"][1], L + "ln2_bwd")
    docat = mm_nn(dz2, W["wo"], L + "d_ocat", b_transposed=True)
    dwo = jnp.concatenate([mm_tn(S["o_gdn"], dz2, L + "d_wo_gdn"), mm_tn(S["o_sb"], dz2, L + "d_wo_sb"),
                           mm_tn(S["o_mla"], dz2, L + "d_wo_mla")], axis=0)
    G["mix_w_o"] = dwo.reshape(-1, LANES, dwo.shape[1])[:, :GDN_D, :].reshape(-1, dwo.shape[1])
    dqkv, dgz, dgab, d_alog, d_dt, d_nw = gdn_chunk_bwd(S["qkv"], S["proj"], W["a_row"], W["dt_row"], W["nw"],
                                                        S["opre"], S["Tm"], S["Sin"], docat, L + "gdn_chunk_bwd")
    dc, dconv = gdn_pre_bwd_pointwise(S["proj"], W["conv"], dqkv, L + "gdn_pre_bwd")
    dxqkv = gdn_pre_bwd_conv(dc, W["conv"], L + "gdn_conv_bwd")
    G["gdn_conv_w"] = _unpad_heads(dconv, GDN_QKV_BLOCKS)
    G["gdn_a_log"], G["gdn_dt_bias"], G["gdn_norm_w"] = d_alog[0, :GDN_HEADS], d_dt[0, :GDN_HEADS], d_nw[0]
    dsq, dsk, dsv = sb_bwd(S["proj"], S["tot"], docat, GDN_HEADS, L + "sb_bwd")
    dQ, dK, dV = mla_bwd(S["Q"], S["K"], S["V"], S["o_mla"], S["lse"], docat, GDN_HEADS + SB_HEADS, L + "mla_bwd")
    dmq, dmkv, dwuq, dwuk, dwuv, dwq, dwkv = mla_pre_bwd(
        S["proj"], W["wq"], W["wkv"], W["wuq"], W["wuk"], W["wuv"], *rope, dQ, dK, dV, L + "mla_pre_bwd")
    G["mla_w_uq"], G["mla_w_ukv"] = _mla_up_unpad(dwuq, dwuk, dwuv)
    G["mla_q_norm_w"], G["mla_kv_norm_w"] = dwq[0], dwkv[0]
    dproj = jnp.concatenate([dxqkv, dgz, dsq, dsk, dsv, dmq, dmkv, dgab], axis=1).astype(BF16)
    G["mix_w_in"] = _unpad_in_proj(mm_tn(S["h1"].T.astype(BF16), dproj, L + "d_in_proj", a_transposed=True))
    dh1 = mm_nn(dproj, W["win"], L + "d_h1", res=dz2, res_scale=ALPHA, b_transposed=True)
    dz1, dg0, db0 = ln_bwd(dh1, S["xh1"], S["rs1"], W["ln_g"][0], L + "ln1_bwd")
    dh0, dGa, dUa, acta = ffn_bwd(dz1, S["Ga"], S["Ua"], W["ffa_in"], W["ffa_out"], i, L + "ffa_bwd")
    bufs["ffa_w_in"] = ffn_dw_in(S["h0"].T.astype(BF16), dGa, dUa, i, DEPTH, bufs.get("ffa_w_in"), L + "d_ffa_in")
    bufs["ffa_w_out"] = ffn_dw_out(acta, dz1, i, DEPTH, bufs.get("ffa_w_out"), L + "d_ffa_out")
    G["ln_g"] = jnp.concatenate([dg0, dg1, dg2], axis=0)
    G["ln_b"] = jnp.concatenate([db0, db1, db2], axis=0)
    return dh0, G, bufs


def _local_step(x, p, positions, target, full):
    rope = _rope_tables(positions)
    Ws = [_layer_weights(full, i) for i in range(DEPTH)]
    h, saved = x, []
    for i in range(DEPTH):
        h, S = _layer_fwd(h, p[i], Ws[i], rope, i)
        saved.append(S)
    loss, dh = loss_head(h, target, "loss_head")
    grads, bufs = [None] * DEPTH, {}
    for i in reversed(range(DEPTH)):
        dh, grads[i], bufs = _layer_bwd(dh, saved[i], Ws[i], rope, i, bufs)
    return loss, dh, {n: jnp.stack([grads[i][n] for i in range(DEPTH)]) for n in WEIGHTS if n not in FFN_SLOT}, bufs


def kernel(x, p, positions, ffa_w_in, ffa_w_out, mix_w_in, gdn_conv_w, gdn_a_log, gdn_dt_bias, gdn_norm_w, mla_q_norm_w, mla_kv_norm_w, mla_w_uq, mla_w_ukv, mix_w_o, ffb_w_in, ffb_w_out, ln_g, ln_b, ple_w_gate, ple_w_proj, loss_target, m_ffa_w_in, m_ffa_w_out, m_mix_w_in, m_gdn_conv_w, m_gdn_a_log, m_gdn_dt_bias, m_gdn_norm_w, m_mla_q_norm_w, m_mla_kv_norm_w, m_mla_w_uq, m_mla_w_ukv, m_mix_w_o, m_ffb_w_in, m_ffb_w_out, m_ln_g, m_ln_b, m_ple_w_gate, m_ple_w_proj, v_ffa_w_in, v_ffa_w_out, v_mix_w_in, v_gdn_conv_w, v_gdn_a_log, v_gdn_dt_bias, v_gdn_norm_w, v_mla_q_norm_w, v_mla_kv_norm_w, v_mla_w_uq, v_mla_w_ukv, v_mix_w_o, v_ffb_w_in, v_ffb_w_out, v_ln_g, v_ln_b, v_ple_w_gate, v_ple_w_proj):
    given = dict(locals())
    shards = {n: given[n] for n in WEIGHTS}
    names = list(SHARDED)
    got = all_gather([_view2d(shards[n].astype(SHARDED[n][1])) for n in names], "gather_weights")
    full = {n: shards[n] for n in REPLICATED}
    for n, g in zip(names, got):
        if n in FFN_SLOT:
            full[n] = g.reshape((N_DEV,) + shards[n].shape)
        else:
            full[n] = _from_slots(g, shards[n].shape, SHARDED[n][0])
    loss, grad_x, G, bufs = _local_step(x[0], p[:, 0], positions[0], loss_target[0], full)
    loss = lax.psum(loss[0, 0], ("x", "y", "c"))
    parts = []
    for n in names:
        if n in FFN_SLOT:
            parts.append(bufs[n].reshape(N_DEV, -1, bufs[n].shape[-1]))
        else:
            parts.append(_to_slots(G[n], SHARDED[n][0]).astype(SHARDED[n][1]))
    received = dict(zip(names, exchange_partials(parts, "scatter_grads")))
    received.update(zip(REPLICATED, all_gather([G[n] for n in REPLICATED], "gather_replicated_grads")))
    grad, delta, new_m, new_v = {}, {}, {}, {}
    for n in WEIGHTS:
        shape = shards[n].shape
        outs = reduce_adamw(received[n], _view2d(shards[n]), _view2d(given["m_" + n]), _view2d(given["v_" + n]),
                            "adamw_" + n)
        grad[n], delta[n], new_m[n], new_v[n] = (t.reshape(shape) for t in outs)
    return (loss, grad_x[None], *[grad[n] for n in WEIGHTS], *[delta[n] for n in WEIGHTS],
            *[new_m[n] for n in WEIGHTS], *[new_v[n] for n in WEIGHTS])
```

```python
import functools
import numpy as np
import jax
import jax.numpy as jnp
from jax import lax
from jax.experimental import pallas as pl
from jax.experimental.pallas import tpu as pltpu

F32 = jnp.float32
BF16 = jnp.bfloat16

DEPTH = 2
LN_EPS = 1e-5
RMS_EPS = 1e-6
ALPHA = (2 * DEPTH) ** 0.25
GDN_HEADS, GDN_D, GDN_CONV, GDN_CHUNK = 8, 64, 4, 64
SB_HEADS, SB_DIM = 4, 64
MLA_HEADS, MLA_NOPE, MLA_ROPE, MLA_V, MLA_Q_RANK, MLA_KV_RANK = 4, 64, 32, 64, 256, 128
ROPE_BASE = 10000.0
HALF_ROPE = MLA_ROPE // 2
LANES = 128
N_DEV = 8
ADAM_LR, ADAM_B1, ADAM_B2, ADAM_EPS, ADAM_WD, ADAM_STEP = 0.001, 0.9, 0.999, 1e-08, 0.01, 10

CB_GQ, CB_GK, CB_GV, CB_GZ = 0, 8, 16, 24
CB_SQ, CB_SK, CB_SV = 32, 36, 40
CB_MQ, CB_MKV, CB_GAB = 44, 46, 48
PROJ_W = 49 * LANES
VMEM_LIMIT = 56 * 1024 * 1024

NT_DIMS = (((1,), (1,)), ((), ()))
TN_DIMS = (((0,), (0,)), ((), ()))


def _cp(sem):
    return pltpu.CompilerParams(dimension_semantics=sem, vmem_limit_bytes=VMEM_LIMIT)


def _bdot(a, b):
    return jnp.dot(a.astype(BF16), b.astype(BF16), preferred_element_type=F32)


def _bdot_nt(a, b):
    return lax.dot_general(a.astype(BF16), b.astype(BF16), NT_DIMS, preferred_element_type=F32)


def _bdot_tn(a, b):
    return lax.dot_general(a.astype(BF16), b.astype(BF16), TN_DIMS, preferred_element_type=F32)


def _split2(a):
    hi = a.astype(BF16)
    lo = (a - hi.astype(F32)).astype(BF16)
    return hi, lo


def _hdot(a, b):
    m = a.shape[0]
    a_hi, a_lo = _split2(a)
    b_hi, b_lo = _split2(b)
    r = jnp.dot(jnp.concatenate([a_hi, a_lo], axis=0), b_hi, preferred_element_type=F32)
    return r[:m] + r[m:] + jnp.dot(a_hi, b_lo, preferred_element_type=F32)


def _hdot_tn(a, b):
    a_hi, a_lo = _split2(a)
    b_hi, b_lo = _split2(b)
    d = functools.partial(lax.dot_general, dimension_numbers=TN_DIMS, preferred_element_type=F32)
    return d(a_hi, b_hi) + d(a_lo, b_hi) + d(a_hi, b_lo)


def _ones_dot(x, ones_bf16):
    hi = x.astype(BF16)
    r1 = x - hi.astype(F32)
    mid = r1.astype(BF16)
    lo = (r1 - mid.astype(F32)).astype(BF16)
    d = functools.partial(jnp.dot, preferred_element_type=F32)
    return d(hi, ones_bf16) + d(mid, ones_bf16) + d(lo, ones_bf16)


def _ones_dot_left(ones_bf16, x):
    hi = x.astype(BF16)
    r1 = x - hi.astype(F32)
    mid = r1.astype(BF16)
    lo = (r1 - mid.astype(F32)).astype(BF16)
    d = functools.partial(jnp.dot, preferred_element_type=F32)
    return d(ones_bf16, hi) + d(ones_bf16, mid) + d(ones_bf16, lo)


def _iota2(shape, dim):
    return lax.broadcasted_iota(jnp.int32, shape, dim)


def _sigmoid(x):
    return 1.0 / (1.0 + jnp.exp(-x))


def _softplus(x):
    return jnp.maximum(x, 0.0) + jnp.log(1.0 + jnp.exp(-jnp.abs(x)))


def _pick(n, limit, mult):
    if n <= limit:
        return n
    best = None
    for t in range(mult, limit + 1, mult):
        if n % t == 0:
            best = t
    assert best is not None, (n, limit, mult)
    return best


MESH = pl.DeviceIdType.MESH
ANY = pl.BlockSpec(memory_space=pl.ANY)


def _place():
    return lax.axis_index("x"), lax.axis_index("y"), lax.axis_index("c")


def _peer(k):
    x, y, c = _place()
    return (1 - x if k & 4 else x, 1 - y if k & 2 else y, 1 - c if k & 1 else c)


class Hosted:
    def __init__(self, kind, arrays):
        self.kind, self.arrays, self.n, self.results = kind, list(arrays), len(arrays), None

    def out_shapes(self):
        if self.kind == "gather":
            return [jax.ShapeDtypeStruct((N_DEV,) + a.shape, a.dtype) for a in self.arrays]
        return [jax.ShapeDtypeStruct(a.shape, a.dtype) for a in self.arrays]

    def sems(self):
        return [pltpu.SemaphoreType.DMA((self.n, N_DEV - 1)), pltpu.SemaphoreType.DMA((self.n, N_DEV - 1)),
                pltpu.SemaphoreType.DMA((self.n,))]

    def _copies(self, src_refs, dst_refs, send_sems, recv_sems, local_sems):
        x, y, c = _place()
        me = 4 * x + 2 * y + c
        local, remote = [], []
        for a in range(self.n):
            gather = self.kind == "gather"
            local.append(pltpu.make_async_copy(src_refs[a] if gather else src_refs[a].at[me], dst_refs[a].at[me],
                                               local_sems.at[a]))
            for k in range(1, N_DEV):
                px, py, pc = _peer(k)
                remote.append(pltpu.make_async_remote_copy(
                    src_ref=src_refs[a] if gather else src_refs[a].at[4 * px + 2 * py + pc], dst_ref=dst_refs[a].at[me],
                    send_sem=send_sems.at[a, k - 1], recv_sem=recv_sems.at[a, k - 1],
                    device_id=(px, py, pc), device_id_type=MESH))
        return local, remote

    def start(self, *refs):
        local, remote = self._copies(*refs)
        for cp in local + remote:
            cp.start()

    def wait(self, *refs):
        local, remote = self._copies(*refs)
        for cp in remote:
            cp.wait_recv()
        for cp in remote:
            cp.wait_send()
        for cp in local:
            cp.wait()


def _hosted_call(hosted, body, *, grid, in_specs, out_specs, out_shape, scratch_shapes=(), compiler_params, name):
    if hosted is None:
        return pl.pallas_call(body, grid=grid, in_specs=in_specs, out_specs=out_specs, out_shape=out_shape,
                              scratch_shapes=scratch_shapes, compiler_params=compiler_params, name=name)
    single = not isinstance(out_shape, (list, tuple))
    o_specs = [out_specs] if single else list(out_specs)
    o_shape = [out_shape] if single else list(out_shape)
    n_in, n_out, n_scr, n = len(in_specs), len(o_specs), len(scratch_shapes), hosted.n

    def wrapped(*refs):
        ins, c_in = refs[:n_in], refs[n_in:n_in + n]
        outs, c_out = refs[n_in + n:n_in + n + n_out], refs[n_in + n + n_out:n_in + 2 * n + n_out]
        rest = refs[n_in + 2 * n + n_out:]
        scr, sems = rest[:n_scr], rest[n_scr:]
        ids = [pl.program_id(ax) for ax in range(len(grid))]
        first = functools.reduce(jnp.logical_and, [i == 0 for i in ids])
        last = functools.reduce(jnp.logical_and, [i == g - 1 for i, g in zip(ids, grid)])

        @pl.when(first)
        def _():
            hosted.start(c_in, c_out, *sems)

        body(*ins, *outs, *scr)

        @pl.when(last)
        def _():
            hosted.wait(c_in, c_out, *sems)

    call = pl.pallas_call(
        wrapped, grid=grid, in_specs=list(in_specs) + [ANY] * n, out_specs=o_specs + [ANY] * n,
        out_shape=o_shape + hosted.out_shapes(), scratch_shapes=list(scratch_shapes) + hosted.sems(),
        compiler_params=_cp(("arbitrary",) * len(grid)), name=name)

    def run(*args):
        outs = call(*args, *hosted.arrays)
        hosted.results = list(outs[n_out:])
        return outs[0] if single else list(outs[:n_out])

    return run


def mm_nn(a, b, name, out_dtype=F32, res=None, res_scale=1.0, b_transposed=False, hosted=None):
    M, K = a.shape
    N = b.shape[0] if b_transposed else b.shape[1]
    tm, tn, tk = _pick(M, 512, 16), _pick(N, 1024, LANES), _pick(K, 1024, LANES)
    nk = K // tk
    has_res = res is not None
    dot = _bdot_nt if b_transposed else _bdot

    def body(*refs):
        if has_res:
            a_ref, b_ref, r_ref, o_ref, acc_ref = refs
        else:
            a_ref, b_ref, o_ref, acc_ref = refs
        k = pl.program_id(2)

        @pl.when(k == 0)
        def _():
            acc_ref[...] = jnp.zeros_like(acc_ref)

        acc_ref[...] += dot(a_ref[...], b_ref[...])

        @pl.when(k == nk - 1)
        def _():
            out = acc_ref[...]
            if has_res:
                out = out + res_scale * r_ref[...]
            o_ref[...] = out.astype(o_ref.dtype)

    b_spec = pl.BlockSpec((tn, tk), lambda i, j, k: (j, k)) if b_transposed else pl.BlockSpec((tk, tn), lambda i, j, k: (k, j))
    in_specs = [pl.BlockSpec((tm, tk), lambda i, j, k: (i, k)), b_spec]
    args = [a, b]
    if has_res:
        in_specs.append(pl.BlockSpec((tm, tn), lambda i, j, k: (i, j)))
        args.append(res)
    return _hosted_call(
        hosted, body, grid=(M // tm, N // tn, nk), in_specs=in_specs,
        out_specs=pl.BlockSpec((tm, tn), lambda i, j, k: (i, j)),
        out_shape=jax.ShapeDtypeStruct((M, N), out_dtype),
        scratch_shapes=[pltpu.VMEM((tm, tn), F32)],
        compiler_params=_cp(("parallel", "parallel", "arbitrary")), name=name)(*args)


def mm_tn(a, b, name, out_dtype=F32, a_transposed=False):
    K, T = a.shape if a_transposed else a.shape[::-1]
    _, N = b.shape
    tk, tn, tt = _pick(K, 512, LANES), _pick(N, 1024, LANES), _pick(T, 512, LANES)
    nt = T // tt
    dot = _bdot if a_transposed else _bdot_tn

    def body(a_ref, b_ref, o_ref, acc_ref):
        t = pl.program_id(2)

        @pl.when(t == 0)
        def _():
            acc_ref[...] = jnp.zeros_like(acc_ref)

        acc_ref[...] += dot(a_ref[...], b_ref[...])

        @pl.when(t == nt - 1)
        def _():
            o_ref[...] = acc_ref[...].astype(o_ref.dtype)

    a_spec = pl.BlockSpec((tk, tt), lambda i, j, t: (i, t)) if a_transposed else pl.BlockSpec((tt, tk), lambda i, j, t: (t, i))
    return pl.pallas_call(
        body, grid=(K // tk, N // tn, nt),
        in_specs=[a_spec, pl.BlockSpec((tt, tn), lambda i, j, t: (t, j))],
        out_specs=pl.BlockSpec((tk, tn), lambda i, j, t: (i, j)),
        out_shape=jax.ShapeDtypeStruct((K, N), out_dtype),
        scratch_shapes=[pltpu.VMEM((tk, tn), F32)],
        compiler_params=_cp(("parallel", "parallel", "arbitrary")), name=name)(a, b)


def _ln_apply(z, g, b):
    mu = jnp.mean(z, axis=-1, keepdims=True)
    zc = z - mu
    var = jnp.mean(zc * zc, axis=-1, keepdims=True)
    rstd = lax.rsqrt(var + LN_EPS)
    xhat = zc * rstd
    return xhat * g + b, xhat, rstd


def ln_bwd(dout, xhat, rstd, g, name):
    T, D = dout.shape
    tm = _pick(T, 512, 8)

    def body(do_ref, xh_ref, rs_ref, g_ref, dz_ref, dg_ref, db_ref):
        i = pl.program_id(0)

        @pl.when(i == 0)
        def _():
            dg_ref[...] = jnp.zeros_like(dg_ref)
            db_ref[...] = jnp.zeros_like(db_ref)

        do = do_ref[...]
        xh = xh_ref[...]
        dxh = do * g_ref[...]
        m1 = jnp.mean(dxh, axis=-1, keepdims=True)
        m2 = jnp.mean(dxh * xh, axis=-1, keepdims=True)
        dz_ref[...] = rs_ref[...] * (dxh - m1 - xh * m2)
        dg_ref[...] += jnp.sum(do * xh, axis=0, keepdims=True)
        db_ref[...] += jnp.sum(do, axis=0, keepdims=True)

    row = pl.BlockSpec((tm, D), lambda i: (i, 0))
    vec = pl.BlockSpec((1, D), lambda i: (0, 0))
    return pl.pallas_call(
        body, grid=(T // tm,),
        in_specs=[row, row, pl.BlockSpec((tm, 1), lambda i: (i, 0)), vec],
        out_specs=[row, vec, vec],
        out_shape=[jax.ShapeDtypeStruct((T, D), F32), jax.ShapeDtypeStruct((1, D), F32), jax.ShapeDtypeStruct((1, D), F32)],
        compiler_params=_cp(("arbitrary",)), name=name)(dout, xhat, rstd, g)


FFN_CHUNKS = N_DEV // 2


def ffn_fwd(h, w_in, w_out, g, b, name, hosted=None):
    T, D = h.shape
    fc = w_in.shape[2]
    half = w_out.shape[1]
    tm = _pick(T, 512, 8)
    nc = FFN_CHUNKS

    def body(h_ref, wg_ref, wu_ref, wo_ref, g_ref, b_ref, out_ref, xh_ref, rs_ref, G_ref, U_ref, acc_ref):
        c = pl.program_id(1)

        @pl.when(c == 0)
        def _():
            acc_ref[...] = jnp.zeros_like(acc_ref)

        hb = h_ref[...].astype(BF16)
        G = jnp.dot(hb, wg_ref[0], preferred_element_type=F32)
        U = jnp.dot(hb, wu_ref[0], preferred_element_type=F32)
        G_ref[0] = G
        U_ref[0] = U
        act = G * _sigmoid(G) * U
        acc_ref[...] += _bdot(act, wo_ref[...].reshape(2 * half, D))

        @pl.when(c == nc - 1)
        def _():
            z = ALPHA * h_ref[...] + 0.5 * acc_ref[...]
            out, xh, rs = _ln_apply(z, g_ref[...], b_ref[...])
            out_ref[...] = out
            xh_ref[...] = xh
            rs_ref[...] = rs

    row = pl.BlockSpec((tm, D), lambda i, c: (i, 0))
    vec = pl.BlockSpec((1, D), lambda i, c: (0, 0))
    cblk = pl.BlockSpec((1, tm, fc), lambda i, c: (c, i, 0))
    csds = jax.ShapeDtypeStruct((nc, T, fc), F32)
    return _hosted_call(
        hosted, body, grid=(T // tm, nc),
        in_specs=[row, pl.BlockSpec((1, D, fc), lambda i, c: (c, 0, 0)),
                  pl.BlockSpec((1, D, fc), lambda i, c: (c + nc, 0, 0)),
                  pl.BlockSpec((2, half, D), lambda i, c: (c, 0, 0)), vec, vec],
        out_specs=[row, row, pl.BlockSpec((tm, 1), lambda i, c: (i, 0)), cblk, cblk],
        out_shape=[jax.ShapeDtypeStruct((T, D), F32), jax.ShapeDtypeStruct((T, D), F32), jax.ShapeDtypeStruct((T, 1), F32),
                   csds, csds],
        scratch_shapes=[pltpu.VMEM((tm, D), F32)],
        compiler_params=_cp(("parallel", "arbitrary")), name=name)(h, w_in, w_in, w_out, g, b)


def ffn_bwd(dz, G, U, w_in, w_out, name, hosted=None):
    T, D = dz.shape
    nc, _, fc = G.shape
    half = w_out.shape[1]
    tm = _pick(T, 512, 16)

    def body(dz_ref, G_ref, U_ref, wg_ref, wu_ref, wo_ref, dh_ref, dG_ref, dU_ref, act_ref, acc_ref):
        c = pl.program_id(1)

        @pl.when(c == 0)
        def _():
            acc_ref[...] = jnp.zeros_like(acc_ref)

        dy = (0.5 * dz_ref[...]).astype(BF16)
        dact = _bdot_nt(dy, wo_ref[...].reshape(2 * half, D))
        G = G_ref[0]
        U = U_ref[0]
        s = _sigmoid(G)
        silu = G * s
        dG = (dact * U * (s * (1.0 + G * (1.0 - s)))).astype(BF16)
        dU = (dact * silu).astype(BF16)
        dG_ref[0] = dG
        dU_ref[0] = dU
        act_ref[0] = (silu * U).astype(BF16)
        acc_ref[...] += _bdot_nt(dG, wg_ref[0]) + _bdot_nt(dU, wu_ref[0])

        @pl.when(c == nc - 1)
        def _():
            dh_ref[...] = ALPHA * dz_ref[...] + acc_ref[...]

    row = pl.BlockSpec((tm, D), lambda i, c: (i, 0))
    cblk = pl.BlockSpec((1, tm, fc), lambda i, c: (c, i, 0))
    csds = jax.ShapeDtypeStruct((nc, T, fc), BF16)
    return _hosted_call(
        hosted, body, grid=(T // tm, nc),
        in_specs=[row, cblk, cblk, pl.BlockSpec((1, D, fc), lambda i, c: (c, 0, 0)),
                  pl.BlockSpec((1, D, fc), lambda i, c: (c + nc, 0, 0)),
                  pl.BlockSpec((2, half, D), lambda i, c: (c, 0, 0))],
        out_specs=[row, cblk, cblk, cblk],
        out_shape=[jax.ShapeDtypeStruct((T, D), F32), csds, csds, csds],
        scratch_shapes=[pltpu.VMEM((tm, D), F32)],
        compiler_params=_cp(("parallel", "arbitrary")), name=name)(dz, G, U, w_in, w_in, w_out)


def ffn_dw_in(h_t, dG, dU, name, hosted=None):
    D, T = h_t.shape
    nc, _, fc = dG.shape
    tt = _pick(T, 512, LANES)
    nt = T // tt

    def body(h_ref, dG_ref, dU_ref, o_ref, acc_ref):
        s = pl.program_id(0)
        t = pl.program_id(1)

        @pl.when(t == 0)
        def _():
            acc_ref[...] = jnp.zeros_like(acc_ref)

        hb = h_ref[:, pl.ds(pl.multiple_of(t * tt, tt), tt)]

        @pl.when(s < nc)
        def _():
            acc_ref[...] += jnp.dot(hb, dG_ref[0], preferred_element_type=F32)

        @pl.when(s >= nc)
        def _():
            acc_ref[...] += jnp.dot(hb, dU_ref[0], preferred_element_type=F32)

        @pl.when(t == nt - 1)
        def _():
            o_ref[0] = acc_ref[...].astype(o_ref.dtype)

    return _hosted_call(
        hosted, body, grid=(2 * nc, nt),
        in_specs=[pl.BlockSpec((D, T), lambda s, t: (0, 0)),
                  pl.BlockSpec((1, tt, fc), lambda s, t: (jnp.minimum(s, nc - 1), jnp.where(s < nc, t, nt - 1), 0)),
                  pl.BlockSpec((1, tt, fc), lambda s, t: (jnp.maximum(s - nc, 0), jnp.where(s >= nc, t, 0), 0))],
        out_specs=pl.BlockSpec((1, D, fc), lambda s, t: (s, 0, 0)),
        out_shape=jax.ShapeDtypeStruct((2 * nc, D, fc), BF16),
        scratch_shapes=[pltpu.VMEM((D, fc), F32)],
        compiler_params=_cp(("parallel", "arbitrary")), name=name)(h_t, dG, dU)


def ffn_dw_out(act, dz, name, hosted=None):
    nc, T, fc = act.shape
    D = dz.shape[1]
    half = fc // 2
    tt = _pick(T, 512, 16)
    nt = T // tt

    def body(a_ref, dz_ref, o_ref, acc_ref):
        t = pl.program_id(1)

        @pl.when(t == 0)
        def _():
            acc_ref[...] = jnp.zeros_like(acc_ref)

        acc_ref[...] += _bdot_tn(a_ref[0], dz_ref[...])

        @pl.when(t == nt - 1)
        def _():
            o_ref[...] = (0.5 * acc_ref[...]).reshape(2, half, D).astype(o_ref.dtype)

    return _hosted_call(
        hosted, body, grid=(nc, nt),
        in_specs=[pl.BlockSpec((1, tt, fc), lambda c, t: (c, t, 0)), pl.BlockSpec((tt, D), lambda c, t: (t, 0))],
        out_specs=pl.BlockSpec((2, half, D), lambda c, t: (c, 0, 0)),
        out_shape=jax.ShapeDtypeStruct((2 * nc, half, D), BF16),
        scratch_shapes=[pltpu.VMEM((fc, D), F32)],
        compiler_params=_cp(("parallel", "arbitrary")), name=name)(act, dz)


def proj_res_ln(parts, w, res, g, b, name):
    T, D = res.shape
    tm = _pick(T, 512, 8)
    widths = [p.shape[1] for p in parts]
    offs = [int(sum(widths[:i])) for i in range(len(parts))]
    n = len(parts)

    def body(*refs):
        p_refs = refs[:n]
        w_ref, r_ref, g_ref, b_ref, out_ref, xh_ref, rs_ref = refs[n:]
        acc = ALPHA * r_ref[...]
        for p_ref, o, wd in zip(p_refs, offs, widths):
            acc = acc + _bdot(p_ref[...], w_ref[o:o + wd, :])
        out, xh, rs = _ln_apply(acc, g_ref[...], b_ref[...])
        out_ref[...] = out
        xh_ref[...] = xh
        rs_ref[...] = rs

    row = pl.BlockSpec((tm, D), lambda i: (i, 0))
    vec = pl.BlockSpec((1, D), lambda i: (0, 0))
    return pl.pallas_call(
        body, grid=(T // tm,),
        in_specs=[pl.BlockSpec((tm, wd), lambda i: (i, 0)) for wd in widths]
        + [pl.BlockSpec(w.shape, lambda i: (0, 0)), row, vec, vec],
        out_specs=[row, row, pl.BlockSpec((tm, 1), lambda i: (i, 0))],
        out_shape=[jax.ShapeDtypeStruct((T, D), F32), jax.ShapeDtypeStruct((T, D), F32), jax.ShapeDtypeStruct((T, 1), F32)],
        compiler_params=_cp(("parallel",)), name=name)(*parts, w, res, g, b)


def ple_fwd(h, p, wg, wp, name):
    T, D = h.shape
    P = p.shape[1]
    tm, tn = _pick(T, 512, 8), _pick(D, 512, LANES)

    def body(h_ref, hn_ref, p_ref, wg_ref, wp_ref, out_ref, a_ref, e_ref):
        a = _bdot(h_ref[...], wg_ref[...])
        e = _bdot(p_ref[...], wp_ref[...])
        a_ref[...] = a
        e_ref[...] = e
        out_ref[...] = hn_ref[...] + _sigmoid(a) * e

    blk = pl.BlockSpec((tm, tn), lambda i, j: (i, j))
    sds = jax.ShapeDtypeStruct((T, D), F32)
    return pl.pallas_call(
        body, grid=(T // tm, D // tn),
        in_specs=[pl.BlockSpec((tm, D), lambda i, j: (i, 0)), blk, pl.BlockSpec((tm, P), lambda i, j: (i, 0)),
                  pl.BlockSpec((D, tn), lambda i, j: (0, j)), pl.BlockSpec((P, tn), lambda i, j: (0, j))],
        out_specs=[blk, blk, blk], out_shape=[sds, sds, sds],
        compiler_params=_cp(("parallel", "parallel")), name=name)(h, h, p, wg, wp)


def ple_bwd(dout, a, e, wg, name):
    T, D = dout.shape
    tm = _pick(T, 512, 16)

    def body(do_ref, a_ref, e_ref, wg_ref, dh_ref, da_ref, de_ref):
        do = do_ref[...]
        s = _sigmoid(a_ref[...])
        da = (do * e_ref[...] * s * (1.0 - s)).astype(BF16)
        da_ref[...] = da
        de_ref[...] = (do * s).astype(BF16)
        dh_ref[...] = do + _bdot_nt(da, wg_ref[...])

    row = pl.BlockSpec((tm, D), lambda i: (i, 0))
    return pl.pallas_call(
        body, grid=(T // tm,),
        in_specs=[row, row, row, pl.BlockSpec((D, D), lambda i: (0, 0))],
        out_specs=[row, row, row],
        out_shape=[jax.ShapeDtypeStruct((T, D), F32), jax.ShapeDtypeStruct((T, D), BF16), jax.ShapeDtypeStruct((T, D), BF16)],
        compiler_params=_cp(("parallel",)), name=name)(dout, a, e, wg)


def loss_head(y, target, name):
    T, D = y.shape
    tm = _pick(T, 512, 8)

    def body(y_ref, t_ref, loss_ref, dy_ref):
        i = pl.program_id(0)

        @pl.when(i == 0)
        def _():
            loss_ref[...] = jnp.zeros_like(loss_ref)

        err = y_ref[...] - t_ref[...]
        dy_ref[...] = err * (1.0 / D)
        per_tok = jnp.sum(err * err, axis=-1, keepdims=True) * (1.0 / D)
        loss_ref[...] += 0.5 * jnp.sum(per_tok, axis=0, keepdims=True)

    row = pl.BlockSpec((tm, D), lambda i: (i, 0))
    return pl.pallas_call(
        body, grid=(T // tm,), in_specs=[row, row],
        out_specs=[pl.BlockSpec((1, 1), lambda i: (0, 0)), row],
        out_shape=[jax.ShapeDtypeStruct((1, 1), F32), jax.ShapeDtypeStruct((T, D), F32)],
        compiler_params=_cp(("arbitrary",)), name=name)(y, target)


GDN_QKV_BLOCKS = 3 * GDN_HEADS
HALO = 8


def _conv_taps(pad_ref, w_ref, tm, base):
    acc = w_ref[0:1, :] * pad_ref[pl.ds(base, tm), :]
    for k in range(1, GDN_CONV):
        acc = acc + w_ref[k:k + 1, :] * pad_ref[pl.ds(base + k, tm), :]
    return acc


GDN_GROUP_W = GDN_HEADS * LANES
GDN_PRE_ROWS = 256


def _head_sums(x):
    rows = x.shape[0]
    parts = [jnp.broadcast_to(jnp.sum(x[:, h * LANES:(h + 1) * LANES], axis=-1, keepdims=True), (rows, LANES))
             for h in range(x.shape[1] // LANES)]
    return jnp.concatenate(parts, axis=1)


def _gdn_pre_common(x_ref, halo_ref, w_ref, pad_ref, tm):
    i = pl.program_id(1)
    grp = pl.program_id(0)
    pad_ref[0:HALO, :] = jnp.where(i == 0, 0.0, halo_ref[...])
    pad_ref[HALO:HALO + tm, :] = x_ref[...]
    c = _conv_taps(pad_ref, w_ref, tm, HALO - (GDN_CONV - 1))
    s = _sigmoid(c)
    y = c * s
    r = lax.rsqrt(_head_sums(y * y) + RMS_EPS)
    scale = jnp.where(grp < 1, GDN_D ** -0.5, 1.0)
    return grp < 2, c, s, y, r, scale


def gdn_pre_fwd(proj, conv_w_p, name):
    T = proj.shape[0]
    tm = _pick(T, GDN_PRE_ROWS, 8)
    GW = GDN_GROUP_W

    def body(x_ref, halo_ref, w_ref, o_ref, pad_ref):
        normed, c, s, y, r, scale = _gdn_pre_common(x_ref, halo_ref, w_ref, pad_ref, tm)
        o_ref[...] = jnp.where(normed, y * r * scale, y)

    return pl.pallas_call(
        body, grid=(3, T // tm),
        in_specs=[pl.BlockSpec((tm, GW), lambda hb, i: (i, hb)),
                  pl.BlockSpec((HALO, GW), lambda hb, i: (jnp.maximum(i * (tm // HALO) - 1, 0), hb)),
                  pl.BlockSpec((GDN_CONV, GW), lambda hb, i: (0, hb))],
        out_specs=pl.BlockSpec((tm, GW), lambda hb, i: (i, hb)),
        out_shape=jax.ShapeDtypeStruct((T, 3 * GW), F32),
        scratch_shapes=[pltpu.VMEM((tm + HALO, GW), F32)],
        compiler_params=_cp(("parallel", "parallel")), name=name)(proj, proj, conv_w_p)


def gdn_pre_bwd_pointwise(proj, conv_w_p, dqkv, name, hosted=None):
    T = proj.shape[0]
    tm = _pick(T, GDN_PRE_ROWS, 8)
    GW = GDN_GROUP_W

    def body(x_ref, halo_ref, w_ref, d_ref, dc_ref, dw_ref, pad_ref):
        i = pl.program_id(1)
        normed, c, s, y, r, scale = _gdn_pre_common(x_ref, halo_ref, w_ref, pad_ref, tm)

        @pl.when(i == 0)
        def _():
            dw_ref[...] = jnp.zeros_like(dw_ref)

        d = d_ref[...]
        n = y * r
        dn = d * scale
        dy = jnp.where(normed, r * (dn - n * _head_sums(dn * n)), d)
        dc = dy * (s * (1.0 + c * (1.0 - s)))
        dc_ref[...] = dc
        for k in range(GDN_CONV):
            xs = pad_ref[pl.ds(HALO - (GDN_CONV - 1) + k, tm), :]
            dw_ref[k:k + 1, :] += jnp.sum(dc * xs, axis=0, keepdims=True)

    blk = pl.BlockSpec((tm, GW), lambda hb, i: (i, hb))
    wblk = pl.BlockSpec((GDN_CONV, GW), lambda hb, i: (0, hb))
    return _hosted_call(
        hosted, body, grid=(3, T // tm),
        in_specs=[blk, pl.BlockSpec((HALO, GW), lambda hb, i: (jnp.maximum(i * (tm // HALO) - 1, 0), hb)), wblk, blk],
        out_specs=[blk, wblk],
        out_shape=[jax.ShapeDtypeStruct((T, 3 * GW), F32), jax.ShapeDtypeStruct((GDN_CONV, 3 * GW), F32)],
        scratch_shapes=[pltpu.VMEM((tm + HALO, GW), F32)],
        compiler_params=_cp(("parallel", "arbitrary")), name=name)(proj, proj, conv_w_p, dqkv)


def gdn_pre_bwd_conv(dc, conv_w_p, name):
    T = dc.shape[0]
    tm = _pick(T, GDN_PRE_ROWS, 8)
    nt = T // tm
    GW = GDN_GROUP_W

    def body(dc_ref, halo_ref, w_ref, dx_ref, pad_ref):
        i = pl.program_id(1)
        pad_ref[0:tm, :] = dc_ref[...]
        pad_ref[tm:tm + HALO, :] = jnp.where(i == nt - 1, 0.0, halo_ref[...])
        acc = w_ref[GDN_CONV - 1:GDN_CONV, :] * pad_ref[pl.ds(0, tm), :]
        for k in range(GDN_CONV - 1):
            acc = acc + w_ref[k:k + 1, :] * pad_ref[pl.ds(GDN_CONV - 1 - k, tm), :]
        dx_ref[...] = acc

    blk = pl.BlockSpec((tm, GW), lambda hb, i: (i, hb))
    return pl.pallas_call(
        body, grid=(3, nt),
        in_specs=[blk, pl.BlockSpec((HALO, GW), lambda hb, i: (jnp.minimum((i + 1) * (tm // HALO), T // HALO - 1), hb)),
                  pl.BlockSpec((GDN_CONV, GW), lambda hb, i: (0, hb))],
        out_specs=blk,
        out_shape=jax.ShapeDtypeStruct((T, 3 * GW), F32),
        scratch_shapes=[pltpu.VMEM((tm + HALO, GW), F32)],
        compiler_params=_cp(("parallel", "parallel")), name=name)(dc, dc, conv_w_p)


def _chunk_masks(C):
    row = _iota2((C, C), 0)
    col = _iota2((C, C), 1)
    return row >= col, row > col, row == col


BNN = (((2,), (1,)), ((0,), (0,)))
BNT = (((2,), (2,)), ((0,), (0,)))
BTN = (((1,), (1,)), ((0,), (0,)))


def _bmm(a, b, dims=BNN):
    return lax.dot_general(a.astype(BF16), b.astype(BF16), dims, preferred_element_type=F32)


def _hbmm(a, b):
    m = a.shape[1]
    a_hi, a_lo = _split2(a)
    b_hi, b_lo = _split2(b)
    r = lax.dot_general(jnp.concatenate([a_hi, a_lo], axis=1), b_hi, BNN, preferred_element_type=F32)
    return r[:, :m] + r[:, m:] + lax.dot_general(a_hi, b_lo, BNN, preferred_element_type=F32)


def _hbmm_tn(a, b):
    a_hi, a_lo = _split2(a)
    b_hi, b_lo = _split2(b)
    d = functools.partial(lax.dot_general, dimension_numbers=BTN, preferred_element_type=F32)
    return d(a_hi, b_hi) + d(a_lo, b_hi) + d(a_hi, b_lo)


def _col_to_row(colv, eye):
    return jnp.sum(jnp.where(eye, colv, 0.0), axis=1, keepdims=True)


def _row_to_col(rowv, eye):
    return jnp.sum(jnp.where(eye, rowv, 0.0), axis=2, keepdims=True)


def _unit_lower_inverse(A, eye):
    C = A.shape[1]
    P = jnp.where(eye, 1.0, 0.0) - A
    Bp = _hbmm(A, A)
    for _ in range(4):
        R = _hbmm(jnp.concatenate([Bp, P], axis=1), Bp)
        Bp = R[:, :C]
        P = P + R[:, C:]
    return P + _hbmm(P, Bp)


def _stack_heads(ref, first_block, n, width=GDN_D):
    return jnp.stack([ref[:, pl.ds((first_block + h) * LANES, width)] for h in range(n)])


def _unstack_heads(ref, first_block, val):
    for h in range(val.shape[0]):
        ref[:, pl.ds((first_block + h) * LANES, val.shape[2])] = val[h]


def _gdn_gates(gab, a_row, dt_row, incl):
    g_all = -jnp.exp(a_row) * _softplus(gab + dt_row)
    beta_all = _sigmoid(gab)
    gc_all = _ones_dot_left(incl.astype(BF16), g_all)
    return g_all, beta_all, gc_all


def _gdn_common(qkv_ref, gc_all, beta_all, incl, strict, eye):
    C, H = GDN_CHUNK, GDN_HEADS
    q, k, v = (_stack_heads(qkv_ref, j * H, H) for j in range(3))
    gc = jnp.stack([gc_all[:, h:h + 1] for h in range(H)])
    beta = jnp.stack([beta_all[:, H + h:H + h + 1] for h in range(H)])
    gc_row = _col_to_row(gc, eye)
    decay = jnp.where(incl, jnp.exp(jnp.where(incl, gc - gc_row, 0.0)), 0.0)
    e_gc = jnp.exp(gc)
    gl = gc[:, C - 1:C, :]
    e_gl = jnp.exp(gl)
    ekd = jnp.exp(gl - gc)
    kb = k * beta
    A = jnp.where(strict, _bmm(kb, k, BNT) * decay, 0.0)
    Pm = jnp.where(incl, _bmm(q, k, BNT) * decay, 0.0)
    return q, k, v, gc, beta, decay, e_gc, e_gl, ekd, kb, A, Pm


def gdn_chunk_fwd(qkv, proj, a_row, dt_row, norm_w, name, hosted=None):
    T = qkv.shape[0]
    C, H, Dh = GDN_CHUNK, GDN_HEADS, GDN_D
    N = T // C

    def body(qkv_ref, gz_ref, gab_ref, a_ref, dt_ref, nw_ref, o_ref, opre_ref, Tm_ref, Sin_ref, S_ref):
        n = pl.program_id(0)

        @pl.when(n == 0)
        def _():
            S_ref[...] = jnp.zeros_like(S_ref)

        incl, strict, eye = _chunk_masks(C)
        _, beta_all, gc_all = _gdn_gates(gab_ref[...], a_ref[...], dt_ref[...], incl)
        o_ref[...] = jnp.zeros_like(o_ref)
        opre_ref[...] = jnp.zeros_like(opre_ref)
        q, k, v, gc, beta, decay, e_gc, e_gl, ekd, kb, A, Pm = _gdn_common(qkv_ref, gc_all, beta_all, incl, strict, eye)
        Tm = _unit_lower_inverse(A, eye)
        u = _hbmm(Tm, v * beta)
        w = _hbmm(Tm, kb * e_gc)
        S = S_ref[...]
        v_new = u - _bmm(w, S)
        o = _bmm(q * e_gc, S) + _bmm(Pm, v_new)
        S_ref[...] = S * e_gl + _bmm(k * ekd, v_new, BTN)
        Sin_ref[0] = S
        Tm_ref[0] = Tm
        r = lax.rsqrt(jnp.mean(o * o, axis=-1, keepdims=True) + RMS_EPS)
        gz = _stack_heads(gz_ref, 0, H)
        _unstack_heads(opre_ref, 0, o)
        _unstack_heads(o_ref, 0, o * r * nw_ref[...] * (gz * _sigmoid(gz)))

    vec = pl.BlockSpec((1, LANES), lambda n: (0, 0))
    hblk = pl.BlockSpec((C, H * LANES), lambda n: (n, 0))
    sblk = pl.BlockSpec((1, H, Dh, Dh), lambda n: (n, 0, 0, 0))
    return _hosted_call(
        hosted, body, grid=(N,),
        in_specs=[pl.BlockSpec((C, GDN_QKV_BLOCKS * LANES), lambda n: (n, 0)),
                  pl.BlockSpec((C, H * LANES), lambda n: (n, CB_GZ // H)),
                  pl.BlockSpec((C, LANES), lambda n: (n, CB_GAB)), vec, vec, pl.BlockSpec((1, Dh), lambda n: (0, 0))],
        out_specs=[hblk, hblk, sblk, sblk],
        out_shape=[jax.ShapeDtypeStruct((T, H * LANES), F32), jax.ShapeDtypeStruct((T, H * LANES), F32),
                   jax.ShapeDtypeStruct((N, H, Dh, Dh), F32), jax.ShapeDtypeStruct((N, H, Dh, Dh), F32)],
        scratch_shapes=[pltpu.VMEM((H, Dh, Dh), F32)],
        compiler_params=_cp(("arbitrary",)), name=name)(qkv, proj, proj, a_row, dt_row, norm_w)


def gdn_chunk_bwd(qkv, proj, a_row, dt_row, norm_w, opre, Tm_all, Sin_all, docat, name, hosted=None):
    T = qkv.shape[0]
    C, H, Dh = GDN_CHUNK, GDN_HEADS, GDN_D
    N = T // C

    def body(qkv_ref, gz_ref, gab_ref, a_ref, dt_ref, nw_ref, opre_ref, Tm_ref, Sin_ref, do_ref,
             dqkv_ref, dgz_ref, dgab_ref, da_ref, ddt_ref, dnw_ref, dS_ref):
        n = pl.program_id(0)

        @pl.when(n == 0)
        def _():
            dS_ref[...] = jnp.zeros_like(dS_ref)
            da_ref[...] = jnp.zeros_like(da_ref)
            ddt_ref[...] = jnp.zeros_like(ddt_ref)
            dnw_ref[...] = jnp.zeros_like(dnw_ref)

        incl, strict, eye = _chunk_masks(C)
        gab = gab_ref[...]
        g_all, beta_all, gc_all = _gdn_gates(gab, a_ref[...], dt_ref[...], incl)
        lane = _iota2((C, LANES), 1)
        rowi = _iota2((C, 1), 0)
        dqkv_ref[...] = jnp.zeros_like(dqkv_ref)
        dgz_ref[...] = jnp.zeros_like(dgz_ref)
        nw = nw_ref[...]
        q, k, v, gc, beta, decay, e_gc, e_gl, ekd, kb, A, Pm = _gdn_common(qkv_ref, gc_all, beta_all, incl, strict, eye)
        Tm = Tm_ref[0]
        S = Sin_ref[0]
        dS = dS_ref[...]
        kbe = kb * e_gc
        u = _hbmm(Tm, v * beta)
        w = _hbmm(Tm, kbe)
        qd = q * e_gc
        kd = k * ekd
        v_new = u - _bmm(w, S)
        o = _stack_heads(opre_ref, 0, H)
        gz = _stack_heads(gz_ref, 0, H)
        don = _stack_heads(do_ref, 0, H)
        r = lax.rsqrt(jnp.mean(o * o, axis=-1, keepdims=True) + RMS_EPS)
        nn = o * r
        sgz = _sigmoid(gz)
        silu = gz * sgz
        _unstack_heads(dgz_ref, 0, don * nn * nw * (sgz * (1.0 + gz * (1.0 - sgz))))
        dnn = don * nw * silu
        dnw_ref[...] += jnp.sum(jnp.sum(don * nn * silu, axis=0), axis=0, keepdims=True)
        do = r * (dnn - nn * jnp.mean(dnn * nn, axis=-1, keepdims=True))
        dv_new = _bmm(Pm, do, BTN) + _bmm(kd, dS)
        dPm = jnp.where(incl, _bmm(do, v_new, BNT), 0.0)
        dqd = _bmm(do, S, BNT)
        dkd = _bmm(v_new, dS, BNT)
        dS_ref[...] = _bmm(qd, do, BTN) + e_gl * dS - _bmm(w, dv_new, BTN)
        dgl = jnp.sum(jnp.sum(dS * S, axis=2, keepdims=True), axis=1, keepdims=True) * e_gl
        dw = -_bmm(dv_new, S, BNT)
        dvb = _hbmm_tn(Tm, dv_new)
        dkbe = _hbmm_tn(Tm, dw)
        dA = -jnp.where(strict, _bmm(dvb, u, BNT) + _bmm(dkbe, w, BNT), 0.0)
        dAD = dA * decay
        dPD = dPm * decay
        Gm = dA * A + dPm * Pm
        dgc = jnp.sum(Gm, axis=2, keepdims=True) - _row_to_col(jnp.sum(Gm, axis=1, keepdims=True), eye)
        dkb = _bmm(dAD, k) + dkbe * e_gc
        dk = _bmm(dAD, kb, BTN) + _bmm(dPD, q, BTN) + dkd * ekd + dkb * beta
        dq = _bmm(dPD, k) + dqd * e_gc
        tkd = jnp.sum(dkd * kd, axis=-1, keepdims=True)
        dgc = dgc + jnp.sum(dqd * qd, axis=-1, keepdims=True) - tkd + jnp.sum(dkbe * kbe, axis=-1, keepdims=True)
        dgl = dgl + jnp.sum(tkd, axis=1, keepdims=True)
        dgc = dgc + jnp.where(rowi == C - 1, dgl, 0.0)
        dbeta = jnp.sum(dvb * v, axis=-1, keepdims=True) + jnp.sum(dkb * k, axis=-1, keepdims=True)
        _unstack_heads(dqkv_ref, 0, dq)
        _unstack_heads(dqkv_ref, H, dk)
        _unstack_heads(dqkv_ref, 2 * H, dvb * beta)
        dgc_all = jnp.zeros((C, LANES), F32)
        dbeta_all = jnp.zeros((C, LANES), F32)
        for h in range(H):
            dgc_all = dgc_all + jnp.where(lane == h, dgc[h], 0.0)
            dbeta_all = dbeta_all + jnp.where(lane == H + h, dbeta[h], 0.0)
        upper = (_iota2((C, C), 0) <= _iota2((C, C), 1)).astype(BF16)
        dg_all = _ones_dot_left(upper, dgc_all)
        dga = dg_all * (-jnp.exp(a_ref[...])) * _sigmoid(gab + dt_ref[...])
        dgb = dbeta_all * beta_all * (1.0 - beta_all)
        dgab_ref[...] = jnp.where(lane < H, dga, jnp.where(lane < 2 * H, dgb, 0.0))
        da_ref[...] += jnp.sum(jnp.where(lane < H, dg_all * g_all, 0.0), axis=0, keepdims=True)
        ddt_ref[...] += jnp.sum(jnp.where(lane < H, dga, 0.0), axis=0, keepdims=True)

    rev = lambda n: N - 1 - n
    vec = pl.BlockSpec((1, LANES), lambda n: (0, 0))
    nwv = pl.BlockSpec((1, Dh), lambda n: (0, 0))
    hblk = pl.BlockSpec((C, H * LANES), lambda n: (rev(n), 0))
    sblk = pl.BlockSpec((1, H, Dh, Dh), lambda n: (rev(n), 0, 0, 0))
    qblk = pl.BlockSpec((C, GDN_QKV_BLOCKS * LANES), lambda n: (rev(n), 0))
    return _hosted_call(
        hosted, body, grid=(N,),
        in_specs=[qblk, pl.BlockSpec((C, H * LANES), lambda n: (rev(n), CB_GZ // H)),
                  pl.BlockSpec((C, LANES), lambda n: (rev(n), CB_GAB)), vec, vec, nwv, hblk, sblk, sblk, hblk],
        out_specs=[qblk, hblk, pl.BlockSpec((C, LANES), lambda n: (rev(n), 0)), vec, vec, nwv],
        out_shape=[jax.ShapeDtypeStruct((T, GDN_QKV_BLOCKS * LANES), F32), jax.ShapeDtypeStruct((T, H * LANES), F32),
                   jax.ShapeDtypeStruct((T, LANES), F32), jax.ShapeDtypeStruct((1, LANES), F32),
                   jax.ShapeDtypeStruct((1, LANES), F32), jax.ShapeDtypeStruct((1, Dh), F32)],
        scratch_shapes=[pltpu.VMEM((H, Dh, Dh), F32)],
        compiler_params=_cp(("arbitrary",)), name=name)(qkv, proj, proj, a_row, dt_row, norm_w, opre, Tm_all, Sin_all, docat)


ATT_BQ, ATT_BK = 256, 512
NEG_BIG = -1e30


def _att_blocks(T):
    bq, bk = min(ATT_BQ, T), min(ATT_BK, T)
    assert bk % bq == 0 and T % bk == 0
    return bq, bk


def _att_specs(T, bq, cbs):
    qspec = lambda cb: pl.BlockSpec((bq, LANES), lambda h, i: (i, cb + h))
    kspec = lambda cb: pl.BlockSpec((T, LANES), lambda h, i: (0, cb + h))
    return qspec, kspec


def _kblock(ref, kb, bk):
    return ref[pl.ds(pl.multiple_of(kb * bk, bk), bk), :]


def _att_pos(i, kb, bq, bk):
    qpos = i * bq + _iota2((bq, bk), 0)
    kpos = kb * bk + _iota2((bq, bk), 1)
    return qpos, kpos


def _suffix_sum(x):
    n = x.shape[1]
    lane = _iota2(x.shape, 1)
    d = 1
    while d < n:
        x = x + jnp.where(lane < n - d, pltpu.roll(x, n - d, 1), 0.0)
        d *= 2
    return x


def _prefix_sum(x):
    n = x.shape[1]
    lane = _iota2(x.shape, 1)
    d = 1
    while d < n:
        x = x + jnp.where(lane >= d, pltpu.roll(x, d, 1), 0.0)
        d *= 2
    return x


SB_BLOCK = 256
SB_DEAD = -104.0


def _sb_blocks(T):
    b = min(SB_BLOCK, T)
    assert T % b == 0 and T // b <= LANES
    return b, b


def sb_fwd(proj, name, hosted=None):
    T = proj.shape[0]
    H = SB_HEADS
    bq, bk = _sb_blocks(T)
    scale = SB_DIM ** -0.5

    def body(q_ref, k_ref, v_ref, o_ref, tot_ref):
        i = pl.program_id(1)
        qb = q_ref[...].astype(BF16)
        diag = (i * bq) // bk
        lane = _iota2((bq, LANES), 1)

        def block(kb, acc, R, masked):
            z = _bdot_nt(qb, _kblock(k_ref, kb, bk)) * scale
            sp = _softplus(z)
            if masked:
                qpos, kpos = _att_pos(i, kb, bq, bk)
                mask = kpos < qpos
                l1m = jnp.where(mask, -sp, 0.0)
            else:
                l1m = -sp
            W = jnp.exp((z - sp) + (_suffix_sum(l1m) - l1m) + R)
            if masked:
                W = jnp.where(mask, W, 0.0)
            acc = acc + _bdot(W, _kblock(v_ref, kb, bk))
            return acc, R + jnp.sum(l1m, axis=-1, keepdims=True)

        acc, R = block(diag, jnp.zeros((bq, LANES), F32), jnp.zeros((bq, 1), F32), True)

        def live(c):
            return jnp.logical_and(c[0] >= 0, jnp.max(c[2]) > SB_DEAD)

        def step(c):
            kb, acc, R, Rb = c
            acc, R_next = block(kb, acc, R, False)
            return kb - 1, acc, R_next, jnp.where(lane == kb, R, Rb)

        _, acc, _, Rb = lax.while_loop(live, step, (diag - 1, acc, R, jnp.where(lane == diag, 0.0, NEG_BIG)))
        o_ref[...] = acc
        tot_ref[...] = Rb

    qspec, kspec = _att_specs(T, bq, None)
    sds = jax.ShapeDtypeStruct((T, H * LANES), F32)
    oblk = pl.BlockSpec((bq, LANES), lambda h, i: (i, h))
    return _hosted_call(
        hosted, body, grid=(H, T // bq), in_specs=[qspec(CB_SQ), kspec(CB_SK), kspec(CB_SV)],
        out_specs=[oblk, oblk], out_shape=[sds, sds],
        compiler_params=_cp(("parallel", "parallel")), name=name)(proj, proj, proj)


def sb_bwd(proj, tot, docat, do_cb, name):
    T = proj.shape[0]
    H = SB_HEADS
    bq, bk = _sb_blocks(T)
    scale = SB_DIM ** -0.5

    def body(q_ref, k_ref, v_ref, tot_ref, do_ref, dq_ref, dk_ref, dv_ref):
        i = pl.program_id(1)

        @pl.when(i == 0)
        def _():
            dk_ref[...] = jnp.zeros_like(dk_ref)
            dv_ref[...] = jnp.zeros_like(dv_ref)

        qb = q_ref[...].astype(BF16)
        dob = do_ref[...].astype(BF16)
        Rb = tot_ref[...]
        diag = (i * bq) // bk
        lane = _iota2((bq, LANES), 1)
        first = lax.while_loop(
            lambda kb: jnp.logical_and(kb < diag, jnp.max(jnp.where(lane == kb, Rb, NEG_BIG)) <= SB_DEAD),
            lambda kb: kb + 1, jnp.int32(0))

        def block(kb, carry, masked):
            dq, Epre = carry
            R = jnp.sum(jnp.where(lane == kb, Rb, 0.0), axis=1, keepdims=True)
            kblk = _kblock(k_ref, kb, bk).astype(BF16)
            z = _bdot_nt(qb, kblk) * scale
            sp = _softplus(z)
            if masked:
                qpos, kpos = _att_pos(i, kb, bq, bk)
                mask = kpos < qpos
                l1m = jnp.where(mask, -sp, 0.0)
            else:
                l1m = -sp
            W = jnp.exp((z - sp) + (_suffix_sum(l1m) - l1m) + R)
            if masked:
                W = jnp.where(mask, W, 0.0)
            E = _bdot_nt(dob, _kblock(v_ref, kb, bk)) * W
            cexcl = (_prefix_sum(E) - E) + Epre
            neg = jnp.exp(-sp)
            dz = E * neg - cexcl * (1.0 - neg)
            if masked:
                dz = jnp.where(mask, dz, 0.0)
            dz = (dz * scale).astype(BF16)
            rows = pl.ds(pl.multiple_of(kb * bk, bk), bk)
            dk_ref[rows, :] += lax.dot_general(dz, qb, TN_DIMS, preferred_element_type=F32)
            dv_ref[rows, :] += lax.dot_general(W.astype(BF16), dob, TN_DIMS, preferred_element_type=F32)
            dq = dq + jnp.dot(dz, kblk, preferred_element_type=F32)
            return dq, Epre + jnp.sum(E, axis=-1, keepdims=True)

        init = (jnp.zeros((bq, LANES), F32), jnp.zeros((bq, 1), F32))
        carry = lax.fori_loop(first, diag, lambda kb, c: block(kb, c, False), init)
        dq, _ = block(diag, carry, True)
        dq_ref[...] = dq

    qspec, kspec = _att_specs(T, bq, None)
    sds = jax.ShapeDtypeStruct((T, H * LANES), F32)
    oblk = pl.BlockSpec((bq, LANES), lambda h, i: (i, h))
    kout = pl.BlockSpec((T, LANES), lambda h, i: (0, h))
    return pl.pallas_call(
        body, grid=(H, T // bq),
        in_specs=[qspec(CB_SQ), kspec(CB_SK), kspec(CB_SV), oblk, qspec(do_cb)],
        out_specs=[oblk, kout, kout], out_shape=[sds, sds, sds],
        compiler_params=_cp(("arbitrary", "arbitrary")), name=name)(proj, proj, proj, tot, docat)


def mla_fwd(Q, K, V, name, hosted=None):
    T = Q.shape[0]
    H = MLA_HEADS
    bq, bk = _att_blocks(T)
    scale = (MLA_NOPE + MLA_ROPE) ** -0.5

    def body(q_ref, k_ref, v_ref, o_ref, lse_ref):
        i = pl.program_id(1)
        qb = q_ref[...]
        diag = (i * bq) // bk

        def block(kb, carry, masked):
            acc, m, l = carry
            s = _bdot_nt(qb, _kblock(k_ref, kb, bk)) * scale
            if masked:
                qpos, kpos = _att_pos(i, kb, bq, bk)
                s = jnp.where(kpos <= qpos, s, NEG_BIG)
            m_new = jnp.maximum(m, jnp.max(s, axis=-1, keepdims=True))
            p = jnp.exp(s - m_new)
            corr = jnp.exp(m - m_new)
            acc = corr * acc + _bdot(p, _kblock(v_ref, kb, bk))
            return acc, m_new, corr * l + jnp.sum(p, axis=-1, keepdims=True)

        init = (jnp.zeros((bq, LANES), F32), jnp.full((bq, 1), NEG_BIG, F32), jnp.zeros((bq, 1), F32))
        carry = lax.fori_loop(0, diag, lambda kb, c: block(kb, c, False), init)
        acc, m, l = block(diag, carry, True)
        o_ref[...] = acc / l
        lse_ref[...] = jnp.broadcast_to(m + jnp.log(l), (bq, LANES))

    qspec, kspec = _att_specs(T, bq, None)
    sds = jax.ShapeDtypeStruct((T, H * LANES), F32)
    oblk = pl.BlockSpec((bq, LANES), lambda h, i: (i, h))
    return _hosted_call(
        hosted, body, grid=(H, T // bq), in_specs=[qspec(0), kspec(0), kspec(0)],
        out_specs=[oblk, oblk], out_shape=[sds, sds],
        compiler_params=_cp(("parallel", "parallel")), name=name)(Q, K, V)


def mla_bwd(Q, K, V, o, lse, docat, do_cb, name, hosted=None):
    T = Q.shape[0]
    H = MLA_HEADS
    bq, bk = _att_blocks(T)
    scale = (MLA_NOPE + MLA_ROPE) ** -0.5

    def body(q_ref, k_ref, v_ref, o_ref, lse_ref, do_ref, dq_ref, dk_ref, dv_ref):
        i = pl.program_id(1)

        @pl.when(i == 0)
        def _():
            dk_ref[...] = jnp.zeros_like(dk_ref)
            dv_ref[...] = jnp.zeros_like(dv_ref)

        qb = q_ref[...]
        do = do_ref[...]
        dob = do.astype(BF16)
        delta = jnp.sum(do * o_ref[...], axis=-1, keepdims=True)
        lse = lse_ref[:, 0:1]

        diag = (i * bq) // bk

        def block(kb, dq, masked):
            kblk = _kblock(k_ref, kb, bk)
            s = _bdot_nt(qb, kblk) * scale
            if masked:
                qpos, kpos = _att_pos(i, kb, bq, bk)
                s = jnp.where(kpos <= qpos, s, NEG_BIG)
            p = jnp.exp(s - lse)
            dp = _bdot_nt(dob, _kblock(v_ref, kb, bk))
            ds = (p * (dp - delta) * scale).astype(BF16)
            rows = pl.ds(pl.multiple_of(kb * bk, bk), bk)
            dk_ref[rows, :] += lax.dot_general(ds, qb, TN_DIMS, preferred_element_type=F32)
            dv_ref[rows, :] += lax.dot_general(p.astype(BF16), dob, TN_DIMS, preferred_element_type=F32)
            return dq + jnp.dot(ds, kblk, preferred_element_type=F32)

        dq = lax.fori_loop(0, diag, lambda kb, c: block(kb, c, False), jnp.zeros((bq, LANES), F32))
        dq_ref[...] = block(diag, dq, True)

    qspec, kspec = _att_specs(T, bq, None)
    sds = jax.ShapeDtypeStruct((T, H * LANES), F32)
    oblk = pl.BlockSpec((bq, LANES), lambda h, i: (i, h))
    kout = pl.BlockSpec((T, LANES), lambda h, i: (0, h))
    return _hosted_call(
        hosted, body, grid=(H, T // bq),
        in_specs=[qspec(0), kspec(0), kspec(0), oblk, oblk, qspec(do_cb)],
        out_specs=[oblk, kout, kout], out_shape=[sds, sds, sds],
        compiler_params=_cp(("arbitrary", "arbitrary")), name=name)(Q, K, V, o, lse, docat)


def _tile_heads(t, n):
    return jnp.concatenate([t] * n, axis=1)


def _rope(X, C, Sn, Sp):
    n = X.shape[1]
    return X * C + pltpu.roll(X, n - HALF_ROPE, 1) * Sn + pltpu.roll(X, HALF_ROPE, 1) * Sp


def _rope_t(dO, C, Sn, Sp):
    n = dO.shape[1]
    return dO * C + pltpu.roll(dO * Sn, HALF_ROPE, 1) + pltpu.roll(dO * Sp, n - HALF_ROPE, 1)


def _rms(x, w):
    r = lax.rsqrt(jnp.mean(x * x, axis=-1, keepdims=True) + RMS_EPS)
    xh = x * r
    return r, xh, xh * w


def _rms_bwd(dn, w, r, xh):
    dxh = dn * w
    return r * (dxh - xh * jnp.mean(dxh * xh, axis=-1, keepdims=True)), jnp.sum(dn * xh, axis=0, keepdims=True)


def _mla_pre_specs(T, tm):
    KV = MLA_KV_RANK
    QR = MLA_Q_RANK
    W = MLA_HEADS * LANES
    full = lambda shape: pl.BlockSpec(shape, lambda i: (0, 0))
    specs = [pl.BlockSpec((tm, QR), lambda i: (i, CB_MQ * LANES // QR)),
             pl.BlockSpec((tm, 2 * LANES), lambda i: (i, CB_MKV // 2)),
             full((1, QR)), full((1, KV))]
    rope = [pl.BlockSpec((tm, LANES), lambda i: (i, 0))] * 3
    return specs, rope, full, W


def mla_pre_fwd(proj, wq, wkv, wuq, wuk, wuv, ropeC, ropeSn, ropeSp, name):
    T = proj.shape[0]
    tm = _pick(T, 512, 16)
    KV = MLA_KV_RANK
    H = MLA_HEADS

    def body(mq_ref, mkv_ref, wq_ref, wkv_ref, wuq_ref, wuk_ref, wuv_ref, c_ref, sn_ref, sp_ref, Q_ref, K_ref, V_ref):
        C, Sn, Sp = (_tile_heads(t[...], H) for t in (c_ref, sn_ref, sp_ref))
        _, _, qn = _rms(mq_ref[...], wq_ref[...])
        Q_ref[...] = _rope(_bdot(qn, wuq_ref[...]), C, Sn, Sp).astype(BF16)
        mkv = mkv_ref[...]
        _, _, kvn = _rms(mkv[:, :KV], wkv_ref[...])
        kr = pltpu.roll(mkv[:, KV:], MLA_NOPE, 1)
        K_ref[...] = _rope(_bdot(kvn, wuk_ref[...]) + _tile_heads(kr, H), C, Sn, Sp).astype(BF16)
        V_ref[...] = _bdot(kvn, wuv_ref[...]).astype(BF16)

    specs, rope, full, W = _mla_pre_specs(T, tm)
    oblk = pl.BlockSpec((tm, W), lambda i: (i, 0))
    sds = jax.ShapeDtypeStruct((T, W), BF16)
    return pl.pallas_call(
        body, grid=(T // tm,),
        in_specs=specs + [full(wuq.shape), full(wuk.shape), full(wuv.shape)] + rope,
        out_specs=[oblk, oblk, oblk], out_shape=[sds, sds, sds],
        compiler_params=_cp(("parallel",)), name=name)(proj, proj, wq, wkv, wuq, wuk, wuv, ropeC, ropeSn, ropeSp)


def mla_pre_bwd(proj, wq, wkv, wuq, wuk, wuv, ropeC, ropeSn, ropeSp, dQ, dK, dV, name):
    T = proj.shape[0]
    tm = _pick(T, 512, 16)
    KV = MLA_KV_RANK
    H = MLA_HEADS

    def body(mq_ref, mkv_ref, wq_ref, wkv_ref, wuq_ref, wuk_ref, wuv_ref,
             c_ref, sn_ref, sp_ref, dQ_ref, dK_ref, dV_ref,
             dmq_ref, dmkv_ref, dwuq_ref, dwuk_ref, dwuv_ref, dwq_ref, dwkv_ref):
        i = pl.program_id(0)

        @pl.when(i == 0)
        def _():
            for ref in (dwuq_ref, dwuk_ref, dwuv_ref, dwq_ref, dwkv_ref):
                ref[...] = jnp.zeros_like(ref)

        C, Sn, Sp = (_tile_heads(t[...], H) for t in (c_ref, sn_ref, sp_ref))
        rq, xq, qn = _rms(mq_ref[...], wq_ref[...])
        mkv = mkv_ref[...]
        rkv, xkv, kvn = _rms(mkv[:, :KV], wkv_ref[...])
        dqf = _rope_t(dQ_ref[...], C, Sn, Sp)
        dkf = _rope_t(dK_ref[...], C, Sn, Sp)
        dv = dV_ref[...]
        dwuq_ref[...] += _bdot_tn(qn, dqf)
        dwuk_ref[...] += _bdot_tn(kvn, dkf)
        dwuv_ref[...] += _bdot_tn(kvn, dv)
        dmq, dwq = _rms_bwd(_bdot_nt(dqf, wuq_ref[...]), wq_ref[...], rq, xq)
        dckv, dwkv = _rms_bwd(_bdot_nt(dkf, wuk_ref[...]) + _bdot_nt(dv, wuv_ref[...]), wkv_ref[...], rkv, xkv)
        dwq_ref[...] += dwq
        dwkv_ref[...] += dwkv
        dmq_ref[...] = dmq
        dkr = dkf[:, 0:LANES]
        for h in range(1, H):
            dkr = dkr + dkf[:, h * LANES:(h + 1) * LANES]
        dkr = pltpu.roll(dkr, LANES - MLA_NOPE, 1)
        dkr = jnp.where(_iota2(dkr.shape, 1) < MLA_ROPE, dkr, 0.0)
        dmkv_ref[...] = jnp.concatenate([dckv, dkr], axis=1)

    specs, rope, full, W = _mla_pre_specs(T, tm)
    wide = pl.BlockSpec((tm, W), lambda i: (i, 0))
    return pl.pallas_call(
        body, grid=(T // tm,),
        in_specs=specs + [full(w.shape) for w in (wuq, wuk, wuv)] + rope + [wide, wide, wide],
        out_specs=[pl.BlockSpec((tm, MLA_Q_RANK), lambda i: (i, 0)), pl.BlockSpec((tm, 2 * LANES), lambda i: (i, 0)),
                   full(wuq.shape), full(wuk.shape), full(wuv.shape), full((1, MLA_Q_RANK)), full((1, KV))],
        out_shape=[jax.ShapeDtypeStruct((T, MLA_Q_RANK), F32), jax.ShapeDtypeStruct((T, 2 * LANES), F32),
                   jax.ShapeDtypeStruct(wuq.shape, F32), jax.ShapeDtypeStruct(wuk.shape, F32),
                   jax.ShapeDtypeStruct(wuv.shape, F32), jax.ShapeDtypeStruct((1, MLA_Q_RANK), F32),
                   jax.ShapeDtypeStruct((1, KV), F32)],
        compiler_params=_cp(("arbitrary",)), name=name)(
            proj, proj, wq, wkv, wuq, wuk, wuv, ropeC, ropeSn, ropeSp, dQ, dK, dV)


def all_gather(shards, name):
    n = len(shards)

    def body(*refs):
        x_refs, out_refs = refs[:n], refs[n:2 * n]
        send_sems, recv_sems, local_sems = refs[2 * n:]
        x, y, c = _place()
        me, sibling = (x, y, c), (x, y, 1 - c)
        chips = [(1 - x, y), (x, 1 - y), (1 - x, 1 - y)]

        def slot(a, px, py, pc):
            return out_refs[a].at[4 * px + 2 * py + pc]

        def copy(a, k, block, to, src=None):
            return pltpu.make_async_remote_copy(
                src_ref=slot(a, *block) if src is None else src, dst_ref=slot(a, *block),
                send_sem=send_sems.at[a, k], recv_sem=recv_sems.at[a, k], device_id=to, device_id_type=MESH)

        mine = [pltpu.make_async_copy(x_refs[a], slot(a, *me), local_sems.at[a]) for a in range(n)]
        first = []
        for a in range(n):
            mine[a].start()
            first.append(copy(a, 0, me, sibling, src=x_refs[a]))
            first += [copy(a, 1 + j, me, (*chip, c), src=x_refs[a]) for j, chip in enumerate(chips)]
        for cp in first:
            cp.start()
        passed = []
        for j, chip in enumerate(chips):
            for a in range(n):
                copy(a, 1 + j, (*chip, c), me).wait_recv()
                passed.append(copy(a, 4 + j, (*chip, c), sibling))
                passed[-1].start()
        for a in range(n):
            copy(a, 0, sibling, me).wait_recv()
            for j, chip in enumerate(chips):
                copy(a, 4 + j, (*chip, 1 - c), me).wait_recv()
        for cp in first + passed:
            cp.wait_send()
        for cp in mine:
            cp.wait()

    return pl.pallas_call(
        body, out_shape=[jax.ShapeDtypeStruct((N_DEV,) + s.shape, s.dtype) for s in shards],
        in_specs=[ANY] * n, out_specs=[ANY] * n,
        scratch_shapes=[pltpu.SemaphoreType.DMA((n, 7)), pltpu.SemaphoreType.DMA((n, 7)), pltpu.SemaphoreType.DMA((n,))],
        name=name)(*shards)


def exchange_partials(parts, name):
    n = len(parts)

    def body(*refs):
        src_refs, dst_refs = refs[:n], refs[n:2 * n]
        send_sems, recv_sems, local_sems = refs[2 * n:]
        x, y, c = _place()
        me = 4 * x + 2 * y + c
        copies = []
        mine = []
        for a in range(n):
            mine.append(pltpu.make_async_copy(src_refs[a].at[me], dst_refs[a].at[me], local_sems.at[a]))
            for k in range(1, N_DEV):
                px = 1 - x if k & 4 else x
                py = 1 - y if k & 2 else y
                pc = 1 - c if k & 1 else c
                copies.append(pltpu.make_async_remote_copy(
                    src_ref=src_refs[a].at[4 * px + 2 * py + pc], dst_ref=dst_refs[a].at[me],
                    send_sem=send_sems.at[a, k - 1], recv_sem=recv_sems.at[a, k - 1],
                    device_id=(px, py, pc), device_id_type=MESH))
        for cp in mine + copies:
            cp.start()
        for cp in copies:
            cp.wait_recv()
        for cp in copies:
            cp.wait_send()
        for cp in mine:
            cp.wait()

    return pl.pallas_call(
        body, out_shape=[jax.ShapeDtypeStruct(p.shape, p.dtype) for p in parts],
        in_specs=[ANY] * n, out_specs=[ANY] * n,
        scratch_shapes=[pltpu.SemaphoreType.DMA((n, 7)), pltpu.SemaphoreType.DMA((n, 7)), pltpu.SemaphoreType.DMA((n,))],
        name=name)(*parts)


def reduce_adamw(parts, w, m, v, name):
    L = len(parts)
    n, Rl, C = parts[0].shape
    R = w.shape[0]
    assert R == L * Rl
    tr = Rl if Rl * C <= 256 * 1024 else _pick(Rl, 256, 16)
    nr = Rl // tr

    def body(*refs):
        p_refs = refs[:L]
        w_ref, m_ref, v_ref, g_ref, d_ref, nm_ref, nv_ref, sum_ref = refs[L:]
        grp = pl.program_id(0)
        for j in range(L):
            @pl.when(grp == j)
            def _(j=j):
                acc = p_refs[j][0].astype(F32)
                for s in range(1, n):
                    acc = acc + p_refs[j][s].astype(F32)
                sum_ref[...] = acc

        g_ = sum_ref[...]
        m_ = ADAM_B1 * m_ref[...] + (1.0 - ADAM_B1) * g_
        v_ = ADAM_B2 * v_ref[...] + (1.0 - ADAM_B2) * (g_ * g_)
        m_hat = m_ / (1.0 - ADAM_B1 ** ADAM_STEP)
        v_hat = v_ / (1.0 - ADAM_B2 ** ADAM_STEP)
        g_ref[...] = g_
        d_ref[...] = -ADAM_LR * (m_hat / (jnp.sqrt(v_hat) + ADAM_EPS) + ADAM_WD * w_ref[...])
        nm_ref[...] = m_
        nv_ref[...] = v_

    blk = pl.BlockSpec((tr, C), lambda l, r: (l * nr + r, 0))
    sds = jax.ShapeDtypeStruct((R, C), F32)
    p_specs = [pl.BlockSpec((n, tr, C), lambda l, r, j=j: (0, jnp.where(l == j, r, 0), 0)) for j in range(L)]
    return pl.pallas_call(
        body, grid=(L, nr), in_specs=p_specs + [blk] * 3,
        out_specs=[blk] * 4, out_shape=[sds] * 4, scratch_shapes=[pltpu.VMEM((tr, C), F32)],
        compiler_params=_cp(("arbitrary", "arbitrary")), name=name)(*parts, w, m, v)


SHARDED = {"ffa_w_in": (2, BF16), "ffa_w_out": (1, BF16), "mix_w_in": (2, BF16), "mla_w_uq": (2, BF16),
           "mla_w_ukv": (2, BF16), "mix_w_o": (1, BF16), "ffb_w_in": (2, BF16), "ffb_w_out": (1, BF16),
           "ple_w_gate": (1, BF16), "ple_w_proj": (2, BF16), "gdn_conv_w": (2, F32), "ln_g": (2, F32), "ln_b": (2, F32)}
FFN_SLOT = ("ffa_w_in", "ffa_w_out", "ffb_w_in", "ffb_w_out")
REPLICATED = ("gdn_a_log", "gdn_dt_bias", "gdn_norm_w", "mla_q_norm_w", "mla_kv_norm_w")
WEIGHTS = ("ffa_w_in", "ffa_w_out", "mix_w_in", "gdn_conv_w", "gdn_a_log", "gdn_dt_bias", "gdn_norm_w", "mla_q_norm_w",
           "mla_kv_norm_w", "mla_w_uq", "mla_w_ukv", "mix_w_o", "ffb_w_in", "ffb_w_out", "ln_g", "ln_b", "ple_w_gate",
           "ple_w_proj")


def _to_slots(full, axis):
    L, a, b = full.shape
    if axis == 2:
        return full.reshape(L, a, N_DEV, b // N_DEV).transpose(2, 0, 1, 3).reshape(N_DEV, L * a, b // N_DEV)
    return full.reshape(L, N_DEV, a // N_DEV, b).transpose(1, 0, 2, 3).reshape(N_DEV, L * a // N_DEV, b)


def _from_slots(slots, shard_shape, axis):
    L, a, b = shard_shape
    t = slots.reshape((N_DEV,) + tuple(shard_shape))
    if axis == 2:
        return t.transpose(1, 2, 0, 3).reshape(L, a, N_DEV * b)
    return t.transpose(1, 0, 2, 3).reshape(L, N_DEV * a, b)


def _view2d(t):
    return t.reshape(-1, t.shape[-1])


def _pad_heads(w, nh):
    K = w.shape[0]
    return jnp.pad(w.reshape(K, nh, GDN_D), ((0, 0), (0, 0), (0, LANES - GDN_D))).reshape(K, nh * LANES)


def _unpad_heads(w, nh):
    K = w.shape[0]
    return w.reshape(K, nh, LANES)[:, :, :GDN_D].reshape(K, nh * GDN_D)


IN_WIDTHS = (512, 512, 512, 512, 8, 8, 256, 256, 256, 256, 160)


def _split_in(w):
    offs = np.cumsum((0,) + IN_WIDTHS)
    return [w[:, int(offs[i]):int(offs[i + 1])] for i in range(len(IN_WIDTHS))]


def _pad_in_proj(w):
    gq, gk, gv, gz, ga, gb, sq, sk, sv, mq, mkv = _split_in(w)
    K = w.shape[0]
    gab = jnp.pad(jnp.concatenate([ga, gb], axis=1), ((0, 0), (0, LANES - 2 * GDN_HEADS)))
    return jnp.concatenate(
        [_pad_heads(t, GDN_HEADS) for t in (gq, gk, gv, gz)] + [_pad_heads(t, SB_HEADS) for t in (sq, sk, sv)]
        + [mq, jnp.pad(mkv, ((0, 0), (0, 2 * LANES - mkv.shape[1]))), gab], axis=1)


def _unpad_in_proj(wp):
    c = lambda cb, n: wp[:, cb * LANES:(cb + n) * LANES]
    gab = c(CB_GAB, 1)
    parts = [_unpad_heads(c(cb, GDN_HEADS), GDN_HEADS) for cb in (CB_GQ, CB_GK, CB_GV, CB_GZ)]
    parts += [gab[:, :GDN_HEADS], gab[:, GDN_HEADS:2 * GDN_HEADS]]
    parts += [_unpad_heads(c(cb, SB_HEADS), SB_HEADS) for cb in (CB_SQ, CB_SK, CB_SV)]
    parts += [c(CB_MQ, 2), c(CB_MKV, 2)[:, :MLA_KV_RANK + MLA_ROPE]]
    return jnp.concatenate(parts, axis=1)


def _pad_lanes(w, width):
    return jnp.pad(w, ((0, 0), (0, width - w.shape[1])))


def _mla_up_pad(w_uq, w_ukv):
    H = MLA_HEADS
    dq = MLA_NOPE + MLA_ROPE
    wuq = jnp.pad(w_uq.reshape(-1, H, dq), ((0, 0), (0, 0), (0, LANES - dq))).reshape(-1, H * LANES)
    kv = w_ukv.reshape(-1, H, MLA_NOPE + MLA_V)
    wuk = jnp.pad(kv[:, :, :MLA_NOPE], ((0, 0), (0, 0), (0, LANES - MLA_NOPE))).reshape(-1, H * LANES)
    wuv = jnp.pad(kv[:, :, MLA_NOPE:], ((0, 0), (0, 0), (0, LANES - MLA_V))).reshape(-1, H * LANES)
    return wuq, wuk, wuv


def _mla_up_unpad(dwuq, dwuk, dwuv):
    H = MLA_HEADS
    dq = MLA_NOPE + MLA_ROPE
    g_uq = dwuq.reshape(-1, H, LANES)[:, :, :dq].reshape(-1, H * dq)
    g_ukv = jnp.concatenate([dwuk.reshape(-1, H, LANES)[:, :, :MLA_NOPE], dwuv.reshape(-1, H, LANES)[:, :, :MLA_V]],
                            axis=2).reshape(-1, H * (MLA_NOPE + MLA_V))
    return g_uq, g_ukv


def _rope_tables(positions):
    inv = 1.0 / (ROPE_BASE ** (jnp.arange(0, MLA_ROPE, 2, dtype=F32) / MLA_ROPE))
    ang = positions.astype(F32)[:, None] * inv
    cos, sin = jnp.cos(ang), jnp.sin(ang)
    T = positions.shape[0]
    one = lambda n: jnp.ones((T, n), F32)
    zero = lambda n: jnp.zeros((T, n), F32)
    tail = LANES - MLA_NOPE - MLA_ROPE
    C = jnp.concatenate([one(MLA_NOPE), cos, cos, one(tail)], axis=1)
    Sn = jnp.concatenate([zero(MLA_NOPE), -sin, zero(HALF_ROPE + tail)], axis=1)
    Sp = jnp.concatenate([zero(MLA_NOPE + HALF_ROPE), sin, zero(tail)], axis=1)
    return C, Sn, Sp


GATHER_FIRST = [("ffa_w_in", 0), ("ffa_w_out", 0)] + [(n, l) for l in range(DEPTH) for n in ("gdn_conv_w", "ln_g", "ln_b")]
GATHER_PLAN = {
    (0, "ffa_fwd"): [("mix_w_in", 0), ("mla_w_uq", 0), ("mla_w_ukv", 0)],
    (0, "gdn_chunk_fwd"): [("mix_w_o", 0), ("ffb_w_in", 0)],
    (0, "sb_fwd"): [("ffb_w_out", 0), ("ple_w_gate", 0), ("ple_w_proj", 0)],
    (0, "mla_fwd"): [("ffa_w_in", 1), ("mix_w_o", 1)],
    (0, "ffb_fwd"): [("ffa_w_out", 1), ("mix_w_in", 1)],
    (1, "ffa_fwd"): [("ffb_w_in", 1)],
    (1, "in_proj"): [("ffb_w_out", 1), ("ple_w_gate", 1), ("ple_w_proj", 1), ("mla_w_uq", 1), ("mla_w_ukv", 1)],
}
SCATTER_PLAN = {
    (1, "gdn_chunk_bwd"): [("ffb_w_in", 1)],
    (1, "gdn_pre_bwd"): [("ffb_w_out", 1), ("ple_w_gate", 1), ("ple_w_proj", 1), ("mix_w_o", 1)],
    (1, "ffa_bwd"): [("mix_w_in", 1), ("mla_w_uq", 1), ("mla_w_ukv", 1), ("gdn_conv_w", 1)],
    (0, "ffb_bwd"): [("ffa_w_in", 1)],
    (0, "gdn_chunk_bwd"): [("ffb_w_in", 0)],
    (0, "gdn_pre_bwd"): [("ffb_w_out", 0), ("ple_w_gate", 0), ("ple_w_proj", 0), ("mix_w_o", 0)],
    (0, "mla_bwd"): [("ffa_w_out", 1), ("ln_g", 1), ("ln_b", 1)],
    (0, "ffa_bwd"): [("mix_w_in", 0), ("mla_w_uq", 0), ("mla_w_ukv", 0), ("gdn_conv_w", 0)],
}
SCATTER_LAST = [("ffa_w_in", 0), ("ffa_w_out", 0), ("ln_g", 0), ("ln_b", 0)]


class Exchanges:
    def __init__(self, shards):
        self.shards = shards
        self.full = {}
        self.partial = {}
        self.received = {}

    def _block(self, key):
        n, l = key
        return self.shards[n][l].astype(SHARDED[n][1])

    def _absorb_gather(self, keys, results):
        for (n, l), g in zip(keys, results):
            blk = self.shards[n][l]
            self.full[(n, l)] = g if n in FFN_SLOT else _from_slots(g, (1,) + blk.shape, SHARDED[n][0])[0]

    def gather_now(self, keys, name):
        self._absorb_gather(keys, all_gather([self._block(k) for k in keys], name))

    def gather_with(self, layer, tag):
        keys = GATHER_PLAN.get((layer, tag))
        return None if keys is None else (keys, Hosted("gather", [self._block(k) for k in keys]))

    def scatter_with(self, layer, tag):
        keys = SCATTER_PLAN.get((layer, tag))
        return None if keys is None else (keys, Hosted("scatter", [self.partial[k] for k in keys]))

    def done(self, carried):
        if carried is not None:
            keys, hosted = carried
            if hosted.kind == "gather":
                self._absorb_gather(keys, hosted.results)
            else:
                self.received.update(zip(keys, hosted.results))

    def add_grad(self, key, g):
        n, l = key
        self.partial[key] = g if n in FFN_SLOT else _to_slots(g[None], SHARDED[n][0]).astype(SHARDED[n][1])


def _carried(c):
    return None if c is None else c[1]


def _layer_fwd(h0, p_i, rope, i, ex, rep):
    L = "L%d_" % i
    S = {"h0": h0, "p": p_i}
    W = ex.full
    ln_g = [W[("ln_g", i)][j][None, :] for j in range(3)]
    ln_b = [W[("ln_b", i)][j][None, :] for j in range(3)]
    S["ln_g"] = ln_g
    c = ex.gather_with(i, "ffa_fwd")
    S["h1"], S["xh1"], S["rs1"], S["Ga"], S["Ua"] = ffn_fwd(h0, W[("ffa_w_in", i)], W[("ffa_w_out", i)], ln_g[0], ln_b[0],
                                                            L + "ffa_fwd", hosted=_carried(c))
    ex.done(c)
    S["win"] = _pad_in_proj(W[("mix_w_in", i)])
    c = ex.gather_with(i, "in_proj")
    S["proj"] = mm_nn(S["h1"], S["win"], L + "in_proj", hosted=_carried(c))
    ex.done(c)
    S["conv"] = _pad_heads(W[("gdn_conv_w", i)], GDN_QKV_BLOCKS)
    S["a_row"] = _pad_lanes(rep["gdn_a_log"][i][None, :], LANES)
    S["dt_row"] = _pad_lanes(rep["gdn_dt_bias"][i][None, :], LANES)
    S["nw"] = rep["gdn_norm_w"][i][None, :]
    S["wq"] = rep["mla_q_norm_w"][i][None, :]
    S["wkv"] = rep["mla_kv_norm_w"][i][None, :]
    S["qkv"] = gdn_pre_fwd(S["proj"], S["conv"], L + "gdn_pre_fwd")
    c = ex.gather_with(i, "gdn_chunk_fwd")
    S["o_gdn"], S["opre"], S["Tm"], S["Sin"] = gdn_chunk_fwd(S["qkv"], S["proj"], S["a_row"], S["dt_row"], S["nw"],
                                                            L + "gdn_chunk_fwd", hosted=_carried(c))
    ex.done(c)
    c = ex.gather_with(i, "sb_fwd")
    S["o_sb"], S["tot"] = sb_fwd(S["proj"], L + "sb_fwd", hosted=_carried(c))
    ex.done(c)
    S["wuq"], S["wuk"], S["wuv"] = _mla_up_pad(W[("mla_w_uq", i)], W[("mla_w_ukv", i)])
    S["Q"], S["K"], S["V"] = mla_pre_fwd(S["proj"], S["wq"], S["wkv"], S["wuq"], S["wuk"], S["wuv"], *rope, L + "mla_pre_fwd")
    c = ex.gather_with(i, "mla_fwd")
    S["o_mla"], S["lse"] = mla_fwd(S["Q"], S["K"], S["V"], L + "mla_fwd", hosted=_carried(c))
    ex.done(c)
    wo = W[("mix_w_o", i)]
    S["wo"] = jnp.pad(wo.reshape(-1, GDN_D, wo.shape[1]), ((0, 0), (0, LANES - GDN_D), (0, 0))).reshape(-1, wo.shape[1])
    S["h2"], S["xh2"], S["rs2"] = proj_res_ln([S["o_gdn"], S["o_sb"], S["o_mla"]], S["wo"], S["h1"],
                                              ln_g[1], ln_b[1], L + "out_proj")
    c = ex.gather_with(i, "ffb_fwd")
    S["h3"], S["xh3"], S["rs3"], S["Gb"], S["Ub"] = ffn_fwd(S["h2"], W[("ffb_w_in", i)], W[("ffb_w_out", i)], ln_g[2], ln_b[2],
                                                            L + "ffb_fwd", hosted=_carried(c))
    ex.done(c)
    h4, S["a"], S["e"] = ple_fwd(S["h3"], p_i, W[("ple_w_gate", i)], W[("ple_w_proj", i)], L + "ple_fwd")
    return h4, S


def _layer_bwd(dh4, S, rope, i, ex):
    L = "L%d_" % i
    W = ex.full
    Grep = {}
    dh3, da, de = ple_bwd(dh4, S["a"], S["e"], W[("ple_w_gate", i)], L + "ple_bwd")
    ex.add_grad(("ple_w_gate", i), mm_tn(S["h3"], da, L + "d_ple_gate"))
    ex.add_grad(("ple_w_proj", i), mm_tn(S["p"], de, L + "d_ple_proj"))
    dz3, dg2, db2 = ln_bwd(dh3, S["xh3"], S["rs3"], S["ln_g"][2], L + "ln3_bwd")
    c = ex.scatter_with(i, "ffb_bwd")
    dh2, dGb, dUb, actb = ffn_bwd(dz3, S["Gb"], S["Ub"], W[("ffb_w_in", i)], W[("ffb_w_out", i)], L + "ffb_bwd",
                                  hosted=_carried(c))
    ex.done(c)
    ex.add_grad(("ffb_w_in", i), ffn_dw_in(S["h2"].T.astype(BF16), dGb, dUb, L + "d_ffb_in"))
    ex.add_grad(("ffb_w_out", i), ffn_dw_out(actb, dz3, L + "d_ffb_out"))
    dz2, dg1, db1 = ln_bwd(dh2, S["xh2"], S["rs2"], S["ln_g"][1], L + "ln2_bwd")
    docat = mm_nn(dz2, S["wo"], L + "d_ocat", b_transposed=True)
    dwo = jnp.concatenate([mm_tn(S["o_gdn"], dz2, L + "d_wo_gdn"), mm_tn(S["o_sb"], dz2, L + "d_wo_sb"),
                           mm_tn(S["o_mla"], dz2, L + "d_wo_mla")], axis=0)
    ex.add_grad(("mix_w_o", i), dwo.reshape(-1, LANES, dwo.shape[1])[:, :GDN_D, :].reshape(-1, dwo.shape[1]))
    c = ex.scatter_with(i, "gdn_chunk_bwd")
    dqkv, dgz, dgab, d_alog, d_dt, d_nw = gdn_chunk_bwd(S["qkv"], S["proj"], S["a_row"], S["dt_row"], S["nw"],
                                                        S["opre"], S["Tm"], S["Sin"], docat, L + "gdn_chunk_bwd",
                                                        hosted=_carried(c))
    ex.done(c)
    c = ex.scatter_with(i, "gdn_pre_bwd")
    dc, dconv = gdn_pre_bwd_pointwise(S["proj"], S["conv"], dqkv, L + "gdn_pre_bwd", hosted=_carried(c))
    ex.done(c)
    dxqkv = gdn_pre_bwd_conv(dc, S["conv"], L + "gdn_conv_bwd")
    ex.add_grad(("gdn_conv_w", i), _unpad_heads(dconv, GDN_QKV_BLOCKS))
    Grep["gdn_a_log"], Grep["gdn_dt_bias"], Grep["gdn_norm_w"] = d_alog[0, :GDN_HEADS], d_dt[0, :GDN_HEADS], d_nw[0]
    dsq, dsk, dsv = sb_bwd(S["proj"], S["tot"], docat, GDN_HEADS, L + "sb_bwd")
    c = ex.scatter_with(i, "mla_bwd")
    dQ, dK, dV = mla_bwd(S["Q"], S["K"], S["V"], S["o_mla"], S["lse"], docat, GDN_HEADS + SB_HEADS, L + "mla_bwd",
                         hosted=_carried(c))
    ex.done(c)
    dmq, dmkv, dwuq, dwuk, dwuv, dwq, dwkv = mla_pre_bwd(
        S["proj"], S["wq"], S["wkv"], S["wuq"], S["wuk"], S["wuv"], *rope, dQ, dK, dV, L + "mla_pre_bwd")
    g_uq, g_ukv = _mla_up_unpad(dwuq, dwuk, dwuv)
    ex.add_grad(("mla_w_uq", i), g_uq)
    ex.add_grad(("mla_w_ukv", i), g_ukv)
    Grep["mla_q_norm_w"], Grep["mla_kv_norm_w"] = dwq[0], dwkv[0]
    dproj = jnp.concatenate([dxqkv, dgz, dsq, dsk, dsv, dmq, dmkv, dgab], axis=1).astype(BF16)
    ex.add_grad(("mix_w_in", i),
                _unpad_in_proj(mm_tn(S["h1"].T.astype(BF16), dproj, L + "d_in_proj", a_transposed=True)))
    dh1 = mm_nn(dproj, S["win"], L + "d_h1", res=dz2, res_scale=ALPHA, b_transposed=True)
    dz1, dg0, db0 = ln_bwd(dh1, S["xh1"], S["rs1"], S["ln_g"][0], L + "ln1_bwd")
    c = ex.scatter_with(i, "ffa_bwd")
    dh0, dGa, dUa, acta = ffn_bwd(dz1, S["Ga"], S["Ua"], W[("ffa_w_in", i)], W[("ffa_w_out", i)], L + "ffa_bwd",
                                  hosted=_carried(c))
    ex.done(c)
    ex.add_grad(("ffa_w_in", i), ffn_dw_in(S["h0"].T.astype(BF16), dGa, dUa, L + "d_ffa_in"))
    ex.add_grad(("ffa_w_out", i), ffn_dw_out(acta, dz1, L + "d_ffa_out"))
    ex.add_grad(("ln_g", i), jnp.concatenate([dg0, dg1, dg2], axis=0))
    ex.add_grad(("ln_b", i), jnp.concatenate([db0, db1, db2], axis=0))
    return dh0, Grep


def _local_step(x, p, positions, target, ex, rep):
    assert DEPTH == 2
    rope = _rope_tables(positions)
    h, saved = x, []
    for i in range(DEPTH):
        h, S = _layer_fwd(h, p[i], rope, i, ex, rep)
        saved.append(S)
    loss, dh = loss_head(h, target, "loss_head")
    grads = [None] * DEPTH
    for i in reversed(range(DEPTH)):
        dh, grads[i] = _layer_bwd(dh, saved[i], rope, i, ex)
    return loss, dh, {n: jnp.stack([grads[i][n] for i in range(DEPTH)]) for n in REPLICATED}


def kernel(x, p, positions, ffa_w_in, ffa_w_out, mix_w_in, gdn_conv_w, gdn_a_log, gdn_dt_bias, gdn_norm_w, mla_q_norm_w, mla_kv_norm_w, mla_w_uq, mla_w_ukv, mix_w_o, ffb_w_in, ffb_w_out, ln_g, ln_b, ple_w_gate, ple_w_proj, loss_target, m_ffa_w_in, m_ffa_w_out, m_mix_w_in, m_gdn_conv_w, m_gdn_a_log, m_gdn_dt_bias, m_gdn_norm_w, m_mla_q_norm_w, m_mla_kv_norm_w, m_mla_w_uq, m_mla_w_ukv, m_mix_w_o, m_ffb_w_in, m_ffb_w_out, m_ln_g, m_ln_b, m_ple_w_gate, m_ple_w_proj, v_ffa_w_in, v_ffa_w_out, v_mix_w_in, v_gdn_conv_w, v_gdn_a_log, v_gdn_dt_bias, v_gdn_norm_w, v_mla_q_norm_w, v_mla_kv_norm_w, v_mla_w_uq, v_mla_w_ukv, v_mix_w_o, v_ffb_w_in, v_ffb_w_out, v_ln_g, v_ln_b, v_ple_w_gate, v_ple_w_proj):
    given = dict(locals())
    shards = {n: given[n] for n in WEIGHTS}
    ex = Exchanges({n: shards[n] for n in SHARDED})
    ex.gather_now(GATHER_FIRST, "gather_first")
    loss, grad_x, Grep = _local_step(x[0], p[:, 0], positions[0], loss_target[0], ex, {n: shards[n] for n in REPLICATED})
    loss = lax.psum(loss[0, 0], ("x", "y", "c"))
    ex.received.update(zip(SCATTER_LAST, exchange_partials([ex.partial[k] for k in SCATTER_LAST], "scatter_last")))
    rep_received = dict(zip(REPLICATED, all_gather([Grep[n] for n in REPLICATED], "gather_replicated_grads")))
    grad, delta, new_m, new_v = {}, {}, {}, {}
    for n in WEIGHTS:
        shape = shards[n].shape
        parts = [rep_received[n]] if n in REPLICATED else [ex.received[(n, l)] for l in range(DEPTH)]
        if parts[0].shape[1] % 8:
            parts = [jnp.concatenate(parts, axis=1)]
        outs = reduce_adamw(parts, _view2d(shards[n]), _view2d(given["m_" + n]), _view2d(given["v_" + n]),
                            "adamw_" + n)
        grad[n], delta[n], new_m[n], new_v[n] = (t.reshape(shape) for t in outs)
    return (loss, grad_x[None], *[grad[n] for n in WEIGHTS], *[delta[n] for n in WEIGHTS],
            *[new_m[n] for n in WEIGHTS], *[new_v[n] for n in WEIGHTS])
```

```python
import functools
import numpy as np
import jax
import jax.numpy as jnp
from jax import lax
from jax.experimental import pallas as pl
from jax.experimental.pallas import tpu as pltpu

F32 = jnp.float32
BF16 = jnp.bfloat16

DEPTH = 2
LN_EPS = 1e-5
RMS_EPS = 1e-6
ALPHA = (2 * DEPTH) ** 0.25
GDN_HEADS, GDN_D, GDN_CONV, GDN_CHUNK = 8, 64, 4, 64
SB_HEADS, SB_DIM = 4, 64
MLA_HEADS, MLA_NOPE, MLA_ROPE, MLA_V, MLA_Q_RANK, MLA_KV_RANK = 4, 64, 32, 64, 256, 128
ROPE_BASE = 10000.0
HALF_ROPE = MLA_ROPE // 2
LANES = 128
N_DEV = 8
ADAM_LR, ADAM_B1, ADAM_B2, ADAM_EPS, ADAM_WD, ADAM_STEP = 0.001, 0.9, 0.999, 1e-08, 0.01, 10

CB_GQ, CB_GK, CB_GV, CB_GZ = 0, 8, 16, 24
CB_SQ, CB_SK, CB_SV = 32, 36, 40
CB_MQ, CB_MKV, CB_GAB = 44, 46, 48
PROJ_W = 49 * LANES
VMEM_LIMIT = 56 * 1024 * 1024

NT_DIMS = (((1,), (1,)), ((), ()))
TN_DIMS = (((0,), (0,)), ((), ()))


def _cp(sem):
    return pltpu.CompilerParams(dimension_semantics=sem, vmem_limit_bytes=VMEM_LIMIT)


def _bdot(a, b):
    return jnp.dot(a.astype(BF16), b.astype(BF16), preferred_element_type=F32)


def _bdot_nt(a, b):
    return lax.dot_general(a.astype(BF16), b.astype(BF16), NT_DIMS, preferred_element_type=F32)


def _bdot_tn(a, b):
    return lax.dot_general(a.astype(BF16), b.astype(BF16), TN_DIMS, preferred_element_type=F32)


def _split2(a):
    hi = a.astype(BF16)
    lo = (a - hi.astype(F32)).astype(BF16)
    return hi, lo


def _ones_dot_left(ones_bf16, x):
    hi = x.astype(BF16)
    r1 = x - hi.astype(F32)
    mid = r1.astype(BF16)
    lo = (r1 - mid.astype(F32)).astype(BF16)
    d = functools.partial(jnp.dot, preferred_element_type=F32)
    return d(ones_bf16, hi) + d(ones_bf16, mid) + d(ones_bf16, lo)


def _iota2(shape, dim):
    return lax.broadcasted_iota(jnp.int32, shape, dim)


def _sigmoid(x):
    return 0.5 * jnp.tanh(0.5 * x) + 0.5


def _softplus(x):
    return jnp.maximum(x, 0.0) + jnp.log(1.0 + jnp.exp(-jnp.abs(x)))


def _pick(n, limit, mult):
    if n <= limit:
        return n
    best = None
    for t in range(mult, limit + 1, mult):
        if n % t == 0:
            best = t
    assert best is not None, (n, limit, mult)
    return best


MESH = pl.DeviceIdType.MESH
ANY = pl.BlockSpec(memory_space=pl.ANY)


def _place():
    return lax.axis_index("x"), lax.axis_index("y"), lax.axis_index("c")


def _peer(k):
    x, y, c = _place()
    return (1 - x if k & 4 else x, 1 - y if k & 2 else y, 1 - c if k & 1 else c)


class Hosted:
    def __init__(self, kind, arrays):
        self.kind, self.arrays, self.n, self.results = kind, list(arrays), len(arrays), None

    def out_shapes(self):
        if self.kind == "gather":
            return [jax.ShapeDtypeStruct((N_DEV,) + a.shape, a.dtype) for a in self.arrays]
        return [jax.ShapeDtypeStruct(a.shape, a.dtype) for a in self.arrays]

    def sems(self):
        return [pltpu.SemaphoreType.DMA((self.n, N_DEV - 1)), pltpu.SemaphoreType.DMA((self.n, N_DEV - 1)),
                pltpu.SemaphoreType.DMA((self.n,))]

    def _copies(self, src_refs, dst_refs, send_sems, recv_sems, local_sems):
        x, y, c = _place()
        me = 4 * x + 2 * y + c
        local, remote = [], []
        for a in range(self.n):
            gather = self.kind == "gather"
            local.append(pltpu.make_async_copy(src_refs[a] if gather else src_refs[a].at[me], dst_refs[a].at[me],
                                               local_sems.at[a]))
            for k in range(1, N_DEV):
                px, py, pc = _peer(k)
                remote.append(pltpu.make_async_remote_copy(
                    src_ref=src_refs[a] if gather else src_refs[a].at[4 * px + 2 * py + pc], dst_ref=dst_refs[a].at[me],
                    send_sem=send_sems.at[a, k - 1], recv_sem=recv_sems.at[a, k - 1],
                    device_id=(px, py, pc), device_id_type=MESH))
        return local, remote

    def start(self, *refs):
        local, remote = self._copies(*refs)
        for cp in local + remote:
            cp.start()

    def wait(self, *refs):
        local, remote = self._copies(*refs)
        for cp in remote:
            cp.wait_recv()
        for cp in remote:
            cp.wait_send()
        for cp in local:
            cp.wait()


def _hosted_call(hosted, body, *, grid, in_specs, out_specs, out_shape, scratch_shapes=(), compiler_params, name):
    if hosted is None:
        return pl.pallas_call(body, grid=grid, in_specs=in_specs, out_specs=out_specs, out_shape=out_shape,
                              scratch_shapes=scratch_shapes, compiler_params=compiler_params, name=name)
    single = not isinstance(out_shape, (list, tuple))
    o_specs = [out_specs] if single else list(out_specs)
    o_shape = [out_shape] if single else list(out_shape)
    n_in, n_out, n_scr, n = len(in_specs), len(o_specs), len(scratch_shapes), hosted.n

    def wrapped(*refs):
        ins, c_in = refs[:n_in], refs[n_in:n_in + n]
        outs, c_out = refs[n_in + n:n_in + n + n_out], refs[n_in + n + n_out:n_in + 2 * n + n_out]
        rest = refs[n_in + 2 * n + n_out:]
        scr, sems = rest[:n_scr], rest[n_scr:]
        ids = [pl.program_id(ax) for ax in range(len(grid))]
        first = functools.reduce(jnp.logical_and, [i == 0 for i in ids])
        last = functools.reduce(jnp.logical_and, [i == g - 1 for i, g in zip(ids, grid)])

        @pl.when(first)
        def _():
            hosted.start(c_in, c_out, *sems)

        body(*ins, *outs, *scr)

        @pl.when(last)
        def _():
            hosted.wait(c_in, c_out, *sems)

    call = pl.pallas_call(
        wrapped, grid=grid, in_specs=list(in_specs) + [ANY] * n, out_specs=o_specs + [ANY] * n,
        out_shape=o_shape + hosted.out_shapes(), scratch_shapes=list(scratch_shapes) + hosted.sems(),
        compiler_params=_cp(("arbitrary",) * len(grid)), name=name)

    def run(*args):
        outs = call(*args, *hosted.arrays)
        hosted.results = list(outs[n_out:])
        return outs[0] if single else list(outs[:n_out])

    return run


def mm_nn(a, b, name, out_dtype=F32, res=None, res_scale=1.0, b_transposed=False, hosted=None):
    M, K = a.shape
    N = b.shape[0] if b_transposed else b.shape[1]
    tm, tn, tk = _pick(M, 512, 16), _pick(N, 1024, LANES), _pick(K, 1024, LANES)
    nk = K // tk
    has_res = res is not None
    dot = _bdot_nt if b_transposed else _bdot

    def body(*refs):
        if has_res:
            a_ref, b_ref, r_ref, o_ref, acc_ref = refs
        else:
            a_ref, b_ref, o_ref, acc_ref = refs
        k = pl.program_id(2)

        @pl.when(k == 0)
        def _():
            acc_ref[...] = jnp.zeros_like(acc_ref)

        acc_ref[...] += dot(a_ref[...], b_ref[...])

        @pl.when(k == nk - 1)
        def _():
            out = acc_ref[...]
            if has_res:
                out = out + res_scale * r_ref[...]
            o_ref[...] = out.astype(o_ref.dtype)

    b_spec = pl.BlockSpec((tn, tk), lambda i, j, k: (j, k)) if b_transposed else pl.BlockSpec((tk, tn), lambda i, j, k: (k, j))
    in_specs = [pl.BlockSpec((tm, tk), lambda i, j, k: (i, k)), b_spec]
    args = [a, b]
    if has_res:
        in_specs.append(pl.BlockSpec((tm, tn), lambda i, j, k: (i, j)))
        args.append(res)
    return _hosted_call(
        hosted, body, grid=(M // tm, N // tn, nk), in_specs=in_specs,
        out_specs=pl.BlockSpec((tm, tn), lambda i, j, k: (i, j)),
        out_shape=jax.ShapeDtypeStruct((M, N), out_dtype),
        scratch_shapes=[pltpu.VMEM((tm, tn), F32)],
        compiler_params=_cp(("parallel", "parallel", "arbitrary")), name=name)(*args)


def mm_tn(a, b, name, out_dtype=F32, a_transposed=False):
    K, T = a.shape if a_transposed else a.shape[::-1]
    _, N = b.shape
    tk = K if a_transposed else _pick(K, 512, LANES)
    tn, tt = _pick(N, 1024, LANES), _pick(T, 512, LANES)
    nt = T // tt

    def body(a_ref, b_ref, o_ref, acc_ref):
        t = pl.program_id(2)

        @pl.when(t == 0)
        def _():
            acc_ref[...] = jnp.zeros_like(acc_ref)

        if a_transposed:
            acc_ref[...] += _bdot(a_ref[:, pl.ds(pl.multiple_of(t * tt, tt), tt)], b_ref[...])
        else:
            acc_ref[...] += _bdot_tn(a_ref[...], b_ref[...])

        @pl.when(t == nt - 1)
        def _():
            o_ref[...] = acc_ref[...].astype(o_ref.dtype)

    a_spec = pl.BlockSpec((K, T), lambda i, j, t: (0, 0)) if a_transposed else pl.BlockSpec((tt, tk), lambda i, j, t: (t, i))
    return pl.pallas_call(
        body, grid=(K // tk, N // tn, nt),
        in_specs=[a_spec, pl.BlockSpec((tt, tn), lambda i, j, t: (t, j))],
        out_specs=pl.BlockSpec((tk, tn), lambda i, j, t: (i, j)),
        out_shape=jax.ShapeDtypeStruct((K, N), out_dtype),
        scratch_shapes=[pltpu.VMEM((tk, tn), F32)],
        compiler_params=_cp(("parallel", "parallel", "arbitrary")), name=name)(a, b)


def _ln_apply(z, g, b):
    mu = jnp.mean(z, axis=-1, keepdims=True)
    zc = z - mu
    var = jnp.mean(zc * zc, axis=-1, keepdims=True)
    rstd = lax.rsqrt(var + LN_EPS)
    xhat = zc * rstd
    return xhat * g + b, xhat, rstd


def ln_bwd(dout, xhat, rstd, g, name):
    T, D = dout.shape
    tm = _pick(T, 512, 8)

    def body(do_ref, xh_ref, rs_ref, g_ref, dz_ref, dg_ref, db_ref):
        i = pl.program_id(0)

        @pl.when(i == 0)
        def _():
            dg_ref[...] = jnp.zeros_like(dg_ref)
            db_ref[...] = jnp.zeros_like(db_ref)

        do = do_ref[...]
        xh = xh_ref[...]
        dxh = do * g_ref[...]
        m1 = jnp.mean(dxh, axis=-1, keepdims=True)
        m2 = jnp.mean(dxh * xh, axis=-1, keepdims=True)
        dz_ref[...] = rs_ref[...] * (dxh - m1 - xh * m2)
        dg_ref[...] += jnp.sum(do * xh, axis=0, keepdims=True)
        db_ref[...] += jnp.sum(do, axis=0, keepdims=True)

    row = pl.BlockSpec((tm, D), lambda i: (i, 0))
    vec = pl.BlockSpec((1, D), lambda i: (0, 0))
    return pl.pallas_call(
        body, grid=(T // tm,),
        in_specs=[row, row, pl.BlockSpec((tm, 1), lambda i: (i, 0)), vec],
        out_specs=[row, vec, vec],
        out_shape=[jax.ShapeDtypeStruct((T, D), F32), jax.ShapeDtypeStruct((1, D), F32), jax.ShapeDtypeStruct((1, D), F32)],
        compiler_params=_cp(("arbitrary",)), name=name)(dout, xhat, rstd, g)


FFN_CHUNKS = N_DEV // 2


def ffn_fwd(h, w_in, w_out, g, b, name, hosted=None):
    T, D = h.shape
    fc = w_in.shape[2]
    half = w_out.shape[1]
    tm = _pick(T, 512, 8)
    nc = FFN_CHUNKS

    def body(h_ref, wg_ref, wu_ref, wo_ref, g_ref, b_ref, out_ref, xh_ref, rs_ref, G_ref, U_ref, acc_ref):
        c = pl.program_id(1)

        @pl.when(c == 0)
        def _():
            acc_ref[...] = jnp.zeros_like(acc_ref)

        hb = h_ref[...].astype(BF16)
        G = jnp.dot(hb, wg_ref[0], preferred_element_type=F32)
        U = jnp.dot(hb, wu_ref[0], preferred_element_type=F32)
        G_ref[0] = G
        U_ref[0] = U
        act = G * _sigmoid(G) * U
        acc_ref[...] += _bdot(act, wo_ref[...].reshape(2 * half, D))

        @pl.when(c == nc - 1)
        def _():
            z = ALPHA * h_ref[...] + 0.5 * acc_ref[...]
            out, xh, rs = _ln_apply(z, g_ref[...], b_ref[...])
            out_ref[...] = out
            xh_ref[...] = xh
            rs_ref[...] = rs

    row = pl.BlockSpec((tm, D), lambda i, c: (i, 0))
    vec = pl.BlockSpec((1, D), lambda i, c: (0, 0))
    cblk = pl.BlockSpec((1, tm, fc), lambda i, c: (c, i, 0))
    csds = jax.ShapeDtypeStruct((nc, T, fc), F32)
    return _hosted_call(
        hosted, body, grid=(T // tm, nc),
        in_specs=[row, pl.BlockSpec((1, D, fc), lambda i, c: (c, 0, 0)),
                  pl.BlockSpec((1, D, fc), lambda i, c: (c + nc, 0, 0)),
                  pl.BlockSpec((2, half, D), lambda i, c: (c, 0, 0)), vec, vec],
        out_specs=[row, row, pl.BlockSpec((tm, 1), lambda i, c: (i, 0)), cblk, cblk],
        out_shape=[jax.ShapeDtypeStruct((T, D), F32), jax.ShapeDtypeStruct((T, D), F32), jax.ShapeDtypeStruct((T, 1), F32),
                   csds, csds],
        scratch_shapes=[pltpu.VMEM((tm, D), F32)],
        compiler_params=_cp(("parallel", "arbitrary")), name=name)(h, w_in, w_in, w_out, g, b)


def ffn_bwd(dz, G, U, w_in, w_out, name, hosted=None):
    T, D = dz.shape
    nc, _, fc = G.shape
    half = w_out.shape[1]
    tm = _pick(T, 512, 16)

    def body(dz_ref, G_ref, U_ref, wg_ref, wu_ref, wo_ref, dh_ref, dG_ref, dU_ref, act_ref, acc_ref):
        c = pl.program_id(1)

        @pl.when(c == 0)
        def _():
            acc_ref[...] = jnp.zeros_like(acc_ref)

        dy = (0.5 * dz_ref[...]).astype(BF16)
        dact = _bdot_nt(dy, wo_ref[...].reshape(2 * half, D))
        G = G_ref[0]
        U = U_ref[0]
        s = _sigmoid(G)
        silu = G * s
        dG = (dact * U * (s * (1.0 + G * (1.0 - s)))).astype(BF16)
        dU = (dact * silu).astype(BF16)
        dG_ref[0] = dG
        dU_ref[0] = dU
        act_ref[0] = (silu * U).astype(BF16)
        acc_ref[...] += _bdot_nt(dG, wg_ref[0]) + _bdot_nt(dU, wu_ref[0])

        @pl.when(c == nc - 1)
        def _():
            dh_ref[...] = ALPHA * dz_ref[...] + acc_ref[...]

    row = pl.BlockSpec((tm, D), lambda i, c: (i, 0))
    cblk = pl.BlockSpec((1, tm, fc), lambda i, c: (c, i, 0))
    csds = jax.ShapeDtypeStruct((nc, T, fc), BF16)
    return _hosted_call(
        hosted, body, grid=(T // tm, nc),
        in_specs=[row, cblk, cblk, pl.BlockSpec((1, D, fc), lambda i, c: (c, 0, 0)),
                  pl.BlockSpec((1, D, fc), lambda i, c: (c + nc, 0, 0)),
                  pl.BlockSpec((2, half, D), lambda i, c: (c, 0, 0))],
        out_specs=[row, cblk, cblk, cblk],
        out_shape=[jax.ShapeDtypeStruct((T, D), F32), csds, csds, csds],
        scratch_shapes=[pltpu.VMEM((tm, D), F32)],
        compiler_params=_cp(("parallel", "arbitrary")), name=name)(dz, G, U, w_in, w_in, w_out)


def ffn_dw_in(h_t, dG, dU, name, hosted=None):
    D, T = h_t.shape
    nc, _, fc = dG.shape
    tt = _pick(T, 512, LANES)
    nt = T // tt

    def body(h_ref, dG_ref, dU_ref, o_ref, acc_ref):
        s = pl.program_id(0)
        t = pl.program_id(1)

        @pl.when(t == 0)
        def _():
            acc_ref[...] = jnp.zeros_like(acc_ref)

        hb = h_ref[:, pl.ds(pl.multiple_of(t * tt, tt), tt)]

        @pl.when(s < nc)
        def _():
            acc_ref[...] += jnp.dot(hb, dG_ref[0], preferred_element_type=F32)

        @pl.when(s >= nc)
        def _():
            acc_ref[...] += jnp.dot(hb, dU_ref[0], preferred_element_type=F32)

        @pl.when(t == nt - 1)
        def _():
            o_ref[0] = acc_ref[...].astype(o_ref.dtype)

    return _hosted_call(
        hosted, body, grid=(2 * nc, nt),
        in_specs=[pl.BlockSpec((D, T), lambda s, t: (0, 0)),
                  pl.BlockSpec((1, tt, fc), lambda s, t: (jnp.minimum(s, nc - 1), jnp.where(s < nc, t, nt - 1), 0)),
                  pl.BlockSpec((1, tt, fc), lambda s, t: (jnp.maximum(s - nc, 0), jnp.where(s >= nc, t, 0), 0))],
        out_specs=pl.BlockSpec((1, D, fc), lambda s, t: (s, 0, 0)),
        out_shape=jax.ShapeDtypeStruct((2 * nc, D, fc), BF16),
        scratch_shapes=[pltpu.VMEM((D, fc), F32)],
        compiler_params=_cp(("parallel", "arbitrary")), name=name)(h_t, dG, dU)


def ffn_dw_out(act, dz, name, hosted=None):
    nc, T, fc = act.shape
    D = dz.shape[1]
    half = fc // 2
    tt = _pick(T, 512, 16)
    nt = T // tt

    def body(a_ref, dz_ref, o_ref, acc_ref):
        t = pl.program_id(1)

        @pl.when(t == 0)
        def _():
            acc_ref[...] = jnp.zeros_like(acc_ref)

        acc_ref[...] += _bdot_tn(a_ref[0], dz_ref[...])

        @pl.when(t == nt - 1)
        def _():
            o_ref[...] = (0.5 * acc_ref[...]).reshape(2, half, D).astype(o_ref.dtype)

    return _hosted_call(
        hosted, body, grid=(nc, nt),
        in_specs=[pl.BlockSpec((1, tt, fc), lambda c, t: (c, t, 0)), pl.BlockSpec((tt, D), lambda c, t: (t, 0))],
        out_specs=pl.BlockSpec((2, half, D), lambda c, t: (c, 0, 0)),
        out_shape=jax.ShapeDtypeStruct((2 * nc, half, D), BF16),
        scratch_shapes=[pltpu.VMEM((fc, D), F32)],
        compiler_params=_cp(("parallel", "arbitrary")), name=name)(act, dz)


def proj_res_ln(parts, w, res, g, b, name):
    T, D = res.shape
    tm = _pick(T, 512, 8)
    widths = [p.shape[1] for p in parts]
    offs = [int(sum(widths[:i])) for i in range(len(parts))]
    n = len(parts)

    def body(*refs):
        p_refs = refs[:n]
        w_ref, r_ref, g_ref, b_ref, out_ref, xh_ref, rs_ref = refs[n:]
        acc = ALPHA * r_ref[...]
        for p_ref, o, wd in zip(p_refs, offs, widths):
            acc = acc + _bdot(p_ref[...], w_ref[o:o + wd, :])
        out, xh, rs = _ln_apply(acc, g_ref[...], b_ref[...])
        out_ref[...] = out
        xh_ref[...] = xh
        rs_ref[...] = rs

    row = pl.BlockSpec((tm, D), lambda i: (i, 0))
    vec = pl.BlockSpec((1, D), lambda i: (0, 0))
    return pl.pallas_call(
        body, grid=(T // tm,),
        in_specs=[pl.BlockSpec((tm, wd), lambda i: (i, 0)) for wd in widths]
        + [pl.BlockSpec(w.shape, lambda i: (0, 0)), row, vec, vec],
        out_specs=[row, row, pl.BlockSpec((tm, 1), lambda i: (i, 0))],
        out_shape=[jax.ShapeDtypeStruct((T, D), F32), jax.ShapeDtypeStruct((T, D), F32), jax.ShapeDtypeStruct((T, 1), F32)],
        compiler_params=_cp(("parallel",)), name=name)(*parts, w, res, g, b)


def ple_fwd(h, p, wg, wp, name):
    T, D = h.shape
    P = p.shape[1]
    tm, tn = _pick(T, 512, 8), _pick(D, 512, LANES)

    def body(h_ref, hn_ref, p_ref, wg_ref, wp_ref, out_ref, a_ref, e_ref):
        a = _bdot(h_ref[...], wg_ref[...])
        e = _bdot(p_ref[...], wp_ref[...])
        a_ref[...] = a
        e_ref[...] = e
        out_ref[...] = hn_ref[...] + _sigmoid(a) * e

    blk = pl.BlockSpec((tm, tn), lambda i, j: (i, j))
    sds = jax.ShapeDtypeStruct((T, D), F32)
    return pl.pallas_call(
        body, grid=(T // tm, D // tn),
        in_specs=[pl.BlockSpec((tm, D), lambda i, j: (i, 0)), blk, pl.BlockSpec((tm, P), lambda i, j: (i, 0)),
                  pl.BlockSpec((D, tn), lambda i, j: (0, j)), pl.BlockSpec((P, tn), lambda i, j: (0, j))],
        out_specs=[blk, blk, blk], out_shape=[sds, sds, sds],
        compiler_params=_cp(("parallel", "parallel")), name=name)(h, h, p, wg, wp)


def ple_bwd(dout, a, e, wg, name):
    T, D = dout.shape
    tm = _pick(T, 512, 16)

    def body(do_ref, a_ref, e_ref, wg_ref, dh_ref, da_ref, de_ref):
        do = do_ref[...]
        s = _sigmoid(a_ref[...])
        da = (do * e_ref[...] * s * (1.0 - s)).astype(BF16)
        da_ref[...] = da
        de_ref[...] = (do * s).astype(BF16)
        dh_ref[...] = do + _bdot_nt(da, wg_ref[...])

    row = pl.BlockSpec((tm, D), lambda i: (i, 0))
    return pl.pallas_call(
        body, grid=(T // tm,),
        in_specs=[row, row, row, pl.BlockSpec((D, D), lambda i: (0, 0))],
        out_specs=[row, row, row],
        out_shape=[jax.ShapeDtypeStruct((T, D), F32), jax.ShapeDtypeStruct((T, D), BF16), jax.ShapeDtypeStruct((T, D), BF16)],
        compiler_params=_cp(("parallel",)), name=name)(dout, a, e, wg)


def loss_head(y, target, name):
    T, D = y.shape
    tm = _pick(T, 512, 8)

    def body(y_ref, t_ref, loss_ref, dy_ref):
        i = pl.program_id(0)

        @pl.when(i == 0)
        def _():
            loss_ref[...] = jnp.zeros_like(loss_ref)

        err = y_ref[...] - t_ref[...]
        dy_ref[...] = err * (1.0 / D)
        per_tok = jnp.sum(err * err, axis=-1, keepdims=True) * (1.0 / D)
        loss_ref[...] += 0.5 * jnp.sum(per_tok, axis=0, keepdims=True)

    row = pl.BlockSpec((tm, D), lambda i: (i, 0))
    return pl.pallas_call(
        body, grid=(T // tm,), in_specs=[row, row],
        out_specs=[pl.BlockSpec((1, 1), lambda i: (0, 0)), row],
        out_shape=[jax.ShapeDtypeStruct((1, 1), F32), jax.ShapeDtypeStruct((T, D), F32)],
        compiler_params=_cp(("arbitrary",)), name=name)(y, target)


GDN_QKV_BLOCKS = 3 * GDN_HEADS
HALO = 8


def _conv_taps(pad_ref, w_ref, tm, base):
    acc = w_ref[0:1, :] * pad_ref[pl.ds(base, tm), :]
    for k in range(1, GDN_CONV):
        acc = acc + w_ref[k:k + 1, :] * pad_ref[pl.ds(base + k, tm), :]
    return acc


GDN_GROUP_W = GDN_HEADS * LANES
GDN_PRE_ROWS = 256


def _head_sums(x):
    rows = x.shape[0]
    parts = [jnp.broadcast_to(jnp.sum(x[:, h * LANES:(h + 1) * LANES], axis=-1, keepdims=True), (rows, LANES))
             for h in range(x.shape[1] // LANES)]
    return jnp.concatenate(parts, axis=1)


def _gdn_pre_common(x_ref, halo_ref, w_ref, pad_ref, tm):
    i = pl.program_id(1)
    grp = pl.program_id(0)
    pad_ref[0:HALO, :] = jnp.where(i == 0, 0.0, halo_ref[...])
    pad_ref[HALO:HALO + tm, :] = x_ref[...]
    c = _conv_taps(pad_ref, w_ref, tm, HALO - (GDN_CONV - 1))
    s = _sigmoid(c)
    y = c * s
    r = lax.rsqrt(_head_sums(y * y) + RMS_EPS)
    scale = jnp.where(grp < 1, GDN_D ** -0.5, 1.0)
    return grp < 2, c, s, y, r, scale


def gdn_pre_fwd(proj, conv_w_p, name):
    T = proj.shape[0]
    tm = _pick(T, GDN_PRE_ROWS, 8)
    GW = GDN_GROUP_W

    def body(x_ref, halo_ref, w_ref, o_ref, pad_ref):
        normed, c, s, y, r, scale = _gdn_pre_common(x_ref, halo_ref, w_ref, pad_ref, tm)
        o_ref[...] = jnp.where(normed, y * r * scale, y)

    return pl.pallas_call(
        body, grid=(3, T // tm),
        in_specs=[pl.BlockSpec((tm, GW), lambda hb, i: (i, hb)),
                  pl.BlockSpec((HALO, GW), lambda hb, i: (jnp.maximum(i * (tm // HALO) - 1, 0), hb)),
                  pl.BlockSpec((GDN_CONV, GW), lambda hb, i: (0, hb))],
        out_specs=pl.BlockSpec((tm, GW), lambda hb, i: (i, hb)),
        out_shape=jax.ShapeDtypeStruct((T, 3 * GW), F32),
        scratch_shapes=[pltpu.VMEM((tm + HALO, GW), F32)],
        compiler_params=_cp(("parallel", "parallel")), name=name)(proj, proj, conv_w_p)


def gdn_pre_bwd_pointwise(proj, conv_w_p, dqkv, name, hosted=None):
    T = proj.shape[0]
    tm = _pick(T, GDN_PRE_ROWS, 8)
    GW = GDN_GROUP_W

    def body(x_ref, halo_ref, w_ref, d_ref, dc_ref, dw_ref, pad_ref):
        i = pl.program_id(1)
        normed, c, s, y, r, scale = _gdn_pre_common(x_ref, halo_ref, w_ref, pad_ref, tm)

        @pl.when(i == 0)
        def _():
            dw_ref[...] = jnp.zeros_like(dw_ref)

        d = d_ref[...]
        n = y * r
        dn = d * scale
        dy = jnp.where(normed, r * (dn - n * _head_sums(dn * n)), d)
        dc = dy * (s * (1.0 + c * (1.0 - s)))
        dc_ref[...] = dc
        for k in range(GDN_CONV):
            xs = pad_ref[pl.ds(HALO - (GDN_CONV - 1) + k, tm), :]
            dw_ref[k:k + 1, :] += jnp.sum(dc * xs, axis=0, keepdims=True)

    blk = pl.BlockSpec((tm, GW), lambda hb, i: (i, hb))
    wblk = pl.BlockSpec((GDN_CONV, GW), lambda hb, i: (0, hb))
    return _hosted_call(
        hosted, body, grid=(3, T // tm),
        in_specs=[blk, pl.BlockSpec((HALO, GW), lambda hb, i: (jnp.maximum(i * (tm // HALO) - 1, 0), hb)), wblk, blk],
        out_specs=[blk, wblk],
        out_shape=[jax.ShapeDtypeStruct((T, 3 * GW), F32), jax.ShapeDtypeStruct((GDN_CONV, 3 * GW), F32)],
        scratch_shapes=[pltpu.VMEM((tm + HALO, GW), F32)],
        compiler_params=_cp(("parallel", "arbitrary")), name=name)(proj, proj, conv_w_p, dqkv)


def gdn_pre_bwd_conv(dc, conv_w_p, name):
    T = dc.shape[0]
    tm = _pick(T, GDN_PRE_ROWS, 8)
    nt = T // tm
    GW = GDN_GROUP_W

    def body(dc_ref, halo_ref, w_ref, dx_ref, pad_ref):
        i = pl.program_id(1)
        pad_ref[0:tm, :] = dc_ref[...]
        pad_ref[tm:tm + HALO, :] = jnp.where(i == nt - 1, 0.0, halo_ref[...])
        acc = w_ref[GDN_CONV - 1:GDN_CONV, :] * pad_ref[pl.ds(0, tm), :]
        for k in range(GDN_CONV - 1):
            acc = acc + w_ref[k:k + 1, :] * pad_ref[pl.ds(GDN_CONV - 1 - k, tm), :]
        dx_ref[...] = acc

    blk = pl.BlockSpec((tm, GW), lambda hb, i: (i, hb))
    return pl.pallas_call(
        body, grid=(3, nt),
        in_specs=[blk, pl.BlockSpec((HALO, GW), lambda hb, i: (jnp.minimum((i + 1) * (tm // HALO), T // HALO - 1), hb)),
                  pl.BlockSpec((GDN_CONV, GW), lambda hb, i: (0, hb))],
        out_specs=blk,
        out_shape=jax.ShapeDtypeStruct((T, 3 * GW), F32),
        scratch_shapes=[pltpu.VMEM((tm + HALO, GW), F32)],
        compiler_params=_cp(("parallel", "parallel")), name=name)(dc, dc, conv_w_p)


def _chunk_masks(C):
    row = _iota2((C, C), 0)
    col = _iota2((C, C), 1)
    return row >= col, row > col, row == col


BNN = (((2,), (1,)), ((0,), (0,)))
BNT = (((2,), (2,)), ((0,), (0,)))
BTN = (((1,), (1,)), ((0,), (0,)))


def _bmm(a, b, dims=BNN):
    return lax.dot_general(a.astype(BF16), b.astype(BF16), dims, preferred_element_type=F32)


def _hbmm(a, b):
    m = a.shape[1]
    a_hi, a_lo = _split2(a)
    b_hi, b_lo = _split2(b)
    r = lax.dot_general(jnp.concatenate([a_hi, a_lo], axis=1), b_hi, BNN, preferred_element_type=F32)
    return r[:, :m] + r[:, m:] + lax.dot_general(a_hi, b_lo, BNN, preferred_element_type=F32)


def _hbmm_tn(a, b):
    a_hi, a_lo = _split2(a)
    b_hi, b_lo = _split2(b)
    d = functools.partial(lax.dot_general, dimension_numbers=BTN, preferred_element_type=F32)
    return d(a_hi, b_hi) + d(a_lo, b_hi) + d(a_hi, b_lo)


def _col_to_row(colv, eye):
    return jnp.sum(jnp.where(eye, colv, 0.0), axis=1, keepdims=True)


def _row_to_col(rowv, eye):
    return jnp.sum(jnp.where(eye, rowv, 0.0), axis=2, keepdims=True)


def _unit_lower_inverse(A, eye):
    C = A.shape[1]
    P = jnp.where(eye, 1.0, 0.0) - A
    Bp = _hbmm(A, A)
    for _ in range(4):
        R = _hbmm(jnp.concatenate([Bp, P], axis=1), Bp)
        Bp = R[:, :C]
        P = P + R[:, C:]
    return P + _hbmm(P, Bp)


def _stack_heads(ref, first_block, n, width=GDN_D):
    return jnp.stack([ref[:, pl.ds((first_block + h) * LANES, width)] for h in range(n)])


def _unstack_heads(ref, first_block, val):
    for h in range(val.shape[0]):
        ref[:, pl.ds((first_block + h) * LANES, val.shape[2])] = val[h]


def _gdn_gates(gab, a_row, dt_row, incl):
    g_all = -jnp.exp(a_row) * _softplus(gab + dt_row)
    beta_all = _sigmoid(gab)
    gc_all = _ones_dot_left(incl.astype(BF16), g_all)
    return g_all, beta_all, gc_all


def _gdn_common(qkv_ref, gc_all, beta_all, incl, strict, eye):
    C, H = GDN_CHUNK, GDN_HEADS
    q, k, v = (_stack_heads(qkv_ref, j * H, H) for j in range(3))
    gc = jnp.stack([gc_all[:, h:h + 1] for h in range(H)])
    beta = jnp.stack([beta_all[:, H + h:H + h + 1] for h in range(H)])
    gc_row = _col_to_row(gc, eye)
    decay = jnp.where(incl, jnp.exp(jnp.where(incl, gc - gc_row, 0.0)), 0.0)
    e_gc = jnp.exp(gc)
    gl = gc[:, C - 1:C, :]
    e_gl = jnp.exp(gl)
    ekd = jnp.exp(gl - gc)
    kb = k * beta
    A = jnp.where(strict, _bmm(kb, k, BNT) * decay, 0.0)
    Pm = jnp.where(incl, _bmm(q, k, BNT) * decay, 0.0)
    return q, k, v, gc, beta, decay, e_gc, e_gl, ekd, kb, A, Pm


def gdn_chunk_fwd(qkv, proj, a_row, dt_row, norm_w, name, hosted=None):
    T = qkv.shape[0]
    C, H, Dh = GDN_CHUNK, GDN_HEADS, GDN_D
    N = T // C

    def body(qkv_ref, gz_ref, gab_ref, a_ref, dt_ref, nw_ref, o_ref, opre_ref, Tm_ref, Sin_ref, S_ref):
        n = pl.program_id(0)

        @pl.when(n == 0)
        def _():
            S_ref[...] = jnp.zeros_like(S_ref)

        incl, strict, eye = _chunk_masks(C)
        _, beta_all, gc_all = _gdn_gates(gab_ref[...], a_ref[...], dt_ref[...], incl)
        o_ref[...] = jnp.zeros_like(o_ref)
        opre_ref[...] = jnp.zeros_like(opre_ref)
        q, k, v, gc, beta, decay, e_gc, e_gl, ekd, kb, A, Pm = _gdn_common(qkv_ref, gc_all, beta_all, incl, strict, eye)
        Tm = _unit_lower_inverse(A, eye)
        u = _hbmm(Tm, v * beta)
        w = _hbmm(Tm, kb * e_gc)
        S = S_ref[...]
        v_new = u - _bmm(w, S)
        o = _bmm(q * e_gc, S) + _bmm(Pm, v_new)
        S_ref[...] = S * e_gl + _bmm(k * ekd, v_new, BTN)
        Sin_ref[0] = S
        Tm_ref[0] = Tm
        r = lax.rsqrt(jnp.mean(o * o, axis=-1, keepdims=True) + RMS_EPS)
        gz = _stack_heads(gz_ref, 0, H)
        _unstack_heads(opre_ref, 0, o)
        _unstack_heads(o_ref, 0, o * r * nw_ref[...] * (gz * _sigmoid(gz)))

    vec = pl.BlockSpec((1, LANES), lambda n: (0, 0))
    hblk = pl.BlockSpec((C, H * LANES), lambda n: (n, 0))
    sblk = pl.BlockSpec((1, H, Dh, Dh), lambda n: (n, 0, 0, 0))
    return _hosted_call(
        hosted, body, grid=(N,),
        in_specs=[pl.BlockSpec((C, GDN_QKV_BLOCKS * LANES), lambda n: (n, 0)),
                  pl.BlockSpec((C, H * LANES), lambda n: (n, CB_GZ // H)),
                  pl.BlockSpec((C, LANES), lambda n: (n, CB_GAB)), vec, vec, pl.BlockSpec((1, Dh), lambda n: (0, 0))],
        out_specs=[hblk, hblk, sblk, sblk],
        out_shape=[jax.ShapeDtypeStruct((T, H * LANES), F32), jax.ShapeDtypeStruct((T, H * LANES), F32),
                   jax.ShapeDtypeStruct((N, H, Dh, Dh), F32), jax.ShapeDtypeStruct((N, H, Dh, Dh), F32)],
        scratch_shapes=[pltpu.VMEM((H, Dh, Dh), F32)],
        compiler_params=_cp(("arbitrary",)), name=name)(qkv, proj, proj, a_row, dt_row, norm_w)


def gdn_chunk_bwd(qkv, proj, a_row, dt_row, norm_w, opre, Tm_all, Sin_all, docat, name, hosted=None):
    T = qkv.shape[0]
    C, H, Dh = GDN_CHUNK, GDN_HEADS, GDN_D
    N = T // C

    def body(qkv_ref, gz_ref, gab_ref, a_ref, dt_ref, nw_ref, opre_ref, Tm_ref, Sin_ref, do_ref,
             dqkv_ref, dgz_ref, dgab_ref, da_ref, ddt_ref, dnw_ref, dS_ref):
        n = pl.program_id(0)

        @pl.when(n == 0)
        def _():
            dS_ref[...] = jnp.zeros_like(dS_ref)
            da_ref[...] = jnp.zeros_like(da_ref)
            ddt_ref[...] = jnp.zeros_like(ddt_ref)
            dnw_ref[...] = jnp.zeros_like(dnw_ref)

        incl, strict, eye = _chunk_masks(C)
        gab = gab_ref[...]
        g_all, beta_all, gc_all = _gdn_gates(gab, a_ref[...], dt_ref[...], incl)
        lane = _iota2((C, LANES), 1)
        rowi = _iota2((C, 1), 0)
        dqkv_ref[...] = jnp.zeros_like(dqkv_ref)
        dgz_ref[...] = jnp.zeros_like(dgz_ref)
        nw = nw_ref[...]
        q, k, v, gc, beta, decay, e_gc, e_gl, ekd, kb, A, Pm = _gdn_common(qkv_ref, gc_all, beta_all, incl, strict, eye)
        Tm = Tm_ref[0]
        S = Sin_ref[0]
        dS = dS_ref[...]
        kbe = kb * e_gc
        u = _hbmm(Tm, v * beta)
        w = _hbmm(Tm, kbe)
        qd = q * e_gc
        kd = k * ekd
        v_new = u - _bmm(w, S)
        o = _stack_heads(opre_ref, 0, H)
        gz = _stack_heads(gz_ref, 0, H)
        don = _stack_heads(do_ref, 0, H)
        r = lax.rsqrt(jnp.mean(o * o, axis=-1, keepdims=True) + RMS_EPS)
        nn = o * r
        sgz = _sigmoid(gz)
        silu = gz * sgz
        _unstack_heads(dgz_ref, 0, don * nn * nw * (sgz * (1.0 + gz * (1.0 - sgz))))
        dnn = don * nw * silu
        dnw_ref[...] += jnp.sum(jnp.sum(don * nn * silu, axis=0), axis=0, keepdims=True)
        do = r * (dnn - nn * jnp.mean(dnn * nn, axis=-1, keepdims=True))
        dv_new = _bmm(Pm, do, BTN) + _bmm(kd, dS)
        dPm = jnp.where(incl, _bmm(do, v_new, BNT), 0.0)
        dqd = _bmm(do, S, BNT)
        dkd = _bmm(v_new, dS, BNT)
        dS_ref[...] = _bmm(qd, do, BTN) + e_gl * dS - _bmm(w, dv_new, BTN)
        dgl = jnp.sum(jnp.sum(dS * S, axis=2, keepdims=True), axis=1, keepdims=True) * e_gl
        dw = -_bmm(dv_new, S, BNT)
        dvb = _hbmm_tn(Tm, dv_new)
        dkbe = _hbmm_tn(Tm, dw)
        dA = -jnp.where(strict, _bmm(dvb, u, BNT) + _bmm(dkbe, w, BNT), 0.0)
        dAD = dA * decay
        dPD = dPm * decay
        Gm = dA * A + dPm * Pm
        dgc = jnp.sum(Gm, axis=2, keepdims=True) - _row_to_col(jnp.sum(Gm, axis=1, keepdims=True), eye)
        dkb = _bmm(dAD, k) + dkbe * e_gc
        dk = _bmm(dAD, kb, BTN) + _bmm(dPD, q, BTN) + dkd * ekd + dkb * beta
        dq = _bmm(dPD, k) + dqd * e_gc
        tkd = jnp.sum(dkd * kd, axis=-1, keepdims=True)
        dgc = dgc + jnp.sum(dqd * qd, axis=-1, keepdims=True) - tkd + jnp.sum(dkbe * kbe, axis=-1, keepdims=True)
        dgl = dgl + jnp.sum(tkd, axis=1, keepdims=True)
        dgc = dgc + jnp.where(rowi == C - 1, dgl, 0.0)
        dbeta = jnp.sum(dvb * v, axis=-1, keepdims=True) + jnp.sum(dkb * k, axis=-1, keepdims=True)
        _unstack_heads(dqkv_ref, 0, dq)
        _unstack_heads(dqkv_ref, H, dk)
        _unstack_heads(dqkv_ref, 2 * H, dvb * beta)
        dgc_all = jnp.zeros((C, LANES), F32)
        dbeta_all = jnp.zeros((C, LANES), F32)
        for h in range(H):
            dgc_all = dgc_all + jnp.where(lane == h, dgc[h], 0.0)
            dbeta_all = dbeta_all + jnp.where(lane == H + h, dbeta[h], 0.0)
        upper = (_iota2((C, C), 0) <= _iota2((C, C), 1)).astype(BF16)
        dg_all = _ones_dot_left(upper, dgc_all)
        dga = dg_all * (-jnp.exp(a_ref[...])) * _sigmoid(gab + dt_ref[...])
        dgb = dbeta_all * beta_all * (1.0 - beta_all)
        dgab_ref[...] = jnp.where(lane < H, dga, jnp.where(lane < 2 * H, dgb, 0.0))
        da_ref[...] += jnp.sum(jnp.where(lane < H, dg_all * g_all, 0.0), axis=0, keepdims=True)
        ddt_ref[...] += jnp.sum(jnp.where(lane < H, dga, 0.0), axis=0, keepdims=True)

    rev = lambda n: N - 1 - n
    vec = pl.BlockSpec((1, LANES), lambda n: (0, 0))
    nwv = pl.BlockSpec((1, Dh), lambda n: (0, 0))
    hblk = pl.BlockSpec((C, H * LANES), lambda n: (rev(n), 0))
    sblk = pl.BlockSpec((1, H, Dh, Dh), lambda n: (rev(n), 0, 0, 0))
    qblk = pl.BlockSpec((C, GDN_QKV_BLOCKS * LANES), lambda n: (rev(n), 0))
    return _hosted_call(
        hosted, body, grid=(N,),
        in_specs=[qblk, pl.BlockSpec((C, H * LANES), lambda n: (rev(n), CB_GZ // H)),
                  pl.BlockSpec((C, LANES), lambda n: (rev(n), CB_GAB)), vec, vec, nwv, hblk, sblk, sblk, hblk],
        out_specs=[qblk, hblk, pl.BlockSpec((C, LANES), lambda n: (rev(n), 0)), vec, vec, nwv],
        out_shape=[jax.ShapeDtypeStruct((T, GDN_QKV_BLOCKS * LANES), F32), jax.ShapeDtypeStruct((T, H * LANES), F32),
                   jax.ShapeDtypeStruct((T, LANES), F32), jax.ShapeDtypeStruct((1, LANES), F32),
                   jax.ShapeDtypeStruct((1, LANES), F32), jax.ShapeDtypeStruct((1, Dh), F32)],
        scratch_shapes=[pltpu.VMEM((H, Dh, Dh), F32)],
        compiler_params=_cp(("arbitrary",)), name=name)(qkv, proj, proj, a_row, dt_row, norm_w, opre, Tm_all, Sin_all, docat)


ATT_BQ, ATT_BK = 256, 512
NEG_BIG = -1e30


def _att_blocks(T):
    bq, bk = min(ATT_BQ, T), min(ATT_BK, T)
    assert bk % bq == 0 and T % bk == 0
    return bq, bk


def _att_specs(T, bq, cbs):
    qspec = lambda cb: pl.BlockSpec((bq, LANES), lambda h, i: (i, cb + h))
    kspec = lambda cb: pl.BlockSpec((T, LANES), lambda h, i: (0, cb + h))
    return qspec, kspec


def _kblock(ref, kb, bk):
    return ref[pl.ds(pl.multiple_of(kb * bk, bk), bk), :]


def _att_pos(i, kb, bq, bk):
    qpos = i * bq + _iota2((bq, bk), 0)
    kpos = kb * bk + _iota2((bq, bk), 1)
    return qpos, kpos


def _suffix_sum(x):
    n = x.shape[1]
    lane = _iota2(x.shape, 1)
    d = 1
    while d < n:
        x = x + jnp.where(lane < n - d, pltpu.roll(x, n - d, 1), 0.0)
        d *= 2
    return x


def _prefix_sum(x):
    n = x.shape[1]
    lane = _iota2(x.shape, 1)
    d = 1
    while d < n:
        x = x + jnp.where(lane >= d, pltpu.roll(x, d, 1), 0.0)
        d *= 2
    return x


SB_BLOCK = 256
SB_DEAD = -104.0


def _sb_blocks(T):
    b = min(SB_BLOCK, T)
    assert T % b == 0 and T // b <= LANES
    return b, b


def sb_fwd(proj, name, hosted=None):
    T = proj.shape[0]
    H = SB_HEADS
    bq, bk = _sb_blocks(T)
    scale = SB_DIM ** -0.5

    def body(q_ref, k_ref, v_ref, o_ref, tot_ref):
        i = pl.program_id(1)
        qb = q_ref[...].astype(BF16)
        diag = (i * bq) // bk
        lane = _iota2((bq, LANES), 1)

        def block(kb, acc, R, masked):
            z = _bdot_nt(qb, _kblock(k_ref, kb, bk)) * scale
            sp = _softplus(z)
            if masked:
                qpos, kpos = _att_pos(i, kb, bq, bk)
                mask = kpos < qpos
                l1m = jnp.where(mask, -sp, 0.0)
            else:
                l1m = -sp
            W = jnp.exp((z - sp) + (_suffix_sum(l1m) - l1m) + R)
            if masked:
                W = jnp.where(mask, W, 0.0)
            acc = acc + _bdot(W, _kblock(v_ref, kb, bk))
            return acc, R + jnp.sum(l1m, axis=-1, keepdims=True)

        acc, R = block(diag, jnp.zeros((bq, LANES), F32), jnp.zeros((bq, 1), F32), True)

        def live(c):
            return jnp.logical_and(c[0] >= 0, jnp.max(c[2]) > SB_DEAD)

        def step(c):
            kb, acc, R, Rb = c
            acc, R_next = block(kb, acc, R, False)
            return kb - 1, acc, R_next, jnp.where(lane == kb, R, Rb)

        _, acc, _, Rb = lax.while_loop(live, step, (diag - 1, acc, R, jnp.where(lane == diag, 0.0, NEG_BIG)))
        o_ref[...] = acc
        tot_ref[...] = Rb

    qspec, kspec = _att_specs(T, bq, None)
    sds = jax.ShapeDtypeStruct((T, H * LANES), F32)
    oblk = pl.BlockSpec((bq, LANES), lambda h, i: (i, h))
    return _hosted_call(
        hosted, body, grid=(H, T // bq), in_specs=[qspec(CB_SQ), kspec(CB_SK), kspec(CB_SV)],
        out_specs=[oblk, oblk], out_shape=[sds, sds],
        compiler_params=_cp(("parallel", "parallel")), name=name)(proj, proj, proj)


def sb_bwd(proj, tot, docat, do_cb, name):
    T = proj.shape[0]
    H = SB_HEADS
    bq, bk = _sb_blocks(T)
    scale = SB_DIM ** -0.5

    def body(q_ref, k_ref, v_ref, tot_ref, do_ref, dq_ref, dk_ref, dv_ref):
        i = pl.program_id(1)

        @pl.when(i == 0)
        def _():
            dk_ref[...] = jnp.zeros_like(dk_ref)
            dv_ref[...] = jnp.zeros_like(dv_ref)

        qb = q_ref[...].astype(BF16)
        dob = do_ref[...].astype(BF16)
        Rb = tot_ref[...]
        diag = (i * bq) // bk
        lane = _iota2((bq, LANES), 1)
        first = lax.while_loop(
            lambda kb: jnp.logical_and(kb < diag, jnp.max(jnp.where(lane == kb, Rb, NEG_BIG)) <= SB_DEAD),
            lambda kb: kb + 1, jnp.int32(0))

        def block(kb, carry, masked):
            dq, Epre = carry
            R = jnp.sum(jnp.where(lane == kb, Rb, 0.0), axis=1, keepdims=True)
            kblk = _kblock(k_ref, kb, bk).astype(BF16)
            z = _bdot_nt(qb, kblk) * scale
            sp = _softplus(z)
            if masked:
                qpos, kpos = _att_pos(i, kb, bq, bk)
                mask = kpos < qpos
                l1m = jnp.where(mask, -sp, 0.0)
            else:
                l1m = -sp
            W = jnp.exp((z - sp) + (_suffix_sum(l1m) - l1m) + R)
            if masked:
                W = jnp.where(mask, W, 0.0)
            E = _bdot_nt(dob, _kblock(v_ref, kb, bk)) * W
            cexcl = (_prefix_sum(E) - E) + Epre
            neg = jnp.exp(-sp)
            dz = E * neg - cexcl * (1.0 - neg)
            if masked:
                dz = jnp.where(mask, dz, 0.0)
            dz = (dz * scale).astype(BF16)
            rows = pl.ds(pl.multiple_of(kb * bk, bk), bk)
            dk_ref[rows, :] += lax.dot_general(dz, qb, TN_DIMS, preferred_element_type=F32)
            dv_ref[rows, :] += lax.dot_general(W.astype(BF16), dob, TN_DIMS, preferred_element_type=F32)
            dq = dq + jnp.dot(dz, kblk, preferred_element_type=F32)
            return dq, Epre + jnp.sum(E, axis=-1, keepdims=True)

        init = (jnp.zeros((bq, LANES), F32), jnp.zeros((bq, 1), F32))
        carry = lax.fori_loop(first, diag, lambda kb, c: block(kb, c, False), init)
        dq, _ = block(diag, carry, True)
        dq_ref[...] = dq

    qspec, kspec = _att_specs(T, bq, None)
    sds = jax.ShapeDtypeStruct((T, H * LANES), F32)
    oblk = pl.BlockSpec((bq, LANES), lambda h, i: (i, h))
    kout = pl.BlockSpec((T, LANES), lambda h, i: (0, h))
    return pl.pallas_call(
        body, grid=(H, T // bq),
        in_specs=[qspec(CB_SQ), kspec(CB_SK), kspec(CB_SV), oblk, qspec(do_cb)],
        out_specs=[oblk, kout, kout], out_shape=[sds, sds, sds],
        compiler_params=_cp(("arbitrary", "arbitrary")), name=name)(proj, proj, proj, tot, docat)


def mla_fwd(Q, K, V, name, hosted=None):
    T = Q.shape[0]
    H = MLA_HEADS
    bq, bk = _att_blocks(T)
    scale = (MLA_NOPE + MLA_ROPE) ** -0.5

    def body(q_ref, k_ref, v_ref, o_ref, lse_ref):
        i = pl.program_id(1)
        qb = q_ref[...]
        diag = (i * bq) // bk

        def block(kb, carry, masked):
            acc, m, l = carry
            s = _bdot_nt(qb, _kblock(k_ref, kb, bk)) * scale
            if masked:
                qpos, kpos = _att_pos(i, kb, bq, bk)
                s = jnp.where(kpos <= qpos, s, NEG_BIG)
            m_new = jnp.maximum(m, jnp.max(s, axis=-1, keepdims=True))
            p = jnp.exp(s - m_new)
            corr = jnp.exp(m - m_new)
            acc = corr * acc + _bdot(p, _kblock(v_ref, kb, bk))
            return acc, m_new, corr * l + jnp.sum(p, axis=-1, keepdims=True)

        init = (jnp.zeros((bq, LANES), F32), jnp.full((bq, 1), NEG_BIG, F32), jnp.zeros((bq, 1), F32))
        carry = lax.fori_loop(0, diag, lambda kb, c: block(kb, c, False), init)
        acc, m, l = block(diag, carry, True)
        o_ref[...] = acc / l
        lse_ref[...] = jnp.broadcast_to(m + jnp.log(l), (bq, LANES))

    qspec, kspec = _att_specs(T, bq, None)
    sds = jax.ShapeDtypeStruct((T, H * LANES), F32)
    oblk = pl.BlockSpec((bq, LANES), lambda h, i: (i, h))
    return _hosted_call(
        hosted, body, grid=(H, T // bq), in_specs=[qspec(0), kspec(0), kspec(0)],
        out_specs=[oblk, oblk], out_shape=[sds, sds],
        compiler_params=_cp(("parallel", "parallel")), name=name)(Q, K, V)


def mla_bwd(Q, K, V, o, lse, docat, do_cb, name, hosted=None):
    T = Q.shape[0]
    H = MLA_HEADS
    bq, bk = _att_blocks(T)
    scale = (MLA_NOPE + MLA_ROPE) ** -0.5

    def body(q_ref, k_ref, v_ref, o_ref, lse_ref, do_ref, dq_ref, dk_ref, dv_ref):
        i = pl.program_id(1)

        @pl.when(i == 0)
        def _():
            dk_ref[...] = jnp.zeros_like(dk_ref)
            dv_ref[...] = jnp.zeros_like(dv_ref)

        qb = q_ref[...]
        do = do_ref[...]
        dob = do.astype(BF16)
        delta = jnp.sum(do * o_ref[...], axis=-1, keepdims=True)
        lse = lse_ref[:, 0:1]

        diag = (i * bq) // bk

        def block(kb, dq, masked):
            kblk = _kblock(k_ref, kb, bk)
            s = _bdot_nt(qb, kblk) * scale
            if masked:
                qpos, kpos = _att_pos(i, kb, bq, bk)
                s = jnp.where(kpos <= qpos, s, NEG_BIG)
            p = jnp.exp(s - lse)
            dp = _bdot_nt(dob, _kblock(v_ref, kb, bk))
            ds = (p * (dp - delta) * scale).astype(BF16)
            rows = pl.ds(pl.multiple_of(kb * bk, bk), bk)
            dk_ref[rows, :] += lax.dot_general(ds, qb, TN_DIMS, preferred_element_type=F32)
            dv_ref[rows, :] += lax.dot_general(p.astype(BF16), dob, TN_DIMS, preferred_element_type=F32)
            return dq + jnp.dot(ds, kblk, preferred_element_type=F32)

        dq = lax.fori_loop(0, diag, lambda kb, c: block(kb, c, False), jnp.zeros((bq, LANES), F32))
        dq_ref[...] = block(diag, dq, True)

    qspec, kspec = _att_specs(T, bq, None)
    sds = jax.ShapeDtypeStruct((T, H * LANES), F32)
    oblk = pl.BlockSpec((bq, LANES), lambda h, i: (i, h))
    kout = pl.BlockSpec((T, LANES), lambda h, i: (0, h))
    return _hosted_call(
        hosted, body, grid=(H, T // bq),
        in_specs=[qspec(0), kspec(0), kspec(0), oblk, oblk, qspec(do_cb)],
        out_specs=[oblk, kout, kout], out_shape=[sds, sds, sds],
        compiler_params=_cp(("arbitrary", "arbitrary")), name=name)(Q, K, V, o, lse, docat)


def _tile_heads(t, n):
    return jnp.concatenate([t] * n, axis=1)


def _rope(X, C, Sn, Sp):
    n = X.shape[1]
    return X * C + pltpu.roll(X, n - HALF_ROPE, 1) * Sn + pltpu.roll(X, HALF_ROPE, 1) * Sp


def _rope_t(dO, C, Sn, Sp):
    n = dO.shape[1]
    return dO * C + pltpu.roll(dO * Sn, HALF_ROPE, 1) + pltpu.roll(dO * Sp, n - HALF_ROPE, 1)


def _rms(x, w):
    r = lax.rsqrt(jnp.mean(x * x, axis=-1, keepdims=True) + RMS_EPS)
    xh = x * r
    return r, xh, xh * w


def _rms_bwd(dn, w, r, xh):
    dxh = dn * w
    return r * (dxh - xh * jnp.mean(dxh * xh, axis=-1, keepdims=True)), jnp.sum(dn * xh, axis=0, keepdims=True)


def _mla_pre_specs(T, tm):
    KV = MLA_KV_RANK
    QR = MLA_Q_RANK
    W = MLA_HEADS * LANES
    full = lambda shape: pl.BlockSpec(shape, lambda i: (0, 0))
    specs = [pl.BlockSpec((tm, QR), lambda i: (i, CB_MQ * LANES // QR)),
             pl.BlockSpec((tm, 2 * LANES), lambda i: (i, CB_MKV // 2)),
             full((1, QR)), full((1, KV))]
    rope = [pl.BlockSpec((tm, LANES), lambda i: (i, 0))] * 3
    return specs, rope, full, W


def mla_pre_fwd(proj, wq, wkv, wuq, wuk, wuv, ropeC, ropeSn, ropeSp, name):
    T = proj.shape[0]
    tm = _pick(T, 512, 16)
    KV = MLA_KV_RANK
    H = MLA_HEADS

    def body(mq_ref, mkv_ref, wq_ref, wkv_ref, wuq_ref, wuk_ref, wuv_ref, c_ref, sn_ref, sp_ref, Q_ref, K_ref, V_ref):
        C, Sn, Sp = (_tile_heads(t[...], H) for t in (c_ref, sn_ref, sp_ref))
        _, _, qn = _rms(mq_ref[...], wq_ref[...])
        Q_ref[...] = _rope(_bdot(qn, wuq_ref[...]), C, Sn, Sp).astype(BF16)
        mkv = mkv_ref[...]
        _, _, kvn = _rms(mkv[:, :KV], wkv_ref[...])
        kr = pltpu.roll(mkv[:, KV:], MLA_NOPE, 1)
        K_ref[...] = _rope(_bdot(kvn, wuk_ref[...]) + _tile_heads(kr, H), C, Sn, Sp).astype(BF16)
        V_ref[...] = _bdot(kvn, wuv_ref[...]).astype(BF16)

    specs, rope, full, W = _mla_pre_specs(T, tm)
    oblk = pl.BlockSpec((tm, W), lambda i: (i, 0))
    sds = jax.ShapeDtypeStruct((T, W), BF16)
    return pl.pallas_call(
        body, grid=(T // tm,),
        in_specs=specs + [full(wuq.shape), full(wuk.shape), full(wuv.shape)] + rope,
        out_specs=[oblk, oblk, oblk], out_shape=[sds, sds, sds],
        compiler_params=_cp(("parallel",)), name=name)(proj, proj, wq, wkv, wuq, wuk, wuv, ropeC, ropeSn, ropeSp)


def mla_pre_bwd(proj, wq, wkv, wuq, wuk, wuv, ropeC, ropeSn, ropeSp, dQ, dK, dV, name):
    T = proj.shape[0]
    tm = _pick(T, 512, 16)
    KV = MLA_KV_RANK
    H = MLA_HEADS

    def body(mq_ref, mkv_ref, wq_ref, wkv_ref, wuq_ref, wuk_ref, wuv_ref,
             c_ref, sn_ref, sp_ref, dQ_ref, dK_ref, dV_ref,
             dmq_ref, dmkv_ref, dwuq_ref, dwuk_ref, dwuv_ref, dwq_ref, dwkv_ref):
        i = pl.program_id(0)

        @pl.when(i == 0)
        def _():
            for ref in (dwuq_ref, dwuk_ref, dwuv_ref, dwq_ref, dwkv_ref):
                ref[...] = jnp.zeros_like(ref)

        C, Sn, Sp = (_tile_heads(t[...], H) for t in (c_ref, sn_ref, sp_ref))
        rq, xq, qn = _rms(mq_ref[...], wq_ref[...])
        mkv = mkv_ref[...]
        rkv, xkv, kvn = _rms(mkv[:, :KV], wkv_ref[...])
        dqf = _rope_t(dQ_ref[...], C, Sn, Sp)
        dkf = _rope_t(dK_ref[...], C, Sn, Sp)
        dv = dV_ref[...]
        dwuq_ref[...] += _bdot_tn(qn, dqf)
        dwuk_ref[...] += _bdot_tn(kvn, dkf)
        dwuv_ref[...] += _bdot_tn(kvn, dv)
        dmq, dwq = _rms_bwd(_bdot_nt(dqf, wuq_ref[...]), wq_ref[...], rq, xq)
        dckv, dwkv = _rms_bwd(_bdot_nt(dkf, wuk_ref[...]) + _bdot_nt(dv, wuv_ref[...]), wkv_ref[...], rkv, xkv)
        dwq_ref[...] += dwq
        dwkv_ref[...] += dwkv
        dmq_ref[...] = dmq
        dkr = dkf[:, 0:LANES]
        for h in range(1, H):
            dkr = dkr + dkf[:, h * LANES:(h + 1) * LANES]
        dkr = pltpu.roll(dkr, LANES - MLA_NOPE, 1)
        dkr = jnp.where(_iota2(dkr.shape, 1) < MLA_ROPE, dkr, 0.0)
        dmkv_ref[...] = jnp.concatenate([dckv, dkr], axis=1)

    specs, rope, full, W = _mla_pre_specs(T, tm)
    wide = pl.BlockSpec((tm, W), lambda i: (i, 0))
    return pl.pallas_call(
        body, grid=(T // tm,),
        in_specs=specs + [full(w.shape) for w in (wuq, wuk, wuv)] + rope + [wide, wide, wide],
        out_specs=[pl.BlockSpec((tm, MLA_Q_RANK), lambda i: (i, 0)), pl.BlockSpec((tm, 2 * LANES), lambda i: (i, 0)),
                   full(wuq.shape), full(wuk.shape), full(wuv.shape), full((1, MLA_Q_RANK)), full((1, KV))],
        out_shape=[jax.ShapeDtypeStruct((T, MLA_Q_RANK), F32), jax.ShapeDtypeStruct((T, 2 * LANES), F32),
                   jax.ShapeDtypeStruct(wuq.shape, F32), jax.ShapeDtypeStruct(wuk.shape, F32),
                   jax.ShapeDtypeStruct(wuv.shape, F32), jax.ShapeDtypeStruct((1, MLA_Q_RANK), F32),
                   jax.ShapeDtypeStruct((1, KV), F32)],
        compiler_params=_cp(("arbitrary",)), name=name)(
            proj, proj, wq, wkv, wuq, wuk, wuv, ropeC, ropeSn, ropeSp, dQ, dK, dV)


def all_gather(shards, name):
    n = len(shards)

    def body(*refs):
        x_refs, out_refs = refs[:n], refs[n:2 * n]
        send_sems, recv_sems, local_sems = refs[2 * n:]
        x, y, c = _place()
        me, sibling = (x, y, c), (x, y, 1 - c)
        chips = [(1 - x, y), (x, 1 - y), (1 - x, 1 - y)]

        def slot(a, px, py, pc):
            return out_refs[a].at[4 * px + 2 * py + pc]

        def copy(a, k, block, to, src=None):
            return pltpu.make_async_remote_copy(
                src_ref=slot(a, *block) if src is None else src, dst_ref=slot(a, *block),
                send_sem=send_sems.at[a, k], recv_sem=recv_sems.at[a, k], device_id=to, device_id_type=MESH)

        mine = [pltpu.make_async_copy(x_refs[a], slot(a, *me), local_sems.at[a]) for a in range(n)]
        first = []
        for a in range(n):
            mine[a].start()
            first.append(copy(a, 0, me, sibling, src=x_refs[a]))
            first += [copy(a, 1 + j, me, (*chip, c), src=x_refs[a]) for j, chip in enumerate(chips)]
        for cp in first:
            cp.start()
        passed = []
        for j, chip in enumerate(chips):
            for a in range(n):
                copy(a, 1 + j, (*chip, c), me).wait_recv()
                passed.append(copy(a, 4 + j, (*chip, c), sibling))
                passed[-1].start()
        for a in range(n):
            copy(a, 0, sibling, me).wait_recv()
            for j, chip in enumerate(chips):
                copy(a, 4 + j, (*chip, 1 - c), me).wait_recv()
        for cp in first + passed:
            cp.wait_send()
        for cp in mine:
            cp.wait()

    return pl.pallas_call(
        body, out_shape=[jax.ShapeDtypeStruct((N_DEV,) + s.shape, s.dtype) for s in shards],
        in_specs=[ANY] * n, out_specs=[ANY] * n,
        scratch_shapes=[pltpu.SemaphoreType.DMA((n, 7)), pltpu.SemaphoreType.DMA((n, 7)), pltpu.SemaphoreType.DMA((n,))],
        name=name)(*shards)


def exchange_partials(parts, name):
    n = len(parts)

    def body(*refs):
        src_refs, dst_refs = refs[:n], refs[n:2 * n]
        send_sems, recv_sems, local_sems = refs[2 * n:]
        x, y, c = _place()
        me = 4 * x + 2 * y + c
        copies = []
        mine = []
        for a in range(n):
            mine.append(pltpu.make_async_copy(src_refs[a].at[me], dst_refs[a].at[me], local_sems.at[a]))
            for k in range(1, N_DEV):
                px = 1 - x if k & 4 else x
                py = 1 - y if k & 2 else y
                pc = 1 - c if k & 1 else c
                copies.append(pltpu.make_async_remote_copy(
                    src_ref=src_refs[a].at[4 * px + 2 * py + pc], dst_ref=dst_refs[a].at[me],
                    send_sem=send_sems.at[a, k - 1], recv_sem=recv_sems.at[a, k - 1],
                    device_id=(px, py, pc), device_id_type=MESH))
        for cp in mine + copies:
            cp.start()
        for cp in copies:
            cp.wait_recv()
        for cp in copies:
            cp.wait_send()
        for cp in mine:
            cp.wait()

    return pl.pallas_call(
        body, out_shape=[jax.ShapeDtypeStruct(p.shape, p.dtype) for p in parts],
        in_specs=[ANY] * n, out_specs=[ANY] * n,
        scratch_shapes=[pltpu.SemaphoreType.DMA((n, 7)), pltpu.SemaphoreType.DMA((n, 7)), pltpu.SemaphoreType.DMA((n,))],
        name=name)(*parts)


def reduce_adamw(parts, w, m, v, name):
    L = len(parts)
    n, Rl, C = parts[0].shape
    R = w.shape[0]
    assert R == L * Rl
    tr = Rl if Rl * C <= 256 * 1024 else _pick(Rl, 256, 16)
    nr = Rl // tr

    def body(*refs):
        p_refs = refs[:L]
        w_ref, m_ref, v_ref, g_ref, d_ref, nm_ref, nv_ref, sum_ref = refs[L:]
        grp = pl.program_id(0)
        for j in range(L):
            @pl.when(grp == j)
            def _(j=j):
                acc = p_refs[j][0].astype(F32)
                for s in range(1, n):
                    acc = acc + p_refs[j][s].astype(F32)
                sum_ref[...] = acc

        g_ = sum_ref[...]
        m_ = ADAM_B1 * m_ref[...] + (1.0 - ADAM_B1) * g_
        v_ = ADAM_B2 * v_ref[...] + (1.0 - ADAM_B2) * (g_ * g_)
        m_hat = m_ / (1.0 - ADAM_B1 ** ADAM_STEP)
        v_hat = v_ / (1.0 - ADAM_B2 ** ADAM_STEP)
        g_ref[...] = g_
        d_ref[...] = -ADAM_LR * (m_hat / (jnp.sqrt(v_hat) + ADAM_EPS) + ADAM_WD * w_ref[...])
        nm_ref[...] = m_
        nv_ref[...] = v_

    blk = pl.BlockSpec((tr, C), lambda l, r: (l * nr + r, 0))
    sds = jax.ShapeDtypeStruct((R, C), F32)
    p_specs = [pl.BlockSpec((n, tr, C), lambda l, r, j=j: (0, jnp.where(l == j, r, 0), 0)) for j in range(L)]
    return pl.pallas_call(
        body, grid=(L, nr), in_specs=p_specs + [blk] * 3,
        out_specs=[blk] * 4, out_shape=[sds] * 4, scratch_shapes=[pltpu.VMEM((tr, C), F32)],
        compiler_params=_cp(("arbitrary", "arbitrary")), name=name)(*parts, w, m, v)


SHARDED = {"ffa_w_in": (2, BF16), "ffa_w_out": (1, BF16), "mix_w_in": (2, BF16), "mla_w_uq": (2, BF16),
           "mla_w_ukv": (2, BF16), "mix_w_o": (1, BF16), "ffb_w_in": (2, BF16), "ffb_w_out": (1, BF16),
           "ple_w_gate": (1, BF16), "ple_w_proj": (2, BF16), "gdn_conv_w": (2, F32), "ln_g": (2, F32), "ln_b": (2, F32)}
FFN_SLOT = ("ffa_w_in", "ffa_w_out", "ffb_w_in", "ffb_w_out")
REPLICATED = ("gdn_a_log", "gdn_dt_bias", "gdn_norm_w", "mla_q_norm_w", "mla_kv_norm_w")
WEIGHTS = ("ffa_w_in", "ffa_w_out", "mix_w_in", "gdn_conv_w", "gdn_a_log", "gdn_dt_bias", "gdn_norm_w", "mla_q_norm_w",
           "mla_kv_norm_w", "mla_w_uq", "mla_w_ukv", "mix_w_o", "ffb_w_in", "ffb_w_out", "ln_g", "ln_b", "ple_w_gate",
           "ple_w_proj")


def _to_slots(full, axis):
    L, a, b = full.shape
    if axis == 2:
        return full.reshape(L, a, N_DEV, b // N_DEV).transpose(2, 0, 1, 3).reshape(N_DEV, L * a, b // N_DEV)
    return full.reshape(L, N_DEV, a // N_DEV, b).transpose(1, 0, 2, 3).reshape(N_DEV, L * a // N_DEV, b)


def _from_slots(slots, shard_shape, axis):
    L, a, b = shard_shape
    t = slots.reshape((N_DEV,) + tuple(shard_shape))
    if axis == 2:
        return t.transpose(1, 2, 0, 3).reshape(L, a, N_DEV * b)
    return t.transpose(1, 0, 2, 3).reshape(L, N_DEV * a, b)


def _view2d(t):
    return t.reshape(-1, t.shape[-1])


def _pad_heads(w, nh):
    K = w.shape[0]
    return jnp.pad(w.reshape(K, nh, GDN_D), ((0, 0), (0, 0), (0, LANES - GDN_D))).reshape(K, nh * LANES)


def _unpad_heads(w, nh):
    K = w.shape[0]
    return w.reshape(K, nh, LANES)[:, :, :GDN_D].reshape(K, nh * GDN_D)


IN_WIDTHS = (512, 512, 512, 512, 8, 8, 256, 256, 256, 256, 160)


def _split_in(w):
    offs = np.cumsum((0,) + IN_WIDTHS)
    return [w[:, int(offs[i]):int(offs[i + 1])] for i in range(len(IN_WIDTHS))]


def _pad_in_proj(w):
    gq, gk, gv, gz, ga, gb, sq, sk, sv, mq, mkv = _split_in(w)
    K = w.shape[0]
    gab = jnp.pad(jnp.concatenate([ga, gb], axis=1), ((0, 0), (0, LANES - 2 * GDN_HEADS)))
    return jnp.concatenate(
        [_pad_heads(t, GDN_HEADS) for t in (gq, gk, gv, gz)] + [_pad_heads(t, SB_HEADS) for t in (sq, sk, sv)]
        + [mq, jnp.pad(mkv, ((0, 0), (0, 2 * LANES - mkv.shape[1]))), gab], axis=1)


def _unpad_in_proj(wp):
    c = lambda cb, n: wp[:, cb * LANES:(cb + n) * LANES]
    gab = c(CB_GAB, 1)
    parts = [_unpad_heads(c(cb, GDN_HEADS), GDN_HEADS) for cb in (CB_GQ, CB_GK, CB_GV, CB_GZ)]
    parts += [gab[:, :GDN_HEADS], gab[:, GDN_HEADS:2 * GDN_HEADS]]
    parts += [_unpad_heads(c(cb, SB_HEADS), SB_HEADS) for cb in (CB_SQ, CB_SK, CB_SV)]
    parts += [c(CB_MQ, 2), c(CB_MKV, 2)[:, :MLA_KV_RANK + MLA_ROPE]]
    return jnp.concatenate(parts, axis=1)


def _pad_lanes(w, width):
    return jnp.pad(w, ((0, 0), (0, width - w.shape[1])))


def _mla_up_pad(w_uq, w_ukv):
    H = MLA_HEADS
    dq = MLA_NOPE + MLA_ROPE
    wuq = jnp.pad(w_uq.reshape(-1, H, dq), ((0, 0), (0, 0), (0, LANES - dq))).reshape(-1, H * LANES)
    kv = w_ukv.reshape(-1, H, MLA_NOPE + MLA_V)
    wuk = jnp.pad(kv[:, :, :MLA_NOPE], ((0, 0), (0, 0), (0, LANES - MLA_NOPE))).reshape(-1, H * LANES)
    wuv = jnp.pad(kv[:, :, MLA_NOPE:], ((0, 0), (0, 0), (0, LANES - MLA_V))).reshape(-1, H * LANES)
    return wuq, wuk, wuv


def _mla_up_unpad(dwuq, dwuk, dwuv):
    H = MLA_HEADS
    dq = MLA_NOPE + MLA_ROPE
    g_uq = dwuq.reshape(-1, H, LANES)[:, :, :dq].reshape(-1, H * dq)
    g_ukv = jnp.concatenate([dwuk.reshape(-1, H, LANES)[:, :, :MLA_NOPE], dwuv.reshape(-1, H, LANES)[:, :, :MLA_V]],
                            axis=2).reshape(-1, H * (MLA_NOPE + MLA_V))
    return g_uq, g_ukv


def _rope_tables(positions):
    inv = 1.0 / (ROPE_BASE ** (jnp.arange(0, MLA_ROPE, 2, dtype=F32) / MLA_ROPE))
    ang = positions.astype(F32)[:, None] * inv
    cos, sin = jnp.cos(ang), jnp.sin(ang)
    T = positions.shape[0]
    one = lambda n: jnp.ones((T, n), F32)
    zero = lambda n: jnp.zeros((T, n), F32)
    tail = LANES - MLA_NOPE - MLA_ROPE
    C = jnp.concatenate([one(MLA_NOPE), cos, cos, one(tail)], axis=1)
    Sn = jnp.concatenate([zero(MLA_NOPE), -sin, zero(HALF_ROPE + tail)], axis=1)
    Sp = jnp.concatenate([zero(MLA_NOPE + HALF_ROPE), sin, zero(tail)], axis=1)
    return C, Sn, Sp


GATHER_FIRST = [("ffa_w_in", 0), ("ffa_w_out", 0)] + [(n, l) for l in range(DEPTH) for n in ("gdn_conv_w", "ln_g", "ln_b")]
GATHER_PLAN = {
    (0, "ffa_fwd"): [("mix_w_in", 0), ("mla_w_uq", 0), ("mla_w_ukv", 0)],
    (0, "gdn_chunk_fwd"): [("mix_w_o", 0), ("ffb_w_in", 0)],
    (0, "sb_fwd"): [("ffb_w_out", 0), ("ple_w_gate", 0), ("ple_w_proj", 0)],
    (0, "mla_fwd"): [("ffa_w_in", 1), ("mix_w_o", 1)],
    (0, "ffb_fwd"): [("ffa_w_out", 1), ("mix_w_in", 1)],
    (1, "ffa_fwd"): [("ffb_w_in", 1)],
    (1, "in_proj"): [("ffb_w_out", 1), ("ple_w_gate", 1), ("ple_w_proj", 1), ("mla_w_uq", 1), ("mla_w_ukv", 1)],
}
SCATTER_PLAN = {
    (1, "gdn_chunk_bwd"): [("ffb_w_in", 1)],
    (1, "gdn_pre_bwd"): [("ffb_w_out", 1), ("ple_w_gate", 1), ("ple_w_proj", 1), ("mix_w_o", 1)],
    (1, "ffa_bwd"): [("mix_w_in", 1), ("mla_w_uq", 1), ("mla_w_ukv", 1), ("gdn_conv_w", 1)],
    (0, "ffb_bwd"): [("ffa_w_in", 1)],
    (0, "gdn_chunk_bwd"): [("ffb_w_in", 0)],
    (0, "gdn_pre_bwd"): [("ffb_w_out", 0), ("ple_w_gate", 0), ("ple_w_proj", 0), ("mix_w_o", 0)],
    (0, "mla_bwd"): [("ffa_w_out", 1), ("ln_g", 1), ("ln_b", 1)],
    (0, "ffa_bwd"): [("mix_w_in", 0), ("mla_w_uq", 0), ("mla_w_ukv", 0), ("gdn_conv_w", 0)],
    (0, "d_ffa_in"): [("ffa_w_out", 0), ("ln_g", 0), ("ln_b", 0)],
}
SCATTER_LAST = [("ffa_w_in", 0)]


class Exchanges:
    def __init__(self, shards):
        self.shards = shards
        self.full = {}
        self.partial = {}
        self.received = {}

    def _block(self, key):
        n, l = key
        return self.shards[n][l].astype(SHARDED[n][1])

    def _absorb_gather(self, keys, results):
        for (n, l), g in zip(keys, results):
            blk = self.shards[n][l]
            self.full[(n, l)] = g if n in FFN_SLOT else _from_slots(g, (1,) + blk.shape, SHARDED[n][0])[0]

    def gather_now(self, keys, name):
        self._absorb_gather(keys, all_gather([self._block(k) for k in keys], name))

    def gather_with(self, layer, tag):
        keys = GATHER_PLAN.get((layer, tag))
        return None if keys is None else (keys, Hosted("gather", [self._block(k) for k in keys]))

    def scatter_with(self, layer, tag):
        keys = SCATTER_PLAN.get((layer, tag))
        return None if keys is None else (keys, Hosted("scatter", [self.partial[k] for k in keys]))

    def done(self, carried):
        if carried is not None:
            keys, hosted = carried
            if hosted.kind == "gather":
                self._absorb_gather(keys, hosted.results)
            else:
                self.received.update(zip(keys, hosted.results))

    def add_grad(self, key, g):
        n, l = key
        self.partial[key] = g if n in FFN_SLOT else _to_slots(g[None], SHARDED[n][0]).astype(SHARDED[n][1])


def _carried(c):
    return None if c is None else c[1]


def _layer_fwd(h0, p_i, rope, i, ex, rep):
    L = "L%d_" % i
    S = {"h0": h0, "p": p_i}
    W = ex.full
    ln_g = [W[("ln_g", i)][j][None, :] for j in range(3)]
    ln_b = [W[("ln_b", i)][j][None, :] for j in range(3)]
    S["ln_g"] = ln_g
    c = ex.gather_with(i, "ffa_fwd")
    S["h1"], S["xh1"], S["rs1"], S["Ga"], S["Ua"] = ffn_fwd(h0, W[("ffa_w_in", i)], W[("ffa_w_out", i)], ln_g[0], ln_b[0],
                                                            L + "ffa_fwd", hosted=_carried(c))
    ex.done(c)
    S["win"] = _pad_in_proj(W[("mix_w_in", i)])
    c = ex.gather_with(i, "in_proj")
    S["proj"] = mm_nn(S["h1"], S["win"], L + "in_proj", hosted=_carried(c))
    ex.done(c)
    S["conv"] = _pad_heads(W[("gdn_conv_w", i)], GDN_QKV_BLOCKS)
    S["a_row"] = _pad_lanes(rep["gdn_a_log"][i][None, :], LANES)
    S["dt_row"] = _pad_lanes(rep["gdn_dt_bias"][i][None, :], LANES)
    S["nw"] = rep["gdn_norm_w"][i][None, :]
    S["wq"] = rep["mla_q_norm_w"][i][None, :]
    S["wkv"] = rep["mla_kv_norm_w"][i][None, :]
    S["qkv"] = gdn_pre_fwd(S["proj"], S["conv"], L + "gdn_pre_fwd")
    c = ex.gather_with(i, "gdn_chunk_fwd")
    S["o_gdn"], S["opre"], S["Tm"], S["Sin"] = gdn_chunk_fwd(S["qkv"], S["proj"], S["a_row"], S["dt_row"], S["nw"],
                                                            L + "gdn_chunk_fwd", hosted=_carried(c))
    ex.done(c)
    c = ex.gather_with(i, "sb_fwd")
    S["o_sb"], S["tot"] = sb_fwd(S["proj"], L + "sb_fwd", hosted=_carried(c))
    ex.done(c)
    S["wuq"], S["wuk"], S["wuv"] = _mla_up_pad(W[("mla_w_uq", i)], W[("mla_w_ukv", i)])
    S["Q"], S["K"], S["V"] = mla_pre_fwd(S["proj"], S["wq"], S["wkv"], S["wuq"], S["wuk"], S["wuv"], *rope, L + "mla_pre_fwd")
    c = ex.gather_with(i, "mla_fwd")
    S["o_mla"], S["lse"] = mla_fwd(S["Q"], S["K"], S["V"], L + "mla_fwd", hosted=_carried(c))
    ex.done(c)
    wo = W[("mix_w_o", i)]
    S["wo"] = jnp.pad(wo.reshape(-1, GDN_D, wo.shape[1]), ((0, 0), (0, LANES - GDN_D), (0, 0))).reshape(-1, wo.shape[1])
    S["h2"], S["xh2"], S["rs2"] = proj_res_ln([S["o_gdn"], S["o_sb"], S["o_mla"]], S["wo"], S["h1"],
                                              ln_g[1], ln_b[1], L + "out_proj")
    c = ex.gather_with(i, "ffb_fwd")
    S["h3"], S["xh3"], S["rs3"], S["Gb"], S["Ub"] = ffn_fwd(S["h2"], W[("ffb_w_in", i)], W[("ffb_w_out", i)], ln_g[2], ln_b[2],
                                                            L + "ffb_fwd", hosted=_carried(c))
    ex.done(c)
    h4, S["a"], S["e"] = ple_fwd(S["h3"], p_i, W[("ple_w_gate", i)], W[("ple_w_proj", i)], L + "ple_fwd")
    return h4, S


def _layer_bwd(dh4, S, rope, i, ex):
    L = "L%d_" % i
    W = ex.full
    Grep = {}
    dh3, da, de = ple_bwd(dh4, S["a"], S["e"], W[("ple_w_gate", i)], L + "ple_bwd")
    ex.add_grad(("ple_w_gate", i), mm_tn(S["h3"], da, L + "d_ple_gate"))
    ex.add_grad(("ple_w_proj", i), mm_tn(S["p"], de, L + "d_ple_proj"))
    dz3, dg2, db2 = ln_bwd(dh3, S["xh3"], S["rs3"], S["ln_g"][2], L + "ln3_bwd")
    c = ex.scatter_with(i, "ffb_bwd")
    dh2, dGb, dUb, actb = ffn_bwd(dz3, S["Gb"], S["Ub"], W[("ffb_w_in", i)], W[("ffb_w_out", i)], L + "ffb_bwd",
                                  hosted=_carried(c))
    ex.done(c)
    ex.add_grad(("ffb_w_in", i), ffn_dw_in(S["h2"].T.astype(BF16), dGb, dUb, L + "d_ffb_in"))
    ex.add_grad(("ffb_w_out", i), ffn_dw_out(actb, dz3, L + "d_ffb_out"))
    dz2, dg1, db1 = ln_bwd(dh2, S["xh2"], S["rs2"], S["ln_g"][1], L + "ln2_bwd")
    docat = mm_nn(dz2, S["wo"], L + "d_ocat", b_transposed=True)
    dwo = jnp.concatenate([mm_tn(S["o_gdn"], dz2, L + "d_wo_gdn"), mm_tn(S["o_sb"], dz2, L + "d_wo_sb"),
                           mm_tn(S["o_mla"], dz2, L + "d_wo_mla")], axis=0)
    ex.add_grad(("mix_w_o", i), dwo.reshape(-1, LANES, dwo.shape[1])[:, :GDN_D, :].reshape(-1, dwo.shape[1]))
    c = ex.scatter_with(i, "gdn_chunk_bwd")
    dqkv, dgz, dgab, d_alog, d_dt, d_nw = gdn_chunk_bwd(S["qkv"], S["proj"], S["a_row"], S["dt_row"], S["nw"],
                                                        S["opre"], S["Tm"], S["Sin"], docat, L + "gdn_chunk_bwd",
                                                        hosted=_carried(c))
    ex.done(c)
    c = ex.scatter_with(i, "gdn_pre_bwd")
    dc, dconv = gdn_pre_bwd_pointwise(S["proj"], S["conv"], dqkv, L + "gdn_pre_bwd", hosted=_carried(c))
    ex.done(c)
    dxqkv = gdn_pre_bwd_conv(dc, S["conv"], L + "gdn_conv_bwd")
    ex.add_grad(("gdn_conv_w", i), _unpad_heads(dconv, GDN_QKV_BLOCKS))
    Grep["gdn_a_log"], Grep["gdn_dt_bias"], Grep["gdn_norm_w"] = d_alog[0, :GDN_HEADS], d_dt[0, :GDN_HEADS], d_nw[0]
    dsq, dsk, dsv = sb_bwd(S["proj"], S["tot"], docat, GDN_HEADS, L + "sb_bwd")
    c = ex.scatter_with(i, "mla_bwd")
    dQ, dK, dV = mla_bwd(S["Q"], S["K"], S["V"], S["o_mla"], S["lse"], docat, GDN_HEADS + SB_HEADS, L + "mla_bwd",
                         hosted=_carried(c))
    ex.done(c)
    dmq, dmkv, dwuq, dwuk, dwuv, dwq, dwkv = mla_pre_bwd(
        S["proj"], S["wq"], S["wkv"], S["wuq"], S["wuk"], S["wuv"], *rope, dQ, dK, dV, L + "mla_pre_bwd")
    g_uq, g_ukv = _mla_up_unpad(dwuq, dwuk, dwuv)
    ex.add_grad(("mla_w_uq", i), g_uq)
    ex.add_grad(("mla_w_ukv", i), g_ukv)
    Grep["mla_q_norm_w"], Grep["mla_kv_norm_w"] = dwq[0], dwkv[0]
    dproj = jnp.concatenate([dxqkv, dgz, dsq, dsk, dsv, dmq, dmkv, dgab], axis=1).astype(BF16)
    ex.add_grad(("mix_w_in", i),
                _unpad_in_proj(mm_tn(S["h1"].T.astype(BF16), dproj, L + "d_in_proj", a_transposed=True)))
    dh1 = mm_nn(dproj, S["win"], L + "d_h1", res=dz2, res_scale=ALPHA, b_transposed=True)
    dz1, dg0, db0 = ln_bwd(dh1, S["xh1"], S["rs1"], S["ln_g"][0], L + "ln1_bwd")
    c = ex.scatter_with(i, "ffa_bwd")
    dh0, dGa, dUa, acta = ffn_bwd(dz1, S["Ga"], S["Ua"], W[("ffa_w_in", i)], W[("ffa_w_out", i)], L + "ffa_bwd",
                                  hosted=_carried(c))
    ex.done(c)
    ex.add_grad(("ffa_w_out", i), ffn_dw_out(acta, dz1, L + "d_ffa_out"))
    ex.add_grad(("ln_g", i), jnp.concatenate([dg0, dg1, dg2], axis=0))
    ex.add_grad(("ln_b", i), jnp.concatenate([db0, db1, db2], axis=0))
    c = ex.scatter_with(i, "d_ffa_in")
    ex.add_grad(("ffa_w_in", i), ffn_dw_in(S["h0"].T.astype(BF16), dGa, dUa, L + "d_ffa_in", hosted=_carried(c)))
    ex.done(c)
    return dh0, Grep


def _local_step(x, p, positions, target, ex, rep):
    assert DEPTH == 2
    rope = _rope_tables(positions)
    h, saved = x, []
    for i in range(DEPTH):
        h, S = _layer_fwd(h, p[i], rope, i, ex, rep)
        saved.append(S)
    loss, dh = loss_head(h, target, "loss_head")
    grads = [None] * DEPTH
    for i in reversed(range(DEPTH)):
        dh, grads[i] = _layer_bwd(dh, saved[i], rope, i, ex)
    return loss, dh, {n: jnp.stack([grads[i][n] for i in range(DEPTH)]) for n in REPLICATED}


def kernel(x, p, positions, ffa_w_in, ffa_w_out, mix_w_in, gdn_conv_w, gdn_a_log, gdn_dt_bias, gdn_norm_w, mla_q_norm_w, mla_kv_norm_w, mla_w_uq, mla_w_ukv, mix_w_o, ffb_w_in, ffb_w_out, ln_g, ln_b, ple_w_gate, ple_w_proj, loss_target, m_ffa_w_in, m_ffa_w_out, m_mix_w_in, m_gdn_conv_w, m_gdn_a_log, m_gdn_dt_bias, m_gdn_norm_w, m_mla_q_norm_w, m_mla_kv_norm_w, m_mla_w_uq, m_mla_w_ukv, m_mix_w_o, m_ffb_w_in, m_ffb_w_out, m_ln_g, m_ln_b, m_ple_w_gate, m_ple_w_proj, v_ffa_w_in, v_ffa_w_out, v_mix_w_in, v_gdn_conv_w, v_gdn_a_log, v_gdn_dt_bias, v_gdn_norm_w, v_mla_q_norm_w, v_mla_kv_norm_w, v_mla_w_uq, v_mla_w_ukv, v_mix_w_o, v_ffb_w_in, v_ffb_w_out, v_ln_g, v_ln_b, v_ple_w_gate, v_ple_w_proj):
    given = dict(locals())
    shards = {n: given[n] for n in WEIGHTS}
    ex = Exchanges({n: shards[n] for n in SHARDED})
    ex.gather_now(GATHER_FIRST, "gather_first")
    loss, grad_x, Grep = _local_step(x[0], p[:, 0], positions[0], loss_target[0], ex, {n: shards[n] for n in REPLICATED})
    loss = lax.psum(loss[0, 0], ("x", "y", "c"))
    ex.received.update(zip(SCATTER_LAST, exchange_partials([ex.partial[k] for k in SCATTER_LAST], "scatter_last")))
    rep_received = dict(zip(REPLICATED, all_gather([Grep[n] for n in REPLICATED], "gather_replicated_grads")))
    grad, delta, new_m, new_v = {}, {}, {}, {}
    for n in WEIGHTS:
        shape = shards[n].shape
        parts = [rep_received[n]] if n in REPLICATED else [ex.received[(n, l)] for l in range(DEPTH)]
        if parts[0].shape[1] % 8:
            parts = [jnp.concatenate(parts, axis=1)]
        outs = reduce_adamw(parts, _view2d(shards[n]), _view2d(given["m_" + n]), _view2d(given["v_" + n]),
                            "adamw_" + n)
        grad[n], delta[n], new_m[n], new_v[n] = (t.reshape(shape) for t in outs)
    return (loss, grad_x[None], *[grad[n] for n in WEIGHTS], *[delta[n] for n in WEIGHTS],
            *[new_m[n] for n in WEIGHTS], *[new_v[n] for n in WEIGHTS])
```

```python
import functools
import numpy as np
import jax
import jax.numpy as jnp
from jax import lax
from jax.experimental import pallas as pl
from jax.experimental.pallas import tpu as pltpu

F32 = jnp.float32
BF16 = jnp.bfloat16

DEPTH = 2
LN_EPS = 1e-5
RMS_EPS = 1e-6
ALPHA = (2 * DEPTH) ** 0.25
GDN_HEADS, GDN_D, GDN_CONV, GDN_CHUNK = 8, 64, 4, 64
SB_HEADS, SB_DIM = 4, 64
MLA_HEADS, MLA_NOPE, MLA_ROPE, MLA_V, MLA_Q_RANK, MLA_KV_RANK = 4, 64, 32, 64, 256, 128
ROPE_BASE = 10000.0
HALF_ROPE = MLA_ROPE // 2
LANES = 128
N_DEV = 8
ADAM_LR, ADAM_B1, ADAM_B2, ADAM_EPS, ADAM_WD, ADAM_STEP = 0.001, 0.9, 0.999, 1e-08, 0.01, 10

CB_GQ, CB_GK, CB_GV, CB_GZ = 0, 4, 8, 12
CB_SQ, CB_SK, CB_SV = 16, 20, 24
CB_MQ, CB_MKV, CB_GAB = 28, 30, 32
PROJ_W = 33 * LANES
GDN_W = GDN_HEADS * GDN_D
DO_SB = GDN_W // LANES
DO_MLA = DO_SB + SB_HEADS
VMEM_LIMIT = 56 * 1024 * 1024

NT_DIMS = (((1,), (1,)), ((), ()))
TN_DIMS = (((0,), (0,)), ((), ()))


def _cp(sem):
    return pltpu.CompilerParams(dimension_semantics=sem, vmem_limit_bytes=VMEM_LIMIT)


def _bdot(a, b):
    return jnp.dot(a.astype(BF16), b.astype(BF16), preferred_element_type=F32)


def _bdot_nt(a, b):
    return lax.dot_general(a.astype(BF16), b.astype(BF16), NT_DIMS, preferred_element_type=F32)


def _bdot_tn(a, b):
    return lax.dot_general(a.astype(BF16), b.astype(BF16), TN_DIMS, preferred_element_type=F32)


def _split2(a):
    hi = a.astype(BF16)
    lo = (a - hi.astype(F32)).astype(BF16)
    return hi, lo


def _ones_dot_left(ones_bf16, x):
    hi = x.astype(BF16)
    r1 = x - hi.astype(F32)
    mid = r1.astype(BF16)
    lo = (r1 - mid.astype(F32)).astype(BF16)
    d = functools.partial(jnp.dot, preferred_element_type=F32)
    return d(ones_bf16, hi) + d(ones_bf16, mid) + d(ones_bf16, lo)


def _iota2(shape, dim):
    return lax.broadcasted_iota(jnp.int32, shape, dim)


def _sigmoid(x):
    return 0.5 * jnp.tanh(0.5 * x) + 0.5


def _softplus(x):
    return jnp.maximum(x, 0.0) + jnp.log(1.0 + jnp.exp(-jnp.abs(x)))


def _pick(n, limit, mult):
    if n <= limit:
        return n
    best = None
    for t in range(mult, limit + 1, mult):
        if n % t == 0:
            best = t
    assert best is not None, (n, limit, mult)
    return best


MESH = pl.DeviceIdType.MESH
ANY = pl.BlockSpec(memory_space=pl.ANY)


def _place():
    return lax.axis_index("x"), lax.axis_index("y"), lax.axis_index("c")


def _peer(k):
    x, y, c = _place()
    return (1 - x if k & 4 else x, 1 - y if k & 2 else y, 1 - c if k & 1 else c)


class Hosted:
    def __init__(self, kind, arrays):
        self.kind, self.arrays, self.n, self.results = kind, list(arrays), len(arrays), None

    def out_shapes(self):
        if self.kind == "gather":
            return [jax.ShapeDtypeStruct((N_DEV,) + a.shape, a.dtype) for a in self.arrays]
        return [jax.ShapeDtypeStruct(a.shape, a.dtype) for a in self.arrays]

    def sems(self):
        return [pltpu.SemaphoreType.DMA((self.n, N_DEV - 1)), pltpu.SemaphoreType.DMA((self.n, N_DEV - 1)),
                pltpu.SemaphoreType.DMA((self.n,))]

    def _copies(self, src_refs, dst_refs, send_sems, recv_sems, local_sems):
        x, y, c = _place()
        me = 4 * x + 2 * y + c
        local, remote = [], []
        for a in range(self.n):
            gather = self.kind == "gather"
            local.append(pltpu.make_async_copy(src_refs[a] if gather else src_refs[a].at[me], dst_refs[a].at[me],
                                               local_sems.at[a]))
            for k in range(1, N_DEV):
                px, py, pc = _peer(k)
                remote.append(pltpu.make_async_remote_copy(
                    src_ref=src_refs[a] if gather else src_refs[a].at[4 * px + 2 * py + pc], dst_ref=dst_refs[a].at[me],
                    send_sem=send_sems.at[a, k - 1], recv_sem=recv_sems.at[a, k - 1],
                    device_id=(px, py, pc), device_id_type=MESH))
        return local, remote

    def start(self, *refs):
        local, remote = self._copies(*refs)
        for cp in local + remote:
            cp.start()

    def wait(self, *refs):
        local, remote = self._copies(*refs)
        for cp in remote:
            cp.wait_recv()
        for cp in remote:
            cp.wait_send()
        for cp in local:
            cp.wait()


def _hosted_call(hosted, body, *, grid, in_specs, out_specs, out_shape, scratch_shapes=(), compiler_params, name):
    if hosted is None:
        return pl.pallas_call(body, grid=grid, in_specs=in_specs, out_specs=out_specs, out_shape=out_shape,
                              scratch_shapes=scratch_shapes, compiler_params=compiler_params, name=name)
    single = not isinstance(out_shape, (list, tuple))
    o_specs = [out_specs] if single else list(out_specs)
    o_shape = [out_shape] if single else list(out_shape)
    n_in, n_out, n_scr, n = len(in_specs), len(o_specs), len(scratch_shapes), hosted.n

    def wrapped(*refs):
        ins, c_in = refs[:n_in], refs[n_in:n_in + n]
        outs, c_out = refs[n_in + n:n_in + n + n_out], refs[n_in + n + n_out:n_in + 2 * n + n_out]
        rest = refs[n_in + 2 * n + n_out:]
        scr, sems = rest[:n_scr], rest[n_scr:]
        ids = [pl.program_id(ax) for ax in range(len(grid))]
        first = functools.reduce(jnp.logical_and, [i == 0 for i in ids])
        last = functools.reduce(jnp.logical_and, [i == g - 1 for i, g in zip(ids, grid)])

        @pl.when(first)
        def _():
            hosted.start(c_in, c_out, *sems)

        body(*ins, *outs, *scr)

        @pl.when(last)
        def _():
            hosted.wait(c_in, c_out, *sems)

    call = pl.pallas_call(
        wrapped, grid=grid, in_specs=list(in_specs) + [ANY] * n, out_specs=o_specs + [ANY] * n,
        out_shape=o_shape + hosted.out_shapes(), scratch_shapes=list(scratch_shapes) + hosted.sems(),
        compiler_params=_cp(("arbitrary",) * len(grid)), name=name)

    def run(*args):
        outs = call(*args, *hosted.arrays)
        hosted.results = list(outs[n_out:])
        return outs[0] if single else list(outs[:n_out])

    return run


def mm_nn(a, b, name, out_dtype=F32, res=None, res_scale=1.0, b_transposed=False, hosted=None):
    M, K = a.shape
    N = b.shape[0] if b_transposed else b.shape[1]
    tm, tn, tk = _pick(M, 512, 16), _pick(N, 1024, LANES), _pick(K, 1024, LANES)
    nk = K // tk
    has_res = res is not None
    dot = _bdot_nt if b_transposed else _bdot

    def body(*refs):
        if has_res:
            a_ref, b_ref, r_ref, o_ref, acc_ref = refs
        else:
            a_ref, b_ref, o_ref, acc_ref = refs
        k = pl.program_id(2)

        @pl.when(k == 0)
        def _():
            acc_ref[...] = jnp.zeros_like(acc_ref)

        acc_ref[...] += dot(a_ref[...], b_ref[...])

        @pl.when(k == nk - 1)
        def _():
            out = acc_ref[...]
            if has_res:
                out = out + res_scale * r_ref[...]
            o_ref[...] = out.astype(o_ref.dtype)

    b_spec = pl.BlockSpec((tn, tk), lambda i, j, k: (j, k)) if b_transposed else pl.BlockSpec((tk, tn), lambda i, j, k: (k, j))
    in_specs = [pl.BlockSpec((tm, tk), lambda i, j, k: (i, k)), b_spec]
    args = [a, b]
    if has_res:
        in_specs.append(pl.BlockSpec((tm, tn), lambda i, j, k: (i, j)))
        args.append(res)
    return _hosted_call(
        hosted, body, grid=(M // tm, N // tn, nk), in_specs=in_specs,
        out_specs=pl.BlockSpec((tm, tn), lambda i, j, k: (i, j)),
        out_shape=jax.ShapeDtypeStruct((M, N), out_dtype),
        scratch_shapes=[pltpu.VMEM((tm, tn), F32)],
        compiler_params=_cp(("parallel", "parallel", "arbitrary")), name=name)(*args)


def mm_tn(a, b, name, out_dtype=F32, a_transposed=False):
    K, T = a.shape if a_transposed else a.shape[::-1]
    _, N = b.shape
    tk = K if a_transposed else _pick(K, 512, LANES)
    tn, tt = _pick(N, 1024, LANES), _pick(T, 512, LANES)
    nt = T // tt

    def body(a_ref, b_ref, o_ref, acc_ref):
        t = pl.program_id(2)

        @pl.when(t == 0)
        def _():
            acc_ref[...] = jnp.zeros_like(acc_ref)

        if a_transposed:
            acc_ref[...] += _bdot(a_ref[:, pl.ds(pl.multiple_of(t * tt, tt), tt)], b_ref[...])
        else:
            acc_ref[...] += _bdot_tn(a_ref[...], b_ref[...])

        @pl.when(t == nt - 1)
        def _():
            o_ref[...] = acc_ref[...].astype(o_ref.dtype)

    a_spec = pl.BlockSpec((K, T), lambda i, j, t: (0, 0)) if a_transposed else pl.BlockSpec((tt, tk), lambda i, j, t: (t, i))
    return pl.pallas_call(
        body, grid=(K // tk, N // tn, nt),
        in_specs=[a_spec, pl.BlockSpec((tt, tn), lambda i, j, t: (t, j))],
        out_specs=pl.BlockSpec((tk, tn), lambda i, j, t: (i, j)),
        out_shape=jax.ShapeDtypeStruct((K, N), out_dtype),
        scratch_shapes=[pltpu.VMEM((tk, tn), F32)],
        compiler_params=_cp(("parallel", "parallel", "arbitrary")), name=name)(a, b)


def _ln_apply(z, g, b):
    mu = jnp.mean(z, axis=-1, keepdims=True)
    zc = z - mu
    var = jnp.mean(zc * zc, axis=-1, keepdims=True)
    rstd = lax.rsqrt(var + LN_EPS)
    xhat = zc * rstd
    return xhat * g + b, xhat, rstd


def ln_bwd(dout, xhat, rstd, g, name):
    T, D = dout.shape
    tm = _pick(T, 512, 8)

    def body(do_ref, xh_ref, rs_ref, g_ref, dz_ref, dg_ref, db_ref):
        i = pl.program_id(0)

        @pl.when(i == 0)
        def _():
            dg_ref[...] = jnp.zeros_like(dg_ref)
            db_ref[...] = jnp.zeros_like(db_ref)

        do = do_ref[...]
        xh = xh_ref[...]
        dxh = do * g_ref[...]
        m1 = jnp.mean(dxh, axis=-1, keepdims=True)
        m2 = jnp.mean(dxh * xh, axis=-1, keepdims=True)
        dz_ref[...] = rs_ref[...] * (dxh - m1 - xh * m2)
        dg_ref[...] += jnp.sum(do * xh, axis=0, keepdims=True)
        db_ref[...] += jnp.sum(do, axis=0, keepdims=True)

    row = pl.BlockSpec((tm, D), lambda i: (i, 0))
    vec = pl.BlockSpec((1, D), lambda i: (0, 0))
    return pl.pallas_call(
        body, grid=(T // tm,),
        in_specs=[row, row, pl.BlockSpec((tm, 1), lambda i: (i, 0)), vec],
        out_specs=[row, vec, vec],
        out_shape=[jax.ShapeDtypeStruct((T, D), F32), jax.ShapeDtypeStruct((1, D), F32), jax.ShapeDtypeStruct((1, D), F32)],
        compiler_params=_cp(("arbitrary",)), name=name)(dout, xhat, rstd, g)


FFN_CHUNKS = N_DEV // 2


def ffn_fwd(h, w_in, w_out, g, b, name, hosted=None):
    T, D = h.shape
    fc = w_in.shape[2]
    half = w_out.shape[1]
    tm = _pick(T, 512, 8)
    nc = FFN_CHUNKS

    def body(h_ref, wg_ref, wu_ref, wo_ref, g_ref, b_ref, out_ref, xh_ref, rs_ref, G_ref, U_ref, acc_ref):
        c = pl.program_id(1)

        @pl.when(c == 0)
        def _():
            acc_ref[...] = jnp.zeros_like(acc_ref)

        hb = h_ref[...].astype(BF16)
        G = jnp.dot(hb, wg_ref[0], preferred_element_type=F32)
        U = jnp.dot(hb, wu_ref[0], preferred_element_type=F32)
        G_ref[0] = G
        U_ref[0] = U
        act = G * _sigmoid(G) * U
        acc_ref[...] += _bdot(act, wo_ref[...].reshape(2 * half, D))

        @pl.when(c == nc - 1)
        def _():
            z = ALPHA * h_ref[...] + 0.5 * acc_ref[...]
            out, xh, rs = _ln_apply(z, g_ref[...], b_ref[...])
            out_ref[...] = out
            xh_ref[...] = xh
            rs_ref[...] = rs

    row = pl.BlockSpec((tm, D), lambda i, c: (i, 0))
    vec = pl.BlockSpec((1, D), lambda i, c: (0, 0))
    cblk = pl.BlockSpec((1, tm, fc), lambda i, c: (c, i, 0))
    csds = jax.ShapeDtypeStruct((nc, T, fc), F32)
    return _hosted_call(
        hosted, body, grid=(T // tm, nc),
        in_specs=[row, pl.BlockSpec((1, D, fc), lambda i, c: (c, 0, 0)),
                  pl.BlockSpec((1, D, fc), lambda i, c: (c + nc, 0, 0)),
                  pl.BlockSpec((2, half, D), lambda i, c: (c, 0, 0)), vec, vec],
        out_specs=[row, row, pl.BlockSpec((tm, 1), lambda i, c: (i, 0)), cblk, cblk],
        out_shape=[jax.ShapeDtypeStruct((T, D), F32), jax.ShapeDtypeStruct((T, D), F32), jax.ShapeDtypeStruct((T, 1), F32),
                   csds, csds],
        scratch_shapes=[pltpu.VMEM((tm, D), F32)],
        compiler_params=_cp(("parallel", "arbitrary")), name=name)(h, w_in, w_in, w_out, g, b)


def ffn_bwd(dz, G, U, w_in, w_out, name, hosted=None):
    T, D = dz.shape
    nc, _, fc = G.shape
    half = w_out.shape[1]
    tm = _pick(T, 512, 16)

    def body(dz_ref, G_ref, U_ref, wg_ref, wu_ref, wo_ref, dh_ref, dG_ref, dU_ref, act_ref, acc_ref):
        c = pl.program_id(1)

        @pl.when(c == 0)
        def _():
            acc_ref[...] = jnp.zeros_like(acc_ref)

        dy = (0.5 * dz_ref[...]).astype(BF16)
        dact = _bdot_nt(dy, wo_ref[...].reshape(2 * half, D))
        G = G_ref[0]
        U = U_ref[0]
        s = _sigmoid(G)
        silu = G * s
        dG = (dact * U * (s * (1.0 + G * (1.0 - s)))).astype(BF16)
        dU = (dact * silu).astype(BF16)
        dG_ref[0] = dG
        dU_ref[0] = dU
        act_ref[0] = (silu * U).astype(BF16)
        acc_ref[...] += _bdot_nt(dG, wg_ref[0]) + _bdot_nt(dU, wu_ref[0])

        @pl.when(c == nc - 1)
        def _():
            dh_ref[...] = ALPHA * dz_ref[...] + acc_ref[...]

    row = pl.BlockSpec((tm, D), lambda i, c: (i, 0))
    cblk = pl.BlockSpec((1, tm, fc), lambda i, c: (c, i, 0))
    csds = jax.ShapeDtypeStruct((nc, T, fc), BF16)
    return _hosted_call(
        hosted, body, grid=(T // tm, nc),
        in_specs=[row, cblk, cblk, pl.BlockSpec((1, D, fc), lambda i, c: (c, 0, 0)),
                  pl.BlockSpec((1, D, fc), lambda i, c: (c + nc, 0, 0)),
                  pl.BlockSpec((2, half, D), lambda i, c: (c, 0, 0))],
        out_specs=[row, cblk, cblk, cblk],
        out_shape=[jax.ShapeDtypeStruct((T, D), F32), csds, csds, csds],
        scratch_shapes=[pltpu.VMEM((tm, D), F32)],
        compiler_params=_cp(("parallel", "arbitrary")), name=name)(dz, G, U, w_in, w_in, w_out)


def ffn_dw_in(h_t, dG, dU, name, hosted=None):
    D, T = h_t.shape
    nc, _, fc = dG.shape
    tt = _pick(T, 512, LANES)
    nt = T // tt

    def body(h_ref, dG_ref, dU_ref, o_ref, acc_ref):
        s = pl.program_id(0)
        t = pl.program_id(1)

        @pl.when(t == 0)
        def _():
            acc_ref[...] = jnp.zeros_like(acc_ref)

        hb = h_ref[:, pl.ds(pl.multiple_of(t * tt, tt), tt)]

        @pl.when(s < nc)
        def _():
            acc_ref[...] += jnp.dot(hb, dG_ref[0], preferred_element_type=F32)

        @pl.when(s >= nc)
        def _():
            acc_ref[...] += jnp.dot(hb, dU_ref[0], preferred_element_type=F32)

        @pl.when(t == nt - 1)
        def _():
            o_ref[0] = acc_ref[...].astype(o_ref.dtype)

    return _hosted_call(
        hosted, body, grid=(2 * nc, nt),
        in_specs=[pl.BlockSpec((D, T), lambda s, t: (0, 0)),
                  pl.BlockSpec((1, tt, fc), lambda s, t: (jnp.minimum(s, nc - 1), jnp.where(s < nc, t, nt - 1), 0)),
                  pl.BlockSpec((1, tt, fc), lambda s, t: (jnp.maximum(s - nc, 0), jnp.where(s >= nc, t, 0), 0))],
        out_specs=pl.BlockSpec((1, D, fc), lambda s, t: (s, 0, 0)),
        out_shape=jax.ShapeDtypeStruct((2 * nc, D, fc), BF16),
        scratch_shapes=[pltpu.VMEM((D, fc), F32)],
        compiler_params=_cp(("parallel", "arbitrary")), name=name)(h_t, dG, dU)


def ffn_dw_out(act, dz, name, hosted=None):
    nc, T, fc = act.shape
    D = dz.shape[1]
    half = fc // 2
    tt = _pick(T, 512, 16)
    nt = T // tt

    def body(a_ref, dz_ref, o_ref, acc_ref):
        t = pl.program_id(1)

        @pl.when(t == 0)
        def _():
            acc_ref[...] = jnp.zeros_like(acc_ref)

        acc_ref[...] += _bdot_tn(a_ref[0], dz_ref[...])

        @pl.when(t == nt - 1)
        def _():
            o_ref[...] = (0.5 * acc_ref[...]).reshape(2, half, D).astype(o_ref.dtype)

    return _hosted_call(
        hosted, body, grid=(nc, nt),
        in_specs=[pl.BlockSpec((1, tt, fc), lambda c, t: (c, t, 0)), pl.BlockSpec((tt, D), lambda c, t: (t, 0))],
        out_specs=pl.BlockSpec((2, half, D), lambda c, t: (c, 0, 0)),
        out_shape=jax.ShapeDtypeStruct((2 * nc, half, D), BF16),
        scratch_shapes=[pltpu.VMEM((fc, D), F32)],
        compiler_params=_cp(("parallel", "arbitrary")), name=name)(act, dz)


def proj_res_ln(parts, w, res, g, b, name):
    T, D = res.shape
    tm = _pick(T, 512, 8)
    widths = [p.shape[1] for p in parts]
    offs = [int(sum(widths[:i])) for i in range(len(parts))]
    n = len(parts)

    def body(*refs):
        p_refs = refs[:n]
        w_ref, r_ref, g_ref, b_ref, out_ref, xh_ref, rs_ref = refs[n:]
        acc = ALPHA * r_ref[...]
        for p_ref, o, wd in zip(p_refs, offs, widths):
            acc = acc + _bdot(p_ref[...], w_ref[o:o + wd, :])
        out, xh, rs = _ln_apply(acc, g_ref[...], b_ref[...])
        out_ref[...] = out
        xh_ref[...] = xh
        rs_ref[...] = rs

    row = pl.BlockSpec((tm, D), lambda i: (i, 0))
    vec = pl.BlockSpec((1, D), lambda i: (0, 0))
    return pl.pallas_call(
        body, grid=(T // tm,),
        in_specs=[pl.BlockSpec((tm, wd), lambda i: (i, 0)) for wd in widths]
        + [pl.BlockSpec(w.shape, lambda i: (0, 0)), row, vec, vec],
        out_specs=[row, row, pl.BlockSpec((tm, 1), lambda i: (i, 0))],
        out_shape=[jax.ShapeDtypeStruct((T, D), F32), jax.ShapeDtypeStruct((T, D), F32), jax.ShapeDtypeStruct((T, 1), F32)],
        compiler_params=_cp(("parallel",)), name=name)(*parts, w, res, g, b)


def ple_fwd(h, p, wg, wp, name):
    T, D = h.shape
    P = p.shape[1]
    tm, tn = _pick(T, 512, 8), _pick(D, 512, LANES)

    def body(h_ref, hn_ref, p_ref, wg_ref, wp_ref, out_ref, a_ref, e_ref):
        a = _bdot(h_ref[...], wg_ref[...])
        e = _bdot(p_ref[...], wp_ref[...])
        a_ref[...] = a
        e_ref[...] = e
        out_ref[...] = hn_ref[...] + _sigmoid(a) * e

    blk = pl.BlockSpec((tm, tn), lambda i, j: (i, j))
    sds = jax.ShapeDtypeStruct((T, D), F32)
    return pl.pallas_call(
        body, grid=(T // tm, D // tn),
        in_specs=[pl.BlockSpec((tm, D), lambda i, j: (i, 0)), blk, pl.BlockSpec((tm, P), lambda i, j: (i, 0)),
                  pl.BlockSpec((D, tn), lambda i, j: (0, j)), pl.BlockSpec((P, tn), lambda i, j: (0, j))],
        out_specs=[blk, blk, blk], out_shape=[sds, sds, sds],
        compiler_params=_cp(("parallel", "parallel")), name=name)(h, h, p, wg, wp)


def ple_bwd(dout, a, e, wg, name):
    T, D = dout.shape
    tm = _pick(T, 512, 16)

    def body(do_ref, a_ref, e_ref, wg_ref, dh_ref, da_ref, de_ref):
        do = do_ref[...]
        s = _sigmoid(a_ref[...])
        da = (do * e_ref[...] * s * (1.0 - s)).astype(BF16)
        da_ref[...] = da
        de_ref[...] = (do * s).astype(BF16)
        dh_ref[...] = do + _bdot_nt(da, wg_ref[...])

    row = pl.BlockSpec((tm, D), lambda i: (i, 0))
    return pl.pallas_call(
        body, grid=(T // tm,),
        in_specs=[row, row, row, pl.BlockSpec((D, D), lambda i: (0, 0))],
        out_specs=[row, row, row],
        out_shape=[jax.ShapeDtypeStruct((T, D), F32), jax.ShapeDtypeStruct((T, D), BF16), jax.ShapeDtypeStruct((T, D), BF16)],
        compiler_params=_cp(("parallel",)), name=name)(dout, a, e, wg)


def loss_head(y, target, name):
    T, D = y.shape
    tm = _pick(T, 512, 8)

    def body(y_ref, t_ref, loss_ref, dy_ref):
        i = pl.program_id(0)

        @pl.when(i == 0)
        def _():
            loss_ref[...] = jnp.zeros_like(loss_ref)

        err = y_ref[...] - t_ref[...]
        dy_ref[...] = err * (1.0 / D)
        per_tok = jnp.sum(err * err, axis=-1, keepdims=True) * (1.0 / D)
        loss_ref[...] += 0.5 * jnp.sum(per_tok, axis=0, keepdims=True)

    row = pl.BlockSpec((tm, D), lambda i: (i, 0))
    return pl.pallas_call(
        body, grid=(T // tm,), in_specs=[row, row],
        out_specs=[pl.BlockSpec((1, 1), lambda i: (0, 0)), row],
        out_shape=[jax.ShapeDtypeStruct((1, 1), F32), jax.ShapeDtypeStruct((T, D), F32)],
        compiler_params=_cp(("arbitrary",)), name=name)(y, target)


HALO = 8


def _conv_taps(pad_ref, w_ref, tm, base):
    acc = w_ref[0:1, :] * pad_ref[pl.ds(base, tm), :]
    for k in range(1, GDN_CONV):
        acc = acc + w_ref[k:k + 1, :] * pad_ref[pl.ds(base + k, tm), :]
    return acc


GDN_GROUP_W = GDN_W
GDN_PRE_ROWS = 512


def _head_segments():
    head = jnp.arange(GDN_W, dtype=jnp.int32) // GDN_D
    return (head[:, None] == head[None, :]).astype(BF16)


def _head_sums(x, seg):
    hi = x.astype(BF16)
    r1 = x - hi.astype(F32)
    mid = r1.astype(BF16)
    lo = (r1 - mid.astype(F32)).astype(BF16)
    d = functools.partial(jnp.dot, preferred_element_type=F32)
    return d(hi, seg) + d(mid, seg) + d(lo, seg)


def _gdn_pre_common(x_ref, halo_ref, w_ref, seg_ref, pad_ref, tm):
    i = pl.program_id(1)
    grp = pl.program_id(0)
    pad_ref[0:HALO, :] = jnp.where(i == 0, 0.0, halo_ref[...])
    pad_ref[HALO:HALO + tm, :] = x_ref[...]
    c = _conv_taps(pad_ref, w_ref, tm, HALO - (GDN_CONV - 1))
    s = _sigmoid(c)
    y = c * s
    r = lax.rsqrt(_head_sums(y * y, seg_ref[...]) + RMS_EPS)
    scale = jnp.where(grp < 1, GDN_D ** -0.5, 1.0)
    return grp < 2, c, s, y, r, scale


def gdn_pre_fwd(proj, conv_w, name):
    T = proj.shape[0]
    tm = _pick(T, GDN_PRE_ROWS, 8)
    GW = GDN_GROUP_W

    def body(x_ref, halo_ref, w_ref, seg_ref, o_ref, pad_ref):
        normed, c, s, y, r, scale = _gdn_pre_common(x_ref, halo_ref, w_ref, seg_ref, pad_ref, tm)
        o_ref[...] = jnp.where(normed, y * r * scale, y)

    return pl.pallas_call(
        body, grid=(3, T // tm),
        in_specs=[pl.BlockSpec((tm, GW), lambda hb, i: (i, hb)),
                  pl.BlockSpec((HALO, GW), lambda hb, i: (jnp.maximum(i * (tm // HALO) - 1, 0), hb)),
                  pl.BlockSpec((GDN_CONV, GW), lambda hb, i: (0, hb)), pl.BlockSpec((GW, GW), lambda hb, i: (0, 0))],
        out_specs=pl.BlockSpec((tm, GW), lambda hb, i: (i, hb)),
        out_shape=jax.ShapeDtypeStruct((T, 3 * GW), F32),
        scratch_shapes=[pltpu.VMEM((tm + HALO, GW), F32)],
        compiler_params=_cp(("parallel", "parallel")), name=name)(proj, proj, conv_w, _head_segments())


def gdn_pre_bwd_pointwise(proj, conv_w, dqkv, name, hosted=None):
    T = proj.shape[0]
    tm = _pick(T, GDN_PRE_ROWS, 8)
    GW = GDN_GROUP_W

    def body(x_ref, halo_ref, w_ref, seg_ref, d_ref, dc_ref, dw_ref, pad_ref):
        i = pl.program_id(1)
        normed, c, s, y, r, scale = _gdn_pre_common(x_ref, halo_ref, w_ref, seg_ref, pad_ref, tm)

        @pl.when(i == 0)
        def _():
            dw_ref[...] = jnp.zeros_like(dw_ref)

        d = d_ref[...]
        n = y * r
        dn = d * scale
        dy = jnp.where(normed, r * (dn - n * _head_sums(dn * n, seg_ref[...])), d)
        dc = dy * (s * (1.0 + c * (1.0 - s)))
        dc_ref[...] = dc
        for k in range(GDN_CONV):
            xs = pad_ref[pl.ds(HALO - (GDN_CONV - 1) + k, tm), :]
            dw_ref[k:k + 1, :] += jnp.sum(dc * xs, axis=0, keepdims=True)

    blk = pl.BlockSpec((tm, GW), lambda hb, i: (i, hb))
    wblk = pl.BlockSpec((GDN_CONV, GW), lambda hb, i: (0, hb))
    return _hosted_call(
        hosted, body, grid=(3, T // tm),
        in_specs=[blk, pl.BlockSpec((HALO, GW), lambda hb, i: (jnp.maximum(i * (tm // HALO) - 1, 0), hb)), wblk,
                  pl.BlockSpec((GW, GW), lambda hb, i: (0, 0)), blk],
        out_specs=[blk, wblk],
        out_shape=[jax.ShapeDtypeStruct((T, 3 * GW), F32), jax.ShapeDtypeStruct((GDN_CONV, 3 * GW), F32)],
        scratch_shapes=[pltpu.VMEM((tm + HALO, GW), F32)],
        compiler_params=_cp(("parallel", "arbitrary")), name=name)(proj, proj, conv_w, _head_segments(), dqkv)


def gdn_pre_bwd_conv(dc, conv_w_p, name):
    T = dc.shape[0]
    tm = _pick(T, GDN_PRE_ROWS, 8)
    nt = T // tm
    GW = GDN_GROUP_W

    def body(dc_ref, halo_ref, w_ref, dx_ref, pad_ref):
        i = pl.program_id(1)
        pad_ref[0:tm, :] = dc_ref[...]
        pad_ref[tm:tm + HALO, :] = jnp.where(i == nt - 1, 0.0, halo_ref[...])
        acc = w_ref[GDN_CONV - 1:GDN_CONV, :] * pad_ref[pl.ds(0, tm), :]
        for k in range(GDN_CONV - 1):
            acc = acc + w_ref[k:k + 1, :] * pad_ref[pl.ds(GDN_CONV - 1 - k, tm), :]
        dx_ref[...] = acc

    blk = pl.BlockSpec((tm, GW), lambda hb, i: (i, hb))
    return pl.pallas_call(
        body, grid=(3, nt),
        in_specs=[blk, pl.BlockSpec((HALO, GW), lambda hb, i: (jnp.minimum((i + 1) * (tm // HALO), T // HALO - 1), hb)),
                  pl.BlockSpec((GDN_CONV, GW), lambda hb, i: (0, hb))],
        out_specs=blk,
        out_shape=jax.ShapeDtypeStruct((T, 3 * GW), F32),
        scratch_shapes=[pltpu.VMEM((tm + HALO, GW), F32)],
        compiler_params=_cp(("parallel", "parallel")), name=name)(dc, dc, conv_w_p)


def _chunk_masks(C):
    row = _iota2((C, C), 0)
    col = _iota2((C, C), 1)
    return row >= col, row > col, row == col


BNN = (((2,), (1,)), ((0,), (0,)))
BNT = (((2,), (2,)), ((0,), (0,)))
BTN = (((1,), (1,)), ((0,), (0,)))


def _bmm(a, b, dims=BNN):
    return lax.dot_general(a.astype(BF16), b.astype(BF16), dims, preferred_element_type=F32)


def _hbmm(a, b):
    m = a.shape[1]
    a_hi, a_lo = _split2(a)
    b_hi, b_lo = _split2(b)
    r = lax.dot_general(jnp.concatenate([a_hi, a_lo], axis=1), b_hi, BNN, preferred_element_type=F32)
    return r[:, :m] + r[:, m:] + lax.dot_general(a_hi, b_lo, BNN, preferred_element_type=F32)


def _hbmm_tn(a, b):
    a_hi, a_lo = _split2(a)
    b_hi, b_lo = _split2(b)
    d = functools.partial(lax.dot_general, dimension_numbers=BTN, preferred_element_type=F32)
    return d(a_hi, b_hi) + d(a_lo, b_hi) + d(a_hi, b_lo)


def _col_to_row(colv, eye):
    return jnp.sum(jnp.where(eye, colv, 0.0), axis=1, keepdims=True)


def _row_to_col(rowv, eye):
    return jnp.sum(jnp.where(eye, rowv, 0.0), axis=2, keepdims=True)


def _unit_lower_inverse(A, eye):
    C = A.shape[1]
    P = jnp.where(eye, 1.0, 0.0) - A
    Bp = _hbmm(A, A)
    for _ in range(4):
        R = _hbmm(jnp.concatenate([Bp, P], axis=1), Bp)
        Bp = R[:, :C]
        P = P + R[:, C:]
    return P + _hbmm(P, Bp)


def _stack_heads(ref, first_head, n):
    return jnp.stack([ref[:, pl.ds((first_head + h) * GDN_D, GDN_D)] for h in range(n)])


def _unstack_heads(ref, first_head, val):
    for h in range(val.shape[0]):
        ref[:, pl.ds((first_head + h) * GDN_D, GDN_D)] = val[h]


def _gdn_gates(gab, a_row, dt_row, incl):
    g_all = -jnp.exp(a_row) * _softplus(gab + dt_row)
    beta_all = _sigmoid(gab)
    gc_all = _ones_dot_left(incl.astype(BF16), g_all)
    return g_all, beta_all, gc_all


def _gdn_common(qkv_ref, gc_all, beta_all, incl, strict, eye):
    C, H = GDN_CHUNK, GDN_HEADS
    q, k, v = (_stack_heads(qkv_ref, j * H, H) for j in range(3))
    gc = jnp.stack([gc_all[:, h:h + 1] for h in range(H)])
    beta = jnp.stack([beta_all[:, H + h:H + h + 1] for h in range(H)])
    gc_row = _col_to_row(gc, eye)
    decay = jnp.where(incl, jnp.exp(jnp.where(incl, gc - gc_row, 0.0)), 0.0)
    e_gc = jnp.exp(gc)
    gl = gc[:, C - 1:C, :]
    e_gl = jnp.exp(gl)
    ekd = jnp.exp(gl - gc)
    kb = k * beta
    A = jnp.where(strict, _bmm(kb, k, BNT) * decay, 0.0)
    Pm = jnp.where(incl, _bmm(q, k, BNT) * decay, 0.0)
    return q, k, v, gc, beta, decay, e_gc, e_gl, ekd, kb, A, Pm


def gdn_chunk_fwd(qkv, proj, a_row, dt_row, norm_w, name, hosted=None):
    T = qkv.shape[0]
    C, H, Dh = GDN_CHUNK, GDN_HEADS, GDN_D
    N = T // C

    def body(qkv_ref, gz_ref, gab_ref, a_ref, dt_ref, nw_ref, o_ref, opre_ref, Tm_ref, Sin_ref, S_ref):
        n = pl.program_id(0)

        @pl.when(n == 0)
        def _():
            S_ref[...] = jnp.zeros_like(S_ref)

        incl, strict, eye = _chunk_masks(C)
        _, beta_all, gc_all = _gdn_gates(gab_ref[...], a_ref[...], dt_ref[...], incl)
        q, k, v, gc, beta, decay, e_gc, e_gl, ekd, kb, A, Pm = _gdn_common(qkv_ref, gc_all, beta_all, incl, strict, eye)
        Tm = _unit_lower_inverse(A, eye)
        u = _hbmm(Tm, v * beta)
        w = _hbmm(Tm, kb * e_gc)
        S = S_ref[...]
        v_new = u - _bmm(w, S)
        o = _bmm(q * e_gc, S) + _bmm(Pm, v_new)
        S_ref[...] = S * e_gl + _bmm(k * ekd, v_new, BTN)
        Sin_ref[0] = S
        Tm_ref[0] = Tm
        r = lax.rsqrt(jnp.mean(o * o, axis=-1, keepdims=True) + RMS_EPS)
        gz = _stack_heads(gz_ref, 0, H)
        _unstack_heads(opre_ref, 0, o)
        _unstack_heads(o_ref, 0, o * r * nw_ref[...] * (gz * _sigmoid(gz)))

    vec = pl.BlockSpec((1, LANES), lambda n: (0, 0))
    hblk = pl.BlockSpec((C, GDN_W), lambda n: (n, 0))
    sblk = pl.BlockSpec((1, H, Dh, Dh), lambda n: (n, 0, 0, 0))
    return _hosted_call(
        hosted, body, grid=(N,),
        in_specs=[pl.BlockSpec((C, 3 * GDN_W), lambda n: (n, 0)),
                  pl.BlockSpec((C, GDN_W), lambda n: (n, CB_GZ * LANES // GDN_W)),
                  pl.BlockSpec((C, LANES), lambda n: (n, CB_GAB)), vec, vec, pl.BlockSpec((1, Dh), lambda n: (0, 0))],
        out_specs=[hblk, hblk, sblk, sblk],
        out_shape=[jax.ShapeDtypeStruct((T, GDN_W), F32), jax.ShapeDtypeStruct((T, GDN_W), F32),
                   jax.ShapeDtypeStruct((N, H, Dh, Dh), F32), jax.ShapeDtypeStruct((N, H, Dh, Dh), F32)],
        scratch_shapes=[pltpu.VMEM((H, Dh, Dh), F32)],
        compiler_params=_cp(("arbitrary",)), name=name)(qkv, proj, proj, a_row, dt_row, norm_w)


def gdn_chunk_bwd(qkv, proj, a_row, dt_row, norm_w, opre, Tm_all, Sin_all, docat, name, hosted=None):
    T = qkv.shape[0]
    C, H, Dh = GDN_CHUNK, GDN_HEADS, GDN_D
    N = T // C

    def body(qkv_ref, gz_ref, gab_ref, a_ref, dt_ref, nw_ref, opre_ref, Tm_ref, Sin_ref, do_ref,
             dqkv_ref, dgz_ref, dgab_ref, da_ref, ddt_ref, dnw_ref, dS_ref):
        n = pl.program_id(0)

        @pl.when(n == 0)
        def _():
            dS_ref[...] = jnp.zeros_like(dS_ref)
            da_ref[...] = jnp.zeros_like(da_ref)
            ddt_ref[...] = jnp.zeros_like(ddt_ref)
            dnw_ref[...] = jnp.zeros_like(dnw_ref)

        incl, strict, eye = _chunk_masks(C)
        gab = gab_ref[...]
        g_all, beta_all, gc_all = _gdn_gates(gab, a_ref[...], dt_ref[...], incl)
        lane = _iota2((C, LANES), 1)
        rowi = _iota2((C, 1), 0)
        nw = nw_ref[...]
        q, k, v, gc, beta, decay, e_gc, e_gl, ekd, kb, A, Pm = _gdn_common(qkv_ref, gc_all, beta_all, incl, strict, eye)
        Tm = Tm_ref[0]
        S = Sin_ref[0]
        dS = dS_ref[...]
        kbe = kb * e_gc
        u = _hbmm(Tm, v * beta)
        w = _hbmm(Tm, kbe)
        qd = q * e_gc
        kd = k * ekd
        v_new = u - _bmm(w, S)
        o = _stack_heads(opre_ref, 0, H)
        gz = _stack_heads(gz_ref, 0, H)
        don = _stack_heads(do_ref, 0, H)
        r = lax.rsqrt(jnp.mean(o * o, axis=-1, keepdims=True) + RMS_EPS)
        nn = o * r
        sgz = _sigmoid(gz)
        silu = gz * sgz
        _unstack_heads(dgz_ref, 0, don * nn * nw * (sgz * (1.0 + gz * (1.0 - sgz))))
        dnn = don * nw * silu
        dnw_ref[...] += jnp.sum(jnp.sum(don * nn * silu, axis=0), axis=0, keepdims=True)
        do = r * (dnn - nn * jnp.mean(dnn * nn, axis=-1, keepdims=True))
        dv_new = _bmm(Pm, do, BTN) + _bmm(kd, dS)
        dPm = jnp.where(incl, _bmm(do, v_new, BNT), 0.0)
        dqd = _bmm(do, S, BNT)
        dkd = _bmm(v_new, dS, BNT)
        dS_ref[...] = _bmm(qd, do, BTN) + e_gl * dS - _bmm(w, dv_new, BTN)
        dgl = jnp.sum(jnp.sum(dS * S, axis=2, keepdims=True), axis=1, keepdims=True) * e_gl
        dw = -_bmm(dv_new, S, BNT)
        dvb = _hbmm_tn(Tm, dv_new)
        dkbe = _hbmm_tn(Tm, dw)
        dA = -jnp.where(strict, _bmm(dvb, u, BNT) + _bmm(dkbe, w, BNT), 0.0)
        dAD = dA * decay
        dPD = dPm * decay
        Gm = dA * A + dPm * Pm
        dgc = jnp.sum(Gm, axis=2, keepdims=True) - _row_to_col(jnp.sum(Gm, axis=1, keepdims=True), eye)
        dkb = _bmm(dAD, k) + dkbe * e_gc
        dk = _bmm(dAD, kb, BTN) + _bmm(dPD, q, BTN) + dkd * ekd + dkb * beta
        dq = _bmm(dPD, k) + dqd * e_gc
        tkd = jnp.sum(dkd * kd, axis=-1, keepdims=True)
        dgc = dgc + jnp.sum(dqd * qd, axis=-1, keepdims=True) - tkd + jnp.sum(dkbe * kbe, axis=-1, keepdims=True)
        dgl = dgl + jnp.sum(tkd, axis=1, keepdims=True)
        dgc = dgc + jnp.where(rowi == C - 1, dgl, 0.0)
        dbeta = jnp.sum(dvb * v, axis=-1, keepdims=True) + jnp.sum(dkb * k, axis=-1, keepdims=True)
        _unstack_heads(dqkv_ref, 0, dq)
        _unstack_heads(dqkv_ref, H, dk)
        _unstack_heads(dqkv_ref, 2 * H, dvb * beta)
        dgc_all = jnp.zeros((C, LANES), F32)
        dbeta_all = jnp.zeros((C, LANES), F32)
        for h in range(H):
            dgc_all = dgc_all + jnp.where(lane == h, dgc[h], 0.0)
            dbeta_all = dbeta_all + jnp.where(lane == H + h, dbeta[h], 0.0)
        upper = (_iota2((C, C), 0) <= _iota2((C, C), 1)).astype(BF16)
        dg_all = _ones_dot_left(upper, dgc_all)
        dga = dg_all * (-jnp.exp(a_ref[...])) * _sigmoid(gab + dt_ref[...])
        dgb = dbeta_all * beta_all * (1.0 - beta_all)
        dgab_ref[...] = jnp.where(lane < H, dga, jnp.where(lane < 2 * H, dgb, 0.0))
        da_ref[...] += jnp.sum(jnp.where(lane < H, dg_all * g_all, 0.0), axis=0, keepdims=True)
        ddt_ref[...] += jnp.sum(jnp.where(lane < H, dga, 0.0), axis=0, keepdims=True)

    rev = lambda n: N - 1 - n
    vec = pl.BlockSpec((1, LANES), lambda n: (0, 0))
    nwv = pl.BlockSpec((1, Dh), lambda n: (0, 0))
    hblk = pl.BlockSpec((C, GDN_W), lambda n: (rev(n), 0))
    sblk = pl.BlockSpec((1, H, Dh, Dh), lambda n: (rev(n), 0, 0, 0))
    qblk = pl.BlockSpec((C, 3 * GDN_W), lambda n: (rev(n), 0))
    return _hosted_call(
        hosted, body, grid=(N,),
        in_specs=[qblk, pl.BlockSpec((C, GDN_W), lambda n: (rev(n), CB_GZ * LANES // GDN_W)),
                  pl.BlockSpec((C, LANES), lambda n: (rev(n), CB_GAB)), vec, vec, nwv, hblk, sblk, sblk, hblk],
        out_specs=[qblk, hblk, pl.BlockSpec((C, LANES), lambda n: (rev(n), 0)), vec, vec, nwv],
        out_shape=[jax.ShapeDtypeStruct((T, 3 * GDN_W), F32), jax.ShapeDtypeStruct((T, GDN_W), F32),
                   jax.ShapeDtypeStruct((T, LANES), F32), jax.ShapeDtypeStruct((1, LANES), F32),
                   jax.ShapeDtypeStruct((1, LANES), F32), jax.ShapeDtypeStruct((1, Dh), F32)],
        scratch_shapes=[pltpu.VMEM((H, Dh, Dh), F32)],
        compiler_params=_cp(("arbitrary",)), name=name)(qkv, proj, proj, a_row, dt_row, norm_w, opre, Tm_all, Sin_all, docat)


ATT_BQ, ATT_BK = 256, 512
NEG_BIG = -1e30


def _att_blocks(T):
    bq, bk = min(ATT_BQ, T), min(ATT_BK, T)
    assert bk % bq == 0 and T % bk == 0
    return bq, bk


def _att_specs(T, bq, cbs):
    qspec = lambda cb: pl.BlockSpec((bq, LANES), lambda h, i: (i, cb + h))
    kspec = lambda cb: pl.BlockSpec((T, LANES), lambda h, i: (0, cb + h))
    return qspec, kspec


def _kblock(ref, kb, bk):
    return ref[pl.ds(pl.multiple_of(kb * bk, bk), bk), :]


def _att_pos(i, kb, bq, bk):
    qpos = i * bq + _iota2((bq, bk), 0)
    kpos = kb * bk + _iota2((bq, bk), 1)
    return qpos, kpos


def _suffix_sum(x):
    n = x.shape[1]
    lane = _iota2(x.shape, 1)
    d = 1
    while d < n:
        x = x + jnp.where(lane < n - d, pltpu.roll(x, n - d, 1), 0.0)
        d *= 2
    return x


def _prefix_sum(x):
    n = x.shape[1]
    lane = _iota2(x.shape, 1)
    d = 1
    while d < n:
        x = x + jnp.where(lane >= d, pltpu.roll(x, d, 1), 0.0)
        d *= 2
    return x


SB_BLOCK = 256
SB_DEAD = -104.0


def _sb_blocks(T):
    b = min(SB_BLOCK, T)
    assert T % b == 0 and T // b <= LANES
    return b, b


def sb_fwd(proj, name, hosted=None):
    T = proj.shape[0]
    H = SB_HEADS
    bq, bk = _sb_blocks(T)
    scale = SB_DIM ** -0.5

    def body(q_ref, k_ref, v_ref, o_ref, tot_ref):
        i = pl.program_id(1)
        qb = q_ref[...].astype(BF16)
        diag = (i * bq) // bk
        lane = _iota2((bq, LANES), 1)

        def block(kb, acc, R, masked):
            z = _bdot_nt(qb, _kblock(k_ref, kb, bk)) * scale
            sp = _softplus(z)
            if masked:
                qpos, kpos = _att_pos(i, kb, bq, bk)
                mask = kpos < qpos
                l1m = jnp.where(mask, -sp, 0.0)
            else:
                l1m = -sp
            W = jnp.exp((z - sp) + (_suffix_sum(l1m) - l1m) + R)
            if masked:
                W = jnp.where(mask, W, 0.0)
            acc = acc + _bdot(W, _kblock(v_ref, kb, bk))
            return acc, R + jnp.sum(l1m, axis=-1, keepdims=True)

        acc, R = block(diag, jnp.zeros((bq, LANES), F32), jnp.zeros((bq, 1), F32), True)

        def live(c):
            return jnp.logical_and(c[0] >= 0, jnp.max(c[2]) > SB_DEAD)

        def step(c):
            kb, acc, R, Rb = c
            acc, R_next = block(kb, acc, R, False)
            return kb - 1, acc, R_next, jnp.where(lane == kb, R, Rb)

        _, acc, _, Rb = lax.while_loop(live, step, (diag - 1, acc, R, jnp.where(lane == diag, 0.0, NEG_BIG)))
        o_ref[...] = acc
        tot_ref[...] = Rb

    qspec, kspec = _att_specs(T, bq, None)
    sds = jax.ShapeDtypeStruct((T, H * LANES), F32)
    oblk = pl.BlockSpec((bq, LANES), lambda h, i: (i, h))
    return _hosted_call(
        hosted, body, grid=(H, T // bq), in_specs=[qspec(CB_SQ), kspec(CB_SK), kspec(CB_SV)],
        out_specs=[oblk, oblk], out_shape=[sds, sds],
        compiler_params=_cp(("parallel", "parallel")), name=name)(proj, proj, proj)


def sb_bwd(proj, tot, docat, do_cb, name):
    T = proj.shape[0]
    H = SB_HEADS
    bq, bk = _sb_blocks(T)
    scale = SB_DIM ** -0.5

    def body(q_ref, k_ref, v_ref, tot_ref, do_ref, dq_ref, dk_ref, dv_ref):
        i = pl.program_id(1)

        @pl.when(i == 0)
        def _():
            dk_ref[...] = jnp.zeros_like(dk_ref)
            dv_ref[...] = jnp.zeros_like(dv_ref)

        qb = q_ref[...].astype(BF16)
        dob = do_ref[...].astype(BF16)
        Rb = tot_ref[...]
        diag = (i * bq) // bk
        lane = _iota2((bq, LANES), 1)
        first = lax.while_loop(
            lambda kb: jnp.logical_and(kb < diag, jnp.max(jnp.where(lane == kb, Rb, NEG_BIG)) <= SB_DEAD),
            lambda kb: kb + 1, jnp.int32(0))

        def block(kb, carry, masked):
            dq, Epre = carry
            R = jnp.sum(jnp.where(lane == kb, Rb, 0.0), axis=1, keepdims=True)
            kblk = _kblock(k_ref, kb, bk).astype(BF16)
            z = _bdot_nt(qb, kblk) * scale
            sp = _softplus(z)
            if masked:
                qpos, kpos = _att_pos(i, kb, bq, bk)
                mask = kpos < qpos
                l1m = jnp.where(mask, -sp, 0.0)
            else:
                l1m = -sp
            W = jnp.exp((z - sp) + (_suffix_sum(l1m) - l1m) + R)
            if masked:
                W = jnp.where(mask, W, 0.0)
            E = _bdot_nt(dob, _kblock(v_ref, kb, bk)) * W
            cexcl = (_prefix_sum(E) - E) + Epre
            neg = jnp.exp(-sp)
            dz = E * neg - cexcl * (1.0 - neg)
            if masked:
                dz = jnp.where(mask, dz, 0.0)
            dz = (dz * scale).astype(BF16)
            rows = pl.ds(pl.multiple_of(kb * bk, bk), bk)
            dk_ref[rows, :] += lax.dot_general(dz, qb, TN_DIMS, preferred_element_type=F32)
            dv_ref[rows, :] += lax.dot_general(W.astype(BF16), dob, TN_DIMS, preferred_element_type=F32)
            dq = dq + jnp.dot(dz, kblk, preferred_element_type=F32)
            return dq, Epre + jnp.sum(E, axis=-1, keepdims=True)

        init = (jnp.zeros((bq, LANES), F32), jnp.zeros((bq, 1), F32))
        carry = lax.fori_loop(first, diag, lambda kb, c: block(kb, c, False), init)
        dq, _ = block(diag, carry, True)
        dq_ref[...] = dq

    qspec, kspec = _att_specs(T, bq, None)
    sds = jax.ShapeDtypeStruct((T, H * LANES), F32)
    oblk = pl.BlockSpec((bq, LANES), lambda h, i: (i, h))
    kout = pl.BlockSpec((T, LANES), lambda h, i: (0, h))
    return pl.pallas_call(
        body, grid=(H, T // bq),
        in_specs=[qspec(CB_SQ), kspec(CB_SK), kspec(CB_SV), oblk, qspec(do_cb)],
        out_specs=[oblk, kout, kout], out_shape=[sds, sds, sds],
        compiler_params=_cp(("arbitrary", "arbitrary")), name=name)(proj, proj, proj, tot, docat)


def mla_fwd(Q, K, V, name, hosted=None):
    T = Q.shape[0]
    H = MLA_HEADS
    bq, bk = _att_blocks(T)
    scale = (MLA_NOPE + MLA_ROPE) ** -0.5

    def body(q_ref, k_ref, v_ref, o_ref, lse_ref):
        i = pl.program_id(1)
        qb = q_ref[...]
        diag = (i * bq) // bk

        def block(kb, carry, masked):
            acc, m, l = carry
            s = _bdot_nt(qb, _kblock(k_ref, kb, bk)) * scale
            if masked:
                qpos, kpos = _att_pos(i, kb, bq, bk)
                s = jnp.where(kpos <= qpos, s, NEG_BIG)
            m_new = jnp.maximum(m, jnp.max(s, axis=-1, keepdims=True))
            p = jnp.exp(s - m_new)
            corr = jnp.exp(m - m_new)
            acc = corr * acc + _bdot(p, _kblock(v_ref, kb, bk))
            return acc, m_new, corr * l + jnp.sum(p, axis=-1, keepdims=True)

        init = (jnp.zeros((bq, LANES), F32), jnp.full((bq, 1), NEG_BIG, F32), jnp.zeros((bq, 1), F32))
        carry = lax.fori_loop(0, diag, lambda kb, c: block(kb, c, False), init)
        acc, m, l = block(diag, carry, True)
        o_ref[...] = acc / l
        lse_ref[...] = jnp.broadcast_to(m + jnp.log(l), (bq, LANES))

    qspec, kspec = _att_specs(T, bq, None)
    sds = jax.ShapeDtypeStruct((T, H * LANES), F32)
    oblk = pl.BlockSpec((bq, LANES), lambda h, i: (i, h))
    return _hosted_call(
        hosted, body, grid=(H, T // bq), in_specs=[qspec(0), kspec(0), kspec(0)],
        out_specs=[oblk, oblk], out_shape=[sds, sds],
        compiler_params=_cp(("parallel", "parallel")), name=name)(Q, K, V)


def mla_bwd(Q, K, V, o, lse, docat, do_cb, name, hosted=None):
    T = Q.shape[0]
    H = MLA_HEADS
    bq, bk = _att_blocks(T)
    scale = (MLA_NOPE + MLA_ROPE) ** -0.5

    def body(q_ref, k_ref, v_ref, o_ref, lse_ref, do_ref, dq_ref, dk_ref, dv_ref):
        i = pl.program_id(1)

        @pl.when(i == 0)
        def _():
            dk_ref[...] = jnp.zeros_like(dk_ref)
            dv_ref[...] = jnp.zeros_like(dv_ref)

        qb = q_ref[...]
        do = do_ref[...]
        dob = do.astype(BF16)
        delta = jnp.sum(do * o_ref[...], axis=-1, keepdims=True)
        lse = lse_ref[:, 0:1]

        diag = (i * bq) // bk

        def block(kb, dq, masked):
            kblk = _kblock(k_ref, kb, bk)
            s = _bdot_nt(qb, kblk) * scale
            if masked:
                qpos, kpos = _att_pos(i, kb, bq, bk)
                s = jnp.where(kpos <= qpos, s, NEG_BIG)
            p = jnp.exp(s - lse)
            dp = _bdot_nt(dob, _kblock(v_ref, kb, bk))
            ds = (p * (dp - delta) * scale).astype(BF16)
            rows = pl.ds(pl.multiple_of(kb * bk, bk), bk)
            dk_ref[rows, :] += lax.dot_general(ds, qb, TN_DIMS, preferred_element_type=F32)
            dv_ref[rows, :] += lax.dot_general(p.astype(BF16), dob, TN_DIMS, preferred_element_type=F32)
            return dq + jnp.dot(ds, kblk, preferred_element_type=F32)

        dq = lax.fori_loop(0, diag, lambda kb, c: block(kb, c, False), jnp.zeros((bq, LANES), F32))
        dq_ref[...] = block(diag, dq, True)

    qspec, kspec = _att_specs(T, bq, None)
    sds = jax.ShapeDtypeStruct((T, H * LANES), F32)
    oblk = pl.BlockSpec((bq, LANES), lambda h, i: (i, h))
    kout = pl.BlockSpec((T, LANES), lambda h, i: (0, h))
    return _hosted_call(
        hosted, body, grid=(H, T // bq),
        in_specs=[qspec(0), kspec(0), kspec(0), oblk, oblk, qspec(do_cb)],
        out_specs=[oblk, kout, kout], out_shape=[sds, sds, sds],
        compiler_params=_cp(("arbitrary", "arbitrary")), name=name)(Q, K, V, o, lse, docat)


def _tile_heads(t, n):
    return jnp.concatenate([t] * n, axis=1)


def _rope(X, C, Sn, Sp):
    n = X.shape[1]
    return X * C + pltpu.roll(X, n - HALF_ROPE, 1) * Sn + pltpu.roll(X, HALF_ROPE, 1) * Sp


def _rope_t(dO, C, Sn, Sp):
    n = dO.shape[1]
    return dO * C + pltpu.roll(dO * Sn, HALF_ROPE, 1) + pltpu.roll(dO * Sp, n - HALF_ROPE, 1)


def _rms(x, w):
    r = lax.rsqrt(jnp.mean(x * x, axis=-1, keepdims=True) + RMS_EPS)
    xh = x * r
    return r, xh, xh * w


def _rms_bwd(dn, w, r, xh):
    dxh = dn * w
    return r * (dxh - xh * jnp.mean(dxh * xh, axis=-1, keepdims=True)), jnp.sum(dn * xh, axis=0, keepdims=True)


def _mla_pre_specs(T, tm):
    KV = MLA_KV_RANK
    QR = MLA_Q_RANK
    W = MLA_HEADS * LANES
    full = lambda shape: pl.BlockSpec(shape, lambda i: (0, 0))
    specs = [pl.BlockSpec((tm, QR), lambda i: (i, CB_MQ * LANES // QR)),
             pl.BlockSpec((tm, 2 * LANES), lambda i: (i, CB_MKV // 2)),
             full((1, QR)), full((1, KV))]
    rope = [pl.BlockSpec((tm, LANES), lambda i: (i, 0))] * 3
    return specs, rope, full, W


def mla_pre_fwd(proj, wq, wkv, wuq, wuk, wuv, ropeC, ropeSn, ropeSp, name):
    T = proj.shape[0]
    tm = _pick(T, 512, 16)
    KV = MLA_KV_RANK
    H = MLA_HEADS

    def body(mq_ref, mkv_ref, wq_ref, wkv_ref, wuq_ref, wuk_ref, wuv_ref, c_ref, sn_ref, sp_ref, Q_ref, K_ref, V_ref):
        C, Sn, Sp = (_tile_heads(t[...], H) for t in (c_ref, sn_ref, sp_ref))
        _, _, qn = _rms(mq_ref[...], wq_ref[...])
        Q_ref[...] = _rope(_bdot(qn, wuq_ref[...]), C, Sn, Sp).astype(BF16)
        mkv = mkv_ref[...]
        _, _, kvn = _rms(mkv[:, :KV], wkv_ref[...])
        kr = pltpu.roll(mkv[:, KV:], MLA_NOPE, 1)
        K_ref[...] = _rope(_bdot(kvn, wuk_ref[...]) + _tile_heads(kr, H), C, Sn, Sp).astype(BF16)
        V_ref[...] = _bdot(kvn, wuv_ref[...]).astype(BF16)

    specs, rope, full, W = _mla_pre_specs(T, tm)
    oblk = pl.BlockSpec((tm, W), lambda i: (i, 0))
    sds = jax.ShapeDtypeStruct((T, W), BF16)
    return pl.pallas_call(
        body, grid=(T // tm,),
        in_specs=specs + [full(wuq.shape), full(wuk.shape), full(wuv.shape)] + rope,
        out_specs=[oblk, oblk, oblk], out_shape=[sds, sds, sds],
        compiler_params=_cp(("parallel",)), name=name)(proj, proj, wq, wkv, wuq, wuk, wuv, ropeC, ropeSn, ropeSp)


def mla_pre_bwd(proj, wq, wkv, wuq, wuk, wuv, ropeC, ropeSn, ropeSp, dQ, dK, dV, name):
    T = proj.shape[0]
    tm = _pick(T, 512, 16)
    KV = MLA_KV_RANK
    H = MLA_HEADS

    def body(mq_ref, mkv_ref, wq_ref, wkv_ref, wuq_ref, wuk_ref, wuv_ref,
             c_ref, sn_ref, sp_ref, dQ_ref, dK_ref, dV_ref,
             dmq_ref, dmkv_ref, dwuq_ref, dwuk_ref, dwuv_ref, dwq_ref, dwkv_ref):
        i = pl.program_id(0)

        @pl.when(i == 0)
        def _():
            for ref in (dwuq_ref, dwuk_ref, dwuv_ref, dwq_ref, dwkv_ref):
                ref[...] = jnp.zeros_like(ref)

        C, Sn, Sp = (_tile_heads(t[...], H) for t in (c_ref, sn_ref, sp_ref))
        rq, xq, qn = _rms(mq_ref[...], wq_ref[...])
        mkv = mkv_ref[...]
        rkv, xkv, kvn = _rms(mkv[:, :KV], wkv_ref[...])
        dqf = _rope_t(dQ_ref[...], C, Sn, Sp)
        dkf = _rope_t(dK_ref[...], C, Sn, Sp)
        dv = dV_ref[...]
        dwuq_ref[...] += _bdot_tn(qn, dqf)
        dwuk_ref[...] += _bdot_tn(kvn, dkf)
        dwuv_ref[...] += _bdot_tn(kvn, dv)
        dmq, dwq = _rms_bwd(_bdot_nt(dqf, wuq_ref[...]), wq_ref[...], rq, xq)
        dckv, dwkv = _rms_bwd(_bdot_nt(dkf, wuk_ref[...]) + _bdot_nt(dv, wuv_ref[...]), wkv_ref[...], rkv, xkv)
        dwq_ref[...] += dwq
        dwkv_ref[...] += dwkv
        dmq_ref[...] = dmq
        dkr = dkf[:, 0:LANES]
        for h in range(1, H):
            dkr = dkr + dkf[:, h * LANES:(h + 1) * LANES]
        dkr = pltpu.roll(dkr, LANES - MLA_NOPE, 1)
        dkr = jnp.where(_iota2(dkr.shape, 1) < MLA_ROPE, dkr, 0.0)
        dmkv_ref[...] = jnp.concatenate([dckv, dkr], axis=1)

    specs, rope, full, W = _mla_pre_specs(T, tm)
    wide = pl.BlockSpec((tm, W), lambda i: (i, 0))
    return pl.pallas_call(
        body, grid=(T // tm,),
        in_specs=specs + [full(w.shape) for w in (wuq, wuk, wuv)] + rope + [wide, wide, wide],
        out_specs=[pl.BlockSpec((tm, MLA_Q_RANK), lambda i: (i, 0)), pl.BlockSpec((tm, 2 * LANES), lambda i: (i, 0)),
                   full(wuq.shape), full(wuk.shape), full(wuv.shape), full((1, MLA_Q_RANK)), full((1, KV))],
        out_shape=[jax.ShapeDtypeStruct((T, MLA_Q_RANK), F32), jax.ShapeDtypeStruct((T, 2 * LANES), F32),
                   jax.ShapeDtypeStruct(wuq.shape, F32), jax.ShapeDtypeStruct(wuk.shape, F32),
                   jax.ShapeDtypeStruct(wuv.shape, F32), jax.ShapeDtypeStruct((1, MLA_Q_RANK), F32),
                   jax.ShapeDtypeStruct((1, KV), F32)],
        compiler_params=_cp(("arbitrary",)), name=name)(
            proj, proj, wq, wkv, wuq, wuk, wuv, ropeC, ropeSn, ropeSp, dQ, dK, dV)


def all_gather(shards, name):
    n = len(shards)

    def body(*refs):
        x_refs, out_refs = refs[:n], refs[n:2 * n]
        send_sems, recv_sems, local_sems = refs[2 * n:]
        x, y, c = _place()
        me, sibling = (x, y, c), (x, y, 1 - c)
        chips = [(1 - x, y), (x, 1 - y), (1 - x, 1 - y)]

        def slot(a, px, py, pc):
            return out_refs[a].at[4 * px + 2 * py + pc]

        def copy(a, k, block, to, src=None):
            return pltpu.make_async_remote_copy(
                src_ref=slot(a, *block) if src is None else src, dst_ref=slot(a, *block),
                send_sem=send_sems.at[a, k], recv_sem=recv_sems.at[a, k], device_id=to, device_id_type=MESH)

        mine = [pltpu.make_async_copy(x_refs[a], slot(a, *me), local_sems.at[a]) for a in range(n)]
        first = []
        for a in range(n):
            mine[a].start()
            first.append(copy(a, 0, me, sibling, src=x_refs[a]))
            first += [copy(a, 1 + j, me, (*chip, c), src=x_refs[a]) for j, chip in enumerate(chips)]
        for cp in first:
            cp.start()
        passed = []
        for j, chip in enumerate(chips):
            for a in range(n):
                copy(a, 1 + j, (*chip, c), me).wait_recv()
                passed.append(copy(a, 4 + j, (*chip, c), sibling))
                passed[-1].start()
        for a in range(n):
            copy(a, 0, sibling, me).wait_recv()
            for j, chip in enumerate(chips):
                copy(a, 4 + j, (*chip, 1 - c), me).wait_recv()
        for cp in first + passed:
            cp.wait_send()
        for cp in mine:
            cp.wait()

    return pl.pallas_call(
        body, out_shape=[jax.ShapeDtypeStruct((N_DEV,) + s.shape, s.dtype) for s in shards],
        in_specs=[ANY] * n, out_specs=[ANY] * n,
        scratch_shapes=[pltpu.SemaphoreType.DMA((n, 7)), pltpu.SemaphoreType.DMA((n, 7)), pltpu.SemaphoreType.DMA((n,))],
        name=name)(*shards)


def exchange_partials(parts, name):
    n = len(parts)

    def body(*refs):
        src_refs, dst_refs = refs[:n], refs[n:2 * n]
        send_sems, recv_sems, local_sems = refs[2 * n:]
        x, y, c = _place()
        me = 4 * x + 2 * y + c
        copies = []
        mine = []
        for a in range(n):
            mine.append(pltpu.make_async_copy(src_refs[a].at[me], dst_refs[a].at[me], local_sems.at[a]))
            for k in range(1, N_DEV):
                px = 1 - x if k & 4 else x
                py = 1 - y if k & 2 else y
                pc = 1 - c if k & 1 else c
                copies.append(pltpu.make_async_remote_copy(
                    src_ref=src_refs[a].at[4 * px + 2 * py + pc], dst_ref=dst_refs[a].at[me],
                    send_sem=send_sems.at[a, k - 1], recv_sem=recv_sems.at[a, k - 1],
                    device_id=(px, py, pc), device_id_type=MESH))
        for cp in mine + copies:
            cp.start()
        for cp in copies:
            cp.wait_recv()
        for cp in copies:
            cp.wait_send()
        for cp in mine:
            cp.wait()

    return pl.pallas_call(
        body, out_shape=[jax.ShapeDtypeStruct(p.shape, p.dtype) for p in parts],
        in_specs=[ANY] * n, out_specs=[ANY] * n,
        scratch_shapes=[pltpu.SemaphoreType.DMA((n, 7)), pltpu.SemaphoreType.DMA((n, 7)), pltpu.SemaphoreType.DMA((n,))],
        name=name)(*parts)


def reduce_adamw(parts, w, m, v, name):
    L = len(parts)
    n, Rl, C = parts[0].shape
    R = w.shape[0]
    assert R == L * Rl
    tr = Rl if Rl * C <= 256 * 1024 else _pick(Rl, 256, 16)
    nr = Rl // tr

    def body(*refs):
        p_refs = refs[:L]
        w_ref, m_ref, v_ref, g_ref, d_ref, nm_ref, nv_ref, sum_ref = refs[L:]
        grp = pl.program_id(0)
        for j in range(L):
            @pl.when(grp == j)
            def _(j=j):
                acc = p_refs[j][0].astype(F32)
                for s in range(1, n):
                    acc = acc + p_refs[j][s].astype(F32)
                sum_ref[...] = acc

        g_ = sum_ref[...]
        m_ = ADAM_B1 * m_ref[...] + (1.0 - ADAM_B1) * g_
        v_ = ADAM_B2 * v_ref[...] + (1.0 - ADAM_B2) * (g_ * g_)
        m_hat = m_ / (1.0 - ADAM_B1 ** ADAM_STEP)
        v_hat = v_ / (1.0 - ADAM_B2 ** ADAM_STEP)
        g_ref[...] = g_
        d_ref[...] = -ADAM_LR * (m_hat / (jnp.sqrt(v_hat) + ADAM_EPS) + ADAM_WD * w_ref[...])
        nm_ref[...] = m_
        nv_ref[...] = v_

    blk = pl.BlockSpec((tr, C), lambda l, r: (l * nr + r, 0))
    sds = jax.ShapeDtypeStruct((R, C), F32)
    p_specs = [pl.BlockSpec((n, tr, C), lambda l, r, j=j: (0, jnp.where(l == j, r, 0), 0)) for j in range(L)]
    return pl.pallas_call(
        body, grid=(L, nr), in_specs=p_specs + [blk] * 3,
        out_specs=[blk] * 4, out_shape=[sds] * 4, scratch_shapes=[pltpu.VMEM((tr, C), F32)],
        compiler_params=_cp(("arbitrary", "arbitrary")), name=name)(*parts, w, m, v)


SHARDED = {"ffa_w_in": (2, BF16), "ffa_w_out": (1, BF16), "mix_w_in": (2, BF16), "mla_w_uq": (2, BF16),
           "mla_w_ukv": (2, BF16), "mix_w_o": (1, BF16), "ffb_w_in": (2, BF16), "ffb_w_out": (1, BF16),
           "ple_w_gate": (1, BF16), "ple_w_proj": (2, BF16), "gdn_conv_w": (2, F32), "ln_g": (2, F32), "ln_b": (2, F32)}
FFN_SLOT = ("ffa_w_in", "ffa_w_out", "ffb_w_in", "ffb_w_out")
REPLICATED = ("gdn_a_log", "gdn_dt_bias", "gdn_norm_w", "mla_q_norm_w", "mla_kv_norm_w")
WEIGHTS = ("ffa_w_in", "ffa_w_out", "mix_w_in", "gdn_conv_w", "gdn_a_log", "gdn_dt_bias", "gdn_norm_w", "mla_q_norm_w",
           "mla_kv_norm_w", "mla_w_uq", "mla_w_ukv", "mix_w_o", "ffb_w_in", "ffb_w_out", "ln_g", "ln_b", "ple_w_gate",
           "ple_w_proj")


def _to_slots(full, axis):
    L, a, b = full.shape
    if axis == 2:
        return full.reshape(L, a, N_DEV, b // N_DEV).transpose(2, 0, 1, 3).reshape(N_DEV, L * a, b // N_DEV)
    return full.reshape(L, N_DEV, a // N_DEV, b).transpose(1, 0, 2, 3).reshape(N_DEV, L * a // N_DEV, b)


def _from_slots(slots, shard_shape, axis):
    L, a, b = shard_shape
    t = slots.reshape((N_DEV,) + tuple(shard_shape))
    if axis == 2:
        return t.transpose(1, 2, 0, 3).reshape(L, a, N_DEV * b)
    return t.transpose(1, 0, 2, 3).reshape(L, N_DEV * a, b)


def _view2d(t):
    return t.reshape(-1, t.shape[-1])


def _pad_heads(w, nh):
    K = w.shape[0]
    return jnp.pad(w.reshape(K, nh, GDN_D), ((0, 0), (0, 0), (0, LANES - GDN_D))).reshape(K, nh * LANES)


def _unpad_heads(w, nh):
    K = w.shape[0]
    return w.reshape(K, nh, LANES)[:, :, :GDN_D].reshape(K, nh * GDN_D)


IN_WIDTHS = (512, 512, 512, 512, 8, 8, 256, 256, 256, 256, 160)


def _split_in(w):
    offs = np.cumsum((0,) + IN_WIDTHS)
    return [w[:, int(offs[i]):int(offs[i + 1])] for i in range(len(IN_WIDTHS))]


def _pad_in_proj(w):
    gq, gk, gv, gz, ga, gb, sq, sk, sv, mq, mkv = _split_in(w)
    gab = jnp.pad(jnp.concatenate([ga, gb], axis=1), ((0, 0), (0, LANES - 2 * GDN_HEADS)))
    return jnp.concatenate(
        [gq, gk, gv, gz] + [_pad_heads(t, SB_HEADS) for t in (sq, sk, sv)]
        + [mq, jnp.pad(mkv, ((0, 0), (0, 2 * LANES - mkv.shape[1]))), gab], axis=1)


def _unpad_in_proj(wp):
    c = lambda cb, n: wp[:, cb * LANES:(cb + n) * LANES]
    gab = c(CB_GAB, 1)
    parts = [c(cb, DO_SB) for cb in (CB_GQ, CB_GK, CB_GV, CB_GZ)]
    parts += [gab[:, :GDN_HEADS], gab[:, GDN_HEADS:2 * GDN_HEADS]]
    parts += [_unpad_heads(c(cb, SB_HEADS), SB_HEADS) for cb in (CB_SQ, CB_SK, CB_SV)]
    parts += [c(CB_MQ, 2), c(CB_MKV, 2)[:, :MLA_KV_RANK + MLA_ROPE]]
    return jnp.concatenate(parts, axis=1)


def _pad_lanes(w, width):
    return jnp.pad(w, ((0, 0), (0, width - w.shape[1])))


def _mla_up_pad(w_uq, w_ukv):
    H = MLA_HEADS
    dq = MLA_NOPE + MLA_ROPE
    wuq = jnp.pad(w_uq.reshape(-1, H, dq), ((0, 0), (0, 0), (0, LANES - dq))).reshape(-1, H * LANES)
    kv = w_ukv.reshape(-1, H, MLA_NOPE + MLA_V)
    wuk = jnp.pad(kv[:, :, :MLA_NOPE], ((0, 0), (0, 0), (0, LANES - MLA_NOPE))).reshape(-1, H * LANES)
    wuv = jnp.pad(kv[:, :, MLA_NOPE:], ((0, 0), (0, 0), (0, LANES - MLA_V))).reshape(-1, H * LANES)
    return wuq, wuk, wuv


def _mla_up_unpad(dwuq, dwuk, dwuv):
    H = MLA_HEADS
    dq = MLA_NOPE + MLA_ROPE
    g_uq = dwuq.reshape(-1, H, LANES)[:, :, :dq].reshape(-1, H * dq)
    g_ukv = jnp.concatenate([dwuk.reshape(-1, H, LANES)[:, :, :MLA_NOPE], dwuv.reshape(-1, H, LANES)[:, :, :MLA_V]],
                            axis=2).reshape(-1, H * (MLA_NOPE + MLA_V))
    return g_uq, g_ukv


def _rope_tables(positions):
    inv = 1.0 / (ROPE_BASE ** (jnp.arange(0, MLA_ROPE, 2, dtype=F32) / MLA_ROPE))
    ang = positions.astype(F32)[:, None] * inv
    cos, sin = jnp.cos(ang), jnp.sin(ang)
    T = positions.shape[0]
    one = lambda n: jnp.ones((T, n), F32)
    zero = lambda n: jnp.zeros((T, n), F32)
    tail = LANES - MLA_NOPE - MLA_ROPE
    C = jnp.concatenate([one(MLA_NOPE), cos, cos, one(tail)], axis=1)
    Sn = jnp.concatenate([zero(MLA_NOPE), -sin, zero(HALF_ROPE + tail)], axis=1)
    Sp = jnp.concatenate([zero(MLA_NOPE + HALF_ROPE), sin, zero(tail)], axis=1)
    return C, Sn, Sp


GATHER_FIRST = [("ffa_w_in", 0), ("ffa_w_out", 0)] + [(n, l) for l in range(DEPTH) for n in ("gdn_conv_w", "ln_g", "ln_b")]
GATHER_PLAN = {
    (0, "ffa_fwd"): [("mix_w_in", 0), ("mla_w_uq", 0), ("mla_w_ukv", 0)],
    (0, "gdn_chunk_fwd"): [("mix_w_o", 0), ("ffb_w_in", 0)],
    (0, "sb_fwd"): [("ffb_w_out", 0), ("ple_w_gate", 0), ("ple_w_proj", 0)],
    (0, "mla_fwd"): [("ffa_w_in", 1), ("mix_w_o", 1)],
    (0, "ffb_fwd"): [("ffa_w_out", 1), ("mix_w_in", 1)],
    (1, "ffa_fwd"): [("ffb_w_in", 1)],
    (1, "in_proj"): [("ffb_w_out", 1), ("ple_w_gate", 1), ("ple_w_proj", 1), ("mla_w_uq", 1), ("mla_w_ukv", 1)],
}
SCATTER_PLAN = {
    (1, "gdn_chunk_bwd"): [("ffb_w_in", 1)],
    (1, "gdn_pre_bwd"): [("ffb_w_out", 1), ("ple_w_gate", 1), ("ple_w_proj", 1), ("mix_w_o", 1)],
    (1, "ffa_bwd"): [("mix_w_in", 1), ("mla_w_uq", 1), ("mla_w_ukv", 1), ("gdn_conv_w", 1)],
    (0, "ffb_bwd"): [("ffa_w_in", 1)],
    (0, "gdn_chunk_bwd"): [("ffb_w_in", 0)],
    (0, "gdn_pre_bwd"): [("ffb_w_out", 0), ("ple_w_gate", 0), ("ple_w_proj", 0), ("mix_w_o", 0)],
    (0, "mla_bwd"): [("ffa_w_out", 1), ("ln_g", 1), ("ln_b", 1)],
    (0, "ffa_bwd"): [("mix_w_in", 0), ("mla_w_uq", 0), ("mla_w_ukv", 0), ("gdn_conv_w", 0)],
    (0, "d_ffa_in"): [("ffa_w_out", 0), ("ln_g", 0), ("ln_b", 0)],
}
SCATTER_LAST = [("ffa_w_in", 0)]


class Exchanges:
    def __init__(self, shards):
        self.shards = shards
        self.full = {}
        self.partial = {}
        self.received = {}

    def _block(self, key):
        n, l = key
        return self.shards[n][l].astype(SHARDED[n][1])

    def _absorb_gather(self, keys, results):
        for (n, l), g in zip(keys, results):
            blk = self.shards[n][l]
            self.full[(n, l)] = g if n in FFN_SLOT else _from_slots(g, (1,) + blk.shape, SHARDED[n][0])[0]

    def gather_now(self, keys, name):
        self._absorb_gather(keys, all_gather([self._block(k) for k in keys], name))

    def gather_with(self, layer, tag):
        keys = GATHER_PLAN.get((layer, tag))
        return None if keys is None else (keys, Hosted("gather", [self._block(k) for k in keys]))

    def scatter_with(self, layer, tag):
        keys = SCATTER_PLAN.get((layer, tag))
        return None if keys is None else (keys, Hosted("scatter", [self.partial[k] for k in keys]))

    def done(self, carried):
        if carried is not None:
            keys, hosted = carried
            if hosted.kind == "gather":
                self._absorb_gather(keys, hosted.results)
            else:
                self.received.update(zip(keys, hosted.results))

    def add_grad(self, key, g):
        n, l = key
        self.partial[key] = g if n in FFN_SLOT else _to_slots(g[None], SHARDED[n][0]).astype(SHARDED[n][1])


def _carried(c):
    return None if c is None else c[1]


def _layer_fwd(h0, p_i, rope, i, ex, rep):
    L = "L%d_" % i
    S = {"h0": h0, "p": p_i}
    W = ex.full
    ln_g = [W[("ln_g", i)][j][None, :] for j in range(3)]
    ln_b = [W[("ln_b", i)][j][None, :] for j in range(3)]
    S["ln_g"] = ln_g
    c = ex.gather_with(i, "ffa_fwd")
    S["h1"], S["xh1"], S["rs1"], S["Ga"], S["Ua"] = ffn_fwd(h0, W[("ffa_w_in", i)], W[("ffa_w_out", i)], ln_g[0], ln_b[0],
                                                            L + "ffa_fwd", hosted=_carried(c))
    ex.done(c)
    S["win"] = _pad_in_proj(W[("mix_w_in", i)])
    c = ex.gather_with(i, "in_proj")
    S["proj"] = mm_nn(S["h1"], S["win"], L + "in_proj", hosted=_carried(c))
    ex.done(c)
    S["conv"] = W[("gdn_conv_w", i)]
    S["a_row"] = _pad_lanes(rep["gdn_a_log"][i][None, :], LANES)
    S["dt_row"] = _pad_lanes(rep["gdn_dt_bias"][i][None, :], LANES)
    S["nw"] = rep["gdn_norm_w"][i][None, :]
    S["wq"] = rep["mla_q_norm_w"][i][None, :]
    S["wkv"] = rep["mla_kv_norm_w"][i][None, :]
    S["qkv"] = gdn_pre_fwd(S["proj"], S["conv"], L + "gdn_pre_fwd")
    c = ex.gather_with(i, "gdn_chunk_fwd")
    S["o_gdn"], S["opre"], S["Tm"], S["Sin"] = gdn_chunk_fwd(S["qkv"], S["proj"], S["a_row"], S["dt_row"], S["nw"],
                                                            L + "gdn_chunk_fwd", hosted=_carried(c))
    ex.done(c)
    c = ex.gather_with(i, "sb_fwd")
    S["o_sb"], S["tot"] = sb_fwd(S["proj"], L + "sb_fwd", hosted=_carried(c))
    ex.done(c)
    S["wuq"], S["wuk"], S["wuv"] = _mla_up_pad(W[("mla_w_uq", i)], W[("mla_w_ukv", i)])
    S["Q"], S["K"], S["V"] = mla_pre_fwd(S["proj"], S["wq"], S["wkv"], S["wuq"], S["wuk"], S["wuv"], *rope, L + "mla_pre_fwd")
    c = ex.gather_with(i, "mla_fwd")
    S["o_mla"], S["lse"] = mla_fwd(S["Q"], S["K"], S["V"], L + "mla_fwd", hosted=_carried(c))
    ex.done(c)
    wo = W[("mix_w_o", i)]
    wo_att = wo[GDN_W:].reshape(-1, GDN_D, wo.shape[1])
    S["wo"] = jnp.concatenate(
        [wo[:GDN_W], jnp.pad(wo_att, ((0, 0), (0, LANES - GDN_D), (0, 0))).reshape(-1, wo.shape[1])], axis=0)
    S["h2"], S["xh2"], S["rs2"] = proj_res_ln([S["o_gdn"], S["o_sb"], S["o_mla"]], S["wo"], S["h1"],
                                              ln_g[1], ln_b[1], L + "out_proj")
    c = ex.gather_with(i, "ffb_fwd")
    S["h3"], S["xh3"], S["rs3"], S["Gb"], S["Ub"] = ffn_fwd(S["h2"], W[("ffb_w_in", i)], W[("ffb_w_out", i)], ln_g[2], ln_b[2],
                                                            L + "ffb_fwd", hosted=_carried(c))
    ex.done(c)
    h4, S["a"], S["e"] = ple_fwd(S["h3"], p_i, W[("ple_w_gate", i)], W[("ple_w_proj", i)], L + "ple_fwd")
    return h4, S


def _layer_bwd(dh4, S, rope, i, ex):
    L = "L%d_" % i
    W = ex.full
    Grep = {}
    dh3, da, de = ple_bwd(dh4, S["a"], S["e"], W[("ple_w_gate", i)], L + "ple_bwd")
    ex.add_grad(("ple_w_gate", i), mm_tn(S["h3"], da, L + "d_ple_gate"))
    ex.add_grad(("ple_w_proj", i), mm_tn(S["p"], de, L + "d_ple_proj"))
    dz3, dg2, db2 = ln_bwd(dh3, S["xh3"], S["rs3"], S["ln_g"][2], L + "ln3_bwd")
    c = ex.scatter_with(i, "ffb_bwd")
    dh2, dGb, dUb, actb = ffn_bwd(dz3, S["Gb"], S["Ub"], W[("ffb_w_in", i)], W[("ffb_w_out", i)], L + "ffb_bwd",
                                  hosted=_carried(c))
    ex.done(c)
    ex.add_grad(("ffb_w_in", i), ffn_dw_in(S["h2"].T.astype(BF16), dGb, dUb, L + "d_ffb_in"))
    ex.add_grad(("ffb_w_out", i), ffn_dw_out(actb, dz3, L + "d_ffb_out"))
    dz2, dg1, db1 = ln_bwd(dh2, S["xh2"], S["rs2"], S["ln_g"][1], L + "ln2_bwd")
    docat = mm_nn(dz2, S["wo"], L + "d_ocat", b_transposed=True)
    dwo_att = jnp.concatenate([mm_tn(S["o_sb"], dz2, L + "d_wo_sb"), mm_tn(S["o_mla"], dz2, L + "d_wo_mla")], axis=0)
    dwo_att = dwo_att.reshape(-1, LANES, dwo_att.shape[1])[:, :GDN_D, :].reshape(-1, dwo_att.shape[1])
    ex.add_grad(("mix_w_o", i), jnp.concatenate([mm_tn(S["o_gdn"], dz2, L + "d_wo_gdn"), dwo_att], axis=0))
    c = ex.scatter_with(i, "gdn_chunk_bwd")
    dqkv, dgz, dgab, d_alog, d_dt, d_nw = gdn_chunk_bwd(S["qkv"], S["proj"], S["a_row"], S["dt_row"], S["nw"],
                                                        S["opre"], S["Tm"], S["Sin"], docat, L + "gdn_chunk_bwd",
                                                        hosted=_carried(c))
    ex.done(c)
    c = ex.scatter_with(i, "gdn_pre_bwd")
    dc, dconv = gdn_pre_bwd_pointwise(S["proj"], S["conv"], dqkv, L + "gdn_pre_bwd", hosted=_carried(c))
    ex.done(c)
    dxqkv = gdn_pre_bwd_conv(dc, S["conv"], L + "gdn_conv_bwd")
    ex.add_grad(("gdn_conv_w", i), dconv)
    Grep["gdn_a_log"], Grep["gdn_dt_bias"], Grep["gdn_norm_w"] = d_alog[0, :GDN_HEADS], d_dt[0, :GDN_HEADS], d_nw[0]
    dsq, dsk, dsv = sb_bwd(S["proj"], S["tot"], docat, DO_SB, L + "sb_bwd")
    c = ex.scatter_with(i, "mla_bwd")
    dQ, dK, dV = mla_bwd(S["Q"], S["K"], S["V"], S["o_mla"], S["lse"], docat, DO_MLA, L + "mla_bwd",
                         hosted=_carried(c))
    ex.done(c)
    dmq, dmkv, dwuq, dwuk, dwuv, dwq, dwkv = mla_pre_bwd(
        S["proj"], S["wq"], S["wkv"], S["wuq"], S["wuk"], S["wuv"], *rope, dQ, dK, dV, L + "mla_pre_bwd")
    g_uq, g_ukv = _mla_up_unpad(dwuq, dwuk, dwuv)
    ex.add_grad(("mla_w_uq", i), g_uq)
    ex.add_grad(("mla_w_ukv", i), g_ukv)
    Grep["mla_q_norm_w"], Grep["mla_kv_norm_w"] = dwq[0], dwkv[0]
    dproj = jnp.concatenate([dxqkv, dgz, dsq, dsk, dsv, dmq, dmkv, dgab], axis=1).astype(BF16)
    ex.add_grad(("mix_w_in", i),
                _unpad_in_proj(mm_tn(S["h1"].T.astype(BF16), dproj, L + "d_in_proj", a_transposed=True)))
    dh1 = mm_nn(dproj, S["win"], L + "d_h1", res=dz2, res_scale=ALPHA, b_transposed=True)
    dz1, dg0, db0 = ln_bwd(dh1, S["xh1"], S["rs1"], S["ln_g"][0], L + "ln1_bwd")
    c = ex.scatter_with(i, "ffa_bwd")
    dh0, dGa, dUa, acta = ffn_bwd(dz1, S["Ga"], S["Ua"], W[("ffa_w_in", i)], W[("ffa_w_out", i)], L + "ffa_bwd",
                                  hosted=_carried(c))
    ex.done(c)
    ex.add_grad(("ffa_w_out", i), ffn_dw_out(acta, dz1, L + "d_ffa_out"))
    ex.add_grad(("ln_g", i), jnp.concatenate([dg0, dg1, dg2], axis=0))
    ex.add_grad(("ln_b", i), jnp.concatenate([db0, db1, db2], axis=0))
    c = ex.scatter_with(i, "d_ffa_in")
    ex.add_grad(("ffa_w_in", i), ffn_dw_in(S["h0"].T.astype(BF16), dGa, dUa, L + "d_ffa_in", hosted=_carried(c)))
    ex.done(c)
    return dh0, Grep


def _local_step(x, p, positions, target, ex, rep):
    assert DEPTH == 2
    rope = _rope_tables(positions)
    h, saved = x, []
    for i in range(DEPTH):
        h, S = _layer_fwd(h, p[i], rope, i, ex, rep)
        saved.append(S)
    loss, dh = loss_head(h, target, "loss_head")
    grads = [None] * DEPTH
    for i in reversed(range(DEPTH)):
        dh, grads[i] = _layer_bwd(dh, saved[i], rope, i, ex)
    return loss, dh, {n: jnp.stack([grads[i][n] for i in range(DEPTH)]) for n in REPLICATED}


def kernel(x, p, positions, ffa_w_in, ffa_w_out, mix_w_in, gdn_conv_w, gdn_a_log, gdn_dt_bias, gdn_norm_w, mla_q_norm_w, mla_kv_norm_w, mla_w_uq, mla_w_ukv, mix_w_o, ffb_w_in, ffb_w_out, ln_g, ln_b, ple_w_gate, ple_w_proj, loss_target, m_ffa_w_in, m_ffa_w_out, m_mix_w_in, m_gdn_conv_w, m_gdn_a_log, m_gdn_dt_bias, m_gdn_norm_w, m_mla_q_norm_w, m_mla_kv_norm_w, m_mla_w_uq, m_mla_w_ukv, m_mix_w_o, m_ffb_w_in, m_ffb_w_out, m_ln_g, m_ln_b, m_ple_w_gate, m_ple_w_proj, v_ffa_w_in, v_ffa_w_out, v_mix_w_in, v_gdn_conv_w, v_gdn_a_log, v_gdn_dt_bias, v_gdn_norm_w, v_mla_q_norm_w, v_mla_kv_norm_w, v_mla_w_uq, v_mla_w_ukv, v_mix_w_o, v_ffb_w_in, v_ffb_w_out, v_ln_g, v_ln_b, v_ple_w_gate, v_ple_w_proj):
    given = dict(locals())
    shards = {n: given[n] for n in WEIGHTS}
    ex = Exchanges({n: shards[n] for n in SHARDED})
    ex.gather_now(GATHER_FIRST, "gather_first")
    loss, grad_x, Grep = _local_step(x[0], p[:, 0], positions[0], loss_target[0], ex, {n: shards[n] for n in REPLICATED})
    loss = lax.psum(loss[0, 0], ("x", "y", "c"))
    ex.received.update(zip(SCATTER_LAST, exchange_partials([ex.partial[k] for k in SCATTER_LAST], "scatter_last")))
    rep_received = dict(zip(REPLICATED, all_gather([Grep[n] for n in REPLICATED], "gather_replicated_grads")))
    grad, delta, new_m, new_v = {}, {}, {}, {}
    for n in WEIGHTS:
        shape = shards[n].shape
        parts = [rep_received[n]] if n in REPLICATED else [ex.received[(n, l)] for l in range(DEPTH)]
        if parts[0].shape[1] % 8:
            parts = [jnp.concatenate(parts, axis=1)]
        outs = reduce_adamw(parts, _view2d(shards[n]), _view2d(given["m_" + n]), _view2d(given["v_" + n]),
                            "adamw_" + n)
        grad[n], delta[n], new_m[n], new_v[n] = (t.reshape(shape) for t in outs)
    return (loss, grad_x[None], *[grad[n] for n in WEIGHTS], *[delta[n] for n in WEIGHTS],
            *[new_m[n] for n in WEIGHTS], *[new_v[n] for n in WEIGHTS])
```

```python
import functools
import numpy as np
import jax
import jax.numpy as jnp
from jax import lax
from jax.experimental import pallas as pl
from jax.experimental.pallas import tpu as pltpu

F32 = jnp.float32
BF16 = jnp.bfloat16

DEPTH = 2
LN_EPS = 1e-5
RMS_EPS = 1e-6
ALPHA = (2 * DEPTH) ** 0.25
GDN_HEADS, GDN_D, GDN_CONV, GDN_CHUNK = 8, 64, 4, 64
SB_HEADS, SB_DIM = 4, 64
MLA_HEADS, MLA_NOPE, MLA_ROPE, MLA_V, MLA_Q_RANK, MLA_KV_RANK = 4, 64, 32, 64, 256, 128
ROPE_BASE = 10000.0
HALF_ROPE = MLA_ROPE // 2
LANES = 128
N_DEV = 8
ADAM_LR, ADAM_B1, ADAM_B2, ADAM_EPS, ADAM_WD, ADAM_STEP = 0.001, 0.9, 0.999, 1e-08, 0.01, 10

CB_GQ, CB_GK, CB_GV, CB_GZ = 0, 4, 8, 12
CB_SQ, CB_SK, CB_SV = 16, 20, 24
CB_MQ, CB_MKV, CB_GAB = 28, 30, 32
PROJ_W = 33 * LANES
GDN_W = GDN_HEADS * GDN_D
DO_SB = GDN_W // LANES
DO_MLA = DO_SB + SB_HEADS
VMEM_LIMIT = 56 * 1024 * 1024
MM_TILE = 1536

NT_DIMS = (((1,), (1,)), ((), ()))
TN_DIMS = (((0,), (0,)), ((), ()))


def _cp(sem):
    return pltpu.CompilerParams(dimension_semantics=sem, vmem_limit_bytes=VMEM_LIMIT)


def _bdot(a, b):
    return jnp.dot(a.astype(BF16), b.astype(BF16), preferred_element_type=F32)


def _bdot_nt(a, b):
    return lax.dot_general(a.astype(BF16), b.astype(BF16), NT_DIMS, preferred_element_type=F32)


def _bdot_tn(a, b):
    return lax.dot_general(a.astype(BF16), b.astype(BF16), TN_DIMS, preferred_element_type=F32)


def _split2(a):
    hi = a.astype(BF16)
    lo = (a - hi.astype(F32)).astype(BF16)
    return hi, lo


def _ones_dot_left(ones_bf16, x):
    hi = x.astype(BF16)
    r1 = x - hi.astype(F32)
    mid = r1.astype(BF16)
    lo = (r1 - mid.astype(F32)).astype(BF16)
    d = functools.partial(jnp.dot, preferred_element_type=F32)
    return d(ones_bf16, hi) + d(ones_bf16, mid) + d(ones_bf16, lo)


def _iota2(shape, dim):
    return lax.broadcasted_iota(jnp.int32, shape, dim)


def _sigmoid(x):
    return 0.5 * jnp.tanh(0.5 * x) + 0.5


def _softplus(x):
    return jnp.maximum(x, 0.0) + jnp.log(1.0 + jnp.exp(-jnp.abs(x)))


def _pick(n, limit, mult):
    if n <= limit:
        return n
    best = None
    for t in range(mult, limit + 1, mult):
        if n % t == 0:
            best = t
    assert best is not None, (n, limit, mult)
    return best


MESH = pl.DeviceIdType.MESH
ANY = pl.BlockSpec(memory_space=pl.ANY)


def _place():
    return lax.axis_index("x"), lax.axis_index("y"), lax.axis_index("c")


def _peer(k):
    x, y, c = _place()
    return (1 - x if k & 4 else x, 1 - y if k & 2 else y, 1 - c if k & 1 else c)


class Hosted:
    def __init__(self, kind, arrays):
        self.kind, self.arrays, self.n, self.results = kind, list(arrays), len(arrays), None

    def out_shapes(self):
        if self.kind == "gather":
            return [jax.ShapeDtypeStruct((N_DEV,) + a.shape, a.dtype) for a in self.arrays]
        return [jax.ShapeDtypeStruct(a.shape, a.dtype) for a in self.arrays]

    def sems(self):
        return [pltpu.SemaphoreType.DMA((self.n, N_DEV - 1)), pltpu.SemaphoreType.DMA((self.n, N_DEV - 1)),
                pltpu.SemaphoreType.DMA((self.n,))]

    def _copies(self, src_refs, dst_refs, send_sems, recv_sems, local_sems):
        x, y, c = _place()
        me = 4 * x + 2 * y + c
        local, remote = [], []
        for a in range(self.n):
            gather = self.kind == "gather"
            local.append(pltpu.make_async_copy(src_refs[a] if gather else src_refs[a].at[me], dst_refs[a].at[me],
                                               local_sems.at[a]))
            for k in range(1, N_DEV):
                px, py, pc = _peer(k)
                remote.append(pltpu.make_async_remote_copy(
                    src_ref=src_refs[a] if gather else src_refs[a].at[4 * px + 2 * py + pc], dst_ref=dst_refs[a].at[me],
                    send_sem=send_sems.at[a, k - 1], recv_sem=recv_sems.at[a, k - 1],
                    device_id=(px, py, pc), device_id_type=MESH))
        return local, remote

    def start(self, *refs):
        local, remote = self._copies(*refs)
        for cp in local + remote:
            cp.start()

    def wait(self, *refs):
        local, remote = self._copies(*refs)
        for cp in remote:
            cp.wait_recv()
        for cp in remote:
            cp.wait_send()
        for cp in local:
            cp.wait()


def _hosted_call(hosted, body, *, grid, in_specs, out_specs, out_shape, scratch_shapes=(), compiler_params, name):
    if hosted is None:
        return pl.pallas_call(body, grid=grid, in_specs=in_specs, out_specs=out_specs, out_shape=out_shape,
                              scratch_shapes=scratch_shapes, compiler_params=compiler_params, name=name)
    single = not isinstance(out_shape, (list, tuple))
    o_specs = [out_specs] if single else list(out_specs)
    o_shape = [out_shape] if single else list(out_shape)
    n_in, n_out, n_scr, n = len(in_specs), len(o_specs), len(scratch_shapes), hosted.n

    def wrapped(*refs):
        ins, c_in = refs[:n_in], refs[n_in:n_in + n]
        outs, c_out = refs[n_in + n:n_in + n + n_out], refs[n_in + n + n_out:n_in + 2 * n + n_out]
        rest = refs[n_in + 2 * n + n_out:]
        scr, sems = rest[:n_scr], rest[n_scr:]
        ids = [pl.program_id(ax) for ax in range(len(grid))]
        first = functools.reduce(jnp.logical_and, [i == 0 for i in ids])
        last = functools.reduce(jnp.logical_and, [i == g - 1 for i, g in zip(ids, grid)])

        @pl.when(first)
        def _():
            hosted.start(c_in, c_out, *sems)

        body(*ins, *outs, *scr)

        @pl.when(last)
        def _():
            hosted.wait(c_in, c_out, *sems)

    call = pl.pallas_call(
        wrapped, grid=grid, in_specs=list(in_specs) + [ANY] * n, out_specs=o_specs + [ANY] * n,
        out_shape=o_shape + hosted.out_shapes(), scratch_shapes=list(scratch_shapes) + hosted.sems(),
        compiler_params=_cp(("arbitrary",) * len(grid)), name=name)

    def run(*args):
        outs = call(*args, *hosted.arrays)
        hosted.results = list(outs[n_out:])
        return outs[0] if single else list(outs[:n_out])

    return run


def mm_nn(a, b, name, out_dtype=F32, res=None, res_scale=1.0, b_transposed=False, hosted=None):
    M, K = a.shape
    N = b.shape[0] if b_transposed else b.shape[1]
    tm, tn, tk = _pick(M, 512, 16), _pick(N, MM_TILE, LANES), _pick(K, MM_TILE, LANES)
    nk = K // tk
    has_res = res is not None
    dot = _bdot_nt if b_transposed else _bdot

    def body(*refs):
        if has_res:
            a_ref, b_ref, r_ref, o_ref, acc_ref = refs
        else:
            a_ref, b_ref, o_ref, acc_ref = refs
        k = pl.program_id(2)

        @pl.when(k == 0)
        def _():
            acc_ref[...] = jnp.zeros_like(acc_ref)

        acc_ref[...] += dot(a_ref[...], b_ref[...])

        @pl.when(k == nk - 1)
        def _():
            out = acc_ref[...]
            if has_res:
                out = out + res_scale * r_ref[...]
            o_ref[...] = out.astype(o_ref.dtype)

    b_spec = pl.BlockSpec((tn, tk), lambda i, j, k: (j, k)) if b_transposed else pl.BlockSpec((tk, tn), lambda i, j, k: (k, j))
    in_specs = [pl.BlockSpec((tm, tk), lambda i, j, k: (i, k)), b_spec]
    args = [a, b]
    if has_res:
        in_specs.append(pl.BlockSpec((tm, tn), lambda i, j, k: (i, j)))
        args.append(res)
    return _hosted_call(
        hosted, body, grid=(M // tm, N // tn, nk), in_specs=in_specs,
        out_specs=pl.BlockSpec((tm, tn), lambda i, j, k: (i, j)),
        out_shape=jax.ShapeDtypeStruct((M, N), out_dtype),
        scratch_shapes=[pltpu.VMEM((tm, tn), F32)],
        compiler_params=_cp(("parallel", "parallel", "arbitrary")), name=name)(*args)


def mm_tn(a, b, name, out_dtype=F32, a_transposed=False):
    K, T = a.shape if a_transposed else a.shape[::-1]
    _, N = b.shape
    tk = K if a_transposed else _pick(K, 512, LANES)
    tn, tt = _pick(N, MM_TILE, LANES), _pick(T, 512, LANES)
    nt = T // tt

    def body(a_ref, b_ref, o_ref, acc_ref):
        t = pl.program_id(2)

        @pl.when(t == 0)
        def _():
            acc_ref[...] = jnp.zeros_like(acc_ref)

        if a_transposed:
            acc_ref[...] += _bdot(a_ref[:, pl.ds(pl.multiple_of(t * tt, tt), tt)], b_ref[...])
        else:
            acc_ref[...] += _bdot_tn(a_ref[...], b_ref[...])

        @pl.when(t == nt - 1)
        def _():
            o_ref[...] = acc_ref[...].astype(o_ref.dtype)

    a_spec = pl.BlockSpec((K, T), lambda i, j, t: (0, 0)) if a_transposed else pl.BlockSpec((tt, tk), lambda i, j, t: (t, i))
    return pl.pallas_call(
        body, grid=(K // tk, N // tn, nt),
        in_specs=[a_spec, pl.BlockSpec((tt, tn), lambda i, j, t: (t, j))],
        out_specs=pl.BlockSpec((tk, tn), lambda i, j, t: (i, j)),
        out_shape=jax.ShapeDtypeStruct((K, N), out_dtype),
        scratch_shapes=[pltpu.VMEM((tk, tn), F32)],
        compiler_params=_cp(("parallel", "parallel", "arbitrary")), name=name)(a, b)


def _ln_apply(z, g, b):
    mu = jnp.mean(z, axis=-1, keepdims=True)
    zc = z - mu
    var = jnp.mean(zc * zc, axis=-1, keepdims=True)
    rstd = lax.rsqrt(var + LN_EPS)
    xhat = zc * rstd
    return xhat * g + b, xhat, rstd


def ln_bwd(dout, xhat, rstd, g, name):
    T, D = dout.shape
    tm = _pick(T, 512, 8)

    def body(do_ref, xh_ref, rs_ref, g_ref, dz_ref, dg_ref, db_ref):
        i = pl.program_id(0)

        @pl.when(i == 0)
        def _():
            dg_ref[...] = jnp.zeros_like(dg_ref)
            db_ref[...] = jnp.zeros_like(db_ref)

        do = do_ref[...]
        xh = xh_ref[...]
        dxh = do * g_ref[...]
        m1 = jnp.mean(dxh, axis=-1, keepdims=True)
        m2 = jnp.mean(dxh * xh, axis=-1, keepdims=True)
        dz_ref[...] = rs_ref[...] * (dxh - m1 - xh * m2)
        dg_ref[...] += jnp.sum(do * xh, axis=0, keepdims=True)
        db_ref[...] += jnp.sum(do, axis=0, keepdims=True)

    row = pl.BlockSpec((tm, D), lambda i: (i, 0))
    vec = pl.BlockSpec((1, D), lambda i: (0, 0))
    return pl.pallas_call(
        body, grid=(T // tm,),
        in_specs=[row, row, pl.BlockSpec((tm, 1), lambda i: (i, 0)), vec],
        out_specs=[row, vec, vec],
        out_shape=[jax.ShapeDtypeStruct((T, D), F32), jax.ShapeDtypeStruct((1, D), F32), jax.ShapeDtypeStruct((1, D), F32)],
        compiler_params=_cp(("arbitrary",)), name=name)(dout, xhat, rstd, g)


FFN_CHUNKS = N_DEV // 2


def ffn_fwd(h, w_in, w_out, g, b, name, hosted=None):
    T, D = h.shape
    fc = w_in.shape[2]
    half = w_out.shape[1]
    tm = _pick(T, 512, 8)
    nc = FFN_CHUNKS

    def body(h_ref, wg_ref, wu_ref, wo_ref, g_ref, b_ref, out_ref, xh_ref, rs_ref, G_ref, U_ref, ob_ref, acc_ref):
        c = pl.program_id(1)

        @pl.when(c == 0)
        def _():
            acc_ref[...] = jnp.zeros_like(acc_ref)

        hb = h_ref[...].astype(BF16)
        G = jnp.dot(hb, wg_ref[0], preferred_element_type=F32)
        U = jnp.dot(hb, wu_ref[0], preferred_element_type=F32)
        G_ref[0] = G
        U_ref[0] = U
        act = G * _sigmoid(G) * U
        acc_ref[...] += _bdot(act, wo_ref[...].reshape(2 * half, D))

        @pl.when(c == nc - 1)
        def _():
            z = ALPHA * h_ref[...] + 0.5 * acc_ref[...]
            out, xh, rs = _ln_apply(z, g_ref[...], b_ref[...])
            out_ref[...] = out
            ob_ref[...] = out.astype(BF16)
            xh_ref[...] = xh
            rs_ref[...] = rs

    row = pl.BlockSpec((tm, D), lambda i, c: (i, 0))
    vec = pl.BlockSpec((1, D), lambda i, c: (0, 0))
    cblk = pl.BlockSpec((1, tm, fc), lambda i, c: (c, i, 0))
    csds = jax.ShapeDtypeStruct((nc, T, fc), F32)
    return _hosted_call(
        hosted, body, grid=(T // tm, nc),
        in_specs=[row, pl.BlockSpec((1, D, fc), lambda i, c: (c, 0, 0)),
                  pl.BlockSpec((1, D, fc), lambda i, c: (c + nc, 0, 0)),
                  pl.BlockSpec((2, half, D), lambda i, c: (c, 0, 0)), vec, vec],
        out_specs=[row, row, pl.BlockSpec((tm, 1), lambda i, c: (i, 0)), cblk, cblk, row],
        out_shape=[jax.ShapeDtypeStruct((T, D), F32), jax.ShapeDtypeStruct((T, D), F32), jax.ShapeDtypeStruct((T, 1), F32),
                   csds, csds, jax.ShapeDtypeStruct((T, D), BF16)],
        scratch_shapes=[pltpu.VMEM((tm, D), F32)],
        compiler_params=_cp(("parallel", "arbitrary")), name=name)(h, w_in, w_in, w_out, g, b)


def ffn_bwd(dz, G, U, w_in, w_out, name, hosted=None):
    T, D = dz.shape
    nc, _, fc = G.shape
    half = w_out.shape[1]
    tm = _pick(T, 512, 16)

    def body(dz_ref, G_ref, U_ref, wg_ref, wu_ref, wo_ref, dh_ref, dG_ref, dU_ref, act_ref, acc_ref):
        c = pl.program_id(1)

        @pl.when(c == 0)
        def _():
            acc_ref[...] = jnp.zeros_like(acc_ref)

        dy = (0.5 * dz_ref[...]).astype(BF16)
        dact = _bdot_nt(dy, wo_ref[...].reshape(2 * half, D))
        G = G_ref[0]
        U = U_ref[0]
        s = _sigmoid(G)
        silu = G * s
        dG = (dact * U * (s * (1.0 + G * (1.0 - s)))).astype(BF16)
        dU = (dact * silu).astype(BF16)
        dG_ref[0] = dG
        dU_ref[0] = dU
        act_ref[0] = (silu * U).astype(BF16)
        acc_ref[...] += _bdot_nt(dG, wg_ref[0]) + _bdot_nt(dU, wu_ref[0])

        @pl.when(c == nc - 1)
        def _():
            dh_ref[...] = ALPHA * dz_ref[...] + acc_ref[...]

    row = pl.BlockSpec((tm, D), lambda i, c: (i, 0))
    cblk = pl.BlockSpec((1, tm, fc), lambda i, c: (c, i, 0))
    csds = jax.ShapeDtypeStruct((nc, T, fc), BF16)
    return _hosted_call(
        hosted, body, grid=(T // tm, nc),
        in_specs=[row, cblk, cblk, pl.BlockSpec((1, D, fc), lambda i, c: (c, 0, 0)),
                  pl.BlockSpec((1, D, fc), lambda i, c: (c + nc, 0, 0)),
                  pl.BlockSpec((2, half, D), lambda i, c: (c, 0, 0))],
        out_specs=[row, cblk, cblk, cblk],
        out_shape=[jax.ShapeDtypeStruct((T, D), F32), csds, csds, csds],
        scratch_shapes=[pltpu.VMEM((tm, D), F32)],
        compiler_params=_cp(("parallel", "arbitrary")), name=name)(dz, G, U, w_in, w_in, w_out)


def ffn_dw_in(h_t, dG, dU, name, hosted=None):
    D, T = h_t.shape
    nc, _, fc = dG.shape
    tt = _pick(T, 512, LANES)
    nt = T // tt

    def body(h_ref, dG_ref, dU_ref, o_ref, acc_ref):
        s = pl.program_id(0)
        t = pl.program_id(1)

        @pl.when(t == 0)
        def _():
            acc_ref[...] = jnp.zeros_like(acc_ref)

        hb = h_ref[:, pl.ds(pl.multiple_of(t * tt, tt), tt)]

        @pl.when(s < nc)
        def _():
            acc_ref[...] += jnp.dot(hb, dG_ref[0], preferred_element_type=F32)

        @pl.when(s >= nc)
        def _():
            acc_ref[...] += jnp.dot(hb, dU_ref[0], preferred_element_type=F32)

        @pl.when(t == nt - 1)
        def _():
            o_ref[0] = acc_ref[...].astype(o_ref.dtype)

    return _hosted_call(
        hosted, body, grid=(2 * nc, nt),
        in_specs=[pl.BlockSpec((D, T), lambda s, t: (0, 0)),
                  pl.BlockSpec((1, tt, fc), lambda s, t: (jnp.minimum(s, nc - 1), jnp.where(s < nc, t, nt - 1), 0)),
                  pl.BlockSpec((1, tt, fc), lambda s, t: (jnp.maximum(s - nc, 0), jnp.where(s >= nc, t, 0), 0))],
        out_specs=pl.BlockSpec((1, D, fc), lambda s, t: (s, 0, 0)),
        out_shape=jax.ShapeDtypeStruct((2 * nc, D, fc), BF16),
        scratch_shapes=[pltpu.VMEM((D, fc), F32)],
        compiler_params=_cp(("parallel", "arbitrary")), name=name)(h_t, dG, dU)


def ffn_dw_out(act, dz, name, hosted=None):
    nc, T, fc = act.shape
    D = dz.shape[1]
    half = fc // 2
    tt = _pick(T, 512, 16)
    nt = T // tt

    def body(a_ref, dz_ref, o_ref, acc_ref):
        t = pl.program_id(1)

        @pl.when(t == 0)
        def _():
            acc_ref[...] = jnp.zeros_like(acc_ref)

        acc_ref[...] += _bdot_tn(a_ref[0], dz_ref[...])

        @pl.when(t == nt - 1)
        def _():
            o_ref[...] = (0.5 * acc_ref[...]).reshape(2, half, D).astype(o_ref.dtype)

    return _hosted_call(
        hosted, body, grid=(nc, nt),
        in_specs=[pl.BlockSpec((1, tt, fc), lambda c, t: (c, t, 0)), pl.BlockSpec((tt, D), lambda c, t: (t, 0))],
        out_specs=pl.BlockSpec((2, half, D), lambda c, t: (c, 0, 0)),
        out_shape=jax.ShapeDtypeStruct((2 * nc, half, D), BF16),
        scratch_shapes=[pltpu.VMEM((fc, D), F32)],
        compiler_params=_cp(("parallel", "arbitrary")), name=name)(act, dz)


def proj_res_ln(parts, w, res, g, b, name):
    T, D = res.shape
    tm = _pick(T, 512, 8)
    widths = [p.shape[1] for p in parts]
    offs = [int(sum(widths[:i])) for i in range(len(parts))]
    n = len(parts)

    def body(*refs):
        p_refs = refs[:n]
        w_ref, r_ref, g_ref, b_ref, out_ref, xh_ref, rs_ref, ob_ref = refs[n:]
        acc = ALPHA * r_ref[...]
        for p_ref, o, wd in zip(p_refs, offs, widths):
            acc = acc + _bdot(p_ref[...], w_ref[o:o + wd, :])
        out, xh, rs = _ln_apply(acc, g_ref[...], b_ref[...])
        out_ref[...] = out
        ob_ref[...] = out.astype(BF16)
        xh_ref[...] = xh
        rs_ref[...] = rs

    row = pl.BlockSpec((tm, D), lambda i: (i, 0))
    vec = pl.BlockSpec((1, D), lambda i: (0, 0))
    return pl.pallas_call(
        body, grid=(T // tm,),
        in_specs=[pl.BlockSpec((tm, wd), lambda i: (i, 0)) for wd in widths]
        + [pl.BlockSpec(w.shape, lambda i: (0, 0)), row, vec, vec],
        out_specs=[row, row, pl.BlockSpec((tm, 1), lambda i: (i, 0)), row],
        out_shape=[jax.ShapeDtypeStruct((T, D), F32), jax.ShapeDtypeStruct((T, D), F32), jax.ShapeDtypeStruct((T, 1), F32),
                   jax.ShapeDtypeStruct((T, D), BF16)],
        compiler_params=_cp(("parallel",)), name=name)(*parts, w, res, g, b)


def ple_fwd(h, p, wg, wp, name):
    T, D = h.shape
    P = p.shape[1]
    tm, tn = _pick(T, 512, 8), _pick(D, 512, LANES)

    def body(h_ref, hn_ref, p_ref, wg_ref, wp_ref, out_ref, a_ref, e_ref):
        a = _bdot(h_ref[...], wg_ref[...])
        e = _bdot(p_ref[...], wp_ref[...])
        a_ref[...] = a
        e_ref[...] = e
        out_ref[...] = hn_ref[...] + _sigmoid(a) * e

    blk = pl.BlockSpec((tm, tn), lambda i, j: (i, j))
    sds = jax.ShapeDtypeStruct((T, D), F32)
    return pl.pallas_call(
        body, grid=(T // tm, D // tn),
        in_specs=[pl.BlockSpec((tm, D), lambda i, j: (i, 0)), blk, pl.BlockSpec((tm, P), lambda i, j: (i, 0)),
                  pl.BlockSpec((D, tn), lambda i, j: (0, j)), pl.BlockSpec((P, tn), lambda i, j: (0, j))],
        out_specs=[blk, blk, blk], out_shape=[sds, sds, sds],
        compiler_params=_cp(("parallel", "parallel")), name=name)(h, h, p, wg, wp)


def ple_bwd(dout, a, e, wg, name):
    T, D = dout.shape
    tm = _pick(T, 512, 16)

    def body(do_ref, a_ref, e_ref, wg_ref, dh_ref, da_ref, de_ref):
        do = do_ref[...]
        s = _sigmoid(a_ref[...])
        da = (do * e_ref[...] * s * (1.0 - s)).astype(BF16)
        da_ref[...] = da
        de_ref[...] = (do * s).astype(BF16)
        dh_ref[...] = do + _bdot_nt(da, wg_ref[...])

    row = pl.BlockSpec((tm, D), lambda i: (i, 0))
    return pl.pallas_call(
        body, grid=(T // tm,),
        in_specs=[row, row, row, pl.BlockSpec((D, D), lambda i: (0, 0))],
        out_specs=[row, row, row],
        out_shape=[jax.ShapeDtypeStruct((T, D), F32), jax.ShapeDtypeStruct((T, D), BF16), jax.ShapeDtypeStruct((T, D), BF16)],
        compiler_params=_cp(("parallel",)), name=name)(dout, a, e, wg)


def loss_head(y, target, name):
    T, D = y.shape
    tm = _pick(T, 512, 8)

    def body(y_ref, t_ref, loss_ref, dy_ref):
        i = pl.program_id(0)

        @pl.when(i == 0)
        def _():
            loss_ref[...] = jnp.zeros_like(loss_ref)

        err = y_ref[...] - t_ref[...]
        dy_ref[...] = err * (1.0 / D)
        per_tok = jnp.sum(err * err, axis=-1, keepdims=True) * (1.0 / D)
        loss_ref[...] += 0.5 * jnp.sum(per_tok, axis=0, keepdims=True)

    row = pl.BlockSpec((tm, D), lambda i: (i, 0))
    return pl.pallas_call(
        body, grid=(T // tm,), in_specs=[row, row],
        out_specs=[pl.BlockSpec((1, 1), lambda i: (0, 0)), row],
        out_shape=[jax.ShapeDtypeStruct((1, 1), F32), jax.ShapeDtypeStruct((T, D), F32)],
        compiler_params=_cp(("arbitrary",)), name=name)(y, target)


HALO = 8


def _conv_taps(pad_ref, w_ref, tm, base):
    acc = w_ref[0:1, :] * pad_ref[pl.ds(base, tm), :]
    for k in range(1, GDN_CONV):
        acc = acc + w_ref[k:k + 1, :] * pad_ref[pl.ds(base + k, tm), :]
    return acc


GDN_GROUP_W = GDN_W
GDN_PRE_ROWS = 512


def _head_segments():
    head = jnp.arange(GDN_W, dtype=jnp.int32) // GDN_D
    return (head[:, None] == head[None, :]).astype(BF16)


def _head_sums(x, seg):
    hi = x.astype(BF16)
    r1 = x - hi.astype(F32)
    mid = r1.astype(BF16)
    lo = (r1 - mid.astype(F32)).astype(BF16)
    d = functools.partial(jnp.dot, preferred_element_type=F32)
    return d(hi, seg) + d(mid, seg) + d(lo, seg)


def _gdn_pre_common(x_ref, halo_ref, w_ref, seg_ref, pad_ref, tm):
    i = pl.program_id(1)
    grp = pl.program_id(0)
    pad_ref[0:HALO, :] = jnp.where(i == 0, 0.0, halo_ref[...])
    pad_ref[HALO:HALO + tm, :] = x_ref[...]
    c = _conv_taps(pad_ref, w_ref, tm, HALO - (GDN_CONV - 1))
    s = _sigmoid(c)
    y = c * s
    r = lax.rsqrt(_head_sums(y * y, seg_ref[...]) + RMS_EPS)
    scale = jnp.where(grp < 1, GDN_D ** -0.5, 1.0)
    return grp < 2, c, s, y, r, scale


def gdn_pre_fwd(proj, conv_w, name):
    T = proj.shape[0]
    tm = _pick(T, GDN_PRE_ROWS, 8)
    GW = GDN_GROUP_W

    def body(x_ref, halo_ref, w_ref, seg_ref, o_ref, pad_ref):
        normed, c, s, y, r, scale = _gdn_pre_common(x_ref, halo_ref, w_ref, seg_ref, pad_ref, tm)
        o_ref[...] = jnp.where(normed, y * r * scale, y)

    return pl.pallas_call(
        body, grid=(3, T // tm),
        in_specs=[pl.BlockSpec((tm, GW), lambda hb, i: (i, hb)),
                  pl.BlockSpec((HALO, GW), lambda hb, i: (jnp.maximum(i * (tm // HALO) - 1, 0), hb)),
                  pl.BlockSpec((GDN_CONV, GW), lambda hb, i: (0, hb)), pl.BlockSpec((GW, GW), lambda hb, i: (0, 0))],
        out_specs=pl.BlockSpec((tm, GW), lambda hb, i: (i, hb)),
        out_shape=jax.ShapeDtypeStruct((T, 3 * GW), F32),
        scratch_shapes=[pltpu.VMEM((tm + HALO, GW), F32)],
        compiler_params=_cp(("parallel", "parallel")), name=name)(proj, proj, conv_w, _head_segments())


def gdn_pre_bwd_pointwise(proj, conv_w, dqkv, name, hosted=None):
    T = proj.shape[0]
    tm = _pick(T, GDN_PRE_ROWS, 8)
    GW = GDN_GROUP_W

    def body(x_ref, halo_ref, w_ref, seg_ref, d_ref, dc_ref, dw_ref, pad_ref):
        i = pl.program_id(1)
        normed, c, s, y, r, scale = _gdn_pre_common(x_ref, halo_ref, w_ref, seg_ref, pad_ref, tm)

        @pl.when(i == 0)
        def _():
            dw_ref[...] = jnp.zeros_like(dw_ref)

        d = d_ref[...]
        n = y * r
        dn = d * scale
        dy = jnp.where(normed, r * (dn - n * _head_sums(dn * n, seg_ref[...])), d)
        dc = dy * (s * (1.0 + c * (1.0 - s)))
        dc_ref[...] = dc
        for k in range(GDN_CONV):
            xs = pad_ref[pl.ds(HALO - (GDN_CONV - 1) + k, tm), :]
            dw_ref[k:k + 1, :] += jnp.sum(dc * xs, axis=0, keepdims=True)

    blk = pl.BlockSpec((tm, GW), lambda hb, i: (i, hb))
    wblk = pl.BlockSpec((GDN_CONV, GW), lambda hb, i: (0, hb))
    return _hosted_call(
        hosted, body, grid=(3, T // tm),
        in_specs=[blk, pl.BlockSpec((HALO, GW), lambda hb, i: (jnp.maximum(i * (tm // HALO) - 1, 0), hb)), wblk,
                  pl.BlockSpec((GW, GW), lambda hb, i: (0, 0)), blk],
        out_specs=[blk, wblk],
        out_shape=[jax.ShapeDtypeStruct((T, 3 * GW), F32), jax.ShapeDtypeStruct((GDN_CONV, 3 * GW), F32)],
        scratch_shapes=[pltpu.VMEM((tm + HALO, GW), F32)],
        compiler_params=_cp(("parallel", "arbitrary")), name=name)(proj, proj, conv_w, _head_segments(), dqkv)


def gdn_pre_bwd_conv(dc, conv_w_p, name):
    T = dc.shape[0]
    tm = _pick(T, GDN_PRE_ROWS, 8)
    nt = T // tm
    GW = GDN_GROUP_W

    def body(dc_ref, halo_ref, w_ref, dx_ref, pad_ref):
        i = pl.program_id(1)
        pad_ref[0:tm, :] = dc_ref[...]
        pad_ref[tm:tm + HALO, :] = jnp.where(i == nt - 1, 0.0, halo_ref[...])
        acc = w_ref[GDN_CONV - 1:GDN_CONV, :] * pad_ref[pl.ds(0, tm), :]
        for k in range(GDN_CONV - 1):
            acc = acc + w_ref[k:k + 1, :] * pad_ref[pl.ds(GDN_CONV - 1 - k, tm), :]
        dx_ref[...] = acc

    blk = pl.BlockSpec((tm, GW), lambda hb, i: (i, hb))
    return pl.pallas_call(
        body, grid=(3, nt),
        in_specs=[blk, pl.BlockSpec((HALO, GW), lambda hb, i: (jnp.minimum((i + 1) * (tm // HALO), T // HALO - 1), hb)),
                  pl.BlockSpec((GDN_CONV, GW), lambda hb, i: (0, hb))],
        out_specs=blk,
        out_shape=jax.ShapeDtypeStruct((T, 3 * GW), F32),
        scratch_shapes=[pltpu.VMEM((tm + HALO, GW), F32)],
        compiler_params=_cp(("parallel", "parallel")), name=name)(dc, dc, conv_w_p)


def _chunk_masks(C):
    row = _iota2((C, C), 0)
    col = _iota2((C, C), 1)
    return row >= col, row > col, row == col


BNN = (((2,), (1,)), ((0,), (0,)))
BNT = (((2,), (2,)), ((0,), (0,)))
BTN = (((1,), (1,)), ((0,), (0,)))


def _bmm(a, b, dims=BNN):
    return lax.dot_general(a.astype(BF16), b.astype(BF16), dims, preferred_element_type=F32)


def _hbmm(a, b):
    m = a.shape[1]
    a_hi, a_lo = _split2(a)
    b_hi, b_lo = _split2(b)
    r = lax.dot_general(jnp.concatenate([a_hi, a_lo], axis=1), b_hi, BNN, preferred_element_type=F32)
    return r[:, :m] + r[:, m:] + lax.dot_general(a_hi, b_lo, BNN, preferred_element_type=F32)


def _hbmm_tn(a, b):
    a_hi, a_lo = _split2(a)
    b_hi, b_lo = _split2(b)
    d = functools.partial(lax.dot_general, dimension_numbers=BTN, preferred_element_type=F32)
    return d(a_hi, b_hi) + d(a_lo, b_hi) + d(a_hi, b_lo)


def _col_to_row(colv, eye):
    return jnp.sum(jnp.where(eye, colv, 0.0), axis=1, keepdims=True)


def _row_to_col(rowv, eye):
    return jnp.sum(jnp.where(eye, rowv, 0.0), axis=2, keepdims=True)


def _unit_lower_inverse(A, eye):
    C = A.shape[1]
    P = jnp.where(eye, 1.0, 0.0) - A
    Bp = _hbmm(A, A)
    for _ in range(4):
        R = _hbmm(jnp.concatenate([Bp, P], axis=1), Bp)
        Bp = R[:, :C]
        P = P + R[:, C:]
    return P + _hbmm(P, Bp)


def _stack_heads(ref, first_head, n):
    return jnp.stack([ref[:, pl.ds((first_head + h) * GDN_D, GDN_D)] for h in range(n)])


def _unstack_heads(ref, first_head, val):
    for h in range(val.shape[0]):
        ref[:, pl.ds((first_head + h) * GDN_D, GDN_D)] = val[h]


def _gdn_gates(gab, a_row, dt_row, incl):
    g_all = -jnp.exp(a_row) * _softplus(gab + dt_row)
    beta_all = _sigmoid(gab)
    gc_all = _ones_dot_left(incl.astype(BF16), g_all)
    return g_all, beta_all, gc_all


def _gdn_common(qkv_ref, gc_all, beta_all, incl, strict, eye):
    C, H = GDN_CHUNK, GDN_HEADS
    q, k, v = (_stack_heads(qkv_ref, j * H, H) for j in range(3))
    gc = jnp.stack([gc_all[:, h:h + 1] for h in range(H)])
    beta = jnp.stack([beta_all[:, H + h:H + h + 1] for h in range(H)])
    gc_row = _col_to_row(gc, eye)
    decay = jnp.where(incl, jnp.exp(jnp.where(incl, gc - gc_row, 0.0)), 0.0)
    e_gc = jnp.exp(gc)
    gl = gc[:, C - 1:C, :]
    e_gl = jnp.exp(gl)
    ekd = jnp.exp(gl - gc)
    kb = k * beta
    A = jnp.where(strict, _bmm(kb, k, BNT) * decay, 0.0)
    Pm = jnp.where(incl, _bmm(q, k, BNT) * decay, 0.0)
    return q, k, v, gc, beta, decay, e_gc, e_gl, ekd, kb, A, Pm


def gdn_chunk_fwd(qkv, proj, a_row, dt_row, norm_w, name, hosted=None):
    T = qkv.shape[0]
    C, H, Dh = GDN_CHUNK, GDN_HEADS, GDN_D
    N = T // C

    def body(qkv_ref, gz_ref, gab_ref, a_ref, dt_ref, nw_ref, o_ref, opre_ref, Tm_ref, Sin_ref, S_ref):
        n = pl.program_id(0)

        @pl.when(n == 0)
        def _():
            S_ref[...] = jnp.zeros_like(S_ref)

        incl, strict, eye = _chunk_masks(C)
        _, beta_all, gc_all = _gdn_gates(gab_ref[...], a_ref[...], dt_ref[...], incl)
        q, k, v, gc, beta, decay, e_gc, e_gl, ekd, kb, A, Pm = _gdn_common(qkv_ref, gc_all, beta_all, incl, strict, eye)
        Tm = _unit_lower_inverse(A, eye)
        u = _hbmm(Tm, v * beta)
        w = _hbmm(Tm, kb * e_gc)
        S = S_ref[...]
        v_new = u - _bmm(w, S)
        o = _bmm(q * e_gc, S) + _bmm(Pm, v_new)
        S_ref[...] = S * e_gl + _bmm(k * ekd, v_new, BTN)
        Sin_ref[0] = S
        Tm_ref[0] = Tm
        r = lax.rsqrt(jnp.mean(o * o, axis=-1, keepdims=True) + RMS_EPS)
        gz = _stack_heads(gz_ref, 0, H)
        _unstack_heads(opre_ref, 0, o)
        _unstack_heads(o_ref, 0, o * r * nw_ref[...] * (gz * _sigmoid(gz)))

    vec = pl.BlockSpec((1, LANES), lambda n: (0, 0))
    hblk = pl.BlockSpec((C, GDN_W), lambda n: (n, 0))
    sblk = pl.BlockSpec((1, H, Dh, Dh), lambda n: (n, 0, 0, 0))
    return _hosted_call(
        hosted, body, grid=(N,),
        in_specs=[pl.BlockSpec((C, 3 * GDN_W), lambda n: (n, 0)),
                  pl.BlockSpec((C, GDN_W), lambda n: (n, CB_GZ * LANES // GDN_W)),
                  pl.BlockSpec((C, LANES), lambda n: (n, CB_GAB)), vec, vec, pl.BlockSpec((1, Dh), lambda n: (0, 0))],
        out_specs=[hblk, hblk, sblk, sblk],
        out_shape=[jax.ShapeDtypeStruct((T, GDN_W), F32), jax.ShapeDtypeStruct((T, GDN_W), F32),
                   jax.ShapeDtypeStruct((N, H, Dh, Dh), F32), jax.ShapeDtypeStruct((N, H, Dh, Dh), F32)],
        scratch_shapes=[pltpu.VMEM((H, Dh, Dh), F32)],
        compiler_params=_cp(("arbitrary",)), name=name)(qkv, proj, proj, a_row, dt_row, norm_w)


def gdn_chunk_bwd(qkv, proj, a_row, dt_row, norm_w, opre, Tm_all, Sin_all, docat, name, hosted=None):
    T = qkv.shape[0]
    C, H, Dh = GDN_CHUNK, GDN_HEADS, GDN_D
    N = T // C

    def body(qkv_ref, gz_ref, gab_ref, a_ref, dt_ref, nw_ref, opre_ref, Tm_ref, Sin_ref, do_ref,
             dqkv_ref, dgz_ref, dgab_ref, da_ref, ddt_ref, dnw_ref, dS_ref):
        n = pl.program_id(0)

        @pl.when(n == 0)
        def _():
            dS_ref[...] = jnp.zeros_like(dS_ref)
            da_ref[...] = jnp.zeros_like(da_ref)
            ddt_ref[...] = jnp.zeros_like(ddt_ref)
            dnw_ref[...] = jnp.zeros_like(dnw_ref)

        incl, strict, eye = _chunk_masks(C)
        gab = gab_ref[...]
        g_all, beta_all, gc_all = _gdn_gates(gab, a_ref[...], dt_ref[...], incl)
        lane = _iota2((C, LANES), 1)
        rowi = _iota2((C, 1), 0)
        nw = nw_ref[...]
        q, k, v, gc, beta, decay, e_gc, e_gl, ekd, kb, A, Pm = _gdn_common(qkv_ref, gc_all, beta_all, incl, strict, eye)
        Tm = Tm_ref[0]
        S = Sin_ref[0]
        dS = dS_ref[...]
        kbe = kb * e_gc
        u = _hbmm(Tm, v * beta)
        w = _hbmm(Tm, kbe)
        qd = q * e_gc
        kd = k * ekd
        v_new = u - _bmm(w, S)
        o = _stack_heads(opre_ref, 0, H)
        gz = _stack_heads(gz_ref, 0, H)
        don = _stack_heads(do_ref, 0, H)
        r = lax.rsqrt(jnp.mean(o * o, axis=-1, keepdims=True) + RMS_EPS)
        nn = o * r
        sgz = _sigmoid(gz)
        silu = gz * sgz
        _unstack_heads(dgz_ref, 0, don * nn * nw * (sgz * (1.0 + gz * (1.0 - sgz))))
        dnn = don * nw * silu
        dnw_ref[...] += jnp.sum(jnp.sum(don * nn * silu, axis=0), axis=0, keepdims=True)
        do = r * (dnn - nn * jnp.mean(dnn * nn, axis=-1, keepdims=True))
        dv_new = _bmm(Pm, do, BTN) + _bmm(kd, dS)
        dPm = jnp.where(incl, _bmm(do, v_new, BNT), 0.0)
        dqd = _bmm(do, S, BNT)
        dkd = _bmm(v_new, dS, BNT)
        dS_ref[...] = _bmm(qd, do, BTN) + e_gl * dS - _bmm(w, dv_new, BTN)
        dgl = jnp.sum(jnp.sum(dS * S, axis=2, keepdims=True), axis=1, keepdims=True) * e_gl
        dw = -_bmm(dv_new, S, BNT)
        dvb = _hbmm_tn(Tm, dv_new)
        dkbe = _hbmm_tn(Tm, dw)
        dA = -jnp.where(strict, _bmm(dvb, u, BNT) + _bmm(dkbe, w, BNT), 0.0)
        dAD = dA * decay
        dPD = dPm * decay
        Gm = dA * A + dPm * Pm
        dgc = jnp.sum(Gm, axis=2, keepdims=True) - _row_to_col(jnp.sum(Gm, axis=1, keepdims=True), eye)
        dkb = _bmm(dAD, k) + dkbe * e_gc
        dk = _bmm(dAD, kb, BTN) + _bmm(dPD, q, BTN) + dkd * ekd + dkb * beta
        dq = _bmm(dPD, k) + dqd * e_gc
        tkd = jnp.sum(dkd * kd, axis=-1, keepdims=True)
        dgc = dgc + jnp.sum(dqd * qd, axis=-1, keepdims=True) - tkd + jnp.sum(dkbe * kbe, axis=-1, keepdims=True)
        dgl = dgl + jnp.sum(tkd, axis=1, keepdims=True)
        dgc = dgc + jnp.where(rowi == C - 1, dgl, 0.0)
        dbeta = jnp.sum(dvb * v, axis=-1, keepdims=True) + jnp.sum(dkb * k, axis=-1, keepdims=True)
        _unstack_heads(dqkv_ref, 0, dq)
        _unstack_heads(dqkv_ref, H, dk)
        _unstack_heads(dqkv_ref, 2 * H, dvb * beta)
        dgc_all = jnp.zeros((C, LANES), F32)
        dbeta_all = jnp.zeros((C, LANES), F32)
        for h in range(H):
            dgc_all = dgc_all + jnp.where(lane == h, dgc[h], 0.0)
            dbeta_all = dbeta_all + jnp.where(lane == H + h, dbeta[h], 0.0)
        upper = (_iota2((C, C), 0) <= _iota2((C, C), 1)).astype(BF16)
        dg_all = _ones_dot_left(upper, dgc_all)
        dga = dg_all * (-jnp.exp(a_ref[...])) * _sigmoid(gab + dt_ref[...])
        dgb = dbeta_all * beta_all * (1.0 - beta_all)
        dgab_ref[...] = jnp.where(lane < H, dga, jnp.where(lane < 2 * H, dgb, 0.0))
        da_ref[...] += jnp.sum(jnp.where(lane < H, dg_all * g_all, 0.0), axis=0, keepdims=True)
        ddt_ref[...] += jnp.sum(jnp.where(lane < H, dga, 0.0), axis=0, keepdims=True)

    rev = lambda n: N - 1 - n
    vec = pl.BlockSpec((1, LANES), lambda n: (0, 0))
    nwv = pl.BlockSpec((1, Dh), lambda n: (0, 0))
    hblk = pl.BlockSpec((C, GDN_W), lambda n: (rev(n), 0))
    sblk = pl.BlockSpec((1, H, Dh, Dh), lambda n: (rev(n), 0, 0, 0))
    qblk = pl.BlockSpec((C, 3 * GDN_W), lambda n: (rev(n), 0))
    return _hosted_call(
        hosted, body, grid=(N,),
        in_specs=[qblk, pl.BlockSpec((C, GDN_W), lambda n: (rev(n), CB_GZ * LANES // GDN_W)),
                  pl.BlockSpec((C, LANES), lambda n: (rev(n), CB_GAB)), vec, vec, nwv, hblk, sblk, sblk, hblk],
        out_specs=[qblk, hblk, pl.BlockSpec((C, LANES), lambda n: (rev(n), 0)), vec, vec, nwv],
        out_shape=[jax.ShapeDtypeStruct((T, 3 * GDN_W), F32), jax.ShapeDtypeStruct((T, GDN_W), F32),
                   jax.ShapeDtypeStruct((T, LANES), F32), jax.ShapeDtypeStruct((1, LANES), F32),
                   jax.ShapeDtypeStruct((1, LANES), F32), jax.ShapeDtypeStruct((1, Dh), F32)],
        scratch_shapes=[pltpu.VMEM((H, Dh, Dh), F32)],
        compiler_params=_cp(("arbitrary",)), name=name)(qkv, proj, proj, a_row, dt_row, norm_w, opre, Tm_all, Sin_all, docat)


ATT_BQ, ATT_BK = 256, 512
NEG_BIG = -1e30


def _att_blocks(T):
    bq, bk = min(ATT_BQ, T), min(ATT_BK, T)
    assert bk % bq == 0 and T % bk == 0
    return bq, bk


def _att_specs(T, bq, cbs):
    qspec = lambda cb: pl.BlockSpec((bq, LANES), lambda h, i: (i, cb + h))
    kspec = lambda cb: pl.BlockSpec((T, LANES), lambda h, i: (0, cb + h))
    return qspec, kspec


def _kblock(ref, kb, bk):
    return ref[pl.ds(pl.multiple_of(kb * bk, bk), bk), :]


def _att_pos(i, kb, bq, bk):
    qpos = i * bq + _iota2((bq, bk), 0)
    kpos = kb * bk + _iota2((bq, bk), 1)
    return qpos, kpos


def _suffix_sum(x):
    n = x.shape[1]
    lane = _iota2(x.shape, 1)
    d = 1
    while d < n:
        x = x + jnp.where(lane < n - d, pltpu.roll(x, n - d, 1), 0.0)
        d *= 2
    return x


def _prefix_sum(x):
    n = x.shape[1]
    lane = _iota2(x.shape, 1)
    d = 1
    while d < n:
        x = x + jnp.where(lane >= d, pltpu.roll(x, d, 1), 0.0)
        d *= 2
    return x


SB_BLOCK = 256
SB_DEAD = -104.0


def _sb_blocks(T):
    b = min(SB_BLOCK, T)
    assert T % b == 0 and T // b <= LANES
    return b, b


def sb_fwd(proj, name, hosted=None):
    T = proj.shape[0]
    H = SB_HEADS
    bq, bk = _sb_blocks(T)
    scale = SB_DIM ** -0.5

    def body(q_ref, k_ref, v_ref, o_ref, tot_ref):
        i = pl.program_id(1)
        qb = q_ref[...].astype(BF16)
        diag = (i * bq) // bk
        lane = _iota2((bq, LANES), 1)

        def block(kb, acc, R, masked):
            z = _bdot_nt(qb, _kblock(k_ref, kb, bk)) * scale
            sp = _softplus(z)
            if masked:
                qpos, kpos = _att_pos(i, kb, bq, bk)
                mask = kpos < qpos
                l1m = jnp.where(mask, -sp, 0.0)
            else:
                l1m = -sp
            W = jnp.exp((z - sp) + (_suffix_sum(l1m) - l1m) + R)
            if masked:
                W = jnp.where(mask, W, 0.0)
            acc = acc + _bdot(W, _kblock(v_ref, kb, bk))
            return acc, R + jnp.sum(l1m, axis=-1, keepdims=True)

        acc, R = block(diag, jnp.zeros((bq, LANES), F32), jnp.zeros((bq, 1), F32), True)

        def live(c):
            return jnp.logical_and(c[0] >= 0, jnp.max(c[2]) > SB_DEAD)

        def step(c):
            kb, acc, R, Rb = c
            acc, R_next = block(kb, acc, R, False)
            return kb - 1, acc, R_next, jnp.where(lane == kb, R, Rb)

        _, acc, _, Rb = lax.while_loop(live, step, (diag - 1, acc, R, jnp.where(lane == diag, 0.0, NEG_BIG)))
        o_ref[...] = acc
        tot_ref[...] = Rb

    qspec, kspec = _att_specs(T, bq, None)
    sds = jax.ShapeDtypeStruct((T, H * LANES), F32)
    oblk = pl.BlockSpec((bq, LANES), lambda h, i: (i, h))
    return _hosted_call(
        hosted, body, grid=(H, T // bq), in_specs=[qspec(CB_SQ), kspec(CB_SK), kspec(CB_SV)],
        out_specs=[oblk, oblk], out_shape=[sds, sds],
        compiler_params=_cp(("parallel", "parallel")), name=name)(proj, proj, proj)


def sb_bwd(proj, tot, docat, do_cb, name):
    T = proj.shape[0]
    H = SB_HEADS
    bq, bk = _sb_blocks(T)
    scale = SB_DIM ** -0.5

    def body(q_ref, k_ref, v_ref, tot_ref, do_ref, dq_ref, dk_ref, dv_ref):
        i = pl.program_id(1)

        @pl.when(i == 0)
        def _():
            dk_ref[...] = jnp.zeros_like(dk_ref)
            dv_ref[...] = jnp.zeros_like(dv_ref)

        qb = q_ref[...].astype(BF16)
        dob = do_ref[...].astype(BF16)
        Rb = tot_ref[...]
        diag = (i * bq) // bk
        lane = _iota2((bq, LANES), 1)
        first = lax.while_loop(
            lambda kb: jnp.logical_and(kb < diag, jnp.max(jnp.where(lane == kb, Rb, NEG_BIG)) <= SB_DEAD),
            lambda kb: kb + 1, jnp.int32(0))

        def block(kb, carry, masked):
            dq, Epre = carry
            R = jnp.sum(jnp.where(lane == kb, Rb, 0.0), axis=1, keepdims=True)
            kblk = _kblock(k_ref, kb, bk).astype(BF16)
            z = _bdot_nt(qb, kblk) * scale
            sp = _softplus(z)
            if masked:
                qpos, kpos = _att_pos(i, kb, bq, bk)
                mask = kpos < qpos
                l1m = jnp.where(mask, -sp, 0.0)
            else:
                l1m = -sp
            W = jnp.exp((z - sp) + (_suffix_sum(l1m) - l1m) + R)
            if masked:
                W = jnp.where(mask, W, 0.0)
            E = _bdot_nt(dob, _kblock(v_ref, kb, bk)) * W
            cexcl = (_prefix_sum(E) - E) + Epre
            neg = jnp.exp(-sp)
            dz = E * neg - cexcl * (1.0 - neg)
            if masked:
                dz = jnp.where(mask, dz, 0.0)
            dz = (dz * scale).astype(BF16)
            rows = pl.ds(pl.multiple_of(kb * bk, bk), bk)
            dk_ref[rows, :] += lax.dot_general(dz, qb, TN_DIMS, preferred_element_type=F32)
            dv_ref[rows, :] += lax.dot_general(W.astype(BF16), dob, TN_DIMS, preferred_element_type=F32)
            dq = dq + jnp.dot(dz, kblk, preferred_element_type=F32)
            return dq, Epre + jnp.sum(E, axis=-1, keepdims=True)

        init = (jnp.zeros((bq, LANES), F32), jnp.zeros((bq, 1), F32))
        carry = lax.fori_loop(first, diag, lambda kb, c: block(kb, c, False), init)
        dq, _ = block(diag, carry, True)
        dq_ref[...] = dq

    qspec, kspec = _att_specs(T, bq, None)
    sds = jax.ShapeDtypeStruct((T, H * LANES), F32)
    oblk = pl.BlockSpec((bq, LANES), lambda h, i: (i, h))
    kout = pl.BlockSpec((T, LANES), lambda h, i: (0, h))
    return pl.pallas_call(
        body, grid=(H, T // bq),
        in_specs=[qspec(CB_SQ), kspec(CB_SK), kspec(CB_SV), oblk, qspec(do_cb)],
        out_specs=[oblk, kout, kout], out_shape=[sds, sds, sds],
        compiler_params=_cp(("arbitrary", "arbitrary")), name=name)(proj, proj, proj, tot, docat)


def mla_fwd(Q, K, V, name, hosted=None):
    T = Q.shape[0]
    H = MLA_HEADS
    bq, bk = _att_blocks(T)
    scale = (MLA_NOPE + MLA_ROPE) ** -0.5

    def body(q_ref, k_ref, v_ref, o_ref, lse_ref):
        i = pl.program_id(1)
        qb = q_ref[...]
        diag = (i * bq) // bk

        def block(kb, carry, masked):
            acc, m, l = carry
            s = _bdot_nt(qb, _kblock(k_ref, kb, bk)) * scale
            if masked:
                qpos, kpos = _att_pos(i, kb, bq, bk)
                s = jnp.where(kpos <= qpos, s, NEG_BIG)
            m_new = jnp.maximum(m, jnp.max(s, axis=-1, keepdims=True))
            p = jnp.exp(s - m_new)
            corr = jnp.exp(m - m_new)
            acc = corr * acc + _bdot(p, _kblock(v_ref, kb, bk))
            return acc, m_new, corr * l + jnp.sum(p, axis=-1, keepdims=True)

        init = (jnp.zeros((bq, LANES), F32), jnp.full((bq, 1), NEG_BIG, F32), jnp.zeros((bq, 1), F32))
        carry = lax.fori_loop(0, diag, lambda kb, c: block(kb, c, False), init)
        acc, m, l = block(diag, carry, True)
        o_ref[...] = acc / l
        lse_ref[...] = jnp.broadcast_to(m + jnp.log(l), (bq, LANES))

    qspec, kspec = _att_specs(T, bq, None)
    sds = jax.ShapeDtypeStruct((T, H * LANES), F32)
    oblk = pl.BlockSpec((bq, LANES), lambda h, i: (i, h))
    return _hosted_call(
        hosted, body, grid=(H, T // bq), in_specs=[qspec(0), kspec(0), kspec(0)],
        out_specs=[oblk, oblk], out_shape=[sds, sds],
        compiler_params=_cp(("parallel", "parallel")), name=name)(Q, K, V)


def mla_bwd(Q, K, V, o, lse, docat, do_cb, name, hosted=None):
    T = Q.shape[0]
    H = MLA_HEADS
    bq, bk = _att_blocks(T)
    scale = (MLA_NOPE + MLA_ROPE) ** -0.5

    def body(q_ref, k_ref, v_ref, o_ref, lse_ref, do_ref, dq_ref, dk_ref, dv_ref):
        i = pl.program_id(1)

        @pl.when(i == 0)
        def _():
            dk_ref[...] = jnp.zeros_like(dk_ref)
            dv_ref[...] = jnp.zeros_like(dv_ref)

        qb = q_ref[...]
        do = do_ref[...]
        dob = do.astype(BF16)
        delta = jnp.sum(do * o_ref[...], axis=-1, keepdims=True)
        lse = lse_ref[:, 0:1]

        diag = (i * bq) // bk

        def block(kb, dq, masked):
            kblk = _kblock(k_ref, kb, bk)
            s = _bdot_nt(qb, kblk) * scale
            if masked:
                qpos, kpos = _att_pos(i, kb, bq, bk)
                s = jnp.where(kpos <= qpos, s, NEG_BIG)
            p = jnp.exp(s - lse)
            dp = _bdot_nt(dob, _kblock(v_ref, kb, bk))
            ds = (p * (dp - delta) * scale).astype(BF16)
            rows = pl.ds(pl.multiple_of(kb * bk, bk), bk)
            dk_ref[rows, :] += lax.dot_general(ds, qb, TN_DIMS, preferred_element_type=F32)
            dv_ref[rows, :] += lax.dot_general(p.astype(BF16), dob, TN_DIMS, preferred_element_type=F32)
            return dq + jnp.dot(ds, kblk, preferred_element_type=F32)

        dq = lax.fori_loop(0, diag, lambda kb, c: block(kb, c, False), jnp.zeros((bq, LANES), F32))
        dq_ref[...] = block(diag, dq, True)

    qspec, kspec = _att_specs(T, bq, None)
    sds = jax.ShapeDtypeStruct((T, H * LANES), F32)
    oblk = pl.BlockSpec((bq, LANES), lambda h, i: (i, h))
    kout = pl.BlockSpec((T, LANES), lambda h, i: (0, h))
    return _hosted_call(
        hosted, body, grid=(H, T // bq),
        in_specs=[qspec(0), kspec(0), kspec(0), oblk, oblk, qspec(do_cb)],
        out_specs=[oblk, kout, kout], out_shape=[sds, sds, sds],
        compiler_params=_cp(("arbitrary", "arbitrary")), name=name)(Q, K, V, o, lse, docat)


def _tile_heads(t, n):
    return jnp.concatenate([t] * n, axis=1)


def _rope(X, C, Sn, Sp):
    n = X.shape[1]
    return X * C + pltpu.roll(X, n - HALF_ROPE, 1) * Sn + pltpu.roll(X, HALF_ROPE, 1) * Sp


def _rope_t(dO, C, Sn, Sp):
    n = dO.shape[1]
    return dO * C + pltpu.roll(dO * Sn, HALF_ROPE, 1) + pltpu.roll(dO * Sp, n - HALF_ROPE, 1)


def _rms(x, w):
    r = lax.rsqrt(jnp.mean(x * x, axis=-1, keepdims=True) + RMS_EPS)
    xh = x * r
    return r, xh, xh * w


def _rms_bwd(dn, w, r, xh):
    dxh = dn * w
    return r * (dxh - xh * jnp.mean(dxh * xh, axis=-1, keepdims=True)), jnp.sum(dn * xh, axis=0, keepdims=True)


def _mla_pre_specs(T, tm):
    KV = MLA_KV_RANK
    QR = MLA_Q_RANK
    W = MLA_HEADS * LANES
    full = lambda shape: pl.BlockSpec(shape, lambda i: (0, 0))
    specs = [pl.BlockSpec((tm, QR), lambda i: (i, CB_MQ * LANES // QR)),
             pl.BlockSpec((tm, 2 * LANES), lambda i: (i, CB_MKV // 2)),
             full((1, QR)), full((1, KV))]
    rope = [pl.BlockSpec((tm, LANES), lambda i: (i, 0))] * 3
    return specs, rope, full, W


def mla_pre_fwd(proj, wq, wkv, wuq, wuk, wuv, ropeC, ropeSn, ropeSp, name):
    T = proj.shape[0]
    tm = _pick(T, 512, 16)
    KV = MLA_KV_RANK
    H = MLA_HEADS

    def body(mq_ref, mkv_ref, wq_ref, wkv_ref, wuq_ref, wuk_ref, wuv_ref, c_ref, sn_ref, sp_ref, Q_ref, K_ref, V_ref):
        C, Sn, Sp = (_tile_heads(t[...], H) for t in (c_ref, sn_ref, sp_ref))
        _, _, qn = _rms(mq_ref[...], wq_ref[...])
        Q_ref[...] = _rope(_bdot(qn, wuq_ref[...]), C, Sn, Sp).astype(BF16)
        mkv = mkv_ref[...]
        _, _, kvn = _rms(mkv[:, :KV], wkv_ref[...])
        kr = pltpu.roll(mkv[:, KV:], MLA_NOPE, 1)
        K_ref[...] = _rope(_bdot(kvn, wuk_ref[...]) + _tile_heads(kr, H), C, Sn, Sp).astype(BF16)
        V_ref[...] = _bdot(kvn, wuv_ref[...]).astype(BF16)

    specs, rope, full, W = _mla_pre_specs(T, tm)
    oblk = pl.BlockSpec((tm, W), lambda i: (i, 0))
    sds = jax.ShapeDtypeStruct((T, W), BF16)
    return pl.pallas_call(
        body, grid=(T // tm,),
        in_specs=specs + [full(wuq.shape), full(wuk.shape), full(wuv.shape)] + rope,
        out_specs=[oblk, oblk, oblk], out_shape=[sds, sds, sds],
        compiler_params=_cp(("parallel",)), name=name)(proj, proj, wq, wkv, wuq, wuk, wuv, ropeC, ropeSn, ropeSp)


def mla_pre_bwd(proj, wq, wkv, wuq, wuk, wuv, ropeC, ropeSn, ropeSp, dQ, dK, dV, name):
    T = proj.shape[0]
    tm = _pick(T, 512, 16)
    KV = MLA_KV_RANK
    H = MLA_HEADS

    def body(mq_ref, mkv_ref, wq_ref, wkv_ref, wuq_ref, wuk_ref, wuv_ref,
             c_ref, sn_ref, sp_ref, dQ_ref, dK_ref, dV_ref,
             dmq_ref, dmkv_ref, dwuq_ref, dwuk_ref, dwuv_ref, dwq_ref, dwkv_ref):
        i = pl.program_id(0)

        @pl.when(i == 0)
        def _():
            for ref in (dwuq_ref, dwuk_ref, dwuv_ref, dwq_ref, dwkv_ref):
                ref[...] = jnp.zeros_like(ref)

        C, Sn, Sp = (_tile_heads(t[...], H) for t in (c_ref, sn_ref, sp_ref))
        rq, xq, qn = _rms(mq_ref[...], wq_ref[...])
        mkv = mkv_ref[...]
        rkv, xkv, kvn = _rms(mkv[:, :KV], wkv_ref[...])
        dqf = _rope_t(dQ_ref[...], C, Sn, Sp)
        dkf = _rope_t(dK_ref[...], C, Sn, Sp)
        dv = dV_ref[...]
        dwuq_ref[...] += _bdot_tn(qn, dqf)
        dwuk_ref[...] += _bdot_tn(kvn, dkf)
        dwuv_ref[...] += _bdot_tn(kvn, dv)
        dmq, dwq = _rms_bwd(_bdot_nt(dqf, wuq_ref[...]), wq_ref[...], rq, xq)
        dckv, dwkv = _rms_bwd(_bdot_nt(dkf, wuk_ref[...]) + _bdot_nt(dv, wuv_ref[...]), wkv_ref[...], rkv, xkv)
        dwq_ref[...] += dwq
        dwkv_ref[...] += dwkv
        dmq_ref[...] = dmq
        dkr = dkf[:, 0:LANES]
        for h in range(1, H):
            dkr = dkr + dkf[:, h * LANES:(h + 1) * LANES]
        dkr = pltpu.roll(dkr, LANES - MLA_NOPE, 1)
        dkr = jnp.where(_iota2(dkr.shape, 1) < MLA_ROPE, dkr, 0.0)
        dmkv_ref[...] = jnp.concatenate([dckv, dkr], axis=1)

    specs, rope, full, W = _mla_pre_specs(T, tm)
    wide = pl.BlockSpec((tm, W), lambda i: (i, 0))
    return pl.pallas_call(
        body, grid=(T // tm,),
        in_specs=specs + [full(w.shape) for w in (wuq, wuk, wuv)] + rope + [wide, wide, wide],
        out_specs=[pl.BlockSpec((tm, MLA_Q_RANK), lambda i: (i, 0)), pl.BlockSpec((tm, 2 * LANES), lambda i: (i, 0)),
                   full(wuq.shape), full(wuk.shape), full(wuv.shape), full((1, MLA_Q_RANK)), full((1, KV))],
        out_shape=[jax.ShapeDtypeStruct((T, MLA_Q_RANK), F32), jax.ShapeDtypeStruct((T, 2 * LANES), F32),
                   jax.ShapeDtypeStruct(wuq.shape, F32), jax.ShapeDtypeStruct(wuk.shape, F32),
                   jax.ShapeDtypeStruct(wuv.shape, F32), jax.ShapeDtypeStruct((1, MLA_Q_RANK), F32),
                   jax.ShapeDtypeStruct((1, KV), F32)],
        compiler_params=_cp(("arbitrary",)), name=name)(
            proj, proj, wq, wkv, wuq, wuk, wuv, ropeC, ropeSn, ropeSp, dQ, dK, dV)


def all_gather(shards, name):
    n = len(shards)

    def body(*refs):
        x_refs, out_refs = refs[:n], refs[n:2 * n]
        send_sems, recv_sems, local_sems = refs[2 * n:]
        x, y, c = _place()
        me, sibling = (x, y, c), (x, y, 1 - c)
        chips = [(1 - x, y), (x, 1 - y), (1 - x, 1 - y)]

        def slot(a, px, py, pc):
            return out_refs[a].at[4 * px + 2 * py + pc]

        def copy(a, k, block, to, src=None):
            return pltpu.make_async_remote_copy(
                src_ref=slot(a, *block) if src is None else src, dst_ref=slot(a, *block),
                send_sem=send_sems.at[a, k], recv_sem=recv_sems.at[a, k], device_id=to, device_id_type=MESH)

        mine = [pltpu.make_async_copy(x_refs[a], slot(a, *me), local_sems.at[a]) for a in range(n)]
        first = []
        for a in range(n):
            mine[a].start()
            first.append(copy(a, 0, me, sibling, src=x_refs[a]))
            first += [copy(a, 1 + j, me, (*chip, c), src=x_refs[a]) for j, chip in enumerate(chips)]
        for cp in first:
            cp.start()
        passed = []
        for j, chip in enumerate(chips):
            for a in range(n):
                copy(a, 1 + j, (*chip, c), me).wait_recv()
                passed.append(copy(a, 4 + j, (*chip, c), sibling))
                passed[-1].start()
        for a in range(n):
            copy(a, 0, sibling, me).wait_recv()
            for j, chip in enumerate(chips):
                copy(a, 4 + j, (*chip, 1 - c), me).wait_recv()
        for cp in first + passed:
            cp.wait_send()
        for cp in mine:
            cp.wait()

    return pl.pallas_call(
        body, out_shape=[jax.ShapeDtypeStruct((N_DEV,) + s.shape, s.dtype) for s in shards],
        in_specs=[ANY] * n, out_specs=[ANY] * n,
        scratch_shapes=[pltpu.SemaphoreType.DMA((n, 7)), pltpu.SemaphoreType.DMA((n, 7)), pltpu.SemaphoreType.DMA((n,))],
        name=name)(*shards)


def exchange_partials(parts, name):
    n = len(parts)

    def body(*refs):
        src_refs, dst_refs = refs[:n], refs[n:2 * n]
        send_sems, recv_sems, local_sems = refs[2 * n:]
        x, y, c = _place()
        me = 4 * x + 2 * y + c
        copies = []
        mine = []
        for a in range(n):
            mine.append(pltpu.make_async_copy(src_refs[a].at[me], dst_refs[a].at[me], local_sems.at[a]))
            for k in range(1, N_DEV):
                px = 1 - x if k & 4 else x
                py = 1 - y if k & 2 else y
                pc = 1 - c if k & 1 else c
                copies.append(pltpu.make_async_remote_copy(
                    src_ref=src_refs[a].at[4 * px + 2 * py + pc], dst_ref=dst_refs[a].at[me],
                    send_sem=send_sems.at[a, k - 1], recv_sem=recv_sems.at[a, k - 1],
                    device_id=(px, py, pc), device_id_type=MESH))
        for cp in mine + copies:
            cp.start()
        for cp in copies:
            cp.wait_recv()
        for cp in copies:
            cp.wait_send()
        for cp in mine:
            cp.wait()

    return pl.pallas_call(
        body, out_shape=[jax.ShapeDtypeStruct(p.shape, p.dtype) for p in parts],
        in_specs=[ANY] * n, out_specs=[ANY] * n,
        scratch_shapes=[pltpu.SemaphoreType.DMA((n, 7)), pltpu.SemaphoreType.DMA((n, 7)), pltpu.SemaphoreType.DMA((n,))],
        name=name)(*parts)


def reduce_adamw(parts, w, m, v, name):
    L = len(parts)
    n, Rl, C = parts[0].shape
    R = w.shape[0]
    assert R == L * Rl
    tr = Rl if Rl * C <= 256 * 1024 else _pick(Rl, 256, 16)
    nr = Rl // tr

    def body(*refs):
        p_refs = refs[:L]
        w_ref, m_ref, v_ref, g_ref, d_ref, nm_ref, nv_ref, sum_ref = refs[L:]
        grp = pl.program_id(0)
        for j in range(L):
            @pl.when(grp == j)
            def _(j=j):
                acc = p_refs[j][0].astype(F32)
                for s in range(1, n):
                    acc = acc + p_refs[j][s].astype(F32)
                sum_ref[...] = acc

        g_ = sum_ref[...]
        m_ = ADAM_B1 * m_ref[...] + (1.0 - ADAM_B1) * g_
        v_ = ADAM_B2 * v_ref[...] + (1.0 - ADAM_B2) * (g_ * g_)
        m_hat = m_ / (1.0 - ADAM_B1 ** ADAM_STEP)
        v_hat = v_ / (1.0 - ADAM_B2 ** ADAM_STEP)
        g_ref[...] = g_
        d_ref[...] = -ADAM_LR * (m_hat / (jnp.sqrt(v_hat) + ADAM_EPS) + ADAM_WD * w_ref[...])
        nm_ref[...] = m_
        nv_ref[...] = v_

    blk = pl.BlockSpec((tr, C), lambda l, r: (l * nr + r, 0))
    sds = jax.ShapeDtypeStruct((R, C), F32)
    p_specs = [pl.BlockSpec((n, tr, C), lambda l, r, j=j: (0, jnp.where(l == j, r, 0), 0)) for j in range(L)]
    return pl.pallas_call(
        body, grid=(L, nr), in_specs=p_specs + [blk] * 3,
        out_specs=[blk] * 4, out_shape=[sds] * 4, scratch_shapes=[pltpu.VMEM((tr, C), F32)],
        compiler_params=_cp(("arbitrary", "arbitrary")), name=name)(*parts, w, m, v)


SHARDED = {"ffa_w_in": (2, BF16), "ffa_w_out": (1, BF16), "mix_w_in": (2, BF16), "mla_w_uq": (2, BF16),
           "mla_w_ukv": (2, BF16), "mix_w_o": (1, BF16), "ffb_w_in": (2, BF16), "ffb_w_out": (1, BF16),
           "ple_w_gate": (1, BF16), "ple_w_proj": (2, BF16), "gdn_conv_w": (2, F32), "ln_g": (2, F32), "ln_b": (2, F32)}
FFN_SLOT = ("ffa_w_in", "ffa_w_out", "ffb_w_in", "ffb_w_out")
REPLICATED = ("gdn_a_log", "gdn_dt_bias", "gdn_norm_w", "mla_q_norm_w", "mla_kv_norm_w")
WEIGHTS = ("ffa_w_in", "ffa_w_out", "mix_w_in", "gdn_conv_w", "gdn_a_log", "gdn_dt_bias", "gdn_norm_w", "mla_q_norm_w",
           "mla_kv_norm_w", "mla_w_uq", "mla_w_ukv", "mix_w_o", "ffb_w_in", "ffb_w_out", "ln_g", "ln_b", "ple_w_gate",
           "ple_w_proj")


def _to_slots(full, axis):
    L, a, b = full.shape
    if axis == 2:
        return full.reshape(L, a, N_DEV, b // N_DEV).transpose(2, 0, 1, 3).reshape(N_DEV, L * a, b // N_DEV)
    return full.reshape(L, N_DEV, a // N_DEV, b).transpose(1, 0, 2, 3).reshape(N_DEV, L * a // N_DEV, b)


def _from_slots(slots, shard_shape, axis):
    L, a, b = shard_shape
    t = slots.reshape((N_DEV,) + tuple(shard_shape))
    if axis == 2:
        return t.transpose(1, 2, 0, 3).reshape(L, a, N_DEV * b)
    return t.transpose(1, 0, 2, 3).reshape(L, N_DEV * a, b)


def _view2d(t):
    return t.reshape(-1, t.shape[-1])


def _pad_heads(w, nh):
    K = w.shape[0]
    return jnp.pad(w.reshape(K, nh, GDN_D), ((0, 0), (0, 0), (0, LANES - GDN_D))).reshape(K, nh * LANES)


def _unpad_heads(w, nh):
    K = w.shape[0]
    return w.reshape(K, nh, LANES)[:, :, :GDN_D].reshape(K, nh * GDN_D)


IN_WIDTHS = (512, 512, 512, 512, 8, 8, 256, 256, 256, 256, 160)


def _split_in(w):
    offs = np.cumsum((0,) + IN_WIDTHS)
    return [w[:, int(offs[i]):int(offs[i + 1])] for i in range(len(IN_WIDTHS))]


def _pad_in_proj(w):
    gq, gk, gv, gz, ga, gb, sq, sk, sv, mq, mkv = _split_in(w)
    gab = jnp.pad(jnp.concatenate([ga, gb], axis=1), ((0, 0), (0, LANES - 2 * GDN_HEADS)))
    return jnp.concatenate(
        [gq, gk, gv, gz] + [_pad_heads(t, SB_HEADS) for t in (sq, sk, sv)]
        + [mq, jnp.pad(mkv, ((0, 0), (0, 2 * LANES - mkv.shape[1]))), gab], axis=1)


def _unpad_in_proj(wp):
    c = lambda cb, n: wp[:, cb * LANES:(cb + n) * LANES]
    gab = c(CB_GAB, 1)
    parts = [c(cb, DO_SB) for cb in (CB_GQ, CB_GK, CB_GV, CB_GZ)]
    parts += [gab[:, :GDN_HEADS], gab[:, GDN_HEADS:2 * GDN_HEADS]]
    parts += [_unpad_heads(c(cb, SB_HEADS), SB_HEADS) for cb in (CB_SQ, CB_SK, CB_SV)]
    parts += [c(CB_MQ, 2), c(CB_MKV, 2)[:, :MLA_KV_RANK + MLA_ROPE]]
    return jnp.concatenate(parts, axis=1)


def _pad_lanes(w, width):
    return jnp.pad(w, ((0, 0), (0, width - w.shape[1])))


def _mla_up_pad(w_uq, w_ukv):
    H = MLA_HEADS
    dq = MLA_NOPE + MLA_ROPE
    wuq = jnp.pad(w_uq.reshape(-1, H, dq), ((0, 0), (0, 0), (0, LANES - dq))).reshape(-1, H * LANES)
    kv = w_ukv.reshape(-1, H, MLA_NOPE + MLA_V)
    wuk = jnp.pad(kv[:, :, :MLA_NOPE], ((0, 0), (0, 0), (0, LANES - MLA_NOPE))).reshape(-1, H * LANES)
    wuv = jnp.pad(kv[:, :, MLA_NOPE:], ((0, 0), (0, 0), (0, LANES - MLA_V))).reshape(-1, H * LANES)
    return wuq, wuk, wuv


def _mla_up_unpad(dwuq, dwuk, dwuv):
    H = MLA_HEADS
    dq = MLA_NOPE + MLA_ROPE
    g_uq = dwuq.reshape(-1, H, LANES)[:, :, :dq].reshape(-1, H * dq)
    g_ukv = jnp.concatenate([dwuk.reshape(-1, H, LANES)[:, :, :MLA_NOPE], dwuv.reshape(-1, H, LANES)[:, :, :MLA_V]],
                            axis=2).reshape(-1, H * (MLA_NOPE + MLA_V))
    return g_uq, g_ukv


def _rope_tables(positions):
    inv = 1.0 / (ROPE_BASE ** (jnp.arange(0, MLA_ROPE, 2, dtype=F32) / MLA_ROPE))
    ang = positions.astype(F32)[:, None] * inv
    cos, sin = jnp.cos(ang), jnp.sin(ang)
    T = positions.shape[0]
    one = lambda n: jnp.ones((T, n), F32)
    zero = lambda n: jnp.zeros((T, n), F32)
    tail = LANES - MLA_NOPE - MLA_ROPE
    C = jnp.concatenate([one(MLA_NOPE), cos, cos, one(tail)], axis=1)
    Sn = jnp.concatenate([zero(MLA_NOPE), -sin, zero(HALF_ROPE + tail)], axis=1)
    Sp = jnp.concatenate([zero(MLA_NOPE + HALF_ROPE), sin, zero(tail)], axis=1)
    return C, Sn, Sp


GATHER_FIRST = [("ffa_w_in", 0), ("ffa_w_out", 0)] + [(n, l) for l in range(DEPTH) for n in ("gdn_conv_w", "ln_g", "ln_b")]
GATHER_PLAN = {
    (0, "ffa_fwd"): [("mix_w_in", 0), ("mla_w_uq", 0), ("mla_w_ukv", 0)],
    (0, "gdn_chunk_fwd"): [("mix_w_o", 0), ("ffb_w_in", 0)],
    (0, "sb_fwd"): [("ffb_w_out", 0), ("ple_w_gate", 0), ("ple_w_proj", 0)],
    (0, "mla_fwd"): [("ffa_w_in", 1), ("mix_w_o", 1)],
    (0, "ffb_fwd"): [("ffa_w_out", 1), ("mix_w_in", 1)],
    (1, "ffa_fwd"): [("ffb_w_in", 1)],
    (1, "in_proj"): [("ffb_w_out", 1), ("ple_w_gate", 1), ("ple_w_proj", 1), ("mla_w_uq", 1), ("mla_w_ukv", 1)],
}
SCATTER_PLAN = {
    (1, "gdn_chunk_bwd"): [("ffb_w_in", 1)],
    (1, "gdn_pre_bwd"): [("ffb_w_out", 1), ("ple_w_gate", 1), ("ple_w_proj", 1), ("mix_w_o", 1)],
    (1, "ffa_bwd"): [("mix_w_in", 1), ("mla_w_uq", 1), ("mla_w_ukv", 1), ("gdn_conv_w", 1)],
    (0, "ffb_bwd"): [("ffa_w_in", 1)],
    (0, "gdn_chunk_bwd"): [("ffb_w_in", 0)],
    (0, "gdn_pre_bwd"): [("ffb_w_out", 0), ("ple_w_gate", 0), ("ple_w_proj", 0), ("mix_w_o", 0)],
    (0, "mla_bwd"): [("ffa_w_out", 1), ("ln_g", 1), ("ln_b", 1)],
    (0, "ffa_bwd"): [("mix_w_in", 0), ("mla_w_uq", 0), ("mla_w_ukv", 0), ("gdn_conv_w", 0)],
    (0, "d_ffa_in"): [("ffa_w_out", 0), ("ln_g", 0), ("ln_b", 0)],
}
SCATTER_LAST = [("ffa_w_in", 0)]


class Exchanges:
    def __init__(self, shards):
        self.shards = shards
        self.full = {}
        self.partial = {}
        self.received = {}

    def _block(self, key):
        n, l = key
        return self.shards[n][l].astype(SHARDED[n][1])

    def _absorb_gather(self, keys, results):
        for (n, l), g in zip(keys, results):
            blk = self.shards[n][l]
            self.full[(n, l)] = g if n in FFN_SLOT else _from_slots(g, (1,) + blk.shape, SHARDED[n][0])[0]

    def gather_now(self, keys, name):
        self._absorb_gather(keys, all_gather([self._block(k) for k in keys], name))

    def gather_with(self, layer, tag):
        keys = GATHER_PLAN.get((layer, tag))
        return None if keys is None else (keys, Hosted("gather", [self._block(k) for k in keys]))

    def scatter_with(self, layer, tag):
        keys = SCATTER_PLAN.get((layer, tag))
        return None if keys is None else (keys, Hosted("scatter", [self.partial[k] for k in keys]))

    def done(self, carried):
        if carried is not None:
            keys, hosted = carried
            if hosted.kind == "gather":
                self._absorb_gather(keys, hosted.results)
            else:
                self.received.update(zip(keys, hosted.results))

    def add_grad(self, key, g):
        n, l = key
        self.partial[key] = g if n in FFN_SLOT else _to_slots(g[None], SHARDED[n][0]).astype(SHARDED[n][1])


def _carried(c):
    return None if c is None else c[1]


def _layer_fwd(h0, p_i, rope, i, ex, rep):
    L = "L%d_" % i
    S = {"h0": h0, "p": p_i}
    W = ex.full
    ln_g = [W[("ln_g", i)][j][None, :] for j in range(3)]
    ln_b = [W[("ln_b", i)][j][None, :] for j in range(3)]
    S["ln_g"] = ln_g
    c = ex.gather_with(i, "ffa_fwd")
    S["h1"], S["xh1"], S["rs1"], S["Ga"], S["Ua"], S["h1b"] = ffn_fwd(
        h0, W[("ffa_w_in", i)], W[("ffa_w_out", i)], ln_g[0], ln_b[0], L + "ffa_fwd", hosted=_carried(c))
    ex.done(c)
    S["win"] = _pad_in_proj(W[("mix_w_in", i)])
    c = ex.gather_with(i, "in_proj")
    S["proj"] = mm_nn(S["h1b"], S["win"], L + "in_proj", hosted=_carried(c))
    ex.done(c)
    S["conv"] = W[("gdn_conv_w", i)]
    S["a_row"] = _pad_lanes(rep["gdn_a_log"][i][None, :], LANES)
    S["dt_row"] = _pad_lanes(rep["gdn_dt_bias"][i][None, :], LANES)
    S["nw"] = rep["gdn_norm_w"][i][None, :]
    S["wq"] = rep["mla_q_norm_w"][i][None, :]
    S["wkv"] = rep["mla_kv_norm_w"][i][None, :]
    S["qkv"] = gdn_pre_fwd(S["proj"], S["conv"], L + "gdn_pre_fwd")
    c = ex.gather_with(i, "gdn_chunk_fwd")
    S["o_gdn"], S["opre"], S["Tm"], S["Sin"] = gdn_chunk_fwd(S["qkv"], S["proj"], S["a_row"], S["dt_row"], S["nw"],
                                                            L + "gdn_chunk_fwd", hosted=_carried(c))
    ex.done(c)
    c = ex.gather_with(i, "sb_fwd")
    S["o_sb"], S["tot"] = sb_fwd(S["proj"], L + "sb_fwd", hosted=_carried(c))
    ex.done(c)
    S["wuq"], S["wuk"], S["wuv"] = _mla_up_pad(W[("mla_w_uq", i)], W[("mla_w_ukv", i)])
    S["Q"], S["K"], S["V"] = mla_pre_fwd(S["proj"], S["wq"], S["wkv"], S["wuq"], S["wuk"], S["wuv"], *rope, L + "mla_pre_fwd")
    c = ex.gather_with(i, "mla_fwd")
    S["o_mla"], S["lse"] = mla_fwd(S["Q"], S["K"], S["V"], L + "mla_fwd", hosted=_carried(c))
    ex.done(c)
    wo = W[("mix_w_o", i)]
    wo_att = wo[GDN_W:].reshape(-1, GDN_D, wo.shape[1])
    S["wo"] = jnp.concatenate(
        [wo[:GDN_W], jnp.pad(wo_att, ((0, 0), (0, LANES - GDN_D), (0, 0))).reshape(-1, wo.shape[1])], axis=0)
    S["h2"], S["xh2"], S["rs2"], S["h2b"] = proj_res_ln([S["o_gdn"], S["o_sb"], S["o_mla"]], S["wo"], S["h1"],
                                                        ln_g[1], ln_b[1], L + "out_proj")
    c = ex.gather_with(i, "ffb_fwd")
    S["h3"], S["xh3"], S["rs3"], S["Gb"], S["Ub"], _ = ffn_fwd(
        S["h2"], W[("ffb_w_in", i)], W[("ffb_w_out", i)], ln_g[2], ln_b[2], L + "ffb_fwd", hosted=_carried(c))
    ex.done(c)
    h4, S["a"], S["e"] = ple_fwd(S["h3"], p_i, W[("ple_w_gate", i)], W[("ple_w_proj", i)], L + "ple_fwd")
    return h4, S


def _layer_bwd(dh4, S, rope, i, ex):
    L = "L%d_" % i
    W = ex.full
    Grep = {}
    dh3, da, de = ple_bwd(dh4, S["a"], S["e"], W[("ple_w_gate", i)], L + "ple_bwd")
    ex.add_grad(("ple_w_gate", i), mm_tn(S["h3"], da, L + "d_ple_gate"))
    ex.add_grad(("ple_w_proj", i), mm_tn(S["p"], de, L + "d_ple_proj"))
    dz3, dg2, db2 = ln_bwd(dh3, S["xh3"], S["rs3"], S["ln_g"][2], L + "ln3_bwd")
    c = ex.scatter_with(i, "ffb_bwd")
    dh2, dGb, dUb, actb = ffn_bwd(dz3, S["Gb"], S["Ub"], W[("ffb_w_in", i)], W[("ffb_w_out", i)], L + "ffb_bwd",
                                  hosted=_carried(c))
    ex.done(c)
    ex.add_grad(("ffb_w_in", i), ffn_dw_in(S["h2b"].T, dGb, dUb, L + "d_ffb_in"))
    ex.add_grad(("ffb_w_out", i), ffn_dw_out(actb, dz3, L + "d_ffb_out"))
    dz2, dg1, db1 = ln_bwd(dh2, S["xh2"], S["rs2"], S["ln_g"][1], L + "ln2_bwd")
    docat = mm_nn(dz2, S["wo"], L + "d_ocat", b_transposed=True)
    dwo_att = jnp.concatenate([mm_tn(S["o_sb"], dz2, L + "d_wo_sb"), mm_tn(S["o_mla"], dz2, L + "d_wo_mla")], axis=0)
    dwo_att = dwo_att.reshape(-1, LANES, dwo_att.shape[1])[:, :GDN_D, :].reshape(-1, dwo_att.shape[1])
    ex.add_grad(("mix_w_o", i), jnp.concatenate([mm_tn(S["o_gdn"], dz2, L + "d_wo_gdn"), dwo_att], axis=0))
    c = ex.scatter_with(i, "gdn_chunk_bwd")
    dqkv, dgz, dgab, d_alog, d_dt, d_nw = gdn_chunk_bwd(S["qkv"], S["proj"], S["a_row"], S["dt_row"], S["nw"],
                                                        S["opre"], S["Tm"], S["Sin"], docat, L + "gdn_chunk_bwd",
                                                        hosted=_carried(c))
    ex.done(c)
    c = ex.scatter_with(i, "gdn_pre_bwd")
    dc, dconv = gdn_pre_bwd_pointwise(S["proj"], S["conv"], dqkv, L + "gdn_pre_bwd", hosted=_carried(c))
    ex.done(c)
    dxqkv = gdn_pre_bwd_conv(dc, S["conv"], L + "gdn_conv_bwd")
    ex.add_grad(("gdn_conv_w", i), dconv)
    Grep["gdn_a_log"], Grep["gdn_dt_bias"], Grep["gdn_norm_w"] = d_alog[0, :GDN_HEADS], d_dt[0, :GDN_HEADS], d_nw[0]
    dsq, dsk, dsv = sb_bwd(S["proj"], S["tot"], docat, DO_SB, L + "sb_bwd")
    c = ex.scatter_with(i, "mla_bwd")
    dQ, dK, dV = mla_bwd(S["Q"], S["K"], S["V"], S["o_mla"], S["lse"], docat, DO_MLA, L + "mla_bwd",
                         hosted=_carried(c))
    ex.done(c)
    dmq, dmkv, dwuq, dwuk, dwuv, dwq, dwkv = mla_pre_bwd(
        S["proj"], S["wq"], S["wkv"], S["wuq"], S["wuk"], S["wuv"], *rope, dQ, dK, dV, L + "mla_pre_bwd")
    g_uq, g_ukv = _mla_up_unpad(dwuq, dwuk, dwuv)
    ex.add_grad(("mla_w_uq", i), g_uq)
    ex.add_grad(("mla_w_ukv", i), g_ukv)
    Grep["mla_q_norm_w"], Grep["mla_kv_norm_w"] = dwq[0], dwkv[0]
    dproj = jnp.concatenate([dxqkv, dgz, dsq, dsk, dsv, dmq, dmkv, dgab], axis=1).astype(BF16)
    ex.add_grad(("mix_w_in", i),
                _unpad_in_proj(mm_tn(S["h1b"].T, dproj, L + "d_in_proj", a_transposed=True)))
    dh1 = mm_nn(dproj, S["win"], L + "d_h1", res=dz2, res_scale=ALPHA, b_transposed=True)
    dz1, dg0, db0 = ln_bwd(dh1, S["xh1"], S["rs1"], S["ln_g"][0], L + "ln1_bwd")
    c = ex.scatter_with(i, "ffa_bwd")
    dh0, dGa, dUa, acta = ffn_bwd(dz1, S["Ga"], S["Ua"], W[("ffa_w_in", i)], W[("ffa_w_out", i)], L + "ffa_bwd",
                                  hosted=_carried(c))
    ex.done(c)
    ex.add_grad(("ffa_w_out", i), ffn_dw_out(acta, dz1, L + "d_ffa_out"))
    ex.add_grad(("ln_g", i), jnp.concatenate([dg0, dg1, dg2], axis=0))
    ex.add_grad(("ln_b", i), jnp.concatenate([db0, db1, db2], axis=0))
    c = ex.scatter_with(i, "d_ffa_in")
    ex.add_grad(("ffa_w_in", i), ffn_dw_in(S["h0"].T.astype(BF16), dGa, dUa, L + "d_ffa_in", hosted=_carried(c)))
    ex.done(c)
    return dh0, Grep


def _local_step(x, p, positions, target, ex, rep):
    assert DEPTH == 2
    rope = _rope_tables(positions)
    h, saved = x, []
    for i in range(DEPTH):
        h, S = _layer_fwd(h, p[i], rope, i, ex, rep)
        saved.append(S)
    loss, dh = loss_head(h, target, "loss_head")
    grads = [None] * DEPTH
    for i in reversed(range(DEPTH)):
        dh, grads[i] = _layer_bwd(dh, saved[i], rope, i, ex)
    return loss, dh, {n: jnp.stack([grads[i][n] for i in range(DEPTH)]) for n in REPLICATED}


def kernel(x, p, positions, ffa_w_in, ffa_w_out, mix_w_in, gdn_conv_w, gdn_a_log, gdn_dt_bias, gdn_norm_w, mla_q_norm_w, mla_kv_norm_w, mla_w_uq, mla_w_ukv, mix_w_o, ffb_w_in, ffb_w_out, ln_g, ln_b, ple_w_gate, ple_w_proj, loss_target, m_ffa_w_in, m_ffa_w_out, m_mix_w_in, m_gdn_conv_w, m_gdn_a_log, m_gdn_dt_bias, m_gdn_norm_w, m_mla_q_norm_w, m_mla_kv_norm_w, m_mla_w_uq, m_mla_w_ukv, m_mix_w_o, m_ffb_w_in, m_ffb_w_out, m_ln_g, m_ln_b, m_ple_w_gate, m_ple_w_proj, v_ffa_w_in, v_ffa_w_out, v_mix_w_in, v_gdn_conv_w, v_gdn_a_log, v_gdn_dt_bias, v_gdn_norm_w, v_mla_q_norm_w, v_mla_kv_norm_w, v_mla_w_uq, v_mla_w_ukv, v_mix_w_o, v_ffb_w_in, v_ffb_w_out, v_ln_g, v_ln_b, v_ple_w_gate, v_ple_w_proj):
    given = dict(locals())
    shards = {n: given[n] for n in WEIGHTS}
    ex = Exchanges({n: shards[n] for n in SHARDED})
    ex.gather_now(GATHER_FIRST, "gather_first")
    loss, grad_x, Grep = _local_step(x[0], p[:, 0], positions[0], loss_target[0], ex, {n: shards[n] for n in REPLICATED})
    loss = lax.psum(loss[0, 0], ("x", "y", "c"))
    ex.received.update(zip(SCATTER_LAST, exchange_partials([ex.partial[k] for k in SCATTER_LAST], "scatter_last")))
    rep_received = dict(zip(REPLICATED, all_gather([Grep[n] for n in REPLICATED], "gather_replicated_grads")))
    grad, delta, new_m, new_v = {}, {}, {}, {}
    for n in WEIGHTS:
        shape = shards[n].shape
        parts = [rep_received[n]] if n in REPLICATED else [ex.received[(n, l)] for l in range(DEPTH)]
        if parts[0].shape[1] % 8:
            parts = [jnp.concatenate(parts, axis=1)]
        outs = reduce_adamw(parts, _view2d(shards[n]), _view2d(given["m_" + n]), _view2d(given["v_" + n]),
                            "adamw_" + n)
        grad[n], delta[n], new_m[n], new_v[n] = (t.reshape(shape) for t in outs)
    return (loss, grad_x[None], *[grad[n] for n in WEIGHTS], *[delta[n] for n in WEIGHTS],
            *[new_m[n] for n in WEIGHTS], *[new_v[n] for n in WEIGHTS])
```

```python
import functools
import numpy as np
import jax
import jax.numpy as jnp
from jax import lax
from jax.experimental import pallas as pl
from jax.experimental.pallas import tpu as pltpu

F32 = jnp.float32
BF16 = jnp.bfloat16

DEPTH = 2
LN_EPS = 1e-5
RMS_EPS = 1e-6
ALPHA = (2 * DEPTH) ** 0.25
GDN_HEADS, GDN_D, GDN_CONV, GDN_CHUNK = 8, 64, 4, 64
SB_HEADS, SB_DIM = 4, 64
MLA_HEADS, MLA_NOPE, MLA_ROPE, MLA_V, MLA_Q_RANK, MLA_KV_RANK = 4, 64, 32, 64, 256, 128
ROPE_BASE = 10000.0
HALF_ROPE = MLA_ROPE // 2
LANES = 128
N_DEV = 8
ADAM_LR, ADAM_B1, ADAM_B2, ADAM_EPS, ADAM_WD, ADAM_STEP = 0.001, 0.9, 0.999, 1e-08, 0.01, 10

CB_GQ, CB_GK, CB_GV, CB_GZ = 0, 4, 8, 12
CB_SQ, CB_SK, CB_SV = 16, 20, 24
CB_MQ, CB_MKV, CB_GAB = 28, 30, 32
PROJ_W = 33 * LANES
GDN_W = GDN_HEADS * GDN_D
DO_SB = GDN_W // LANES
DO_MLA = DO_SB + SB_HEADS
VMEM_LIMIT = 56 * 1024 * 1024
MM_TILE = 1536

NT_DIMS = (((1,), (1,)), ((), ()))
TN_DIMS = (((0,), (0,)), ((), ()))


def _cp(sem):
    return pltpu.CompilerParams(dimension_semantics=sem, vmem_limit_bytes=VMEM_LIMIT)


def _bdot(a, b):
    return jnp.dot(a.astype(BF16), b.astype(BF16), preferred_element_type=F32)


def _bdot_nt(a, b):
    return lax.dot_general(a.astype(BF16), b.astype(BF16), NT_DIMS, preferred_element_type=F32)


def _bdot_tn(a, b):
    return lax.dot_general(a.astype(BF16), b.astype(BF16), TN_DIMS, preferred_element_type=F32)


def _split2(a):
    hi = a.astype(BF16)
    lo = (a - hi.astype(F32)).astype(BF16)
    return hi, lo


def _ones_dot_left(ones_bf16, x):
    hi = x.astype(BF16)
    r1 = x - hi.astype(F32)
    mid = r1.astype(BF16)
    lo = (r1 - mid.astype(F32)).astype(BF16)
    d = functools.partial(jnp.dot, preferred_element_type=F32)
    return d(ones_bf16, hi) + d(ones_bf16, mid) + d(ones_bf16, lo)


def _iota2(shape, dim):
    return lax.broadcasted_iota(jnp.int32, shape, dim)


def _sigmoid(x):
    return 0.5 * jnp.tanh(0.5 * x) + 0.5


def _softplus(x):
    return jnp.maximum(x, 0.0) + jnp.log(1.0 + jnp.exp(-jnp.abs(x)))


def _pick(n, limit, mult):
    if n <= limit:
        return n
    best = None
    for t in range(mult, limit + 1, mult):
        if n % t == 0:
            best = t
    assert best is not None, (n, limit, mult)
    return best


MESH = pl.DeviceIdType.MESH
ANY = pl.BlockSpec(memory_space=pl.ANY)


def _place():
    return lax.axis_index("x"), lax.axis_index("y"), lax.axis_index("c")


def _peer(k):
    x, y, c = _place()
    return (1 - x if k & 4 else x, 1 - y if k & 2 else y, 1 - c if k & 1 else c)


class Hosted:
    def __init__(self, kind, arrays):
        self.kind, self.arrays, self.n, self.results = kind, list(arrays), len(arrays), None

    def out_shapes(self):
        if self.kind == "gather":
            return [jax.ShapeDtypeStruct((N_DEV,) + a.shape, a.dtype) for a in self.arrays]
        return [jax.ShapeDtypeStruct(a.shape, a.dtype) for a in self.arrays]

    def sems(self):
        return [pltpu.SemaphoreType.DMA((self.n, N_DEV - 1)), pltpu.SemaphoreType.DMA((self.n, N_DEV - 1)),
                pltpu.SemaphoreType.DMA((self.n,))]

    def _copies(self, src_refs, dst_refs, send_sems, recv_sems, local_sems):
        x, y, c = _place()
        me = 4 * x + 2 * y + c
        local, remote = [], []
        for a in range(self.n):
            gather = self.kind == "gather"
            local.append(pltpu.make_async_copy(src_refs[a] if gather else src_refs[a].at[me], dst_refs[a].at[me],
                                               local_sems.at[a]))
            for k in range(1, N_DEV):
                px, py, pc = _peer(k)
                remote.append(pltpu.make_async_remote_copy(
                    src_ref=src_refs[a] if gather else src_refs[a].at[4 * px + 2 * py + pc], dst_ref=dst_refs[a].at[me],
                    send_sem=send_sems.at[a, k - 1], recv_sem=recv_sems.at[a, k - 1],
                    device_id=(px, py, pc), device_id_type=MESH))
        return local, remote

    def start(self, *refs):
        local, remote = self._copies(*refs)
        for cp in local + remote:
            cp.start()

    def wait(self, *refs):
        local, remote = self._copies(*refs)
        for cp in remote:
            cp.wait_recv()
        for cp in remote:
            cp.wait_send()
        for cp in local:
            cp.wait()


def _hosted_call(hosted, body, *, grid, in_specs, out_specs, out_shape, scratch_shapes=(), compiler_params, name):
    if hosted is None:
        return pl.pallas_call(body, grid=grid, in_specs=in_specs, out_specs=out_specs, out_shape=out_shape,
                              scratch_shapes=scratch_shapes, compiler_params=compiler_params, name=name)
    single = not isinstance(out_shape, (list, tuple))
    o_specs = [out_specs] if single else list(out_specs)
    o_shape = [out_shape] if single else list(out_shape)
    n_in, n_out, n_scr, n = len(in_specs), len(o_specs), len(scratch_shapes), hosted.n

    def wrapped(*refs):
        ins, c_in = refs[:n_in], refs[n_in:n_in + n]
        outs, c_out = refs[n_in + n:n_in + n + n_out], refs[n_in + n + n_out:n_in + 2 * n + n_out]
        rest = refs[n_in + 2 * n + n_out:]
        scr, sems = rest[:n_scr], rest[n_scr:]
        ids = [pl.program_id(ax) for ax in range(len(grid))]
        first = functools.reduce(jnp.logical_and, [i == 0 for i in ids])
        last = functools.reduce(jnp.logical_and, [i == g - 1 for i, g in zip(ids, grid)])

        @pl.when(first)
        def _():
            hosted.start(c_in, c_out, *sems)

        body(*ins, *outs, *scr)

        @pl.when(last)
        def _():
            hosted.wait(c_in, c_out, *sems)

    call = pl.pallas_call(
        wrapped, grid=grid, in_specs=list(in_specs) + [ANY] * n, out_specs=o_specs + [ANY] * n,
        out_shape=o_shape + hosted.out_shapes(), scratch_shapes=list(scratch_shapes) + hosted.sems(),
        compiler_params=_cp(("arbitrary",) * len(grid)), name=name)

    def run(*args):
        outs = call(*args, *hosted.arrays)
        hosted.results = list(outs[n_out:])
        return outs[0] if single else list(outs[:n_out])

    return run


def mm_nn(a, b, name, out_dtype=F32, res=None, res_scale=1.0, b_transposed=False, hosted=None):
    M, K = a.shape
    N = b.shape[0] if b_transposed else b.shape[1]
    tm, tn, tk = _pick(M, 512, 16), _pick(N, MM_TILE, LANES), _pick(K, MM_TILE, LANES)
    nk = K // tk
    has_res = res is not None
    dot = _bdot_nt if b_transposed else _bdot

    def body(*refs):
        if has_res:
            a_ref, b_ref, r_ref, o_ref, acc_ref = refs
        else:
            a_ref, b_ref, o_ref, acc_ref = refs
        k = pl.program_id(2)

        @pl.when(k == 0)
        def _():
            acc_ref[...] = jnp.zeros_like(acc_ref)

        acc_ref[...] += dot(a_ref[...], b_ref[...])

        @pl.when(k == nk - 1)
        def _():
            out = acc_ref[...]
            if has_res:
                out = out + res_scale * r_ref[...]
            o_ref[...] = out.astype(o_ref.dtype)

    b_spec = pl.BlockSpec((tn, tk), lambda i, j, k: (j, k)) if b_transposed else pl.BlockSpec((tk, tn), lambda i, j, k: (k, j))
    in_specs = [pl.BlockSpec((tm, tk), lambda i, j, k: (i, k)), b_spec]
    args = [a, b]
    if has_res:
        in_specs.append(pl.BlockSpec((tm, tn), lambda i, j, k: (i, j)))
        args.append(res)
    return _hosted_call(
        hosted, body, grid=(M // tm, N // tn, nk), in_specs=in_specs,
        out_specs=pl.BlockSpec((tm, tn), lambda i, j, k: (i, j)),
        out_shape=jax.ShapeDtypeStruct((M, N), out_dtype),
        scratch_shapes=[pltpu.VMEM((tm, tn), F32)],
        compiler_params=_cp(("parallel", "parallel", "arbitrary")), name=name)(*args)


def mm_tn(a, b, name, out_dtype=F32, a_transposed=False):
    K, T = a.shape if a_transposed else a.shape[::-1]
    _, N = b.shape
    tk = K if a_transposed else _pick(K, 512, LANES)
    tn, tt = _pick(N, MM_TILE, LANES), _pick(T, 512, LANES)
    nt = T // tt

    def body(a_ref, b_ref, o_ref, acc_ref):
        t = pl.program_id(2)

        @pl.when(t == 0)
        def _():
            acc_ref[...] = jnp.zeros_like(acc_ref)

        if a_transposed:
            acc_ref[...] += _bdot(a_ref[:, pl.ds(pl.multiple_of(t * tt, tt), tt)], b_ref[...])
        else:
            acc_ref[...] += _bdot_tn(a_ref[...], b_ref[...])

        @pl.when(t == nt - 1)
        def _():
            o_ref[...] = acc_ref[...].astype(o_ref.dtype)

    a_spec = pl.BlockSpec((K, T), lambda i, j, t: (0, 0)) if a_transposed else pl.BlockSpec((tt, tk), lambda i, j, t: (t, i))
    return pl.pallas_call(
        body, grid=(K // tk, N // tn, nt),
        in_specs=[a_spec, pl.BlockSpec((tt, tn), lambda i, j, t: (t, j))],
        out_specs=pl.BlockSpec((tk, tn), lambda i, j, t: (i, j)),
        out_shape=jax.ShapeDtypeStruct((K, N), out_dtype),
        scratch_shapes=[pltpu.VMEM((tk, tn), F32)],
        compiler_params=_cp(("parallel", "parallel", "arbitrary")), name=name)(a, b)


def _ln_apply(z, g, b):
    mu = jnp.mean(z, axis=-1, keepdims=True)
    zc = z - mu
    var = jnp.mean(zc * zc, axis=-1, keepdims=True)
    rstd = lax.rsqrt(var + LN_EPS)
    xhat = zc * rstd
    return xhat * g + b, xhat, rstd


def ln_bwd(dout, xhat, rstd, g, name):
    T, D = dout.shape
    tm = _pick(T, 512, 8)

    def body(do_ref, xh_ref, rs_ref, g_ref, dz_ref, dg_ref, db_ref):
        i = pl.program_id(0)

        @pl.when(i == 0)
        def _():
            dg_ref[...] = jnp.zeros_like(dg_ref)
            db_ref[...] = jnp.zeros_like(db_ref)

        do = do_ref[...]
        xh = xh_ref[...]
        dxh = do * g_ref[...]
        m1 = jnp.mean(dxh, axis=-1, keepdims=True)
        m2 = jnp.mean(dxh * xh, axis=-1, keepdims=True)
        dz_ref[...] = rs_ref[...] * (dxh - m1 - xh * m2)
        dg_ref[...] += jnp.sum(do * xh, axis=0, keepdims=True)
        db_ref[...] += jnp.sum(do, axis=0, keepdims=True)

    row = pl.BlockSpec((tm, D), lambda i: (i, 0))
    vec = pl.BlockSpec((1, D), lambda i: (0, 0))
    return pl.pallas_call(
        body, grid=(T // tm,),
        in_specs=[row, row, pl.BlockSpec((tm, 1), lambda i: (i, 0)), vec],
        out_specs=[row, vec, vec],
        out_shape=[jax.ShapeDtypeStruct((T, D), F32), jax.ShapeDtypeStruct((1, D), F32), jax.ShapeDtypeStruct((1, D), F32)],
        compiler_params=_cp(("arbitrary",)), name=name)(dout, xhat, rstd, g)


FFN_CHUNKS = N_DEV // 2


def ffn_fwd(h, w_in, w_out, g, b, name, hosted=None):
    T, D = h.shape
    fc = w_in.shape[2]
    half = w_out.shape[1]
    tm = _pick(T, 512, 8)
    nc = FFN_CHUNKS

    def body(h_ref, wg_ref, wu_ref, wo_ref, g_ref, b_ref, out_ref, xh_ref, rs_ref, G_ref, U_ref, ob_ref, acc_ref):
        c = pl.program_id(1)

        @pl.when(c == 0)
        def _():
            acc_ref[...] = jnp.zeros_like(acc_ref)

        hb = h_ref[...].astype(BF16)
        G = jnp.dot(hb, wg_ref[0], preferred_element_type=F32)
        U = jnp.dot(hb, wu_ref[0], preferred_element_type=F32)
        G_ref[0] = G
        U_ref[0] = U
        act = G * _sigmoid(G) * U
        acc_ref[...] += _bdot(act, wo_ref[...].reshape(2 * half, D))

        @pl.when(c == nc - 1)
        def _():
            z = ALPHA * h_ref[...] + 0.5 * acc_ref[...]
            out, xh, rs = _ln_apply(z, g_ref[...], b_ref[...])
            out_ref[...] = out
            ob_ref[...] = out.astype(BF16)
            xh_ref[...] = xh
            rs_ref[...] = rs

    row = pl.BlockSpec((tm, D), lambda i, c: (i, 0))
    vec = pl.BlockSpec((1, D), lambda i, c: (0, 0))
    cblk = pl.BlockSpec((1, tm, fc), lambda i, c: (c, i, 0))
    csds = jax.ShapeDtypeStruct((nc, T, fc), F32)
    return _hosted_call(
        hosted, body, grid=(T // tm, nc),
        in_specs=[row, pl.BlockSpec((1, D, fc), lambda i, c: (c, 0, 0)),
                  pl.BlockSpec((1, D, fc), lambda i, c: (c + nc, 0, 0)),
                  pl.BlockSpec((2, half, D), lambda i, c: (c, 0, 0)), vec, vec],
        out_specs=[row, row, pl.BlockSpec((tm, 1), lambda i, c: (i, 0)), cblk, cblk, row],
        out_shape=[jax.ShapeDtypeStruct((T, D), F32), jax.ShapeDtypeStruct((T, D), F32), jax.ShapeDtypeStruct((T, 1), F32),
                   csds, csds, jax.ShapeDtypeStruct((T, D), BF16)],
        scratch_shapes=[pltpu.VMEM((tm, D), F32)],
        compiler_params=_cp(("parallel", "arbitrary")), name=name)(h, w_in, w_in, w_out, g, b)


def ffn_bwd(dz, G, U, w_in, w_out, name, hosted=None):
    T, D = dz.shape
    nc, _, fc = G.shape
    half = w_out.shape[1]
    tm = _pick(T, 512, 16)

    def body(dz_ref, G_ref, U_ref, wg_ref, wu_ref, wo_ref, dh_ref, dG_ref, dU_ref, act_ref, acc_ref):
        c = pl.program_id(1)

        @pl.when(c == 0)
        def _():
            acc_ref[...] = jnp.zeros_like(acc_ref)

        dy = (0.5 * dz_ref[...]).astype(BF16)
        dact = _bdot_nt(dy, wo_ref[...].reshape(2 * half, D))
        G = G_ref[0]
        U = U_ref[0]
        s = _sigmoid(G)
        silu = G * s
        dG = (dact * U * (s * (1.0 + G * (1.0 - s)))).astype(BF16)
        dU = (dact * silu).astype(BF16)
        dG_ref[0] = dG
        dU_ref[0] = dU
        act_ref[0] = (silu * U).astype(BF16)
        acc_ref[...] += _bdot_nt(dG, wg_ref[0]) + _bdot_nt(dU, wu_ref[0])

        @pl.when(c == nc - 1)
        def _():
            dh_ref[...] = ALPHA * dz_ref[...] + acc_ref[...]

    row = pl.BlockSpec((tm, D), lambda i, c: (i, 0))
    cblk = pl.BlockSpec((1, tm, fc), lambda i, c: (c, i, 0))
    csds = jax.ShapeDtypeStruct((nc, T, fc), BF16)
    return _hosted_call(
        hosted, body, grid=(T // tm, nc),
        in_specs=[row, cblk, cblk, pl.BlockSpec((1, D, fc), lambda i, c: (c, 0, 0)),
                  pl.BlockSpec((1, D, fc), lambda i, c: (c + nc, 0, 0)),
                  pl.BlockSpec((2, half, D), lambda i, c: (c, 0, 0))],
        out_specs=[row, cblk, cblk, cblk],
        out_shape=[jax.ShapeDtypeStruct((T, D), F32), csds, csds, csds],
        scratch_shapes=[pltpu.VMEM((tm, D), F32)],
        compiler_params=_cp(("parallel", "arbitrary")), name=name)(dz, G, U, w_in, w_in, w_out)


def ffn_dw_in(h_t, dG, dU, name, hosted=None):
    D, T = h_t.shape
    nc, _, fc = dG.shape
    tt = _pick(T, 512, LANES)
    nt = T // tt

    def body(h_ref, dG_ref, dU_ref, o_ref, acc_ref):
        s = pl.program_id(0)
        t = pl.program_id(1)

        @pl.when(t == 0)
        def _():
            acc_ref[...] = jnp.zeros_like(acc_ref)

        hb = h_ref[:, pl.ds(pl.multiple_of(t * tt, tt), tt)]

        @pl.when(s < nc)
        def _():
            acc_ref[...] += jnp.dot(hb, dG_ref[0], preferred_element_type=F32)

        @pl.when(s >= nc)
        def _():
            acc_ref[...] += jnp.dot(hb, dU_ref[0], preferred_element_type=F32)

        @pl.when(t == nt - 1)
        def _():
            o_ref[0] = acc_ref[...].astype(o_ref.dtype)

    return _hosted_call(
        hosted, body, grid=(2 * nc, nt),
        in_specs=[pl.BlockSpec((D, T), lambda s, t: (0, 0)),
                  pl.BlockSpec((1, tt, fc), lambda s, t: (jnp.minimum(s, nc - 1), jnp.where(s < nc, t, nt - 1), 0)),
                  pl.BlockSpec((1, tt, fc), lambda s, t: (jnp.maximum(s - nc, 0), jnp.where(s >= nc, t, 0), 0))],
        out_specs=pl.BlockSpec((1, D, fc), lambda s, t: (s, 0, 0)),
        out_shape=jax.ShapeDtypeStruct((2 * nc, D, fc), BF16),
        scratch_shapes=[pltpu.VMEM((D, fc), F32)],
        compiler_params=_cp(("parallel", "arbitrary")), name=name)(h_t, dG, dU)


def ffn_dw_out(act, dz, name, hosted=None):
    nc, T, fc = act.shape
    D = dz.shape[1]
    half = fc // 2
    tt = _pick(T, 512, 16)
    nt = T // tt

    def body(a_ref, dz_ref, o_ref, acc_ref):
        t = pl.program_id(1)

        @pl.when(t == 0)
        def _():
            acc_ref[...] = jnp.zeros_like(acc_ref)

        acc_ref[...] += _bdot_tn(a_ref[0], dz_ref[...])

        @pl.when(t == nt - 1)
        def _():
            o_ref[...] = (0.5 * acc_ref[...]).reshape(2, half, D).astype(o_ref.dtype)

    return _hosted_call(
        hosted, body, grid=(nc, nt),
        in_specs=[pl.BlockSpec((1, tt, fc), lambda c, t: (c, t, 0)), pl.BlockSpec((tt, D), lambda c, t: (t, 0))],
        out_specs=pl.BlockSpec((2, half, D), lambda c, t: (c, 0, 0)),
        out_shape=jax.ShapeDtypeStruct((2 * nc, half, D), BF16),
        scratch_shapes=[pltpu.VMEM((fc, D), F32)],
        compiler_params=_cp(("parallel", "arbitrary")), name=name)(act, dz)


def proj_res_ln(parts, w, res, g, b, name):
    T, D = res.shape
    tm = _pick(T, 512, 8)
    widths = [p.shape[1] for p in parts]
    offs = [int(sum(widths[:i])) for i in range(len(parts))]
    n = len(parts)

    def body(*refs):
        p_refs = refs[:n]
        w_ref, r_ref, g_ref, b_ref, out_ref, xh_ref, rs_ref, ob_ref = refs[n:]
        acc = ALPHA * r_ref[...]
        for p_ref, o, wd in zip(p_refs, offs, widths):
            acc = acc + _bdot(p_ref[...], w_ref[o:o + wd, :])
        out, xh, rs = _ln_apply(acc, g_ref[...], b_ref[...])
        out_ref[...] = out
        ob_ref[...] = out.astype(BF16)
        xh_ref[...] = xh
        rs_ref[...] = rs

    row = pl.BlockSpec((tm, D), lambda i: (i, 0))
    vec = pl.BlockSpec((1, D), lambda i: (0, 0))
    return pl.pallas_call(
        body, grid=(T // tm,),
        in_specs=[pl.BlockSpec((tm, wd), lambda i: (i, 0)) for wd in widths]
        + [pl.BlockSpec(w.shape, lambda i: (0, 0)), row, vec, vec],
        out_specs=[row, row, pl.BlockSpec((tm, 1), lambda i: (i, 0)), row],
        out_shape=[jax.ShapeDtypeStruct((T, D), F32), jax.ShapeDtypeStruct((T, D), F32), jax.ShapeDtypeStruct((T, 1), F32),
                   jax.ShapeDtypeStruct((T, D), BF16)],
        compiler_params=_cp(("parallel",)), name=name)(*parts, w, res, g, b)


def ple_fwd(h, p, wg, wp, name):
    T, D = h.shape
    P = p.shape[1]
    tm, tn = _pick(T, 512, 8), _pick(D, 512, LANES)

    def body(h_ref, hn_ref, p_ref, wg_ref, wp_ref, out_ref, a_ref, e_ref):
        a = _bdot(h_ref[...], wg_ref[...])
        e = _bdot(p_ref[...], wp_ref[...])
        a_ref[...] = a
        e_ref[...] = e
        out_ref[...] = hn_ref[...] + _sigmoid(a) * e

    blk = pl.BlockSpec((tm, tn), lambda i, j: (i, j))
    sds = jax.ShapeDtypeStruct((T, D), F32)
    return pl.pallas_call(
        body, grid=(T // tm, D // tn),
        in_specs=[pl.BlockSpec((tm, D), lambda i, j: (i, 0)), blk, pl.BlockSpec((tm, P), lambda i, j: (i, 0)),
                  pl.BlockSpec((D, tn), lambda i, j: (0, j)), pl.BlockSpec((P, tn), lambda i, j: (0, j))],
        out_specs=[blk, blk, blk], out_shape=[sds, sds, sds],
        compiler_params=_cp(("parallel", "parallel")), name=name)(h, h, p, wg, wp)


def ple_bwd(dout, a, e, wg, name):
    T, D = dout.shape
    tm = _pick(T, 512, 16)

    def body(do_ref, a_ref, e_ref, wg_ref, dh_ref, da_ref, de_ref):
        do = do_ref[...]
        s = _sigmoid(a_ref[...])
        da = (do * e_ref[...] * s * (1.0 - s)).astype(BF16)
        da_ref[...] = da
        de_ref[...] = (do * s).astype(BF16)
        dh_ref[...] = do + _bdot_nt(da, wg_ref[...])

    row = pl.BlockSpec((tm, D), lambda i: (i, 0))
    return pl.pallas_call(
        body, grid=(T // tm,),
        in_specs=[row, row, row, pl.BlockSpec((D, D), lambda i: (0, 0))],
        out_specs=[row, row, row],
        out_shape=[jax.ShapeDtypeStruct((T, D), F32), jax.ShapeDtypeStruct((T, D), BF16), jax.ShapeDtypeStruct((T, D), BF16)],
        compiler_params=_cp(("parallel",)), name=name)(dout, a, e, wg)


def loss_head(y, target, name):
    T, D = y.shape
    tm = _pick(T, 512, 8)

    def body(y_ref, t_ref, loss_ref, dy_ref):
        i = pl.program_id(0)

        @pl.when(i == 0)
        def _():
            loss_ref[...] = jnp.zeros_like(loss_ref)

        err = y_ref[...] - t_ref[...]
        dy_ref[...] = err * (1.0 / D)
        per_tok = jnp.sum(err * err, axis=-1, keepdims=True) * (1.0 / D)
        loss_ref[...] += 0.5 * jnp.sum(per_tok, axis=0, keepdims=True)

    row = pl.BlockSpec((tm, D), lambda i: (i, 0))
    return pl.pallas_call(
        body, grid=(T // tm,), in_specs=[row, row],
        out_specs=[pl.BlockSpec((1, 1), lambda i: (0, 0)), row],
        out_shape=[jax.ShapeDtypeStruct((1, 1), F32), jax.ShapeDtypeStruct((T, D), F32)],
        compiler_params=_cp(("arbitrary",)), name=name)(y, target)


HALO = 8


def _conv_taps(pad_ref, w_ref, tm, base):
    acc = w_ref[0:1, :] * pad_ref[pl.ds(base, tm), :]
    for k in range(1, GDN_CONV):
        acc = acc + w_ref[k:k + 1, :] * pad_ref[pl.ds(base + k, tm), :]
    return acc


GDN_GROUP_W = GDN_W
GDN_PRE_ROWS = 512


def _head_segments():
    head = jnp.arange(GDN_W, dtype=jnp.int32) // GDN_D
    return (head[:, None] == head[None, :]).astype(BF16)


def _head_sums(x, seg):
    hi = x.astype(BF16)
    r1 = x - hi.astype(F32)
    mid = r1.astype(BF16)
    lo = (r1 - mid.astype(F32)).astype(BF16)
    d = functools.partial(jnp.dot, preferred_element_type=F32)
    return d(hi, seg) + d(mid, seg) + d(lo, seg)


def _gdn_pre_common(x_ref, halo_ref, w_ref, seg_ref, pad_ref, tm):
    i = pl.program_id(1)
    grp = pl.program_id(0)
    pad_ref[0:HALO, :] = jnp.where(i == 0, 0.0, halo_ref[...])
    pad_ref[HALO:HALO + tm, :] = x_ref[...]
    c = _conv_taps(pad_ref, w_ref, tm, HALO - (GDN_CONV - 1))
    s = _sigmoid(c)
    y = c * s
    r = lax.rsqrt(_head_sums(y * y, seg_ref[...]) + RMS_EPS)
    scale = jnp.where(grp < 1, GDN_D ** -0.5, 1.0)
    return grp < 2, c, s, y, r, scale


def gdn_pre_fwd(proj, conv_w, name):
    T = proj.shape[0]
    tm = _pick(T, GDN_PRE_ROWS, 8)
    GW = GDN_GROUP_W

    def body(x_ref, halo_ref, w_ref, seg_ref, o_ref, pad_ref):
        normed, c, s, y, r, scale = _gdn_pre_common(x_ref, halo_ref, w_ref, seg_ref, pad_ref, tm)
        o_ref[...] = jnp.where(normed, y * r * scale, y)

    return pl.pallas_call(
        body, grid=(3, T // tm),
        in_specs=[pl.BlockSpec((tm, GW), lambda hb, i: (i, hb)),
                  pl.BlockSpec((HALO, GW), lambda hb, i: (jnp.maximum(i * (tm // HALO) - 1, 0), hb)),
                  pl.BlockSpec((GDN_CONV, GW), lambda hb, i: (0, hb)), pl.BlockSpec((GW, GW), lambda hb, i: (0, 0))],
        out_specs=pl.BlockSpec((tm, GW), lambda hb, i: (i, hb)),
        out_shape=jax.ShapeDtypeStruct((T, 3 * GW), F32),
        scratch_shapes=[pltpu.VMEM((tm + HALO, GW), F32)],
        compiler_params=_cp(("parallel", "parallel")), name=name)(proj, proj, conv_w, _head_segments())


def gdn_pre_bwd_pointwise(proj, conv_w, dqkv, name, hosted=None):
    T = proj.shape[0]
    tm = _pick(T, GDN_PRE_ROWS, 8)
    GW = GDN_GROUP_W

    def body(x_ref, halo_ref, w_ref, seg_ref, d_ref, dc_ref, dw_ref, pad_ref):
        i = pl.program_id(1)
        normed, c, s, y, r, scale = _gdn_pre_common(x_ref, halo_ref, w_ref, seg_ref, pad_ref, tm)

        @pl.when(i == 0)
        def _():
            dw_ref[...] = jnp.zeros_like(dw_ref)

        d = d_ref[...]
        n = y * r
        dn = d * scale
        dy = jnp.where(normed, r * (dn - n * _head_sums(dn * n, seg_ref[...])), d)
        dc = dy * (s * (1.0 + c * (1.0 - s)))
        dc_ref[...] = dc
        for k in range(GDN_CONV):
            xs = pad_ref[pl.ds(HALO - (GDN_CONV - 1) + k, tm), :]
            dw_ref[k:k + 1, :] += jnp.sum(dc * xs, axis=0, keepdims=True)

    blk = pl.BlockSpec((tm, GW), lambda hb, i: (i, hb))
    wblk = pl.BlockSpec((GDN_CONV, GW), lambda hb, i: (0, hb))
    return _hosted_call(
        hosted, body, grid=(3, T // tm),
        in_specs=[blk, pl.BlockSpec((HALO, GW), lambda hb, i: (jnp.maximum(i * (tm // HALO) - 1, 0), hb)), wblk,
                  pl.BlockSpec((GW, GW), lambda hb, i: (0, 0)), blk],
        out_specs=[blk, wblk],
        out_shape=[jax.ShapeDtypeStruct((T, 3 * GW), F32), jax.ShapeDtypeStruct((GDN_CONV, 3 * GW), F32)],
        scratch_shapes=[pltpu.VMEM((tm + HALO, GW), F32)],
        compiler_params=_cp(("parallel", "arbitrary")), name=name)(proj, proj, conv_w, _head_segments(), dqkv)


def gdn_pre_bwd_conv(dc, conv_w_p, name):
    T = dc.shape[0]
    tm = _pick(T, GDN_PRE_ROWS, 8)
    nt = T // tm
    GW = GDN_GROUP_W

    def body(dc_ref, halo_ref, w_ref, dx_ref, pad_ref):
        i = pl.program_id(1)
        pad_ref[0:tm, :] = dc_ref[...]
        pad_ref[tm:tm + HALO, :] = jnp.where(i == nt - 1, 0.0, halo_ref[...])
        acc = w_ref[GDN_CONV - 1:GDN_CONV, :] * pad_ref[pl.ds(0, tm), :]
        for k in range(GDN_CONV - 1):
            acc = acc + w_ref[k:k + 1, :] * pad_ref[pl.ds(GDN_CONV - 1 - k, tm), :]
        dx_ref[...] = acc

    blk = pl.BlockSpec((tm, GW), lambda hb, i: (i, hb))
    return pl.pallas_call(
        body, grid=(3, nt),
        in_specs=[blk, pl.BlockSpec((HALO, GW), lambda hb, i: (jnp.minimum((i + 1) * (tm // HALO), T // HALO - 1), hb)),
                  pl.BlockSpec((GDN_CONV, GW), lambda hb, i: (0, hb))],
        out_specs=blk,
        out_shape=jax.ShapeDtypeStruct((T, 3 * GW), F32),
        scratch_shapes=[pltpu.VMEM((tm + HALO, GW), F32)],
        compiler_params=_cp(("parallel", "parallel")), name=name)(dc, dc, conv_w_p)


def _chunk_masks(C):
    row = _iota2((C, C), 0)
    col = _iota2((C, C), 1)
    return row >= col, row > col, row == col


BNN = (((2,), (1,)), ((0,), (0,)))
BNT = (((2,), (2,)), ((0,), (0,)))
BTN = (((1,), (1,)), ((0,), (0,)))


def _bmm(a, b, dims=BNN):
    return lax.dot_general(a.astype(BF16), b.astype(BF16), dims, preferred_element_type=F32)


def _hbmm(a, b):
    m = a.shape[1]
    a_hi, a_lo = _split2(a)
    b_hi, b_lo = _split2(b)
    r = lax.dot_general(jnp.concatenate([a_hi, a_lo], axis=1), b_hi, BNN, preferred_element_type=F32)
    return r[:, :m] + r[:, m:] + lax.dot_general(a_hi, b_lo, BNN, preferred_element_type=F32)


def _hbmm_tn(a, b):
    a_hi, a_lo = _split2(a)
    b_hi, b_lo = _split2(b)
    d = functools.partial(lax.dot_general, dimension_numbers=BTN, preferred_element_type=F32)
    return d(a_hi, b_hi) + d(a_lo, b_hi) + d(a_hi, b_lo)


def _col_to_row(colv, eye):
    return jnp.sum(jnp.where(eye, colv, 0.0), axis=1, keepdims=True)


def _row_to_col(rowv, eye):
    return jnp.sum(jnp.where(eye, rowv, 0.0), axis=2, keepdims=True)


def _unit_lower_inverse(A, eye):
    C = A.shape[1]
    P = jnp.where(eye, 1.0, 0.0) - A
    Bp = _hbmm(A, A)
    for _ in range(4):
        R = _hbmm(jnp.concatenate([Bp, P], axis=1), Bp)
        Bp = R[:, :C]
        P = P + R[:, C:]
    return P + _hbmm(P, Bp)


def _stack_heads(ref, first_head, n):
    return jnp.stack([ref[:, pl.ds((first_head + h) * GDN_D, GDN_D)] for h in range(n)])


def _unstack_heads(ref, first_head, val):
    for h in range(val.shape[0]):
        ref[:, pl.ds((first_head + h) * GDN_D, GDN_D)] = val[h]


def _gdn_gates(gab, a_row, dt_row, incl):
    g_all = -jnp.exp(a_row) * _softplus(gab + dt_row)
    beta_all = _sigmoid(gab)
    gc_all = _ones_dot_left(incl.astype(BF16), g_all)
    return g_all, beta_all, gc_all


def _gdn_common(qkv_ref, gc_all, beta_all, incl, strict, eye):
    C, H = GDN_CHUNK, GDN_HEADS
    q, k, v = (_stack_heads(qkv_ref, j * H, H) for j in range(3))
    gc = jnp.stack([gc_all[:, h:h + 1] for h in range(H)])
    beta = jnp.stack([beta_all[:, H + h:H + h + 1] for h in range(H)])
    gc_row = _col_to_row(gc, eye)
    decay = jnp.where(incl, jnp.exp(jnp.where(incl, gc - gc_row, 0.0)), 0.0)
    e_gc = jnp.exp(gc)
    gl = gc[:, C - 1:C, :]
    e_gl = jnp.exp(gl)
    ekd = jnp.exp(gl - gc)
    kb = k * beta
    A = jnp.where(strict, _bmm(kb, k, BNT) * decay, 0.0)
    Pm = jnp.where(incl, _bmm(q, k, BNT) * decay, 0.0)
    return q, k, v, gc, beta, decay, e_gc, e_gl, ekd, kb, A, Pm


def gdn_chunk_fwd(qkv, proj, a_row, dt_row, norm_w, name, hosted=None):
    T = qkv.shape[0]
    C, H, Dh = GDN_CHUNK, GDN_HEADS, GDN_D
    N = T // C

    def body(qkv_ref, gz_ref, gab_ref, a_ref, dt_ref, nw_ref, o_ref, opre_ref, Tm_ref, Sin_ref, S_ref):
        n = pl.program_id(0)

        @pl.when(n == 0)
        def _():
            S_ref[...] = jnp.zeros_like(S_ref)

        incl, strict, eye = _chunk_masks(C)
        _, beta_all, gc_all = _gdn_gates(gab_ref[...], a_ref[...], dt_ref[...], incl)
        q, k, v, gc, beta, decay, e_gc, e_gl, ekd, kb, A, Pm = _gdn_common(qkv_ref, gc_all, beta_all, incl, strict, eye)
        Tm = _unit_lower_inverse(A, eye)
        u = _hbmm(Tm, v * beta)
        w = _hbmm(Tm, kb * e_gc)
        S = S_ref[...]
        v_new = u - _bmm(w, S)
        o = _bmm(q * e_gc, S) + _bmm(Pm, v_new)
        S_ref[...] = S * e_gl + _bmm(k * ekd, v_new, BTN)
        Sin_ref[0] = S
        Tm_ref[0] = Tm
        r = lax.rsqrt(jnp.mean(o * o, axis=-1, keepdims=True) + RMS_EPS)
        gz = _stack_heads(gz_ref, 0, H)
        _unstack_heads(opre_ref, 0, o)
        _unstack_heads(o_ref, 0, o * r * nw_ref[...] * (gz * _sigmoid(gz)))

    vec = pl.BlockSpec((1, LANES), lambda n: (0, 0))
    hblk = pl.BlockSpec((C, GDN_W), lambda n: (n, 0))
    sblk = pl.BlockSpec((1, H, Dh, Dh), lambda n: (n, 0, 0, 0))
    return _hosted_call(
        hosted, body, grid=(N,),
        in_specs=[pl.BlockSpec((C, 3 * GDN_W), lambda n: (n, 0)),
                  pl.BlockSpec((C, GDN_W), lambda n: (n, CB_GZ * LANES // GDN_W)),
                  pl.BlockSpec((C, LANES), lambda n: (n, CB_GAB)), vec, vec, pl.BlockSpec((1, Dh), lambda n: (0, 0))],
        out_specs=[hblk, hblk, sblk, sblk],
        out_shape=[jax.ShapeDtypeStruct((T, GDN_W), F32), jax.ShapeDtypeStruct((T, GDN_W), F32),
                   jax.ShapeDtypeStruct((N, H, Dh, Dh), F32), jax.ShapeDtypeStruct((N, H, Dh, Dh), F32)],
        scratch_shapes=[pltpu.VMEM((H, Dh, Dh), F32)],
        compiler_params=_cp(("arbitrary",)), name=name)(qkv, proj, proj, a_row, dt_row, norm_w)


def gdn_chunk_bwd(qkv, proj, a_row, dt_row, norm_w, opre, Tm_all, Sin_all, docat, name, hosted=None):
    T = qkv.shape[0]
    C, H, Dh = GDN_CHUNK, GDN_HEADS, GDN_D
    N = T // C

    def body(qkv_ref, gz_ref, gab_ref, a_ref, dt_ref, nw_ref, opre_ref, Tm_ref, Sin_ref, do_ref,
             dqkv_ref, dgz_ref, dgab_ref, da_ref, ddt_ref, dnw_ref, dS_ref):
        n = pl.program_id(0)

        @pl.when(n == 0)
        def _():
            dS_ref[...] = jnp.zeros_like(dS_ref)
            da_ref[...] = jnp.zeros_like(da_ref)
            ddt_ref[...] = jnp.zeros_like(ddt_ref)
            dnw_ref[...] = jnp.zeros_like(dnw_ref)

        incl, strict, eye = _chunk_masks(C)
        gab = gab_ref[...]
        g_all, beta_all, gc_all = _gdn_gates(gab, a_ref[...], dt_ref[...], incl)
        lane = _iota2((C, LANES), 1)
        rowi = _iota2((C, 1), 0)
        nw = nw_ref[...]
        q, k, v, gc, beta, decay, e_gc, e_gl, ekd, kb, A, Pm = _gdn_common(qkv_ref, gc_all, beta_all, incl, strict, eye)
        Tm = Tm_ref[0]
        S = Sin_ref[0]
        dS = dS_ref[...]
        kbe = kb * e_gc
        u = _hbmm(Tm, v * beta)
        w = _hbmm(Tm, kbe)
        qd = q * e_gc
        kd = k * ekd
        v_new = u - _bmm(w, S)
        o = _stack_heads(opre_ref, 0, H)
        gz = _stack_heads(gz_ref, 0, H)
        don = _stack_heads(do_ref, 0, H)
        r = lax.rsqrt(jnp.mean(o * o, axis=-1, keepdims=True) + RMS_EPS)
        nn = o * r
        sgz = _sigmoid(gz)
        silu = gz * sgz
        _unstack_heads(dgz_ref, 0, don * nn * nw * (sgz * (1.0 + gz * (1.0 - sgz))))
        dnn = don * nw * silu
        dnw_ref[...] += jnp.sum(jnp.sum(don * nn * silu, axis=0), axis=0, keepdims=True)
        do = r * (dnn - nn * jnp.mean(dnn * nn, axis=-1, keepdims=True))
        dv_new = _bmm(Pm, do, BTN) + _bmm(kd, dS)
        dPm = jnp.where(incl, _bmm(do, v_new, BNT), 0.0)
        dqd = _bmm(do, S, BNT)
        dkd = _bmm(v_new, dS, BNT)
        dS_ref[...] = _bmm(qd, do, BTN) + e_gl * dS - _bmm(w, dv_new, BTN)
        dgl = jnp.sum(jnp.sum(dS * S, axis=2, keepdims=True), axis=1, keepdims=True) * e_gl
        dw = -_bmm(dv_new, S, BNT)
        dvb = _hbmm_tn(Tm, dv_new)
        dkbe = _hbmm_tn(Tm, dw)
        dA = -jnp.where(strict, _bmm(dvb, u, BNT) + _bmm(dkbe, w, BNT), 0.0)
        dAD = dA * decay
        dPD = dPm * decay
        Gm = dA * A + dPm * Pm
        dgc = jnp.sum(Gm, axis=2, keepdims=True) - _row_to_col(jnp.sum(Gm, axis=1, keepdims=True), eye)
        dkb = _bmm(dAD, k) + dkbe * e_gc
        dk = _bmm(dAD, kb, BTN) + _bmm(dPD, q, BTN) + dkd * ekd + dkb * beta
        dq = _bmm(dPD, k) + dqd * e_gc
        tkd = jnp.sum(dkd * kd, axis=-1, keepdims=True)
        dgc = dgc + jnp.sum(dqd * qd, axis=-1, keepdims=True) - tkd + jnp.sum(dkbe * kbe, axis=-1, keepdims=True)
        dgl = dgl + jnp.sum(tkd, axis=1, keepdims=True)
        dgc = dgc + jnp.where(rowi == C - 1, dgl, 0.0)
        dbeta = jnp.sum(dvb * v, axis=-1, keepdims=True) + jnp.sum(dkb * k, axis=-1, keepdims=True)
        _unstack_heads(dqkv_ref, 0, dq)
        _unstack_heads(dqkv_ref, H, dk)
        _unstack_heads(dqkv_ref, 2 * H, dvb * beta)
        dgc_all = jnp.zeros((C, LANES), F32)
        dbeta_all = jnp.zeros((C, LANES), F32)
        for h in range(H):
            dgc_all = dgc_all + jnp.where(lane == h, dgc[h], 0.0)
            dbeta_all = dbeta_all + jnp.where(lane == H + h, dbeta[h], 0.0)
        upper = (_iota2((C, C), 0) <= _iota2((C, C), 1)).astype(BF16)
        dg_all = _ones_dot_left(upper, dgc_all)
        dga = dg_all * (-jnp.exp(a_ref[...])) * _sigmoid(gab + dt_ref[...])
        dgb = dbeta_all * beta_all * (1.0 - beta_all)
        dgab_ref[...] = jnp.where(lane < H, dga, jnp.where(lane < 2 * H, dgb, 0.0))
        da_ref[...] += jnp.sum(jnp.where(lane < H, dg_all * g_all, 0.0), axis=0, keepdims=True)
        ddt_ref[...] += jnp.sum(jnp.where(lane < H, dga, 0.0), axis=0, keepdims=True)

    rev = lambda n: N - 1 - n
    vec = pl.BlockSpec((1, LANES), lambda n: (0, 0))
    nwv = pl.BlockSpec((1, Dh), lambda n: (0, 0))
    hblk = pl.BlockSpec((C, GDN_W), lambda n: (rev(n), 0))
    sblk = pl.BlockSpec((1, H, Dh, Dh), lambda n: (rev(n), 0, 0, 0))
    qblk = pl.BlockSpec((C, 3 * GDN_W), lambda n: (rev(n), 0))
    return _hosted_call(
        hosted, body, grid=(N,),
        in_specs=[qblk, pl.BlockSpec((C, GDN_W), lambda n: (rev(n), CB_GZ * LANES // GDN_W)),
                  pl.BlockSpec((C, LANES), lambda n: (rev(n), CB_GAB)), vec, vec, nwv, hblk, sblk, sblk, hblk],
        out_specs=[qblk, hblk, pl.BlockSpec((C, LANES), lambda n: (rev(n), 0)), vec, vec, nwv],
        out_shape=[jax.ShapeDtypeStruct((T, 3 * GDN_W), F32), jax.ShapeDtypeStruct((T, GDN_W), F32),
                   jax.ShapeDtypeStruct((T, LANES), F32), jax.ShapeDtypeStruct((1, LANES), F32),
                   jax.ShapeDtypeStruct((1, LANES), F32), jax.ShapeDtypeStruct((1, Dh), F32)],
        scratch_shapes=[pltpu.VMEM((H, Dh, Dh), F32)],
        compiler_params=_cp(("arbitrary",)), name=name)(qkv, proj, proj, a_row, dt_row, norm_w, opre, Tm_all, Sin_all, docat)


ATT_BQ, ATT_BK = 256, 512
NEG_BIG = -1e30


def _att_blocks(T):
    bq, bk = min(ATT_BQ, T), min(ATT_BK, T)
    assert bk % bq == 0 and T % bk == 0
    return bq, bk


def _att_specs(T, bq, cbs):
    qspec = lambda cb: pl.BlockSpec((bq, LANES), lambda h, i: (i, cb + h))
    kspec = lambda cb: pl.BlockSpec((T, LANES), lambda h, i: (0, cb + h))
    return qspec, kspec


def _kblock(ref, kb, bk):
    return ref[pl.ds(pl.multiple_of(kb * bk, bk), bk), :]


def _att_pos(i, kb, bq, bk):
    qpos = i * bq + _iota2((bq, bk), 0)
    kpos = kb * bk + _iota2((bq, bk), 1)
    return qpos, kpos


def _later_keys(n):
    return (_iota2((n, n), 0) > _iota2((n, n), 1)).astype(BF16)


def _earlier_keys(n):
    return (_iota2((n, n), 0) < _iota2((n, n), 1)).astype(BF16)


def _tri_dot(x, tri, terms):
    acc, rest = None, x
    for t in range(terms):
        part = rest.astype(BF16)
        if t + 1 < terms:
            rest = rest - part.astype(F32)
        d = jnp.dot(part, tri, preferred_element_type=F32)
        acc = d if acc is None else acc + d
    return acc


SB_BLOCK = 256
SB_DEAD = -104.0


def _sb_blocks(T):
    b = min(SB_BLOCK, T)
    assert T % b == 0 and T // b <= LANES
    return b, b


def sb_fwd(proj, name, hosted=None):
    T = proj.shape[0]
    H = SB_HEADS
    bq, bk = _sb_blocks(T)
    scale = SB_DIM ** -0.5

    def body(q_ref, k_ref, v_ref, o_ref, tot_ref):
        i = pl.program_id(1)
        qb = q_ref[...].astype(BF16)
        diag = (i * bq) // bk
        lane = _iota2((bq, LANES), 1)
        later = _later_keys(bk)

        def block(kb, acc, R, masked):
            z = _bdot_nt(qb, _kblock(k_ref, kb, bk)) * scale
            sp = _softplus(z)
            if masked:
                qpos, kpos = _att_pos(i, kb, bq, bk)
                mask = kpos < qpos
                l1m = jnp.where(mask, -sp, 0.0)
            else:
                l1m = -sp
            W = jnp.exp((z - sp) + _tri_dot(l1m, later, 3) + R)
            if masked:
                W = jnp.where(mask, W, 0.0)
            acc = acc + _bdot(W, _kblock(v_ref, kb, bk))
            return acc, R + jnp.sum(l1m, axis=-1, keepdims=True)

        acc, R = block(diag, jnp.zeros((bq, LANES), F32), jnp.zeros((bq, 1), F32), True)

        def live(c):
            return jnp.logical_and(c[0] >= 0, jnp.max(c[2]) > SB_DEAD)

        def step(c):
            kb, acc, R, Rb = c
            acc, R_next = block(kb, acc, R, False)
            return kb - 1, acc, R_next, jnp.where(lane == kb, R, Rb)

        _, acc, _, Rb = lax.while_loop(live, step, (diag - 1, acc, R, jnp.where(lane == diag, 0.0, NEG_BIG)))
        o_ref[...] = acc
        tot_ref[...] = Rb

    qspec, kspec = _att_specs(T, bq, None)
    sds = jax.ShapeDtypeStruct((T, H * LANES), F32)
    oblk = pl.BlockSpec((bq, LANES), lambda h, i: (i, h))
    return _hosted_call(
        hosted, body, grid=(H, T // bq), in_specs=[qspec(CB_SQ), kspec(CB_SK), kspec(CB_SV)],
        out_specs=[oblk, oblk], out_shape=[sds, sds],
        compiler_params=_cp(("parallel", "parallel")), name=name)(proj, proj, proj)


def sb_bwd(proj, tot, docat, do_cb, name):
    T = proj.shape[0]
    H = SB_HEADS
    bq, bk = _sb_blocks(T)
    scale = SB_DIM ** -0.5

    def body(q_ref, k_ref, v_ref, tot_ref, do_ref, dq_ref, dk_ref, dv_ref):
        i = pl.program_id(1)

        @pl.when(i == 0)
        def _():
            dk_ref[...] = jnp.zeros_like(dk_ref)
            dv_ref[...] = jnp.zeros_like(dv_ref)

        qb = q_ref[...].astype(BF16)
        dob = do_ref[...].astype(BF16)
        Rb = tot_ref[...]
        diag = (i * bq) // bk
        lane = _iota2((bq, LANES), 1)
        later, earlier = _later_keys(bk), _earlier_keys(bk)
        first = lax.while_loop(
            lambda kb: jnp.logical_and(kb < diag, jnp.max(jnp.where(lane == kb, Rb, NEG_BIG)) <= SB_DEAD),
            lambda kb: kb + 1, jnp.int32(0))

        def block(kb, carry, masked):
            dq, Epre = carry
            R = jnp.sum(jnp.where(lane == kb, Rb, 0.0), axis=1, keepdims=True)
            kblk = _kblock(k_ref, kb, bk).astype(BF16)
            z = _bdot_nt(qb, kblk) * scale
            sp = _softplus(z)
            if masked:
                qpos, kpos = _att_pos(i, kb, bq, bk)
                mask = kpos < qpos
                l1m = jnp.where(mask, -sp, 0.0)
            else:
                l1m = -sp
            W = jnp.exp((z - sp) + _tri_dot(l1m, later, 3) + R)
            if masked:
                W = jnp.where(mask, W, 0.0)
            E = _bdot_nt(dob, _kblock(v_ref, kb, bk)) * W
            cexcl = _tri_dot(E, earlier, 3) + Epre
            neg = jnp.exp(-sp)
            dz = E * neg - cexcl * (1.0 - neg)
            if masked:
                dz = jnp.where(mask, dz, 0.0)
            dz = (dz * scale).astype(BF16)
            rows = pl.ds(pl.multiple_of(kb * bk, bk), bk)
            dk_ref[rows, :] += lax.dot_general(dz, qb, TN_DIMS, preferred_element_type=F32)
            dv_ref[rows, :] += lax.dot_general(W.astype(BF16), dob, TN_DIMS, preferred_element_type=F32)
            dq = dq + jnp.dot(dz, kblk, preferred_element_type=F32)
            return dq, Epre + jnp.sum(E, axis=-1, keepdims=True)

        init = (jnp.zeros((bq, LANES), F32), jnp.zeros((bq, 1), F32))
        carry = lax.fori_loop(first, diag, lambda kb, c: block(kb, c, False), init)
        dq, _ = block(diag, carry, True)
        dq_ref[...] = dq

    qspec, kspec = _att_specs(T, bq, None)
    sds = jax.ShapeDtypeStruct((T, H * LANES), F32)
    oblk = pl.BlockSpec((bq, LANES), lambda h, i: (i, h))
    kout = pl.BlockSpec((T, LANES), lambda h, i: (0, h))
    return pl.pallas_call(
        body, grid=(H, T // bq),
        in_specs=[qspec(CB_SQ), kspec(CB_SK), kspec(CB_SV), oblk, qspec(do_cb)],
        out_specs=[oblk, kout, kout], out_shape=[sds, sds, sds],
        compiler_params=_cp(("arbitrary", "arbitrary")), name=name)(proj, proj, proj, tot, docat)


def mla_fwd(Q, K, V, name, hosted=None):
    T = Q.shape[0]
    H = MLA_HEADS
    bq, bk = _att_blocks(T)
    scale = (MLA_NOPE + MLA_ROPE) ** -0.5

    def body(q_ref, k_ref, v_ref, o_ref, lse_ref):
        i = pl.program_id(1)
        qb = q_ref[...]
        diag = (i * bq) // bk

        def block(kb, carry, masked):
            acc, m, l = carry
            s = _bdot_nt(qb, _kblock(k_ref, kb, bk)) * scale
            if masked:
                qpos, kpos = _att_pos(i, kb, bq, bk)
                s = jnp.where(kpos <= qpos, s, NEG_BIG)
            m_new = jnp.maximum(m, jnp.max(s, axis=-1, keepdims=True))
            p = jnp.exp(s - m_new)
            corr = jnp.exp(m - m_new)
            acc = corr * acc + _bdot(p, _kblock(v_ref, kb, bk))
            return acc, m_new, corr * l + jnp.sum(p, axis=-1, keepdims=True)

        init = (jnp.zeros((bq, LANES), F32), jnp.full((bq, 1), NEG_BIG, F32), jnp.zeros((bq, 1), F32))
        carry = lax.fori_loop(0, diag, lambda kb, c: block(kb, c, False), init)
        acc, m, l = block(diag, carry, True)
        o_ref[...] = acc / l
        lse_ref[...] = jnp.broadcast_to(m + jnp.log(l), (bq, LANES))

    qspec, kspec = _att_specs(T, bq, None)
    sds = jax.ShapeDtypeStruct((T, H * LANES), F32)
    oblk = pl.BlockSpec((bq, LANES), lambda h, i: (i, h))
    return _hosted_call(
        hosted, body, grid=(H, T // bq), in_specs=[qspec(0), kspec(0), kspec(0)],
        out_specs=[oblk, oblk], out_shape=[sds, sds],
        compiler_params=_cp(("parallel", "parallel")), name=name)(Q, K, V)


def mla_bwd(Q, K, V, o, lse, docat, do_cb, name, hosted=None):
    T = Q.shape[0]
    H = MLA_HEADS
    bq, bk = _att_blocks(T)
    scale = (MLA_NOPE + MLA_ROPE) ** -0.5

    def body(q_ref, k_ref, v_ref, o_ref, lse_ref, do_ref, dq_ref, dk_ref, dv_ref):
        i = pl.program_id(1)

        @pl.when(i == 0)
        def _():
            dk_ref[...] = jnp.zeros_like(dk_ref)
            dv_ref[...] = jnp.zeros_like(dv_ref)

        qb = q_ref[...]
        do = do_ref[...]
        dob = do.astype(BF16)
        delta = jnp.sum(do * o_ref[...], axis=-1, keepdims=True)
        lse = lse_ref[:, 0:1]

        diag = (i * bq) // bk

        def block(kb, dq, masked):
            kblk = _kblock(k_ref, kb, bk)
            s = _bdot_nt(qb, kblk) * scale
            if masked:
                qpos, kpos = _att_pos(i, kb, bq, bk)
                s = jnp.where(kpos <= qpos, s, NEG_BIG)
            p = jnp.exp(s - lse)
            dp = _bdot_nt(dob, _kblock(v_ref, kb, bk))
            ds = (p * (dp - delta) * scale).astype(BF16)
            rows = pl.ds(pl.multiple_of(kb * bk, bk), bk)
            dk_ref[rows, :] += lax.dot_general(ds, qb, TN_DIMS, preferred_element_type=F32)
            dv_ref[rows, :] += lax.dot_general(p.astype(BF16), dob, TN_DIMS, preferred_element_type=F32)
            return dq + jnp.dot(ds, kblk, preferred_element_type=F32)

        dq = lax.fori_loop(0, diag, lambda kb, c: block(kb, c, False), jnp.zeros((bq, LANES), F32))
        dq_ref[...] = block(diag, dq, True)

    qspec, kspec = _att_specs(T, bq, None)
    sds = jax.ShapeDtypeStruct((T, H * LANES), F32)
    oblk = pl.BlockSpec((bq, LANES), lambda h, i: (i, h))
    kout = pl.BlockSpec((T, LANES), lambda h, i: (0, h))
    return _hosted_call(
        hosted, body, grid=(H, T // bq),
        in_specs=[qspec(0), kspec(0), kspec(0), oblk, oblk, qspec(do_cb)],
        out_specs=[oblk, kout, kout], out_shape=[sds, sds, sds],
        compiler_params=_cp(("arbitrary", "arbitrary")), name=name)(Q, K, V, o, lse, docat)


def _tile_heads(t, n):
    return jnp.concatenate([t] * n, axis=1)


def _rope(X, C, Sn, Sp):
    n = X.shape[1]
    return X * C + pltpu.roll(X, n - HALF_ROPE, 1) * Sn + pltpu.roll(X, HALF_ROPE, 1) * Sp


def _rope_t(dO, C, Sn, Sp):
    n = dO.shape[1]
    return dO * C + pltpu.roll(dO * Sn, HALF_ROPE, 1) + pltpu.roll(dO * Sp, n - HALF_ROPE, 1)


def _rms(x, w):
    r = lax.rsqrt(jnp.mean(x * x, axis=-1, keepdims=True) + RMS_EPS)
    xh = x * r
    return r, xh, xh * w


def _rms_bwd(dn, w, r, xh):
    dxh = dn * w
    return r * (dxh - xh * jnp.mean(dxh * xh, axis=-1, keepdims=True)), jnp.sum(dn * xh, axis=0, keepdims=True)


def _mla_pre_specs(T, tm):
    KV = MLA_KV_RANK
    QR = MLA_Q_RANK
    W = MLA_HEADS * LANES
    full = lambda shape: pl.BlockSpec(shape, lambda i: (0, 0))
    specs = [pl.BlockSpec((tm, QR), lambda i: (i, CB_MQ * LANES // QR)),
             pl.BlockSpec((tm, 2 * LANES), lambda i: (i, CB_MKV // 2)),
             full((1, QR)), full((1, KV))]
    rope = [pl.BlockSpec((tm, LANES), lambda i: (i, 0))] * 3
    return specs, rope, full, W


def mla_pre_fwd(proj, wq, wkv, wuq, wuk, wuv, ropeC, ropeSn, ropeSp, name):
    T = proj.shape[0]
    tm = _pick(T, 512, 16)
    KV = MLA_KV_RANK
    H = MLA_HEADS

    def body(mq_ref, mkv_ref, wq_ref, wkv_ref, wuq_ref, wuk_ref, wuv_ref, c_ref, sn_ref, sp_ref, Q_ref, K_ref, V_ref):
        C, Sn, Sp = (_tile_heads(t[...], H) for t in (c_ref, sn_ref, sp_ref))
        _, _, qn = _rms(mq_ref[...], wq_ref[...])
        Q_ref[...] = _rope(_bdot(qn, wuq_ref[...]), C, Sn, Sp).astype(BF16)
        mkv = mkv_ref[...]
        _, _, kvn = _rms(mkv[:, :KV], wkv_ref[...])
        kr = pltpu.roll(mkv[:, KV:], MLA_NOPE, 1)
        K_ref[...] = _rope(_bdot(kvn, wuk_ref[...]) + _tile_heads(kr, H), C, Sn, Sp).astype(BF16)
        V_ref[...] = _bdot(kvn, wuv_ref[...]).astype(BF16)

    specs, rope, full, W = _mla_pre_specs(T, tm)
    oblk = pl.BlockSpec((tm, W), lambda i: (i, 0))
    sds = jax.ShapeDtypeStruct((T, W), BF16)
    return pl.pallas_call(
        body, grid=(T // tm,),
        in_specs=specs + [full(wuq.shape), full(wuk.shape), full(wuv.shape)] + rope,
        out_specs=[oblk, oblk, oblk], out_shape=[sds, sds, sds],
        compiler_params=_cp(("parallel",)), name=name)(proj, proj, wq, wkv, wuq, wuk, wuv, ropeC, ropeSn, ropeSp)


def mla_pre_bwd(proj, wq, wkv, wuq, wuk, wuv, ropeC, ropeSn, ropeSp, dQ, dK, dV, name):
    T = proj.shape[0]
    tm = _pick(T, 512, 16)
    KV = MLA_KV_RANK
    H = MLA_HEADS

    def body(mq_ref, mkv_ref, wq_ref, wkv_ref, wuq_ref, wuk_ref, wuv_ref,
             c_ref, sn_ref, sp_ref, dQ_ref, dK_ref, dV_ref,
             dmq_ref, dmkv_ref, dwuq_ref, dwuk_ref, dwuv_ref, dwq_ref, dwkv_ref):
        i = pl.program_id(0)

        @pl.when(i == 0)
        def _():
            for ref in (dwuq_ref, dwuk_ref, dwuv_ref, dwq_ref, dwkv_ref):
                ref[...] = jnp.zeros_like(ref)

        C, Sn, Sp = (_tile_heads(t[...], H) for t in (c_ref, sn_ref, sp_ref))
        rq, xq, qn = _rms(mq_ref[...], wq_ref[...])
        mkv = mkv_ref[...]
        rkv, xkv, kvn = _rms(mkv[:, :KV], wkv_ref[...])
        dqf = _rope_t(dQ_ref[...], C, Sn, Sp)
        dkf = _rope_t(dK_ref[...], C, Sn, Sp)
        dv = dV_ref[...]
        dwuq_ref[...] += _bdot_tn(qn, dqf)
        dwuk_ref[...] += _bdot_tn(kvn, dkf)
        dwuv_ref[...] += _bdot_tn(kvn, dv)
        dmq, dwq = _rms_bwd(_bdot_nt(dqf, wuq_ref[...]), wq_ref[...], rq, xq)
        dckv, dwkv = _rms_bwd(_bdot_nt(dkf, wuk_ref[...]) + _bdot_nt(dv, wuv_ref[...]), wkv_ref[...], rkv, xkv)
        dwq_ref[...] += dwq
        dwkv_ref[...] += dwkv
        dmq_ref[...] = dmq
        dkr = dkf[:, 0:LANES]
        for h in range(1, H):
            dkr = dkr + dkf[:, h * LANES:(h + 1) * LANES]
        dkr = pltpu.roll(dkr, LANES - MLA_NOPE, 1)
        dkr = jnp.where(_iota2(dkr.shape, 1) < MLA_ROPE, dkr, 0.0)
        dmkv_ref[...] = jnp.concatenate([dckv, dkr], axis=1)

    specs, rope, full, W = _mla_pre_specs(T, tm)
    wide = pl.BlockSpec((tm, W), lambda i: (i, 0))
    return pl.pallas_call(
        body, grid=(T // tm,),
        in_specs=specs + [full(w.shape) for w in (wuq, wuk, wuv)] + rope + [wide, wide, wide],
        out_specs=[pl.BlockSpec((tm, MLA_Q_RANK), lambda i: (i, 0)), pl.BlockSpec((tm, 2 * LANES), lambda i: (i, 0)),
                   full(wuq.shape), full(wuk.shape), full(wuv.shape), full((1, MLA_Q_RANK)), full((1, KV))],
        out_shape=[jax.ShapeDtypeStruct((T, MLA_Q_RANK), F32), jax.ShapeDtypeStruct((T, 2 * LANES), F32),
                   jax.ShapeDtypeStruct(wuq.shape, F32), jax.ShapeDtypeStruct(wuk.shape, F32),
                   jax.ShapeDtypeStruct(wuv.shape, F32), jax.ShapeDtypeStruct((1, MLA_Q_RANK), F32),
                   jax.ShapeDtypeStruct((1, KV), F32)],
        compiler_params=_cp(("arbitrary",)), name=name)(
            proj, proj, wq, wkv, wuq, wuk, wuv, ropeC, ropeSn, ropeSp, dQ, dK, dV)


def all_gather(shards, name):
    n = len(shards)

    def body(*refs):
        x_refs, out_refs = refs[:n], refs[n:2 * n]
        send_sems, recv_sems, local_sems = refs[2 * n:]
        x, y, c = _place()
        me, sibling = (x, y, c), (x, y, 1 - c)
        chips = [(1 - x, y), (x, 1 - y), (1 - x, 1 - y)]

        def slot(a, px, py, pc):
            return out_refs[a].at[4 * px + 2 * py + pc]

        def copy(a, k, block, to, src=None):
            return pltpu.make_async_remote_copy(
                src_ref=slot(a, *block) if src is None else src, dst_ref=slot(a, *block),
                send_sem=send_sems.at[a, k], recv_sem=recv_sems.at[a, k], device_id=to, device_id_type=MESH)

        mine = [pltpu.make_async_copy(x_refs[a], slot(a, *me), local_sems.at[a]) for a in range(n)]
        first = []
        for a in range(n):
            mine[a].start()
            first.append(copy(a, 0, me, sibling, src=x_refs[a]))
            first += [copy(a, 1 + j, me, (*chip, c), src=x_refs[a]) for j, chip in enumerate(chips)]
        for cp in first:
            cp.start()
        passed = []
        for j, chip in enumerate(chips):
            for a in range(n):
                copy(a, 1 + j, (*chip, c), me).wait_recv()
                passed.append(copy(a, 4 + j, (*chip, c), sibling))
                passed[-1].start()
        for a in range(n):
            copy(a, 0, sibling, me).wait_recv()
            for j, chip in enumerate(chips):
                copy(a, 4 + j, (*chip, 1 - c), me).wait_recv()
        for cp in first + passed:
            cp.wait_send()
        for cp in mine:
            cp.wait()

    return pl.pallas_call(
        body, out_shape=[jax.ShapeDtypeStruct((N_DEV,) + s.shape, s.dtype) for s in shards],
        in_specs=[ANY] * n, out_specs=[ANY] * n,
        scratch_shapes=[pltpu.SemaphoreType.DMA((n, 7)), pltpu.SemaphoreType.DMA((n, 7)), pltpu.SemaphoreType.DMA((n,))],
        name=name)(*shards)


def exchange_partials(parts, name):
    n = len(parts)

    def body(*refs):
        src_refs, dst_refs = refs[:n], refs[n:2 * n]
        send_sems, recv_sems, local_sems = refs[2 * n:]
        x, y, c = _place()
        me = 4 * x + 2 * y + c
        copies = []
        mine = []
        for a in range(n):
            mine.append(pltpu.make_async_copy(src_refs[a].at[me], dst_refs[a].at[me], local_sems.at[a]))
            for k in range(1, N_DEV):
                px = 1 - x if k & 4 else x
                py = 1 - y if k & 2 else y
                pc = 1 - c if k & 1 else c
                copies.append(pltpu.make_async_remote_copy(
                    src_ref=src_refs[a].at[4 * px + 2 * py + pc], dst_ref=dst_refs[a].at[me],
                    send_sem=send_sems.at[a, k - 1], recv_sem=recv_sems.at[a, k - 1],
                    device_id=(px, py, pc), device_id_type=MESH))
        for cp in mine + copies:
            cp.start()
        for cp in copies:
            cp.wait_recv()
        for cp in copies:
            cp.wait_send()
        for cp in mine:
            cp.wait()

    return pl.pallas_call(
        body, out_shape=[jax.ShapeDtypeStruct(p.shape, p.dtype) for p in parts],
        in_specs=[ANY] * n, out_specs=[ANY] * n,
        scratch_shapes=[pltpu.SemaphoreType.DMA((n, 7)), pltpu.SemaphoreType.DMA((n, 7)), pltpu.SemaphoreType.DMA((n,))],
        name=name)(*parts)


def reduce_adamw(parts, w, m, v, name):
    L = len(parts)
    n, Rl, C = parts[0].shape
    R = w.shape[0]
    assert R == L * Rl
    tr = Rl if Rl * C <= 256 * 1024 else _pick(Rl, 256, 16)
    nr = Rl // tr

    def body(*refs):
        p_refs = refs[:L]
        w_ref, m_ref, v_ref, g_ref, d_ref, nm_ref, nv_ref, sum_ref = refs[L:]
        grp = pl.program_id(0)
        for j in range(L):
            @pl.when(grp == j)
            def _(j=j):
                acc = p_refs[j][0].astype(F32)
                for s in range(1, n):
                    acc = acc + p_refs[j][s].astype(F32)
                sum_ref[...] = acc

        g_ = sum_ref[...]
        m_ = ADAM_B1 * m_ref[...] + (1.0 - ADAM_B1) * g_
        v_ = ADAM_B2 * v_ref[...] + (1.0 - ADAM_B2) * (g_ * g_)
        m_hat = m_ / (1.0 - ADAM_B1 ** ADAM_STEP)
        v_hat = v_ / (1.0 - ADAM_B2 ** ADAM_STEP)
        g_ref[...] = g_
        d_ref[...] = -ADAM_LR * (m_hat / (jnp.sqrt(v_hat) + ADAM_EPS) + ADAM_WD * w_ref[...])
        nm_ref[...] = m_
        nv_ref[...] = v_

    blk = pl.BlockSpec((tr, C), lambda l, r: (l * nr + r, 0))
    sds = jax.ShapeDtypeStruct((R, C), F32)
    p_specs = [pl.BlockSpec((n, tr, C), lambda l, r, j=j: (0, jnp.where(l == j, r, 0), 0)) for j in range(L)]
    return pl.pallas_call(
        body, grid=(L, nr), in_specs=p_specs + [blk] * 3,
        out_specs=[blk] * 4, out_shape=[sds] * 4, scratch_shapes=[pltpu.VMEM((tr, C), F32)],
        compiler_params=_cp(("arbitrary", "arbitrary")), name=name)(*parts, w, m, v)


SHARDED = {"ffa_w_in": (2, BF16), "ffa_w_out": (1, BF16), "mix_w_in": (2, BF16), "mla_w_uq": (2, BF16),
           "mla_w_ukv": (2, BF16), "mix_w_o": (1, BF16), "ffb_w_in": (2, BF16), "ffb_w_out": (1, BF16),
           "ple_w_gate": (1, BF16), "ple_w_proj": (2, BF16), "gdn_conv_w": (2, F32), "ln_g": (2, F32), "ln_b": (2, F32)}
FFN_SLOT = ("ffa_w_in", "ffa_w_out", "ffb_w_in", "ffb_w_out")
REPLICATED = ("gdn_a_log", "gdn_dt_bias", "gdn_norm_w", "mla_q_norm_w", "mla_kv_norm_w")
WEIGHTS = ("ffa_w_in", "ffa_w_out", "mix_w_in", "gdn_conv_w", "gdn_a_log", "gdn_dt_bias", "gdn_norm_w", "mla_q_norm_w",
           "mla_kv_norm_w", "mla_w_uq", "mla_w_ukv", "mix_w_o", "ffb_w_in", "ffb_w_out", "ln_g", "ln_b", "ple_w_gate",
           "ple_w_proj")


def _to_slots(full, axis):
    L, a, b = full.shape
    if axis == 2:
        return full.reshape(L, a, N_DEV, b // N_DEV).transpose(2, 0, 1, 3).reshape(N_DEV, L * a, b // N_DEV)
    return full.reshape(L, N_DEV, a // N_DEV, b).transpose(1, 0, 2, 3).reshape(N_DEV, L * a // N_DEV, b)


def _from_slots(slots, shard_shape, axis):
    L, a, b = shard_shape
    t = slots.reshape((N_DEV,) + tuple(shard_shape))
    if axis == 2:
        return t.transpose(1, 2, 0, 3).reshape(L, a, N_DEV * b)
    return t.transpose(1, 0, 2, 3).reshape(L, N_DEV * a, b)


def _view2d(t):
    return t.reshape(-1, t.shape[-1])


def _pad_heads(w, nh):
    K = w.shape[0]
    return jnp.pad(w.reshape(K, nh, GDN_D), ((0, 0), (0, 0), (0, LANES - GDN_D))).reshape(K, nh * LANES)


def _unpad_heads(w, nh):
    K = w.shape[0]
    return w.reshape(K, nh, LANES)[:, :, :GDN_D].reshape(K, nh * GDN_D)


IN_WIDTHS = (512, 512, 512, 512, 8, 8, 256, 256, 256, 256, 160)


def _split_in(w):
    offs = np.cumsum((0,) + IN_WIDTHS)
    return [w[:, int(offs[i]):int(offs[i + 1])] for i in range(len(IN_WIDTHS))]


def _pad_in_proj(w):
    gq, gk, gv, gz, ga, gb, sq, sk, sv, mq, mkv = _split_in(w)
    gab = jnp.pad(jnp.concatenate([ga, gb], axis=1), ((0, 0), (0, LANES - 2 * GDN_HEADS)))
    return jnp.concatenate(
        [gq, gk, gv, gz] + [_pad_heads(t, SB_HEADS) for t in (sq, sk, sv)]
        + [mq, jnp.pad(mkv, ((0, 0), (0, 2 * LANES - mkv.shape[1]))), gab], axis=1)


def _unpad_in_proj(wp):
    c = lambda cb, n: wp[:, cb * LANES:(cb + n) * LANES]
    gab = c(CB_GAB, 1)
    parts = [c(cb, DO_SB) for cb in (CB_GQ, CB_GK, CB_GV, CB_GZ)]
    parts += [gab[:, :GDN_HEADS], gab[:, GDN_HEADS:2 * GDN_HEADS]]
    parts += [_unpad_heads(c(cb, SB_HEADS), SB_HEADS) for cb in (CB_SQ, CB_SK, CB_SV)]
    parts += [c(CB_MQ, 2), c(CB_MKV, 2)[:, :MLA_KV_RANK + MLA_ROPE]]
    return jnp.concatenate(parts, axis=1)


def _pad_lanes(w, width):
    return jnp.pad(w, ((0, 0), (0, width - w.shape[1])))


def _mla_up_pad(w_uq, w_ukv):
    H = MLA_HEADS
    dq = MLA_NOPE + MLA_ROPE
    wuq = jnp.pad(w_uq.reshape(-1, H, dq), ((0, 0), (0, 0), (0, LANES - dq))).reshape(-1, H * LANES)
    kv = w_ukv.reshape(-1, H, MLA_NOPE + MLA_V)
    wuk = jnp.pad(kv[:, :, :MLA_NOPE], ((0, 0), (0, 0), (0, LANES - MLA_NOPE))).reshape(-1, H * LANES)
    wuv = jnp.pad(kv[:, :, MLA_NOPE:], ((0, 0), (0, 0), (0, LANES - MLA_V))).reshape(-1, H * LANES)
    return wuq, wuk, wuv


def _mla_up_unpad(dwuq, dwuk, dwuv):
    H = MLA_HEADS
    dq = MLA_NOPE + MLA_ROPE
    g_uq = dwuq.reshape(-1, H, LANES)[:, :, :dq].reshape(-1, H * dq)
    g_ukv = jnp.concatenate([dwuk.reshape(-1, H, LANES)[:, :, :MLA_NOPE], dwuv.reshape(-1, H, LANES)[:, :, :MLA_V]],
                            axis=2).reshape(-1, H * (MLA_NOPE + MLA_V))
    return g_uq, g_ukv


def _rope_tables(positions):
    inv = 1.0 / (ROPE_BASE ** (jnp.arange(0, MLA_ROPE, 2, dtype=F32) / MLA_ROPE))
    ang = positions.astype(F32)[:, None] * inv
    cos, sin = jnp.cos(ang), jnp.sin(ang)
    T = positions.shape[0]
    one = lambda n: jnp.ones((T, n), F32)
    zero = lambda n: jnp.zeros((T, n), F32)
    tail = LANES - MLA_NOPE - MLA_ROPE
    C = jnp.concatenate([one(MLA_NOPE), cos, cos, one(tail)], axis=1)
    Sn = jnp.concatenate([zero(MLA_NOPE), -sin, zero(HALF_ROPE + tail)], axis=1)
    Sp = jnp.concatenate([zero(MLA_NOPE + HALF_ROPE), sin, zero(tail)], axis=1)
    return C, Sn, Sp


GATHER_FIRST = [("ffa_w_in", 0), ("ffa_w_out", 0)] + [(n, l) for l in range(DEPTH) for n in ("gdn_conv_w", "ln_g", "ln_b")]
GATHER_PLAN = {
    (0, "ffa_fwd"): [("mix_w_in", 0), ("mla_w_uq", 0), ("mla_w_ukv", 0)],
    (0, "gdn_chunk_fwd"): [("mix_w_o", 0), ("ffb_w_in", 0)],
    (0, "sb_fwd"): [("ffb_w_out", 0), ("ple_w_gate", 0), ("ple_w_proj", 0)],
    (0, "mla_fwd"): [("ffa_w_in", 1), ("mix_w_o", 1)],
    (0, "ffb_fwd"): [("ffa_w_out", 1), ("mix_w_in", 1)],
    (1, "ffa_fwd"): [("ffb_w_in", 1)],
    (1, "in_proj"): [("ffb_w_out", 1), ("ple_w_gate", 1), ("ple_w_proj", 1), ("mla_w_uq", 1), ("mla_w_ukv", 1)],
}
SCATTER_PLAN = {
    (1, "gdn_chunk_bwd"): [("ffb_w_in", 1)],
    (1, "gdn_pre_bwd"): [("ffb_w_out", 1), ("ple_w_gate", 1), ("ple_w_proj", 1), ("mix_w_o", 1)],
    (1, "ffa_bwd"): [("mix_w_in", 1), ("mla_w_uq", 1), ("mla_w_ukv", 1), ("gdn_conv_w", 1)],
    (0, "ffb_bwd"): [("ffa_w_in", 1)],
    (0, "gdn_chunk_bwd"): [("ffb_w_in", 0)],
    (0, "gdn_pre_bwd"): [("ffb_w_out", 0), ("ple_w_gate", 0), ("ple_w_proj", 0), ("mix_w_o", 0)],
    (0, "mla_bwd"): [("ffa_w_out", 1), ("ln_g", 1), ("ln_b", 1)],
    (0, "ffa_bwd"): [("mix_w_in", 0), ("mla_w_uq", 0), ("mla_w_ukv", 0), ("gdn_conv_w", 0)],
    (0, "d_ffa_in"): [("ffa_w_out", 0), ("ln_g", 0), ("ln_b", 0)],
}
SCATTER_LAST = [("ffa_w_in", 0)]


class Exchanges:
    def __init__(self, shards):
        self.shards = shards
        self.full = {}
        self.partial = {}
        self.received = {}

    def _block(self, key):
        n, l = key
        return self.shards[n][l].astype(SHARDED[n][1])

    def _absorb_gather(self, keys, results):
        for (n, l), g in zip(keys, results):
            blk = self.shards[n][l]
            self.full[(n, l)] = g if n in FFN_SLOT else _from_slots(g, (1,) + blk.shape, SHARDED[n][0])[0]

    def gather_now(self, keys, name):
        self._absorb_gather(keys, all_gather([self._block(k) for k in keys], name))

    def gather_with(self, layer, tag):
        keys = GATHER_PLAN.get((layer, tag))
        return None if keys is None else (keys, Hosted("gather", [self._block(k) for k in keys]))

    def scatter_with(self, layer, tag):
        keys = SCATTER_PLAN.get((layer, tag))
        return None if keys is None else (keys, Hosted("scatter", [self.partial[k] for k in keys]))

    def done(self, carried):
        if carried is not None:
            keys, hosted = carried
            if hosted.kind == "gather":
                self._absorb_gather(keys, hosted.results)
            else:
                self.received.update(zip(keys, hosted.results))

    def add_grad(self, key, g):
        n, l = key
        self.partial[key] = g if n in FFN_SLOT else _to_slots(g[None], SHARDED[n][0]).astype(SHARDED[n][1])


def _carried(c):
    return None if c is None else c[1]


def _layer_fwd(h0, p_i, rope, i, ex, rep):
    L = "L%d_" % i
    S = {"h0": h0, "p": p_i}
    W = ex.full
    ln_g = [W[("ln_g", i)][j][None, :] for j in range(3)]
    ln_b = [W[("ln_b", i)][j][None, :] for j in range(3)]
    S["ln_g"] = ln_g
    c = ex.gather_with(i, "ffa_fwd")
    S["h1"], S["xh1"], S["rs1"], S["Ga"], S["Ua"], S["h1b"] = ffn_fwd(
        h0, W[("ffa_w_in", i)], W[("ffa_w_out", i)], ln_g[0], ln_b[0], L + "ffa_fwd", hosted=_carried(c))
    ex.done(c)
    S["win"] = _pad_in_proj(W[("mix_w_in", i)])
    c = ex.gather_with(i, "in_proj")
    S["proj"] = mm_nn(S["h1b"], S["win"], L + "in_proj", hosted=_carried(c))
    ex.done(c)
    S["conv"] = W[("gdn_conv_w", i)]
    S["a_row"] = _pad_lanes(rep["gdn_a_log"][i][None, :], LANES)
    S["dt_row"] = _pad_lanes(rep["gdn_dt_bias"][i][None, :], LANES)
    S["nw"] = rep["gdn_norm_w"][i][None, :]
    S["wq"] = rep["mla_q_norm_w"][i][None, :]
    S["wkv"] = rep["mla_kv_norm_w"][i][None, :]
    S["qkv"] = gdn_pre_fwd(S["proj"], S["conv"], L + "gdn_pre_fwd")
    c = ex.gather_with(i, "gdn_chunk_fwd")
    S["o_gdn"], S["opre"], S["Tm"], S["Sin"] = gdn_chunk_fwd(S["qkv"], S["proj"], S["a_row"], S["dt_row"], S["nw"],
                                                            L + "gdn_chunk_fwd", hosted=_carried(c))
    ex.done(c)
    c = ex.gather_with(i, "sb_fwd")
    S["o_sb"], S["tot"] = sb_fwd(S["proj"], L + "sb_fwd", hosted=_carried(c))
    ex.done(c)
    S["wuq"], S["wuk"], S["wuv"] = _mla_up_pad(W[("mla_w_uq", i)], W[("mla_w_ukv", i)])
    S["Q"], S["K"], S["V"] = mla_pre_fwd(S["proj"], S["wq"], S["wkv"], S["wuq"], S["wuk"], S["wuv"], *rope, L + "mla_pre_fwd")
    c = ex.gather_with(i, "mla_fwd")
    S["o_mla"], S["lse"] = mla_fwd(S["Q"], S["K"], S["V"], L + "mla_fwd", hosted=_carried(c))
    ex.done(c)
    wo = W[("mix_w_o", i)]
    wo_att = wo[GDN_W:].reshape(-1, GDN_D, wo.shape[1])
    S["wo"] = jnp.concatenate(
        [wo[:GDN_W], jnp.pad(wo_att, ((0, 0), (0, LANES - GDN_D), (0, 0))).reshape(-1, wo.shape[1])], axis=0)
    S["h2"], S["xh2"], S["rs2"], S["h2b"] = proj_res_ln([S["o_gdn"], S["o_sb"], S["o_mla"]], S["wo"], S["h1"],
                                                        ln_g[1], ln_b[1], L + "out_proj")
    c = ex.gather_with(i, "ffb_fwd")
    S["h3"], S["xh3"], S["rs3"], S["Gb"], S["Ub"], _ = ffn_fwd(
        S["h2"], W[("ffb_w_in", i)], W[("ffb_w_out", i)], ln_g[2], ln_b[2], L + "ffb_fwd", hosted=_carried(c))
    ex.done(c)
    h4, S["a"], S["e"] = ple_fwd(S["h3"], p_i, W[("ple_w_gate", i)], W[("ple_w_proj", i)], L + "ple_fwd")
    return h4, S


def _layer_bwd(dh4, S, rope, i, ex):
    L = "L%d_" % i
    W = ex.full
    Grep = {}
    dh3, da, de = ple_bwd(dh4, S["a"], S["e"], W[("ple_w_gate", i)], L + "ple_bwd")
    ex.add_grad(("ple_w_gate", i), mm_tn(S["h3"], da, L + "d_ple_gate"))
    ex.add_grad(("ple_w_proj", i), mm_tn(S["p"], de, L + "d_ple_proj"))
    dz3, dg2, db2 = ln_bwd(dh3, S["xh3"], S["rs3"], S["ln_g"][2], L + "ln3_bwd")
    c = ex.scatter_with(i, "ffb_bwd")
    dh2, dGb, dUb, actb = ffn_bwd(dz3, S["Gb"], S["Ub"], W[("ffb_w_in", i)], W[("ffb_w_out", i)], L + "ffb_bwd",
                                  hosted=_carried(c))
    ex.done(c)
    ex.add_grad(("ffb_w_in", i), ffn_dw_in(S["h2b"].T, dGb, dUb, L + "d_ffb_in"))
    ex.add_grad(("ffb_w_out", i), ffn_dw_out(actb, dz3, L + "d_ffb_out"))
    dz2, dg1, db1 = ln_bwd(dh2, S["xh2"], S["rs2"], S["ln_g"][1], L + "ln2_bwd")
    docat = mm_nn(dz2, S["wo"], L + "d_ocat", b_transposed=True)
    dwo_att = jnp.concatenate([mm_tn(S["o_sb"], dz2, L + "d_wo_sb"), mm_tn(S["o_mla"], dz2, L + "d_wo_mla")], axis=0)
    dwo_att = dwo_att.reshape(-1, LANES, dwo_att.shape[1])[:, :GDN_D, :].reshape(-1, dwo_att.shape[1])
    ex.add_grad(("mix_w_o", i), jnp.concatenate([mm_tn(S["o_gdn"], dz2, L + "d_wo_gdn"), dwo_att], axis=0))
    c = ex.scatter_with(i, "gdn_chunk_bwd")
    dqkv, dgz, dgab, d_alog, d_dt, d_nw = gdn_chunk_bwd(S["qkv"], S["proj"], S["a_row"], S["dt_row"], S["nw"],
                                                        S["opre"], S["Tm"], S["Sin"], docat, L + "gdn_chunk_bwd",
                                                        hosted=_carried(c))
    ex.done(c)
    c = ex.scatter_with(i, "gdn_pre_bwd")
    dc, dconv = gdn_pre_bwd_pointwise(S["proj"], S["conv"], dqkv, L + "gdn_pre_bwd", hosted=_carried(c))
    ex.done(c)
    dxqkv = gdn_pre_bwd_conv(dc, S["conv"], L + "gdn_conv_bwd")
    ex.add_grad(("gdn_conv_w", i), dconv)
    Grep["gdn_a_log"], Grep["gdn_dt_bias"], Grep["gdn_norm_w"] = d_alog[0, :GDN_HEADS], d_dt[0, :GDN_HEADS], d_nw[0]
    dsq, dsk, dsv = sb_bwd(S["proj"], S["tot"], docat, DO_SB, L + "sb_bwd")
    c = ex.scatter_with(i, "mla_bwd")
    dQ, dK, dV = mla_bwd(S["Q"], S["K"], S["V"], S["o_mla"], S["lse"], docat, DO_MLA, L + "mla_bwd",
                         hosted=_carried(c))
    ex.done(c)
    dmq, dmkv, dwuq, dwuk, dwuv, dwq, dwkv = mla_pre_bwd(
        S["proj"], S["wq"], S["wkv"], S["wuq"], S["wuk"], S["wuv"], *rope, dQ, dK, dV, L + "mla_pre_bwd")
    g_uq, g_ukv = _mla_up_unpad(dwuq, dwuk, dwuv)
    ex.add_grad(("mla_w_uq", i), g_uq)
    ex.add_grad(("mla_w_ukv", i), g_ukv)
    Grep["mla_q_norm_w"], Grep["mla_kv_norm_w"] = dwq[0], dwkv[0]
    dproj = jnp.concatenate([dxqkv, dgz, dsq, dsk, dsv, dmq, dmkv, dgab], axis=1).astype(BF16)
    ex.add_grad(("mix_w_in", i),
                _unpad_in_proj(mm_tn(S["h1b"].T, dproj, L + "d_in_proj", a_transposed=True)))
    dh1 = mm_nn(dproj, S["win"], L + "d_h1", res=dz2, res_scale=ALPHA, b_transposed=True)
    dz1, dg0, db0 = ln_bwd(dh1, S["xh1"], S["rs1"], S["ln_g"][0], L + "ln1_bwd")
    c = ex.scatter_with(i, "ffa_bwd")
    dh0, dGa, dUa, acta = ffn_bwd(dz1, S["Ga"], S["Ua"], W[("ffa_w_in", i)], W[("ffa_w_out", i)], L + "ffa_bwd",
                                  hosted=_carried(c))
    ex.done(c)
    ex.add_grad(("ffa_w_out", i), ffn_dw_out(acta, dz1, L + "d_ffa_out"))
    ex.add_grad(("ln_g", i), jnp.concatenate([dg0, dg1, dg2], axis=0))
    ex.add_grad(("ln_b", i), jnp.concatenate([db0, db1, db2], axis=0))
    c = ex.scatter_with(i, "d_ffa_in")
    ex.add_grad(("ffa_w_in", i), ffn_dw_in(S["h0"].T.astype(BF16), dGa, dUa, L + "d_ffa_in", hosted=_carried(c)))
    ex.done(c)
    return dh0, Grep


def _local_step(x, p, positions, target, ex, rep):
    assert DEPTH == 2
    rope = _rope_tables(positions)
    h, saved = x, []
    for i in range(DEPTH):
        h, S = _layer_fwd(h, p[i], rope, i, ex, rep)
        saved.append(S)
    loss, dh = loss_head(h, target, "loss_head")
    grads = [None] * DEPTH
    for i in reversed(range(DEPTH)):
        dh, grads[i] = _layer_bwd(dh, saved[i], rope, i, ex)
    return loss, dh, {n: jnp.stack([grads[i][n] for i in range(DEPTH)]) for n in REPLICATED}


def kernel(x, p, positions, ffa_w_in, ffa_w_out, mix_w_in, gdn_conv_w, gdn_a_log, gdn_dt_bias, gdn_norm_w, mla_q_norm_w, mla_kv_norm_w, mla_w_uq, mla_w_ukv, mix_w_o, ffb_w_in, ffb_w_out, ln_g, ln_b, ple_w_gate, ple_w_proj, loss_target, m_ffa_w_in, m_ffa_w_out, m_mix_w_in, m_gdn_conv_w, m_gdn_a_log, m_gdn_dt_bias, m_gdn_norm_w, m_mla_q_norm_w, m_mla_kv_norm_w, m_mla_w_uq, m_mla_w_ukv, m_mix_w_o, m_ffb_w_in, m_ffb_w_out, m_ln_g, m_ln_b, m_ple_w_gate, m_ple_w_proj, v_ffa_w_in, v_ffa_w_out, v_mix_w_in, v_gdn_conv_w, v_gdn_a_log, v_gdn_dt_bias, v_gdn_norm_w, v_mla_q_norm_w, v_mla_kv_norm_w, v_mla_w_uq, v_mla_w_ukv, v_mix_w_o, v_ffb_w_in, v_ffb_w_out, v_ln_g, v_ln_b, v_ple_w_gate, v_ple_w_proj):
    given = dict(locals())
    shards = {n: given[n] for n in WEIGHTS}
    ex = Exchanges({n: shards[n] for n in SHARDED})
    ex.gather_now(GATHER_FIRST, "gather_first")
    loss, grad_x, Grep = _local_step(x[0], p[:, 0], positions[0], loss_target[0], ex, {n: shards[n] for n in REPLICATED})
    loss = lax.psum(loss[0, 0], ("x", "y", "c"))
    ex.received.update(zip(SCATTER_LAST, exchange_partials([ex.partial[k] for k in SCATTER_LAST], "scatter_last")))
    rep_received = dict(zip(REPLICATED, all_gather([Grep[n] for n in REPLICATED], "gather_replicated_grads")))
    grad, delta, new_m, new_v = {}, {}, {}, {}
    for n in WEIGHTS:
        shape = shards[n].shape
        parts = [rep_received[n]] if n in REPLICATED else [ex.received[(n, l)] for l in range(DEPTH)]
        if parts[0].shape[1] % 8:
            parts = [jnp.concatenate(parts, axis=1)]
        outs = reduce_adamw(parts, _view2d(shards[n]), _view2d(given["m_" + n]), _view2d(given["v_" + n]),
                            "adamw_" + n)
        grad[n], delta[n], new_m[n], new_v[n] = (t.reshape(shape) for t in outs)
    return (loss, grad_x[None], *[grad[n] for n in WEIGHTS], *[delta[n] for n in WEIGHTS],
            *[new_m[n] for n in WEIGHTS], *[new_v[n] for n in WEIGHTS])
```

```python
import functools
import numpy as np
import jax
import jax.numpy as jnp
from jax import lax
from jax.experimental import pallas as pl
from jax.experimental.pallas import tpu as pltpu

F32 = jnp.float32
BF16 = jnp.bfloat16

DEPTH = 2
LN_EPS = 1e-5
RMS_EPS = 1e-6
ALPHA = (2 * DEPTH) ** 0.25
GDN_HEADS, GDN_D, GDN_CONV, GDN_CHUNK = 8, 64, 4, 64
SB_HEADS, SB_DIM = 4, 64
MLA_HEADS, MLA_NOPE, MLA_ROPE, MLA_V, MLA_Q_RANK, MLA_KV_RANK = 4, 64, 32, 64, 256, 128
ROPE_BASE = 10000.0
HALF_ROPE = MLA_ROPE // 2
LANES = 128
N_DEV = 8
ADAM_LR, ADAM_B1, ADAM_B2, ADAM_EPS, ADAM_WD, ADAM_STEP = 0.001, 0.9, 0.999, 1e-08, 0.01, 10

CB_GQ, CB_GK, CB_GV, CB_GZ = 0, 4, 8, 12
CB_SQ, CB_SK, CB_SV = 16, 20, 24
CB_MQ, CB_MKV, CB_GAB = 28, 30, 32
PROJ_W = 33 * LANES
GDN_W = GDN_HEADS * GDN_D
DO_SB = GDN_W // LANES
DO_MLA = DO_SB + SB_HEADS
VMEM_LIMIT = 56 * 1024 * 1024
MM_TILE = 1536

NT_DIMS = (((1,), (1,)), ((), ()))
TN_DIMS = (((0,), (0,)), ((), ()))


def _cp(sem):
    return pltpu.CompilerParams(dimension_semantics=sem, vmem_limit_bytes=VMEM_LIMIT)


def _bdot(a, b):
    return jnp.dot(a.astype(BF16), b.astype(BF16), preferred_element_type=F32)


def _bdot_nt(a, b):
    return lax.dot_general(a.astype(BF16), b.astype(BF16), NT_DIMS, preferred_element_type=F32)


def _bdot_tn(a, b):
    return lax.dot_general(a.astype(BF16), b.astype(BF16), TN_DIMS, preferred_element_type=F32)


def _split2(a):
    hi = a.astype(BF16)
    lo = (a - hi.astype(F32)).astype(BF16)
    return hi, lo


def _ones_dot_left(ones_bf16, x):
    hi = x.astype(BF16)
    r1 = x - hi.astype(F32)
    mid = r1.astype(BF16)
    lo = (r1 - mid.astype(F32)).astype(BF16)
    d = functools.partial(jnp.dot, preferred_element_type=F32)
    return d(ones_bf16, hi) + d(ones_bf16, mid) + d(ones_bf16, lo)


def _iota2(shape, dim):
    return lax.broadcasted_iota(jnp.int32, shape, dim)


def _sigmoid(x):
    return 0.5 * jnp.tanh(0.5 * x) + 0.5


def _softplus(x):
    return jnp.maximum(x, 0.0) + jnp.log(1.0 + jnp.exp(-jnp.abs(x)))


def _pick(n, limit, mult):
    if n <= limit:
        return n
    best = None
    for t in range(mult, limit + 1, mult):
        if n % t == 0:
            best = t
    assert best is not None, (n, limit, mult)
    return best


MESH = pl.DeviceIdType.MESH
ANY = pl.BlockSpec(memory_space=pl.ANY)


def _place():
    return lax.axis_index("x"), lax.axis_index("y"), lax.axis_index("c")


def _peer(k):
    x, y, c = _place()
    return (1 - x if k & 4 else x, 1 - y if k & 2 else y, 1 - c if k & 1 else c)


class Hosted:
    def __init__(self, kind, arrays):
        self.kind, self.arrays, self.n, self.results = kind, list(arrays), len(arrays), None

    def out_shapes(self):
        if self.kind == "gather":
            return [jax.ShapeDtypeStruct((N_DEV,) + a.shape, a.dtype) for a in self.arrays]
        return [jax.ShapeDtypeStruct(a.shape, a.dtype) for a in self.arrays]

    def sems(self):
        return [pltpu.SemaphoreType.DMA((self.n, N_DEV - 1)), pltpu.SemaphoreType.DMA((self.n, N_DEV - 1)),
                pltpu.SemaphoreType.DMA((self.n,))]

    def _copies(self, src_refs, dst_refs, send_sems, recv_sems, local_sems):
        x, y, c = _place()
        me = 4 * x + 2 * y + c
        local, remote = [], []
        for a in range(self.n):
            gather = self.kind == "gather"
            local.append(pltpu.make_async_copy(src_refs[a] if gather else src_refs[a].at[me], dst_refs[a].at[me],
                                               local_sems.at[a]))
            for k in range(1, N_DEV):
                px, py, pc = _peer(k)
                remote.append(pltpu.make_async_remote_copy(
                    src_ref=src_refs[a] if gather else src_refs[a].at[4 * px + 2 * py + pc], dst_ref=dst_refs[a].at[me],
                    send_sem=send_sems.at[a, k - 1], recv_sem=recv_sems.at[a, k - 1],
                    device_id=(px, py, pc), device_id_type=MESH))
        return local, remote

    def start(self, *refs):
        local, remote = self._copies(*refs)
        for cp in local + remote:
            cp.start()

    def wait(self, *refs):
        local, remote = self._copies(*refs)
        for cp in remote:
            cp.wait_recv()
        for cp in remote:
            cp.wait_send()
        for cp in local:
            cp.wait()


def _hosted_call(hosted, body, *, grid, in_specs, out_specs, out_shape, scratch_shapes=(), compiler_params, name):
    if hosted is None:
        return pl.pallas_call(body, grid=grid, in_specs=in_specs, out_specs=out_specs, out_shape=out_shape,
                              scratch_shapes=scratch_shapes, compiler_params=compiler_params, name=name)
    single = not isinstance(out_shape, (list, tuple))
    o_specs = [out_specs] if single else list(out_specs)
    o_shape = [out_shape] if single else list(out_shape)
    n_in, n_out, n_scr, n = len(in_specs), len(o_specs), len(scratch_shapes), hosted.n

    def wrapped(*refs):
        ins, c_in = refs[:n_in], refs[n_in:n_in + n]
        outs, c_out = refs[n_in + n:n_in + n + n_out], refs[n_in + n + n_out:n_in + 2 * n + n_out]
        rest = refs[n_in + 2 * n + n_out:]
        scr, sems = rest[:n_scr], rest[n_scr:]
        ids = [pl.program_id(ax) for ax in range(len(grid))]
        first = functools.reduce(jnp.logical_and, [i == 0 for i in ids])
        last = functools.reduce(jnp.logical_and, [i == g - 1 for i, g in zip(ids, grid)])

        @pl.when(first)
        def _():
            hosted.start(c_in, c_out, *sems)

        body(*ins, *outs, *scr)

        @pl.when(last)
        def _():
            hosted.wait(c_in, c_out, *sems)

    call = pl.pallas_call(
        wrapped, grid=grid, in_specs=list(in_specs) + [ANY] * n, out_specs=o_specs + [ANY] * n,
        out_shape=o_shape + hosted.out_shapes(), scratch_shapes=list(scratch_shapes) + hosted.sems(),
        compiler_params=_cp(("arbitrary",) * len(grid)), name=name)

    def run(*args):
        outs = call(*args, *hosted.arrays)
        hosted.results = list(outs[n_out:])
        return outs[0] if single else list(outs[:n_out])

    return run


def mm_nn(a, b, name, out_dtype=F32, res=None, res_scale=1.0, b_transposed=False, hosted=None):
    M, K = a.shape
    N = b.shape[0] if b_transposed else b.shape[1]
    tm, tn, tk = _pick(M, 512, 16), _pick(N, MM_TILE, LANES), _pick(K, MM_TILE, LANES)
    nk = K // tk
    has_res = res is not None
    dot = _bdot_nt if b_transposed else _bdot

    def body(*refs):
        if has_res:
            a_ref, b_ref, r_ref, o_ref, acc_ref = refs
        else:
            a_ref, b_ref, o_ref, acc_ref = refs
        k = pl.program_id(2)

        @pl.when(k == 0)
        def _():
            acc_ref[...] = jnp.zeros_like(acc_ref)

        acc_ref[...] += dot(a_ref[...], b_ref[...])

        @pl.when(k == nk - 1)
        def _():
            out = acc_ref[...]
            if has_res:
                out = out + res_scale * r_ref[...]
            o_ref[...] = out.astype(o_ref.dtype)

    b_spec = pl.BlockSpec((tn, tk), lambda i, j, k: (j, k)) if b_transposed else pl.BlockSpec((tk, tn), lambda i, j, k: (k, j))
    in_specs = [pl.BlockSpec((tm, tk), lambda i, j, k: (i, k)), b_spec]
    args = [a, b]
    if has_res:
        in_specs.append(pl.BlockSpec((tm, tn), lambda i, j, k: (i, j)))
        args.append(res)
    return _hosted_call(
        hosted, body, grid=(M // tm, N // tn, nk), in_specs=in_specs,
        out_specs=pl.BlockSpec((tm, tn), lambda i, j, k: (i, j)),
        out_shape=jax.ShapeDtypeStruct((M, N), out_dtype),
        scratch_shapes=[pltpu.VMEM((tm, tn), F32)],
        compiler_params=_cp(("parallel", "parallel", "arbitrary")), name=name)(*args)


def mm_tn(a, b, name, out_dtype=F32, a_transposed=False):
    K, T = a.shape if a_transposed else a.shape[::-1]
    _, N = b.shape
    tk = K if a_transposed else _pick(K, 512, LANES)
    tn, tt = _pick(N, MM_TILE, LANES), _pick(T, 512, LANES)
    nt = T // tt

    def body(a_ref, b_ref, o_ref, acc_ref):
        t = pl.program_id(2)

        @pl.when(t == 0)
        def _():
            acc_ref[...] = jnp.zeros_like(acc_ref)

        if a_transposed:
            acc_ref[...] += _bdot(a_ref[:, pl.ds(pl.multiple_of(t * tt, tt), tt)], b_ref[...])
        else:
            acc_ref[...] += _bdot_tn(a_ref[...], b_ref[...])

        @pl.when(t == nt - 1)
        def _():
            o_ref[...] = acc_ref[...].astype(o_ref.dtype)

    a_spec = pl.BlockSpec((K, T), lambda i, j, t: (0, 0)) if a_transposed else pl.BlockSpec((tt, tk), lambda i, j, t: (t, i))
    return pl.pallas_call(
        body, grid=(K // tk, N // tn, nt),
        in_specs=[a_spec, pl.BlockSpec((tt, tn), lambda i, j, t: (t, j))],
        out_specs=pl.BlockSpec((tk, tn), lambda i, j, t: (i, j)),
        out_shape=jax.ShapeDtypeStruct((K, N), out_dtype),
        scratch_shapes=[pltpu.VMEM((tk, tn), F32)],
        compiler_params=_cp(("parallel", "parallel", "arbitrary")), name=name)(a, b)


def _ln_apply(z, g, b):
    mu = jnp.mean(z, axis=-1, keepdims=True)
    zc = z - mu
    var = jnp.mean(zc * zc, axis=-1, keepdims=True)
    rstd = lax.rsqrt(var + LN_EPS)
    xhat = zc * rstd
    return xhat * g + b, xhat, rstd


def ln_bwd(dout, xhat, rstd, g, name):
    T, D = dout.shape
    tm = _pick(T, 512, 8)

    def body(do_ref, xh_ref, rs_ref, g_ref, dz_ref, dg_ref, db_ref):
        i = pl.program_id(0)

        @pl.when(i == 0)
        def _():
            dg_ref[...] = jnp.zeros_like(dg_ref)
            db_ref[...] = jnp.zeros_like(db_ref)

        do = do_ref[...]
        xh = xh_ref[...]
        dxh = do * g_ref[...]
        m1 = jnp.mean(dxh, axis=-1, keepdims=True)
        m2 = jnp.mean(dxh * xh, axis=-1, keepdims=True)
        dz_ref[...] = rs_ref[...] * (dxh - m1 - xh * m2)
        dg_ref[...] += jnp.sum(do * xh, axis=0, keepdims=True)
        db_ref[...] += jnp.sum(do, axis=0, keepdims=True)

    row = pl.BlockSpec((tm, D), lambda i: (i, 0))
    vec = pl.BlockSpec((1, D), lambda i: (0, 0))
    return pl.pallas_call(
        body, grid=(T // tm,),
        in_specs=[row, row, pl.BlockSpec((tm, 1), lambda i: (i, 0)), vec],
        out_specs=[row, vec, vec],
        out_shape=[jax.ShapeDtypeStruct((T, D), F32), jax.ShapeDtypeStruct((1, D), F32), jax.ShapeDtypeStruct((1, D), F32)],
        compiler_params=_cp(("arbitrary",)), name=name)(dout, xhat, rstd, g)


FFN_CHUNKS = N_DEV // 2


def ffn_fwd(h, w_in, w_out, g, b, name, hosted=None):
    T, D = h.shape
    fc = w_in.shape[2]
    half = w_out.shape[1]
    tm = _pick(T, 512, 8)
    nc = FFN_CHUNKS

    def body(h_ref, wg_ref, wu_ref, wo_ref, g_ref, b_ref, out_ref, xh_ref, rs_ref, G_ref, U_ref, ob_ref, acc_ref):
        c = pl.program_id(1)

        @pl.when(c == 0)
        def _():
            acc_ref[...] = jnp.zeros_like(acc_ref)

        hb = h_ref[...].astype(BF16)
        G = jnp.dot(hb, wg_ref[0], preferred_element_type=F32)
        U = jnp.dot(hb, wu_ref[0], preferred_element_type=F32)
        G_ref[0] = G
        U_ref[0] = U
        act = G * _sigmoid(G) * U
        acc_ref[...] += _bdot(act, wo_ref[...].reshape(2 * half, D))

        @pl.when(c == nc - 1)
        def _():
            z = ALPHA * h_ref[...] + 0.5 * acc_ref[...]
            out, xh, rs = _ln_apply(z, g_ref[...], b_ref[...])
            out_ref[...] = out
            ob_ref[...] = out.astype(BF16)
            xh_ref[...] = xh
            rs_ref[...] = rs

    row = pl.BlockSpec((tm, D), lambda i, c: (i, 0))
    vec = pl.BlockSpec((1, D), lambda i, c: (0, 0))
    cblk = pl.BlockSpec((1, tm, fc), lambda i, c: (c, i, 0))
    csds = jax.ShapeDtypeStruct((nc, T, fc), F32)
    return _hosted_call(
        hosted, body, grid=(T // tm, nc),
        in_specs=[row, pl.BlockSpec((1, D, fc), lambda i, c: (c, 0, 0)),
                  pl.BlockSpec((1, D, fc), lambda i, c: (c + nc, 0, 0)),
                  pl.BlockSpec((2, half, D), lambda i, c: (c, 0, 0)), vec, vec],
        out_specs=[row, row, pl.BlockSpec((tm, 1), lambda i, c: (i, 0)), cblk, cblk, row],
        out_shape=[jax.ShapeDtypeStruct((T, D), F32), jax.ShapeDtypeStruct((T, D), F32), jax.ShapeDtypeStruct((T, 1), F32),
                   csds, csds, jax.ShapeDtypeStruct((T, D), BF16)],
        scratch_shapes=[pltpu.VMEM((tm, D), F32)],
        compiler_params=_cp(("parallel", "arbitrary")), name=name)(h, w_in, w_in, w_out, g, b)


def ffn_bwd(dz, G, U, w_in, w_out, name, hosted=None):
    T, D = dz.shape
    nc, _, fc = G.shape
    half = w_out.shape[1]
    tm = _pick(T, 512, 16)

    def body(dz_ref, G_ref, U_ref, wg_ref, wu_ref, wo_ref, dh_ref, dG_ref, dU_ref, act_ref, acc_ref):
        c = pl.program_id(1)

        @pl.when(c == 0)
        def _():
            acc_ref[...] = jnp.zeros_like(acc_ref)

        dy = (0.5 * dz_ref[...]).astype(BF16)
        dact = _bdot_nt(dy, wo_ref[...].reshape(2 * half, D))
        G = G_ref[0]
        U = U_ref[0]
        s = _sigmoid(G)
        silu = G * s
        dG = (dact * U * (s * (1.0 + G * (1.0 - s)))).astype(BF16)
        dU = (dact * silu).astype(BF16)
        dG_ref[0] = dG
        dU_ref[0] = dU
        act_ref[0] = (silu * U).astype(BF16)
        acc_ref[...] += _bdot_nt(dG, wg_ref[0]) + _bdot_nt(dU, wu_ref[0])

        @pl.when(c == nc - 1)
        def _():
            dh_ref[...] = ALPHA * dz_ref[...] + acc_ref[...]

    row = pl.BlockSpec((tm, D), lambda i, c: (i, 0))
    cblk = pl.BlockSpec((1, tm, fc), lambda i, c: (c, i, 0))
    csds = jax.ShapeDtypeStruct((nc, T, fc), BF16)
    return _hosted_call(
        hosted, body, grid=(T // tm, nc),
        in_specs=[row, cblk, cblk, pl.BlockSpec((1, D, fc), lambda i, c: (c, 0, 0)),
                  pl.BlockSpec((1, D, fc), lambda i, c: (c + nc, 0, 0)),
                  pl.BlockSpec((2, half, D), lambda i, c: (c, 0, 0))],
        out_specs=[row, cblk, cblk, cblk],
        out_shape=[jax.ShapeDtypeStruct((T, D), F32), csds, csds, csds],
        scratch_shapes=[pltpu.VMEM((tm, D), F32)],
        compiler_params=_cp(("parallel", "arbitrary")), name=name)(dz, G, U, w_in, w_in, w_out)


def ffn_dw_in(h_t, dG, dU, name, hosted=None):
    D, T = h_t.shape
    nc, _, fc = dG.shape
    tt = _pick(T, 512, LANES)
    nt = T // tt

    def body(h_ref, dG_ref, dU_ref, o_ref, acc_ref):
        s = pl.program_id(0)
        t = pl.program_id(1)

        @pl.when(t == 0)
        def _():
            acc_ref[...] = jnp.zeros_like(acc_ref)

        hb = h_ref[:, pl.ds(pl.multiple_of(t * tt, tt), tt)]

        @pl.when(s < nc)
        def _():
            acc_ref[...] += jnp.dot(hb, dG_ref[0], preferred_element_type=F32)

        @pl.when(s >= nc)
        def _():
            acc_ref[...] += jnp.dot(hb, dU_ref[0], preferred_element_type=F32)

        @pl.when(t == nt - 1)
        def _():
            o_ref[0] = acc_ref[...].astype(o_ref.dtype)

    return _hosted_call(
        hosted, body, grid=(2 * nc, nt),
        in_specs=[pl.BlockSpec((D, T), lambda s, t: (0, 0)),
                  pl.BlockSpec((1, tt, fc), lambda s, t: (jnp.minimum(s, nc - 1), jnp.where(s < nc, t, nt - 1), 0)),
                  pl.BlockSpec((1, tt, fc), lambda s, t: (jnp.maximum(s - nc, 0), jnp.where(s >= nc, t, 0), 0))],
        out_specs=pl.BlockSpec((1, D, fc), lambda s, t: (s, 0, 0)),
        out_shape=jax.ShapeDtypeStruct((2 * nc, D, fc), BF16),
        scratch_shapes=[pltpu.VMEM((D, fc), F32)],
        compiler_params=_cp(("parallel", "arbitrary")), name=name)(h_t, dG, dU)


def ffn_dw_out(act, dz, name, hosted=None):
    nc, T, fc = act.shape
    D = dz.shape[1]
    half = fc // 2
    tt = _pick(T, 512, 16)
    nt = T // tt

    def body(a_ref, dz_ref, o_ref, acc_ref):
        t = pl.program_id(1)

        @pl.when(t == 0)
        def _():
            acc_ref[...] = jnp.zeros_like(acc_ref)

        acc_ref[...] += _bdot_tn(a_ref[0], dz_ref[...])

        @pl.when(t == nt - 1)
        def _():
            o_ref[...] = (0.5 * acc_ref[...]).reshape(2, half, D).astype(o_ref.dtype)

    return _hosted_call(
        hosted, body, grid=(nc, nt),
        in_specs=[pl.BlockSpec((1, tt, fc), lambda c, t: (c, t, 0)), pl.BlockSpec((tt, D), lambda c, t: (t, 0))],
        out_specs=pl.BlockSpec((2, half, D), lambda c, t: (c, 0, 0)),
        out_shape=jax.ShapeDtypeStruct((2 * nc, half, D), BF16),
        scratch_shapes=[pltpu.VMEM((fc, D), F32)],
        compiler_params=_cp(("parallel", "arbitrary")), name=name)(act, dz)


def proj_res_ln(parts, w, res, g, b, name):
    T, D = res.shape
    tm = _pick(T, 512, 8)
    widths = [p.shape[1] for p in parts]
    offs = [int(sum(widths[:i])) for i in range(len(parts))]
    n = len(parts)

    def body(*refs):
        p_refs = refs[:n]
        w_ref, r_ref, g_ref, b_ref, out_ref, xh_ref, rs_ref, ob_ref = refs[n:]
        acc = ALPHA * r_ref[...]
        for p_ref, o, wd in zip(p_refs, offs, widths):
            acc = acc + _bdot(p_ref[...], w_ref[o:o + wd, :])
        out, xh, rs = _ln_apply(acc, g_ref[...], b_ref[...])
        out_ref[...] = out
        ob_ref[...] = out.astype(BF16)
        xh_ref[...] = xh
        rs_ref[...] = rs

    row = pl.BlockSpec((tm, D), lambda i: (i, 0))
    vec = pl.BlockSpec((1, D), lambda i: (0, 0))
    return pl.pallas_call(
        body, grid=(T // tm,),
        in_specs=[pl.BlockSpec((tm, wd), lambda i: (i, 0)) for wd in widths]
        + [pl.BlockSpec(w.shape, lambda i: (0, 0)), row, vec, vec],
        out_specs=[row, row, pl.BlockSpec((tm, 1), lambda i: (i, 0)), row],
        out_shape=[jax.ShapeDtypeStruct((T, D), F32), jax.ShapeDtypeStruct((T, D), F32), jax.ShapeDtypeStruct((T, 1), F32),
                   jax.ShapeDtypeStruct((T, D), BF16)],
        compiler_params=_cp(("parallel",)), name=name)(*parts, w, res, g, b)


def ple_fwd(h, p, wg, wp, name):
    T, D = h.shape
    P = p.shape[1]
    tm, tn = _pick(T, 512, 8), _pick(D, 512, LANES)

    def body(h_ref, hn_ref, p_ref, wg_ref, wp_ref, out_ref, a_ref, e_ref):
        a = _bdot(h_ref[...], wg_ref[...])
        e = _bdot(p_ref[...], wp_ref[...])
        a_ref[...] = a
        e_ref[...] = e
        out_ref[...] = hn_ref[...] + _sigmoid(a) * e

    blk = pl.BlockSpec((tm, tn), lambda i, j: (i, j))
    sds = jax.ShapeDtypeStruct((T, D), F32)
    return pl.pallas_call(
        body, grid=(T // tm, D // tn),
        in_specs=[pl.BlockSpec((tm, D), lambda i, j: (i, 0)), blk, pl.BlockSpec((tm, P), lambda i, j: (i, 0)),
                  pl.BlockSpec((D, tn), lambda i, j: (0, j)), pl.BlockSpec((P, tn), lambda i, j: (0, j))],
        out_specs=[blk, blk, blk], out_shape=[sds, sds, sds],
        compiler_params=_cp(("parallel", "parallel")), name=name)(h, h, p, wg, wp)


def ple_bwd(dout, a, e, wg, name):
    T, D = dout.shape
    tm = _pick(T, 512, 16)

    def body(do_ref, a_ref, e_ref, wg_ref, dh_ref, da_ref, de_ref):
        do = do_ref[...]
        s = _sigmoid(a_ref[...])
        da = (do * e_ref[...] * s * (1.0 - s)).astype(BF16)
        da_ref[...] = da
        de_ref[...] = (do * s).astype(BF16)
        dh_ref[...] = do + _bdot_nt(da, wg_ref[...])

    row = pl.BlockSpec((tm, D), lambda i: (i, 0))
    return pl.pallas_call(
        body, grid=(T // tm,),
        in_specs=[row, row, row, pl.BlockSpec((D, D), lambda i: (0, 0))],
        out_specs=[row, row, row],
        out_shape=[jax.ShapeDtypeStruct((T, D), F32), jax.ShapeDtypeStruct((T, D), BF16), jax.ShapeDtypeStruct((T, D), BF16)],
        compiler_params=_cp(("parallel",)), name=name)(dout, a, e, wg)


def loss_head(y, target, name):
    T, D = y.shape
    tm = _pick(T, 512, 8)

    def body(y_ref, t_ref, loss_ref, dy_ref):
        i = pl.program_id(0)

        @pl.when(i == 0)
        def _():
            loss_ref[...] = jnp.zeros_like(loss_ref)

        err = y_ref[...] - t_ref[...]
        dy_ref[...] = err * (1.0 / D)
        per_tok = jnp.sum(err * err, axis=-1, keepdims=True) * (1.0 / D)
        loss_ref[...] += 0.5 * jnp.sum(per_tok, axis=0, keepdims=True)

    row = pl.BlockSpec((tm, D), lambda i: (i, 0))
    return pl.pallas_call(
        body, grid=(T // tm,), in_specs=[row, row],
        out_specs=[pl.BlockSpec((1, 1), lambda i: (0, 0)), row],
        out_shape=[jax.ShapeDtypeStruct((1, 1), F32), jax.ShapeDtypeStruct((T, D), F32)],
        compiler_params=_cp(("arbitrary",)), name=name)(y, target)


HALO = 8


def _conv_taps(pad_ref, w_ref, tm, base):
    acc = w_ref[0:1, :] * pad_ref[pl.ds(base, tm), :]
    for k in range(1, GDN_CONV):
        acc = acc + w_ref[k:k + 1, :] * pad_ref[pl.ds(base + k, tm), :]
    return acc


GDN_GROUP_W = GDN_W
GDN_PRE_ROWS = 512


def _head_segments():
    head = jnp.arange(GDN_W, dtype=jnp.int32) // GDN_D
    return (head[:, None] == head[None, :]).astype(BF16)


def _head_sums(x, seg):
    hi = x.astype(BF16)
    r1 = x - hi.astype(F32)
    mid = r1.astype(BF16)
    lo = (r1 - mid.astype(F32)).astype(BF16)
    d = functools.partial(jnp.dot, preferred_element_type=F32)
    return d(hi, seg) + d(mid, seg) + d(lo, seg)


def _gdn_pre_common(x_ref, halo_ref, w_ref, seg_ref, pad_ref, tm):
    i = pl.program_id(1)
    grp = pl.program_id(0)
    pad_ref[0:HALO, :] = jnp.where(i == 0, 0.0, halo_ref[...])
    pad_ref[HALO:HALO + tm, :] = x_ref[...]
    c = _conv_taps(pad_ref, w_ref, tm, HALO - (GDN_CONV - 1))
    s = _sigmoid(c)
    y = c * s
    r = lax.rsqrt(_head_sums(y * y, seg_ref[...]) + RMS_EPS)
    scale = jnp.where(grp < 1, GDN_D ** -0.5, 1.0)
    return grp < 2, c, s, y, r, scale


def gdn_pre_fwd(proj, conv_w, name):
    T = proj.shape[0]
    tm = _pick(T, GDN_PRE_ROWS, 8)
    GW = GDN_GROUP_W

    def body(x_ref, halo_ref, w_ref, seg_ref, o_ref, pad_ref):
        normed, c, s, y, r, scale = _gdn_pre_common(x_ref, halo_ref, w_ref, seg_ref, pad_ref, tm)
        o_ref[...] = jnp.where(normed, y * r * scale, y)

    return pl.pallas_call(
        body, grid=(3, T // tm),
        in_specs=[pl.BlockSpec((tm, GW), lambda hb, i: (i, hb)),
                  pl.BlockSpec((HALO, GW), lambda hb, i: (jnp.maximum(i * (tm // HALO) - 1, 0), hb)),
                  pl.BlockSpec((GDN_CONV, GW), lambda hb, i: (0, hb)), pl.BlockSpec((GW, GW), lambda hb, i: (0, 0))],
        out_specs=pl.BlockSpec((tm, GW), lambda hb, i: (i, hb)),
        out_shape=jax.ShapeDtypeStruct((T, 3 * GW), F32),
        scratch_shapes=[pltpu.VMEM((tm + HALO, GW), F32)],
        compiler_params=_cp(("parallel", "parallel")), name=name)(proj, proj, conv_w, _head_segments())


def gdn_pre_bwd_pointwise(proj, conv_w, dqkv, name, hosted=None):
    T = proj.shape[0]
    tm = _pick(T, GDN_PRE_ROWS, 8)
    GW = GDN_GROUP_W

    def body(x_ref, halo_ref, w_ref, seg_ref, d_ref, dc_ref, dw_ref, pad_ref):
        i = pl.program_id(1)
        normed, c, s, y, r, scale = _gdn_pre_common(x_ref, halo_ref, w_ref, seg_ref, pad_ref, tm)

        @pl.when(i == 0)
        def _():
            dw_ref[...] = jnp.zeros_like(dw_ref)

        d = d_ref[...]
        n = y * r
        dn = d * scale
        dy = jnp.where(normed, r * (dn - n * _head_sums(dn * n, seg_ref[...])), d)
        dc = dy * (s * (1.0 + c * (1.0 - s)))
        dc_ref[...] = dc
        for k in range(GDN_CONV):
            xs = pad_ref[pl.ds(HALO - (GDN_CONV - 1) + k, tm), :]
            dw_ref[k:k + 1, :] += jnp.sum(dc * xs, axis=0, keepdims=True)

    blk = pl.BlockSpec((tm, GW), lambda hb, i: (i, hb))
    wblk = pl.BlockSpec((GDN_CONV, GW), lambda hb, i: (0, hb))
    return _hosted_call(
        hosted, body, grid=(3, T // tm),
        in_specs=[blk, pl.BlockSpec((HALO, GW), lambda hb, i: (jnp.maximum(i * (tm // HALO) - 1, 0), hb)), wblk,
                  pl.BlockSpec((GW, GW), lambda hb, i: (0, 0)), blk],
        out_specs=[blk, wblk],
        out_shape=[jax.ShapeDtypeStruct((T, 3 * GW), F32), jax.ShapeDtypeStruct((GDN_CONV, 3 * GW), F32)],
        scratch_shapes=[pltpu.VMEM((tm + HALO, GW), F32)],
        compiler_params=_cp(("parallel", "arbitrary")), name=name)(proj, proj, conv_w, _head_segments(), dqkv)


def gdn_pre_bwd_conv(dc, conv_w_p, name):
    T = dc.shape[0]
    tm = _pick(T, GDN_PRE_ROWS, 8)
    nt = T // tm
    GW = GDN_GROUP_W

    def body(dc_ref, halo_ref, w_ref, dx_ref, pad_ref):
        i = pl.program_id(1)
        pad_ref[0:tm, :] = dc_ref[...]
        pad_ref[tm:tm + HALO, :] = jnp.where(i == nt - 1, 0.0, halo_ref[...])
        acc = w_ref[GDN_CONV - 1:GDN_CONV, :] * pad_ref[pl.ds(0, tm), :]
        for k in range(GDN_CONV - 1):
            acc = acc + w_ref[k:k + 1, :] * pad_ref[pl.ds(GDN_CONV - 1 - k, tm), :]
        dx_ref[...] = acc

    blk = pl.BlockSpec((tm, GW), lambda hb, i: (i, hb))
    return pl.pallas_call(
        body, grid=(3, nt),
        in_specs=[blk, pl.BlockSpec((HALO, GW), lambda hb, i: (jnp.minimum((i + 1) * (tm // HALO), T // HALO - 1), hb)),
                  pl.BlockSpec((GDN_CONV, GW), lambda hb, i: (0, hb))],
        out_specs=blk,
        out_shape=jax.ShapeDtypeStruct((T, 3 * GW), F32),
        scratch_shapes=[pltpu.VMEM((tm + HALO, GW), F32)],
        compiler_params=_cp(("parallel", "parallel")), name=name)(dc, dc, conv_w_p)


def _chunk_masks(C):
    row = _iota2((C, C), 0)
    col = _iota2((C, C), 1)
    return row >= col, row > col, row == col


BNN = (((2,), (1,)), ((0,), (0,)))
BNT = (((2,), (2,)), ((0,), (0,)))
BTN = (((1,), (1,)), ((0,), (0,)))


def _bmm(a, b, dims=BNN):
    return lax.dot_general(a.astype(BF16), b.astype(BF16), dims, preferred_element_type=F32)


def _hbmm(a, b):
    m = a.shape[1]
    a_hi, a_lo = _split2(a)
    b_hi, b_lo = _split2(b)
    r = lax.dot_general(jnp.concatenate([a_hi, a_lo], axis=1), b_hi, BNN, preferred_element_type=F32)
    return r[:, :m] + r[:, m:] + lax.dot_general(a_hi, b_lo, BNN, preferred_element_type=F32)


def _hbmm_tn(a, b):
    a_hi, a_lo = _split2(a)
    b_hi, b_lo = _split2(b)
    d = functools.partial(lax.dot_general, dimension_numbers=BTN, preferred_element_type=F32)
    return d(a_hi, b_hi) + d(a_lo, b_hi) + d(a_hi, b_lo)


def _col_to_row(colv, eye):
    return jnp.sum(jnp.where(eye, colv, 0.0), axis=1, keepdims=True)


def _row_to_col(rowv, eye):
    return jnp.sum(jnp.where(eye, rowv, 0.0), axis=2, keepdims=True)


def _unit_lower_inverse(A, eye):
    C = A.shape[1]
    P = jnp.where(eye, 1.0, 0.0) - A
    Bp = _hbmm(A, A)
    for _ in range(4):
        R = _hbmm(jnp.concatenate([Bp, P], axis=1), Bp)
        Bp = R[:, :C]
        P = P + R[:, C:]
    return P + _hbmm(P, Bp)


def _stack_heads(ref, first_head, n):
    return jnp.stack([ref[:, pl.ds((first_head + h) * GDN_D, GDN_D)] for h in range(n)])


def _unstack_heads(ref, first_head, val):
    for h in range(val.shape[0]):
        ref[:, pl.ds((first_head + h) * GDN_D, GDN_D)] = val[h]


def _gdn_gates(gab, a_row, dt_row, incl):
    g_all = -jnp.exp(a_row) * _softplus(gab + dt_row)
    beta_all = _sigmoid(gab)
    gc_all = _ones_dot_left(incl.astype(BF16), g_all)
    return g_all, beta_all, gc_all


def _gdn_common(qkv_ref, gc_all, beta_all, incl, strict, eye):
    C, H = GDN_CHUNK, GDN_HEADS
    q, k, v = (_stack_heads(qkv_ref, j * H, H) for j in range(3))
    gc = jnp.stack([gc_all[:, h:h + 1] for h in range(H)])
    beta = jnp.stack([beta_all[:, H + h:H + h + 1] for h in range(H)])
    gc_row = _col_to_row(gc, eye)
    decay = jnp.where(incl, jnp.exp(jnp.where(incl, gc - gc_row, 0.0)), 0.0)
    e_gc = jnp.exp(gc)
    gl = gc[:, C - 1:C, :]
    e_gl = jnp.exp(gl)
    ekd = jnp.exp(gl - gc)
    kb = k * beta
    A = jnp.where(strict, _bmm(kb, k, BNT) * decay, 0.0)
    Pm = jnp.where(incl, _bmm(q, k, BNT) * decay, 0.0)
    return q, k, v, gc, beta, decay, e_gc, e_gl, ekd, kb, A, Pm


def gdn_chunk_fwd(qkv, proj, a_row, dt_row, norm_w, name, hosted=None):
    T = qkv.shape[0]
    C, H, Dh = GDN_CHUNK, GDN_HEADS, GDN_D
    N = T // C

    def body(qkv_ref, gz_ref, gab_ref, a_ref, dt_ref, nw_ref, o_ref, opre_ref, Tm_ref, Sin_ref, S_ref):
        n = pl.program_id(0)

        @pl.when(n == 0)
        def _():
            S_ref[...] = jnp.zeros_like(S_ref)

        incl, strict, eye = _chunk_masks(C)
        _, beta_all, gc_all = _gdn_gates(gab_ref[...], a_ref[...], dt_ref[...], incl)
        q, k, v, gc, beta, decay, e_gc, e_gl, ekd, kb, A, Pm = _gdn_common(qkv_ref, gc_all, beta_all, incl, strict, eye)
        Tm = _unit_lower_inverse(A, eye)
        u = _hbmm(Tm, v * beta)
        w = _hbmm(Tm, kb * e_gc)
        S = S_ref[...]
        v_new = u - _bmm(w, S)
        o = _bmm(q * e_gc, S) + _bmm(Pm, v_new)
        S_ref[...] = S * e_gl + _bmm(k * ekd, v_new, BTN)
        Sin_ref[0] = S
        Tm_ref[0] = Tm
        r = lax.rsqrt(jnp.mean(o * o, axis=-1, keepdims=True) + RMS_EPS)
        gz = _stack_heads(gz_ref, 0, H)
        _unstack_heads(opre_ref, 0, o)
        _unstack_heads(o_ref, 0, o * r * nw_ref[...] * (gz * _sigmoid(gz)))

    vec = pl.BlockSpec((1, LANES), lambda n: (0, 0))
    hblk = pl.BlockSpec((C, GDN_W), lambda n: (n, 0))
    sblk = pl.BlockSpec((1, H, Dh, Dh), lambda n: (n, 0, 0, 0))
    return _hosted_call(
        hosted, body, grid=(N,),
        in_specs=[pl.BlockSpec((C, 3 * GDN_W), lambda n: (n, 0)),
                  pl.BlockSpec((C, GDN_W), lambda n: (n, CB_GZ * LANES // GDN_W)),
                  pl.BlockSpec((C, LANES), lambda n: (n, CB_GAB)), vec, vec, pl.BlockSpec((1, Dh), lambda n: (0, 0))],
        out_specs=[hblk, hblk, sblk, sblk],
        out_shape=[jax.ShapeDtypeStruct((T, GDN_W), F32), jax.ShapeDtypeStruct((T, GDN_W), F32),
                   jax.ShapeDtypeStruct((N, H, Dh, Dh), F32), jax.ShapeDtypeStruct((N, H, Dh, Dh), F32)],
        scratch_shapes=[pltpu.VMEM((H, Dh, Dh), F32)],
        compiler_params=_cp(("arbitrary",)), name=name)(qkv, proj, proj, a_row, dt_row, norm_w)


def gdn_chunk_bwd(qkv, proj, a_row, dt_row, norm_w, opre, Tm_all, Sin_all, docat, name, hosted=None):
    T = qkv.shape[0]
    C, H, Dh = GDN_CHUNK, GDN_HEADS, GDN_D
    N = T // C

    def body(qkv_ref, gz_ref, gab_ref, a_ref, dt_ref, nw_ref, opre_ref, Tm_ref, Sin_ref, do_ref,
             dqkv_ref, dgz_ref, dgab_ref, da_ref, ddt_ref, dnw_ref, dS_ref):
        n = pl.program_id(0)

        @pl.when(n == 0)
        def _():
            dS_ref[...] = jnp.zeros_like(dS_ref)
            da_ref[...] = jnp.zeros_like(da_ref)
            ddt_ref[...] = jnp.zeros_like(ddt_ref)
            dnw_ref[...] = jnp.zeros_like(dnw_ref)

        incl, strict, eye = _chunk_masks(C)
        gab = gab_ref[...]
        g_all, beta_all, gc_all = _gdn_gates(gab, a_ref[...], dt_ref[...], incl)
        lane = _iota2((C, LANES), 1)
        rowi = _iota2((C, 1), 0)
        nw = nw_ref[...]
        q, k, v, gc, beta, decay, e_gc, e_gl, ekd, kb, A, Pm = _gdn_common(qkv_ref, gc_all, beta_all, incl, strict, eye)
        Tm = Tm_ref[0]
        S = Sin_ref[0]
        dS = dS_ref[...]
        kbe = kb * e_gc
        u = _hbmm(Tm, v * beta)
        w = _hbmm(Tm, kbe)
        qd = q * e_gc
        kd = k * ekd
        v_new = u - _bmm(w, S)
        o = _stack_heads(opre_ref, 0, H)
        gz = _stack_heads(gz_ref, 0, H)
        don = _stack_heads(do_ref, 0, H)
        r = lax.rsqrt(jnp.mean(o * o, axis=-1, keepdims=True) + RMS_EPS)
        nn = o * r
        sgz = _sigmoid(gz)
        silu = gz * sgz
        _unstack_heads(dgz_ref, 0, don * nn * nw * (sgz * (1.0 + gz * (1.0 - sgz))))
        dnn = don * nw * silu
        dnw_ref[...] += jnp.sum(jnp.sum(don * nn * silu, axis=0), axis=0, keepdims=True)
        do = r * (dnn - nn * jnp.mean(dnn * nn, axis=-1, keepdims=True))
        dv_new = _bmm(Pm, do, BTN) + _bmm(kd, dS)
        dPm = jnp.where(incl, _bmm(do, v_new, BNT), 0.0)
        dqd = _bmm(do, S, BNT)
        dkd = _bmm(v_new, dS, BNT)
        dS_ref[...] = _bmm(qd, do, BTN) + e_gl * dS - _bmm(w, dv_new, BTN)
        dgl = jnp.sum(jnp.sum(dS * S, axis=2, keepdims=True), axis=1, keepdims=True) * e_gl
        dw = -_bmm(dv_new, S, BNT)
        dvb = _hbmm_tn(Tm, dv_new)
        dkbe = _hbmm_tn(Tm, dw)
        dA = -jnp.where(strict, _bmm(dvb, u, BNT) + _bmm(dkbe, w, BNT), 0.0)
        dAD = dA * decay
        dPD = dPm * decay
        Gm = dA * A + dPm * Pm
        dgc = jnp.sum(Gm, axis=2, keepdims=True) - _row_to_col(jnp.sum(Gm, axis=1, keepdims=True), eye)
        dkb = _bmm(dAD, k) + dkbe * e_gc
        dk = _bmm(dAD, kb, BTN) + _bmm(dPD, q, BTN) + dkd * ekd + dkb * beta
        dq = _bmm(dPD, k) + dqd * e_gc
        tkd = jnp.sum(dkd * kd, axis=-1, keepdims=True)
        dgc = dgc + jnp.sum(dqd * qd, axis=-1, keepdims=True) - tkd + jnp.sum(dkbe * kbe, axis=-1, keepdims=True)
        dgl = dgl + jnp.sum(tkd, axis=1, keepdims=True)
        dgc = dgc + jnp.where(rowi == C - 1, dgl, 0.0)
        dbeta = jnp.sum(dvb * v, axis=-1, keepdims=True) + jnp.sum(dkb * k, axis=-1, keepdims=True)
        _unstack_heads(dqkv_ref, 0, dq)
        _unstack_heads(dqkv_ref, H, dk)
        _unstack_heads(dqkv_ref, 2 * H, dvb * beta)
        dgc_all = jnp.zeros((C, LANES), F32)
        dbeta_all = jnp.zeros((C, LANES), F32)
        for h in range(H):
            dgc_all = dgc_all + jnp.where(lane == h, dgc[h], 0.0)
            dbeta_all = dbeta_all + jnp.where(lane == H + h, dbeta[h], 0.0)
        upper = (_iota2((C, C), 0) <= _iota2((C, C), 1)).astype(BF16)
        dg_all = _ones_dot_left(upper, dgc_all)
        dga = dg_all * (-jnp.exp(a_ref[...])) * _sigmoid(gab + dt_ref[...])
        dgb = dbeta_all * beta_all * (1.0 - beta_all)
        dgab_ref[...] = jnp.where(lane < H, dga, jnp.where(lane < 2 * H, dgb, 0.0))
        da_ref[...] += jnp.sum(jnp.where(lane < H, dg_all * g_all, 0.0), axis=0, keepdims=True)
        ddt_ref[...] += jnp.sum(jnp.where(lane < H, dga, 0.0), axis=0, keepdims=True)

    rev = lambda n: N - 1 - n
    vec = pl.BlockSpec((1, LANES), lambda n: (0, 0))
    nwv = pl.BlockSpec((1, Dh), lambda n: (0, 0))
    hblk = pl.BlockSpec((C, GDN_W), lambda n: (rev(n), 0))
    sblk = pl.BlockSpec((1, H, Dh, Dh), lambda n: (rev(n), 0, 0, 0))
    qblk = pl.BlockSpec((C, 3 * GDN_W), lambda n: (rev(n), 0))
    return _hosted_call(
        hosted, body, grid=(N,),
        in_specs=[qblk, pl.BlockSpec((C, GDN_W), lambda n: (rev(n), CB_GZ * LANES // GDN_W)),
                  pl.BlockSpec((C, LANES), lambda n: (rev(n), CB_GAB)), vec, vec, nwv, hblk, sblk, sblk, hblk],
        out_specs=[qblk, hblk, pl.BlockSpec((C, LANES), lambda n: (rev(n), 0)), vec, vec, nwv],
        out_shape=[jax.ShapeDtypeStruct((T, 3 * GDN_W), F32), jax.ShapeDtypeStruct((T, GDN_W), F32),
                   jax.ShapeDtypeStruct((T, LANES), F32), jax.ShapeDtypeStruct((1, LANES), F32),
                   jax.ShapeDtypeStruct((1, LANES), F32), jax.ShapeDtypeStruct((1, Dh), F32)],
        scratch_shapes=[pltpu.VMEM((H, Dh, Dh), F32)],
        compiler_params=_cp(("arbitrary",)), name=name)(qkv, proj, proj, a_row, dt_row, norm_w, opre, Tm_all, Sin_all, docat)


ATT_BQ, ATT_BK = 512, 512
NEG_BIG = -1e30


def _att_blocks(T):
    bq, bk = min(ATT_BQ, T), min(ATT_BK, T)
    assert bk % bq == 0 and T % bk == 0
    return bq, bk


def _att_specs(T, bq, cbs):
    qspec = lambda cb: pl.BlockSpec((bq, LANES), lambda h, i: (i, cb + h))
    kspec = lambda cb: pl.BlockSpec((T, LANES), lambda h, i: (0, cb + h))
    return qspec, kspec


def _kblock(ref, kb, bk):
    return ref[pl.ds(pl.multiple_of(kb * bk, bk), bk), :]


def _att_pos(i, kb, bq, bk):
    qpos = i * bq + _iota2((bq, bk), 0)
    kpos = kb * bk + _iota2((bq, bk), 1)
    return qpos, kpos


def _later_keys(n):
    return (_iota2((n, n), 0) > _iota2((n, n), 1)).astype(BF16)


def _earlier_keys(n):
    return (_iota2((n, n), 0) < _iota2((n, n), 1)).astype(BF16)


def _tri_dot(x, tri, terms):
    acc, rest = None, x
    for t in range(terms):
        part = rest.astype(BF16)
        if t + 1 < terms:
            rest = rest - part.astype(F32)
        d = jnp.dot(part, tri, preferred_element_type=F32)
        acc = d if acc is None else acc + d
    return acc


SB_BLOCK = 256
SB_DEAD = -104.0


def _sb_blocks(T):
    b = min(SB_BLOCK, T)
    assert T % b == 0 and T // b <= LANES
    return b, b


def sb_fwd(proj, name, hosted=None):
    T = proj.shape[0]
    H = SB_HEADS
    bq, bk = _sb_blocks(T)
    scale = SB_DIM ** -0.5

    def body(q_ref, k_ref, v_ref, o_ref, tot_ref):
        i = pl.program_id(1)
        qb = q_ref[...].astype(BF16)
        diag = (i * bq) // bk
        lane = _iota2((bq, LANES), 1)
        later = _later_keys(bk)

        def block(kb, acc, R, masked):
            z = _bdot_nt(qb, _kblock(k_ref, kb, bk)) * scale
            sp = _softplus(z)
            if masked:
                qpos, kpos = _att_pos(i, kb, bq, bk)
                mask = kpos < qpos
                l1m = jnp.where(mask, -sp, 0.0)
            else:
                l1m = -sp
            W = jnp.exp((z - sp) + _tri_dot(l1m, later, 3) + R)
            if masked:
                W = jnp.where(mask, W, 0.0)
            acc = acc + _bdot(W, _kblock(v_ref, kb, bk))
            return acc, R + jnp.sum(l1m, axis=-1, keepdims=True)

        acc, R = block(diag, jnp.zeros((bq, LANES), F32), jnp.zeros((bq, 1), F32), True)

        def live(c):
            return jnp.logical_and(c[0] >= 0, jnp.max(c[2]) > SB_DEAD)

        def step(c):
            kb, acc, R, Rb = c
            acc, R_next = block(kb, acc, R, False)
            return kb - 1, acc, R_next, jnp.where(lane == kb, R, Rb)

        _, acc, _, Rb = lax.while_loop(live, step, (diag - 1, acc, R, jnp.where(lane == diag, 0.0, NEG_BIG)))
        o_ref[...] = acc
        tot_ref[...] = Rb

    qspec, kspec = _att_specs(T, bq, None)
    sds = jax.ShapeDtypeStruct((T, H * LANES), F32)
    oblk = pl.BlockSpec((bq, LANES), lambda h, i: (i, h))
    return _hosted_call(
        hosted, body, grid=(H, T // bq), in_specs=[qspec(CB_SQ), kspec(CB_SK), kspec(CB_SV)],
        out_specs=[oblk, oblk], out_shape=[sds, sds],
        compiler_params=_cp(("parallel", "parallel")), name=name)(proj, proj, proj)


def sb_bwd(proj, tot, docat, do_cb, name):
    T = proj.shape[0]
    H = SB_HEADS
    bq, bk = _sb_blocks(T)
    scale = SB_DIM ** -0.5

    def body(q_ref, k_ref, v_ref, tot_ref, do_ref, dq_ref, dk_ref, dv_ref):
        i = pl.program_id(1)

        @pl.when(i == 0)
        def _():
            dk_ref[...] = jnp.zeros_like(dk_ref)
            dv_ref[...] = jnp.zeros_like(dv_ref)

        qb = q_ref[...].astype(BF16)
        dob = do_ref[...].astype(BF16)
        Rb = tot_ref[...]
        diag = (i * bq) // bk
        lane = _iota2((bq, LANES), 1)
        later, earlier = _later_keys(bk), _earlier_keys(bk)
        first = lax.while_loop(
            lambda kb: jnp.logical_and(kb < diag, jnp.max(jnp.where(lane == kb, Rb, NEG_BIG)) <= SB_DEAD),
            lambda kb: kb + 1, jnp.int32(0))

        def block(kb, carry, masked):
            dq, Epre = carry
            R = jnp.sum(jnp.where(lane == kb, Rb, 0.0), axis=1, keepdims=True)
            kblk = _kblock(k_ref, kb, bk).astype(BF16)
            z = _bdot_nt(qb, kblk) * scale
            sp = _softplus(z)
            if masked:
                qpos, kpos = _att_pos(i, kb, bq, bk)
                mask = kpos < qpos
                l1m = jnp.where(mask, -sp, 0.0)
            else:
                l1m = -sp
            W = jnp.exp((z - sp) + _tri_dot(l1m, later, 3) + R)
            if masked:
                W = jnp.where(mask, W, 0.0)
            E = _bdot_nt(dob, _kblock(v_ref, kb, bk)) * W
            cexcl = _tri_dot(E, earlier, 3) + Epre
            neg = jnp.exp(-sp)
            dz = E * neg - cexcl * (1.0 - neg)
            if masked:
                dz = jnp.where(mask, dz, 0.0)
            dz = (dz * scale).astype(BF16)
            rows = pl.ds(pl.multiple_of(kb * bk, bk), bk)
            dk_ref[rows, :] += lax.dot_general(dz, qb, TN_DIMS, preferred_element_type=F32)
            dv_ref[rows, :] += lax.dot_general(W.astype(BF16), dob, TN_DIMS, preferred_element_type=F32)
            dq = dq + jnp.dot(dz, kblk, preferred_element_type=F32)
            return dq, Epre + jnp.sum(E, axis=-1, keepdims=True)

        init = (jnp.zeros((bq, LANES), F32), jnp.zeros((bq, 1), F32))
        carry = lax.fori_loop(first, diag, lambda kb, c: block(kb, c, False), init)
        dq, _ = block(diag, carry, True)
        dq_ref[...] = dq

    qspec, kspec = _att_specs(T, bq, None)
    sds = jax.ShapeDtypeStruct((T, H * LANES), F32)
    oblk = pl.BlockSpec((bq, LANES), lambda h, i: (i, h))
    kout = pl.BlockSpec((T, LANES), lambda h, i: (0, h))
    return pl.pallas_call(
        body, grid=(H, T // bq),
        in_specs=[qspec(CB_SQ), kspec(CB_SK), kspec(CB_SV), oblk, qspec(do_cb)],
        out_specs=[oblk, kout, kout], out_shape=[sds, sds, sds],
        compiler_params=_cp(("arbitrary", "arbitrary")), name=name)(proj, proj, proj, tot, docat)


def mla_fwd(Q, K, V, name, hosted=None):
    T = Q.shape[0]
    H = MLA_HEADS
    bq, bk = _att_blocks(T)
    scale = (MLA_NOPE + MLA_ROPE) ** -0.5

    def body(q_ref, k_ref, v_ref, o_ref, lse_ref):
        i = pl.program_id(1)
        qb = q_ref[...]
        diag = (i * bq) // bk

        def block(kb, carry, masked):
            acc, m, l = carry
            s = _bdot_nt(qb, _kblock(k_ref, kb, bk)) * scale
            if masked:
                qpos, kpos = _att_pos(i, kb, bq, bk)
                s = jnp.where(kpos <= qpos, s, NEG_BIG)
            m_new = jnp.maximum(m, jnp.max(s, axis=-1, keepdims=True))
            p = jnp.exp(s - m_new)
            corr = jnp.exp(m - m_new)
            acc = corr * acc + _bdot(p, _kblock(v_ref, kb, bk))
            return acc, m_new, corr * l + jnp.sum(p, axis=-1, keepdims=True)

        init = (jnp.zeros((bq, LANES), F32), jnp.full((bq, 1), NEG_BIG, F32), jnp.zeros((bq, 1), F32))
        carry = lax.fori_loop(0, diag, lambda kb, c: block(kb, c, False), init)
        acc, m, l = block(diag, carry, True)
        o_ref[...] = acc / l
        lse_ref[...] = jnp.broadcast_to(m + jnp.log(l), (bq, LANES))

    qspec, kspec = _att_specs(T, bq, None)
    sds = jax.ShapeDtypeStruct((T, H * LANES), F32)
    oblk = pl.BlockSpec((bq, LANES), lambda h, i: (i, h))
    return _hosted_call(
        hosted, body, grid=(H, T // bq), in_specs=[qspec(0), kspec(0), kspec(0)],
        out_specs=[oblk, oblk], out_shape=[sds, sds],
        compiler_params=_cp(("parallel", "parallel")), name=name)(Q, K, V)


def mla_bwd(Q, K, V, o, lse, docat, do_cb, name, hosted=None):
    T = Q.shape[0]
    H = MLA_HEADS
    bq, bk = _att_blocks(T)
    scale = (MLA_NOPE + MLA_ROPE) ** -0.5

    def body(q_ref, k_ref, v_ref, o_ref, lse_ref, do_ref, dq_ref, dk_ref, dv_ref):
        i = pl.program_id(1)

        @pl.when(i == 0)
        def _():
            dk_ref[...] = jnp.zeros_like(dk_ref)
            dv_ref[...] = jnp.zeros_like(dv_ref)

        qb = q_ref[...]
        do = do_ref[...]
        dob = do.astype(BF16)
        delta = jnp.sum(do * o_ref[...], axis=-1, keepdims=True)
        lse = lse_ref[:, 0:1]

        diag = (i * bq) // bk

        def block(kb, dq, masked):
            kblk = _kblock(k_ref, kb, bk)
            s = _bdot_nt(qb, kblk) * scale
            if masked:
                qpos, kpos = _att_pos(i, kb, bq, bk)
                s = jnp.where(kpos <= qpos, s, NEG_BIG)
            p = jnp.exp(s - lse)
            dp = _bdot_nt(dob, _kblock(v_ref, kb, bk))
            ds = (p * (dp - delta) * scale).astype(BF16)
            rows = pl.ds(pl.multiple_of(kb * bk, bk), bk)
            dk_ref[rows, :] += lax.dot_general(ds, qb, TN_DIMS, preferred_element_type=F32)
            dv_ref[rows, :] += lax.dot_general(p.astype(BF16), dob, TN_DIMS, preferred_element_type=F32)
            return dq + jnp.dot(ds, kblk, preferred_element_type=F32)

        dq = lax.fori_loop(0, diag, lambda kb, c: block(kb, c, False), jnp.zeros((bq, LANES), F32))
        dq_ref[...] = block(diag, dq, True)

    qspec, kspec = _att_specs(T, bq, None)
    sds = jax.ShapeDtypeStruct((T, H * LANES), F32)
    oblk = pl.BlockSpec((bq, LANES), lambda h, i: (i, h))
    kout = pl.BlockSpec((T, LANES), lambda h, i: (0, h))
    return _hosted_call(
        hosted, body, grid=(H, T // bq),
        in_specs=[qspec(0), kspec(0), kspec(0), oblk, oblk, qspec(do_cb)],
        out_specs=[oblk, kout, kout], out_shape=[sds, sds, sds],
        compiler_params=_cp(("arbitrary", "arbitrary")), name=name)(Q, K, V, o, lse, docat)


def _tile_heads(t, n):
    return jnp.concatenate([t] * n, axis=1)


def _rope(X, C, Sn, Sp):
    n = X.shape[1]
    return X * C + pltpu.roll(X, n - HALF_ROPE, 1) * Sn + pltpu.roll(X, HALF_ROPE, 1) * Sp


def _rope_t(dO, C, Sn, Sp):
    n = dO.shape[1]
    return dO * C + pltpu.roll(dO * Sn, HALF_ROPE, 1) + pltpu.roll(dO * Sp, n - HALF_ROPE, 1)


def _rms(x, w):
    r = lax.rsqrt(jnp.mean(x * x, axis=-1, keepdims=True) + RMS_EPS)
    xh = x * r
    return r, xh, xh * w


def _rms_bwd(dn, w, r, xh):
    dxh = dn * w
    return r * (dxh - xh * jnp.mean(dxh * xh, axis=-1, keepdims=True)), jnp.sum(dn * xh, axis=0, keepdims=True)


def _mla_pre_specs(T, tm):
    KV = MLA_KV_RANK
    QR = MLA_Q_RANK
    W = MLA_HEADS * LANES
    full = lambda shape: pl.BlockSpec(shape, lambda i: (0, 0))
    specs = [pl.BlockSpec((tm, QR), lambda i: (i, CB_MQ * LANES // QR)),
             pl.BlockSpec((tm, 2 * LANES), lambda i: (i, CB_MKV // 2)),
             full((1, QR)), full((1, KV))]
    rope = [pl.BlockSpec((tm, LANES), lambda i: (i, 0))] * 3
    return specs, rope, full, W


def mla_pre_fwd(proj, wq, wkv, wuq, wuk, wuv, ropeC, ropeSn, ropeSp, name):
    T = proj.shape[0]
    tm = _pick(T, 512, 16)
    KV = MLA_KV_RANK
    H = MLA_HEADS

    def body(mq_ref, mkv_ref, wq_ref, wkv_ref, wuq_ref, wuk_ref, wuv_ref, c_ref, sn_ref, sp_ref, Q_ref, K_ref, V_ref):
        C, Sn, Sp = (_tile_heads(t[...], H) for t in (c_ref, sn_ref, sp_ref))
        _, _, qn = _rms(mq_ref[...], wq_ref[...])
        Q_ref[...] = _rope(_bdot(qn, wuq_ref[...]), C, Sn, Sp).astype(BF16)
        mkv = mkv_ref[...]
        _, _, kvn = _rms(mkv[:, :KV], wkv_ref[...])
        kr = pltpu.roll(mkv[:, KV:], MLA_NOPE, 1)
        K_ref[...] = _rope(_bdot(kvn, wuk_ref[...]) + _tile_heads(kr, H), C, Sn, Sp).astype(BF16)
        V_ref[...] = _bdot(kvn, wuv_ref[...]).astype(BF16)

    specs, rope, full, W = _mla_pre_specs(T, tm)
    oblk = pl.BlockSpec((tm, W), lambda i: (i, 0))
    sds = jax.ShapeDtypeStruct((T, W), BF16)
    return pl.pallas_call(
        body, grid=(T // tm,),
        in_specs=specs + [full(wuq.shape), full(wuk.shape), full(wuv.shape)] + rope,
        out_specs=[oblk, oblk, oblk], out_shape=[sds, sds, sds],
        compiler_params=_cp(("parallel",)), name=name)(proj, proj, wq, wkv, wuq, wuk, wuv, ropeC, ropeSn, ropeSp)


def mla_pre_bwd(proj, wq, wkv, wuq, wuk, wuv, ropeC, ropeSn, ropeSp, dQ, dK, dV, name):
    T = proj.shape[0]
    tm = _pick(T, 512, 16)
    KV = MLA_KV_RANK
    H = MLA_HEADS

    def body(mq_ref, mkv_ref, wq_ref, wkv_ref, wuq_ref, wuk_ref, wuv_ref,
             c_ref, sn_ref, sp_ref, dQ_ref, dK_ref, dV_ref,
             dmq_ref, dmkv_ref, dwuq_ref, dwuk_ref, dwuv_ref, dwq_ref, dwkv_ref):
        i = pl.program_id(0)

        @pl.when(i == 0)
        def _():
            for ref in (dwuq_ref, dwuk_ref, dwuv_ref, dwq_ref, dwkv_ref):
                ref[...] = jnp.zeros_like(ref)

        C, Sn, Sp = (_tile_heads(t[...], H) for t in (c_ref, sn_ref, sp_ref))
        rq, xq, qn = _rms(mq_ref[...], wq_ref[...])
        mkv = mkv_ref[...]
        rkv, xkv, kvn = _rms(mkv[:, :KV], wkv_ref[...])
        dqf = _rope_t(dQ_ref[...], C, Sn, Sp)
        dkf = _rope_t(dK_ref[...], C, Sn, Sp)
        dv = dV_ref[...]
        dwuq_ref[...] += _bdot_tn(qn, dqf)
        dwuk_ref[...] += _bdot_tn(kvn, dkf)
        dwuv_ref[...] += _bdot_tn(kvn, dv)
        dmq, dwq = _rms_bwd(_bdot_nt(dqf, wuq_ref[...]), wq_ref[...], rq, xq)
        dckv, dwkv = _rms_bwd(_bdot_nt(dkf, wuk_ref[...]) + _bdot_nt(dv, wuv_ref[...]), wkv_ref[...], rkv, xkv)
        dwq_ref[...] += dwq
        dwkv_ref[...] += dwkv
        dmq_ref[...] = dmq
        dkr = dkf[:, 0:LANES]
        for h in range(1, H):
            dkr = dkr + dkf[:, h * LANES:(h + 1) * LANES]
        dkr = pltpu.roll(dkr, LANES - MLA_NOPE, 1)
        dkr = jnp.where(_iota2(dkr.shape, 1) < MLA_ROPE, dkr, 0.0)
        dmkv_ref[...] = jnp.concatenate([dckv, dkr], axis=1)

    specs, rope, full, W = _mla_pre_specs(T, tm)
    wide = pl.BlockSpec((tm, W), lambda i: (i, 0))
    return pl.pallas_call(
        body, grid=(T // tm,),
        in_specs=specs + [full(w.shape) for w in (wuq, wuk, wuv)] + rope + [wide, wide, wide],
        out_specs=[pl.BlockSpec((tm, MLA_Q_RANK), lambda i: (i, 0)), pl.BlockSpec((tm, 2 * LANES), lambda i: (i, 0)),
                   full(wuq.shape), full(wuk.shape), full(wuv.shape), full((1, MLA_Q_RANK)), full((1, KV))],
        out_shape=[jax.ShapeDtypeStruct((T, MLA_Q_RANK), F32), jax.ShapeDtypeStruct((T, 2 * LANES), F32),
                   jax.ShapeDtypeStruct(wuq.shape, F32), jax.ShapeDtypeStruct(wuk.shape, F32),
                   jax.ShapeDtypeStruct(wuv.shape, F32), jax.ShapeDtypeStruct((1, MLA_Q_RANK), F32),
                   jax.ShapeDtypeStruct((1, KV), F32)],
        compiler_params=_cp(("arbitrary",)), name=name)(
            proj, proj, wq, wkv, wuq, wuk, wuv, ropeC, ropeSn, ropeSp, dQ, dK, dV)


def all_gather(shards, name):
    n = len(shards)

    def body(*refs):
        x_refs, out_refs = refs[:n], refs[n:2 * n]
        send_sems, recv_sems, local_sems = refs[2 * n:]
        x, y, c = _place()
        me, sibling = (x, y, c), (x, y, 1 - c)
        chips = [(1 - x, y), (x, 1 - y), (1 - x, 1 - y)]

        def slot(a, px, py, pc):
            return out_refs[a].at[4 * px + 2 * py + pc]

        def copy(a, k, block, to, src=None):
            return pltpu.make_async_remote_copy(
                src_ref=slot(a, *block) if src is None else src, dst_ref=slot(a, *block),
                send_sem=send_sems.at[a, k], recv_sem=recv_sems.at[a, k], device_id=to, device_id_type=MESH)

        mine = [pltpu.make_async_copy(x_refs[a], slot(a, *me), local_sems.at[a]) for a in range(n)]
        first = []
        for a in range(n):
            mine[a].start()
            first.append(copy(a, 0, me, sibling, src=x_refs[a]))
            first += [copy(a, 1 + j, me, (*chip, c), src=x_refs[a]) for j, chip in enumerate(chips)]
        for cp in first:
            cp.start()
        passed = []
        for j, chip in enumerate(chips):
            for a in range(n):
                copy(a, 1 + j, (*chip, c), me).wait_recv()
                passed.append(copy(a, 4 + j, (*chip, c), sibling))
                passed[-1].start()
        for a in range(n):
            copy(a, 0, sibling, me).wait_recv()
            for j, chip in enumerate(chips):
                copy(a, 4 + j, (*chip, 1 - c), me).wait_recv()
        for cp in first + passed:
            cp.wait_send()
        for cp in mine:
            cp.wait()

    return pl.pallas_call(
        body, out_shape=[jax.ShapeDtypeStruct((N_DEV,) + s.shape, s.dtype) for s in shards],
        in_specs=[ANY] * n, out_specs=[ANY] * n,
        scratch_shapes=[pltpu.SemaphoreType.DMA((n, 7)), pltpu.SemaphoreType.DMA((n, 7)), pltpu.SemaphoreType.DMA((n,))],
        name=name)(*shards)


def exchange_partials(parts, name):
    n = len(parts)

    def body(*refs):
        src_refs, dst_refs = refs[:n], refs[n:2 * n]
        send_sems, recv_sems, local_sems = refs[2 * n:]
        x, y, c = _place()
        me = 4 * x + 2 * y + c
        copies = []
        mine = []
        for a in range(n):
            mine.append(pltpu.make_async_copy(src_refs[a].at[me], dst_refs[a].at[me], local_sems.at[a]))
            for k in range(1, N_DEV):
                px = 1 - x if k & 4 else x
                py = 1 - y if k & 2 else y
                pc = 1 - c if k & 1 else c
                copies.append(pltpu.make_async_remote_copy(
                    src_ref=src_refs[a].at[4 * px + 2 * py + pc], dst_ref=dst_refs[a].at[me],
                    send_sem=send_sems.at[a, k - 1], recv_sem=recv_sems.at[a, k - 1],
                    device_id=(px, py, pc), device_id_type=MESH))
        for cp in mine + copies:
            cp.start()
        for cp in copies:
            cp.wait_recv()
        for cp in copies:
            cp.wait_send()
        for cp in mine:
            cp.wait()

    return pl.pallas_call(
        body, out_shape=[jax.ShapeDtypeStruct(p.shape, p.dtype) for p in parts],
        in_specs=[ANY] * n, out_specs=[ANY] * n,
        scratch_shapes=[pltpu.SemaphoreType.DMA((n, 7)), pltpu.SemaphoreType.DMA((n, 7)), pltpu.SemaphoreType.DMA((n,))],
        name=name)(*parts)


def reduce_adamw(parts, w, m, v, name):
    L = len(parts)
    n, Rl, C = parts[0].shape
    R = w.shape[0]
    assert R == L * Rl
    tr = Rl if Rl * C <= 256 * 1024 else _pick(Rl, 256, 16)
    nr = Rl // tr

    def body(*refs):
        p_refs = refs[:L]
        w_ref, m_ref, v_ref, g_ref, d_ref, nm_ref, nv_ref, sum_ref = refs[L:]
        grp = pl.program_id(0)
        for j in range(L):
            @pl.when(grp == j)
            def _(j=j):
                acc = p_refs[j][0].astype(F32)
                for s in range(1, n):
                    acc = acc + p_refs[j][s].astype(F32)
                sum_ref[...] = acc

        g_ = sum_ref[...]
        m_ = ADAM_B1 * m_ref[...] + (1.0 - ADAM_B1) * g_
        v_ = ADAM_B2 * v_ref[...] + (1.0 - ADAM_B2) * (g_ * g_)
        m_hat = m_ / (1.0 - ADAM_B1 ** ADAM_STEP)
        v_hat = v_ / (1.0 - ADAM_B2 ** ADAM_STEP)
        g_ref[...] = g_
        d_ref[...] = -ADAM_LR * (m_hat / (jnp.sqrt(v_hat) + ADAM_EPS) + ADAM_WD * w_ref[...])
        nm_ref[...] = m_
        nv_ref[...] = v_

    blk = pl.BlockSpec((tr, C), lambda l, r: (l * nr + r, 0))
    sds = jax.ShapeDtypeStruct((R, C), F32)
    p_specs = [pl.BlockSpec((n, tr, C), lambda l, r, j=j: (0, jnp.where(l == j, r, 0), 0)) for j in range(L)]
    return pl.pallas_call(
        body, grid=(L, nr), in_specs=p_specs + [blk] * 3,
        out_specs=[blk] * 4, out_shape=[sds] * 4, scratch_shapes=[pltpu.VMEM((tr, C), F32)],
        compiler_params=_cp(("arbitrary", "arbitrary")), name=name)(*parts, w, m, v)


SHARDED = {"ffa_w_in": (2, BF16), "ffa_w_out": (1, BF16), "mix_w_in": (2, BF16), "mla_w_uq": (2, BF16),
           "mla_w_ukv": (2, BF16), "mix_w_o": (1, BF16), "ffb_w_in": (2, BF16), "ffb_w_out": (1, BF16),
           "ple_w_gate": (1, BF16), "ple_w_proj": (2, BF16), "gdn_conv_w": (2, F32), "ln_g": (2, F32), "ln_b": (2, F32)}
FFN_SLOT = ("ffa_w_in", "ffa_w_out", "ffb_w_in", "ffb_w_out")
REPLICATED = ("gdn_a_log", "gdn_dt_bias", "gdn_norm_w", "mla_q_norm_w", "mla_kv_norm_w")
WEIGHTS = ("ffa_w_in", "ffa_w_out", "mix_w_in", "gdn_conv_w", "gdn_a_log", "gdn_dt_bias", "gdn_norm_w", "mla_q_norm_w",
           "mla_kv_norm_w", "mla_w_uq", "mla_w_ukv", "mix_w_o", "ffb_w_in", "ffb_w_out", "ln_g", "ln_b", "ple_w_gate",
           "ple_w_proj")


def _to_slots(full, axis):
    L, a, b = full.shape
    if axis == 2:
        return full.reshape(L, a, N_DEV, b // N_DEV).transpose(2, 0, 1, 3).reshape(N_DEV, L * a, b // N_DEV)
    return full.reshape(L, N_DEV, a // N_DEV, b).transpose(1, 0, 2, 3).reshape(N_DEV, L * a // N_DEV, b)


def _from_slots(slots, shard_shape, axis):
    L, a, b = shard_shape
    t = slots.reshape((N_DEV,) + tuple(shard_shape))
    if axis == 2:
        return t.transpose(1, 2, 0, 3).reshape(L, a, N_DEV * b)
    return t.transpose(1, 0, 2, 3).reshape(L, N_DEV * a, b)


def _view2d(t):
    return t.reshape(-1, t.shape[-1])


def _pad_heads(w, nh):
    K = w.shape[0]
    return jnp.pad(w.reshape(K, nh, GDN_D), ((0, 0), (0, 0), (0, LANES - GDN_D))).reshape(K, nh * LANES)


def _unpad_heads(w, nh):
    K = w.shape[0]
    return w.reshape(K, nh, LANES)[:, :, :GDN_D].reshape(K, nh * GDN_D)


IN_WIDTHS = (512, 512, 512, 512, 8, 8, 256, 256, 256, 256, 160)


def _split_in(w):
    offs = np.cumsum((0,) + IN_WIDTHS)
    return [w[:, int(offs[i]):int(offs[i + 1])] for i in range(len(IN_WIDTHS))]


def _pad_in_proj(w):
    gq, gk, gv, gz, ga, gb, sq, sk, sv, mq, mkv = _split_in(w)
    gab = jnp.pad(jnp.concatenate([ga, gb], axis=1), ((0, 0), (0, LANES - 2 * GDN_HEADS)))
    return jnp.concatenate(
        [gq, gk, gv, gz] + [_pad_heads(t, SB_HEADS) for t in (sq, sk, sv)]
        + [mq, jnp.pad(mkv, ((0, 0), (0, 2 * LANES - mkv.shape[1]))), gab], axis=1)


def _unpad_in_proj(wp):
    c = lambda cb, n: wp[:, cb * LANES:(cb + n) * LANES]
    gab = c(CB_GAB, 1)
    parts = [c(cb, DO_SB) for cb in (CB_GQ, CB_GK, CB_GV, CB_GZ)]
    parts += [gab[:, :GDN_HEADS], gab[:, GDN_HEADS:2 * GDN_HEADS]]
    parts += [_unpad_heads(c(cb, SB_HEADS), SB_HEADS) for cb in (CB_SQ, CB_SK, CB_SV)]
    parts += [c(CB_MQ, 2), c(CB_MKV, 2)[:, :MLA_KV_RANK + MLA_ROPE]]
    return jnp.concatenate(parts, axis=1)


def _pad_lanes(w, width):
    return jnp.pad(w, ((0, 0), (0, width - w.shape[1])))


def _mla_up_pad(w_uq, w_ukv):
    H = MLA_HEADS
    dq = MLA_NOPE + MLA_ROPE
    wuq = jnp.pad(w_uq.reshape(-1, H, dq), ((0, 0), (0, 0), (0, LANES - dq))).reshape(-1, H * LANES)
    kv = w_ukv.reshape(-1, H, MLA_NOPE + MLA_V)
    wuk = jnp.pad(kv[:, :, :MLA_NOPE], ((0, 0), (0, 0), (0, LANES - MLA_NOPE))).reshape(-1, H * LANES)
    wuv = jnp.pad(kv[:, :, MLA_NOPE:], ((0, 0), (0, 0), (0, LANES - MLA_V))).reshape(-1, H * LANES)
    return wuq, wuk, wuv


def _mla_up_unpad(dwuq, dwuk, dwuv):
    H = MLA_HEADS
    dq = MLA_NOPE + MLA_ROPE
    g_uq = dwuq.reshape(-1, H, LANES)[:, :, :dq].reshape(-1, H * dq)
    g_ukv = jnp.concatenate([dwuk.reshape(-1, H, LANES)[:, :, :MLA_NOPE], dwuv.reshape(-1, H, LANES)[:, :, :MLA_V]],
                            axis=2).reshape(-1, H * (MLA_NOPE + MLA_V))
    return g_uq, g_ukv


def _rope_tables(positions):
    inv = 1.0 / (ROPE_BASE ** (jnp.arange(0, MLA_ROPE, 2, dtype=F32) / MLA_ROPE))
    ang = positions.astype(F32)[:, None] * inv
    cos, sin = jnp.cos(ang), jnp.sin(ang)
    T = positions.shape[0]
    one = lambda n: jnp.ones((T, n), F32)
    zero = lambda n: jnp.zeros((T, n), F32)
    tail = LANES - MLA_NOPE - MLA_ROPE
    C = jnp.concatenate([one(MLA_NOPE), cos, cos, one(tail)], axis=1)
    Sn = jnp.concatenate([zero(MLA_NOPE), -sin, zero(HALF_ROPE + tail)], axis=1)
    Sp = jnp.concatenate([zero(MLA_NOPE + HALF_ROPE), sin, zero(tail)], axis=1)
    return C, Sn, Sp


GATHER_FIRST = [("ffa_w_in", 0), ("ffa_w_out", 0)] + [(n, l) for l in range(DEPTH) for n in ("gdn_conv_w", "ln_g", "ln_b")]
GATHER_PLAN = {
    (0, "ffa_fwd"): [("mix_w_in", 0), ("mla_w_uq", 0), ("mla_w_ukv", 0)],
    (0, "gdn_chunk_fwd"): [("mix_w_o", 0), ("ffb_w_in", 0)],
    (0, "sb_fwd"): [("ffb_w_out", 0), ("ple_w_gate", 0), ("ple_w_proj", 0)],
    (0, "mla_fwd"): [("ffa_w_in", 1), ("mix_w_o", 1)],
    (0, "ffb_fwd"): [("ffa_w_out", 1), ("mix_w_in", 1)],
    (1, "ffa_fwd"): [("ffb_w_in", 1)],
    (1, "in_proj"): [("ffb_w_out", 1), ("ple_w_gate", 1), ("ple_w_proj", 1), ("mla_w_uq", 1), ("mla_w_ukv", 1)],
}
SCATTER_PLAN = {
    (1, "gdn_chunk_bwd"): [("ffb_w_in", 1)],
    (1, "gdn_pre_bwd"): [("ffb_w_out", 1), ("ple_w_gate", 1), ("ple_w_proj", 1), ("mix_w_o", 1)],
    (1, "ffa_bwd"): [("mix_w_in", 1), ("mla_w_uq", 1), ("mla_w_ukv", 1), ("gdn_conv_w", 1)],
    (0, "ffb_bwd"): [("ffa_w_in", 1)],
    (0, "gdn_chunk_bwd"): [("ffb_w_in", 0)],
    (0, "gdn_pre_bwd"): [("ffb_w_out", 0), ("ple_w_gate", 0), ("ple_w_proj", 0), ("mix_w_o", 0)],
    (0, "mla_bwd"): [("ffa_w_out", 1), ("ln_g", 1), ("ln_b", 1)],
    (0, "ffa_bwd"): [("mix_w_in", 0), ("mla_w_uq", 0), ("mla_w_ukv", 0), ("gdn_conv_w", 0)],
    (0, "d_ffa_in"): [("ffa_w_out", 0), ("ln_g", 0), ("ln_b", 0)],
}
SCATTER_LAST = [("ffa_w_in", 0)]


class Exchanges:
    def __init__(self, shards):
        self.shards = shards
        self.full = {}
        self.partial = {}
        self.received = {}

    def _block(self, key):
        n, l = key
        return self.shards[n][l].astype(SHARDED[n][1])

    def _absorb_gather(self, keys, results):
        for (n, l), g in zip(keys, results):
            blk = self.shards[n][l]
            self.full[(n, l)] = g if n in FFN_SLOT else _from_slots(g, (1,) + blk.shape, SHARDED[n][0])[0]

    def gather_now(self, keys, name):
        self._absorb_gather(keys, all_gather([self._block(k) for k in keys], name))

    def gather_with(self, layer, tag):
        keys = GATHER_PLAN.get((layer, tag))
        return None if keys is None else (keys, Hosted("gather", [self._block(k) for k in keys]))

    def scatter_with(self, layer, tag):
        keys = SCATTER_PLAN.get((layer, tag))
        return None if keys is None else (keys, Hosted("scatter", [self.partial[k] for k in keys]))

    def done(self, carried):
        if carried is not None:
            keys, hosted = carried
            if hosted.kind == "gather":
                self._absorb_gather(keys, hosted.results)
            else:
                self.received.update(zip(keys, hosted.results))

    def add_grad(self, key, g):
        n, l = key
        self.partial[key] = g if n in FFN_SLOT else _to_slots(g[None], SHARDED[n][0]).astype(SHARDED[n][1])


def _carried(c):
    return None if c is None else c[1]


def _layer_fwd(h0, p_i, rope, i, ex, rep):
    L = "L%d_" % i
    S = {"h0": h0, "p": p_i}
    W = ex.full
    ln_g = [W[("ln_g", i)][j][None, :] for j in range(3)]
    ln_b = [W[("ln_b", i)][j][None, :] for j in range(3)]
    S["ln_g"] = ln_g
    c = ex.gather_with(i, "ffa_fwd")
    S["h1"], S["xh1"], S["rs1"], S["Ga"], S["Ua"], S["h1b"] = ffn_fwd(
        h0, W[("ffa_w_in", i)], W[("ffa_w_out", i)], ln_g[0], ln_b[0], L + "ffa_fwd", hosted=_carried(c))
    ex.done(c)
    S["win"] = _pad_in_proj(W[("mix_w_in", i)])
    c = ex.gather_with(i, "in_proj")
    S["proj"] = mm_nn(S["h1b"], S["win"], L + "in_proj", hosted=_carried(c))
    ex.done(c)
    S["conv"] = W[("gdn_conv_w", i)]
    S["a_row"] = _pad_lanes(rep["gdn_a_log"][i][None, :], LANES)
    S["dt_row"] = _pad_lanes(rep["gdn_dt_bias"][i][None, :], LANES)
    S["nw"] = rep["gdn_norm_w"][i][None, :]
    S["wq"] = rep["mla_q_norm_w"][i][None, :]
    S["wkv"] = rep["mla_kv_norm_w"][i][None, :]
    S["qkv"] = gdn_pre_fwd(S["proj"], S["conv"], L + "gdn_pre_fwd")
    c = ex.gather_with(i, "gdn_chunk_fwd")
    S["o_gdn"], S["opre"], S["Tm"], S["Sin"] = gdn_chunk_fwd(S["qkv"], S["proj"], S["a_row"], S["dt_row"], S["nw"],
                                                            L + "gdn_chunk_fwd", hosted=_carried(c))
    ex.done(c)
    c = ex.gather_with(i, "sb_fwd")
    S["o_sb"], S["tot"] = sb_fwd(S["proj"], L + "sb_fwd", hosted=_carried(c))
    ex.done(c)
    S["wuq"], S["wuk"], S["wuv"] = _mla_up_pad(W[("mla_w_uq", i)], W[("mla_w_ukv", i)])
    S["Q"], S["K"], S["V"] = mla_pre_fwd(S["proj"], S["wq"], S["wkv"], S["wuq"], S["wuk"], S["wuv"], *rope, L + "mla_pre_fwd")
    c = ex.gather_with(i, "mla_fwd")
    S["o_mla"], S["lse"] = mla_fwd(S["Q"], S["K"], S["V"], L + "mla_fwd", hosted=_carried(c))
    ex.done(c)
    wo = W[("mix_w_o", i)]
    wo_att = wo[GDN_W:].reshape(-1, GDN_D, wo.shape[1])
    S["wo"] = jnp.concatenate(
        [wo[:GDN_W], jnp.pad(wo_att, ((0, 0), (0, LANES - GDN_D), (0, 0))).reshape(-1, wo.shape[1])], axis=0)
    S["h2"], S["xh2"], S["rs2"], S["h2b"] = proj_res_ln([S["o_gdn"], S["o_sb"], S["o_mla"]], S["wo"], S["h1"],
                                                        ln_g[1], ln_b[1], L + "out_proj")
    c = ex.gather_with(i, "ffb_fwd")
    S["h3"], S["xh3"], S["rs3"], S["Gb"], S["Ub"], _ = ffn_fwd(
        S["h2"], W[("ffb_w_in", i)], W[("ffb_w_out", i)], ln_g[2], ln_b[2], L + "ffb_fwd", hosted=_carried(c))
    ex.done(c)
    h4, S["a"], S["e"] = ple_fwd(S["h3"], p_i, W[("ple_w_gate", i)], W[("ple_w_proj", i)], L + "ple_fwd")
    return h4, S


def _layer_bwd(dh4, S, rope, i, ex):
    L = "L%d_" % i
    W = ex.full
    Grep = {}
    dh3, da, de = ple_bwd(dh4, S["a"], S["e"], W[("ple_w_gate", i)], L + "ple_bwd")
    ex.add_grad(("ple_w_gate", i), mm_tn(S["h3"], da, L + "d_ple_gate"))
    ex.add_grad(("ple_w_proj", i), mm_tn(S["p"], de, L + "d_ple_proj"))
    dz3, dg2, db2 = ln_bwd(dh3, S["xh3"], S["rs3"], S["ln_g"][2], L + "ln3_bwd")
    c = ex.scatter_with(i, "ffb_bwd")
    dh2, dGb, dUb, actb = ffn_bwd(dz3, S["Gb"], S["Ub"], W[("ffb_w_in", i)], W[("ffb_w_out", i)], L + "ffb_bwd",
                                  hosted=_carried(c))
    ex.done(c)
    ex.add_grad(("ffb_w_in", i), ffn_dw_in(S["h2b"].T, dGb, dUb, L + "d_ffb_in"))
    ex.add_grad(("ffb_w_out", i), ffn_dw_out(actb, dz3, L + "d_ffb_out"))
    dz2, dg1, db1 = ln_bwd(dh2, S["xh2"], S["rs2"], S["ln_g"][1], L + "ln2_bwd")
    docat = mm_nn(dz2, S["wo"], L + "d_ocat", b_transposed=True)
    dwo_att = jnp.concatenate([mm_tn(S["o_sb"], dz2, L + "d_wo_sb"), mm_tn(S["o_mla"], dz2, L + "d_wo_mla")], axis=0)
    dwo_att = dwo_att.reshape(-1, LANES, dwo_att.shape[1])[:, :GDN_D, :].reshape(-1, dwo_att.shape[1])
    ex.add_grad(("mix_w_o", i), jnp.concatenate([mm_tn(S["o_gdn"], dz2, L + "d_wo_gdn"), dwo_att], axis=0))
    c = ex.scatter_with(i, "gdn_chunk_bwd")
    dqkv, dgz, dgab, d_alog, d_dt, d_nw = gdn_chunk_bwd(S["qkv"], S["proj"], S["a_row"], S["dt_row"], S["nw"],
                                                        S["opre"], S["Tm"], S["Sin"], docat, L + "gdn_chunk_bwd",
                                                        hosted=_carried(c))
    ex.done(c)
    c = ex.scatter_with(i, "gdn_pre_bwd")
    dc, dconv = gdn_pre_bwd_pointwise(S["proj"], S["conv"], dqkv, L + "gdn_pre_bwd", hosted=_carried(c))
    ex.done(c)
    dxqkv = gdn_pre_bwd_conv(dc, S["conv"], L + "gdn_conv_bwd")
    ex.add_grad(("gdn_conv_w", i), dconv)
    Grep["gdn_a_log"], Grep["gdn_dt_bias"], Grep["gdn_norm_w"] = d_alog[0, :GDN_HEADS], d_dt[0, :GDN_HEADS], d_nw[0]
    dsq, dsk, dsv = sb_bwd(S["proj"], S["tot"], docat, DO_SB, L + "sb_bwd")
    c = ex.scatter_with(i, "mla_bwd")
    dQ, dK, dV = mla_bwd(S["Q"], S["K"], S["V"], S["o_mla"], S["lse"], docat, DO_MLA, L + "mla_bwd",
                         hosted=_carried(c))
    ex.done(c)
    dmq, dmkv, dwuq, dwuk, dwuv, dwq, dwkv = mla_pre_bwd(
        S["proj"], S["wq"], S["wkv"], S["wuq"], S["wuk"], S["wuv"], *rope, dQ, dK, dV, L + "mla_pre_bwd")
    g_uq, g_ukv = _mla_up_unpad(dwuq, dwuk, dwuv)
    ex.add_grad(("mla_w_uq", i), g_uq)
    ex.add_grad(("mla_w_ukv", i), g_ukv)
    Grep["mla_q_norm_w"], Grep["mla_kv_norm_w"] = dwq[0], dwkv[0]
    dproj = jnp.concatenate([dxqkv, dgz, dsq, dsk, dsv, dmq, dmkv, dgab], axis=1).astype(BF16)
    ex.add_grad(("mix_w_in", i),
                _unpad_in_proj(mm_tn(S["h1b"].T, dproj, L + "d_in_proj", a_transposed=True)))
    dh1 = mm_nn(dproj, S["win"], L + "d_h1", res=dz2, res_scale=ALPHA, b_transposed=True)
    dz1, dg0, db0 = ln_bwd(dh1, S["xh1"], S["rs1"], S["ln_g"][0], L + "ln1_bwd")
    c = ex.scatter_with(i, "ffa_bwd")
    dh0, dGa, dUa, acta = ffn_bwd(dz1, S["Ga"], S["Ua"], W[("ffa_w_in", i)], W[("ffa_w_out", i)], L + "ffa_bwd",
                                  hosted=_carried(c))
    ex.done(c)
    ex.add_grad(("ffa_w_out", i), ffn_dw_out(acta, dz1, L + "d_ffa_out"))
    ex.add_grad(("ln_g", i), jnp.concatenate([dg0, dg1, dg2], axis=0))
    ex.add_grad(("ln_b", i), jnp.concatenate([db0, db1, db2], axis=0))
    c = ex.scatter_with(i, "d_ffa_in")
    ex.add_grad(("ffa_w_in", i), ffn_dw_in(S["h0"].T.astype(BF16), dGa, dUa, L + "d_ffa_in", hosted=_carried(c)))
    ex.done(c)
    return dh0, Grep


def _local_step(x, p, positions, target, ex, rep):
    assert DEPTH == 2
    rope = _rope_tables(positions)
    h, saved = x, []
    for i in range(DEPTH):
        h, S = _layer_fwd(h, p[i], rope, i, ex, rep)
        saved.append(S)
    loss, dh = loss_head(h, target, "loss_head")
    grads = [None] * DEPTH
    for i in reversed(range(DEPTH)):
        dh, grads[i] = _layer_bwd(dh, saved[i], rope, i, ex)
    return loss, dh, {n: jnp.stack([grads[i][n] for i in range(DEPTH)]) for n in REPLICATED}


def kernel(x, p, positions, ffa_w_in, ffa_w_out, mix_w_in, gdn_conv_w, gdn_a_log, gdn_dt_bias, gdn_norm_w, mla_q_norm_w, mla_kv_norm_w, mla_w_uq, mla_w_ukv, mix_w_o, ffb_w_in, ffb_w_out, ln_g, ln_b, ple_w_gate, ple_w_proj, loss_target, m_ffa_w_in, m_ffa_w_out, m_mix_w_in, m_gdn_conv_w, m_gdn_a_log, m_gdn_dt_bias, m_gdn_norm_w, m_mla_q_norm_w, m_mla_kv_norm_w, m_mla_w_uq, m_mla_w_ukv, m_mix_w_o, m_ffb_w_in, m_ffb_w_out, m_ln_g, m_ln_b, m_ple_w_gate, m_ple_w_proj, v_ffa_w_in, v_ffa_w_out, v_mix_w_in, v_gdn_conv_w, v_gdn_a_log, v_gdn_dt_bias, v_gdn_norm_w, v_mla_q_norm_w, v_mla_kv_norm_w, v_mla_w_uq, v_mla_w_ukv, v_mix_w_o, v_ffb_w_in, v_ffb_w_out, v_ln_g, v_ln_b, v_ple_w_gate, v_ple_w_proj):
    given = dict(locals())
    shards = {n: given[n] for n in WEIGHTS}
    ex = Exchanges({n: shards[n] for n in SHARDED})
    ex.gather_now(GATHER_FIRST, "gather_first")
    loss, grad_x, Grep = _local_step(x[0], p[:, 0], positions[0], loss_target[0], ex, {n: shards[n] for n in REPLICATED})
    loss = lax.psum(loss[0, 0], ("x", "y", "c"))
    ex.received.update(zip(SCATTER_LAST, exchange_partials([ex.partial[k] for k in SCATTER_LAST], "scatter_last")))
    rep_received = dict(zip(REPLICATED, all_gather([Grep[n] for n in REPLICATED], "gather_replicated_grads")))
    grad, delta, new_m, new_v = {}, {}, {}, {}
    for n in WEIGHTS:
        shape = shards[n].shape
        parts = [rep_received[n]] if n in REPLICATED else [ex.received[(n, l)] for l in range(DEPTH)]
        if parts[0].shape[1] % 8:
            parts = [jnp.concatenate(parts, axis=1)]
        outs = reduce_adamw(parts, _view2d(shards[n]), _view2d(given["m_" + n]), _view2d(given["v_" + n]),
                            "adamw_" + n)
        grad[n], delta[n], new_m[n], new_v[n] = (t.reshape(shape) for t in outs)
    return (loss, grad_x[None], *[grad[n] for n in WEIGHTS], *[delta[n] for n in WEIGHTS],
            *[new_m[n] for n in WEIGHTS], *[new_v[n] for n in WEIGHTS])
```

```python
import functools
import numpy as np
import jax
import jax.numpy as jnp
from jax import lax
from jax.experimental import pallas as pl
from jax.experimental.pallas import tpu as pltpu

F32 = jnp.float32
BF16 = jnp.bfloat16

DEPTH = 2
LN_EPS = 1e-5
RMS_EPS = 1e-6
ALPHA = (2 * DEPTH) ** 0.25
GDN_HEADS, GDN_D, GDN_CONV, GDN_CHUNK = 8, 64, 4, 64
SB_HEADS, SB_DIM = 4, 64
MLA_HEADS, MLA_NOPE, MLA_ROPE, MLA_V, MLA_Q_RANK, MLA_KV_RANK = 4, 64, 32, 64, 256, 128
ROPE_BASE = 10000.0
HALF_ROPE = MLA_ROPE // 2
LANES = 128
N_DEV = 8
ADAM_LR, ADAM_B1, ADAM_B2, ADAM_EPS, ADAM_WD, ADAM_STEP = 0.001, 0.9, 0.999, 1e-08, 0.01, 10

CB_GQ, CB_GK, CB_GV, CB_GZ = 0, 4, 8, 12
CB_SQ, CB_SK, CB_SV = 16, 20, 24
CB_MQ, CB_MKV, CB_GAB = 28, 30, 32
PROJ_W = 33 * LANES
GDN_W = GDN_HEADS * GDN_D
DO_SB = GDN_W // LANES
DO_MLA = DO_SB + SB_HEADS
VMEM_LIMIT = 56 * 1024 * 1024
MM_TILE = 1536

NT_DIMS = (((1,), (1,)), ((), ()))
TN_DIMS = (((0,), (0,)), ((), ()))


def _cp(sem):
    return pltpu.CompilerParams(dimension_semantics=sem, vmem_limit_bytes=VMEM_LIMIT)


def _bdot(a, b):
    return jnp.dot(a.astype(BF16), b.astype(BF16), preferred_element_type=F32)


def _bdot_nt(a, b):
    return lax.dot_general(a.astype(BF16), b.astype(BF16), NT_DIMS, preferred_element_type=F32)


def _bdot_tn(a, b):
    return lax.dot_general(a.astype(BF16), b.astype(BF16), TN_DIMS, preferred_element_type=F32)


def _split2(a):
    hi = a.astype(BF16)
    lo = (a - hi.astype(F32)).astype(BF16)
    return hi, lo


def _ones_dot_left(ones_bf16, x):
    hi = x.astype(BF16)
    r1 = x - hi.astype(F32)
    mid = r1.astype(BF16)
    lo = (r1 - mid.astype(F32)).astype(BF16)
    d = functools.partial(jnp.dot, preferred_element_type=F32)
    return d(ones_bf16, hi) + d(ones_bf16, mid) + d(ones_bf16, lo)


def _iota2(shape, dim):
    return lax.broadcasted_iota(jnp.int32, shape, dim)


def _sigmoid(x):
    return 0.5 * jnp.tanh(0.5 * x) + 0.5


def _softplus(x):
    return jnp.maximum(x, 0.0) + jnp.log(1.0 + jnp.exp(-jnp.abs(x)))


def _pick(n, limit, mult):
    if n <= limit:
        return n
    best = None
    for t in range(mult, limit + 1, mult):
        if n % t == 0:
            best = t
    assert best is not None, (n, limit, mult)
    return best


MESH = pl.DeviceIdType.MESH
ANY = pl.BlockSpec(memory_space=pl.ANY)


def _place():
    return lax.axis_index("x"), lax.axis_index("y"), lax.axis_index("c")


def _peer(k):
    x, y, c = _place()
    return (1 - x if k & 4 else x, 1 - y if k & 2 else y, 1 - c if k & 1 else c)


class Hosted:
    def __init__(self, kind, arrays):
        self.kind, self.arrays, self.n, self.results = kind, list(arrays), len(arrays), None

    def out_shapes(self):
        if self.kind == "gather":
            return [jax.ShapeDtypeStruct((N_DEV,) + a.shape, a.dtype) for a in self.arrays]
        return [jax.ShapeDtypeStruct(a.shape, a.dtype) for a in self.arrays]

    def sems(self):
        return [pltpu.SemaphoreType.DMA((self.n, N_DEV - 1)), pltpu.SemaphoreType.DMA((self.n, N_DEV - 1)),
                pltpu.SemaphoreType.DMA((self.n,))]

    def _copies(self, src_refs, dst_refs, send_sems, recv_sems, local_sems):
        x, y, c = _place()
        me = 4 * x + 2 * y + c
        local, remote = [], []
        for a in range(self.n):
            gather = self.kind == "gather"
            local.append(pltpu.make_async_copy(src_refs[a] if gather else src_refs[a].at[me], dst_refs[a].at[me],
                                               local_sems.at[a]))
            for k in range(1, N_DEV):
                px, py, pc = _peer(k)
                remote.append(pltpu.make_async_remote_copy(
                    src_ref=src_refs[a] if gather else src_refs[a].at[4 * px + 2 * py + pc], dst_ref=dst_refs[a].at[me],
                    send_sem=send_sems.at[a, k - 1], recv_sem=recv_sems.at[a, k - 1],
                    device_id=(px, py, pc), device_id_type=MESH))
        return local, remote

    def start(self, *refs):
        local, remote = self._copies(*refs)
        for cp in local + remote:
            cp.start()

    def wait(self, *refs):
        local, remote = self._copies(*refs)
        for cp in remote:
            cp.wait_recv()
        for cp in remote:
            cp.wait_send()
        for cp in local:
            cp.wait()


def _hosted_call(hosted, body, *, grid, in_specs, out_specs, out_shape, scratch_shapes=(), compiler_params, name):
    if hosted is None:
        return pl.pallas_call(body, grid=grid, in_specs=in_specs, out_specs=out_specs, out_shape=out_shape,
                              scratch_shapes=scratch_shapes, compiler_params=compiler_params, name=name)
    single = not isinstance(out_shape, (list, tuple))
    o_specs = [out_specs] if single else list(out_specs)
    o_shape = [out_shape] if single else list(out_shape)
    n_in, n_out, n_scr, n = len(in_specs), len(o_specs), len(scratch_shapes), hosted.n

    def wrapped(*refs):
        ins, c_in = refs[:n_in], refs[n_in:n_in + n]
        outs, c_out = refs[n_in + n:n_in + n + n_out], refs[n_in + n + n_out:n_in + 2 * n + n_out]
        rest = refs[n_in + 2 * n + n_out:]
        scr, sems = rest[:n_scr], rest[n_scr:]
        ids = [pl.program_id(ax) for ax in range(len(grid))]
        first = functools.reduce(jnp.logical_and, [i == 0 for i in ids])
        last = functools.reduce(jnp.logical_and, [i == g - 1 for i, g in zip(ids, grid)])

        @pl.when(first)
        def _():
            hosted.start(c_in, c_out, *sems)

        body(*ins, *outs, *scr)

        @pl.when(last)
        def _():
            hosted.wait(c_in, c_out, *sems)

    call = pl.pallas_call(
        wrapped, grid=grid, in_specs=list(in_specs) + [ANY] * n, out_specs=o_specs + [ANY] * n,
        out_shape=o_shape + hosted.out_shapes(), scratch_shapes=list(scratch_shapes) + hosted.sems(),
        compiler_params=_cp(("arbitrary",) * len(grid)), name=name)

    def run(*args):
        outs = call(*args, *hosted.arrays)
        hosted.results = list(outs[n_out:])
        return outs[0] if single else list(outs[:n_out])

    return run


def mm_nn(a, b, name, out_dtype=F32, res=None, res_scale=1.0, b_transposed=False, hosted=None):
    M, K = a.shape
    N = b.shape[0] if b_transposed else b.shape[1]
    tm, tn, tk = _pick(M, 512, 16), _pick(N, MM_TILE, LANES), _pick(K, MM_TILE, LANES)
    nk = K // tk
    has_res = res is not None
    dot = _bdot_nt if b_transposed else _bdot

    def body(*refs):
        if has_res:
            a_ref, b_ref, r_ref, o_ref, acc_ref = refs
        else:
            a_ref, b_ref, o_ref, acc_ref = refs
        k = pl.program_id(2)

        @pl.when(k == 0)
        def _():
            acc_ref[...] = jnp.zeros_like(acc_ref)

        acc_ref[...] += dot(a_ref[...], b_ref[...])

        @pl.when(k == nk - 1)
        def _():
            out = acc_ref[...]
            if has_res:
                out = out + res_scale * r_ref[...]
            o_ref[...] = out.astype(o_ref.dtype)

    b_spec = pl.BlockSpec((tn, tk), lambda i, j, k: (j, k)) if b_transposed else pl.BlockSpec((tk, tn), lambda i, j, k: (k, j))
    in_specs = [pl.BlockSpec((tm, tk), lambda i, j, k: (i, k)), b_spec]
    args = [a, b]
    if has_res:
        in_specs.append(pl.BlockSpec((tm, tn), lambda i, j, k: (i, j)))
        args.append(res)
    return _hosted_call(
        hosted, body, grid=(M // tm, N // tn, nk), in_specs=in_specs,
        out_specs=pl.BlockSpec((tm, tn), lambda i, j, k: (i, j)),
        out_shape=jax.ShapeDtypeStruct((M, N), out_dtype),
        scratch_shapes=[pltpu.VMEM((tm, tn), F32)],
        compiler_params=_cp(("parallel", "parallel", "arbitrary")), name=name)(*args)


def mm_tn(a, b, name, out_dtype=F32, a_transposed=False):
    K, T = a.shape if a_transposed else a.shape[::-1]
    _, N = b.shape
    tk = K if a_transposed else _pick(K, 512, LANES)
    tn, tt = _pick(N, MM_TILE, LANES), _pick(T, 512, LANES)
    nt = T // tt

    def body(a_ref, b_ref, o_ref, acc_ref):
        t = pl.program_id(2)

        @pl.when(t == 0)
        def _():
            acc_ref[...] = jnp.zeros_like(acc_ref)

        if a_transposed:
            acc_ref[...] += _bdot(a_ref[:, pl.ds(pl.multiple_of(t * tt, tt), tt)], b_ref[...])
        else:
            acc_ref[...] += _bdot_tn(a_ref[...], b_ref[...])

        @pl.when(t == nt - 1)
        def _():
            o_ref[...] = acc_ref[...].astype(o_ref.dtype)

    a_spec = pl.BlockSpec((K, T), lambda i, j, t: (0, 0)) if a_transposed else pl.BlockSpec((tt, tk), lambda i, j, t: (t, i))
    return pl.pallas_call(
        body, grid=(K // tk, N // tn, nt),
        in_specs=[a_spec, pl.BlockSpec((tt, tn), lambda i, j, t: (t, j))],
        out_specs=pl.BlockSpec((tk, tn), lambda i, j, t: (i, j)),
        out_shape=jax.ShapeDtypeStruct((K, N), out_dtype),
        scratch_shapes=[pltpu.VMEM((tk, tn), F32)],
        compiler_params=_cp(("parallel", "parallel", "arbitrary")), name=name)(a, b)


def _ln_apply(z, g, b):
    mu = jnp.mean(z, axis=-1, keepdims=True)
    zc = z - mu
    var = jnp.mean(zc * zc, axis=-1, keepdims=True)
    rstd = lax.rsqrt(var + LN_EPS)
    xhat = zc * rstd
    return xhat * g + b, xhat, rstd


def ln_bwd(dout, xhat, rstd, g, name):
    T, D = dout.shape
    tm = _pick(T, 512, 8)

    def body(do_ref, xh_ref, rs_ref, g_ref, dz_ref, dg_ref, db_ref):
        i = pl.program_id(0)

        @pl.when(i == 0)
        def _():
            dg_ref[...] = jnp.zeros_like(dg_ref)
            db_ref[...] = jnp.zeros_like(db_ref)

        do = do_ref[...]
        xh = xh_ref[...]
        dxh = do * g_ref[...]
        m1 = jnp.mean(dxh, axis=-1, keepdims=True)
        m2 = jnp.mean(dxh * xh, axis=-1, keepdims=True)
        dz_ref[...] = rs_ref[...] * (dxh - m1 - xh * m2)
        dg_ref[...] += jnp.sum(do * xh, axis=0, keepdims=True)
        db_ref[...] += jnp.sum(do, axis=0, keepdims=True)

    row = pl.BlockSpec((tm, D), lambda i: (i, 0))
    vec = pl.BlockSpec((1, D), lambda i: (0, 0))
    return pl.pallas_call(
        body, grid=(T // tm,),
        in_specs=[row, row, pl.BlockSpec((tm, 1), lambda i: (i, 0)), vec],
        out_specs=[row, vec, vec],
        out_shape=[jax.ShapeDtypeStruct((T, D), F32), jax.ShapeDtypeStruct((1, D), F32), jax.ShapeDtypeStruct((1, D), F32)],
        compiler_params=_cp(("arbitrary",)), name=name)(dout, xhat, rstd, g)


FFN_CHUNKS = N_DEV // 2


def ffn_fwd(h, w_in, w_out, g, b, name, hosted=None):
    T, D = h.shape
    fc = w_in.shape[2]
    half = w_out.shape[1]
    tm = _pick(T, 512, 8)
    nc = FFN_CHUNKS

    def body(h_ref, wg_ref, wu_ref, wo_ref, g_ref, b_ref, out_ref, xh_ref, rs_ref, G_ref, U_ref, ob_ref, acc_ref):
        c = pl.program_id(1)

        @pl.when(c == 0)
        def _():
            acc_ref[...] = jnp.zeros_like(acc_ref)

        hb = h_ref[...].astype(BF16)
        G = jnp.dot(hb, wg_ref[0], preferred_element_type=F32)
        U = jnp.dot(hb, wu_ref[0], preferred_element_type=F32)
        G_ref[0] = G
        U_ref[0] = U
        act = G * _sigmoid(G) * U
        acc_ref[...] += _bdot(act, wo_ref[...].reshape(2 * half, D))

        @pl.when(c == nc - 1)
        def _():
            z = ALPHA * h_ref[...] + 0.5 * acc_ref[...]
            out, xh, rs = _ln_apply(z, g_ref[...], b_ref[...])
            out_ref[...] = out
            ob_ref[...] = out.astype(BF16)
            xh_ref[...] = xh
            rs_ref[...] = rs

    row = pl.BlockSpec((tm, D), lambda i, c: (i, 0))
    vec = pl.BlockSpec((1, D), lambda i, c: (0, 0))
    cblk = pl.BlockSpec((1, tm, fc), lambda i, c: (c, i, 0))
    csds = jax.ShapeDtypeStruct((nc, T, fc), F32)
    return _hosted_call(
        hosted, body, grid=(T // tm, nc),
        in_specs=[row, pl.BlockSpec((1, D, fc), lambda i, c: (c, 0, 0)),
                  pl.BlockSpec((1, D, fc), lambda i, c: (c + nc, 0, 0)),
                  pl.BlockSpec((2, half, D), lambda i, c: (c, 0, 0)), vec, vec],
        out_specs=[row, row, pl.BlockSpec((tm, 1), lambda i, c: (i, 0)), cblk, cblk, row],
        out_shape=[jax.ShapeDtypeStruct((T, D), F32), jax.ShapeDtypeStruct((T, D), F32), jax.ShapeDtypeStruct((T, 1), F32),
                   csds, csds, jax.ShapeDtypeStruct((T, D), BF16)],
        scratch_shapes=[pltpu.VMEM((tm, D), F32)],
        compiler_params=_cp(("parallel", "arbitrary")), name=name)(h, w_in, w_in, w_out, g, b)


def ffn_bwd(dz, G, U, w_in, w_out, name, hosted=None):
    T, D = dz.shape
    nc, _, fc = G.shape
    half = w_out.shape[1]
    tm = _pick(T, 512, 16)

    def body(dz_ref, G_ref, U_ref, wg_ref, wu_ref, wo_ref, dh_ref, dG_ref, dU_ref, act_ref, acc_ref):
        c = pl.program_id(1)

        @pl.when(c == 0)
        def _():
            acc_ref[...] = jnp.zeros_like(acc_ref)

        dy = (0.5 * dz_ref[...]).astype(BF16)
        dact = _bdot_nt(dy, wo_ref[...].reshape(2 * half, D))
        G = G_ref[0]
        U = U_ref[0]
        s = _sigmoid(G)
        silu = G * s
        dG = (dact * U * (s * (1.0 + G * (1.0 - s)))).astype(BF16)
        dU = (dact * silu).astype(BF16)
        dG_ref[0] = dG
        dU_ref[0] = dU
        act_ref[0] = (silu * U).astype(BF16)
        acc_ref[...] += _bdot_nt(dG, wg_ref[0]) + _bdot_nt(dU, wu_ref[0])

        @pl.when(c == nc - 1)
        def _():
            dh_ref[...] = ALPHA * dz_ref[...] + acc_ref[...]

    row = pl.BlockSpec((tm, D), lambda i, c: (i, 0))
    cblk = pl.BlockSpec((1, tm, fc), lambda i, c: (c, i, 0))
    csds = jax.ShapeDtypeStruct((nc, T, fc), BF16)
    return _hosted_call(
        hosted, body, grid=(T // tm, nc),
        in_specs=[row, cblk, cblk, pl.BlockSpec((1, D, fc), lambda i, c: (c, 0, 0)),
                  pl.BlockSpec((1, D, fc), lambda i, c: (c + nc, 0, 0)),
                  pl.BlockSpec((2, half, D), lambda i, c: (c, 0, 0))],
        out_specs=[row, cblk, cblk, cblk],
        out_shape=[jax.ShapeDtypeStruct((T, D), F32), csds, csds, csds],
        scratch_shapes=[pltpu.VMEM((tm, D), F32)],
        compiler_params=_cp(("parallel", "arbitrary")), name=name)(dz, G, U, w_in, w_in, w_out)


def ffn_dw_in(h_t, dG, dU, name, hosted=None):
    D, T = h_t.shape
    nc, _, fc = dG.shape
    tt = _pick(T, 512, LANES)
    nt = T // tt

    def body(h_ref, dG_ref, dU_ref, o_ref, acc_ref):
        s = pl.program_id(0)
        t = pl.program_id(1)

        @pl.when(t == 0)
        def _():
            acc_ref[...] = jnp.zeros_like(acc_ref)

        hb = h_ref[:, pl.ds(pl.multiple_of(t * tt, tt), tt)]

        @pl.when(s < nc)
        def _():
            acc_ref[...] += jnp.dot(hb, dG_ref[0], preferred_element_type=F32)

        @pl.when(s >= nc)
        def _():
            acc_ref[...] += jnp.dot(hb, dU_ref[0], preferred_element_type=F32)

        @pl.when(t == nt - 1)
        def _():
            o_ref[0] = acc_ref[...].astype(o_ref.dtype)

    return _hosted_call(
        hosted, body, grid=(2 * nc, nt),
        in_specs=[pl.BlockSpec((D, T), lambda s, t: (0, 0)),
                  pl.BlockSpec((1, tt, fc), lambda s, t: (jnp.minimum(s, nc - 1), jnp.where(s < nc, t, nt - 1), 0)),
                  pl.BlockSpec((1, tt, fc), lambda s, t: (jnp.maximum(s - nc, 0), jnp.where(s >= nc, t, 0), 0))],
        out_specs=pl.BlockSpec((1, D, fc), lambda s, t: (s, 0, 0)),
        out_shape=jax.ShapeDtypeStruct((2 * nc, D, fc), BF16),
        scratch_shapes=[pltpu.VMEM((D, fc), F32)],
        compiler_params=_cp(("parallel", "arbitrary")), name=name)(h_t, dG, dU)


def ffn_dw_out(act, dz, name, hosted=None):
    nc, T, fc = act.shape
    D = dz.shape[1]
    half = fc // 2
    tt = _pick(T, 512, 16)
    nt = T // tt

    def body(a_ref, dz_ref, o_ref, acc_ref):
        t = pl.program_id(1)

        @pl.when(t == 0)
        def _():
            acc_ref[...] = jnp.zeros_like(acc_ref)

        acc_ref[...] += _bdot_tn(a_ref[0], dz_ref[...])

        @pl.when(t == nt - 1)
        def _():
            o_ref[...] = (0.5 * acc_ref[...]).reshape(2, half, D).astype(o_ref.dtype)

    return _hosted_call(
        hosted, body, grid=(nc, nt),
        in_specs=[pl.BlockSpec((1, tt, fc), lambda c, t: (c, t, 0)), pl.BlockSpec((tt, D), lambda c, t: (t, 0))],
        out_specs=pl.BlockSpec((2, half, D), lambda c, t: (c, 0, 0)),
        out_shape=jax.ShapeDtypeStruct((2 * nc, half, D), BF16),
        scratch_shapes=[pltpu.VMEM((fc, D), F32)],
        compiler_params=_cp(("parallel", "arbitrary")), name=name)(act, dz)


def proj_res_ln(parts, w, res, g, b, name):
    T, D = res.shape
    tm = _pick(T, 512, 8)
    widths = [p.shape[1] for p in parts]
    offs = [int(sum(widths[:i])) for i in range(len(parts))]
    n = len(parts)

    def body(*refs):
        p_refs = refs[:n]
        w_ref, r_ref, g_ref, b_ref, out_ref, xh_ref, rs_ref, ob_ref = refs[n:]
        acc = ALPHA * r_ref[...]
        for p_ref, o, wd in zip(p_refs, offs, widths):
            acc = acc + _bdot(p_ref[...], w_ref[o:o + wd, :])
        out, xh, rs = _ln_apply(acc, g_ref[...], b_ref[...])
        out_ref[...] = out
        ob_ref[...] = out.astype(BF16)
        xh_ref[...] = xh
        rs_ref[...] = rs

    row = pl.BlockSpec((tm, D), lambda i: (i, 0))
    vec = pl.BlockSpec((1, D), lambda i: (0, 0))
    return pl.pallas_call(
        body, grid=(T // tm,),
        in_specs=[pl.BlockSpec((tm, wd), lambda i: (i, 0)) for wd in widths]
        + [pl.BlockSpec(w.shape, lambda i: (0, 0)), row, vec, vec],
        out_specs=[row, row, pl.BlockSpec((tm, 1), lambda i: (i, 0)), row],
        out_shape=[jax.ShapeDtypeStruct((T, D), F32), jax.ShapeDtypeStruct((T, D), F32), jax.ShapeDtypeStruct((T, 1), F32),
                   jax.ShapeDtypeStruct((T, D), BF16)],
        compiler_params=_cp(("parallel",)), name=name)(*parts, w, res, g, b)


def ple_fwd(h, p, wg, wp, name):
    T, D = h.shape
    P = p.shape[1]
    tm, tn = _pick(T, 512, 8), _pick(D, 512, LANES)

    def body(h_ref, hn_ref, p_ref, wg_ref, wp_ref, out_ref, a_ref, e_ref):
        a = _bdot(h_ref[...], wg_ref[...])
        e = _bdot(p_ref[...], wp_ref[...])
        a_ref[...] = a
        e_ref[...] = e
        out_ref[...] = hn_ref[...] + _sigmoid(a) * e

    blk = pl.BlockSpec((tm, tn), lambda i, j: (i, j))
    sds = jax.ShapeDtypeStruct((T, D), F32)
    return pl.pallas_call(
        body, grid=(T // tm, D // tn),
        in_specs=[pl.BlockSpec((tm, D), lambda i, j: (i, 0)), blk, pl.BlockSpec((tm, P), lambda i, j: (i, 0)),
                  pl.BlockSpec((D, tn), lambda i, j: (0, j)), pl.BlockSpec((P, tn), lambda i, j: (0, j))],
        out_specs=[blk, blk, blk], out_shape=[sds, sds, sds],
        compiler_params=_cp(("parallel", "parallel")), name=name)(h, h, p, wg, wp)


def ple_bwd(dout, a, e, wg, name):
    T, D = dout.shape
    tm = _pick(T, 512, 16)

    def body(do_ref, a_ref, e_ref, wg_ref, dh_ref, da_ref, de_ref):
        do = do_ref[...]
        s = _sigmoid(a_ref[...])
        da = (do * e_ref[...] * s * (1.0 - s)).astype(BF16)
        da_ref[...] = da
        de_ref[...] = (do * s).astype(BF16)
        dh_ref[...] = do + _bdot_nt(da, wg_ref[...])

    row = pl.BlockSpec((tm, D), lambda i: (i, 0))
    return pl.pallas_call(
        body, grid=(T // tm,),
        in_specs=[row, row, row, pl.BlockSpec((D, D), lambda i: (0, 0))],
        out_specs=[row, row, row],
        out_shape=[jax.ShapeDtypeStruct((T, D), F32), jax.ShapeDtypeStruct((T, D), BF16), jax.ShapeDtypeStruct((T, D), BF16)],
        compiler_params=_cp(("parallel",)), name=name)(dout, a, e, wg)


def loss_head(y, target, name):
    T, D = y.shape
    tm = _pick(T, 512, 8)

    def body(y_ref, t_ref, loss_ref, dy_ref):
        i = pl.program_id(0)

        @pl.when(i == 0)
        def _():
            loss_ref[...] = jnp.zeros_like(loss_ref)

        err = y_ref[...] - t_ref[...]
        dy_ref[...] = err * (1.0 / D)
        per_tok = jnp.sum(err * err, axis=-1, keepdims=True) * (1.0 / D)
        loss_ref[...] += 0.5 * jnp.sum(per_tok, axis=0, keepdims=True)

    row = pl.BlockSpec((tm, D), lambda i: (i, 0))
    return pl.pallas_call(
        body, grid=(T // tm,), in_specs=[row, row],
        out_specs=[pl.BlockSpec((1, 1), lambda i: (0, 0)), row],
        out_shape=[jax.ShapeDtypeStruct((1, 1), F32), jax.ShapeDtypeStruct((T, D), F32)],
        compiler_params=_cp(("arbitrary",)), name=name)(y, target)


HALO = 8


def _conv_taps(pad_ref, w_ref, tm, base):
    acc = w_ref[0:1, :] * pad_ref[pl.ds(base, tm), :]
    for k in range(1, GDN_CONV):
        acc = acc + w_ref[k:k + 1, :] * pad_ref[pl.ds(base + k, tm), :]
    return acc


GDN_GROUP_W = GDN_W
GDN_PRE_ROWS = 512


def _head_segments():
    head = jnp.arange(GDN_W, dtype=jnp.int32) // GDN_D
    return (head[:, None] == head[None, :]).astype(BF16)


def _head_sums(x, seg):
    hi = x.astype(BF16)
    r1 = x - hi.astype(F32)
    mid = r1.astype(BF16)
    lo = (r1 - mid.astype(F32)).astype(BF16)
    d = functools.partial(jnp.dot, preferred_element_type=F32)
    return d(hi, seg) + d(mid, seg) + d(lo, seg)


def _gdn_pre_common(x_ref, halo_ref, w_ref, seg_ref, pad_ref, tm):
    i = pl.program_id(1)
    grp = pl.program_id(0)
    pad_ref[0:HALO, :] = jnp.where(i == 0, 0.0, halo_ref[...])
    pad_ref[HALO:HALO + tm, :] = x_ref[...]
    c = _conv_taps(pad_ref, w_ref, tm, HALO - (GDN_CONV - 1))
    s = _sigmoid(c)
    y = c * s
    r = lax.rsqrt(_head_sums(y * y, seg_ref[...]) + RMS_EPS)
    scale = jnp.where(grp < 1, GDN_D ** -0.5, 1.0)
    return grp < 2, c, s, y, r, scale


def gdn_pre_fwd(proj, conv_w, name):
    T = proj.shape[0]
    tm = _pick(T, GDN_PRE_ROWS, 8)
    GW = GDN_GROUP_W

    def body(x_ref, halo_ref, w_ref, seg_ref, o_ref, pad_ref):
        normed, c, s, y, r, scale = _gdn_pre_common(x_ref, halo_ref, w_ref, seg_ref, pad_ref, tm)
        o_ref[...] = jnp.where(normed, y * r * scale, y)

    return pl.pallas_call(
        body, grid=(3, T // tm),
        in_specs=[pl.BlockSpec((tm, GW), lambda hb, i: (i, hb)),
                  pl.BlockSpec((HALO, GW), lambda hb, i: (jnp.maximum(i * (tm // HALO) - 1, 0), hb)),
                  pl.BlockSpec((GDN_CONV, GW), lambda hb, i: (0, hb)), pl.BlockSpec((GW, GW), lambda hb, i: (0, 0))],
        out_specs=pl.BlockSpec((tm, GW), lambda hb, i: (i, hb)),
        out_shape=jax.ShapeDtypeStruct((T, 3 * GW), F32),
        scratch_shapes=[pltpu.VMEM((tm + HALO, GW), F32)],
        compiler_params=_cp(("parallel", "parallel")), name=name)(proj, proj, conv_w, _head_segments())


def gdn_pre_bwd_pointwise(proj, conv_w, dqkv, name, hosted=None):
    T = proj.shape[0]
    tm = _pick(T, GDN_PRE_ROWS, 8)
    GW = GDN_GROUP_W

    def body(x_ref, halo_ref, w_ref, seg_ref, d_ref, dc_ref, dw_ref, pad_ref):
        i = pl.program_id(1)
        normed, c, s, y, r, scale = _gdn_pre_common(x_ref, halo_ref, w_ref, seg_ref, pad_ref, tm)

        @pl.when(i == 0)
        def _():
            dw_ref[...] = jnp.zeros_like(dw_ref)

        d = d_ref[...]
        n = y * r
        dn = d * scale
        dy = jnp.where(normed, r * (dn - n * _head_sums(dn * n, seg_ref[...])), d)
        dc = dy * (s * (1.0 + c * (1.0 - s)))
        dc_ref[...] = dc
        for k in range(GDN_CONV):
            xs = pad_ref[pl.ds(HALO - (GDN_CONV - 1) + k, tm), :]
            dw_ref[k:k + 1, :] += jnp.sum(dc * xs, axis=0, keepdims=True)

    blk = pl.BlockSpec((tm, GW), lambda hb, i: (i, hb))
    wblk = pl.BlockSpec((GDN_CONV, GW), lambda hb, i: (0, hb))
    return _hosted_call(
        hosted, body, grid=(3, T // tm),
        in_specs=[blk, pl.BlockSpec((HALO, GW), lambda hb, i: (jnp.maximum(i * (tm // HALO) - 1, 0), hb)), wblk,
                  pl.BlockSpec((GW, GW), lambda hb, i: (0, 0)), blk],
        out_specs=[blk, wblk],
        out_shape=[jax.ShapeDtypeStruct((T, 3 * GW), F32), jax.ShapeDtypeStruct((GDN_CONV, 3 * GW), F32)],
        scratch_shapes=[pltpu.VMEM((tm + HALO, GW), F32)],
        compiler_params=_cp(("parallel", "arbitrary")), name=name)(proj, proj, conv_w, _head_segments(), dqkv)


def gdn_pre_bwd_conv(dc, conv_w_p, name):
    T = dc.shape[0]
    tm = _pick(T, GDN_PRE_ROWS, 8)
    nt = T // tm
    GW = GDN_GROUP_W

    def body(dc_ref, halo_ref, w_ref, dx_ref, pad_ref):
        i = pl.program_id(1)
        pad_ref[0:tm, :] = dc_ref[...]
        pad_ref[tm:tm + HALO, :] = jnp.where(i == nt - 1, 0.0, halo_ref[...])
        acc = w_ref[GDN_CONV - 1:GDN_CONV, :] * pad_ref[pl.ds(0, tm), :]
        for k in range(GDN_CONV - 1):
            acc = acc + w_ref[k:k + 1, :] * pad_ref[pl.ds(GDN_CONV - 1 - k, tm), :]
        dx_ref[...] = acc

    blk = pl.BlockSpec((tm, GW), lambda hb, i: (i, hb))
    return pl.pallas_call(
        body, grid=(3, nt),
        in_specs=[blk, pl.BlockSpec((HALO, GW), lambda hb, i: (jnp.minimum((i + 1) * (tm // HALO), T // HALO - 1), hb)),
                  pl.BlockSpec((GDN_CONV, GW), lambda hb, i: (0, hb))],
        out_specs=blk,
        out_shape=jax.ShapeDtypeStruct((T, 3 * GW), F32),
        scratch_shapes=[pltpu.VMEM((tm + HALO, GW), F32)],
        compiler_params=_cp(("parallel", "parallel")), name=name)(dc, dc, conv_w_p)


def _chunk_masks(C):
    row = _iota2((C, C), 0)
    col = _iota2((C, C), 1)
    return row >= col, row > col, row == col


GDN_FWD_CHUNKS = 4
BNN = (((2,), (1,)), ((0,), (0,)))
BNT = (((2,), (2,)), ((0,), (0,)))
BTN = (((1,), (1,)), ((0,), (0,)))


def _bmm(a, b, dims=BNN):
    return lax.dot_general(a.astype(BF16), b.astype(BF16), dims, preferred_element_type=F32)


def _hbmm(a, b):
    m = a.shape[1]
    a_hi, a_lo = _split2(a)
    b_hi, b_lo = _split2(b)
    r = lax.dot_general(jnp.concatenate([a_hi, a_lo], axis=1), b_hi, BNN, preferred_element_type=F32)
    return r[:, :m] + r[:, m:] + lax.dot_general(a_hi, b_lo, BNN, preferred_element_type=F32)


def _hbmm_tn(a, b):
    a_hi, a_lo = _split2(a)
    b_hi, b_lo = _split2(b)
    d = functools.partial(lax.dot_general, dimension_numbers=BTN, preferred_element_type=F32)
    return d(a_hi, b_hi) + d(a_lo, b_hi) + d(a_hi, b_lo)


def _col_to_row(colv, eye):
    return jnp.sum(jnp.where(eye, colv, 0.0), axis=1, keepdims=True)


def _row_to_col(rowv, eye):
    return jnp.sum(jnp.where(eye, rowv, 0.0), axis=2, keepdims=True)


def _unit_lower_inverse(A, eye):
    C = A.shape[1]
    P = jnp.where(eye, 1.0, 0.0) - A
    Bp = _hbmm(A, A)
    for _ in range(4):
        R = _hbmm(jnp.concatenate([Bp, P], axis=1), Bp)
        Bp = R[:, :C]
        P = P + R[:, C:]
    return P + _hbmm(P, Bp)


def _stack_heads(ref, first_head, n, row0=0):
    rows = pl.ds(row0, GDN_CHUNK)
    return jnp.stack([ref[rows, pl.ds((first_head + h) * GDN_D, GDN_D)] for h in range(n)])


def _unstack_heads(ref, first_head, val, row0=0):
    rows = pl.ds(row0, GDN_CHUNK)
    for h in range(val.shape[0]):
        ref[rows, pl.ds((first_head + h) * GDN_D, GDN_D)] = val[h]


def _gdn_gates(gab, a_row, dt_row, incl):
    g_all = -jnp.exp(a_row) * _softplus(gab + dt_row)
    beta_all = _sigmoid(gab)
    gc_all = _ones_dot_left(incl.astype(BF16), g_all)
    return g_all, beta_all, gc_all


def _gdn_common(qkv_ref, gc_all, beta_all, incl, strict, eye, row0=0):
    C, H = GDN_CHUNK, GDN_HEADS
    q, k, v = (_stack_heads(qkv_ref, j * H, H, row0) for j in range(3))
    gc = jnp.stack([gc_all[:, h:h + 1] for h in range(H)])
    beta = jnp.stack([beta_all[:, H + h:H + h + 1] for h in range(H)])
    gc_row = _col_to_row(gc, eye)
    decay = jnp.where(incl, jnp.exp(jnp.where(incl, gc - gc_row, 0.0)), 0.0)
    e_gc = jnp.exp(gc)
    gl = gc[:, C - 1:C, :]
    e_gl = jnp.exp(gl)
    ekd = jnp.exp(gl - gc)
    kb = k * beta
    A = jnp.where(strict, _bmm(kb, k, BNT) * decay, 0.0)
    Pm = jnp.where(incl, _bmm(q, k, BNT) * decay, 0.0)
    return q, k, v, gc, beta, decay, e_gc, e_gl, ekd, kb, A, Pm


def gdn_chunk_fwd(qkv, proj, a_row, dt_row, norm_w, name, hosted=None):
    T = qkv.shape[0]
    C, H, Dh = GDN_CHUNK, GDN_HEADS, GDN_D
    N = T // C
    J = GDN_FWD_CHUNKS if N % GDN_FWD_CHUNKS == 0 else 1

    def body(qkv_ref, gz_ref, gab_ref, a_ref, dt_ref, nw_ref, o_ref, opre_ref, Tm_ref, Sin_ref, S_ref):
        n = pl.program_id(0)

        @pl.when(n == 0)
        def _():
            S_ref[...] = jnp.zeros_like(S_ref)

        incl, strict, eye = _chunk_masks(C)
        gab = gab_ref[...]
        per_chunk = []
        for j in range(J):
            _, beta_all, gc_all = _gdn_gates(gab[j * C:(j + 1) * C], a_ref[...], dt_ref[...], incl)
            per_chunk.append(_gdn_common(qkv_ref, gc_all, beta_all, incl, strict, eye, row0=j * C))
        q, k, v, gc, beta, decay, e_gc, e_gl, ekd, kb, A, Pm = (jnp.concatenate(t, axis=0) for t in zip(*per_chunk))
        Tm = _unit_lower_inverse(A, eye)
        u = _hbmm(Tm, v * beta)
        w = _hbmm(Tm, kb * e_gc)
        qd = q * e_gc
        kd = k * ekd
        S = S_ref[...]
        for j in range(J):
            hs = slice(j * H, (j + 1) * H)
            v_new = u[hs] - _bmm(w[hs], S)
            o = _bmm(qd[hs], S) + _bmm(Pm[hs], v_new)
            Sin_ref[j] = S
            Tm_ref[j] = Tm[hs]
            S = S * e_gl[hs] + _bmm(kd[hs], v_new, BTN)
            r = lax.rsqrt(jnp.mean(o * o, axis=-1, keepdims=True) + RMS_EPS)
            gz = _stack_heads(gz_ref, 0, H, j * C)
            _unstack_heads(opre_ref, 0, o, j * C)
            _unstack_heads(o_ref, 0, o * r * nw_ref[...] * (gz * _sigmoid(gz)), j * C)
        S_ref[...] = S

    vec = pl.BlockSpec((1, LANES), lambda n: (0, 0))
    hblk = pl.BlockSpec((J * C, GDN_W), lambda n: (n, 0))
    sblk = pl.BlockSpec((J, H, Dh, Dh), lambda n: (n, 0, 0, 0))
    return _hosted_call(
        hosted, body, grid=(N // J,),
        in_specs=[pl.BlockSpec((J * C, 3 * GDN_W), lambda n: (n, 0)),
                  pl.BlockSpec((J * C, GDN_W), lambda n: (n, CB_GZ * LANES // GDN_W)),
                  pl.BlockSpec((J * C, LANES), lambda n: (n, CB_GAB)), vec, vec, pl.BlockSpec((1, Dh), lambda n: (0, 0))],
        out_specs=[hblk, hblk, sblk, sblk],
        out_shape=[jax.ShapeDtypeStruct((T, GDN_W), F32), jax.ShapeDtypeStruct((T, GDN_W), F32),
                   jax.ShapeDtypeStruct((N, H, Dh, Dh), F32), jax.ShapeDtypeStruct((N, H, Dh, Dh), F32)],
        scratch_shapes=[pltpu.VMEM((H, Dh, Dh), F32)],
        compiler_params=_cp(("arbitrary",)), name=name)(qkv, proj, proj, a_row, dt_row, norm_w)


def gdn_chunk_bwd(qkv, proj, a_row, dt_row, norm_w, opre, Tm_all, Sin_all, docat, name, hosted=None):
    T = qkv.shape[0]
    C, H, Dh = GDN_CHUNK, GDN_HEADS, GDN_D
    N = T // C

    def body(qkv_ref, gz_ref, gab_ref, a_ref, dt_ref, nw_ref, opre_ref, Tm_ref, Sin_ref, do_ref,
             dqkv_ref, dgz_ref, dgab_ref, da_ref, ddt_ref, dnw_ref, dS_ref):
        n = pl.program_id(0)

        @pl.when(n == 0)
        def _():
            dS_ref[...] = jnp.zeros_like(dS_ref)
            da_ref[...] = jnp.zeros_like(da_ref)
            ddt_ref[...] = jnp.zeros_like(ddt_ref)
            dnw_ref[...] = jnp.zeros_like(dnw_ref)

        incl, strict, eye = _chunk_masks(C)
        gab = gab_ref[...]
        g_all, beta_all, gc_all = _gdn_gates(gab, a_ref[...], dt_ref[...], incl)
        lane = _iota2((C, LANES), 1)
        rowi = _iota2((C, 1), 0)
        nw = nw_ref[...]
        q, k, v, gc, beta, decay, e_gc, e_gl, ekd, kb, A, Pm = _gdn_common(qkv_ref, gc_all, beta_all, incl, strict, eye)
        Tm = Tm_ref[0]
        S = Sin_ref[0]
        dS = dS_ref[...]
        kbe = kb * e_gc
        u = _hbmm(Tm, v * beta)
        w = _hbmm(Tm, kbe)
        qd = q * e_gc
        kd = k * ekd
        v_new = u - _bmm(w, S)
        o = _stack_heads(opre_ref, 0, H)
        gz = _stack_heads(gz_ref, 0, H)
        don = _stack_heads(do_ref, 0, H)
        r = lax.rsqrt(jnp.mean(o * o, axis=-1, keepdims=True) + RMS_EPS)
        nn = o * r
        sgz = _sigmoid(gz)
        silu = gz * sgz
        _unstack_heads(dgz_ref, 0, don * nn * nw * (sgz * (1.0 + gz * (1.0 - sgz))))
        dnn = don * nw * silu
        dnw_ref[...] += jnp.sum(jnp.sum(don * nn * silu, axis=0), axis=0, keepdims=True)
        do = r * (dnn - nn * jnp.mean(dnn * nn, axis=-1, keepdims=True))
        dv_new = _bmm(Pm, do, BTN) + _bmm(kd, dS)
        dPm = jnp.where(incl, _bmm(do, v_new, BNT), 0.0)
        dqd = _bmm(do, S, BNT)
        dkd = _bmm(v_new, dS, BNT)
        dS_ref[...] = _bmm(qd, do, BTN) + e_gl * dS - _bmm(w, dv_new, BTN)
        dgl = jnp.sum(jnp.sum(dS * S, axis=2, keepdims=True), axis=1, keepdims=True) * e_gl
        dw = -_bmm(dv_new, S, BNT)
        dvb = _hbmm_tn(Tm, dv_new)
        dkbe = _hbmm_tn(Tm, dw)
        dA = -jnp.where(strict, _bmm(dvb, u, BNT) + _bmm(dkbe, w, BNT), 0.0)
        dAD = dA * decay
        dPD = dPm * decay
        Gm = dA * A + dPm * Pm
        dgc = jnp.sum(Gm, axis=2, keepdims=True) - _row_to_col(jnp.sum(Gm, axis=1, keepdims=True), eye)
        dkb = _bmm(dAD, k) + dkbe * e_gc
        dk = _bmm(dAD, kb, BTN) + _bmm(dPD, q, BTN) + dkd * ekd + dkb * beta
        dq = _bmm(dPD, k) + dqd * e_gc
        tkd = jnp.sum(dkd * kd, axis=-1, keepdims=True)
        dgc = dgc + jnp.sum(dqd * qd, axis=-1, keepdims=True) - tkd + jnp.sum(dkbe * kbe, axis=-1, keepdims=True)
        dgl = dgl + jnp.sum(tkd, axis=1, keepdims=True)
        dgc = dgc + jnp.where(rowi == C - 1, dgl, 0.0)
        dbeta = jnp.sum(dvb * v, axis=-1, keepdims=True) + jnp.sum(dkb * k, axis=-1, keepdims=True)
        _unstack_heads(dqkv_ref, 0, dq)
        _unstack_heads(dqkv_ref, H, dk)
        _unstack_heads(dqkv_ref, 2 * H, dvb * beta)
        dgc_all = jnp.zeros((C, LANES), F32)
        dbeta_all = jnp.zeros((C, LANES), F32)
        for h in range(H):
            dgc_all = dgc_all + jnp.where(lane == h, dgc[h], 0.0)
            dbeta_all = dbeta_all + jnp.where(lane == H + h, dbeta[h], 0.0)
        upper = (_iota2((C, C), 0) <= _iota2((C, C), 1)).astype(BF16)
        dg_all = _ones_dot_left(upper, dgc_all)
        dga = dg_all * (-jnp.exp(a_ref[...])) * _sigmoid(gab + dt_ref[...])
        dgb = dbeta_all * beta_all * (1.0 - beta_all)
        dgab_ref[...] = jnp.where(lane < H, dga, jnp.where(lane < 2 * H, dgb, 0.0))
        da_ref[...] += jnp.sum(jnp.where(lane < H, dg_all * g_all, 0.0), axis=0, keepdims=True)
        ddt_ref[...] += jnp.sum(jnp.where(lane < H, dga, 0.0), axis=0, keepdims=True)

    rev = lambda n: N - 1 - n
    vec = pl.BlockSpec((1, LANES), lambda n: (0, 0))
    nwv = pl.BlockSpec((1, Dh), lambda n: (0, 0))
    hblk = pl.BlockSpec((C, GDN_W), lambda n: (rev(n), 0))
    sblk = pl.BlockSpec((1, H, Dh, Dh), lambda n: (rev(n), 0, 0, 0))
    qblk = pl.BlockSpec((C, 3 * GDN_W), lambda n: (rev(n), 0))
    return _hosted_call(
        hosted, body, grid=(N,),
        in_specs=[qblk, pl.BlockSpec((C, GDN_W), lambda n: (rev(n), CB_GZ * LANES // GDN_W)),
                  pl.BlockSpec((C, LANES), lambda n: (rev(n), CB_GAB)), vec, vec, nwv, hblk, sblk, sblk, hblk],
        out_specs=[qblk, hblk, pl.BlockSpec((C, LANES), lambda n: (rev(n), 0)), vec, vec, nwv],
        out_shape=[jax.ShapeDtypeStruct((T, 3 * GDN_W), F32), jax.ShapeDtypeStruct((T, GDN_W), F32),
                   jax.ShapeDtypeStruct((T, LANES), F32), jax.ShapeDtypeStruct((1, LANES), F32),
                   jax.ShapeDtypeStruct((1, LANES), F32), jax.ShapeDtypeStruct((1, Dh), F32)],
        scratch_shapes=[pltpu.VMEM((H, Dh, Dh), F32)],
        compiler_params=_cp(("arbitrary",)), name=name)(qkv, proj, proj, a_row, dt_row, norm_w, opre, Tm_all, Sin_all, docat)


ATT_BQ, ATT_BK = 512, 512
NEG_BIG = -1e30


def _att_blocks(T):
    bq, bk = min(ATT_BQ, T), min(ATT_BK, T)
    assert bk % bq == 0 and T % bk == 0
    return bq, bk


def _att_specs(T, bq, cbs):
    qspec = lambda cb: pl.BlockSpec((bq, LANES), lambda h, i: (i, cb + h))
    kspec = lambda cb: pl.BlockSpec((T, LANES), lambda h, i: (0, cb + h))
    return qspec, kspec


def _kblock(ref, kb, bk):
    return ref[pl.ds(pl.multiple_of(kb * bk, bk), bk), :]


def _att_pos(i, kb, bq, bk):
    qpos = i * bq + _iota2((bq, bk), 0)
    kpos = kb * bk + _iota2((bq, bk), 1)
    return qpos, kpos


def _later_keys(n):
    return (_iota2((n, n), 0) > _iota2((n, n), 1)).astype(BF16)


def _earlier_keys(n):
    return (_iota2((n, n), 0) < _iota2((n, n), 1)).astype(BF16)


def _tri_dot(x, tri, terms):
    acc, rest = None, x
    for t in range(terms):
        part = rest.astype(BF16)
        if t + 1 < terms:
            rest = rest - part.astype(F32)
        d = jnp.dot(part, tri, preferred_element_type=F32)
        acc = d if acc is None else acc + d
    return acc


SB_BLOCK = 256
SB_DEAD = -104.0


def _sb_blocks(T):
    b = min(SB_BLOCK, T)
    assert T % b == 0 and T // b <= LANES
    return b, b


def sb_fwd(proj, name, hosted=None):
    T = proj.shape[0]
    H = SB_HEADS
    bq, bk = _sb_blocks(T)
    scale = SB_DIM ** -0.5

    def body(q_ref, k_ref, v_ref, o_ref, tot_ref):
        i = pl.program_id(1)
        qb = q_ref[...].astype(BF16)
        diag = (i * bq) // bk
        lane = _iota2((bq, LANES), 1)
        later = _later_keys(bk)

        def block(kb, acc, R, masked):
            z = _bdot_nt(qb, _kblock(k_ref, kb, bk)) * scale
            sp = _softplus(z)
            if masked:
                qpos, kpos = _att_pos(i, kb, bq, bk)
                mask = kpos < qpos
                l1m = jnp.where(mask, -sp, 0.0)
            else:
                l1m = -sp
            W = jnp.exp((z - sp) + _tri_dot(l1m, later, 3) + R)
            if masked:
                W = jnp.where(mask, W, 0.0)
            acc = acc + _bdot(W, _kblock(v_ref, kb, bk))
            return acc, R + jnp.sum(l1m, axis=-1, keepdims=True)

        acc, R = block(diag, jnp.zeros((bq, LANES), F32), jnp.zeros((bq, 1), F32), True)

        def live(c):
            return jnp.logical_and(c[0] >= 0, jnp.max(c[2]) > SB_DEAD)

        def step(c):
            kb, acc, R, Rb = c
            acc, R_next = block(kb, acc, R, False)
            return kb - 1, acc, R_next, jnp.where(lane == kb, R, Rb)

        _, acc, _, Rb = lax.while_loop(live, step, (diag - 1, acc, R, jnp.where(lane == diag, 0.0, NEG_BIG)))
        o_ref[...] = acc
        tot_ref[...] = Rb

    qspec, kspec = _att_specs(T, bq, None)
    sds = jax.ShapeDtypeStruct((T, H * LANES), F32)
    oblk = pl.BlockSpec((bq, LANES), lambda h, i: (i, h))
    return _hosted_call(
        hosted, body, grid=(H, T // bq), in_specs=[qspec(CB_SQ), kspec(CB_SK), kspec(CB_SV)],
        out_specs=[oblk, oblk], out_shape=[sds, sds],
        compiler_params=_cp(("parallel", "parallel")), name=name)(proj, proj, proj)


def sb_bwd(proj, tot, docat, do_cb, name):
    T = proj.shape[0]
    H = SB_HEADS
    bq, bk = _sb_blocks(T)
    scale = SB_DIM ** -0.5

    def body(q_ref, k_ref, v_ref, tot_ref, do_ref, dq_ref, dk_ref, dv_ref):
        i = pl.program_id(1)

        @pl.when(i == 0)
        def _():
            dk_ref[...] = jnp.zeros_like(dk_ref)
            dv_ref[...] = jnp.zeros_like(dv_ref)

        qb = q_ref[...].astype(BF16)
        dob = do_ref[...].astype(BF16)
        Rb = tot_ref[...]
        diag = (i * bq) // bk
        lane = _iota2((bq, LANES), 1)
        later, earlier = _later_keys(bk), _earlier_keys(bk)
        first = lax.while_loop(
            lambda kb: jnp.logical_and(kb < diag, jnp.max(jnp.where(lane == kb, Rb, NEG_BIG)) <= SB_DEAD),
            lambda kb: kb + 1, jnp.int32(0))

        def block(kb, carry, masked):
            dq, Epre = carry
            R = jnp.sum(jnp.where(lane == kb, Rb, 0.0), axis=1, keepdims=True)
            kblk = _kblock(k_ref, kb, bk).astype(BF16)
            z = _bdot_nt(qb, kblk) * scale
            sp = _softplus(z)
            if masked:
                qpos, kpos = _att_pos(i, kb, bq, bk)
                mask = kpos < qpos
                l1m = jnp.where(mask, -sp, 0.0)
            else:
                l1m = -sp
            W = jnp.exp((z - sp) + _tri_dot(l1m, later, 3) + R)
            if masked:
                W = jnp.where(mask, W, 0.0)
            E = _bdot_nt(dob, _kblock(v_ref, kb, bk)) * W
            cexcl = _tri_dot(E, earlier, 3) + Epre
            neg = jnp.exp(-sp)
            dz = E * neg - cexcl * (1.0 - neg)
            if masked:
                dz = jnp.where(mask, dz, 0.0)
            dz = (dz * scale).astype(BF16)
            rows = pl.ds(pl.multiple_of(kb * bk, bk), bk)
            dk_ref[rows, :] += lax.dot_general(dz, qb, TN_DIMS, preferred_element_type=F32)
            dv_ref[rows, :] += lax.dot_general(W.astype(BF16), dob, TN_DIMS, preferred_element_type=F32)
            dq = dq + jnp.dot(dz, kblk, preferred_element_type=F32)
            return dq, Epre + jnp.sum(E, axis=-1, keepdims=True)

        init = (jnp.zeros((bq, LANES), F32), jnp.zeros((bq, 1), F32))
        carry = lax.fori_loop(first, diag, lambda kb, c: block(kb, c, False), init)
        dq, _ = block(diag, carry, True)
        dq_ref[...] = dq

    qspec, kspec = _att_specs(T, bq, None)
    sds = jax.ShapeDtypeStruct((T, H * LANES), F32)
    oblk = pl.BlockSpec((bq, LANES), lambda h, i: (i, h))
    kout = pl.BlockSpec((T, LANES), lambda h, i: (0, h))
    return pl.pallas_call(
        body, grid=(H, T // bq),
        in_specs=[qspec(CB_SQ), kspec(CB_SK), kspec(CB_SV), oblk, qspec(do_cb)],
        out_specs=[oblk, kout, kout], out_shape=[sds, sds, sds],
        compiler_params=_cp(("arbitrary", "arbitrary")), name=name)(proj, proj, proj, tot, docat)


def mla_fwd(Q, K, V, name, hosted=None):
    T = Q.shape[0]
    H = MLA_HEADS
    bq, bk = _att_blocks(T)
    scale = (MLA_NOPE + MLA_ROPE) ** -0.5

    def body(q_ref, k_ref, v_ref, o_ref, lse_ref):
        i = pl.program_id(1)
        qb = q_ref[...]
        diag = (i * bq) // bk

        def block(kb, carry, masked):
            acc, m, l = carry
            s = _bdot_nt(qb, _kblock(k_ref, kb, bk)) * scale
            if masked:
                qpos, kpos = _att_pos(i, kb, bq, bk)
                s = jnp.where(kpos <= qpos, s, NEG_BIG)
            m_new = jnp.maximum(m, jnp.max(s, axis=-1, keepdims=True))
            p = jnp.exp(s - m_new)
            corr = jnp.exp(m - m_new)
            acc = corr * acc + _bdot(p, _kblock(v_ref, kb, bk))
            return acc, m_new, corr * l + jnp.sum(p, axis=-1, keepdims=True)

        init = (jnp.zeros((bq, LANES), F32), jnp.full((bq, 1), NEG_BIG, F32), jnp.zeros((bq, 1), F32))
        carry = lax.fori_loop(0, diag, lambda kb, c: block(kb, c, False), init)
        acc, m, l = block(diag, carry, True)
        o_ref[...] = acc / l
        lse_ref[...] = jnp.broadcast_to(m + jnp.log(l), (bq, LANES))

    qspec, kspec = _att_specs(T, bq, None)
    sds = jax.ShapeDtypeStruct((T, H * LANES), F32)
    oblk = pl.BlockSpec((bq, LANES), lambda h, i: (i, h))
    return _hosted_call(
        hosted, body, grid=(H, T // bq), in_specs=[qspec(0), kspec(0), kspec(0)],
        out_specs=[oblk, oblk], out_shape=[sds, sds],
        compiler_params=_cp(("parallel", "parallel")), name=name)(Q, K, V)


def mla_bwd(Q, K, V, o, lse, docat, do_cb, name, hosted=None):
    T = Q.shape[0]
    H = MLA_HEADS
    bq, bk = _att_blocks(T)
    scale = (MLA_NOPE + MLA_ROPE) ** -0.5

    def body(q_ref, k_ref, v_ref, o_ref, lse_ref, do_ref, dq_ref, dk_ref, dv_ref):
        i = pl.program_id(1)

        @pl.when(i == 0)
        def _():
            dk_ref[...] = jnp.zeros_like(dk_ref)
            dv_ref[...] = jnp.zeros_like(dv_ref)

        qb = q_ref[...]
        do = do_ref[...]
        dob = do.astype(BF16)
        delta = jnp.sum(do * o_ref[...], axis=-1, keepdims=True)
        lse = lse_ref[:, 0:1]

        diag = (i * bq) // bk

        def block(kb, dq, masked):
            kblk = _kblock(k_ref, kb, bk)
            s = _bdot_nt(qb, kblk) * scale
            if masked:
                qpos, kpos = _att_pos(i, kb, bq, bk)
                s = jnp.where(kpos <= qpos, s, NEG_BIG)
            p = jnp.exp(s - lse)
            dp = _bdot_nt(dob, _kblock(v_ref, kb, bk))
            ds = (p * (dp - delta) * scale).astype(BF16)
            rows = pl.ds(pl.multiple_of(kb * bk, bk), bk)
            dk_ref[rows, :] += lax.dot_general(ds, qb, TN_DIMS, preferred_element_type=F32)
            dv_ref[rows, :] += lax.dot_general(p.astype(BF16), dob, TN_DIMS, preferred_element_type=F32)
            return dq + jnp.dot(ds, kblk, preferred_element_type=F32)

        dq = lax.fori_loop(0, diag, lambda kb, c: block(kb, c, False), jnp.zeros((bq, LANES), F32))
        dq_ref[...] = block(diag, dq, True)

    qspec, kspec = _att_specs(T, bq, None)
    sds = jax.ShapeDtypeStruct((T, H * LANES), F32)
    oblk = pl.BlockSpec((bq, LANES), lambda h, i: (i, h))
    kout = pl.BlockSpec((T, LANES), lambda h, i: (0, h))
    return _hosted_call(
        hosted, body, grid=(H, T // bq),
        in_specs=[qspec(0), kspec(0), kspec(0), oblk, oblk, qspec(do_cb)],
        out_specs=[oblk, kout, kout], out_shape=[sds, sds, sds],
        compiler_params=_cp(("arbitrary", "arbitrary")), name=name)(Q, K, V, o, lse, docat)


def _tile_heads(t, n):
    return jnp.concatenate([t] * n, axis=1)


def _rope(X, C, Sn, Sp):
    n = X.shape[1]
    return X * C + pltpu.roll(X, n - HALF_ROPE, 1) * Sn + pltpu.roll(X, HALF_ROPE, 1) * Sp


def _rope_t(dO, C, Sn, Sp):
    n = dO.shape[1]
    return dO * C + pltpu.roll(dO * Sn, HALF_ROPE, 1) + pltpu.roll(dO * Sp, n - HALF_ROPE, 1)


def _rms(x, w):
    r = lax.rsqrt(jnp.mean(x * x, axis=-1, keepdims=True) + RMS_EPS)
    xh = x * r
    return r, xh, xh * w


def _rms_bwd(dn, w, r, xh):
    dxh = dn * w
    return r * (dxh - xh * jnp.mean(dxh * xh, axis=-1, keepdims=True)), jnp.sum(dn * xh, axis=0, keepdims=True)


def _mla_pre_specs(T, tm):
    KV = MLA_KV_RANK
    QR = MLA_Q_RANK
    W = MLA_HEADS * LANES
    full = lambda shape: pl.BlockSpec(shape, lambda i: (0, 0))
    specs = [pl.BlockSpec((tm, QR), lambda i: (i, CB_MQ * LANES // QR)),
             pl.BlockSpec((tm, 2 * LANES), lambda i: (i, CB_MKV // 2)),
             full((1, QR)), full((1, KV))]
    rope = [pl.BlockSpec((tm, LANES), lambda i: (i, 0))] * 3
    return specs, rope, full, W


def mla_pre_fwd(proj, wq, wkv, wuq, wuk, wuv, ropeC, ropeSn, ropeSp, name):
    T = proj.shape[0]
    tm = _pick(T, 512, 16)
    KV = MLA_KV_RANK
    H = MLA_HEADS

    def body(mq_ref, mkv_ref, wq_ref, wkv_ref, wuq_ref, wuk_ref, wuv_ref, c_ref, sn_ref, sp_ref, Q_ref, K_ref, V_ref):
        C, Sn, Sp = (_tile_heads(t[...], H) for t in (c_ref, sn_ref, sp_ref))
        _, _, qn = _rms(mq_ref[...], wq_ref[...])
        Q_ref[...] = _rope(_bdot(qn, wuq_ref[...]), C, Sn, Sp).astype(BF16)
        mkv = mkv_ref[...]
        _, _, kvn = _rms(mkv[:, :KV], wkv_ref[...])
        kr = pltpu.roll(mkv[:, KV:], MLA_NOPE, 1)
        K_ref[...] = _rope(_bdot(kvn, wuk_ref[...]) + _tile_heads(kr, H), C, Sn, Sp).astype(BF16)
        V_ref[...] = _bdot(kvn, wuv_ref[...]).astype(BF16)

    specs, rope, full, W = _mla_pre_specs(T, tm)
    oblk = pl.BlockSpec((tm, W), lambda i: (i, 0))
    sds = jax.ShapeDtypeStruct((T, W), BF16)
    return pl.pallas_call(
        body, grid=(T // tm,),
        in_specs=specs + [full(wuq.shape), full(wuk.shape), full(wuv.shape)] + rope,
        out_specs=[oblk, oblk, oblk], out_shape=[sds, sds, sds],
        compiler_params=_cp(("parallel",)), name=name)(proj, proj, wq, wkv, wuq, wuk, wuv, ropeC, ropeSn, ropeSp)


def mla_pre_bwd(proj, wq, wkv, wuq, wuk, wuv, ropeC, ropeSn, ropeSp, dQ, dK, dV, name):
    T = proj.shape[0]
    tm = _pick(T, 512, 16)
    KV = MLA_KV_RANK
    H = MLA_HEADS

    def body(mq_ref, mkv_ref, wq_ref, wkv_ref, wuq_ref, wuk_ref, wuv_ref,
             c_ref, sn_ref, sp_ref, dQ_ref, dK_ref, dV_ref,
             dmq_ref, dmkv_ref, dwuq_ref, dwuk_ref, dwuv_ref, dwq_ref, dwkv_ref):
        i = pl.program_id(0)

        @pl.when(i == 0)
        def _():
            for ref in (dwuq_ref, dwuk_ref, dwuv_ref, dwq_ref, dwkv_ref):
                ref[...] = jnp.zeros_like(ref)

        C, Sn, Sp = (_tile_heads(t[...], H) for t in (c_ref, sn_ref, sp_ref))
        rq, xq, qn = _rms(mq_ref[...], wq_ref[...])
        mkv = mkv_ref[...]
        rkv, xkv, kvn = _rms(mkv[:, :KV], wkv_ref[...])
        dqf = _rope_t(dQ_ref[...], C, Sn, Sp)
        dkf = _rope_t(dK_ref[...], C, Sn, Sp)
        dv = dV_ref[...]
        dwuq_ref[...] += _bdot_tn(qn, dqf)
        dwuk_ref[...] += _bdot_tn(kvn, dkf)
        dwuv_ref[...] += _bdot_tn(kvn, dv)
        dmq, dwq = _rms_bwd(_bdot_nt(dqf, wuq_ref[...]), wq_ref[...], rq, xq)
        dckv, dwkv = _rms_bwd(_bdot_nt(dkf, wuk_ref[...]) + _bdot_nt(dv, wuv_ref[...]), wkv_ref[...], rkv, xkv)
        dwq_ref[...] += dwq
        dwkv_ref[...] += dwkv
        dmq_ref[...] = dmq
        dkr = dkf[:, 0:LANES]
        for h in range(1, H):
            dkr = dkr + dkf[:, h * LANES:(h + 1) * LANES]
        dkr = pltpu.roll(dkr, LANES - MLA_NOPE, 1)
        dkr = jnp.where(_iota2(dkr.shape, 1) < MLA_ROPE, dkr, 0.0)
        dmkv_ref[...] = jnp.concatenate([dckv, dkr], axis=1)

    specs, rope, full, W = _mla_pre_specs(T, tm)
    wide = pl.BlockSpec((tm, W), lambda i: (i, 0))
    return pl.pallas_call(
        body, grid=(T // tm,),
        in_specs=specs + [full(w.shape) for w in (wuq, wuk, wuv)] + rope + [wide, wide, wide],
        out_specs=[pl.BlockSpec((tm, MLA_Q_RANK), lambda i: (i, 0)), pl.BlockSpec((tm, 2 * LANES), lambda i: (i, 0)),
                   full(wuq.shape), full(wuk.shape), full(wuv.shape), full((1, MLA_Q_RANK)), full((1, KV))],
        out_shape=[jax.ShapeDtypeStruct((T, MLA_Q_RANK), F32), jax.ShapeDtypeStruct((T, 2 * LANES), F32),
                   jax.ShapeDtypeStruct(wuq.shape, F32), jax.ShapeDtypeStruct(wuk.shape, F32),
                   jax.ShapeDtypeStruct(wuv.shape, F32), jax.ShapeDtypeStruct((1, MLA_Q_RANK), F32),
                   jax.ShapeDtypeStruct((1, KV), F32)],
        compiler_params=_cp(("arbitrary",)), name=name)(
            proj, proj, wq, wkv, wuq, wuk, wuv, ropeC, ropeSn, ropeSp, dQ, dK, dV)


def all_gather(shards, name):
    n = len(shards)

    def body(*refs):
        x_refs, out_refs = refs[:n], refs[n:2 * n]
        send_sems, recv_sems, local_sems = refs[2 * n:]
        x, y, c = _place()
        me, sibling = (x, y, c), (x, y, 1 - c)
        chips = [(1 - x, y), (x, 1 - y), (1 - x, 1 - y)]

        def slot(a, px, py, pc):
            return out_refs[a].at[4 * px + 2 * py + pc]

        def copy(a, k, block, to, src=None):
            return pltpu.make_async_remote_copy(
                src_ref=slot(a, *block) if src is None else src, dst_ref=slot(a, *block),
                send_sem=send_sems.at[a, k], recv_sem=recv_sems.at[a, k], device_id=to, device_id_type=MESH)

        mine = [pltpu.make_async_copy(x_refs[a], slot(a, *me), local_sems.at[a]) for a in range(n)]
        first = []
        for a in range(n):
            mine[a].start()
            first.append(copy(a, 0, me, sibling, src=x_refs[a]))
            first += [copy(a, 1 + j, me, (*chip, c), src=x_refs[a]) for j, chip in enumerate(chips)]
        for cp in first:
            cp.start()
        passed = []
        for j, chip in enumerate(chips):
            for a in range(n):
                copy(a, 1 + j, (*chip, c), me).wait_recv()
                passed.append(copy(a, 4 + j, (*chip, c), sibling))
                passed[-1].start()
        for a in range(n):
            copy(a, 0, sibling, me).wait_recv()
            for j, chip in enumerate(chips):
                copy(a, 4 + j, (*chip, 1 - c), me).wait_recv()
        for cp in first + passed:
            cp.wait_send()
        for cp in mine:
            cp.wait()

    return pl.pallas_call(
        body, out_shape=[jax.ShapeDtypeStruct((N_DEV,) + s.shape, s.dtype) for s in shards],
        in_specs=[ANY] * n, out_specs=[ANY] * n,
        scratch_shapes=[pltpu.SemaphoreType.DMA((n, 7)), pltpu.SemaphoreType.DMA((n, 7)), pltpu.SemaphoreType.DMA((n,))],
        name=name)(*shards)


def exchange_partials(parts, name):
    n = len(parts)

    def body(*refs):
        src_refs, dst_refs = refs[:n], refs[n:2 * n]
        send_sems, recv_sems, local_sems = refs[2 * n:]
        x, y, c = _place()
        me = 4 * x + 2 * y + c
        copies = []
        mine = []
        for a in range(n):
            mine.append(pltpu.make_async_copy(src_refs[a].at[me], dst_refs[a].at[me], local_sems.at[a]))
            for k in range(1, N_DEV):
                px = 1 - x if k & 4 else x
                py = 1 - y if k & 2 else y
                pc = 1 - c if k & 1 else c
                copies.append(pltpu.make_async_remote_copy(
                    src_ref=src_refs[a].at[4 * px + 2 * py + pc], dst_ref=dst_refs[a].at[me],
                    send_sem=send_sems.at[a, k - 1], recv_sem=recv_sems.at[a, k - 1],
                    device_id=(px, py, pc), device_id_type=MESH))
        for cp in mine + copies:
            cp.start()
        for cp in copies:
            cp.wait_recv()
        for cp in copies:
            cp.wait_send()
        for cp in mine:
            cp.wait()

    return pl.pallas_call(
        body, out_shape=[jax.ShapeDtypeStruct(p.shape, p.dtype) for p in parts],
        in_specs=[ANY] * n, out_specs=[ANY] * n,
        scratch_shapes=[pltpu.SemaphoreType.DMA((n, 7)), pltpu.SemaphoreType.DMA((n, 7)), pltpu.SemaphoreType.DMA((n,))],
        name=name)(*parts)


def reduce_adamw(parts, w, m, v, name):
    L = len(parts)
    n, Rl, C = parts[0].shape
    R = w.shape[0]
    assert R == L * Rl
    tr = Rl if Rl * C <= 256 * 1024 else _pick(Rl, 256, 16)
    nr = Rl // tr

    def body(*refs):
        p_refs = refs[:L]
        w_ref, m_ref, v_ref, g_ref, d_ref, nm_ref, nv_ref, sum_ref = refs[L:]
        grp = pl.program_id(0)
        for j in range(L):
            @pl.when(grp == j)
            def _(j=j):
                acc = p_refs[j][0].astype(F32)
                for s in range(1, n):
                    acc = acc + p_refs[j][s].astype(F32)
                sum_ref[...] = acc

        g_ = sum_ref[...]
        m_ = ADAM_B1 * m_ref[...] + (1.0 - ADAM_B1) * g_
        v_ = ADAM_B2 * v_ref[...] + (1.0 - ADAM_B2) * (g_ * g_)
        m_hat = m_ / (1.0 - ADAM_B1 ** ADAM_STEP)
        v_hat = v_ / (1.0 - ADAM_B2 ** ADAM_STEP)
        g_ref[...] = g_
        d_ref[...] = -ADAM_LR * (m_hat / (jnp.sqrt(v_hat) + ADAM_EPS) + ADAM_WD * w_ref[...])
        nm_ref[...] = m_
        nv_ref[...] = v_

    blk = pl.BlockSpec((tr, C), lambda l, r: (l * nr + r, 0))
    sds = jax.ShapeDtypeStruct((R, C), F32)
    p_specs = [pl.BlockSpec((n, tr, C), lambda l, r, j=j: (0, jnp.where(l == j, r, 0), 0)) for j in range(L)]
    return pl.pallas_call(
        body, grid=(L, nr), in_specs=p_specs + [blk] * 3,
        out_specs=[blk] * 4, out_shape=[sds] * 4, scratch_shapes=[pltpu.VMEM((tr, C), F32)],
        compiler_params=_cp(("arbitrary", "arbitrary")), name=name)(*parts, w, m, v)


SHARDED = {"ffa_w_in": (2, BF16), "ffa_w_out": (1, BF16), "mix_w_in": (2, BF16), "mla_w_uq": (2, BF16),
           "mla_w_ukv": (2, BF16), "mix_w_o": (1, BF16), "ffb_w_in": (2, BF16), "ffb_w_out": (1, BF16),
           "ple_w_gate": (1, BF16), "ple_w_proj": (2, BF16), "gdn_conv_w": (2, F32), "ln_g": (2, F32), "ln_b": (2, F32)}
FFN_SLOT = ("ffa_w_in", "ffa_w_out", "ffb_w_in", "ffb_w_out")
REPLICATED = ("gdn_a_log", "gdn_dt_bias", "gdn_norm_w", "mla_q_norm_w", "mla_kv_norm_w")
WEIGHTS = ("ffa_w_in", "ffa_w_out", "mix_w_in", "gdn_conv_w", "gdn_a_log", "gdn_dt_bias", "gdn_norm_w", "mla_q_norm_w",
           "mla_kv_norm_w", "mla_w_uq", "mla_w_ukv", "mix_w_o", "ffb_w_in", "ffb_w_out", "ln_g", "ln_b", "ple_w_gate",
           "ple_w_proj")


def _to_slots(full, axis):
    L, a, b = full.shape
    if axis == 2:
        return full.reshape(L, a, N_DEV, b // N_DEV).transpose(2, 0, 1, 3).reshape(N_DEV, L * a, b // N_DEV)
    return full.reshape(L, N_DEV, a // N_DEV, b).transpose(1, 0, 2, 3).reshape(N_DEV, L * a // N_DEV, b)


def _from_slots(slots, shard_shape, axis):
    L, a, b = shard_shape
    t = slots.reshape((N_DEV,) + tuple(shard_shape))
    if axis == 2:
        return t.transpose(1, 2, 0, 3).reshape(L, a, N_DEV * b)
    return t.transpose(1, 0, 2, 3).reshape(L, N_DEV * a, b)


def _view2d(t):
    return t.reshape(-1, t.shape[-1])


def _pad_heads(w, nh):
    K = w.shape[0]
    return jnp.pad(w.reshape(K, nh, GDN_D), ((0, 0), (0, 0), (0, LANES - GDN_D))).reshape(K, nh * LANES)


def _unpad_heads(w, nh):
    K = w.shape[0]
    return w.reshape(K, nh, LANES)[:, :, :GDN_D].reshape(K, nh * GDN_D)


IN_WIDTHS = (512, 512, 512, 512, 8, 8, 256, 256, 256, 256, 160)


def _split_in(w):
    offs = np.cumsum((0,) + IN_WIDTHS)
    return [w[:, int(offs[i]):int(offs[i + 1])] for i in range(len(IN_WIDTHS))]


def _pad_in_proj(w):
    gq, gk, gv, gz, ga, gb, sq, sk, sv, mq, mkv = _split_in(w)
    gab = jnp.pad(jnp.concatenate([ga, gb], axis=1), ((0, 0), (0, LANES - 2 * GDN_HEADS)))
    return jnp.concatenate(
        [gq, gk, gv, gz] + [_pad_heads(t, SB_HEADS) for t in (sq, sk, sv)]
        + [mq, jnp.pad(mkv, ((0, 0), (0, 2 * LANES - mkv.shape[1]))), gab], axis=1)


def _unpad_in_proj(wp):
    c = lambda cb, n: wp[:, cb * LANES:(cb + n) * LANES]
    gab = c(CB_GAB, 1)
    parts = [c(cb, DO_SB) for cb in (CB_GQ, CB_GK, CB_GV, CB_GZ)]
    parts += [gab[:, :GDN_HEADS], gab[:, GDN_HEADS:2 * GDN_HEADS]]
    parts += [_unpad_heads(c(cb, SB_HEADS), SB_HEADS) for cb in (CB_SQ, CB_SK, CB_SV)]
    parts += [c(CB_MQ, 2), c(CB_MKV, 2)[:, :MLA_KV_RANK + MLA_ROPE]]
    return jnp.concatenate(parts, axis=1)


def _pad_lanes(w, width):
    return jnp.pad(w, ((0, 0), (0, width - w.shape[1])))


def _mla_up_pad(w_uq, w_ukv):
    H = MLA_HEADS
    dq = MLA_NOPE + MLA_ROPE
    wuq = jnp.pad(w_uq.reshape(-1, H, dq), ((0, 0), (0, 0), (0, LANES - dq))).reshape(-1, H * LANES)
    kv = w_ukv.reshape(-1, H, MLA_NOPE + MLA_V)
    wuk = jnp.pad(kv[:, :, :MLA_NOPE], ((0, 0), (0, 0), (0, LANES - MLA_NOPE))).reshape(-1, H * LANES)
    wuv = jnp.pad(kv[:, :, MLA_NOPE:], ((0, 0), (0, 0), (0, LANES - MLA_V))).reshape(-1, H * LANES)
    return wuq, wuk, wuv


def _mla_up_unpad(dwuq, dwuk, dwuv):
    H = MLA_HEADS
    dq = MLA_NOPE + MLA_ROPE
    g_uq = dwuq.reshape(-1, H, LANES)[:, :, :dq].reshape(-1, H * dq)
    g_ukv = jnp.concatenate([dwuk.reshape(-1, H, LANES)[:, :, :MLA_NOPE], dwuv.reshape(-1, H, LANES)[:, :, :MLA_V]],
                            axis=2).reshape(-1, H * (MLA_NOPE + MLA_V))
    return g_uq, g_ukv


def _rope_tables(positions):
    inv = 1.0 / (ROPE_BASE ** (jnp.arange(0, MLA_ROPE, 2, dtype=F32) / MLA_ROPE))
    ang = positions.astype(F32)[:, None] * inv
    cos, sin = jnp.cos(ang), jnp.sin(ang)
    T = positions.shape[0]
    one = lambda n: jnp.ones((T, n), F32)
    zero = lambda n: jnp.zeros((T, n), F32)
    tail = LANES - MLA_NOPE - MLA_ROPE
    C = jnp.concatenate([one(MLA_NOPE), cos, cos, one(tail)], axis=1)
    Sn = jnp.concatenate([zero(MLA_NOPE), -sin, zero(HALF_ROPE + tail)], axis=1)
    Sp = jnp.concatenate([zero(MLA_NOPE + HALF_ROPE), sin, zero(tail)], axis=1)
    return C, Sn, Sp


GATHER_FIRST = [("ffa_w_in", 0), ("ffa_w_out", 0)] + [(n, l) for l in range(DEPTH) for n in ("gdn_conv_w", "ln_g", "ln_b")]
GATHER_PLAN = {
    (0, "ffa_fwd"): [("mix_w_in", 0), ("mla_w_uq", 0), ("mla_w_ukv", 0)],
    (0, "gdn_chunk_fwd"): [("mix_w_o", 0), ("ffb_w_in", 0), ("mix_w_o", 1)],
    (0, "sb_fwd"): [("ffb_w_out", 0), ("ple_w_gate", 0), ("ple_w_proj", 0)],
    (0, "mla_fwd"): [("ffa_w_in", 1)],
    (0, "ffb_fwd"): [("ffa_w_out", 1)],
    (1, "ffa_fwd"): [("mix_w_in", 1)],
    (1, "in_proj"): [("mla_w_uq", 1), ("mla_w_ukv", 1)],
    (1, "gdn_chunk_fwd"): [("ffb_w_in", 1)],
    (1, "sb_fwd"): [("ffb_w_out", 1), ("ple_w_gate", 1), ("ple_w_proj", 1)],
}
SCATTER_PLAN = {
    (1, "gdn_chunk_bwd"): [("ffb_w_in", 1)],
    (1, "gdn_pre_bwd"): [("ffb_w_out", 1), ("ple_w_gate", 1), ("ple_w_proj", 1), ("mix_w_o", 1)],
    (1, "ffa_bwd"): [("mix_w_in", 1), ("mla_w_uq", 1), ("mla_w_ukv", 1), ("gdn_conv_w", 1)],
    (0, "ffb_bwd"): [("ffa_w_in", 1)],
    (0, "gdn_chunk_bwd"): [("ffb_w_in", 0)],
    (0, "gdn_pre_bwd"): [("ffb_w_out", 0), ("ple_w_gate", 0), ("ple_w_proj", 0), ("mix_w_o", 0)],
    (0, "mla_bwd"): [("ffa_w_out", 1), ("ln_g", 1), ("ln_b", 1)],
    (0, "ffa_bwd"): [("mix_w_in", 0), ("mla_w_uq", 0), ("mla_w_ukv", 0), ("gdn_conv_w", 0)],
    (0, "d_ffa_in"): [("ffa_w_out", 0), ("ln_g", 0), ("ln_b", 0)],
}
SCATTER_LAST = [("ffa_w_in", 0)]


class Exchanges:
    def __init__(self, shards):
        self.shards = shards
        self.full = {}
        self.partial = {}
        self.received = {}

    def _block(self, key):
        n, l = key
        return self.shards[n][l].astype(SHARDED[n][1])

    def _absorb_gather(self, keys, results):
        for (n, l), g in zip(keys, results):
            blk = self.shards[n][l]
            self.full[(n, l)] = g if n in FFN_SLOT else _from_slots(g, (1,) + blk.shape, SHARDED[n][0])[0]

    def gather_now(self, keys, name):
        self._absorb_gather(keys, all_gather([self._block(k) for k in keys], name))

    def gather_with(self, layer, tag):
        keys = GATHER_PLAN.get((layer, tag))
        return None if keys is None else (keys, Hosted("gather", [self._block(k) for k in keys]))

    def scatter_with(self, layer, tag):
        keys = SCATTER_PLAN.get((layer, tag))
        return None if keys is None else (keys, Hosted("scatter", [self.partial[k] for k in keys]))

    def done(self, carried):
        if carried is not None:
            keys, hosted = carried
            if hosted.kind == "gather":
                self._absorb_gather(keys, hosted.results)
            else:
                self.received.update(zip(keys, hosted.results))

    def add_grad(self, key, g):
        n, l = key
        self.partial[key] = g if n in FFN_SLOT else _to_slots(g[None], SHARDED[n][0]).astype(SHARDED[n][1])


def _carried(c):
    return None if c is None else c[1]


def _layer_fwd(h0, p_i, rope, i, ex, rep):
    L = "L%d_" % i
    S = {"h0": h0, "p": p_i}
    W = ex.full
    ln_g = [W[("ln_g", i)][j][None, :] for j in range(3)]
    ln_b = [W[("ln_b", i)][j][None, :] for j in range(3)]
    S["ln_g"] = ln_g
    c = ex.gather_with(i, "ffa_fwd")
    S["h1"], S["xh1"], S["rs1"], S["Ga"], S["Ua"], S["h1b"] = ffn_fwd(
        h0, W[("ffa_w_in", i)], W[("ffa_w_out", i)], ln_g[0], ln_b[0], L + "ffa_fwd", hosted=_carried(c))
    ex.done(c)
    S["win"] = _pad_in_proj(W[("mix_w_in", i)])
    c = ex.gather_with(i, "in_proj")
    S["proj"] = mm_nn(S["h1b"], S["win"], L + "in_proj", hosted=_carried(c))
    ex.done(c)
    S["conv"] = W[("gdn_conv_w", i)]
    S["a_row"] = _pad_lanes(rep["gdn_a_log"][i][None, :], LANES)
    S["dt_row"] = _pad_lanes(rep["gdn_dt_bias"][i][None, :], LANES)
    S["nw"] = rep["gdn_norm_w"][i][None, :]
    S["wq"] = rep["mla_q_norm_w"][i][None, :]
    S["wkv"] = rep["mla_kv_norm_w"][i][None, :]
    S["qkv"] = gdn_pre_fwd(S["proj"], S["conv"], L + "gdn_pre_fwd")
    c = ex.gather_with(i, "gdn_chunk_fwd")
    S["o_gdn"], S["opre"], S["Tm"], S["Sin"] = gdn_chunk_fwd(S["qkv"], S["proj"], S["a_row"], S["dt_row"], S["nw"],
                                                            L + "gdn_chunk_fwd", hosted=_carried(c))
    ex.done(c)
    c = ex.gather_with(i, "sb_fwd")
    S["o_sb"], S["tot"] = sb_fwd(S["proj"], L + "sb_fwd", hosted=_carried(c))
    ex.done(c)
    S["wuq"], S["wuk"], S["wuv"] = _mla_up_pad(W[("mla_w_uq", i)], W[("mla_w_ukv", i)])
    S["Q"], S["K"], S["V"] = mla_pre_fwd(S["proj"], S["wq"], S["wkv"], S["wuq"], S["wuk"], S["wuv"], *rope, L + "mla_pre_fwd")
    c = ex.gather_with(i, "mla_fwd")
    S["o_mla"], S["lse"] = mla_fwd(S["Q"], S["K"], S["V"], L + "mla_fwd", hosted=_carried(c))
    ex.done(c)
    wo = W[("mix_w_o", i)]
    wo_att = wo[GDN_W:].reshape(-1, GDN_D, wo.shape[1])
    S["wo"] = jnp.concatenate(
        [wo[:GDN_W], jnp.pad(wo_att, ((0, 0), (0, LANES - GDN_D), (0, 0))).reshape(-1, wo.shape[1])], axis=0)
    S["h2"], S["xh2"], S["rs2"], S["h2b"] = proj_res_ln([S["o_gdn"], S["o_sb"], S["o_mla"]], S["wo"], S["h1"],
                                                        ln_g[1], ln_b[1], L + "out_proj")
    c = ex.gather_with(i, "ffb_fwd")
    S["h3"], S["xh3"], S["rs3"], S["Gb"], S["Ub"], _ = ffn_fwd(
        S["h2"], W[("ffb_w_in", i)], W[("ffb_w_out", i)], ln_g[2], ln_b[2], L + "ffb_fwd", hosted=_carried(c))
    ex.done(c)
    h4, S["a"], S["e"] = ple_fwd(S["h3"], p_i, W[("ple_w_gate", i)], W[("ple_w_proj", i)], L + "ple_fwd")
    return h4, S


def _layer_bwd(dh4, S, rope, i, ex):
    L = "L%d_" % i
    W = ex.full
    Grep = {}
    dh3, da, de = ple_bwd(dh4, S["a"], S["e"], W[("ple_w_gate", i)], L + "ple_bwd")
    ex.add_grad(("ple_w_gate", i), mm_tn(S["h3"], da, L + "d_ple_gate"))
    ex.add_grad(("ple_w_proj", i), mm_tn(S["p"], de, L + "d_ple_proj"))
    dz3, dg2, db2 = ln_bwd(dh3, S["xh3"], S["rs3"], S["ln_g"][2], L + "ln3_bwd")
    c = ex.scatter_with(i, "ffb_bwd")
    dh2, dGb, dUb, actb = ffn_bwd(dz3, S["Gb"], S["Ub"], W[("ffb_w_in", i)], W[("ffb_w_out", i)], L + "ffb_bwd",
                                  hosted=_carried(c))
    ex.done(c)
    ex.add_grad(("ffb_w_in", i), ffn_dw_in(S["h2b"].T, dGb, dUb, L + "d_ffb_in"))
    ex.add_grad(("ffb_w_out", i), ffn_dw_out(actb, dz3, L + "d_ffb_out"))
    dz2, dg1, db1 = ln_bwd(dh2, S["xh2"], S["rs2"], S["ln_g"][1], L + "ln2_bwd")
    docat = mm_nn(dz2, S["wo"], L + "d_ocat", b_transposed=True)
    dwo_att = jnp.concatenate([mm_tn(S["o_sb"], dz2, L + "d_wo_sb"), mm_tn(S["o_mla"], dz2, L + "d_wo_mla")], axis=0)
    dwo_att = dwo_att.reshape(-1, LANES, dwo_att.shape[1])[:, :GDN_D, :].reshape(-1, dwo_att.shape[1])
    ex.add_grad(("mix_w_o", i), jnp.concatenate([mm_tn(S["o_gdn"], dz2, L + "d_wo_gdn"), dwo_att], axis=0))
    c = ex.scatter_with(i, "gdn_chunk_bwd")
    dqkv, dgz, dgab, d_alog, d_dt, d_nw = gdn_chunk_bwd(S["qkv"], S["proj"], S["a_row"], S["dt_row"], S["nw"],
                                                        S["opre"], S["Tm"], S["Sin"], docat, L + "gdn_chunk_bwd",
                                                        hosted=_carried(c))
    ex.done(c)
    c = ex.scatter_with(i, "gdn_pre_bwd")
    dc, dconv = gdn_pre_bwd_pointwise(S["proj"], S["conv"], dqkv, L + "gdn_pre_bwd", hosted=_carried(c))
    ex.done(c)
    dxqkv = gdn_pre_bwd_conv(dc, S["conv"], L + "gdn_conv_bwd")
    ex.add_grad(("gdn_conv_w", i), dconv)
    Grep["gdn_a_log"], Grep["gdn_dt_bias"], Grep["gdn_norm_w"] = d_alog[0, :GDN_HEADS], d_dt[0, :GDN_HEADS], d_nw[0]
    dsq, dsk, dsv = sb_bwd(S["proj"], S["tot"], docat, DO_SB, L + "sb_bwd")
    c = ex.scatter_with(i, "mla_bwd")
    dQ, dK, dV = mla_bwd(S["Q"], S["K"], S["V"], S["o_mla"], S["lse"], docat, DO_MLA, L + "mla_bwd",
                         hosted=_carried(c))
    ex.done(c)
    dmq, dmkv, dwuq, dwuk, dwuv, dwq, dwkv = mla_pre_bwd(
        S["proj"], S["wq"], S["wkv"], S["wuq"], S["wuk"], S["wuv"], *rope, dQ, dK, dV, L + "mla_pre_bwd")
    g_uq, g_ukv = _mla_up_unpad(dwuq, dwuk, dwuv)
    ex.add_grad(("mla_w_uq", i), g_uq)
    ex.add_grad(("mla_w_ukv", i), g_ukv)
    Grep["mla_q_norm_w"], Grep["mla_kv_norm_w"] = dwq[0], dwkv[0]
    dproj = jnp.concatenate([dxqkv, dgz, dsq, dsk, dsv, dmq, dmkv, dgab], axis=1).astype(BF16)
    ex.add_grad(("mix_w_in", i),
                _unpad_in_proj(mm_tn(S["h1b"].T, dproj, L + "d_in_proj", a_transposed=True)))
    dh1 = mm_nn(dproj, S["win"], L + "d_h1", res=dz2, res_scale=ALPHA, b_transposed=True)
    dz1, dg0, db0 = ln_bwd(dh1, S["xh1"], S["rs1"], S["ln_g"][0], L + "ln1_bwd")
    c = ex.scatter_with(i, "ffa_bwd")
    dh0, dGa, dUa, acta = ffn_bwd(dz1, S["Ga"], S["Ua"], W[("ffa_w_in", i)], W[("ffa_w_out", i)], L + "ffa_bwd",
                                  hosted=_carried(c))
    ex.done(c)
    ex.add_grad(("ffa_w_out", i), ffn_dw_out(acta, dz1, L + "d_ffa_out"))
    ex.add_grad(("ln_g", i), jnp.concatenate([dg0, dg1, dg2], axis=0))
    ex.add_grad(("ln_b", i), jnp.concatenate([db0, db1, db2], axis=0))
    c = ex.scatter_with(i, "d_ffa_in")
    ex.add_grad(("ffa_w_in", i), ffn_dw_in(S["h0"].T.astype(BF16), dGa, dUa, L + "d_ffa_in", hosted=_carried(c)))
    ex.done(c)
    return dh0, Grep


def _local_step(x, p, positions, target, ex, rep):
    assert DEPTH == 2
    rope = _rope_tables(positions)
    h, saved = x, []
    for i in range(DEPTH):
        h, S = _layer_fwd(h, p[i], rope, i, ex, rep)
        saved.append(S)
    loss, dh = loss_head(h, target, "loss_head")
    grads = [None] * DEPTH
    for i in reversed(range(DEPTH)):
        dh, grads[i] = _layer_bwd(dh, saved[i], rope, i, ex)
    return loss, dh, {n: jnp.stack([grads[i][n] for i in range(DEPTH)]) for n in REPLICATED}


def kernel(x, p, positions, ffa_w_in, ffa_w_out, mix_w_in, gdn_conv_w, gdn_a_log, gdn_dt_bias, gdn_norm_w, mla_q_norm_w, mla_kv_norm_w, mla_w_uq, mla_w_ukv, mix_w_o, ffb_w_in, ffb_w_out, ln_g, ln_b, ple_w_gate, ple_w_proj, loss_target, m_ffa_w_in, m_ffa_w_out, m_mix_w_in, m_gdn_conv_w, m_gdn_a_log, m_gdn_dt_bias, m_gdn_norm_w, m_mla_q_norm_w, m_mla_kv_norm_w, m_mla_w_uq, m_mla_w_ukv, m_mix_w_o, m_ffb_w_in, m_ffb_w_out, m_ln_g, m_ln_b, m_ple_w_gate, m_ple_w_proj, v_ffa_w_in, v_ffa_w_out, v_mix_w_in, v_gdn_conv_w, v_gdn_a_log, v_gdn_dt_bias, v_gdn_norm_w, v_mla_q_norm_w, v_mla_kv_norm_w, v_mla_w_uq, v_mla_w_ukv, v_mix_w_o, v_ffb_w_in, v_ffb_w_out, v_ln_g, v_ln_b, v_ple_w_gate, v_ple_w_proj):
    given = dict(locals())
    shards = {n: given[n] for n in WEIGHTS}
    ex = Exchanges({n: shards[n] for n in SHARDED})
    ex.gather_now(GATHER_FIRST, "gather_first")
    loss, grad_x, Grep = _local_step(x[0], p[:, 0], positions[0], loss_target[0], ex, {n: shards[n] for n in REPLICATED})
    loss = lax.psum(loss[0, 0], ("x", "y", "c"))
    ex.received.update(zip(SCATTER_LAST, exchange_partials([ex.partial[k] for k in SCATTER_LAST], "scatter_last")))
    rep_received = dict(zip(REPLICATED, all_gather([Grep[n] for n in REPLICATED], "gather_replicated_grads")))
    grad, delta, new_m, new_v = {}, {}, {}, {}
    for n in WEIGHTS:
        shape = shards[n].shape
        parts = [rep_received[n]] if n in REPLICATED else [ex.received[(n, l)] for l in range(DEPTH)]
        if parts[0].shape[1] % 8:
            parts = [jnp.concatenate(parts, axis=1)]
        outs = reduce_adamw(parts, _view2d(shards[n]), _view2d(given["m_" + n]), _view2d(given["v_" + n]),
                            "adamw_" + n)
        grad[n], delta[n], new_m[n], new_v[n] = (t.reshape(shape) for t in outs)
    return (loss, grad_x[None], *[grad[n] for n in WEIGHTS], *[delta[n] for n in WEIGHTS],
            *[new_m[n] for n in WEIGHTS], *[new_v[n] for n in WEIGHTS])
```

```python
import functools
import numpy as np
import jax
import jax.numpy as jnp
from jax import lax
from jax.experimental import pallas as pl
from jax.experimental.pallas import tpu as pltpu

F32 = jnp.float32
BF16 = jnp.bfloat16

DEPTH = 2
LN_EPS = 1e-5
RMS_EPS = 1e-6
ALPHA = (2 * DEPTH) ** 0.25
GDN_HEADS, GDN_D, GDN_CONV, GDN_CHUNK = 8, 64, 4, 64
SB_HEADS, SB_DIM = 4, 64
MLA_HEADS, MLA_NOPE, MLA_ROPE, MLA_V, MLA_Q_RANK, MLA_KV_RANK = 4, 64, 32, 64, 256, 128
ROPE_BASE = 10000.0
HALF_ROPE = MLA_ROPE // 2
LANES = 128
N_DEV = 8
ADAM_LR, ADAM_B1, ADAM_B2, ADAM_EPS, ADAM_WD, ADAM_STEP = 0.001, 0.9, 0.999, 1e-08, 0.01, 10

CB_GQ, CB_GK, CB_GV, CB_GZ = 0, 4, 8, 12
CB_SQ, CB_SK, CB_SV = 16, 20, 24
CB_MQ, CB_MKV, CB_GAB = 28, 30, 32
PROJ_W = 33 * LANES
GDN_W = GDN_HEADS * GDN_D
DO_SB = GDN_W // LANES
DO_MLA = DO_SB + SB_HEADS
VMEM_LIMIT = 56 * 1024 * 1024
MM_TILE = 1536

NT_DIMS = (((1,), (1,)), ((), ()))
TN_DIMS = (((0,), (0,)), ((), ()))


def _cp(sem):
    return pltpu.CompilerParams(dimension_semantics=sem, vmem_limit_bytes=VMEM_LIMIT)


def _bdot(a, b):
    return jnp.dot(a.astype(BF16), b.astype(BF16), preferred_element_type=F32)


def _bdot_nt(a, b):
    return lax.dot_general(a.astype(BF16), b.astype(BF16), NT_DIMS, preferred_element_type=F32)


def _bdot_tn(a, b):
    return lax.dot_general(a.astype(BF16), b.astype(BF16), TN_DIMS, preferred_element_type=F32)


def _split2(a):
    hi = a.astype(BF16)
    lo = (a - hi.astype(F32)).astype(BF16)
    return hi, lo


def _ones_dot_left(ones_bf16, x):
    hi = x.astype(BF16)
    r1 = x - hi.astype(F32)
    mid = r1.astype(BF16)
    lo = (r1 - mid.astype(F32)).astype(BF16)
    d = functools.partial(jnp.dot, preferred_element_type=F32)
    return d(ones_bf16, hi) + d(ones_bf16, mid) + d(ones_bf16, lo)


def _iota2(shape, dim):
    return lax.broadcasted_iota(jnp.int32, shape, dim)


def _sigmoid(x):
    return 0.5 * jnp.tanh(0.5 * x) + 0.5


def _softplus(x):
    return jnp.maximum(x, 0.0) + jnp.log(1.0 + jnp.exp(-jnp.abs(x)))


def _pick(n, limit, mult):
    if n <= limit:
        return n
    best = None
    for t in range(mult, limit + 1, mult):
        if n % t == 0:
            best = t
    assert best is not None, (n, limit, mult)
    return best


MESH = pl.DeviceIdType.MESH
ANY = pl.BlockSpec(memory_space=pl.ANY)


def _place():
    return lax.axis_index("x"), lax.axis_index("y"), lax.axis_index("c")


def _peer(k):
    x, y, c = _place()
    return (1 - x if k & 4 else x, 1 - y if k & 2 else y, 1 - c if k & 1 else c)


class Hosted:
    def __init__(self, kind, arrays):
        self.kind, self.arrays, self.n, self.results = kind, list(arrays), len(arrays), None

    def out_shapes(self):
        if self.kind == "gather":
            return [jax.ShapeDtypeStruct((N_DEV,) + a.shape, a.dtype) for a in self.arrays]
        return [jax.ShapeDtypeStruct(a.shape, a.dtype) for a in self.arrays]

    def sems(self):
        return [pltpu.SemaphoreType.DMA((self.n, N_DEV - 1)), pltpu.SemaphoreType.DMA((self.n, N_DEV - 1)),
                pltpu.SemaphoreType.DMA((self.n,))]

    def _copies(self, src_refs, dst_refs, send_sems, recv_sems, local_sems):
        x, y, c = _place()
        me = 4 * x + 2 * y + c
        local, remote = [], []
        for a in range(self.n):
            gather = self.kind == "gather"
            local.append(pltpu.make_async_copy(src_refs[a] if gather else src_refs[a].at[me], dst_refs[a].at[me],
                                               local_sems.at[a]))
            for k in range(1, N_DEV):
                px, py, pc = _peer(k)
                remote.append(pltpu.make_async_remote_copy(
                    src_ref=src_refs[a] if gather else src_refs[a].at[4 * px + 2 * py + pc], dst_ref=dst_refs[a].at[me],
                    send_sem=send_sems.at[a, k - 1], recv_sem=recv_sems.at[a, k - 1],
                    device_id=(px, py, pc), device_id_type=MESH))
        return local, remote

    def start(self, *refs):
        local, remote = self._copies(*refs)
        for cp in local + remote:
            cp.start()

    def wait(self, *refs):
        local, remote = self._copies(*refs)
        for cp in remote:
            cp.wait_recv()
        for cp in remote:
            cp.wait_send()
        for cp in local:
            cp.wait()


def _hosted_call(hosted, body, *, grid, in_specs, out_specs, out_shape, scratch_shapes=(), compiler_params, name):
    if hosted is None:
        return pl.pallas_call(body, grid=grid, in_specs=in_specs, out_specs=out_specs, out_shape=out_shape,
                              scratch_shapes=scratch_shapes, compiler_params=compiler_params, name=name)
    single = not isinstance(out_shape, (list, tuple))
    o_specs = [out_specs] if single else list(out_specs)
    o_shape = [out_shape] if single else list(out_shape)
    n_in, n_out, n_scr, n = len(in_specs), len(o_specs), len(scratch_shapes), hosted.n

    def wrapped(*refs):
        ins, c_in = refs[:n_in], refs[n_in:n_in + n]
        outs, c_out = refs[n_in + n:n_in + n + n_out], refs[n_in + n + n_out:n_in + 2 * n + n_out]
        rest = refs[n_in + 2 * n + n_out:]
        scr, sems = rest[:n_scr], rest[n_scr:]
        ids = [pl.program_id(ax) for ax in range(len(grid))]
        first = functools.reduce(jnp.logical_and, [i == 0 for i in ids])
        last = functools.reduce(jnp.logical_and, [i == g - 1 for i, g in zip(ids, grid)])

        @pl.when(first)
        def _():
            hosted.start(c_in, c_out, *sems)

        body(*ins, *outs, *scr)

        @pl.when(last)
        def _():
            hosted.wait(c_in, c_out, *sems)

    call = pl.pallas_call(
        wrapped, grid=grid, in_specs=list(in_specs) + [ANY] * n, out_specs=o_specs + [ANY] * n,
        out_shape=o_shape + hosted.out_shapes(), scratch_shapes=list(scratch_shapes) + hosted.sems(),
        compiler_params=_cp(("arbitrary",) * len(grid)), name=name)

    def run(*args):
        outs = call(*args, *hosted.arrays)
        hosted.results = list(outs[n_out:])
        return outs[0] if single else list(outs[:n_out])

    return run


def mm_nn(a, b, name, out_dtype=F32, res=None, res_scale=1.0, b_transposed=False, hosted=None):
    M, K = a.shape
    N = b.shape[0] if b_transposed else b.shape[1]
    tm, tn, tk = _pick(M, 512, 16), _pick(N, MM_TILE, LANES), _pick(K, MM_TILE, LANES)
    nk = K // tk
    has_res = res is not None
    dot = _bdot_nt if b_transposed else _bdot

    def body(*refs):
        if has_res:
            a_ref, b_ref, r_ref, o_ref, acc_ref = refs
        else:
            a_ref, b_ref, o_ref, acc_ref = refs
        k = pl.program_id(2)

        @pl.when(k == 0)
        def _():
            acc_ref[...] = jnp.zeros_like(acc_ref)

        acc_ref[...] += dot(a_ref[...], b_ref[...])

        @pl.when(k == nk - 1)
        def _():
            out = acc_ref[...]
            if has_res:
                out = out + res_scale * r_ref[...]
            o_ref[...] = out.astype(o_ref.dtype)

    b_spec = pl.BlockSpec((tn, tk), lambda i, j, k: (j, k)) if b_transposed else pl.BlockSpec((tk, tn), lambda i, j, k: (k, j))
    in_specs = [pl.BlockSpec((tm, tk), lambda i, j, k: (i, k)), b_spec]
    args = [a, b]
    if has_res:
        in_specs.append(pl.BlockSpec((tm, tn), lambda i, j, k: (i, j)))
        args.append(res)
    return _hosted_call(
        hosted, body, grid=(M // tm, N // tn, nk), in_specs=in_specs,
        out_specs=pl.BlockSpec((tm, tn), lambda i, j, k: (i, j)),
        out_shape=jax.ShapeDtypeStruct((M, N), out_dtype),
        scratch_shapes=[pltpu.VMEM((tm, tn), F32)],
        compiler_params=_cp(("parallel", "parallel", "arbitrary")), name=name)(*args)


def mm_tn(a, b, name, out_dtype=F32, a_transposed=False):
    K, T = a.shape if a_transposed else a.shape[::-1]
    _, N = b.shape
    tk = K if a_transposed else _pick(K, 512, LANES)
    tn, tt = _pick(N, MM_TILE, LANES), _pick(T, 512, LANES)
    nt = T // tt

    def body(a_ref, b_ref, o_ref, acc_ref):
        t = pl.program_id(2)

        @pl.when(t == 0)
        def _():
            acc_ref[...] = jnp.zeros_like(acc_ref)

        if a_transposed:
            acc_ref[...] += _bdot(a_ref[:, pl.ds(pl.multiple_of(t * tt, tt), tt)], b_ref[...])
        else:
            acc_ref[...] += _bdot_tn(a_ref[...], b_ref[...])

        @pl.when(t == nt - 1)
        def _():
            o_ref[...] = acc_ref[...].astype(o_ref.dtype)

    a_spec = pl.BlockSpec((K, T), lambda i, j, t: (0, 0)) if a_transposed else pl.BlockSpec((tt, tk), lambda i, j, t: (t, i))
    return pl.pallas_call(
        body, grid=(K // tk, N // tn, nt),
        in_specs=[a_spec, pl.BlockSpec((tt, tn), lambda i, j, t: (t, j))],
        out_specs=pl.BlockSpec((tk, tn), lambda i, j, t: (i, j)),
        out_shape=jax.ShapeDtypeStruct((K, N), out_dtype),
        scratch_shapes=[pltpu.VMEM((tk, tn), F32)],
        compiler_params=_cp(("parallel", "parallel", "arbitrary")), name=name)(a, b)


def _ln_apply(z, g, b):
    mu = jnp.mean(z, axis=-1, keepdims=True)
    zc = z - mu
    var = jnp.mean(zc * zc, axis=-1, keepdims=True)
    rstd = lax.rsqrt(var + LN_EPS)
    xhat = zc * rstd
    return xhat * g + b, xhat, rstd


def ln_bwd(dout, xhat, rstd, g, name):
    T, D = dout.shape
    tm = _pick(T, 512, 8)

    def body(do_ref, xh_ref, rs_ref, g_ref, dz_ref, dg_ref, db_ref):
        i = pl.program_id(0)

        @pl.when(i == 0)
        def _():
            dg_ref[...] = jnp.zeros_like(dg_ref)
            db_ref[...] = jnp.zeros_like(db_ref)

        do = do_ref[...]
        xh = xh_ref[...]
        dxh = do * g_ref[...]
        m1 = jnp.mean(dxh, axis=-1, keepdims=True)
        m2 = jnp.mean(dxh * xh, axis=-1, keepdims=True)
        dz_ref[...] = rs_ref[...] * (dxh - m1 - xh * m2)
        dg_ref[...] += jnp.sum(do * xh, axis=0, keepdims=True)
        db_ref[...] += jnp.sum(do, axis=0, keepdims=True)

    row = pl.BlockSpec((tm, D), lambda i: (i, 0))
    vec = pl.BlockSpec((1, D), lambda i: (0, 0))
    return pl.pallas_call(
        body, grid=(T // tm,),
        in_specs=[row, row, pl.BlockSpec((tm, 1), lambda i: (i, 0)), vec],
        out_specs=[row, vec, vec],
        out_shape=[jax.ShapeDtypeStruct((T, D), F32), jax.ShapeDtypeStruct((1, D), F32), jax.ShapeDtypeStruct((1, D), F32)],
        compiler_params=_cp(("arbitrary",)), name=name)(dout, xhat, rstd, g)


FFN_CHUNKS = N_DEV // 2


def ffn_fwd(h, w_in, w_out, g, b, name, hosted=None):
    T, D = h.shape
    fc = w_in.shape[2]
    half = w_out.shape[1]
    tm = _pick(T, 512, 8)
    nc = FFN_CHUNKS

    def body(h_ref, wg_ref, wu_ref, wo_ref, g_ref, b_ref, out_ref, xh_ref, rs_ref, G_ref, U_ref, ob_ref, acc_ref):
        c = pl.program_id(1)

        @pl.when(c == 0)
        def _():
            acc_ref[...] = jnp.zeros_like(acc_ref)

        hb = h_ref[...].astype(BF16)
        G = jnp.dot(hb, wg_ref[0], preferred_element_type=F32)
        U = jnp.dot(hb, wu_ref[0], preferred_element_type=F32)
        G_ref[0] = G
        U_ref[0] = U
        act = G * _sigmoid(G) * U
        acc_ref[...] += _bdot(act, wo_ref[...].reshape(2 * half, D))

        @pl.when(c == nc - 1)
        def _():
            z = ALPHA * h_ref[...] + 0.5 * acc_ref[...]
            out, xh, rs = _ln_apply(z, g_ref[...], b_ref[...])
            out_ref[...] = out
            ob_ref[...] = out.astype(BF16)
            xh_ref[...] = xh
            rs_ref[...] = rs

    row = pl.BlockSpec((tm, D), lambda i, c: (i, 0))
    vec = pl.BlockSpec((1, D), lambda i, c: (0, 0))
    cblk = pl.BlockSpec((1, tm, fc), lambda i, c: (c, i, 0))
    csds = jax.ShapeDtypeStruct((nc, T, fc), F32)
    return _hosted_call(
        hosted, body, grid=(T // tm, nc),
        in_specs=[row, pl.BlockSpec((1, D, fc), lambda i, c: (c, 0, 0)),
                  pl.BlockSpec((1, D, fc), lambda i, c: (c + nc, 0, 0)),
                  pl.BlockSpec((2, half, D), lambda i, c: (c, 0, 0)), vec, vec],
        out_specs=[row, row, pl.BlockSpec((tm, 1), lambda i, c: (i, 0)), cblk, cblk, row],
        out_shape=[jax.ShapeDtypeStruct((T, D), F32), jax.ShapeDtypeStruct((T, D), F32), jax.ShapeDtypeStruct((T, 1), F32),
                   csds, csds, jax.ShapeDtypeStruct((T, D), BF16)],
        scratch_shapes=[pltpu.VMEM((tm, D), F32)],
        compiler_params=_cp(("parallel", "arbitrary")), name=name)(h, w_in, w_in, w_out, g, b)


def ffn_bwd(dz, G, U, w_in, w_out, name, hosted=None):
    T, D = dz.shape
    nc, _, fc = G.shape
    half = w_out.shape[1]
    tm = _pick(T, 512, 16)

    def body(dz_ref, G_ref, U_ref, wg_ref, wu_ref, wo_ref, dh_ref, dG_ref, dU_ref, act_ref, acc_ref):
        c = pl.program_id(1)

        @pl.when(c == 0)
        def _():
            acc_ref[...] = jnp.zeros_like(acc_ref)

        dy = (0.5 * dz_ref[...]).astype(BF16)
        dact = _bdot_nt(dy, wo_ref[...].reshape(2 * half, D))
        G = G_ref[0]
        U = U_ref[0]
        s = _sigmoid(G)
        silu = G * s
        dG = (dact * U * (s * (1.0 + G * (1.0 - s)))).astype(BF16)
        dU = (dact * silu).astype(BF16)
        dG_ref[0] = dG
        dU_ref[0] = dU
        act_ref[0] = (silu * U).astype(BF16)
        acc_ref[...] += _bdot_nt(dG, wg_ref[0]) + _bdot_nt(dU, wu_ref[0])

        @pl.when(c == nc - 1)
        def _():
            dh_ref[...] = ALPHA * dz_ref[...] + acc_ref[...]

    row = pl.BlockSpec((tm, D), lambda i, c: (i, 0))
    cblk = pl.BlockSpec((1, tm, fc), lambda i, c: (c, i, 0))
    csds = jax.ShapeDtypeStruct((nc, T, fc), BF16)
    return _hosted_call(
        hosted, body, grid=(T // tm, nc),
        in_specs=[row, cblk, cblk, pl.BlockSpec((1, D, fc), lambda i, c: (c, 0, 0)),
                  pl.BlockSpec((1, D, fc), lambda i, c: (c + nc, 0, 0)),
                  pl.BlockSpec((2, half, D), lambda i, c: (c, 0, 0))],
        out_specs=[row, cblk, cblk, cblk],
        out_shape=[jax.ShapeDtypeStruct((T, D), F32), csds, csds, csds],
        scratch_shapes=[pltpu.VMEM((tm, D), F32)],
        compiler_params=_cp(("parallel", "arbitrary")), name=name)(dz, G, U, w_in, w_in, w_out)


def ffn_dw_in(h_t, dG, dU, name, hosted=None):
    D, T = h_t.shape
    nc, _, fc = dG.shape
    tt = _pick(T, 512, LANES)
    nt = T // tt

    def body(h_ref, dG_ref, dU_ref, o_ref, acc_ref):
        s = pl.program_id(0)
        t = pl.program_id(1)

        @pl.when(t == 0)
        def _():
            acc_ref[...] = jnp.zeros_like(acc_ref)

        hb = h_ref[:, pl.ds(pl.multiple_of(t * tt, tt), tt)]

        @pl.when(s < nc)
        def _():
            acc_ref[...] += jnp.dot(hb, dG_ref[0], preferred_element_type=F32)

        @pl.when(s >= nc)
        def _():
            acc_ref[...] += jnp.dot(hb, dU_ref[0], preferred_element_type=F32)

        @pl.when(t == nt - 1)
        def _():
            o_ref[0] = acc_ref[...].astype(o_ref.dtype)

    return _hosted_call(
        hosted, body, grid=(2 * nc, nt),
        in_specs=[pl.BlockSpec((D, T), lambda s, t: (0, 0)),
                  pl.BlockSpec((1, tt, fc), lambda s, t: (jnp.minimum(s, nc - 1), jnp.where(s < nc, t, nt - 1), 0)),
                  pl.BlockSpec((1, tt, fc), lambda s, t: (jnp.maximum(s - nc, 0), jnp.where(s >= nc, t, 0), 0))],
        out_specs=pl.BlockSpec((1, D, fc), lambda s, t: (s, 0, 0)),
        out_shape=jax.ShapeDtypeStruct((2 * nc, D, fc), BF16),
        scratch_shapes=[pltpu.VMEM((D, fc), F32)],
        compiler_params=_cp(("parallel", "arbitrary")), name=name)(h_t, dG, dU)


def ffn_dw_in_push(h_t, dG, dU, name, hosted=None):
    D, T = h_t.shape
    nc, _, fc = dG.shape
    tt = _pick(T, 512, LANES)
    nt = T // tt
    ns = 2 * nc

    def body(h_ref, dG_ref, dU_ref, recv_ref, acc_ref, stage_ref, send_sems, recv_sems, local_sem):
        s = pl.program_id(0)
        t = pl.program_id(1)
        x, y, c = _place()
        me = 4 * x + 2 * y + c

        def push(slot):
            rel = jnp.maximum(jnp.bitwise_xor(slot, me), 1)
            return pltpu.make_async_remote_copy(
                src_ref=stage_ref.at[slot % 2], dst_ref=recv_ref.at[me],
                send_sem=send_sems.at[slot % 2], recv_sem=recv_sems.at[rel - 1],
                device_id=(slot // 4, (slot // 2) % 2, slot % 2), device_id_type=MESH)

        def keep(slot):
            return pltpu.make_async_copy(stage_ref.at[slot % 2], recv_ref.at[me], local_sem)

        def release(slot):
            @pl.when(slot != me)
            def _():
                push(slot).wait_send()

            @pl.when(slot == me)
            def _():
                keep(slot).wait()

        @pl.when(t == 0)
        def _():
            acc_ref[...] = jnp.zeros_like(acc_ref)

        hb = h_ref[:, pl.ds(pl.multiple_of(t * tt, tt), tt)]

        @pl.when(s < nc)
        def _():
            acc_ref[...] += jnp.dot(hb, dG_ref[0], preferred_element_type=F32)

        @pl.when(s >= nc)
        def _():
            acc_ref[...] += jnp.dot(hb, dU_ref[0], preferred_element_type=F32)

        @pl.when(t == nt - 1)
        def _():
            @pl.when(s >= 2)
            def _():
                release(s - 2)

            stage_ref[s % 2] = acc_ref[...].astype(BF16)

            @pl.when(s != me)
            def _():
                push(s).start()

            @pl.when(s == me)
            def _():
                keep(s).start()

            @pl.when(s == ns - 1)
            def _():
                release(ns - 2)
                release(ns - 1)
                for k in range(1, N_DEV):
                    pltpu.make_async_remote_copy(
                        src_ref=stage_ref.at[0], dst_ref=recv_ref.at[me], send_sem=send_sems.at[0],
                        recv_sem=recv_sems.at[k - 1], device_id=_peer(k), device_id_type=MESH).wait_recv()

    return _hosted_call(
        hosted, body, grid=(ns, nt),
        in_specs=[pl.BlockSpec((D, T), lambda s, t: (0, 0)),
                  pl.BlockSpec((1, tt, fc), lambda s, t: (jnp.minimum(s, nc - 1), jnp.where(s < nc, t, nt - 1), 0)),
                  pl.BlockSpec((1, tt, fc), lambda s, t: (jnp.maximum(s - nc, 0), jnp.where(s >= nc, t, 0), 0))],
        out_specs=ANY, out_shape=jax.ShapeDtypeStruct((ns, D, fc), BF16),
        scratch_shapes=[pltpu.VMEM((D, fc), F32), pltpu.VMEM((2, D, fc), BF16), pltpu.SemaphoreType.DMA((2,)),
                        pltpu.SemaphoreType.DMA((N_DEV - 1,)), pltpu.SemaphoreType.DMA],
        compiler_params=_cp(("arbitrary", "arbitrary")), name=name)(h_t, dG, dU)


def ffn_dw_out(act, dz, name, hosted=None):
    nc, T, fc = act.shape
    D = dz.shape[1]
    half = fc // 2
    tt = _pick(T, 512, 16)
    nt = T // tt

    def body(a_ref, dz_ref, o_ref, acc_ref):
        t = pl.program_id(1)

        @pl.when(t == 0)
        def _():
            acc_ref[...] = jnp.zeros_like(acc_ref)

        acc_ref[...] += _bdot_tn(a_ref[0], dz_ref[...])

        @pl.when(t == nt - 1)
        def _():
            o_ref[...] = (0.5 * acc_ref[...]).reshape(2, half, D).astype(o_ref.dtype)

    return _hosted_call(
        hosted, body, grid=(nc, nt),
        in_specs=[pl.BlockSpec((1, tt, fc), lambda c, t: (c, t, 0)), pl.BlockSpec((tt, D), lambda c, t: (t, 0))],
        out_specs=pl.BlockSpec((2, half, D), lambda c, t: (c, 0, 0)),
        out_shape=jax.ShapeDtypeStruct((2 * nc, half, D), BF16),
        scratch_shapes=[pltpu.VMEM((fc, D), F32)],
        compiler_params=_cp(("parallel", "arbitrary")), name=name)(act, dz)


def proj_res_ln(parts, w, res, g, b, name):
    T, D = res.shape
    tm = _pick(T, 512, 8)
    widths = [p.shape[1] for p in parts]
    offs = [int(sum(widths[:i])) for i in range(len(parts))]
    n = len(parts)

    def body(*refs):
        p_refs = refs[:n]
        w_ref, r_ref, g_ref, b_ref, out_ref, xh_ref, rs_ref, ob_ref = refs[n:]
        acc = ALPHA * r_ref[...]
        for p_ref, o, wd in zip(p_refs, offs, widths):
            acc = acc + _bdot(p_ref[...], w_ref[o:o + wd, :])
        out, xh, rs = _ln_apply(acc, g_ref[...], b_ref[...])
        out_ref[...] = out
        ob_ref[...] = out.astype(BF16)
        xh_ref[...] = xh
        rs_ref[...] = rs

    row = pl.BlockSpec((tm, D), lambda i: (i, 0))
    vec = pl.BlockSpec((1, D), lambda i: (0, 0))
    return pl.pallas_call(
        body, grid=(T // tm,),
        in_specs=[pl.BlockSpec((tm, wd), lambda i: (i, 0)) for wd in widths]
        + [pl.BlockSpec(w.shape, lambda i: (0, 0)), row, vec, vec],
        out_specs=[row, row, pl.BlockSpec((tm, 1), lambda i: (i, 0)), row],
        out_shape=[jax.ShapeDtypeStruct((T, D), F32), jax.ShapeDtypeStruct((T, D), F32), jax.ShapeDtypeStruct((T, 1), F32),
                   jax.ShapeDtypeStruct((T, D), BF16)],
        compiler_params=_cp(("parallel",)), name=name)(*parts, w, res, g, b)


def ple_fwd(h, p, wg, wp, name):
    T, D = h.shape
    P = p.shape[1]
    tm, tn = _pick(T, 512, 8), _pick(D, 512, LANES)

    def body(h_ref, hn_ref, p_ref, wg_ref, wp_ref, out_ref, a_ref, e_ref):
        a = _bdot(h_ref[...], wg_ref[...])
        e = _bdot(p_ref[...], wp_ref[...])
        a_ref[...] = a
        e_ref[...] = e
        out_ref[...] = hn_ref[...] + _sigmoid(a) * e

    blk = pl.BlockSpec((tm, tn), lambda i, j: (i, j))
    sds = jax.ShapeDtypeStruct((T, D), F32)
    return pl.pallas_call(
        body, grid=(T // tm, D // tn),
        in_specs=[pl.BlockSpec((tm, D), lambda i, j: (i, 0)), blk, pl.BlockSpec((tm, P), lambda i, j: (i, 0)),
                  pl.BlockSpec((D, tn), lambda i, j: (0, j)), pl.BlockSpec((P, tn), lambda i, j: (0, j))],
        out_specs=[blk, blk, blk], out_shape=[sds, sds, sds],
        compiler_params=_cp(("parallel", "parallel")), name=name)(h, h, p, wg, wp)


def ple_bwd(dout, a, e, wg, name):
    T, D = dout.shape
    tm = _pick(T, 512, 16)

    def body(do_ref, a_ref, e_ref, wg_ref, dh_ref, da_ref, de_ref):
        do = do_ref[...]
        s = _sigmoid(a_ref[...])
        da = (do * e_ref[...] * s * (1.0 - s)).astype(BF16)
        da_ref[...] = da
        de_ref[...] = (do * s).astype(BF16)
        dh_ref[...] = do + _bdot_nt(da, wg_ref[...])

    row = pl.BlockSpec((tm, D), lambda i: (i, 0))
    return pl.pallas_call(
        body, grid=(T // tm,),
        in_specs=[row, row, row, pl.BlockSpec((D, D), lambda i: (0, 0))],
        out_specs=[row, row, row],
        out_shape=[jax.ShapeDtypeStruct((T, D), F32), jax.ShapeDtypeStruct((T, D), BF16), jax.ShapeDtypeStruct((T, D), BF16)],
        compiler_params=_cp(("parallel",)), name=name)(dout, a, e, wg)


def loss_head(y, target, name):
    T, D = y.shape
    tm = _pick(T, 512, 8)

    def body(y_ref, t_ref, loss_ref, dy_ref):
        i = pl.program_id(0)

        @pl.when(i == 0)
        def _():
            loss_ref[...] = jnp.zeros_like(loss_ref)

        err = y_ref[...] - t_ref[...]
        dy_ref[...] = err * (1.0 / D)
        per_tok = jnp.sum(err * err, axis=-1, keepdims=True) * (1.0 / D)
        loss_ref[...] += 0.5 * jnp.sum(per_tok, axis=0, keepdims=True)

    row = pl.BlockSpec((tm, D), lambda i: (i, 0))
    return pl.pallas_call(
        body, grid=(T // tm,), in_specs=[row, row],
        out_specs=[pl.BlockSpec((1, 1), lambda i: (0, 0)), row],
        out_shape=[jax.ShapeDtypeStruct((1, 1), F32), jax.ShapeDtypeStruct((T, D), F32)],
        compiler_params=_cp(("arbitrary",)), name=name)(y, target)


HALO = 8


def _conv_taps(pad_ref, w_ref, tm, base):
    acc = w_ref[0:1, :] * pad_ref[pl.ds(base, tm), :]
    for k in range(1, GDN_CONV):
        acc = acc + w_ref[k:k + 1, :] * pad_ref[pl.ds(base + k, tm), :]
    return acc


GDN_GROUP_W = GDN_W
GDN_PRE_ROWS = 512


def _head_segments():
    head = jnp.arange(GDN_W, dtype=jnp.int32) // GDN_D
    return (head[:, None] == head[None, :]).astype(BF16)


def _head_sums(x, seg):
    hi = x.astype(BF16)
    r1 = x - hi.astype(F32)
    mid = r1.astype(BF16)
    lo = (r1 - mid.astype(F32)).astype(BF16)
    d = functools.partial(jnp.dot, preferred_element_type=F32)
    return d(hi, seg) + d(mid, seg) + d(lo, seg)


def _gdn_pre_common(x_ref, halo_ref, w_ref, seg_ref, pad_ref, tm):
    i = pl.program_id(1)
    grp = pl.program_id(0)
    pad_ref[0:HALO, :] = jnp.where(i == 0, 0.0, halo_ref[...])
    pad_ref[HALO:HALO + tm, :] = x_ref[...]
    c = _conv_taps(pad_ref, w_ref, tm, HALO - (GDN_CONV - 1))
    s = _sigmoid(c)
    y = c * s
    r = lax.rsqrt(_head_sums(y * y, seg_ref[...]) + RMS_EPS)
    scale = jnp.where(grp < 1, GDN_D ** -0.5, 1.0)
    return grp < 2, c, s, y, r, scale


def gdn_pre_fwd(proj, conv_w, name):
    T = proj.shape[0]
    tm = _pick(T, GDN_PRE_ROWS, 8)
    GW = GDN_GROUP_W

    def body(x_ref, halo_ref, w_ref, seg_ref, o_ref, pad_ref):
        normed, c, s, y, r, scale = _gdn_pre_common(x_ref, halo_ref, w_ref, seg_ref, pad_ref, tm)
        o_ref[...] = jnp.where(normed, y * r * scale, y)

    return pl.pallas_call(
        body, grid=(3, T // tm),
        in_specs=[pl.BlockSpec((tm, GW), lambda hb, i: (i, hb)),
                  pl.BlockSpec((HALO, GW), lambda hb, i: (jnp.maximum(i * (tm // HALO) - 1, 0), hb)),
                  pl.BlockSpec((GDN_CONV, GW), lambda hb, i: (0, hb)), pl.BlockSpec((GW, GW), lambda hb, i: (0, 0))],
        out_specs=pl.BlockSpec((tm, GW), lambda hb, i: (i, hb)),
        out_shape=jax.ShapeDtypeStruct((T, 3 * GW), F32),
        scratch_shapes=[pltpu.VMEM((tm + HALO, GW), F32)],
        compiler_params=_cp(("parallel", "parallel")), name=name)(proj, proj, conv_w, _head_segments())


def gdn_pre_bwd_pointwise(proj, conv_w, dqkv, name, hosted=None):
    T = proj.shape[0]
    tm = _pick(T, GDN_PRE_ROWS, 8)
    GW = GDN_GROUP_W

    def body(x_ref, halo_ref, w_ref, seg_ref, d_ref, dc_ref, dw_ref, pad_ref):
        i = pl.program_id(1)
        normed, c, s, y, r, scale = _gdn_pre_common(x_ref, halo_ref, w_ref, seg_ref, pad_ref, tm)

        @pl.when(i == 0)
        def _():
            dw_ref[...] = jnp.zeros_like(dw_ref)

        d = d_ref[...]
        n = y * r
        dn = d * scale
        dy = jnp.where(normed, r * (dn - n * _head_sums(dn * n, seg_ref[...])), d)
        dc = dy * (s * (1.0 + c * (1.0 - s)))
        dc_ref[...] = dc
        for k in range(GDN_CONV):
            xs = pad_ref[pl.ds(HALO - (GDN_CONV - 1) + k, tm), :]
            dw_ref[k:k + 1, :] += jnp.sum(dc * xs, axis=0, keepdims=True)

    blk = pl.BlockSpec((tm, GW), lambda hb, i: (i, hb))
    wblk = pl.BlockSpec((GDN_CONV, GW), lambda hb, i: (0, hb))
    return _hosted_call(
        hosted, body, grid=(3, T // tm),
        in_specs=[blk, pl.BlockSpec((HALO, GW), lambda hb, i: (jnp.maximum(i * (tm // HALO) - 1, 0), hb)), wblk,
                  pl.BlockSpec((GW, GW), lambda hb, i: (0, 0)), blk],
        out_specs=[blk, wblk],
        out_shape=[jax.ShapeDtypeStruct((T, 3 * GW), F32), jax.ShapeDtypeStruct((GDN_CONV, 3 * GW), F32)],
        scratch_shapes=[pltpu.VMEM((tm + HALO, GW), F32)],
        compiler_params=_cp(("parallel", "arbitrary")), name=name)(proj, proj, conv_w, _head_segments(), dqkv)


def gdn_pre_bwd_conv(dc, conv_w_p, name):
    T = dc.shape[0]
    tm = _pick(T, GDN_PRE_ROWS, 8)
    nt = T // tm
    GW = GDN_GROUP_W

    def body(dc_ref, halo_ref, w_ref, dx_ref, pad_ref):
        i = pl.program_id(1)
        pad_ref[0:tm, :] = dc_ref[...]
        pad_ref[tm:tm + HALO, :] = jnp.where(i == nt - 1, 0.0, halo_ref[...])
        acc = w_ref[GDN_CONV - 1:GDN_CONV, :] * pad_ref[pl.ds(0, tm), :]
        for k in range(GDN_CONV - 1):
            acc = acc + w_ref[k:k + 1, :] * pad_ref[pl.ds(GDN_CONV - 1 - k, tm), :]
        dx_ref[...] = acc

    blk = pl.BlockSpec((tm, GW), lambda hb, i: (i, hb))
    return pl.pallas_call(
        body, grid=(3, nt),
        in_specs=[blk, pl.BlockSpec((HALO, GW), lambda hb, i: (jnp.minimum((i + 1) * (tm // HALO), T // HALO - 1), hb)),
                  pl.BlockSpec((GDN_CONV, GW), lambda hb, i: (0, hb))],
        out_specs=blk,
        out_shape=jax.ShapeDtypeStruct((T, 3 * GW), F32),
        scratch_shapes=[pltpu.VMEM((tm + HALO, GW), F32)],
        compiler_params=_cp(("parallel", "parallel")), name=name)(dc, dc, conv_w_p)


def _chunk_masks(C):
    row = _iota2((C, C), 0)
    col = _iota2((C, C), 1)
    return row >= col, row > col, row == col


GDN_FWD_CHUNKS = 4
BNN = (((2,), (1,)), ((0,), (0,)))
BNT = (((2,), (2,)), ((0,), (0,)))
BTN = (((1,), (1,)), ((0,), (0,)))


def _bmm(a, b, dims=BNN):
    return lax.dot_general(a.astype(BF16), b.astype(BF16), dims, preferred_element_type=F32)


def _hbmm(a, b):
    m = a.shape[1]
    a_hi, a_lo = _split2(a)
    b_hi, b_lo = _split2(b)
    r = lax.dot_general(jnp.concatenate([a_hi, a_lo], axis=1), b_hi, BNN, preferred_element_type=F32)
    return r[:, :m] + r[:, m:] + lax.dot_general(a_hi, b_lo, BNN, preferred_element_type=F32)


def _hbmm_tn(a, b):
    a_hi, a_lo = _split2(a)
    b_hi, b_lo = _split2(b)
    d = functools.partial(lax.dot_general, dimension_numbers=BTN, preferred_element_type=F32)
    return d(a_hi, b_hi) + d(a_lo, b_hi) + d(a_hi, b_lo)


def _col_to_row(colv, eye):
    return jnp.sum(jnp.where(eye, colv, 0.0), axis=1, keepdims=True)


def _row_to_col(rowv, eye):
    return jnp.sum(jnp.where(eye, rowv, 0.0), axis=2, keepdims=True)


def _unit_lower_inverse(A, eye):
    C = A.shape[1]
    P = jnp.where(eye, 1.0, 0.0) - A
    Bp = _hbmm(A, A)
    for _ in range(4):
        R = _hbmm(jnp.concatenate([Bp, P], axis=1), Bp)
        Bp = R[:, :C]
        P = P + R[:, C:]
    return P + _hbmm(P, Bp)


def _stack_heads(ref, first_head, n, row0=0):
    rows = pl.ds(row0, GDN_CHUNK)
    return jnp.stack([ref[rows, pl.ds((first_head + h) * GDN_D, GDN_D)] for h in range(n)])


def _unstack_heads(ref, first_head, val, row0=0):
    rows = pl.ds(row0, GDN_CHUNK)
    for h in range(val.shape[0]):
        ref[rows, pl.ds((first_head + h) * GDN_D, GDN_D)] = val[h]


def _gdn_gates(gab, a_row, dt_row, incl):
    g_all = -jnp.exp(a_row) * _softplus(gab + dt_row)
    beta_all = _sigmoid(gab)
    gc_all = _ones_dot_left(incl.astype(BF16), g_all)
    return g_all, beta_all, gc_all


def _gdn_common(qkv_ref, gc_all, beta_all, incl, strict, eye, row0=0):
    C, H = GDN_CHUNK, GDN_HEADS
    q, k, v = (_stack_heads(qkv_ref, j * H, H, row0) for j in range(3))
    gc = jnp.stack([gc_all[:, h:h + 1] for h in range(H)])
    beta = jnp.stack([beta_all[:, H + h:H + h + 1] for h in range(H)])
    gc_row = _col_to_row(gc, eye)
    decay = jnp.where(incl, jnp.exp(jnp.where(incl, gc - gc_row, 0.0)), 0.0)
    e_gc = jnp.exp(gc)
    gl = gc[:, C - 1:C, :]
    e_gl = jnp.exp(gl)
    ekd = jnp.exp(gl - gc)
    kb = k * beta
    A = jnp.where(strict, _bmm(kb, k, BNT) * decay, 0.0)
    Pm = jnp.where(incl, _bmm(q, k, BNT) * decay, 0.0)
    return q, k, v, gc, beta, decay, e_gc, e_gl, ekd, kb, A, Pm


def gdn_chunk_fwd(qkv, proj, a_row, dt_row, norm_w, name, hosted=None):
    T = qkv.shape[0]
    C, H, Dh = GDN_CHUNK, GDN_HEADS, GDN_D
    N = T // C
    J = GDN_FWD_CHUNKS if N % GDN_FWD_CHUNKS == 0 else 1

    def body(qkv_ref, gz_ref, gab_ref, a_ref, dt_ref, nw_ref, o_ref, opre_ref, Tm_ref, Sin_ref, S_ref):
        n = pl.program_id(0)

        @pl.when(n == 0)
        def _():
            S_ref[...] = jnp.zeros_like(S_ref)

        incl, strict, eye = _chunk_masks(C)
        gab = gab_ref[...]
        per_chunk = []
        for j in range(J):
            _, beta_all, gc_all = _gdn_gates(gab[j * C:(j + 1) * C], a_ref[...], dt_ref[...], incl)
            per_chunk.append(_gdn_common(qkv_ref, gc_all, beta_all, incl, strict, eye, row0=j * C))
        q, k, v, gc, beta, decay, e_gc, e_gl, ekd, kb, A, Pm = (jnp.concatenate(t, axis=0) for t in zip(*per_chunk))
        Tm = _unit_lower_inverse(A, eye)
        u = _hbmm(Tm, v * beta)
        w = _hbmm(Tm, kb * e_gc)
        qd = q * e_gc
        kd = k * ekd
        S = S_ref[...]
        for j in range(J):
            hs = slice(j * H, (j + 1) * H)
            v_new = u[hs] - _bmm(w[hs], S)
            o = _bmm(qd[hs], S) + _bmm(Pm[hs], v_new)
            Sin_ref[j] = S
            Tm_ref[j] = Tm[hs]
            S = S * e_gl[hs] + _bmm(kd[hs], v_new, BTN)
            r = lax.rsqrt(jnp.mean(o * o, axis=-1, keepdims=True) + RMS_EPS)
            gz = _stack_heads(gz_ref, 0, H, j * C)
            _unstack_heads(opre_ref, 0, o, j * C)
            _unstack_heads(o_ref, 0, o * r * nw_ref[...] * (gz * _sigmoid(gz)), j * C)
        S_ref[...] = S

    vec = pl.BlockSpec((1, LANES), lambda n: (0, 0))
    hblk = pl.BlockSpec((J * C, GDN_W), lambda n: (n, 0))
    sblk = pl.BlockSpec((J, H, Dh, Dh), lambda n: (n, 0, 0, 0))
    return _hosted_call(
        hosted, body, grid=(N // J,),
        in_specs=[pl.BlockSpec((J * C, 3 * GDN_W), lambda n: (n, 0)),
                  pl.BlockSpec((J * C, GDN_W), lambda n: (n, CB_GZ * LANES // GDN_W)),
                  pl.BlockSpec((J * C, LANES), lambda n: (n, CB_GAB)), vec, vec, pl.BlockSpec((1, Dh), lambda n: (0, 0))],
        out_specs=[hblk, hblk, sblk, sblk],
        out_shape=[jax.ShapeDtypeStruct((T, GDN_W), F32), jax.ShapeDtypeStruct((T, GDN_W), F32),
                   jax.ShapeDtypeStruct((N, H, Dh, Dh), F32), jax.ShapeDtypeStruct((N, H, Dh, Dh), F32)],
        scratch_shapes=[pltpu.VMEM((H, Dh, Dh), F32)],
        compiler_params=_cp(("arbitrary",)), name=name)(qkv, proj, proj, a_row, dt_row, norm_w)


def gdn_chunk_bwd(qkv, proj, a_row, dt_row, norm_w, opre, Tm_all, Sin_all, docat, name, hosted=None):
    T = qkv.shape[0]
    C, H, Dh = GDN_CHUNK, GDN_HEADS, GDN_D
    N = T // C

    def body(qkv_ref, gz_ref, gab_ref, a_ref, dt_ref, nw_ref, opre_ref, Tm_ref, Sin_ref, do_ref,
             dqkv_ref, dgz_ref, dgab_ref, da_ref, ddt_ref, dnw_ref, dS_ref):
        n = pl.program_id(0)

        @pl.when(n == 0)
        def _():
            dS_ref[...] = jnp.zeros_like(dS_ref)
            da_ref[...] = jnp.zeros_like(da_ref)
            ddt_ref[...] = jnp.zeros_like(ddt_ref)
            dnw_ref[...] = jnp.zeros_like(dnw_ref)

        incl, strict, eye = _chunk_masks(C)
        gab = gab_ref[...]
        g_all, beta_all, gc_all = _gdn_gates(gab, a_ref[...], dt_ref[...], incl)
        lane = _iota2((C, LANES), 1)
        rowi = _iota2((C, 1), 0)
        nw = nw_ref[...]
        q, k, v, gc, beta, decay, e_gc, e_gl, ekd, kb, A, Pm = _gdn_common(qkv_ref, gc_all, beta_all, incl, strict, eye)
        Tm = Tm_ref[0]
        S = Sin_ref[0]
        dS = dS_ref[...]
        kbe = kb * e_gc
        u = _hbmm(Tm, v * beta)
        w = _hbmm(Tm, kbe)
        qd = q * e_gc
        kd = k * ekd
        v_new = u - _bmm(w, S)
        o = _stack_heads(opre_ref, 0, H)
        gz = _stack_heads(gz_ref, 0, H)
        don = _stack_heads(do_ref, 0, H)
        r = lax.rsqrt(jnp.mean(o * o, axis=-1, keepdims=True) + RMS_EPS)
        nn = o * r
        sgz = _sigmoid(gz)
        silu = gz * sgz
        _unstack_heads(dgz_ref, 0, don * nn * nw * (sgz * (1.0 + gz * (1.0 - sgz))))
        dnn = don * nw * silu
        dnw_ref[...] += jnp.sum(jnp.sum(don * nn * silu, axis=0), axis=0, keepdims=True)
        do = r * (dnn - nn * jnp.mean(dnn * nn, axis=-1, keepdims=True))
        dv_new = _bmm(Pm, do, BTN) + _bmm(kd, dS)
        dPm = jnp.where(incl, _bmm(do, v_new, BNT), 0.0)
        dqd = _bmm(do, S, BNT)
        dkd = _bmm(v_new, dS, BNT)
        dS_ref[...] = _bmm(qd, do, BTN) + e_gl * dS - _bmm(w, dv_new, BTN)
        dgl = jnp.sum(jnp.sum(dS * S, axis=2, keepdims=True), axis=1, keepdims=True) * e_gl
        dw = -_bmm(dv_new, S, BNT)
        dvb = _hbmm_tn(Tm, dv_new)
        dkbe = _hbmm_tn(Tm, dw)
        dA = -jnp.where(strict, _bmm(dvb, u, BNT) + _bmm(dkbe, w, BNT), 0.0)
        dAD = dA * decay
        dPD = dPm * decay
        Gm = dA * A + dPm * Pm
        dgc = jnp.sum(Gm, axis=2, keepdims=True) - _row_to_col(jnp.sum(Gm, axis=1, keepdims=True), eye)
        dkb = _bmm(dAD, k) + dkbe * e_gc
        dk = _bmm(dAD, kb, BTN) + _bmm(dPD, q, BTN) + dkd * ekd + dkb * beta
        dq = _bmm(dPD, k) + dqd * e_gc
        tkd = jnp.sum(dkd * kd, axis=-1, keepdims=True)
        dgc = dgc + jnp.sum(dqd * qd, axis=-1, keepdims=True) - tkd + jnp.sum(dkbe * kbe, axis=-1, keepdims=True)
        dgl = dgl + jnp.sum(tkd, axis=1, keepdims=True)
        dgc = dgc + jnp.where(rowi == C - 1, dgl, 0.0)
        dbeta = jnp.sum(dvb * v, axis=-1, keepdims=True) + jnp.sum(dkb * k, axis=-1, keepdims=True)
        _unstack_heads(dqkv_ref, 0, dq)
        _unstack_heads(dqkv_ref, H, dk)
        _unstack_heads(dqkv_ref, 2 * H, dvb * beta)
        dgc_all = jnp.zeros((C, LANES), F32)
        dbeta_all = jnp.zeros((C, LANES), F32)
        for h in range(H):
            dgc_all = dgc_all + jnp.where(lane == h, dgc[h], 0.0)
            dbeta_all = dbeta_all + jnp.where(lane == H + h, dbeta[h], 0.0)
        upper = (_iota2((C, C), 0) <= _iota2((C, C), 1)).astype(BF16)
        dg_all = _ones_dot_left(upper, dgc_all)
        dga = dg_all * (-jnp.exp(a_ref[...])) * _sigmoid(gab + dt_ref[...])
        dgb = dbeta_all * beta_all * (1.0 - beta_all)
        dgab_ref[...] = jnp.where(lane < H, dga, jnp.where(lane < 2 * H, dgb, 0.0))
        da_ref[...] += jnp.sum(jnp.where(lane < H, dg_all * g_all, 0.0), axis=0, keepdims=True)
        ddt_ref[...] += jnp.sum(jnp.where(lane < H, dga, 0.0), axis=0, keepdims=True)

    rev = lambda n: N - 1 - n
    vec = pl.BlockSpec((1, LANES), lambda n: (0, 0))
    nwv = pl.BlockSpec((1, Dh), lambda n: (0, 0))
    hblk = pl.BlockSpec((C, GDN_W), lambda n: (rev(n), 0))
    sblk = pl.BlockSpec((1, H, Dh, Dh), lambda n: (rev(n), 0, 0, 0))
    qblk = pl.BlockSpec((C, 3 * GDN_W), lambda n: (rev(n), 0))
    return _hosted_call(
        hosted, body, grid=(N,),
        in_specs=[qblk, pl.BlockSpec((C, GDN_W), lambda n: (rev(n), CB_GZ * LANES // GDN_W)),
                  pl.BlockSpec((C, LANES), lambda n: (rev(n), CB_GAB)), vec, vec, nwv, hblk, sblk, sblk, hblk],
        out_specs=[qblk, hblk, pl.BlockSpec((C, LANES), lambda n: (rev(n), 0)), vec, vec, nwv],
        out_shape=[jax.ShapeDtypeStruct((T, 3 * GDN_W), F32), jax.ShapeDtypeStruct((T, GDN_W), F32),
                   jax.ShapeDtypeStruct((T, LANES), F32), jax.ShapeDtypeStruct((1, LANES), F32),
                   jax.ShapeDtypeStruct((1, LANES), F32), jax.ShapeDtypeStruct((1, Dh), F32)],
        scratch_shapes=[pltpu.VMEM((H, Dh, Dh), F32)],
        compiler_params=_cp(("arbitrary",)), name=name)(qkv, proj, proj, a_row, dt_row, norm_w, opre, Tm_all, Sin_all, docat)


ATT_BQ, ATT_BK = 512, 512
NEG_BIG = -1e30


def _att_blocks(T):
    bq, bk = min(ATT_BQ, T), min(ATT_BK, T)
    assert bk % bq == 0 and T % bk == 0
    return bq, bk


def _att_specs(T, bq, cbs):
    qspec = lambda cb: pl.BlockSpec((bq, LANES), lambda h, i: (i, cb + h))
    kspec = lambda cb: pl.BlockSpec((T, LANES), lambda h, i: (0, cb + h))
    return qspec, kspec


def _kblock(ref, kb, bk):
    return ref[pl.ds(pl.multiple_of(kb * bk, bk), bk), :]


def _att_pos(i, kb, bq, bk):
    qpos = i * bq + _iota2((bq, bk), 0)
    kpos = kb * bk + _iota2((bq, bk), 1)
    return qpos, kpos


def _later_keys(n):
    return (_iota2((n, n), 0) > _iota2((n, n), 1)).astype(BF16)


def _earlier_keys(n):
    return (_iota2((n, n), 0) < _iota2((n, n), 1)).astype(BF16)


def _tri_dot(x, tri, terms):
    acc, rest = None, x
    for t in range(terms):
        part = rest.astype(BF16)
        if t + 1 < terms:
            rest = rest - part.astype(F32)
        d = jnp.dot(part, tri, preferred_element_type=F32)
        acc = d if acc is None else acc + d
    return acc


SB_BLOCK = 256
SB_DEAD = -104.0


def _sb_blocks(T):
    b = min(SB_BLOCK, T)
    assert T % b == 0 and T // b <= LANES
    return b, b


def sb_fwd(proj, name, hosted=None):
    T = proj.shape[0]
    H = SB_HEADS
    bq, bk = _sb_blocks(T)
    scale = SB_DIM ** -0.5

    def body(q_ref, k_ref, v_ref, o_ref, tot_ref):
        i = pl.program_id(1)
        qb = q_ref[...].astype(BF16)
        diag = (i * bq) // bk
        lane = _iota2((bq, LANES), 1)
        later = _later_keys(bk)

        def block(kb, acc, R, masked):
            z = _bdot_nt(qb, _kblock(k_ref, kb, bk)) * scale
            sp = _softplus(z)
            if masked:
                qpos, kpos = _att_pos(i, kb, bq, bk)
                mask = kpos < qpos
                l1m = jnp.where(mask, -sp, 0.0)
            else:
                l1m = -sp
            W = jnp.exp((z - sp) + _tri_dot(l1m, later, 3) + R)
            if masked:
                W = jnp.where(mask, W, 0.0)
            acc = acc + _bdot(W, _kblock(v_ref, kb, bk))
            return acc, R + jnp.sum(l1m, axis=-1, keepdims=True)

        acc, R = block(diag, jnp.zeros((bq, LANES), F32), jnp.zeros((bq, 1), F32), True)

        def live(c):
            return jnp.logical_and(c[0] >= 0, jnp.max(c[2]) > SB_DEAD)

        def step(c):
            kb, acc, R, Rb = c
            acc, R_next = block(kb, acc, R, False)
            return kb - 1, acc, R_next, jnp.where(lane == kb, R, Rb)

        _, acc, _, Rb = lax.while_loop(live, step, (diag - 1, acc, R, jnp.where(lane == diag, 0.0, NEG_BIG)))
        o_ref[...] = acc
        tot_ref[...] = Rb

    qspec, kspec = _att_specs(T, bq, None)
    sds = jax.ShapeDtypeStruct((T, H * LANES), F32)
    oblk = pl.BlockSpec((bq, LANES), lambda h, i: (i, h))
    return _hosted_call(
        hosted, body, grid=(H, T // bq), in_specs=[qspec(CB_SQ), kspec(CB_SK), kspec(CB_SV)],
        out_specs=[oblk, oblk], out_shape=[sds, sds],
        compiler_params=_cp(("parallel", "parallel")), name=name)(proj, proj, proj)


def sb_bwd(proj, tot, docat, do_cb, name):
    T = proj.shape[0]
    H = SB_HEADS
    bq, bk = _sb_blocks(T)
    scale = SB_DIM ** -0.5

    def body(q_ref, k_ref, v_ref, tot_ref, do_ref, dq_ref, dk_ref, dv_ref):
        i = pl.program_id(1)

        @pl.when(i == 0)
        def _():
            dk_ref[...] = jnp.zeros_like(dk_ref)
            dv_ref[...] = jnp.zeros_like(dv_ref)

        qb = q_ref[...].astype(BF16)
        dob = do_ref[...].astype(BF16)
        Rb = tot_ref[...]
        diag = (i * bq) // bk
        lane = _iota2((bq, LANES), 1)
        later, earlier = _later_keys(bk), _earlier_keys(bk)
        first = lax.while_loop(
            lambda kb: jnp.logical_and(kb < diag, jnp.max(jnp.where(lane == kb, Rb, NEG_BIG)) <= SB_DEAD),
            lambda kb: kb + 1, jnp.int32(0))

        def block(kb, carry, masked):
            dq, Epre = carry
            R = jnp.sum(jnp.where(lane == kb, Rb, 0.0), axis=1, keepdims=True)
            kblk = _kblock(k_ref, kb, bk).astype(BF16)
            z = _bdot_nt(qb, kblk) * scale
            sp = _softplus(z)
            if masked:
                qpos, kpos = _att_pos(i, kb, bq, bk)
                mask = kpos < qpos
                l1m = jnp.where(mask, -sp, 0.0)
            else:
                l1m = -sp
            W = jnp.exp((z - sp) + _tri_dot(l1m, later, 3) + R)
            if masked:
                W = jnp.where(mask, W, 0.0)
            E = _bdot_nt(dob, _kblock(v_ref, kb, bk)) * W
            cexcl = _tri_dot(E, earlier, 3) + Epre
            neg = jnp.exp(-sp)
            dz = E * neg - cexcl * (1.0 - neg)
            if masked:
                dz = jnp.where(mask, dz, 0.0)
            dz = (dz * scale).astype(BF16)
            rows = pl.ds(pl.multiple_of(kb * bk, bk), bk)
            dk_ref[rows, :] += lax.dot_general(dz, qb, TN_DIMS, preferred_element_type=F32)
            dv_ref[rows, :] += lax.dot_general(W.astype(BF16), dob, TN_DIMS, preferred_element_type=F32)
            dq = dq + jnp.dot(dz, kblk, preferred_element_type=F32)
            return dq, Epre + jnp.sum(E, axis=-1, keepdims=True)

        init = (jnp.zeros((bq, LANES), F32), jnp.zeros((bq, 1), F32))
        carry = lax.fori_loop(first, diag, lambda kb, c: block(kb, c, False), init)
        dq, _ = block(diag, carry, True)
        dq_ref[...] = dq

    qspec, kspec = _att_specs(T, bq, None)
    sds = jax.ShapeDtypeStruct((T, H * LANES), F32)
    oblk = pl.BlockSpec((bq, LANES), lambda h, i: (i, h))
    kout = pl.BlockSpec((T, LANES), lambda h, i: (0, h))
    return pl.pallas_call(
        body, grid=(H, T // bq),
        in_specs=[qspec(CB_SQ), kspec(CB_SK), kspec(CB_SV), oblk, qspec(do_cb)],
        out_specs=[oblk, kout, kout], out_shape=[sds, sds, sds],
        compiler_params=_cp(("arbitrary", "arbitrary")), name=name)(proj, proj, proj, tot, docat)


def mla_fwd(Q, K, V, name, hosted=None):
    T = Q.shape[0]
    H = MLA_HEADS
    bq, bk = _att_blocks(T)
    scale = (MLA_NOPE + MLA_ROPE) ** -0.5

    def body(q_ref, k_ref, v_ref, o_ref, lse_ref):
        i = pl.program_id(1)
        qb = q_ref[...]
        diag = (i * bq) // bk

        def block(kb, carry, masked):
            acc, m, l = carry
            s = _bdot_nt(qb, _kblock(k_ref, kb, bk)) * scale
            if masked:
                qpos, kpos = _att_pos(i, kb, bq, bk)
                s = jnp.where(kpos <= qpos, s, NEG_BIG)
            m_new = jnp.maximum(m, jnp.max(s, axis=-1, keepdims=True))
            p = jnp.exp(s - m_new)
            corr = jnp.exp(m - m_new)
            acc = corr * acc + _bdot(p, _kblock(v_ref, kb, bk))
            return acc, m_new, corr * l + jnp.sum(p, axis=-1, keepdims=True)

        init = (jnp.zeros((bq, LANES), F32), jnp.full((bq, 1), NEG_BIG, F32), jnp.zeros((bq, 1), F32))
        carry = lax.fori_loop(0, diag, lambda kb, c: block(kb, c, False), init)
        acc, m, l = block(diag, carry, True)
        o_ref[...] = acc / l
        lse_ref[...] = jnp.broadcast_to(m + jnp.log(l), (bq, LANES))

    qspec, kspec = _att_specs(T, bq, None)
    sds = jax.ShapeDtypeStruct((T, H * LANES), F32)
    oblk = pl.BlockSpec((bq, LANES), lambda h, i: (i, h))
    return _hosted_call(
        hosted, body, grid=(H, T // bq), in_specs=[qspec(0), kspec(0), kspec(0)],
        out_specs=[oblk, oblk], out_shape=[sds, sds],
        compiler_params=_cp(("parallel", "parallel")), name=name)(Q, K, V)


def mla_bwd(Q, K, V, o, lse, docat, do_cb, name, hosted=None):
    T = Q.shape[0]
    H = MLA_HEADS
    bq, bk = _att_blocks(T)
    scale = (MLA_NOPE + MLA_ROPE) ** -0.5

    def body(q_ref, k_ref, v_ref, o_ref, lse_ref, do_ref, dq_ref, dk_ref, dv_ref):
        i = pl.program_id(1)

        @pl.when(i == 0)
        def _():
            dk_ref[...] = jnp.zeros_like(dk_ref)
            dv_ref[...] = jnp.zeros_like(dv_ref)

        qb = q_ref[...]
        do = do_ref[...]
        dob = do.astype(BF16)
        delta = jnp.sum(do * o_ref[...], axis=-1, keepdims=True)
        lse = lse_ref[:, 0:1]

        diag = (i * bq) // bk

        def block(kb, dq, masked):
            kblk = _kblock(k_ref, kb, bk)
            s = _bdot_nt(qb, kblk) * scale
            if masked:
                qpos, kpos = _att_pos(i, kb, bq, bk)
                s = jnp.where(kpos <= qpos, s, NEG_BIG)
            p = jnp.exp(s - lse)
            dp = _bdot_nt(dob, _kblock(v_ref, kb, bk))
            ds = (p * (dp - delta) * scale).astype(BF16)
            rows = pl.ds(pl.multiple_of(kb * bk, bk), bk)
            dk_ref[rows, :] += lax.dot_general(ds, qb, TN_DIMS, preferred_element_type=F32)
            dv_ref[rows, :] += lax.dot_general(p.astype(BF16), dob, TN_DIMS, preferred_element_type=F32)
            return dq + jnp.dot(ds, kblk, preferred_element_type=F32)

        dq = lax.fori_loop(0, diag, lambda kb, c: block(kb, c, False), jnp.zeros((bq, LANES), F32))
        dq_ref[...] = block(diag, dq, True)

    qspec, kspec = _att_specs(T, bq, None)
    sds = jax.ShapeDtypeStruct((T, H * LANES), F32)
    oblk = pl.BlockSpec((bq, LANES), lambda h, i: (i, h))
    kout = pl.BlockSpec((T, LANES), lambda h, i: (0, h))
    return _hosted_call(
        hosted, body, grid=(H, T // bq),
        in_specs=[qspec(0), kspec(0), kspec(0), oblk, oblk, qspec(do_cb)],
        out_specs=[oblk, kout, kout], out_shape=[sds, sds, sds],
        compiler_params=_cp(("arbitrary", "arbitrary")), name=name)(Q, K, V, o, lse, docat)


def _tile_heads(t, n):
    return jnp.concatenate([t] * n, axis=1)


def _rope(X, C, Sn, Sp):
    n = X.shape[1]
    return X * C + pltpu.roll(X, n - HALF_ROPE, 1) * Sn + pltpu.roll(X, HALF_ROPE, 1) * Sp


def _rope_t(dO, C, Sn, Sp):
    n = dO.shape[1]
    return dO * C + pltpu.roll(dO * Sn, HALF_ROPE, 1) + pltpu.roll(dO * Sp, n - HALF_ROPE, 1)


def _rms(x, w):
    r = lax.rsqrt(jnp.mean(x * x, axis=-1, keepdims=True) + RMS_EPS)
    xh = x * r
    return r, xh, xh * w


def _rms_bwd(dn, w, r, xh):
    dxh = dn * w
    return r * (dxh - xh * jnp.mean(dxh * xh, axis=-1, keepdims=True)), jnp.sum(dn * xh, axis=0, keepdims=True)


def _mla_pre_specs(T, tm):
    KV = MLA_KV_RANK
    QR = MLA_Q_RANK
    W = MLA_HEADS * LANES
    full = lambda shape: pl.BlockSpec(shape, lambda i: (0, 0))
    specs = [pl.BlockSpec((tm, QR), lambda i: (i, CB_MQ * LANES // QR)),
             pl.BlockSpec((tm, 2 * LANES), lambda i: (i, CB_MKV // 2)),
             full((1, QR)), full((1, KV))]
    rope = [pl.BlockSpec((tm, LANES), lambda i: (i, 0))] * 3
    return specs, rope, full, W


def mla_pre_fwd(proj, wq, wkv, wuq, wuk, wuv, ropeC, ropeSn, ropeSp, name):
    T = proj.shape[0]
    tm = _pick(T, 512, 16)
    KV = MLA_KV_RANK
    H = MLA_HEADS

    def body(mq_ref, mkv_ref, wq_ref, wkv_ref, wuq_ref, wuk_ref, wuv_ref, c_ref, sn_ref, sp_ref, Q_ref, K_ref, V_ref):
        C, Sn, Sp = (_tile_heads(t[...], H) for t in (c_ref, sn_ref, sp_ref))
        _, _, qn = _rms(mq_ref[...], wq_ref[...])
        Q_ref[...] = _rope(_bdot(qn, wuq_ref[...]), C, Sn, Sp).astype(BF16)
        mkv = mkv_ref[...]
        _, _, kvn = _rms(mkv[:, :KV], wkv_ref[...])
        kr = pltpu.roll(mkv[:, KV:], MLA_NOPE, 1)
        K_ref[...] = _rope(_bdot(kvn, wuk_ref[...]) + _tile_heads(kr, H), C, Sn, Sp).astype(BF16)
        V_ref[...] = _bdot(kvn, wuv_ref[...]).astype(BF16)

    specs, rope, full, W = _mla_pre_specs(T, tm)
    oblk = pl.BlockSpec((tm, W), lambda i: (i, 0))
    sds = jax.ShapeDtypeStruct((T, W), BF16)
    return pl.pallas_call(
        body, grid=(T // tm,),
        in_specs=specs + [full(wuq.shape), full(wuk.shape), full(wuv.shape)] + rope,
        out_specs=[oblk, oblk, oblk], out_shape=[sds, sds, sds],
        compiler_params=_cp(("parallel",)), name=name)(proj, proj, wq, wkv, wuq, wuk, wuv, ropeC, ropeSn, ropeSp)


def mla_pre_bwd(proj, wq, wkv, wuq, wuk, wuv, ropeC, ropeSn, ropeSp, dQ, dK, dV, name):
    T = proj.shape[0]
    tm = _pick(T, 512, 16)
    KV = MLA_KV_RANK
    H = MLA_HEADS

    def body(mq_ref, mkv_ref, wq_ref, wkv_ref, wuq_ref, wuk_ref, wuv_ref,
             c_ref, sn_ref, sp_ref, dQ_ref, dK_ref, dV_ref,
             dmq_ref, dmkv_ref, dwuq_ref, dwuk_ref, dwuv_ref, dwq_ref, dwkv_ref):
        i = pl.program_id(0)

        @pl.when(i == 0)
        def _():
            for ref in (dwuq_ref, dwuk_ref, dwuv_ref, dwq_ref, dwkv_ref):
                ref[...] = jnp.zeros_like(ref)

        C, Sn, Sp = (_tile_heads(t[...], H) for t in (c_ref, sn_ref, sp_ref))
        rq, xq, qn = _rms(mq_ref[...], wq_ref[...])
        mkv = mkv_ref[...]
        rkv, xkv, kvn = _rms(mkv[:, :KV], wkv_ref[...])
        dqf = _rope_t(dQ_ref[...], C, Sn, Sp)
        dkf = _rope_t(dK_ref[...], C, Sn, Sp)
        dv = dV_ref[...]
        dwuq_ref[...] += _bdot_tn(qn, dqf)
        dwuk_ref[...] += _bdot_tn(kvn, dkf)
        dwuv_ref[...] += _bdot_tn(kvn, dv)
        dmq, dwq = _rms_bwd(_bdot_nt(dqf, wuq_ref[...]), wq_ref[...], rq, xq)
        dckv, dwkv = _rms_bwd(_bdot_nt(dkf, wuk_ref[...]) + _bdot_nt(dv, wuv_ref[...]), wkv_ref[...], rkv, xkv)
        dwq_ref[...] += dwq
        dwkv_ref[...] += dwkv
        dmq_ref[...] = dmq
        dkr = dkf[:, 0:LANES]
        for h in range(1, H):
            dkr = dkr + dkf[:, h * LANES:(h + 1) * LANES]
        dkr = pltpu.roll(dkr, LANES - MLA_NOPE, 1)
        dkr = jnp.where(_iota2(dkr.shape, 1) < MLA_ROPE, dkr, 0.0)
        dmkv_ref[...] = jnp.concatenate([dckv, dkr], axis=1)

    specs, rope, full, W = _mla_pre_specs(T, tm)
    wide = pl.BlockSpec((tm, W), lambda i: (i, 0))
    return pl.pallas_call(
        body, grid=(T // tm,),
        in_specs=specs + [full(w.shape) for w in (wuq, wuk, wuv)] + rope + [wide, wide, wide],
        out_specs=[pl.BlockSpec((tm, MLA_Q_RANK), lambda i: (i, 0)), pl.BlockSpec((tm, 2 * LANES), lambda i: (i, 0)),
                   full(wuq.shape), full(wuk.shape), full(wuv.shape), full((1, MLA_Q_RANK)), full((1, KV))],
        out_shape=[jax.ShapeDtypeStruct((T, MLA_Q_RANK), F32), jax.ShapeDtypeStruct((T, 2 * LANES), F32),
                   jax.ShapeDtypeStruct(wuq.shape, F32), jax.ShapeDtypeStruct(wuk.shape, F32),
                   jax.ShapeDtypeStruct(wuv.shape, F32), jax.ShapeDtypeStruct((1, MLA_Q_RANK), F32),
                   jax.ShapeDtypeStruct((1, KV), F32)],
        compiler_params=_cp(("arbitrary",)), name=name)(
            proj, proj, wq, wkv, wuq, wuk, wuv, ropeC, ropeSn, ropeSp, dQ, dK, dV)


def all_gather(shards, name):
    n = len(shards)

    def body(*refs):
        x_refs, out_refs = refs[:n], refs[n:2 * n]
        send_sems, recv_sems, local_sems = refs[2 * n:]
        x, y, c = _place()
        me, sibling = (x, y, c), (x, y, 1 - c)
        chips = [(1 - x, y), (x, 1 - y), (1 - x, 1 - y)]

        def slot(a, px, py, pc):
            return out_refs[a].at[4 * px + 2 * py + pc]

        def copy(a, k, block, to, src=None):
            return pltpu.make_async_remote_copy(
                src_ref=slot(a, *block) if src is None else src, dst_ref=slot(a, *block),
                send_sem=send_sems.at[a, k], recv_sem=recv_sems.at[a, k], device_id=to, device_id_type=MESH)

        mine = [pltpu.make_async_copy(x_refs[a], slot(a, *me), local_sems.at[a]) for a in range(n)]
        first = []
        for a in range(n):
            mine[a].start()
            first.append(copy(a, 0, me, sibling, src=x_refs[a]))
            first += [copy(a, 1 + j, me, (*chip, c), src=x_refs[a]) for j, chip in enumerate(chips)]
        for cp in first:
            cp.start()
        passed = []
        for j, chip in enumerate(chips):
            for a in range(n):
                copy(a, 1 + j, (*chip, c), me).wait_recv()
                passed.append(copy(a, 4 + j, (*chip, c), sibling))
                passed[-1].start()
        for a in range(n):
            copy(a, 0, sibling, me).wait_recv()
            for j, chip in enumerate(chips):
                copy(a, 4 + j, (*chip, 1 - c), me).wait_recv()
        for cp in first + passed:
            cp.wait_send()
        for cp in mine:
            cp.wait()

    return pl.pallas_call(
        body, out_shape=[jax.ShapeDtypeStruct((N_DEV,) + s.shape, s.dtype) for s in shards],
        in_specs=[ANY] * n, out_specs=[ANY] * n,
        scratch_shapes=[pltpu.SemaphoreType.DMA((n, 7)), pltpu.SemaphoreType.DMA((n, 7)), pltpu.SemaphoreType.DMA((n,))],
        name=name)(*shards)


def reduce_adamw(parts, w, m, v, name):
    L = len(parts)
    n, Rl, C = parts[0].shape
    R = w.shape[0]
    assert R == L * Rl
    tr = Rl if Rl * C <= 256 * 1024 else _pick(Rl, 256, 16)
    nr = Rl // tr

    def body(*refs):
        p_refs = refs[:L]
        w_ref, m_ref, v_ref, g_ref, d_ref, nm_ref, nv_ref, sum_ref = refs[L:]
        grp = pl.program_id(0)
        for j in range(L):
            @pl.when(grp == j)
            def _(j=j):
                acc = p_refs[j][0].astype(F32)
                for s in range(1, n):
                    acc = acc + p_refs[j][s].astype(F32)
                sum_ref[...] = acc

        g_ = sum_ref[...]
        m_ = ADAM_B1 * m_ref[...] + (1.0 - ADAM_B1) * g_
        v_ = ADAM_B2 * v_ref[...] + (1.0 - ADAM_B2) * (g_ * g_)
        m_hat = m_ / (1.0 - ADAM_B1 ** ADAM_STEP)
        v_hat = v_ / (1.0 - ADAM_B2 ** ADAM_STEP)
        g_ref[...] = g_
        d_ref[...] = -ADAM_LR * (m_hat / (jnp.sqrt(v_hat) + ADAM_EPS) + ADAM_WD * w_ref[...])
        nm_ref[...] = m_
        nv_ref[...] = v_

    blk = pl.BlockSpec((tr, C), lambda l, r: (l * nr + r, 0))
    sds = jax.ShapeDtypeStruct((R, C), F32)
    p_specs = [pl.BlockSpec((n, tr, C), lambda l, r, j=j: (0, jnp.where(l == j, r, 0), 0)) for j in range(L)]
    return pl.pallas_call(
        body, grid=(L, nr), in_specs=p_specs + [blk] * 3,
        out_specs=[blk] * 4, out_shape=[sds] * 4, scratch_shapes=[pltpu.VMEM((tr, C), F32)],
        compiler_params=_cp(("arbitrary", "arbitrary")), name=name)(*parts, w, m, v)


SHARDED = {"ffa_w_in": (2, BF16), "ffa_w_out": (1, BF16), "mix_w_in": (2, BF16), "mla_w_uq": (2, BF16),
           "mla_w_ukv": (2, BF16), "mix_w_o": (1, BF16), "ffb_w_in": (2, BF16), "ffb_w_out": (1, BF16),
           "ple_w_gate": (1, BF16), "ple_w_proj": (2, BF16), "gdn_conv_w": (2, F32), "ln_g": (2, F32), "ln_b": (2, F32)}
FFN_SLOT = ("ffa_w_in", "ffa_w_out", "ffb_w_in", "ffb_w_out")
REPLICATED = ("gdn_a_log", "gdn_dt_bias", "gdn_norm_w", "mla_q_norm_w", "mla_kv_norm_w")
WEIGHTS = ("ffa_w_in", "ffa_w_out", "mix_w_in", "gdn_conv_w", "gdn_a_log", "gdn_dt_bias", "gdn_norm_w", "mla_q_norm_w",
           "mla_kv_norm_w", "mla_w_uq", "mla_w_ukv", "mix_w_o", "ffb_w_in", "ffb_w_out", "ln_g", "ln_b", "ple_w_gate",
           "ple_w_proj")


def _to_slots(full, axis):
    L, a, b = full.shape
    if axis == 2:
        return full.reshape(L, a, N_DEV, b // N_DEV).transpose(2, 0, 1, 3).reshape(N_DEV, L * a, b // N_DEV)
    return full.reshape(L, N_DEV, a // N_DEV, b).transpose(1, 0, 2, 3).reshape(N_DEV, L * a // N_DEV, b)


def _from_slots(slots, shard_shape, axis):
    L, a, b = shard_shape
    t = slots.reshape((N_DEV,) + tuple(shard_shape))
    if axis == 2:
        return t.transpose(1, 2, 0, 3).reshape(L, a, N_DEV * b)
    return t.transpose(1, 0, 2, 3).reshape(L, N_DEV * a, b)


def _view2d(t):
    return t.reshape(-1, t.shape[-1])


def _pad_heads(w, nh):
    K = w.shape[0]
    return jnp.pad(w.reshape(K, nh, GDN_D), ((0, 0), (0, 0), (0, LANES - GDN_D))).reshape(K, nh * LANES)


def _unpad_heads(w, nh):
    K = w.shape[0]
    return w.reshape(K, nh, LANES)[:, :, :GDN_D].reshape(K, nh * GDN_D)


IN_WIDTHS = (512, 512, 512, 512, 8, 8, 256, 256, 256, 256, 160)


def _split_in(w):
    offs = np.cumsum((0,) + IN_WIDTHS)
    return [w[:, int(offs[i]):int(offs[i + 1])] for i in range(len(IN_WIDTHS))]


def _pad_in_proj(w):
    gq, gk, gv, gz, ga, gb, sq, sk, sv, mq, mkv = _split_in(w)
    gab = jnp.pad(jnp.concatenate([ga, gb], axis=1), ((0, 0), (0, LANES - 2 * GDN_HEADS)))
    return jnp.concatenate(
        [gq, gk, gv, gz] + [_pad_heads(t, SB_HEADS) for t in (sq, sk, sv)]
        + [mq, jnp.pad(mkv, ((0, 0), (0, 2 * LANES - mkv.shape[1]))), gab], axis=1)


def _unpad_in_proj(wp):
    c = lambda cb, n: wp[:, cb * LANES:(cb + n) * LANES]
    gab = c(CB_GAB, 1)
    parts = [c(cb, DO_SB) for cb in (CB_GQ, CB_GK, CB_GV, CB_GZ)]
    parts += [gab[:, :GDN_HEADS], gab[:, GDN_HEADS:2 * GDN_HEADS]]
    parts += [_unpad_heads(c(cb, SB_HEADS), SB_HEADS) for cb in (CB_SQ, CB_SK, CB_SV)]
    parts += [c(CB_MQ, 2), c(CB_MKV, 2)[:, :MLA_KV_RANK + MLA_ROPE]]
    return jnp.concatenate(parts, axis=1)


def _pad_lanes(w, width):
    return jnp.pad(w, ((0, 0), (0, width - w.shape[1])))


def _mla_up_pad(w_uq, w_ukv):
    H = MLA_HEADS
    dq = MLA_NOPE + MLA_ROPE
    wuq = jnp.pad(w_uq.reshape(-1, H, dq), ((0, 0), (0, 0), (0, LANES - dq))).reshape(-1, H * LANES)
    kv = w_ukv.reshape(-1, H, MLA_NOPE + MLA_V)
    wuk = jnp.pad(kv[:, :, :MLA_NOPE], ((0, 0), (0, 0), (0, LANES - MLA_NOPE))).reshape(-1, H * LANES)
    wuv = jnp.pad(kv[:, :, MLA_NOPE:], ((0, 0), (0, 0), (0, LANES - MLA_V))).reshape(-1, H * LANES)
    return wuq, wuk, wuv


def _mla_up_unpad(dwuq, dwuk, dwuv):
    H = MLA_HEADS
    dq = MLA_NOPE + MLA_ROPE
    g_uq = dwuq.reshape(-1, H, LANES)[:, :, :dq].reshape(-1, H * dq)
    g_ukv = jnp.concatenate([dwuk.reshape(-1, H, LANES)[:, :, :MLA_NOPE], dwuv.reshape(-1, H, LANES)[:, :, :MLA_V]],
                            axis=2).reshape(-1, H * (MLA_NOPE + MLA_V))
    return g_uq, g_ukv


def _rope_tables(positions):
    inv = 1.0 / (ROPE_BASE ** (jnp.arange(0, MLA_ROPE, 2, dtype=F32) / MLA_ROPE))
    ang = positions.astype(F32)[:, None] * inv
    cos, sin = jnp.cos(ang), jnp.sin(ang)
    T = positions.shape[0]
    one = lambda n: jnp.ones((T, n), F32)
    zero = lambda n: jnp.zeros((T, n), F32)
    tail = LANES - MLA_NOPE - MLA_ROPE
    C = jnp.concatenate([one(MLA_NOPE), cos, cos, one(tail)], axis=1)
    Sn = jnp.concatenate([zero(MLA_NOPE), -sin, zero(HALF_ROPE + tail)], axis=1)
    Sp = jnp.concatenate([zero(MLA_NOPE + HALF_ROPE), sin, zero(tail)], axis=1)
    return C, Sn, Sp


GATHER_FIRST = [("ffa_w_in", 0), ("ffa_w_out", 0)] + [(n, l) for l in range(DEPTH) for n in ("gdn_conv_w", "ln_g", "ln_b")]
GATHER_PLAN = {
    (0, "ffa_fwd"): [("mix_w_in", 0), ("mla_w_uq", 0), ("mla_w_ukv", 0), ("mix_w_o", 0)],
    (0, "in_proj"): [("ple_w_gate", 0), ("ple_w_proj", 0)],
    (0, "gdn_chunk_fwd"): [("ffb_w_in", 0)],
    (0, "sb_fwd"): [("ffb_w_out", 0), ("mix_w_o", 1)],
    (0, "mla_fwd"): [("ffa_w_out", 1)],
    (0, "ffb_fwd"): [("ffa_w_in", 1)],
    (1, "ffa_fwd"): [("mix_w_in", 1)],
    (1, "in_proj"): [("mla_w_uq", 1), ("mla_w_ukv", 1)],
    (1, "gdn_chunk_fwd"): [("ffb_w_in", 1)],
    (1, "sb_fwd"): [("ffb_w_out", 1), ("ple_w_gate", 1), ("ple_w_proj", 1)],
}
SCATTER_PLAN = {
    (1, "gdn_chunk_bwd"): [("ffb_w_in", 1)],
    (1, "gdn_pre_bwd"): [("ffb_w_out", 1), ("ple_w_gate", 1), ("ple_w_proj", 1), ("mix_w_o", 1)],
    (1, "ffa_bwd"): [("mix_w_in", 1), ("mla_w_uq", 1), ("mla_w_ukv", 1), ("gdn_conv_w", 1)],
    (0, "ffb_bwd"): [("ffa_w_in", 1)],
    (0, "gdn_chunk_bwd"): [("ffb_w_in", 0)],
    (0, "gdn_pre_bwd"): [("ffb_w_out", 0), ("ple_w_gate", 0), ("ple_w_proj", 0), ("mix_w_o", 0)],
    (0, "mla_bwd"): [("ffa_w_out", 1), ("ln_g", 1), ("ln_b", 1)],
    (0, "ffa_bwd"): [("mix_w_in", 0), ("mla_w_uq", 0), ("mla_w_ukv", 0), ("gdn_conv_w", 0)],
    (0, "d_ffa_in"): [("ffa_w_out", 0), ("ln_g", 0), ("ln_b", 0)],
}


class Exchanges:
    def __init__(self, shards):
        self.shards = shards
        self.full = {}
        self.partial = {}
        self.received = {}

    def _block(self, key):
        n, l = key
        return self.shards[n][l].astype(SHARDED[n][1])

    def _absorb_gather(self, keys, results):
        for (n, l), g in zip(keys, results):
            blk = self.shards[n][l]
            self.full[(n, l)] = g if n in FFN_SLOT else _from_slots(g, (1,) + blk.shape, SHARDED[n][0])[0]

    def gather_now(self, keys, name):
        self._absorb_gather(keys, all_gather([self._block(k) for k in keys], name))

    def gather_with(self, layer, tag):
        keys = GATHER_PLAN.get((layer, tag))
        return None if keys is None else (keys, Hosted("gather", [self._block(k) for k in keys]))

    def scatter_with(self, layer, tag):
        keys = SCATTER_PLAN.get((layer, tag))
        return None if keys is None else (keys, Hosted("scatter", [self.partial[k] for k in keys]))

    def done(self, carried):
        if carried is not None:
            keys, hosted = carried
            if hosted.kind == "gather":
                self._absorb_gather(keys, hosted.results)
            else:
                self.received.update(zip(keys, hosted.results))

    def add_grad(self, key, g):
        n, l = key
        self.partial[key] = g if n in FFN_SLOT else _to_slots(g[None], SHARDED[n][0]).astype(SHARDED[n][1])


def _carried(c):
    return None if c is None else c[1]


def _layer_fwd(h0, p_i, rope, i, ex, rep):
    L = "L%d_" % i
    S = {"h0": h0, "p": p_i}
    W = ex.full
    ln_g = [W[("ln_g", i)][j][None, :] for j in range(3)]
    ln_b = [W[("ln_b", i)][j][None, :] for j in range(3)]
    S["ln_g"] = ln_g
    c = ex.gather_with(i, "ffa_fwd")
    S["h1"], S["xh1"], S["rs1"], S["Ga"], S["Ua"], S["h1b"] = ffn_fwd(
        h0, W[("ffa_w_in", i)], W[("ffa_w_out", i)], ln_g[0], ln_b[0], L + "ffa_fwd", hosted=_carried(c))
    ex.done(c)
    S["win"] = _pad_in_proj(W[("mix_w_in", i)])
    c = ex.gather_with(i, "in_proj")
    S["proj"] = mm_nn(S["h1b"], S["win"], L + "in_proj", hosted=_carried(c))
    ex.done(c)
    S["conv"] = W[("gdn_conv_w", i)]
    S["a_row"] = _pad_lanes(rep["gdn_a_log"][i][None, :], LANES)
    S["dt_row"] = _pad_lanes(rep["gdn_dt_bias"][i][None, :], LANES)
    S["nw"] = rep["gdn_norm_w"][i][None, :]
    S["wq"] = rep["mla_q_norm_w"][i][None, :]
    S["wkv"] = rep["mla_kv_norm_w"][i][None, :]
    S["qkv"] = gdn_pre_fwd(S["proj"], S["conv"], L + "gdn_pre_fwd")
    c = ex.gather_with(i, "gdn_chunk_fwd")
    S["o_gdn"], S["opre"], S["Tm"], S["Sin"] = gdn_chunk_fwd(S["qkv"], S["proj"], S["a_row"], S["dt_row"], S["nw"],
                                                            L + "gdn_chunk_fwd", hosted=_carried(c))
    ex.done(c)
    c = ex.gather_with(i, "sb_fwd")
    S["o_sb"], S["tot"] = sb_fwd(S["proj"], L + "sb_fwd", hosted=_carried(c))
    ex.done(c)
    S["wuq"], S["wuk"], S["wuv"] = _mla_up_pad(W[("mla_w_uq", i)], W[("mla_w_ukv", i)])
    S["Q"], S["K"], S["V"] = mla_pre_fwd(S["proj"], S["wq"], S["wkv"], S["wuq"], S["wuk"], S["wuv"], *rope, L + "mla_pre_fwd")
    c = ex.gather_with(i, "mla_fwd")
    S["o_mla"], S["lse"] = mla_fwd(S["Q"], S["K"], S["V"], L + "mla_fwd", hosted=_carried(c))
    ex.done(c)
    wo = W[("mix_w_o", i)]
    wo_att = wo[GDN_W:].reshape(-1, GDN_D, wo.shape[1])
    S["wo"] = jnp.concatenate(
        [wo[:GDN_W], jnp.pad(wo_att, ((0, 0), (0, LANES - GDN_D), (0, 0))).reshape(-1, wo.shape[1])], axis=0)
    S["h2"], S["xh2"], S["rs2"], S["h2b"] = proj_res_ln([S["o_gdn"], S["o_sb"], S["o_mla"]], S["wo"], S["h1"],
                                                        ln_g[1], ln_b[1], L + "out_proj")
    c = ex.gather_with(i, "ffb_fwd")
    S["h3"], S["xh3"], S["rs3"], S["Gb"], S["Ub"], _ = ffn_fwd(
        S["h2"], W[("ffb_w_in", i)], W[("ffb_w_out", i)], ln_g[2], ln_b[2], L + "ffb_fwd", hosted=_carried(c))
    ex.done(c)
    h4, S["a"], S["e"] = ple_fwd(S["h3"], p_i, W[("ple_w_gate", i)], W[("ple_w_proj", i)], L + "ple_fwd")
    return h4, S


def _layer_bwd(dh4, S, rope, i, ex):
    L = "L%d_" % i
    W = ex.full
    Grep = {}
    dh3, da, de = ple_bwd(dh4, S["a"], S["e"], W[("ple_w_gate", i)], L + "ple_bwd")
    ex.add_grad(("ple_w_gate", i), mm_tn(S["h3"], da, L + "d_ple_gate"))
    ex.add_grad(("ple_w_proj", i), mm_tn(S["p"], de, L + "d_ple_proj"))
    dz3, dg2, db2 = ln_bwd(dh3, S["xh3"], S["rs3"], S["ln_g"][2], L + "ln3_bwd")
    c = ex.scatter_with(i, "ffb_bwd")
    dh2, dGb, dUb, actb = ffn_bwd(dz3, S["Gb"], S["Ub"], W[("ffb_w_in", i)], W[("ffb_w_out", i)], L + "ffb_bwd",
                                  hosted=_carried(c))
    ex.done(c)
    ex.add_grad(("ffb_w_in", i), ffn_dw_in(S["h2b"].T, dGb, dUb, L + "d_ffb_in"))
    ex.add_grad(("ffb_w_out", i), ffn_dw_out(actb, dz3, L + "d_ffb_out"))
    dz2, dg1, db1 = ln_bwd(dh2, S["xh2"], S["rs2"], S["ln_g"][1], L + "ln2_bwd")
    docat = mm_nn(dz2, S["wo"], L + "d_ocat", b_transposed=True)
    dwo_att = jnp.concatenate([mm_tn(S["o_sb"], dz2, L + "d_wo_sb"), mm_tn(S["o_mla"], dz2, L + "d_wo_mla")], axis=0)
    dwo_att = dwo_att.reshape(-1, LANES, dwo_att.shape[1])[:, :GDN_D, :].reshape(-1, dwo_att.shape[1])
    ex.add_grad(("mix_w_o", i), jnp.concatenate([mm_tn(S["o_gdn"], dz2, L + "d_wo_gdn"), dwo_att], axis=0))
    c = ex.scatter_with(i, "gdn_chunk_bwd")
    dqkv, dgz, dgab, d_alog, d_dt, d_nw = gdn_chunk_bwd(S["qkv"], S["proj"], S["a_row"], S["dt_row"], S["nw"],
                                                        S["opre"], S["Tm"], S["Sin"], docat, L + "gdn_chunk_bwd",
                                                        hosted=_carried(c))
    ex.done(c)
    c = ex.scatter_with(i, "gdn_pre_bwd")
    dc, dconv = gdn_pre_bwd_pointwise(S["proj"], S["conv"], dqkv, L + "gdn_pre_bwd", hosted=_carried(c))
    ex.done(c)
    dxqkv = gdn_pre_bwd_conv(dc, S["conv"], L + "gdn_conv_bwd")
    ex.add_grad(("gdn_conv_w", i), dconv)
    Grep["gdn_a_log"], Grep["gdn_dt_bias"], Grep["gdn_norm_w"] = d_alog[0, :GDN_HEADS], d_dt[0, :GDN_HEADS], d_nw[0]
    dsq, dsk, dsv = sb_bwd(S["proj"], S["tot"], docat, DO_SB, L + "sb_bwd")
    c = ex.scatter_with(i, "mla_bwd")
    dQ, dK, dV = mla_bwd(S["Q"], S["K"], S["V"], S["o_mla"], S["lse"], docat, DO_MLA, L + "mla_bwd",
                         hosted=_carried(c))
    ex.done(c)
    dmq, dmkv, dwuq, dwuk, dwuv, dwq, dwkv = mla_pre_bwd(
        S["proj"], S["wq"], S["wkv"], S["wuq"], S["wuk"], S["wuv"], *rope, dQ, dK, dV, L + "mla_pre_bwd")
    g_uq, g_ukv = _mla_up_unpad(dwuq, dwuk, dwuv)
    ex.add_grad(("mla_w_uq", i), g_uq)
    ex.add_grad(("mla_w_ukv", i), g_ukv)
    Grep["mla_q_norm_w"], Grep["mla_kv_norm_w"] = dwq[0], dwkv[0]
    dproj = jnp.concatenate([dxqkv, dgz, dsq, dsk, dsv, dmq, dmkv, dgab], axis=1).astype(BF16)
    ex.add_grad(("mix_w_in", i),
                _unpad_in_proj(mm_tn(S["h1b"].T, dproj, L + "d_in_proj", a_transposed=True)))
    dh1 = mm_nn(dproj, S["win"], L + "d_h1", res=dz2, res_scale=ALPHA, b_transposed=True)
    dz1, dg0, db0 = ln_bwd(dh1, S["xh1"], S["rs1"], S["ln_g"][0], L + "ln1_bwd")
    c = ex.scatter_with(i, "ffa_bwd")
    dh0, dGa, dUa, acta = ffn_bwd(dz1, S["Ga"], S["Ua"], W[("ffa_w_in", i)], W[("ffa_w_out", i)], L + "ffa_bwd",
                                  hosted=_carried(c))
    ex.done(c)
    ex.add_grad(("ffa_w_out", i), ffn_dw_out(acta, dz1, L + "d_ffa_out"))
    ex.add_grad(("ln_g", i), jnp.concatenate([dg0, dg1, dg2], axis=0))
    ex.add_grad(("ln_b", i), jnp.concatenate([db0, db1, db2], axis=0))
    c = ex.scatter_with(i, "d_ffa_in")
    if i == 0:
        ex.received[("ffa_w_in", i)] = ffn_dw_in_push(S["h0"].T.astype(BF16), dGa, dUa, L + "d_ffa_in", hosted=_carried(c))
    else:
        ex.add_grad(("ffa_w_in", i), ffn_dw_in(S["h0"].T.astype(BF16), dGa, dUa, L + "d_ffa_in", hosted=_carried(c)))
    ex.done(c)
    return dh0, Grep


def _local_step(x, p, positions, target, ex, rep):
    assert DEPTH == 2
    rope = _rope_tables(positions)
    h, saved = x, []
    for i in range(DEPTH):
        h, S = _layer_fwd(h, p[i], rope, i, ex, rep)
        saved.append(S)
    loss, dh = loss_head(h, target, "loss_head")
    grads = [None] * DEPTH
    for i in reversed(range(DEPTH)):
        dh, grads[i] = _layer_bwd(dh, saved[i], rope, i, ex)
    return loss, dh, {n: jnp.stack([grads[i][n] for i in range(DEPTH)]) for n in REPLICATED}


def kernel(x, p, positions, ffa_w_in, ffa_w_out, mix_w_in, gdn_conv_w, gdn_a_log, gdn_dt_bias, gdn_norm_w, mla_q_norm_w, mla_kv_norm_w, mla_w_uq, mla_w_ukv, mix_w_o, ffb_w_in, ffb_w_out, ln_g, ln_b, ple_w_gate, ple_w_proj, loss_target, m_ffa_w_in, m_ffa_w_out, m_mix_w_in, m_gdn_conv_w, m_gdn_a_log, m_gdn_dt_bias, m_gdn_norm_w, m_mla_q_norm_w, m_mla_kv_norm_w, m_mla_w_uq, m_mla_w_ukv, m_mix_w_o, m_ffb_w_in, m_ffb_w_out, m_ln_g, m_ln_b, m_ple_w_gate, m_ple_w_proj, v_ffa_w_in, v_ffa_w_out, v_mix_w_in, v_gdn_conv_w, v_gdn_a_log, v_gdn_dt_bias, v_gdn_norm_w, v_mla_q_norm_w, v_mla_kv_norm_w, v_mla_w_uq, v_mla_w_ukv, v_mix_w_o, v_ffb_w_in, v_ffb_w_out, v_ln_g, v_ln_b, v_ple_w_gate, v_ple_w_proj):
    given = dict(locals())
    shards = {n: given[n] for n in WEIGHTS}
    ex = Exchanges({n: shards[n] for n in SHARDED})
    ex.gather_now(GATHER_FIRST, "gather_first")
    loss, grad_x, Grep = _local_step(x[0], p[:, 0], positions[0], loss_target[0], ex, {n: shards[n] for n in REPLICATED})
    loss = lax.psum(loss[0, 0], ("x", "y", "c"))
    rep_received = dict(zip(REPLICATED, all_gather([Grep[n] for n in REPLICATED], "gather_replicated_grads")))
    grad, delta, new_m, new_v = {}, {}, {}, {}
    for n in WEIGHTS:
        shape = shards[n].shape
        parts = [rep_received[n]] if n in REPLICATED else [ex.received[(n, l)] for l in range(DEPTH)]
        if parts[0].shape[1] % 8:
            parts = [jnp.concatenate(parts, axis=1)]
        outs = reduce_adamw(parts, _view2d(shards[n]), _view2d(given["m_" + n]), _view2d(given["v_" + n]),
                            "adamw_" + n)
        grad[n], delta[n], new_m[n], new_v[n] = (t.reshape(shape) for t in outs)
    return (loss, grad_x[None], *[grad[n] for n in WEIGHTS], *[delta[n] for n in WEIGHTS],
            *[new_m[n] for n in WEIGHTS], *[new_v[n] for n in WEIGHTS])
```

```python
import functools
import numpy as np
import jax
import jax.numpy as jnp
from jax import lax
from jax.experimental import pallas as pl
from jax.experimental.pallas import tpu as pltpu

F32 = jnp.float32
BF16 = jnp.bfloat16

DEPTH = 2
LN_EPS = 1e-5
RMS_EPS = 1e-6
ALPHA = (2 * DEPTH) ** 0.25
GDN_HEADS, GDN_D, GDN_CONV, GDN_CHUNK = 8, 64, 4, 64
SB_HEADS, SB_DIM = 4, 64
MLA_HEADS, MLA_NOPE, MLA_ROPE, MLA_V, MLA_Q_RANK, MLA_KV_RANK = 4, 64, 32, 64, 256, 128
ROPE_BASE = 10000.0
HALF_ROPE = MLA_ROPE // 2
LANES = 128
N_DEV = 8
ADAM_LR, ADAM_B1, ADAM_B2, ADAM_EPS, ADAM_WD, ADAM_STEP = 0.001, 0.9, 0.999, 1e-08, 0.01, 10

CB_GQ, CB_GK, CB_GV, CB_GZ = 0, 4, 8, 12
CB_SQ, CB_SK, CB_SV = 16, 20, 24
CB_MQ, CB_MKV, CB_GAB = 28, 30, 32
PROJ_W = 33 * LANES
GDN_W = GDN_HEADS * GDN_D
DO_SB = GDN_W // LANES
DO_MLA = DO_SB + SB_HEADS
VMEM_LIMIT = 56 * 1024 * 1024
MM_TILE = 1536

NT_DIMS = (((1,), (1,)), ((), ()))
TN_DIMS = (((0,), (0,)), ((), ()))


def _cp(sem):
    return pltpu.CompilerParams(dimension_semantics=sem, vmem_limit_bytes=VMEM_LIMIT)


def _bdot(a, b):
    return jnp.dot(a.astype(BF16), b.astype(BF16), preferred_element_type=F32)


def _bdot_nt(a, b):
    return lax.dot_general(a.astype(BF16), b.astype(BF16), NT_DIMS, preferred_element_type=F32)


def _bdot_tn(a, b):
    return lax.dot_general(a.astype(BF16), b.astype(BF16), TN_DIMS, preferred_element_type=F32)


def _split2(a):
    hi = a.astype(BF16)
    lo = (a - hi.astype(F32)).astype(BF16)
    return hi, lo


def _ones_dot_left(ones_bf16, x):
    hi = x.astype(BF16)
    r1 = x - hi.astype(F32)
    mid = r1.astype(BF16)
    lo = (r1 - mid.astype(F32)).astype(BF16)
    d = functools.partial(jnp.dot, preferred_element_type=F32)
    return d(ones_bf16, hi) + d(ones_bf16, mid) + d(ones_bf16, lo)


def _iota2(shape, dim):
    return lax.broadcasted_iota(jnp.int32, shape, dim)


def _sigmoid(x):
    return 0.5 * jnp.tanh(0.5 * x) + 0.5


def _softplus(x):
    return jnp.maximum(x, 0.0) + jnp.log(1.0 + jnp.exp(-jnp.abs(x)))


def _pick(n, limit, mult):
    if n <= limit:
        return n
    best = None
    for t in range(mult, limit + 1, mult):
        if n % t == 0:
            best = t
    assert best is not None, (n, limit, mult)
    return best


MESH = pl.DeviceIdType.MESH
ANY = pl.BlockSpec(memory_space=pl.ANY)


def _place():
    return lax.axis_index("x"), lax.axis_index("y"), lax.axis_index("c")


def _peer(k):
    x, y, c = _place()
    return (1 - x if k & 4 else x, 1 - y if k & 2 else y, 1 - c if k & 1 else c)


class Hosted:
    def __init__(self, kind, arrays):
        self.kind, self.arrays, self.n, self.results = kind, list(arrays), len(arrays), None

    def out_shapes(self):
        if self.kind == "gather":
            return [jax.ShapeDtypeStruct((N_DEV,) + a.shape, a.dtype) for a in self.arrays]
        return [jax.ShapeDtypeStruct(a.shape, a.dtype) for a in self.arrays]

    def sems(self):
        return [pltpu.SemaphoreType.DMA((self.n, N_DEV - 1)), pltpu.SemaphoreType.DMA((self.n, N_DEV - 1)),
                pltpu.SemaphoreType.DMA((self.n,))]

    def _copies(self, src_refs, dst_refs, send_sems, recv_sems, local_sems):
        x, y, c = _place()
        me = 4 * x + 2 * y + c
        local, remote = [], []
        for a in range(self.n):
            gather = self.kind == "gather"
            local.append(pltpu.make_async_copy(src_refs[a] if gather else src_refs[a].at[me], dst_refs[a].at[me],
                                               local_sems.at[a]))
            for k in range(1, N_DEV):
                px, py, pc = _peer(k)
                remote.append(pltpu.make_async_remote_copy(
                    src_ref=src_refs[a] if gather else src_refs[a].at[4 * px + 2 * py + pc], dst_ref=dst_refs[a].at[me],
                    send_sem=send_sems.at[a, k - 1], recv_sem=recv_sems.at[a, k - 1],
                    device_id=(px, py, pc), device_id_type=MESH))
        return local, remote

    def start(self, *refs):
        local, remote = self._copies(*refs)
        for cp in local + remote:
            cp.start()

    def wait(self, *refs):
        local, remote = self._copies(*refs)
        for cp in remote:
            cp.wait_recv()
        for cp in remote:
            cp.wait_send()
        for cp in local:
            cp.wait()


def _hosted_call(hosted, body, *, grid, in_specs, out_specs, out_shape, scratch_shapes=(), compiler_params, name):
    if hosted is None:
        return pl.pallas_call(body, grid=grid, in_specs=in_specs, out_specs=out_specs, out_shape=out_shape,
                              scratch_shapes=scratch_shapes, compiler_params=compiler_params, name=name)
    single = not isinstance(out_shape, (list, tuple))
    o_specs = [out_specs] if single else list(out_specs)
    o_shape = [out_shape] if single else list(out_shape)
    n_in, n_out, n_scr, n = len(in_specs), len(o_specs), len(scratch_shapes), hosted.n

    def wrapped(*refs):
        ins, c_in = refs[:n_in], refs[n_in:n_in + n]
        outs, c_out = refs[n_in + n:n_in + n + n_out], refs[n_in + n + n_out:n_in + 2 * n + n_out]
        rest = refs[n_in + 2 * n + n_out:]
        scr, sems = rest[:n_scr], rest[n_scr:]
        ids = [pl.program_id(ax) for ax in range(len(grid))]
        first = functools.reduce(jnp.logical_and, [i == 0 for i in ids])
        last = functools.reduce(jnp.logical_and, [i == g - 1 for i, g in zip(ids, grid)])

        @pl.when(first)
        def _():
            hosted.start(c_in, c_out, *sems)

        body(*ins, *outs, *scr)

        @pl.when(last)
        def _():
            hosted.wait(c_in, c_out, *sems)

    call = pl.pallas_call(
        wrapped, grid=grid, in_specs=list(in_specs) + [ANY] * n, out_specs=o_specs + [ANY] * n,
        out_shape=o_shape + hosted.out_shapes(), scratch_shapes=list(scratch_shapes) + hosted.sems(),
        compiler_params=_cp(("arbitrary",) * len(grid)), name=name)

    def run(*args):
        outs = call(*args, *hosted.arrays)
        hosted.results = list(outs[n_out:])
        return outs[0] if single else list(outs[:n_out])

    return run


def exchange_now(hosted, name):
    n = hosted.n

    def body(*refs):
        src, dst, sems = refs[:n], refs[n:2 * n], refs[2 * n:]
        hosted.start(src, dst, *sems)
        hosted.wait(src, dst, *sems)

    return pl.pallas_call(body, out_shape=hosted.out_shapes(), in_specs=[ANY] * n, out_specs=[ANY] * n,
                          scratch_shapes=hosted.sems(), name=name)(*hosted.arrays)


def mm_nn(a, b, name, out_dtype=F32, res=None, res_scale=1.0, b_transposed=False, hosted=None):
    M, K = a.shape
    N = b.shape[0] if b_transposed else b.shape[1]
    tm, tn, tk = _pick(M, 512, 16), _pick(N, MM_TILE, LANES), _pick(K, MM_TILE, LANES)
    nk = K // tk
    has_res = res is not None
    dot = _bdot_nt if b_transposed else _bdot

    def body(*refs):
        if has_res:
            a_ref, b_ref, r_ref, o_ref, acc_ref = refs
        else:
            a_ref, b_ref, o_ref, acc_ref = refs
        k = pl.program_id(2)

        @pl.when(k == 0)
        def _():
            acc_ref[...] = jnp.zeros_like(acc_ref)

        acc_ref[...] += dot(a_ref[...], b_ref[...])

        @pl.when(k == nk - 1)
        def _():
            out = acc_ref[...]
            if has_res:
                out = out + res_scale * r_ref[...]
            o_ref[...] = out.astype(o_ref.dtype)

    b_spec = pl.BlockSpec((tn, tk), lambda i, j, k: (j, k)) if b_transposed else pl.BlockSpec((tk, tn), lambda i, j, k: (k, j))
    in_specs = [pl.BlockSpec((tm, tk), lambda i, j, k: (i, k)), b_spec]
    args = [a, b]
    if has_res:
        in_specs.append(pl.BlockSpec((tm, tn), lambda i, j, k: (i, j)))
        args.append(res)
    return _hosted_call(
        hosted, body, grid=(M // tm, N // tn, nk), in_specs=in_specs,
        out_specs=pl.BlockSpec((tm, tn), lambda i, j, k: (i, j)),
        out_shape=jax.ShapeDtypeStruct((M, N), out_dtype),
        scratch_shapes=[pltpu.VMEM((tm, tn), F32)],
        compiler_params=_cp(("parallel", "parallel", "arbitrary")), name=name)(*args)


def mm_tn(a, b, name, out_dtype=F32, a_transposed=False):
    K, T = a.shape if a_transposed else a.shape[::-1]
    _, N = b.shape
    tk = K if a_transposed else _pick(K, 512, LANES)
    tn, tt = _pick(N, MM_TILE, LANES), _pick(T, 512, LANES)
    nt = T // tt

    def body(a_ref, b_ref, o_ref, acc_ref):
        t = pl.program_id(2)

        @pl.when(t == 0)
        def _():
            acc_ref[...] = jnp.zeros_like(acc_ref)

        if a_transposed:
            acc_ref[...] += _bdot(a_ref[:, pl.ds(pl.multiple_of(t * tt, tt), tt)], b_ref[...])
        else:
            acc_ref[...] += _bdot_tn(a_ref[...], b_ref[...])

        @pl.when(t == nt - 1)
        def _():
            o_ref[...] = acc_ref[...].astype(o_ref.dtype)

    a_spec = pl.BlockSpec((K, T), lambda i, j, t: (0, 0)) if a_transposed else pl.BlockSpec((tt, tk), lambda i, j, t: (t, i))
    return pl.pallas_call(
        body, grid=(K // tk, N // tn, nt),
        in_specs=[a_spec, pl.BlockSpec((tt, tn), lambda i, j, t: (t, j))],
        out_specs=pl.BlockSpec((tk, tn), lambda i, j, t: (i, j)),
        out_shape=jax.ShapeDtypeStruct((K, N), out_dtype),
        scratch_shapes=[pltpu.VMEM((tk, tn), F32)],
        compiler_params=_cp(("parallel", "parallel", "arbitrary")), name=name)(a, b)


def _ln_apply(z, g, b):
    mu = jnp.mean(z, axis=-1, keepdims=True)
    zc = z - mu
    var = jnp.mean(zc * zc, axis=-1, keepdims=True)
    rstd = lax.rsqrt(var + LN_EPS)
    xhat = zc * rstd
    return xhat * g + b, xhat, rstd


def ln_bwd(dout, xhat, rstd, g, name):
    T, D = dout.shape
    tm = _pick(T, 512, 8)

    def body(do_ref, xh_ref, rs_ref, g_ref, dz_ref, dg_ref, db_ref):
        i = pl.program_id(0)

        @pl.when(i == 0)
        def _():
            dg_ref[...] = jnp.zeros_like(dg_ref)
            db_ref[...] = jnp.zeros_like(db_ref)

        do = do_ref[...]
        xh = xh_ref[...]
        dxh = do * g_ref[...]
        m1 = jnp.mean(dxh, axis=-1, keepdims=True)
        m2 = jnp.mean(dxh * xh, axis=-1, keepdims=True)
        dz_ref[...] = rs_ref[...] * (dxh - m1 - xh * m2)
        dg_ref[...] += jnp.sum(do * xh, axis=0, keepdims=True)
        db_ref[...] += jnp.sum(do, axis=0, keepdims=True)

    row = pl.BlockSpec((tm, D), lambda i: (i, 0))
    vec = pl.BlockSpec((1, D), lambda i: (0, 0))
    return pl.pallas_call(
        body, grid=(T // tm,),
        in_specs=[row, row, pl.BlockSpec((tm, 1), lambda i: (i, 0)), vec],
        out_specs=[row, vec, vec],
        out_shape=[jax.ShapeDtypeStruct((T, D), F32), jax.ShapeDtypeStruct((1, D), F32), jax.ShapeDtypeStruct((1, D), F32)],
        compiler_params=_cp(("arbitrary",)), name=name)(dout, xhat, rstd, g)


FFN_CHUNKS = N_DEV // 2


def ffn_fwd(h, w_in, w_out, g, b, name, hosted=None):
    T, D = h.shape
    fc = w_in.shape[2]
    half = w_out.shape[1]
    tm = _pick(T, 512, 8)
    nc = FFN_CHUNKS

    def body(h_ref, wg_ref, wu_ref, wo_ref, g_ref, b_ref, out_ref, xh_ref, rs_ref, G_ref, U_ref, ob_ref, acc_ref):
        c = pl.program_id(1)

        @pl.when(c == 0)
        def _():
            acc_ref[...] = jnp.zeros_like(acc_ref)

        hb = h_ref[...].astype(BF16)
        G = jnp.dot(hb, wg_ref[0], preferred_element_type=F32)
        U = jnp.dot(hb, wu_ref[0], preferred_element_type=F32)
        G_ref[0] = G
        U_ref[0] = U
        act = G * _sigmoid(G) * U
        acc_ref[...] += _bdot(act, wo_ref[...].reshape(2 * half, D))

        @pl.when(c == nc - 1)
        def _():
            z = ALPHA * h_ref[...] + 0.5 * acc_ref[...]
            out, xh, rs = _ln_apply(z, g_ref[...], b_ref[...])
            out_ref[...] = out
            ob_ref[...] = out.astype(BF16)
            xh_ref[...] = xh
            rs_ref[...] = rs

    row = pl.BlockSpec((tm, D), lambda i, c: (i, 0))
    vec = pl.BlockSpec((1, D), lambda i, c: (0, 0))
    cblk = pl.BlockSpec((1, tm, fc), lambda i, c: (c, i, 0))
    csds = jax.ShapeDtypeStruct((nc, T, fc), F32)
    return _hosted_call(
        hosted, body, grid=(T // tm, nc),
        in_specs=[row, pl.BlockSpec((1, D, fc), lambda i, c: (c, 0, 0)),
                  pl.BlockSpec((1, D, fc), lambda i, c: (c + nc, 0, 0)),
                  pl.BlockSpec((2, half, D), lambda i, c: (c, 0, 0)), vec, vec],
        out_specs=[row, row, pl.BlockSpec((tm, 1), lambda i, c: (i, 0)), cblk, cblk, row],
        out_shape=[jax.ShapeDtypeStruct((T, D), F32), jax.ShapeDtypeStruct((T, D), F32), jax.ShapeDtypeStruct((T, 1), F32),
                   csds, csds, jax.ShapeDtypeStruct((T, D), BF16)],
        scratch_shapes=[pltpu.VMEM((tm, D), F32)],
        compiler_params=_cp(("parallel", "arbitrary")), name=name)(h, w_in, w_in, w_out, g, b)


def ffn_bwd(dz, G, U, w_in, w_out, name, hosted=None):
    T, D = dz.shape
    nc, _, fc = G.shape
    half = w_out.shape[1]
    tm = _pick(T, 512, 16)

    def body(dz_ref, G_ref, U_ref, wg_ref, wu_ref, wo_ref, dh_ref, dG_ref, dU_ref, act_ref, acc_ref):
        c = pl.program_id(1)

        @pl.when(c == 0)
        def _():
            acc_ref[...] = jnp.zeros_like(acc_ref)

        dy = (0.5 * dz_ref[...]).astype(BF16)
        dact = _bdot_nt(dy, wo_ref[...].reshape(2 * half, D))
        G = G_ref[0]
        U = U_ref[0]
        s = _sigmoid(G)
        silu = G * s
        dG = (dact * U * (s * (1.0 + G * (1.0 - s)))).astype(BF16)
        dU = (dact * silu).astype(BF16)
        dG_ref[0] = dG
        dU_ref[0] = dU
        act_ref[0] = (silu * U).astype(BF16)
        acc_ref[...] += _bdot_nt(dG, wg_ref[0]) + _bdot_nt(dU, wu_ref[0])

        @pl.when(c == nc - 1)
        def _():
            dh_ref[...] = ALPHA * dz_ref[...] + acc_ref[...]

    row = pl.BlockSpec((tm, D), lambda i, c: (i, 0))
    cblk = pl.BlockSpec((1, tm, fc), lambda i, c: (c, i, 0))
    csds = jax.ShapeDtypeStruct((nc, T, fc), BF16)
    return _hosted_call(
        hosted, body, grid=(T // tm, nc),
        in_specs=[row, cblk, cblk, pl.BlockSpec((1, D, fc), lambda i, c: (c, 0, 0)),
                  pl.BlockSpec((1, D, fc), lambda i, c: (c + nc, 0, 0)),
                  pl.BlockSpec((2, half, D), lambda i, c: (c, 0, 0))],
        out_specs=[row, cblk, cblk, cblk],
        out_shape=[jax.ShapeDtypeStruct((T, D), F32), csds, csds, csds],
        scratch_shapes=[pltpu.VMEM((tm, D), F32)],
        compiler_params=_cp(("parallel", "arbitrary")), name=name)(dz, G, U, w_in, w_in, w_out)


def ffn_dw_in(h_t, dG, dU, name, hosted=None):
    D, T = h_t.shape
    nc, _, fc = dG.shape
    tt = _pick(T, 512, LANES)
    nt = T // tt

    def body(h_ref, dG_ref, dU_ref, o_ref, acc_ref):
        s = pl.program_id(0)
        t = pl.program_id(1)

        @pl.when(t == 0)
        def _():
            acc_ref[...] = jnp.zeros_like(acc_ref)

        hb = h_ref[:, pl.ds(pl.multiple_of(t * tt, tt), tt)]

        @pl.when(s < nc)
        def _():
            acc_ref[...] += jnp.dot(hb, dG_ref[0], preferred_element_type=F32)

        @pl.when(s >= nc)
        def _():
            acc_ref[...] += jnp.dot(hb, dU_ref[0], preferred_element_type=F32)

        @pl.when(t == nt - 1)
        def _():
            o_ref[0] = acc_ref[...].astype(o_ref.dtype)

    return _hosted_call(
        hosted, body, grid=(2 * nc, nt),
        in_specs=[pl.BlockSpec((D, T), lambda s, t: (0, 0)),
                  pl.BlockSpec((1, tt, fc), lambda s, t: (jnp.minimum(s, nc - 1), jnp.where(s < nc, t, nt - 1), 0)),
                  pl.BlockSpec((1, tt, fc), lambda s, t: (jnp.maximum(s - nc, 0), jnp.where(s >= nc, t, 0), 0))],
        out_specs=pl.BlockSpec((1, D, fc), lambda s, t: (s, 0, 0)),
        out_shape=jax.ShapeDtypeStruct((2 * nc, D, fc), BF16),
        scratch_shapes=[pltpu.VMEM((D, fc), F32)],
        compiler_params=_cp(("parallel", "arbitrary")), name=name)(h_t, dG, dU)


def ffn_dw_out(act, dz, name, hosted=None):
    nc, T, fc = act.shape
    D = dz.shape[1]
    half = fc // 2
    tt = _pick(T, 512, 16)
    nt = T // tt

    def body(a_ref, dz_ref, o_ref, acc_ref):
        t = pl.program_id(1)

        @pl.when(t == 0)
        def _():
            acc_ref[...] = jnp.zeros_like(acc_ref)

        acc_ref[...] += _bdot_tn(a_ref[0], dz_ref[...])

        @pl.when(t == nt - 1)
        def _():
            o_ref[...] = (0.5 * acc_ref[...]).reshape(2, half, D).astype(o_ref.dtype)

    return _hosted_call(
        hosted, body, grid=(nc, nt),
        in_specs=[pl.BlockSpec((1, tt, fc), lambda c, t: (c, t, 0)), pl.BlockSpec((tt, D), lambda c, t: (t, 0))],
        out_specs=pl.BlockSpec((2, half, D), lambda c, t: (c, 0, 0)),
        out_shape=jax.ShapeDtypeStruct((2 * nc, half, D), BF16),
        scratch_shapes=[pltpu.VMEM((fc, D), F32)],
        compiler_params=_cp(("parallel", "arbitrary")), name=name)(act, dz)


def proj_res_ln(parts, w, res, g, b, name):
    T, D = res.shape
    tm = _pick(T, 512, 8)
    widths = [p.shape[1] for p in parts]
    offs = [int(sum(widths[:i])) for i in range(len(parts))]
    n = len(parts)

    def body(*refs):
        p_refs = refs[:n]
        w_ref, r_ref, g_ref, b_ref, out_ref, xh_ref, rs_ref, ob_ref = refs[n:]
        acc = ALPHA * r_ref[...]
        for p_ref, o, wd in zip(p_refs, offs, widths):
            acc = acc + _bdot(p_ref[...], w_ref[o:o + wd, :])
        out, xh, rs = _ln_apply(acc, g_ref[...], b_ref[...])
        out_ref[...] = out
        ob_ref[...] = out.astype(BF16)
        xh_ref[...] = xh
        rs_ref[...] = rs

    row = pl.BlockSpec((tm, D), lambda i: (i, 0))
    vec = pl.BlockSpec((1, D), lambda i: (0, 0))
    return pl.pallas_call(
        body, grid=(T // tm,),
        in_specs=[pl.BlockSpec((tm, wd), lambda i: (i, 0)) for wd in widths]
        + [pl.BlockSpec(w.shape, lambda i: (0, 0)), row, vec, vec],
        out_specs=[row, row, pl.BlockSpec((tm, 1), lambda i: (i, 0)), row],
        out_shape=[jax.ShapeDtypeStruct((T, D), F32), jax.ShapeDtypeStruct((T, D), F32), jax.ShapeDtypeStruct((T, 1), F32),
                   jax.ShapeDtypeStruct((T, D), BF16)],
        compiler_params=_cp(("parallel",)), name=name)(*parts, w, res, g, b)


def ple_fwd(h, p, wg, wp, name):
    T, D = h.shape
    P = p.shape[1]
    tm, tn = _pick(T, 512, 8), _pick(D, 512, LANES)

    def body(h_ref, hn_ref, p_ref, wg_ref, wp_ref, out_ref, a_ref, e_ref):
        a = _bdot(h_ref[...], wg_ref[...])
        e = _bdot(p_ref[...], wp_ref[...])
        a_ref[...] = a
        e_ref[...] = e
        out_ref[...] = hn_ref[...] + _sigmoid(a) * e

    blk = pl.BlockSpec((tm, tn), lambda i, j: (i, j))
    sds = jax.ShapeDtypeStruct((T, D), F32)
    return pl.pallas_call(
        body, grid=(T // tm, D // tn),
        in_specs=[pl.BlockSpec((tm, D), lambda i, j: (i, 0)), blk, pl.BlockSpec((tm, P), lambda i, j: (i, 0)),
                  pl.BlockSpec((D, tn), lambda i, j: (0, j)), pl.BlockSpec((P, tn), lambda i, j: (0, j))],
        out_specs=[blk, blk, blk], out_shape=[sds, sds, sds],
        compiler_params=_cp(("parallel", "parallel")), name=name)(h, h, p, wg, wp)


def ple_bwd(dout, a, e, wg, name):
    T, D = dout.shape
    tm = _pick(T, 512, 16)

    def body(do_ref, a_ref, e_ref, wg_ref, dh_ref, da_ref, de_ref):
        do = do_ref[...]
        s = _sigmoid(a_ref[...])
        da = (do * e_ref[...] * s * (1.0 - s)).astype(BF16)
        da_ref[...] = da
        de_ref[...] = (do * s).astype(BF16)
        dh_ref[...] = do + _bdot_nt(da, wg_ref[...])

    row = pl.BlockSpec((tm, D), lambda i: (i, 0))
    return pl.pallas_call(
        body, grid=(T // tm,),
        in_specs=[row, row, row, pl.BlockSpec((D, D), lambda i: (0, 0))],
        out_specs=[row, row, row],
        out_shape=[jax.ShapeDtypeStruct((T, D), F32), jax.ShapeDtypeStruct((T, D), BF16), jax.ShapeDtypeStruct((T, D), BF16)],
        compiler_params=_cp(("parallel",)), name=name)(dout, a, e, wg)


def loss_head(y, target, name):
    T, D = y.shape
    tm = _pick(T, 512, 8)

    def body(y_ref, t_ref, loss_ref, dy_ref):
        i = pl.program_id(0)

        @pl.when(i == 0)
        def _():
            loss_ref[...] = jnp.zeros_like(loss_ref)

        err = y_ref[...] - t_ref[...]
        dy_ref[...] = err * (1.0 / D)
        per_tok = jnp.sum(err * err, axis=-1, keepdims=True) * (1.0 / D)
        loss_ref[...] += 0.5 * jnp.sum(per_tok, axis=0, keepdims=True)

    row = pl.BlockSpec((tm, D), lambda i: (i, 0))
    return pl.pallas_call(
        body, grid=(T // tm,), in_specs=[row, row],
        out_specs=[pl.BlockSpec((1, 1), lambda i: (0, 0)), row],
        out_shape=[jax.ShapeDtypeStruct((1, 1), F32), jax.ShapeDtypeStruct((T, D), F32)],
        compiler_params=_cp(("arbitrary",)), name=name)(y, target)


HALO = 8


def _conv_taps(pad_ref, w_ref, tm, base):
    acc = w_ref[0:1, :] * pad_ref[pl.ds(base, tm), :]
    for k in range(1, GDN_CONV):
        acc = acc + w_ref[k:k + 1, :] * pad_ref[pl.ds(base + k, tm), :]
    return acc


GDN_GROUP_W = GDN_W
GDN_PRE_ROWS = 512


def _head_segments():
    head = jnp.arange(GDN_W, dtype=jnp.int32) // GDN_D
    return (head[:, None] == head[None, :]).astype(BF16)


def _head_sums(x, seg):
    hi = x.astype(BF16)
    r1 = x - hi.astype(F32)
    mid = r1.astype(BF16)
    lo = (r1 - mid.astype(F32)).astype(BF16)
    d = functools.partial(jnp.dot, preferred_element_type=F32)
    return d(hi, seg) + d(mid, seg) + d(lo, seg)


def _gdn_pre_common(x_ref, halo_ref, w_ref, seg_ref, pad_ref, tm):
    i = pl.program_id(1)
    grp = pl.program_id(0)
    pad_ref[0:HALO, :] = jnp.where(i == 0, 0.0, halo_ref[...])
    pad_ref[HALO:HALO + tm, :] = x_ref[...]
    c = _conv_taps(pad_ref, w_ref, tm, HALO - (GDN_CONV - 1))
    s = _sigmoid(c)
    y = c * s
    r = lax.rsqrt(_head_sums(y * y, seg_ref[...]) + RMS_EPS)
    scale = jnp.where(grp < 1, GDN_D ** -0.5, 1.0)
    return grp < 2, c, s, y, r, scale


def gdn_pre_fwd(proj, conv_w, name):
    T = proj.shape[0]
    tm = _pick(T, GDN_PRE_ROWS, 8)
    GW = GDN_GROUP_W

    def body(x_ref, halo_ref, w_ref, seg_ref, o_ref, pad_ref):
        normed, c, s, y, r, scale = _gdn_pre_common(x_ref, halo_ref, w_ref, seg_ref, pad_ref, tm)
        o_ref[...] = jnp.where(normed, y * r * scale, y)

    return pl.pallas_call(
        body, grid=(3, T // tm),
        in_specs=[pl.BlockSpec((tm, GW), lambda hb, i: (i, hb)),
                  pl.BlockSpec((HALO, GW), lambda hb, i: (jnp.maximum(i * (tm // HALO) - 1, 0), hb)),
                  pl.BlockSpec((GDN_CONV, GW), lambda hb, i: (0, hb)), pl.BlockSpec((GW, GW), lambda hb, i: (0, 0))],
        out_specs=pl.BlockSpec((tm, GW), lambda hb, i: (i, hb)),
        out_shape=jax.ShapeDtypeStruct((T, 3 * GW), F32),
        scratch_shapes=[pltpu.VMEM((tm + HALO, GW), F32)],
        compiler_params=_cp(("parallel", "parallel")), name=name)(proj, proj, conv_w, _head_segments())


def gdn_pre_bwd_pointwise(proj, conv_w, dqkv, name, hosted=None):
    T = proj.shape[0]
    tm = _pick(T, GDN_PRE_ROWS, 8)
    GW = GDN_GROUP_W

    def body(x_ref, halo_ref, w_ref, seg_ref, d_ref, dc_ref, dw_ref, pad_ref):
        i = pl.program_id(1)
        normed, c, s, y, r, scale = _gdn_pre_common(x_ref, halo_ref, w_ref, seg_ref, pad_ref, tm)

        @pl.when(i == 0)
        def _():
            dw_ref[...] = jnp.zeros_like(dw_ref)

        d = d_ref[...]
        n = y * r
        dn = d * scale
        dy = jnp.where(normed, r * (dn - n * _head_sums(dn * n, seg_ref[...])), d)
        dc = dy * (s * (1.0 + c * (1.0 - s)))
        dc_ref[...] = dc
        for k in range(GDN_CONV):
            xs = pad_ref[pl.ds(HALO - (GDN_CONV - 1) + k, tm), :]
            dw_ref[k:k + 1, :] += jnp.sum(dc * xs, axis=0, keepdims=True)

    blk = pl.BlockSpec((tm, GW), lambda hb, i: (i, hb))
    wblk = pl.BlockSpec((GDN_CONV, GW), lambda hb, i: (0, hb))
    return _hosted_call(
        hosted, body, grid=(3, T // tm),
        in_specs=[blk, pl.BlockSpec((HALO, GW), lambda hb, i: (jnp.maximum(i * (tm // HALO) - 1, 0), hb)), wblk,
                  pl.BlockSpec((GW, GW), lambda hb, i: (0, 0)), blk],
        out_specs=[blk, wblk],
        out_shape=[jax.ShapeDtypeStruct((T, 3 * GW), F32), jax.ShapeDtypeStruct((GDN_CONV, 3 * GW), F32)],
        scratch_shapes=[pltpu.VMEM((tm + HALO, GW), F32)],
        compiler_params=_cp(("parallel", "arbitrary")), name=name)(proj, proj, conv_w, _head_segments(), dqkv)


def gdn_pre_bwd_conv(dc, conv_w_p, name):
    T = dc.shape[0]
    tm = _pick(T, GDN_PRE_ROWS, 8)
    nt = T // tm
    GW = GDN_GROUP_W

    def body(dc_ref, halo_ref, w_ref, dx_ref, pad_ref):
        i = pl.program_id(1)
        pad_ref[0:tm, :] = dc_ref[...]
        pad_ref[tm:tm + HALO, :] = jnp.where(i == nt - 1, 0.0, halo_ref[...])
        acc = w_ref[GDN_CONV - 1:GDN_CONV, :] * pad_ref[pl.ds(0, tm), :]
        for k in range(GDN_CONV - 1):
            acc = acc + w_ref[k:k + 1, :] * pad_ref[pl.ds(GDN_CONV - 1 - k, tm), :]
        dx_ref[...] = acc

    blk = pl.BlockSpec((tm, GW), lambda hb, i: (i, hb))
    return pl.pallas_call(
        body, grid=(3, nt),
        in_specs=[blk, pl.BlockSpec((HALO, GW), lambda hb, i: (jnp.minimum((i + 1) * (tm // HALO), T // HALO - 1), hb)),
                  pl.BlockSpec((GDN_CONV, GW), lambda hb, i: (0, hb))],
        out_specs=blk,
        out_shape=jax.ShapeDtypeStruct((T, 3 * GW), F32),
        scratch_shapes=[pltpu.VMEM((tm + HALO, GW), F32)],
        compiler_params=_cp(("parallel", "parallel")), name=name)(dc, dc, conv_w_p)


def _chunk_masks(C):
    row = _iota2((C, C), 0)
    col = _iota2((C, C), 1)
    return row >= col, row > col, row == col


GDN_FWD_CHUNKS = 4
BNN = (((2,), (1,)), ((0,), (0,)))
BNT = (((2,), (2,)), ((0,), (0,)))
BTN = (((1,), (1,)), ((0,), (0,)))


def _bmm(a, b, dims=BNN):
    return lax.dot_general(a.astype(BF16), b.astype(BF16), dims, preferred_element_type=F32)


def _hbmm(a, b):
    m = a.shape[1]
    a_hi, a_lo = _split2(a)
    b_hi, b_lo = _split2(b)
    r = lax.dot_general(jnp.concatenate([a_hi, a_lo], axis=1), b_hi, BNN, preferred_element_type=F32)
    return r[:, :m] + r[:, m:] + lax.dot_general(a_hi, b_lo, BNN, preferred_element_type=F32)


def _hbmm_tn(a, b):
    a_hi, a_lo = _split2(a)
    b_hi, b_lo = _split2(b)
    d = functools.partial(lax.dot_general, dimension_numbers=BTN, preferred_element_type=F32)
    return d(a_hi, b_hi) + d(a_lo, b_hi) + d(a_hi, b_lo)


def _col_to_row(colv, eye):
    return jnp.sum(jnp.where(eye, colv, 0.0), axis=1, keepdims=True)


def _row_to_col(rowv, eye):
    return jnp.sum(jnp.where(eye, rowv, 0.0), axis=2, keepdims=True)


def _unit_lower_inverse(A, eye):
    C = A.shape[1]
    P = jnp.where(eye, 1.0, 0.0) - A
    Bp = _hbmm(A, A)
    for _ in range(4):
        R = _hbmm(jnp.concatenate([Bp, P], axis=1), Bp)
        Bp = R[:, :C]
        P = P + R[:, C:]
    return P + _hbmm(P, Bp)


def _stack_heads(ref, first_head, n, row0=0):
    rows = pl.ds(row0, GDN_CHUNK)
    return jnp.stack([ref[rows, pl.ds((first_head + h) * GDN_D, GDN_D)] for h in range(n)])


def _unstack_heads(ref, first_head, val, row0=0):
    rows = pl.ds(row0, GDN_CHUNK)
    for h in range(val.shape[0]):
        ref[rows, pl.ds((first_head + h) * GDN_D, GDN_D)] = val[h]


def _gdn_gates(gab, a_row, dt_row, incl):
    g_all = -jnp.exp(a_row) * _softplus(gab + dt_row)
    beta_all = _sigmoid(gab)
    gc_all = _ones_dot_left(incl.astype(BF16), g_all)
    return g_all, beta_all, gc_all


def _gdn_common(qkv_ref, gc_all, beta_all, incl, strict, eye, row0=0):
    C, H = GDN_CHUNK, GDN_HEADS
    q, k, v = (_stack_heads(qkv_ref, j * H, H, row0) for j in range(3))
    gc = jnp.stack([gc_all[:, h:h + 1] for h in range(H)])
    beta = jnp.stack([beta_all[:, H + h:H + h + 1] for h in range(H)])
    gc_row = _col_to_row(gc, eye)
    decay = jnp.where(incl, jnp.exp(jnp.where(incl, gc - gc_row, 0.0)), 0.0)
    e_gc = jnp.exp(gc)
    gl = gc[:, C - 1:C, :]
    e_gl = jnp.exp(gl)
    ekd = jnp.exp(gl - gc)
    kb = k * beta
    A = jnp.where(strict, _bmm(kb, k, BNT) * decay, 0.0)
    Pm = jnp.where(incl, _bmm(q, k, BNT) * decay, 0.0)
    return q, k, v, gc, beta, decay, e_gc, e_gl, ekd, kb, A, Pm


def gdn_chunk_fwd(qkv, proj, a_row, dt_row, norm_w, name, hosted=None):
    T = qkv.shape[0]
    C, H, Dh = GDN_CHUNK, GDN_HEADS, GDN_D
    N = T // C
    J = GDN_FWD_CHUNKS if N % GDN_FWD_CHUNKS == 0 else 1

    def body(qkv_ref, gz_ref, gab_ref, a_ref, dt_ref, nw_ref, o_ref, opre_ref, Tm_ref, Sin_ref, S_ref):
        n = pl.program_id(0)

        @pl.when(n == 0)
        def _():
            S_ref[...] = jnp.zeros_like(S_ref)

        incl, strict, eye = _chunk_masks(C)
        gab = gab_ref[...]
        per_chunk = []
        for j in range(J):
            _, beta_all, gc_all = _gdn_gates(gab[j * C:(j + 1) * C], a_ref[...], dt_ref[...], incl)
            per_chunk.append(_gdn_common(qkv_ref, gc_all, beta_all, incl, strict, eye, row0=j * C))
        q, k, v, gc, beta, decay, e_gc, e_gl, ekd, kb, A, Pm = (jnp.concatenate(t, axis=0) for t in zip(*per_chunk))
        Tm = _unit_lower_inverse(A, eye)
        u = _hbmm(Tm, v * beta)
        w = _hbmm(Tm, kb * e_gc)
        qd = q * e_gc
        kd = k * ekd
        S = S_ref[...]
        for j in range(J):
            hs = slice(j * H, (j + 1) * H)
            v_new = u[hs] - _bmm(w[hs], S)
            o = _bmm(qd[hs], S) + _bmm(Pm[hs], v_new)
            Sin_ref[j] = S
            Tm_ref[j] = Tm[hs]
            S = S * e_gl[hs] + _bmm(kd[hs], v_new, BTN)
            r = lax.rsqrt(jnp.mean(o * o, axis=-1, keepdims=True) + RMS_EPS)
            gz = _stack_heads(gz_ref, 0, H, j * C)
            _unstack_heads(opre_ref, 0, o, j * C)
            _unstack_heads(o_ref, 0, o * r * nw_ref[...] * (gz * _sigmoid(gz)), j * C)
        S_ref[...] = S

    vec = pl.BlockSpec((1, LANES), lambda n: (0, 0))
    hblk = pl.BlockSpec((J * C, GDN_W), lambda n: (n, 0))
    sblk = pl.BlockSpec((J, H, Dh, Dh), lambda n: (n, 0, 0, 0))
    return _hosted_call(
        hosted, body, grid=(N // J,),
        in_specs=[pl.BlockSpec((J * C, 3 * GDN_W), lambda n: (n, 0)),
                  pl.BlockSpec((J * C, GDN_W), lambda n: (n, CB_GZ * LANES // GDN_W)),
                  pl.BlockSpec((J * C, LANES), lambda n: (n, CB_GAB)), vec, vec, pl.BlockSpec((1, Dh), lambda n: (0, 0))],
        out_specs=[hblk, hblk, sblk, sblk],
        out_shape=[jax.ShapeDtypeStruct((T, GDN_W), F32), jax.ShapeDtypeStruct((T, GDN_W), F32),
                   jax.ShapeDtypeStruct((N, H, Dh, Dh), F32), jax.ShapeDtypeStruct((N, H, Dh, Dh), F32)],
        scratch_shapes=[pltpu.VMEM((H, Dh, Dh), F32)],
        compiler_params=_cp(("arbitrary",)), name=name)(qkv, proj, proj, a_row, dt_row, norm_w)


def gdn_chunk_bwd(qkv, proj, a_row, dt_row, norm_w, opre, Tm_all, Sin_all, docat, name, hosted=None):
    T = qkv.shape[0]
    C, H, Dh = GDN_CHUNK, GDN_HEADS, GDN_D
    N = T // C

    def body(qkv_ref, gz_ref, gab_ref, a_ref, dt_ref, nw_ref, opre_ref, Tm_ref, Sin_ref, do_ref,
             dqkv_ref, dgz_ref, dgab_ref, da_ref, ddt_ref, dnw_ref, dS_ref):
        n = pl.program_id(0)

        @pl.when(n == 0)
        def _():
            dS_ref[...] = jnp.zeros_like(dS_ref)
            da_ref[...] = jnp.zeros_like(da_ref)
            ddt_ref[...] = jnp.zeros_like(ddt_ref)
            dnw_ref[...] = jnp.zeros_like(dnw_ref)

        incl, strict, eye = _chunk_masks(C)
        gab = gab_ref[...]
        g_all, beta_all, gc_all = _gdn_gates(gab, a_ref[...], dt_ref[...], incl)
        lane = _iota2((C, LANES), 1)
        rowi = _iota2((C, 1), 0)
        nw = nw_ref[...]
        q, k, v, gc, beta, decay, e_gc, e_gl, ekd, kb, A, Pm = _gdn_common(qkv_ref, gc_all, beta_all, incl, strict, eye)
        Tm = Tm_ref[0]
        S = Sin_ref[0]
        dS = dS_ref[...]
        kbe = kb * e_gc
        u = _hbmm(Tm, v * beta)
        w = _hbmm(Tm, kbe)
        qd = q * e_gc
        kd = k * ekd
        v_new = u - _bmm(w, S)
        o = _stack_heads(opre_ref, 0, H)
        gz = _stack_heads(gz_ref, 0, H)
        don = _stack_heads(do_ref, 0, H)
        r = lax.rsqrt(jnp.mean(o * o, axis=-1, keepdims=True) + RMS_EPS)
        nn = o * r
        sgz = _sigmoid(gz)
        silu = gz * sgz
        _unstack_heads(dgz_ref, 0, don * nn * nw * (sgz * (1.0 + gz * (1.0 - sgz))))
        dnn = don * nw * silu
        dnw_ref[...] += jnp.sum(jnp.sum(don * nn * silu, axis=0), axis=0, keepdims=True)
        do = r * (dnn - nn * jnp.mean(dnn * nn, axis=-1, keepdims=True))
        dv_new = _bmm(Pm, do, BTN) + _bmm(kd, dS)
        dPm = jnp.where(incl, _bmm(do, v_new, BNT), 0.0)
        dqd = _bmm(do, S, BNT)
        dkd = _bmm(v_new, dS, BNT)
        dS_ref[...] = _bmm(qd, do, BTN) + e_gl * dS - _bmm(w, dv_new, BTN)
        dgl = jnp.sum(jnp.sum(dS * S, axis=2, keepdims=True), axis=1, keepdims=True) * e_gl
        dw = -_bmm(dv_new, S, BNT)
        dvb = _hbmm_tn(Tm, dv_new)
        dkbe = _hbmm_tn(Tm, dw)
        dA = -jnp.where(strict, _bmm(dvb, u, BNT) + _bmm(dkbe, w, BNT), 0.0)
        dAD = dA * decay
        dPD = dPm * decay
        Gm = dA * A + dPm * Pm
        dgc = jnp.sum(Gm, axis=2, keepdims=True) - _row_to_col(jnp.sum(Gm, axis=1, keepdims=True), eye)
        dkb = _bmm(dAD, k) + dkbe * e_gc
        dk = _bmm(dAD, kb, BTN) + _bmm(dPD, q, BTN) + dkd * ekd + dkb * beta
        dq = _bmm(dPD, k) + dqd * e_gc
        tkd = jnp.sum(dkd * kd, axis=-1, keepdims=True)
        dgc = dgc + jnp.sum(dqd * qd, axis=-1, keepdims=True) - tkd + jnp.sum(dkbe * kbe, axis=-1, keepdims=True)
        dgl = dgl + jnp.sum(tkd, axis=1, keepdims=True)
        dgc = dgc + jnp.where(rowi == C - 1, dgl, 0.0)
        dbeta = jnp.sum(dvb * v, axis=-1, keepdims=True) + jnp.sum(dkb * k, axis=-1, keepdims=True)
        _unstack_heads(dqkv_ref, 0, dq)
        _unstack_heads(dqkv_ref, H, dk)
        _unstack_heads(dqkv_ref, 2 * H, dvb * beta)
        dgc_all = jnp.zeros((C, LANES), F32)
        dbeta_all = jnp.zeros((C, LANES), F32)
        for h in range(H):
            dgc_all = dgc_all + jnp.where(lane == h, dgc[h], 0.0)
            dbeta_all = dbeta_all + jnp.where(lane == H + h, dbeta[h], 0.0)
        upper = (_iota2((C, C), 0) <= _iota2((C, C), 1)).astype(BF16)
        dg_all = _ones_dot_left(upper, dgc_all)
        dga = dg_all * (-jnp.exp(a_ref[...])) * _sigmoid(gab + dt_ref[...])
        dgb = dbeta_all * beta_all * (1.0 - beta_all)
        dgab_ref[...] = jnp.where(lane < H, dga, jnp.where(lane < 2 * H, dgb, 0.0))
        da_ref[...] += jnp.sum(jnp.where(lane < H, dg_all * g_all, 0.0), axis=0, keepdims=True)
        ddt_ref[...] += jnp.sum(jnp.where(lane < H, dga, 0.0), axis=0, keepdims=True)

    rev = lambda n: N - 1 - n
    vec = pl.BlockSpec((1, LANES), lambda n: (0, 0))
    nwv = pl.BlockSpec((1, Dh), lambda n: (0, 0))
    hblk = pl.BlockSpec((C, GDN_W), lambda n: (rev(n), 0))
    sblk = pl.BlockSpec((1, H, Dh, Dh), lambda n: (rev(n), 0, 0, 0))
    qblk = pl.BlockSpec((C, 3 * GDN_W), lambda n: (rev(n), 0))
    return _hosted_call(
        hosted, body, grid=(N,),
        in_specs=[qblk, pl.BlockSpec((C, GDN_W), lambda n: (rev(n), CB_GZ * LANES // GDN_W)),
                  pl.BlockSpec((C, LANES), lambda n: (rev(n), CB_GAB)), vec, vec, nwv, hblk, sblk, sblk, hblk],
        out_specs=[qblk, hblk, pl.BlockSpec((C, LANES), lambda n: (rev(n), 0)), vec, vec, nwv],
        out_shape=[jax.ShapeDtypeStruct((T, 3 * GDN_W), F32), jax.ShapeDtypeStruct((T, GDN_W), F32),
                   jax.ShapeDtypeStruct((T, LANES), F32), jax.ShapeDtypeStruct((1, LANES), F32),
                   jax.ShapeDtypeStruct((1, LANES), F32), jax.ShapeDtypeStruct((1, Dh), F32)],
        scratch_shapes=[pltpu.VMEM((H, Dh, Dh), F32)],
        compiler_params=_cp(("arbitrary",)), name=name)(qkv, proj, proj, a_row, dt_row, norm_w, opre, Tm_all, Sin_all, docat)


ATT_BQ, ATT_BK = 512, 512
NEG_BIG = -1e30


def _att_blocks(T):
    bq, bk = min(ATT_BQ, T), min(ATT_BK, T)
    assert bk % bq == 0 and T % bk == 0
    return bq, bk


def _att_specs(T, bq, cbs):
    qspec = lambda cb: pl.BlockSpec((bq, LANES), lambda h, i: (i, cb + h))
    kspec = lambda cb: pl.BlockSpec((T, LANES), lambda h, i: (0, cb + h))
    return qspec, kspec


def _kblock(ref, kb, bk):
    return ref[pl.ds(pl.multiple_of(kb * bk, bk), bk), :]


def _att_pos(i, kb, bq, bk):
    qpos = i * bq + _iota2((bq, bk), 0)
    kpos = kb * bk + _iota2((bq, bk), 1)
    return qpos, kpos


def _later_keys(n):
    return (_iota2((n, n), 0) > _iota2((n, n), 1)).astype(BF16)


def _earlier_keys(n):
    return (_iota2((n, n), 0) < _iota2((n, n), 1)).astype(BF16)


def _tri_dot(x, tri, terms):
    acc, rest = None, x
    for t in range(terms):
        part = rest.astype(BF16)
        if t + 1 < terms:
            rest = rest - part.astype(F32)
        d = jnp.dot(part, tri, preferred_element_type=F32)
        acc = d if acc is None else acc + d
    return acc


SB_BLOCK = 256
SB_DEAD = -104.0


def _sb_blocks(T):
    b = min(SB_BLOCK, T)
    assert T % b == 0 and T // b <= LANES
    return b, b


def sb_fwd(proj, name, hosted=None):
    T = proj.shape[0]
    H = SB_HEADS
    bq, bk = _sb_blocks(T)
    scale = SB_DIM ** -0.5

    def body(q_ref, k_ref, v_ref, o_ref, tot_ref):
        i = pl.program_id(1)
        qb = q_ref[...].astype(BF16)
        diag = (i * bq) // bk
        lane = _iota2((bq, LANES), 1)
        later = _later_keys(bk)

        def block(kb, acc, R, masked):
            z = _bdot_nt(qb, _kblock(k_ref, kb, bk)) * scale
            sp = _softplus(z)
            if masked:
                qpos, kpos = _att_pos(i, kb, bq, bk)
                mask = kpos < qpos
                l1m = jnp.where(mask, -sp, 0.0)
            else:
                l1m = -sp
            W = jnp.exp((z - sp) + _tri_dot(l1m, later, 3) + R)
            if masked:
                W = jnp.where(mask, W, 0.0)
            acc = acc + _bdot(W, _kblock(v_ref, kb, bk))
            return acc, R + jnp.sum(l1m, axis=-1, keepdims=True)

        acc, R = block(diag, jnp.zeros((bq, LANES), F32), jnp.zeros((bq, 1), F32), True)

        def live(c):
            return jnp.logical_and(c[0] >= 0, jnp.max(c[2]) > SB_DEAD)

        def step(c):
            kb, acc, R, Rb = c
            acc, R_next = block(kb, acc, R, False)
            return kb - 1, acc, R_next, jnp.where(lane == kb, R, Rb)

        _, acc, _, Rb = lax.while_loop(live, step, (diag - 1, acc, R, jnp.where(lane == diag, 0.0, NEG_BIG)))
        o_ref[...] = acc
        tot_ref[...] = Rb

    qspec, kspec = _att_specs(T, bq, None)
    sds = jax.ShapeDtypeStruct((T, H * LANES), F32)
    oblk = pl.BlockSpec((bq, LANES), lambda h, i: (i, h))
    return _hosted_call(
        hosted, body, grid=(H, T // bq), in_specs=[qspec(CB_SQ), kspec(CB_SK), kspec(CB_SV)],
        out_specs=[oblk, oblk], out_shape=[sds, sds],
        compiler_params=_cp(("parallel", "parallel")), name=name)(proj, proj, proj)


def sb_bwd(proj, tot, docat, do_cb, name):
    T = proj.shape[0]
    H = SB_HEADS
    bq, bk = _sb_blocks(T)
    scale = SB_DIM ** -0.5

    def body(q_ref, k_ref, v_ref, tot_ref, do_ref, dq_ref, dk_ref, dv_ref):
        i = pl.program_id(1)

        @pl.when(i == 0)
        def _():
            dk_ref[...] = jnp.zeros_like(dk_ref)
            dv_ref[...] = jnp.zeros_like(dv_ref)

        qb = q_ref[...].astype(BF16)
        dob = do_ref[...].astype(BF16)
        Rb = tot_ref[...]
        diag = (i * bq) // bk
        lane = _iota2((bq, LANES), 1)
        later, earlier = _later_keys(bk), _earlier_keys(bk)
        first = lax.while_loop(
            lambda kb: jnp.logical_and(kb < diag, jnp.max(jnp.where(lane == kb, Rb, NEG_BIG)) <= SB_DEAD),
            lambda kb: kb + 1, jnp.int32(0))

        def block(kb, carry, masked):
            dq, Epre = carry
            R = jnp.sum(jnp.where(lane == kb, Rb, 0.0), axis=1, keepdims=True)
            kblk = _kblock(k_ref, kb, bk).astype(BF16)
            z = _bdot_nt(qb, kblk) * scale
            sp = _softplus(z)
            if masked:
                qpos, kpos = _att_pos(i, kb, bq, bk)
                mask = kpos < qpos
                l1m = jnp.where(mask, -sp, 0.0)
            else:
                l1m = -sp
            W = jnp.exp((z - sp) + _tri_dot(l1m, later, 3) + R)
            if masked:
                W = jnp.where(mask, W, 0.0)
            E = _bdot_nt(dob, _kblock(v_ref, kb, bk)) * W
            cexcl = _tri_dot(E, earlier, 3) + Epre
            neg = jnp.exp(-sp)
            dz = E * neg - cexcl * (1.0 - neg)
            if masked:
                dz = jnp.where(mask, dz, 0.0)
            dz = (dz * scale).astype(BF16)
            rows = pl.ds(pl.multiple_of(kb * bk, bk), bk)
            dk_ref[rows, :] += lax.dot_general(dz, qb, TN_DIMS, preferred_element_type=F32)
            dv_ref[rows, :] += lax.dot_general(W.astype(BF16), dob, TN_DIMS, preferred_element_type=F32)
            dq = dq + jnp.dot(dz, kblk, preferred_element_type=F32)
            return dq, Epre + jnp.sum(E, axis=-1, keepdims=True)

        init = (jnp.zeros((bq, LANES), F32), jnp.zeros((bq, 1), F32))
        carry = lax.fori_loop(first, diag, lambda kb, c: block(kb, c, False), init)
        dq, _ = block(diag, carry, True)
        dq_ref[...] = dq

    qspec, kspec = _att_specs(T, bq, None)
    sds = jax.ShapeDtypeStruct((T, H * LANES), F32)
    oblk = pl.BlockSpec((bq, LANES), lambda h, i: (i, h))
    kout = pl.BlockSpec((T, LANES), lambda h, i: (0, h))
    return pl.pallas_call(
        body, grid=(H, T // bq),
        in_specs=[qspec(CB_SQ), kspec(CB_SK), kspec(CB_SV), oblk, qspec(do_cb)],
        out_specs=[oblk, kout, kout], out_shape=[sds, sds, sds],
        compiler_params=_cp(("arbitrary", "arbitrary")), name=name)(proj, proj, proj, tot, docat)


def mla_fwd(Q, K, V, name, hosted=None):
    T = Q.shape[0]
    H = MLA_HEADS
    bq, bk = _att_blocks(T)
    scale = (MLA_NOPE + MLA_ROPE) ** -0.5

    def body(q_ref, k_ref, v_ref, o_ref, lse_ref):
        i = pl.program_id(1)
        qb = q_ref[...]
        diag = (i * bq) // bk

        def block(kb, carry, masked):
            acc, m, l = carry
            s = _bdot_nt(qb, _kblock(k_ref, kb, bk)) * scale
            if masked:
                qpos, kpos = _att_pos(i, kb, bq, bk)
                s = jnp.where(kpos <= qpos, s, NEG_BIG)
            m_new = jnp.maximum(m, jnp.max(s, axis=-1, keepdims=True))
            p = jnp.exp(s - m_new)
            corr = jnp.exp(m - m_new)
            acc = corr * acc + _bdot(p, _kblock(v_ref, kb, bk))
            return acc, m_new, corr * l + jnp.sum(p, axis=-1, keepdims=True)

        init = (jnp.zeros((bq, LANES), F32), jnp.full((bq, 1), NEG_BIG, F32), jnp.zeros((bq, 1), F32))
        carry = lax.fori_loop(0, diag, lambda kb, c: block(kb, c, False), init)
        acc, m, l = block(diag, carry, True)
        o_ref[...] = acc / l
        lse_ref[...] = jnp.broadcast_to(m + jnp.log(l), (bq, LANES))

    qspec, kspec = _att_specs(T, bq, None)
    sds = jax.ShapeDtypeStruct((T, H * LANES), F32)
    oblk = pl.BlockSpec((bq, LANES), lambda h, i: (i, h))
    return _hosted_call(
        hosted, body, grid=(H, T // bq), in_specs=[qspec(0), kspec(0), kspec(0)],
        out_specs=[oblk, oblk], out_shape=[sds, sds],
        compiler_params=_cp(("parallel", "parallel")), name=name)(Q, K, V)


def mla_bwd(Q, K, V, o, lse, docat, do_cb, name, hosted=None):
    T = Q.shape[0]
    H = MLA_HEADS
    bq, bk = _att_blocks(T)
    scale = (MLA_NOPE + MLA_ROPE) ** -0.5

    def body(q_ref, k_ref, v_ref, o_ref, lse_ref, do_ref, dq_ref, dk_ref, dv_ref):
        i = pl.program_id(1)

        @pl.when(i == 0)
        def _():
            dk_ref[...] = jnp.zeros_like(dk_ref)
            dv_ref[...] = jnp.zeros_like(dv_ref)

        qb = q_ref[...]
        do = do_ref[...]
        dob = do.astype(BF16)
        delta = jnp.sum(do * o_ref[...], axis=-1, keepdims=True)
        lse = lse_ref[:, 0:1]

        diag = (i * bq) // bk

        def block(kb, dq, masked):
            kblk = _kblock(k_ref, kb, bk)
            s = _bdot_nt(qb, kblk) * scale
            if masked:
                qpos, kpos = _att_pos(i, kb, bq, bk)
                s = jnp.where(kpos <= qpos, s, NEG_BIG)
            p = jnp.exp(s - lse)
            dp = _bdot_nt(dob, _kblock(v_ref, kb, bk))
            ds = (p * (dp - delta) * scale).astype(BF16)
            rows = pl.ds(pl.multiple_of(kb * bk, bk), bk)
            dk_ref[rows, :] += lax.dot_general(ds, qb, TN_DIMS, preferred_element_type=F32)
            dv_ref[rows, :] += lax.dot_general(p.astype(BF16), dob, TN_DIMS, preferred_element_type=F32)
            return dq + jnp.dot(ds, kblk, preferred_element_type=F32)

        dq = lax.fori_loop(0, diag, lambda kb, c: block(kb, c, False), jnp.zeros((bq, LANES), F32))
        dq_ref[...] = block(diag, dq, True)

    qspec, kspec = _att_specs(T, bq, None)
    sds = jax.ShapeDtypeStruct((T, H * LANES), F32)
    oblk = pl.BlockSpec((bq, LANES), lambda h, i: (i, h))
    kout = pl.BlockSpec((T, LANES), lambda h, i: (0, h))
    return _hosted_call(
        hosted, body, grid=(H, T // bq),
        in_specs=[qspec(0), kspec(0), kspec(0), oblk, oblk, qspec(do_cb)],
        out_specs=[oblk, kout, kout], out_shape=[sds, sds, sds],
        compiler_params=_cp(("arbitrary", "arbitrary")), name=name)(Q, K, V, o, lse, docat)


def _tile_heads(t, n):
    return jnp.concatenate([t] * n, axis=1)


def _rope(X, C, Sn, Sp):
    n = X.shape[1]
    return X * C + pltpu.roll(X, n - HALF_ROPE, 1) * Sn + pltpu.roll(X, HALF_ROPE, 1) * Sp


def _rope_t(dO, C, Sn, Sp):
    n = dO.shape[1]
    return dO * C + pltpu.roll(dO * Sn, HALF_ROPE, 1) + pltpu.roll(dO * Sp, n - HALF_ROPE, 1)


def _rms(x, w):
    r = lax.rsqrt(jnp.mean(x * x, axis=-1, keepdims=True) + RMS_EPS)
    xh = x * r
    return r, xh, xh * w


def _rms_bwd(dn, w, r, xh):
    dxh = dn * w
    return r * (dxh - xh * jnp.mean(dxh * xh, axis=-1, keepdims=True)), jnp.sum(dn * xh, axis=0, keepdims=True)


def _mla_pre_specs(T, tm):
    KV = MLA_KV_RANK
    QR = MLA_Q_RANK
    W = MLA_HEADS * LANES
    full = lambda shape: pl.BlockSpec(shape, lambda i: (0, 0))
    specs = [pl.BlockSpec((tm, QR), lambda i: (i, CB_MQ * LANES // QR)),
             pl.BlockSpec((tm, 2 * LANES), lambda i: (i, CB_MKV // 2)),
             full((1, QR)), full((1, KV))]
    rope = [pl.BlockSpec((tm, LANES), lambda i: (i, 0))] * 3
    return specs, rope, full, W


def mla_pre_fwd(proj, wq, wkv, wuq, wuk, wuv, ropeC, ropeSn, ropeSp, name):
    T = proj.shape[0]
    tm = _pick(T, 512, 16)
    KV = MLA_KV_RANK
    H = MLA_HEADS

    def body(mq_ref, mkv_ref, wq_ref, wkv_ref, wuq_ref, wuk_ref, wuv_ref, c_ref, sn_ref, sp_ref, Q_ref, K_ref, V_ref):
        C, Sn, Sp = (_tile_heads(t[...], H) for t in (c_ref, sn_ref, sp_ref))
        _, _, qn = _rms(mq_ref[...], wq_ref[...])
        Q_ref[...] = _rope(_bdot(qn, wuq_ref[...]), C, Sn, Sp).astype(BF16)
        mkv = mkv_ref[...]
        _, _, kvn = _rms(mkv[:, :KV], wkv_ref[...])
        kr = pltpu.roll(mkv[:, KV:], MLA_NOPE, 1)
        K_ref[...] = _rope(_bdot(kvn, wuk_ref[...]) + _tile_heads(kr, H), C, Sn, Sp).astype(BF16)
        V_ref[...] = _bdot(kvn, wuv_ref[...]).astype(BF16)

    specs, rope, full, W = _mla_pre_specs(T, tm)
    oblk = pl.BlockSpec((tm, W), lambda i: (i, 0))
    sds = jax.ShapeDtypeStruct((T, W), BF16)
    return pl.pallas_call(
        body, grid=(T // tm,),
        in_specs=specs + [full(wuq.shape), full(wuk.shape), full(wuv.shape)] + rope,
        out_specs=[oblk, oblk, oblk], out_shape=[sds, sds, sds],
        compiler_params=_cp(("parallel",)), name=name)(proj, proj, wq, wkv, wuq, wuk, wuv, ropeC, ropeSn, ropeSp)


def mla_pre_bwd(proj, wq, wkv, wuq, wuk, wuv, ropeC, ropeSn, ropeSp, dQ, dK, dV, name):
    T = proj.shape[0]
    tm = _pick(T, 512, 16)
    KV = MLA_KV_RANK
    H = MLA_HEADS

    def body(mq_ref, mkv_ref, wq_ref, wkv_ref, wuq_ref, wuk_ref, wuv_ref,
             c_ref, sn_ref, sp_ref, dQ_ref, dK_ref, dV_ref,
             dmq_ref, dmkv_ref, dwuq_ref, dwuk_ref, dwuv_ref, dwq_ref, dwkv_ref):
        i = pl.program_id(0)

        @pl.when(i == 0)
        def _():
            for ref in (dwuq_ref, dwuk_ref, dwuv_ref, dwq_ref, dwkv_ref):
                ref[...] = jnp.zeros_like(ref)

        C, Sn, Sp = (_tile_heads(t[...], H) for t in (c_ref, sn_ref, sp_ref))
        rq, xq, qn = _rms(mq_ref[...], wq_ref[...])
        mkv = mkv_ref[...]
        rkv, xkv, kvn = _rms(mkv[:, :KV], wkv_ref[...])
        dqf = _rope_t(dQ_ref[...], C, Sn, Sp)
        dkf = _rope_t(dK_ref[...], C, Sn, Sp)
        dv = dV_ref[...]
        dwuq_ref[...] += _bdot_tn(qn, dqf)
        dwuk_ref[...] += _bdot_tn(kvn, dkf)
        dwuv_ref[...] += _bdot_tn(kvn, dv)
        dmq, dwq = _rms_bwd(_bdot_nt(dqf, wuq_ref[...]), wq_ref[...], rq, xq)
        dckv, dwkv = _rms_bwd(_bdot_nt(dkf, wuk_ref[...]) + _bdot_nt(dv, wuv_ref[...]), wkv_ref[...], rkv, xkv)
        dwq_ref[...] += dwq
        dwkv_ref[...] += dwkv
        dmq_ref[...] = dmq
        dkr = dkf[:, 0:LANES]
        for h in range(1, H):
            dkr = dkr + dkf[:, h * LANES:(h + 1) * LANES]
        dkr = pltpu.roll(dkr, LANES - MLA_NOPE, 1)
        dkr = jnp.where(_iota2(dkr.shape, 1) < MLA_ROPE, dkr, 0.0)
        dmkv_ref[...] = jnp.concatenate([dckv, dkr], axis=1)

    specs, rope, full, W = _mla_pre_specs(T, tm)
    wide = pl.BlockSpec((tm, W), lambda i: (i, 0))
    return pl.pallas_call(
        body, grid=(T // tm,),
        in_specs=specs + [full(w.shape) for w in (wuq, wuk, wuv)] + rope + [wide, wide, wide],
        out_specs=[pl.BlockSpec((tm, MLA_Q_RANK), lambda i: (i, 0)), pl.BlockSpec((tm, 2 * LANES), lambda i: (i, 0)),
                   full(wuq.shape), full(wuk.shape), full(wuv.shape), full((1, MLA_Q_RANK)), full((1, KV))],
        out_shape=[jax.ShapeDtypeStruct((T, MLA_Q_RANK), F32), jax.ShapeDtypeStruct((T, 2 * LANES), F32),
                   jax.ShapeDtypeStruct(wuq.shape, F32), jax.ShapeDtypeStruct(wuk.shape, F32),
                   jax.ShapeDtypeStruct(wuv.shape, F32), jax.ShapeDtypeStruct((1, MLA_Q_RANK), F32),
                   jax.ShapeDtypeStruct((1, KV), F32)],
        compiler_params=_cp(("arbitrary",)), name=name)(
            proj, proj, wq, wkv, wuq, wuk, wuv, ropeC, ropeSn, ropeSp, dQ, dK, dV)


def all_gather(shards, name):
    n = len(shards)

    def body(*refs):
        x_refs, out_refs = refs[:n], refs[n:2 * n]
        send_sems, recv_sems, local_sems = refs[2 * n:]
        x, y, c = _place()
        me, sibling = (x, y, c), (x, y, 1 - c)
        chips = [(1 - x, y), (x, 1 - y), (1 - x, 1 - y)]

        def slot(a, px, py, pc):
            return out_refs[a].at[4 * px + 2 * py + pc]

        def copy(a, k, block, to, src=None):
            return pltpu.make_async_remote_copy(
                src_ref=slot(a, *block) if src is None else src, dst_ref=slot(a, *block),
                send_sem=send_sems.at[a, k], recv_sem=recv_sems.at[a, k], device_id=to, device_id_type=MESH)

        mine = [pltpu.make_async_copy(x_refs[a], slot(a, *me), local_sems.at[a]) for a in range(n)]
        first = []
        for a in range(n):
            mine[a].start()
            first.append(copy(a, 0, me, sibling, src=x_refs[a]))
            first += [copy(a, 1 + j, me, (*chip, c), src=x_refs[a]) for j, chip in enumerate(chips)]
        for cp in first:
            cp.start()
        passed = []
        for j, chip in enumerate(chips):
            for a in range(n):
                copy(a, 1 + j, (*chip, c), me).wait_recv()
                passed.append(copy(a, 4 + j, (*chip, c), sibling))
                passed[-1].start()
        for a in range(n):
            copy(a, 0, sibling, me).wait_recv()
            for j, chip in enumerate(chips):
                copy(a, 4 + j, (*chip, 1 - c), me).wait_recv()
        for cp in first + passed:
            cp.wait_send()
        for cp in mine:
            cp.wait()

    return pl.pallas_call(
        body, out_shape=[jax.ShapeDtypeStruct((N_DEV,) + s.shape, s.dtype) for s in shards],
        in_specs=[ANY] * n, out_specs=[ANY] * n,
        scratch_shapes=[pltpu.SemaphoreType.DMA((n, 7)), pltpu.SemaphoreType.DMA((n, 7)), pltpu.SemaphoreType.DMA((n,))],
        name=name)(*shards)


def reduce_adamw(parts, w, m, v, name):
    L = len(parts)
    n, Rl, C = parts[0].shape
    R = w.shape[0]
    assert R == L * Rl
    tr = Rl if Rl * C <= 256 * 1024 else _pick(Rl, 256, 16)
    nr = Rl // tr

    def body(*refs):
        p_refs = refs[:L]
        w_ref, m_ref, v_ref, g_ref, d_ref, nm_ref, nv_ref, sum_ref = refs[L:]
        grp = pl.program_id(0)
        for j in range(L):
            @pl.when(grp == j)
            def _(j=j):
                acc = p_refs[j][0].astype(F32)
                for s in range(1, n):
                    acc = acc + p_refs[j][s].astype(F32)
                sum_ref[...] = acc

        g_ = sum_ref[...]
        m_ = ADAM_B1 * m_ref[...] + (1.0 - ADAM_B1) * g_
        v_ = ADAM_B2 * v_ref[...] + (1.0 - ADAM_B2) * (g_ * g_)
        m_hat = m_ / (1.0 - ADAM_B1 ** ADAM_STEP)
        v_hat = v_ / (1.0 - ADAM_B2 ** ADAM_STEP)
        g_ref[...] = g_
        d_ref[...] = -ADAM_LR * (m_hat / (jnp.sqrt(v_hat) + ADAM_EPS) + ADAM_WD * w_ref[...])
        nm_ref[...] = m_
        nv_ref[...] = v_

    blk = pl.BlockSpec((tr, C), lambda l, r: (l * nr + r, 0))
    sds = jax.ShapeDtypeStruct((R, C), F32)
    p_specs = [pl.BlockSpec((n, tr, C), lambda l, r, j=j: (0, jnp.where(l == j, r, 0), 0)) for j in range(L)]
    return pl.pallas_call(
        body, grid=(L, nr), in_specs=p_specs + [blk] * 3,
        out_specs=[blk] * 4, out_shape=[sds] * 4, scratch_shapes=[pltpu.VMEM((tr, C), F32)],
        compiler_params=_cp(("arbitrary", "arbitrary")), name=name)(*parts, w, m, v)


SHARDED = {"ffa_w_in": (2, BF16), "ffa_w_out": (1, BF16), "mix_w_in": (2, BF16), "mla_w_uq": (2, BF16),
           "mla_w_ukv": (2, BF16), "mix_w_o": (1, BF16), "ffb_w_in": (2, BF16), "ffb_w_out": (1, BF16),
           "ple_w_gate": (1, BF16), "ple_w_proj": (2, BF16), "gdn_conv_w": (2, F32), "ln_g": (2, F32), "ln_b": (2, F32)}
FFN_SLOT = ("ffa_w_in", "ffa_w_out", "ffb_w_in", "ffb_w_out")
REPLICATED = ("gdn_a_log", "gdn_dt_bias", "gdn_norm_w", "mla_q_norm_w", "mla_kv_norm_w")
WEIGHTS = ("ffa_w_in", "ffa_w_out", "mix_w_in", "gdn_conv_w", "gdn_a_log", "gdn_dt_bias", "gdn_norm_w", "mla_q_norm_w",
           "mla_kv_norm_w", "mla_w_uq", "mla_w_ukv", "mix_w_o", "ffb_w_in", "ffb_w_out", "ln_g", "ln_b", "ple_w_gate",
           "ple_w_proj")


def _to_slots(full, axis):
    L, a, b = full.shape
    if axis == 2:
        return full.reshape(L, a, N_DEV, b // N_DEV).transpose(2, 0, 1, 3).reshape(N_DEV, L * a, b // N_DEV)
    return full.reshape(L, N_DEV, a // N_DEV, b).transpose(1, 0, 2, 3).reshape(N_DEV, L * a // N_DEV, b)


def _from_slots(slots, shard_shape, axis):
    L, a, b = shard_shape
    t = slots.reshape((N_DEV,) + tuple(shard_shape))
    if axis == 2:
        return t.transpose(1, 2, 0, 3).reshape(L, a, N_DEV * b)
    return t.transpose(1, 0, 2, 3).reshape(L, N_DEV * a, b)


def _view2d(t):
    return t.reshape(-1, t.shape[-1])


def _pad_heads(w, nh):
    K = w.shape[0]
    return jnp.pad(w.reshape(K, nh, GDN_D), ((0, 0), (0, 0), (0, LANES - GDN_D))).reshape(K, nh * LANES)


def _unpad_heads(w, nh):
    K = w.shape[0]
    return w.reshape(K, nh, LANES)[:, :, :GDN_D].reshape(K, nh * GDN_D)


IN_WIDTHS = (512, 512, 512, 512, 8, 8, 256, 256, 256, 256, 160)


def _split_in(w):
    offs = np.cumsum((0,) + IN_WIDTHS)
    return [w[:, int(offs[i]):int(offs[i + 1])] for i in range(len(IN_WIDTHS))]


def _pad_in_proj(w):
    gq, gk, gv, gz, ga, gb, sq, sk, sv, mq, mkv = _split_in(w)
    gab = jnp.pad(jnp.concatenate([ga, gb], axis=1), ((0, 0), (0, LANES - 2 * GDN_HEADS)))
    return jnp.concatenate(
        [gq, gk, gv, gz] + [_pad_heads(t, SB_HEADS) for t in (sq, sk, sv)]
        + [mq, jnp.pad(mkv, ((0, 0), (0, 2 * LANES - mkv.shape[1]))), gab], axis=1)


def _unpad_in_proj(wp):
    c = lambda cb, n: wp[:, cb * LANES:(cb + n) * LANES]
    gab = c(CB_GAB, 1)
    parts = [c(cb, DO_SB) for cb in (CB_GQ, CB_GK, CB_GV, CB_GZ)]
    parts += [gab[:, :GDN_HEADS], gab[:, GDN_HEADS:2 * GDN_HEADS]]
    parts += [_unpad_heads(c(cb, SB_HEADS), SB_HEADS) for cb in (CB_SQ, CB_SK, CB_SV)]
    parts += [c(CB_MQ, 2), c(CB_MKV, 2)[:, :MLA_KV_RANK + MLA_ROPE]]
    return jnp.concatenate(parts, axis=1)


def _pad_lanes(w, width):
    return jnp.pad(w, ((0, 0), (0, width - w.shape[1])))


def _mla_up_pad(w_uq, w_ukv):
    H = MLA_HEADS
    dq = MLA_NOPE + MLA_ROPE
    wuq = jnp.pad(w_uq.reshape(-1, H, dq), ((0, 0), (0, 0), (0, LANES - dq))).reshape(-1, H * LANES)
    kv = w_ukv.reshape(-1, H, MLA_NOPE + MLA_V)
    wuk = jnp.pad(kv[:, :, :MLA_NOPE], ((0, 0), (0, 0), (0, LANES - MLA_NOPE))).reshape(-1, H * LANES)
    wuv = jnp.pad(kv[:, :, MLA_NOPE:], ((0, 0), (0, 0), (0, LANES - MLA_V))).reshape(-1, H * LANES)
    return wuq, wuk, wuv


def _mla_up_unpad(dwuq, dwuk, dwuv):
    H = MLA_HEADS
    dq = MLA_NOPE + MLA_ROPE
    g_uq = dwuq.reshape(-1, H, LANES)[:, :, :dq].reshape(-1, H * dq)
    g_ukv = jnp.concatenate([dwuk.reshape(-1, H, LANES)[:, :, :MLA_NOPE], dwuv.reshape(-1, H, LANES)[:, :, :MLA_V]],
                            axis=2).reshape(-1, H * (MLA_NOPE + MLA_V))
    return g_uq, g_ukv


def _rope_tables(positions):
    inv = 1.0 / (ROPE_BASE ** (jnp.arange(0, MLA_ROPE, 2, dtype=F32) / MLA_ROPE))
    ang = positions.astype(F32)[:, None] * inv
    cos, sin = jnp.cos(ang), jnp.sin(ang)
    T = positions.shape[0]
    one = lambda n: jnp.ones((T, n), F32)
    zero = lambda n: jnp.zeros((T, n), F32)
    tail = LANES - MLA_NOPE - MLA_ROPE
    C = jnp.concatenate([one(MLA_NOPE), cos, cos, one(tail)], axis=1)
    Sn = jnp.concatenate([zero(MLA_NOPE), -sin, zero(HALF_ROPE + tail)], axis=1)
    Sp = jnp.concatenate([zero(MLA_NOPE + HALF_ROPE), sin, zero(tail)], axis=1)
    return C, Sn, Sp


GATHER_FIRST = [("ffa_w_in", 0), ("ffa_w_out", 0)] + [(n, l) for l in range(DEPTH) for n in ("gdn_conv_w", "ln_g", "ln_b")]
GATHER_PLAN = {
    (0, "ffa_fwd"): [("mix_w_in", 0), ("mla_w_uq", 0), ("mla_w_ukv", 0), ("mix_w_o", 0)],
    (0, "in_proj"): [("ple_w_gate", 0), ("ple_w_proj", 0)],
    (0, "gdn_chunk_fwd"): [("ffb_w_in", 0)],
    (0, "sb_fwd"): [("ffb_w_out", 0), ("mix_w_o", 1)],
    (0, "mla_fwd"): [("ffa_w_out", 1)],
    (0, "ffb_fwd"): [("ffa_w_in", 1)],
    (1, "ffa_fwd"): [("mix_w_in", 1)],
    (1, "in_proj"): [("mla_w_uq", 1), ("mla_w_ukv", 1)],
    (1, "gdn_chunk_fwd"): [("ffb_w_in", 1)],
    (1, "sb_fwd"): [("ffb_w_out", 1), ("ple_w_gate", 1), ("ple_w_proj", 1)],
}
SCATTER_PLAN = {
    (1, "gdn_chunk_bwd"): [("ffb_w_in", 1)],
    (1, "gdn_pre_bwd"): [("ffb_w_out", 1), ("ple_w_gate", 1), ("ple_w_proj", 1), ("mix_w_o", 1)],
    (1, "ffa_bwd"): [("mix_w_in", 1), ("mla_w_uq", 1), ("mla_w_ukv", 1), ("gdn_conv_w", 1)],
    (0, "ffb_bwd"): [("ffa_w_in", 1)],
    (0, "gdn_chunk_bwd"): [("ffb_w_in", 0)],
    (0, "gdn_pre_bwd"): [("ffb_w_out", 0), ("ple_w_gate", 0), ("ple_w_proj", 0), ("mix_w_o", 0)],
    (0, "mla_bwd"): [("ffa_w_out", 1), ("ln_g", 1), ("ln_b", 1)],
    (0, "ffa_bwd"): [("mix_w_in", 0), ("mla_w_uq", 0), ("mla_w_ukv", 0), ("gdn_conv_w", 0)],
    (0, "d_ffa_in"): [("ffa_w_out", 0), ("ln_g", 0), ("ln_b", 0)],
}
SCATTER_LAST = [("ffa_w_in", 0)]


class Exchanges:
    def __init__(self, shards):
        self.shards = shards
        self.full = {}
        self.partial = {}
        self.received = {}

    def _block(self, key):
        n, l = key
        return self.shards[n][l].astype(SHARDED[n][1])

    def _absorb_gather(self, keys, results):
        for (n, l), g in zip(keys, results):
            blk = self.shards[n][l]
            self.full[(n, l)] = g if n in FFN_SLOT else _from_slots(g, (1,) + blk.shape, SHARDED[n][0])[0]

    def gather_now(self, keys, name):
        self._absorb_gather(keys, all_gather([self._block(k) for k in keys], name))

    def gather_with(self, layer, tag):
        keys = GATHER_PLAN.get((layer, tag))
        return None if keys is None else (keys, Hosted("gather", [self._block(k) for k in keys]))

    def scatter_with(self, layer, tag):
        keys = SCATTER_PLAN.get((layer, tag))
        return None if keys is None else (keys, Hosted("scatter", [self.partial[k] for k in keys]))

    def done(self, carried):
        if carried is not None:
            keys, hosted = carried
            if hosted.kind == "gather":
                self._absorb_gather(keys, hosted.results)
            else:
                self.received.update(zip(keys, hosted.results))

    def add_grad(self, key, g):
        n, l = key
        self.partial[key] = g if n in FFN_SLOT else _to_slots(g[None], SHARDED[n][0]).astype(SHARDED[n][1])


def _carried(c):
    return None if c is None else c[1]


def _layer_fwd(h0, p_i, rope, i, ex, rep):
    L = "L%d_" % i
    S = {"h0": h0, "p": p_i}
    W = ex.full
    ln_g = [W[("ln_g", i)][j][None, :] for j in range(3)]
    ln_b = [W[("ln_b", i)][j][None, :] for j in range(3)]
    S["ln_g"] = ln_g
    c = ex.gather_with(i, "ffa_fwd")
    S["h1"], S["xh1"], S["rs1"], S["Ga"], S["Ua"], S["h1b"] = ffn_fwd(
        h0, W[("ffa_w_in", i)], W[("ffa_w_out", i)], ln_g[0], ln_b[0], L + "ffa_fwd", hosted=_carried(c))
    ex.done(c)
    S["win"] = _pad_in_proj(W[("mix_w_in", i)])
    c = ex.gather_with(i, "in_proj")
    S["proj"] = mm_nn(S["h1b"], S["win"], L + "in_proj", hosted=_carried(c))
    ex.done(c)
    S["conv"] = W[("gdn_conv_w", i)]
    S["a_row"] = _pad_lanes(rep["gdn_a_log"][i][None, :], LANES)
    S["dt_row"] = _pad_lanes(rep["gdn_dt_bias"][i][None, :], LANES)
    S["nw"] = rep["gdn_norm_w"][i][None, :]
    S["wq"] = rep["mla_q_norm_w"][i][None, :]
    S["wkv"] = rep["mla_kv_norm_w"][i][None, :]
    S["qkv"] = gdn_pre_fwd(S["proj"], S["conv"], L + "gdn_pre_fwd")
    c = ex.gather_with(i, "gdn_chunk_fwd")
    S["o_gdn"], S["opre"], S["Tm"], S["Sin"] = gdn_chunk_fwd(S["qkv"], S["proj"], S["a_row"], S["dt_row"], S["nw"],
                                                            L + "gdn_chunk_fwd", hosted=_carried(c))
    ex.done(c)
    c = ex.gather_with(i, "sb_fwd")
    S["o_sb"], S["tot"] = sb_fwd(S["proj"], L + "sb_fwd", hosted=_carried(c))
    ex.done(c)
    S["wuq"], S["wuk"], S["wuv"] = _mla_up_pad(W[("mla_w_uq", i)], W[("mla_w_ukv", i)])
    S["Q"], S["K"], S["V"] = mla_pre_fwd(S["proj"], S["wq"], S["wkv"], S["wuq"], S["wuk"], S["wuv"], *rope, L + "mla_pre_fwd")
    c = ex.gather_with(i, "mla_fwd")
    S["o_mla"], S["lse"] = mla_fwd(S["Q"], S["K"], S["V"], L + "mla_fwd", hosted=_carried(c))
    ex.done(c)
    wo = W[("mix_w_o", i)]
    wo_att = wo[GDN_W:].reshape(-1, GDN_D, wo.shape[1])
    S["wo"] = jnp.concatenate(
        [wo[:GDN_W], jnp.pad(wo_att, ((0, 0), (0, LANES - GDN_D), (0, 0))).reshape(-1, wo.shape[1])], axis=0)
    S["h2"], S["xh2"], S["rs2"], S["h2b"] = proj_res_ln([S["o_gdn"], S["o_sb"], S["o_mla"]], S["wo"], S["h1"],
                                                        ln_g[1], ln_b[1], L + "out_proj")
    c = ex.gather_with(i, "ffb_fwd")
    S["h3"], S["xh3"], S["rs3"], S["Gb"], S["Ub"], _ = ffn_fwd(
        S["h2"], W[("ffb_w_in", i)], W[("ffb_w_out", i)], ln_g[2], ln_b[2], L + "ffb_fwd", hosted=_carried(c))
    ex.done(c)
    h4, S["a"], S["e"] = ple_fwd(S["h3"], p_i, W[("ple_w_gate", i)], W[("ple_w_proj", i)], L + "ple_fwd")
    return h4, S


def _layer_bwd(dh4, S, rope, i, ex):
    L = "L%d_" % i
    W = ex.full
    Grep = {}
    dh3, da, de = ple_bwd(dh4, S["a"], S["e"], W[("ple_w_gate", i)], L + "ple_bwd")
    ex.add_grad(("ple_w_gate", i), mm_tn(S["h3"], da, L + "d_ple_gate"))
    ex.add_grad(("ple_w_proj", i), mm_tn(S["p"], de, L + "d_ple_proj"))
    dz3, dg2, db2 = ln_bwd(dh3, S["xh3"], S["rs3"], S["ln_g"][2], L + "ln3_bwd")
    c = ex.scatter_with(i, "ffb_bwd")
    dh2, dGb, dUb, actb = ffn_bwd(dz3, S["Gb"], S["Ub"], W[("ffb_w_in", i)], W[("ffb_w_out", i)], L + "ffb_bwd",
                                  hosted=_carried(c))
    ex.done(c)
    ex.add_grad(("ffb_w_in", i), ffn_dw_in(S["h2b"].T, dGb, dUb, L + "d_ffb_in"))
    ex.add_grad(("ffb_w_out", i), ffn_dw_out(actb, dz3, L + "d_ffb_out"))
    dz2, dg1, db1 = ln_bwd(dh2, S["xh2"], S["rs2"], S["ln_g"][1], L + "ln2_bwd")
    docat = mm_nn(dz2, S["wo"], L + "d_ocat", b_transposed=True)
    dwo_att = jnp.concatenate([mm_tn(S["o_sb"], dz2, L + "d_wo_sb"), mm_tn(S["o_mla"], dz2, L + "d_wo_mla")], axis=0)
    dwo_att = dwo_att.reshape(-1, LANES, dwo_att.shape[1])[:, :GDN_D, :].reshape(-1, dwo_att.shape[1])
    ex.add_grad(("mix_w_o", i), jnp.concatenate([mm_tn(S["o_gdn"], dz2, L + "d_wo_gdn"), dwo_att], axis=0))
    c = ex.scatter_with(i, "gdn_chunk_bwd")
    dqkv, dgz, dgab, d_alog, d_dt, d_nw = gdn_chunk_bwd(S["qkv"], S["proj"], S["a_row"], S["dt_row"], S["nw"],
                                                        S["opre"], S["Tm"], S["Sin"], docat, L + "gdn_chunk_bwd",
                                                        hosted=_carried(c))
    ex.done(c)
    c = ex.scatter_with(i, "gdn_pre_bwd")
    dc, dconv = gdn_pre_bwd_pointwise(S["proj"], S["conv"], dqkv, L + "gdn_pre_bwd", hosted=_carried(c))
    ex.done(c)
    dxqkv = gdn_pre_bwd_conv(dc, S["conv"], L + "gdn_conv_bwd")
    ex.add_grad(("gdn_conv_w", i), dconv)
    Grep["gdn_a_log"], Grep["gdn_dt_bias"], Grep["gdn_norm_w"] = d_alog[0, :GDN_HEADS], d_dt[0, :GDN_HEADS], d_nw[0]
    dsq, dsk, dsv = sb_bwd(S["proj"], S["tot"], docat, DO_SB, L + "sb_bwd")
    c = ex.scatter_with(i, "mla_bwd")
    dQ, dK, dV = mla_bwd(S["Q"], S["K"], S["V"], S["o_mla"], S["lse"], docat, DO_MLA, L + "mla_bwd",
                         hosted=_carried(c))
    ex.done(c)
    dmq, dmkv, dwuq, dwuk, dwuv, dwq, dwkv = mla_pre_bwd(
        S["proj"], S["wq"], S["wkv"], S["wuq"], S["wuk"], S["wuv"], *rope, dQ, dK, dV, L + "mla_pre_bwd")
    g_uq, g_ukv = _mla_up_unpad(dwuq, dwuk, dwuv)
    ex.add_grad(("mla_w_uq", i), g_uq)
    ex.add_grad(("mla_w_ukv", i), g_ukv)
    Grep["mla_q_norm_w"], Grep["mla_kv_norm_w"] = dwq[0], dwkv[0]
    dproj = jnp.concatenate([dxqkv, dgz, dsq, dsk, dsv, dmq, dmkv, dgab], axis=1).astype(BF16)
    ex.add_grad(("mix_w_in", i),
                _unpad_in_proj(mm_tn(S["h1b"].T, dproj, L + "d_in_proj", a_transposed=True)))
    dh1 = mm_nn(dproj, S["win"], L + "d_h1", res=dz2, res_scale=ALPHA, b_transposed=True)
    dz1, dg0, db0 = ln_bwd(dh1, S["xh1"], S["rs1"], S["ln_g"][0], L + "ln1_bwd")
    c = ex.scatter_with(i, "ffa_bwd")
    dh0, dGa, dUa, acta = ffn_bwd(dz1, S["Ga"], S["Ua"], W[("ffa_w_in", i)], W[("ffa_w_out", i)], L + "ffa_bwd",
                                  hosted=_carried(c))
    ex.done(c)
    ex.add_grad(("ffa_w_out", i), ffn_dw_out(acta, dz1, L + "d_ffa_out"))
    ex.add_grad(("ln_g", i), jnp.concatenate([dg0, dg1, dg2], axis=0))
    ex.add_grad(("ln_b", i), jnp.concatenate([db0, db1, db2], axis=0))
    c = ex.scatter_with(i, "d_ffa_in")
    ex.add_grad(("ffa_w_in", i), ffn_dw_in(S["h0"].T.astype(BF16), dGa, dUa, L + "d_ffa_in", hosted=_carried(c)))
    ex.done(c)
    return dh0, Grep


def _local_step(x, p, positions, target, ex, rep):
    assert DEPTH == 2
    rope = _rope_tables(positions)
    h, saved = x, []
    for i in range(DEPTH):
        h, S = _layer_fwd(h, p[i], rope, i, ex, rep)
        saved.append(S)
    loss, dh = loss_head(h, target, "loss_head")
    grads = [None] * DEPTH
    for i in reversed(range(DEPTH)):
        dh, grads[i] = _layer_bwd(dh, saved[i], rope, i, ex)
    return loss, dh, {n: jnp.stack([grads[i][n] for i in range(DEPTH)]) for n in REPLICATED}


def kernel(x, p, positions, ffa_w_in, ffa_w_out, mix_w_in, gdn_conv_w, gdn_a_log, gdn_dt_bias, gdn_norm_w, mla_q_norm_w, mla_kv_norm_w, mla_w_uq, mla_w_ukv, mix_w_o, ffb_w_in, ffb_w_out, ln_g, ln_b, ple_w_gate, ple_w_proj, loss_target, m_ffa_w_in, m_ffa_w_out, m_mix_w_in, m_gdn_conv_w, m_gdn_a_log, m_gdn_dt_bias, m_gdn_norm_w, m_mla_q_norm_w, m_mla_kv_norm_w, m_mla_w_uq, m_mla_w_ukv, m_mix_w_o, m_ffb_w_in, m_ffb_w_out, m_ln_g, m_ln_b, m_ple_w_gate, m_ple_w_proj, v_ffa_w_in, v_ffa_w_out, v_mix_w_in, v_gdn_conv_w, v_gdn_a_log, v_gdn_dt_bias, v_gdn_norm_w, v_mla_q_norm_w, v_mla_kv_norm_w, v_mla_w_uq, v_mla_w_ukv, v_mix_w_o, v_ffb_w_in, v_ffb_w_out, v_ln_g, v_ln_b, v_ple_w_gate, v_ple_w_proj):
    given = dict(locals())
    shards = {n: given[n] for n in WEIGHTS}
    ex = Exchanges({n: shards[n] for n in SHARDED})
    ex.gather_now(GATHER_FIRST, "gather_first")
    loss, grad_x, Grep = _local_step(x[0], p[:, 0], positions[0], loss_target[0], ex, {n: shards[n] for n in REPLICATED})
    loss = lax.psum(loss[0, 0], ("x", "y", "c"))
    last = Hosted("scatter", [ex.partial[k] for k in SCATTER_LAST])
    ex.received.update(zip(SCATTER_LAST, exchange_now(last, "scatter_last")))
    rep_received = dict(zip(REPLICATED, all_gather([Grep[n] for n in REPLICATED], "gather_replicated_grads")))
    grad, delta, new_m, new_v = {}, {}, {}, {}
    for n in WEIGHTS:
        shape = shards[n].shape
        parts = [rep_received[n]] if n in REPLICATED else [ex.received[(n, l)] for l in range(DEPTH)]
        if parts[0].shape[1] % 8:
            parts = [jnp.concatenate(parts, axis=1)]
        outs = reduce_adamw(parts, _view2d(shards[n]), _view2d(given["m_" + n]), _view2d(given["v_" + n]),
                            "adamw_" + n)
        grad[n], delta[n], new_m[n], new_v[n] = (t.reshape(shape) for t in outs)
    return (loss, grad_x[None], *[grad[n] for n in WEIGHTS], *[delta[n] for n in WEIGHTS],
            *[new_m[n] for n in WEIGHTS], *[new_v[n] for n in WEIGHTS])
```

```python
import functools
import numpy as np
import jax
import jax.numpy as jnp
from jax import lax
from jax.experimental import pallas as pl
from jax.experimental.pallas import tpu as pltpu

F32 = jnp.float32
BF16 = jnp.bfloat16

DEPTH = 2
LN_EPS = 1e-5
RMS_EPS = 1e-6
ALPHA = (2 * DEPTH) ** 0.25
GDN_HEADS, GDN_D, GDN_CONV, GDN_CHUNK = 8, 64, 4, 64
SB_HEADS, SB_DIM = 4, 64
MLA_HEADS, MLA_NOPE, MLA_ROPE, MLA_V, MLA_Q_RANK, MLA_KV_RANK = 4, 64, 32, 64, 256, 128
ROPE_BASE = 10000.0
HALF_ROPE = MLA_ROPE // 2
LANES = 128
N_DEV = 8
ADAM_LR, ADAM_B1, ADAM_B2, ADAM_EPS, ADAM_WD, ADAM_STEP = 0.001, 0.9, 0.999, 1e-08, 0.01, 10

CB_GQ, CB_GK, CB_GV, CB_GZ = 0, 4, 8, 12
CB_SQ, CB_SK, CB_SV = 16, 20, 24
CB_MQ, CB_MKV, CB_GAB = 28, 30, 32
PROJ_W = 33 * LANES
GDN_W = GDN_HEADS * GDN_D
DO_SB = GDN_W // LANES
DO_MLA = DO_SB + SB_HEADS
VMEM_LIMIT = 56 * 1024 * 1024
MM_TILE = 1536

NT_DIMS = (((1,), (1,)), ((), ()))
TN_DIMS = (((0,), (0,)), ((), ()))


def _cp(sem):
    return pltpu.CompilerParams(dimension_semantics=sem, vmem_limit_bytes=VMEM_LIMIT)


def _bdot(a, b):
    return jnp.dot(a.astype(BF16), b.astype(BF16), preferred_element_type=F32)


def _bdot_nt(a, b):
    return lax.dot_general(a.astype(BF16), b.astype(BF16), NT_DIMS, preferred_element_type=F32)


def _bdot_tn(a, b):
    return lax.dot_general(a.astype(BF16), b.astype(BF16), TN_DIMS, preferred_element_type=F32)


def _split2(a):
    hi = a.astype(BF16)
    lo = (a - hi.astype(F32)).astype(BF16)
    return hi, lo


def _ones_dot_left(ones_bf16, x):
    hi = x.astype(BF16)
    r1 = x - hi.astype(F32)
    mid = r1.astype(BF16)
    lo = (r1 - mid.astype(F32)).astype(BF16)
    d = functools.partial(jnp.dot, preferred_element_type=F32)
    return d(ones_bf16, hi) + d(ones_bf16, mid) + d(ones_bf16, lo)


def _iota2(shape, dim):
    return lax.broadcasted_iota(jnp.int32, shape, dim)


def _sigmoid(x):
    return 0.5 * jnp.tanh(0.5 * x) + 0.5


def _softplus(x):
    return jnp.maximum(x, 0.0) + jnp.log(1.0 + jnp.exp(-jnp.abs(x)))


def _pick(n, limit, mult):
    if n <= limit:
        return n
    best = None
    for t in range(mult, limit + 1, mult):
        if n % t == 0:
            best = t
    assert best is not None, (n, limit, mult)
    return best


MESH = pl.DeviceIdType.MESH
ANY = pl.BlockSpec(memory_space=pl.ANY)


def _place():
    return lax.axis_index("x"), lax.axis_index("y"), lax.axis_index("c")


def _peer(k):
    x, y, c = _place()
    return (1 - x if k & 4 else x, 1 - y if k & 2 else y, 1 - c if k & 1 else c)


class Hosted:
    def __init__(self, kind, arrays):
        self.kind, self.arrays, self.n, self.results = kind, list(arrays), len(arrays), None

    def out_shapes(self):
        if self.kind == "gather":
            return [jax.ShapeDtypeStruct((N_DEV,) + a.shape, a.dtype) for a in self.arrays]
        return [jax.ShapeDtypeStruct(a.shape, a.dtype) for a in self.arrays]

    def sems(self):
        return [pltpu.SemaphoreType.DMA((self.n, N_DEV - 1)), pltpu.SemaphoreType.DMA((self.n, N_DEV - 1)),
                pltpu.SemaphoreType.DMA((self.n,))]

    def _copies(self, src_refs, dst_refs, send_sems, recv_sems, local_sems):
        x, y, c = _place()
        me = 4 * x + 2 * y + c
        local, remote = [], []
        for a in range(self.n):
            gather = self.kind == "gather"
            local.append(pltpu.make_async_copy(src_refs[a] if gather else src_refs[a].at[me], dst_refs[a].at[me],
                                               local_sems.at[a]))
            for k in range(1, N_DEV):
                px, py, pc = _peer(k)
                remote.append(pltpu.make_async_remote_copy(
                    src_ref=src_refs[a] if gather else src_refs[a].at[4 * px + 2 * py + pc], dst_ref=dst_refs[a].at[me],
                    send_sem=send_sems.at[a, k - 1], recv_sem=recv_sems.at[a, k - 1],
                    device_id=(px, py, pc), device_id_type=MESH))
        return local, remote

    def start(self, *refs):
        local, remote = self._copies(*refs)
        for cp in local + remote:
            cp.start()

    def wait(self, *refs):
        local, remote = self._copies(*refs)
        for cp in remote:
            cp.wait_recv()
        for cp in remote:
            cp.wait_send()
        for cp in local:
            cp.wait()


def _hosted_call(hosted, body, *, grid, in_specs, out_specs, out_shape, scratch_shapes=(), compiler_params, name):
    if hosted is None:
        return pl.pallas_call(body, grid=grid, in_specs=in_specs, out_specs=out_specs, out_shape=out_shape,
                              scratch_shapes=scratch_shapes, compiler_params=compiler_params, name=name)
    single = not isinstance(out_shape, (list, tuple))
    o_specs = [out_specs] if single else list(out_specs)
    o_shape = [out_shape] if single else list(out_shape)
    n_in, n_out, n_scr, n = len(in_specs), len(o_specs), len(scratch_shapes), hosted.n

    def wrapped(*refs):
        ins, c_in = refs[:n_in], refs[n_in:n_in + n]
        outs, c_out = refs[n_in + n:n_in + n + n_out], refs[n_in + n + n_out:n_in + 2 * n + n_out]
        rest = refs[n_in + 2 * n + n_out:]
        scr, sems = rest[:n_scr], rest[n_scr:]
        ids = [pl.program_id(ax) for ax in range(len(grid))]
        first = functools.reduce(jnp.logical_and, [i == 0 for i in ids])
        last = functools.reduce(jnp.logical_and, [i == g - 1 for i, g in zip(ids, grid)])

        @pl.when(first)
        def _():
            hosted.start(c_in, c_out, *sems)

        body(*ins, *outs, *scr)

        @pl.when(last)
        def _():
            hosted.wait(c_in, c_out, *sems)

    call = pl.pallas_call(
        wrapped, grid=grid, in_specs=list(in_specs) + [ANY] * n, out_specs=o_specs + [ANY] * n,
        out_shape=o_shape + hosted.out_shapes(), scratch_shapes=list(scratch_shapes) + hosted.sems(),
        compiler_params=_cp(("arbitrary",) * len(grid)), name=name)

    def run(*args):
        outs = call(*args, *hosted.arrays)
        hosted.results = list(outs[n_out:])
        return outs[0] if single else list(outs[:n_out])

    return run


def exchange_now(hosted, name):
    n = hosted.n

    def body(*refs):
        src, dst, sems = refs[:n], refs[n:2 * n], refs[2 * n:]
        hosted.start(src, dst, *sems)
        hosted.wait(src, dst, *sems)

    return pl.pallas_call(body, out_shape=hosted.out_shapes(), in_specs=[ANY] * n, out_specs=[ANY] * n,
                          scratch_shapes=hosted.sems(), name=name)(*hosted.arrays)


def mm_nn(a, b, name, out_dtype=F32, res=None, res_scale=1.0, b_transposed=False, hosted=None):
    M, K = a.shape
    N = b.shape[0] if b_transposed else b.shape[1]
    tm, tn, tk = _pick(M, 512, 16), _pick(N, MM_TILE, LANES), _pick(K, MM_TILE, LANES)
    nk = K // tk
    has_res = res is not None
    dot = _bdot_nt if b_transposed else _bdot

    def body(*refs):
        if has_res:
            a_ref, b_ref, r_ref, o_ref, acc_ref = refs
        else:
            a_ref, b_ref, o_ref, acc_ref = refs
        k = pl.program_id(2)

        @pl.when(k == 0)
        def _():
            acc_ref[...] = jnp.zeros_like(acc_ref)

        acc_ref[...] += dot(a_ref[...], b_ref[...])

        @pl.when(k == nk - 1)
        def _():
            out = acc_ref[...]
            if has_res:
                out = out + res_scale * r_ref[...]
            o_ref[...] = out.astype(o_ref.dtype)

    b_spec = pl.BlockSpec((tn, tk), lambda i, j, k: (j, k)) if b_transposed else pl.BlockSpec((tk, tn), lambda i, j, k: (k, j))
    in_specs = [pl.BlockSpec((tm, tk), lambda i, j, k: (i, k)), b_spec]
    args = [a, b]
    if has_res:
        in_specs.append(pl.BlockSpec((tm, tn), lambda i, j, k: (i, j)))
        args.append(res)
    return _hosted_call(
        hosted, body, grid=(M // tm, N // tn, nk), in_specs=in_specs,
        out_specs=pl.BlockSpec((tm, tn), lambda i, j, k: (i, j)),
        out_shape=jax.ShapeDtypeStruct((M, N), out_dtype),
        scratch_shapes=[pltpu.VMEM((tm, tn), F32)],
        compiler_params=_cp(("parallel", "parallel", "arbitrary")), name=name)(*args)


def mm_tn(a, b, name, out_dtype=F32, a_transposed=False):
    K, T = a.shape if a_transposed else a.shape[::-1]
    _, N = b.shape
    tk = K if a_transposed else _pick(K, 512, LANES)
    tn, tt = _pick(N, MM_TILE, LANES), _pick(T, 512, LANES)
    nt = T // tt

    def body(a_ref, b_ref, o_ref, acc_ref):
        t = pl.program_id(2)

        @pl.when(t == 0)
        def _():
            acc_ref[...] = jnp.zeros_like(acc_ref)

        if a_transposed:
            acc_ref[...] += _bdot(a_ref[:, pl.ds(pl.multiple_of(t * tt, tt), tt)], b_ref[...])
        else:
            acc_ref[...] += _bdot_tn(a_ref[...], b_ref[...])

        @pl.when(t == nt - 1)
        def _():
            o_ref[...] = acc_ref[...].astype(o_ref.dtype)

    a_spec = pl.BlockSpec((K, T), lambda i, j, t: (0, 0)) if a_transposed else pl.BlockSpec((tt, tk), lambda i, j, t: (t, i))
    return pl.pallas_call(
        body, grid=(K // tk, N // tn, nt),
        in_specs=[a_spec, pl.BlockSpec((tt, tn), lambda i, j, t: (t, j))],
        out_specs=pl.BlockSpec((tk, tn), lambda i, j, t: (i, j)),
        out_shape=jax.ShapeDtypeStruct((K, N), out_dtype),
        scratch_shapes=[pltpu.VMEM((tk, tn), F32)],
        compiler_params=_cp(("parallel", "parallel", "arbitrary")), name=name)(a, b)


def _ln_apply(z, g, b):
    mu = jnp.mean(z, axis=-1, keepdims=True)
    zc = z - mu
    var = jnp.mean(zc * zc, axis=-1, keepdims=True)
    rstd = lax.rsqrt(var + LN_EPS)
    xhat = zc * rstd
    return xhat * g + b, xhat, rstd


def ln_bwd(dout, xhat, rstd, g, name):
    T, D = dout.shape
    tm = _pick(T, 512, 8)

    def body(do_ref, xh_ref, rs_ref, g_ref, dz_ref, dg_ref, db_ref):
        i = pl.program_id(0)

        @pl.when(i == 0)
        def _():
            dg_ref[...] = jnp.zeros_like(dg_ref)
            db_ref[...] = jnp.zeros_like(db_ref)

        do = do_ref[...]
        xh = xh_ref[...]
        dxh = do * g_ref[...]
        m1 = jnp.mean(dxh, axis=-1, keepdims=True)
        m2 = jnp.mean(dxh * xh, axis=-1, keepdims=True)
        dz_ref[...] = rs_ref[...] * (dxh - m1 - xh * m2)
        dg_ref[...] += jnp.sum(do * xh, axis=0, keepdims=True)
        db_ref[...] += jnp.sum(do, axis=0, keepdims=True)

    row = pl.BlockSpec((tm, D), lambda i: (i, 0))
    vec = pl.BlockSpec((1, D), lambda i: (0, 0))
    return pl.pallas_call(
        body, grid=(T // tm,),
        in_specs=[row, row, pl.BlockSpec((tm, 1), lambda i: (i, 0)), vec],
        out_specs=[row, vec, vec],
        out_shape=[jax.ShapeDtypeStruct((T, D), F32), jax.ShapeDtypeStruct((1, D), F32), jax.ShapeDtypeStruct((1, D), F32)],
        compiler_params=_cp(("arbitrary",)), name=name)(dout, xhat, rstd, g)


FFN_CHUNKS = N_DEV // 2


def ffn_fwd(h, w_in, w_out, g, b, name, hosted=None):
    T, D = h.shape
    fc = w_in.shape[2]
    half = w_out.shape[1]
    tm = _pick(T, 512, 8)
    nc = FFN_CHUNKS

    def body(h_ref, wg_ref, wu_ref, wo_ref, g_ref, b_ref, out_ref, xh_ref, rs_ref, G_ref, U_ref, ob_ref, acc_ref):
        c = pl.program_id(1)

        @pl.when(c == 0)
        def _():
            acc_ref[...] = jnp.zeros_like(acc_ref)

        hb = h_ref[...].astype(BF16)
        G = jnp.dot(hb, wg_ref[0], preferred_element_type=F32)
        U = jnp.dot(hb, wu_ref[0], preferred_element_type=F32)
        G_ref[0] = G
        U_ref[0] = U
        act = G * _sigmoid(G) * U
        acc_ref[...] += _bdot(act, wo_ref[...].reshape(2 * half, D))

        @pl.when(c == nc - 1)
        def _():
            z = ALPHA * h_ref[...] + 0.5 * acc_ref[...]
            out, xh, rs = _ln_apply(z, g_ref[...], b_ref[...])
            out_ref[...] = out
            ob_ref[...] = out.astype(BF16)
            xh_ref[...] = xh
            rs_ref[...] = rs

    row = pl.BlockSpec((tm, D), lambda i, c: (i, 0))
    vec = pl.BlockSpec((1, D), lambda i, c: (0, 0))
    cblk = pl.BlockSpec((1, tm, fc), lambda i, c: (c, i, 0))
    csds = jax.ShapeDtypeStruct((nc, T, fc), F32)
    return _hosted_call(
        hosted, body, grid=(T // tm, nc),
        in_specs=[row, pl.BlockSpec((1, D, fc), lambda i, c: (c, 0, 0)),
                  pl.BlockSpec((1, D, fc), lambda i, c: (c + nc, 0, 0)),
                  pl.BlockSpec((2, half, D), lambda i, c: (c, 0, 0)), vec, vec],
        out_specs=[row, row, pl.BlockSpec((tm, 1), lambda i, c: (i, 0)), cblk, cblk, row],
        out_shape=[jax.ShapeDtypeStruct((T, D), F32), jax.ShapeDtypeStruct((T, D), F32), jax.ShapeDtypeStruct((T, 1), F32),
                   csds, csds, jax.ShapeDtypeStruct((T, D), BF16)],
        scratch_shapes=[pltpu.VMEM((tm, D), F32)],
        compiler_params=_cp(("parallel", "arbitrary")), name=name)(h, w_in, w_in, w_out, g, b)


def ffn_bwd(dz, G, U, w_in, w_out, name, hosted=None):
    T, D = dz.shape
    nc, _, fc = G.shape
    half = w_out.shape[1]
    tm = _pick(T, 512, 16)

    def body(dz_ref, G_ref, U_ref, wg_ref, wu_ref, wo_ref, dh_ref, dG_ref, dU_ref, act_ref, acc_ref):
        c = pl.program_id(1)

        @pl.when(c == 0)
        def _():
            acc_ref[...] = jnp.zeros_like(acc_ref)

        dy = (0.5 * dz_ref[...]).astype(BF16)
        dact = _bdot_nt(dy, wo_ref[...].reshape(2 * half, D))
        G = G_ref[0]
        U = U_ref[0]
        s = _sigmoid(G)
        silu = G * s
        dG = (dact * U * (s * (1.0 + G * (1.0 - s)))).astype(BF16)
        dU = (dact * silu).astype(BF16)
        dG_ref[0] = dG
        dU_ref[0] = dU
        act_ref[0] = (silu * U).astype(BF16)
        acc_ref[...] += _bdot_nt(dG, wg_ref[0]) + _bdot_nt(dU, wu_ref[0])

        @pl.when(c == nc - 1)
        def _():
            dh_ref[...] = ALPHA * dz_ref[...] + acc_ref[...]

    row = pl.BlockSpec((tm, D), lambda i, c: (i, 0))
    cblk = pl.BlockSpec((1, tm, fc), lambda i, c: (c, i, 0))
    csds = jax.ShapeDtypeStruct((nc, T, fc), BF16)
    return _hosted_call(
        hosted, body, grid=(T // tm, nc),
        in_specs=[row, cblk, cblk, pl.BlockSpec((1, D, fc), lambda i, c: (c, 0, 0)),
                  pl.BlockSpec((1, D, fc), lambda i, c: (c + nc, 0, 0)),
                  pl.BlockSpec((2, half, D), lambda i, c: (c, 0, 0))],
        out_specs=[row, cblk, cblk, cblk],
        out_shape=[jax.ShapeDtypeStruct((T, D), F32), csds, csds, csds],
        scratch_shapes=[pltpu.VMEM((tm, D), F32)],
        compiler_params=_cp(("parallel", "arbitrary")), name=name)(dz, G, U, w_in, w_in, w_out)


def ffn_dw_in(h_t, dG, dU, name, hosted=None):
    D, T = h_t.shape
    nc, _, fc = dG.shape
    tt = _pick(T, 512, LANES)
    nt = T // tt

    def body(h_ref, dG_ref, dU_ref, o_ref, acc_ref):
        s = pl.program_id(0)
        t = pl.program_id(1)

        @pl.when(t == 0)
        def _():
            acc_ref[...] = jnp.zeros_like(acc_ref)

        hb = h_ref[:, pl.ds(pl.multiple_of(t * tt, tt), tt)]

        @pl.when(s < nc)
        def _():
            acc_ref[...] += jnp.dot(hb, dG_ref[0], preferred_element_type=F32)

        @pl.when(s >= nc)
        def _():
            acc_ref[...] += jnp.dot(hb, dU_ref[0], preferred_element_type=F32)

        @pl.when(t == nt - 1)
        def _():
            o_ref[0] = acc_ref[...].astype(o_ref.dtype)

    return _hosted_call(
        hosted, body, grid=(2 * nc, nt),
        in_specs=[pl.BlockSpec((D, T), lambda s, t: (0, 0)),
                  pl.BlockSpec((1, tt, fc), lambda s, t: (jnp.minimum(s, nc - 1), jnp.where(s < nc, t, nt - 1), 0)),
                  pl.BlockSpec((1, tt, fc), lambda s, t: (jnp.maximum(s - nc, 0), jnp.where(s >= nc, t, 0), 0))],
        out_specs=pl.BlockSpec((1, D, fc), lambda s, t: (s, 0, 0)),
        out_shape=jax.ShapeDtypeStruct((2 * nc, D, fc), BF16),
        scratch_shapes=[pltpu.VMEM((D, fc), F32)],
        compiler_params=_cp(("parallel", "arbitrary")), name=name)(h_t, dG, dU)


def ffn_dw_out(act, dz, name, hosted=None):
    nc, T, fc = act.shape
    D = dz.shape[1]
    half = fc // 2
    tt = _pick(T, 512, 16)
    nt = T // tt

    def body(a_ref, dz_ref, o_ref, acc_ref):
        t = pl.program_id(1)

        @pl.when(t == 0)
        def _():
            acc_ref[...] = jnp.zeros_like(acc_ref)

        acc_ref[...] += _bdot_tn(a_ref[0], dz_ref[...])

        @pl.when(t == nt - 1)
        def _():
            o_ref[...] = (0.5 * acc_ref[...]).reshape(2, half, D).astype(o_ref.dtype)

    return _hosted_call(
        hosted, body, grid=(nc, nt),
        in_specs=[pl.BlockSpec((1, tt, fc), lambda c, t: (c, t, 0)), pl.BlockSpec((tt, D), lambda c, t: (t, 0))],
        out_specs=pl.BlockSpec((2, half, D), lambda c, t: (c, 0, 0)),
        out_shape=jax.ShapeDtypeStruct((2 * nc, half, D), BF16),
        scratch_shapes=[pltpu.VMEM((fc, D), F32)],
        compiler_params=_cp(("parallel", "arbitrary")), name=name)(act, dz)


def proj_res_ln(parts, w, res, g, b, name):
    T, D = res.shape
    tm = _pick(T, 512, 8)
    widths = [p.shape[1] for p in parts]
    offs = [int(sum(widths[:i])) for i in range(len(parts))]
    n = len(parts)

    def body(*refs):
        p_refs = refs[:n]
        w_ref, r_ref, g_ref, b_ref, out_ref, xh_ref, rs_ref, ob_ref = refs[n:]
        acc = ALPHA * r_ref[...]
        for p_ref, o, wd in zip(p_refs, offs, widths):
            acc = acc + _bdot(p_ref[...], w_ref[o:o + wd, :])
        out, xh, rs = _ln_apply(acc, g_ref[...], b_ref[...])
        out_ref[...] = out
        ob_ref[...] = out.astype(BF16)
        xh_ref[...] = xh
        rs_ref[...] = rs

    row = pl.BlockSpec((tm, D), lambda i: (i, 0))
    vec = pl.BlockSpec((1, D), lambda i: (0, 0))
    return pl.pallas_call(
        body, grid=(T // tm,),
        in_specs=[pl.BlockSpec((tm, wd), lambda i: (i, 0)) for wd in widths]
        + [pl.BlockSpec(w.shape, lambda i: (0, 0)), row, vec, vec],
        out_specs=[row, row, pl.BlockSpec((tm, 1), lambda i: (i, 0)), row],
        out_shape=[jax.ShapeDtypeStruct((T, D), F32), jax.ShapeDtypeStruct((T, D), F32), jax.ShapeDtypeStruct((T, 1), F32),
                   jax.ShapeDtypeStruct((T, D), BF16)],
        compiler_params=_cp(("parallel",)), name=name)(*parts, w, res, g, b)


def ple_fwd(h, p, wg, wp, name):
    T, D = h.shape
    P = p.shape[1]
    tm, tn = _pick(T, 512, 8), _pick(D, 512, LANES)

    def body(h_ref, hn_ref, p_ref, wg_ref, wp_ref, out_ref, a_ref, e_ref):
        a = _bdot(h_ref[...], wg_ref[...])
        e = _bdot(p_ref[...], wp_ref[...])
        a_ref[...] = a
        e_ref[...] = e
        out_ref[...] = hn_ref[...] + _sigmoid(a) * e

    blk = pl.BlockSpec((tm, tn), lambda i, j: (i, j))
    sds = jax.ShapeDtypeStruct((T, D), F32)
    return pl.pallas_call(
        body, grid=(T // tm, D // tn),
        in_specs=[pl.BlockSpec((tm, D), lambda i, j: (i, 0)), blk, pl.BlockSpec((tm, P), lambda i, j: (i, 0)),
                  pl.BlockSpec((D, tn), lambda i, j: (0, j)), pl.BlockSpec((P, tn), lambda i, j: (0, j))],
        out_specs=[blk, blk, blk], out_shape=[sds, sds, sds],
        compiler_params=_cp(("parallel", "parallel")), name=name)(h, h, p, wg, wp)


def ple_bwd(dout, a, e, wg, name):
    T, D = dout.shape
    tm = _pick(T, 512, 16)

    def body(do_ref, a_ref, e_ref, wg_ref, dh_ref, da_ref, de_ref):
        do = do_ref[...]
        s = _sigmoid(a_ref[...])
        da = (do * e_ref[...] * s * (1.0 - s)).astype(BF16)
        da_ref[...] = da
        de_ref[...] = (do * s).astype(BF16)
        dh_ref[...] = do + _bdot_nt(da, wg_ref[...])

    row = pl.BlockSpec((tm, D), lambda i: (i, 0))
    return pl.pallas_call(
        body, grid=(T // tm,),
        in_specs=[row, row, row, pl.BlockSpec((D, D), lambda i: (0, 0))],
        out_specs=[row, row, row],
        out_shape=[jax.ShapeDtypeStruct((T, D), F32), jax.ShapeDtypeStruct((T, D), BF16), jax.ShapeDtypeStruct((T, D), BF16)],
        compiler_params=_cp(("parallel",)), name=name)(dout, a, e, wg)


def loss_head(y, target, name):
    T, D = y.shape
    tm = _pick(T, 512, 8)

    def body(y_ref, t_ref, loss_ref, dy_ref):
        i = pl.program_id(0)

        @pl.when(i == 0)
        def _():
            loss_ref[...] = jnp.zeros_like(loss_ref)

        err = y_ref[...] - t_ref[...]
        dy_ref[...] = err * (1.0 / D)
        per_tok = jnp.sum(err * err, axis=-1, keepdims=True) * (1.0 / D)
        loss_ref[...] += 0.5 * jnp.sum(per_tok, axis=0, keepdims=True)

    row = pl.BlockSpec((tm, D), lambda i: (i, 0))
    return pl.pallas_call(
        body, grid=(T // tm,), in_specs=[row, row],
        out_specs=[pl.BlockSpec((1, 1), lambda i: (0, 0)), row],
        out_shape=[jax.ShapeDtypeStruct((1, 1), F32), jax.ShapeDtypeStruct((T, D), F32)],
        compiler_params=_cp(("arbitrary",)), name=name)(y, target)


HALO = 8


def _conv_taps(pad_ref, w_ref, tm, base):
    acc = w_ref[0:1, :] * pad_ref[pl.ds(base, tm), :]
    for k in range(1, GDN_CONV):
        acc = acc + w_ref[k:k + 1, :] * pad_ref[pl.ds(base + k, tm), :]
    return acc


GDN_GROUP_W = GDN_W
GDN_PRE_ROWS = 512


def _head_segments():
    head = jnp.arange(GDN_W, dtype=jnp.int32) // GDN_D
    return (head[:, None] == head[None, :]).astype(BF16)


def _head_sums(x, seg):
    hi = x.astype(BF16)
    r1 = x - hi.astype(F32)
    mid = r1.astype(BF16)
    lo = (r1 - mid.astype(F32)).astype(BF16)
    d = functools.partial(jnp.dot, preferred_element_type=F32)
    return d(hi, seg) + d(mid, seg) + d(lo, seg)


def _gdn_pre_common(x_ref, halo_ref, w_ref, seg_ref, pad_ref, tm):
    i = pl.program_id(1)
    grp = pl.program_id(0)
    pad_ref[0:HALO, :] = jnp.where(i == 0, 0.0, halo_ref[...])
    pad_ref[HALO:HALO + tm, :] = x_ref[...]
    c = _conv_taps(pad_ref, w_ref, tm, HALO - (GDN_CONV - 1))
    s = _sigmoid(c)
    y = c * s
    r = lax.rsqrt(_head_sums(y * y, seg_ref[...]) + RMS_EPS)
    scale = jnp.where(grp < 1, GDN_D ** -0.5, 1.0)
    return grp < 2, c, s, y, r, scale


def gdn_pre_fwd(proj, conv_w, name):
    T = proj.shape[0]
    tm = _pick(T, GDN_PRE_ROWS, 8)
    GW = GDN_GROUP_W

    def body(x_ref, halo_ref, w_ref, seg_ref, o_ref, pad_ref):
        normed, c, s, y, r, scale = _gdn_pre_common(x_ref, halo_ref, w_ref, seg_ref, pad_ref, tm)
        o_ref[...] = jnp.where(normed, y * r * scale, y)

    return pl.pallas_call(
        body, grid=(3, T // tm),
        in_specs=[pl.BlockSpec((tm, GW), lambda hb, i: (i, hb)),
                  pl.BlockSpec((HALO, GW), lambda hb, i: (jnp.maximum(i * (tm // HALO) - 1, 0), hb)),
                  pl.BlockSpec((GDN_CONV, GW), lambda hb, i: (0, hb)), pl.BlockSpec((GW, GW), lambda hb, i: (0, 0))],
        out_specs=pl.BlockSpec((tm, GW), lambda hb, i: (i, hb)),
        out_shape=jax.ShapeDtypeStruct((T, 3 * GW), F32),
        scratch_shapes=[pltpu.VMEM((tm + HALO, GW), F32)],
        compiler_params=_cp(("parallel", "parallel")), name=name)(proj, proj, conv_w, _head_segments())


def gdn_pre_bwd_pointwise(proj, conv_w, dqkv, name, hosted=None):
    T = proj.shape[0]
    tm = _pick(T, GDN_PRE_ROWS, 8)
    GW = GDN_GROUP_W

    def body(x_ref, halo_ref, w_ref, seg_ref, d_ref, dc_ref, dw_ref, pad_ref):
        i = pl.program_id(1)
        normed, c, s, y, r, scale = _gdn_pre_common(x_ref, halo_ref, w_ref, seg_ref, pad_ref, tm)

        @pl.when(i == 0)
        def _():
            dw_ref[...] = jnp.zeros_like(dw_ref)

        d = d_ref[...]
        n = y * r
        dn = d * scale
        dy = jnp.where(normed, r * (dn - n * _head_sums(dn * n, seg_ref[...])), d)
        dc = dy * (s * (1.0 + c * (1.0 - s)))
        dc_ref[...] = dc
        for k in range(GDN_CONV):
            xs = pad_ref[pl.ds(HALO - (GDN_CONV - 1) + k, tm), :]
            dw_ref[k:k + 1, :] += jnp.sum(dc * xs, axis=0, keepdims=True)

    blk = pl.BlockSpec((tm, GW), lambda hb, i: (i, hb))
    wblk = pl.BlockSpec((GDN_CONV, GW), lambda hb, i: (0, hb))
    return _hosted_call(
        hosted, body, grid=(3, T // tm),
        in_specs=[blk, pl.BlockSpec((HALO, GW), lambda hb, i: (jnp.maximum(i * (tm // HALO) - 1, 0), hb)), wblk,
                  pl.BlockSpec((GW, GW), lambda hb, i: (0, 0)), blk],
        out_specs=[blk, wblk],
        out_shape=[jax.ShapeDtypeStruct((T, 3 * GW), F32), jax.ShapeDtypeStruct((GDN_CONV, 3 * GW), F32)],
        scratch_shapes=[pltpu.VMEM((tm + HALO, GW), F32)],
        compiler_params=_cp(("parallel", "arbitrary")), name=name)(proj, proj, conv_w, _head_segments(), dqkv)


def gdn_pre_bwd_conv(dc, conv_w_p, name):
    T = dc.shape[0]
    tm = _pick(T, GDN_PRE_ROWS, 8)
    nt = T // tm
    GW = GDN_GROUP_W

    def body(dc_ref, halo_ref, w_ref, dx_ref, pad_ref):
        i = pl.program_id(1)
        pad_ref[0:tm, :] = dc_ref[...]
        pad_ref[tm:tm + HALO, :] = jnp.where(i == nt - 1, 0.0, halo_ref[...])
        acc = w_ref[GDN_CONV - 1:GDN_CONV, :] * pad_ref[pl.ds(0, tm), :]
        for k in range(GDN_CONV - 1):
            acc = acc + w_ref[k:k + 1, :] * pad_ref[pl.ds(GDN_CONV - 1 - k, tm), :]
        dx_ref[...] = acc

    blk = pl.BlockSpec((tm, GW), lambda hb, i: (i, hb))
    return pl.pallas_call(
        body, grid=(3, nt),
        in_specs=[blk, pl.BlockSpec((HALO, GW), lambda hb, i: (jnp.minimum((i + 1) * (tm // HALO), T // HALO - 1), hb)),
                  pl.BlockSpec((GDN_CONV, GW), lambda hb, i: (0, hb))],
        out_specs=blk,
        out_shape=jax.ShapeDtypeStruct((T, 3 * GW), F32),
        scratch_shapes=[pltpu.VMEM((tm + HALO, GW), F32)],
        compiler_params=_cp(("parallel", "parallel")), name=name)(dc, dc, conv_w_p)


def _chunk_masks(C):
    row = _iota2((C, C), 0)
    col = _iota2((C, C), 1)
    return row >= col, row > col, row == col


GDN_FWD_CHUNKS = 4
BNN = (((2,), (1,)), ((0,), (0,)))
BNT = (((2,), (2,)), ((0,), (0,)))
BTN = (((1,), (1,)), ((0,), (0,)))


def _bmm(a, b, dims=BNN):
    return lax.dot_general(a.astype(BF16), b.astype(BF16), dims, preferred_element_type=F32)


def _hbmm(a, b):
    m = a.shape[1]
    a_hi, a_lo = _split2(a)
    b_hi, b_lo = _split2(b)
    r = lax.dot_general(jnp.concatenate([a_hi, a_lo], axis=1), b_hi, BNN, preferred_element_type=F32)
    return r[:, :m] + r[:, m:] + lax.dot_general(a_hi, b_lo, BNN, preferred_element_type=F32)


def _hbmm_tn(a, b):
    a_hi, a_lo = _split2(a)
    b_hi, b_lo = _split2(b)
    d = functools.partial(lax.dot_general, dimension_numbers=BTN, preferred_element_type=F32)
    return d(a_hi, b_hi) + d(a_lo, b_hi) + d(a_hi, b_lo)


def _col_to_row(colv, eye):
    return jnp.sum(jnp.where(eye, colv, 0.0), axis=1, keepdims=True)


def _row_to_col(rowv, eye):
    return jnp.sum(jnp.where(eye, rowv, 0.0), axis=2, keepdims=True)


def _unit_lower_inverse(A, eye):
    C = A.shape[1]
    P = jnp.where(eye, 1.0, 0.0) - A
    Bp = _hbmm(A, A)
    for _ in range(4):
        R = _hbmm(jnp.concatenate([Bp, P], axis=1), Bp)
        Bp = R[:, :C]
        P = P + R[:, C:]
    return P + _hbmm(P, Bp)


def _stack_heads(ref, first_head, n, row0=0):
    rows = pl.ds(row0, GDN_CHUNK)
    return jnp.stack([ref[rows, pl.ds((first_head + h) * GDN_D, GDN_D)] for h in range(n)])


def _unstack_heads(ref, first_head, val, row0=0):
    rows = pl.ds(row0, GDN_CHUNK)
    for h in range(val.shape[0]):
        ref[rows, pl.ds((first_head + h) * GDN_D, GDN_D)] = val[h]


def _gdn_gates(gab, a_row, dt_row, incl):
    g_all = -jnp.exp(a_row) * _softplus(gab + dt_row)
    beta_all = _sigmoid(gab)
    gc_all = _ones_dot_left(incl.astype(BF16), g_all)
    return g_all, beta_all, gc_all


def _gdn_common(qkv_ref, gc_all, beta_all, incl, strict, eye, row0=0):
    C, H = GDN_CHUNK, GDN_HEADS
    q, k, v = (_stack_heads(qkv_ref, j * H, H, row0) for j in range(3))
    gc = jnp.stack([gc_all[:, h:h + 1] for h in range(H)])
    beta = jnp.stack([beta_all[:, H + h:H + h + 1] for h in range(H)])
    gc_row = _col_to_row(gc, eye)
    decay = jnp.where(incl, jnp.exp(jnp.where(incl, gc - gc_row, 0.0)), 0.0)
    e_gc = jnp.exp(gc)
    gl = gc[:, C - 1:C, :]
    e_gl = jnp.exp(gl)
    ekd = jnp.exp(gl - gc)
    kb = k * beta
    A = jnp.where(strict, _bmm(kb, k, BNT) * decay, 0.0)
    Pm = jnp.where(incl, _bmm(q, k, BNT) * decay, 0.0)
    return q, k, v, gc, beta, decay, e_gc, e_gl, ekd, kb, A, Pm


def gdn_chunk_fwd(qkv, proj, a_row, dt_row, norm_w, name, hosted=None):
    T = qkv.shape[0]
    C, H, Dh = GDN_CHUNK, GDN_HEADS, GDN_D
    N = T // C
    J = GDN_FWD_CHUNKS if N % GDN_FWD_CHUNKS == 0 else 1

    def body(qkv_ref, gz_ref, gab_ref, a_ref, dt_ref, nw_ref, o_ref, opre_ref, Tm_ref, Sin_ref, S_ref):
        n = pl.program_id(0)

        @pl.when(n == 0)
        def _():
            S_ref[...] = jnp.zeros_like(S_ref)

        incl, strict, eye = _chunk_masks(C)
        gab = gab_ref[...]
        per_chunk = []
        for j in range(J):
            _, beta_all, gc_all = _gdn_gates(gab[j * C:(j + 1) * C], a_ref[...], dt_ref[...], incl)
            per_chunk.append(_gdn_common(qkv_ref, gc_all, beta_all, incl, strict, eye, row0=j * C))
        q, k, v, gc, beta, decay, e_gc, e_gl, ekd, kb, A, Pm = (jnp.concatenate(t, axis=0) for t in zip(*per_chunk))
        Tm = _unit_lower_inverse(A, eye)
        u = _hbmm(Tm, v * beta)
        w = _hbmm(Tm, kb * e_gc)
        qd = q * e_gc
        kd = k * ekd
        S = S_ref[...]
        for j in range(J):
            hs = slice(j * H, (j + 1) * H)
            v_new = u[hs] - _bmm(w[hs], S)
            o = _bmm(qd[hs], S) + _bmm(Pm[hs], v_new)
            Sin_ref[j] = S
            Tm_ref[j] = Tm[hs]
            S = S * e_gl[hs] + _bmm(kd[hs], v_new, BTN)
            r = lax.rsqrt(jnp.mean(o * o, axis=-1, keepdims=True) + RMS_EPS)
            gz = _stack_heads(gz_ref, 0, H, j * C)
            _unstack_heads(opre_ref, 0, o, j * C)
            _unstack_heads(o_ref, 0, o * r * nw_ref[...] * (gz * _sigmoid(gz)), j * C)
        S_ref[...] = S

    vec = pl.BlockSpec((1, LANES), lambda n: (0, 0))
    hblk = pl.BlockSpec((J * C, GDN_W), lambda n: (n, 0))
    sblk = pl.BlockSpec((J, H, Dh, Dh), lambda n: (n, 0, 0, 0))
    return _hosted_call(
        hosted, body, grid=(N // J,),
        in_specs=[pl.BlockSpec((J * C, 3 * GDN_W), lambda n: (n, 0)),
                  pl.BlockSpec((J * C, GDN_W), lambda n: (n, CB_GZ * LANES // GDN_W)),
                  pl.BlockSpec((J * C, LANES), lambda n: (n, CB_GAB)), vec, vec, pl.BlockSpec((1, Dh), lambda n: (0, 0))],
        out_specs=[hblk, hblk, sblk, sblk],
        out_shape=[jax.ShapeDtypeStruct((T, GDN_W), F32), jax.ShapeDtypeStruct((T, GDN_W), F32),
                   jax.ShapeDtypeStruct((N, H, Dh, Dh), F32), jax.ShapeDtypeStruct((N, H, Dh, Dh), F32)],
        scratch_shapes=[pltpu.VMEM((H, Dh, Dh), F32)],
        compiler_params=_cp(("arbitrary",)), name=name)(qkv, proj, proj, a_row, dt_row, norm_w)


def gdn_chunk_bwd(qkv, proj, a_row, dt_row, norm_w, opre, Tm_all, Sin_all, docat, name, hosted=None):
    T = qkv.shape[0]
    C, H, Dh = GDN_CHUNK, GDN_HEADS, GDN_D
    N = T // C

    def body(qkv_ref, gz_ref, gab_ref, a_ref, dt_ref, nw_ref, opre_ref, Tm_ref, Sin_ref, do_ref,
             dqkv_ref, dgz_ref, dgab_ref, da_ref, ddt_ref, dnw_ref, dS_ref):
        n = pl.program_id(0)

        @pl.when(n == 0)
        def _():
            dS_ref[...] = jnp.zeros_like(dS_ref)
            da_ref[...] = jnp.zeros_like(da_ref)
            ddt_ref[...] = jnp.zeros_like(ddt_ref)
            dnw_ref[...] = jnp.zeros_like(dnw_ref)

        incl, strict, eye = _chunk_masks(C)
        gab = gab_ref[...]
        g_all, beta_all, gc_all = _gdn_gates(gab, a_ref[...], dt_ref[...], incl)
        lane = _iota2((C, LANES), 1)
        rowi = _iota2((C, 1), 0)
        nw = nw_ref[...]
        q, k, v, gc, beta, decay, e_gc, e_gl, ekd, kb, A, Pm = _gdn_common(qkv_ref, gc_all, beta_all, incl, strict, eye)
        Tm = Tm_ref[0]
        S = Sin_ref[0]
        dS = dS_ref[...]
        kbe = kb * e_gc
        u = _hbmm(Tm, v * beta)
        w = _hbmm(Tm, kbe)
        qd = q * e_gc
        kd = k * ekd
        v_new = u - _bmm(w, S)
        o = _stack_heads(opre_ref, 0, H)
        gz = _stack_heads(gz_ref, 0, H)
        don = _stack_heads(do_ref, 0, H)
        r = lax.rsqrt(jnp.mean(o * o, axis=-1, keepdims=True) + RMS_EPS)
        nn = o * r
        sgz = _sigmoid(gz)
        silu = gz * sgz
        _unstack_heads(dgz_ref, 0, don * nn * nw * (sgz * (1.0 + gz * (1.0 - sgz))))
        dnn = don * nw * silu
        dnw_ref[...] += jnp.sum(jnp.sum(don * nn * silu, axis=0), axis=0, keepdims=True)
        do = r * (dnn - nn * jnp.mean(dnn * nn, axis=-1, keepdims=True))
        dv_new = _bmm(Pm, do, BTN) + _bmm(kd, dS)
        dPm = jnp.where(incl, _bmm(do, v_new, BNT), 0.0)
        dqd = _bmm(do, S, BNT)
        dkd = _bmm(v_new, dS, BNT)
        dS_ref[...] = _bmm(qd, do, BTN) + e_gl * dS - _bmm(w, dv_new, BTN)
        dgl = jnp.sum(jnp.sum(dS * S, axis=2, keepdims=True), axis=1, keepdims=True) * e_gl
        dw = -_bmm(dv_new, S, BNT)
        dvb = _hbmm_tn(Tm, dv_new)
        dkbe = _hbmm_tn(Tm, dw)
        dA = -jnp.where(strict, _bmm(dvb, u, BNT) + _bmm(dkbe, w, BNT), 0.0)
        dAD = dA * decay
        dPD = dPm * decay
        Gm = dA * A + dPm * Pm
        dgc = jnp.sum(Gm, axis=2, keepdims=True) - _row_to_col(jnp.sum(Gm, axis=1, keepdims=True), eye)
        dkb = _bmm(dAD, k) + dkbe * e_gc
        dk = _bmm(dAD, kb, BTN) + _bmm(dPD, q, BTN) + dkd * ekd + dkb * beta
        dq = _bmm(dPD, k) + dqd * e_gc
        tkd = jnp.sum(dkd * kd, axis=-1, keepdims=True)
        dgc = dgc + jnp.sum(dqd * qd, axis=-1, keepdims=True) - tkd + jnp.sum(dkbe * kbe, axis=-1, keepdims=True)
        dgl = dgl + jnp.sum(tkd, axis=1, keepdims=True)
        dgc = dgc + jnp.where(rowi == C - 1, dgl, 0.0)
        dbeta = jnp.sum(dvb * v, axis=-1, keepdims=True) + jnp.sum(dkb * k, axis=-1, keepdims=True)
        _unstack_heads(dqkv_ref, 0, dq)
        _unstack_heads(dqkv_ref, H, dk)
        _unstack_heads(dqkv_ref, 2 * H, dvb * beta)
        dgc_all = jnp.zeros((C, LANES), F32)
        dbeta_all = jnp.zeros((C, LANES), F32)
        for h in range(H):
            dgc_all = dgc_all + jnp.where(lane == h, dgc[h], 0.0)
            dbeta_all = dbeta_all + jnp.where(lane == H + h, dbeta[h], 0.0)
        upper = (_iota2((C, C), 0) <= _iota2((C, C), 1)).astype(BF16)
        dg_all = _ones_dot_left(upper, dgc_all)
        dga = dg_all * (-jnp.exp(a_ref[...])) * _sigmoid(gab + dt_ref[...])
        dgb = dbeta_all * beta_all * (1.0 - beta_all)
        dgab_ref[...] = jnp.where(lane < H, dga, jnp.where(lane < 2 * H, dgb, 0.0))
        da_ref[...] += jnp.sum(jnp.where(lane < H, dg_all * g_all, 0.0), axis=0, keepdims=True)
        ddt_ref[...] += jnp.sum(jnp.where(lane < H, dga, 0.0), axis=0, keepdims=True)

    rev = lambda n: N - 1 - n
    vec = pl.BlockSpec((1, LANES), lambda n: (0, 0))
    nwv = pl.BlockSpec((1, Dh), lambda n: (0, 0))
    hblk = pl.BlockSpec((C, GDN_W), lambda n: (rev(n), 0))
    sblk = pl.BlockSpec((1, H, Dh, Dh), lambda n: (rev(n), 0, 0, 0))
    qblk = pl.BlockSpec((C, 3 * GDN_W), lambda n: (rev(n), 0))
    return _hosted_call(
        hosted, body, grid=(N,),
        in_specs=[qblk, pl.BlockSpec((C, GDN_W), lambda n: (rev(n), CB_GZ * LANES // GDN_W)),
                  pl.BlockSpec((C, LANES), lambda n: (rev(n), CB_GAB)), vec, vec, nwv, hblk, sblk, sblk, hblk],
        out_specs=[qblk, hblk, pl.BlockSpec((C, LANES), lambda n: (rev(n), 0)), vec, vec, nwv],
        out_shape=[jax.ShapeDtypeStruct((T, 3 * GDN_W), F32), jax.ShapeDtypeStruct((T, GDN_W), F32),
                   jax.ShapeDtypeStruct((T, LANES), F32), jax.ShapeDtypeStruct((1, LANES), F32),
                   jax.ShapeDtypeStruct((1, LANES), F32), jax.ShapeDtypeStruct((1, Dh), F32)],
        scratch_shapes=[pltpu.VMEM((H, Dh, Dh), F32)],
        compiler_params=_cp(("arbitrary",)), name=name)(qkv, proj, proj, a_row, dt_row, norm_w, opre, Tm_all, Sin_all, docat)


ATT_BQ, ATT_BK = 512, 1024
NEG_BIG = -1e30


def _att_blocks(T):
    bq, bk = min(ATT_BQ, T), min(ATT_BK, T)
    assert bk % bq == 0 and T % bk == 0
    return bq, bk


def _att_specs(T, bq, cbs):
    qspec = lambda cb: pl.BlockSpec((bq, LANES), lambda h, i: (i, cb + h))
    kspec = lambda cb: pl.BlockSpec((T, LANES), lambda h, i: (0, cb + h))
    return qspec, kspec


def _kblock(ref, kb, bk):
    return ref[pl.ds(pl.multiple_of(kb * bk, bk), bk), :]


def _att_pos(i, kb, bq, bk):
    qpos = i * bq + _iota2((bq, bk), 0)
    kpos = kb * bk + _iota2((bq, bk), 1)
    return qpos, kpos


def _later_keys(n):
    return (_iota2((n, n), 0) > _iota2((n, n), 1)).astype(BF16)


def _earlier_keys(n):
    return (_iota2((n, n), 0) < _iota2((n, n), 1)).astype(BF16)


def _tri_dot(x, tri, terms):
    acc, rest = None, x
    for t in range(terms):
        part = rest.astype(BF16)
        if t + 1 < terms:
            rest = rest - part.astype(F32)
        d = jnp.dot(part, tri, preferred_element_type=F32)
        acc = d if acc is None else acc + d
    return acc


SB_BLOCK = 256
SB_DEAD = -104.0


def _sb_blocks(T):
    b = min(SB_BLOCK, T)
    assert T % b == 0 and T // b <= LANES
    return b, b


def sb_fwd(proj, name, hosted=None):
    T = proj.shape[0]
    H = SB_HEADS
    bq, bk = _sb_blocks(T)
    scale = SB_DIM ** -0.5

    def body(q_ref, k_ref, v_ref, o_ref, tot_ref):
        i = pl.program_id(1)
        qb = q_ref[...].astype(BF16)
        diag = (i * bq) // bk
        lane = _iota2((bq, LANES), 1)
        later = _later_keys(bk)

        def block(kb, acc, R, masked):
            z = _bdot_nt(qb, _kblock(k_ref, kb, bk)) * scale
            sp = _softplus(z)
            if masked:
                qpos, kpos = _att_pos(i, kb, bq, bk)
                mask = kpos < qpos
                l1m = jnp.where(mask, -sp, 0.0)
            else:
                l1m = -sp
            W = jnp.exp((z - sp) + _tri_dot(l1m, later, 3) + R)
            if masked:
                W = jnp.where(mask, W, 0.0)
            acc = acc + _bdot(W, _kblock(v_ref, kb, bk))
            return acc, R + jnp.sum(l1m, axis=-1, keepdims=True)

        acc, R = block(diag, jnp.zeros((bq, LANES), F32), jnp.zeros((bq, 1), F32), True)

        def live(c):
            return jnp.logical_and(c[0] >= 0, jnp.max(c[2]) > SB_DEAD)

        def step(c):
            kb, acc, R, Rb = c
            acc, R_next = block(kb, acc, R, False)
            return kb - 1, acc, R_next, jnp.where(lane == kb, R, Rb)

        _, acc, _, Rb = lax.while_loop(live, step, (diag - 1, acc, R, jnp.where(lane == diag, 0.0, NEG_BIG)))
        o_ref[...] = acc
        tot_ref[...] = Rb

    qspec, kspec = _att_specs(T, bq, None)
    sds = jax.ShapeDtypeStruct((T, H * LANES), F32)
    oblk = pl.BlockSpec((bq, LANES), lambda h, i: (i, h))
    return _hosted_call(
        hosted, body, grid=(H, T // bq), in_specs=[qspec(CB_SQ), kspec(CB_SK), kspec(CB_SV)],
        out_specs=[oblk, oblk], out_shape=[sds, sds],
        compiler_params=_cp(("parallel", "parallel")), name=name)(proj, proj, proj)


def sb_bwd(proj, tot, docat, do_cb, name):
    T = proj.shape[0]
    H = SB_HEADS
    bq, bk = _sb_blocks(T)
    scale = SB_DIM ** -0.5

    def body(q_ref, k_ref, v_ref, tot_ref, do_ref, dq_ref, dk_ref, dv_ref):
        i = pl.program_id(1)

        @pl.when(i == 0)
        def _():
            dk_ref[...] = jnp.zeros_like(dk_ref)
            dv_ref[...] = jnp.zeros_like(dv_ref)

        qb = q_ref[...].astype(BF16)
        dob = do_ref[...].astype(BF16)
        Rb = tot_ref[...]
        diag = (i * bq) // bk
        lane = _iota2((bq, LANES), 1)
        later, earlier = _later_keys(bk), _earlier_keys(bk)
        first = lax.while_loop(
            lambda kb: jnp.logical_and(kb < diag, jnp.max(jnp.where(lane == kb, Rb, NEG_BIG)) <= SB_DEAD),
            lambda kb: kb + 1, jnp.int32(0))

        def block(kb, carry, masked):
            dq, Epre = carry
            R = jnp.sum(jnp.where(lane == kb, Rb, 0.0), axis=1, keepdims=True)
            kblk = _kblock(k_ref, kb, bk).astype(BF16)
            z = _bdot_nt(qb, kblk) * scale
            sp = _softplus(z)
            if masked:
                qpos, kpos = _att_pos(i, kb, bq, bk)
                mask = kpos < qpos
                l1m = jnp.where(mask, -sp, 0.0)
            else:
                l1m = -sp
            W = jnp.exp((z - sp) + _tri_dot(l1m, later, 3) + R)
            if masked:
                W = jnp.where(mask, W, 0.0)
            E = _bdot_nt(dob, _kblock(v_ref, kb, bk)) * W
            cexcl = _tri_dot(E, earlier, 3) + Epre
            neg = jnp.exp(-sp)
            dz = E * neg - cexcl * (1.0 - neg)
            if masked:
                dz = jnp.where(mask, dz, 0.0)
            dz = (dz * scale).astype(BF16)
            rows = pl.ds(pl.multiple_of(kb * bk, bk), bk)
            dk_ref[rows, :] += lax.dot_general(dz, qb, TN_DIMS, preferred_element_type=F32)
            dv_ref[rows, :] += lax.dot_general(W.astype(BF16), dob, TN_DIMS, preferred_element_type=F32)
            dq = dq + jnp.dot(dz, kblk, preferred_element_type=F32)
            return dq, Epre + jnp.sum(E, axis=-1, keepdims=True)

        init = (jnp.zeros((bq, LANES), F32), jnp.zeros((bq, 1), F32))
        carry = lax.fori_loop(first, diag, lambda kb, c: block(kb, c, False), init)
        dq, _ = block(diag, carry, True)
        dq_ref[...] = dq

    qspec, kspec = _att_specs(T, bq, None)
    sds = jax.ShapeDtypeStruct((T, H * LANES), F32)
    oblk = pl.BlockSpec((bq, LANES), lambda h, i: (i, h))
    kout = pl.BlockSpec((T, LANES), lambda h, i: (0, h))
    return pl.pallas_call(
        body, grid=(H, T // bq),
        in_specs=[qspec(CB_SQ), kspec(CB_SK), kspec(CB_SV), oblk, qspec(do_cb)],
        out_specs=[oblk, kout, kout], out_shape=[sds, sds, sds],
        compiler_params=_cp(("arbitrary", "arbitrary")), name=name)(proj, proj, proj, tot, docat)


def mla_fwd(Q, K, V, name, hosted=None):
    T = Q.shape[0]
    H = MLA_HEADS
    bq, bk = _att_blocks(T)
    scale = (MLA_NOPE + MLA_ROPE) ** -0.5

    def body(q_ref, k_ref, v_ref, o_ref, lse_ref):
        i = pl.program_id(1)
        qb = q_ref[...]
        diag = (i * bq) // bk

        def block(kb, carry, masked):
            acc, m, l = carry
            s = _bdot_nt(qb, _kblock(k_ref, kb, bk)) * scale
            if masked:
                qpos, kpos = _att_pos(i, kb, bq, bk)
                s = jnp.where(kpos <= qpos, s, NEG_BIG)
            m_new = jnp.maximum(m, jnp.max(s, axis=-1, keepdims=True))
            p = jnp.exp(s - m_new)
            corr = jnp.exp(m - m_new)
            acc = corr * acc + _bdot(p, _kblock(v_ref, kb, bk))
            return acc, m_new, corr * l + jnp.sum(p, axis=-1, keepdims=True)

        init = (jnp.zeros((bq, LANES), F32), jnp.full((bq, 1), NEG_BIG, F32), jnp.zeros((bq, 1), F32))
        carry = lax.fori_loop(0, diag, lambda kb, c: block(kb, c, False), init)
        acc, m, l = block(diag, carry, True)
        o_ref[...] = acc / l
        lse_ref[...] = jnp.broadcast_to(m + jnp.log(l), (bq, LANES))

    qspec, kspec = _att_specs(T, bq, None)
    sds = jax.ShapeDtypeStruct((T, H * LANES), F32)
    oblk = pl.BlockSpec((bq, LANES), lambda h, i: (i, h))
    return _hosted_call(
        hosted, body, grid=(H, T // bq), in_specs=[qspec(0), kspec(0), kspec(0)],
        out_specs=[oblk, oblk], out_shape=[sds, sds],
        compiler_params=_cp(("parallel", "parallel")), name=name)(Q, K, V)


def mla_bwd(Q, K, V, o, lse, docat, do_cb, name, hosted=None):
    T = Q.shape[0]
    H = MLA_HEADS
    bq, bk = _att_blocks(T)
    scale = (MLA_NOPE + MLA_ROPE) ** -0.5

    def body(q_ref, k_ref, v_ref, o_ref, lse_ref, do_ref, dq_ref, dk_ref, dv_ref):
        i = pl.program_id(1)

        @pl.when(i == 0)
        def _():
            dk_ref[...] = jnp.zeros_like(dk_ref)
            dv_ref[...] = jnp.zeros_like(dv_ref)

        qb = q_ref[...]
        do = do_ref[...]
        dob = do.astype(BF16)
        delta = jnp.sum(do * o_ref[...], axis=-1, keepdims=True)
        lse = lse_ref[:, 0:1]

        diag = (i * bq) // bk

        def block(kb, dq, masked):
            kblk = _kblock(k_ref, kb, bk)
            s = _bdot_nt(qb, kblk) * scale
            if masked:
                qpos, kpos = _att_pos(i, kb, bq, bk)
                s = jnp.where(kpos <= qpos, s, NEG_BIG)
            p = jnp.exp(s - lse)
            dp = _bdot_nt(dob, _kblock(v_ref, kb, bk))
            ds = (p * (dp - delta) * scale).astype(BF16)
            rows = pl.ds(pl.multiple_of(kb * bk, bk), bk)
            dk_ref[rows, :] += lax.dot_general(ds, qb, TN_DIMS, preferred_element_type=F32)
            dv_ref[rows, :] += lax.dot_general(p.astype(BF16), dob, TN_DIMS, preferred_element_type=F32)
            return dq + jnp.dot(ds, kblk, preferred_element_type=F32)

        dq = lax.fori_loop(0, diag, lambda kb, c: block(kb, c, False), jnp.zeros((bq, LANES), F32))
        dq_ref[...] = block(diag, dq, True)

    qspec, kspec = _att_specs(T, bq, None)
    sds = jax.ShapeDtypeStruct((T, H * LANES), F32)
    oblk = pl.BlockSpec((bq, LANES), lambda h, i: (i, h))
    kout = pl.BlockSpec((T, LANES), lambda h, i: (0, h))
    return _hosted_call(
        hosted, body, grid=(H, T // bq),
        in_specs=[qspec(0), kspec(0), kspec(0), oblk, oblk, qspec(do_cb)],
        out_specs=[oblk, kout, kout], out_shape=[sds, sds, sds],
        compiler_params=_cp(("arbitrary", "arbitrary")), name=name)(Q, K, V, o, lse, docat)


def _tile_heads(t, n):
    return jnp.concatenate([t] * n, axis=1)


def _rope(X, C, Sn, Sp):
    n = X.shape[1]
    return X * C + pltpu.roll(X, n - HALF_ROPE, 1) * Sn + pltpu.roll(X, HALF_ROPE, 1) * Sp


def _rope_t(dO, C, Sn, Sp):
    n = dO.shape[1]
    return dO * C + pltpu.roll(dO * Sn, HALF_ROPE, 1) + pltpu.roll(dO * Sp, n - HALF_ROPE, 1)


def _rms(x, w):
    r = lax.rsqrt(jnp.mean(x * x, axis=-1, keepdims=True) + RMS_EPS)
    xh = x * r
    return r, xh, xh * w


def _rms_bwd(dn, w, r, xh):
    dxh = dn * w
    return r * (dxh - xh * jnp.mean(dxh * xh, axis=-1, keepdims=True)), jnp.sum(dn * xh, axis=0, keepdims=True)


def _mla_pre_specs(T, tm):
    KV = MLA_KV_RANK
    QR = MLA_Q_RANK
    W = MLA_HEADS * LANES
    full = lambda shape: pl.BlockSpec(shape, lambda i: (0, 0))
    specs = [pl.BlockSpec((tm, QR), lambda i: (i, CB_MQ * LANES // QR)),
             pl.BlockSpec((tm, 2 * LANES), lambda i: (i, CB_MKV // 2)),
             full((1, QR)), full((1, KV))]
    rope = [pl.BlockSpec((tm, LANES), lambda i: (i, 0))] * 3
    return specs, rope, full, W


def mla_pre_fwd(proj, wq, wkv, wuq, wuk, wuv, ropeC, ropeSn, ropeSp, name):
    T = proj.shape[0]
    tm = _pick(T, 512, 16)
    KV = MLA_KV_RANK
    H = MLA_HEADS

    def body(mq_ref, mkv_ref, wq_ref, wkv_ref, wuq_ref, wuk_ref, wuv_ref, c_ref, sn_ref, sp_ref, Q_ref, K_ref, V_ref):
        C, Sn, Sp = (_tile_heads(t[...], H) for t in (c_ref, sn_ref, sp_ref))
        _, _, qn = _rms(mq_ref[...], wq_ref[...])
        Q_ref[...] = _rope(_bdot(qn, wuq_ref[...]), C, Sn, Sp).astype(BF16)
        mkv = mkv_ref[...]
        _, _, kvn = _rms(mkv[:, :KV], wkv_ref[...])
        kr = pltpu.roll(mkv[:, KV:], MLA_NOPE, 1)
        K_ref[...] = _rope(_bdot(kvn, wuk_ref[...]) + _tile_heads(kr, H), C, Sn, Sp).astype(BF16)
        V_ref[...] = _bdot(kvn, wuv_ref[...]).astype(BF16)

    specs, rope, full, W = _mla_pre_specs(T, tm)
    oblk = pl.BlockSpec((tm, W), lambda i: (i, 0))
    sds = jax.ShapeDtypeStruct((T, W), BF16)
    return pl.pallas_call(
        body, grid=(T // tm,),
        in_specs=specs + [full(wuq.shape), full(wuk.shape), full(wuv.shape)] + rope,
        out_specs=[oblk, oblk, oblk], out_shape=[sds, sds, sds],
        compiler_params=_cp(("parallel",)), name=name)(proj, proj, wq, wkv, wuq, wuk, wuv, ropeC, ropeSn, ropeSp)


def mla_pre_bwd(proj, wq, wkv, wuq, wuk, wuv, ropeC, ropeSn, ropeSp, dQ, dK, dV, name):
    T = proj.shape[0]
    tm = _pick(T, 512, 16)
    KV = MLA_KV_RANK
    H = MLA_HEADS

    def body(mq_ref, mkv_ref, wq_ref, wkv_ref, wuq_ref, wuk_ref, wuv_ref,
             c_ref, sn_ref, sp_ref, dQ_ref, dK_ref, dV_ref,
             dmq_ref, dmkv_ref, dwuq_ref, dwuk_ref, dwuv_ref, dwq_ref, dwkv_ref):
        i = pl.program_id(0)

        @pl.when(i == 0)
        def _():
            for ref in (dwuq_ref, dwuk_ref, dwuv_ref, dwq_ref, dwkv_ref):
                ref[...] = jnp.zeros_like(ref)

        C, Sn, Sp = (_tile_heads(t[...], H) for t in (c_ref, sn_ref, sp_ref))
        rq, xq, qn = _rms(mq_ref[...], wq_ref[...])
        mkv = mkv_ref[...]
        rkv, xkv, kvn = _rms(mkv[:, :KV], wkv_ref[...])
        dqf = _rope_t(dQ_ref[...], C, Sn, Sp)
        dkf = _rope_t(dK_ref[...], C, Sn, Sp)
        dv = dV_ref[...]
        dwuq_ref[...] += _bdot_tn(qn, dqf)
        dwuk_ref[...] += _bdot_tn(kvn, dkf)
        dwuv_ref[...] += _bdot_tn(kvn, dv)
        dmq, dwq = _rms_bwd(_bdot_nt(dqf, wuq_ref[...]), wq_ref[...], rq, xq)
        dckv, dwkv = _rms_bwd(_bdot_nt(dkf, wuk_ref[...]) + _bdot_nt(dv, wuv_ref[...]), wkv_ref[...], rkv, xkv)
        dwq_ref[...] += dwq
        dwkv_ref[...] += dwkv
        dmq_ref[...] = dmq
        dkr = dkf[:, 0:LANES]
        for h in range(1, H):
            dkr = dkr + dkf[:, h * LANES:(h + 1) * LANES]
        dkr = pltpu.roll(dkr, LANES - MLA_NOPE, 1)
        dkr = jnp.where(_iota2(dkr.shape, 1) < MLA_ROPE, dkr, 0.0)
        dmkv_ref[...] = jnp.concatenate([dckv, dkr], axis=1)

    specs, rope, full, W = _mla_pre_specs(T, tm)
    wide = pl.BlockSpec((tm, W), lambda i: (i, 0))
    return pl.pallas_call(
        body, grid=(T // tm,),
        in_specs=specs + [full(w.shape) for w in (wuq, wuk, wuv)] + rope + [wide, wide, wide],
        out_specs=[pl.BlockSpec((tm, MLA_Q_RANK), lambda i: (i, 0)), pl.BlockSpec((tm, 2 * LANES), lambda i: (i, 0)),
                   full(wuq.shape), full(wuk.shape), full(wuv.shape), full((1, MLA_Q_RANK)), full((1, KV))],
        out_shape=[jax.ShapeDtypeStruct((T, MLA_Q_RANK), F32), jax.ShapeDtypeStruct((T, 2 * LANES), F32),
                   jax.ShapeDtypeStruct(wuq.shape, F32), jax.ShapeDtypeStruct(wuk.shape, F32),
                   jax.ShapeDtypeStruct(wuv.shape, F32), jax.ShapeDtypeStruct((1, MLA_Q_RANK), F32),
                   jax.ShapeDtypeStruct((1, KV), F32)],
        compiler_params=_cp(("arbitrary",)), name=name)(
            proj, proj, wq, wkv, wuq, wuk, wuv, ropeC, ropeSn, ropeSp, dQ, dK, dV)


def all_gather(shards, name):
    n = len(shards)

    def body(*refs):
        x_refs, out_refs = refs[:n], refs[n:2 * n]
        send_sems, recv_sems, local_sems = refs[2 * n:]
        x, y, c = _place()
        me, sibling = (x, y, c), (x, y, 1 - c)
        chips = [(1 - x, y), (x, 1 - y), (1 - x, 1 - y)]

        def slot(a, px, py, pc):
            return out_refs[a].at[4 * px + 2 * py + pc]

        def copy(a, k, block, to, src=None):
            return pltpu.make_async_remote_copy(
                src_ref=slot(a, *block) if src is None else src, dst_ref=slot(a, *block),
                send_sem=send_sems.at[a, k], recv_sem=recv_sems.at[a, k], device_id=to, device_id_type=MESH)

        mine = [pltpu.make_async_copy(x_refs[a], slot(a, *me), local_sems.at[a]) for a in range(n)]
        first = []
        for a in range(n):
            mine[a].start()
            first.append(copy(a, 0, me, sibling, src=x_refs[a]))
            first += [copy(a, 1 + j, me, (*chip, c), src=x_refs[a]) for j, chip in enumerate(chips)]
        for cp in first:
            cp.start()
        passed = []
        for j, chip in enumerate(chips):
            for a in range(n):
                copy(a, 1 + j, (*chip, c), me).wait_recv()
                passed.append(copy(a, 4 + j, (*chip, c), sibling))
                passed[-1].start()
        for a in range(n):
            copy(a, 0, sibling, me).wait_recv()
            for j, chip in enumerate(chips):
                copy(a, 4 + j, (*chip, 1 - c), me).wait_recv()
        for cp in first + passed:
            cp.wait_send()
        for cp in mine:
            cp.wait()

    return pl.pallas_call(
        body, out_shape=[jax.ShapeDtypeStruct((N_DEV,) + s.shape, s.dtype) for s in shards],
        in_specs=[ANY] * n, out_specs=[ANY] * n,
        scratch_shapes=[pltpu.SemaphoreType.DMA((n, 7)), pltpu.SemaphoreType.DMA((n, 7)), pltpu.SemaphoreType.DMA((n,))],
        name=name)(*shards)


def reduce_adamw(parts, w, m, v, name):
    L = len(parts)
    n, Rl, C = parts[0].shape
    R = w.shape[0]
    assert R == L * Rl
    tr = Rl if Rl * C <= 256 * 1024 else _pick(Rl, 256, 16)
    nr = Rl // tr

    def body(*refs):
        p_refs = refs[:L]
        w_ref, m_ref, v_ref, g_ref, d_ref, nm_ref, nv_ref, sum_ref = refs[L:]
        grp = pl.program_id(0)
        for j in range(L):
            @pl.when(grp == j)
            def _(j=j):
                acc = p_refs[j][0].astype(F32)
                for s in range(1, n):
                    acc = acc + p_refs[j][s].astype(F32)
                sum_ref[...] = acc

        g_ = sum_ref[...]
        m_ = ADAM_B1 * m_ref[...] + (1.0 - ADAM_B1) * g_
        v_ = ADAM_B2 * v_ref[...] + (1.0 - ADAM_B2) * (g_ * g_)
        m_hat = m_ / (1.0 - ADAM_B1 ** ADAM_STEP)
        v_hat = v_ / (1.0 - ADAM_B2 ** ADAM_STEP)
        g_ref[...] = g_
        d_ref[...] = -ADAM_LR * (m_hat / (jnp.sqrt(v_hat) + ADAM_EPS) + ADAM_WD * w_ref[...])
        nm_ref[...] = m_
        nv_ref[...] = v_

    blk = pl.BlockSpec((tr, C), lambda l, r: (l * nr + r, 0))
    sds = jax.ShapeDtypeStruct((R, C), F32)
    p_specs = [pl.BlockSpec((n, tr, C), lambda l, r, j=j: (0, jnp.where(l == j, r, 0), 0)) for j in range(L)]
    return pl.pallas_call(
        body, grid=(L, nr), in_specs=p_specs + [blk] * 3,
        out_specs=[blk] * 4, out_shape=[sds] * 4, scratch_shapes=[pltpu.VMEM((tr, C), F32)],
        compiler_params=_cp(("arbitrary", "arbitrary")), name=name)(*parts, w, m, v)


SHARDED = {"ffa_w_in": (2, BF16), "ffa_w_out": (1, BF16), "mix_w_in": (2, BF16), "mla_w_uq": (2, BF16),
           "mla_w_ukv": (2, BF16), "mix_w_o": (1, BF16), "ffb_w_in": (2, BF16), "ffb_w_out": (1, BF16),
           "ple_w_gate": (1, BF16), "ple_w_proj": (2, BF16), "gdn_conv_w": (2, F32), "ln_g": (2, F32), "ln_b": (2, F32)}
FFN_SLOT = ("ffa_w_in", "ffa_w_out", "ffb_w_in", "ffb_w_out")
REPLICATED = ("gdn_a_log", "gdn_dt_bias", "gdn_norm_w", "mla_q_norm_w", "mla_kv_norm_w")
WEIGHTS = ("ffa_w_in", "ffa_w_out", "mix_w_in", "gdn_conv_w", "gdn_a_log", "gdn_dt_bias", "gdn_norm_w", "mla_q_norm_w",
           "mla_kv_norm_w", "mla_w_uq", "mla_w_ukv", "mix_w_o", "ffb_w_in", "ffb_w_out", "ln_g", "ln_b", "ple_w_gate",
           "ple_w_proj")


def _to_slots(full, axis):
    L, a, b = full.shape
    if axis == 2:
        return full.reshape(L, a, N_DEV, b // N_DEV).transpose(2, 0, 1, 3).reshape(N_DEV, L * a, b // N_DEV)
    return full.reshape(L, N_DEV, a // N_DEV, b).transpose(1, 0, 2, 3).reshape(N_DEV, L * a // N_DEV, b)


def _from_slots(slots, shard_shape, axis):
    L, a, b = shard_shape
    t = slots.reshape((N_DEV,) + tuple(shard_shape))
    if axis == 2:
        return t.transpose(1, 2, 0, 3).reshape(L, a, N_DEV * b)
    return t.transpose(1, 0, 2, 3).reshape(L, N_DEV * a, b)


def _view2d(t):
    return t.reshape(-1, t.shape[-1])


def _pad_heads(w, nh):
    K = w.shape[0]
    return jnp.pad(w.reshape(K, nh, GDN_D), ((0, 0), (0, 0), (0, LANES - GDN_D))).reshape(K, nh * LANES)


def _unpad_heads(w, nh):
    K = w.shape[0]
    return w.reshape(K, nh, LANES)[:, :, :GDN_D].reshape(K, nh * GDN_D)


IN_WIDTHS = (512, 512, 512, 512, 8, 8, 256, 256, 256, 256, 160)


def _split_in(w):
    offs = np.cumsum((0,) + IN_WIDTHS)
    return [w[:, int(offs[i]):int(offs[i + 1])] for i in range(len(IN_WIDTHS))]


def _pad_in_proj(w):
    gq, gk, gv, gz, ga, gb, sq, sk, sv, mq, mkv = _split_in(w)
    gab = jnp.pad(jnp.concatenate([ga, gb], axis=1), ((0, 0), (0, LANES - 2 * GDN_HEADS)))
    return jnp.concatenate(
        [gq, gk, gv, gz] + [_pad_heads(t, SB_HEADS) for t in (sq, sk, sv)]
        + [mq, jnp.pad(mkv, ((0, 0), (0, 2 * LANES - mkv.shape[1]))), gab], axis=1)


def _unpad_in_proj(wp):
    c = lambda cb, n: wp[:, cb * LANES:(cb + n) * LANES]
    gab = c(CB_GAB, 1)
    parts = [c(cb, DO_SB) for cb in (CB_GQ, CB_GK, CB_GV, CB_GZ)]
    parts += [gab[:, :GDN_HEADS], gab[:, GDN_HEADS:2 * GDN_HEADS]]
    parts += [_unpad_heads(c(cb, SB_HEADS), SB_HEADS) for cb in (CB_SQ, CB_SK, CB_SV)]
    parts += [c(CB_MQ, 2), c(CB_MKV, 2)[:, :MLA_KV_RANK + MLA_ROPE]]
    return jnp.concatenate(parts, axis=1)


def _pad_lanes(w, width):
    return jnp.pad(w, ((0, 0), (0, width - w.shape[1])))


def _mla_up_pad(w_uq, w_ukv):
    H = MLA_HEADS
    dq = MLA_NOPE + MLA_ROPE
    wuq = jnp.pad(w_uq.reshape(-1, H, dq), ((0, 0), (0, 0), (0, LANES - dq))).reshape(-1, H * LANES)
    kv = w_ukv.reshape(-1, H, MLA_NOPE + MLA_V)
    wuk = jnp.pad(kv[:, :, :MLA_NOPE], ((0, 0), (0, 0), (0, LANES - MLA_NOPE))).reshape(-1, H * LANES)
    wuv = jnp.pad(kv[:, :, MLA_NOPE:], ((0, 0), (0, 0), (0, LANES - MLA_V))).reshape(-1, H * LANES)
    return wuq, wuk, wuv


def _mla_up_unpad(dwuq, dwuk, dwuv):
    H = MLA_HEADS
    dq = MLA_NOPE + MLA_ROPE
    g_uq = dwuq.reshape(-1, H, LANES)[:, :, :dq].reshape(-1, H * dq)
    g_ukv = jnp.concatenate([dwuk.reshape(-1, H, LANES)[:, :, :MLA_NOPE], dwuv.reshape(-1, H, LANES)[:, :, :MLA_V]],
                            axis=2).reshape(-1, H * (MLA_NOPE + MLA_V))
    return g_uq, g_ukv


def _rope_tables(positions):
    inv = 1.0 / (ROPE_BASE ** (jnp.arange(0, MLA_ROPE, 2, dtype=F32) / MLA_ROPE))
    ang = positions.astype(F32)[:, None] * inv
    cos, sin = jnp.cos(ang), jnp.sin(ang)
    T = positions.shape[0]
    one = lambda n: jnp.ones((T, n), F32)
    zero = lambda n: jnp.zeros((T, n), F32)
    tail = LANES - MLA_NOPE - MLA_ROPE
    C = jnp.concatenate([one(MLA_NOPE), cos, cos, one(tail)], axis=1)
    Sn = jnp.concatenate([zero(MLA_NOPE), -sin, zero(HALF_ROPE + tail)], axis=1)
    Sp = jnp.concatenate([zero(MLA_NOPE + HALF_ROPE), sin, zero(tail)], axis=1)
    return C, Sn, Sp


GATHER_FIRST = [("ffa_w_in", 0), ("ffa_w_out", 0)] + [(n, l) for l in range(DEPTH) for n in ("gdn_conv_w", "ln_g", "ln_b")]
GATHER_PLAN = {
    (0, "ffa_fwd"): [("mix_w_in", 0), ("mla_w_uq", 0), ("mla_w_ukv", 0), ("mix_w_o", 0)],
    (0, "in_proj"): [("ple_w_gate", 0), ("ple_w_proj", 0)],
    (0, "gdn_chunk_fwd"): [("ffb_w_in", 0)],
    (0, "sb_fwd"): [("ffb_w_out", 0), ("mix_w_o", 1)],
    (0, "mla_fwd"): [("ffa_w_out", 1)],
    (0, "ffb_fwd"): [("ffa_w_in", 1)],
    (1, "ffa_fwd"): [("mix_w_in", 1)],
    (1, "in_proj"): [("mla_w_uq", 1), ("mla_w_ukv", 1)],
    (1, "gdn_chunk_fwd"): [("ffb_w_in", 1)],
    (1, "sb_fwd"): [("ffb_w_out", 1), ("ple_w_gate", 1), ("ple_w_proj", 1)],
}
SCATTER_PLAN = {
    (1, "gdn_chunk_bwd"): [("ffb_w_in", 1)],
    (1, "gdn_pre_bwd"): [("ffb_w_out", 1), ("ple_w_gate", 1), ("ple_w_proj", 1), ("mix_w_o", 1)],
    (1, "ffa_bwd"): [("mix_w_in", 1), ("mla_w_uq", 1), ("mla_w_ukv", 1), ("gdn_conv_w", 1)],
    (0, "ffb_bwd"): [("ffa_w_in", 1)],
    (0, "gdn_chunk_bwd"): [("ffb_w_in", 0)],
    (0, "gdn_pre_bwd"): [("ffb_w_out", 0), ("ple_w_gate", 0), ("ple_w_proj", 0), ("mix_w_o", 0)],
    (0, "mla_bwd"): [("ffa_w_out", 1), ("ln_g", 1), ("ln_b", 1)],
    (0, "ffa_bwd"): [("mix_w_in", 0), ("mla_w_uq", 0), ("mla_w_ukv", 0), ("gdn_conv_w", 0)],
    (0, "d_ffa_in"): [("ffa_w_out", 0), ("ln_g", 0), ("ln_b", 0)],
}
SCATTER_LAST = [("ffa_w_in", 0)]


class Exchanges:
    def __init__(self, shards):
        self.shards = shards
        self.full = {}
        self.partial = {}
        self.received = {}

    def _block(self, key):
        n, l = key
        return self.shards[n][l].astype(SHARDED[n][1])

    def _absorb_gather(self, keys, results):
        for (n, l), g in zip(keys, results):
            blk = self.shards[n][l]
            self.full[(n, l)] = g if n in FFN_SLOT else _from_slots(g, (1,) + blk.shape, SHARDED[n][0])[0]

    def gather_now(self, keys, name):
        self._absorb_gather(keys, all_gather([self._block(k) for k in keys], name))

    def gather_with(self, layer, tag):
        keys = GATHER_PLAN.get((layer, tag))
        return None if keys is None else (keys, Hosted("gather", [self._block(k) for k in keys]))

    def scatter_with(self, layer, tag):
        keys = SCATTER_PLAN.get((layer, tag))
        return None if keys is None else (keys, Hosted("scatter", [self.partial[k] for k in keys]))

    def done(self, carried):
        if carried is not None:
            keys, hosted = carried
            if hosted.kind == "gather":
                self._absorb_gather(keys, hosted.results)
            else:
                self.received.update(zip(keys, hosted.results))

    def add_grad(self, key, g):
        n, l = key
        self.partial[key] = g if n in FFN_SLOT else _to_slots(g[None], SHARDED[n][0]).astype(SHARDED[n][1])


def _carried(c):
    return None if c is None else c[1]


def _layer_fwd(h0, p_i, rope, i, ex, rep):
    L = "L%d_" % i
    S = {"h0": h0, "p": p_i}
    W = ex.full
    ln_g = [W[("ln_g", i)][j][None, :] for j in range(3)]
    ln_b = [W[("ln_b", i)][j][None, :] for j in range(3)]
    S["ln_g"] = ln_g
    c = ex.gather_with(i, "ffa_fwd")
    S["h1"], S["xh1"], S["rs1"], S["Ga"], S["Ua"], S["h1b"] = ffn_fwd(
        h0, W[("ffa_w_in", i)], W[("ffa_w_out", i)], ln_g[0], ln_b[0], L + "ffa_fwd", hosted=_carried(c))
    ex.done(c)
    S["win"] = _pad_in_proj(W[("mix_w_in", i)])
    c = ex.gather_with(i, "in_proj")
    S["proj"] = mm_nn(S["h1b"], S["win"], L + "in_proj", hosted=_carried(c))
    ex.done(c)
    S["conv"] = W[("gdn_conv_w", i)]
    S["a_row"] = _pad_lanes(rep["gdn_a_log"][i][None, :], LANES)
    S["dt_row"] = _pad_lanes(rep["gdn_dt_bias"][i][None, :], LANES)
    S["nw"] = rep["gdn_norm_w"][i][None, :]
    S["wq"] = rep["mla_q_norm_w"][i][None, :]
    S["wkv"] = rep["mla_kv_norm_w"][i][None, :]
    S["qkv"] = gdn_pre_fwd(S["proj"], S["conv"], L + "gdn_pre_fwd")
    c = ex.gather_with(i, "gdn_chunk_fwd")
    S["o_gdn"], S["opre"], S["Tm"], S["Sin"] = gdn_chunk_fwd(S["qkv"], S["proj"], S["a_row"], S["dt_row"], S["nw"],
                                                            L + "gdn_chunk_fwd", hosted=_carried(c))
    ex.done(c)
    c = ex.gather_with(i, "sb_fwd")
    S["o_sb"], S["tot"] = sb_fwd(S["proj"], L + "sb_fwd", hosted=_carried(c))
    ex.done(c)
    S["wuq"], S["wuk"], S["wuv"] = _mla_up_pad(W[("mla_w_uq", i)], W[("mla_w_ukv", i)])
    S["Q"], S["K"], S["V"] = mla_pre_fwd(S["proj"], S["wq"], S["wkv"], S["wuq"], S["wuk"], S["wuv"], *rope, L + "mla_pre_fwd")
    c = ex.gather_with(i, "mla_fwd")
    S["o_mla"], S["lse"] = mla_fwd(S["Q"], S["K"], S["V"], L + "mla_fwd", hosted=_carried(c))
    ex.done(c)
    wo = W[("mix_w_o", i)]
    wo_att = wo[GDN_W:].reshape(-1, GDN_D, wo.shape[1])
    S["wo"] = jnp.concatenate(
        [wo[:GDN_W], jnp.pad(wo_att, ((0, 0), (0, LANES - GDN_D), (0, 0))).reshape(-1, wo.shape[1])], axis=0)
    S["h2"], S["xh2"], S["rs2"], S["h2b"] = proj_res_ln([S["o_gdn"], S["o_sb"], S["o_mla"]], S["wo"], S["h1"],
                                                        ln_g[1], ln_b[1], L + "out_proj")
    c = ex.gather_with(i, "ffb_fwd")
    S["h3"], S["xh3"], S["rs3"], S["Gb"], S["Ub"], _ = ffn_fwd(
        S["h2"], W[("ffb_w_in", i)], W[("ffb_w_out", i)], ln_g[2], ln_b[2], L + "ffb_fwd", hosted=_carried(c))
    ex.done(c)
    h4, S["a"], S["e"] = ple_fwd(S["h3"], p_i, W[("ple_w_gate", i)], W[("ple_w_proj", i)], L + "ple_fwd")
    return h4, S


def _layer_bwd(dh4, S, rope, i, ex):
    L = "L%d_" % i
    W = ex.full
    Grep = {}
    dh3, da, de = ple_bwd(dh4, S["a"], S["e"], W[("ple_w_gate", i)], L + "ple_bwd")
    ex.add_grad(("ple_w_gate", i), mm_tn(S["h3"], da, L + "d_ple_gate"))
    ex.add_grad(("ple_w_proj", i), mm_tn(S["p"], de, L + "d_ple_proj"))
    dz3, dg2, db2 = ln_bwd(dh3, S["xh3"], S["rs3"], S["ln_g"][2], L + "ln3_bwd")
    c = ex.scatter_with(i, "ffb_bwd")
    dh2, dGb, dUb, actb = ffn_bwd(dz3, S["Gb"], S["Ub"], W[("ffb_w_in", i)], W[("ffb_w_out", i)], L + "ffb_bwd",
                                  hosted=_carried(c))
    ex.done(c)
    ex.add_grad(("ffb_w_in", i), ffn_dw_in(S["h2b"].T, dGb, dUb, L + "d_ffb_in"))
    ex.add_grad(("ffb_w_out", i), ffn_dw_out(actb, dz3, L + "d_ffb_out"))
    dz2, dg1, db1 = ln_bwd(dh2, S["xh2"], S["rs2"], S["ln_g"][1], L + "ln2_bwd")
    docat = mm_nn(dz2, S["wo"], L + "d_ocat", b_transposed=True)
    dwo_att = jnp.concatenate([mm_tn(S["o_sb"], dz2, L + "d_wo_sb"), mm_tn(S["o_mla"], dz2, L + "d_wo_mla")], axis=0)
    dwo_att = dwo_att.reshape(-1, LANES, dwo_att.shape[1])[:, :GDN_D, :].reshape(-1, dwo_att.shape[1])
    ex.add_grad(("mix_w_o", i), jnp.concatenate([mm_tn(S["o_gdn"], dz2, L + "d_wo_gdn"), dwo_att], axis=0))
    c = ex.scatter_with(i, "gdn_chunk_bwd")
    dqkv, dgz, dgab, d_alog, d_dt, d_nw = gdn_chunk_bwd(S["qkv"], S["proj"], S["a_row"], S["dt_row"], S["nw"],
                                                        S["opre"], S["Tm"], S["Sin"], docat, L + "gdn_chunk_bwd",
                                                        hosted=_carried(c))
    ex.done(c)
    c = ex.scatter_with(i, "gdn_pre_bwd")
    dc, dconv = gdn_pre_bwd_pointwise(S["proj"], S["conv"], dqkv, L + "gdn_pre_bwd", hosted=_carried(c))
    ex.done(c)
    dxqkv = gdn_pre_bwd_conv(dc, S["conv"], L + "gdn_conv_bwd")
    ex.add_grad(("gdn_conv_w", i), dconv)
    Grep["gdn_a_log"], Grep["gdn_dt_bias"], Grep["gdn_norm_w"] = d_alog[0, :GDN_HEADS], d_dt[0, :GDN_HEADS], d_nw[0]
    dsq, dsk, dsv = sb_bwd(S["proj"], S["tot"], docat, DO_SB, L + "sb_bwd")
    c = ex.scatter_with(i, "mla_bwd")
    dQ, dK, dV = mla_bwd(S["Q"], S["K"], S["V"], S["o_mla"], S["lse"], docat, DO_MLA, L + "mla_bwd",
                         hosted=_carried(c))
    ex.done(c)
    dmq, dmkv, dwuq, dwuk, dwuv, dwq, dwkv = mla_pre_bwd(
        S["proj"], S["wq"], S["wkv"], S["wuq"], S["wuk"], S["wuv"], *rope, dQ, dK, dV, L + "mla_pre_bwd")
    g_uq, g_ukv = _mla_up_unpad(dwuq, dwuk, dwuv)
    ex.add_grad(("mla_w_uq", i), g_uq)
    ex.add_grad(("mla_w_ukv", i), g_ukv)
    Grep["mla_q_norm_w"], Grep["mla_kv_norm_w"] = dwq[0], dwkv[0]
    dproj = jnp.concatenate([dxqkv, dgz, dsq, dsk, dsv, dmq, dmkv, dgab], axis=1).astype(BF16)
    ex.add_grad(("mix_w_in", i),
                _unpad_in_proj(mm_tn(S["h1b"].T, dproj, L + "d_in_proj", a_transposed=True)))
    dh1 = mm_nn(dproj, S["win"], L + "d_h1", res=dz2, res_scale=ALPHA, b_transposed=True)
    dz1, dg0, db0 = ln_bwd(dh1, S["xh1"], S["rs1"], S["ln_g"][0], L + "ln1_bwd")
    c = ex.scatter_with(i, "ffa_bwd")
    dh0, dGa, dUa, acta = ffn_bwd(dz1, S["Ga"], S["Ua"], W[("ffa_w_in", i)], W[("ffa_w_out", i)], L + "ffa_bwd",
                                  hosted=_carried(c))
    ex.done(c)
    ex.add_grad(("ffa_w_out", i), ffn_dw_out(acta, dz1, L + "d_ffa_out"))
    ex.add_grad(("ln_g", i), jnp.concatenate([dg0, dg1, dg2], axis=0))
    ex.add_grad(("ln_b", i), jnp.concatenate([db0, db1, db2], axis=0))
    c = ex.scatter_with(i, "d_ffa_in")
    ex.add_grad(("ffa_w_in", i), ffn_dw_in(S["h0"].T.astype(BF16), dGa, dUa, L + "d_ffa_in", hosted=_carried(c)))
    ex.done(c)
    return dh0, Grep


def _local_step(x, p, positions, target, ex, rep):
    assert DEPTH == 2
    rope = _rope_tables(positions)
    h, saved = x, []
    for i in range(DEPTH):
        h, S = _layer_fwd(h, p[i], rope, i, ex, rep)
        saved.append(S)
    loss, dh = loss_head(h, target, "loss_head")
    grads = [None] * DEPTH
    for i in reversed(range(DEPTH)):
        dh, grads[i] = _layer_bwd(dh, saved[i], rope, i, ex)
    return loss, dh, {n: jnp.stack([grads[i][n] for i in range(DEPTH)]) for n in REPLICATED}


def kernel(x, p, positions, ffa_w_in, ffa_w_out, mix_w_in, gdn_conv_w, gdn_a_log, gdn_dt_bias, gdn_norm_w, mla_q_norm_w, mla_kv_norm_w, mla_w_uq, mla_w_ukv, mix_w_o, ffb_w_in, ffb_w_out, ln_g, ln_b, ple_w_gate, ple_w_proj, loss_target, m_ffa_w_in, m_ffa_w_out, m_mix_w_in, m_gdn_conv_w, m_gdn_a_log, m_gdn_dt_bias, m_gdn_norm_w, m_mla_q_norm_w, m_mla_kv_norm_w, m_mla_w_uq, m_mla_w_ukv, m_mix_w_o, m_ffb_w_in, m_ffb_w_out, m_ln_g, m_ln_b, m_ple_w_gate, m_ple_w_proj, v_ffa_w_in, v_ffa_w_out, v_mix_w_in, v_gdn_conv_w, v_gdn_a_log, v_gdn_dt_bias, v_gdn_norm_w, v_mla_q_norm_w, v_mla_kv_norm_w, v_mla_w_uq, v_mla_w_ukv, v_mix_w_o, v_ffb_w_in, v_ffb_w_out, v_ln_g, v_ln_b, v_ple_w_gate, v_ple_w_proj):
    given = dict(locals())
    shards = {n: given[n] for n in WEIGHTS}
    ex = Exchanges({n: shards[n] for n in SHARDED})
    ex.gather_now(GATHER_FIRST, "gather_first")
    loss, grad_x, Grep = _local_step(x[0], p[:, 0], positions[0], loss_target[0], ex, {n: shards[n] for n in REPLICATED})
    loss = lax.psum(loss[0, 0], ("x", "y", "c"))
    last = Hosted("scatter", [ex.partial[k] for k in SCATTER_LAST])
    ex.received.update(zip(SCATTER_LAST, exchange_now(last, "scatter_last")))
    rep_received = dict(zip(REPLICATED, all_gather([Grep[n] for n in REPLICATED], "gather_replicated_grads")))
    grad, delta, new_m, new_v = {}, {}, {}, {}
    for n in WEIGHTS:
        shape = shards[n].shape
        parts = [rep_received[n]] if n in REPLICATED else [ex.received[(n, l)] for l in range(DEPTH)]
        if parts[0].shape[1] % 8:
            parts = [jnp.concatenate(parts, axis=1)]
        outs = reduce_adamw(parts, _view2d(shards[n]), _view2d(given["m_" + n]), _view2d(given["v_" + n]),
                            "adamw_" + n)
        grad[n], delta[n], new_m[n], new_v[n] = (t.reshape(shape) for t in outs)
    return (loss, grad_x[None], *[grad[n] for n in WEIGHTS], *[delta[n] for n in WEIGHTS],
            *[new_m[n] for n in WEIGHTS], *[new_v[n] for n in WEIGHTS])
```

```python
import functools
import numpy as np
import jax
import jax.numpy as jnp
from jax import lax
from jax.experimental import pallas as pl
from jax.experimental.pallas import tpu as pltpu

F32 = jnp.float32
BF16 = jnp.bfloat16

DEPTH = 2
LN_EPS = 1e-5
RMS_EPS = 1e-6
ALPHA = (2 * DEPTH) ** 0.25
GDN_HEADS, GDN_D, GDN_CONV, GDN_CHUNK = 8, 64, 4, 64
SB_HEADS, SB_DIM = 4, 64
MLA_HEADS, MLA_NOPE, MLA_ROPE, MLA_V, MLA_Q_RANK, MLA_KV_RANK = 4, 64, 32, 64, 256, 128
ROPE_BASE = 10000.0
HALF_ROPE = MLA_ROPE // 2
LANES = 128
N_DEV = 8
ADAM_LR, ADAM_B1, ADAM_B2, ADAM_EPS, ADAM_WD, ADAM_STEP = 0.001, 0.9, 0.999, 1e-08, 0.01, 10

CB_GQ, CB_GK, CB_GV, CB_GZ = 0, 4, 8, 12
CB_SQ, CB_SK, CB_SV = 16, 20, 24
CB_MQ, CB_MKV, CB_GAB = 28, 30, 32
PROJ_W = 33 * LANES
GDN_W = GDN_HEADS * GDN_D
DO_SB = GDN_W // LANES
DO_MLA = DO_SB + SB_HEADS
VMEM_LIMIT = 56 * 1024 * 1024
MM_TILE = 1536

NT_DIMS = (((1,), (1,)), ((), ()))
TN_DIMS = (((0,), (0,)), ((), ()))


def _cp(sem):
    return pltpu.CompilerParams(dimension_semantics=sem, vmem_limit_bytes=VMEM_LIMIT)


def _bdot(a, b):
    return jnp.dot(a.astype(BF16), b.astype(BF16), preferred_element_type=F32)


def _bdot_nt(a, b):
    return lax.dot_general(a.astype(BF16), b.astype(BF16), NT_DIMS, preferred_element_type=F32)


def _bdot_tn(a, b):
    return lax.dot_general(a.astype(BF16), b.astype(BF16), TN_DIMS, preferred_element_type=F32)


def _split2(a):
    hi = a.astype(BF16)
    lo = (a - hi.astype(F32)).astype(BF16)
    return hi, lo


def _ones_dot_left(ones_bf16, x):
    hi = x.astype(BF16)
    r1 = x - hi.astype(F32)
    mid = r1.astype(BF16)
    lo = (r1 - mid.astype(F32)).astype(BF16)
    d = functools.partial(jnp.dot, preferred_element_type=F32)
    return d(ones_bf16, hi) + d(ones_bf16, mid) + d(ones_bf16, lo)


def _iota2(shape, dim):
    return lax.broadcasted_iota(jnp.int32, shape, dim)


def _sigmoid(x):
    return 0.5 * jnp.tanh(0.5 * x) + 0.5


def _softplus(x):
    return jnp.maximum(x, 0.0) + jnp.log(1.0 + jnp.exp(-jnp.abs(x)))


def _pick(n, limit, mult):
    if n <= limit:
        return n
    best = None
    for t in range(mult, limit + 1, mult):
        if n % t == 0:
            best = t
    assert best is not None, (n, limit, mult)
    return best


MESH = pl.DeviceIdType.MESH
ANY = pl.BlockSpec(memory_space=pl.ANY)


def _place():
    return lax.axis_index("x"), lax.axis_index("y"), lax.axis_index("c")


def _peer(k):
    x, y, c = _place()
    return (1 - x if k & 4 else x, 1 - y if k & 2 else y, 1 - c if k & 1 else c)


class Hosted:
    def __init__(self, kind, arrays):
        self.kind, self.arrays, self.n, self.results = kind, list(arrays), len(arrays), None

    def out_shapes(self):
        if self.kind == "gather":
            return [jax.ShapeDtypeStruct((N_DEV,) + a.shape, a.dtype) for a in self.arrays]
        return [jax.ShapeDtypeStruct(a.shape, a.dtype) for a in self.arrays]

    def sems(self):
        return [pltpu.SemaphoreType.DMA((self.n, N_DEV - 1)), pltpu.SemaphoreType.DMA((self.n, N_DEV - 1)),
                pltpu.SemaphoreType.DMA((self.n,))]

    def _copies(self, src_refs, dst_refs, send_sems, recv_sems, local_sems):
        x, y, c = _place()
        me = 4 * x + 2 * y + c
        local, remote = [], []
        for a in range(self.n):
            gather = self.kind == "gather"
            local.append(pltpu.make_async_copy(src_refs[a] if gather else src_refs[a].at[me], dst_refs[a].at[me],
                                               local_sems.at[a]))
            for k in range(1, N_DEV):
                px, py, pc = _peer(k)
                remote.append(pltpu.make_async_remote_copy(
                    src_ref=src_refs[a] if gather else src_refs[a].at[4 * px + 2 * py + pc], dst_ref=dst_refs[a].at[me],
                    send_sem=send_sems.at[a, k - 1], recv_sem=recv_sems.at[a, k - 1],
                    device_id=(px, py, pc), device_id_type=MESH))
        return local, remote

    def start(self, *refs):
        local, remote = self._copies(*refs)
        for cp in local + remote:
            cp.start()

    def wait(self, *refs):
        local, remote = self._copies(*refs)
        for cp in remote:
            cp.wait_recv()
        for cp in remote:
            cp.wait_send()
        for cp in local:
            cp.wait()


def _hosted_call(hosted, body, *, grid, in_specs, out_specs, out_shape, scratch_shapes=(), compiler_params, name):
    if hosted is None:
        return pl.pallas_call(body, grid=grid, in_specs=in_specs, out_specs=out_specs, out_shape=out_shape,
                              scratch_shapes=scratch_shapes, compiler_params=compiler_params, name=name)
    single = not isinstance(out_shape, (list, tuple))
    o_specs = [out_specs] if single else list(out_specs)
    o_shape = [out_shape] if single else list(out_shape)
    n_in, n_out, n_scr, n = len(in_specs), len(o_specs), len(scratch_shapes), hosted.n

    def wrapped(*refs):
        ins, c_in = refs[:n_in], refs[n_in:n_in + n]
        outs, c_out = refs[n_in + n:n_in + n + n_out], refs[n_in + n + n_out:n_in + 2 * n + n_out]
        rest = refs[n_in + 2 * n + n_out:]
        scr, sems = rest[:n_scr], rest[n_scr:]
        ids = [pl.program_id(ax) for ax in range(len(grid))]
        first = functools.reduce(jnp.logical_and, [i == 0 for i in ids])
        last = functools.reduce(jnp.logical_and, [i == g - 1 for i, g in zip(ids, grid)])

        @pl.when(first)
        def _():
            hosted.start(c_in, c_out, *sems)

        body(*ins, *outs, *scr)

        @pl.when(last)
        def _():
            hosted.wait(c_in, c_out, *sems)

    call = pl.pallas_call(
        wrapped, grid=grid, in_specs=list(in_specs) + [ANY] * n, out_specs=o_specs + [ANY] * n,
        out_shape=o_shape + hosted.out_shapes(), scratch_shapes=list(scratch_shapes) + hosted.sems(),
        compiler_params=_cp(("arbitrary",) * len(grid)), name=name)

    def run(*args):
        outs = call(*args, *hosted.arrays)
        hosted.results = list(outs[n_out:])
        return outs[0] if single else list(outs[:n_out])

    return run


def exchange_now(hosted, name):
    n = hosted.n

    def body(*refs):
        src, dst, sems = refs[:n], refs[n:2 * n], refs[2 * n:]
        hosted.start(src, dst, *sems)
        hosted.wait(src, dst, *sems)

    return pl.pallas_call(body, out_shape=hosted.out_shapes(), in_specs=[ANY] * n, out_specs=[ANY] * n,
                          scratch_shapes=hosted.sems(), name=name)(*hosted.arrays)


def mm_nn(a, b, name, out_dtype=F32, res=None, res_scale=1.0, b_transposed=False, hosted=None):
    M, K = a.shape
    N = b.shape[0] if b_transposed else b.shape[1]
    tm, tn, tk = _pick(M, 512, 16), _pick(N, MM_TILE, LANES), _pick(K, MM_TILE, LANES)
    nk = K // tk
    has_res = res is not None
    dot = _bdot_nt if b_transposed else _bdot

    def body(*refs):
        if has_res:
            a_ref, b_ref, r_ref, o_ref, acc_ref = refs
        else:
            a_ref, b_ref, o_ref, acc_ref = refs
        k = pl.program_id(2)

        @pl.when(k == 0)
        def _():
            acc_ref[...] = jnp.zeros_like(acc_ref)

        acc_ref[...] += dot(a_ref[...], b_ref[...])

        @pl.when(k == nk - 1)
        def _():
            out = acc_ref[...]
            if has_res:
                out = out + res_scale * r_ref[...]
            o_ref[...] = out.astype(o_ref.dtype)

    b_spec = pl.BlockSpec((tn, tk), lambda i, j, k: (j, k)) if b_transposed else pl.BlockSpec((tk, tn), lambda i, j, k: (k, j))
    in_specs = [pl.BlockSpec((tm, tk), lambda i, j, k: (i, k)), b_spec]
    args = [a, b]
    if has_res:
        in_specs.append(pl.BlockSpec((tm, tn), lambda i, j, k: (i, j)))
        args.append(res)
    return _hosted_call(
        hosted, body, grid=(M // tm, N // tn, nk), in_specs=in_specs,
        out_specs=pl.BlockSpec((tm, tn), lambda i, j, k: (i, j)),
        out_shape=jax.ShapeDtypeStruct((M, N), out_dtype),
        scratch_shapes=[pltpu.VMEM((tm, tn), F32)],
        compiler_params=_cp(("parallel", "parallel", "arbitrary")), name=name)(*args)


def mm_tn(a, b, name, out_dtype=F32, a_transposed=False):
    K, T = a.shape if a_transposed else a.shape[::-1]
    _, N = b.shape
    tk = K if a_transposed else _pick(K, 512, LANES)
    tn, tt = _pick(N, MM_TILE, LANES), _pick(T, 512, LANES)
    nt = T // tt

    def body(a_ref, b_ref, o_ref, acc_ref):
        t = pl.program_id(2)

        @pl.when(t == 0)
        def _():
            acc_ref[...] = jnp.zeros_like(acc_ref)

        if a_transposed:
            acc_ref[...] += _bdot(a_ref[:, pl.ds(pl.multiple_of(t * tt, tt), tt)], b_ref[...])
        else:
            acc_ref[...] += _bdot_tn(a_ref[...], b_ref[...])

        @pl.when(t == nt - 1)
        def _():
            o_ref[...] = acc_ref[...].astype(o_ref.dtype)

    a_spec = pl.BlockSpec((K, T), lambda i, j, t: (0, 0)) if a_transposed else pl.BlockSpec((tt, tk), lambda i, j, t: (t, i))
    return pl.pallas_call(
        body, grid=(K // tk, N // tn, nt),
        in_specs=[a_spec, pl.BlockSpec((tt, tn), lambda i, j, t: (t, j))],
        out_specs=pl.BlockSpec((tk, tn), lambda i, j, t: (i, j)),
        out_shape=jax.ShapeDtypeStruct((K, N), out_dtype),
        scratch_shapes=[pltpu.VMEM((tk, tn), F32)],
        compiler_params=_cp(("parallel", "parallel", "arbitrary")), name=name)(a, b)


def _ln_apply(z, g, b):
    mu = jnp.mean(z, axis=-1, keepdims=True)
    zc = z - mu
    var = jnp.mean(zc * zc, axis=-1, keepdims=True)
    rstd = lax.rsqrt(var + LN_EPS)
    xhat = zc * rstd
    return xhat * g + b, xhat, rstd


def ln_bwd(dout, xhat, rstd, g, name):
    T, D = dout.shape
    tm = _pick(T, 512, 8)

    def body(do_ref, xh_ref, rs_ref, g_ref, dz_ref, dg_ref, db_ref):
        i = pl.program_id(0)

        @pl.when(i == 0)
        def _():
            dg_ref[...] = jnp.zeros_like(dg_ref)
            db_ref[...] = jnp.zeros_like(db_ref)

        do = do_ref[...]
        xh = xh_ref[...]
        dxh = do * g_ref[...]
        m1 = jnp.mean(dxh, axis=-1, keepdims=True)
        m2 = jnp.mean(dxh * xh, axis=-1, keepdims=True)
        dz_ref[...] = rs_ref[...] * (dxh - m1 - xh * m2)
        dg_ref[...] += jnp.sum(do * xh, axis=0, keepdims=True)
        db_ref[...] += jnp.sum(do, axis=0, keepdims=True)

    row = pl.BlockSpec((tm, D), lambda i: (i, 0))
    vec = pl.BlockSpec((1, D), lambda i: (0, 0))
    return pl.pallas_call(
        body, grid=(T // tm,),
        in_specs=[row, row, pl.BlockSpec((tm, 1), lambda i: (i, 0)), vec],
        out_specs=[row, vec, vec],
        out_shape=[jax.ShapeDtypeStruct((T, D), F32), jax.ShapeDtypeStruct((1, D), F32), jax.ShapeDtypeStruct((1, D), F32)],
        compiler_params=_cp(("arbitrary",)), name=name)(dout, xhat, rstd, g)


FFN_CHUNKS = N_DEV // 2


def ffn_fwd(h, w_in, w_out, g, b, name, hosted=None):
    T, D = h.shape
    fc = w_in.shape[2]
    half = w_out.shape[1]
    tm = _pick(T, 512, 8)
    nc = FFN_CHUNKS

    def body(h_ref, wg_ref, wu_ref, wo_ref, g_ref, b_ref, out_ref, xh_ref, rs_ref, G_ref, U_ref, ob_ref, acc_ref):
        c = pl.program_id(1)

        @pl.when(c == 0)
        def _():
            acc_ref[...] = jnp.zeros_like(acc_ref)

        hb = h_ref[...].astype(BF16)
        G = jnp.dot(hb, wg_ref[0], preferred_element_type=F32)
        U = jnp.dot(hb, wu_ref[0], preferred_element_type=F32)
        G_ref[0] = G
        U_ref[0] = U
        act = G * _sigmoid(G) * U
        acc_ref[...] += _bdot(act, wo_ref[...].reshape(2 * half, D))

        @pl.when(c == nc - 1)
        def _():
            z = ALPHA * h_ref[...] + 0.5 * acc_ref[...]
            out, xh, rs = _ln_apply(z, g_ref[...], b_ref[...])
            out_ref[...] = out
            ob_ref[...] = out.astype(BF16)
            xh_ref[...] = xh
            rs_ref[...] = rs

    row = pl.BlockSpec((tm, D), lambda i, c: (i, 0))
    vec = pl.BlockSpec((1, D), lambda i, c: (0, 0))
    cblk = pl.BlockSpec((1, tm, fc), lambda i, c: (c, i, 0))
    csds = jax.ShapeDtypeStruct((nc, T, fc), F32)
    return _hosted_call(
        hosted, body, grid=(T // tm, nc),
        in_specs=[row, pl.BlockSpec((1, D, fc), lambda i, c: (c, 0, 0)),
                  pl.BlockSpec((1, D, fc), lambda i, c: (c + nc, 0, 0)),
                  pl.BlockSpec((2, half, D), lambda i, c: (c, 0, 0)), vec, vec],
        out_specs=[row, row, pl.BlockSpec((tm, 1), lambda i, c: (i, 0)), cblk, cblk, row],
        out_shape=[jax.ShapeDtypeStruct((T, D), F32), jax.ShapeDtypeStruct((T, D), F32), jax.ShapeDtypeStruct((T, 1), F32),
                   csds, csds, jax.ShapeDtypeStruct((T, D), BF16)],
        scratch_shapes=[pltpu.VMEM((tm, D), F32)],
        compiler_params=_cp(("parallel", "arbitrary")), name=name)(h, w_in, w_in, w_out, g, b)


def ffn_bwd(dz, G, U, w_in, w_out, name, hosted=None):
    T, D = dz.shape
    nc, _, fc = G.shape
    half = w_out.shape[1]
    tm = _pick(T, 512, 16)

    def body(dz_ref, G_ref, U_ref, wg_ref, wu_ref, wo_ref, dh_ref, dG_ref, dU_ref, act_ref, acc_ref):
        c = pl.program_id(1)

        @pl.when(c == 0)
        def _():
            acc_ref[...] = jnp.zeros_like(acc_ref)

        dy = (0.5 * dz_ref[...]).astype(BF16)
        dact = _bdot_nt(dy, wo_ref[...].reshape(2 * half, D))
        G = G_ref[0]
        U = U_ref[0]
        s = _sigmoid(G)
        silu = G * s
        dG = (dact * U * (s * (1.0 + G * (1.0 - s)))).astype(BF16)
        dU = (dact * silu).astype(BF16)
        dG_ref[0] = dG
        dU_ref[0] = dU
        act_ref[0] = (silu * U).astype(BF16)
        acc_ref[...] += _bdot_nt(dG, wg_ref[0]) + _bdot_nt(dU, wu_ref[0])

        @pl.when(c == nc - 1)
        def _():
            dh_ref[...] = ALPHA * dz_ref[...] + acc_ref[...]

    row = pl.BlockSpec((tm, D), lambda i, c: (i, 0))
    cblk = pl.BlockSpec((1, tm, fc), lambda i, c: (c, i, 0))
    csds = jax.ShapeDtypeStruct((nc, T, fc), BF16)
    return _hosted_call(
        hosted, body, grid=(T // tm, nc),
        in_specs=[row, cblk, cblk, pl.BlockSpec((1, D, fc), lambda i, c: (c, 0, 0)),
                  pl.BlockSpec((1, D, fc), lambda i, c: (c + nc, 0, 0)),
                  pl.BlockSpec((2, half, D), lambda i, c: (c, 0, 0))],
        out_specs=[row, cblk, cblk, cblk],
        out_shape=[jax.ShapeDtypeStruct((T, D), F32), csds, csds, csds],
        scratch_shapes=[pltpu.VMEM((tm, D), F32)],
        compiler_params=_cp(("parallel", "arbitrary")), name=name)(dz, G, U, w_in, w_in, w_out)


def ffn_dw_in(h_t, dG, dU, name, hosted=None):
    D, T = h_t.shape
    nc, _, fc = dG.shape
    tt = _pick(T, 512, LANES)
    nt = T // tt

    def body(h_ref, dG_ref, dU_ref, o_ref, acc_ref):
        s = pl.program_id(0)
        t = pl.program_id(1)

        @pl.when(t == 0)
        def _():
            acc_ref[...] = jnp.zeros_like(acc_ref)

        hb = h_ref[:, pl.ds(pl.multiple_of(t * tt, tt), tt)]

        @pl.when(s < nc)
        def _():
            acc_ref[...] += jnp.dot(hb, dG_ref[0], preferred_element_type=F32)

        @pl.when(s >= nc)
        def _():
            acc_ref[...] += jnp.dot(hb, dU_ref[0], preferred_element_type=F32)

        @pl.when(t == nt - 1)
        def _():
            o_ref[0] = acc_ref[...].astype(o_ref.dtype)

    return _hosted_call(
        hosted, body, grid=(2 * nc, nt),
        in_specs=[pl.BlockSpec((D, T), lambda s, t: (0, 0)),
                  pl.BlockSpec((1, tt, fc), lambda s, t: (jnp.minimum(s, nc - 1), jnp.where(s < nc, t, nt - 1), 0)),
                  pl.BlockSpec((1, tt, fc), lambda s, t: (jnp.maximum(s - nc, 0), jnp.where(s >= nc, t, 0), 0))],
        out_specs=pl.BlockSpec((1, D, fc), lambda s, t: (s, 0, 0)),
        out_shape=jax.ShapeDtypeStruct((2 * nc, D, fc), BF16),
        scratch_shapes=[pltpu.VMEM((D, fc), F32)],
        compiler_params=_cp(("parallel", "arbitrary")), name=name)(h_t, dG, dU)


def ffn_dw_out(act, dz, name, hosted=None):
    nc, T, fc = act.shape
    D = dz.shape[1]
    half = fc // 2
    tt = _pick(T, 512, 16)
    nt = T // tt

    def body(a_ref, dz_ref, o_ref, acc_ref):
        t = pl.program_id(1)

        @pl.when(t == 0)
        def _():
            acc_ref[...] = jnp.zeros_like(acc_ref)

        acc_ref[...] += _bdot_tn(a_ref[0], dz_ref[...])

        @pl.when(t == nt - 1)
        def _():
            o_ref[...] = (0.5 * acc_ref[...]).reshape(2, half, D).astype(o_ref.dtype)

    return _hosted_call(
        hosted, body, grid=(nc, nt),
        in_specs=[pl.BlockSpec((1, tt, fc), lambda c, t: (c, t, 0)), pl.BlockSpec((tt, D), lambda c, t: (t, 0))],
        out_specs=pl.BlockSpec((2, half, D), lambda c, t: (c, 0, 0)),
        out_shape=jax.ShapeDtypeStruct((2 * nc, half, D), BF16),
        scratch_shapes=[pltpu.VMEM((fc, D), F32)],
        compiler_params=_cp(("parallel", "arbitrary")), name=name)(act, dz)


def proj_res_ln(parts, w, res, g, b, name):
    T, D = res.shape
    tm = _pick(T, 512, 8)
    widths = [p.shape[1] for p in parts]
    offs = [int(sum(widths[:i])) for i in range(len(parts))]
    n = len(parts)

    def body(*refs):
        p_refs = refs[:n]
        w_ref, r_ref, g_ref, b_ref, out_ref, xh_ref, rs_ref, ob_ref = refs[n:]
        acc = ALPHA * r_ref[...]
        for p_ref, o, wd in zip(p_refs, offs, widths):
            acc = acc + _bdot(p_ref[...], w_ref[o:o + wd, :])
        out, xh, rs = _ln_apply(acc, g_ref[...], b_ref[...])
        out_ref[...] = out
        ob_ref[...] = out.astype(BF16)
        xh_ref[...] = xh
        rs_ref[...] = rs

    row = pl.BlockSpec((tm, D), lambda i: (i, 0))
    vec = pl.BlockSpec((1, D), lambda i: (0, 0))
    return pl.pallas_call(
        body, grid=(T // tm,),
        in_specs=[pl.BlockSpec((tm, wd), lambda i: (i, 0)) for wd in widths]
        + [pl.BlockSpec(w.shape, lambda i: (0, 0)), row, vec, vec],
        out_specs=[row, row, pl.BlockSpec((tm, 1), lambda i: (i, 0)), row],
        out_shape=[jax.ShapeDtypeStruct((T, D), F32), jax.ShapeDtypeStruct((T, D), F32), jax.ShapeDtypeStruct((T, 1), F32),
                   jax.ShapeDtypeStruct((T, D), BF16)],
        compiler_params=_cp(("parallel",)), name=name)(*parts, w, res, g, b)


def ple_fwd(h, h_b, p, wg, wp, name):
    T, D = h.shape
    P = p.shape[1]
    tm, tn = _pick(T, 512, 16), _pick(D, MM_TILE, LANES)

    def body(h_ref, hn_ref, p_ref, wg_ref, wp_ref, out_ref, a_ref, e_ref):
        a = _bdot(h_ref[...], wg_ref[...])
        e = _bdot(p_ref[...], wp_ref[...])
        a_ref[...] = a
        e_ref[...] = e
        out_ref[...] = hn_ref[...] + _sigmoid(a) * e

    blk = pl.BlockSpec((tm, tn), lambda i, j: (i, j))
    sds = jax.ShapeDtypeStruct((T, D), F32)
    return pl.pallas_call(
        body, grid=(T // tm, D // tn),
        in_specs=[pl.BlockSpec((tm, D), lambda i, j: (i, 0)), blk, pl.BlockSpec((tm, P), lambda i, j: (i, 0)),
                  pl.BlockSpec((D, tn), lambda i, j: (0, j)), pl.BlockSpec((P, tn), lambda i, j: (0, j))],
        out_specs=[blk, blk, blk], out_shape=[sds, sds, sds],
        compiler_params=_cp(("parallel", "parallel")), name=name)(h_b, h, p, wg, wp)


def ple_bwd(dout, a, e, wg, name):
    T, D = dout.shape
    tm = _pick(T, 512, 16)

    def body(do_ref, a_ref, e_ref, wg_ref, dh_ref, da_ref, de_ref):
        do = do_ref[...]
        s = _sigmoid(a_ref[...])
        da = (do * e_ref[...] * s * (1.0 - s)).astype(BF16)
        da_ref[...] = da
        de_ref[...] = (do * s).astype(BF16)
        dh_ref[...] = do + _bdot_nt(da, wg_ref[...])

    row = pl.BlockSpec((tm, D), lambda i: (i, 0))
    return pl.pallas_call(
        body, grid=(T // tm,),
        in_specs=[row, row, row, pl.BlockSpec((D, D), lambda i: (0, 0))],
        out_specs=[row, row, row],
        out_shape=[jax.ShapeDtypeStruct((T, D), F32), jax.ShapeDtypeStruct((T, D), BF16), jax.ShapeDtypeStruct((T, D), BF16)],
        compiler_params=_cp(("parallel",)), name=name)(dout, a, e, wg)


def loss_head(y, target, name):
    T, D = y.shape
    tm = _pick(T, 512, 8)

    def body(y_ref, t_ref, loss_ref, dy_ref):
        i = pl.program_id(0)

        @pl.when(i == 0)
        def _():
            loss_ref[...] = jnp.zeros_like(loss_ref)

        err = y_ref[...] - t_ref[...]
        dy_ref[...] = err * (1.0 / D)
        per_tok = jnp.sum(err * err, axis=-1, keepdims=True) * (1.0 / D)
        loss_ref[...] += 0.5 * jnp.sum(per_tok, axis=0, keepdims=True)

    row = pl.BlockSpec((tm, D), lambda i: (i, 0))
    return pl.pallas_call(
        body, grid=(T // tm,), in_specs=[row, row],
        out_specs=[pl.BlockSpec((1, 1), lambda i: (0, 0)), row],
        out_shape=[jax.ShapeDtypeStruct((1, 1), F32), jax.ShapeDtypeStruct((T, D), F32)],
        compiler_params=_cp(("arbitrary",)), name=name)(y, target)


HALO = 8


def _conv_taps(pad_ref, w_ref, tm, base):
    acc = w_ref[0:1, :] * pad_ref[pl.ds(base, tm), :]
    for k in range(1, GDN_CONV):
        acc = acc + w_ref[k:k + 1, :] * pad_ref[pl.ds(base + k, tm), :]
    return acc


GDN_GROUP_W = GDN_W
GDN_PRE_ROWS = 512


def _head_segments():
    head = jnp.arange(GDN_W, dtype=jnp.int32) // GDN_D
    return (head[:, None] == head[None, :]).astype(BF16)


def _head_sums(x, seg):
    hi = x.astype(BF16)
    r1 = x - hi.astype(F32)
    mid = r1.astype(BF16)
    lo = (r1 - mid.astype(F32)).astype(BF16)
    d = functools.partial(jnp.dot, preferred_element_type=F32)
    return d(hi, seg) + d(mid, seg) + d(lo, seg)


def _gdn_pre_common(x_ref, halo_ref, w_ref, seg_ref, pad_ref, tm):
    i = pl.program_id(1)
    grp = pl.program_id(0)
    pad_ref[0:HALO, :] = jnp.where(i == 0, 0.0, halo_ref[...])
    pad_ref[HALO:HALO + tm, :] = x_ref[...]
    c = _conv_taps(pad_ref, w_ref, tm, HALO - (GDN_CONV - 1))
    s = _sigmoid(c)
    y = c * s
    r = lax.rsqrt(_head_sums(y * y, seg_ref[...]) + RMS_EPS)
    scale = jnp.where(grp < 1, GDN_D ** -0.5, 1.0)
    return grp < 2, c, s, y, r, scale


def gdn_pre_fwd(proj, conv_w, name):
    T = proj.shape[0]
    tm = _pick(T, GDN_PRE_ROWS, 8)
    GW = GDN_GROUP_W

    def body(x_ref, halo_ref, w_ref, seg_ref, o_ref, pad_ref):
        normed, c, s, y, r, scale = _gdn_pre_common(x_ref, halo_ref, w_ref, seg_ref, pad_ref, tm)
        o_ref[...] = jnp.where(normed, y * r * scale, y)

    return pl.pallas_call(
        body, grid=(3, T // tm),
        in_specs=[pl.BlockSpec((tm, GW), lambda hb, i: (i, hb)),
                  pl.BlockSpec((HALO, GW), lambda hb, i: (jnp.maximum(i * (tm // HALO) - 1, 0), hb)),
                  pl.BlockSpec((GDN_CONV, GW), lambda hb, i: (0, hb)), pl.BlockSpec((GW, GW), lambda hb, i: (0, 0))],
        out_specs=pl.BlockSpec((tm, GW), lambda hb, i: (i, hb)),
        out_shape=jax.ShapeDtypeStruct((T, 3 * GW), F32),
        scratch_shapes=[pltpu.VMEM((tm + HALO, GW), F32)],
        compiler_params=_cp(("parallel", "parallel")), name=name)(proj, proj, conv_w, _head_segments())


def gdn_pre_bwd_pointwise(proj, conv_w, dqkv, name, hosted=None):
    T = proj.shape[0]
    tm = _pick(T, GDN_PRE_ROWS, 8)
    GW = GDN_GROUP_W

    def body(x_ref, halo_ref, w_ref, seg_ref, d_ref, dc_ref, dw_ref, pad_ref):
        i = pl.program_id(1)
        normed, c, s, y, r, scale = _gdn_pre_common(x_ref, halo_ref, w_ref, seg_ref, pad_ref, tm)

        @pl.when(i == 0)
        def _():
            dw_ref[...] = jnp.zeros_like(dw_ref)

        d = d_ref[...]
        n = y * r
        dn = d * scale
        dy = jnp.where(normed, r * (dn - n * _head_sums(dn * n, seg_ref[...])), d)
        dc = dy * (s * (1.0 + c * (1.0 - s)))
        dc_ref[...] = dc
        for k in range(GDN_CONV):
            xs = pad_ref[pl.ds(HALO - (GDN_CONV - 1) + k, tm), :]
            dw_ref[k:k + 1, :] += jnp.sum(dc * xs, axis=0, keepdims=True)

    blk = pl.BlockSpec((tm, GW), lambda hb, i: (i, hb))
    wblk = pl.BlockSpec((GDN_CONV, GW), lambda hb, i: (0, hb))
    return _hosted_call(
        hosted, body, grid=(3, T // tm),
        in_specs=[blk, pl.BlockSpec((HALO, GW), lambda hb, i: (jnp.maximum(i * (tm // HALO) - 1, 0), hb)), wblk,
                  pl.BlockSpec((GW, GW), lambda hb, i: (0, 0)), blk],
        out_specs=[blk, wblk],
        out_shape=[jax.ShapeDtypeStruct((T, 3 * GW), F32), jax.ShapeDtypeStruct((GDN_CONV, 3 * GW), F32)],
        scratch_shapes=[pltpu.VMEM((tm + HALO, GW), F32)],
        compiler_params=_cp(("parallel", "arbitrary")), name=name)(proj, proj, conv_w, _head_segments(), dqkv)


def gdn_pre_bwd_conv(dc, conv_w_p, name):
    T = dc.shape[0]
    tm = _pick(T, GDN_PRE_ROWS, 8)
    nt = T // tm
    GW = GDN_GROUP_W

    def body(dc_ref, halo_ref, w_ref, dx_ref, pad_ref):
        i = pl.program_id(1)
        pad_ref[0:tm, :] = dc_ref[...]
        pad_ref[tm:tm + HALO, :] = jnp.where(i == nt - 1, 0.0, halo_ref[...])
        acc = w_ref[GDN_CONV - 1:GDN_CONV, :] * pad_ref[pl.ds(0, tm), :]
        for k in range(GDN_CONV - 1):
            acc = acc + w_ref[k:k + 1, :] * pad_ref[pl.ds(GDN_CONV - 1 - k, tm), :]
        dx_ref[...] = acc

    blk = pl.BlockSpec((tm, GW), lambda hb, i: (i, hb))
    return pl.pallas_call(
        body, grid=(3, nt),
        in_specs=[blk, pl.BlockSpec((HALO, GW), lambda hb, i: (jnp.minimum((i + 1) * (tm // HALO), T // HALO - 1), hb)),
                  pl.BlockSpec((GDN_CONV, GW), lambda hb, i: (0, hb))],
        out_specs=blk,
        out_shape=jax.ShapeDtypeStruct((T, 3 * GW), F32),
        scratch_shapes=[pltpu.VMEM((tm + HALO, GW), F32)],
        compiler_params=_cp(("parallel", "parallel")), name=name)(dc, dc, conv_w_p)


def _chunk_masks(C):
    row = _iota2((C, C), 0)
    col = _iota2((C, C), 1)
    return row >= col, row > col, row == col


GDN_FWD_CHUNKS = 4
BNN = (((2,), (1,)), ((0,), (0,)))
BNT = (((2,), (2,)), ((0,), (0,)))
BTN = (((1,), (1,)), ((0,), (0,)))


def _bmm(a, b, dims=BNN):
    return lax.dot_general(a.astype(BF16), b.astype(BF16), dims, preferred_element_type=F32)


def _hbmm(a, b):
    m = a.shape[1]
    a_hi, a_lo = _split2(a)
    b_hi, b_lo = _split2(b)
    r = lax.dot_general(jnp.concatenate([a_hi, a_lo], axis=1), b_hi, BNN, preferred_element_type=F32)
    return r[:, :m] + r[:, m:] + lax.dot_general(a_hi, b_lo, BNN, preferred_element_type=F32)


def _hbmm_tn(a, b):
    a_hi, a_lo = _split2(a)
    b_hi, b_lo = _split2(b)
    d = functools.partial(lax.dot_general, dimension_numbers=BTN, preferred_element_type=F32)
    return d(a_hi, b_hi) + d(a_lo, b_hi) + d(a_hi, b_lo)


def _col_to_row(colv, eye):
    return jnp.sum(jnp.where(eye, colv, 0.0), axis=1, keepdims=True)


def _row_to_col(rowv, eye):
    return jnp.sum(jnp.where(eye, rowv, 0.0), axis=2, keepdims=True)


def _unit_lower_inverse(A, eye):
    C = A.shape[1]
    P = jnp.where(eye, 1.0, 0.0) - A
    Bp = _hbmm(A, A)
    for _ in range(4):
        R = _hbmm(jnp.concatenate([Bp, P], axis=1), Bp)
        Bp = R[:, :C]
        P = P + R[:, C:]
    return P + _hbmm(P, Bp)


def _stack_heads(ref, first_head, n, row0=0):
    rows = pl.ds(row0, GDN_CHUNK)
    return jnp.stack([ref[rows, pl.ds((first_head + h) * GDN_D, GDN_D)] for h in range(n)])


def _unstack_heads(ref, first_head, val, row0=0):
    rows = pl.ds(row0, GDN_CHUNK)
    for h in range(val.shape[0]):
        ref[rows, pl.ds((first_head + h) * GDN_D, GDN_D)] = val[h]


def _gdn_gates(gab, a_row, dt_row, incl):
    g_all = -jnp.exp(a_row) * _softplus(gab + dt_row)
    beta_all = _sigmoid(gab)
    gc_all = _ones_dot_left(incl.astype(BF16), g_all)
    return g_all, beta_all, gc_all


def _gdn_common(qkv_ref, gc_all, beta_all, incl, strict, eye, row0=0):
    C, H = GDN_CHUNK, GDN_HEADS
    q, k, v = (_stack_heads(qkv_ref, j * H, H, row0) for j in range(3))
    gc = jnp.stack([gc_all[:, h:h + 1] for h in range(H)])
    beta = jnp.stack([beta_all[:, H + h:H + h + 1] for h in range(H)])
    gc_row = _col_to_row(gc, eye)
    decay = jnp.where(incl, jnp.exp(jnp.where(incl, gc - gc_row, 0.0)), 0.0)
    e_gc = jnp.exp(gc)
    gl = gc[:, C - 1:C, :]
    e_gl = jnp.exp(gl)
    ekd = jnp.exp(gl - gc)
    kb = k * beta
    A = jnp.where(strict, _bmm(kb, k, BNT) * decay, 0.0)
    Pm = jnp.where(incl, _bmm(q, k, BNT) * decay, 0.0)
    return q, k, v, gc, beta, decay, e_gc, e_gl, ekd, kb, A, Pm


def gdn_chunk_fwd(qkv, proj, a_row, dt_row, norm_w, name, hosted=None):
    T = qkv.shape[0]
    C, H, Dh = GDN_CHUNK, GDN_HEADS, GDN_D
    N = T // C
    J = GDN_FWD_CHUNKS if N % GDN_FWD_CHUNKS == 0 else 1

    def body(qkv_ref, gz_ref, gab_ref, a_ref, dt_ref, nw_ref, o_ref, opre_ref, Tm_ref, Sin_ref, S_ref):
        n = pl.program_id(0)

        @pl.when(n == 0)
        def _():
            S_ref[...] = jnp.zeros_like(S_ref)

        incl, strict, eye = _chunk_masks(C)
        gab = gab_ref[...]
        per_chunk = []
        for j in range(J):
            _, beta_all, gc_all = _gdn_gates(gab[j * C:(j + 1) * C], a_ref[...], dt_ref[...], incl)
            per_chunk.append(_gdn_common(qkv_ref, gc_all, beta_all, incl, strict, eye, row0=j * C))
        q, k, v, gc, beta, decay, e_gc, e_gl, ekd, kb, A, Pm = (jnp.concatenate(t, axis=0) for t in zip(*per_chunk))
        Tm = _unit_lower_inverse(A, eye)
        u = _hbmm(Tm, v * beta)
        w = _hbmm(Tm, kb * e_gc)
        qd = q * e_gc
        kd = k * ekd
        S = S_ref[...]
        for j in range(J):
            hs = slice(j * H, (j + 1) * H)
            v_new = u[hs] - _bmm(w[hs], S)
            o = _bmm(qd[hs], S) + _bmm(Pm[hs], v_new)
            Sin_ref[j] = S
            Tm_ref[j] = Tm[hs]
            S = S * e_gl[hs] + _bmm(kd[hs], v_new, BTN)
            r = lax.rsqrt(jnp.mean(o * o, axis=-1, keepdims=True) + RMS_EPS)
            gz = _stack_heads(gz_ref, 0, H, j * C)
            _unstack_heads(opre_ref, 0, o, j * C)
            _unstack_heads(o_ref, 0, o * r * nw_ref[...] * (gz * _sigmoid(gz)), j * C)
        S_ref[...] = S

    vec = pl.BlockSpec((1, LANES), lambda n: (0, 0))
    hblk = pl.BlockSpec((J * C, GDN_W), lambda n: (n, 0))
    sblk = pl.BlockSpec((J, H, Dh, Dh), lambda n: (n, 0, 0, 0))
    return _hosted_call(
        hosted, body, grid=(N // J,),
        in_specs=[pl.BlockSpec((J * C, 3 * GDN_W), lambda n: (n, 0)),
                  pl.BlockSpec((J * C, GDN_W), lambda n: (n, CB_GZ * LANES // GDN_W)),
                  pl.BlockSpec((J * C, LANES), lambda n: (n, CB_GAB)), vec, vec, pl.BlockSpec((1, Dh), lambda n: (0, 0))],
        out_specs=[hblk, hblk, sblk, sblk],
        out_shape=[jax.ShapeDtypeStruct((T, GDN_W), F32), jax.ShapeDtypeStruct((T, GDN_W), F32)]
        + [jax.ShapeDtypeStruct((N, H, Dh, Dh), F32)] * 2,
        scratch_shapes=[pltpu.VMEM((H, Dh, Dh), F32)],
        compiler_params=_cp(("arbitrary",)), name=name)(qkv, proj, proj, a_row, dt_row, norm_w)


def gdn_chunk_bwd(qkv, proj, a_row, dt_row, norm_w, opre, Tm_all, Sin_all, docat, name, hosted=None):
    T = qkv.shape[0]
    C, H, Dh = GDN_CHUNK, GDN_HEADS, GDN_D
    N = T // C

    def body(qkv_ref, gz_ref, gab_ref, a_ref, dt_ref, nw_ref, opre_ref, Tm_ref, Sin_ref, do_ref,
             dqkv_ref, dgz_ref, dgab_ref, da_ref, ddt_ref, dnw_ref, dS_ref):
        n = pl.program_id(0)

        @pl.when(n == 0)
        def _():
            dS_ref[...] = jnp.zeros_like(dS_ref)
            da_ref[...] = jnp.zeros_like(da_ref)
            ddt_ref[...] = jnp.zeros_like(ddt_ref)
            dnw_ref[...] = jnp.zeros_like(dnw_ref)

        incl, strict, eye = _chunk_masks(C)
        gab = gab_ref[...]
        g_all, beta_all, gc_all = _gdn_gates(gab, a_ref[...], dt_ref[...], incl)
        lane = _iota2((C, LANES), 1)
        rowi = _iota2((C, 1), 0)
        nw = nw_ref[...]
        q, k, v, gc, beta, decay, e_gc, e_gl, ekd, kb, A, Pm = _gdn_common(qkv_ref, gc_all, beta_all, incl, strict, eye)
        Tm = Tm_ref[0]
        S = Sin_ref[0]
        dS = dS_ref[...]
        kbe = kb * e_gc
        u = _hbmm(Tm, v * beta)
        w = _hbmm(Tm, kbe)
        qd = q * e_gc
        kd = k * ekd
        v_new = u - _bmm(w, S)
        o = _stack_heads(opre_ref, 0, H)
        gz = _stack_heads(gz_ref, 0, H)
        don = _stack_heads(do_ref, 0, H)
        r = lax.rsqrt(jnp.mean(o * o, axis=-1, keepdims=True) + RMS_EPS)
        nn = o * r
        sgz = _sigmoid(gz)
        silu = gz * sgz
        _unstack_heads(dgz_ref, 0, don * nn * nw * (sgz * (1.0 + gz * (1.0 - sgz))))
        dnn = don * nw * silu
        dnw_ref[...] += jnp.sum(jnp.sum(don * nn * silu, axis=0), axis=0, keepdims=True)
        do = r * (dnn - nn * jnp.mean(dnn * nn, axis=-1, keepdims=True))
        dv_new = _bmm(Pm, do, BTN) + _bmm(kd, dS)
        dPm = jnp.where(incl, _bmm(do, v_new, BNT), 0.0)
        dqd = _bmm(do, S, BNT)
        dkd = _bmm(v_new, dS, BNT)
        dS_ref[...] = _bmm(qd, do, BTN) + e_gl * dS - _bmm(w, dv_new, BTN)
        dgl = jnp.sum(jnp.sum(dS * S, axis=2, keepdims=True), axis=1, keepdims=True) * e_gl
        dw = -_bmm(dv_new, S, BNT)
        dvb = _hbmm_tn(Tm, dv_new)
        dkbe = _hbmm_tn(Tm, dw)
        dA = -jnp.where(strict, _bmm(dvb, u, BNT) + _bmm(dkbe, w, BNT), 0.0)
        dAD = dA * decay
        dPD = dPm * decay
        Gm = dA * A + dPm * Pm
        dgc = jnp.sum(Gm, axis=2, keepdims=True) - _row_to_col(jnp.sum(Gm, axis=1, keepdims=True), eye)
        dkb = _bmm(dAD, k) + dkbe * e_gc
        dk = _bmm(dAD, kb, BTN) + _bmm(dPD, q, BTN) + dkd * ekd + dkb * beta
        dq = _bmm(dPD, k) + dqd * e_gc
        tkd = jnp.sum(dkd * kd, axis=-1, keepdims=True)
        dgc = dgc + jnp.sum(dqd * qd, axis=-1, keepdims=True) - tkd + jnp.sum(dkbe * kbe, axis=-1, keepdims=True)
        dgl = dgl + jnp.sum(tkd, axis=1, keepdims=True)
        dgc = dgc + jnp.where(rowi == C - 1, dgl, 0.0)
        dbeta = jnp.sum(dvb * v, axis=-1, keepdims=True) + jnp.sum(dkb * k, axis=-1, keepdims=True)
        _unstack_heads(dqkv_ref, 0, dq)
        _unstack_heads(dqkv_ref, H, dk)
        _unstack_heads(dqkv_ref, 2 * H, dvb * beta)
        dgc_all = jnp.zeros((C, LANES), F32)
        dbeta_all = jnp.zeros((C, LANES), F32)
        for h in range(H):
            dgc_all = dgc_all + jnp.where(lane == h, dgc[h], 0.0)
            dbeta_all = dbeta_all + jnp.where(lane == H + h, dbeta[h], 0.0)
        upper = (_iota2((C, C), 0) <= _iota2((C, C), 1)).astype(BF16)
        dg_all = _ones_dot_left(upper, dgc_all)
        dga = dg_all * (-jnp.exp(a_ref[...])) * _sigmoid(gab + dt_ref[...])
        dgb = dbeta_all * beta_all * (1.0 - beta_all)
        dgab_ref[...] = jnp.where(lane < H, dga, jnp.where(lane < 2 * H, dgb, 0.0))
        da_ref[...] += jnp.sum(jnp.where(lane < H, dg_all * g_all, 0.0), axis=0, keepdims=True)
        ddt_ref[...] += jnp.sum(jnp.where(lane < H, dga, 0.0), axis=0, keepdims=True)

    rev = lambda n: N - 1 - n
    vec = pl.BlockSpec((1, LANES), lambda n: (0, 0))
    nwv = pl.BlockSpec((1, Dh), lambda n: (0, 0))
    hblk = pl.BlockSpec((C, GDN_W), lambda n: (rev(n), 0))
    sblk = pl.BlockSpec((1, H, Dh, Dh), lambda n: (rev(n), 0, 0, 0))
    qblk = pl.BlockSpec((C, 3 * GDN_W), lambda n: (rev(n), 0))
    return _hosted_call(
        hosted, body, grid=(N,),
        in_specs=[qblk, pl.BlockSpec((C, GDN_W), lambda n: (rev(n), CB_GZ * LANES // GDN_W)),
                  pl.BlockSpec((C, LANES), lambda n: (rev(n), CB_GAB)), vec, vec, nwv, hblk, sblk, sblk, hblk],
        out_specs=[qblk, hblk, pl.BlockSpec((C, LANES), lambda n: (rev(n), 0)), vec, vec, nwv],
        out_shape=[jax.ShapeDtypeStruct((T, 3 * GDN_W), F32), jax.ShapeDtypeStruct((T, GDN_W), F32),
                   jax.ShapeDtypeStruct((T, LANES), F32), jax.ShapeDtypeStruct((1, LANES), F32),
                   jax.ShapeDtypeStruct((1, LANES), F32), jax.ShapeDtypeStruct((1, Dh), F32)],
        scratch_shapes=[pltpu.VMEM((H, Dh, Dh), F32)],
        compiler_params=_cp(("arbitrary",)), name=name)(qkv, proj, proj, a_row, dt_row, norm_w, opre, Tm_all, Sin_all, docat)


ATT_BQ, ATT_BK = 512, 1024
NEG_BIG = -1e30


def _att_blocks(T):
    bq, bk = min(ATT_BQ, T), min(ATT_BK, T)
    assert bk % bq == 0 and T % bk == 0
    return bq, bk


def _att_specs(T, bq, cbs):
    qspec = lambda cb: pl.BlockSpec((bq, LANES), lambda h, i: (i, cb + h))
    kspec = lambda cb: pl.BlockSpec((T, LANES), lambda h, i: (0, cb + h))
    return qspec, kspec


def _kblock(ref, kb, bk):
    return ref[pl.ds(pl.multiple_of(kb * bk, bk), bk), :]


def _att_pos(i, kb, bq, bk):
    qpos = i * bq + _iota2((bq, bk), 0)
    kpos = kb * bk + _iota2((bq, bk), 1)
    return qpos, kpos


def _later_keys(n):
    return (_iota2((n, n), 0) > _iota2((n, n), 1)).astype(BF16)


def _earlier_keys(n):
    return (_iota2((n, n), 0) < _iota2((n, n), 1)).astype(BF16)


def _tri_dot(x, tri, terms):
    acc, rest = None, x
    for t in range(terms):
        part = rest.astype(BF16)
        if t + 1 < terms:
            rest = rest - part.astype(F32)
        d = jnp.dot(part, tri, preferred_element_type=F32)
        acc = d if acc is None else acc + d
    return acc


SB_BLOCK = 256
SB_DEAD = -104.0


def _sb_blocks(T):
    b = min(SB_BLOCK, T)
    assert T % b == 0 and T // b <= LANES
    return b, b


def sb_fwd(proj, name, hosted=None):
    T = proj.shape[0]
    H = SB_HEADS
    bq, bk = _sb_blocks(T)
    scale = SB_DIM ** -0.5

    def body(q_ref, k_ref, v_ref, o_ref, tot_ref):
        i = pl.program_id(1)
        qb = q_ref[...].astype(BF16)
        diag = (i * bq) // bk
        lane = _iota2((bq, LANES), 1)
        later = _later_keys(bk)

        def block(kb, acc, R, masked):
            z = _bdot_nt(qb, _kblock(k_ref, kb, bk)) * scale
            sp = _softplus(z)
            if masked:
                qpos, kpos = _att_pos(i, kb, bq, bk)
                mask = kpos < qpos
                l1m = jnp.where(mask, -sp, 0.0)
            else:
                l1m = -sp
            W = jnp.exp((z - sp) + _tri_dot(l1m, later, 3) + R)
            if masked:
                W = jnp.where(mask, W, 0.0)
            acc = acc + _bdot(W, _kblock(v_ref, kb, bk))
            return acc, R + jnp.sum(l1m, axis=-1, keepdims=True)

        acc, R = block(diag, jnp.zeros((bq, LANES), F32), jnp.zeros((bq, 1), F32), True)

        def live(c):
            return jnp.logical_and(c[0] >= 0, jnp.max(c[2]) > SB_DEAD)

        def step(c):
            kb, acc, R, Rb = c
            acc, R_next = block(kb, acc, R, False)
            return kb - 1, acc, R_next, jnp.where(lane == kb, R, Rb)

        _, acc, _, Rb = lax.while_loop(live, step, (diag - 1, acc, R, jnp.where(lane == diag, 0.0, NEG_BIG)))
        o_ref[...] = acc
        tot_ref[...] = Rb

    qspec, kspec = _att_specs(T, bq, None)
    sds = jax.ShapeDtypeStruct((T, H * LANES), F32)
    oblk = pl.BlockSpec((bq, LANES), lambda h, i: (i, h))
    return _hosted_call(
        hosted, body, grid=(H, T // bq), in_specs=[qspec(CB_SQ), kspec(CB_SK), kspec(CB_SV)],
        out_specs=[oblk, oblk], out_shape=[sds, sds],
        compiler_params=_cp(("parallel", "parallel")), name=name)(proj, proj, proj)


def sb_bwd(proj, tot, docat, do_cb, name):
    T = proj.shape[0]
    H = SB_HEADS
    bq, bk = _sb_blocks(T)
    scale = SB_DIM ** -0.5

    def body(q_ref, k_ref, v_ref, tot_ref, do_ref, dq_ref, dk_ref, dv_ref):
        i = pl.program_id(1)

        @pl.when(i == 0)
        def _():
            dk_ref[...] = jnp.zeros_like(dk_ref)
            dv_ref[...] = jnp.zeros_like(dv_ref)

        qb = q_ref[...].astype(BF16)
        dob = do_ref[...].astype(BF16)
        Rb = tot_ref[...]
        diag = (i * bq) // bk
        lane = _iota2((bq, LANES), 1)
        later, earlier = _later_keys(bk), _earlier_keys(bk)
        first = lax.while_loop(
            lambda kb: jnp.logical_and(kb < diag, jnp.max(jnp.where(lane == kb, Rb, NEG_BIG)) <= SB_DEAD),
            lambda kb: kb + 1, jnp.int32(0))

        def block(kb, carry, masked):
            dq, Epre = carry
            R = jnp.sum(jnp.where(lane == kb, Rb, 0.0), axis=1, keepdims=True)
            kblk = _kblock(k_ref, kb, bk).astype(BF16)
            z = _bdot_nt(qb, kblk) * scale
            sp = _softplus(z)
            if masked:
                qpos, kpos = _att_pos(i, kb, bq, bk)
                mask = kpos < qpos
                l1m = jnp.where(mask, -sp, 0.0)
            else:
                l1m = -sp
            W = jnp.exp((z - sp) + _tri_dot(l1m, later, 3) + R)
            if masked:
                W = jnp.where(mask, W, 0.0)
            E = _bdot_nt(dob, _kblock(v_ref, kb, bk)) * W
            cexcl = _tri_dot(E, earlier, 3) + Epre
            neg = jnp.exp(-sp)
            dz = E * neg - cexcl * (1.0 - neg)
            if masked:
                dz = jnp.where(mask, dz, 0.0)
            dz = (dz * scale).astype(BF16)
            rows = pl.ds(pl.multiple_of(kb * bk, bk), bk)
            dk_ref[rows, :] += lax.dot_general(dz, qb, TN_DIMS, preferred_element_type=F32)
            dv_ref[rows, :] += lax.dot_general(W.astype(BF16), dob, TN_DIMS, preferred_element_type=F32)
            dq = dq + jnp.dot(dz, kblk, preferred_element_type=F32)
            return dq, Epre + jnp.sum(E, axis=-1, keepdims=True)

        init = (jnp.zeros((bq, LANES), F32), jnp.zeros((bq, 1), F32))
        carry = lax.fori_loop(first, diag, lambda kb, c: block(kb, c, False), init)
        dq, _ = block(diag, carry, True)
        dq_ref[...] = dq

    qspec, kspec = _att_specs(T, bq, None)
    sds = jax.ShapeDtypeStruct((T, H * LANES), F32)
    oblk = pl.BlockSpec((bq, LANES), lambda h, i: (i, h))
    kout = pl.BlockSpec((T, LANES), lambda h, i: (0, h))
    return pl.pallas_call(
        body, grid=(H, T // bq),
        in_specs=[qspec(CB_SQ), kspec(CB_SK), kspec(CB_SV), oblk, qspec(do_cb)],
        out_specs=[oblk, kout, kout], out_shape=[sds, sds, sds],
        compiler_params=_cp(("arbitrary", "arbitrary")), name=name)(proj, proj, proj, tot, docat)


def mla_fwd(Q, K, V, name, hosted=None):
    T = Q.shape[0]
    H = MLA_HEADS
    bq, bk = _att_blocks(T)
    scale = (MLA_NOPE + MLA_ROPE) ** -0.5

    def body(q_ref, k_ref, v_ref, o_ref, lse_ref):
        i = pl.program_id(1)
        qb = q_ref[...]
        diag = (i * bq) // bk

        def block(kb, carry, masked):
            acc, m, l = carry
            s = _bdot_nt(qb, _kblock(k_ref, kb, bk)) * scale
            if masked:
                qpos, kpos = _att_pos(i, kb, bq, bk)
                s = jnp.where(kpos <= qpos, s, NEG_BIG)
            m_new = jnp.maximum(m, jnp.max(s, axis=-1, keepdims=True))
            p = jnp.exp(s - m_new)
            corr = jnp.exp(m - m_new)
            acc = corr * acc + _bdot(p, _kblock(v_ref, kb, bk))
            return acc, m_new, corr * l + jnp.sum(p, axis=-1, keepdims=True)

        init = (jnp.zeros((bq, LANES), F32), jnp.full((bq, 1), NEG_BIG, F32), jnp.zeros((bq, 1), F32))
        carry = lax.fori_loop(0, diag, lambda kb, c: block(kb, c, False), init)
        acc, m, l = block(diag, carry, True)
        o_ref[...] = acc / l
        lse_ref[...] = jnp.broadcast_to(m + jnp.log(l), (bq, LANES))

    qspec, kspec = _att_specs(T, bq, None)
    sds = jax.ShapeDtypeStruct((T, H * LANES), F32)
    oblk = pl.BlockSpec((bq, LANES), lambda h, i: (i, h))
    return _hosted_call(
        hosted, body, grid=(H, T // bq), in_specs=[qspec(0), kspec(0), kspec(0)],
        out_specs=[oblk, oblk], out_shape=[sds, sds],
        compiler_params=_cp(("parallel", "parallel")), name=name)(Q, K, V)


def mla_bwd(Q, K, V, o, lse, docat, do_cb, name, hosted=None):
    T = Q.shape[0]
    H = MLA_HEADS
    bq, bk = _att_blocks(T)
    scale = (MLA_NOPE + MLA_ROPE) ** -0.5

    def body(q_ref, k_ref, v_ref, o_ref, lse_ref, do_ref, dq_ref, dk_ref, dv_ref):
        i = pl.program_id(1)

        @pl.when(i == 0)
        def _():
            dk_ref[...] = jnp.zeros_like(dk_ref)
            dv_ref[...] = jnp.zeros_like(dv_ref)

        qb = q_ref[...]
        do = do_ref[...]
        dob = do.astype(BF16)
        delta = jnp.sum(do * o_ref[...], axis=-1, keepdims=True)
        lse = lse_ref[:, 0:1]

        diag = (i * bq) // bk

        def block(kb, dq, masked):
            kblk = _kblock(k_ref, kb, bk)
            s = _bdot_nt(qb, kblk) * scale
            if masked:
                qpos, kpos = _att_pos(i, kb, bq, bk)
                s = jnp.where(kpos <= qpos, s, NEG_BIG)
            p = jnp.exp(s - lse)
            dp = _bdot_nt(dob, _kblock(v_ref, kb, bk))
            ds = (p * (dp - delta) * scale).astype(BF16)
            rows = pl.ds(pl.multiple_of(kb * bk, bk), bk)
            dk_ref[rows, :] += lax.dot_general(ds, qb, TN_DIMS, preferred_element_type=F32)
            dv_ref[rows, :] += lax.dot_general(p.astype(BF16), dob, TN_DIMS, preferred_element_type=F32)
            return dq + jnp.dot(ds, kblk, preferred_element_type=F32)

        dq = lax.fori_loop(0, diag, lambda kb, c: block(kb, c, False), jnp.zeros((bq, LANES), F32))
        dq_ref[...] = block(diag, dq, True)

    qspec, kspec = _att_specs(T, bq, None)
    sds = jax.ShapeDtypeStruct((T, H * LANES), F32)
    oblk = pl.BlockSpec((bq, LANES), lambda h, i: (i, h))
    kout = pl.BlockSpec((T, LANES), lambda h, i: (0, h))
    return _hosted_call(
        hosted, body, grid=(H, T // bq),
        in_specs=[qspec(0), kspec(0), kspec(0), oblk, oblk, qspec(do_cb)],
        out_specs=[oblk, kout, kout], out_shape=[sds, sds, sds],
        compiler_params=_cp(("arbitrary", "arbitrary")), name=name)(Q, K, V, o, lse, docat)


def _tile_heads(t, n):
    return jnp.concatenate([t] * n, axis=1)


def _rope(X, C, Sn, Sp):
    n = X.shape[1]
    return X * C + pltpu.roll(X, n - HALF_ROPE, 1) * Sn + pltpu.roll(X, HALF_ROPE, 1) * Sp


def _rope_t(dO, C, Sn, Sp):
    n = dO.shape[1]
    return dO * C + pltpu.roll(dO * Sn, HALF_ROPE, 1) + pltpu.roll(dO * Sp, n - HALF_ROPE, 1)


def _rms(x, w):
    r = lax.rsqrt(jnp.mean(x * x, axis=-1, keepdims=True) + RMS_EPS)
    xh = x * r
    return r, xh, xh * w


def _rms_bwd(dn, w, r, xh):
    dxh = dn * w
    return r * (dxh - xh * jnp.mean(dxh * xh, axis=-1, keepdims=True)), jnp.sum(dn * xh, axis=0, keepdims=True)


def _mla_pre_specs(T, tm):
    KV = MLA_KV_RANK
    QR = MLA_Q_RANK
    W = MLA_HEADS * LANES
    full = lambda shape: pl.BlockSpec(shape, lambda i: (0, 0))
    specs = [pl.BlockSpec((tm, QR), lambda i: (i, CB_MQ * LANES // QR)),
             pl.BlockSpec((tm, 2 * LANES), lambda i: (i, CB_MKV // 2)),
             full((1, QR)), full((1, KV))]
    rope = [pl.BlockSpec((tm, LANES), lambda i: (i, 0))] * 3
    return specs, rope, full, W


def mla_pre_fwd(proj, wq, wkv, wuq, wuk, wuv, ropeC, ropeSn, ropeSp, name):
    T = proj.shape[0]
    tm = _pick(T, 512, 16)
    KV = MLA_KV_RANK
    H = MLA_HEADS

    def body(mq_ref, mkv_ref, wq_ref, wkv_ref, wuq_ref, wuk_ref, wuv_ref, c_ref, sn_ref, sp_ref, Q_ref, K_ref, V_ref):
        C, Sn, Sp = (_tile_heads(t[...], H) for t in (c_ref, sn_ref, sp_ref))
        _, _, qn = _rms(mq_ref[...], wq_ref[...])
        Q_ref[...] = _rope(_bdot(qn, wuq_ref[...]), C, Sn, Sp).astype(BF16)
        mkv = mkv_ref[...]
        _, _, kvn = _rms(mkv[:, :KV], wkv_ref[...])
        kr = pltpu.roll(mkv[:, KV:], MLA_NOPE, 1)
        K_ref[...] = _rope(_bdot(kvn, wuk_ref[...]) + _tile_heads(kr, H), C, Sn, Sp).astype(BF16)
        V_ref[...] = _bdot(kvn, wuv_ref[...]).astype(BF16)

    specs, rope, full, W = _mla_pre_specs(T, tm)
    oblk = pl.BlockSpec((tm, W), lambda i: (i, 0))
    sds = jax.ShapeDtypeStruct((T, W), BF16)
    return pl.pallas_call(
        body, grid=(T // tm,),
        in_specs=specs + [full(wuq.shape), full(wuk.shape), full(wuv.shape)] + rope,
        out_specs=[oblk, oblk, oblk], out_shape=[sds, sds, sds],
        compiler_params=_cp(("parallel",)), name=name)(proj, proj, wq, wkv, wuq, wuk, wuv, ropeC, ropeSn, ropeSp)


def mla_pre_bwd(proj, wq, wkv, wuq, wuk, wuv, ropeC, ropeSn, ropeSp, dQ, dK, dV, name):
    T = proj.shape[0]
    tm = _pick(T, 512, 16)
    KV = MLA_KV_RANK
    H = MLA_HEADS

    def body(mq_ref, mkv_ref, wq_ref, wkv_ref, wuq_ref, wuk_ref, wuv_ref,
             c_ref, sn_ref, sp_ref, dQ_ref, dK_ref, dV_ref,
             dmq_ref, dmkv_ref, dwuq_ref, dwuk_ref, dwuv_ref, dwq_ref, dwkv_ref):
        i = pl.program_id(0)

        @pl.when(i == 0)
        def _():
            for ref in (dwuq_ref, dwuk_ref, dwuv_ref, dwq_ref, dwkv_ref):
                ref[...] = jnp.zeros_like(ref)

        C, Sn, Sp = (_tile_heads(t[...], H) for t in (c_ref, sn_ref, sp_ref))
        rq, xq, qn = _rms(mq_ref[...], wq_ref[...])
        mkv = mkv_ref[...]
        rkv, xkv, kvn = _rms(mkv[:, :KV], wkv_ref[...])
        dqf = _rope_t(dQ_ref[...], C, Sn, Sp)
        dkf = _rope_t(dK_ref[...], C, Sn, Sp)
        dv = dV_ref[...]
        dwuq_ref[...] += _bdot_tn(qn, dqf)
        dwuk_ref[...] += _bdot_tn(kvn, dkf)
        dwuv_ref[...] += _bdot_tn(kvn, dv)
        dmq, dwq = _rms_bwd(_bdot_nt(dqf, wuq_ref[...]), wq_ref[...], rq, xq)
        dckv, dwkv = _rms_bwd(_bdot_nt(dkf, wuk_ref[...]) + _bdot_nt(dv, wuv_ref[...]), wkv_ref[...], rkv, xkv)
        dwq_ref[...] += dwq
        dwkv_ref[...] += dwkv
        dmq_ref[...] = dmq
        dkr = dkf[:, 0:LANES]
        for h in range(1, H):
            dkr = dkr + dkf[:, h * LANES:(h + 1) * LANES]
        dkr = pltpu.roll(dkr, LANES - MLA_NOPE, 1)
        dkr = jnp.where(_iota2(dkr.shape, 1) < MLA_ROPE, dkr, 0.0)
        dmkv_ref[...] = jnp.concatenate([dckv, dkr], axis=1)

    specs, rope, full, W = _mla_pre_specs(T, tm)
    wide = pl.BlockSpec((tm, W), lambda i: (i, 0))
    return pl.pallas_call(
        body, grid=(T // tm,),
        in_specs=specs + [full(w.shape) for w in (wuq, wuk, wuv)] + rope + [wide, wide, wide],
        out_specs=[pl.BlockSpec((tm, MLA_Q_RANK), lambda i: (i, 0)), pl.BlockSpec((tm, 2 * LANES), lambda i: (i, 0)),
                   full(wuq.shape), full(wuk.shape), full(wuv.shape), full((1, MLA_Q_RANK)), full((1, KV))],
        out_shape=[jax.ShapeDtypeStruct((T, MLA_Q_RANK), F32), jax.ShapeDtypeStruct((T, 2 * LANES), F32),
                   jax.ShapeDtypeStruct(wuq.shape, F32), jax.ShapeDtypeStruct(wuk.shape, F32),
                   jax.ShapeDtypeStruct(wuv.shape, F32), jax.ShapeDtypeStruct((1, MLA_Q_RANK), F32),
                   jax.ShapeDtypeStruct((1, KV), F32)],
        compiler_params=_cp(("arbitrary",)), name=name)(
            proj, proj, wq, wkv, wuq, wuk, wuv, ropeC, ropeSn, ropeSp, dQ, dK, dV)


def all_gather(shards, name):
    n = len(shards)

    def body(*refs):
        x_refs, out_refs = refs[:n], refs[n:2 * n]
        send_sems, recv_sems, local_sems = refs[2 * n:]
        x, y, c = _place()
        me, sibling = (x, y, c), (x, y, 1 - c)
        chips = [(1 - x, y), (x, 1 - y), (1 - x, 1 - y)]

        def slot(a, px, py, pc):
            return out_refs[a].at[4 * px + 2 * py + pc]

        def copy(a, k, block, to, src=None):
            return pltpu.make_async_remote_copy(
                src_ref=slot(a, *block) if src is None else src, dst_ref=slot(a, *block),
                send_sem=send_sems.at[a, k], recv_sem=recv_sems.at[a, k], device_id=to, device_id_type=MESH)

        mine = [pltpu.make_async_copy(x_refs[a], slot(a, *me), local_sems.at[a]) for a in range(n)]
        first = []
        for a in range(n):
            mine[a].start()
            first.append(copy(a, 0, me, sibling, src=x_refs[a]))
            first += [copy(a, 1 + j, me, (*chip, c), src=x_refs[a]) for j, chip in enumerate(chips)]
        for cp in first:
            cp.start()
        passed = []
        for j, chip in enumerate(chips):
            for a in range(n):
                copy(a, 1 + j, (*chip, c), me).wait_recv()
                passed.append(copy(a, 4 + j, (*chip, c), sibling))
                passed[-1].start()
        for a in range(n):
            copy(a, 0, sibling, me).wait_recv()
            for j, chip in enumerate(chips):
                copy(a, 4 + j, (*chip, 1 - c), me).wait_recv()
        for cp in first + passed:
            cp.wait_send()
        for cp in mine:
            cp.wait()

    return pl.pallas_call(
        body, out_shape=[jax.ShapeDtypeStruct((N_DEV,) + s.shape, s.dtype) for s in shards],
        in_specs=[ANY] * n, out_specs=[ANY] * n,
        scratch_shapes=[pltpu.SemaphoreType.DMA((n, 7)), pltpu.SemaphoreType.DMA((n, 7)), pltpu.SemaphoreType.DMA((n,))],
        name=name)(*shards)


def reduce_adamw(parts, w, m, v, name):
    L = len(parts)
    n, Rl, C = parts[0].shape
    R = w.shape[0]
    assert R == L * Rl
    tr = Rl if Rl * C <= 256 * 1024 else _pick(Rl, 256, 16)
    nr = Rl // tr

    def body(*refs):
        p_refs = refs[:L]
        w_ref, m_ref, v_ref, g_ref, d_ref, nm_ref, nv_ref, sum_ref = refs[L:]
        grp = pl.program_id(0)
        for j in range(L):
            @pl.when(grp == j)
            def _(j=j):
                acc = p_refs[j][0].astype(F32)
                for s in range(1, n):
                    acc = acc + p_refs[j][s].astype(F32)
                sum_ref[...] = acc

        g_ = sum_ref[...]
        m_ = ADAM_B1 * m_ref[...] + (1.0 - ADAM_B1) * g_
        v_ = ADAM_B2 * v_ref[...] + (1.0 - ADAM_B2) * (g_ * g_)
        m_hat = m_ / (1.0 - ADAM_B1 ** ADAM_STEP)
        v_hat = v_ / (1.0 - ADAM_B2 ** ADAM_STEP)
        g_ref[...] = g_
        d_ref[...] = -ADAM_LR * (m_hat / (jnp.sqrt(v_hat) + ADAM_EPS) + ADAM_WD * w_ref[...])
        nm_ref[...] = m_
        nv_ref[...] = v_

    blk = pl.BlockSpec((tr, C), lambda l, r: (l * nr + r, 0))
    sds = jax.ShapeDtypeStruct((R, C), F32)
    p_specs = [pl.BlockSpec((n, tr, C), lambda l, r, j=j: (0, jnp.where(l == j, r, 0), 0)) for j in range(L)]
    return pl.pallas_call(
        body, grid=(L, nr), in_specs=p_specs + [blk] * 3,
        out_specs=[blk] * 4, out_shape=[sds] * 4, scratch_shapes=[pltpu.VMEM((tr, C), F32)],
        compiler_params=_cp(("arbitrary", "arbitrary")), name=name)(*parts, w, m, v)


SHARDED = {"ffa_w_in": (2, BF16), "ffa_w_out": (1, BF16), "mix_w_in": (2, BF16), "mla_w_uq": (2, BF16),
           "mla_w_ukv": (2, BF16), "mix_w_o": (1, BF16), "ffb_w_in": (2, BF16), "ffb_w_out": (1, BF16),
           "ple_w_gate": (1, BF16), "ple_w_proj": (2, BF16), "gdn_conv_w": (2, F32), "ln_g": (2, F32), "ln_b": (2, F32)}
FFN_SLOT = ("ffa_w_in", "ffa_w_out", "ffb_w_in", "ffb_w_out")
REPLICATED = ("gdn_a_log", "gdn_dt_bias", "gdn_norm_w", "mla_q_norm_w", "mla_kv_norm_w")
WEIGHTS = ("ffa_w_in", "ffa_w_out", "mix_w_in", "gdn_conv_w", "gdn_a_log", "gdn_dt_bias", "gdn_norm_w", "mla_q_norm_w",
           "mla_kv_norm_w", "mla_w_uq", "mla_w_ukv", "mix_w_o", "ffb_w_in", "ffb_w_out", "ln_g", "ln_b", "ple_w_gate",
           "ple_w_proj")


def _to_slots(full, axis):
    L, a, b = full.shape
    if axis == 2:
        return full.reshape(L, a, N_DEV, b // N_DEV).transpose(2, 0, 1, 3).reshape(N_DEV, L * a, b // N_DEV)
    return full.reshape(L, N_DEV, a // N_DEV, b).transpose(1, 0, 2, 3).reshape(N_DEV, L * a // N_DEV, b)


def _from_slots(slots, shard_shape, axis):
    L, a, b = shard_shape
    t = slots.reshape((N_DEV,) + tuple(shard_shape))
    if axis == 2:
        return t.transpose(1, 2, 0, 3).reshape(L, a, N_DEV * b)
    return t.transpose(1, 0, 2, 3).reshape(L, N_DEV * a, b)


def _view2d(t):
    return t.reshape(-1, t.shape[-1])


def _pad_heads(w, nh):
    K = w.shape[0]
    return jnp.pad(w.reshape(K, nh, GDN_D), ((0, 0), (0, 0), (0, LANES - GDN_D))).reshape(K, nh * LANES)


def _unpad_heads(w, nh):
    K = w.shape[0]
    return w.reshape(K, nh, LANES)[:, :, :GDN_D].reshape(K, nh * GDN_D)


IN_WIDTHS = (512, 512, 512, 512, 8, 8, 256, 256, 256, 256, 160)


def _split_in(w):
    offs = np.cumsum((0,) + IN_WIDTHS)
    return [w[:, int(offs[i]):int(offs[i + 1])] for i in range(len(IN_WIDTHS))]


def _pad_in_proj(w):
    gq, gk, gv, gz, ga, gb, sq, sk, sv, mq, mkv = _split_in(w)
    gab = jnp.pad(jnp.concatenate([ga, gb], axis=1), ((0, 0), (0, LANES - 2 * GDN_HEADS)))
    return jnp.concatenate(
        [gq, gk, gv, gz] + [_pad_heads(t, SB_HEADS) for t in (sq, sk, sv)]
        + [mq, jnp.pad(mkv, ((0, 0), (0, 2 * LANES - mkv.shape[1]))), gab], axis=1)


def _unpad_in_proj(wp):
    c = lambda cb, n: wp[:, cb * LANES:(cb + n) * LANES]
    gab = c(CB_GAB, 1)
    parts = [c(cb, DO_SB) for cb in (CB_GQ, CB_GK, CB_GV, CB_GZ)]
    parts += [gab[:, :GDN_HEADS], gab[:, GDN_HEADS:2 * GDN_HEADS]]
    parts += [_unpad_heads(c(cb, SB_HEADS), SB_HEADS) for cb in (CB_SQ, CB_SK, CB_SV)]
    parts += [c(CB_MQ, 2), c(CB_MKV, 2)[:, :MLA_KV_RANK + MLA_ROPE]]
    return jnp.concatenate(parts, axis=1)


def _pad_lanes(w, width):
    return jnp.pad(w, ((0, 0), (0, width - w.shape[1])))


def _mla_up_pad(w_uq, w_ukv):
    H = MLA_HEADS
    dq = MLA_NOPE + MLA_ROPE
    wuq = jnp.pad(w_uq.reshape(-1, H, dq), ((0, 0), (0, 0), (0, LANES - dq))).reshape(-1, H * LANES)
    kv = w_ukv.reshape(-1, H, MLA_NOPE + MLA_V)
    wuk = jnp.pad(kv[:, :, :MLA_NOPE], ((0, 0), (0, 0), (0, LANES - MLA_NOPE))).reshape(-1, H * LANES)
    wuv = jnp.pad(kv[:, :, MLA_NOPE:], ((0, 0), (0, 0), (0, LANES - MLA_V))).reshape(-1, H * LANES)
    return wuq, wuk, wuv


def _mla_up_unpad(dwuq, dwuk, dwuv):
    H = MLA_HEADS
    dq = MLA_NOPE + MLA_ROPE
    g_uq = dwuq.reshape(-1, H, LANES)[:, :, :dq].reshape(-1, H * dq)
    g_ukv = jnp.concatenate([dwuk.reshape(-1, H, LANES)[:, :, :MLA_NOPE], dwuv.reshape(-1, H, LANES)[:, :, :MLA_V]],
                            axis=2).reshape(-1, H * (MLA_NOPE + MLA_V))
    return g_uq, g_ukv


def _rope_tables(positions):
    inv = 1.0 / (ROPE_BASE ** (jnp.arange(0, MLA_ROPE, 2, dtype=F32) / MLA_ROPE))
    ang = positions.astype(F32)[:, None] * inv
    cos, sin = jnp.cos(ang), jnp.sin(ang)
    T = positions.shape[0]
    one = lambda n: jnp.ones((T, n), F32)
    zero = lambda n: jnp.zeros((T, n), F32)
    tail = LANES - MLA_NOPE - MLA_ROPE
    C = jnp.concatenate([one(MLA_NOPE), cos, cos, one(tail)], axis=1)
    Sn = jnp.concatenate([zero(MLA_NOPE), -sin, zero(HALF_ROPE + tail)], axis=1)
    Sp = jnp.concatenate([zero(MLA_NOPE + HALF_ROPE), sin, zero(tail)], axis=1)
    return C, Sn, Sp


GATHER_FIRST = [("ffa_w_in", 0), ("ffa_w_out", 0)] + [(n, l) for l in range(DEPTH) for n in ("gdn_conv_w", "ln_g", "ln_b")]
GATHER_PLAN = {
    (0, "ffa_fwd"): [("mix_w_in", 0), ("mla_w_uq", 0), ("mla_w_ukv", 0), ("mix_w_o", 0)],
    (0, "in_proj"): [("ple_w_gate", 0), ("ple_w_proj", 0)],
    (0, "gdn_chunk_fwd"): [("ffb_w_in", 0)],
    (0, "sb_fwd"): [("ffb_w_out", 0), ("mix_w_o", 1)],
    (0, "mla_fwd"): [("ffa_w_out", 1)],
    (0, "ffb_fwd"): [("ffa_w_in", 1)],
    (1, "ffa_fwd"): [("mix_w_in", 1)],
    (1, "in_proj"): [("mla_w_uq", 1), ("mla_w_ukv", 1)],
    (1, "gdn_chunk_fwd"): [("ffb_w_in", 1)],
    (1, "sb_fwd"): [("ffb_w_out", 1), ("ple_w_gate", 1), ("ple_w_proj", 1)],
}
SCATTER_PLAN = {
    (1, "gdn_chunk_bwd"): [("ffb_w_in", 1)],
    (1, "gdn_pre_bwd"): [("ffb_w_out", 1), ("ple_w_gate", 1), ("ple_w_proj", 1), ("mix_w_o", 1)],
    (1, "ffa_bwd"): [("mix_w_in", 1), ("mla_w_uq", 1), ("mla_w_ukv", 1), ("gdn_conv_w", 1)],
    (0, "ffb_bwd"): [("ffa_w_in", 1)],
    (0, "gdn_chunk_bwd"): [("ffb_w_in", 0)],
    (0, "gdn_pre_bwd"): [("ffb_w_out", 0), ("ple_w_gate", 0), ("ple_w_proj", 0), ("mix_w_o", 0)],
    (0, "mla_bwd"): [("ffa_w_out", 1), ("ln_g", 1), ("ln_b", 1)],
    (0, "ffa_bwd"): [("mix_w_in", 0), ("mla_w_uq", 0), ("mla_w_ukv", 0), ("gdn_conv_w", 0)],
    (0, "d_ffa_in"): [("ffa_w_out", 0), ("ln_g", 0), ("ln_b", 0)],
}
SCATTER_LAST = [("ffa_w_in", 0)]


class Exchanges:
    def __init__(self, shards):
        self.shards = shards
        self.full = {}
        self.partial = {}
        self.received = {}

    def _block(self, key):
        n, l = key
        return self.shards[n][l].astype(SHARDED[n][1])

    def _absorb_gather(self, keys, results):
        for (n, l), g in zip(keys, results):
            blk = self.shards[n][l]
            self.full[(n, l)] = g if n in FFN_SLOT else _from_slots(g, (1,) + blk.shape, SHARDED[n][0])[0]

    def gather_now(self, keys, name):
        self._absorb_gather(keys, all_gather([self._block(k) for k in keys], name))

    def gather_with(self, layer, tag):
        keys = GATHER_PLAN.get((layer, tag))
        return None if keys is None else (keys, Hosted("gather", [self._block(k) for k in keys]))

    def scatter_with(self, layer, tag):
        keys = SCATTER_PLAN.get((layer, tag))
        return None if keys is None else (keys, Hosted("scatter", [self.partial[k] for k in keys]))

    def done(self, carried):
        if carried is not None:
            keys, hosted = carried
            if hosted.kind == "gather":
                self._absorb_gather(keys, hosted.results)
            else:
                self.received.update(zip(keys, hosted.results))

    def add_grad(self, key, g):
        n, l = key
        self.partial[key] = g if n in FFN_SLOT else _to_slots(g[None], SHARDED[n][0]).astype(SHARDED[n][1])


def _carried(c):
    return None if c is None else c[1]


def _layer_fwd(h0, p_i, rope, i, ex, rep):
    L = "L%d_" % i
    S = {"h0": h0, "p": p_i}
    W = ex.full
    ln_g = [W[("ln_g", i)][j][None, :] for j in range(3)]
    ln_b = [W[("ln_b", i)][j][None, :] for j in range(3)]
    S["ln_g"] = ln_g
    c = ex.gather_with(i, "ffa_fwd")
    S["h1"], S["xh1"], S["rs1"], S["Ga"], S["Ua"], S["h1b"] = ffn_fwd(
        h0, W[("ffa_w_in", i)], W[("ffa_w_out", i)], ln_g[0], ln_b[0], L + "ffa_fwd", hosted=_carried(c))
    ex.done(c)
    S["win"] = _pad_in_proj(W[("mix_w_in", i)])
    c = ex.gather_with(i, "in_proj")
    S["proj"] = mm_nn(S["h1b"], S["win"], L + "in_proj", hosted=_carried(c))
    ex.done(c)
    S["conv"] = W[("gdn_conv_w", i)]
    S["a_row"] = _pad_lanes(rep["gdn_a_log"][i][None, :], LANES)
    S["dt_row"] = _pad_lanes(rep["gdn_dt_bias"][i][None, :], LANES)
    S["nw"] = rep["gdn_norm_w"][i][None, :]
    S["wq"] = rep["mla_q_norm_w"][i][None, :]
    S["wkv"] = rep["mla_kv_norm_w"][i][None, :]
    S["qkv"] = gdn_pre_fwd(S["proj"], S["conv"], L + "gdn_pre_fwd")
    c = ex.gather_with(i, "gdn_chunk_fwd")
    S["o_gdn"], S["opre"], S["Tm"], S["Sin"] = gdn_chunk_fwd(
        S["qkv"], S["proj"], S["a_row"], S["dt_row"], S["nw"], L + "gdn_chunk_fwd", hosted=_carried(c))
    ex.done(c)
    c = ex.gather_with(i, "sb_fwd")
    S["o_sb"], S["tot"] = sb_fwd(S["proj"], L + "sb_fwd", hosted=_carried(c))
    ex.done(c)
    S["wuq"], S["wuk"], S["wuv"] = _mla_up_pad(W[("mla_w_uq", i)], W[("mla_w_ukv", i)])
    S["Q"], S["K"], S["V"] = mla_pre_fwd(S["proj"], S["wq"], S["wkv"], S["wuq"], S["wuk"], S["wuv"], *rope, L + "mla_pre_fwd")
    c = ex.gather_with(i, "mla_fwd")
    S["o_mla"], S["lse"] = mla_fwd(S["Q"], S["K"], S["V"], L + "mla_fwd", hosted=_carried(c))
    ex.done(c)
    wo = W[("mix_w_o", i)]
    wo_att = wo[GDN_W:].reshape(-1, GDN_D, wo.shape[1])
    S["wo"] = jnp.concatenate(
        [wo[:GDN_W], jnp.pad(wo_att, ((0, 0), (0, LANES - GDN_D), (0, 0))).reshape(-1, wo.shape[1])], axis=0)
    S["h2"], S["xh2"], S["rs2"], S["h2b"] = proj_res_ln([S["o_gdn"], S["o_sb"], S["o_mla"]], S["wo"], S["h1"],
                                                        ln_g[1], ln_b[1], L + "out_proj")
    c = ex.gather_with(i, "ffb_fwd")
    S["h3"], S["xh3"], S["rs3"], S["Gb"], S["Ub"], h3b = ffn_fwd(
        S["h2"], W[("ffb_w_in", i)], W[("ffb_w_out", i)], ln_g[2], ln_b[2], L + "ffb_fwd", hosted=_carried(c))
    ex.done(c)
    h4, S["a"], S["e"] = ple_fwd(S["h3"], h3b, p_i, W[("ple_w_gate", i)], W[("ple_w_proj", i)], L + "ple_fwd")
    return h4, S


def _layer_bwd(dh4, S, rope, i, ex):
    L = "L%d_" % i
    W = ex.full
    Grep = {}
    dh3, da, de = ple_bwd(dh4, S["a"], S["e"], W[("ple_w_gate", i)], L + "ple_bwd")
    ex.add_grad(("ple_w_gate", i), mm_tn(S["h3"], da, L + "d_ple_gate"))
    ex.add_grad(("ple_w_proj", i), mm_tn(S["p"], de, L + "d_ple_proj"))
    dz3, dg2, db2 = ln_bwd(dh3, S["xh3"], S["rs3"], S["ln_g"][2], L + "ln3_bwd")
    c = ex.scatter_with(i, "ffb_bwd")
    dh2, dGb, dUb, actb = ffn_bwd(dz3, S["Gb"], S["Ub"], W[("ffb_w_in", i)], W[("ffb_w_out", i)], L + "ffb_bwd",
                                  hosted=_carried(c))
    ex.done(c)
    ex.add_grad(("ffb_w_in", i), ffn_dw_in(S["h2b"].T, dGb, dUb, L + "d_ffb_in"))
    ex.add_grad(("ffb_w_out", i), ffn_dw_out(actb, dz3, L + "d_ffb_out"))
    dz2, dg1, db1 = ln_bwd(dh2, S["xh2"], S["rs2"], S["ln_g"][1], L + "ln2_bwd")
    docat = mm_nn(dz2, S["wo"], L + "d_ocat", b_transposed=True)
    dwo_att = jnp.concatenate([mm_tn(S["o_sb"], dz2, L + "d_wo_sb"), mm_tn(S["o_mla"], dz2, L + "d_wo_mla")], axis=0)
    dwo_att = dwo_att.reshape(-1, LANES, dwo_att.shape[1])[:, :GDN_D, :].reshape(-1, dwo_att.shape[1])
    ex.add_grad(("mix_w_o", i), jnp.concatenate([mm_tn(S["o_gdn"], dz2, L + "d_wo_gdn"), dwo_att], axis=0))
    c = ex.scatter_with(i, "gdn_chunk_bwd")
    dqkv, dgz, dgab, d_alog, d_dt, d_nw = gdn_chunk_bwd(S["qkv"], S["proj"], S["a_row"], S["dt_row"], S["nw"],
                                                        S["opre"], S["Tm"], S["Sin"], docat, L + "gdn_chunk_bwd",
                                                        hosted=_carried(c))
    ex.done(c)
    c = ex.scatter_with(i, "gdn_pre_bwd")
    dc, dconv = gdn_pre_bwd_pointwise(S["proj"], S["conv"], dqkv, L + "gdn_pre_bwd", hosted=_carried(c))
    ex.done(c)
    dxqkv = gdn_pre_bwd_conv(dc, S["conv"], L + "gdn_conv_bwd")
    ex.add_grad(("gdn_conv_w", i), dconv)
    Grep["gdn_a_log"], Grep["gdn_dt_bias"], Grep["gdn_norm_w"] = d_alog[0, :GDN_HEADS], d_dt[0, :GDN_HEADS], d_nw[0]
    dsq, dsk, dsv = sb_bwd(S["proj"], S["tot"], docat, DO_SB, L + "sb_bwd")
    c = ex.scatter_with(i, "mla_bwd")
    dQ, dK, dV = mla_bwd(S["Q"], S["K"], S["V"], S["o_mla"], S["lse"], docat, DO_MLA, L + "mla_bwd",
                         hosted=_carried(c))
    ex.done(c)
    dmq, dmkv, dwuq, dwuk, dwuv, dwq, dwkv = mla_pre_bwd(
        S["proj"], S["wq"], S["wkv"], S["wuq"], S["wuk"], S["wuv"], *rope, dQ, dK, dV, L + "mla_pre_bwd")
    g_uq, g_ukv = _mla_up_unpad(dwuq, dwuk, dwuv)
    ex.add_grad(("mla_w_uq", i), g_uq)
    ex.add_grad(("mla_w_ukv", i), g_ukv)
    Grep["mla_q_norm_w"], Grep["mla_kv_norm_w"] = dwq[0], dwkv[0]
    dproj = jnp.concatenate([dxqkv, dgz, dsq, dsk, dsv, dmq, dmkv, dgab], axis=1).astype(BF16)
    ex.add_grad(("mix_w_in", i),
                _unpad_in_proj(mm_tn(S["h1b"].T, dproj, L + "d_in_proj", a_transposed=True)))
    dh1 = mm_nn(dproj, S["win"], L + "d_h1", res=dz2, res_scale=ALPHA, b_transposed=True)
    dz1, dg0, db0 = ln_bwd(dh1, S["xh1"], S["rs1"], S["ln_g"][0], L + "ln1_bwd")
    c = ex.scatter_with(i, "ffa_bwd")
    dh0, dGa, dUa, acta = ffn_bwd(dz1, S["Ga"], S["Ua"], W[("ffa_w_in", i)], W[("ffa_w_out", i)], L + "ffa_bwd",
                                  hosted=_carried(c))
    ex.done(c)
    ex.add_grad(("ffa_w_out", i), ffn_dw_out(acta, dz1, L + "d_ffa_out"))
    ex.add_grad(("ln_g", i), jnp.concatenate([dg0, dg1, dg2], axis=0))
    ex.add_grad(("ln_b", i), jnp.concatenate([db0, db1, db2], axis=0))
    c = ex.scatter_with(i, "d_ffa_in")
    ex.add_grad(("ffa_w_in", i), ffn_dw_in(S["h0"].T.astype(BF16), dGa, dUa, L + "d_ffa_in", hosted=_carried(c)))
    ex.done(c)
    return dh0, Grep


def _local_step(x, p, positions, target, ex, rep):
    assert DEPTH == 2
    rope = _rope_tables(positions)
    h, saved = x, []
    for i in range(DEPTH):
        h, S = _layer_fwd(h, p[i], rope, i, ex, rep)
        saved.append(S)
    loss, dh = loss_head(h, target, "loss_head")
    grads = [None] * DEPTH
    for i in reversed(range(DEPTH)):
        dh, grads[i] = _layer_bwd(dh, saved[i], rope, i, ex)
    return loss, dh, {n: jnp.stack([grads[i][n] for i in range(DEPTH)]) for n in REPLICATED}


def kernel(x, p, positions, ffa_w_in, ffa_w_out, mix_w_in, gdn_conv_w, gdn_a_log, gdn_dt_bias, gdn_norm_w, mla_q_norm_w, mla_kv_norm_w, mla_w_uq, mla_w_ukv, mix_w_o, ffb_w_in, ffb_w_out, ln_g, ln_b, ple_w_gate, ple_w_proj, loss_target, m_ffa_w_in, m_ffa_w_out, m_mix_w_in, m_gdn_conv_w, m_gdn_a_log, m_gdn_dt_bias, m_gdn_norm_w, m_mla_q_norm_w, m_mla_kv_norm_w, m_mla_w_uq, m_mla_w_ukv, m_mix_w_o, m_ffb_w_in, m_ffb_w_out, m_ln_g, m_ln_b, m_ple_w_gate, m_ple_w_proj, v_ffa_w_in, v_ffa_w_out, v_mix_w_in, v_gdn_conv_w, v_gdn_a_log, v_gdn_dt_bias, v_gdn_norm_w, v_mla_q_norm_w, v_mla_kv_norm_w, v_mla_w_uq, v_mla_w_ukv, v_mix_w_o, v_ffb_w_in, v_ffb_w_out, v_ln_g, v_ln_b, v_ple_w_gate, v_ple_w_proj):
    given = dict(locals())
    shards = {n: given[n] for n in WEIGHTS}
    ex = Exchanges({n: shards[n] for n in SHARDED})
    ex.gather_now(GATHER_FIRST, "gather_first")
    loss, grad_x, Grep = _local_step(x[0], p[:, 0], positions[0], loss_target[0], ex, {n: shards[n] for n in REPLICATED})
    loss = lax.psum(loss[0, 0], ("x", "y", "c"))
    last = Hosted("scatter", [ex.partial[k] for k in SCATTER_LAST])
    ex.received.update(zip(SCATTER_LAST, exchange_now(last, "scatter_last")))
    rep_received = dict(zip(REPLICATED, all_gather([Grep[n] for n in REPLICATED], "gather_replicated_grads")))
    grad, delta, new_m, new_v = {}, {}, {}, {}
    for n in WEIGHTS:
        shape = shards[n].shape
        parts = [rep_received[n]] if n in REPLICATED else [ex.received[(n, l)] for l in range(DEPTH)]
        if parts[0].shape[1] % 8:
            parts = [jnp.concatenate(parts, axis=1)]
        outs = reduce_adamw(parts, _view2d(shards[n]), _view2d(given["m_" + n]), _view2d(given["v_" + n]),
                            "adamw_" + n)
        grad[n], delta[n], new_m[n], new_v[n] = (t.reshape(shape) for t in outs)
    return (loss, grad_x[None], *[grad[n] for n in WEIGHTS], *[delta[n] for n in WEIGHTS],
            *[new_m[n] for n in WEIGHTS], *[new_v[n] for n in WEIGHTS])
```

```python
import functools
import numpy as np
import jax
import jax.numpy as jnp
from jax import lax
from jax.experimental import pallas as pl
from jax.experimental.pallas import tpu as pltpu

F32 = jnp.float32
BF16 = jnp.bfloat16

DEPTH = 2
LN_EPS = 1e-5
RMS_EPS = 1e-6
ALPHA = (2 * DEPTH) ** 0.25
GDN_HEADS, GDN_D, GDN_CONV, GDN_CHUNK = 8, 64, 4, 64
SB_HEADS, SB_DIM = 4, 64
MLA_HEADS, MLA_NOPE, MLA_ROPE, MLA_V, MLA_Q_RANK, MLA_KV_RANK = 4, 64, 32, 64, 256, 128
ROPE_BASE = 10000.0
HALF_ROPE = MLA_ROPE // 2
LANES = 128
N_DEV = 8
ADAM_LR, ADAM_B1, ADAM_B2, ADAM_EPS, ADAM_WD, ADAM_STEP = 0.001, 0.9, 0.999, 1e-08, 0.01, 10

CB_GQ, CB_GK, CB_GV, CB_GZ = 0, 4, 8, 12
CB_SQ, CB_SK, CB_SV = 16, 20, 24
CB_MQ, CB_MKV, CB_GAB = 28, 30, 32
PROJ_W = 33 * LANES
GDN_W = GDN_HEADS * GDN_D
DO_SB = GDN_W // LANES
DO_MLA = DO_SB + SB_HEADS
VMEM_LIMIT = 56 * 1024 * 1024
MM_TILE = 1536

NT_DIMS = (((1,), (1,)), ((), ()))
TN_DIMS = (((0,), (0,)), ((), ()))


def _cp(sem):
    return pltpu.CompilerParams(dimension_semantics=sem, vmem_limit_bytes=VMEM_LIMIT)


def _bdot(a, b):
    return jnp.dot(a.astype(BF16), b.astype(BF16), preferred_element_type=F32)


def _bdot_nt(a, b):
    return lax.dot_general(a.astype(BF16), b.astype(BF16), NT_DIMS, preferred_element_type=F32)


def _bdot_tn(a, b):
    return lax.dot_general(a.astype(BF16), b.astype(BF16), TN_DIMS, preferred_element_type=F32)


def _split2(a):
    hi = a.astype(BF16)
    lo = (a - hi.astype(F32)).astype(BF16)
    return hi, lo


def _ones_dot_left(ones_bf16, x):
    hi = x.astype(BF16)
    r1 = x - hi.astype(F32)
    mid = r1.astype(BF16)
    lo = (r1 - mid.astype(F32)).astype(BF16)
    d = functools.partial(jnp.dot, preferred_element_type=F32)
    return d(ones_bf16, hi) + d(ones_bf16, mid) + d(ones_bf16, lo)


def _iota2(shape, dim):
    return lax.broadcasted_iota(jnp.int32, shape, dim)


def _sigmoid(x):
    return 0.5 * jnp.tanh(0.5 * x) + 0.5


def _softplus(x):
    return jnp.maximum(x, 0.0) + jnp.log(1.0 + jnp.exp(-jnp.abs(x)))


def _pick(n, limit, mult):
    if n <= limit:
        return n
    best = None
    for t in range(mult, limit + 1, mult):
        if n % t == 0:
            best = t
    assert best is not None, (n, limit, mult)
    return best


MESH = pl.DeviceIdType.MESH
ANY = pl.BlockSpec(memory_space=pl.ANY)


def _place():
    return lax.axis_index("x"), lax.axis_index("y"), lax.axis_index("c")


def _peer(k):
    x, y, c = _place()
    return (1 - x if k & 4 else x, 1 - y if k & 2 else y, 1 - c if k & 1 else c)


class Hosted:
    def __init__(self, kind, arrays):
        self.kind, self.arrays, self.n, self.results = kind, list(arrays), len(arrays), None

    def out_shapes(self):
        if self.kind == "gather":
            return [jax.ShapeDtypeStruct((N_DEV,) + a.shape, a.dtype) for a in self.arrays]
        return [jax.ShapeDtypeStruct(a.shape, a.dtype) for a in self.arrays]

    def sems(self):
        return [pltpu.SemaphoreType.DMA((self.n, N_DEV - 1)), pltpu.SemaphoreType.DMA((self.n, N_DEV - 1)),
                pltpu.SemaphoreType.DMA((self.n,))]

    def _copies(self, src_refs, dst_refs, send_sems, recv_sems, local_sems):
        x, y, c = _place()
        me = 4 * x + 2 * y + c
        local, remote = [], []
        for a in range(self.n):
            gather = self.kind == "gather"
            local.append(pltpu.make_async_copy(src_refs[a] if gather else src_refs[a].at[me], dst_refs[a].at[me],
                                               local_sems.at[a]))
            for k in range(1, N_DEV):
                px, py, pc = _peer(k)
                remote.append(pltpu.make_async_remote_copy(
                    src_ref=src_refs[a] if gather else src_refs[a].at[4 * px + 2 * py + pc], dst_ref=dst_refs[a].at[me],
                    send_sem=send_sems.at[a, k - 1], recv_sem=recv_sems.at[a, k - 1],
                    device_id=(px, py, pc), device_id_type=MESH))
        return local, remote

    def start(self, *refs):
        local, remote = self._copies(*refs)
        for cp in local + remote:
            cp.start()

    def wait(self, *refs):
        local, remote = self._copies(*refs)
        for cp in remote:
            cp.wait_recv()
        for cp in remote:
            cp.wait_send()
        for cp in local:
            cp.wait()


def _hosted_call(hosted, body, *, grid, in_specs, out_specs, out_shape, scratch_shapes=(), compiler_params, name):
    if hosted is None:
        return pl.pallas_call(body, grid=grid, in_specs=in_specs, out_specs=out_specs, out_shape=out_shape,
                              scratch_shapes=scratch_shapes, compiler_params=compiler_params, name=name)
    single = not isinstance(out_shape, (list, tuple))
    o_specs = [out_specs] if single else list(out_specs)
    o_shape = [out_shape] if single else list(out_shape)
    n_in, n_out, n_scr, n = len(in_specs), len(o_specs), len(scratch_shapes), hosted.n

    def wrapped(*refs):
        ins, c_in = refs[:n_in], refs[n_in:n_in + n]
        outs, c_out = refs[n_in + n:n_in + n + n_out], refs[n_in + n + n_out:n_in + 2 * n + n_out]
        rest = refs[n_in + 2 * n + n_out:]
        scr, sems = rest[:n_scr], rest[n_scr:]
        ids = [pl.program_id(ax) for ax in range(len(grid))]
        first = functools.reduce(jnp.logical_and, [i == 0 for i in ids])
        last = functools.reduce(jnp.logical_and, [i == g - 1 for i, g in zip(ids, grid)])

        @pl.when(first)
        def _():
            hosted.start(c_in, c_out, *sems)

        body(*ins, *outs, *scr)

        @pl.when(last)
        def _():
            hosted.wait(c_in, c_out, *sems)

    call = pl.pallas_call(
        wrapped, grid=grid, in_specs=list(in_specs) + [ANY] * n, out_specs=o_specs + [ANY] * n,
        out_shape=o_shape + hosted.out_shapes(), scratch_shapes=list(scratch_shapes) + hosted.sems(),
        compiler_params=_cp(("arbitrary",) * len(grid)), name=name)

    def run(*args):
        outs = call(*args, *hosted.arrays)
        hosted.results = list(outs[n_out:])
        return outs[0] if single else list(outs[:n_out])

    return run


def exchange_now(hosted, name):
    n = hosted.n

    def body(*refs):
        src, dst, sems = refs[:n], refs[n:2 * n], refs[2 * n:]
        hosted.start(src, dst, *sems)
        hosted.wait(src, dst, *sems)

    return pl.pallas_call(body, out_shape=hosted.out_shapes(), in_specs=[ANY] * n, out_specs=[ANY] * n,
                          scratch_shapes=hosted.sems(), name=name)(*hosted.arrays)


def mm_nn(a, b, name, out_dtype=F32, res=None, res_scale=1.0, b_transposed=False, hosted=None):
    M, K = a.shape
    N = b.shape[0] if b_transposed else b.shape[1]
    tm, tn, tk = _pick(M, 512, 16), _pick(N, MM_TILE, LANES), _pick(K, MM_TILE, LANES)
    nk = K // tk
    has_res = res is not None
    dot = _bdot_nt if b_transposed else _bdot

    def body(*refs):
        if has_res:
            a_ref, b_ref, r_ref, o_ref, acc_ref = refs
        else:
            a_ref, b_ref, o_ref, acc_ref = refs
        k = pl.program_id(2)

        @pl.when(k == 0)
        def _():
            acc_ref[...] = jnp.zeros_like(acc_ref)

        acc_ref[...] += dot(a_ref[...], b_ref[...])

        @pl.when(k == nk - 1)
        def _():
            out = acc_ref[...]
            if has_res:
                out = out + res_scale * r_ref[...]
            o_ref[...] = out.astype(o_ref.dtype)

    b_spec = pl.BlockSpec((tn, tk), lambda i, j, k: (j, k)) if b_transposed else pl.BlockSpec((tk, tn), lambda i, j, k: (k, j))
    in_specs = [pl.BlockSpec((tm, tk), lambda i, j, k: (i, k)), b_spec]
    args = [a, b]
    if has_res:
        in_specs.append(pl.BlockSpec((tm, tn), lambda i, j, k: (i, j)))
        args.append(res)
    return _hosted_call(
        hosted, body, grid=(M // tm, N // tn, nk), in_specs=in_specs,
        out_specs=pl.BlockSpec((tm, tn), lambda i, j, k: (i, j)),
        out_shape=jax.ShapeDtypeStruct((M, N), out_dtype),
        scratch_shapes=[pltpu.VMEM((tm, tn), F32)],
        compiler_params=_cp(("parallel", "parallel", "arbitrary")), name=name)(*args)


def mm_tn(a, b, name, out_dtype=F32, a_transposed=False):
    K, T = a.shape if a_transposed else a.shape[::-1]
    _, N = b.shape
    tk = K if a_transposed else _pick(K, 512, LANES)
    tn, tt = _pick(N, MM_TILE, LANES), _pick(T, 512, LANES)
    nt = T // tt

    def body(a_ref, b_ref, o_ref, acc_ref):
        t = pl.program_id(2)

        @pl.when(t == 0)
        def _():
            acc_ref[...] = jnp.zeros_like(acc_ref)

        if a_transposed:
            acc_ref[...] += _bdot(a_ref[:, pl.ds(pl.multiple_of(t * tt, tt), tt)], b_ref[...])
        else:
            acc_ref[...] += _bdot_tn(a_ref[...], b_ref[...])

        @pl.when(t == nt - 1)
        def _():
            o_ref[...] = acc_ref[...].astype(o_ref.dtype)

    a_spec = pl.BlockSpec((K, T), lambda i, j, t: (0, 0)) if a_transposed else pl.BlockSpec((tt, tk), lambda i, j, t: (t, i))
    return pl.pallas_call(
        body, grid=(K // tk, N // tn, nt),
        in_specs=[a_spec, pl.BlockSpec((tt, tn), lambda i, j, t: (t, j))],
        out_specs=pl.BlockSpec((tk, tn), lambda i, j, t: (i, j)),
        out_shape=jax.ShapeDtypeStruct((K, N), out_dtype),
        scratch_shapes=[pltpu.VMEM((tk, tn), F32)],
        compiler_params=_cp(("parallel", "parallel", "arbitrary")), name=name)(a, b)


def _ln_apply(z, g, b):
    mu = jnp.mean(z, axis=-1, keepdims=True)
    zc = z - mu
    var = jnp.mean(zc * zc, axis=-1, keepdims=True)
    rstd = lax.rsqrt(var + LN_EPS)
    xhat = zc * rstd
    return xhat * g + b, xhat, rstd


def ln_bwd(dout, xhat, rstd, g, name):
    T, D = dout.shape
    tm = _pick(T, 512, 8)

    def body(do_ref, xh_ref, rs_ref, g_ref, dz_ref, dg_ref, db_ref):
        i = pl.program_id(0)

        @pl.when(i == 0)
        def _():
            dg_ref[...] = jnp.zeros_like(dg_ref)
            db_ref[...] = jnp.zeros_like(db_ref)

        do = do_ref[...]
        xh = xh_ref[...]
        dxh = do * g_ref[...]
        m1 = jnp.mean(dxh, axis=-1, keepdims=True)
        m2 = jnp.mean(dxh * xh, axis=-1, keepdims=True)
        dz_ref[...] = rs_ref[...] * (dxh - m1 - xh * m2)
        dg_ref[...] += jnp.sum(do * xh, axis=0, keepdims=True)
        db_ref[...] += jnp.sum(do, axis=0, keepdims=True)

    row = pl.BlockSpec((tm, D), lambda i: (i, 0))
    vec = pl.BlockSpec((1, D), lambda i: (0, 0))
    return pl.pallas_call(
        body, grid=(T // tm,),
        in_specs=[row, row, pl.BlockSpec((tm, 1), lambda i: (i, 0)), vec],
        out_specs=[row, vec, vec],
        out_shape=[jax.ShapeDtypeStruct((T, D), F32), jax.ShapeDtypeStruct((1, D), F32), jax.ShapeDtypeStruct((1, D), F32)],
        compiler_params=_cp(("arbitrary",)), name=name)(dout, xhat, rstd, g)


FFN_CHUNKS = N_DEV // 2


def ffn_fwd(h, w_in, w_out, g, b, name, hosted=None):
    T, D = h.shape
    fc = w_in.shape[2]
    half = w_out.shape[1]
    tm = _pick(T, 512, 8)
    nc = FFN_CHUNKS

    def body(h_ref, wg_ref, wu_ref, wo_ref, g_ref, b_ref, out_ref, xh_ref, rs_ref, G_ref, U_ref, ob_ref, obt_ref, acc_ref):
        c = pl.program_id(1)

        @pl.when(c == 0)
        def _():
            acc_ref[...] = jnp.zeros_like(acc_ref)

        hb = h_ref[...].astype(BF16)
        G = jnp.dot(hb, wg_ref[0], preferred_element_type=F32)
        U = jnp.dot(hb, wu_ref[0], preferred_element_type=F32)
        G_ref[0] = G
        U_ref[0] = U
        act = G * _sigmoid(G) * U
        acc_ref[...] += _bdot(act, wo_ref[...].reshape(2 * half, D))

        @pl.when(c == nc - 1)
        def _():
            z = ALPHA * h_ref[...] + 0.5 * acc_ref[...]
            out, xh, rs = _ln_apply(z, g_ref[...], b_ref[...])
            out_ref[...] = out
            ob_ref[...] = out.astype(BF16)
            obt_ref[...] = out.T.astype(BF16)
            xh_ref[...] = xh
            rs_ref[...] = rs

    row = pl.BlockSpec((tm, D), lambda i, c: (i, 0))
    vec = pl.BlockSpec((1, D), lambda i, c: (0, 0))
    cblk = pl.BlockSpec((1, tm, fc), lambda i, c: (c, i, 0))
    csds = jax.ShapeDtypeStruct((nc, T, fc), F32)
    return _hosted_call(
        hosted, body, grid=(T // tm, nc),
        in_specs=[row, pl.BlockSpec((1, D, fc), lambda i, c: (c, 0, 0)),
                  pl.BlockSpec((1, D, fc), lambda i, c: (c + nc, 0, 0)),
                  pl.BlockSpec((2, half, D), lambda i, c: (c, 0, 0)), vec, vec],
        out_specs=[row, row, pl.BlockSpec((tm, 1), lambda i, c: (i, 0)), cblk, cblk, row,
                   pl.BlockSpec((D, tm), lambda i, c: (0, i))],
        out_shape=[jax.ShapeDtypeStruct((T, D), F32), jax.ShapeDtypeStruct((T, D), F32), jax.ShapeDtypeStruct((T, 1), F32),
                   csds, csds, jax.ShapeDtypeStruct((T, D), BF16), jax.ShapeDtypeStruct((D, T), BF16)],
        scratch_shapes=[pltpu.VMEM((tm, D), F32)],
        compiler_params=_cp(("parallel", "arbitrary")), name=name)(h, w_in, w_in, w_out, g, b)


def ffn_bwd(dz, G, U, w_in, w_out, name, hosted=None):
    T, D = dz.shape
    nc, _, fc = G.shape
    half = w_out.shape[1]
    tm = _pick(T, 512, 16)

    def body(dz_ref, G_ref, U_ref, wg_ref, wu_ref, wo_ref, dh_ref, dG_ref, dU_ref, act_ref, acc_ref):
        c = pl.program_id(1)

        @pl.when(c == 0)
        def _():
            acc_ref[...] = jnp.zeros_like(acc_ref)

        dy = (0.5 * dz_ref[...]).astype(BF16)
        dact = _bdot_nt(dy, wo_ref[...].reshape(2 * half, D))
        G = G_ref[0]
        U = U_ref[0]
        s = _sigmoid(G)
        silu = G * s
        dG = (dact * U * (s * (1.0 + G * (1.0 - s)))).astype(BF16)
        dU = (dact * silu).astype(BF16)
        dG_ref[0] = dG
        dU_ref[0] = dU
        act_ref[0] = (silu * U).astype(BF16)
        acc_ref[...] += _bdot_nt(dG, wg_ref[0]) + _bdot_nt(dU, wu_ref[0])

        @pl.when(c == nc - 1)
        def _():
            dh_ref[...] = ALPHA * dz_ref[...] + acc_ref[...]

    row = pl.BlockSpec((tm, D), lambda i, c: (i, 0))
    cblk = pl.BlockSpec((1, tm, fc), lambda i, c: (c, i, 0))
    csds = jax.ShapeDtypeStruct((nc, T, fc), BF16)
    return _hosted_call(
        hosted, body, grid=(T // tm, nc),
        in_specs=[row, cblk, cblk, pl.BlockSpec((1, D, fc), lambda i, c: (c, 0, 0)),
                  pl.BlockSpec((1, D, fc), lambda i, c: (c + nc, 0, 0)),
                  pl.BlockSpec((2, half, D), lambda i, c: (c, 0, 0))],
        out_specs=[row, cblk, cblk, cblk],
        out_shape=[jax.ShapeDtypeStruct((T, D), F32), csds, csds, csds],
        scratch_shapes=[pltpu.VMEM((tm, D), F32)],
        compiler_params=_cp(("parallel", "arbitrary")), name=name)(dz, G, U, w_in, w_in, w_out)


def ffn_dw_in(h_t, dG, dU, name, hosted=None):
    D, T = h_t.shape
    nc, _, fc = dG.shape
    tt = _pick(T, 512, LANES)
    nt = T // tt

    def body(h_ref, dG_ref, dU_ref, o_ref, acc_ref):
        s = pl.program_id(0)
        t = pl.program_id(1)

        @pl.when(t == 0)
        def _():
            acc_ref[...] = jnp.zeros_like(acc_ref)

        hb = h_ref[:, pl.ds(pl.multiple_of(t * tt, tt), tt)]

        @pl.when(s < nc)
        def _():
            acc_ref[...] += jnp.dot(hb, dG_ref[0], preferred_element_type=F32)

        @pl.when(s >= nc)
        def _():
            acc_ref[...] += jnp.dot(hb, dU_ref[0], preferred_element_type=F32)

        @pl.when(t == nt - 1)
        def _():
            o_ref[0] = acc_ref[...].astype(o_ref.dtype)

    return _hosted_call(
        hosted, body, grid=(2 * nc, nt),
        in_specs=[pl.BlockSpec((D, T), lambda s, t: (0, 0)),
                  pl.BlockSpec((1, tt, fc), lambda s, t: (jnp.minimum(s, nc - 1), jnp.where(s < nc, t, nt - 1), 0)),
                  pl.BlockSpec((1, tt, fc), lambda s, t: (jnp.maximum(s - nc, 0), jnp.where(s >= nc, t, 0), 0))],
        out_specs=pl.BlockSpec((1, D, fc), lambda s, t: (s, 0, 0)),
        out_shape=jax.ShapeDtypeStruct((2 * nc, D, fc), BF16),
        scratch_shapes=[pltpu.VMEM((D, fc), F32)],
        compiler_params=_cp(("parallel", "arbitrary")), name=name)(h_t, dG, dU)


def ffn_dw_out(act, dz, name, hosted=None):
    nc, T, fc = act.shape
    D = dz.shape[1]
    half = fc // 2
    tt = _pick(T, 512, 16)
    nt = T // tt

    def body(a_ref, dz_ref, o_ref, acc_ref):
        t = pl.program_id(1)

        @pl.when(t == 0)
        def _():
            acc_ref[...] = jnp.zeros_like(acc_ref)

        acc_ref[...] += _bdot_tn(a_ref[0], dz_ref[...])

        @pl.when(t == nt - 1)
        def _():
            o_ref[...] = (0.5 * acc_ref[...]).reshape(2, half, D).astype(o_ref.dtype)

    return _hosted_call(
        hosted, body, grid=(nc, nt),
        in_specs=[pl.BlockSpec((1, tt, fc), lambda c, t: (c, t, 0)), pl.BlockSpec((tt, D), lambda c, t: (t, 0))],
        out_specs=pl.BlockSpec((2, half, D), lambda c, t: (c, 0, 0)),
        out_shape=jax.ShapeDtypeStruct((2 * nc, half, D), BF16),
        scratch_shapes=[pltpu.VMEM((fc, D), F32)],
        compiler_params=_cp(("parallel", "arbitrary")), name=name)(act, dz)


def proj_res_ln(parts, w, res, g, b, name):
    T, D = res.shape
    tm = _pick(T, 512, 8)
    widths = [p.shape[1] for p in parts]
    offs = [int(sum(widths[:i])) for i in range(len(parts))]
    n = len(parts)

    def body(*refs):
        p_refs = refs[:n]
        w_ref, r_ref, g_ref, b_ref, out_ref, xh_ref, rs_ref, ob_ref, obt_ref = refs[n:]
        acc = ALPHA * r_ref[...]
        for p_ref, o, wd in zip(p_refs, offs, widths):
            acc = acc + _bdot(p_ref[...], w_ref[o:o + wd, :])
        out, xh, rs = _ln_apply(acc, g_ref[...], b_ref[...])
        out_ref[...] = out
        ob_ref[...] = out.astype(BF16)
        obt_ref[...] = out.T.astype(BF16)
        xh_ref[...] = xh
        rs_ref[...] = rs

    row = pl.BlockSpec((tm, D), lambda i: (i, 0))
    vec = pl.BlockSpec((1, D), lambda i: (0, 0))
    return pl.pallas_call(
        body, grid=(T // tm,),
        in_specs=[pl.BlockSpec((tm, wd), lambda i: (i, 0)) for wd in widths]
        + [pl.BlockSpec(w.shape, lambda i: (0, 0)), row, vec, vec],
        out_specs=[row, row, pl.BlockSpec((tm, 1), lambda i: (i, 0)), row, pl.BlockSpec((D, tm), lambda i: (0, i))],
        out_shape=[jax.ShapeDtypeStruct((T, D), F32), jax.ShapeDtypeStruct((T, D), F32), jax.ShapeDtypeStruct((T, 1), F32),
                   jax.ShapeDtypeStruct((T, D), BF16), jax.ShapeDtypeStruct((D, T), BF16)],
        compiler_params=_cp(("parallel",)), name=name)(*parts, w, res, g, b)


def ple_fwd(h, h_b, p, wg, wp, name):
    T, D = h.shape
    P = p.shape[1]
    tm, tn = _pick(T, 512, 16), _pick(D, MM_TILE, LANES)

    def body(h_ref, hn_ref, p_ref, wg_ref, wp_ref, out_ref, a_ref, e_ref):
        a = _bdot(h_ref[...], wg_ref[...])
        e = _bdot(p_ref[...], wp_ref[...])
        a_ref[...] = a
        e_ref[...] = e
        out_ref[...] = hn_ref[...] + _sigmoid(a) * e

    blk = pl.BlockSpec((tm, tn), lambda i, j: (i, j))
    sds = jax.ShapeDtypeStruct((T, D), F32)
    return pl.pallas_call(
        body, grid=(T // tm, D // tn),
        in_specs=[pl.BlockSpec((tm, D), lambda i, j: (i, 0)), blk, pl.BlockSpec((tm, P), lambda i, j: (i, 0)),
                  pl.BlockSpec((D, tn), lambda i, j: (0, j)), pl.BlockSpec((P, tn), lambda i, j: (0, j))],
        out_specs=[blk, blk, blk], out_shape=[sds, sds, sds],
        compiler_params=_cp(("parallel", "parallel")), name=name)(h_b, h, p, wg, wp)


def ple_bwd(dout, a, e, wg, name):
    T, D = dout.shape
    tm = _pick(T, 512, 16)

    def body(do_ref, a_ref, e_ref, wg_ref, dh_ref, da_ref, de_ref):
        do = do_ref[...]
        s = _sigmoid(a_ref[...])
        da = (do * e_ref[...] * s * (1.0 - s)).astype(BF16)
        da_ref[...] = da
        de_ref[...] = (do * s).astype(BF16)
        dh_ref[...] = do + _bdot_nt(da, wg_ref[...])

    row = pl.BlockSpec((tm, D), lambda i: (i, 0))
    return pl.pallas_call(
        body, grid=(T // tm,),
        in_specs=[row, row, row, pl.BlockSpec((D, D), lambda i: (0, 0))],
        out_specs=[row, row, row],
        out_shape=[jax.ShapeDtypeStruct((T, D), F32), jax.ShapeDtypeStruct((T, D), BF16), jax.ShapeDtypeStruct((T, D), BF16)],
        compiler_params=_cp(("parallel",)), name=name)(dout, a, e, wg)


def loss_head(y, target, name):
    T, D = y.shape
    tm = _pick(T, 512, 8)

    def body(y_ref, t_ref, loss_ref, dy_ref):
        i = pl.program_id(0)

        @pl.when(i == 0)
        def _():
            loss_ref[...] = jnp.zeros_like(loss_ref)

        err = y_ref[...] - t_ref[...]
        dy_ref[...] = err * (1.0 / D)
        per_tok = jnp.sum(err * err, axis=-1, keepdims=True) * (1.0 / D)
        loss_ref[...] += 0.5 * jnp.sum(per_tok, axis=0, keepdims=True)

    row = pl.BlockSpec((tm, D), lambda i: (i, 0))
    return pl.pallas_call(
        body, grid=(T // tm,), in_specs=[row, row],
        out_specs=[pl.BlockSpec((1, 1), lambda i: (0, 0)), row],
        out_shape=[jax.ShapeDtypeStruct((1, 1), F32), jax.ShapeDtypeStruct((T, D), F32)],
        compiler_params=_cp(("arbitrary",)), name=name)(y, target)


HALO = 8


def _conv_taps(pad_ref, w_ref, tm, base):
    acc = w_ref[0:1, :] * pad_ref[pl.ds(base, tm), :]
    for k in range(1, GDN_CONV):
        acc = acc + w_ref[k:k + 1, :] * pad_ref[pl.ds(base + k, tm), :]
    return acc


GDN_GROUP_W = GDN_W
GDN_PRE_ROWS = 512


def _head_segments():
    head = jnp.arange(GDN_W, dtype=jnp.int32) // GDN_D
    return (head[:, None] == head[None, :]).astype(BF16)


def _head_sums(x, seg):
    hi = x.astype(BF16)
    r1 = x - hi.astype(F32)
    mid = r1.astype(BF16)
    lo = (r1 - mid.astype(F32)).astype(BF16)
    d = functools.partial(jnp.dot, preferred_element_type=F32)
    return d(hi, seg) + d(mid, seg) + d(lo, seg)


def _gdn_pre_common(x_ref, halo_ref, w_ref, seg_ref, pad_ref, tm):
    i = pl.program_id(1)
    grp = pl.program_id(0)
    pad_ref[0:HALO, :] = jnp.where(i == 0, 0.0, halo_ref[...])
    pad_ref[HALO:HALO + tm, :] = x_ref[...]
    c = _conv_taps(pad_ref, w_ref, tm, HALO - (GDN_CONV - 1))
    s = _sigmoid(c)
    y = c * s
    r = lax.rsqrt(_head_sums(y * y, seg_ref[...]) + RMS_EPS)
    scale = jnp.where(grp < 1, GDN_D ** -0.5, 1.0)
    return grp < 2, c, s, y, r, scale


def gdn_pre_fwd(proj, conv_w, name):
    T = proj.shape[0]
    tm = _pick(T, GDN_PRE_ROWS, 8)
    GW = GDN_GROUP_W

    def body(x_ref, halo_ref, w_ref, seg_ref, o_ref, pad_ref):
        normed, c, s, y, r, scale = _gdn_pre_common(x_ref, halo_ref, w_ref, seg_ref, pad_ref, tm)
        o_ref[...] = jnp.where(normed, y * r * scale, y)

    return pl.pallas_call(
        body, grid=(3, T // tm),
        in_specs=[pl.BlockSpec((tm, GW), lambda hb, i: (i, hb)),
                  pl.BlockSpec((HALO, GW), lambda hb, i: (jnp.maximum(i * (tm // HALO) - 1, 0), hb)),
                  pl.BlockSpec((GDN_CONV, GW), lambda hb, i: (0, hb)), pl.BlockSpec((GW, GW), lambda hb, i: (0, 0))],
        out_specs=pl.BlockSpec((tm, GW), lambda hb, i: (i, hb)),
        out_shape=jax.ShapeDtypeStruct((T, 3 * GW), F32),
        scratch_shapes=[pltpu.VMEM((tm + HALO, GW), F32)],
        compiler_params=_cp(("parallel", "parallel")), name=name)(proj, proj, conv_w, _head_segments())


def gdn_pre_bwd_pointwise(proj, conv_w, dqkv, name, hosted=None):
    T = proj.shape[0]
    tm = _pick(T, GDN_PRE_ROWS, 8)
    GW = GDN_GROUP_W

    def body(x_ref, halo_ref, w_ref, seg_ref, d_ref, dc_ref, dw_ref, pad_ref):
        i = pl.program_id(1)
        normed, c, s, y, r, scale = _gdn_pre_common(x_ref, halo_ref, w_ref, seg_ref, pad_ref, tm)

        @pl.when(i == 0)
        def _():
            dw_ref[...] = jnp.zeros_like(dw_ref)

        d = d_ref[...]
        n = y * r
        dn = d * scale
        dy = jnp.where(normed, r * (dn - n * _head_sums(dn * n, seg_ref[...])), d)
        dc = dy * (s * (1.0 + c * (1.0 - s)))
        dc_ref[...] = dc
        for k in range(GDN_CONV):
            xs = pad_ref[pl.ds(HALO - (GDN_CONV - 1) + k, tm), :]
            dw_ref[k:k + 1, :] += jnp.sum(dc * xs, axis=0, keepdims=True)

    blk = pl.BlockSpec((tm, GW), lambda hb, i: (i, hb))
    wblk = pl.BlockSpec((GDN_CONV, GW), lambda hb, i: (0, hb))
    return _hosted_call(
        hosted, body, grid=(3, T // tm),
        in_specs=[blk, pl.BlockSpec((HALO, GW), lambda hb, i: (jnp.maximum(i * (tm // HALO) - 1, 0), hb)), wblk,
                  pl.BlockSpec((GW, GW), lambda hb, i: (0, 0)), blk],
        out_specs=[blk, wblk],
        out_shape=[jax.ShapeDtypeStruct((T, 3 * GW), F32), jax.ShapeDtypeStruct((GDN_CONV, 3 * GW), F32)],
        scratch_shapes=[pltpu.VMEM((tm + HALO, GW), F32)],
        compiler_params=_cp(("parallel", "arbitrary")), name=name)(proj, proj, conv_w, _head_segments(), dqkv)


def gdn_pre_bwd_conv(dc, conv_w_p, name):
    T = dc.shape[0]
    tm = _pick(T, GDN_PRE_ROWS, 8)
    nt = T // tm
    GW = GDN_GROUP_W

    def body(dc_ref, halo_ref, w_ref, dx_ref, pad_ref):
        i = pl.program_id(1)
        pad_ref[0:tm, :] = dc_ref[...]
        pad_ref[tm:tm + HALO, :] = jnp.where(i == nt - 1, 0.0, halo_ref[...])
        acc = w_ref[GDN_CONV - 1:GDN_CONV, :] * pad_ref[pl.ds(0, tm), :]
        for k in range(GDN_CONV - 1):
            acc = acc + w_ref[k:k + 1, :] * pad_ref[pl.ds(GDN_CONV - 1 - k, tm), :]
        dx_ref[...] = acc

    blk = pl.BlockSpec((tm, GW), lambda hb, i: (i, hb))
    return pl.pallas_call(
        body, grid=(3, nt),
        in_specs=[blk, pl.BlockSpec((HALO, GW), lambda hb, i: (jnp.minimum((i + 1) * (tm // HALO), T // HALO - 1), hb)),
                  pl.BlockSpec((GDN_CONV, GW), lambda hb, i: (0, hb))],
        out_specs=blk,
        out_shape=jax.ShapeDtypeStruct((T, 3 * GW), F32),
        scratch_shapes=[pltpu.VMEM((tm + HALO, GW), F32)],
        compiler_params=_cp(("parallel", "parallel")), name=name)(dc, dc, conv_w_p)


def _chunk_masks(C):
    row = _iota2((C, C), 0)
    col = _iota2((C, C), 1)
    return row >= col, row > col, row == col


GDN_FWD_CHUNKS = 4
BNN = (((2,), (1,)), ((0,), (0,)))
BNT = (((2,), (2,)), ((0,), (0,)))
BTN = (((1,), (1,)), ((0,), (0,)))


def _bmm(a, b, dims=BNN):
    return lax.dot_general(a.astype(BF16), b.astype(BF16), dims, preferred_element_type=F32)


def _hbmm(a, b):
    m = a.shape[1]
    a_hi, a_lo = _split2(a)
    b_hi, b_lo = _split2(b)
    r = lax.dot_general(jnp.concatenate([a_hi, a_lo], axis=1), b_hi, BNN, preferred_element_type=F32)
    return r[:, :m] + r[:, m:] + lax.dot_general(a_hi, b_lo, BNN, preferred_element_type=F32)


def _hbmm_tn(a, b):
    a_hi, a_lo = _split2(a)
    b_hi, b_lo = _split2(b)
    d = functools.partial(lax.dot_general, dimension_numbers=BTN, preferred_element_type=F32)
    return d(a_hi, b_hi) + d(a_lo, b_hi) + d(a_hi, b_lo)


def _col_to_row(colv, eye):
    return jnp.sum(jnp.where(eye, colv, 0.0), axis=1, keepdims=True)


def _row_to_col(rowv, eye):
    return jnp.sum(jnp.where(eye, rowv, 0.0), axis=2, keepdims=True)


def _unit_lower_inverse(A, eye):
    C = A.shape[1]
    P = jnp.where(eye, 1.0, 0.0) - A
    Bp = _hbmm(A, A)
    for _ in range(4):
        R = _hbmm(jnp.concatenate([Bp, P], axis=1), Bp)
        Bp = R[:, :C]
        P = P + R[:, C:]
    return P + _hbmm(P, Bp)


def _stack_heads(ref, first_head, n, row0=0):
    rows = pl.ds(row0, GDN_CHUNK)
    return jnp.stack([ref[rows, pl.ds((first_head + h) * GDN_D, GDN_D)] for h in range(n)])


def _unstack_heads(ref, first_head, val, row0=0):
    rows = pl.ds(row0, GDN_CHUNK)
    for h in range(val.shape[0]):
        ref[rows, pl.ds((first_head + h) * GDN_D, GDN_D)] = val[h]


def _gdn_gates(gab, a_row, dt_row, incl):
    g_all = -jnp.exp(a_row) * _softplus(gab + dt_row)
    beta_all = _sigmoid(gab)
    gc_all = _ones_dot_left(incl.astype(BF16), g_all)
    return g_all, beta_all, gc_all


def _gdn_common(qkv_ref, gc_all, beta_all, incl, strict, eye, row0=0):
    C, H = GDN_CHUNK, GDN_HEADS
    q, k, v = (_stack_heads(qkv_ref, j * H, H, row0) for j in range(3))
    gc = jnp.stack([gc_all[:, h:h + 1] for h in range(H)])
    beta = jnp.stack([beta_all[:, H + h:H + h + 1] for h in range(H)])
    gc_row = _col_to_row(gc, eye)
    decay = jnp.where(incl, jnp.exp(jnp.where(incl, gc - gc_row, 0.0)), 0.0)
    e_gc = jnp.exp(gc)
    gl = gc[:, C - 1:C, :]
    e_gl = jnp.exp(gl)
    ekd = jnp.exp(gl - gc)
    kb = k * beta
    A = jnp.where(strict, _bmm(kb, k, BNT) * decay, 0.0)
    Pm = jnp.where(incl, _bmm(q, k, BNT) * decay, 0.0)
    return q, k, v, gc, beta, decay, e_gc, e_gl, ekd, kb, A, Pm


def gdn_chunk_fwd(qkv, proj, a_row, dt_row, norm_w, name, hosted=None):
    T = qkv.shape[0]
    C, H, Dh = GDN_CHUNK, GDN_HEADS, GDN_D
    N = T // C
    J = GDN_FWD_CHUNKS if N % GDN_FWD_CHUNKS == 0 else 1

    def body(qkv_ref, gz_ref, gab_ref, a_ref, dt_ref, nw_ref, o_ref, opre_ref, Tm_ref, Sin_ref, S_ref):
        n = pl.program_id(0)

        @pl.when(n == 0)
        def _():
            S_ref[...] = jnp.zeros_like(S_ref)

        incl, strict, eye = _chunk_masks(C)
        gab = gab_ref[...]
        per_chunk = []
        for j in range(J):
            _, beta_all, gc_all = _gdn_gates(gab[j * C:(j + 1) * C], a_ref[...], dt_ref[...], incl)
            per_chunk.append(_gdn_common(qkv_ref, gc_all, beta_all, incl, strict, eye, row0=j * C))
        q, k, v, gc, beta, decay, e_gc, e_gl, ekd, kb, A, Pm = (jnp.concatenate(t, axis=0) for t in zip(*per_chunk))
        Tm = _unit_lower_inverse(A, eye)
        u = _hbmm(Tm, v * beta)
        w = _hbmm(Tm, kb * e_gc)
        qd = q * e_gc
        kd = k * ekd
        S = S_ref[...]
        for j in range(J):
            hs = slice(j * H, (j + 1) * H)
            v_new = u[hs] - _bmm(w[hs], S)
            o = _bmm(qd[hs], S) + _bmm(Pm[hs], v_new)
            Sin_ref[j] = S
            Tm_ref[j] = Tm[hs]
            S = S * e_gl[hs] + _bmm(kd[hs], v_new, BTN)
            r = lax.rsqrt(jnp.mean(o * o, axis=-1, keepdims=True) + RMS_EPS)
            gz = _stack_heads(gz_ref, 0, H, j * C)
            _unstack_heads(opre_ref, 0, o, j * C)
            _unstack_heads(o_ref, 0, o * r * nw_ref[...] * (gz * _sigmoid(gz)), j * C)
        S_ref[...] = S

    vec = pl.BlockSpec((1, LANES), lambda n: (0, 0))
    hblk = pl.BlockSpec((J * C, GDN_W), lambda n: (n, 0))
    sblk = pl.BlockSpec((J, H, Dh, Dh), lambda n: (n, 0, 0, 0))
    return _hosted_call(
        hosted, body, grid=(N // J,),
        in_specs=[pl.BlockSpec((J * C, 3 * GDN_W), lambda n: (n, 0)),
                  pl.BlockSpec((J * C, GDN_W), lambda n: (n, CB_GZ * LANES // GDN_W)),
                  pl.BlockSpec((J * C, LANES), lambda n: (n, CB_GAB)), vec, vec, pl.BlockSpec((1, Dh), lambda n: (0, 0))],
        out_specs=[hblk, hblk, sblk, sblk],
        out_shape=[jax.ShapeDtypeStruct((T, GDN_W), F32), jax.ShapeDtypeStruct((T, GDN_W), F32)]
        + [jax.ShapeDtypeStruct((N, H, Dh, Dh), F32)] * 2,
        scratch_shapes=[pltpu.VMEM((H, Dh, Dh), F32)],
        compiler_params=_cp(("arbitrary",)), name=name)(qkv, proj, proj, a_row, dt_row, norm_w)


def gdn_chunk_bwd(qkv, proj, a_row, dt_row, norm_w, opre, Tm_all, Sin_all, docat, name, hosted=None):
    T = qkv.shape[0]
    C, H, Dh = GDN_CHUNK, GDN_HEADS, GDN_D
    N = T // C

    def body(qkv_ref, gz_ref, gab_ref, a_ref, dt_ref, nw_ref, opre_ref, Tm_ref, Sin_ref, do_ref,
             dqkv_ref, dgz_ref, dgab_ref, da_ref, ddt_ref, dnw_ref, dS_ref):
        n = pl.program_id(0)

        @pl.when(n == 0)
        def _():
            dS_ref[...] = jnp.zeros_like(dS_ref)
            da_ref[...] = jnp.zeros_like(da_ref)
            ddt_ref[...] = jnp.zeros_like(ddt_ref)
            dnw_ref[...] = jnp.zeros_like(dnw_ref)

        incl, strict, eye = _chunk_masks(C)
        gab = gab_ref[...]
        g_all, beta_all, gc_all = _gdn_gates(gab, a_ref[...], dt_ref[...], incl)
        lane = _iota2((C, LANES), 1)
        rowi = _iota2((C, 1), 0)
        nw = nw_ref[...]
        q, k, v, gc, beta, decay, e_gc, e_gl, ekd, kb, A, Pm = _gdn_common(qkv_ref, gc_all, beta_all, incl, strict, eye)
        Tm = Tm_ref[0]
        S = Sin_ref[0]
        dS = dS_ref[...]
        kbe = kb * e_gc
        u = _hbmm(Tm, v * beta)
        w = _hbmm(Tm, kbe)
        qd = q * e_gc
        kd = k * ekd
        v_new = u - _bmm(w, S)
        o = _stack_heads(opre_ref, 0, H)
        gz = _stack_heads(gz_ref, 0, H)
        don = _stack_heads(do_ref, 0, H)
        r = lax.rsqrt(jnp.mean(o * o, axis=-1, keepdims=True) + RMS_EPS)
        nn = o * r
        sgz = _sigmoid(gz)
        silu = gz * sgz
        _unstack_heads(dgz_ref, 0, don * nn * nw * (sgz * (1.0 + gz * (1.0 - sgz))))
        dnn = don * nw * silu
        dnw_ref[...] += jnp.sum(jnp.sum(don * nn * silu, axis=0), axis=0, keepdims=True)
        do = r * (dnn - nn * jnp.mean(dnn * nn, axis=-1, keepdims=True))
        dv_new = _bmm(Pm, do, BTN) + _bmm(kd, dS)
        dPm = jnp.where(incl, _bmm(do, v_new, BNT), 0.0)
        dqd = _bmm(do, S, BNT)
        dkd = _bmm(v_new, dS, BNT)
        dS_ref[...] = _bmm(qd, do, BTN) + e_gl * dS - _bmm(w, dv_new, BTN)
        dgl = jnp.sum(jnp.sum(dS * S, axis=2, keepdims=True), axis=1, keepdims=True) * e_gl
        dw = -_bmm(dv_new, S, BNT)
        dvb = _hbmm_tn(Tm, dv_new)
        dkbe = _hbmm_tn(Tm, dw)
        dA = -jnp.where(strict, _bmm(dvb, u, BNT) + _bmm(dkbe, w, BNT), 0.0)
        dAD = dA * decay
        dPD = dPm * decay
        Gm = dA * A + dPm * Pm
        dgc = jnp.sum(Gm, axis=2, keepdims=True) - _row_to_col(jnp.sum(Gm, axis=1, keepdims=True), eye)
        dkb = _bmm(dAD, k) + dkbe * e_gc
        dk = _bmm(dAD, kb, BTN) + _bmm(dPD, q, BTN) + dkd * ekd + dkb * beta
        dq = _bmm(dPD, k) + dqd * e_gc
        tkd = jnp.sum(dkd * kd, axis=-1, keepdims=True)
        dgc = dgc + jnp.sum(dqd * qd, axis=-1, keepdims=True) - tkd + jnp.sum(dkbe * kbe, axis=-1, keepdims=True)
        dgl = dgl + jnp.sum(tkd, axis=1, keepdims=True)
        dgc = dgc + jnp.where(rowi == C - 1, dgl, 0.0)
        dbeta = jnp.sum(dvb * v, axis=-1, keepdims=True) + jnp.sum(dkb * k, axis=-1, keepdims=True)
        _unstack_heads(dqkv_ref, 0, dq)
        _unstack_heads(dqkv_ref, H, dk)
        _unstack_heads(dqkv_ref, 2 * H, dvb * beta)
        dgc_all = jnp.zeros((C, LANES), F32)
        dbeta_all = jnp.zeros((C, LANES), F32)
        for h in range(H):
            dgc_all = dgc_all + jnp.where(lane == h, dgc[h], 0.0)
            dbeta_all = dbeta_all + jnp.where(lane == H + h, dbeta[h], 0.0)
        upper = (_iota2((C, C), 0) <= _iota2((C, C), 1)).astype(BF16)
        dg_all = _ones_dot_left(upper, dgc_all)
        dga = dg_all * (-jnp.exp(a_ref[...])) * _sigmoid(gab + dt_ref[...])
        dgb = dbeta_all * beta_all * (1.0 - beta_all)
        dgab_ref[...] = jnp.where(lane < H, dga, jnp.where(lane < 2 * H, dgb, 0.0))
        da_ref[...] += jnp.sum(jnp.where(lane < H, dg_all * g_all, 0.0), axis=0, keepdims=True)
        ddt_ref[...] += jnp.sum(jnp.where(lane < H, dga, 0.0), axis=0, keepdims=True)

    rev = lambda n: N - 1 - n
    vec = pl.BlockSpec((1, LANES), lambda n: (0, 0))
    nwv = pl.BlockSpec((1, Dh), lambda n: (0, 0))
    hblk = pl.BlockSpec((C, GDN_W), lambda n: (rev(n), 0))
    sblk = pl.BlockSpec((1, H, Dh, Dh), lambda n: (rev(n), 0, 0, 0))
    qblk = pl.BlockSpec((C, 3 * GDN_W), lambda n: (rev(n), 0))
    return _hosted_call(
        hosted, body, grid=(N,),
        in_specs=[qblk, pl.BlockSpec((C, GDN_W), lambda n: (rev(n), CB_GZ * LANES // GDN_W)),
                  pl.BlockSpec((C, LANES), lambda n: (rev(n), CB_GAB)), vec, vec, nwv, hblk, sblk, sblk, hblk],
        out_specs=[qblk, hblk, pl.BlockSpec((C, LANES), lambda n: (rev(n), 0)), vec, vec, nwv],
        out_shape=[jax.ShapeDtypeStruct((T, 3 * GDN_W), F32), jax.ShapeDtypeStruct((T, GDN_W), F32),
                   jax.ShapeDtypeStruct((T, LANES), F32), jax.ShapeDtypeStruct((1, LANES), F32),
                   jax.ShapeDtypeStruct((1, LANES), F32), jax.ShapeDtypeStruct((1, Dh), F32)],
        scratch_shapes=[pltpu.VMEM((H, Dh, Dh), F32)],
        compiler_params=_cp(("arbitrary",)), name=name)(qkv, proj, proj, a_row, dt_row, norm_w, opre, Tm_all, Sin_all, docat)


ATT_BQ, ATT_BK = 512, 1024
NEG_BIG = -1e30


def _att_blocks(T):
    bq, bk = min(ATT_BQ, T), min(ATT_BK, T)
    assert bk % bq == 0 and T % bk == 0
    return bq, bk


def _att_specs(T, bq, cbs):
    qspec = lambda cb: pl.BlockSpec((bq, LANES), lambda h, i: (i, cb + h))
    kspec = lambda cb: pl.BlockSpec((T, LANES), lambda h, i: (0, cb + h))
    return qspec, kspec


def _kblock(ref, kb, bk):
    return ref[pl.ds(pl.multiple_of(kb * bk, bk), bk), :]


def _att_pos(i, kb, bq, bk):
    qpos = i * bq + _iota2((bq, bk), 0)
    kpos = kb * bk + _iota2((bq, bk), 1)
    return qpos, kpos


def _later_keys(n):
    return (_iota2((n, n), 0) > _iota2((n, n), 1)).astype(BF16)


def _earlier_keys(n):
    return (_iota2((n, n), 0) < _iota2((n, n), 1)).astype(BF16)


def _tri_dot(x, tri, terms):
    acc, rest = None, x
    for t in range(terms):
        part = rest.astype(BF16)
        if t + 1 < terms:
            rest = rest - part.astype(F32)
        d = jnp.dot(part, tri, preferred_element_type=F32)
        acc = d if acc is None else acc + d
    return acc


SB_BLOCK = 256
SB_DEAD = -104.0


def _sb_blocks(T):
    b = min(SB_BLOCK, T)
    assert T % b == 0 and T // b <= LANES
    return b, b


def sb_fwd(proj, name, hosted=None):
    T = proj.shape[0]
    H = SB_HEADS
    bq, bk = _sb_blocks(T)
    scale = SB_DIM ** -0.5

    def body(q_ref, k_ref, v_ref, o_ref, tot_ref):
        i = pl.program_id(1)
        qb = q_ref[...].astype(BF16)
        diag = (i * bq) // bk
        lane = _iota2((bq, LANES), 1)
        later = _later_keys(bk)

        def block(kb, acc, R, masked):
            z = _bdot_nt(qb, _kblock(k_ref, kb, bk)) * scale
            sp = _softplus(z)
            if masked:
                qpos, kpos = _att_pos(i, kb, bq, bk)
                mask = kpos < qpos
                l1m = jnp.where(mask, -sp, 0.0)
            else:
                l1m = -sp
            W = jnp.exp((z - sp) + _tri_dot(l1m, later, 3) + R)
            if masked:
                W = jnp.where(mask, W, 0.0)
            acc = acc + _bdot(W, _kblock(v_ref, kb, bk))
            return acc, R + jnp.sum(l1m, axis=-1, keepdims=True)

        acc, R = block(diag, jnp.zeros((bq, LANES), F32), jnp.zeros((bq, 1), F32), True)

        def live(c):
            return jnp.logical_and(c[0] >= 0, jnp.max(c[2]) > SB_DEAD)

        def step(c):
            kb, acc, R, Rb = c
            acc, R_next = block(kb, acc, R, False)
            return kb - 1, acc, R_next, jnp.where(lane == kb, R, Rb)

        _, acc, _, Rb = lax.while_loop(live, step, (diag - 1, acc, R, jnp.where(lane == diag, 0.0, NEG_BIG)))
        o_ref[...] = acc
        tot_ref[...] = Rb

    qspec, kspec = _att_specs(T, bq, None)
    sds = jax.ShapeDtypeStruct((T, H * LANES), F32)
    oblk = pl.BlockSpec((bq, LANES), lambda h, i: (i, h))
    return _hosted_call(
        hosted, body, grid=(H, T // bq), in_specs=[qspec(CB_SQ), kspec(CB_SK), kspec(CB_SV)],
        out_specs=[oblk, oblk], out_shape=[sds, sds],
        compiler_params=_cp(("parallel", "parallel")), name=name)(proj, proj, proj)


def sb_bwd(proj, tot, docat, do_cb, name):
    T = proj.shape[0]
    H = SB_HEADS
    bq, bk = _sb_blocks(T)
    scale = SB_DIM ** -0.5

    def body(q_ref, k_ref, v_ref, tot_ref, do_ref, dq_ref, dk_ref, dv_ref):
        i = pl.program_id(1)

        @pl.when(i == 0)
        def _():
            dk_ref[...] = jnp.zeros_like(dk_ref)
            dv_ref[...] = jnp.zeros_like(dv_ref)

        qb = q_ref[...].astype(BF16)
        dob = do_ref[...].astype(BF16)
        Rb = tot_ref[...]
        diag = (i * bq) // bk
        lane = _iota2((bq, LANES), 1)
        later, earlier = _later_keys(bk), _earlier_keys(bk)
        first = lax.while_loop(
            lambda kb: jnp.logical_and(kb < diag, jnp.max(jnp.where(lane == kb, Rb, NEG_BIG)) <= SB_DEAD),
            lambda kb: kb + 1, jnp.int32(0))

        def block(kb, carry, masked):
            dq, Epre = carry
            R = jnp.sum(jnp.where(lane == kb, Rb, 0.0), axis=1, keepdims=True)
            kblk = _kblock(k_ref, kb, bk).astype(BF16)
            z = _bdot_nt(qb, kblk) * scale
            sp = _softplus(z)
            if masked:
                qpos, kpos = _att_pos(i, kb, bq, bk)
                mask = kpos < qpos
                l1m = jnp.where(mask, -sp, 0.0)
            else:
                l1m = -sp
            W = jnp.exp((z - sp) + _tri_dot(l1m, later, 3) + R)
            if masked:
                W = jnp.where(mask, W, 0.0)
            E = _bdot_nt(dob, _kblock(v_ref, kb, bk)) * W
            cexcl = _tri_dot(E, earlier, 3) + Epre
            neg = jnp.exp(-sp)
            dz = E * neg - cexcl * (1.0 - neg)
            if masked:
                dz = jnp.where(mask, dz, 0.0)
            dz = (dz * scale).astype(BF16)
            rows = pl.ds(pl.multiple_of(kb * bk, bk), bk)
            dk_ref[rows, :] += lax.dot_general(dz, qb, TN_DIMS, preferred_element_type=F32)
            dv_ref[rows, :] += lax.dot_general(W.astype(BF16), dob, TN_DIMS, preferred_element_type=F32)
            dq = dq + jnp.dot(dz, kblk, preferred_element_type=F32)
            return dq, Epre + jnp.sum(E, axis=-1, keepdims=True)

        init = (jnp.zeros((bq, LANES), F32), jnp.zeros((bq, 1), F32))
        carry = lax.fori_loop(first, diag, lambda kb, c: block(kb, c, False), init)
        dq, _ = block(diag, carry, True)
        dq_ref[...] = dq

    qspec, kspec = _att_specs(T, bq, None)
    sds = jax.ShapeDtypeStruct((T, H * LANES), F32)
    oblk = pl.BlockSpec((bq, LANES), lambda h, i: (i, h))
    kout = pl.BlockSpec((T, LANES), lambda h, i: (0, h))
    return pl.pallas_call(
        body, grid=(H, T // bq),
        in_specs=[qspec(CB_SQ), kspec(CB_SK), kspec(CB_SV), oblk, qspec(do_cb)],
        out_specs=[oblk, kout, kout], out_shape=[sds, sds, sds],
        compiler_params=_cp(("arbitrary", "arbitrary")), name=name)(proj, proj, proj, tot, docat)


def mla_fwd(Q, K, V, name, hosted=None):
    T = Q.shape[0]
    H = MLA_HEADS
    bq, bk = _att_blocks(T)
    scale = (MLA_NOPE + MLA_ROPE) ** -0.5

    def body(q_ref, k_ref, v_ref, o_ref, lse_ref):
        i = pl.program_id(1)
        qb = q_ref[...]
        diag = (i * bq) // bk

        def block(kb, carry, masked):
            acc, m, l = carry
            s = _bdot_nt(qb, _kblock(k_ref, kb, bk)) * scale
            if masked:
                qpos, kpos = _att_pos(i, kb, bq, bk)
                s = jnp.where(kpos <= qpos, s, NEG_BIG)
            m_new = jnp.maximum(m, jnp.max(s, axis=-1, keepdims=True))
            p = jnp.exp(s - m_new)
            corr = jnp.exp(m - m_new)
            acc = corr * acc + _bdot(p, _kblock(v_ref, kb, bk))
            return acc, m_new, corr * l + jnp.sum(p, axis=-1, keepdims=True)

        init = (jnp.zeros((bq, LANES), F32), jnp.full((bq, 1), NEG_BIG, F32), jnp.zeros((bq, 1), F32))
        carry = lax.fori_loop(0, diag, lambda kb, c: block(kb, c, False), init)
        acc, m, l = block(diag, carry, True)
        o_ref[...] = acc / l
        lse_ref[...] = jnp.broadcast_to(m + jnp.log(l), (bq, LANES))

    qspec, kspec = _att_specs(T, bq, None)
    sds = jax.ShapeDtypeStruct((T, H * LANES), F32)
    oblk = pl.BlockSpec((bq, LANES), lambda h, i: (i, h))
    return _hosted_call(
        hosted, body, grid=(H, T // bq), in_specs=[qspec(0), kspec(0), kspec(0)],
        out_specs=[oblk, oblk], out_shape=[sds, sds],
        compiler_params=_cp(("parallel", "parallel")), name=name)(Q, K, V)


def mla_bwd(Q, K, V, o, lse, docat, do_cb, name, hosted=None):
    T = Q.shape[0]
    H = MLA_HEADS
    bq, bk = _att_blocks(T)
    scale = (MLA_NOPE + MLA_ROPE) ** -0.5

    def body(q_ref, k_ref, v_ref, o_ref, lse_ref, do_ref, dq_ref, dk_ref, dv_ref):
        i = pl.program_id(1)

        @pl.when(i == 0)
        def _():
            dk_ref[...] = jnp.zeros_like(dk_ref)
            dv_ref[...] = jnp.zeros_like(dv_ref)

        qb = q_ref[...]
        do = do_ref[...]
        dob = do.astype(BF16)
        delta = jnp.sum(do * o_ref[...], axis=-1, keepdims=True)
        lse = lse_ref[:, 0:1]

        diag = (i * bq) // bk

        def block(kb, dq, masked):
            kblk = _kblock(k_ref, kb, bk)
            s = _bdot_nt(qb, kblk) * scale
            if masked:
                qpos, kpos = _att_pos(i, kb, bq, bk)
                s = jnp.where(kpos <= qpos, s, NEG_BIG)
            p = jnp.exp(s - lse)
            dp = _bdot_nt(dob, _kblock(v_ref, kb, bk))
            ds = (p * (dp - delta) * scale).astype(BF16)
            rows = pl.ds(pl.multiple_of(kb * bk, bk), bk)
            dk_ref[rows, :] += lax.dot_general(ds, qb, TN_DIMS, preferred_element_type=F32)
            dv_ref[rows, :] += lax.dot_general(p.astype(BF16), dob, TN_DIMS, preferred_element_type=F32)
            return dq + jnp.dot(ds, kblk, preferred_element_type=F32)

        dq = lax.fori_loop(0, diag, lambda kb, c: block(kb, c, False), jnp.zeros((bq, LANES), F32))
        dq_ref[...] = block(diag, dq, True)

    qspec, kspec = _att_specs(T, bq, None)
    sds = jax.ShapeDtypeStruct((T, H * LANES), F32)
    oblk = pl.BlockSpec((bq, LANES), lambda h, i: (i, h))
    kout = pl.BlockSpec((T, LANES), lambda h, i: (0, h))
    return _hosted_call(
        hosted, body, grid=(H, T // bq),
        in_specs=[qspec(0), kspec(0), kspec(0), oblk, oblk, qspec(do_cb)],
        out_specs=[oblk, kout, kout], out_shape=[sds, sds, sds],
        compiler_params=_cp(("arbitrary", "arbitrary")), name=name)(Q, K, V, o, lse, docat)


def _tile_heads(t, n):
    return jnp.concatenate([t] * n, axis=1)


def _rope(X, C, Sn, Sp):
    n = X.shape[1]
    return X * C + pltpu.roll(X, n - HALF_ROPE, 1) * Sn + pltpu.roll(X, HALF_ROPE, 1) * Sp


def _rope_t(dO, C, Sn, Sp):
    n = dO.shape[1]
    return dO * C + pltpu.roll(dO * Sn, HALF_ROPE, 1) + pltpu.roll(dO * Sp, n - HALF_ROPE, 1)


def _rms(x, w):
    r = lax.rsqrt(jnp.mean(x * x, axis=-1, keepdims=True) + RMS_EPS)
    xh = x * r
    return r, xh, xh * w


def _rms_bwd(dn, w, r, xh):
    dxh = dn * w
    return r * (dxh - xh * jnp.mean(dxh * xh, axis=-1, keepdims=True)), jnp.sum(dn * xh, axis=0, keepdims=True)


def _mla_pre_specs(T, tm):
    KV = MLA_KV_RANK
    QR = MLA_Q_RANK
    W = MLA_HEADS * LANES
    full = lambda shape: pl.BlockSpec(shape, lambda i: (0, 0))
    specs = [pl.BlockSpec((tm, QR), lambda i: (i, CB_MQ * LANES // QR)),
             pl.BlockSpec((tm, 2 * LANES), lambda i: (i, CB_MKV // 2)),
             full((1, QR)), full((1, KV))]
    rope = [pl.BlockSpec((tm, LANES), lambda i: (i, 0))] * 3
    return specs, rope, full, W


def mla_pre_fwd(proj, wq, wkv, wuq, wuk, wuv, ropeC, ropeSn, ropeSp, name):
    T = proj.shape[0]
    tm = _pick(T, 512, 16)
    KV = MLA_KV_RANK
    H = MLA_HEADS

    def body(mq_ref, mkv_ref, wq_ref, wkv_ref, wuq_ref, wuk_ref, wuv_ref, c_ref, sn_ref, sp_ref, Q_ref, K_ref, V_ref):
        C, Sn, Sp = (_tile_heads(t[...], H) for t in (c_ref, sn_ref, sp_ref))
        _, _, qn = _rms(mq_ref[...], wq_ref[...])
        Q_ref[...] = _rope(_bdot(qn, wuq_ref[...]), C, Sn, Sp).astype(BF16)
        mkv = mkv_ref[...]
        _, _, kvn = _rms(mkv[:, :KV], wkv_ref[...])
        kr = pltpu.roll(mkv[:, KV:], MLA_NOPE, 1)
        K_ref[...] = _rope(_bdot(kvn, wuk_ref[...]) + _tile_heads(kr, H), C, Sn, Sp).astype(BF16)
        V_ref[...] = _bdot(kvn, wuv_ref[...]).astype(BF16)

    specs, rope, full, W = _mla_pre_specs(T, tm)
    oblk = pl.BlockSpec((tm, W), lambda i: (i, 0))
    sds = jax.ShapeDtypeStruct((T, W), BF16)
    return pl.pallas_call(
        body, grid=(T // tm,),
        in_specs=specs + [full(wuq.shape), full(wuk.shape), full(wuv.shape)] + rope,
        out_specs=[oblk, oblk, oblk], out_shape=[sds, sds, sds],
        compiler_params=_cp(("parallel",)), name=name)(proj, proj, wq, wkv, wuq, wuk, wuv, ropeC, ropeSn, ropeSp)


def mla_pre_bwd(proj, wq, wkv, wuq, wuk, wuv, ropeC, ropeSn, ropeSp, dQ, dK, dV, name):
    T = proj.shape[0]
    tm = _pick(T, 512, 16)
    KV = MLA_KV_RANK
    H = MLA_HEADS

    def body(mq_ref, mkv_ref, wq_ref, wkv_ref, wuq_ref, wuk_ref, wuv_ref,
             c_ref, sn_ref, sp_ref, dQ_ref, dK_ref, dV_ref,
             dmq_ref, dmkv_ref, dwuq_ref, dwuk_ref, dwuv_ref, dwq_ref, dwkv_ref):
        i = pl.program_id(0)

        @pl.when(i == 0)
        def _():
            for ref in (dwuq_ref, dwuk_ref, dwuv_ref, dwq_ref, dwkv_ref):
                ref[...] = jnp.zeros_like(ref)

        C, Sn, Sp = (_tile_heads(t[...], H) for t in (c_ref, sn_ref, sp_ref))
        rq, xq, qn = _rms(mq_ref[...], wq_ref[...])
        mkv = mkv_ref[...]
        rkv, xkv, kvn = _rms(mkv[:, :KV], wkv_ref[...])
        dqf = _rope_t(dQ_ref[...], C, Sn, Sp)
        dkf = _rope_t(dK_ref[...], C, Sn, Sp)
        dv = dV_ref[...]
        dwuq_ref[...] += _bdot_tn(qn, dqf)
        dwuk_ref[...] += _bdot_tn(kvn, dkf)
        dwuv_ref[...] += _bdot_tn(kvn, dv)
        dmq, dwq = _rms_bwd(_bdot_nt(dqf, wuq_ref[...]), wq_ref[...], rq, xq)
        dckv, dwkv = _rms_bwd(_bdot_nt(dkf, wuk_ref[...]) + _bdot_nt(dv, wuv_ref[...]), wkv_ref[...], rkv, xkv)
        dwq_ref[...] += dwq
        dwkv_ref[...] += dwkv
        dmq_ref[...] = dmq
        dkr = dkf[:, 0:LANES]
        for h in range(1, H):
            dkr = dkr + dkf[:, h * LANES:(h + 1) * LANES]
        dkr = pltpu.roll(dkr, LANES - MLA_NOPE, 1)
        dkr = jnp.where(_iota2(dkr.shape, 1) < MLA_ROPE, dkr, 0.0)
        dmkv_ref[...] = jnp.concatenate([dckv, dkr], axis=1)

    specs, rope, full, W = _mla_pre_specs(T, tm)
    wide = pl.BlockSpec((tm, W), lambda i: (i, 0))
    return pl.pallas_call(
        body, grid=(T // tm,),
        in_specs=specs + [full(w.shape) for w in (wuq, wuk, wuv)] + rope + [wide, wide, wide],
        out_specs=[pl.BlockSpec((tm, MLA_Q_RANK), lambda i: (i, 0)), pl.BlockSpec((tm, 2 * LANES), lambda i: (i, 0)),
                   full(wuq.shape), full(wuk.shape), full(wuv.shape), full((1, MLA_Q_RANK)), full((1, KV))],
        out_shape=[jax.ShapeDtypeStruct((T, MLA_Q_RANK), F32), jax.ShapeDtypeStruct((T, 2 * LANES), F32),
                   jax.ShapeDtypeStruct(wuq.shape, F32), jax.ShapeDtypeStruct(wuk.shape, F32),
                   jax.ShapeDtypeStruct(wuv.shape, F32), jax.ShapeDtypeStruct((1, MLA_Q_RANK), F32),
                   jax.ShapeDtypeStruct((1, KV), F32)],
        compiler_params=_cp(("arbitrary",)), name=name)(
            proj, proj, wq, wkv, wuq, wuk, wuv, ropeC, ropeSn, ropeSp, dQ, dK, dV)


def all_gather(shards, name):
    n = len(shards)

    def body(*refs):
        x_refs, out_refs = refs[:n], refs[n:2 * n]
        send_sems, recv_sems, local_sems = refs[2 * n:]
        x, y, c = _place()
        me, sibling = (x, y, c), (x, y, 1 - c)
        chips = [(1 - x, y), (x, 1 - y), (1 - x, 1 - y)]

        def slot(a, px, py, pc):
            return out_refs[a].at[4 * px + 2 * py + pc]

        def copy(a, k, block, to, src=None):
            return pltpu.make_async_remote_copy(
                src_ref=slot(a, *block) if src is None else src, dst_ref=slot(a, *block),
                send_sem=send_sems.at[a, k], recv_sem=recv_sems.at[a, k], device_id=to, device_id_type=MESH)

        mine = [pltpu.make_async_copy(x_refs[a], slot(a, *me), local_sems.at[a]) for a in range(n)]
        first = []
        for a in range(n):
            mine[a].start()
            first.append(copy(a, 0, me, sibling, src=x_refs[a]))
            first += [copy(a, 1 + j, me, (*chip, c), src=x_refs[a]) for j, chip in enumerate(chips)]
        for cp in first:
            cp.start()
        passed = []
        for j, chip in enumerate(chips):
            for a in range(n):
                copy(a, 1 + j, (*chip, c), me).wait_recv()
                passed.append(copy(a, 4 + j, (*chip, c), sibling))
                passed[-1].start()
        for a in range(n):
            copy(a, 0, sibling, me).wait_recv()
            for j, chip in enumerate(chips):
                copy(a, 4 + j, (*chip, 1 - c), me).wait_recv()
        for cp in first + passed:
            cp.wait_send()
        for cp in mine:
            cp.wait()

    return pl.pallas_call(
        body, out_shape=[jax.ShapeDtypeStruct((N_DEV,) + s.shape, s.dtype) for s in shards],
        in_specs=[ANY] * n, out_specs=[ANY] * n,
        scratch_shapes=[pltpu.SemaphoreType.DMA((n, 7)), pltpu.SemaphoreType.DMA((n, 7)), pltpu.SemaphoreType.DMA((n,))],
        name=name)(*shards)


def reduce_adamw(parts, w, m, v, name):
    L = len(parts)
    n, Rl, C = parts[0].shape
    R = w.shape[0]
    assert R == L * Rl
    tr = Rl if Rl * C <= 256 * 1024 else _pick(Rl, 256, 16)
    nr = Rl // tr

    def body(*refs):
        p_refs = refs[:L]
        w_ref, m_ref, v_ref, g_ref, d_ref, nm_ref, nv_ref, sum_ref = refs[L:]
        grp = pl.program_id(0)
        for j in range(L):
            @pl.when(grp == j)
            def _(j=j):
                acc = p_refs[j][0].astype(F32)
                for s in range(1, n):
                    acc = acc + p_refs[j][s].astype(F32)
                sum_ref[...] = acc

        g_ = sum_ref[...]
        m_ = ADAM_B1 * m_ref[...] + (1.0 - ADAM_B1) * g_
        v_ = ADAM_B2 * v_ref[...] + (1.0 - ADAM_B2) * (g_ * g_)
        m_hat = m_ / (1.0 - ADAM_B1 ** ADAM_STEP)
        v_hat = v_ / (1.0 - ADAM_B2 ** ADAM_STEP)
        g_ref[...] = g_
        d_ref[...] = -ADAM_LR * (m_hat / (jnp.sqrt(v_hat) + ADAM_EPS) + ADAM_WD * w_ref[...])
        nm_ref[...] = m_
        nv_ref[...] = v_

    blk = pl.BlockSpec((tr, C), lambda l, r: (l * nr + r, 0))
    sds = jax.ShapeDtypeStruct((R, C), F32)
    p_specs = [pl.BlockSpec((n, tr, C), lambda l, r, j=j: (0, jnp.where(l == j, r, 0), 0)) for j in range(L)]
    return pl.pallas_call(
        body, grid=(L, nr), in_specs=p_specs + [blk] * 3,
        out_specs=[blk] * 4, out_shape=[sds] * 4, scratch_shapes=[pltpu.VMEM((tr, C), F32)],
        compiler_params=_cp(("arbitrary", "arbitrary")), name=name)(*parts, w, m, v)


SHARDED = {"ffa_w_in": (2, BF16), "ffa_w_out": (1, BF16), "mix_w_in": (2, BF16), "mla_w_uq": (2, BF16),
           "mla_w_ukv": (2, BF16), "mix_w_o": (1, BF16), "ffb_w_in": (2, BF16), "ffb_w_out": (1, BF16),
           "ple_w_gate": (1, BF16), "ple_w_proj": (2, BF16), "gdn_conv_w": (2, F32), "ln_g": (2, F32), "ln_b": (2, F32)}
FFN_SLOT = ("ffa_w_in", "ffa_w_out", "ffb_w_in", "ffb_w_out")
REPLICATED = ("gdn_a_log", "gdn_dt_bias", "gdn_norm_w", "mla_q_norm_w", "mla_kv_norm_w")
WEIGHTS = ("ffa_w_in", "ffa_w_out", "mix_w_in", "gdn_conv_w", "gdn_a_log", "gdn_dt_bias", "gdn_norm_w", "mla_q_norm_w",
           "mla_kv_norm_w", "mla_w_uq", "mla_w_ukv", "mix_w_o", "ffb_w_in", "ffb_w_out", "ln_g", "ln_b", "ple_w_gate",
           "ple_w_proj")


def _to_slots(full, axis):
    L, a, b = full.shape
    if axis == 2:
        return full.reshape(L, a, N_DEV, b // N_DEV).transpose(2, 0, 1, 3).reshape(N_DEV, L * a, b // N_DEV)
    return full.reshape(L, N_DEV, a // N_DEV, b).transpose(1, 0, 2, 3).reshape(N_DEV, L * a // N_DEV, b)


def _from_slots(slots, shard_shape, axis):
    L, a, b = shard_shape
    t = slots.reshape((N_DEV,) + tuple(shard_shape))
    if axis == 2:
        return t.transpose(1, 2, 0, 3).reshape(L, a, N_DEV * b)
    return t.transpose(1, 0, 2, 3).reshape(L, N_DEV * a, b)


def _view2d(t):
    return t.reshape(-1, t.shape[-1])


def _pad_heads(w, nh):
    K = w.shape[0]
    return jnp.pad(w.reshape(K, nh, GDN_D), ((0, 0), (0, 0), (0, LANES - GDN_D))).reshape(K, nh * LANES)


def _unpad_heads(w, nh):
    K = w.shape[0]
    return w.reshape(K, nh, LANES)[:, :, :GDN_D].reshape(K, nh * GDN_D)


IN_WIDTHS = (512, 512, 512, 512, 8, 8, 256, 256, 256, 256, 160)


def _split_in(w):
    offs = np.cumsum((0,) + IN_WIDTHS)
    return [w[:, int(offs[i]):int(offs[i + 1])] for i in range(len(IN_WIDTHS))]


def _pad_in_proj(w):
    gq, gk, gv, gz, ga, gb, sq, sk, sv, mq, mkv = _split_in(w)
    gab = jnp.pad(jnp.concatenate([ga, gb], axis=1), ((0, 0), (0, LANES - 2 * GDN_HEADS)))
    return jnp.concatenate(
        [gq, gk, gv, gz] + [_pad_heads(t, SB_HEADS) for t in (sq, sk, sv)]
        + [mq, jnp.pad(mkv, ((0, 0), (0, 2 * LANES - mkv.shape[1]))), gab], axis=1)


def _unpad_in_proj(wp):
    c = lambda cb, n: wp[:, cb * LANES:(cb + n) * LANES]
    gab = c(CB_GAB, 1)
    parts = [c(cb, DO_SB) for cb in (CB_GQ, CB_GK, CB_GV, CB_GZ)]
    parts += [gab[:, :GDN_HEADS], gab[:, GDN_HEADS:2 * GDN_HEADS]]
    parts += [_unpad_heads(c(cb, SB_HEADS), SB_HEADS) for cb in (CB_SQ, CB_SK, CB_SV)]
    parts += [c(CB_MQ, 2), c(CB_MKV, 2)[:, :MLA_KV_RANK + MLA_ROPE]]
    return jnp.concatenate(parts, axis=1)


def _pad_lanes(w, width):
    return jnp.pad(w, ((0, 0), (0, width - w.shape[1])))


def _mla_up_pad(w_uq, w_ukv):
    H = MLA_HEADS
    dq = MLA_NOPE + MLA_ROPE
    wuq = jnp.pad(w_uq.reshape(-1, H, dq), ((0, 0), (0, 0), (0, LANES - dq))).reshape(-1, H * LANES)
    kv = w_ukv.reshape(-1, H, MLA_NOPE + MLA_V)
    wuk = jnp.pad(kv[:, :, :MLA_NOPE], ((0, 0), (0, 0), (0, LANES - MLA_NOPE))).reshape(-1, H * LANES)
    wuv = jnp.pad(kv[:, :, MLA_NOPE:], ((0, 0), (0, 0), (0, LANES - MLA_V))).reshape(-1, H * LANES)
    return wuq, wuk, wuv


def _mla_up_unpad(dwuq, dwuk, dwuv):
    H = MLA_HEADS
    dq = MLA_NOPE + MLA_ROPE
    g_uq = dwuq.reshape(-1, H, LANES)[:, :, :dq].reshape(-1, H * dq)
    g_ukv = jnp.concatenate([dwuk.reshape(-1, H, LANES)[:, :, :MLA_NOPE], dwuv.reshape(-1, H, LANES)[:, :, :MLA_V]],
                            axis=2).reshape(-1, H * (MLA_NOPE + MLA_V))
    return g_uq, g_ukv


def _rope_tables(positions):
    inv = 1.0 / (ROPE_BASE ** (jnp.arange(0, MLA_ROPE, 2, dtype=F32) / MLA_ROPE))
    ang = positions.astype(F32)[:, None] * inv
    cos, sin = jnp.cos(ang), jnp.sin(ang)
    T = positions.shape[0]
    one = lambda n: jnp.ones((T, n), F32)
    zero = lambda n: jnp.zeros((T, n), F32)
    tail = LANES - MLA_NOPE - MLA_ROPE
    C = jnp.concatenate([one(MLA_NOPE), cos, cos, one(tail)], axis=1)
    Sn = jnp.concatenate([zero(MLA_NOPE), -sin, zero(HALF_ROPE + tail)], axis=1)
    Sp = jnp.concatenate([zero(MLA_NOPE + HALF_ROPE), sin, zero(tail)], axis=1)
    return C, Sn, Sp


GATHER_FIRST = [("ffa_w_in", 0), ("ffa_w_out", 0)] + [(n, l) for l in range(DEPTH) for n in ("gdn_conv_w", "ln_g", "ln_b")]
GATHER_PLAN = {
    (0, "ffa_fwd"): [("mix_w_in", 0), ("mla_w_uq", 0), ("mla_w_ukv", 0), ("mix_w_o", 0)],
    (0, "in_proj"): [("ple_w_gate", 0), ("ple_w_proj", 0)],
    (0, "gdn_chunk_fwd"): [("ffb_w_in", 0)],
    (0, "sb_fwd"): [("ffb_w_out", 0), ("mix_w_o", 1)],
    (0, "mla_fwd"): [("ffa_w_out", 1)],
    (0, "ffb_fwd"): [("ffa_w_in", 1)],
    (1, "ffa_fwd"): [("mix_w_in", 1)],
    (1, "in_proj"): [("mla_w_uq", 1), ("mla_w_ukv", 1)],
    (1, "gdn_chunk_fwd"): [("ffb_w_in", 1)],
    (1, "sb_fwd"): [("ffb_w_out", 1), ("ple_w_gate", 1), ("ple_w_proj", 1)],
}
SCATTER_PLAN = {
    (1, "gdn_chunk_bwd"): [("ffb_w_in", 1)],
    (1, "gdn_pre_bwd"): [("ffb_w_out", 1), ("ple_w_gate", 1), ("ple_w_proj", 1), ("mix_w_o", 1)],
    (1, "ffa_bwd"): [("mix_w_in", 1), ("mla_w_uq", 1), ("mla_w_ukv", 1), ("gdn_conv_w", 1)],
    (0, "ffb_bwd"): [("ffa_w_in", 1)],
    (0, "gdn_chunk_bwd"): [("ffb_w_in", 0)],
    (0, "gdn_pre_bwd"): [("ffb_w_out", 0), ("ple_w_gate", 0), ("ple_w_proj", 0), ("mix_w_o", 0)],
    (0, "mla_bwd"): [("ffa_w_out", 1), ("ln_g", 1), ("ln_b", 1)],
    (0, "ffa_bwd"): [("mix_w_in", 0), ("mla_w_uq", 0), ("mla_w_ukv", 0), ("gdn_conv_w", 0)],
    (0, "d_ffa_in"): [("ffa_w_out", 0), ("ln_g", 0), ("ln_b", 0)],
}
SCATTER_LAST = [("ffa_w_in", 0)]


class Exchanges:
    def __init__(self, shards):
        self.shards = shards
        self.full = {}
        self.partial = {}
        self.received = {}

    def _block(self, key):
        n, l = key
        return self.shards[n][l].astype(SHARDED[n][1])

    def _absorb_gather(self, keys, results):
        for (n, l), g in zip(keys, results):
            blk = self.shards[n][l]
            self.full[(n, l)] = g if n in FFN_SLOT else _from_slots(g, (1,) + blk.shape, SHARDED[n][0])[0]

    def gather_now(self, keys, name):
        self._absorb_gather(keys, all_gather([self._block(k) for k in keys], name))

    def gather_with(self, layer, tag):
        keys = GATHER_PLAN.get((layer, tag))
        return None if keys is None else (keys, Hosted("gather", [self._block(k) for k in keys]))

    def scatter_with(self, layer, tag):
        keys = SCATTER_PLAN.get((layer, tag))
        return None if keys is None else (keys, Hosted("scatter", [self.partial[k] for k in keys]))

    def done(self, carried):
        if carried is not None:
            keys, hosted = carried
            if hosted.kind == "gather":
                self._absorb_gather(keys, hosted.results)
            else:
                self.received.update(zip(keys, hosted.results))

    def add_grad(self, key, g):
        n, l = key
        self.partial[key] = g if n in FFN_SLOT else _to_slots(g[None], SHARDED[n][0]).astype(SHARDED[n][1])


def _carried(c):
    return None if c is None else c[1]


def _layer_fwd(h0, p_i, rope, i, ex, rep):
    L = "L%d_" % i
    S = {"h0": h0, "p": p_i}
    W = ex.full
    ln_g = [W[("ln_g", i)][j][None, :] for j in range(3)]
    ln_b = [W[("ln_b", i)][j][None, :] for j in range(3)]
    S["ln_g"] = ln_g
    c = ex.gather_with(i, "ffa_fwd")
    S["h1"], S["xh1"], S["rs1"], S["Ga"], S["Ua"], S["h1b"], S["h1t"] = ffn_fwd(
        h0, W[("ffa_w_in", i)], W[("ffa_w_out", i)], ln_g[0], ln_b[0], L + "ffa_fwd", hosted=_carried(c))
    ex.done(c)
    S["win"] = _pad_in_proj(W[("mix_w_in", i)])
    c = ex.gather_with(i, "in_proj")
    S["proj"] = mm_nn(S["h1b"], S["win"], L + "in_proj", hosted=_carried(c))
    ex.done(c)
    S["conv"] = W[("gdn_conv_w", i)]
    S["a_row"] = _pad_lanes(rep["gdn_a_log"][i][None, :], LANES)
    S["dt_row"] = _pad_lanes(rep["gdn_dt_bias"][i][None, :], LANES)
    S["nw"] = rep["gdn_norm_w"][i][None, :]
    S["wq"] = rep["mla_q_norm_w"][i][None, :]
    S["wkv"] = rep["mla_kv_norm_w"][i][None, :]
    S["qkv"] = gdn_pre_fwd(S["proj"], S["conv"], L + "gdn_pre_fwd")
    c = ex.gather_with(i, "gdn_chunk_fwd")
    S["o_gdn"], S["opre"], S["Tm"], S["Sin"] = gdn_chunk_fwd(
        S["qkv"], S["proj"], S["a_row"], S["dt_row"], S["nw"], L + "gdn_chunk_fwd", hosted=_carried(c))
    ex.done(c)
    c = ex.gather_with(i, "sb_fwd")
    S["o_sb"], S["tot"] = sb_fwd(S["proj"], L + "sb_fwd", hosted=_carried(c))
    ex.done(c)
    S["wuq"], S["wuk"], S["wuv"] = _mla_up_pad(W[("mla_w_uq", i)], W[("mla_w_ukv", i)])
    S["Q"], S["K"], S["V"] = mla_pre_fwd(S["proj"], S["wq"], S["wkv"], S["wuq"], S["wuk"], S["wuv"], *rope, L + "mla_pre_fwd")
    c = ex.gather_with(i, "mla_fwd")
    S["o_mla"], S["lse"] = mla_fwd(S["Q"], S["K"], S["V"], L + "mla_fwd", hosted=_carried(c))
    ex.done(c)
    wo = W[("mix_w_o", i)]
    wo_att = wo[GDN_W:].reshape(-1, GDN_D, wo.shape[1])
    S["wo"] = jnp.concatenate(
        [wo[:GDN_W], jnp.pad(wo_att, ((0, 0), (0, LANES - GDN_D), (0, 0))).reshape(-1, wo.shape[1])], axis=0)
    S["h2"], S["xh2"], S["rs2"], _, S["h2t"] = proj_res_ln([S["o_gdn"], S["o_sb"], S["o_mla"]], S["wo"], S["h1"],
                                                        ln_g[1], ln_b[1], L + "out_proj")
    c = ex.gather_with(i, "ffb_fwd")
    S["h3"], S["xh3"], S["rs3"], S["Gb"], S["Ub"], h3b, _ = ffn_fwd(
        S["h2"], W[("ffb_w_in", i)], W[("ffb_w_out", i)], ln_g[2], ln_b[2], L + "ffb_fwd", hosted=_carried(c))
    ex.done(c)
    h4, S["a"], S["e"] = ple_fwd(S["h3"], h3b, p_i, W[("ple_w_gate", i)], W[("ple_w_proj", i)], L + "ple_fwd")
    return h4, S


def _layer_bwd(dh4, S, rope, i, ex):
    L = "L%d_" % i
    W = ex.full
    Grep = {}
    dh3, da, de = ple_bwd(dh4, S["a"], S["e"], W[("ple_w_gate", i)], L + "ple_bwd")
    ex.add_grad(("ple_w_gate", i), mm_tn(S["h3"], da, L + "d_ple_gate"))
    ex.add_grad(("ple_w_proj", i), mm_tn(S["p"], de, L + "d_ple_proj"))
    dz3, dg2, db2 = ln_bwd(dh3, S["xh3"], S["rs3"], S["ln_g"][2], L + "ln3_bwd")
    c = ex.scatter_with(i, "ffb_bwd")
    dh2, dGb, dUb, actb = ffn_bwd(dz3, S["Gb"], S["Ub"], W[("ffb_w_in", i)], W[("ffb_w_out", i)], L + "ffb_bwd",
                                  hosted=_carried(c))
    ex.done(c)
    ex.add_grad(("ffb_w_in", i), ffn_dw_in(S["h2t"], dGb, dUb, L + "d_ffb_in"))
    ex.add_grad(("ffb_w_out", i), ffn_dw_out(actb, dz3, L + "d_ffb_out"))
    dz2, dg1, db1 = ln_bwd(dh2, S["xh2"], S["rs2"], S["ln_g"][1], L + "ln2_bwd")
    docat = mm_nn(dz2, S["wo"], L + "d_ocat", b_transposed=True)
    dwo_att = jnp.concatenate([mm_tn(S["o_sb"], dz2, L + "d_wo_sb"), mm_tn(S["o_mla"], dz2, L + "d_wo_mla")], axis=0)
    dwo_att = dwo_att.reshape(-1, LANES, dwo_att.shape[1])[:, :GDN_D, :].reshape(-1, dwo_att.shape[1])
    ex.add_grad(("mix_w_o", i), jnp.concatenate([mm_tn(S["o_gdn"], dz2, L + "d_wo_gdn"), dwo_att], axis=0))
    c = ex.scatter_with(i, "gdn_chunk_bwd")
    dqkv, dgz, dgab, d_alog, d_dt, d_nw = gdn_chunk_bwd(S["qkv"], S["proj"], S["a_row"], S["dt_row"], S["nw"],
                                                        S["opre"], S["Tm"], S["Sin"], docat, L + "gdn_chunk_bwd",
                                                        hosted=_carried(c))
    ex.done(c)
    c = ex.scatter_with(i, "gdn_pre_bwd")
    dc, dconv = gdn_pre_bwd_pointwise(S["proj"], S["conv"], dqkv, L + "gdn_pre_bwd", hosted=_carried(c))
    ex.done(c)
    dxqkv = gdn_pre_bwd_conv(dc, S["conv"], L + "gdn_conv_bwd")
    ex.add_grad(("gdn_conv_w", i), dconv)
    Grep["gdn_a_log"], Grep["gdn_dt_bias"], Grep["gdn_norm_w"] = d_alog[0, :GDN_HEADS], d_dt[0, :GDN_HEADS], d_nw[0]
    dsq, dsk, dsv = sb_bwd(S["proj"], S["tot"], docat, DO_SB, L + "sb_bwd")
    c = ex.scatter_with(i, "mla_bwd")
    dQ, dK, dV = mla_bwd(S["Q"], S["K"], S["V"], S["o_mla"], S["lse"], docat, DO_MLA, L + "mla_bwd",
                         hosted=_carried(c))
    ex.done(c)
    dmq, dmkv, dwuq, dwuk, dwuv, dwq, dwkv = mla_pre_bwd(
        S["proj"], S["wq"], S["wkv"], S["wuq"], S["wuk"], S["wuv"], *rope, dQ, dK, dV, L + "mla_pre_bwd")
    g_uq, g_ukv = _mla_up_unpad(dwuq, dwuk, dwuv)
    ex.add_grad(("mla_w_uq", i), g_uq)
    ex.add_grad(("mla_w_ukv", i), g_ukv)
    Grep["mla_q_norm_w"], Grep["mla_kv_norm_w"] = dwq[0], dwkv[0]
    dproj = jnp.concatenate([dxqkv, dgz, dsq, dsk, dsv, dmq, dmkv, dgab], axis=1).astype(BF16)
    ex.add_grad(("mix_w_in", i),
                _unpad_in_proj(mm_tn(S["h1t"], dproj, L + "d_in_proj", a_transposed=True)))
    dh1 = mm_nn(dproj, S["win"], L + "d_h1", res=dz2, res_scale=ALPHA, b_transposed=True)
    dz1, dg0, db0 = ln_bwd(dh1, S["xh1"], S["rs1"], S["ln_g"][0], L + "ln1_bwd")
    c = ex.scatter_with(i, "ffa_bwd")
    dh0, dGa, dUa, acta = ffn_bwd(dz1, S["Ga"], S["Ua"], W[("ffa_w_in", i)], W[("ffa_w_out", i)], L + "ffa_bwd",
                                  hosted=_carried(c))
    ex.done(c)
    ex.add_grad(("ffa_w_out", i), ffn_dw_out(acta, dz1, L + "d_ffa_out"))
    ex.add_grad(("ln_g", i), jnp.concatenate([dg0, dg1, dg2], axis=0))
    ex.add_grad(("ln_b", i), jnp.concatenate([db0, db1, db2], axis=0))
    c = ex.scatter_with(i, "d_ffa_in")
    ex.add_grad(("ffa_w_in", i), ffn_dw_in(S["h0"].T.astype(BF16), dGa, dUa, L + "d_ffa_in", hosted=_carried(c)))
    ex.done(c)
    return dh0, Grep


def _local_step(x, p, positions, target, ex, rep):
    assert DEPTH == 2
    rope = _rope_tables(positions)
    h, saved = x, []
    for i in range(DEPTH):
        h, S = _layer_fwd(h, p[i], rope, i, ex, rep)
        saved.append(S)
    loss, dh = loss_head(h, target, "loss_head")
    grads = [None] * DEPTH
    for i in reversed(range(DEPTH)):
        dh, grads[i] = _layer_bwd(dh, saved[i], rope, i, ex)
    return loss, dh, {n: jnp.stack([grads[i][n] for i in range(DEPTH)]) for n in REPLICATED}


def kernel(x, p, positions, ffa_w_in, ffa_w_out, mix_w_in, gdn_conv_w, gdn_a_log, gdn_dt_bias, gdn_norm_w, mla_q_norm_w, mla_kv_norm_w, mla_w_uq, mla_w_ukv, mix_w_o, ffb_w_in, ffb_w_out, ln_g, ln_b, ple_w_gate, ple_w_proj, loss_target, m_ffa_w_in, m_ffa_w_out, m_mix_w_in, m_gdn_conv_w, m_gdn_a_log, m_gdn_dt_bias, m_gdn_norm_w, m_mla_q_norm_w, m_mla_kv_norm_w, m_mla_w_uq, m_mla_w_ukv, m_mix_w_o, m_ffb_w_in, m_ffb_w_out, m_ln_g, m_ln_b, m_ple_w_gate, m_ple_w_proj, v_ffa_w_in, v_ffa_w_out, v_mix_w_in, v_gdn_conv_w, v_gdn_a_log, v_gdn_dt_bias, v_gdn_norm_w, v_mla_q_norm_w, v_mla_kv_norm_w, v_mla_w_uq, v_mla_w_ukv, v_mix_w_o, v_ffb_w_in, v_ffb_w_out, v_ln_g, v_ln_b, v_ple_w_gate, v_ple_w_proj):
    given = dict(locals())
    shards = {n: given[n] for n in WEIGHTS}
    ex = Exchanges({n: shards[n] for n in SHARDED})
    ex.gather_now(GATHER_FIRST, "gather_first")
    loss, grad_x, Grep = _local_step(x[0], p[:, 0], positions[0], loss_target[0], ex, {n: shards[n] for n in REPLICATED})
    loss = lax.psum(loss[0, 0], ("x", "y", "c"))
    last = Hosted("scatter", [ex.partial[k] for k in SCATTER_LAST])
    ex.received.update(zip(SCATTER_LAST, exchange_now(last, "scatter_last")))
    rep_received = dict(zip(REPLICATED, all_gather([Grep[n] for n in REPLICATED], "gather_replicated_grads")))
    grad, delta, new_m, new_v = {}, {}, {}, {}
    for n in WEIGHTS:
        shape = shards[n].shape
        parts = [rep_received[n]] if n in REPLICATED else [ex.received[(n, l)] for l in range(DEPTH)]
        if parts[0].shape[1] % 8:
            parts = [jnp.concatenate(parts, axis=1)]
        outs = reduce_adamw(parts, _view2d(shards[n]), _view2d(given["m_" + n]), _view2d(given["v_" + n]),
                            "adamw_" + n)
        grad[n], delta[n], new_m[n], new_v[n] = (t.reshape(shape) for t in outs)
    return (loss, grad_x[None], *[grad[n] for n in WEIGHTS], *[delta[n] for n in WEIGHTS],
            *[new_m[n] for n in WEIGHTS], *[new_v[n] for n in WEIGHTS])
```

```python
import functools
import numpy as np
import jax
import jax.numpy as jnp
from jax import lax
from jax.experimental import pallas as pl
from jax.experimental.pallas import tpu as pltpu

F32 = jnp.float32
BF16 = jnp.bfloat16

DEPTH = 2
LN_EPS = 1e-5
RMS_EPS = 1e-6
ALPHA = (2 * DEPTH) ** 0.25
GDN_HEADS, GDN_D, GDN_CONV, GDN_CHUNK = 8, 64, 4, 64
SB_HEADS, SB_DIM = 4, 64
MLA_HEADS, MLA_NOPE, MLA_ROPE, MLA_V, MLA_Q_RANK, MLA_KV_RANK = 4, 64, 32, 64, 256, 128
ROPE_BASE = 10000.0
HALF_ROPE = MLA_ROPE // 2
LANES = 128
N_DEV = 8
ADAM_LR, ADAM_B1, ADAM_B2, ADAM_EPS, ADAM_WD, ADAM_STEP = 0.001, 0.9, 0.999, 1e-08, 0.01, 10

CB_GQ, CB_GK, CB_GV, CB_GZ = 0, 4, 8, 12
CB_SQ, CB_SK, CB_SV = 16, 20, 24
CB_MQ, CB_MKV, CB_GAB = 28, 30, 32
PROJ_W = 33 * LANES
GDN_W = GDN_HEADS * GDN_D
DO_SB = GDN_W // LANES
DO_MLA = DO_SB + SB_HEADS
VMEM_LIMIT = 56 * 1024 * 1024
MM_TILE = 1536

NT_DIMS = (((1,), (1,)), ((), ()))
TN_DIMS = (((0,), (0,)), ((), ()))


def _cp(sem):
    return pltpu.CompilerParams(dimension_semantics=sem, vmem_limit_bytes=VMEM_LIMIT)


def _bdot(a, b):
    return jnp.dot(a.astype(BF16), b.astype(BF16), preferred_element_type=F32)


def _bdot_nt(a, b):
    return lax.dot_general(a.astype(BF16), b.astype(BF16), NT_DIMS, preferred_element_type=F32)


def _bdot_tn(a, b):
    return lax.dot_general(a.astype(BF16), b.astype(BF16), TN_DIMS, preferred_element_type=F32)


def _split2(a):
    hi = a.astype(BF16)
    lo = (a - hi.astype(F32)).astype(BF16)
    return hi, lo


def _ones_dot_left(ones_bf16, x):
    hi = x.astype(BF16)
    r1 = x - hi.astype(F32)
    mid = r1.astype(BF16)
    lo = (r1 - mid.astype(F32)).astype(BF16)
    d = functools.partial(jnp.dot, preferred_element_type=F32)
    return d(ones_bf16, hi) + d(ones_bf16, mid) + d(ones_bf16, lo)


def _iota2(shape, dim):
    return lax.broadcasted_iota(jnp.int32, shape, dim)


def _sigmoid(x):
    return 0.5 * jnp.tanh(0.5 * x) + 0.5


def _softplus(x):
    return jnp.maximum(x, 0.0) + jnp.log(1.0 + jnp.exp(-jnp.abs(x)))


def _pick(n, limit, mult):
    if n <= limit:
        return n
    best = None
    for t in range(mult, limit + 1, mult):
        if n % t == 0:
            best = t
    assert best is not None, (n, limit, mult)
    return best


MESH = pl.DeviceIdType.MESH
ANY = pl.BlockSpec(memory_space=pl.ANY)


def _place():
    return lax.axis_index("x"), lax.axis_index("y"), lax.axis_index("c")


def _peer(k):
    x, y, c = _place()
    return (1 - x if k & 4 else x, 1 - y if k & 2 else y, 1 - c if k & 1 else c)


class Hosted:
    def __init__(self, kind, arrays):
        self.kind, self.arrays, self.n, self.results = kind, list(arrays), len(arrays), None

    def out_shapes(self):
        if self.kind == "gather":
            return [jax.ShapeDtypeStruct((N_DEV,) + a.shape, a.dtype) for a in self.arrays]
        return [jax.ShapeDtypeStruct(a.shape, a.dtype) for a in self.arrays]

    def sems(self):
        return [pltpu.SemaphoreType.DMA((self.n, N_DEV - 1)), pltpu.SemaphoreType.DMA((self.n, N_DEV - 1)),
                pltpu.SemaphoreType.DMA((self.n,))]

    def _copies(self, src_refs, dst_refs, send_sems, recv_sems, local_sems):
        x, y, c = _place()
        me = 4 * x + 2 * y + c
        local, remote = [], []
        for a in range(self.n):
            gather = self.kind == "gather"
            local.append(pltpu.make_async_copy(src_refs[a] if gather else src_refs[a].at[me], dst_refs[a].at[me],
                                               local_sems.at[a]))
            for k in range(1, N_DEV):
                px, py, pc = _peer(k)
                remote.append(pltpu.make_async_remote_copy(
                    src_ref=src_refs[a] if gather else src_refs[a].at[4 * px + 2 * py + pc], dst_ref=dst_refs[a].at[me],
                    send_sem=send_sems.at[a, k - 1], recv_sem=recv_sems.at[a, k - 1],
                    device_id=(px, py, pc), device_id_type=MESH))
        return local, remote

    def start(self, *refs):
        local, remote = self._copies(*refs)
        for cp in local + remote:
            cp.start()

    def wait(self, *refs):
        local, remote = self._copies(*refs)
        for cp in remote:
            cp.wait_recv()
        for cp in remote:
            cp.wait_send()
        for cp in local:
            cp.wait()


def _hosted_call(hosted, body, *, grid, in_specs, out_specs, out_shape, scratch_shapes=(), compiler_params, name):
    if hosted is None:
        return pl.pallas_call(body, grid=grid, in_specs=in_specs, out_specs=out_specs, out_shape=out_shape,
                              scratch_shapes=scratch_shapes, compiler_params=compiler_params, name=name)
    single = not isinstance(out_shape, (list, tuple))
    o_specs = [out_specs] if single else list(out_specs)
    o_shape = [out_shape] if single else list(out_shape)
    n_in, n_out, n_scr, n = len(in_specs), len(o_specs), len(scratch_shapes), hosted.n

    def wrapped(*refs):
        ins, c_in = refs[:n_in], refs[n_in:n_in + n]
        outs, c_out = refs[n_in + n:n_in + n + n_out], refs[n_in + n + n_out:n_in + 2 * n + n_out]
        rest = refs[n_in + 2 * n + n_out:]
        scr, sems = rest[:n_scr], rest[n_scr:]
        ids = [pl.program_id(ax) for ax in range(len(grid))]
        first = functools.reduce(jnp.logical_and, [i == 0 for i in ids])
        last = functools.reduce(jnp.logical_and, [i == g - 1 for i, g in zip(ids, grid)])

        @pl.when(first)
        def _():
            hosted.start(c_in, c_out, *sems)

        body(*ins, *outs, *scr)

        @pl.when(last)
        def _():
            hosted.wait(c_in, c_out, *sems)

    call = pl.pallas_call(
        wrapped, grid=grid, in_specs=list(in_specs) + [ANY] * n, out_specs=o_specs + [ANY] * n,
        out_shape=o_shape + hosted.out_shapes(), scratch_shapes=list(scratch_shapes) + hosted.sems(),
        compiler_params=_cp(("arbitrary",) * len(grid)), name=name)

    def run(*args):
        outs = call(*args, *hosted.arrays)
        hosted.results = list(outs[n_out:])
        return outs[0] if single else list(outs[:n_out])

    return run


def exchange_now(hosted, name):
    n = hosted.n

    def body(*refs):
        src, dst, sems = refs[:n], refs[n:2 * n], refs[2 * n:]
        hosted.start(src, dst, *sems)
        hosted.wait(src, dst, *sems)

    return pl.pallas_call(body, out_shape=hosted.out_shapes(), in_specs=[ANY] * n, out_specs=[ANY] * n,
                          scratch_shapes=hosted.sems(), name=name)(*hosted.arrays)


def mm_nn(a, b, name, out_dtype=F32, res=None, res_scale=1.0, b_transposed=False, hosted=None):
    M, K = a.shape
    N = b.shape[0] if b_transposed else b.shape[1]
    tm, tn, tk = _pick(M, 512, 16), _pick(N, MM_TILE, LANES), _pick(K, MM_TILE, LANES)
    nk = K // tk
    has_res = res is not None
    dot = _bdot_nt if b_transposed else _bdot

    def body(*refs):
        if has_res:
            a_ref, b_ref, r_ref, o_ref, acc_ref = refs
        else:
            a_ref, b_ref, o_ref, acc_ref = refs
        k = pl.program_id(2)

        @pl.when(k == 0)
        def _():
            acc_ref[...] = jnp.zeros_like(acc_ref)

        acc_ref[...] += dot(a_ref[...], b_ref[...])

        @pl.when(k == nk - 1)
        def _():
            out = acc_ref[...]
            if has_res:
                out = out + res_scale * r_ref[...]
            o_ref[...] = out.astype(o_ref.dtype)

    b_spec = pl.BlockSpec((tn, tk), lambda i, j, k: (j, k)) if b_transposed else pl.BlockSpec((tk, tn), lambda i, j, k: (k, j))
    in_specs = [pl.BlockSpec((tm, tk), lambda i, j, k: (i, k)), b_spec]
    args = [a, b]
    if has_res:
        in_specs.append(pl.BlockSpec((tm, tn), lambda i, j, k: (i, j)))
        args.append(res)
    return _hosted_call(
        hosted, body, grid=(M // tm, N // tn, nk), in_specs=in_specs,
        out_specs=pl.BlockSpec((tm, tn), lambda i, j, k: (i, j)),
        out_shape=jax.ShapeDtypeStruct((M, N), out_dtype),
        scratch_shapes=[pltpu.VMEM((tm, tn), F32)],
        compiler_params=_cp(("parallel", "parallel", "arbitrary")), name=name)(*args)


def mm_tn(a, b, name, out_dtype=F32, a_transposed=False):
    K, T = a.shape if a_transposed else a.shape[::-1]
    _, N = b.shape
    tk = K if a_transposed else _pick(K, 512, LANES)
    tn, tt = _pick(N, MM_TILE, LANES), _pick(T, 512, LANES)
    nt = T // tt

    def body(a_ref, b_ref, o_ref, acc_ref):
        t = pl.program_id(2)

        @pl.when(t == 0)
        def _():
            acc_ref[...] = jnp.zeros_like(acc_ref)

        if a_transposed:
            acc_ref[...] += _bdot(a_ref[:, pl.ds(pl.multiple_of(t * tt, tt), tt)], b_ref[...])
        else:
            acc_ref[...] += _bdot_tn(a_ref[...], b_ref[...])

        @pl.when(t == nt - 1)
        def _():
            o_ref[...] = acc_ref[...].astype(o_ref.dtype)

    a_spec = pl.BlockSpec((K, T), lambda i, j, t: (0, 0)) if a_transposed else pl.BlockSpec((tt, tk), lambda i, j, t: (t, i))
    return pl.pallas_call(
        body, grid=(K // tk, N // tn, nt),
        in_specs=[a_spec, pl.BlockSpec((tt, tn), lambda i, j, t: (t, j))],
        out_specs=pl.BlockSpec((tk, tn), lambda i, j, t: (i, j)),
        out_shape=jax.ShapeDtypeStruct((K, N), out_dtype),
        scratch_shapes=[pltpu.VMEM((tk, tn), F32)],
        compiler_params=_cp(("parallel", "parallel", "arbitrary")), name=name)(a, b)


def _ln_apply(z, g, b):
    mu = jnp.mean(z, axis=-1, keepdims=True)
    zc = z - mu
    var = jnp.mean(zc * zc, axis=-1, keepdims=True)
    rstd = lax.rsqrt(var + LN_EPS)
    xhat = zc * rstd
    return xhat * g + b, xhat, rstd


def ln_bwd(dout, xhat, rstd, g, name):
    T, D = dout.shape
    tm = _pick(T, 512, 8)

    def body(do_ref, xh_ref, rs_ref, g_ref, dz_ref, dg_ref, db_ref):
        i = pl.program_id(0)

        @pl.when(i == 0)
        def _():
            dg_ref[...] = jnp.zeros_like(dg_ref)
            db_ref[...] = jnp.zeros_like(db_ref)

        do = do_ref[...]
        xh = xh_ref[...]
        dxh = do * g_ref[...]
        m1 = jnp.mean(dxh, axis=-1, keepdims=True)
        m2 = jnp.mean(dxh * xh, axis=-1, keepdims=True)
        dz_ref[...] = rs_ref[...] * (dxh - m1 - xh * m2)
        dg_ref[...] += jnp.sum(do * xh, axis=0, keepdims=True)
        db_ref[...] += jnp.sum(do, axis=0, keepdims=True)

    row = pl.BlockSpec((tm, D), lambda i: (i, 0))
    vec = pl.BlockSpec((1, D), lambda i: (0, 0))
    return pl.pallas_call(
        body, grid=(T // tm,),
        in_specs=[row, row, pl.BlockSpec((tm, 1), lambda i: (i, 0)), vec],
        out_specs=[row, vec, vec],
        out_shape=[jax.ShapeDtypeStruct((T, D), F32), jax.ShapeDtypeStruct((1, D), F32), jax.ShapeDtypeStruct((1, D), F32)],
        compiler_params=_cp(("arbitrary",)), name=name)(dout, xhat, rstd, g)


FFN_CHUNKS = N_DEV // 2


def ffn_fwd(h, w_in, w_out, g, b, name, hosted=None):
    T, D = h.shape
    fc = w_in.shape[2]
    half = w_out.shape[1]
    tm = _pick(T, 512, 8)
    nc = FFN_CHUNKS

    def body(h_ref, wg_ref, wu_ref, wo_ref, g_ref, b_ref, out_ref, xh_ref, rs_ref, G_ref, U_ref, ob_ref, obt_ref, acc_ref):
        c = pl.program_id(1)

        @pl.when(c == 0)
        def _():
            acc_ref[...] = jnp.zeros_like(acc_ref)

        hb = h_ref[...].astype(BF16)
        G = jnp.dot(hb, wg_ref[0], preferred_element_type=F32)
        U = jnp.dot(hb, wu_ref[0], preferred_element_type=F32)
        G_ref[0] = G
        U_ref[0] = U
        act = G * _sigmoid(G) * U
        acc_ref[...] += _bdot(act, wo_ref[...].reshape(2 * half, D))

        @pl.when(c == nc - 1)
        def _():
            z = ALPHA * h_ref[...] + 0.5 * acc_ref[...]
            out, xh, rs = _ln_apply(z, g_ref[...], b_ref[...])
            out_ref[...] = out
            ob_ref[...] = out.astype(BF16)
            obt_ref[...] = out.T.astype(BF16)
            xh_ref[...] = xh
            rs_ref[...] = rs

    row = pl.BlockSpec((tm, D), lambda i, c: (i, 0))
    vec = pl.BlockSpec((1, D), lambda i, c: (0, 0))
    cblk = pl.BlockSpec((1, tm, fc), lambda i, c: (c, i, 0))
    csds = jax.ShapeDtypeStruct((nc, T, fc), F32)
    return _hosted_call(
        hosted, body, grid=(T // tm, nc),
        in_specs=[row, pl.BlockSpec((1, D, fc), lambda i, c: (c, 0, 0)),
                  pl.BlockSpec((1, D, fc), lambda i, c: (c + nc, 0, 0)),
                  pl.BlockSpec((2, half, D), lambda i, c: (c, 0, 0)), vec, vec],
        out_specs=[row, row, pl.BlockSpec((tm, 1), lambda i, c: (i, 0)), cblk, cblk, row,
                   pl.BlockSpec((D, tm), lambda i, c: (0, i))],
        out_shape=[jax.ShapeDtypeStruct((T, D), F32), jax.ShapeDtypeStruct((T, D), F32), jax.ShapeDtypeStruct((T, 1), F32),
                   csds, csds, jax.ShapeDtypeStruct((T, D), BF16), jax.ShapeDtypeStruct((D, T), BF16)],
        scratch_shapes=[pltpu.VMEM((tm, D), F32)],
        compiler_params=_cp(("parallel", "arbitrary")), name=name)(h, w_in, w_in, w_out, g, b)


def ffn_bwd(dz, G, U, w_in, w_out, name, hosted=None):
    T, D = dz.shape
    nc, _, fc = G.shape
    half = w_out.shape[1]
    tm = _pick(T, 512, 16)

    def body(dz_ref, G_ref, U_ref, wg_ref, wu_ref, wo_ref, dh_ref, dG_ref, dU_ref, act_ref, acc_ref):
        c = pl.program_id(1)

        @pl.when(c == 0)
        def _():
            acc_ref[...] = jnp.zeros_like(acc_ref)

        dy = (0.5 * dz_ref[...]).astype(BF16)
        dact = _bdot_nt(dy, wo_ref[...].reshape(2 * half, D))
        G = G_ref[0]
        U = U_ref[0]
        s = _sigmoid(G)
        silu = G * s
        dG = (dact * U * (s * (1.0 + G * (1.0 - s)))).astype(BF16)
        dU = (dact * silu).astype(BF16)
        dG_ref[0] = dG
        dU_ref[0] = dU
        act_ref[0] = (silu * U).astype(BF16)
        acc_ref[...] += _bdot_nt(dG, wg_ref[0]) + _bdot_nt(dU, wu_ref[0])

        @pl.when(c == nc - 1)
        def _():
            dh_ref[...] = ALPHA * dz_ref[...] + acc_ref[...]

    row = pl.BlockSpec((tm, D), lambda i, c: (i, 0))
    cblk = pl.BlockSpec((1, tm, fc), lambda i, c: (c, i, 0))
    csds = jax.ShapeDtypeStruct((nc, T, fc), BF16)
    return _hosted_call(
        hosted, body, grid=(T // tm, nc),
        in_specs=[row, cblk, cblk, pl.BlockSpec((1, D, fc), lambda i, c: (c, 0, 0)),
                  pl.BlockSpec((1, D, fc), lambda i, c: (c + nc, 0, 0)),
                  pl.BlockSpec((2, half, D), lambda i, c: (c, 0, 0))],
        out_specs=[row, cblk, cblk, cblk],
        out_shape=[jax.ShapeDtypeStruct((T, D), F32), csds, csds, csds],
        scratch_shapes=[pltpu.VMEM((tm, D), F32)],
        compiler_params=_cp(("parallel", "arbitrary")), name=name)(dz, G, U, w_in, w_in, w_out)


def ffn_dw_in(h_t, dG, dU, name, hosted=None):
    D, T = h_t.shape
    nc, _, fc = dG.shape
    tt = _pick(T, 512, LANES)
    nt = T // tt

    def body(h_ref, dG_ref, dU_ref, o_ref, acc_ref):
        s = pl.program_id(0)
        t = pl.program_id(1)

        @pl.when(t == 0)
        def _():
            acc_ref[...] = jnp.zeros_like(acc_ref)

        hb = h_ref[:, pl.ds(pl.multiple_of(t * tt, tt), tt)]

        @pl.when(s < nc)
        def _():
            acc_ref[...] += jnp.dot(hb, dG_ref[0], preferred_element_type=F32)

        @pl.when(s >= nc)
        def _():
            acc_ref[...] += jnp.dot(hb, dU_ref[0], preferred_element_type=F32)

        @pl.when(t == nt - 1)
        def _():
            o_ref[0] = acc_ref[...].astype(o_ref.dtype)

    return _hosted_call(
        hosted, body, grid=(2 * nc, nt),
        in_specs=[pl.BlockSpec((D, T), lambda s, t: (0, 0)),
                  pl.BlockSpec((1, tt, fc), lambda s, t: (jnp.minimum(s, nc - 1), jnp.where(s < nc, t, nt - 1), 0)),
                  pl.BlockSpec((1, tt, fc), lambda s, t: (jnp.maximum(s - nc, 0), jnp.where(s >= nc, t, 0), 0))],
        out_specs=pl.BlockSpec((1, D, fc), lambda s, t: (s, 0, 0)),
        out_shape=jax.ShapeDtypeStruct((2 * nc, D, fc), BF16),
        scratch_shapes=[pltpu.VMEM((D, fc), F32)],
        compiler_params=_cp(("parallel", "arbitrary")), name=name)(h_t, dG, dU)


def ffn_dw_out(act, dz, name, hosted=None):
    nc, T, fc = act.shape
    D = dz.shape[1]
    half = fc // 2
    tt = _pick(T, 512, 16)
    nt = T // tt

    def body(a_ref, dz_ref, o_ref, acc_ref):
        t = pl.program_id(1)

        @pl.when(t == 0)
        def _():
            acc_ref[...] = jnp.zeros_like(acc_ref)

        acc_ref[...] += _bdot_tn(a_ref[0], dz_ref[...])

        @pl.when(t == nt - 1)
        def _():
            o_ref[...] = (0.5 * acc_ref[...]).reshape(2, half, D).astype(o_ref.dtype)

    return _hosted_call(
        hosted, body, grid=(nc, nt),
        in_specs=[pl.BlockSpec((1, tt, fc), lambda c, t: (c, t, 0)), pl.BlockSpec((tt, D), lambda c, t: (t, 0))],
        out_specs=pl.BlockSpec((2, half, D), lambda c, t: (c, 0, 0)),
        out_shape=jax.ShapeDtypeStruct((2 * nc, half, D), BF16),
        scratch_shapes=[pltpu.VMEM((fc, D), F32)],
        compiler_params=_cp(("parallel", "arbitrary")), name=name)(act, dz)


def proj_res_ln(parts, w, res, g, b, name):
    T, D = res.shape
    tm = _pick(T, 512, 8)
    widths = [p.shape[1] for p in parts]
    offs = [int(sum(widths[:i])) for i in range(len(parts))]
    n = len(parts)

    def body(*refs):
        p_refs = refs[:n]
        w_ref, r_ref, g_ref, b_ref, out_ref, xh_ref, rs_ref, ob_ref, obt_ref = refs[n:]
        acc = ALPHA * r_ref[...]
        for p_ref, o, wd in zip(p_refs, offs, widths):
            acc = acc + _bdot(p_ref[...], w_ref[o:o + wd, :])
        out, xh, rs = _ln_apply(acc, g_ref[...], b_ref[...])
        out_ref[...] = out
        ob_ref[...] = out.astype(BF16)
        obt_ref[...] = out.T.astype(BF16)
        xh_ref[...] = xh
        rs_ref[...] = rs

    row = pl.BlockSpec((tm, D), lambda i: (i, 0))
    vec = pl.BlockSpec((1, D), lambda i: (0, 0))
    return pl.pallas_call(
        body, grid=(T // tm,),
        in_specs=[pl.BlockSpec((tm, wd), lambda i: (i, 0)) for wd in widths]
        + [pl.BlockSpec(w.shape, lambda i: (0, 0)), row, vec, vec],
        out_specs=[row, row, pl.BlockSpec((tm, 1), lambda i: (i, 0)), row, pl.BlockSpec((D, tm), lambda i: (0, i))],
        out_shape=[jax.ShapeDtypeStruct((T, D), F32), jax.ShapeDtypeStruct((T, D), F32), jax.ShapeDtypeStruct((T, 1), F32),
                   jax.ShapeDtypeStruct((T, D), BF16), jax.ShapeDtypeStruct((D, T), BF16)],
        compiler_params=_cp(("parallel",)), name=name)(*parts, w, res, g, b)


def ple_fwd(h, h_b, p, wg, wp, name):
    T, D = h.shape
    P = p.shape[1]
    tm, tn = _pick(T, 512, 16), D

    def body(h_ref, hn_ref, p_ref, wg_ref, wp_ref, out_ref, a_ref, e_ref, ot_ref):
        a = _bdot(h_ref[...], wg_ref[...])
        e = _bdot(p_ref[...], wp_ref[...])
        a_ref[...] = a
        e_ref[...] = e
        out = hn_ref[...] + _sigmoid(a) * e
        out_ref[...] = out
        ot_ref[...] = out.T.astype(BF16)

    blk = pl.BlockSpec((tm, tn), lambda i, j: (i, j))
    sds = jax.ShapeDtypeStruct((T, D), F32)
    return pl.pallas_call(
        body, grid=(T // tm, D // tn),
        in_specs=[pl.BlockSpec((tm, D), lambda i, j: (i, 0)), blk, pl.BlockSpec((tm, P), lambda i, j: (i, 0)),
                  pl.BlockSpec((D, tn), lambda i, j: (0, j)), pl.BlockSpec((P, tn), lambda i, j: (0, j))],
        out_specs=[blk, blk, blk, pl.BlockSpec((D, tm), lambda i, j: (0, i))],
        out_shape=[sds, sds, sds, jax.ShapeDtypeStruct((D, T), BF16)],
        compiler_params=_cp(("parallel", "parallel")), name=name)(h_b, h, p, wg, wp)


def ple_bwd(dout, a, e, wg, name):
    T, D = dout.shape
    tm = _pick(T, 512, 16)

    def body(do_ref, a_ref, e_ref, wg_ref, dh_ref, da_ref, de_ref):
        do = do_ref[...]
        s = _sigmoid(a_ref[...])
        da = (do * e_ref[...] * s * (1.0 - s)).astype(BF16)
        da_ref[...] = da
        de_ref[...] = (do * s).astype(BF16)
        dh_ref[...] = do + _bdot_nt(da, wg_ref[...])

    row = pl.BlockSpec((tm, D), lambda i: (i, 0))
    return pl.pallas_call(
        body, grid=(T // tm,),
        in_specs=[row, row, row, pl.BlockSpec((D, D), lambda i: (0, 0))],
        out_specs=[row, row, row],
        out_shape=[jax.ShapeDtypeStruct((T, D), F32), jax.ShapeDtypeStruct((T, D), BF16), jax.ShapeDtypeStruct((T, D), BF16)],
        compiler_params=_cp(("parallel",)), name=name)(dout, a, e, wg)


def loss_head(y, target, name):
    T, D = y.shape
    tm = _pick(T, 512, 8)

    def body(y_ref, t_ref, loss_ref, dy_ref):
        i = pl.program_id(0)

        @pl.when(i == 0)
        def _():
            loss_ref[...] = jnp.zeros_like(loss_ref)

        err = y_ref[...] - t_ref[...]
        dy_ref[...] = err * (1.0 / D)
        per_tok = jnp.sum(err * err, axis=-1, keepdims=True) * (1.0 / D)
        loss_ref[...] += 0.5 * jnp.sum(per_tok, axis=0, keepdims=True)

    row = pl.BlockSpec((tm, D), lambda i: (i, 0))
    return pl.pallas_call(
        body, grid=(T // tm,), in_specs=[row, row],
        out_specs=[pl.BlockSpec((1, 1), lambda i: (0, 0)), row],
        out_shape=[jax.ShapeDtypeStruct((1, 1), F32), jax.ShapeDtypeStruct((T, D), F32)],
        compiler_params=_cp(("arbitrary",)), name=name)(y, target)


HALO = 8


def _conv_taps(pad_ref, w_ref, tm, base):
    acc = w_ref[0:1, :] * pad_ref[pl.ds(base, tm), :]
    for k in range(1, GDN_CONV):
        acc = acc + w_ref[k:k + 1, :] * pad_ref[pl.ds(base + k, tm), :]
    return acc


GDN_GROUP_W = GDN_W
GDN_PRE_ROWS = 512


def _head_segments():
    head = jnp.arange(GDN_W, dtype=jnp.int32) // GDN_D
    return (head[:, None] == head[None, :]).astype(BF16)


def _head_sums(x, seg):
    hi = x.astype(BF16)
    r1 = x - hi.astype(F32)
    mid = r1.astype(BF16)
    lo = (r1 - mid.astype(F32)).astype(BF16)
    d = functools.partial(jnp.dot, preferred_element_type=F32)
    return d(hi, seg) + d(mid, seg) + d(lo, seg)


def _gdn_pre_common(x_ref, halo_ref, w_ref, seg_ref, pad_ref, tm):
    i = pl.program_id(1)
    grp = pl.program_id(0)
    pad_ref[0:HALO, :] = jnp.where(i == 0, 0.0, halo_ref[...])
    pad_ref[HALO:HALO + tm, :] = x_ref[...]
    c = _conv_taps(pad_ref, w_ref, tm, HALO - (GDN_CONV - 1))
    s = _sigmoid(c)
    y = c * s
    r = lax.rsqrt(_head_sums(y * y, seg_ref[...]) + RMS_EPS)
    scale = jnp.where(grp < 1, GDN_D ** -0.5, 1.0)
    return grp < 2, c, s, y, r, scale


def gdn_pre_fwd(proj, conv_w, name):
    T = proj.shape[0]
    tm = _pick(T, GDN_PRE_ROWS, 8)
    GW = GDN_GROUP_W

    def body(x_ref, halo_ref, w_ref, seg_ref, o_ref, pad_ref):
        normed, c, s, y, r, scale = _gdn_pre_common(x_ref, halo_ref, w_ref, seg_ref, pad_ref, tm)
        o_ref[...] = jnp.where(normed, y * r * scale, y)

    return pl.pallas_call(
        body, grid=(3, T // tm),
        in_specs=[pl.BlockSpec((tm, GW), lambda hb, i: (i, hb)),
                  pl.BlockSpec((HALO, GW), lambda hb, i: (jnp.maximum(i * (tm // HALO) - 1, 0), hb)),
                  pl.BlockSpec((GDN_CONV, GW), lambda hb, i: (0, hb)), pl.BlockSpec((GW, GW), lambda hb, i: (0, 0))],
        out_specs=pl.BlockSpec((tm, GW), lambda hb, i: (i, hb)),
        out_shape=jax.ShapeDtypeStruct((T, 3 * GW), F32),
        scratch_shapes=[pltpu.VMEM((tm + HALO, GW), F32)],
        compiler_params=_cp(("parallel", "parallel")), name=name)(proj, proj, conv_w, _head_segments())


def gdn_pre_bwd_pointwise(proj, conv_w, dqkv, name, hosted=None):
    T = proj.shape[0]
    tm = _pick(T, GDN_PRE_ROWS, 8)
    GW = GDN_GROUP_W

    def body(x_ref, halo_ref, w_ref, seg_ref, d_ref, dc_ref, dw_ref, pad_ref):
        i = pl.program_id(1)
        normed, c, s, y, r, scale = _gdn_pre_common(x_ref, halo_ref, w_ref, seg_ref, pad_ref, tm)

        @pl.when(i == 0)
        def _():
            dw_ref[...] = jnp.zeros_like(dw_ref)

        d = d_ref[...]
        n = y * r
        dn = d * scale
        dy = jnp.where(normed, r * (dn - n * _head_sums(dn * n, seg_ref[...])), d)
        dc = dy * (s * (1.0 + c * (1.0 - s)))
        dc_ref[...] = dc
        for k in range(GDN_CONV):
            xs = pad_ref[pl.ds(HALO - (GDN_CONV - 1) + k, tm), :]
            dw_ref[k:k + 1, :] += jnp.sum(dc * xs, axis=0, keepdims=True)

    blk = pl.BlockSpec((tm, GW), lambda hb, i: (i, hb))
    wblk = pl.BlockSpec((GDN_CONV, GW), lambda hb, i: (0, hb))
    return _hosted_call(
        hosted, body, grid=(3, T // tm),
        in_specs=[blk, pl.BlockSpec((HALO, GW), lambda hb, i: (jnp.maximum(i * (tm // HALO) - 1, 0), hb)), wblk,
                  pl.BlockSpec((GW, GW), lambda hb, i: (0, 0)), blk],
        out_specs=[blk, wblk],
        out_shape=[jax.ShapeDtypeStruct((T, 3 * GW), F32), jax.ShapeDtypeStruct((GDN_CONV, 3 * GW), F32)],
        scratch_shapes=[pltpu.VMEM((tm + HALO, GW), F32)],
        compiler_params=_cp(("parallel", "arbitrary")), name=name)(proj, proj, conv_w, _head_segments(), dqkv)


def gdn_pre_bwd_conv(dc, conv_w_p, name):
    T = dc.shape[0]
    tm = _pick(T, GDN_PRE_ROWS, 8)
    nt = T // tm
    GW = GDN_GROUP_W

    def body(dc_ref, halo_ref, w_ref, dx_ref, pad_ref):
        i = pl.program_id(1)
        pad_ref[0:tm, :] = dc_ref[...]
        pad_ref[tm:tm + HALO, :] = jnp.where(i == nt - 1, 0.0, halo_ref[...])
        acc = w_ref[GDN_CONV - 1:GDN_CONV, :] * pad_ref[pl.ds(0, tm), :]
        for k in range(GDN_CONV - 1):
            acc = acc + w_ref[k:k + 1, :] * pad_ref[pl.ds(GDN_CONV - 1 - k, tm), :]
        dx_ref[...] = acc

    blk = pl.BlockSpec((tm, GW), lambda hb, i: (i, hb))
    return pl.pallas_call(
        body, grid=(3, nt),
        in_specs=[blk, pl.BlockSpec((HALO, GW), lambda hb, i: (jnp.minimum((i + 1) * (tm // HALO), T // HALO - 1), hb)),
                  pl.BlockSpec((GDN_CONV, GW), lambda hb, i: (0, hb))],
        out_specs=blk,
        out_shape=jax.ShapeDtypeStruct((T, 3 * GW), F32),
        scratch_shapes=[pltpu.VMEM((tm + HALO, GW), F32)],
        compiler_params=_cp(("parallel", "parallel")), name=name)(dc, dc, conv_w_p)


def _chunk_masks(C):
    row = _iota2((C, C), 0)
    col = _iota2((C, C), 1)
    return row >= col, row > col, row == col


GDN_FWD_CHUNKS = 4
BNN = (((2,), (1,)), ((0,), (0,)))
BNT = (((2,), (2,)), ((0,), (0,)))
BTN = (((1,), (1,)), ((0,), (0,)))


def _bmm(a, b, dims=BNN):
    return lax.dot_general(a.astype(BF16), b.astype(BF16), dims, preferred_element_type=F32)


def _hbmm(a, b):
    m = a.shape[1]
    a_hi, a_lo = _split2(a)
    b_hi, b_lo = _split2(b)
    r = lax.dot_general(jnp.concatenate([a_hi, a_lo], axis=1), b_hi, BNN, preferred_element_type=F32)
    return r[:, :m] + r[:, m:] + lax.dot_general(a_hi, b_lo, BNN, preferred_element_type=F32)


def _hbmm_tn(a, b):
    a_hi, a_lo = _split2(a)
    b_hi, b_lo = _split2(b)
    d = functools.partial(lax.dot_general, dimension_numbers=BTN, preferred_element_type=F32)
    return d(a_hi, b_hi) + d(a_lo, b_hi) + d(a_hi, b_lo)


def _col_to_row(colv, eye):
    return jnp.sum(jnp.where(eye, colv, 0.0), axis=1, keepdims=True)


def _row_to_col(rowv, eye):
    return jnp.sum(jnp.where(eye, rowv, 0.0), axis=2, keepdims=True)


def _unit_lower_inverse(A, eye):
    C = A.shape[1]
    P = jnp.where(eye, 1.0, 0.0) - A
    Bp = _hbmm(A, A)
    for _ in range(4):
        R = _hbmm(jnp.concatenate([Bp, P], axis=1), Bp)
        Bp = R[:, :C]
        P = P + R[:, C:]
    return P + _hbmm(P, Bp)


def _stack_heads(ref, first_head, n, row0=0):
    rows = pl.ds(row0, GDN_CHUNK)
    return jnp.stack([ref[rows, pl.ds((first_head + h) * GDN_D, GDN_D)] for h in range(n)])


def _unstack_heads(ref, first_head, val, row0=0):
    rows = pl.ds(row0, GDN_CHUNK)
    for h in range(val.shape[0]):
        ref[rows, pl.ds((first_head + h) * GDN_D, GDN_D)] = val[h]


def _gdn_gates(gab, a_row, dt_row, incl):
    g_all = -jnp.exp(a_row) * _softplus(gab + dt_row)
    beta_all = _sigmoid(gab)
    gc_all = _ones_dot_left(incl.astype(BF16), g_all)
    return g_all, beta_all, gc_all


def _gdn_common(qkv_ref, gc_all, beta_all, incl, strict, eye, row0=0):
    C, H = GDN_CHUNK, GDN_HEADS
    q, k, v = (_stack_heads(qkv_ref, j * H, H, row0) for j in range(3))
    gc = jnp.stack([gc_all[:, h:h + 1] for h in range(H)])
    beta = jnp.stack([beta_all[:, H + h:H + h + 1] for h in range(H)])
    gc_row = _col_to_row(gc, eye)
    decay = jnp.where(incl, jnp.exp(jnp.where(incl, gc - gc_row, 0.0)), 0.0)
    e_gc = jnp.exp(gc)
    gl = gc[:, C - 1:C, :]
    e_gl = jnp.exp(gl)
    ekd = jnp.exp(gl - gc)
    kb = k * beta
    A = jnp.where(strict, _bmm(kb, k, BNT) * decay, 0.0)
    Pm = jnp.where(incl, _bmm(q, k, BNT) * decay, 0.0)
    return q, k, v, gc, beta, decay, e_gc, e_gl, ekd, kb, A, Pm


def gdn_chunk_fwd(qkv, proj, a_row, dt_row, norm_w, name, hosted=None):
    T = qkv.shape[0]
    C, H, Dh = GDN_CHUNK, GDN_HEADS, GDN_D
    N = T // C
    J = GDN_FWD_CHUNKS if N % GDN_FWD_CHUNKS == 0 else 1

    def body(qkv_ref, gz_ref, gab_ref, a_ref, dt_ref, nw_ref, o_ref, opre_ref, Tm_ref, Sin_ref, S_ref):
        n = pl.program_id(0)

        @pl.when(n == 0)
        def _():
            S_ref[...] = jnp.zeros_like(S_ref)

        incl, strict, eye = _chunk_masks(C)
        gab = gab_ref[...]
        per_chunk = []
        for j in range(J):
            _, beta_all, gc_all = _gdn_gates(gab[j * C:(j + 1) * C], a_ref[...], dt_ref[...], incl)
            per_chunk.append(_gdn_common(qkv_ref, gc_all, beta_all, incl, strict, eye, row0=j * C))
        q, k, v, gc, beta, decay, e_gc, e_gl, ekd, kb, A, Pm = (jnp.concatenate(t, axis=0) for t in zip(*per_chunk))
        Tm = _unit_lower_inverse(A, eye)
        u = _hbmm(Tm, v * beta)
        w = _hbmm(Tm, kb * e_gc)
        qd = q * e_gc
        kd = k * ekd
        S = S_ref[...]
        for j in range(J):
            hs = slice(j * H, (j + 1) * H)
            v_new = u[hs] - _bmm(w[hs], S)
            o = _bmm(qd[hs], S) + _bmm(Pm[hs], v_new)
            Sin_ref[j] = S
            Tm_ref[j] = Tm[hs]
            S = S * e_gl[hs] + _bmm(kd[hs], v_new, BTN)
            r = lax.rsqrt(jnp.mean(o * o, axis=-1, keepdims=True) + RMS_EPS)
            gz = _stack_heads(gz_ref, 0, H, j * C)
            _unstack_heads(opre_ref, 0, o, j * C)
            _unstack_heads(o_ref, 0, o * r * nw_ref[...] * (gz * _sigmoid(gz)), j * C)
        S_ref[...] = S

    vec = pl.BlockSpec((1, LANES), lambda n: (0, 0))
    hblk = pl.BlockSpec((J * C, GDN_W), lambda n: (n, 0))
    sblk = pl.BlockSpec((J, H, Dh, Dh), lambda n: (n, 0, 0, 0))
    return _hosted_call(
        hosted, body, grid=(N // J,),
        in_specs=[pl.BlockSpec((J * C, 3 * GDN_W), lambda n: (n, 0)),
                  pl.BlockSpec((J * C, GDN_W), lambda n: (n, CB_GZ * LANES // GDN_W)),
                  pl.BlockSpec((J * C, LANES), lambda n: (n, CB_GAB)), vec, vec, pl.BlockSpec((1, Dh), lambda n: (0, 0))],
        out_specs=[hblk, hblk, sblk, sblk],
        out_shape=[jax.ShapeDtypeStruct((T, GDN_W), F32), jax.ShapeDtypeStruct((T, GDN_W), F32)]
        + [jax.ShapeDtypeStruct((N, H, Dh, Dh), F32)] * 2,
        scratch_shapes=[pltpu.VMEM((H, Dh, Dh), F32)],
        compiler_params=_cp(("arbitrary",)), name=name)(qkv, proj, proj, a_row, dt_row, norm_w)


def gdn_chunk_bwd(qkv, proj, a_row, dt_row, norm_w, opre, Tm_all, Sin_all, docat, name, hosted=None):
    T = qkv.shape[0]
    C, H, Dh = GDN_CHUNK, GDN_HEADS, GDN_D
    N = T // C

    def body(qkv_ref, gz_ref, gab_ref, a_ref, dt_ref, nw_ref, opre_ref, Tm_ref, Sin_ref, do_ref,
             dqkv_ref, dgz_ref, dgab_ref, da_ref, ddt_ref, dnw_ref, dS_ref):
        n = pl.program_id(0)

        @pl.when(n == 0)
        def _():
            dS_ref[...] = jnp.zeros_like(dS_ref)
            da_ref[...] = jnp.zeros_like(da_ref)
            ddt_ref[...] = jnp.zeros_like(ddt_ref)
            dnw_ref[...] = jnp.zeros_like(dnw_ref)

        incl, strict, eye = _chunk_masks(C)
        gab = gab_ref[...]
        g_all, beta_all, gc_all = _gdn_gates(gab, a_ref[...], dt_ref[...], incl)
        lane = _iota2((C, LANES), 1)
        rowi = _iota2((C, 1), 0)
        nw = nw_ref[...]
        q, k, v, gc, beta, decay, e_gc, e_gl, ekd, kb, A, Pm = _gdn_common(qkv_ref, gc_all, beta_all, incl, strict, eye)
        Tm = Tm_ref[0]
        S = Sin_ref[0]
        dS = dS_ref[...]
        kbe = kb * e_gc
        u = _hbmm(Tm, v * beta)
        w = _hbmm(Tm, kbe)
        qd = q * e_gc
        kd = k * ekd
        v_new = u - _bmm(w, S)
        o = _stack_heads(opre_ref, 0, H)
        gz = _stack_heads(gz_ref, 0, H)
        don = _stack_heads(do_ref, 0, H)
        r = lax.rsqrt(jnp.mean(o * o, axis=-1, keepdims=True) + RMS_EPS)
        nn = o * r
        sgz = _sigmoid(gz)
        silu = gz * sgz
        _unstack_heads(dgz_ref, 0, don * nn * nw * (sgz * (1.0 + gz * (1.0 - sgz))))
        dnn = don * nw * silu
        dnw_ref[...] += jnp.sum(jnp.sum(don * nn * silu, axis=0), axis=0, keepdims=True)
        do = r * (dnn - nn * jnp.mean(dnn * nn, axis=-1, keepdims=True))
        dv_new = _bmm(Pm, do, BTN) + _bmm(kd, dS)
        dPm = jnp.where(incl, _bmm(do, v_new, BNT), 0.0)
        dqd = _bmm(do, S, BNT)
        dkd = _bmm(v_new, dS, BNT)
        dS_ref[...] = _bmm(qd, do, BTN) + e_gl * dS - _bmm(w, dv_new, BTN)
        dgl = jnp.sum(jnp.sum(dS * S, axis=2, keepdims=True), axis=1, keepdims=True) * e_gl
        dw = -_bmm(dv_new, S, BNT)
        dvb = _hbmm_tn(Tm, dv_new)
        dkbe = _hbmm_tn(Tm, dw)
        dA = -jnp.where(strict, _bmm(dvb, u, BNT) + _bmm(dkbe, w, BNT), 0.0)
        dAD = dA * decay
        dPD = dPm * decay
        Gm = dA * A + dPm * Pm
        dgc = jnp.sum(Gm, axis=2, keepdims=True) - _row_to_col(jnp.sum(Gm, axis=1, keepdims=True), eye)
        dkb = _bmm(dAD, k) + dkbe * e_gc
        dk = _bmm(dAD, kb, BTN) + _bmm(dPD, q, BTN) + dkd * ekd + dkb * beta
        dq = _bmm(dPD, k) + dqd * e_gc
        tkd = jnp.sum(dkd * kd, axis=-1, keepdims=True)
        dgc = dgc + jnp.sum(dqd * qd, axis=-1, keepdims=True) - tkd + jnp.sum(dkbe * kbe, axis=-1, keepdims=True)
        dgl = dgl + jnp.sum(tkd, axis=1, keepdims=True)
        dgc = dgc + jnp.where(rowi == C - 1, dgl, 0.0)
        dbeta = jnp.sum(dvb * v, axis=-1, keepdims=True) + jnp.sum(dkb * k, axis=-1, keepdims=True)
        _unstack_heads(dqkv_ref, 0, dq)
        _unstack_heads(dqkv_ref, H, dk)
        _unstack_heads(dqkv_ref, 2 * H, dvb * beta)
        dgc_all = jnp.zeros((C, LANES), F32)
        dbeta_all = jnp.zeros((C, LANES), F32)
        for h in range(H):
            dgc_all = dgc_all + jnp.where(lane == h, dgc[h], 0.0)
            dbeta_all = dbeta_all + jnp.where(lane == H + h, dbeta[h], 0.0)
        upper = (_iota2((C, C), 0) <= _iota2((C, C), 1)).astype(BF16)
        dg_all = _ones_dot_left(upper, dgc_all)
        dga = dg_all * (-jnp.exp(a_ref[...])) * _sigmoid(gab + dt_ref[...])
        dgb = dbeta_all * beta_all * (1.0 - beta_all)
        dgab_ref[...] = jnp.where(lane < H, dga, jnp.where(lane < 2 * H, dgb, 0.0))
        da_ref[...] += jnp.sum(jnp.where(lane < H, dg_all * g_all, 0.0), axis=0, keepdims=True)
        ddt_ref[...] += jnp.sum(jnp.where(lane < H, dga, 0.0), axis=0, keepdims=True)

    rev = lambda n: N - 1 - n
    vec = pl.BlockSpec((1, LANES), lambda n: (0, 0))
    nwv = pl.BlockSpec((1, Dh), lambda n: (0, 0))
    hblk = pl.BlockSpec((C, GDN_W), lambda n: (rev(n), 0))
    sblk = pl.BlockSpec((1, H, Dh, Dh), lambda n: (rev(n), 0, 0, 0))
    qblk = pl.BlockSpec((C, 3 * GDN_W), lambda n: (rev(n), 0))
    return _hosted_call(
        hosted, body, grid=(N,),
        in_specs=[qblk, pl.BlockSpec((C, GDN_W), lambda n: (rev(n), CB_GZ * LANES // GDN_W)),
                  pl.BlockSpec((C, LANES), lambda n: (rev(n), CB_GAB)), vec, vec, nwv, hblk, sblk, sblk, hblk],
        out_specs=[qblk, hblk, pl.BlockSpec((C, LANES), lambda n: (rev(n), 0)), vec, vec, nwv],
        out_shape=[jax.ShapeDtypeStruct((T, 3 * GDN_W), F32), jax.ShapeDtypeStruct((T, GDN_W), F32),
                   jax.ShapeDtypeStruct((T, LANES), F32), jax.ShapeDtypeStruct((1, LANES), F32),
                   jax.ShapeDtypeStruct((1, LANES), F32), jax.ShapeDtypeStruct((1, Dh), F32)],
        scratch_shapes=[pltpu.VMEM((H, Dh, Dh), F32)],
        compiler_params=_cp(("arbitrary",)), name=name)(qkv, proj, proj, a_row, dt_row, norm_w, opre, Tm_all, Sin_all, docat)


ATT_BQ, ATT_BK = 512, 1024
NEG_BIG = -1e30


def _att_blocks(T):
    bq, bk = min(ATT_BQ, T), min(ATT_BK, T)
    assert bk % bq == 0 and T % bk == 0
    return bq, bk


def _att_specs(T, bq, cbs):
    qspec = lambda cb: pl.BlockSpec((bq, LANES), lambda h, i: (i, cb + h))
    kspec = lambda cb: pl.BlockSpec((T, LANES), lambda h, i: (0, cb + h))
    return qspec, kspec


def _kblock(ref, kb, bk):
    return ref[pl.ds(pl.multiple_of(kb * bk, bk), bk), :]


def _att_pos(i, kb, bq, bk):
    qpos = i * bq + _iota2((bq, bk), 0)
    kpos = kb * bk + _iota2((bq, bk), 1)
    return qpos, kpos


def _later_keys(n):
    return (_iota2((n, n), 0) > _iota2((n, n), 1)).astype(BF16)


def _earlier_keys(n):
    return (_iota2((n, n), 0) < _iota2((n, n), 1)).astype(BF16)


def _tri_dot(x, tri, terms):
    acc, rest = None, x
    for t in range(terms):
        part = rest.astype(BF16)
        if t + 1 < terms:
            rest = rest - part.astype(F32)
        d = jnp.dot(part, tri, preferred_element_type=F32)
        acc = d if acc is None else acc + d
    return acc


SB_BLOCK = 256
SB_DEAD = -104.0


def _sb_blocks(T):
    b = min(SB_BLOCK, T)
    assert T % b == 0 and T // b <= LANES
    return b, b


def sb_fwd(proj, name, hosted=None):
    T = proj.shape[0]
    H = SB_HEADS
    bq, bk = _sb_blocks(T)
    scale = SB_DIM ** -0.5

    def body(q_ref, k_ref, v_ref, o_ref, tot_ref):
        i = pl.program_id(1)
        qb = q_ref[...].astype(BF16)
        diag = (i * bq) // bk
        lane = _iota2((bq, LANES), 1)
        later = _later_keys(bk)

        def block(kb, acc, R, masked):
            z = _bdot_nt(qb, _kblock(k_ref, kb, bk)) * scale
            sp = _softplus(z)
            if masked:
                qpos, kpos = _att_pos(i, kb, bq, bk)
                mask = kpos < qpos
                l1m = jnp.where(mask, -sp, 0.0)
            else:
                l1m = -sp
            W = jnp.exp((z - sp) + _tri_dot(l1m, later, 3) + R)
            if masked:
                W = jnp.where(mask, W, 0.0)
            acc = acc + _bdot(W, _kblock(v_ref, kb, bk))
            return acc, R + jnp.sum(l1m, axis=-1, keepdims=True)

        acc, R = block(diag, jnp.zeros((bq, LANES), F32), jnp.zeros((bq, 1), F32), True)

        def live(c):
            return jnp.logical_and(c[0] >= 0, jnp.max(c[2]) > SB_DEAD)

        def step(c):
            kb, acc, R, Rb = c
            acc, R_next = block(kb, acc, R, False)
            return kb - 1, acc, R_next, jnp.where(lane == kb, R, Rb)

        _, acc, _, Rb = lax.while_loop(live, step, (diag - 1, acc, R, jnp.where(lane == diag, 0.0, NEG_BIG)))
        o_ref[...] = acc
        tot_ref[...] = Rb

    qspec, kspec = _att_specs(T, bq, None)
    sds = jax.ShapeDtypeStruct((T, H * LANES), F32)
    oblk = pl.BlockSpec((bq, LANES), lambda h, i: (i, h))
    return _hosted_call(
        hosted, body, grid=(H, T // bq), in_specs=[qspec(CB_SQ), kspec(CB_SK), kspec(CB_SV)],
        out_specs=[oblk, oblk], out_shape=[sds, sds],
        compiler_params=_cp(("parallel", "parallel")), name=name)(proj, proj, proj)


def sb_bwd(proj, tot, docat, do_cb, name):
    T = proj.shape[0]
    H = SB_HEADS
    bq, bk = _sb_blocks(T)
    scale = SB_DIM ** -0.5

    def body(q_ref, k_ref, v_ref, tot_ref, do_ref, dq_ref, dk_ref, dv_ref):
        i = pl.program_id(1)

        @pl.when(i == 0)
        def _():
            dk_ref[...] = jnp.zeros_like(dk_ref)
            dv_ref[...] = jnp.zeros_like(dv_ref)

        qb = q_ref[...].astype(BF16)
        dob = do_ref[...].astype(BF16)
        Rb = tot_ref[...]
        diag = (i * bq) // bk
        lane = _iota2((bq, LANES), 1)
        later, earlier = _later_keys(bk), _earlier_keys(bk)
        first = lax.while_loop(
            lambda kb: jnp.logical_and(kb < diag, jnp.max(jnp.where(lane == kb, Rb, NEG_BIG)) <= SB_DEAD),
            lambda kb: kb + 1, jnp.int32(0))

        def block(kb, carry, masked):
            dq, Epre = carry
            R = jnp.sum(jnp.where(lane == kb, Rb, 0.0), axis=1, keepdims=True)
            kblk = _kblock(k_ref, kb, bk).astype(BF16)
            z = _bdot_nt(qb, kblk) * scale
            sp = _softplus(z)
            if masked:
                qpos, kpos = _att_pos(i, kb, bq, bk)
                mask = kpos < qpos
                l1m = jnp.where(mask, -sp, 0.0)
            else:
                l1m = -sp
            W = jnp.exp((z - sp) + _tri_dot(l1m, later, 3) + R)
            if masked:
                W = jnp.where(mask, W, 0.0)
            E = _bdot_nt(dob, _kblock(v_ref, kb, bk)) * W
            cexcl = _tri_dot(E, earlier, 3) + Epre
            neg = jnp.exp(-sp)
            dz = E * neg - cexcl * (1.0 - neg)
            if masked:
                dz = jnp.where(mask, dz, 0.0)
            dz = (dz * scale).astype(BF16)
            rows = pl.ds(pl.multiple_of(kb * bk, bk), bk)
            dk_ref[rows, :] += lax.dot_general(dz, qb, TN_DIMS, preferred_element_type=F32)
            dv_ref[rows, :] += lax.dot_general(W.astype(BF16), dob, TN_DIMS, preferred_element_type=F32)
            dq = dq + jnp.dot(dz, kblk, preferred_element_type=F32)
            return dq, Epre + jnp.sum(E, axis=-1, keepdims=True)

        init = (jnp.zeros((bq, LANES), F32), jnp.zeros((bq, 1), F32))
        carry = lax.fori_loop(first, diag, lambda kb, c: block(kb, c, False), init)
        dq, _ = block(diag, carry, True)
        dq_ref[...] = dq

    qspec, kspec = _att_specs(T, bq, None)
    sds = jax.ShapeDtypeStruct((T, H * LANES), F32)
    oblk = pl.BlockSpec((bq, LANES), lambda h, i: (i, h))
    kout = pl.BlockSpec((T, LANES), lambda h, i: (0, h))
    return pl.pallas_call(
        body, grid=(H, T // bq),
        in_specs=[qspec(CB_SQ), kspec(CB_SK), kspec(CB_SV), oblk, qspec(do_cb)],
        out_specs=[oblk, kout, kout], out_shape=[sds, sds, sds],
        compiler_params=_cp(("arbitrary", "arbitrary")), name=name)(proj, proj, proj, tot, docat)


def mla_fwd(Q, K, V, name, hosted=None):
    T = Q.shape[0]
    H = MLA_HEADS
    bq, bk = _att_blocks(T)
    scale = (MLA_NOPE + MLA_ROPE) ** -0.5

    def body(q_ref, k_ref, v_ref, o_ref, lse_ref):
        i = pl.program_id(1)
        qb = q_ref[...]
        diag = (i * bq) // bk

        def block(kb, carry, masked):
            acc, m, l = carry
            s = _bdot_nt(qb, _kblock(k_ref, kb, bk)) * scale
            if masked:
                qpos, kpos = _att_pos(i, kb, bq, bk)
                s = jnp.where(kpos <= qpos, s, NEG_BIG)
            m_new = jnp.maximum(m, jnp.max(s, axis=-1, keepdims=True))
            p = jnp.exp(s - m_new)
            corr = jnp.exp(m - m_new)
            acc = corr * acc + _bdot(p, _kblock(v_ref, kb, bk))
            return acc, m_new, corr * l + jnp.sum(p, axis=-1, keepdims=True)

        init = (jnp.zeros((bq, LANES), F32), jnp.full((bq, 1), NEG_BIG, F32), jnp.zeros((bq, 1), F32))
        carry = lax.fori_loop(0, diag, lambda kb, c: block(kb, c, False), init)
        acc, m, l = block(diag, carry, True)
        o_ref[...] = acc / l
        lse_ref[...] = jnp.broadcast_to(m + jnp.log(l), (bq, LANES))

    qspec, kspec = _att_specs(T, bq, None)
    sds = jax.ShapeDtypeStruct((T, H * LANES), F32)
    oblk = pl.BlockSpec((bq, LANES), lambda h, i: (i, h))
    return _hosted_call(
        hosted, body, grid=(H, T // bq), in_specs=[qspec(0), kspec(0), kspec(0)],
        out_specs=[oblk, oblk], out_shape=[sds, sds],
        compiler_params=_cp(("parallel", "parallel")), name=name)(Q, K, V)


def mla_bwd(Q, K, V, o, lse, docat, do_cb, name, hosted=None):
    T = Q.shape[0]
    H = MLA_HEADS
    bq, bk = _att_blocks(T)
    scale = (MLA_NOPE + MLA_ROPE) ** -0.5

    def body(q_ref, k_ref, v_ref, o_ref, lse_ref, do_ref, dq_ref, dk_ref, dv_ref):
        i = pl.program_id(1)

        @pl.when(i == 0)
        def _():
            dk_ref[...] = jnp.zeros_like(dk_ref)
            dv_ref[...] = jnp.zeros_like(dv_ref)

        qb = q_ref[...]
        do = do_ref[...]
        dob = do.astype(BF16)
        delta = jnp.sum(do * o_ref[...], axis=-1, keepdims=True)
        lse = lse_ref[:, 0:1]

        diag = (i * bq) // bk

        def block(kb, dq, masked):
            kblk = _kblock(k_ref, kb, bk)
            s = _bdot_nt(qb, kblk) * scale
            if masked:
                qpos, kpos = _att_pos(i, kb, bq, bk)
                s = jnp.where(kpos <= qpos, s, NEG_BIG)
            p = jnp.exp(s - lse)
            dp = _bdot_nt(dob, _kblock(v_ref, kb, bk))
            ds = (p * (dp - delta) * scale).astype(BF16)
            rows = pl.ds(pl.multiple_of(kb * bk, bk), bk)
            dk_ref[rows, :] += lax.dot_general(ds, qb, TN_DIMS, preferred_element_type=F32)
            dv_ref[rows, :] += lax.dot_general(p.astype(BF16), dob, TN_DIMS, preferred_element_type=F32)
            return dq + jnp.dot(ds, kblk, preferred_element_type=F32)

        dq = lax.fori_loop(0, diag, lambda kb, c: block(kb, c, False), jnp.zeros((bq, LANES), F32))
        dq_ref[...] = block(diag, dq, True)

    qspec, kspec = _att_specs(T, bq, None)
    sds = jax.ShapeDtypeStruct((T, H * LANES), F32)
    oblk = pl.BlockSpec((bq, LANES), lambda h, i: (i, h))
    kout = pl.BlockSpec((T, LANES), lambda h, i: (0, h))
    return _hosted_call(
        hosted, body, grid=(H, T // bq),
        in_specs=[qspec(0), kspec(0), kspec(0), oblk, oblk, qspec(do_cb)],
        out_specs=[oblk, kout, kout], out_shape=[sds, sds, sds],
        compiler_params=_cp(("arbitrary", "arbitrary")), name=name)(Q, K, V, o, lse, docat)


def _tile_heads(t, n):
    return jnp.concatenate([t] * n, axis=1)


def _rope(X, C, Sn, Sp):
    n = X.shape[1]
    return X * C + pltpu.roll(X, n - HALF_ROPE, 1) * Sn + pltpu.roll(X, HALF_ROPE, 1) * Sp


def _rope_t(dO, C, Sn, Sp):
    n = dO.shape[1]
    return dO * C + pltpu.roll(dO * Sn, HALF_ROPE, 1) + pltpu.roll(dO * Sp, n - HALF_ROPE, 1)


def _rms(x, w):
    r = lax.rsqrt(jnp.mean(x * x, axis=-1, keepdims=True) + RMS_EPS)
    xh = x * r
    return r, xh, xh * w


def _rms_bwd(dn, w, r, xh):
    dxh = dn * w
    return r * (dxh - xh * jnp.mean(dxh * xh, axis=-1, keepdims=True)), jnp.sum(dn * xh, axis=0, keepdims=True)


def _mla_pre_specs(T, tm):
    KV = MLA_KV_RANK
    QR = MLA_Q_RANK
    W = MLA_HEADS * LANES
    full = lambda shape: pl.BlockSpec(shape, lambda i: (0, 0))
    specs = [pl.BlockSpec((tm, QR), lambda i: (i, CB_MQ * LANES // QR)),
             pl.BlockSpec((tm, 2 * LANES), lambda i: (i, CB_MKV // 2)),
             full((1, QR)), full((1, KV))]
    rope = [pl.BlockSpec((tm, LANES), lambda i: (i, 0))] * 3
    return specs, rope, full, W


def mla_pre_fwd(proj, wq, wkv, wuq, wuk, wuv, ropeC, ropeSn, ropeSp, name):
    T = proj.shape[0]
    tm = _pick(T, 512, 16)
    KV = MLA_KV_RANK
    H = MLA_HEADS

    def body(mq_ref, mkv_ref, wq_ref, wkv_ref, wuq_ref, wuk_ref, wuv_ref, c_ref, sn_ref, sp_ref, Q_ref, K_ref, V_ref):
        C, Sn, Sp = (_tile_heads(t[...], H) for t in (c_ref, sn_ref, sp_ref))
        _, _, qn = _rms(mq_ref[...], wq_ref[...])
        Q_ref[...] = _rope(_bdot(qn, wuq_ref[...]), C, Sn, Sp).astype(BF16)
        mkv = mkv_ref[...]
        _, _, kvn = _rms(mkv[:, :KV], wkv_ref[...])
        kr = pltpu.roll(mkv[:, KV:], MLA_NOPE, 1)
        K_ref[...] = _rope(_bdot(kvn, wuk_ref[...]) + _tile_heads(kr, H), C, Sn, Sp).astype(BF16)
        V_ref[...] = _bdot(kvn, wuv_ref[...]).astype(BF16)

    specs, rope, full, W = _mla_pre_specs(T, tm)
    oblk = pl.BlockSpec((tm, W), lambda i: (i, 0))
    sds = jax.ShapeDtypeStruct((T, W), BF16)
    return pl.pallas_call(
        body, grid=(T // tm,),
        in_specs=specs + [full(wuq.shape), full(wuk.shape), full(wuv.shape)] + rope,
        out_specs=[oblk, oblk, oblk], out_shape=[sds, sds, sds],
        compiler_params=_cp(("parallel",)), name=name)(proj, proj, wq, wkv, wuq, wuk, wuv, ropeC, ropeSn, ropeSp)


def mla_pre_bwd(proj, wq, wkv, wuq, wuk, wuv, ropeC, ropeSn, ropeSp, dQ, dK, dV, name):
    T = proj.shape[0]
    tm = _pick(T, 512, 16)
    KV = MLA_KV_RANK
    H = MLA_HEADS

    def body(mq_ref, mkv_ref, wq_ref, wkv_ref, wuq_ref, wuk_ref, wuv_ref,
             c_ref, sn_ref, sp_ref, dQ_ref, dK_ref, dV_ref,
             dmq_ref, dmkv_ref, dwuq_ref, dwuk_ref, dwuv_ref, dwq_ref, dwkv_ref):
        i = pl.program_id(0)

        @pl.when(i == 0)
        def _():
            for ref in (dwuq_ref, dwuk_ref, dwuv_ref, dwq_ref, dwkv_ref):
                ref[...] = jnp.zeros_like(ref)

        C, Sn, Sp = (_tile_heads(t[...], H) for t in (c_ref, sn_ref, sp_ref))
        rq, xq, qn = _rms(mq_ref[...], wq_ref[...])
        mkv = mkv_ref[...]
        rkv, xkv, kvn = _rms(mkv[:, :KV], wkv_ref[...])
        dqf = _rope_t(dQ_ref[...], C, Sn, Sp)
        dkf = _rope_t(dK_ref[...], C, Sn, Sp)
        dv = dV_ref[...]
        dwuq_ref[...] += _bdot_tn(qn, dqf)
        dwuk_ref[...] += _bdot_tn(kvn, dkf)
        dwuv_ref[...] += _bdot_tn(kvn, dv)
        dmq, dwq = _rms_bwd(_bdot_nt(dqf, wuq_ref[...]), wq_ref[...], rq, xq)
        dckv, dwkv = _rms_bwd(_bdot_nt(dkf, wuk_ref[...]) + _bdot_nt(dv, wuv_ref[...]), wkv_ref[...], rkv, xkv)
        dwq_ref[...] += dwq
        dwkv_ref[...] += dwkv
        dmq_ref[...] = dmq
        dkr = dkf[:, 0:LANES]
        for h in range(1, H):
            dkr = dkr + dkf[:, h * LANES:(h + 1) * LANES]
        dkr = pltpu.roll(dkr, LANES - MLA_NOPE, 1)
        dkr = jnp.where(_iota2(dkr.shape, 1) < MLA_ROPE, dkr, 0.0)
        dmkv_ref[...] = jnp.concatenate([dckv, dkr], axis=1)

    specs, rope, full, W = _mla_pre_specs(T, tm)
    wide = pl.BlockSpec((tm, W), lambda i: (i, 0))
    return pl.pallas_call(
        body, grid=(T // tm,),
        in_specs=specs + [full(w.shape) for w in (wuq, wuk, wuv)] + rope + [wide, wide, wide],
        out_specs=[pl.BlockSpec((tm, MLA_Q_RANK), lambda i: (i, 0)), pl.BlockSpec((tm, 2 * LANES), lambda i: (i, 0)),
                   full(wuq.shape), full(wuk.shape), full(wuv.shape), full((1, MLA_Q_RANK)), full((1, KV))],
        out_shape=[jax.ShapeDtypeStruct((T, MLA_Q_RANK), F32), jax.ShapeDtypeStruct((T, 2 * LANES), F32),
                   jax.ShapeDtypeStruct(wuq.shape, F32), jax.ShapeDtypeStruct(wuk.shape, F32),
                   jax.ShapeDtypeStruct(wuv.shape, F32), jax.ShapeDtypeStruct((1, MLA_Q_RANK), F32),
                   jax.ShapeDtypeStruct((1, KV), F32)],
        compiler_params=_cp(("arbitrary",)), name=name)(
            proj, proj, wq, wkv, wuq, wuk, wuv, ropeC, ropeSn, ropeSp, dQ, dK, dV)


def all_gather(shards, name):
    n = len(shards)

    def body(*refs):
        x_refs, out_refs = refs[:n], refs[n:2 * n]
        send_sems, recv_sems, local_sems = refs[2 * n:]
        x, y, c = _place()
        me, sibling = (x, y, c), (x, y, 1 - c)
        chips = [(1 - x, y), (x, 1 - y), (1 - x, 1 - y)]

        def slot(a, px, py, pc):
            return out_refs[a].at[4 * px + 2 * py + pc]

        def copy(a, k, block, to, src=None):
            return pltpu.make_async_remote_copy(
                src_ref=slot(a, *block) if src is None else src, dst_ref=slot(a, *block),
                send_sem=send_sems.at[a, k], recv_sem=recv_sems.at[a, k], device_id=to, device_id_type=MESH)

        mine = [pltpu.make_async_copy(x_refs[a], slot(a, *me), local_sems.at[a]) for a in range(n)]
        first = []
        for a in range(n):
            mine[a].start()
            first.append(copy(a, 0, me, sibling, src=x_refs[a]))
            first += [copy(a, 1 + j, me, (*chip, c), src=x_refs[a]) for j, chip in enumerate(chips)]
        for cp in first:
            cp.start()
        passed = []
        for j, chip in enumerate(chips):
            for a in range(n):
                copy(a, 1 + j, (*chip, c), me).wait_recv()
                passed.append(copy(a, 4 + j, (*chip, c), sibling))
                passed[-1].start()
        for a in range(n):
            copy(a, 0, sibling, me).wait_recv()
            for j, chip in enumerate(chips):
                copy(a, 4 + j, (*chip, 1 - c), me).wait_recv()
        for cp in first + passed:
            cp.wait_send()
        for cp in mine:
            cp.wait()

    return pl.pallas_call(
        body, out_shape=[jax.ShapeDtypeStruct((N_DEV,) + s.shape, s.dtype) for s in shards],
        in_specs=[ANY] * n, out_specs=[ANY] * n,
        scratch_shapes=[pltpu.SemaphoreType.DMA((n, 7)), pltpu.SemaphoreType.DMA((n, 7)), pltpu.SemaphoreType.DMA((n,))],
        name=name)(*shards)


def reduce_adamw(parts, w, m, v, name):
    L = len(parts)
    n, Rl, C = parts[0].shape
    R = w.shape[0]
    assert R == L * Rl
    tr = Rl if Rl * C <= 256 * 1024 else _pick(Rl, 256, 16)
    nr = Rl // tr

    def body(*refs):
        p_refs = refs[:L]
        w_ref, m_ref, v_ref, g_ref, d_ref, nm_ref, nv_ref, sum_ref = refs[L:]
        grp = pl.program_id(0)
        for j in range(L):
            @pl.when(grp == j)
            def _(j=j):
                acc = p_refs[j][0].astype(F32)
                for s in range(1, n):
                    acc = acc + p_refs[j][s].astype(F32)
                sum_ref[...] = acc

        g_ = sum_ref[...]
        m_ = ADAM_B1 * m_ref[...] + (1.0 - ADAM_B1) * g_
        v_ = ADAM_B2 * v_ref[...] + (1.0 - ADAM_B2) * (g_ * g_)
        m_hat = m_ / (1.0 - ADAM_B1 ** ADAM_STEP)
        v_hat = v_ / (1.0 - ADAM_B2 ** ADAM_STEP)
        g_ref[...] = g_
        d_ref[...] = -ADAM_LR * (m_hat / (jnp.sqrt(v_hat) + ADAM_EPS) + ADAM_WD * w_ref[...])
        nm_ref[...] = m_
        nv_ref[...] = v_

    blk = pl.BlockSpec((tr, C), lambda l, r: (l * nr + r, 0))
    sds = jax.ShapeDtypeStruct((R, C), F32)
    p_specs = [pl.BlockSpec((n, tr, C), lambda l, r, j=j: (0, jnp.where(l == j, r, 0), 0)) for j in range(L)]
    return pl.pallas_call(
        body, grid=(L, nr), in_specs=p_specs + [blk] * 3,
        out_specs=[blk] * 4, out_shape=[sds] * 4, scratch_shapes=[pltpu.VMEM((tr, C), F32)],
        compiler_params=_cp(("arbitrary", "arbitrary")), name=name)(*parts, w, m, v)


SHARDED = {"ffa_w_in": (2, BF16), "ffa_w_out": (1, BF16), "mix_w_in": (2, BF16), "mla_w_uq": (2, BF16),
           "mla_w_ukv": (2, BF16), "mix_w_o": (1, BF16), "ffb_w_in": (2, BF16), "ffb_w_out": (1, BF16),
           "ple_w_gate": (1, BF16), "ple_w_proj": (2, BF16), "gdn_conv_w": (2, F32), "ln_g": (2, F32), "ln_b": (2, F32)}
FFN_SLOT = ("ffa_w_in", "ffa_w_out", "ffb_w_in", "ffb_w_out")
REPLICATED = ("gdn_a_log", "gdn_dt_bias", "gdn_norm_w", "mla_q_norm_w", "mla_kv_norm_w")
WEIGHTS = ("ffa_w_in", "ffa_w_out", "mix_w_in", "gdn_conv_w", "gdn_a_log", "gdn_dt_bias", "gdn_norm_w", "mla_q_norm_w",
           "mla_kv_norm_w", "mla_w_uq", "mla_w_ukv", "mix_w_o", "ffb_w_in", "ffb_w_out", "ln_g", "ln_b", "ple_w_gate",
           "ple_w_proj")


def _to_slots(full, axis):
    L, a, b = full.shape
    if axis == 2:
        return full.reshape(L, a, N_DEV, b // N_DEV).transpose(2, 0, 1, 3).reshape(N_DEV, L * a, b // N_DEV)
    return full.reshape(L, N_DEV, a // N_DEV, b).transpose(1, 0, 2, 3).reshape(N_DEV, L * a // N_DEV, b)


def _from_slots(slots, shard_shape, axis):
    L, a, b = shard_shape
    t = slots.reshape((N_DEV,) + tuple(shard_shape))
    if axis == 2:
        return t.transpose(1, 2, 0, 3).reshape(L, a, N_DEV * b)
    return t.transpose(1, 0, 2, 3).reshape(L, N_DEV * a, b)


def _view2d(t):
    return t.reshape(-1, t.shape[-1])


def _pad_heads(w, nh):
    K = w.shape[0]
    return jnp.pad(w.reshape(K, nh, GDN_D), ((0, 0), (0, 0), (0, LANES - GDN_D))).reshape(K, nh * LANES)


def _unpad_heads(w, nh):
    K = w.shape[0]
    return w.reshape(K, nh, LANES)[:, :, :GDN_D].reshape(K, nh * GDN_D)


IN_WIDTHS = (512, 512, 512, 512, 8, 8, 256, 256, 256, 256, 160)


def _split_in(w):
    offs = np.cumsum((0,) + IN_WIDTHS)
    return [w[:, int(offs[i]):int(offs[i + 1])] for i in range(len(IN_WIDTHS))]


def _pad_in_proj(w):
    gq, gk, gv, gz, ga, gb, sq, sk, sv, mq, mkv = _split_in(w)
    gab = jnp.pad(jnp.concatenate([ga, gb], axis=1), ((0, 0), (0, LANES - 2 * GDN_HEADS)))
    return jnp.concatenate(
        [gq, gk, gv, gz] + [_pad_heads(t, SB_HEADS) for t in (sq, sk, sv)]
        + [mq, jnp.pad(mkv, ((0, 0), (0, 2 * LANES - mkv.shape[1]))), gab], axis=1)


def _unpad_in_proj(wp):
    c = lambda cb, n: wp[:, cb * LANES:(cb + n) * LANES]
    gab = c(CB_GAB, 1)
    parts = [c(cb, DO_SB) for cb in (CB_GQ, CB_GK, CB_GV, CB_GZ)]
    parts += [gab[:, :GDN_HEADS], gab[:, GDN_HEADS:2 * GDN_HEADS]]
    parts += [_unpad_heads(c(cb, SB_HEADS), SB_HEADS) for cb in (CB_SQ, CB_SK, CB_SV)]
    parts += [c(CB_MQ, 2), c(CB_MKV, 2)[:, :MLA_KV_RANK + MLA_ROPE]]
    return jnp.concatenate(parts, axis=1)


def _pad_lanes(w, width):
    return jnp.pad(w, ((0, 0), (0, width - w.shape[1])))


def _mla_up_pad(w_uq, w_ukv):
    H = MLA_HEADS
    dq = MLA_NOPE + MLA_ROPE
    wuq = jnp.pad(w_uq.reshape(-1, H, dq), ((0, 0), (0, 0), (0, LANES - dq))).reshape(-1, H * LANES)
    kv = w_ukv.reshape(-1, H, MLA_NOPE + MLA_V)
    wuk = jnp.pad(kv[:, :, :MLA_NOPE], ((0, 0), (0, 0), (0, LANES - MLA_NOPE))).reshape(-1, H * LANES)
    wuv = jnp.pad(kv[:, :, MLA_NOPE:], ((0, 0), (0, 0), (0, LANES - MLA_V))).reshape(-1, H * LANES)
    return wuq, wuk, wuv


def _mla_up_unpad(dwuq, dwuk, dwuv):
    H = MLA_HEADS
    dq = MLA_NOPE + MLA_ROPE
    g_uq = dwuq.reshape(-1, H, LANES)[:, :, :dq].reshape(-1, H * dq)
    g_ukv = jnp.concatenate([dwuk.reshape(-1, H, LANES)[:, :, :MLA_NOPE], dwuv.reshape(-1, H, LANES)[:, :, :MLA_V]],
                            axis=2).reshape(-1, H * (MLA_NOPE + MLA_V))
    return g_uq, g_ukv


def _rope_tables(positions):
    inv = 1.0 / (ROPE_BASE ** (jnp.arange(0, MLA_ROPE, 2, dtype=F32) / MLA_ROPE))
    ang = positions.astype(F32)[:, None] * inv
    cos, sin = jnp.cos(ang), jnp.sin(ang)
    T = positions.shape[0]
    one = lambda n: jnp.ones((T, n), F32)
    zero = lambda n: jnp.zeros((T, n), F32)
    tail = LANES - MLA_NOPE - MLA_ROPE
    C = jnp.concatenate([one(MLA_NOPE), cos, cos, one(tail)], axis=1)
    Sn = jnp.concatenate([zero(MLA_NOPE), -sin, zero(HALF_ROPE + tail)], axis=1)
    Sp = jnp.concatenate([zero(MLA_NOPE + HALF_ROPE), sin, zero(tail)], axis=1)
    return C, Sn, Sp


GATHER_FIRST = [("ffa_w_in", 0), ("ffa_w_out", 0)] + [(n, l) for l in range(DEPTH) for n in ("gdn_conv_w", "ln_g", "ln_b")]
GATHER_PLAN = {
    (0, "ffa_fwd"): [("mix_w_in", 0), ("mla_w_uq", 0), ("mla_w_ukv", 0), ("mix_w_o", 0)],
    (0, "in_proj"): [("ple_w_gate", 0), ("ple_w_proj", 0)],
    (0, "gdn_chunk_fwd"): [("ffb_w_in", 0)],
    (0, "sb_fwd"): [("ffb_w_out", 0), ("mix_w_o", 1)],
    (0, "mla_fwd"): [("ffa_w_out", 1)],
    (0, "ffb_fwd"): [("ffa_w_in", 1)],
    (1, "ffa_fwd"): [("mix_w_in", 1)],
    (1, "in_proj"): [("mla_w_uq", 1), ("mla_w_ukv", 1)],
    (1, "gdn_chunk_fwd"): [("ffb_w_in", 1)],
    (1, "sb_fwd"): [("ffb_w_out", 1), ("ple_w_gate", 1), ("ple_w_proj", 1)],
}
SCATTER_PLAN = {
    (1, "gdn_chunk_bwd"): [("ffb_w_in", 1)],
    (1, "gdn_pre_bwd"): [("ffb_w_out", 1), ("ple_w_gate", 1), ("ple_w_proj", 1), ("mix_w_o", 1)],
    (1, "ffa_bwd"): [("mix_w_in", 1), ("mla_w_uq", 1), ("mla_w_ukv", 1), ("gdn_conv_w", 1)],
    (0, "ffb_bwd"): [("ffa_w_in", 1)],
    (0, "gdn_chunk_bwd"): [("ffb_w_in", 0)],
    (0, "gdn_pre_bwd"): [("ffb_w_out", 0), ("ple_w_gate", 0), ("ple_w_proj", 0), ("mix_w_o", 0)],
    (0, "mla_bwd"): [("ffa_w_out", 1), ("ln_g", 1), ("ln_b", 1)],
    (0, "ffa_bwd"): [("mix_w_in", 0), ("mla_w_uq", 0), ("mla_w_ukv", 0), ("gdn_conv_w", 0)],
    (0, "d_ffa_in"): [("ffa_w_out", 0), ("ln_g", 0), ("ln_b", 0)],
}
SCATTER_LAST = [("ffa_w_in", 0)]


class Exchanges:
    def __init__(self, shards):
        self.shards = shards
        self.full = {}
        self.partial = {}
        self.received = {}

    def _block(self, key):
        n, l = key
        return self.shards[n][l].astype(SHARDED[n][1])

    def _absorb_gather(self, keys, results):
        for (n, l), g in zip(keys, results):
            blk = self.shards[n][l]
            self.full[(n, l)] = g if n in FFN_SLOT else _from_slots(g, (1,) + blk.shape, SHARDED[n][0])[0]

    def gather_now(self, keys, name):
        self._absorb_gather(keys, all_gather([self._block(k) for k in keys], name))

    def gather_with(self, layer, tag):
        keys = GATHER_PLAN.get((layer, tag))
        return None if keys is None else (keys, Hosted("gather", [self._block(k) for k in keys]))

    def scatter_with(self, layer, tag):
        keys = SCATTER_PLAN.get((layer, tag))
        return None if keys is None else (keys, Hosted("scatter", [self.partial[k] for k in keys]))

    def done(self, carried):
        if carried is not None:
            keys, hosted = carried
            if hosted.kind == "gather":
                self._absorb_gather(keys, hosted.results)
            else:
                self.received.update(zip(keys, hosted.results))

    def add_grad(self, key, g):
        n, l = key
        self.partial[key] = g if n in FFN_SLOT else _to_slots(g[None], SHARDED[n][0]).astype(SHARDED[n][1])


def _carried(c):
    return None if c is None else c[1]


def _layer_fwd(h0, h0t, p_i, rope, i, ex, rep):
    L = "L%d_" % i
    S = {"h0": h0, "h0t": h0t, "p": p_i}
    W = ex.full
    ln_g = [W[("ln_g", i)][j][None, :] for j in range(3)]
    ln_b = [W[("ln_b", i)][j][None, :] for j in range(3)]
    S["ln_g"] = ln_g
    c = ex.gather_with(i, "ffa_fwd")
    S["h1"], S["xh1"], S["rs1"], S["Ga"], S["Ua"], S["h1b"], S["h1t"] = ffn_fwd(
        h0, W[("ffa_w_in", i)], W[("ffa_w_out", i)], ln_g[0], ln_b[0], L + "ffa_fwd", hosted=_carried(c))
    ex.done(c)
    S["win"] = _pad_in_proj(W[("mix_w_in", i)])
    c = ex.gather_with(i, "in_proj")
    S["proj"] = mm_nn(S["h1b"], S["win"], L + "in_proj", hosted=_carried(c))
    ex.done(c)
    S["conv"] = W[("gdn_conv_w", i)]
    S["a_row"] = _pad_lanes(rep["gdn_a_log"][i][None, :], LANES)
    S["dt_row"] = _pad_lanes(rep["gdn_dt_bias"][i][None, :], LANES)
    S["nw"] = rep["gdn_norm_w"][i][None, :]
    S["wq"] = rep["mla_q_norm_w"][i][None, :]
    S["wkv"] = rep["mla_kv_norm_w"][i][None, :]
    S["qkv"] = gdn_pre_fwd(S["proj"], S["conv"], L + "gdn_pre_fwd")
    c = ex.gather_with(i, "gdn_chunk_fwd")
    S["o_gdn"], S["opre"], S["Tm"], S["Sin"] = gdn_chunk_fwd(
        S["qkv"], S["proj"], S["a_row"], S["dt_row"], S["nw"], L + "gdn_chunk_fwd", hosted=_carried(c))
    ex.done(c)
    c = ex.gather_with(i, "sb_fwd")
    S["o_sb"], S["tot"] = sb_fwd(S["proj"], L + "sb_fwd", hosted=_carried(c))
    ex.done(c)
    S["wuq"], S["wuk"], S["wuv"] = _mla_up_pad(W[("mla_w_uq", i)], W[("mla_w_ukv", i)])
    S["Q"], S["K"], S["V"] = mla_pre_fwd(S["proj"], S["wq"], S["wkv"], S["wuq"], S["wuk"], S["wuv"], *rope, L + "mla_pre_fwd")
    c = ex.gather_with(i, "mla_fwd")
    S["o_mla"], S["lse"] = mla_fwd(S["Q"], S["K"], S["V"], L + "mla_fwd", hosted=_carried(c))
    ex.done(c)
    wo = W[("mix_w_o", i)]
    wo_att = wo[GDN_W:].reshape(-1, GDN_D, wo.shape[1])
    S["wo"] = jnp.concatenate(
        [wo[:GDN_W], jnp.pad(wo_att, ((0, 0), (0, LANES - GDN_D), (0, 0))).reshape(-1, wo.shape[1])], axis=0)
    S["h2"], S["xh2"], S["rs2"], _, S["h2t"] = proj_res_ln([S["o_gdn"], S["o_sb"], S["o_mla"]], S["wo"], S["h1"],
                                                        ln_g[1], ln_b[1], L + "out_proj")
    c = ex.gather_with(i, "ffb_fwd")
    S["h3"], S["xh3"], S["rs3"], S["Gb"], S["Ub"], h3b, _ = ffn_fwd(
        S["h2"], W[("ffb_w_in", i)], W[("ffb_w_out", i)], ln_g[2], ln_b[2], L + "ffb_fwd", hosted=_carried(c))
    ex.done(c)
    h4, S["a"], S["e"], h4t = ple_fwd(S["h3"], h3b, p_i, W[("ple_w_gate", i)], W[("ple_w_proj", i)], L + "ple_fwd")
    return h4, h4t, S


def _layer_bwd(dh4, S, rope, i, ex):
    L = "L%d_" % i
    W = ex.full
    Grep = {}
    dh3, da, de = ple_bwd(dh4, S["a"], S["e"], W[("ple_w_gate", i)], L + "ple_bwd")
    ex.add_grad(("ple_w_gate", i), mm_tn(S["h3"], da, L + "d_ple_gate"))
    ex.add_grad(("ple_w_proj", i), mm_tn(S["p"], de, L + "d_ple_proj"))
    dz3, dg2, db2 = ln_bwd(dh3, S["xh3"], S["rs3"], S["ln_g"][2], L + "ln3_bwd")
    c = ex.scatter_with(i, "ffb_bwd")
    dh2, dGb, dUb, actb = ffn_bwd(dz3, S["Gb"], S["Ub"], W[("ffb_w_in", i)], W[("ffb_w_out", i)], L + "ffb_bwd",
                                  hosted=_carried(c))
    ex.done(c)
    ex.add_grad(("ffb_w_in", i), ffn_dw_in(S["h2t"], dGb, dUb, L + "d_ffb_in"))
    ex.add_grad(("ffb_w_out", i), ffn_dw_out(actb, dz3, L + "d_ffb_out"))
    dz2, dg1, db1 = ln_bwd(dh2, S["xh2"], S["rs2"], S["ln_g"][1], L + "ln2_bwd")
    docat = mm_nn(dz2, S["wo"], L + "d_ocat", b_transposed=True)
    dwo_att = jnp.concatenate([mm_tn(S["o_sb"], dz2, L + "d_wo_sb"), mm_tn(S["o_mla"], dz2, L + "d_wo_mla")], axis=0)
    dwo_att = dwo_att.reshape(-1, LANES, dwo_att.shape[1])[:, :GDN_D, :].reshape(-1, dwo_att.shape[1])
    ex.add_grad(("mix_w_o", i), jnp.concatenate([mm_tn(S["o_gdn"], dz2, L + "d_wo_gdn"), dwo_att], axis=0))
    c = ex.scatter_with(i, "gdn_chunk_bwd")
    dqkv, dgz, dgab, d_alog, d_dt, d_nw = gdn_chunk_bwd(S["qkv"], S["proj"], S["a_row"], S["dt_row"], S["nw"],
                                                        S["opre"], S["Tm"], S["Sin"], docat, L + "gdn_chunk_bwd",
                                                        hosted=_carried(c))
    ex.done(c)
    c = ex.scatter_with(i, "gdn_pre_bwd")
    dc, dconv = gdn_pre_bwd_pointwise(S["proj"], S["conv"], dqkv, L + "gdn_pre_bwd", hosted=_carried(c))
    ex.done(c)
    dxqkv = gdn_pre_bwd_conv(dc, S["conv"], L + "gdn_conv_bwd")
    ex.add_grad(("gdn_conv_w", i), dconv)
    Grep["gdn_a_log"], Grep["gdn_dt_bias"], Grep["gdn_norm_w"] = d_alog[0, :GDN_HEADS], d_dt[0, :GDN_HEADS], d_nw[0]
    dsq, dsk, dsv = sb_bwd(S["proj"], S["tot"], docat, DO_SB, L + "sb_bwd")
    c = ex.scatter_with(i, "mla_bwd")
    dQ, dK, dV = mla_bwd(S["Q"], S["K"], S["V"], S["o_mla"], S["lse"], docat, DO_MLA, L + "mla_bwd",
                         hosted=_carried(c))
    ex.done(c)
    dmq, dmkv, dwuq, dwuk, dwuv, dwq, dwkv = mla_pre_bwd(
        S["proj"], S["wq"], S["wkv"], S["wuq"], S["wuk"], S["wuv"], *rope, dQ, dK, dV, L + "mla_pre_bwd")
    g_uq, g_ukv = _mla_up_unpad(dwuq, dwuk, dwuv)
    ex.add_grad(("mla_w_uq", i), g_uq)
    ex.add_grad(("mla_w_ukv", i), g_ukv)
    Grep["mla_q_norm_w"], Grep["mla_kv_norm_w"] = dwq[0], dwkv[0]
    dproj = jnp.concatenate([dxqkv, dgz, dsq, dsk, dsv, dmq, dmkv, dgab], axis=1).astype(BF16)
    ex.add_grad(("mix_w_in", i),
                _unpad_in_proj(mm_tn(S["h1t"], dproj, L + "d_in_proj", a_transposed=True)))
    dh1 = mm_nn(dproj, S["win"], L + "d_h1", res=dz2, res_scale=ALPHA, b_transposed=True)
    dz1, dg0, db0 = ln_bwd(dh1, S["xh1"], S["rs1"], S["ln_g"][0], L + "ln1_bwd")
    c = ex.scatter_with(i, "ffa_bwd")
    dh0, dGa, dUa, acta = ffn_bwd(dz1, S["Ga"], S["Ua"], W[("ffa_w_in", i)], W[("ffa_w_out", i)], L + "ffa_bwd",
                                  hosted=_carried(c))
    ex.done(c)
    ex.add_grad(("ffa_w_out", i), ffn_dw_out(acta, dz1, L + "d_ffa_out"))
    ex.add_grad(("ln_g", i), jnp.concatenate([dg0, dg1, dg2], axis=0))
    ex.add_grad(("ln_b", i), jnp.concatenate([db0, db1, db2], axis=0))
    c = ex.scatter_with(i, "d_ffa_in")
    ex.add_grad(("ffa_w_in", i), ffn_dw_in(S["h0t"], dGa, dUa, L + "d_ffa_in", hosted=_carried(c)))
    ex.done(c)
    return dh0, Grep


def _local_step(x, p, positions, target, ex, rep):
    assert DEPTH == 2
    rope = _rope_tables(positions)
    h, ht, saved = x, x.T.astype(BF16), []
    for i in range(DEPTH):
        h, ht, S = _layer_fwd(h, ht, p[i], rope, i, ex, rep)
        saved.append(S)
    loss, dh = loss_head(h, target, "loss_head")
    grads = [None] * DEPTH
    for i in reversed(range(DEPTH)):
        dh, grads[i] = _layer_bwd(dh, saved[i], rope, i, ex)
    return loss, dh, {n: jnp.stack([grads[i][n] for i in range(DEPTH)]) for n in REPLICATED}


def kernel(x, p, positions, ffa_w_in, ffa_w_out, mix_w_in, gdn_conv_w, gdn_a_log, gdn_dt_bias, gdn_norm_w, mla_q_norm_w, mla_kv_norm_w, mla_w_uq, mla_w_ukv, mix_w_o, ffb_w_in, ffb_w_out, ln_g, ln_b, ple_w_gate, ple_w_proj, loss_target, m_ffa_w_in, m_ffa_w_out, m_mix_w_in, m_gdn_conv_w, m_gdn_a_log, m_gdn_dt_bias, m_gdn_norm_w, m_mla_q_norm_w, m_mla_kv_norm_w, m_mla_w_uq, m_mla_w_ukv, m_mix_w_o, m_ffb_w_in, m_ffb_w_out, m_ln_g, m_ln_b, m_ple_w_gate, m_ple_w_proj, v_ffa_w_in, v_ffa_w_out, v_mix_w_in, v_gdn_conv_w, v_gdn_a_log, v_gdn_dt_bias, v_gdn_norm_w, v_mla_q_norm_w, v_mla_kv_norm_w, v_mla_w_uq, v_mla_w_ukv, v_mix_w_o, v_ffb_w_in, v_ffb_w_out, v_ln_g, v_ln_b, v_ple_w_gate, v_ple_w_proj):
    given = dict(locals())
    shards = {n: given[n] for n in WEIGHTS}
    ex = Exchanges({n: shards[n] for n in SHARDED})
    ex.gather_now(GATHER_FIRST, "gather_first")
    loss, grad_x, Grep = _local_step(x[0], p[:, 0], positions[0], loss_target[0], ex, {n: shards[n] for n in REPLICATED})
    loss = lax.psum(loss[0, 0], ("x", "y", "c"))
    last = Hosted("scatter", [ex.partial[k] for k in SCATTER_LAST])
    ex.received.update(zip(SCATTER_LAST, exchange_now(last, "scatter_last")))
    rep_received = dict(zip(REPLICATED, all_gather([Grep[n] for n in REPLICATED], "gather_replicated_grads")))
    grad, delta, new_m, new_v = {}, {}, {}, {}
    for n in WEIGHTS:
        shape = shards[n].shape
        parts = [rep_received[n]] if n in REPLICATED else [ex.received[(n, l)] for l in range(DEPTH)]
        if parts[0].shape[1] % 8:
            parts = [jnp.concatenate(parts, axis=1)]
        outs = reduce_adamw(parts, _view2d(shards[n]), _view2d(given["m_" + n]), _view2d(given["v_" + n]),
                            "adamw_" + n)
        grad[n], delta[n], new_m[n], new_v[n] = (t.reshape(shape) for t in outs)
    return (loss, grad_x[None], *[grad[n] for n in WEIGHTS], *[delta[n] for n in WEIGHTS],
            *[new_m[n] for n in WEIGHTS], *[new_v[n] for n in WEIGHTS])
```

```python
import functools
import numpy as np
import jax
import jax.numpy as jnp
from jax import lax
from jax.experimental import pallas as pl
from jax.experimental.pallas import tpu as pltpu

F32 = jnp.float32
BF16 = jnp.bfloat16

DEPTH = 2
LN_EPS = 1e-5
RMS_EPS = 1e-6
ALPHA = (2 * DEPTH) ** 0.25
GDN_HEADS, GDN_D, GDN_CONV, GDN_CHUNK = 8, 64, 4, 64
SB_HEADS, SB_DIM = 4, 64
MLA_HEADS, MLA_NOPE, MLA_ROPE, MLA_V, MLA_Q_RANK, MLA_KV_RANK = 4, 64, 32, 64, 256, 128
ROPE_BASE = 10000.0
HALF_ROPE = MLA_ROPE // 2
LANES = 128
N_DEV = 8
ADAM_LR, ADAM_B1, ADAM_B2, ADAM_EPS, ADAM_WD, ADAM_STEP = 0.001, 0.9, 0.999, 1e-08, 0.01, 10

CB_GQ, CB_GK, CB_GV, CB_GZ = 0, 4, 8, 12
CB_SQ, CB_SK, CB_SV = 16, 20, 24
CB_MQ, CB_MKV, CB_GAB = 28, 30, 32
PROJ_W = 33 * LANES
GDN_W = GDN_HEADS * GDN_D
DO_SB = GDN_W // LANES
DO_MLA = DO_SB + SB_HEADS
VMEM_LIMIT = 56 * 1024 * 1024
MM_TILE = 1536

NT_DIMS = (((1,), (1,)), ((), ()))
TN_DIMS = (((0,), (0,)), ((), ()))


def _cp(sem):
    return pltpu.CompilerParams(dimension_semantics=sem, vmem_limit_bytes=VMEM_LIMIT)


def _bdot(a, b):
    return jnp.dot(a.astype(BF16), b.astype(BF16), preferred_element_type=F32)


def _bdot_nt(a, b):
    return lax.dot_general(a.astype(BF16), b.astype(BF16), NT_DIMS, preferred_element_type=F32)


def _bdot_tn(a, b):
    return lax.dot_general(a.astype(BF16), b.astype(BF16), TN_DIMS, preferred_element_type=F32)


def _split2(a):
    hi = a.astype(BF16)
    lo = (a - hi.astype(F32)).astype(BF16)
    return hi, lo


def _ones_dot_left(ones_bf16, x):
    hi = x.astype(BF16)
    r1 = x - hi.astype(F32)
    mid = r1.astype(BF16)
    lo = (r1 - mid.astype(F32)).astype(BF16)
    d = functools.partial(jnp.dot, preferred_element_type=F32)
    return d(ones_bf16, hi) + d(ones_bf16, mid) + d(ones_bf16, lo)


def _iota2(shape, dim):
    return lax.broadcasted_iota(jnp.int32, shape, dim)


def _sigmoid(x):
    return 0.5 * jnp.tanh(0.5 * x) + 0.5


def _softplus(x):
    return jnp.maximum(x, 0.0) + jnp.log(1.0 + jnp.exp(-jnp.abs(x)))


def _pick(n, limit, mult):
    if n <= limit:
        return n
    best = None
    for t in range(mult, limit + 1, mult):
        if n % t == 0:
            best = t
    assert best is not None, (n, limit, mult)
    return best


MESH = pl.DeviceIdType.MESH
ANY = pl.BlockSpec(memory_space=pl.ANY)


def _place():
    return lax.axis_index("x"), lax.axis_index("y"), lax.axis_index("c")


def _peer(k):
    x, y, c = _place()
    return (1 - x if k & 4 else x, 1 - y if k & 2 else y, 1 - c if k & 1 else c)


class Hosted:
    def __init__(self, kind, arrays):
        self.kind, self.arrays, self.n, self.results = kind, list(arrays), len(arrays), None

    def out_shapes(self):
        if self.kind == "gather":
            return [jax.ShapeDtypeStruct((N_DEV,) + a.shape, a.dtype) for a in self.arrays]
        return [jax.ShapeDtypeStruct(a.shape, a.dtype) for a in self.arrays]

    def sems(self):
        return [pltpu.SemaphoreType.DMA((self.n, N_DEV - 1)), pltpu.SemaphoreType.DMA((self.n, N_DEV - 1)),
                pltpu.SemaphoreType.DMA((self.n,))]

    def _copies(self, src_refs, dst_refs, send_sems, recv_sems, local_sems):
        x, y, c = _place()
        me = 4 * x + 2 * y + c
        local, remote = [], []
        for a in range(self.n):
            gather = self.kind == "gather"
            local.append(pltpu.make_async_copy(src_refs[a] if gather else src_refs[a].at[me], dst_refs[a].at[me],
                                               local_sems.at[a]))
            for k in range(1, N_DEV):
                px, py, pc = _peer(k)
                remote.append(pltpu.make_async_remote_copy(
                    src_ref=src_refs[a] if gather else src_refs[a].at[4 * px + 2 * py + pc], dst_ref=dst_refs[a].at[me],
                    send_sem=send_sems.at[a, k - 1], recv_sem=recv_sems.at[a, k - 1],
                    device_id=(px, py, pc), device_id_type=MESH))
        return local, remote

    def start(self, *refs):
        local, remote = self._copies(*refs)
        for cp in local + remote:
            cp.start()

    def wait(self, *refs):
        local, remote = self._copies(*refs)
        for cp in remote:
            cp.wait_recv()
        for cp in remote:
            cp.wait_send()
        for cp in local:
            cp.wait()


def _hosted_call(hosted, body, *, grid, in_specs, out_specs, out_shape, scratch_shapes=(), compiler_params, name):
    if hosted is None:
        return pl.pallas_call(body, grid=grid, in_specs=in_specs, out_specs=out_specs, out_shape=out_shape,
                              scratch_shapes=scratch_shapes, compiler_params=compiler_params, name=name)
    single = not isinstance(out_shape, (list, tuple))
    o_specs = [out_specs] if single else list(out_specs)
    o_shape = [out_shape] if single else list(out_shape)
    n_in, n_out, n_scr, n = len(in_specs), len(o_specs), len(scratch_shapes), hosted.n

    def wrapped(*refs):
        ins, c_in = refs[:n_in], refs[n_in:n_in + n]
        outs, c_out = refs[n_in + n:n_in + n + n_out], refs[n_in + n + n_out:n_in + 2 * n + n_out]
        rest = refs[n_in + 2 * n + n_out:]
        scr, sems = rest[:n_scr], rest[n_scr:]
        ids = [pl.program_id(ax) for ax in range(len(grid))]
        first = functools.reduce(jnp.logical_and, [i == 0 for i in ids])
        last = functools.reduce(jnp.logical_and, [i == g - 1 for i, g in zip(ids, grid)])

        @pl.when(first)
        def _():
            hosted.start(c_in, c_out, *sems)

        body(*ins, *outs, *scr)

        @pl.when(last)
        def _():
            hosted.wait(c_in, c_out, *sems)

    call = pl.pallas_call(
        wrapped, grid=grid, in_specs=list(in_specs) + [ANY] * n, out_specs=o_specs + [ANY] * n,
        out_shape=o_shape + hosted.out_shapes(), scratch_shapes=list(scratch_shapes) + hosted.sems(),
        compiler_params=_cp(("arbitrary",) * len(grid)), name=name)

    def run(*args):
        outs = call(*args, *hosted.arrays)
        hosted.results = list(outs[n_out:])
        return outs[0] if single else list(outs[:n_out])

    return run


def exchange_now(hosted, name):
    n = hosted.n

    def body(*refs):
        src, dst, sems = refs[:n], refs[n:2 * n], refs[2 * n:]
        hosted.start(src, dst, *sems)
        hosted.wait(src, dst, *sems)

    return pl.pallas_call(body, out_shape=hosted.out_shapes(), in_specs=[ANY] * n, out_specs=[ANY] * n,
                          scratch_shapes=hosted.sems(), name=name)(*hosted.arrays)


def mm_nn(a, b, name, out_dtype=F32, res=None, res_scale=1.0, b_transposed=False, hosted=None):
    M, K = a.shape
    N = b.shape[0] if b_transposed else b.shape[1]
    tm, tn, tk = _pick(M, 512, 16), _pick(N, MM_TILE, LANES), _pick(K, MM_TILE, LANES)
    nk = K // tk
    has_res = res is not None
    dot = _bdot_nt if b_transposed else _bdot

    def body(*refs):
        if has_res:
            a_ref, b_ref, r_ref, o_ref, acc_ref = refs
        else:
            a_ref, b_ref, o_ref, acc_ref = refs
        k = pl.program_id(2)

        @pl.when(k == 0)
        def _():
            acc_ref[...] = jnp.zeros_like(acc_ref)

        acc_ref[...] += dot(a_ref[...], b_ref[...])

        @pl.when(k == nk - 1)
        def _():
            out = acc_ref[...]
            if has_res:
                out = out + res_scale * r_ref[...]
            o_ref[...] = out.astype(o_ref.dtype)

    b_spec = pl.BlockSpec((tn, tk), lambda i, j, k: (j, k)) if b_transposed else pl.BlockSpec((tk, tn), lambda i, j, k: (k, j))
    in_specs = [pl.BlockSpec((tm, tk), lambda i, j, k: (i, k)), b_spec]
    args = [a, b]
    if has_res:
        in_specs.append(pl.BlockSpec((tm, tn), lambda i, j, k: (i, j)))
        args.append(res)
    return _hosted_call(
        hosted, body, grid=(M // tm, N // tn, nk), in_specs=in_specs,
        out_specs=pl.BlockSpec((tm, tn), lambda i, j, k: (i, j)),
        out_shape=jax.ShapeDtypeStruct((M, N), out_dtype),
        scratch_shapes=[pltpu.VMEM((tm, tn), F32)],
        compiler_params=_cp(("parallel", "parallel", "arbitrary")), name=name)(*args)


def mm_tn(a, b, name, out_dtype=F32, a_transposed=False):
    K, T = a.shape if a_transposed else a.shape[::-1]
    _, N = b.shape
    tk = K if a_transposed else _pick(K, 512, LANES)
    tn, tt = _pick(N, MM_TILE, LANES), _pick(T, 512, LANES)
    nt = T // tt

    def body(a_ref, b_ref, o_ref, acc_ref):
        t = pl.program_id(2)

        @pl.when(t == 0)
        def _():
            acc_ref[...] = jnp.zeros_like(acc_ref)

        if a_transposed:
            acc_ref[...] += _bdot(a_ref[:, pl.ds(pl.multiple_of(t * tt, tt), tt)], b_ref[...])
        else:
            acc_ref[...] += _bdot_tn(a_ref[...], b_ref[...])

        @pl.when(t == nt - 1)
        def _():
            o_ref[...] = acc_ref[...].astype(o_ref.dtype)

    a_spec = pl.BlockSpec((K, T), lambda i, j, t: (0, 0)) if a_transposed else pl.BlockSpec((tt, tk), lambda i, j, t: (t, i))
    return pl.pallas_call(
        body, grid=(K // tk, N // tn, nt),
        in_specs=[a_spec, pl.BlockSpec((tt, tn), lambda i, j, t: (t, j))],
        out_specs=pl.BlockSpec((tk, tn), lambda i, j, t: (i, j)),
        out_shape=jax.ShapeDtypeStruct((K, N), out_dtype),
        scratch_shapes=[pltpu.VMEM((tk, tn), F32)],
        compiler_params=_cp(("parallel", "parallel", "arbitrary")), name=name)(a, b)


def _ln_apply(z, g, b):
    mu = jnp.mean(z, axis=-1, keepdims=True)
    zc = z - mu
    var = jnp.mean(zc * zc, axis=-1, keepdims=True)
    rstd = lax.rsqrt(var + LN_EPS)
    xhat = zc * rstd
    return xhat * g + b, xhat, rstd


def ln_bwd(dout, xhat, rstd, g, name):
    T, D = dout.shape
    tm = _pick(T, 512, 8)

    def body(do_ref, xh_ref, rs_ref, g_ref, dz_ref, dg_ref, db_ref):
        i = pl.program_id(0)

        @pl.when(i == 0)
        def _():
            dg_ref[...] = jnp.zeros_like(dg_ref)
            db_ref[...] = jnp.zeros_like(db_ref)

        do = do_ref[...]
        xh = xh_ref[...]
        dxh = do * g_ref[...]
        m1 = jnp.mean(dxh, axis=-1, keepdims=True)
        m2 = jnp.mean(dxh * xh, axis=-1, keepdims=True)
        dz_ref[...] = rs_ref[...] * (dxh - m1 - xh * m2)
        dg_ref[...] += jnp.sum(do * xh, axis=0, keepdims=True)
        db_ref[...] += jnp.sum(do, axis=0, keepdims=True)

    row = pl.BlockSpec((tm, D), lambda i: (i, 0))
    vec = pl.BlockSpec((1, D), lambda i: (0, 0))
    return pl.pallas_call(
        body, grid=(T // tm,),
        in_specs=[row, row, pl.BlockSpec((tm, 1), lambda i: (i, 0)), vec],
        out_specs=[row, vec, vec],
        out_shape=[jax.ShapeDtypeStruct((T, D), F32), jax.ShapeDtypeStruct((1, D), F32), jax.ShapeDtypeStruct((1, D), F32)],
        compiler_params=_cp(("arbitrary",)), name=name)(dout, xhat, rstd, g)


FFN_CHUNKS = N_DEV // 2


def ffn_fwd(h, w_in, w_out, g, b, name, hosted=None):
    T, D = h.shape
    fc = w_in.shape[2]
    half = w_out.shape[1]
    tm = _pick(T, 512, 8)
    nc = FFN_CHUNKS

    def body(h_ref, wg_ref, wu_ref, wo_ref, g_ref, b_ref, out_ref, xh_ref, rs_ref, G_ref, U_ref, ob_ref, obt_ref, acc_ref):
        c = pl.program_id(1)

        @pl.when(c == 0)
        def _():
            acc_ref[...] = jnp.zeros_like(acc_ref)

        hb = h_ref[...].astype(BF16)
        G = jnp.dot(hb, wg_ref[0], preferred_element_type=F32)
        U = jnp.dot(hb, wu_ref[0], preferred_element_type=F32)
        G_ref[0] = G
        U_ref[0] = U
        act = G * _sigmoid(G) * U
        acc_ref[...] += _bdot(act, wo_ref[...].reshape(2 * half, D))

        @pl.when(c == nc - 1)
        def _():
            z = ALPHA * h_ref[...] + 0.5 * acc_ref[...]
            out, xh, rs = _ln_apply(z, g_ref[...], b_ref[...])
            out_ref[...] = out
            ob_ref[...] = out.astype(BF16)
            obt_ref[...] = out.T.astype(BF16)
            xh_ref[...] = xh
            rs_ref[...] = rs

    row = pl.BlockSpec((tm, D), lambda i, c: (i, 0))
    vec = pl.BlockSpec((1, D), lambda i, c: (0, 0))
    cblk = pl.BlockSpec((1, tm, fc), lambda i, c: (c, i, 0))
    csds = jax.ShapeDtypeStruct((nc, T, fc), F32)
    return _hosted_call(
        hosted, body, grid=(T // tm, nc),
        in_specs=[row, pl.BlockSpec((1, D, fc), lambda i, c: (c, 0, 0)),
                  pl.BlockSpec((1, D, fc), lambda i, c: (c + nc, 0, 0)),
                  pl.BlockSpec((2, half, D), lambda i, c: (c, 0, 0)), vec, vec],
        out_specs=[row, row, pl.BlockSpec((tm, 1), lambda i, c: (i, 0)), cblk, cblk, row,
                   pl.BlockSpec((D, tm), lambda i, c: (0, i))],
        out_shape=[jax.ShapeDtypeStruct((T, D), F32), jax.ShapeDtypeStruct((T, D), F32), jax.ShapeDtypeStruct((T, 1), F32),
                   csds, csds, jax.ShapeDtypeStruct((T, D), BF16), jax.ShapeDtypeStruct((D, T), BF16)],
        scratch_shapes=[pltpu.VMEM((tm, D), F32)],
        compiler_params=_cp(("parallel", "arbitrary")), name=name)(h, w_in, w_in, w_out, g, b)


def ffn_bwd(dz, G, U, w_in, w_out, name, hosted=None):
    T, D = dz.shape
    nc, _, fc = G.shape
    half = w_out.shape[1]
    tm = _pick(T, 512, 16)

    def body(dz_ref, G_ref, U_ref, wg_ref, wu_ref, wo_ref, dh_ref, dG_ref, dU_ref, act_ref, acc_ref):
        c = pl.program_id(1)

        @pl.when(c == 0)
        def _():
            acc_ref[...] = jnp.zeros_like(acc_ref)

        dy = (0.5 * dz_ref[...]).astype(BF16)
        dact = _bdot_nt(dy, wo_ref[...].reshape(2 * half, D))
        G = G_ref[0]
        U = U_ref[0]
        s = _sigmoid(G)
        silu = G * s
        dG = (dact * U * (s * (1.0 + G * (1.0 - s)))).astype(BF16)
        dU = (dact * silu).astype(BF16)
        dG_ref[0] = dG
        dU_ref[0] = dU
        act_ref[0] = (silu * U).astype(BF16)
        acc_ref[...] += _bdot_nt(dG, wg_ref[0]) + _bdot_nt(dU, wu_ref[0])

        @pl.when(c == nc - 1)
        def _():
            dh_ref[...] = ALPHA * dz_ref[...] + acc_ref[...]

    row = pl.BlockSpec((tm, D), lambda i, c: (i, 0))
    cblk = pl.BlockSpec((1, tm, fc), lambda i, c: (c, i, 0))
    csds = jax.ShapeDtypeStruct((nc, T, fc), BF16)
    return _hosted_call(
        hosted, body, grid=(T // tm, nc),
        in_specs=[row, cblk, cblk, pl.BlockSpec((1, D, fc), lambda i, c: (c, 0, 0)),
                  pl.BlockSpec((1, D, fc), lambda i, c: (c + nc, 0, 0)),
                  pl.BlockSpec((2, half, D), lambda i, c: (c, 0, 0))],
        out_specs=[row, cblk, cblk, cblk],
        out_shape=[jax.ShapeDtypeStruct((T, D), F32), csds, csds, csds],
        scratch_shapes=[pltpu.VMEM((tm, D), F32)],
        compiler_params=_cp(("parallel", "arbitrary")), name=name)(dz, G, U, w_in, w_in, w_out)


def ffn_dw_in(h_t, dG, dU, name, hosted=None):
    D, T = h_t.shape
    nc, _, fc = dG.shape
    tt = _pick(T, 512, LANES)
    nt = T // tt

    def body(h_ref, dG_ref, dU_ref, o_ref, acc_ref):
        s = pl.program_id(0)
        t = pl.program_id(1)

        @pl.when(t == 0)
        def _():
            acc_ref[...] = jnp.zeros_like(acc_ref)

        hb = h_ref[:, pl.ds(pl.multiple_of(t * tt, tt), tt)]

        @pl.when(s < nc)
        def _():
            acc_ref[...] += jnp.dot(hb, dG_ref[0], preferred_element_type=F32)

        @pl.when(s >= nc)
        def _():
            acc_ref[...] += jnp.dot(hb, dU_ref[0], preferred_element_type=F32)

        @pl.when(t == nt - 1)
        def _():
            o_ref[0] = acc_ref[...].astype(o_ref.dtype)

    return _hosted_call(
        hosted, body, grid=(2 * nc, nt),
        in_specs=[pl.BlockSpec((D, T), lambda s, t: (0, 0)),
                  pl.BlockSpec((1, tt, fc), lambda s, t: (jnp.minimum(s, nc - 1), jnp.where(s < nc, t, nt - 1), 0)),
                  pl.BlockSpec((1, tt, fc), lambda s, t: (jnp.maximum(s - nc, 0), jnp.where(s >= nc, t, 0), 0))],
        out_specs=pl.BlockSpec((1, D, fc), lambda s, t: (s, 0, 0)),
        out_shape=jax.ShapeDtypeStruct((2 * nc, D, fc), BF16),
        scratch_shapes=[pltpu.VMEM((D, fc), F32)],
        compiler_params=_cp(("parallel", "arbitrary")), name=name)(h_t, dG, dU)


def ffn_dw_out(act, dz, name, hosted=None):
    nc, T, fc = act.shape
    D = dz.shape[1]
    half = fc // 2
    tt = _pick(T, 512, 16)
    nt = T // tt

    def body(a_ref, dz_ref, o_ref, acc_ref):
        t = pl.program_id(1)

        @pl.when(t == 0)
        def _():
            acc_ref[...] = jnp.zeros_like(acc_ref)

        acc_ref[...] += _bdot_tn(a_ref[0], dz_ref[...])

        @pl.when(t == nt - 1)
        def _():
            o_ref[...] = (0.5 * acc_ref[...]).reshape(2, half, D).astype(o_ref.dtype)

    return _hosted_call(
        hosted, body, grid=(nc, nt),
        in_specs=[pl.BlockSpec((1, tt, fc), lambda c, t: (c, t, 0)), pl.BlockSpec((tt, D), lambda c, t: (t, 0))],
        out_specs=pl.BlockSpec((2, half, D), lambda c, t: (c, 0, 0)),
        out_shape=jax.ShapeDtypeStruct((2 * nc, half, D), BF16),
        scratch_shapes=[pltpu.VMEM((fc, D), F32)],
        compiler_params=_cp(("parallel", "arbitrary")), name=name)(act, dz)


def proj_res_ln(parts, w, res, g, b, name):
    T, D = res.shape
    tm = _pick(T, 512, 8)
    widths = [p.shape[1] for p in parts]
    offs = [int(sum(widths[:i])) for i in range(len(parts))]
    n = len(parts)

    def body(*refs):
        p_refs = refs[:n]
        w_ref, r_ref, g_ref, b_ref, out_ref, xh_ref, rs_ref, ob_ref, obt_ref = refs[n:]
        acc = ALPHA * r_ref[...]
        for p_ref, o, wd in zip(p_refs, offs, widths):
            acc = acc + _bdot(p_ref[...], w_ref[o:o + wd, :])
        out, xh, rs = _ln_apply(acc, g_ref[...], b_ref[...])
        out_ref[...] = out
        ob_ref[...] = out.astype(BF16)
        obt_ref[...] = out.T.astype(BF16)
        xh_ref[...] = xh
        rs_ref[...] = rs

    row = pl.BlockSpec((tm, D), lambda i: (i, 0))
    vec = pl.BlockSpec((1, D), lambda i: (0, 0))
    return pl.pallas_call(
        body, grid=(T // tm,),
        in_specs=[pl.BlockSpec((tm, wd), lambda i: (i, 0)) for wd in widths]
        + [pl.BlockSpec(w.shape, lambda i: (0, 0)), row, vec, vec],
        out_specs=[row, row, pl.BlockSpec((tm, 1), lambda i: (i, 0)), row, pl.BlockSpec((D, tm), lambda i: (0, i))],
        out_shape=[jax.ShapeDtypeStruct((T, D), F32), jax.ShapeDtypeStruct((T, D), F32), jax.ShapeDtypeStruct((T, 1), F32),
                   jax.ShapeDtypeStruct((T, D), BF16), jax.ShapeDtypeStruct((D, T), BF16)],
        compiler_params=_cp(("parallel",)), name=name)(*parts, w, res, g, b)


def ple_fwd(h, h_b, p, wg, wp, name):
    T, D = h.shape
    P = p.shape[1]
    tm, tn = _pick(T, 512, 16), D

    def body(h_ref, hn_ref, p_ref, wg_ref, wp_ref, out_ref, a_ref, e_ref, ot_ref):
        a = _bdot(h_ref[...], wg_ref[...])
        e = _bdot(p_ref[...], wp_ref[...])
        a_ref[...] = a
        e_ref[...] = e
        out = hn_ref[...] + _sigmoid(a) * e
        out_ref[...] = out
        ot_ref[...] = out.T.astype(BF16)

    blk = pl.BlockSpec((tm, tn), lambda i, j: (i, j))
    sds = jax.ShapeDtypeStruct((T, D), F32)
    return pl.pallas_call(
        body, grid=(T // tm, D // tn),
        in_specs=[pl.BlockSpec((tm, D), lambda i, j: (i, 0)), blk, pl.BlockSpec((tm, P), lambda i, j: (i, 0)),
                  pl.BlockSpec((D, tn), lambda i, j: (0, j)), pl.BlockSpec((P, tn), lambda i, j: (0, j))],
        out_specs=[blk, blk, blk, pl.BlockSpec((D, tm), lambda i, j: (0, i))],
        out_shape=[sds, sds, sds, jax.ShapeDtypeStruct((D, T), BF16)],
        compiler_params=_cp(("parallel", "parallel")), name=name)(h_b, h, p, wg, wp)


def ple_bwd(dout, a, e, wg, name):
    T, D = dout.shape
    tm = _pick(T, 512, 16)

    def body(do_ref, a_ref, e_ref, wg_ref, dh_ref, da_ref, de_ref):
        do = do_ref[...]
        s = _sigmoid(a_ref[...])
        da = (do * e_ref[...] * s * (1.0 - s)).astype(BF16)
        da_ref[...] = da
        de_ref[...] = (do * s).astype(BF16)
        dh_ref[...] = do + _bdot_nt(da, wg_ref[...])

    row = pl.BlockSpec((tm, D), lambda i: (i, 0))
    return pl.pallas_call(
        body, grid=(T // tm,),
        in_specs=[row, row, row, pl.BlockSpec((D, D), lambda i: (0, 0))],
        out_specs=[row, row, row],
        out_shape=[jax.ShapeDtypeStruct((T, D), F32), jax.ShapeDtypeStruct((T, D), BF16), jax.ShapeDtypeStruct((T, D), BF16)],
        compiler_params=_cp(("parallel",)), name=name)(dout, a, e, wg)


def loss_head(y, target, name):
    T, D = y.shape
    tm = _pick(T, 512, 8)

    def body(y_ref, t_ref, loss_ref, dy_ref):
        i = pl.program_id(0)

        @pl.when(i == 0)
        def _():
            loss_ref[...] = jnp.zeros_like(loss_ref)

        err = y_ref[...] - t_ref[...]
        dy_ref[...] = err * (1.0 / D)
        per_tok = jnp.sum(err * err, axis=-1, keepdims=True) * (1.0 / D)
        loss_ref[...] += 0.5 * jnp.sum(per_tok, axis=0, keepdims=True)

    row = pl.BlockSpec((tm, D), lambda i: (i, 0))
    return pl.pallas_call(
        body, grid=(T // tm,), in_specs=[row, row],
        out_specs=[pl.BlockSpec((1, 1), lambda i: (0, 0)), row],
        out_shape=[jax.ShapeDtypeStruct((1, 1), F32), jax.ShapeDtypeStruct((T, D), F32)],
        compiler_params=_cp(("arbitrary",)), name=name)(y, target)


HALO = 8


def _conv_taps(pad_ref, w_ref, tm, base):
    acc = w_ref[0:1, :] * pad_ref[pl.ds(base, tm), :]
    for k in range(1, GDN_CONV):
        acc = acc + w_ref[k:k + 1, :] * pad_ref[pl.ds(base + k, tm), :]
    return acc


GDN_GROUP_W = GDN_W
GDN_PRE_ROWS = 512


def _head_segments():
    head = jnp.arange(GDN_W, dtype=jnp.int32) // GDN_D
    return (head[:, None] == head[None, :]).astype(BF16)


def _head_sums(x, seg):
    hi = x.astype(BF16)
    r1 = x - hi.astype(F32)
    mid = r1.astype(BF16)
    lo = (r1 - mid.astype(F32)).astype(BF16)
    d = functools.partial(jnp.dot, preferred_element_type=F32)
    return d(hi, seg) + d(mid, seg) + d(lo, seg)


def _gdn_pre_common(x_ref, halo_ref, w_ref, seg_ref, pad_ref, tm):
    i = pl.program_id(1)
    grp = pl.program_id(0)
    pad_ref[0:HALO, :] = jnp.where(i == 0, 0.0, halo_ref[...])
    pad_ref[HALO:HALO + tm, :] = x_ref[...]
    c = _conv_taps(pad_ref, w_ref, tm, HALO - (GDN_CONV - 1))
    s = _sigmoid(c)
    y = c * s
    r = lax.rsqrt(_head_sums(y * y, seg_ref[...]) + RMS_EPS)
    scale = jnp.where(grp < 1, GDN_D ** -0.5, 1.0)
    return grp < 2, c, s, y, r, scale


def gdn_pre_fwd(proj, conv_w, name):
    T = proj.shape[0]
    tm = _pick(T, GDN_PRE_ROWS, 8)
    GW = GDN_GROUP_W

    def body(x_ref, halo_ref, w_ref, seg_ref, o_ref, pad_ref):
        normed, c, s, y, r, scale = _gdn_pre_common(x_ref, halo_ref, w_ref, seg_ref, pad_ref, tm)
        o_ref[...] = jnp.where(normed, y * r * scale, y)

    return pl.pallas_call(
        body, grid=(3, T // tm),
        in_specs=[pl.BlockSpec((tm, GW), lambda hb, i: (i, hb)),
                  pl.BlockSpec((HALO, GW), lambda hb, i: (jnp.maximum(i * (tm // HALO) - 1, 0), hb)),
                  pl.BlockSpec((GDN_CONV, GW), lambda hb, i: (0, hb)), pl.BlockSpec((GW, GW), lambda hb, i: (0, 0))],
        out_specs=pl.BlockSpec((tm, GW), lambda hb, i: (i, hb)),
        out_shape=jax.ShapeDtypeStruct((T, 3 * GW), F32),
        scratch_shapes=[pltpu.VMEM((tm + HALO, GW), F32)],
        compiler_params=_cp(("parallel", "parallel")), name=name)(proj, proj, conv_w, _head_segments())


def gdn_pre_bwd_pointwise(proj, conv_w, dqkv, name, hosted=None):
    T = proj.shape[0]
    tm = _pick(T, GDN_PRE_ROWS, 8)
    GW = GDN_GROUP_W

    def body(x_ref, halo_ref, w_ref, seg_ref, d_ref, dc_ref, dw_ref, pad_ref):
        i = pl.program_id(1)
        normed, c, s, y, r, scale = _gdn_pre_common(x_ref, halo_ref, w_ref, seg_ref, pad_ref, tm)

        @pl.when(i == 0)
        def _():
            dw_ref[...] = jnp.zeros_like(dw_ref)

        d = d_ref[...]
        n = y * r
        dn = d * scale
        dy = jnp.where(normed, r * (dn - n * _head_sums(dn * n, seg_ref[...])), d)
        dc = dy * (s * (1.0 + c * (1.0 - s)))
        dc_ref[...] = dc
        for k in range(GDN_CONV):
            xs = pad_ref[pl.ds(HALO - (GDN_CONV - 1) + k, tm), :]
            dw_ref[k:k + 1, :] += jnp.sum(dc * xs, axis=0, keepdims=True)

    blk = pl.BlockSpec((tm, GW), lambda hb, i: (i, hb))
    wblk = pl.BlockSpec((GDN_CONV, GW), lambda hb, i: (0, hb))
    return _hosted_call(
        hosted, body, grid=(3, T // tm),
        in_specs=[blk, pl.BlockSpec((HALO, GW), lambda hb, i: (jnp.maximum(i * (tm // HALO) - 1, 0), hb)), wblk,
                  pl.BlockSpec((GW, GW), lambda hb, i: (0, 0)), blk],
        out_specs=[blk, wblk],
        out_shape=[jax.ShapeDtypeStruct((T, 3 * GW), F32), jax.ShapeDtypeStruct((GDN_CONV, 3 * GW), F32)],
        scratch_shapes=[pltpu.VMEM((tm + HALO, GW), F32)],
        compiler_params=_cp(("parallel", "arbitrary")), name=name)(proj, proj, conv_w, _head_segments(), dqkv)


def gdn_pre_bwd_conv(dc, conv_w_p, name):
    T = dc.shape[0]
    tm = _pick(T, GDN_PRE_ROWS, 8)
    nt = T // tm
    GW = GDN_GROUP_W

    def body(dc_ref, halo_ref, w_ref, dx_ref, pad_ref):
        i = pl.program_id(1)
        pad_ref[0:tm, :] = dc_ref[...]
        pad_ref[tm:tm + HALO, :] = jnp.where(i == nt - 1, 0.0, halo_ref[...])
        acc = w_ref[GDN_CONV - 1:GDN_CONV, :] * pad_ref[pl.ds(0, tm), :]
        for k in range(GDN_CONV - 1):
            acc = acc + w_ref[k:k + 1, :] * pad_ref[pl.ds(GDN_CONV - 1 - k, tm), :]
        dx_ref[...] = acc

    blk = pl.BlockSpec((tm, GW), lambda hb, i: (i, hb))
    return pl.pallas_call(
        body, grid=(3, nt),
        in_specs=[blk, pl.BlockSpec((HALO, GW), lambda hb, i: (jnp.minimum((i + 1) * (tm // HALO), T // HALO - 1), hb)),
                  pl.BlockSpec((GDN_CONV, GW), lambda hb, i: (0, hb))],
        out_specs=blk,
        out_shape=jax.ShapeDtypeStruct((T, 3 * GW), F32),
        scratch_shapes=[pltpu.VMEM((tm + HALO, GW), F32)],
        compiler_params=_cp(("parallel", "parallel")), name=name)(dc, dc, conv_w_p)


def _chunk_masks(C):
    row = _iota2((C, C), 0)
    col = _iota2((C, C), 1)
    return row >= col, row > col, row == col


GDN_FWD_CHUNKS = 8
BNN = (((2,), (1,)), ((0,), (0,)))
BNT = (((2,), (2,)), ((0,), (0,)))
BTN = (((1,), (1,)), ((0,), (0,)))


def _bmm(a, b, dims=BNN):
    return lax.dot_general(a.astype(BF16), b.astype(BF16), dims, preferred_element_type=F32)


def _hbmm(a, b):
    m = a.shape[1]
    a_hi, a_lo = _split2(a)
    b_hi, b_lo = _split2(b)
    r = lax.dot_general(jnp.concatenate([a_hi, a_lo], axis=1), b_hi, BNN, preferred_element_type=F32)
    return r[:, :m] + r[:, m:] + lax.dot_general(a_hi, b_lo, BNN, preferred_element_type=F32)


def _hbmm_tn(a, b):
    a_hi, a_lo = _split2(a)
    b_hi, b_lo = _split2(b)
    d = functools.partial(lax.dot_general, dimension_numbers=BTN, preferred_element_type=F32)
    return d(a_hi, b_hi) + d(a_lo, b_hi) + d(a_hi, b_lo)


def _col_to_row(colv, eye):
    return jnp.sum(jnp.where(eye, colv, 0.0), axis=1, keepdims=True)


def _row_to_col(rowv, eye):
    return jnp.sum(jnp.where(eye, rowv, 0.0), axis=2, keepdims=True)


def _unit_lower_inverse(A, eye):
    C = A.shape[1]
    P = jnp.where(eye, 1.0, 0.0) - A
    Bp = _hbmm(A, A)
    for _ in range(4):
        R = _hbmm(jnp.concatenate([Bp, P], axis=1), Bp)
        Bp = R[:, :C]
        P = P + R[:, C:]
    return P + _hbmm(P, Bp)


def _stack_heads(ref, first_head, n, row0=0):
    rows = pl.ds(row0, GDN_CHUNK)
    return jnp.stack([ref[rows, pl.ds((first_head + h) * GDN_D, GDN_D)] for h in range(n)])


def _unstack_heads(ref, first_head, val, row0=0):
    rows = pl.ds(row0, GDN_CHUNK)
    for h in range(val.shape[0]):
        ref[rows, pl.ds((first_head + h) * GDN_D, GDN_D)] = val[h]


def _gdn_gates(gab, a_row, dt_row, incl):
    g_all = -jnp.exp(a_row) * _softplus(gab + dt_row)
    beta_all = _sigmoid(gab)
    gc_all = _ones_dot_left(incl.astype(BF16), g_all)
    return g_all, beta_all, gc_all


def _gdn_common(qkv_ref, gc_all, beta_all, incl, strict, eye, row0=0):
    C, H = GDN_CHUNK, GDN_HEADS
    q, k, v = (_stack_heads(qkv_ref, j * H, H, row0) for j in range(3))
    gc = jnp.stack([gc_all[:, h:h + 1] for h in range(H)])
    beta = jnp.stack([beta_all[:, H + h:H + h + 1] for h in range(H)])
    gc_row = _col_to_row(gc, eye)
    decay = jnp.where(incl, jnp.exp(jnp.where(incl, gc - gc_row, 0.0)), 0.0)
    e_gc = jnp.exp(gc)
    gl = gc[:, C - 1:C, :]
    e_gl = jnp.exp(gl)
    ekd = jnp.exp(gl - gc)
    kb = k * beta
    A = jnp.where(strict, _bmm(kb, k, BNT) * decay, 0.0)
    Pm = jnp.where(incl, _bmm(q, k, BNT) * decay, 0.0)
    return q, k, v, gc, beta, decay, e_gc, e_gl, ekd, kb, A, Pm


def gdn_chunk_fwd(qkv, proj, a_row, dt_row, norm_w, name, hosted=None):
    T = qkv.shape[0]
    C, H, Dh = GDN_CHUNK, GDN_HEADS, GDN_D
    N = T // C
    J = GDN_FWD_CHUNKS if N % GDN_FWD_CHUNKS == 0 else 1

    def body(qkv_ref, gz_ref, gab_ref, a_ref, dt_ref, nw_ref, o_ref, opre_ref, Tm_ref, Sin_ref, S_ref):
        n = pl.program_id(0)

        @pl.when(n == 0)
        def _():
            S_ref[...] = jnp.zeros_like(S_ref)

        incl, strict, eye = _chunk_masks(C)
        gab = gab_ref[...]
        per_chunk = []
        for j in range(J):
            _, beta_all, gc_all = _gdn_gates(gab[j * C:(j + 1) * C], a_ref[...], dt_ref[...], incl)
            per_chunk.append(_gdn_common(qkv_ref, gc_all, beta_all, incl, strict, eye, row0=j * C))
        q, k, v, gc, beta, decay, e_gc, e_gl, ekd, kb, A, Pm = (jnp.concatenate(t, axis=0) for t in zip(*per_chunk))
        Tm = _unit_lower_inverse(A, eye)
        u = _hbmm(Tm, v * beta)
        w = _hbmm(Tm, kb * e_gc)
        qd = q * e_gc
        kd = k * ekd
        S = S_ref[...]
        for j in range(J):
            hs = slice(j * H, (j + 1) * H)
            v_new = u[hs] - _bmm(w[hs], S)
            o = _bmm(qd[hs], S) + _bmm(Pm[hs], v_new)
            Sin_ref[j] = S
            Tm_ref[j] = Tm[hs]
            S = S * e_gl[hs] + _bmm(kd[hs], v_new, BTN)
            r = lax.rsqrt(jnp.mean(o * o, axis=-1, keepdims=True) + RMS_EPS)
            gz = _stack_heads(gz_ref, 0, H, j * C)
            _unstack_heads(opre_ref, 0, o, j * C)
            _unstack_heads(o_ref, 0, o * r * nw_ref[...] * (gz * _sigmoid(gz)), j * C)
        S_ref[...] = S

    vec = pl.BlockSpec((1, LANES), lambda n: (0, 0))
    hblk = pl.BlockSpec((J * C, GDN_W), lambda n: (n, 0))
    sblk = pl.BlockSpec((J, H, Dh, Dh), lambda n: (n, 0, 0, 0))
    return _hosted_call(
        hosted, body, grid=(N // J,),
        in_specs=[pl.BlockSpec((J * C, 3 * GDN_W), lambda n: (n, 0)),
                  pl.BlockSpec((J * C, GDN_W), lambda n: (n, CB_GZ * LANES // GDN_W)),
                  pl.BlockSpec((J * C, LANES), lambda n: (n, CB_GAB)), vec, vec, pl.BlockSpec((1, Dh), lambda n: (0, 0))],
        out_specs=[hblk, hblk, sblk, sblk],
        out_shape=[jax.ShapeDtypeStruct((T, GDN_W), F32), jax.ShapeDtypeStruct((T, GDN_W), F32)]
        + [jax.ShapeDtypeStruct((N, H, Dh, Dh), F32)] * 2,
        scratch_shapes=[pltpu.VMEM((H, Dh, Dh), F32)],
        compiler_params=_cp(("arbitrary",)), name=name)(qkv, proj, proj, a_row, dt_row, norm_w)


def gdn_chunk_bwd(qkv, proj, a_row, dt_row, norm_w, opre, Tm_all, Sin_all, docat, name, hosted=None):
    T = qkv.shape[0]
    C, H, Dh = GDN_CHUNK, GDN_HEADS, GDN_D
    N = T // C

    def body(qkv_ref, gz_ref, gab_ref, a_ref, dt_ref, nw_ref, opre_ref, Tm_ref, Sin_ref, do_ref,
             dqkv_ref, dgz_ref, dgab_ref, da_ref, ddt_ref, dnw_ref, dS_ref):
        n = pl.program_id(0)

        @pl.when(n == 0)
        def _():
            dS_ref[...] = jnp.zeros_like(dS_ref)
            da_ref[...] = jnp.zeros_like(da_ref)
            ddt_ref[...] = jnp.zeros_like(ddt_ref)
            dnw_ref[...] = jnp.zeros_like(dnw_ref)

        incl, strict, eye = _chunk_masks(C)
        gab = gab_ref[...]
        g_all, beta_all, gc_all = _gdn_gates(gab, a_ref[...], dt_ref[...], incl)
        lane = _iota2((C, LANES), 1)
        rowi = _iota2((C, 1), 0)
        nw = nw_ref[...]
        q, k, v, gc, beta, decay, e_gc, e_gl, ekd, kb, A, Pm = _gdn_common(qkv_ref, gc_all, beta_all, incl, strict, eye)
        Tm = Tm_ref[0]
        S = Sin_ref[0]
        dS = dS_ref[...]
        kbe = kb * e_gc
        u = _hbmm(Tm, v * beta)
        w = _hbmm(Tm, kbe)
        qd = q * e_gc
        kd = k * ekd
        v_new = u - _bmm(w, S)
        o = _stack_heads(opre_ref, 0, H)
        gz = _stack_heads(gz_ref, 0, H)
        don = _stack_heads(do_ref, 0, H)
        r = lax.rsqrt(jnp.mean(o * o, axis=-1, keepdims=True) + RMS_EPS)
        nn = o * r
        sgz = _sigmoid(gz)
        silu = gz * sgz
        _unstack_heads(dgz_ref, 0, don * nn * nw * (sgz * (1.0 + gz * (1.0 - sgz))))
        dnn = don * nw * silu
        dnw_ref[...] += jnp.sum(jnp.sum(don * nn * silu, axis=0), axis=0, keepdims=True)
        do = r * (dnn - nn * jnp.mean(dnn * nn, axis=-1, keepdims=True))
        dv_new = _bmm(Pm, do, BTN) + _bmm(kd, dS)
        dPm = jnp.where(incl, _bmm(do, v_new, BNT), 0.0)
        dqd = _bmm(do, S, BNT)
        dkd = _bmm(v_new, dS, BNT)
        dS_ref[...] = _bmm(qd, do, BTN) + e_gl * dS - _bmm(w, dv_new, BTN)
        dgl = jnp.sum(jnp.sum(dS * S, axis=2, keepdims=True), axis=1, keepdims=True) * e_gl
        dw = -_bmm(dv_new, S, BNT)
        dvb = _hbmm_tn(Tm, dv_new)
        dkbe = _hbmm_tn(Tm, dw)
        dA = -jnp.where(strict, _bmm(dvb, u, BNT) + _bmm(dkbe, w, BNT), 0.0)
        dAD = dA * decay
        dPD = dPm * decay
        Gm = dA * A + dPm * Pm
        dgc = jnp.sum(Gm, axis=2, keepdims=True) - _row_to_col(jnp.sum(Gm, axis=1, keepdims=True), eye)
        dkb = _bmm(dAD, k) + dkbe * e_gc
        dk = _bmm(dAD, kb, BTN) + _bmm(dPD, q, BTN) + dkd * ekd + dkb * beta
        dq = _bmm(dPD, k) + dqd * e_gc
        tkd = jnp.sum(dkd * kd, axis=-1, keepdims=True)
        dgc = dgc + jnp.sum(dqd * qd, axis=-1, keepdims=True) - tkd + jnp.sum(dkbe * kbe, axis=-1, keepdims=True)
        dgl = dgl + jnp.sum(tkd, axis=1, keepdims=True)
        dgc = dgc + jnp.where(rowi == C - 1, dgl, 0.0)
        dbeta = jnp.sum(dvb * v, axis=-1, keepdims=True) + jnp.sum(dkb * k, axis=-1, keepdims=True)
        _unstack_heads(dqkv_ref, 0, dq)
        _unstack_heads(dqkv_ref, H, dk)
        _unstack_heads(dqkv_ref, 2 * H, dvb * beta)
        dgc_all = jnp.zeros((C, LANES), F32)
        dbeta_all = jnp.zeros((C, LANES), F32)
        for h in range(H):
            dgc_all = dgc_all + jnp.where(lane == h, dgc[h], 0.0)
            dbeta_all = dbeta_all + jnp.where(lane == H + h, dbeta[h], 0.0)
        upper = (_iota2((C, C), 0) <= _iota2((C, C), 1)).astype(BF16)
        dg_all = _ones_dot_left(upper, dgc_all)
        dga = dg_all * (-jnp.exp(a_ref[...])) * _sigmoid(gab + dt_ref[...])
        dgb = dbeta_all * beta_all * (1.0 - beta_all)
        dgab_ref[...] = jnp.where(lane < H, dga, jnp.where(lane < 2 * H, dgb, 0.0))
        da_ref[...] += jnp.sum(jnp.where(lane < H, dg_all * g_all, 0.0), axis=0, keepdims=True)
        ddt_ref[...] += jnp.sum(jnp.where(lane < H, dga, 0.0), axis=0, keepdims=True)

    rev = lambda n: N - 1 - n
    vec = pl.BlockSpec((1, LANES), lambda n: (0, 0))
    nwv = pl.BlockSpec((1, Dh), lambda n: (0, 0))
    hblk = pl.BlockSpec((C, GDN_W), lambda n: (rev(n), 0))
    sblk = pl.BlockSpec((1, H, Dh, Dh), lambda n: (rev(n), 0, 0, 0))
    qblk = pl.BlockSpec((C, 3 * GDN_W), lambda n: (rev(n), 0))
    return _hosted_call(
        hosted, body, grid=(N,),
        in_specs=[qblk, pl.BlockSpec((C, GDN_W), lambda n: (rev(n), CB_GZ * LANES // GDN_W)),
                  pl.BlockSpec((C, LANES), lambda n: (rev(n), CB_GAB)), vec, vec, nwv, hblk, sblk, sblk, hblk],
        out_specs=[qblk, hblk, pl.BlockSpec((C, LANES), lambda n: (rev(n), 0)), vec, vec, nwv],
        out_shape=[jax.ShapeDtypeStruct((T, 3 * GDN_W), F32), jax.ShapeDtypeStruct((T, GDN_W), F32),
                   jax.ShapeDtypeStruct((T, LANES), F32), jax.ShapeDtypeStruct((1, LANES), F32),
                   jax.ShapeDtypeStruct((1, LANES), F32), jax.ShapeDtypeStruct((1, Dh), F32)],
        scratch_shapes=[pltpu.VMEM((H, Dh, Dh), F32)],
        compiler_params=_cp(("arbitrary",)), name=name)(qkv, proj, proj, a_row, dt_row, norm_w, opre, Tm_all, Sin_all, docat)


ATT_BQ, ATT_BK = 512, 1024
NEG_BIG = -1e30


def _att_blocks(T):
    bq, bk = min(ATT_BQ, T), min(ATT_BK, T)
    assert bk % bq == 0 and T % bk == 0
    return bq, bk


def _att_specs(T, bq, cbs):
    qspec = lambda cb: pl.BlockSpec((bq, LANES), lambda h, i: (i, cb + h))
    kspec = lambda cb: pl.BlockSpec((T, LANES), lambda h, i: (0, cb + h))
    return qspec, kspec


def _kblock(ref, kb, bk):
    return ref[pl.ds(pl.multiple_of(kb * bk, bk), bk), :]


def _att_pos(i, kb, bq, bk):
    qpos = i * bq + _iota2((bq, bk), 0)
    kpos = kb * bk + _iota2((bq, bk), 1)
    return qpos, kpos


def _later_keys(n):
    return (_iota2((n, n), 0) > _iota2((n, n), 1)).astype(BF16)


def _earlier_keys(n):
    return (_iota2((n, n), 0) < _iota2((n, n), 1)).astype(BF16)


def _tri_dot(x, tri, terms):
    acc, rest = None, x
    for t in range(terms):
        part = rest.astype(BF16)
        if t + 1 < terms:
            rest = rest - part.astype(F32)
        d = jnp.dot(part, tri, preferred_element_type=F32)
        acc = d if acc is None else acc + d
    return acc


SB_BLOCK = 256
SB_DEAD = -104.0


def _sb_blocks(T):
    b = min(SB_BLOCK, T)
    assert T % b == 0 and T // b <= LANES
    return b, b


def sb_fwd(proj, name, hosted=None):
    T = proj.shape[0]
    H = SB_HEADS
    bq, bk = _sb_blocks(T)
    scale = SB_DIM ** -0.5

    def body(q_ref, k_ref, v_ref, o_ref, tot_ref):
        i = pl.program_id(1)
        qb = q_ref[...].astype(BF16)
        diag = (i * bq) // bk
        lane = _iota2((bq, LANES), 1)
        later = _later_keys(bk)

        def block(kb, acc, R, masked):
            z = _bdot_nt(qb, _kblock(k_ref, kb, bk)) * scale
            sp = _softplus(z)
            if masked:
                qpos, kpos = _att_pos(i, kb, bq, bk)
                mask = kpos < qpos
                l1m = jnp.where(mask, -sp, 0.0)
            else:
                l1m = -sp
            W = jnp.exp((z - sp) + _tri_dot(l1m, later, 3) + R)
            if masked:
                W = jnp.where(mask, W, 0.0)
            acc = acc + _bdot(W, _kblock(v_ref, kb, bk))
            return acc, R + jnp.sum(l1m, axis=-1, keepdims=True)

        acc, R = block(diag, jnp.zeros((bq, LANES), F32), jnp.zeros((bq, 1), F32), True)

        def live(c):
            return jnp.logical_and(c[0] >= 0, jnp.max(c[2]) > SB_DEAD)

        def step(c):
            kb, acc, R, Rb = c
            acc, R_next = block(kb, acc, R, False)
            return kb - 1, acc, R_next, jnp.where(lane == kb, R, Rb)

        _, acc, _, Rb = lax.while_loop(live, step, (diag - 1, acc, R, jnp.where(lane == diag, 0.0, NEG_BIG)))
        o_ref[...] = acc
        tot_ref[...] = Rb

    qspec, kspec = _att_specs(T, bq, None)
    sds = jax.ShapeDtypeStruct((T, H * LANES), F32)
    oblk = pl.BlockSpec((bq, LANES), lambda h, i: (i, h))
    return _hosted_call(
        hosted, body, grid=(H, T // bq), in_specs=[qspec(CB_SQ), kspec(CB_SK), kspec(CB_SV)],
        out_specs=[oblk, oblk], out_shape=[sds, sds],
        compiler_params=_cp(("parallel", "parallel")), name=name)(proj, proj, proj)


def sb_bwd(proj, tot, docat, do_cb, name):
    T = proj.shape[0]
    H = SB_HEADS
    bq, bk = _sb_blocks(T)
    scale = SB_DIM ** -0.5

    def body(q_ref, k_ref, v_ref, tot_ref, do_ref, dq_ref, dk_ref, dv_ref):
        i = pl.program_id(1)

        @pl.when(i == 0)
        def _():
            dk_ref[...] = jnp.zeros_like(dk_ref)
            dv_ref[...] = jnp.zeros_like(dv_ref)

        qb = q_ref[...].astype(BF16)
        dob = do_ref[...].astype(BF16)
        Rb = tot_ref[...]
        diag = (i * bq) // bk
        lane = _iota2((bq, LANES), 1)
        later, earlier = _later_keys(bk), _earlier_keys(bk)
        first = lax.while_loop(
            lambda kb: jnp.logical_and(kb < diag, jnp.max(jnp.where(lane == kb, Rb, NEG_BIG)) <= SB_DEAD),
            lambda kb: kb + 1, jnp.int32(0))

        def block(kb, carry, masked):
            dq, Epre = carry
            R = jnp.sum(jnp.where(lane == kb, Rb, 0.0), axis=1, keepdims=True)
            kblk = _kblock(k_ref, kb, bk).astype(BF16)
            z = _bdot_nt(qb, kblk) * scale
            sp = _softplus(z)
            if masked:
                qpos, kpos = _att_pos(i, kb, bq, bk)
                mask = kpos < qpos
                l1m = jnp.where(mask, -sp, 0.0)
            else:
                l1m = -sp
            W = jnp.exp((z - sp) + _tri_dot(l1m, later, 3) + R)
            if masked:
                W = jnp.where(mask, W, 0.0)
            E = _bdot_nt(dob, _kblock(v_ref, kb, bk)) * W
            cexcl = _tri_dot(E, earlier, 3) + Epre
            neg = jnp.exp(-sp)
            dz = E * neg - cexcl * (1.0 - neg)
            if masked:
                dz = jnp.where(mask, dz, 0.0)
            dz = (dz * scale).astype(BF16)
            rows = pl.ds(pl.multiple_of(kb * bk, bk), bk)
            dk_ref[rows, :] += lax.dot_general(dz, qb, TN_DIMS, preferred_element_type=F32)
            dv_ref[rows, :] += lax.dot_general(W.astype(BF16), dob, TN_DIMS, preferred_element_type=F32)
            dq = dq + jnp.dot(dz, kblk, preferred_element_type=F32)
            return dq, Epre + jnp.sum(E, axis=-1, keepdims=True)

        init = (jnp.zeros((bq, LANES), F32), jnp.zeros((bq, 1), F32))
        carry = lax.fori_loop(first, diag, lambda kb, c: block(kb, c, False), init)
        dq, _ = block(diag, carry, True)
        dq_ref[...] = dq

    qspec, kspec = _att_specs(T, bq, None)
    sds = jax.ShapeDtypeStruct((T, H * LANES), F32)
    oblk = pl.BlockSpec((bq, LANES), lambda h, i: (i, h))
    kout = pl.BlockSpec((T, LANES), lambda h, i: (0, h))
    return pl.pallas_call(
        body, grid=(H, T // bq),
        in_specs=[qspec(CB_SQ), kspec(CB_SK), kspec(CB_SV), oblk, qspec(do_cb)],
        out_specs=[oblk, kout, kout], out_shape=[sds, sds, sds],
        compiler_params=_cp(("arbitrary", "arbitrary")), name=name)(proj, proj, proj, tot, docat)


def mla_fwd(Q, K, V, name, hosted=None):
    T = Q.shape[0]
    H = MLA_HEADS
    bq, bk = _att_blocks(T)
    scale = (MLA_NOPE + MLA_ROPE) ** -0.5

    def body(q_ref, k_ref, v_ref, o_ref, lse_ref):
        i = pl.program_id(1)
        qb = q_ref[...]
        diag = (i * bq) // bk

        def block(kb, carry, masked):
            acc, m, l = carry
            s = _bdot_nt(qb, _kblock(k_ref, kb, bk)) * scale
            if masked:
                qpos, kpos = _att_pos(i, kb, bq, bk)
                s = jnp.where(kpos <= qpos, s, NEG_BIG)
            m_new = jnp.maximum(m, jnp.max(s, axis=-1, keepdims=True))
            p = jnp.exp(s - m_new)
            corr = jnp.exp(m - m_new)
            acc = corr * acc + _bdot(p, _kblock(v_ref, kb, bk))
            return acc, m_new, corr * l + jnp.sum(p, axis=-1, keepdims=True)

        init = (jnp.zeros((bq, LANES), F32), jnp.full((bq, 1), NEG_BIG, F32), jnp.zeros((bq, 1), F32))
        carry = lax.fori_loop(0, diag, lambda kb, c: block(kb, c, False), init)
        acc, m, l = block(diag, carry, True)
        o_ref[...] = acc / l
        lse_ref[...] = jnp.broadcast_to(m + jnp.log(l), (bq, LANES))

    qspec, kspec = _att_specs(T, bq, None)
    sds = jax.ShapeDtypeStruct((T, H * LANES), F32)
    oblk = pl.BlockSpec((bq, LANES), lambda h, i: (i, h))
    return _hosted_call(
        hosted, body, grid=(H, T // bq), in_specs=[qspec(0), kspec(0), kspec(0)],
        out_specs=[oblk, oblk], out_shape=[sds, sds],
        compiler_params=_cp(("parallel", "parallel")), name=name)(Q, K, V)


def mla_bwd(Q, K, V, o, lse, docat, do_cb, name, hosted=None):
    T = Q.shape[0]
    H = MLA_HEADS
    bq, bk = _att_blocks(T)
    scale = (MLA_NOPE + MLA_ROPE) ** -0.5

    def body(q_ref, k_ref, v_ref, o_ref, lse_ref, do_ref, dq_ref, dk_ref, dv_ref):
        i = pl.program_id(1)

        @pl.when(i == 0)
        def _():
            dk_ref[...] = jnp.zeros_like(dk_ref)
            dv_ref[...] = jnp.zeros_like(dv_ref)

        qb = q_ref[...]
        do = do_ref[...]
        dob = do.astype(BF16)
        delta = jnp.sum(do * o_ref[...], axis=-1, keepdims=True)
        lse = lse_ref[:, 0:1]

        diag = (i * bq) // bk

        def block(kb, dq, masked):
            kblk = _kblock(k_ref, kb, bk)
            s = _bdot_nt(qb, kblk) * scale
            if masked:
                qpos, kpos = _att_pos(i, kb, bq, bk)
                s = jnp.where(kpos <= qpos, s, NEG_BIG)
            p = jnp.exp(s - lse)
            dp = _bdot_nt(dob, _kblock(v_ref, kb, bk))
            ds = (p * (dp - delta) * scale).astype(BF16)
            rows = pl.ds(pl.multiple_of(kb * bk, bk), bk)
            dk_ref[rows, :] += lax.dot_general(ds, qb, TN_DIMS, preferred_element_type=F32)
            dv_ref[rows, :] += lax.dot_general(p.astype(BF16), dob, TN_DIMS, preferred_element_type=F32)
            return dq + jnp.dot(ds, kblk, preferred_element_type=F32)

        dq = lax.fori_loop(0, diag, lambda kb, c: block(kb, c, False), jnp.zeros((bq, LANES), F32))
        dq_ref[...] = block(diag, dq, True)

    qspec, kspec = _att_specs(T, bq, None)
    sds = jax.ShapeDtypeStruct((T, H * LANES), F32)
    oblk = pl.BlockSpec((bq, LANES), lambda h, i: (i, h))
    kout = pl.BlockSpec((T, LANES), lambda h, i: (0, h))
    return _hosted_call(
        hosted, body, grid=(H, T // bq),
        in_specs=[qspec(0), kspec(0), kspec(0), oblk, oblk, qspec(do_cb)],
        out_specs=[oblk, kout, kout], out_shape=[sds, sds, sds],
        compiler_params=_cp(("arbitrary", "arbitrary")), name=name)(Q, K, V, o, lse, docat)


def _tile_heads(t, n):
    return jnp.concatenate([t] * n, axis=1)


def _rope(X, C, Sn, Sp):
    n = X.shape[1]
    return X * C + pltpu.roll(X, n - HALF_ROPE, 1) * Sn + pltpu.roll(X, HALF_ROPE, 1) * Sp


def _rope_t(dO, C, Sn, Sp):
    n = dO.shape[1]
    return dO * C + pltpu.roll(dO * Sn, HALF_ROPE, 1) + pltpu.roll(dO * Sp, n - HALF_ROPE, 1)


def _rms(x, w):
    r = lax.rsqrt(jnp.mean(x * x, axis=-1, keepdims=True) + RMS_EPS)
    xh = x * r
    return r, xh, xh * w


def _rms_bwd(dn, w, r, xh):
    dxh = dn * w
    return r * (dxh - xh * jnp.mean(dxh * xh, axis=-1, keepdims=True)), jnp.sum(dn * xh, axis=0, keepdims=True)


def _mla_pre_specs(T, tm):
    KV = MLA_KV_RANK
    QR = MLA_Q_RANK
    W = MLA_HEADS * LANES
    full = lambda shape: pl.BlockSpec(shape, lambda i: (0, 0))
    specs = [pl.BlockSpec((tm, QR), lambda i: (i, CB_MQ * LANES // QR)),
             pl.BlockSpec((tm, 2 * LANES), lambda i: (i, CB_MKV // 2)),
             full((1, QR)), full((1, KV))]
    rope = [pl.BlockSpec((tm, LANES), lambda i: (i, 0))] * 3
    return specs, rope, full, W


def mla_pre_fwd(proj, wq, wkv, wuq, wuk, wuv, ropeC, ropeSn, ropeSp, name):
    T = proj.shape[0]
    tm = _pick(T, 512, 16)
    KV = MLA_KV_RANK
    H = MLA_HEADS

    def body(mq_ref, mkv_ref, wq_ref, wkv_ref, wuq_ref, wuk_ref, wuv_ref, c_ref, sn_ref, sp_ref, Q_ref, K_ref, V_ref):
        C, Sn, Sp = (_tile_heads(t[...], H) for t in (c_ref, sn_ref, sp_ref))
        _, _, qn = _rms(mq_ref[...], wq_ref[...])
        Q_ref[...] = _rope(_bdot(qn, wuq_ref[...]), C, Sn, Sp).astype(BF16)
        mkv = mkv_ref[...]
        _, _, kvn = _rms(mkv[:, :KV], wkv_ref[...])
        kr = pltpu.roll(mkv[:, KV:], MLA_NOPE, 1)
        K_ref[...] = _rope(_bdot(kvn, wuk_ref[...]) + _tile_heads(kr, H), C, Sn, Sp).astype(BF16)
        V_ref[...] = _bdot(kvn, wuv_ref[...]).astype(BF16)

    specs, rope, full, W = _mla_pre_specs(T, tm)
    oblk = pl.BlockSpec((tm, W), lambda i: (i, 0))
    sds = jax.ShapeDtypeStruct((T, W), BF16)
    return pl.pallas_call(
        body, grid=(T // tm,),
        in_specs=specs + [full(wuq.shape), full(wuk.shape), full(wuv.shape)] + rope,
        out_specs=[oblk, oblk, oblk], out_shape=[sds, sds, sds],
        compiler_params=_cp(("parallel",)), name=name)(proj, proj, wq, wkv, wuq, wuk, wuv, ropeC, ropeSn, ropeSp)


def mla_pre_bwd(proj, wq, wkv, wuq, wuk, wuv, ropeC, ropeSn, ropeSp, dQ, dK, dV, name):
    T = proj.shape[0]
    tm = _pick(T, 512, 16)
    KV = MLA_KV_RANK
    H = MLA_HEADS

    def body(mq_ref, mkv_ref, wq_ref, wkv_ref, wuq_ref, wuk_ref, wuv_ref,
             c_ref, sn_ref, sp_ref, dQ_ref, dK_ref, dV_ref,
             dmq_ref, dmkv_ref, dwuq_ref, dwuk_ref, dwuv_ref, dwq_ref, dwkv_ref):
        i = pl.program_id(0)

        @pl.when(i == 0)
        def _():
            for ref in (dwuq_ref, dwuk_ref, dwuv_ref, dwq_ref, dwkv_ref):
                ref[...] = jnp.zeros_like(ref)

        C, Sn, Sp = (_tile_heads(t[...], H) for t in (c_ref, sn_ref, sp_ref))
        rq, xq, qn = _rms(mq_ref[...], wq_ref[...])
        mkv = mkv_ref[...]
        rkv, xkv, kvn = _rms(mkv[:, :KV], wkv_ref[...])
        dqf = _rope_t(dQ_ref[...], C, Sn, Sp)
        dkf = _rope_t(dK_ref[...], C, Sn, Sp)
        dv = dV_ref[...]
        dwuq_ref[...] += _bdot_tn(qn, dqf)
        dwuk_ref[...] += _bdot_tn(kvn, dkf)
        dwuv_ref[...] += _bdot_tn(kvn, dv)
        dmq, dwq = _rms_bwd(_bdot_nt(dqf, wuq_ref[...]), wq_ref[...], rq, xq)
        dckv, dwkv = _rms_bwd(_bdot_nt(dkf, wuk_ref[...]) + _bdot_nt(dv, wuv_ref[...]), wkv_ref[...], rkv, xkv)
        dwq_ref[...] += dwq
        dwkv_ref[...] += dwkv
        dmq_ref[...] = dmq
        dkr = dkf[:, 0:LANES]
        for h in range(1, H):
            dkr = dkr + dkf[:, h * LANES:(h + 1) * LANES]
        dkr = pltpu.roll(dkr, LANES - MLA_NOPE, 1)
        dkr = jnp.where(_iota2(dkr.shape, 1) < MLA_ROPE, dkr, 0.0)
        dmkv_ref[...] = jnp.concatenate([dckv, dkr], axis=1)

    specs, rope, full, W = _mla_pre_specs(T, tm)
    wide = pl.BlockSpec((tm, W), lambda i: (i, 0))
    return pl.pallas_call(
        body, grid=(T // tm,),
        in_specs=specs + [full(w.shape) for w in (wuq, wuk, wuv)] + rope + [wide, wide, wide],
        out_specs=[pl.BlockSpec((tm, MLA_Q_RANK), lambda i: (i, 0)), pl.BlockSpec((tm, 2 * LANES), lambda i: (i, 0)),
                   full(wuq.shape), full(wuk.shape), full(wuv.shape), full((1, MLA_Q_RANK)), full((1, KV))],
        out_shape=[jax.ShapeDtypeStruct((T, MLA_Q_RANK), F32), jax.ShapeDtypeStruct((T, 2 * LANES), F32),
                   jax.ShapeDtypeStruct(wuq.shape, F32), jax.ShapeDtypeStruct(wuk.shape, F32),
                   jax.ShapeDtypeStruct(wuv.shape, F32), jax.ShapeDtypeStruct((1, MLA_Q_RANK), F32),
                   jax.ShapeDtypeStruct((1, KV), F32)],
        compiler_params=_cp(("arbitrary",)), name=name)(
            proj, proj, wq, wkv, wuq, wuk, wuv, ropeC, ropeSn, ropeSp, dQ, dK, dV)


def all_gather(shards, name):
    n = len(shards)

    def body(*refs):
        x_refs, out_refs = refs[:n], refs[n:2 * n]
        send_sems, recv_sems, local_sems = refs[2 * n:]
        x, y, c = _place()
        me, sibling = (x, y, c), (x, y, 1 - c)
        chips = [(1 - x, y), (x, 1 - y), (1 - x, 1 - y)]

        def slot(a, px, py, pc):
            return out_refs[a].at[4 * px + 2 * py + pc]

        def copy(a, k, block, to, src=None):
            return pltpu.make_async_remote_copy(
                src_ref=slot(a, *block) if src is None else src, dst_ref=slot(a, *block),
                send_sem=send_sems.at[a, k], recv_sem=recv_sems.at[a, k], device_id=to, device_id_type=MESH)

        mine = [pltpu.make_async_copy(x_refs[a], slot(a, *me), local_sems.at[a]) for a in range(n)]
        first = []
        for a in range(n):
            mine[a].start()
            first.append(copy(a, 0, me, sibling, src=x_refs[a]))
            first += [copy(a, 1 + j, me, (*chip, c), src=x_refs[a]) for j, chip in enumerate(chips)]
        for cp in first:
            cp.start()
        passed = []
        for j, chip in enumerate(chips):
            for a in range(n):
                copy(a, 1 + j, (*chip, c), me).wait_recv()
                passed.append(copy(a, 4 + j, (*chip, c), sibling))
                passed[-1].start()
        for a in range(n):
            copy(a, 0, sibling, me).wait_recv()
            for j, chip in enumerate(chips):
                copy(a, 4 + j, (*chip, 1 - c), me).wait_recv()
        for cp in first + passed:
            cp.wait_send()
        for cp in mine:
            cp.wait()

    return pl.pallas_call(
        body, out_shape=[jax.ShapeDtypeStruct((N_DEV,) + s.shape, s.dtype) for s in shards],
        in_specs=[ANY] * n, out_specs=[ANY] * n,
        scratch_shapes=[pltpu.SemaphoreType.DMA((n, 7)), pltpu.SemaphoreType.DMA((n, 7)), pltpu.SemaphoreType.DMA((n,))],
        name=name)(*shards)


def reduce_adamw(parts, w, m, v, name):
    L = len(parts)
    n, Rl, C = parts[0].shape
    R = w.shape[0]
    assert R == L * Rl
    tr = Rl if Rl * C <= 256 * 1024 else _pick(Rl, 256, 16)
    nr = Rl // tr

    def body(*refs):
        p_refs = refs[:L]
        w_ref, m_ref, v_ref, g_ref, d_ref, nm_ref, nv_ref, sum_ref = refs[L:]
        grp = pl.program_id(0)
        for j in range(L):
            @pl.when(grp == j)
            def _(j=j):
                acc = p_refs[j][0].astype(F32)
                for s in range(1, n):
                    acc = acc + p_refs[j][s].astype(F32)
                sum_ref[...] = acc

        g_ = sum_ref[...]
        m_ = ADAM_B1 * m_ref[...] + (1.0 - ADAM_B1) * g_
        v_ = ADAM_B2 * v_ref[...] + (1.0 - ADAM_B2) * (g_ * g_)
        m_hat = m_ / (1.0 - ADAM_B1 ** ADAM_STEP)
        v_hat = v_ / (1.0 - ADAM_B2 ** ADAM_STEP)
        g_ref[...] = g_
        d_ref[...] = -ADAM_LR * (m_hat / (jnp.sqrt(v_hat) + ADAM_EPS) + ADAM_WD * w_ref[...])
        nm_ref[...] = m_
        nv_ref[...] = v_

    blk = pl.BlockSpec((tr, C), lambda l, r: (l * nr + r, 0))
    sds = jax.ShapeDtypeStruct((R, C), F32)
    p_specs = [pl.BlockSpec((n, tr, C), lambda l, r, j=j: (0, jnp.where(l == j, r, 0), 0)) for j in range(L)]
    return pl.pallas_call(
        body, grid=(L, nr), in_specs=p_specs + [blk] * 3,
        out_specs=[blk] * 4, out_shape=[sds] * 4, scratch_shapes=[pltpu.VMEM((tr, C), F32)],
        compiler_params=_cp(("arbitrary", "arbitrary")), name=name)(*parts, w, m, v)


SHARDED = {"ffa_w_in": (2, BF16), "ffa_w_out": (1, BF16), "mix_w_in": (2, BF16), "mla_w_uq": (2, BF16),
           "mla_w_ukv": (2, BF16), "mix_w_o": (1, BF16), "ffb_w_in": (2, BF16), "ffb_w_out": (1, BF16),
           "ple_w_gate": (1, BF16), "ple_w_proj": (2, BF16), "gdn_conv_w": (2, F32), "ln_g": (2, F32), "ln_b": (2, F32)}
FFN_SLOT = ("ffa_w_in", "ffa_w_out", "ffb_w_in", "ffb_w_out")
REPLICATED = ("gdn_a_log", "gdn_dt_bias", "gdn_norm_w", "mla_q_norm_w", "mla_kv_norm_w")
WEIGHTS = ("ffa_w_in", "ffa_w_out", "mix_w_in", "gdn_conv_w", "gdn_a_log", "gdn_dt_bias", "gdn_norm_w", "mla_q_norm_w",
           "mla_kv_norm_w", "mla_w_uq", "mla_w_ukv", "mix_w_o", "ffb_w_in", "ffb_w_out", "ln_g", "ln_b", "ple_w_gate",
           "ple_w_proj")


def _to_slots(full, axis):
    L, a, b = full.shape
    if axis == 2:
        return full.reshape(L, a, N_DEV, b // N_DEV).transpose(2, 0, 1, 3).reshape(N_DEV, L * a, b // N_DEV)
    return full.reshape(L, N_DEV, a // N_DEV, b).transpose(1, 0, 2, 3).reshape(N_DEV, L * a // N_DEV, b)


def _from_slots(slots, shard_shape, axis):
    L, a, b = shard_shape
    t = slots.reshape((N_DEV,) + tuple(shard_shape))
    if axis == 2:
        return t.transpose(1, 2, 0, 3).reshape(L, a, N_DEV * b)
    return t.transpose(1, 0, 2, 3).reshape(L, N_DEV * a, b)


def _view2d(t):
    return t.reshape(-1, t.shape[-1])


def _pad_heads(w, nh):
    K = w.shape[0]
    return jnp.pad(w.reshape(K, nh, GDN_D), ((0, 0), (0, 0), (0, LANES - GDN_D))).reshape(K, nh * LANES)


def _unpad_heads(w, nh):
    K = w.shape[0]
    return w.reshape(K, nh, LANES)[:, :, :GDN_D].reshape(K, nh * GDN_D)


IN_WIDTHS = (512, 512, 512, 512, 8, 8, 256, 256, 256, 256, 160)


def _split_in(w):
    offs = np.cumsum((0,) + IN_WIDTHS)
    return [w[:, int(offs[i]):int(offs[i + 1])] for i in range(len(IN_WIDTHS))]


def _pad_in_proj(w):
    gq, gk, gv, gz, ga, gb, sq, sk, sv, mq, mkv = _split_in(w)
    gab = jnp.pad(jnp.concatenate([ga, gb], axis=1), ((0, 0), (0, LANES - 2 * GDN_HEADS)))
    return jnp.concatenate(
        [gq, gk, gv, gz] + [_pad_heads(t, SB_HEADS) for t in (sq, sk, sv)]
        + [mq, jnp.pad(mkv, ((0, 0), (0, 2 * LANES - mkv.shape[1]))), gab], axis=1)


def _unpad_in_proj(wp):
    c = lambda cb, n: wp[:, cb * LANES:(cb + n) * LANES]
    gab = c(CB_GAB, 1)
    parts = [c(cb, DO_SB) for cb in (CB_GQ, CB_GK, CB_GV, CB_GZ)]
    parts += [gab[:, :GDN_HEADS], gab[:, GDN_HEADS:2 * GDN_HEADS]]
    parts += [_unpad_heads(c(cb, SB_HEADS), SB_HEADS) for cb in (CB_SQ, CB_SK, CB_SV)]
    parts += [c(CB_MQ, 2), c(CB_MKV, 2)[:, :MLA_KV_RANK + MLA_ROPE]]
    return jnp.concatenate(parts, axis=1)


def _pad_lanes(w, width):
    return jnp.pad(w, ((0, 0), (0, width - w.shape[1])))


def _mla_up_pad(w_uq, w_ukv):
    H = MLA_HEADS
    dq = MLA_NOPE + MLA_ROPE
    wuq = jnp.pad(w_uq.reshape(-1, H, dq), ((0, 0), (0, 0), (0, LANES - dq))).reshape(-1, H * LANES)
    kv = w_ukv.reshape(-1, H, MLA_NOPE + MLA_V)
    wuk = jnp.pad(kv[:, :, :MLA_NOPE], ((0, 0), (0, 0), (0, LANES - MLA_NOPE))).reshape(-1, H * LANES)
    wuv = jnp.pad(kv[:, :, MLA_NOPE:], ((0, 0), (0, 0), (0, LANES - MLA_V))).reshape(-1, H * LANES)
    return wuq, wuk, wuv


def _mla_up_unpad(dwuq, dwuk, dwuv):
    H = MLA_HEADS
    dq = MLA_NOPE + MLA_ROPE
    g_uq = dwuq.reshape(-1, H, LANES)[:, :, :dq].reshape(-1, H * dq)
    g_ukv = jnp.concatenate([dwuk.reshape(-1, H, LANES)[:, :, :MLA_NOPE], dwuv.reshape(-1, H, LANES)[:, :, :MLA_V]],
                            axis=2).reshape(-1, H * (MLA_NOPE + MLA_V))
    return g_uq, g_ukv


def _rope_tables(positions):
    inv = 1.0 / (ROPE_BASE ** (jnp.arange(0, MLA_ROPE, 2, dtype=F32) / MLA_ROPE))
    ang = positions.astype(F32)[:, None] * inv
    cos, sin = jnp.cos(ang), jnp.sin(ang)
    T = positions.shape[0]
    one = lambda n: jnp.ones((T, n), F32)
    zero = lambda n: jnp.zeros((T, n), F32)
    tail = LANES - MLA_NOPE - MLA_ROPE
    C = jnp.concatenate([one(MLA_NOPE), cos, cos, one(tail)], axis=1)
    Sn = jnp.concatenate([zero(MLA_NOPE), -sin, zero(HALF_ROPE + tail)], axis=1)
    Sp = jnp.concatenate([zero(MLA_NOPE + HALF_ROPE), sin, zero(tail)], axis=1)
    return C, Sn, Sp


GATHER_FIRST = [("ffa_w_in", 0), ("ffa_w_out", 0)] + [(n, l) for l in range(DEPTH) for n in ("gdn_conv_w", "ln_g", "ln_b")]
GATHER_PLAN = {
    (0, "ffa_fwd"): [("mix_w_in", 0), ("mla_w_uq", 0), ("mla_w_ukv", 0), ("mix_w_o", 0)],
    (0, "in_proj"): [("ple_w_gate", 0), ("ple_w_proj", 0)],
    (0, "gdn_chunk_fwd"): [("ffb_w_in", 0)],
    (0, "sb_fwd"): [("ffb_w_out", 0), ("mix_w_o", 1)],
    (0, "mla_fwd"): [("ffa_w_out", 1)],
    (0, "ffb_fwd"): [("ffa_w_in", 1)],
    (1, "ffa_fwd"): [("mix_w_in", 1)],
    (1, "in_proj"): [("mla_w_uq", 1), ("mla_w_ukv", 1)],
    (1, "gdn_chunk_fwd"): [("ffb_w_in", 1)],
    (1, "sb_fwd"): [("ffb_w_out", 1), ("ple_w_gate", 1), ("ple_w_proj", 1)],
}
SCATTER_PLAN = {
    (1, "gdn_chunk_bwd"): [("ffb_w_in", 1)],
    (1, "gdn_pre_bwd"): [("ffb_w_out", 1), ("ple_w_gate", 1), ("ple_w_proj", 1), ("mix_w_o", 1)],
    (1, "ffa_bwd"): [("mix_w_in", 1), ("mla_w_uq", 1), ("mla_w_ukv", 1), ("gdn_conv_w", 1)],
    (0, "ffb_bwd"): [("ffa_w_in", 1)],
    (0, "gdn_chunk_bwd"): [("ffb_w_in", 0)],
    (0, "gdn_pre_bwd"): [("ffb_w_out", 0), ("ple_w_gate", 0), ("ple_w_proj", 0), ("mix_w_o", 0)],
    (0, "mla_bwd"): [("ffa_w_out", 1), ("ln_g", 1), ("ln_b", 1)],
    (0, "ffa_bwd"): [("mix_w_in", 0), ("mla_w_uq", 0), ("mla_w_ukv", 0), ("gdn_conv_w", 0)],
    (0, "d_ffa_in"): [("ffa_w_out", 0), ("ln_g", 0), ("ln_b", 0)],
}
SCATTER_LAST = [("ffa_w_in", 0)]


class Exchanges:
    def __init__(self, shards):
        self.shards = shards
        self.full = {}
        self.partial = {}
        self.received = {}

    def _block(self, key):
        n, l = key
        return self.shards[n][l].astype(SHARDED[n][1])

    def _absorb_gather(self, keys, results):
        for (n, l), g in zip(keys, results):
            blk = self.shards[n][l]
            self.full[(n, l)] = g if n in FFN_SLOT else _from_slots(g, (1,) + blk.shape, SHARDED[n][0])[0]

    def gather_now(self, keys, name):
        self._absorb_gather(keys, all_gather([self._block(k) for k in keys], name))

    def gather_with(self, layer, tag):
        keys = GATHER_PLAN.get((layer, tag))
        return None if keys is None else (keys, Hosted("gather", [self._block(k) for k in keys]))

    def scatter_with(self, layer, tag):
        keys = SCATTER_PLAN.get((layer, tag))
        return None if keys is None else (keys, Hosted("scatter", [self.partial[k] for k in keys]))

    def done(self, carried):
        if carried is not None:
            keys, hosted = carried
            if hosted.kind == "gather":
                self._absorb_gather(keys, hosted.results)
            else:
                self.received.update(zip(keys, hosted.results))

    def add_grad(self, key, g):
        n, l = key
        self.partial[key] = g if n in FFN_SLOT else _to_slots(g[None], SHARDED[n][0]).astype(SHARDED[n][1])


def _carried(c):
    return None if c is None else c[1]


def _layer_fwd(h0, h0t, p_i, rope, i, ex, rep):
    L = "L%d_" % i
    S = {"h0": h0, "h0t": h0t, "p": p_i}
    W = ex.full
    ln_g = [W[("ln_g", i)][j][None, :] for j in range(3)]
    ln_b = [W[("ln_b", i)][j][None, :] for j in range(3)]
    S["ln_g"] = ln_g
    c = ex.gather_with(i, "ffa_fwd")
    S["h1"], S["xh1"], S["rs1"], S["Ga"], S["Ua"], S["h1b"], S["h1t"] = ffn_fwd(
        h0, W[("ffa_w_in", i)], W[("ffa_w_out", i)], ln_g[0], ln_b[0], L + "ffa_fwd", hosted=_carried(c))
    ex.done(c)
    S["win"] = _pad_in_proj(W[("mix_w_in", i)])
    c = ex.gather_with(i, "in_proj")
    S["proj"] = mm_nn(S["h1b"], S["win"], L + "in_proj", hosted=_carried(c))
    ex.done(c)
    S["conv"] = W[("gdn_conv_w", i)]
    S["a_row"] = _pad_lanes(rep["gdn_a_log"][i][None, :], LANES)
    S["dt_row"] = _pad_lanes(rep["gdn_dt_bias"][i][None, :], LANES)
    S["nw"] = rep["gdn_norm_w"][i][None, :]
    S["wq"] = rep["mla_q_norm_w"][i][None, :]
    S["wkv"] = rep["mla_kv_norm_w"][i][None, :]
    S["qkv"] = gdn_pre_fwd(S["proj"], S["conv"], L + "gdn_pre_fwd")
    c = ex.gather_with(i, "gdn_chunk_fwd")
    S["o_gdn"], S["opre"], S["Tm"], S["Sin"] = gdn_chunk_fwd(
        S["qkv"], S["proj"], S["a_row"], S["dt_row"], S["nw"], L + "gdn_chunk_fwd", hosted=_carried(c))
    ex.done(c)
    c = ex.gather_with(i, "sb_fwd")
    S["o_sb"], S["tot"] = sb_fwd(S["proj"], L + "sb_fwd", hosted=_carried(c))
    ex.done(c)
    S["wuq"], S["wuk"], S["wuv"] = _mla_up_pad(W[("mla_w_uq", i)], W[("mla_w_ukv", i)])
    S["Q"], S["K"], S["V"] = mla_pre_fwd(S["proj"], S["wq"], S["wkv"], S["wuq"], S["wuk"], S["wuv"], *rope, L + "mla_pre_fwd")
    c = ex.gather_with(i, "mla_fwd")
    S["o_mla"], S["lse"] = mla_fwd(S["Q"], S["K"], S["V"], L + "mla_fwd", hosted=_carried(c))
    ex.done(c)
    wo = W[("mix_w_o", i)]
    wo_att = wo[GDN_W:].reshape(-1, GDN_D, wo.shape[1])
    S["wo"] = jnp.concatenate(
        [wo[:GDN_W], jnp.pad(wo_att, ((0, 0), (0, LANES - GDN_D), (0, 0))).reshape(-1, wo.shape[1])], axis=0)
    S["h2"], S["xh2"], S["rs2"], _, S["h2t"] = proj_res_ln([S["o_gdn"], S["o_sb"], S["o_mla"]], S["wo"], S["h1"],
                                                        ln_g[1], ln_b[1], L + "out_proj")
    c = ex.gather_with(i, "ffb_fwd")
    S["h3"], S["xh3"], S["rs3"], S["Gb"], S["Ub"], h3b, _ = ffn_fwd(
        S["h2"], W[("ffb_w_in", i)], W[("ffb_w_out", i)], ln_g[2], ln_b[2], L + "ffb_fwd", hosted=_carried(c))
    ex.done(c)
    h4, S["a"], S["e"], h4t = ple_fwd(S["h3"], h3b, p_i, W[("ple_w_gate", i)], W[("ple_w_proj", i)], L + "ple_fwd")
    return h4, h4t, S


def _layer_bwd(dh4, S, rope, i, ex):
    L = "L%d_" % i
    W = ex.full
    Grep = {}
    dh3, da, de = ple_bwd(dh4, S["a"], S["e"], W[("ple_w_gate", i)], L + "ple_bwd")
    ex.add_grad(("ple_w_gate", i), mm_tn(S["h3"], da, L + "d_ple_gate"))
    ex.add_grad(("ple_w_proj", i), mm_tn(S["p"], de, L + "d_ple_proj"))
    dz3, dg2, db2 = ln_bwd(dh3, S["xh3"], S["rs3"], S["ln_g"][2], L + "ln3_bwd")
    c = ex.scatter_with(i, "ffb_bwd")
    dh2, dGb, dUb, actb = ffn_bwd(dz3, S["Gb"], S["Ub"], W[("ffb_w_in", i)], W[("ffb_w_out", i)], L + "ffb_bwd",
                                  hosted=_carried(c))
    ex.done(c)
    ex.add_grad(("ffb_w_in", i), ffn_dw_in(S["h2t"], dGb, dUb, L + "d_ffb_in"))
    ex.add_grad(("ffb_w_out", i), ffn_dw_out(actb, dz3, L + "d_ffb_out"))
    dz2, dg1, db1 = ln_bwd(dh2, S["xh2"], S["rs2"], S["ln_g"][1], L + "ln2_bwd")
    docat = mm_nn(dz2, S["wo"], L + "d_ocat", b_transposed=True)
    dwo_att = jnp.concatenate([mm_tn(S["o_sb"], dz2, L + "d_wo_sb"), mm_tn(S["o_mla"], dz2, L + "d_wo_mla")], axis=0)
    dwo_att = dwo_att.reshape(-1, LANES, dwo_att.shape[1])[:, :GDN_D, :].reshape(-1, dwo_att.shape[1])
    ex.add_grad(("mix_w_o", i), jnp.concatenate([mm_tn(S["o_gdn"], dz2, L + "d_wo_gdn"), dwo_att], axis=0))
    c = ex.scatter_with(i, "gdn_chunk_bwd")
    dqkv, dgz, dgab, d_alog, d_dt, d_nw = gdn_chunk_bwd(S["qkv"], S["proj"], S["a_row"], S["dt_row"], S["nw"],
                                                        S["opre"], S["Tm"], S["Sin"], docat, L + "gdn_chunk_bwd",
                                                        hosted=_carried(c))
    ex.done(c)
    c = ex.scatter_with(i, "gdn_pre_bwd")
    dc, dconv = gdn_pre_bwd_pointwise(S["proj"], S["conv"], dqkv, L + "gdn_pre_bwd", hosted=_carried(c))
    ex.done(c)
    dxqkv = gdn_pre_bwd_conv(dc, S["conv"], L + "gdn_conv_bwd")
    ex.add_grad(("gdn_conv_w", i), dconv)
    Grep["gdn_a_log"], Grep["gdn_dt_bias"], Grep["gdn_norm_w"] = d_alog[0, :GDN_HEADS], d_dt[0, :GDN_HEADS], d_nw[0]
    dsq, dsk, dsv = sb_bwd(S["proj"], S["tot"], docat, DO_SB, L + "sb_bwd")
    c = ex.scatter_with(i, "mla_bwd")
    dQ, dK, dV = mla_bwd(S["Q"], S["K"], S["V"], S["o_mla"], S["lse"], docat, DO_MLA, L + "mla_bwd",
                         hosted=_carried(c))
    ex.done(c)
    dmq, dmkv, dwuq, dwuk, dwuv, dwq, dwkv = mla_pre_bwd(
        S["proj"], S["wq"], S["wkv"], S["wuq"], S["wuk"], S["wuv"], *rope, dQ, dK, dV, L + "mla_pre_bwd")
    g_uq, g_ukv = _mla_up_unpad(dwuq, dwuk, dwuv)
    ex.add_grad(("mla_w_uq", i), g_uq)
    ex.add_grad(("mla_w_ukv", i), g_ukv)
    Grep["mla_q_norm_w"], Grep["mla_kv_norm_w"] = dwq[0], dwkv[0]
    dproj = jnp.concatenate([dxqkv, dgz, dsq, dsk, dsv, dmq, dmkv, dgab], axis=1).astype(BF16)
    ex.add_grad(("mix_w_in", i),
                _unpad_in_proj(mm_tn(S["h1t"], dproj, L + "d_in_proj", a_transposed=True)))
    dh1 = mm_nn(dproj, S["win"], L + "d_h1", res=dz2, res_scale=ALPHA, b_transposed=True)
    dz1, dg0, db0 = ln_bwd(dh1, S["xh1"], S["rs1"], S["ln_g"][0], L + "ln1_bwd")
    c = ex.scatter_with(i, "ffa_bwd")
    dh0, dGa, dUa, acta = ffn_bwd(dz1, S["Ga"], S["Ua"], W[("ffa_w_in", i)], W[("ffa_w_out", i)], L + "ffa_bwd",
                                  hosted=_carried(c))
    ex.done(c)
    ex.add_grad(("ffa_w_out", i), ffn_dw_out(acta, dz1, L + "d_ffa_out"))
    ex.add_grad(("ln_g", i), jnp.concatenate([dg0, dg1, dg2], axis=0))
    ex.add_grad(("ln_b", i), jnp.concatenate([db0, db1, db2], axis=0))
    c = ex.scatter_with(i, "d_ffa_in")
    ex.add_grad(("ffa_w_in", i), ffn_dw_in(S["h0t"], dGa, dUa, L + "d_ffa_in", hosted=_carried(c)))
    ex.done(c)
    return dh0, Grep


def _local_step(x, p, positions, target, ex, rep):
    assert DEPTH == 2
    rope = _rope_tables(positions)
    h, ht, saved = x, x.T.astype(BF16), []
    for i in range(DEPTH):
        h, ht, S = _layer_fwd(h, ht, p[i], rope, i, ex, rep)
        saved.append(S)
    loss, dh = loss_head(h, target, "loss_head")
    grads = [None] * DEPTH
    for i in reversed(range(DEPTH)):
        dh, grads[i] = _layer_bwd(dh, saved[i], rope, i, ex)
    return loss, dh, {n: jnp.stack([grads[i][n] for i in range(DEPTH)]) for n in REPLICATED}


def kernel(x, p, positions, ffa_w_in, ffa_w_out, mix_w_in, gdn_conv_w, gdn_a_log, gdn_dt_bias, gdn_norm_w, mla_q_norm_w, mla_kv_norm_w, mla_w_uq, mla_w_ukv, mix_w_o, ffb_w_in, ffb_w_out, ln_g, ln_b, ple_w_gate, ple_w_proj, loss_target, m_ffa_w_in, m_ffa_w_out, m_mix_w_in, m_gdn_conv_w, m_gdn_a_log, m_gdn_dt_bias, m_gdn_norm_w, m_mla_q_norm_w, m_mla_kv_norm_w, m_mla_w_uq, m_mla_w_ukv, m_mix_w_o, m_ffb_w_in, m_ffb_w_out, m_ln_g, m_ln_b, m_ple_w_gate, m_ple_w_proj, v_ffa_w_in, v_ffa_w_out, v_mix_w_in, v_gdn_conv_w, v_gdn_a_log, v_gdn_dt_bias, v_gdn_norm_w, v_mla_q_norm_w, v_mla_kv_norm_w, v_mla_w_uq, v_mla_w_ukv, v_mix_w_o, v_ffb_w_in, v_ffb_w_out, v_ln_g, v_ln_b, v_ple_w_gate, v_ple_w_proj):
    given = dict(locals())
    shards = {n: given[n] for n in WEIGHTS}
    ex = Exchanges({n: shards[n] for n in SHARDED})
    ex.gather_now(GATHER_FIRST, "gather_first")
    loss, grad_x, Grep = _local_step(x[0], p[:, 0], positions[0], loss_target[0], ex, {n: shards[n] for n in REPLICATED})
    loss = lax.psum(loss[0, 0], ("x", "y", "c"))
    last = Hosted("scatter", [ex.partial[k] for k in SCATTER_LAST])
    ex.received.update(zip(SCATTER_LAST, exchange_now(last, "scatter_last")))
    rep_received = dict(zip(REPLICATED, all_gather([Grep[n] for n in REPLICATED], "gather_replicated_grads")))
    grad, delta, new_m, new_v = {}, {}, {}, {}
    for n in WEIGHTS:
        shape = shards[n].shape
        parts = [rep_received[n]] if n in REPLICATED else [ex.received[(n, l)] for l in range(DEPTH)]
        if parts[0].shape[1] % 8:
            parts = [jnp.concatenate(parts, axis=1)]
        outs = reduce_adamw(parts, _view2d(shards[n]), _view2d(given["m_" + n]), _view2d(given["v_" + n]),
                            "adamw_" + n)
        grad[n], delta[n], new_m[n], new_v[n] = (t.reshape(shape) for t in outs)
    return (loss, grad_x[None], *[grad[n] for n in WEIGHTS], *[delta[n] for n in WEIGHTS],
            *[new_m[n] for n in WEIGHTS], *[new_v[n] for n in WEIGHTS])
```

```python
import functools
import numpy as np
import jax
import jax.numpy as jnp
from jax import lax
from jax.experimental import pallas as pl
from jax.experimental.pallas import tpu as pltpu

F32 = jnp.float32
BF16 = jnp.bfloat16

DEPTH = 2
LN_EPS = 1e-5
RMS_EPS = 1e-6
ALPHA = (2 * DEPTH) ** 0.25
GDN_HEADS, GDN_D, GDN_CONV, GDN_CHUNK = 8, 64, 4, 64
SB_HEADS, SB_DIM = 4, 64
MLA_HEADS, MLA_NOPE, MLA_ROPE, MLA_V, MLA_Q_RANK, MLA_KV_RANK = 4, 64, 32, 64, 256, 128
ROPE_BASE = 10000.0
HALF_ROPE = MLA_ROPE // 2
LANES = 128
N_DEV = 8
ADAM_LR, ADAM_B1, ADAM_B2, ADAM_EPS, ADAM_WD, ADAM_STEP = 0.001, 0.9, 0.999, 1e-08, 0.01, 10

CB_GQ, CB_GK, CB_GV, CB_GZ = 0, 4, 8, 12
CB_SQ, CB_SK, CB_SV = 16, 20, 24
CB_MQ, CB_MKV, CB_GAB = 28, 30, 32
PROJ_W = 33 * LANES
GDN_W = GDN_HEADS * GDN_D
DO_SB = GDN_W // LANES
DO_MLA = DO_SB + SB_HEADS
VMEM_LIMIT = 56 * 1024 * 1024
MM_TILE = 1536

NT_DIMS = (((1,), (1,)), ((), ()))
TN_DIMS = (((0,), (0,)), ((), ()))


def _cp(sem):
    return pltpu.CompilerParams(dimension_semantics=sem, vmem_limit_bytes=VMEM_LIMIT)


def _bdot(a, b):
    return jnp.dot(a.astype(BF16), b.astype(BF16), preferred_element_type=F32)


def _bdot_nt(a, b):
    return lax.dot_general(a.astype(BF16), b.astype(BF16), NT_DIMS, preferred_element_type=F32)


def _bdot_tn(a, b):
    return lax.dot_general(a.astype(BF16), b.astype(BF16), TN_DIMS, preferred_element_type=F32)


def _split2(a):
    hi = a.astype(BF16)
    lo = (a - hi.astype(F32)).astype(BF16)
    return hi, lo


def _ones_dot_left(ones_bf16, x):
    hi = x.astype(BF16)
    r1 = x - hi.astype(F32)
    mid = r1.astype(BF16)
    lo = (r1 - mid.astype(F32)).astype(BF16)
    d = functools.partial(jnp.dot, preferred_element_type=F32)
    return d(ones_bf16, hi) + d(ones_bf16, mid) + d(ones_bf16, lo)


def _iota2(shape, dim):
    return lax.broadcasted_iota(jnp.int32, shape, dim)


def _sigmoid(x):
    return 0.5 * jnp.tanh(0.5 * x) + 0.5


def _softplus(x):
    return jnp.maximum(x, 0.0) + jnp.log(1.0 + jnp.exp(-jnp.abs(x)))


def _pick(n, limit, mult):
    if n <= limit:
        return n
    best = None
    for t in range(mult, limit + 1, mult):
        if n % t == 0:
            best = t
    assert best is not None, (n, limit, mult)
    return best


MESH = pl.DeviceIdType.MESH
ANY = pl.BlockSpec(memory_space=pl.ANY)


def _place():
    return lax.axis_index("x"), lax.axis_index("y"), lax.axis_index("c")


def _peer(k):
    x, y, c = _place()
    return (1 - x if k & 4 else x, 1 - y if k & 2 else y, 1 - c if k & 1 else c)


class Hosted:
    def __init__(self, kind, arrays):
        self.kind, self.arrays, self.n, self.results = kind, list(arrays), len(arrays), None

    def out_shapes(self):
        if self.kind == "gather":
            return [jax.ShapeDtypeStruct((N_DEV,) + a.shape, a.dtype) for a in self.arrays]
        return [jax.ShapeDtypeStruct(a.shape, a.dtype) for a in self.arrays]

    def sems(self):
        return [pltpu.SemaphoreType.DMA((self.n, N_DEV - 1)), pltpu.SemaphoreType.DMA((self.n, N_DEV - 1)),
                pltpu.SemaphoreType.DMA((self.n,))]

    def _copies(self, src_refs, dst_refs, send_sems, recv_sems, local_sems):
        x, y, c = _place()
        me = 4 * x + 2 * y + c
        local, remote = [], []
        for a in range(self.n):
            gather = self.kind == "gather"
            local.append(pltpu.make_async_copy(src_refs[a] if gather else src_refs[a].at[me], dst_refs[a].at[me],
                                               local_sems.at[a]))
            for k in range(1, N_DEV):
                px, py, pc = _peer(k)
                remote.append(pltpu.make_async_remote_copy(
                    src_ref=src_refs[a] if gather else src_refs[a].at[4 * px + 2 * py + pc], dst_ref=dst_refs[a].at[me],
                    send_sem=send_sems.at[a, k - 1], recv_sem=recv_sems.at[a, k - 1],
                    device_id=(px, py, pc), device_id_type=MESH))
        return local, remote

    def start(self, *refs):
        local, remote = self._copies(*refs)
        for cp in local + remote:
            cp.start()

    def wait(self, *refs):
        local, remote = self._copies(*refs)
        for cp in remote:
            cp.wait_recv()
        for cp in remote:
            cp.wait_send()
        for cp in local:
            cp.wait()


def _hosted_call(hosted, body, *, grid, in_specs, out_specs, out_shape, scratch_shapes=(), compiler_params, name):
    if hosted is None:
        return pl.pallas_call(body, grid=grid, in_specs=in_specs, out_specs=out_specs, out_shape=out_shape,
                              scratch_shapes=scratch_shapes, compiler_params=compiler_params, name=name)
    single = not isinstance(out_shape, (list, tuple))
    o_specs = [out_specs] if single else list(out_specs)
    o_shape = [out_shape] if single else list(out_shape)
    n_in, n_out, n_scr, n = len(in_specs), len(o_specs), len(scratch_shapes), hosted.n

    def wrapped(*refs):
        ins, c_in = refs[:n_in], refs[n_in:n_in + n]
        outs, c_out = refs[n_in + n:n_in + n + n_out], refs[n_in + n + n_out:n_in + 2 * n + n_out]
        rest = refs[n_in + 2 * n + n_out:]
        scr, sems = rest[:n_scr], rest[n_scr:]
        ids = [pl.program_id(ax) for ax in range(len(grid))]
        first = functools.reduce(jnp.logical_and, [i == 0 for i in ids])
        last = functools.reduce(jnp.logical_and, [i == g - 1 for i, g in zip(ids, grid)])

        @pl.when(first)
        def _():
            hosted.start(c_in, c_out, *sems)

        body(*ins, *outs, *scr)

        @pl.when(last)
        def _():
            hosted.wait(c_in, c_out, *sems)

    call = pl.pallas_call(
        wrapped, grid=grid, in_specs=list(in_specs) + [ANY] * n, out_specs=o_specs + [ANY] * n,
        out_shape=o_shape + hosted.out_shapes(), scratch_shapes=list(scratch_shapes) + hosted.sems(),
        compiler_params=_cp(("arbitrary",) * len(grid)), name=name)

    def run(*args):
        outs = call(*args, *hosted.arrays)
        hosted.results = list(outs[n_out:])
        return outs[0] if single else list(outs[:n_out])

    return run


def exchange_now(hosted, name):
    n = hosted.n

    def body(*refs):
        src, dst, sems = refs[:n], refs[n:2 * n], refs[2 * n:]
        hosted.start(src, dst, *sems)
        hosted.wait(src, dst, *sems)

    return pl.pallas_call(body, out_shape=hosted.out_shapes(), in_specs=[ANY] * n, out_specs=[ANY] * n,
                          scratch_shapes=hosted.sems(), name=name)(*hosted.arrays)


def mm_nn(a, b, name, out_dtype=F32, res=None, res_scale=1.0, b_transposed=False, hosted=None):
    M, K = a.shape
    N = b.shape[0] if b_transposed else b.shape[1]
    tm, tn, tk = _pick(M, 512, 16), _pick(N, MM_TILE, LANES), _pick(K, MM_TILE, LANES)
    nk = K // tk
    has_res = res is not None
    dot = _bdot_nt if b_transposed else _bdot

    def body(*refs):
        if has_res:
            a_ref, b_ref, r_ref, o_ref, acc_ref = refs
        else:
            a_ref, b_ref, o_ref, acc_ref = refs
        k = pl.program_id(2)

        @pl.when(k == 0)
        def _():
            acc_ref[...] = jnp.zeros_like(acc_ref)

        acc_ref[...] += dot(a_ref[...], b_ref[...])

        @pl.when(k == nk - 1)
        def _():
            out = acc_ref[...]
            if has_res:
                out = out + res_scale * r_ref[...]
            o_ref[...] = out.astype(o_ref.dtype)

    b_spec = pl.BlockSpec((tn, tk), lambda i, j, k: (j, k)) if b_transposed else pl.BlockSpec((tk, tn), lambda i, j, k: (k, j))
    in_specs = [pl.BlockSpec((tm, tk), lambda i, j, k: (i, k)), b_spec]
    args = [a, b]
    if has_res:
        in_specs.append(pl.BlockSpec((tm, tn), lambda i, j, k: (i, j)))
        args.append(res)
    return _hosted_call(
        hosted, body, grid=(M // tm, N // tn, nk), in_specs=in_specs,
        out_specs=pl.BlockSpec((tm, tn), lambda i, j, k: (i, j)),
        out_shape=jax.ShapeDtypeStruct((M, N), out_dtype),
        scratch_shapes=[pltpu.VMEM((tm, tn), F32)],
        compiler_params=_cp(("parallel", "parallel", "arbitrary")), name=name)(*args)


def mm_tn(a, b, name, out_dtype=F32, a_transposed=False):
    K, T = a.shape if a_transposed else a.shape[::-1]
    _, N = b.shape
    tk = K if a_transposed else _pick(K, 512, LANES)
    tn, tt = _pick(N, MM_TILE, LANES), _pick(T, 512, LANES)
    nt = T // tt

    def body(a_ref, b_ref, o_ref, acc_ref):
        t = pl.program_id(2)

        @pl.when(t == 0)
        def _():
            acc_ref[...] = jnp.zeros_like(acc_ref)

        if a_transposed:
            acc_ref[...] += _bdot(a_ref[:, pl.ds(pl.multiple_of(t * tt, tt), tt)], b_ref[...])
        else:
            acc_ref[...] += _bdot_tn(a_ref[...], b_ref[...])

        @pl.when(t == nt - 1)
        def _():
            o_ref[...] = acc_ref[...].astype(o_ref.dtype)

    a_spec = pl.BlockSpec((K, T), lambda i, j, t: (0, 0)) if a_transposed else pl.BlockSpec((tt, tk), lambda i, j, t: (t, i))
    return pl.pallas_call(
        body, grid=(K // tk, N // tn, nt),
        in_specs=[a_spec, pl.BlockSpec((tt, tn), lambda i, j, t: (t, j))],
        out_specs=pl.BlockSpec((tk, tn), lambda i, j, t: (i, j)),
        out_shape=jax.ShapeDtypeStruct((K, N), out_dtype),
        scratch_shapes=[pltpu.VMEM((tk, tn), F32)],
        compiler_params=_cp(("parallel", "parallel", "arbitrary")), name=name)(a, b)


def _ln_apply(z, g, b):
    mu = jnp.mean(z, axis=-1, keepdims=True)
    zc = z - mu
    var = jnp.mean(zc * zc, axis=-1, keepdims=True)
    rstd = lax.rsqrt(var + LN_EPS)
    xhat = zc * rstd
    return xhat * g + b, xhat, rstd


def ln_bwd(dout, xhat, rstd, g, name):
    T, D = dout.shape
    tm = _pick(T, 512, 8)

    def body(do_ref, xh_ref, rs_ref, g_ref, dz_ref, dg_ref, db_ref):
        i = pl.program_id(0)

        @pl.when(i == 0)
        def _():
            dg_ref[...] = jnp.zeros_like(dg_ref)
            db_ref[...] = jnp.zeros_like(db_ref)

        do = do_ref[...]
        xh = xh_ref[...]
        dxh = do * g_ref[...]
        m1 = jnp.mean(dxh, axis=-1, keepdims=True)
        m2 = jnp.mean(dxh * xh, axis=-1, keepdims=True)
        dz_ref[...] = rs_ref[...] * (dxh - m1 - xh * m2)
        dg_ref[...] += jnp.sum(do * xh, axis=0, keepdims=True)
        db_ref[...] += jnp.sum(do, axis=0, keepdims=True)

    row = pl.BlockSpec((tm, D), lambda i: (i, 0))
    vec = pl.BlockSpec((1, D), lambda i: (0, 0))
    return pl.pallas_call(
        body, grid=(T // tm,),
        in_specs=[row, row, pl.BlockSpec((tm, 1), lambda i: (i, 0)), vec],
        out_specs=[row, vec, vec],
        out_shape=[jax.ShapeDtypeStruct((T, D), F32), jax.ShapeDtypeStruct((1, D), F32), jax.ShapeDtypeStruct((1, D), F32)],
        compiler_params=_cp(("arbitrary",)), name=name)(dout, xhat, rstd, g)


FFN_CHUNKS = N_DEV // 2


def ffn_fwd(h, w_in, w_out, g, b, name, hosted=None):
    T, D = h.shape
    fc = w_in.shape[2]
    half = w_out.shape[1]
    tm = _pick(T, 512, 8)
    nc = FFN_CHUNKS

    def body(h_ref, wg_ref, wu_ref, wo_ref, g_ref, b_ref, out_ref, xh_ref, rs_ref, G_ref, U_ref, ob_ref, obt_ref, acc_ref):
        c = pl.program_id(1)

        @pl.when(c == 0)
        def _():
            acc_ref[...] = jnp.zeros_like(acc_ref)

        hb = h_ref[...].astype(BF16)
        G = jnp.dot(hb, wg_ref[0], preferred_element_type=F32)
        U = jnp.dot(hb, wu_ref[0], preferred_element_type=F32)
        G_ref[0] = G
        U_ref[0] = U
        act = G * _sigmoid(G) * U
        acc_ref[...] += _bdot(act, wo_ref[...].reshape(2 * half, D))

        @pl.when(c == nc - 1)
        def _():
            z = ALPHA * h_ref[...] + 0.5 * acc_ref[...]
            out, xh, rs = _ln_apply(z, g_ref[...], b_ref[...])
            out_ref[...] = out
            ob_ref[...] = out.astype(BF16)
            obt_ref[...] = out.T.astype(BF16)
            xh_ref[...] = xh
            rs_ref[...] = rs

    row = pl.BlockSpec((tm, D), lambda i, c: (i, 0))
    vec = pl.BlockSpec((1, D), lambda i, c: (0, 0))
    cblk = pl.BlockSpec((1, tm, fc), lambda i, c: (c, i, 0))
    csds = jax.ShapeDtypeStruct((nc, T, fc), F32)
    return _hosted_call(
        hosted, body, grid=(T // tm, nc),
        in_specs=[row, pl.BlockSpec((1, D, fc), lambda i, c: (c, 0, 0)),
                  pl.BlockSpec((1, D, fc), lambda i, c: (c + nc, 0, 0)),
                  pl.BlockSpec((2, half, D), lambda i, c: (c, 0, 0)), vec, vec],
        out_specs=[row, row, pl.BlockSpec((tm, 1), lambda i, c: (i, 0)), cblk, cblk, row,
                   pl.BlockSpec((D, tm), lambda i, c: (0, i))],
        out_shape=[jax.ShapeDtypeStruct((T, D), F32), jax.ShapeDtypeStruct((T, D), F32), jax.ShapeDtypeStruct((T, 1), F32),
                   csds, csds, jax.ShapeDtypeStruct((T, D), BF16), jax.ShapeDtypeStruct((D, T), BF16)],
        scratch_shapes=[pltpu.VMEM((tm, D), F32)],
        compiler_params=_cp(("parallel", "arbitrary")), name=name)(h, w_in, w_in, w_out, g, b)


def ffn_bwd(dz, G, U, w_in, w_out, name, hosted=None):
    T, D = dz.shape
    nc, _, fc = G.shape
    half = w_out.shape[1]
    tm = _pick(T, 512, 16)

    def body(dz_ref, G_ref, U_ref, wg_ref, wu_ref, wo_ref, dh_ref, dG_ref, dU_ref, act_ref, acc_ref):
        c = pl.program_id(1)

        @pl.when(c == 0)
        def _():
            acc_ref[...] = jnp.zeros_like(acc_ref)

        dy = (0.5 * dz_ref[...]).astype(BF16)
        dact = _bdot_nt(dy, wo_ref[...].reshape(2 * half, D))
        G = G_ref[0]
        U = U_ref[0]
        s = _sigmoid(G)
        silu = G * s
        dG = (dact * U * (s * (1.0 + G * (1.0 - s)))).astype(BF16)
        dU = (dact * silu).astype(BF16)
        dG_ref[0] = dG
        dU_ref[0] = dU
        act_ref[0] = (silu * U).astype(BF16)
        acc_ref[...] += _bdot_nt(dG, wg_ref[0]) + _bdot_nt(dU, wu_ref[0])

        @pl.when(c == nc - 1)
        def _():
            dh_ref[...] = ALPHA * dz_ref[...] + acc_ref[...]

    row = pl.BlockSpec((tm, D), lambda i, c: (i, 0))
    cblk = pl.BlockSpec((1, tm, fc), lambda i, c: (c, i, 0))
    csds = jax.ShapeDtypeStruct((nc, T, fc), BF16)
    return _hosted_call(
        hosted, body, grid=(T // tm, nc),
        in_specs=[row, cblk, cblk, pl.BlockSpec((1, D, fc), lambda i, c: (c, 0, 0)),
                  pl.BlockSpec((1, D, fc), lambda i, c: (c + nc, 0, 0)),
                  pl.BlockSpec((2, half, D), lambda i, c: (c, 0, 0))],
        out_specs=[row, cblk, cblk, cblk],
        out_shape=[jax.ShapeDtypeStruct((T, D), F32), csds, csds, csds],
        scratch_shapes=[pltpu.VMEM((tm, D), F32)],
        compiler_params=_cp(("parallel", "arbitrary")), name=name)(dz, G, U, w_in, w_in, w_out)


def ffn_dw_in(h_t, dG, dU, name, hosted=None):
    D, T = h_t.shape
    nc, _, fc = dG.shape
    tt = _pick(T, 512, LANES)
    nt = T // tt

    def body(h_ref, dG_ref, dU_ref, o_ref, acc_ref):
        s = pl.program_id(0)
        t = pl.program_id(1)

        @pl.when(t == 0)
        def _():
            acc_ref[...] = jnp.zeros_like(acc_ref)

        hb = h_ref[:, pl.ds(pl.multiple_of(t * tt, tt), tt)]

        @pl.when(s < nc)
        def _():
            acc_ref[...] += jnp.dot(hb, dG_ref[0], preferred_element_type=F32)

        @pl.when(s >= nc)
        def _():
            acc_ref[...] += jnp.dot(hb, dU_ref[0], preferred_element_type=F32)

        @pl.when(t == nt - 1)
        def _():
            o_ref[0] = acc_ref[...].astype(o_ref.dtype)

    return _hosted_call(
        hosted, body, grid=(2 * nc, nt),
        in_specs=[pl.BlockSpec((D, T), lambda s, t: (0, 0)),
                  pl.BlockSpec((1, tt, fc), lambda s, t: (jnp.minimum(s, nc - 1), jnp.where(s < nc, t, nt - 1), 0)),
                  pl.BlockSpec((1, tt, fc), lambda s, t: (jnp.maximum(s - nc, 0), jnp.where(s >= nc, t, 0), 0))],
        out_specs=pl.BlockSpec((1, D, fc), lambda s, t: (s, 0, 0)),
        out_shape=jax.ShapeDtypeStruct((2 * nc, D, fc), BF16),
        scratch_shapes=[pltpu.VMEM((D, fc), F32)],
        compiler_params=_cp(("parallel", "arbitrary")), name=name)(h_t, dG, dU)


def ffn_dw_out(act, dz, name, hosted=None):
    nc, T, fc = act.shape
    D = dz.shape[1]
    half = fc // 2
    tt = _pick(T, 512, 16)
    nt = T // tt

    def body(a_ref, dz_ref, o_ref, acc_ref):
        t = pl.program_id(1)

        @pl.when(t == 0)
        def _():
            acc_ref[...] = jnp.zeros_like(acc_ref)

        acc_ref[...] += _bdot_tn(a_ref[0], dz_ref[...])

        @pl.when(t == nt - 1)
        def _():
            o_ref[...] = (0.5 * acc_ref[...]).reshape(2, half, D).astype(o_ref.dtype)

    return _hosted_call(
        hosted, body, grid=(nc, nt),
        in_specs=[pl.BlockSpec((1, tt, fc), lambda c, t: (c, t, 0)), pl.BlockSpec((tt, D), lambda c, t: (t, 0))],
        out_specs=pl.BlockSpec((2, half, D), lambda c, t: (c, 0, 0)),
        out_shape=jax.ShapeDtypeStruct((2 * nc, half, D), BF16),
        scratch_shapes=[pltpu.VMEM((fc, D), F32)],
        compiler_params=_cp(("parallel", "arbitrary")), name=name)(act, dz)


def proj_res_ln(parts, w, res, g, b, name):
    T, D = res.shape
    tm = _pick(T, 512, 8)
    widths = [p.shape[1] for p in parts]
    offs = [int(sum(widths[:i])) for i in range(len(parts))]
    n = len(parts)

    def body(*refs):
        p_refs = refs[:n]
        w_ref, r_ref, g_ref, b_ref, out_ref, xh_ref, rs_ref, ob_ref, obt_ref = refs[n:]
        acc = ALPHA * r_ref[...]
        for p_ref, o, wd in zip(p_refs, offs, widths):
            acc = acc + _bdot(p_ref[...], w_ref[o:o + wd, :])
        out, xh, rs = _ln_apply(acc, g_ref[...], b_ref[...])
        out_ref[...] = out
        ob_ref[...] = out.astype(BF16)
        obt_ref[...] = out.T.astype(BF16)
        xh_ref[...] = xh
        rs_ref[...] = rs

    row = pl.BlockSpec((tm, D), lambda i: (i, 0))
    vec = pl.BlockSpec((1, D), lambda i: (0, 0))
    return pl.pallas_call(
        body, grid=(T // tm,),
        in_specs=[pl.BlockSpec((tm, wd), lambda i: (i, 0)) for wd in widths]
        + [pl.BlockSpec(w.shape, lambda i: (0, 0)), row, vec, vec],
        out_specs=[row, row, pl.BlockSpec((tm, 1), lambda i: (i, 0)), row, pl.BlockSpec((D, tm), lambda i: (0, i))],
        out_shape=[jax.ShapeDtypeStruct((T, D), F32), jax.ShapeDtypeStruct((T, D), F32), jax.ShapeDtypeStruct((T, 1), F32),
                   jax.ShapeDtypeStruct((T, D), BF16), jax.ShapeDtypeStruct((D, T), BF16)],
        compiler_params=_cp(("parallel",)), name=name)(*parts, w, res, g, b)


def ple_fwd(h, h_b, p, wg, wp, name):
    T, D = h.shape
    P = p.shape[1]
    tm, tn = _pick(T, 512, 16), D

    def body(h_ref, hn_ref, p_ref, wg_ref, wp_ref, out_ref, a_ref, e_ref, ot_ref):
        a = _bdot(h_ref[...], wg_ref[...])
        e = _bdot(p_ref[...], wp_ref[...])
        a_ref[...] = a
        e_ref[...] = e
        out = hn_ref[...] + _sigmoid(a) * e
        out_ref[...] = out
        ot_ref[...] = out.T.astype(BF16)

    blk = pl.BlockSpec((tm, tn), lambda i, j: (i, j))
    sds = jax.ShapeDtypeStruct((T, D), F32)
    return pl.pallas_call(
        body, grid=(T // tm, D // tn),
        in_specs=[pl.BlockSpec((tm, D), lambda i, j: (i, 0)), blk, pl.BlockSpec((tm, P), lambda i, j: (i, 0)),
                  pl.BlockSpec((D, tn), lambda i, j: (0, j)), pl.BlockSpec((P, tn), lambda i, j: (0, j))],
        out_specs=[blk, blk, blk, pl.BlockSpec((D, tm), lambda i, j: (0, i))],
        out_shape=[sds, sds, sds, jax.ShapeDtypeStruct((D, T), BF16)],
        compiler_params=_cp(("parallel", "parallel")), name=name)(h_b, h, p, wg, wp)


def ple_ln_bwd(dout, a, e, wg, xhat, rstd, g, name):
    T, D = dout.shape
    tm = _pick(T, 512, 16)

    def body(do_ref, a_ref, e_ref, wg_ref, xh_ref, rs_ref, g_ref, dz_ref, da_ref, de_ref, dg_ref, db_ref):
        i = pl.program_id(0)

        @pl.when(i == 0)
        def _():
            dg_ref[...] = jnp.zeros_like(dg_ref)
            db_ref[...] = jnp.zeros_like(db_ref)

        do = do_ref[...]
        s = _sigmoid(a_ref[...])
        da = (do * e_ref[...] * s * (1.0 - s)).astype(BF16)
        da_ref[...] = da
        de_ref[...] = (do * s).astype(BF16)
        dh = do + _bdot_nt(da, wg_ref[...])
        xh = xh_ref[...]
        dxh = dh * g_ref[...]
        m1 = jnp.mean(dxh, axis=-1, keepdims=True)
        m2 = jnp.mean(dxh * xh, axis=-1, keepdims=True)
        dz_ref[...] = rs_ref[...] * (dxh - m1 - xh * m2)
        dg_ref[...] += jnp.sum(dh * xh, axis=0, keepdims=True)
        db_ref[...] += jnp.sum(dh, axis=0, keepdims=True)

    row = pl.BlockSpec((tm, D), lambda i: (i, 0))
    vec = pl.BlockSpec((1, D), lambda i: (0, 0))
    return pl.pallas_call(
        body, grid=(T // tm,),
        in_specs=[row, row, row, pl.BlockSpec((D, D), lambda i: (0, 0)), row, pl.BlockSpec((tm, 1), lambda i: (i, 0)), vec],
        out_specs=[row, row, row, vec, vec],
        out_shape=[jax.ShapeDtypeStruct((T, D), F32), jax.ShapeDtypeStruct((T, D), BF16), jax.ShapeDtypeStruct((T, D), BF16),
                   jax.ShapeDtypeStruct((1, D), F32), jax.ShapeDtypeStruct((1, D), F32)],
        compiler_params=_cp(("arbitrary",)), name=name)(dout, a, e, wg, xhat, rstd, g)


def loss_head(y, target, name):
    T, D = y.shape
    tm = _pick(T, 512, 8)

    def body(y_ref, t_ref, loss_ref, dy_ref):
        i = pl.program_id(0)

        @pl.when(i == 0)
        def _():
            loss_ref[...] = jnp.zeros_like(loss_ref)

        err = y_ref[...] - t_ref[...]
        dy_ref[...] = err * (1.0 / D)
        per_tok = jnp.sum(err * err, axis=-1, keepdims=True) * (1.0 / D)
        loss_ref[...] += 0.5 * jnp.sum(per_tok, axis=0, keepdims=True)

    row = pl.BlockSpec((tm, D), lambda i: (i, 0))
    return pl.pallas_call(
        body, grid=(T // tm,), in_specs=[row, row],
        out_specs=[pl.BlockSpec((1, 1), lambda i: (0, 0)), row],
        out_shape=[jax.ShapeDtypeStruct((1, 1), F32), jax.ShapeDtypeStruct((T, D), F32)],
        compiler_params=_cp(("arbitrary",)), name=name)(y, target)


HALO = 8


def _conv_taps(pad_ref, w_ref, tm, base):
    acc = w_ref[0:1, :] * pad_ref[pl.ds(base, tm), :]
    for k in range(1, GDN_CONV):
        acc = acc + w_ref[k:k + 1, :] * pad_ref[pl.ds(base + k, tm), :]
    return acc


GDN_GROUP_W = GDN_W
GDN_PRE_ROWS = 512


def _head_segments():
    head = jnp.arange(GDN_W, dtype=jnp.int32) // GDN_D
    return (head[:, None] == head[None, :]).astype(BF16)


def _head_sums(x, seg):
    hi = x.astype(BF16)
    r1 = x - hi.astype(F32)
    mid = r1.astype(BF16)
    lo = (r1 - mid.astype(F32)).astype(BF16)
    d = functools.partial(jnp.dot, preferred_element_type=F32)
    return d(hi, seg) + d(mid, seg) + d(lo, seg)


def _gdn_pre_common(x_ref, halo_ref, w_ref, seg_ref, pad_ref, tm):
    i = pl.program_id(1)
    grp = pl.program_id(0)
    pad_ref[0:HALO, :] = jnp.where(i == 0, 0.0, halo_ref[...])
    pad_ref[HALO:HALO + tm, :] = x_ref[...]
    c = _conv_taps(pad_ref, w_ref, tm, HALO - (GDN_CONV - 1))
    s = _sigmoid(c)
    y = c * s
    r = lax.rsqrt(_head_sums(y * y, seg_ref[...]) + RMS_EPS)
    scale = jnp.where(grp < 1, GDN_D ** -0.5, 1.0)
    return grp < 2, c, s, y, r, scale


def gdn_pre_fwd(proj, conv_w, name):
    T = proj.shape[0]
    tm = _pick(T, GDN_PRE_ROWS, 8)
    GW = GDN_GROUP_W

    def body(x_ref, halo_ref, w_ref, seg_ref, o_ref, pad_ref):
        normed, c, s, y, r, scale = _gdn_pre_common(x_ref, halo_ref, w_ref, seg_ref, pad_ref, tm)
        o_ref[...] = jnp.where(normed, y * r * scale, y)

    return pl.pallas_call(
        body, grid=(3, T // tm),
        in_specs=[pl.BlockSpec((tm, GW), lambda hb, i: (i, hb)),
                  pl.BlockSpec((HALO, GW), lambda hb, i: (jnp.maximum(i * (tm // HALO) - 1, 0), hb)),
                  pl.BlockSpec((GDN_CONV, GW), lambda hb, i: (0, hb)), pl.BlockSpec((GW, GW), lambda hb, i: (0, 0))],
        out_specs=pl.BlockSpec((tm, GW), lambda hb, i: (i, hb)),
        out_shape=jax.ShapeDtypeStruct((T, 3 * GW), F32),
        scratch_shapes=[pltpu.VMEM((tm + HALO, GW), F32)],
        compiler_params=_cp(("parallel", "parallel")), name=name)(proj, proj, conv_w, _head_segments())


def gdn_pre_bwd_pointwise(proj, conv_w, dqkv, name, hosted=None):
    T = proj.shape[0]
    tm = _pick(T, GDN_PRE_ROWS, 8)
    GW = GDN_GROUP_W

    def body(x_ref, halo_ref, w_ref, seg_ref, d_ref, dc_ref, dw_ref, pad_ref):
        i = pl.program_id(1)
        normed, c, s, y, r, scale = _gdn_pre_common(x_ref, halo_ref, w_ref, seg_ref, pad_ref, tm)

        @pl.when(i == 0)
        def _():
            dw_ref[...] = jnp.zeros_like(dw_ref)

        d = d_ref[...]
        n = y * r
        dn = d * scale
        dy = jnp.where(normed, r * (dn - n * _head_sums(dn * n, seg_ref[...])), d)
        dc = dy * (s * (1.0 + c * (1.0 - s)))
        dc_ref[...] = dc
        for k in range(GDN_CONV):
            xs = pad_ref[pl.ds(HALO - (GDN_CONV - 1) + k, tm), :]
            dw_ref[k:k + 1, :] += jnp.sum(dc * xs, axis=0, keepdims=True)

    blk = pl.BlockSpec((tm, GW), lambda hb, i: (i, hb))
    wblk = pl.BlockSpec((GDN_CONV, GW), lambda hb, i: (0, hb))
    return _hosted_call(
        hosted, body, grid=(3, T // tm),
        in_specs=[blk, pl.BlockSpec((HALO, GW), lambda hb, i: (jnp.maximum(i * (tm // HALO) - 1, 0), hb)), wblk,
                  pl.BlockSpec((GW, GW), lambda hb, i: (0, 0)), blk],
        out_specs=[blk, wblk],
        out_shape=[jax.ShapeDtypeStruct((T, 3 * GW), F32), jax.ShapeDtypeStruct((GDN_CONV, 3 * GW), F32)],
        scratch_shapes=[pltpu.VMEM((tm + HALO, GW), F32)],
        compiler_params=_cp(("parallel", "arbitrary")), name=name)(proj, proj, conv_w, _head_segments(), dqkv)


def gdn_pre_bwd_conv(dc, conv_w_p, name):
    T = dc.shape[0]
    tm = _pick(T, GDN_PRE_ROWS, 8)
    nt = T // tm
    GW = GDN_GROUP_W

    def body(dc_ref, halo_ref, w_ref, dx_ref, pad_ref):
        i = pl.program_id(1)
        pad_ref[0:tm, :] = dc_ref[...]
        pad_ref[tm:tm + HALO, :] = jnp.where(i == nt - 1, 0.0, halo_ref[...])
        acc = w_ref[GDN_CONV - 1:GDN_CONV, :] * pad_ref[pl.ds(0, tm), :]
        for k in range(GDN_CONV - 1):
            acc = acc + w_ref[k:k + 1, :] * pad_ref[pl.ds(GDN_CONV - 1 - k, tm), :]
        dx_ref[...] = acc

    blk = pl.BlockSpec((tm, GW), lambda hb, i: (i, hb))
    return pl.pallas_call(
        body, grid=(3, nt),
        in_specs=[blk, pl.BlockSpec((HALO, GW), lambda hb, i: (jnp.minimum((i + 1) * (tm // HALO), T // HALO - 1), hb)),
                  pl.BlockSpec((GDN_CONV, GW), lambda hb, i: (0, hb))],
        out_specs=blk,
        out_shape=jax.ShapeDtypeStruct((T, 3 * GW), F32),
        scratch_shapes=[pltpu.VMEM((tm + HALO, GW), F32)],
        compiler_params=_cp(("parallel", "parallel")), name=name)(dc, dc, conv_w_p)


def _chunk_masks(C):
    row = _iota2((C, C), 0)
    col = _iota2((C, C), 1)
    return row >= col, row > col, row == col


GDN_FWD_CHUNKS = 4
BNN = (((2,), (1,)), ((0,), (0,)))
BNT = (((2,), (2,)), ((0,), (0,)))
BTN = (((1,), (1,)), ((0,), (0,)))


def _bmm(a, b, dims=BNN):
    return lax.dot_general(a.astype(BF16), b.astype(BF16), dims, preferred_element_type=F32)


def _hbmm(a, b):
    m = a.shape[1]
    a_hi, a_lo = _split2(a)
    b_hi, b_lo = _split2(b)
    r = lax.dot_general(jnp.concatenate([a_hi, a_lo], axis=1), b_hi, BNN, preferred_element_type=F32)
    return r[:, :m] + r[:, m:] + lax.dot_general(a_hi, b_lo, BNN, preferred_element_type=F32)


def _hbmm_tn(a, b):
    a_hi, a_lo = _split2(a)
    b_hi, b_lo = _split2(b)
    d = functools.partial(lax.dot_general, dimension_numbers=BTN, preferred_element_type=F32)
    return d(a_hi, b_hi) + d(a_lo, b_hi) + d(a_hi, b_lo)


def _col_to_row(colv, eye):
    return jnp.sum(jnp.where(eye, colv, 0.0), axis=1, keepdims=True)


def _row_to_col(rowv, eye):
    return jnp.sum(jnp.where(eye, rowv, 0.0), axis=2, keepdims=True)


def _unit_lower_inverse(A, eye):
    C = A.shape[1]
    P = jnp.where(eye, 1.0, 0.0) - A
    Bp = _hbmm(A, A)
    for _ in range(4):
        R = _hbmm(jnp.concatenate([Bp, P], axis=1), Bp)
        Bp = R[:, :C]
        P = P + R[:, C:]
    return P + _hbmm(P, Bp)


def _stack_heads(ref, first_head, n, row0=0):
    rows = pl.ds(row0, GDN_CHUNK)
    return jnp.stack([ref[rows, pl.ds((first_head + h) * GDN_D, GDN_D)] for h in range(n)])


def _unstack_heads(ref, first_head, val, row0=0):
    rows = pl.ds(row0, GDN_CHUNK)
    for h in range(val.shape[0]):
        ref[rows, pl.ds((first_head + h) * GDN_D, GDN_D)] = val[h]


def _gdn_gates(gab, a_row, dt_row, incl):
    g_all = -jnp.exp(a_row) * _softplus(gab + dt_row)
    beta_all = _sigmoid(gab)
    gc_all = _ones_dot_left(incl.astype(BF16), g_all)
    return g_all, beta_all, gc_all


def _gdn_common(qkv_ref, gc_all, beta_all, incl, strict, eye, row0=0):
    C, H = GDN_CHUNK, GDN_HEADS
    q, k, v = (_stack_heads(qkv_ref, j * H, H, row0) for j in range(3))
    gc = jnp.stack([gc_all[:, h:h + 1] for h in range(H)])
    beta = jnp.stack([beta_all[:, H + h:H + h + 1] for h in range(H)])
    gc_row = _col_to_row(gc, eye)
    decay = jnp.where(incl, jnp.exp(jnp.where(incl, gc - gc_row, 0.0)), 0.0)
    e_gc = jnp.exp(gc)
    gl = gc[:, C - 1:C, :]
    e_gl = jnp.exp(gl)
    ekd = jnp.exp(gl - gc)
    kb = k * beta
    A = jnp.where(strict, _bmm(kb, k, BNT) * decay, 0.0)
    Pm = jnp.where(incl, _bmm(q, k, BNT) * decay, 0.0)
    return q, k, v, gc, beta, decay, e_gc, e_gl, ekd, kb, A, Pm


def gdn_chunk_fwd(qkv, proj, a_row, dt_row, norm_w, name, hosted=None):
    T = qkv.shape[0]
    C, H, Dh = GDN_CHUNK, GDN_HEADS, GDN_D
    N = T // C
    J = GDN_FWD_CHUNKS if N % GDN_FWD_CHUNKS == 0 else 1

    def body(qkv_ref, gz_ref, gab_ref, a_ref, dt_ref, nw_ref, o_ref, opre_ref, Tm_ref, Sin_ref, S_ref):
        n = pl.program_id(0)

        @pl.when(n == 0)
        def _():
            S_ref[...] = jnp.zeros_like(S_ref)

        incl, strict, eye = _chunk_masks(C)
        gab = gab_ref[...]
        per_chunk = []
        for j in range(J):
            _, beta_all, gc_all = _gdn_gates(gab[j * C:(j + 1) * C], a_ref[...], dt_ref[...], incl)
            per_chunk.append(_gdn_common(qkv_ref, gc_all, beta_all, incl, strict, eye, row0=j * C))
        q, k, v, gc, beta, decay, e_gc, e_gl, ekd, kb, A, Pm = (jnp.concatenate(t, axis=0) for t in zip(*per_chunk))
        Tm = _unit_lower_inverse(A, eye)
        u = _hbmm(Tm, v * beta)
        w = _hbmm(Tm, kb * e_gc)
        qd = q * e_gc
        kd = k * ekd
        S = S_ref[...]
        for j in range(J):
            hs = slice(j * H, (j + 1) * H)
            v_new = u[hs] - _bmm(w[hs], S)
            o = _bmm(qd[hs], S) + _bmm(Pm[hs], v_new)
            Sin_ref[j] = S
            Tm_ref[j] = Tm[hs]
            S = S * e_gl[hs] + _bmm(kd[hs], v_new, BTN)
            r = lax.rsqrt(jnp.mean(o * o, axis=-1, keepdims=True) + RMS_EPS)
            gz = _stack_heads(gz_ref, 0, H, j * C)
            _unstack_heads(opre_ref, 0, o, j * C)
            _unstack_heads(o_ref, 0, o * r * nw_ref[...] * (gz * _sigmoid(gz)), j * C)
        S_ref[...] = S

    vec = pl.BlockSpec((1, LANES), lambda n: (0, 0))
    hblk = pl.BlockSpec((J * C, GDN_W), lambda n: (n, 0))
    sblk = pl.BlockSpec((J, H, Dh, Dh), lambda n: (n, 0, 0, 0))
    return _hosted_call(
        hosted, body, grid=(N // J,),
        in_specs=[pl.BlockSpec((J * C, 3 * GDN_W), lambda n: (n, 0)),
                  pl.BlockSpec((J * C, GDN_W), lambda n: (n, CB_GZ * LANES // GDN_W)),
                  pl.BlockSpec((J * C, LANES), lambda n: (n, CB_GAB)), vec, vec, pl.BlockSpec((1, Dh), lambda n: (0, 0))],
        out_specs=[hblk, hblk, sblk, sblk],
        out_shape=[jax.ShapeDtypeStruct((T, GDN_W), F32), jax.ShapeDtypeStruct((T, GDN_W), F32)]
        + [jax.ShapeDtypeStruct((N, H, Dh, Dh), F32)] * 2,
        scratch_shapes=[pltpu.VMEM((H, Dh, Dh), F32)],
        compiler_params=_cp(("arbitrary",)), name=name)(qkv, proj, proj, a_row, dt_row, norm_w)


def gdn_chunk_bwd(qkv, proj, a_row, dt_row, norm_w, opre, Tm_all, Sin_all, docat, name, hosted=None):
    T = qkv.shape[0]
    C, H, Dh = GDN_CHUNK, GDN_HEADS, GDN_D
    N = T // C

    def body(qkv_ref, gz_ref, gab_ref, a_ref, dt_ref, nw_ref, opre_ref, Tm_ref, Sin_ref, do_ref,
             dqkv_ref, dgz_ref, dgab_ref, da_ref, ddt_ref, dnw_ref, dS_ref):
        n = pl.program_id(0)

        @pl.when(n == 0)
        def _():
            dS_ref[...] = jnp.zeros_like(dS_ref)
            da_ref[...] = jnp.zeros_like(da_ref)
            ddt_ref[...] = jnp.zeros_like(ddt_ref)
            dnw_ref[...] = jnp.zeros_like(dnw_ref)

        incl, strict, eye = _chunk_masks(C)
        gab = gab_ref[...]
        g_all, beta_all, gc_all = _gdn_gates(gab, a_ref[...], dt_ref[...], incl)
        lane = _iota2((C, LANES), 1)
        rowi = _iota2((C, 1), 0)
        nw = nw_ref[...]
        q, k, v, gc, beta, decay, e_gc, e_gl, ekd, kb, A, Pm = _gdn_common(qkv_ref, gc_all, beta_all, incl, strict, eye)
        Tm = Tm_ref[0]
        S = Sin_ref[0]
        dS = dS_ref[...]
        kbe = kb * e_gc
        u = _hbmm(Tm, v * beta)
        w = _hbmm(Tm, kbe)
        qd = q * e_gc
        kd = k * ekd
        v_new = u - _bmm(w, S)
        o = _stack_heads(opre_ref, 0, H)
        gz = _stack_heads(gz_ref, 0, H)
        don = _stack_heads(do_ref, 0, H)
        r = lax.rsqrt(jnp.mean(o * o, axis=-1, keepdims=True) + RMS_EPS)
        nn = o * r
        sgz = _sigmoid(gz)
        silu = gz * sgz
        _unstack_heads(dgz_ref, 0, don * nn * nw * (sgz * (1.0 + gz * (1.0 - sgz))))
        dnn = don * nw * silu
        dnw_ref[...] += jnp.sum(jnp.sum(don * nn * silu, axis=0), axis=0, keepdims=True)
        do = r * (dnn - nn * jnp.mean(dnn * nn, axis=-1, keepdims=True))
        dv_new = _bmm(Pm, do, BTN) + _bmm(kd, dS)
        dPm = jnp.where(incl, _bmm(do, v_new, BNT), 0.0)
        dqd = _bmm(do, S, BNT)
        dkd = _bmm(v_new, dS, BNT)
        dS_ref[...] = _bmm(qd, do, BTN) + e_gl * dS - _bmm(w, dv_new, BTN)
        dgl = jnp.sum(jnp.sum(dS * S, axis=2, keepdims=True), axis=1, keepdims=True) * e_gl
        dw = -_bmm(dv_new, S, BNT)
        dvb = _hbmm_tn(Tm, dv_new)
        dkbe = _hbmm_tn(Tm, dw)
        dA = -jnp.where(strict, _bmm(dvb, u, BNT) + _bmm(dkbe, w, BNT), 0.0)
        dAD = dA * decay
        dPD = dPm * decay
        Gm = dA * A + dPm * Pm
        dgc = jnp.sum(Gm, axis=2, keepdims=True) - _row_to_col(jnp.sum(Gm, axis=1, keepdims=True), eye)
        dkb = _bmm(dAD, k) + dkbe * e_gc
        dk = _bmm(dAD, kb, BTN) + _bmm(dPD, q, BTN) + dkd * ekd + dkb * beta
        dq = _bmm(dPD, k) + dqd * e_gc
        tkd = jnp.sum(dkd * kd, axis=-1, keepdims=True)
        dgc = dgc + jnp.sum(dqd * qd, axis=-1, keepdims=True) - tkd + jnp.sum(dkbe * kbe, axis=-1, keepdims=True)
        dgl = dgl + jnp.sum(tkd, axis=1, keepdims=True)
        dgc = dgc + jnp.where(rowi == C - 1, dgl, 0.0)
        dbeta = jnp.sum(dvb * v, axis=-1, keepdims=True) + jnp.sum(dkb * k, axis=-1, keepdims=True)
        _unstack_heads(dqkv_ref, 0, dq)
        _unstack_heads(dqkv_ref, H, dk)
        _unstack_heads(dqkv_ref, 2 * H, dvb * beta)
        dgc_all = jnp.zeros((C, LANES), F32)
        dbeta_all = jnp.zeros((C, LANES), F32)
        for h in range(H):
            dgc_all = dgc_all + jnp.where(lane == h, dgc[h], 0.0)
            dbeta_all = dbeta_all + jnp.where(lane == H + h, dbeta[h], 0.0)
        upper = (_iota2((C, C), 0) <= _iota2((C, C), 1)).astype(BF16)
        dg_all = _ones_dot_left(upper, dgc_all)
        dga = dg_all * (-jnp.exp(a_ref[...])) * _sigmoid(gab + dt_ref[...])
        dgb = dbeta_all * beta_all * (1.0 - beta_all)
        dgab_ref[...] = jnp.where(lane < H, dga, jnp.where(lane < 2 * H, dgb, 0.0))
        da_ref[...] += jnp.sum(jnp.where(lane < H, dg_all * g_all, 0.0), axis=0, keepdims=True)
        ddt_ref[...] += jnp.sum(jnp.where(lane < H, dga, 0.0), axis=0, keepdims=True)

    rev = lambda n: N - 1 - n
    vec = pl.BlockSpec((1, LANES), lambda n: (0, 0))
    nwv = pl.BlockSpec((1, Dh), lambda n: (0, 0))
    hblk = pl.BlockSpec((C, GDN_W), lambda n: (rev(n), 0))
    sblk = pl.BlockSpec((1, H, Dh, Dh), lambda n: (rev(n), 0, 0, 0))
    qblk = pl.BlockSpec((C, 3 * GDN_W), lambda n: (rev(n), 0))
    return _hosted_call(
        hosted, body, grid=(N,),
        in_specs=[qblk, pl.BlockSpec((C, GDN_W), lambda n: (rev(n), CB_GZ * LANES // GDN_W)),
                  pl.BlockSpec((C, LANES), lambda n: (rev(n), CB_GAB)), vec, vec, nwv, hblk, sblk, sblk, hblk],
        out_specs=[qblk, hblk, pl.BlockSpec((C, LANES), lambda n: (rev(n), 0)), vec, vec, nwv],
        out_shape=[jax.ShapeDtypeStruct((T, 3 * GDN_W), F32), jax.ShapeDtypeStruct((T, GDN_W), F32),
                   jax.ShapeDtypeStruct((T, LANES), F32), jax.ShapeDtypeStruct((1, LANES), F32),
                   jax.ShapeDtypeStruct((1, LANES), F32), jax.ShapeDtypeStruct((1, Dh), F32)],
        scratch_shapes=[pltpu.VMEM((H, Dh, Dh), F32)],
        compiler_params=_cp(("arbitrary",)), name=name)(qkv, proj, proj, a_row, dt_row, norm_w, opre, Tm_all, Sin_all, docat)


ATT_BQ, ATT_BK = 512, 1024
NEG_BIG = -1e30


def _att_blocks(T):
    bq, bk = min(ATT_BQ, T), min(ATT_BK, T)
    assert bk % bq == 0 and T % bk == 0
    return bq, bk


def _att_specs(T, bq, cbs):
    qspec = lambda cb: pl.BlockSpec((bq, LANES), lambda h, i: (i, cb + h))
    kspec = lambda cb: pl.BlockSpec((T, LANES), lambda h, i: (0, cb + h))
    return qspec, kspec


def _kblock(ref, kb, bk):
    return ref[pl.ds(pl.multiple_of(kb * bk, bk), bk), :]


def _att_pos(i, kb, bq, bk):
    qpos = i * bq + _iota2((bq, bk), 0)
    kpos = kb * bk + _iota2((bq, bk), 1)
    return qpos, kpos


def _later_keys(n):
    return (_iota2((n, n), 0) > _iota2((n, n), 1)).astype(BF16)


def _earlier_keys(n):
    return (_iota2((n, n), 0) < _iota2((n, n), 1)).astype(BF16)


def _tri_dot(x, tri, terms):
    acc, rest = None, x
    for t in range(terms):
        part = rest.astype(BF16)
        if t + 1 < terms:
            rest = rest - part.astype(F32)
        d = jnp.dot(part, tri, preferred_element_type=F32)
        acc = d if acc is None else acc + d
    return acc


SB_BLOCK = 256
SB_DEAD = -104.0


def _sb_blocks(T):
    b = min(SB_BLOCK, T)
    assert T % b == 0 and T // b <= LANES
    return b, b


def sb_fwd(proj, name, hosted=None):
    T = proj.shape[0]
    H = SB_HEADS
    bq, bk = _sb_blocks(T)
    scale = SB_DIM ** -0.5

    def body(q_ref, k_ref, v_ref, o_ref, tot_ref):
        i = pl.program_id(1)
        qb = q_ref[...].astype(BF16)
        diag = (i * bq) // bk
        lane = _iota2((bq, LANES), 1)
        later = _later_keys(bk)

        def block(kb, acc, R, masked):
            z = _bdot_nt(qb, _kblock(k_ref, kb, bk)) * scale
            sp = _softplus(z)
            if masked:
                qpos, kpos = _att_pos(i, kb, bq, bk)
                mask = kpos < qpos
                l1m = jnp.where(mask, -sp, 0.0)
            else:
                l1m = -sp
            W = jnp.exp((z - sp) + _tri_dot(l1m, later, 3) + R)
            if masked:
                W = jnp.where(mask, W, 0.0)
            acc = acc + _bdot(W, _kblock(v_ref, kb, bk))
            return acc, R + jnp.sum(l1m, axis=-1, keepdims=True)

        acc, R = block(diag, jnp.zeros((bq, LANES), F32), jnp.zeros((bq, 1), F32), True)

        def live(c):
            return jnp.logical_and(c[0] >= 0, jnp.max(c[2]) > SB_DEAD)

        def step(c):
            kb, acc, R, Rb = c
            acc, R_next = block(kb, acc, R, False)
            return kb - 1, acc, R_next, jnp.where(lane == kb, R, Rb)

        _, acc, _, Rb = lax.while_loop(live, step, (diag - 1, acc, R, jnp.where(lane == diag, 0.0, NEG_BIG)))
        o_ref[...] = acc
        tot_ref[...] = Rb

    qspec, kspec = _att_specs(T, bq, None)
    sds = jax.ShapeDtypeStruct((T, H * LANES), F32)
    oblk = pl.BlockSpec((bq, LANES), lambda h, i: (i, h))
    return _hosted_call(
        hosted, body, grid=(H, T // bq), in_specs=[qspec(CB_SQ), kspec(CB_SK), kspec(CB_SV)],
        out_specs=[oblk, oblk], out_shape=[sds, sds],
        compiler_params=_cp(("parallel", "parallel")), name=name)(proj, proj, proj)


def sb_bwd(proj, tot, docat, do_cb, name):
    T = proj.shape[0]
    H = SB_HEADS
    bq, bk = _sb_blocks(T)
    scale = SB_DIM ** -0.5

    def body(q_ref, k_ref, v_ref, tot_ref, do_ref, dq_ref, dk_ref, dv_ref):
        i = pl.program_id(1)

        @pl.when(i == 0)
        def _():
            dk_ref[...] = jnp.zeros_like(dk_ref)
            dv_ref[...] = jnp.zeros_like(dv_ref)

        qb = q_ref[...].astype(BF16)
        dob = do_ref[...].astype(BF16)
        Rb = tot_ref[...]
        diag = (i * bq) // bk
        lane = _iota2((bq, LANES), 1)
        later, earlier = _later_keys(bk), _earlier_keys(bk)
        first = lax.while_loop(
            lambda kb: jnp.logical_and(kb < diag, jnp.max(jnp.where(lane == kb, Rb, NEG_BIG)) <= SB_DEAD),
            lambda kb: kb + 1, jnp.int32(0))

        def block(kb, carry, masked):
            dq, Epre = carry
            R = jnp.sum(jnp.where(lane == kb, Rb, 0.0), axis=1, keepdims=True)
            kblk = _kblock(k_ref, kb, bk).astype(BF16)
            z = _bdot_nt(qb, kblk) * scale
            sp = _softplus(z)
            if masked:
                qpos, kpos = _att_pos(i, kb, bq, bk)
                mask = kpos < qpos
                l1m = jnp.where(mask, -sp, 0.0)
            else:
                l1m = -sp
            W = jnp.exp((z - sp) + _tri_dot(l1m, later, 3) + R)
            if masked:
                W = jnp.where(mask, W, 0.0)
            E = _bdot_nt(dob, _kblock(v_ref, kb, bk)) * W
            cexcl = _tri_dot(E, earlier, 3) + Epre
            neg = jnp.exp(-sp)
            dz = E * neg - cexcl * (1.0 - neg)
            if masked:
                dz = jnp.where(mask, dz, 0.0)
            dz = (dz * scale).astype(BF16)
            rows = pl.ds(pl.multiple_of(kb * bk, bk), bk)
            dk_ref[rows, :] += lax.dot_general(dz, qb, TN_DIMS, preferred_element_type=F32)
            dv_ref[rows, :] += lax.dot_general(W.astype(BF16), dob, TN_DIMS, preferred_element_type=F32)
            dq = dq + jnp.dot(dz, kblk, preferred_element_type=F32)
            return dq, Epre + jnp.sum(E, axis=-1, keepdims=True)

        init = (jnp.zeros((bq, LANES), F32), jnp.zeros((bq, 1), F32))
        carry = lax.fori_loop(first, diag, lambda kb, c: block(kb, c, False), init)
        dq, _ = block(diag, carry, True)
        dq_ref[...] = dq

    qspec, kspec = _att_specs(T, bq, None)
    sds = jax.ShapeDtypeStruct((T, H * LANES), F32)
    oblk = pl.BlockSpec((bq, LANES), lambda h, i: (i, h))
    kout = pl.BlockSpec((T, LANES), lambda h, i: (0, h))
    return pl.pallas_call(
        body, grid=(H, T // bq),
        in_specs=[qspec(CB_SQ), kspec(CB_SK), kspec(CB_SV), oblk, qspec(do_cb)],
        out_specs=[oblk, kout, kout], out_shape=[sds, sds, sds],
        compiler_params=_cp(("arbitrary", "arbitrary")), name=name)(proj, proj, proj, tot, docat)


def mla_fwd(Q, K, V, name, hosted=None):
    T = Q.shape[0]
    H = MLA_HEADS
    bq, bk = _att_blocks(T)
    scale = (MLA_NOPE + MLA_ROPE) ** -0.5

    def body(q_ref, k_ref, v_ref, o_ref, lse_ref):
        i = pl.program_id(1)
        qb = q_ref[...]
        diag = (i * bq) // bk

        def block(kb, carry, masked):
            acc, m, l = carry
            s = _bdot_nt(qb, _kblock(k_ref, kb, bk)) * scale
            if masked:
                qpos, kpos = _att_pos(i, kb, bq, bk)
                s = jnp.where(kpos <= qpos, s, NEG_BIG)
            m_new = jnp.maximum(m, jnp.max(s, axis=-1, keepdims=True))
            p = jnp.exp(s - m_new)
            corr = jnp.exp(m - m_new)
            acc = corr * acc + _bdot(p, _kblock(v_ref, kb, bk))
            return acc, m_new, corr * l + jnp.sum(p, axis=-1, keepdims=True)

        init = (jnp.zeros((bq, LANES), F32), jnp.full((bq, 1), NEG_BIG, F32), jnp.zeros((bq, 1), F32))
        carry = lax.fori_loop(0, diag, lambda kb, c: block(kb, c, False), init)
        acc, m, l = block(diag, carry, True)
        o_ref[...] = acc / l
        lse_ref[...] = jnp.broadcast_to(m + jnp.log(l), (bq, LANES))

    qspec, kspec = _att_specs(T, bq, None)
    sds = jax.ShapeDtypeStruct((T, H * LANES), F32)
    oblk = pl.BlockSpec((bq, LANES), lambda h, i: (i, h))
    return _hosted_call(
        hosted, body, grid=(H, T // bq), in_specs=[qspec(0), kspec(0), kspec(0)],
        out_specs=[oblk, oblk], out_shape=[sds, sds],
        compiler_params=_cp(("parallel", "parallel")), name=name)(Q, K, V)


def mla_bwd(Q, K, V, o, lse, docat, do_cb, name, hosted=None):
    T = Q.shape[0]
    H = MLA_HEADS
    bq, bk = _att_blocks(T)
    scale = (MLA_NOPE + MLA_ROPE) ** -0.5

    def body(q_ref, k_ref, v_ref, o_ref, lse_ref, do_ref, dq_ref, dk_ref, dv_ref):
        i = pl.program_id(1)

        @pl.when(i == 0)
        def _():
            dk_ref[...] = jnp.zeros_like(dk_ref)
            dv_ref[...] = jnp.zeros_like(dv_ref)

        qb = q_ref[...]
        do = do_ref[...]
        dob = do.astype(BF16)
        delta = jnp.sum(do * o_ref[...], axis=-1, keepdims=True)
        lse = lse_ref[:, 0:1]

        diag = (i * bq) // bk

        def block(kb, dq, masked):
            kblk = _kblock(k_ref, kb, bk)
            s = _bdot_nt(qb, kblk) * scale
            if masked:
                qpos, kpos = _att_pos(i, kb, bq, bk)
                s = jnp.where(kpos <= qpos, s, NEG_BIG)
            p = jnp.exp(s - lse)
            dp = _bdot_nt(dob, _kblock(v_ref, kb, bk))
            ds = (p * (dp - delta) * scale).astype(BF16)
            rows = pl.ds(pl.multiple_of(kb * bk, bk), bk)
            dk_ref[rows, :] += lax.dot_general(ds, qb, TN_DIMS, preferred_element_type=F32)
            dv_ref[rows, :] += lax.dot_general(p.astype(BF16), dob, TN_DIMS, preferred_element_type=F32)
            return dq + jnp.dot(ds, kblk, preferred_element_type=F32)

        dq = lax.fori_loop(0, diag, lambda kb, c: block(kb, c, False), jnp.zeros((bq, LANES), F32))
        dq_ref[...] = block(diag, dq, True)

    qspec, kspec = _att_specs(T, bq, None)
    sds = jax.ShapeDtypeStruct((T, H * LANES), F32)
    oblk = pl.BlockSpec((bq, LANES), lambda h, i: (i, h))
    kout = pl.BlockSpec((T, LANES), lambda h, i: (0, h))
    return _hosted_call(
        hosted, body, grid=(H, T // bq),
        in_specs=[qspec(0), kspec(0), kspec(0), oblk, oblk, qspec(do_cb)],
        out_specs=[oblk, kout, kout], out_shape=[sds, sds, sds],
        compiler_params=_cp(("arbitrary", "arbitrary")), name=name)(Q, K, V, o, lse, docat)


def _tile_heads(t, n):
    return jnp.concatenate([t] * n, axis=1)


def _rope(X, C, Sn, Sp):
    n = X.shape[1]
    return X * C + pltpu.roll(X, n - HALF_ROPE, 1) * Sn + pltpu.roll(X, HALF_ROPE, 1) * Sp


def _rope_t(dO, C, Sn, Sp):
    n = dO.shape[1]
    return dO * C + pltpu.roll(dO * Sn, HALF_ROPE, 1) + pltpu.roll(dO * Sp, n - HALF_ROPE, 1)


def _rms(x, w):
    r = lax.rsqrt(jnp.mean(x * x, axis=-1, keepdims=True) + RMS_EPS)
    xh = x * r
    return r, xh, xh * w


def _rms_bwd(dn, w, r, xh):
    dxh = dn * w
    return r * (dxh - xh * jnp.mean(dxh * xh, axis=-1, keepdims=True)), jnp.sum(dn * xh, axis=0, keepdims=True)


def _mla_pre_specs(T, tm):
    KV = MLA_KV_RANK
    QR = MLA_Q_RANK
    W = MLA_HEADS * LANES
    full = lambda shape: pl.BlockSpec(shape, lambda i: (0, 0))
    specs = [pl.BlockSpec((tm, QR), lambda i: (i, CB_MQ * LANES // QR)),
             pl.BlockSpec((tm, 2 * LANES), lambda i: (i, CB_MKV // 2)),
             full((1, QR)), full((1, KV))]
    rope = [pl.BlockSpec((tm, LANES), lambda i: (i, 0))] * 3
    return specs, rope, full, W


def mla_pre_fwd(proj, wq, wkv, wuq, wuk, wuv, ropeC, ropeSn, ropeSp, name):
    T = proj.shape[0]
    tm = _pick(T, 512, 16)
    KV = MLA_KV_RANK
    H = MLA_HEADS

    def body(mq_ref, mkv_ref, wq_ref, wkv_ref, wuq_ref, wuk_ref, wuv_ref, c_ref, sn_ref, sp_ref, Q_ref, K_ref, V_ref):
        C, Sn, Sp = (_tile_heads(t[...], H) for t in (c_ref, sn_ref, sp_ref))
        _, _, qn = _rms(mq_ref[...], wq_ref[...])
        Q_ref[...] = _rope(_bdot(qn, wuq_ref[...]), C, Sn, Sp).astype(BF16)
        mkv = mkv_ref[...]
        _, _, kvn = _rms(mkv[:, :KV], wkv_ref[...])
        kr = pltpu.roll(mkv[:, KV:], MLA_NOPE, 1)
        K_ref[...] = _rope(_bdot(kvn, wuk_ref[...]) + _tile_heads(kr, H), C, Sn, Sp).astype(BF16)
        V_ref[...] = _bdot(kvn, wuv_ref[...]).astype(BF16)

    specs, rope, full, W = _mla_pre_specs(T, tm)
    oblk = pl.BlockSpec((tm, W), lambda i: (i, 0))
    sds = jax.ShapeDtypeStruct((T, W), BF16)
    return pl.pallas_call(
        body, grid=(T // tm,),
        in_specs=specs + [full(wuq.shape), full(wuk.shape), full(wuv.shape)] + rope,
        out_specs=[oblk, oblk, oblk], out_shape=[sds, sds, sds],
        compiler_params=_cp(("parallel",)), name=name)(proj, proj, wq, wkv, wuq, wuk, wuv, ropeC, ropeSn, ropeSp)


def mla_pre_bwd(proj, wq, wkv, wuq, wuk, wuv, ropeC, ropeSn, ropeSp, dQ, dK, dV, name):
    T = proj.shape[0]
    tm = _pick(T, 512, 16)
    KV = MLA_KV_RANK
    H = MLA_HEADS

    def body(mq_ref, mkv_ref, wq_ref, wkv_ref, wuq_ref, wuk_ref, wuv_ref,
             c_ref, sn_ref, sp_ref, dQ_ref, dK_ref, dV_ref,
             dmq_ref, dmkv_ref, dwuq_ref, dwuk_ref, dwuv_ref, dwq_ref, dwkv_ref):
        i = pl.program_id(0)

        @pl.when(i == 0)
        def _():
            for ref in (dwuq_ref, dwuk_ref, dwuv_ref, dwq_ref, dwkv_ref):
                ref[...] = jnp.zeros_like(ref)

        C, Sn, Sp = (_tile_heads(t[...], H) for t in (c_ref, sn_ref, sp_ref))
        rq, xq, qn = _rms(mq_ref[...], wq_ref[...])
        mkv = mkv_ref[...]
        rkv, xkv, kvn = _rms(mkv[:, :KV], wkv_ref[...])
        dqf = _rope_t(dQ_ref[...], C, Sn, Sp)
        dkf = _rope_t(dK_ref[...], C, Sn, Sp)
        dv = dV_ref[...]
        dwuq_ref[...] += _bdot_tn(qn, dqf)
        dwuk_ref[...] += _bdot_tn(kvn, dkf)
        dwuv_ref[...] += _bdot_tn(kvn, dv)
        dmq, dwq = _rms_bwd(_bdot_nt(dqf, wuq_ref[...]), wq_ref[...], rq, xq)
        dckv, dwkv = _rms_bwd(_bdot_nt(dkf, wuk_ref[...]) + _bdot_nt(dv, wuv_ref[...]), wkv_ref[...], rkv, xkv)
        dwq_ref[...] += dwq
        dwkv_ref[...] += dwkv
        dmq_ref[...] = dmq
        dkr = dkf[:, 0:LANES]
        for h in range(1, H):
            dkr = dkr + dkf[:, h * LANES:(h + 1) * LANES]
        dkr = pltpu.roll(dkr, LANES - MLA_NOPE, 1)
        dkr = jnp.where(_iota2(dkr.shape, 1) < MLA_ROPE, dkr, 0.0)
        dmkv_ref[...] = jnp.concatenate([dckv, dkr], axis=1)

    specs, rope, full, W = _mla_pre_specs(T, tm)
    wide = pl.BlockSpec((tm, W), lambda i: (i, 0))
    return pl.pallas_call(
        body, grid=(T // tm,),
        in_specs=specs + [full(w.shape) for w in (wuq, wuk, wuv)] + rope + [wide, wide, wide],
        out_specs=[pl.BlockSpec((tm, MLA_Q_RANK), lambda i: (i, 0)), pl.BlockSpec((tm, 2 * LANES), lambda i: (i, 0)),
                   full(wuq.shape), full(wuk.shape), full(wuv.shape), full((1, MLA_Q_RANK)), full((1, KV))],
        out_shape=[jax.ShapeDtypeStruct((T, MLA_Q_RANK), F32), jax.ShapeDtypeStruct((T, 2 * LANES), F32),
                   jax.ShapeDtypeStruct(wuq.shape, F32), jax.ShapeDtypeStruct(wuk.shape, F32),
                   jax.ShapeDtypeStruct(wuv.shape, F32), jax.ShapeDtypeStruct((1, MLA_Q_RANK), F32),
                   jax.ShapeDtypeStruct((1, KV), F32)],
        compiler_params=_cp(("arbitrary",)), name=name)(
            proj, proj, wq, wkv, wuq, wuk, wuv, ropeC, ropeSn, ropeSp, dQ, dK, dV)


def all_gather(shards, name):
    n = len(shards)

    def body(*refs):
        x_refs, out_refs = refs[:n], refs[n:2 * n]
        send_sems, recv_sems, local_sems = refs[2 * n:]
        x, y, c = _place()
        me, sibling = (x, y, c), (x, y, 1 - c)
        chips = [(1 - x, y), (x, 1 - y), (1 - x, 1 - y)]

        def slot(a, px, py, pc):
            return out_refs[a].at[4 * px + 2 * py + pc]

        def copy(a, k, block, to, src=None):
            return pltpu.make_async_remote_copy(
                src_ref=slot(a, *block) if src is None else src, dst_ref=slot(a, *block),
                send_sem=send_sems.at[a, k], recv_sem=recv_sems.at[a, k], device_id=to, device_id_type=MESH)

        mine = [pltpu.make_async_copy(x_refs[a], slot(a, *me), local_sems.at[a]) for a in range(n)]
        first = []
        for a in range(n):
            mine[a].start()
            first.append(copy(a, 0, me, sibling, src=x_refs[a]))
            first += [copy(a, 1 + j, me, (*chip, c), src=x_refs[a]) for j, chip in enumerate(chips)]
        for cp in first:
            cp.start()
        passed = []
        for j, chip in enumerate(chips):
            for a in range(n):
                copy(a, 1 + j, (*chip, c), me).wait_recv()
                passed.append(copy(a, 4 + j, (*chip, c), sibling))
                passed[-1].start()
        for a in range(n):
            copy(a, 0, sibling, me).wait_recv()
            for j, chip in enumerate(chips):
                copy(a, 4 + j, (*chip, 1 - c), me).wait_recv()
        for cp in first + passed:
            cp.wait_send()
        for cp in mine:
            cp.wait()

    return pl.pallas_call(
        body, out_shape=[jax.ShapeDtypeStruct((N_DEV,) + s.shape, s.dtype) for s in shards],
        in_specs=[ANY] * n, out_specs=[ANY] * n,
        scratch_shapes=[pltpu.SemaphoreType.DMA((n, 7)), pltpu.SemaphoreType.DMA((n, 7)), pltpu.SemaphoreType.DMA((n,))],
        name=name)(*shards)


def reduce_adamw(parts, w, m, v, name):
    L = len(parts)
    n, Rl, C = parts[0].shape
    R = w.shape[0]
    assert R == L * Rl
    tr = Rl if Rl * C <= 256 * 1024 else _pick(Rl, 256, 16)
    nr = Rl // tr

    def body(*refs):
        p_refs = refs[:L]
        w_ref, m_ref, v_ref, g_ref, d_ref, nm_ref, nv_ref, sum_ref = refs[L:]
        grp = pl.program_id(0)
        for j in range(L):
            @pl.when(grp == j)
            def _(j=j):
                acc = p_refs[j][0].astype(F32)
                for s in range(1, n):
                    acc = acc + p_refs[j][s].astype(F32)
                sum_ref[...] = acc

        g_ = sum_ref[...]
        m_ = ADAM_B1 * m_ref[...] + (1.0 - ADAM_B1) * g_
        v_ = ADAM_B2 * v_ref[...] + (1.0 - ADAM_B2) * (g_ * g_)
        m_hat = m_ / (1.0 - ADAM_B1 ** ADAM_STEP)
        v_hat = v_ / (1.0 - ADAM_B2 ** ADAM_STEP)
        g_ref[...] = g_
        d_ref[...] = -ADAM_LR * (m_hat / (jnp.sqrt(v_hat) + ADAM_EPS) + ADAM_WD * w_ref[...])
        nm_ref[...] = m_
        nv_ref[...] = v_

    blk = pl.BlockSpec((tr, C), lambda l, r: (l * nr + r, 0))
    sds = jax.ShapeDtypeStruct((R, C), F32)
    p_specs = [pl.BlockSpec((n, tr, C), lambda l, r, j=j: (0, jnp.where(l == j, r, 0), 0)) for j in range(L)]
    return pl.pallas_call(
        body, grid=(L, nr), in_specs=p_specs + [blk] * 3,
        out_specs=[blk] * 4, out_shape=[sds] * 4, scratch_shapes=[pltpu.VMEM((tr, C), F32)],
        compiler_params=_cp(("arbitrary", "arbitrary")), name=name)(*parts, w, m, v)


SHARDED = {"ffa_w_in": (2, BF16), "ffa_w_out": (1, BF16), "mix_w_in": (2, BF16), "mla_w_uq": (2, BF16),
           "mla_w_ukv": (2, BF16), "mix_w_o": (1, BF16), "ffb_w_in": (2, BF16), "ffb_w_out": (1, BF16),
           "ple_w_gate": (1, BF16), "ple_w_proj": (2, BF16), "gdn_conv_w": (2, F32), "ln_g": (2, F32), "ln_b": (2, F32)}
FFN_SLOT = ("ffa_w_in", "ffa_w_out", "ffb_w_in", "ffb_w_out")
REPLICATED = ("gdn_a_log", "gdn_dt_bias", "gdn_norm_w", "mla_q_norm_w", "mla_kv_norm_w")
WEIGHTS = ("ffa_w_in", "ffa_w_out", "mix_w_in", "gdn_conv_w", "gdn_a_log", "gdn_dt_bias", "gdn_norm_w", "mla_q_norm_w",
           "mla_kv_norm_w", "mla_w_uq", "mla_w_ukv", "mix_w_o", "ffb_w_in", "ffb_w_out", "ln_g", "ln_b", "ple_w_gate",
           "ple_w_proj")


def _to_slots(full, axis):
    L, a, b = full.shape
    if axis == 2:
        return full.reshape(L, a, N_DEV, b // N_DEV).transpose(2, 0, 1, 3).reshape(N_DEV, L * a, b // N_DEV)
    return full.reshape(L, N_DEV, a // N_DEV, b).transpose(1, 0, 2, 3).reshape(N_DEV, L * a // N_DEV, b)


def _from_slots(slots, shard_shape, axis):
    L, a, b = shard_shape
    t = slots.reshape((N_DEV,) + tuple(shard_shape))
    if axis == 2:
        return t.transpose(1, 2, 0, 3).reshape(L, a, N_DEV * b)
    return t.transpose(1, 0, 2, 3).reshape(L, N_DEV * a, b)


def _view2d(t):
    return t.reshape(-1, t.shape[-1])


def _pad_heads(w, nh):
    K = w.shape[0]
    return jnp.pad(w.reshape(K, nh, GDN_D), ((0, 0), (0, 0), (0, LANES - GDN_D))).reshape(K, nh * LANES)


def _unpad_heads(w, nh):
    K = w.shape[0]
    return w.reshape(K, nh, LANES)[:, :, :GDN_D].reshape(K, nh * GDN_D)


IN_WIDTHS = (512, 512, 512, 512, 8, 8, 256, 256, 256, 256, 160)


def _split_in(w):
    offs = np.cumsum((0,) + IN_WIDTHS)
    return [w[:, int(offs[i]):int(offs[i + 1])] for i in range(len(IN_WIDTHS))]


def _pad_in_proj(w):
    gq, gk, gv, gz, ga, gb, sq, sk, sv, mq, mkv = _split_in(w)
    gab = jnp.pad(jnp.concatenate([ga, gb], axis=1), ((0, 0), (0, LANES - 2 * GDN_HEADS)))
    return jnp.concatenate(
        [gq, gk, gv, gz] + [_pad_heads(t, SB_HEADS) for t in (sq, sk, sv)]
        + [mq, jnp.pad(mkv, ((0, 0), (0, 2 * LANES - mkv.shape[1]))), gab], axis=1)


def _unpad_in_proj(wp):
    c = lambda cb, n: wp[:, cb * LANES:(cb + n) * LANES]
    gab = c(CB_GAB, 1)
    parts = [c(cb, DO_SB) for cb in (CB_GQ, CB_GK, CB_GV, CB_GZ)]
    parts += [gab[:, :GDN_HEADS], gab[:, GDN_HEADS:2 * GDN_HEADS]]
    parts += [_unpad_heads(c(cb, SB_HEADS), SB_HEADS) for cb in (CB_SQ, CB_SK, CB_SV)]
    parts += [c(CB_MQ, 2), c(CB_MKV, 2)[:, :MLA_KV_RANK + MLA_ROPE]]
    return jnp.concatenate(parts, axis=1)


def _pad_lanes(w, width):
    return jnp.pad(w, ((0, 0), (0, width - w.shape[1])))


def _mla_up_pad(w_uq, w_ukv):
    H = MLA_HEADS
    dq = MLA_NOPE + MLA_ROPE
    wuq = jnp.pad(w_uq.reshape(-1, H, dq), ((0, 0), (0, 0), (0, LANES - dq))).reshape(-1, H * LANES)
    kv = w_ukv.reshape(-1, H, MLA_NOPE + MLA_V)
    wuk = jnp.pad(kv[:, :, :MLA_NOPE], ((0, 0), (0, 0), (0, LANES - MLA_NOPE))).reshape(-1, H * LANES)
    wuv = jnp.pad(kv[:, :, MLA_NOPE:], ((0, 0), (0, 0), (0, LANES - MLA_V))).reshape(-1, H * LANES)
    return wuq, wuk, wuv


def _mla_up_unpad(dwuq, dwuk, dwuv):
    H = MLA_HEADS
    dq = MLA_NOPE + MLA_ROPE
    g_uq = dwuq.reshape(-1, H, LANES)[:, :, :dq].reshape(-1, H * dq)
    g_ukv = jnp.concatenate([dwuk.reshape(-1, H, LANES)[:, :, :MLA_NOPE], dwuv.reshape(-1, H, LANES)[:, :, :MLA_V]],
                            axis=2).reshape(-1, H * (MLA_NOPE + MLA_V))
    return g_uq, g_ukv


def _rope_tables(positions):
    inv = 1.0 / (ROPE_BASE ** (jnp.arange(0, MLA_ROPE, 2, dtype=F32) / MLA_ROPE))
    ang = positions.astype(F32)[:, None] * inv
    cos, sin = jnp.cos(ang), jnp.sin(ang)
    T = positions.shape[0]
    one = lambda n: jnp.ones((T, n), F32)
    zero = lambda n: jnp.zeros((T, n), F32)
    tail = LANES - MLA_NOPE - MLA_ROPE
    C = jnp.concatenate([one(MLA_NOPE), cos, cos, one(tail)], axis=1)
    Sn = jnp.concatenate([zero(MLA_NOPE), -sin, zero(HALF_ROPE + tail)], axis=1)
    Sp = jnp.concatenate([zero(MLA_NOPE + HALF_ROPE), sin, zero(tail)], axis=1)
    return C, Sn, Sp


GATHER_FIRST = [("ffa_w_in", 0), ("ffa_w_out", 0)] + [(n, l) for l in range(DEPTH) for n in ("gdn_conv_w", "ln_g", "ln_b")]
GATHER_PLAN = {
    (0, "ffa_fwd"): [("mix_w_in", 0), ("mla_w_uq", 0), ("mla_w_ukv", 0), ("mix_w_o", 0)],
    (0, "in_proj"): [("ple_w_gate", 0), ("ple_w_proj", 0)],
    (0, "gdn_chunk_fwd"): [("ffb_w_in", 0)],
    (0, "sb_fwd"): [("ffb_w_out", 0), ("mix_w_o", 1)],
    (0, "mla_fwd"): [("ffa_w_out", 1)],
    (0, "ffb_fwd"): [("ffa_w_in", 1)],
    (1, "ffa_fwd"): [("mix_w_in", 1)],
    (1, "in_proj"): [("mla_w_uq", 1), ("mla_w_ukv", 1)],
    (1, "gdn_chunk_fwd"): [("ffb_w_in", 1)],
    (1, "sb_fwd"): [("ffb_w_out", 1), ("ple_w_gate", 1), ("ple_w_proj", 1)],
}
SCATTER_PLAN = {
    (1, "gdn_chunk_bwd"): [("ffb_w_in", 1)],
    (1, "gdn_pre_bwd"): [("ffb_w_out", 1), ("ple_w_gate", 1), ("ple_w_proj", 1), ("mix_w_o", 1)],
    (1, "ffa_bwd"): [("mix_w_in", 1), ("mla_w_uq", 1), ("mla_w_ukv", 1), ("gdn_conv_w", 1)],
    (0, "ffb_bwd"): [("ffa_w_in", 1)],
    (0, "gdn_chunk_bwd"): [("ffb_w_in", 0)],
    (0, "gdn_pre_bwd"): [("ffb_w_out", 0), ("ple_w_gate", 0), ("ple_w_proj", 0), ("mix_w_o", 0)],
    (0, "mla_bwd"): [("ffa_w_out", 1), ("ln_g", 1), ("ln_b", 1)],
    (0, "ffa_bwd"): [("mix_w_in", 0), ("mla_w_uq", 0), ("mla_w_ukv", 0), ("gdn_conv_w", 0)],
    (0, "d_ffa_in"): [("ffa_w_out", 0), ("ln_g", 0), ("ln_b", 0)],
}
SCATTER_LAST = [("ffa_w_in", 0)]


class Exchanges:
    def __init__(self, shards):
        self.shards = shards
        self.full = {}
        self.partial = {}
        self.received = {}

    def _block(self, key):
        n, l = key
        return self.shards[n][l].astype(SHARDED[n][1])

    def _absorb_gather(self, keys, results):
        for (n, l), g in zip(keys, results):
            blk = self.shards[n][l]
            self.full[(n, l)] = g if n in FFN_SLOT else _from_slots(g, (1,) + blk.shape, SHARDED[n][0])[0]

    def gather_now(self, keys, name):
        self._absorb_gather(keys, all_gather([self._block(k) for k in keys], name))

    def gather_with(self, layer, tag):
        keys = GATHER_PLAN.get((layer, tag))
        return None if keys is None else (keys, Hosted("gather", [self._block(k) for k in keys]))

    def scatter_with(self, layer, tag):
        keys = SCATTER_PLAN.get((layer, tag))
        return None if keys is None else (keys, Hosted("scatter", [self.partial[k] for k in keys]))

    def done(self, carried):
        if carried is not None:
            keys, hosted = carried
            if hosted.kind == "gather":
                self._absorb_gather(keys, hosted.results)
            else:
                self.received.update(zip(keys, hosted.results))

    def add_grad(self, key, g):
        n, l = key
        self.partial[key] = g if n in FFN_SLOT else _to_slots(g[None], SHARDED[n][0]).astype(SHARDED[n][1])


def _carried(c):
    return None if c is None else c[1]


def _layer_fwd(h0, h0t, p_i, rope, i, ex, rep):
    L = "L%d_" % i
    S = {"h0": h0, "h0t": h0t, "p": p_i}
    W = ex.full
    ln_g = [W[("ln_g", i)][j][None, :] for j in range(3)]
    ln_b = [W[("ln_b", i)][j][None, :] for j in range(3)]
    S["ln_g"] = ln_g
    c = ex.gather_with(i, "ffa_fwd")
    S["h1"], S["xh1"], S["rs1"], S["Ga"], S["Ua"], S["h1b"], S["h1t"] = ffn_fwd(
        h0, W[("ffa_w_in", i)], W[("ffa_w_out", i)], ln_g[0], ln_b[0], L + "ffa_fwd", hosted=_carried(c))
    ex.done(c)
    S["win"] = _pad_in_proj(W[("mix_w_in", i)])
    c = ex.gather_with(i, "in_proj")
    S["proj"] = mm_nn(S["h1b"], S["win"], L + "in_proj", hosted=_carried(c))
    ex.done(c)
    S["conv"] = W[("gdn_conv_w", i)]
    S["a_row"] = _pad_lanes(rep["gdn_a_log"][i][None, :], LANES)
    S["dt_row"] = _pad_lanes(rep["gdn_dt_bias"][i][None, :], LANES)
    S["nw"] = rep["gdn_norm_w"][i][None, :]
    S["wq"] = rep["mla_q_norm_w"][i][None, :]
    S["wkv"] = rep["mla_kv_norm_w"][i][None, :]
    S["qkv"] = gdn_pre_fwd(S["proj"], S["conv"], L + "gdn_pre_fwd")
    c = ex.gather_with(i, "gdn_chunk_fwd")
    S["o_gdn"], S["opre"], S["Tm"], S["Sin"] = gdn_chunk_fwd(
        S["qkv"], S["proj"], S["a_row"], S["dt_row"], S["nw"], L + "gdn_chunk_fwd", hosted=_carried(c))
    ex.done(c)
    c = ex.gather_with(i, "sb_fwd")
    S["o_sb"], S["tot"] = sb_fwd(S["proj"], L + "sb_fwd", hosted=_carried(c))
    ex.done(c)
    S["wuq"], S["wuk"], S["wuv"] = _mla_up_pad(W[("mla_w_uq", i)], W[("mla_w_ukv", i)])
    S["Q"], S["K"], S["V"] = mla_pre_fwd(S["proj"], S["wq"], S["wkv"], S["wuq"], S["wuk"], S["wuv"], *rope, L + "mla_pre_fwd")
    c = ex.gather_with(i, "mla_fwd")
    S["o_mla"], S["lse"] = mla_fwd(S["Q"], S["K"], S["V"], L + "mla_fwd", hosted=_carried(c))
    ex.done(c)
    wo = W[("mix_w_o", i)]
    wo_att = wo[GDN_W:].reshape(-1, GDN_D, wo.shape[1])
    S["wo"] = jnp.concatenate(
        [wo[:GDN_W], jnp.pad(wo_att, ((0, 0), (0, LANES - GDN_D), (0, 0))).reshape(-1, wo.shape[1])], axis=0)
    S["h2"], S["xh2"], S["rs2"], _, S["h2t"] = proj_res_ln([S["o_gdn"], S["o_sb"], S["o_mla"]], S["wo"], S["h1"],
                                                        ln_g[1], ln_b[1], L + "out_proj")
    c = ex.gather_with(i, "ffb_fwd")
    S["h3"], S["xh3"], S["rs3"], S["Gb"], S["Ub"], h3b, _ = ffn_fwd(
        S["h2"], W[("ffb_w_in", i)], W[("ffb_w_out", i)], ln_g[2], ln_b[2], L + "ffb_fwd", hosted=_carried(c))
    ex.done(c)
    h4, S["a"], S["e"], h4t = ple_fwd(S["h3"], h3b, p_i, W[("ple_w_gate", i)], W[("ple_w_proj", i)], L + "ple_fwd")
    return h4, h4t, S


def _layer_bwd(dh4, S, rope, i, ex):
    L = "L%d_" % i
    W = ex.full
    Grep = {}
    dz3, da, de, dg2, db2 = ple_ln_bwd(dh4, S["a"], S["e"], W[("ple_w_gate", i)], S["xh3"], S["rs3"], S["ln_g"][2],
                                       L + "ple_ln3_bwd")
    ex.add_grad(("ple_w_gate", i), mm_tn(S["h3"], da, L + "d_ple_gate"))
    ex.add_grad(("ple_w_proj", i), mm_tn(S["p"], de, L + "d_ple_proj"))
    c = ex.scatter_with(i, "ffb_bwd")
    dh2, dGb, dUb, actb = ffn_bwd(dz3, S["Gb"], S["Ub"], W[("ffb_w_in", i)], W[("ffb_w_out", i)], L + "ffb_bwd",
                                  hosted=_carried(c))
    ex.done(c)
    ex.add_grad(("ffb_w_in", i), ffn_dw_in(S["h2t"], dGb, dUb, L + "d_ffb_in"))
    ex.add_grad(("ffb_w_out", i), ffn_dw_out(actb, dz3, L + "d_ffb_out"))
    dz2, dg1, db1 = ln_bwd(dh2, S["xh2"], S["rs2"], S["ln_g"][1], L + "ln2_bwd")
    docat = mm_nn(dz2, S["wo"], L + "d_ocat", b_transposed=True)
    dwo_att = jnp.concatenate([mm_tn(S["o_sb"], dz2, L + "d_wo_sb"), mm_tn(S["o_mla"], dz2, L + "d_wo_mla")], axis=0)
    dwo_att = dwo_att.reshape(-1, LANES, dwo_att.shape[1])[:, :GDN_D, :].reshape(-1, dwo_att.shape[1])
    ex.add_grad(("mix_w_o", i), jnp.concatenate([mm_tn(S["o_gdn"], dz2, L + "d_wo_gdn"), dwo_att], axis=0))
    c = ex.scatter_with(i, "gdn_chunk_bwd")
    dqkv, dgz, dgab, d_alog, d_dt, d_nw = gdn_chunk_bwd(S["qkv"], S["proj"], S["a_row"], S["dt_row"], S["nw"],
                                                        S["opre"], S["Tm"], S["Sin"], docat, L + "gdn_chunk_bwd",
                                                        hosted=_carried(c))
    ex.done(c)
    c = ex.scatter_with(i, "gdn_pre_bwd")
    dc, dconv = gdn_pre_bwd_pointwise(S["proj"], S["conv"], dqkv, L + "gdn_pre_bwd", hosted=_carried(c))
    ex.done(c)
    dxqkv = gdn_pre_bwd_conv(dc, S["conv"], L + "gdn_conv_bwd")
    ex.add_grad(("gdn_conv_w", i), dconv)
    Grep["gdn_a_log"], Grep["gdn_dt_bias"], Grep["gdn_norm_w"] = d_alog[0, :GDN_HEADS], d_dt[0, :GDN_HEADS], d_nw[0]
    dsq, dsk, dsv = sb_bwd(S["proj"], S["tot"], docat, DO_SB, L + "sb_bwd")
    c = ex.scatter_with(i, "mla_bwd")
    dQ, dK, dV = mla_bwd(S["Q"], S["K"], S["V"], S["o_mla"], S["lse"], docat, DO_MLA, L + "mla_bwd",
                         hosted=_carried(c))
    ex.done(c)
    dmq, dmkv, dwuq, dwuk, dwuv, dwq, dwkv = mla_pre_bwd(
        S["proj"], S["wq"], S["wkv"], S["wuq"], S["wuk"], S["wuv"], *rope, dQ, dK, dV, L + "mla_pre_bwd")
    g_uq, g_ukv = _mla_up_unpad(dwuq, dwuk, dwuv)
    ex.add_grad(("mla_w_uq", i), g_uq)
    ex.add_grad(("mla_w_ukv", i), g_ukv)
    Grep["mla_q_norm_w"], Grep["mla_kv_norm_w"] = dwq[0], dwkv[0]
    dproj = jnp.concatenate([dxqkv, dgz, dsq, dsk, dsv, dmq, dmkv, dgab], axis=1).astype(BF16)
    ex.add_grad(("mix_w_in", i),
                _unpad_in_proj(mm_tn(S["h1t"], dproj, L + "d_in_proj", a_transposed=True)))
    dh1 = mm_nn(dproj, S["win"], L + "d_h1", res=dz2, res_scale=ALPHA, b_transposed=True)
    dz1, dg0, db0 = ln_bwd(dh1, S["xh1"], S["rs1"], S["ln_g"][0], L + "ln1_bwd")
    c = ex.scatter_with(i, "ffa_bwd")
    dh0, dGa, dUa, acta = ffn_bwd(dz1, S["Ga"], S["Ua"], W[("ffa_w_in", i)], W[("ffa_w_out", i)], L + "ffa_bwd",
                                  hosted=_carried(c))
    ex.done(c)
    ex.add_grad(("ffa_w_out", i), ffn_dw_out(acta, dz1, L + "d_ffa_out"))
    ex.add_grad(("ln_g", i), jnp.concatenate([dg0, dg1, dg2], axis=0))
    ex.add_grad(("ln_b", i), jnp.concatenate([db0, db1, db2], axis=0))
    c = ex.scatter_with(i, "d_ffa_in")
    ex.add_grad(("ffa_w_in", i), ffn_dw_in(S["h0t"], dGa, dUa, L + "d_ffa_in", hosted=_carried(c)))
    ex.done(c)
    return dh0, Grep


def _local_step(x, p, positions, target, ex, rep):
    assert DEPTH == 2
    rope = _rope_tables(positions)
    h, ht, saved = x, x.T.astype(BF16), []
    for i in range(DEPTH):
        h, ht, S = _layer_fwd(h, ht, p[i], rope, i, ex, rep)
        saved.append(S)
    loss, dh = loss_head(h, target, "loss_head")
    grads = [None] * DEPTH
    for i in reversed(range(DEPTH)):
        dh, grads[i] = _layer_bwd(dh, saved[i], rope, i, ex)
    return loss, dh, {n: jnp.stack([grads[i][n] for i in range(DEPTH)]) for n in REPLICATED}


def kernel(x, p, positions, ffa_w_in, ffa_w_out, mix_w_in, gdn_conv_w, gdn_a_log, gdn_dt_bias, gdn_norm_w, mla_q_norm_w, mla_kv_norm_w, mla_w_uq, mla_w_ukv, mix_w_o, ffb_w_in, ffb_w_out, ln_g, ln_b, ple_w_gate, ple_w_proj, loss_target, m_ffa_w_in, m_ffa_w_out, m_mix_w_in, m_gdn_conv_w, m_gdn_a_log, m_gdn_dt_bias, m_gdn_norm_w, m_mla_q_norm_w, m_mla_kv_norm_w, m_mla_w_uq, m_mla_w_ukv, m_mix_w_o, m_ffb_w_in, m_ffb_w_out, m_ln_g, m_ln_b, m_ple_w_gate, m_ple_w_proj, v_ffa_w_in, v_ffa_w_out, v_mix_w_in, v_gdn_conv_w, v_gdn_a_log, v_gdn_dt_bias, v_gdn_norm_w, v_mla_q_norm_w, v_mla_kv_norm_w, v_mla_w_uq, v_mla_w_ukv, v_mix_w_o, v_ffb_w_in, v_ffb_w_out, v_ln_g, v_ln_b, v_ple_w_gate, v_ple_w_proj):
    given = dict(locals())
    shards = {n: given[n] for n in WEIGHTS}
    ex = Exchanges({n: shards[n] for n in SHARDED})
    ex.gather_now(GATHER_FIRST, "gather_first")
    loss, grad_x, Grep = _local_step(x[0], p[:, 0], positions[0], loss_target[0], ex, {n: shards[n] for n in REPLICATED})
    loss = lax.psum(loss[0, 0], ("x", "y", "c"))
    last = Hosted("scatter", [ex.partial[k] for k in SCATTER_LAST])
    ex.received.update(zip(SCATTER_LAST, exchange_now(last, "scatter_last")))
    rep_received = dict(zip(REPLICATED, all_gather([Grep[n] for n in REPLICATED], "gather_replicated_grads")))
    grad, delta, new_m, new_v = {}, {}, {}, {}
    for n in WEIGHTS:
        shape = shards[n].shape
        parts = [rep_received[n]] if n in REPLICATED else [ex.received[(n, l)] for l in range(DEPTH)]
        if parts[0].shape[1] % 8:
            parts = [jnp.concatenate(parts, axis=1)]
        outs = reduce_adamw(parts, _view2d(shards[n]), _view2d(given["m_" + n]), _view2d(given["v_" + n]),
                            "adamw_" + n)
        grad[n], delta[n], new_m[n], new_v[n] = (t.reshape(shape) for t in outs)
    return (loss, grad_x[None], *[grad[n] for n in WEIGHTS], *[delta[n] for n in WEIGHTS],
            *[new_m[n] for n in WEIGHTS], *[new_v[n] for n in WEIGHTS])
```
